```python
import jax, jax.numpy as jnp
from jax import lax
import numpy as np

D_MODEL = 1024
BATCH = 8
SEQ = 4096
DEPTH = 1

N_META = 16
D_MIX = D_MODEL
D_RG = D_MIX // 2
RG_HEADS = 8
RG_HEAD_DIM = D_RG // RG_HEADS
CONV_W = 4
LRU_C = 8.0
D_HG = D_MIX - D_RG
HG_HEAD_DIM = 128
HG_HEADS = D_HG // HG_HEAD_DIM
HG_CHUNK = 64
D_IN = 2 * D_RG + 4 * D_HG
D_FF = ((8 * D_MODEL // 3 + 255) // 256) * 256
EPS = 1e-6

kernel_name = "hymba_rglru_hgrn2_block"


def rmsnorm(x, g):
    xf = x.astype(jnp.float32)
    y = xf * lax.rsqrt(jnp.mean(xf * xf, axis=-1, keepdims=True) + EPS) * g.astype(jnp.float32)
    return y.astype(x.dtype)


def _lin_combine(e1, e2):
    a1, b1 = e1
    a2, b2 = e2
    return a1 * a2, a2 * b1 + b2


def rg_lru_group(xr, gr, conv_w, conv_b, w_r, b_r, w_i, b_i, lam, norm_g):
    B, L, _ = xr.shape
    xp = jnp.pad(xr.astype(jnp.float32), ((0, 0), (CONV_W - 1, 0), (0, 0)))
    cw = conv_w.astype(jnp.float32)
    xc = conv_b.astype(jnp.float32) + sum(xp[:, j:j + L] * cw[j] for j in range(CONV_W))
    xh = xc.reshape(B, L, RG_HEADS, RG_HEAD_DIM)
    r = jax.nn.sigmoid(jnp.einsum('blhi,hij->blhj', xh, w_r.astype(jnp.float32)).reshape(B, L, D_RG) + b_r.astype(jnp.float32))
    i = jax.nn.sigmoid(jnp.einsum('blhi,hij->blhj', xh, w_i.astype(jnp.float32)).reshape(B, L, D_RG) + b_i.astype(jnp.float32))
    log_a = -LRU_C * jax.nn.softplus(-lam.astype(jnp.float32)) * r
    a = jnp.exp(log_a)
    bx = jnp.sqrt(-jnp.expm1(2.0 * log_a)) * (i * xc)
    _, h = lax.associative_scan(_lin_combine, (a, bx), axis=1)
    y = jax.nn.gelu(gr.astype(jnp.float32)) * h
    return rmsnorm(y, norm_g)


def _to_chunks(t, pad):
    B, L, _ = t.shape
    t = jnp.pad(t, ((0, 0), (pad, 0), (0, 0)))
    n = (L + pad) // HG_CHUNK
    t = t.reshape(B, n, HG_CHUNK, HG_HEADS, HG_HEAD_DIM)
    return jnp.transpose(t, (1, 0, 3, 2, 4))


def _hgrn2_chunk_step(S, inp):
    q, k, v, lf = inp
    b = jnp.cumsum(lf, axis=2)
    inter = jnp.einsum('bhck,bhkv->bhcv', q * jnp.exp(b), S)
    diff = b[:, :, :, None, :] - b[:, :, None, :, :]
    causal = (jnp.arange(HG_CHUNK)[:, None] >= jnp.arange(HG_CHUNK)[None, :])[None, None, :, :, None]
    decay = jnp.where(causal, jnp.exp(jnp.where(causal, diff, 0.0)), 0.0)
    A = jnp.einsum('bhtsk,bhsk->bhts', q[:, :, :, None, :] * decay, k)
    intra = jnp.einsum('bhts,bhsv->bhtv', A, v)
    b_last = b[:, :, -1:, :]
    S_new = jnp.exp(b_last[:, :, 0, :])[..., None] * S + jnp.einsum('bhsk,bhsv->bhkv', k * jnp.exp(b_last - b), v)
    return S_new, inter + intra


def hgrn2_group(hq, hf, hi, hg, lb, norm_g):
    B, L, _ = hq.shape
    lb = lb.astype(jnp.float32)
    q = jax.nn.silu(hq.astype(jnp.float32))
    f = lb + (1.0 - lb) * jax.nn.sigmoid(hf.astype(jnp.float32))
    log_f = jnp.log(f)
    k = 1.0 - f
    v = hi.astype(jnp.float32)
    pad = HG_CHUNK - N_META
    qc, kc, vc, lfc = (_to_chunks(t, pad) for t in (q, k, v, log_f))
    S0 = jnp.zeros((B, HG_HEADS, HG_HEAD_DIM, HG_HEAD_DIM), jnp.float32)
    _, o = lax.scan(_hgrn2_chunk_step, S0, (qc, kc, vc, lfc))
    n = o.shape[0]
    o = jnp.transpose(o, (1, 0, 3, 2, 4)).reshape(B, n * HG_CHUNK, HG_HEADS, HG_HEAD_DIM)[:, pad:]
    o = rmsnorm(o, norm_g).astype(jnp.float32) * jax.nn.silu(hg.astype(jnp.float32).reshape(B, L, HG_HEADS, HG_HEAD_DIM))
    return o.reshape(B, L, D_HG)


def _fwd_setup_inputs(seed: int = 0) -> dict:
    key = jax.random.key(seed)
    ks = jax.random.split(key, 24)
    f32 = jnp.float32
    nrm = lambda k, shape, s: s * jax.random.normal(k, shape, f32)
    u = jax.random.uniform(ks[9], (DEPTH, D_RG), f32, 0.9, 0.999)
    s = u ** (1.0 / LRU_C)
    lru_lambda = jnp.log(s) - jnp.log1p(-s)
    return {
        "x": jax.random.normal(ks[0], (BATCH, SEQ, D_MODEL), f32),
        "meta_tokens": nrm(ks[1], (N_META, D_MODEL), 1.0),
        "mix_norm_g": 1.0 + nrm(ks[2], (DEPTH, D_MODEL), 0.02),
        "w_in": nrm(ks[3], (DEPTH, D_MODEL, D_IN), D_MODEL ** -0.5),
        "conv_w": nrm(ks[4], (DEPTH, CONV_W, D_RG), CONV_W ** -0.5),
        "conv_b": nrm(ks[5], (DEPTH, D_RG), 0.01),
        "w_rgate": nrm(ks[6], (DEPTH, RG_HEADS, RG_HEAD_DIM, RG_HEAD_DIM), RG_HEAD_DIM ** -0.5),
        "b_rgate": nrm(ks[7], (DEPTH, D_RG), 0.01),
        "w_igate": nrm(ks[8], (DEPTH, RG_HEADS, RG_HEAD_DIM, RG_HEAD_DIM), RG_HEAD_DIM ** -0.5),
        "b_igate": nrm(ks[10], (DEPTH, D_RG), 0.01),
        "lru_lambda": lru_lambda,
        "rg_norm_g": 1.0 + nrm(ks[11], (DEPTH, D_RG), 0.02),
        "hg_lower_bound": nrm(ks[12], (DEPTH + 1, D_HG), 0.1),
        "hg_norm_g": 1.0 + nrm(ks[13], (DEPTH, HG_HEAD_DIM), 0.02),
        "w_out": nrm(ks[14], (DEPTH, D_MIX, D_MODEL), D_MIX ** -0.5),
        "ffn_norm_g": 1.0 + nrm(ks[15], (DEPTH, D_MODEL), 0.02),
        "w_gate_up": nrm(ks[16], (DEPTH, D_MODEL, 2 * D_FF), D_MODEL ** -0.5),
        "w_down": nrm(ks[17], (DEPTH, D_FF, D_MODEL), D_FF ** -0.5),
        "final_norm_g": 1.0 + nrm(ks[18], (D_MODEL,), 0.02),
    }


def _fwd_reference(x, meta_tokens, mix_norm_g, w_in, conv_w, conv_b, w_rgate, b_rgate, w_igate, b_igate,
              lru_lambda, rg_norm_g, hg_lower_bound, hg_norm_g, w_out, ffn_norm_g, w_gate_up, w_down,
              final_norm_g):
    B = x.shape[0]
    meta = jnp.broadcast_to(meta_tokens.astype(x.dtype)[None], (B, N_META, D_MODEL))
    h = jnp.concatenate([meta, x], axis=1)
    lbs = jnp.cumsum(jax.nn.softmax(hg_lower_bound.astype(jnp.float32), axis=0), axis=0)
    splits = np.cumsum([D_RG, D_RG, D_HG, D_HG, D_HG])
    for l in range(DEPTH):
        u = rmsnorm(h, mix_norm_g[l])
        p = jnp.einsum('bld,de->ble', u, w_in[l])
        rg_x, rg_g, hq, hf, hi, hg = jnp.split(p, splits, axis=-1)
        y_rg = rg_lru_group(rg_x, rg_g, conv_w[l], conv_b[l], w_rgate[l], b_rgate[l],
                            w_igate[l], b_igate[l], lru_lambda[l], rg_norm_g[l])
        y_hg = hgrn2_group(hq, hf, hi, hg, lbs[l], hg_norm_g[l])
        y = jnp.concatenate([y_rg.astype(h.dtype), y_hg.astype(h.dtype)], axis=-1)
        h = h + jnp.einsum('ble,ed->bld', y, w_out[l])
        v = rmsnorm(h, ffn_norm_g[l])
        gate, up = jnp.split(jnp.einsum('bld,df->blf', v, w_gate_up[l]), 2, axis=-1)
        h = h + jnp.einsum('blf,fd->bld', jax.nn.silu(gate) * up, w_down[l])
    return rmsnorm(h, final_norm_g)[:, N_META:]


import jax as _jax
import jax.numpy as _jnp

TWIN_FORMAT = 'train_step'
FWD_PARAMS = ['x', 'meta_tokens', 'mix_norm_g', 'w_in', 'conv_w', 'conv_b', 'w_rgate', 'b_rgate', 'w_igate', 'b_igate', 'lru_lambda', 'rg_norm_g', 'hg_lower_bound', 'hg_norm_g', 'w_out', 'ffn_norm_g', 'w_gate_up', 'w_down', 'final_norm_g']
TWIN_WEIGHTS = ['meta_tokens', 'mix_norm_g', 'w_in', 'conv_w', 'conv_b', 'w_rgate', 'b_rgate', 'w_igate', 'b_igate', 'lru_lambda', 'rg_norm_g', 'hg_lower_bound', 'hg_norm_g', 'w_out', 'ffn_norm_g', 'w_gate_up', 'w_down', 'final_norm_g']
TWIN_DIFF_INPUT = 'x'
TWIN_INPUTS = ['x', 'meta_tokens', 'mix_norm_g', 'w_in', 'conv_w', 'conv_b', 'w_rgate', 'b_rgate', 'w_igate', 'b_igate', 'lru_lambda', 'rg_norm_g', 'hg_lower_bound', 'hg_norm_g', 'w_out', 'ffn_norm_g', 'w_gate_up', 'w_down', 'final_norm_g', 'loss_target', 'm_meta_tokens', 'm_mix_norm_g', 'm_w_in', 'm_conv_w', 'm_conv_b', 'm_w_rgate', 'm_b_rgate', 'm_w_igate', 'm_b_igate', 'm_lru_lambda', 'm_rg_norm_g', 'm_hg_lower_bound', 'm_hg_norm_g', 'm_w_out', 'm_ffn_norm_g', 'm_w_gate_up', 'm_w_down', 'm_final_norm_g', 'v_meta_tokens', 'v_mix_norm_g', 'v_w_in', 'v_conv_w', 'v_conv_b', 'v_w_rgate', 'v_b_rgate', 'v_w_igate', 'v_b_igate', 'v_lru_lambda', 'v_rg_norm_g', 'v_hg_lower_bound', 'v_hg_norm_g', 'v_w_out', 'v_ffn_norm_g', 'v_w_gate_up', 'v_w_down', 'v_final_norm_g']
TWIN_OUTPUTS = ['loss', 'grad_x', 'grad_meta_tokens', 'grad_mix_norm_g', 'grad_w_in', 'grad_conv_w', 'grad_conv_b', 'grad_w_rgate', 'grad_b_rgate', 'grad_w_igate', 'grad_b_igate', 'grad_lru_lambda', 'grad_rg_norm_g', 'grad_hg_lower_bound', 'grad_hg_norm_g', 'grad_w_out', 'grad_ffn_norm_g', 'grad_w_gate_up', 'grad_w_down', 'grad_final_norm_g', 'delta_meta_tokens', 'delta_mix_norm_g', 'delta_w_in', 'delta_conv_w', 'delta_conv_b', 'delta_w_rgate', 'delta_b_rgate', 'delta_w_igate', 'delta_b_igate', 'delta_lru_lambda', 'delta_rg_norm_g', 'delta_hg_lower_bound', 'delta_hg_norm_g', 'delta_w_out', 'delta_ffn_norm_g', 'delta_w_gate_up', 'delta_w_down', 'delta_final_norm_g', 'new_m_meta_tokens', 'new_m_mix_norm_g', 'new_m_w_in', 'new_m_conv_w', 'new_m_conv_b', 'new_m_w_rgate', 'new_m_b_rgate', 'new_m_w_igate', 'new_m_b_igate', 'new_m_lru_lambda', 'new_m_rg_norm_g', 'new_m_hg_lower_bound', 'new_m_hg_norm_g', 'new_m_w_out', 'new_m_ffn_norm_g', 'new_m_w_gate_up', 'new_m_w_down', 'new_m_final_norm_g', 'new_v_meta_tokens', 'new_v_mix_norm_g', 'new_v_w_in', 'new_v_conv_w', 'new_v_conv_b', 'new_v_w_rgate', 'new_v_b_rgate', 'new_v_w_igate', 'new_v_b_igate', 'new_v_lru_lambda', 'new_v_rg_norm_g', 'new_v_hg_lower_bound', 'new_v_hg_norm_g', 'new_v_w_out', 'new_v_ffn_norm_g', 'new_v_w_gate_up', 'new_v_w_down', 'new_v_final_norm_g']
TWIN_LEAF_KINDS = {'loss': 'loss', 'grad_x': 'grad_x', 'grad_meta_tokens': 'grad_w', 'grad_mix_norm_g': 'grad_w', 'grad_w_in': 'grad_w', 'grad_conv_w': 'grad_w', 'grad_conv_b': 'grad_w', 'grad_w_rgate': 'grad_w', 'grad_b_rgate': 'grad_w', 'grad_w_igate': 'grad_w', 'grad_b_igate': 'grad_w', 'grad_lru_lambda': 'grad_w', 'grad_rg_norm_g': 'grad_w', 'grad_hg_lower_bound': 'grad_w', 'grad_hg_norm_g': 'grad_w', 'grad_w_out': 'grad_w', 'grad_ffn_norm_g': 'grad_w', 'grad_w_gate_up': 'grad_w', 'grad_w_down': 'grad_w', 'grad_final_norm_g': 'grad_w', 'delta_meta_tokens': 'delta_w', 'delta_mix_norm_g': 'delta_w', 'delta_w_in': 'delta_w', 'delta_conv_w': 'delta_w', 'delta_conv_b': 'delta_w', 'delta_w_rgate': 'delta_w', 'delta_b_rgate': 'delta_w', 'delta_w_igate': 'delta_w', 'delta_b_igate': 'delta_w', 'delta_lru_lambda': 'delta_w', 'delta_rg_norm_g': 'delta_w', 'delta_hg_lower_bound': 'delta_w', 'delta_hg_norm_g': 'delta_w', 'delta_w_out': 'delta_w', 'delta_ffn_norm_g': 'delta_w', 'delta_w_gate_up': 'delta_w', 'delta_w_down': 'delta_w', 'delta_final_norm_g': 'delta_w', 'new_m_meta_tokens': 'new_m', 'new_m_mix_norm_g': 'new_m', 'new_m_w_in': 'new_m', 'new_m_conv_w': 'new_m', 'new_m_conv_b': 'new_m', 'new_m_w_rgate': 'new_m', 'new_m_b_rgate': 'new_m', 'new_m_w_igate': 'new_m', 'new_m_b_igate': 'new_m', 'new_m_lru_lambda': 'new_m', 'new_m_rg_norm_g': 'new_m', 'new_m_hg_lower_bound': 'new_m', 'new_m_hg_norm_g': 'new_m', 'new_m_w_out': 'new_m', 'new_m_ffn_norm_g': 'new_m', 'new_m_w_gate_up': 'new_m', 'new_m_w_down': 'new_m', 'new_m_final_norm_g': 'new_m', 'new_v_meta_tokens': 'new_v', 'new_v_mix_norm_g': 'new_v', 'new_v_w_in': 'new_v', 'new_v_conv_w': 'new_v', 'new_v_conv_b': 'new_v', 'new_v_w_rgate': 'new_v', 'new_v_b_rgate': 'new_v', 'new_v_w_igate': 'new_v', 'new_v_b_igate': 'new_v', 'new_v_lru_lambda': 'new_v', 'new_v_rg_norm_g': 'new_v', 'new_v_hg_lower_bound': 'new_v', 'new_v_hg_norm_g': 'new_v', 'new_v_w_out': 'new_v', 'new_v_ffn_norm_g': 'new_v', 'new_v_w_gate_up': 'new_v', 'new_v_w_down': 'new_v', 'new_v_final_norm_g': 'new_v'}


def _forward(args):
    return _fwd_reference(*[args[k] for k in FWD_PARAMS])


def _output_shape():
    out = _jax.eval_shape(lambda: _forward(_fwd_setup_inputs(0)))
    return out.shape, out.dtype

N_MICROBATCH = 1
ADAM_LR = 0.001
ADAM_B1 = 0.9
ADAM_B2 = 0.999
ADAM_EPS = 1e-08
ADAM_WD = 0.01
ADAM_STEP = 10
PER_EXAMPLE_BATCH_AXIS = {'x': 0, 'loss_target': 0}
SHARED_INPUTS = []
_WEIGHT_DTYPES = {'meta_tokens': _jnp.float32, 'mix_norm_g': _jnp.float32, 'w_in': _jnp.float32, 'conv_w': _jnp.float32, 'conv_b': _jnp.float32, 'w_rgate': _jnp.float32, 'b_rgate': _jnp.float32, 'w_igate': _jnp.float32, 'b_igate': _jnp.float32, 'lru_lambda': _jnp.float32, 'rg_norm_g': _jnp.float32, 'hg_lower_bound': _jnp.float32, 'hg_norm_g': _jnp.float32, 'w_out': _jnp.float32, 'ffn_norm_g': _jnp.float32, 'w_gate_up': _jnp.float32, 'w_down': _jnp.float32, 'final_norm_g': _jnp.float32}
MOMENT_SCALE = {'meta_tokens': 7.026204e-03, 'mix_norm_g': 1.895134e-01, 'w_in': 1.019645e-01, 'conv_w': 1.492083e-01, 'conv_b': 1.562583e+00, 'w_rgate': 5.348674e-02, 'b_rgate': 5.029486e-02, 'w_igate': 9.642797e-02, 'b_igate': 5.089040e-02, 'lru_lambda': 9.856181e-02, 'rg_norm_g': 1.524418e-01, 'hg_lower_bound': 8.510708e-03, 'hg_norm_g': 1.837464e-01, 'w_out': 1.241077e-01, 'ffn_norm_g': 1.108729e-01, 'w_gate_up': 4.551418e-02, 'w_down': 7.445977e-02, 'final_norm_g': 3.198431e+01}


def _to_microbatches(a, axis):
    t = _jnp.moveaxis(a, axis, 0)
    t = t.reshape((N_MICROBATCH, t.shape[0] // N_MICROBATCH) + t.shape[1:])
    return _jnp.moveaxis(t, 1, axis + 1)


def setup_inputs(seed: int = 0) -> dict:
    inp = _fwd_setup_inputs(seed)
    key = _jax.random.fold_in(_jax.random.key(seed), 7919)
    shape, _ = _output_shape()
    out = dict(inp)
    out["loss_target"] = _jax.random.normal(_jax.random.fold_in(key, 0), shape, _jnp.float32)
    for i, name in enumerate(TWIN_WEIGHTS):
        w = inp[name].astype(_jnp.float32)
        if MOMENT_SCALE is None:
            s = _jnp.sqrt(_jnp.mean(_jnp.square(w)) + 1e-30)
        else:
            s = MOMENT_SCALE[name]
        km, kv = _jax.random.split(_jax.random.fold_in(key, i + 1))
        out[name] = w
        out["m_" + name] = s * _jax.random.normal(km, w.shape, _jnp.float32)
        out["v_" + name] = (s * s) * _jax.random.uniform(kv, w.shape, _jnp.float32, 0.5, 1.5)
    if N_MICROBATCH > 1:
        for name, axis in PER_EXAMPLE_BATCH_AXIS.items():
            out[name] = _to_microbatches(out[name], axis)
    return {'x': out['x'], 'meta_tokens': out['meta_tokens'], 'mix_norm_g': out['mix_norm_g'], 'w_in': out['w_in'], 'conv_w': out['conv_w'], 'conv_b': out['conv_b'], 'w_rgate': out['w_rgate'], 'b_rgate': out['b_rgate'], 'w_igate': out['w_igate'], 'b_igate': out['b_igate'], 'lru_lambda': out['lru_lambda'], 'rg_norm_g': out['rg_norm_g'], 'hg_lower_bound': out['hg_lower_bound'], 'hg_norm_g': out['hg_norm_g'], 'w_out': out['w_out'], 'ffn_norm_g': out['ffn_norm_g'], 'w_gate_up': out['w_gate_up'], 'w_down': out['w_down'], 'final_norm_g': out['final_norm_g'], 'loss_target': out['loss_target'], 'm_meta_tokens': out['m_meta_tokens'], 'm_mix_norm_g': out['m_mix_norm_g'], 'm_w_in': out['m_w_in'], 'm_conv_w': out['m_conv_w'], 'm_conv_b': out['m_conv_b'], 'm_w_rgate': out['m_w_rgate'], 'm_b_rgate': out['m_b_rgate'], 'm_w_igate': out['m_w_igate'], 'm_b_igate': out['m_b_igate'], 'm_lru_lambda': out['m_lru_lambda'], 'm_rg_norm_g': out['m_rg_norm_g'], 'm_hg_lower_bound': out['m_hg_lower_bound'], 'm_hg_norm_g': out['m_hg_norm_g'], 'm_w_out': out['m_w_out'], 'm_ffn_norm_g': out['m_ffn_norm_g'], 'm_w_gate_up': out['m_w_gate_up'], 'm_w_down': out['m_w_down'], 'm_final_norm_g': out['m_final_norm_g'], 'v_meta_tokens': out['v_meta_tokens'], 'v_mix_norm_g': out['v_mix_norm_g'], 'v_w_in': out['v_w_in'], 'v_conv_w': out['v_conv_w'], 'v_conv_b': out['v_conv_b'], 'v_w_rgate': out['v_w_rgate'], 'v_b_rgate': out['v_b_rgate'], 'v_w_igate': out['v_w_igate'], 'v_b_igate': out['v_b_igate'], 'v_lru_lambda': out['v_lru_lambda'], 'v_rg_norm_g': out['v_rg_norm_g'], 'v_hg_lower_bound': out['v_hg_lower_bound'], 'v_hg_norm_g': out['v_hg_norm_g'], 'v_w_out': out['v_w_out'], 'v_ffn_norm_g': out['v_ffn_norm_g'], 'v_w_gate_up': out['v_w_gate_up'], 'v_w_down': out['v_w_down'], 'v_final_norm_g': out['v_final_norm_g']}


def _loss(weights, diff, rest, loss_target):
    with _jax.named_scope("forward"):
        args = {**rest, TWIN_DIFF_INPUT: diff, **{k: w.astype(_WEIGHT_DTYPES[k]) for k, w in weights.items()}}
        y = _forward(args)
    with _jax.named_scope("loss_head"):
        err = _jnp.square(y.astype(_jnp.float32) - loss_target)
        return 0.5 * _jnp.sum(_jnp.mean(err, axis=-1)) if err.ndim else 0.5 * err


def _adamw(w, g, m, v):
    m = ADAM_B1 * m + (1.0 - ADAM_B1) * g
    v = ADAM_B2 * v + (1.0 - ADAM_B2) * _jnp.square(g)
    m_hat = m / (1.0 - ADAM_B1 ** ADAM_STEP)
    v_hat = v / (1.0 - ADAM_B2 ** ADAM_STEP)
    delta = -ADAM_LR * (m_hat / (_jnp.sqrt(v_hat) + ADAM_EPS) + ADAM_WD * w)
    return delta, m, v


def reference(x, meta_tokens, mix_norm_g, w_in, conv_w, conv_b, w_rgate, b_rgate, w_igate, b_igate, lru_lambda, rg_norm_g, hg_lower_bound, hg_norm_g, w_out, ffn_norm_g, w_gate_up, w_down, final_norm_g, loss_target, m_meta_tokens, m_mix_norm_g, m_w_in, m_conv_w, m_conv_b, m_w_rgate, m_b_rgate, m_w_igate, m_b_igate, m_lru_lambda, m_rg_norm_g, m_hg_lower_bound, m_hg_norm_g, m_w_out, m_ffn_norm_g, m_w_gate_up, m_w_down, m_final_norm_g, v_meta_tokens, v_mix_norm_g, v_w_in, v_conv_w, v_conv_b, v_w_rgate, v_b_rgate, v_w_igate, v_b_igate, v_lru_lambda, v_rg_norm_g, v_hg_lower_bound, v_hg_norm_g, v_w_out, v_ffn_norm_g, v_w_gate_up, v_w_down, v_final_norm_g):
    given = dict(x=x, meta_tokens=meta_tokens, mix_norm_g=mix_norm_g, w_in=w_in, conv_w=conv_w, conv_b=conv_b, w_rgate=w_rgate, b_rgate=b_rgate, w_igate=w_igate, b_igate=b_igate, lru_lambda=lru_lambda, rg_norm_g=rg_norm_g, hg_lower_bound=hg_lower_bound, hg_norm_g=hg_norm_g, w_out=w_out, ffn_norm_g=ffn_norm_g, w_gate_up=w_gate_up, w_down=w_down, final_norm_g=final_norm_g, loss_target=loss_target, m_meta_tokens=m_meta_tokens, m_mix_norm_g=m_mix_norm_g, m_w_in=m_w_in, m_conv_w=m_conv_w, m_conv_b=m_conv_b, m_w_rgate=m_w_rgate, m_b_rgate=m_b_rgate, m_w_igate=m_w_igate, m_b_igate=m_b_igate, m_lru_lambda=m_lru_lambda, m_rg_norm_g=m_rg_norm_g, m_hg_lower_bound=m_hg_lower_bound, m_hg_norm_g=m_hg_norm_g, m_w_out=m_w_out, m_ffn_norm_g=m_ffn_norm_g, m_w_gate_up=m_w_gate_up, m_w_down=m_w_down, m_final_norm_g=m_final_norm_g, v_meta_tokens=v_meta_tokens, v_mix_norm_g=v_mix_norm_g, v_w_in=v_w_in, v_conv_w=v_conv_w, v_conv_b=v_conv_b, v_w_rgate=v_w_rgate, v_b_rgate=v_b_rgate, v_w_igate=v_w_igate, v_b_igate=v_b_igate, v_lru_lambda=v_lru_lambda, v_rg_norm_g=v_rg_norm_g, v_hg_lower_bound=v_hg_lower_bound, v_hg_norm_g=v_hg_norm_g, v_w_out=v_w_out, v_ffn_norm_g=v_ffn_norm_g, v_w_gate_up=v_w_gate_up, v_w_down=v_w_down, v_final_norm_g=v_final_norm_g)
    weights = {n: given[n] for n in TWIN_WEIGHTS}
    shared = {n: given[n] for n in SHARED_INPUTS}
    per_example = {n: given[n] for n in ['x']}
    grad_fn = _jax.value_and_grad(_loss, argnums=(0, 1))

    def one_microbatch(ex, loss_target):
        ex = dict(ex)
        diff = ex.pop(TWIN_DIFF_INPUT)
        return grad_fn(weights, diff, {**shared, **ex}, loss_target)

    if N_MICROBATCH == 1:
        loss, (grad_w, grad_x) = one_microbatch(per_example, given["loss_target"])
    else:
        def body(carry, xs):
            loss_sum, grad_sum = carry
            l_k, (gw_k, gx_k) = one_microbatch(xs[0], xs[1])
            with _jax.named_scope("update"):
                return (loss_sum + l_k, _jax.tree.map(_jnp.add, grad_sum, gw_k)), gx_k

        init = (_jnp.zeros((), _jnp.float32), _jax.tree.map(_jnp.zeros_like, weights))
        (loss, grad_w), grad_x = _jax.lax.scan(body, init, (per_example, given["loss_target"]))
    with _jax.named_scope("update"):
        delta_w, new_m, new_v = {}, {}, {}
        for n in TWIN_WEIGHTS:
            delta_w[n], new_m[n], new_v[n] = _adamw(weights[n], grad_w[n], given["m_" + n], given["v_" + n])
    return (loss, grad_x, *[grad_w[n] for n in TWIN_WEIGHTS], *[delta_w[n] for n in TWIN_WEIGHTS],
            *[new_m[n] for n in TWIN_WEIGHTS], *[new_v[n] for n in TWIN_WEIGHTS])
```

```python
import functools

import jax
import jax.numpy as jnp
from jax import lax
from jax.experimental import pallas as pl
from jax.experimental.pallas import tpu as pltpu

_BF = jnp.bfloat16
_F32 = jnp.float32
_S = jax.ShapeDtypeStruct
_MESH = pl.DeviceIdType.MESH

N_DEV = 8
N_META = 16
D = 1024
D_RG = 512
D_HG = 512
HD = 128
NH = D_HG // HD
D_IN = 3072
D_FF = 2816
FFB = D_FF // 4
WIN_B = D_IN // N_DEV
EPS = 1e-6
LRU_C = 8.0
TM = 256
HC = 64
VMEM_LIMIT = 56 * 1024 * 1024

ADAM_LR = 0.001
ADAM_B1 = 0.9
ADAM_B2 = 0.999
ADAM_EPS = 1e-08
ADAM_WD = 0.01
ADAM_STEP = 10

R_CONVB, R_BR, R_BI, R_LAM, R_GRG, R_HB0, R_HB1, R_GHG, R_CONVW = 0, 1, 2, 3, 4, 5, 6, 7, 8
R_GMIX, R_GFFN, R_GFIN, R_META = 0, 1, 2, 8


def _cp(sem=None, **kw):
    return pltpu.CompilerParams(dimension_semantics=sem, vmem_limit_bytes=VMEM_LIMIT, **kw)


def _dot(a, b):
    return jnp.dot(a, b, preferred_element_type=_F32)


def _dot_nt(a, b):
    return lax.dot_general(a, b, (((1,), (1,)), ((), ())), preferred_element_type=_F32)


def _dot_tn(a, b):
    return lax.dot_general(a, b, (((0,), (0,)), ((), ())), preferred_element_type=_F32)


def _dot_hi(a, b):
    return jnp.dot(a, b, preferred_element_type=_F32, precision=lax.Precision.HIGHEST)


def _sigmoid(x):
    return jax.nn.sigmoid(x)


def _dsilu(x, s):
    return s * (1.0 + x * (1.0 - s))


_GELU_C = 0.7978845608028654


def _gelu_parts(x):
    t = jnp.tanh(_GELU_C * (x + 0.044715 * (x * x * x)))
    g = 0.5 * x * (1.0 + t)
    dg = 0.5 * (1.0 + t) + 0.5 * x * (1.0 - t * t) * (_GELU_C * (1.0 + 3.0 * 0.044715 * (x * x)))
    return g, dg


def _softplus(z):
    e = jnp.exp(-jnp.abs(z))
    w = 1.0 + e
    l1p = jnp.where(w == 1.0, e, jnp.log(w) * e / jnp.where(w == 1.0, 1.0, w - 1.0))
    return jnp.maximum(z, 0.0) + l1p


def _rms_fwd(x):
    r = lax.rsqrt(jnp.mean(x * x, axis=-1, keepdims=True) + EPS)
    return x * r, r


def _rms_bwd(dyg, n, r):
    return r * (dyg - n * jnp.mean(dyg * n, axis=-1, keepdims=True))


def _full(shape):
    nd = len(shape)
    return pl.BlockSpec(shape, lambda i: (0,) * nd)


def _const(shape):
    nd = len(shape)
    return pl.BlockSpec(shape, lambda i: (0,) * nd, pipeline_mode=pl.Buffered(1))


def _inproj(h0, g_mix, w_in):
    t_pad = h0.shape[0]

    def body(h_ref, g_ref, w_ref, p_ref, u_ref):
        n, _ = _rms_fwd(h_ref[...])
        u = (n * g_ref[...]).astype(_BF)
        u_ref[...] = u
        for j in range(N_DEV):
            p_ref[:, WIN_B * j:WIN_B * (j + 1)] = _dot(u, w_ref[j])

    return pl.pallas_call(
        body, name="inproj", grid=(t_pad // TM,),
        in_specs=[pl.BlockSpec((TM, D), lambda i: (i, 0)), _full((1, D)), _const((N_DEV, D, WIN_B))],
        out_specs=[pl.BlockSpec((TM, D_IN), lambda i: (i, 0)), pl.BlockSpec((TM, D), lambda i: (i, 0))],
        out_shape=[_S((t_pad, D_IN), _F32), _S((t_pad, D), _BF)],
        compiler_params=_cp(("arbitrary",)),
    )(h0, g_mix, w_in)


def _rg_gates(xc, wr_ref, wi_ref, vec_ref):
    xcb = xc.astype(_BF)
    r = _sigmoid(_dot(xcb, wr_ref[...]) + vec_ref[R_BR:R_BR + 1, :])
    ig = _sigmoid(_dot(xcb, wi_ref[...]) + vec_ref[R_BI:R_BI + 1, :])
    nsp8 = -LRU_C * _softplus(-vec_ref[R_LAM:R_LAM + 1, :])
    la = nsp8 * r
    a = jnp.exp(la)
    th = jnp.tanh(la)
    s = jnp.sqrt(-2.0 * th / (1.0 - th))
    return r, ig, a, s, nsp8


def _conv(xbuf, vec_ref):
    acc = vec_ref[R_CONVW:R_CONVW + 1, :] * xbuf[pl.ds(5, TM), :]
    for j in range(1, 4):
        acc = acc + vec_ref[R_CONVW + j:R_CONVW + j + 1, :] * xbuf[pl.ds(5 + j, TM), :]
    return vec_ref[R_CONVB:R_CONVB + 1, :] + acc


def _hg_chunk_fwd(p_ref, rows, h, lbh, tri):
    hq = p_ref[rows, pl.ds(2 * D_RG + HD * h, HD)]
    hf = p_ref[rows, pl.ds(2 * D_RG + D_HG + HD * h, HD)]
    v = p_ref[rows, pl.ds(2 * D_RG + 2 * D_HG + HD * h, HD)]
    sq = _sigmoid(hq)
    q = hq * sq
    sg = _sigmoid(hf)
    f = lbh + (1.0 - lbh) * sg
    k = 1.0 - f
    b = _dot_hi(tri, jnp.log(f))
    bm = b[HC // 2 - 1:HC // 2, :]
    bl = b[HC - 1:HC, :]
    e_q = jnp.exp(b - bm)
    e_k = jnp.exp(bm - b)
    e_b = jnp.exp(b)
    e_l = jnp.exp(bl - b)
    return dict(hq=hq, sq=sq, q=q, sg=sg, f=f, k=k, v=v, e_q=e_q, e_k=e_k, e_b=e_b, e_l=e_l,
                qd=q * e_q, kd=k * e_k, qe=q * e_b, ke=k * e_l, e_end=jnp.exp(bl))


def _mixer_fwd(p, wr, wi, vec, hb, g_hg):
    t_pad = p.shape[0]
    nc_t = TM // HC

    def body(p_ref, wr_ref, wi_ref, vec_ref, hb_ref, ghg_ref, y_ref, hs_ref, o_ref, sc_ref,
             xbuf, a_s, b_s, hcar, st):
        i = pl.program_id(0)

        @pl.when(i == 0)
        def _():
            xbuf[pl.ds(0, 8), :] = jnp.zeros((8, D_RG), _F32)
            hcar[...] = jnp.zeros_like(hcar)
            st[...] = jnp.zeros_like(st)

        x = p_ref[:, pl.ds(0, D_RG)]
        xbuf[pl.ds(8, TM), :] = x
        xc = _conv(xbuf, vec_ref)
        xbuf[pl.ds(0, 8), :] = x[TM - 8:, :]
        r, ig, a, s, _ = _rg_gates(xc, wr_ref, wi_ref, vec_ref)
        a_s[...] = a
        b_s[...] = s * (ig * xc)

        def step(t, h):
            h = a_s[pl.ds(t, 1), :] * h + b_s[pl.ds(t, 1), :]
            hs_ref[pl.ds(t, 1), :] = h
            return h

        hcar[pl.ds(0, 1), :] = lax.fori_loop(0, TM, step, hcar[pl.ds(0, 1), :], unroll=8)
        gel, _ = _gelu_parts(p_ref[:, pl.ds(D_RG, D_RG)])
        n, _ = _rms_fwd(gel * hs_ref[...])
        y_ref[:, pl.ds(0, D_RG)] = (n * vec_ref[R_GRG:R_GRG + 1, :]).astype(_BF)

        lb = _sigmoid(hb_ref[0:1, :] - hb_ref[1:2, :])
        ti = lax.broadcasted_iota(jnp.int32, (HC, HC), 0)
        si = lax.broadcasted_iota(jnp.int32, (HC, HC), 1)
        causal = ti >= si
        tri = causal.astype(_F32)

        def chunk(c, carry):
            rows = pl.ds(pl.multiple_of(c * HC, HC), HC)
            for h in range(NH):
                q = _hg_chunk_fwd(p_ref, rows, h, lb[:, HD * h:HD * (h + 1)], tri)
                s0 = st[h]
                sc_ref[c, h] = s0
                amat = jnp.where(causal, _dot_nt(q["qd"].astype(_BF), q["kd"].astype(_BF)), 0.0)
                vb = q["v"].astype(_BF)
                o = _dot_nt(q["qe"].astype(_BF), s0.astype(_BF)) + _dot(amat.astype(_BF), vb)
                st[h] = q["e_end"] * s0 + _dot_tn(vb, q["ke"].astype(_BF))
                o_ref[rows, pl.ds(HD * h, HD)] = o
                n_o, _ = _rms_fwd(o)
                hg = p_ref[rows, pl.ds(2 * D_RG + 3 * D_HG + HD * h, HD)]
                yh = (n_o * ghg_ref[...]) * (hg * _sigmoid(hg))
                y_ref[rows, pl.ds(D_RG + HD * h, HD)] = yh.astype(_BF)
            return carry

        lax.fori_loop(0, nc_t, chunk, 0)

    return pl.pallas_call(
        body, name="mixer_fwd", grid=(t_pad // TM,),
        in_specs=[pl.BlockSpec((TM, D_IN), lambda i: (i, 0)), _full((D_RG, D_RG)), _full((D_RG, D_RG)),
                  _full((16, D_RG)), _full((2, D_HG)), _full((1, HD))],
        out_specs=[pl.BlockSpec((TM, D), lambda i: (i, 0)), pl.BlockSpec((TM, D_RG), lambda i: (i, 0)),
                   pl.BlockSpec((TM, D_HG), lambda i: (i, 0)),
                   pl.BlockSpec((nc_t, NH, HD, HD), lambda i: (i, 0, 0, 0))],
        out_shape=[_S((t_pad, D), _BF), _S((t_pad, D_RG), _F32), _S((t_pad, D_HG), _F32),
                   _S((t_pad // HC, NH, HD, HD), _F32)],
        scratch_shapes=[pltpu.VMEM((TM + 8, D_RG), _F32), pltpu.VMEM((TM, D_RG), _F32),
                        pltpu.VMEM((TM, D_RG), _F32), pltpu.VMEM((8, D_RG), _F32),
                        pltpu.VMEM((NH, HD, HD), _F32)],
        compiler_params=_cp(("arbitrary",)),
    )(p, wr, wi, vec, hb, g_hg)


def _outproj(h0, y, w_out, g_ffn):
    t_pad = h0.shape[0]

    def body(h_ref, y_ref, w_ref, g_ref, h1_ref, v_ref):
        h1 = h_ref[...] + _dot(y_ref[...], w_ref[...])
        h1_ref[...] = h1
        n, _ = _rms_fwd(h1)
        v_ref[...] = (n * g_ref[...]).astype(_BF)

    return pl.pallas_call(
        body, name="outproj", grid=(t_pad // TM,),
        in_specs=[pl.BlockSpec((TM, D), lambda i: (i, 0)), pl.BlockSpec((TM, D), lambda i: (i, 0)),
                  _full((D, D)), _full((1, D))],
        out_specs=[pl.BlockSpec((TM, D), lambda i: (i, 0)), pl.BlockSpec((TM, D), lambda i: (i, 0))],
        out_shape=[_S((t_pad, D), _F32), _S((t_pad, D), _BF)],
        compiler_params=_cp(("arbitrary",)),
    )(h0, y, w_out, g_ffn)


def _ffn_loss(v, h1, w_gu, w_down, g_fin, tgt, n_valid):
    t_pad = v.shape[0]

    def body(v_ref, h1_ref, wgu_ref, wd_ref, g_ref, t_ref, gu_ref, act_ref, dh2_ref, dh2b_ref, loss_ref, gfin_ref):
        i = pl.program_id(0)

        @pl.when(i == 0)
        def _():
            loss_ref[...] = jnp.zeros_like(loss_ref)
            gfin_ref[...] = jnp.zeros_like(gfin_ref)

        vb = v_ref[...]
        h2 = h1_ref[...]
        for b in range(4):
            gate = _dot(vb, wgu_ref[b])
            up = _dot(vb, wgu_ref[4 + b])
            gu_ref[b] = gate
            gu_ref[4 + b] = up
            act = ((gate * _sigmoid(gate)) * up).astype(_BF)
            act_ref[b] = act
            h2 = h2 + _dot(act, wd_ref[b])
        n, r = _rms_fwd(h2)
        out = n * g_ref[...]
        row = i * TM + lax.broadcasted_iota(jnp.int32, (TM, 1), 0)
        valid = (row >= N_META) & (row < n_valid)
        err = jnp.where(valid, out - t_ref[...], 0.0)
        loss_ref[...] += (0.5 / D) * jnp.sum(err * err)
        dout = err * (1.0 / D)
        gfin_ref[...] += jnp.sum(dout * n, axis=0, keepdims=True)
        dh2 = _rms_bwd(dout * g_ref[...], n, r)
        dh2_ref[...] = dh2
        dh2b_ref[...] = dh2.astype(_BF)

    return pl.pallas_call(
        body, name="ffn_loss", grid=(t_pad // TM,),
        in_specs=[pl.BlockSpec((TM, D), lambda i: (i, 0)), pl.BlockSpec((TM, D), lambda i: (i, 0)),
                  _const((N_DEV, D, FFB)), _const((4, FFB, D)), _full((1, D)),
                  pl.BlockSpec((TM, D), lambda i: (i, 0))],
        out_specs=[pl.BlockSpec((N_DEV, TM, FFB), lambda i: (0, i, 0)), pl.BlockSpec((4, TM, FFB), lambda i: (0, i, 0)),
                   pl.BlockSpec((TM, D), lambda i: (i, 0)), pl.BlockSpec((TM, D), lambda i: (i, 0)),
                   _full((8, 128)), _full((1, D))],
        out_shape=[_S((N_DEV, t_pad, FFB), _F32), _S((4, t_pad, FFB), _BF), _S((t_pad, D), _F32),
                   _S((t_pad, D), _BF), _S((8, 128), _F32), _S((1, D), _F32)],
        compiler_params=_cp(("arbitrary",)),
    )(v, h1, w_gu, w_down, g_fin, tgt)


def _ffn_bwd(dh2, dh2b, gu, h1, g_ffn, w_gu, w_down, w_out):
    t_pad = dh2.shape[0]

    def body(dh2_ref, dh2b_ref, gu_ref, h1_ref, g_ref, wgu_ref, wd_ref, wo_ref,
             dgu_ref, dh1_ref, dh1b_ref, dy_ref, gffn_ref):
        i = pl.program_id(0)

        @pl.when(i == 0)
        def _():
            gffn_ref[...] = jnp.zeros_like(gffn_ref)

        db = dh2b_ref[...]
        dv = jnp.zeros((TM, D), _F32)
        for b in range(4):
            dact = _dot_nt(db, wd_ref[b])
            gate = gu_ref[b]
            up = gu_ref[4 + b]
            sg = _sigmoid(gate)
            dgate = ((dact * up) * _dsilu(gate, sg)).astype(_BF)
            dup = (dact * (gate * sg)).astype(_BF)
            dgu_ref[b] = dgate
            dgu_ref[4 + b] = dup
            dv = dv + _dot_nt(dgate, wgu_ref[b]) + _dot_nt(dup, wgu_ref[4 + b])
        n, r = _rms_fwd(h1_ref[...])
        gffn_ref[...] += jnp.sum(dv * n, axis=0, keepdims=True)
        dh1 = dh2_ref[...] + _rms_bwd(dv * g_ref[...], n, r)
        dh1_ref[...] = dh1
        dh1b = dh1.astype(_BF)
        dh1b_ref[...] = dh1b
        dy_ref[...] = _dot_nt(dh1b, wo_ref[...])

    tile = pl.BlockSpec((TM, D), lambda i: (i, 0))
    return pl.pallas_call(
        body, name="ffn_bwd", grid=(t_pad // TM,),
        in_specs=[tile, tile, pl.BlockSpec((N_DEV, TM, FFB), lambda i: (0, i, 0)), tile, _full((1, D)),
                  _const((N_DEV, D, FFB)), _const((4, FFB, D)), _const((D, D))],
        out_specs=[pl.BlockSpec((N_DEV, TM, FFB), lambda i: (0, i, 0)), tile, tile, tile, _full((1, D))],
        out_shape=[_S((N_DEV, t_pad, FFB), _BF), _S((t_pad, D), _F32), _S((t_pad, D), _BF),
                   _S((t_pad, D), _F32), _S((1, D), _F32)],
        compiler_params=_cp(("arbitrary",)),
    )(dh2, dh2b, gu, h1, g_ffn, w_gu, w_down, w_out)


def _mixer_bwd(p, hs, o, sc, dy, wr, wi, vec, hb, g_hg):
    t_pad = p.shape[0]
    nt = t_pad // TM
    nc_t = TM // HC

    def rev(i):
        return nt - 1 - i

    def body(p_ref, pprev_ref, hs_ref, hprev_ref, o_ref, sc_ref, dy_ref, wr_ref, wi_ref, vec_ref, hb_ref, ghg_ref,
             dp_ref, gvec_ref, gw_ref, xbuf, hbuf, dbuf, a_s, g_s, ccar, dst):
        i = pl.program_id(0)
        first_tile = i == nt - 1

        @pl.when(i == 0)
        def _():
            gvec_ref[...] = jnp.zeros_like(gvec_ref)
            gw_ref[...] = jnp.zeros_like(gw_ref)
            dbuf[pl.ds(TM, 8), :] = jnp.zeros((8, D_RG), _F32)
            ccar[...] = jnp.zeros_like(ccar)
            dst[...] = jnp.zeros_like(dst)

        def acc(row, val):
            gvec_ref[row:row + 1, :] += jnp.sum(val, axis=0, keepdims=True)

        keep = jnp.where(first_tile, 0.0, 1.0)
        x = p_ref[:, pl.ds(0, D_RG)]
        xbuf[pl.ds(0, 8), :] = pprev_ref[...] * keep
        xbuf[pl.ds(8, TM), :] = x
        xc = _conv(xbuf, vec_ref)
        r, ig, a, s, nsp8 = _rg_gates(xc, wr_ref, wi_ref, vec_ref)
        h = hs_ref[...]
        hbuf[pl.ds(0, 8), :] = hprev_ref[...] * keep
        hbuf[pl.ds(8, TM), :] = h
        hm1 = hbuf[pl.ds(7, TM), :]
        gr = p_ref[:, pl.ds(D_RG, D_RG)]
        gel, dgel = _gelu_parts(gr)
        n, rr = _rms_fwd(gel * h)
        dyn = dy_ref[:, pl.ds(0, D_RG)]
        acc(R_GRG, dyn * n)
        dpre = _rms_bwd(dyn * vec_ref[R_GRG:R_GRG + 1, :], n, rr)
        dp_ref[:, pl.ds(D_RG, D_RG)] = ((dpre * h) * dgel).astype(_BF)
        a_s[...] = a
        g_s[...] = dpre * gel

        def step(k, c):
            t = TM - 1 - k
            g = g_s[pl.ds(t, 1), :] + c
            g_s[pl.ds(t, 1), :] = g
            return a_s[pl.ds(t, 1), :] * g

        ccar[pl.ds(0, 1), :] = lax.fori_loop(0, TM, step, ccar[pl.ds(0, 1), :], unroll=8)
        gt = g_s[...]
        da = gt * hm1
        ixc = ig * xc
        ds = gt * ixc
        dig = (gt * s) * xc
        dxc = (gt * s) * ig
        dla = da * a - ds * ((a * a) / s)
        lam = vec_ref[R_LAM:R_LAM + 1, :]
        gvec_ref[R_LAM:R_LAM + 1, :] += jnp.sum(dla * r, axis=0, keepdims=True) * (LRU_C * _sigmoid(-lam))
        dzr = (dla * nsp8) * (r * (1.0 - r))
        dzi = dig * (ig * (1.0 - ig))
        acc(R_BR, dzr)
        acc(R_BI, dzi)
        xcb = xc.astype(_BF)
        dzrb = dzr.astype(_BF)
        dzib = dzi.astype(_BF)
        gw_ref[0] += _dot_tn(xcb, dzrb)
        gw_ref[1] += _dot_tn(xcb, dzib)
        dxc = dxc + _dot_nt(dzrb, wr_ref[...]) + _dot_nt(dzib, wi_ref[...])
        acc(R_CONVB, dxc)
        for j in range(4):
            acc(R_CONVW + j, dxc * xbuf[pl.ds(5 + j, TM), :])
        dbuf[pl.ds(0, TM), :] = dxc
        dx = vec_ref[R_CONVW + 3:R_CONVW + 4, :] * dxc
        for j in range(3):
            dx = dx + vec_ref[R_CONVW + j:R_CONVW + j + 1, :] * dbuf[pl.ds(3 - j, TM), :]
        dbuf[pl.ds(TM, 8), :] = dxc[0:8, :]
        dp_ref[:, pl.ds(0, D_RG)] = dx.astype(_BF)

        lb = _sigmoid(hb_ref[0:1, :] - hb_ref[1:2, :])
        ti = lax.broadcasted_iota(jnp.int32, (HC, HC), 0)
        si = lax.broadcasted_iota(jnp.int32, (HC, HC), 1)
        causal = ti >= si
        tri = causal.astype(_F32)
        tri_u = (si >= ti).astype(_F32)
        last_row = lax.broadcasted_iota(jnp.int32, (HC, 1), 0) == HC - 1
        ghg = ghg_ref[...]

        def chunk(cc, carry):
            c = nc_t - 1 - cc
            rows = pl.ds(pl.multiple_of(c * HC, HC), HC)
            for hh in range(NH):
                lbh = lb[:, HD * hh:HD * (hh + 1)]
                q = _hg_chunk_fwd(p_ref, rows, hh, lbh, tri)
                s0 = sc_ref[c, hh]
                s0b = s0.astype(_BF)
                dst_h = dst[hh]
                dstb = dst_h.astype(_BF)
                qdb, kdb = q["qd"].astype(_BF), q["kd"].astype(_BF)
                qeb, keb = q["qe"].astype(_BF), q["ke"].astype(_BF)
                vb = q["v"].astype(_BF)
                amat = jnp.where(causal, _dot_nt(qdb, kdb), 0.0)
                o = o_ref[rows, pl.ds(HD * hh, HD)]
                hg = p_ref[rows, pl.ds(2 * D_RG + 3 * D_HG + HD * hh, HD)]
                sh = _sigmoid(hg)
                n_o, r_o = _rms_fwd(o)
                dyh = dy_ref[rows, pl.ds(D_RG + HD * hh, HD)]
                dp_ref[rows, pl.ds(2 * D_RG + 3 * D_HG + HD * hh, HD)] = (
                    (dyh * (n_o * ghg)) * _dsilu(hg, sh)).astype(_BF)
                dn = dyh * (hg * sh)
                gvec_ref[R_GHG:R_GHG + 1, pl.ds(0, HD)] += jnp.sum(dn * n_o, axis=0, keepdims=True)
                dob = _rms_bwd(dn * ghg, n_o, r_o).astype(_BF)
                da_m = jnp.where(causal, _dot_nt(dob, vb), 0.0).astype(_BF)
                dqd = _dot(da_m, kdb)
                dkd = _dot_tn(da_m, qdb)
                dqe = _dot(dob, s0b)
                dke = _dot(vb, dstb)
                dv = _dot_tn(amat.astype(_BF), dob) + _dot_nt(keb, dstb)
                d_end = jnp.sum(s0 * dst_h, axis=0, keepdims=True)
                dst[hh] = _dot_tn(dob, qeb) + q["e_end"] * dst_h
                dq = dqd * q["e_q"] + dqe * q["e_b"]
                dk = dkd * q["e_k"] + dke * q["e_l"]
                dkeke = dke * q["ke"]
                db = dqd * qdb.astype(_F32) - dkd * kdb.astype(_F32) + dqe * q["qe"] - dkeke
                extra = jnp.sum(dkeke, axis=0, keepdims=True) + d_end * q["e_end"]
                db = db + jnp.where(last_row, extra, 0.0)
                dlf = _dot_hi(tri_u, db)
                df = dlf / q["f"] - dk
                sg = q["sg"]
                gvec_ref[R_HB0:R_HB0 + 1, pl.ds(HD * hh, HD)] += jnp.sum(df * (1.0 - sg), axis=0, keepdims=True)
                dp_ref[rows, pl.ds(2 * D_RG + HD * hh, HD)] = (dq * _dsilu(q["hq"], q["sq"])).astype(_BF)
                dp_ref[rows, pl.ds(2 * D_RG + D_HG + HD * hh, HD)] = (
                    (df * (1.0 - lbh)) * (sg * (1.0 - sg))).astype(_BF)
                dp_ref[rows, pl.ds(2 * D_RG + 2 * D_HG + HD * hh, HD)] = dv.astype(_BF)
            return carry

        lax.fori_loop(0, nc_t, chunk, 0)

        @pl.when(i == nt - 1)
        def _():
            glb = gvec_ref[R_HB0:R_HB0 + 1, :] * (lb * (1.0 - lb))
            gvec_ref[R_HB0:R_HB0 + 1, :] = glb
            gvec_ref[R_HB1:R_HB1 + 1, :] = -glb

    return pl.pallas_call(
        body, name="mixer_bwd", grid=(nt,),
        in_specs=[pl.BlockSpec((TM, D_IN), lambda i: (rev(i), 0)),
                  pl.BlockSpec((8, D_RG), lambda i: (jnp.maximum(rev(i) * (TM // 8) - 1, 0), 0)),
                  pl.BlockSpec((TM, D_RG), lambda i: (rev(i), 0)),
                  pl.BlockSpec((8, D_RG), lambda i: (jnp.maximum(rev(i) * (TM // 8) - 1, 0), 0)),
                  pl.BlockSpec((TM, D_HG), lambda i: (rev(i), 0)),
                  pl.BlockSpec((nc_t, NH, HD, HD), lambda i: (rev(i), 0, 0, 0)),
                  pl.BlockSpec((TM, D), lambda i: (rev(i), 0)),
                  _full((D_RG, D_RG)), _full((D_RG, D_RG)), _full((16, D_RG)), _full((2, D_HG)), _full((1, HD))],
        out_specs=[pl.BlockSpec((TM, D_IN), lambda i: (rev(i), 0)), _full((16, D_RG)), _full((2, D_RG, D_RG))],
        out_shape=[_S((t_pad, D_IN), _BF), _S((16, D_RG), _F32), _S((2, D_RG, D_RG), _F32)],
        scratch_shapes=[pltpu.VMEM((TM + 8, D_RG), _F32), pltpu.VMEM((TM + 8, D_RG), _F32),
                        pltpu.VMEM((TM + 8, D_RG), _F32), pltpu.VMEM((TM, D_RG), _F32),
                        pltpu.VMEM((TM, D_RG), _F32), pltpu.VMEM((8, D_RG), _F32),
                        pltpu.VMEM((NH, HD, HD), _F32)],
        compiler_params=_cp(("arbitrary",)),
    )(p, p, hs, hs, o, sc, dy, wr, wi, vec, hb, g_hg)


def _inproj_bwd(dp, w_in, h0, dh1, g_mix):
    t_pad = dp.shape[0]

    def body(dp_ref, w_ref, h_ref, dh1_ref, g_ref, dh0_ref, gmix_ref):
        i = pl.program_id(0)

        @pl.when(i == 0)
        def _():
            gmix_ref[...] = jnp.zeros_like(gmix_ref)

        du = jnp.zeros((TM, D), _F32)
        for j in range(N_DEV):
            du = du + _dot_nt(dp_ref[:, WIN_B * j:WIN_B * (j + 1)], w_ref[j])
        n, r = _rms_fwd(h_ref[...])
        gmix_ref[...] += jnp.sum(du * n, axis=0, keepdims=True)
        dh0_ref[...] = dh1_ref[...] + _rms_bwd(du * g_ref[...], n, r)

    tile = pl.BlockSpec((TM, D), lambda i: (i, 0))
    return pl.pallas_call(
        body, name="inproj_bwd", grid=(t_pad // TM,),
        in_specs=[pl.BlockSpec((TM, D_IN), lambda i: (i, 0)), _const((N_DEV, D, WIN_B)), tile, tile, _full((1, D))],
        out_specs=[tile, _full((1, D))],
        out_shape=[_S((t_pad, D), _F32), _S((1, D), _F32)],
        compiler_params=_cp(("arbitrary",)),
    )(dp, w_in, h0, dh1, g_mix)


def _wgrad(name, a, b, a_spec, b_spec, n_blocks, out_block):
    def body(a_ref, b_ref, o_ref):
        av = a_ref[0] if len(a_ref.shape) == 3 else a_ref[...]
        bv = b_ref[0] if len(b_ref.shape) == 3 else b_ref[...]
        o_ref[0] = _dot_tn(av, bv).astype(_BF)

    return pl.pallas_call(
        body, name=name, grid=(n_blocks,),
        in_specs=[a_spec, b_spec],
        out_specs=pl.BlockSpec((1,) + out_block, lambda j: (j, 0, 0)),
        out_shape=_S((n_blocks,) + out_block, _BF),
        compiler_params=_cp(("arbitrary",)),
    )(a, b)


def _coords():
    return lax.axis_index("x"), lax.axis_index("y"), lax.axis_index("c")


def _allgather(locals_, out_dtypes):
    na = len(locals_)

    def body(*refs):
        ins, outs, stage = refs[:na], refs[na:2 * na], refs[2 * na:3 * na]
        send_sems, recv_sems, local_sems = refs[3 * na:]
        x, y, c = _coords()
        me = 4 * x + 2 * y + c
        sibling = (x, y, 1 - c)
        chips = [(1 - x, y), (x, 1 - y), (1 - x, 1 - y)]

        def slot(px, py, pc):
            return 4 * px + 2 * py + pc

        def copy(a, k, block, to, src=None):
            return pltpu.make_async_remote_copy(
                src_ref=outs[a].at[block] if src is None else src, dst_ref=outs[a].at[block],
                send_sem=send_sems.at[7 * a + k], recv_sem=recv_sems.at[7 * a + k],
                device_id=to, device_id_type=_MESH)

        for a in range(na):
            stage[a][...] = ins[a][...].astype(out_dtypes[a])
        mine = [pltpu.make_async_copy(stage[a], outs[a].at[me], local_sems.at[a]) for a in range(na)]
        for cp in mine:
            cp.start()
        first = []
        for a in range(na):
            first.append(copy(a, 0, me, sibling, src=stage[a]))
            first += [copy(a, 1 + j, me, (*chip, c), src=stage[a]) for j, chip in enumerate(chips)]
        for cp in first:
            cp.start()
        passed = []
        for j, chip in enumerate(chips):
            for a in range(na):
                copy(a, 1 + j, slot(*chip, c), (x, y, c)).wait_recv()
                cp = copy(a, 4 + j, slot(*chip, c), sibling)
                cp.start()
                passed.append(cp)
        for a in range(na):
            copy(a, 0, slot(x, y, 1 - c), (x, y, c)).wait_recv()
        for j, chip in enumerate(chips):
            for a in range(na):
                copy(a, 4 + j, slot(*chip, 1 - c), (x, y, c)).wait_recv()
        for cp in first + passed:
            cp.wait_send()
        for cp in mine:
            cp.wait()

    return pl.pallas_call(
        body, name="allgather_weights",
        in_specs=[pl.BlockSpec(memory_space=pltpu.VMEM)] * na,
        out_specs=[pl.BlockSpec(memory_space=pl.ANY)] * na,
        out_shape=[_S((N_DEV,) + l.shape, dt) for l, dt in zip(locals_, out_dtypes)],
        scratch_shapes=[pltpu.VMEM(l.shape, dt) for l, dt in zip(locals_, out_dtypes)]
        + [pltpu.SemaphoreType.DMA((7 * na,)), pltpu.SemaphoreType.DMA((7 * na,)), pltpu.SemaphoreType.DMA((na,))],
        compiler_params=pltpu.CompilerParams(vmem_limit_bytes=VMEM_LIMIT),
    )(*locals_)


def _exchange(scatter, gather):
    ns, ng = len(scatter), len(gather)
    na = ns + ng

    def body(*refs):
        ins, outs = refs[:na], refs[na:2 * na]
        send_sems, recv_sems, local_sems = refs[2 * na:]
        x, y, c = _coords()
        me = 4 * x + 2 * y + c
        local, sends = [], []
        for a in range(na):
            src = ins[a].at[me] if a < ns else ins[a]
            cp = pltpu.make_async_copy(src, outs[a].at[me], local_sems.at[a])
            cp.start()
            local.append(cp)
        for r in range(1, N_DEV):
            px, py, pc = x ^ (r >> 2), y ^ ((r >> 1) & 1), c ^ (r & 1)
            peer = 4 * px + 2 * py + pc
            for a in range(na):
                src = ins[a].at[peer] if a < ns else ins[a]
                cp = pltpu.make_async_remote_copy(
                    src_ref=src, dst_ref=outs[a].at[me], send_sem=send_sems.at[7 * a + r - 1],
                    recv_sem=recv_sems.at[7 * a + r - 1], device_id=(px, py, pc), device_id_type=_MESH)
                cp.start()
                sends.append(cp)
        for r in range(1, N_DEV):
            px, py, pc = x ^ (r >> 2), y ^ ((r >> 1) & 1), c ^ (r & 1)
            peer = 4 * px + 2 * py + pc
            for a in range(na):
                src = ins[a].at[me] if a < ns else ins[a]
                pltpu.make_async_remote_copy(
                    src_ref=src, dst_ref=outs[a].at[peer], send_sem=send_sems.at[7 * a + r - 1],
                    recv_sem=recv_sems.at[7 * a + r - 1], device_id=(px, py, pc), device_id_type=_MESH).wait_recv()
        for cp in sends:
            cp.wait_send()
        for cp in local:
            cp.wait()

    arrs = list(scatter) + list(gather)
    return pl.pallas_call(
        body, name="exchange_grads",
        in_specs=[pl.BlockSpec(memory_space=pl.ANY)] * na,
        out_specs=[pl.BlockSpec(memory_space=pl.ANY)] * na,
        out_shape=[_S(s.shape, s.dtype) for s in scatter] + [_S((N_DEV,) + g.shape, g.dtype) for g in gather],
        scratch_shapes=[pltpu.SemaphoreType.DMA((7 * na,)), pltpu.SemaphoreType.DMA((7 * na,)),
                        pltpu.SemaphoreType.DMA((na,))],
    )(*arrs)


def _adamw_math(w, g, m, v):
    m2 = ADAM_B1 * m + (1.0 - ADAM_B1) * g
    v2 = ADAM_B2 * v + (1.0 - ADAM_B2) * (g * g)
    m_hat = m2 / (1.0 - ADAM_B1 ** ADAM_STEP)
    v_hat = v2 / (1.0 - ADAM_B2 ** ADAM_STEP)
    delta = -ADAM_LR * (m_hat / (jnp.sqrt(v_hat) + ADAM_EPS) + ADAM_WD * w)
    return delta, m2, v2


def _adamw_big(name, recv, w, m, v, rows):
    r_all, c_all = w.shape

    def body(r_ref, w_ref, m_ref, v_ref, g_out, d_out, m_out, v_out):
        g = r_ref[0].astype(_F32)
        for k in range(1, N_DEV):
            g = g + r_ref[k].astype(_F32)
        delta, m2, v2 = _adamw_math(w_ref[...], g, m_ref[...], v_ref[...])
        g_out[...] = g
        d_out[...] = delta
        m_out[...] = m2
        v_out[...] = v2

    tile = pl.BlockSpec((rows, c_all), lambda i: (i, 0))
    return pl.pallas_call(
        body, name=name, grid=(r_all // rows,),
        in_specs=[pl.BlockSpec((N_DEV, rows, c_all), lambda i: (0, i, 0)), tile, tile, tile],
        out_specs=[tile] * 4,
        out_shape=[_S(w.shape, _F32)] * 4,
        compiler_params=_cp(("arbitrary",)),
    )(recv, w, m, v)


def _adamw_small(gathered, slices, wmv):
    ng, npar = len(gathered), len(slices)

    def body(*refs):
        g_refs = refs[:ng]
        wmv_refs = refs[ng:ng + 3 * npar]
        outs = refs[ng + 3 * npar:]
        for i, (ai, r0, nr, ncol) in enumerate(slices):
            g = g_refs[ai][0, pl.ds(r0, nr), pl.ds(0, ncol)]
            for k in range(1, N_DEV):
                g = g + g_refs[ai][k, pl.ds(r0, nr), pl.ds(0, ncol)]
            w_ref, m_ref, v_ref = wmv_refs[3 * i:3 * i + 3]
            delta, m2, v2 = _adamw_math(w_ref[...], g, m_ref[...], v_ref[...])
            outs[4 * i][...] = g
            outs[4 * i + 1][...] = delta
            outs[4 * i + 2][...] = m2
            outs[4 * i + 3][...] = v2

    flat = [t for trip in wmv for t in trip]
    out_shape = []
    for w, _, _ in wmv:
        out_shape += [_S(w.shape, _F32)] * 4
    return pl.pallas_call(
        body, name="adamw_small", out_shape=out_shape,
        compiler_params=pltpu.CompilerParams(vmem_limit_bytes=VMEM_LIMIT),
    )(*gathered, *flat)


def _block_diag(w):
    eye = jnp.eye(8, dtype=w.dtype)
    return (w[:, :, None, :] * eye[:, None, :, None]).reshape(D_RG, D_RG)


def _diag_blocks(g):
    return jnp.concatenate([g[64 * h:64 * (h + 1), 64 * h:64 * (h + 1)] for h in range(8)], axis=0)


def _local_step(x, tgt, meta, g_mix, w_in, vec, wr, wi, hb, g_hg, w_out, g_ffn, w_gu, w_down, g_fin):
    seq = x.shape[0]
    n_valid = N_META + seq
    t_pad = -(-n_valid // TM) * TM
    h0 = jnp.concatenate([meta, x, jnp.zeros((t_pad - n_valid, D), _F32)], axis=0)
    tgt_p = jnp.concatenate([jnp.zeros((N_META, D), _F32), tgt, jnp.zeros((t_pad - n_valid, D), _F32)], axis=0)

    p, u = _inproj(h0, g_mix, w_in)
    y, hs, o, sc = _mixer_fwd(p, wr, wi, vec, hb, g_hg)
    h1, v = _outproj(h0, y, w_out, g_ffn)
    gu, act, dh2, dh2b, loss, gfin = _ffn_loss(v, h1, w_gu, w_down, g_fin, tgt_p, n_valid)

    dgu, dh1, dh1b, dy, gffn = _ffn_bwd(dh2, dh2b, gu, h1, g_ffn, w_gu, w_down, w_out)
    g_wdown = _wgrad("wgrad_down", act, dh2b, pl.BlockSpec((1, t_pad, FFB), lambda j: (j, 0, 0)),
                     pl.BlockSpec((t_pad, D), lambda j: (0, 0)), 4, (FFB, D))
    g_wgu = _wgrad("wgrad_gate_up", v, dgu, pl.BlockSpec((t_pad, D), lambda j: (0, 0)),
                   pl.BlockSpec((1, t_pad, FFB), lambda j: (j, 0, 0)), N_DEV, (D, FFB))
    g_wout = _wgrad("wgrad_out", y, dh1b, pl.BlockSpec((t_pad, D // N_DEV), lambda j: (0, j)),
                    pl.BlockSpec((t_pad, D), lambda j: (0, 0)), N_DEV, (D // N_DEV, D))
    dp, gvec, gw = _mixer_bwd(p, hs, o, sc, dy, wr, wi, vec, hb, g_hg)
    dh0, gmix = _inproj_bwd(dp, w_in, h0, dh1, g_mix)
    g_win = _wgrad("wgrad_in", u, dp, pl.BlockSpec((t_pad, D), lambda j: (0, 0)),
                   pl.BlockSpec((t_pad, WIN_B), lambda j: (0, j)), N_DEV, (D, WIN_B))
    return loss, dh0, (g_win, g_wgu, g_wout, g_wdown), (gmix, gffn, gfin, gvec, gw)


def kernel(x, meta_tokens, mix_norm_g, w_in, conv_w, conv_b, w_rgate, b_rgate, w_igate, b_igate, lru_lambda, rg_norm_g, hg_lower_bound, hg_norm_g, w_out, ffn_norm_g, w_gate_up, w_down, final_norm_g, loss_target, m_meta_tokens, m_mix_norm_g, m_w_in, m_conv_w, m_conv_b, m_w_rgate, m_b_rgate, m_w_igate, m_b_igate, m_lru_lambda, m_rg_norm_g, m_hg_lower_bound, m_hg_norm_g, m_w_out, m_ffn_norm_g, m_w_gate_up, m_w_down, m_final_norm_g, v_meta_tokens, v_mix_norm_g, v_w_in, v_conv_w, v_conv_b, v_w_rgate, v_b_rgate, v_w_igate, v_b_igate, v_lru_lambda, v_rg_norm_g, v_hg_lower_bound, v_hg_norm_g, v_w_out, v_ffn_norm_g, v_w_gate_up, v_w_down, v_final_norm_g):
    seq = x.shape[1]
    me = 4 * lax.axis_index("x") + 2 * lax.axis_index("y") + lax.axis_index("c")

    small_l = jnp.concatenate([meta_tokens, jnp.pad(conv_w[0], ((0, 4), (0, 64)))], axis=0)
    w_in_g, w_gu_g, w_out_g, w_down_g, small_g = _allgather(
        [w_in[0], w_gate_up[0], w_out[0], w_down[0], small_l], [_BF, _BF, _BF, _BF, _F32])
    meta_full = jnp.transpose(small_g[:, :N_META, :], (1, 0, 2)).reshape(N_META, D)
    conv_w_full = jnp.transpose(small_g[:, N_META:N_META + 4, :64], (1, 0, 2)).reshape(4, D_RG)
    vec = jnp.concatenate([conv_b, b_rgate, b_igate, lru_lambda, rg_norm_g, jnp.zeros((3, D_RG), _F32),
                           conv_w_full, jnp.zeros((4, D_RG), _F32)], axis=0)
    wr = _block_diag(w_rgate[0]).astype(_BF)
    wi = _block_diag(w_igate[0]).astype(_BF)

    loss, dh0, big, small = _local_step(
        x[0], loss_target[0], meta_full, mix_norm_g, w_in_g, vec, wr, wi, hg_lower_bound, hg_norm_g,
        w_out_g.reshape(D, D), ffn_norm_g, w_gu_g, w_down_g.reshape(4, FFB, D), final_norm_g.reshape(1, D))
    g_win, g_wgu, g_wout, g_wdown = big
    gmix, gffn, gfin, gvec, gw = small
    grad_x = dh0[N_META:N_META + seq][None]
    pack_a = jnp.concatenate([gmix, gffn, gfin, jnp.zeros((5, D), _F32), dh0[:N_META]], axis=0)
    pack_c = jnp.concatenate([_diag_blocks(gw[0]), _diag_blocks(gw[1])], axis=0)

    r_win, r_wgu, r_wout, r_wdown, all_a, all_b, all_c = _exchange(
        [g_win, g_wgu, g_wout, g_wdown.reshape(N_DEV, D_FF // N_DEV, D)], [pack_a, gvec, pack_c])

    outs = {}
    outs["w_in"] = _adamw_big("adamw_w_in", r_win, w_in[0], m_w_in[0], v_w_in[0], 256)
    outs["w_gate_up"] = _adamw_big("adamw_w_gate_up", r_wgu, w_gate_up[0], m_w_gate_up[0], v_w_gate_up[0], 256)
    outs["w_out"] = _adamw_big("adamw_w_out", r_wout, w_out[0], m_w_out[0], v_w_out[0], 128)
    outs["w_down"] = _adamw_big("adamw_w_down", r_wdown, w_down[0], m_w_down[0], v_w_down[0], 176)

    meta_part = lax.dynamic_slice_in_dim(all_a[:, R_META:R_META + N_META, :], me * 128, 128, axis=2)
    convw_part = lax.dynamic_slice_in_dim(all_b[:, R_CONVW:R_CONVW + 4, :], me * 64, 64, axis=2)
    gathered = [all_a, all_b, all_c, meta_part, convw_part]
    small_params = [
        ("meta_tokens", (3, 0, N_META, 128), (meta_tokens, m_meta_tokens, v_meta_tokens), (N_META, 128)),
        ("mix_norm_g", (0, R_GMIX, 1, D), (mix_norm_g, m_mix_norm_g, v_mix_norm_g), (1, D)),
        ("conv_w", (4, 0, 4, 64), (conv_w, m_conv_w, v_conv_w), (4, 64)),
        ("conv_b", (1, R_CONVB, 1, D_RG), (conv_b, m_conv_b, v_conv_b), (1, D_RG)),
        ("w_rgate", (2, 0, 512, 64), (w_rgate, m_w_rgate, v_w_rgate), (512, 64)),
        ("b_rgate", (1, R_BR, 1, D_RG), (b_rgate, m_b_rgate, v_b_rgate), (1, D_RG)),
        ("w_igate", (2, 512, 512, 64), (w_igate, m_w_igate, v_w_igate), (512, 64)),
        ("b_igate", (1, R_BI, 1, D_RG), (b_igate, m_b_igate, v_b_igate), (1, D_RG)),
        ("lru_lambda", (1, R_LAM, 1, D_RG), (lru_lambda, m_lru_lambda, v_lru_lambda), (1, D_RG)),
        ("rg_norm_g", (1, R_GRG, 1, D_RG), (rg_norm_g, m_rg_norm_g, v_rg_norm_g), (1, D_RG)),
        ("hg_lower_bound", (1, R_HB0, 2, D_HG), (hg_lower_bound, m_hg_lower_bound, v_hg_lower_bound), (2, D_HG)),
        ("hg_norm_g", (1, R_GHG, 1, HD), (hg_norm_g, m_hg_norm_g, v_hg_norm_g), (1, HD)),
        ("ffn_norm_g", (0, R_GFFN, 1, D), (ffn_norm_g, m_ffn_norm_g, v_ffn_norm_g), (1, D)),
        ("final_norm_g", (0, R_GFIN, 1, D), (final_norm_g, m_final_norm_g, v_final_norm_g), (1, D)),
    ]
    res = _adamw_small(gathered, [s[1] for s in small_params],
                       [tuple(t.reshape(s[3]) for t in s[2]) for s in small_params])
    for i, s in enumerate(small_params):
        outs[s[0]] = [r.reshape(s[2][0].shape) for r in res[4 * i:4 * i + 4]]
    for n, ref in (("w_in", w_in), ("w_gate_up", w_gate_up), ("w_out", w_out), ("w_down", w_down)):
        outs[n] = [r.reshape(ref.shape) for r in outs[n]]

    loss_all = lax.psum(loss[0, 0], ("x", "y", "c"))
    order = ["meta_tokens", "mix_norm_g", "w_in", "conv_w", "conv_b", "w_rgate", "b_rgate", "w_igate", "b_igate",
             "lru_lambda", "rg_norm_g", "hg_lower_bound", "hg_norm_g", "w_out", "ffn_norm_g", "w_gate_up", "w_down",
             "final_norm_g"]
    return (loss_all, grad_x, *[outs[n][0] for n in order], *[outs[n][1] for n in order],
            *[outs[n][2] for n in order], *[outs[n][3] for n in order])
```

```python
import functools

import jax
import jax.numpy as jnp
from jax import lax
from jax.experimental import pallas as pl
from jax.experimental.pallas import tpu as pltpu

_BF = jnp.bfloat16
_F32 = jnp.float32
_S = jax.ShapeDtypeStruct
_MESH = pl.DeviceIdType.MESH

N_DEV = 8
N_META = 16
D = 1024
D_RG = 512
D_HG = 512
HD = 128
NH = D_HG // HD
D_IN = 3072
D_FF = 2816
FFB = D_FF // 4
WIN_B = D_IN // N_DEV
EPS = 1e-6
LRU_C = 8.0
TM = 256
HC = 64
VMEM_LIMIT = 56 * 1024 * 1024

ADAM_LR = 0.001
ADAM_B1 = 0.9
ADAM_B2 = 0.999
ADAM_EPS = 1e-08
ADAM_WD = 0.01
ADAM_STEP = 10

R_CONVB, R_BR, R_BI, R_LAM, R_GRG, R_HB0, R_HB1, R_GHG, R_CONVW = 0, 1, 2, 3, 4, 5, 6, 7, 8
R_GMIX, R_GFFN, R_GFIN, R_META = 0, 1, 2, 8


def _cp(sem=None, **kw):
    return pltpu.CompilerParams(dimension_semantics=sem, vmem_limit_bytes=VMEM_LIMIT, **kw)


def _dot(a, b):
    return jnp.dot(a, b, preferred_element_type=_F32)


def _dot_nt(a, b):
    return lax.dot_general(a, b, (((1,), (1,)), ((), ())), preferred_element_type=_F32)


def _dot_tn(a, b):
    return lax.dot_general(a, b, (((0,), (0,)), ((), ())), preferred_element_type=_F32)


def _dot_hi(a, b):
    return jnp.dot(a, b, preferred_element_type=_F32, precision=lax.Precision.HIGHEST)


def _sigmoid(x):
    return jax.nn.sigmoid(x)


def _dsilu(x, s):
    return s * (1.0 + x * (1.0 - s))


_GELU_C = 0.7978845608028654


def _gelu_parts(x):
    t = jnp.tanh(_GELU_C * (x + 0.044715 * (x * x * x)))
    g = 0.5 * x * (1.0 + t)
    dg = 0.5 * (1.0 + t) + 0.5 * x * (1.0 - t * t) * (_GELU_C * (1.0 + 3.0 * 0.044715 * (x * x)))
    return g, dg


def _softplus(z):
    e = jnp.exp(-jnp.abs(z))
    w = 1.0 + e
    l1p = jnp.where(w == 1.0, e, jnp.log(w) * e / jnp.where(w == 1.0, 1.0, w - 1.0))
    return jnp.maximum(z, 0.0) + l1p


def _rms_fwd(x):
    r = lax.rsqrt(jnp.mean(x * x, axis=-1, keepdims=True) + EPS)
    return x * r, r


def _rms_bwd(dyg, n, r):
    return r * (dyg - n * jnp.mean(dyg * n, axis=-1, keepdims=True))


def _full(shape):
    nd = len(shape)
    return pl.BlockSpec(shape, lambda i: (0,) * nd)


def _const(shape):
    nd = len(shape)
    return pl.BlockSpec(shape, lambda i: (0,) * nd, pipeline_mode=pl.Buffered(1))


def _inproj(h0, g_mix, w_in):
    t_pad = h0.shape[0]

    def body(h_ref, g_ref, w_ref, p_ref, u_ref):
        n, _ = _rms_fwd(h_ref[...])
        u = (n * g_ref[...]).astype(_BF)
        u_ref[...] = u
        for j in range(N_DEV):
            p_ref[:, WIN_B * j:WIN_B * (j + 1)] = _dot(u, w_ref[j])

    return pl.pallas_call(
        body, name="inproj", grid=(t_pad // TM,),
        in_specs=[pl.BlockSpec((TM, D), lambda i: (i, 0)), _full((1, D)), _const((N_DEV, D, WIN_B))],
        out_specs=[pl.BlockSpec((TM, D_IN), lambda i: (i, 0)), pl.BlockSpec((TM, D), lambda i: (i, 0))],
        out_shape=[_S((t_pad, D_IN), _F32), _S((t_pad, D), _BF)],
        compiler_params=_cp(("arbitrary",)),
    )(h0, g_mix, w_in)


def _rg_gates(xc, wr_ref, wi_ref, vec_ref):
    xcb = xc.astype(_BF)
    r = _sigmoid(_dot(xcb, wr_ref[...]) + vec_ref[R_BR:R_BR + 1, :])
    ig = _sigmoid(_dot(xcb, wi_ref[...]) + vec_ref[R_BI:R_BI + 1, :])
    nsp8 = -LRU_C * _softplus(-vec_ref[R_LAM:R_LAM + 1, :])
    la = nsp8 * r
    a = jnp.exp(la)
    th = jnp.tanh(la)
    s = jnp.sqrt(-2.0 * th / (1.0 - th))
    return r, ig, a, s, nsp8


def _conv(xbuf, vec_ref):
    acc = vec_ref[R_CONVW:R_CONVW + 1, :] * xbuf[pl.ds(5, TM), :]
    for j in range(1, 4):
        acc = acc + vec_ref[R_CONVW + j:R_CONVW + j + 1, :] * xbuf[pl.ds(5 + j, TM), :]
    return vec_ref[R_CONVB:R_CONVB + 1, :] + acc


def _hg_chunk_fwd(p_ref, rows, h, lbh, tri):
    hq = p_ref[rows, pl.ds(2 * D_RG + HD * h, HD)]
    hf = p_ref[rows, pl.ds(2 * D_RG + D_HG + HD * h, HD)]
    v = p_ref[rows, pl.ds(2 * D_RG + 2 * D_HG + HD * h, HD)]
    sq = _sigmoid(hq)
    q = hq * sq
    sg = _sigmoid(hf)
    f = lbh + (1.0 - lbh) * sg
    k = 1.0 - f
    b = _dot_hi(tri, jnp.log(f))
    bm = b[HC // 2 - 1:HC // 2, :]
    bl = b[HC - 1:HC, :]
    e_q = jnp.exp(b - bm)
    e_k = jnp.exp(bm - b)
    e_b = jnp.exp(b)
    e_l = jnp.exp(bl - b)
    return dict(hq=hq, sq=sq, q=q, sg=sg, f=f, k=k, v=v, e_q=e_q, e_k=e_k, e_b=e_b, e_l=e_l,
                qd=q * e_q, kd=k * e_k, qe=q * e_b, ke=k * e_l, e_end=jnp.exp(bl))


def _mixer_fwd(p, wr, wi, vec, hb, g_hg, shards):
    t_pad = p.shape[0]
    nt = t_pad // TM
    nc_t = TM // HC
    nsh = len(shards)

    def body(p_ref, wr_ref, wi_ref, vec_ref, hb_ref, ghg_ref, *rest):
        sh_refs, rest = rest[:nsh], rest[nsh:]
        y_ref, hs_ref, o_ref, sc_ref = rest[:4]
        gath_refs, rest = rest[4:4 + nsh], rest[4 + nsh:]
        xbuf, a_s, b_s, hcar, st = rest[:5]
        gather = _Gather(sh_refs, gath_refs, rest[5:])
        i = pl.program_id(0)

        @pl.when(i == 0)
        def _():
            gather.start()
            xbuf[pl.ds(0, 8), :] = jnp.zeros((8, D_RG), _F32)
            hcar[...] = jnp.zeros_like(hcar)
            st[...] = jnp.zeros_like(st)

        x = p_ref[:, pl.ds(0, D_RG)]
        xbuf[pl.ds(8, TM), :] = x
        xc = _conv(xbuf, vec_ref)
        xbuf[pl.ds(0, 8), :] = x[TM - 8:, :]
        r, ig, a, s, _ = _rg_gates(xc, wr_ref, wi_ref, vec_ref)
        a_s[...] = a
        b_s[...] = s * (ig * xc)

        def step(t, h):
            h = a_s[pl.ds(t, 1), :] * h + b_s[pl.ds(t, 1), :]
            hs_ref[pl.ds(t, 1), :] = h
            return h

        hcar[pl.ds(0, 1), :] = lax.fori_loop(0, TM, step, hcar[pl.ds(0, 1), :], unroll=8)
        gel, _ = _gelu_parts(p_ref[:, pl.ds(D_RG, D_RG)])
        n, _ = _rms_fwd(gel * hs_ref[...])
        y_ref[:, pl.ds(0, D_RG)] = (n * vec_ref[R_GRG:R_GRG + 1, :]).astype(_BF)

        lb = _sigmoid(hb_ref[0:1, :] - hb_ref[1:2, :])
        ti = lax.broadcasted_iota(jnp.int32, (HC, HC), 0)
        si = lax.broadcasted_iota(jnp.int32, (HC, HC), 1)
        causal = ti >= si
        tri = causal.astype(_F32)

        def chunk(c, carry):
            rows = pl.ds(pl.multiple_of(c * HC, HC), HC)
            for h in range(NH):
                q = _hg_chunk_fwd(p_ref, rows, h, lb[:, HD * h:HD * (h + 1)], tri)
                s0 = st[h]
                sc_ref[c, h] = s0
                amat = jnp.where(causal, _dot_nt(q["qd"].astype(_BF), q["kd"].astype(_BF)), 0.0)
                vb = q["v"].astype(_BF)
                o = _dot_nt(q["qe"].astype(_BF), s0.astype(_BF)) + _dot(amat.astype(_BF), vb)
                st[h] = q["e_end"] * s0 + _dot_tn(vb, q["ke"].astype(_BF))
                o_ref[rows, pl.ds(HD * h, HD)] = o
                n_o, _ = _rms_fwd(o)
                hg = p_ref[rows, pl.ds(2 * D_RG + 3 * D_HG + HD * h, HD)]
                yh = (n_o * ghg_ref[...]) * (hg * _sigmoid(hg))
                y_ref[rows, pl.ds(D_RG + HD * h, HD)] = yh.astype(_BF)
            return carry

        lax.fori_loop(0, nc_t, chunk, 0)

        @pl.when(i == nt // 2)
        def _():
            gather.forward()

        @pl.when(i == nt - 1)
        def _():
            gather.finish()

    hbm = pl.BlockSpec(memory_space=pl.ANY)
    return pl.pallas_call(
        body, name="mixer_fwd", grid=(nt,),
        in_specs=[pl.BlockSpec((TM, D_IN), lambda i: (i, 0)), _full((D_RG, D_RG)), _full((D_RG, D_RG)),
                  _full((16, D_RG)), _full((2, D_HG)), _full((1, HD))] + [hbm] * nsh,
        out_specs=[pl.BlockSpec((TM, D), lambda i: (i, 0)), pl.BlockSpec((TM, D_RG), lambda i: (i, 0)),
                   pl.BlockSpec((TM, D_HG), lambda i: (i, 0)),
                   pl.BlockSpec((nc_t, NH, HD, HD), lambda i: (i, 0, 0, 0))] + [hbm] * nsh,
        out_shape=[_S((t_pad, D), _BF), _S((t_pad, D_RG), _F32), _S((t_pad, D_HG), _F32),
                   _S((t_pad // HC, NH, HD, HD), _F32)] + [_S((N_DEV,) + s.shape, s.dtype) for s in shards],
        scratch_shapes=[pltpu.VMEM((TM + 8, D_RG), _F32), pltpu.VMEM((TM, D_RG), _F32),
                        pltpu.VMEM((TM, D_RG), _F32), pltpu.VMEM((8, D_RG), _F32),
                        pltpu.VMEM((NH, HD, HD), _F32)] + _sem_shapes(nsh),
        compiler_params=_cp(("arbitrary",)),
    )(p, wr, wi, vec, hb, g_hg, *shards)


def _outproj(h0, y, w_out, g_ffn):
    t_pad = h0.shape[0]

    def body(h_ref, y_ref, w_ref, g_ref, h1_ref, v_ref):
        h1 = h_ref[...] + _dot(y_ref[...], w_ref[...])
        h1_ref[...] = h1
        n, _ = _rms_fwd(h1)
        v_ref[...] = (n * g_ref[...]).astype(_BF)

    return pl.pallas_call(
        body, name="outproj", grid=(t_pad // TM,),
        in_specs=[pl.BlockSpec((TM, D), lambda i: (i, 0)), pl.BlockSpec((TM, D), lambda i: (i, 0)),
                  _full((D, D)), _full((1, D))],
        out_specs=[pl.BlockSpec((TM, D), lambda i: (i, 0)), pl.BlockSpec((TM, D), lambda i: (i, 0))],
        out_shape=[_S((t_pad, D), _F32), _S((t_pad, D), _BF)],
        compiler_params=_cp(("arbitrary",)),
    )(h0, y, w_out, g_ffn)


def _ffn_loss(v, h1, w_gu, w_down, g_fin, tgt, n_valid):
    t_pad = v.shape[0]

    def body(v_ref, h1_ref, wgu_ref, wd_ref, g_ref, t_ref, gu_ref, act_ref, dh2_ref, dh2b_ref, loss_ref, gfin_ref):
        i = pl.program_id(0)

        @pl.when(i == 0)
        def _():
            loss_ref[...] = jnp.zeros_like(loss_ref)
            gfin_ref[...] = jnp.zeros_like(gfin_ref)

        vb = v_ref[...]
        h2 = h1_ref[...]
        for b in range(4):
            gate = _dot(vb, wgu_ref[b])
            up = _dot(vb, wgu_ref[4 + b])
            gu_ref[b] = gate
            gu_ref[4 + b] = up
            act = ((gate * _sigmoid(gate)) * up).astype(_BF)
            act_ref[b] = act
            h2 = h2 + _dot(act, wd_ref[b])
        n, r = _rms_fwd(h2)
        out = n * g_ref[...]
        row = i * TM + lax.broadcasted_iota(jnp.int32, (TM, 1), 0)
        valid = (row >= N_META) & (row < n_valid)
        err = jnp.where(valid, out - t_ref[...], 0.0)
        loss_ref[...] += (0.5 / D) * jnp.sum(err * err)
        dout = err * (1.0 / D)
        gfin_ref[...] += jnp.sum(dout * n, axis=0, keepdims=True)
        dh2 = _rms_bwd(dout * g_ref[...], n, r)
        dh2_ref[...] = dh2
        dh2b_ref[...] = dh2.astype(_BF)

    return pl.pallas_call(
        body, name="ffn_loss", grid=(t_pad // TM,),
        in_specs=[pl.BlockSpec((TM, D), lambda i: (i, 0)), pl.BlockSpec((TM, D), lambda i: (i, 0)),
                  _const((N_DEV, D, FFB)), _const((4, FFB, D)), _full((1, D)),
                  pl.BlockSpec((TM, D), lambda i: (i, 0))],
        out_specs=[pl.BlockSpec((N_DEV, TM, FFB), lambda i: (0, i, 0)), pl.BlockSpec((4, TM, FFB), lambda i: (0, i, 0)),
                   pl.BlockSpec((TM, D), lambda i: (i, 0)), pl.BlockSpec((TM, D), lambda i: (i, 0)),
                   _full((8, 128)), _full((1, D))],
        out_shape=[_S((N_DEV, t_pad, FFB), _F32), _S((4, t_pad, FFB), _BF), _S((t_pad, D), _F32),
                   _S((t_pad, D), _BF), _S((8, 128), _F32), _S((1, D), _F32)],
        compiler_params=_cp(("arbitrary",)),
    )(v, h1, w_gu, w_down, g_fin, tgt)


def _ffn_bwd(dh2, dh2b, gu, h1, g_ffn, w_gu, w_down, w_out):
    t_pad = dh2.shape[0]

    def body(dh2_ref, dh2b_ref, gu_ref, h1_ref, g_ref, wgu_ref, wd_ref, wo_ref,
             dgu_ref, dh1_ref, dh1b_ref, dy_ref, gffn_ref):
        i = pl.program_id(0)

        @pl.when(i == 0)
        def _():
            gffn_ref[...] = jnp.zeros_like(gffn_ref)

        db = dh2b_ref[...]
        dv = jnp.zeros((TM, D), _F32)
        for b in range(4):
            dact = _dot_nt(db, wd_ref[b])
            gate = gu_ref[b]
            up = gu_ref[4 + b]
            sg = _sigmoid(gate)
            dgate = ((dact * up) * _dsilu(gate, sg)).astype(_BF)
            dup = (dact * (gate * sg)).astype(_BF)
            dgu_ref[b] = dgate
            dgu_ref[4 + b] = dup
            dv = dv + _dot_nt(dgate, wgu_ref[b]) + _dot_nt(dup, wgu_ref[4 + b])
        n, r = _rms_fwd(h1_ref[...])
        gffn_ref[...] += jnp.sum(dv * n, axis=0, keepdims=True)
        dh1 = dh2_ref[...] + _rms_bwd(dv * g_ref[...], n, r)
        dh1_ref[...] = dh1
        dh1b = dh1.astype(_BF)
        dh1b_ref[...] = dh1b
        dy_ref[...] = _dot_nt(dh1b, wo_ref[...])

    tile = pl.BlockSpec((TM, D), lambda i: (i, 0))
    return pl.pallas_call(
        body, name="ffn_bwd", grid=(t_pad // TM,),
        in_specs=[tile, tile, pl.BlockSpec((N_DEV, TM, FFB), lambda i: (0, i, 0)), tile, _full((1, D)),
                  _const((N_DEV, D, FFB)), _const((4, FFB, D)), _const((D, D))],
        out_specs=[pl.BlockSpec((N_DEV, TM, FFB), lambda i: (0, i, 0)), tile, tile, tile, _full((1, D))],
        out_shape=[_S((N_DEV, t_pad, FFB), _BF), _S((t_pad, D), _F32), _S((t_pad, D), _BF),
                   _S((t_pad, D), _F32), _S((1, D), _F32)],
        compiler_params=_cp(("arbitrary",)),
    )(dh2, dh2b, gu, h1, g_ffn, w_gu, w_down, w_out)


def _mixer_bwd(p, hs, o, sc, dy, wr, wi, vec, hb, g_hg, scatter):
    t_pad = p.shape[0]
    nt = t_pad // TM
    nc_t = TM // HC
    nsc = len(scatter)

    def rev(i):
        return nt - 1 - i

    def body(p_ref, pprev_ref, hs_ref, hprev_ref, o_ref, sc_ref, dy_ref, wr_ref, wi_ref, vec_ref, hb_ref, ghg_ref,
             *rest):
        send_refs, rest = rest[:nsc], rest[nsc:]
        dp_ref, gvec_ref, gw_ref = rest[:3]
        recv_refs, rest = rest[3:3 + nsc], rest[3 + nsc:]
        xbuf, hbuf, dbuf, a_s, g_s, ccar, dst = rest[:7]
        exchange = _Exchange(send_refs, [], recv_refs, rest[7:])
        i = pl.program_id(0)
        first_tile = i == nt - 1

        @pl.when(i == 0)
        def _():
            exchange.start()
            gvec_ref[...] = jnp.zeros_like(gvec_ref)
            gw_ref[...] = jnp.zeros_like(gw_ref)
            dbuf[pl.ds(TM, 8), :] = jnp.zeros((8, D_RG), _F32)
            ccar[...] = jnp.zeros_like(ccar)
            dst[...] = jnp.zeros_like(dst)

        def acc(row, val):
            gvec_ref[row:row + 1, :] += jnp.sum(val, axis=0, keepdims=True)

        keep = jnp.where(first_tile, 0.0, 1.0)
        x = p_ref[:, pl.ds(0, D_RG)]
        xbuf[pl.ds(0, 8), :] = pprev_ref[...] * keep
        xbuf[pl.ds(8, TM), :] = x
        xc = _conv(xbuf, vec_ref)
        r, ig, a, s, nsp8 = _rg_gates(xc, wr_ref, wi_ref, vec_ref)
        h = hs_ref[...]
        hbuf[pl.ds(0, 8), :] = hprev_ref[...] * keep
        hbuf[pl.ds(8, TM), :] = h
        hm1 = hbuf[pl.ds(7, TM), :]
        gr = p_ref[:, pl.ds(D_RG, D_RG)]
        gel, dgel = _gelu_parts(gr)
        n, rr = _rms_fwd(gel * h)
        dyn = dy_ref[:, pl.ds(0, D_RG)]
        acc(R_GRG, dyn * n)
        dpre = _rms_bwd(dyn * vec_ref[R_GRG:R_GRG + 1, :], n, rr)
        dp_ref[:, pl.ds(D_RG, D_RG)] = ((dpre * h) * dgel).astype(_BF)
        a_s[...] = a
        g_s[...] = dpre * gel

        def step(k, c):
            t = TM - 1 - k
            g = g_s[pl.ds(t, 1), :] + c
            g_s[pl.ds(t, 1), :] = g
            return a_s[pl.ds(t, 1), :] * g

        ccar[pl.ds(0, 1), :] = lax.fori_loop(0, TM, step, ccar[pl.ds(0, 1), :], unroll=8)
        gt = g_s[...]
        da = gt * hm1
        ixc = ig * xc
        ds = gt * ixc
        dig = (gt * s) * xc
        dxc = (gt * s) * ig
        dla = da * a - ds * ((a * a) / s)
        lam = vec_ref[R_LAM:R_LAM + 1, :]
        gvec_ref[R_LAM:R_LAM + 1, :] += jnp.sum(dla * r, axis=0, keepdims=True) * (LRU_C * _sigmoid(-lam))
        dzr = (dla * nsp8) * (r * (1.0 - r))
        dzi = dig * (ig * (1.0 - ig))
        acc(R_BR, dzr)
        acc(R_BI, dzi)
        xcb = xc.astype(_BF)
        dzrb = dzr.astype(_BF)
        dzib = dzi.astype(_BF)
        gw_ref[0] += _dot_tn(xcb, dzrb)
        gw_ref[1] += _dot_tn(xcb, dzib)
        dxc = dxc + _dot_nt(dzrb, wr_ref[...]) + _dot_nt(dzib, wi_ref[...])
        acc(R_CONVB, dxc)
        for j in range(4):
            acc(R_CONVW + j, dxc * xbuf[pl.ds(5 + j, TM), :])
        dbuf[pl.ds(0, TM), :] = dxc
        dx = vec_ref[R_CONVW + 3:R_CONVW + 4, :] * dxc
        for j in range(3):
            dx = dx + vec_ref[R_CONVW + j:R_CONVW + j + 1, :] * dbuf[pl.ds(3 - j, TM), :]
        dbuf[pl.ds(TM, 8), :] = dxc[0:8, :]
        dp_ref[:, pl.ds(0, D_RG)] = dx.astype(_BF)

        lb = _sigmoid(hb_ref[0:1, :] - hb_ref[1:2, :])
        ti = lax.broadcasted_iota(jnp.int32, (HC, HC), 0)
        si = lax.broadcasted_iota(jnp.int32, (HC, HC), 1)
        causal = ti >= si
        tri = causal.astype(_F32)
        tri_u = (si >= ti).astype(_F32)
        last_row = lax.broadcasted_iota(jnp.int32, (HC, 1), 0) == HC - 1
        ghg = ghg_ref[...]

        def chunk(cc, carry):
            c = nc_t - 1 - cc
            rows = pl.ds(pl.multiple_of(c * HC, HC), HC)
            for hh in range(NH):
                lbh = lb[:, HD * hh:HD * (hh + 1)]
                q = _hg_chunk_fwd(p_ref, rows, hh, lbh, tri)
                s0 = sc_ref[c, hh]
                s0b = s0.astype(_BF)
                dst_h = dst[hh]
                dstb = dst_h.astype(_BF)
                qdb, kdb = q["qd"].astype(_BF), q["kd"].astype(_BF)
                qeb, keb = q["qe"].astype(_BF), q["ke"].astype(_BF)
                vb = q["v"].astype(_BF)
                amat = jnp.where(causal, _dot_nt(qdb, kdb), 0.0)
                o = o_ref[rows, pl.ds(HD * hh, HD)]
                hg = p_ref[rows, pl.ds(2 * D_RG + 3 * D_HG + HD * hh, HD)]
                sh = _sigmoid(hg)
                n_o, r_o = _rms_fwd(o)
                dyh = dy_ref[rows, pl.ds(D_RG + HD * hh, HD)]
                dp_ref[rows, pl.ds(2 * D_RG + 3 * D_HG + HD * hh, HD)] = (
                    (dyh * (n_o * ghg)) * _dsilu(hg, sh)).astype(_BF)
                dn = dyh * (hg * sh)
                gvec_ref[R_GHG:R_GHG + 1, pl.ds(0, HD)] += jnp.sum(dn * n_o, axis=0, keepdims=True)
                dob = _rms_bwd(dn * ghg, n_o, r_o).astype(_BF)
                da_m = jnp.where(causal, _dot_nt(dob, vb), 0.0).astype(_BF)
                dqd = _dot(da_m, kdb)
                dkd = _dot_tn(da_m, qdb)
                dqe = _dot(dob, s0b)
                dke = _dot(vb, dstb)
                dv = _dot_tn(amat.astype(_BF), dob) + _dot_nt(keb, dstb)
                d_end = jnp.sum(s0 * dst_h, axis=0, keepdims=True)
                dst[hh] = _dot_tn(dob, qeb) + q["e_end"] * dst_h
                dq = dqd * q["e_q"] + dqe * q["e_b"]
                dk = dkd * q["e_k"] + dke * q["e_l"]
                dkeke = dke * q["ke"]
                db = dqd * qdb.astype(_F32) - dkd * kdb.astype(_F32) + dqe * q["qe"] - dkeke
                extra = jnp.sum(dkeke, axis=0, keepdims=True) + d_end * q["e_end"]
                db = db + jnp.where(last_row, extra, 0.0)
                dlf = _dot_hi(tri_u, db)
                df = dlf / q["f"] - dk
                sg = q["sg"]
                gvec_ref[R_HB0:R_HB0 + 1, pl.ds(HD * hh, HD)] += jnp.sum(df * (1.0 - sg), axis=0, keepdims=True)
                dp_ref[rows, pl.ds(2 * D_RG + HD * hh, HD)] = (dq * _dsilu(q["hq"], q["sq"])).astype(_BF)
                dp_ref[rows, pl.ds(2 * D_RG + D_HG + HD * hh, HD)] = (
                    (df * (1.0 - lbh)) * (sg * (1.0 - sg))).astype(_BF)
                dp_ref[rows, pl.ds(2 * D_RG + 2 * D_HG + HD * hh, HD)] = dv.astype(_BF)
            return carry

        lax.fori_loop(0, nc_t, chunk, 0)

        @pl.when(i == nt - 1)
        def _():
            glb = gvec_ref[R_HB0:R_HB0 + 1, :] * (lb * (1.0 - lb))
            gvec_ref[R_HB0:R_HB0 + 1, :] = glb
            gvec_ref[R_HB1:R_HB1 + 1, :] = -glb
            exchange.finish()

    hbm = pl.BlockSpec(memory_space=pl.ANY)
    return pl.pallas_call(
        body, name="mixer_bwd", grid=(nt,),
        in_specs=[pl.BlockSpec((TM, D_IN), lambda i: (rev(i), 0)),
                  pl.BlockSpec((8, D_RG), lambda i: (jnp.maximum(rev(i) * (TM // 8) - 1, 0), 0)),
                  pl.BlockSpec((TM, D_RG), lambda i: (rev(i), 0)),
                  pl.BlockSpec((8, D_RG), lambda i: (jnp.maximum(rev(i) * (TM // 8) - 1, 0), 0)),
                  pl.BlockSpec((TM, D_HG), lambda i: (rev(i), 0)),
                  pl.BlockSpec((nc_t, NH, HD, HD), lambda i: (rev(i), 0, 0, 0)),
                  pl.BlockSpec((TM, D), lambda i: (rev(i), 0)),
                  _full((D_RG, D_RG)), _full((D_RG, D_RG)), _full((16, D_RG)), _full((2, D_HG)), _full((1, HD))]
        + [hbm] * nsc,
        out_specs=[pl.BlockSpec((TM, D_IN), lambda i: (rev(i), 0)), _full((16, D_RG)), _full((2, D_RG, D_RG))]
        + [hbm] * nsc,
        out_shape=[_S((t_pad, D_IN), _BF), _S((16, D_RG), _F32), _S((2, D_RG, D_RG), _F32)]
        + [_S(s.shape, s.dtype) for s in scatter],
        scratch_shapes=[pltpu.VMEM((TM + 8, D_RG), _F32), pltpu.VMEM((TM + 8, D_RG), _F32),
                        pltpu.VMEM((TM + 8, D_RG), _F32), pltpu.VMEM((TM, D_RG), _F32),
                        pltpu.VMEM((TM, D_RG), _F32), pltpu.VMEM((8, D_RG), _F32),
                        pltpu.VMEM((NH, HD, HD), _F32)] + _sem_shapes(nsc),
        compiler_params=_cp(("arbitrary",)),
    )(p, p, hs, hs, o, sc, dy, wr, wi, vec, hb, g_hg, *scatter)


def _inproj_bwd(dp, w_in, h0, dh1, g_mix):
    t_pad = dp.shape[0]

    def body(dp_ref, w_ref, h_ref, dh1_ref, g_ref, dh0_ref, gmix_ref):
        i = pl.program_id(0)

        @pl.when(i == 0)
        def _():
            gmix_ref[...] = jnp.zeros_like(gmix_ref)

        du = jnp.zeros((TM, D), _F32)
        for j in range(N_DEV):
            du = du + _dot_nt(dp_ref[:, WIN_B * j:WIN_B * (j + 1)], w_ref[j])
        n, r = _rms_fwd(h_ref[...])
        gmix_ref[...] += jnp.sum(du * n, axis=0, keepdims=True)
        dh0_ref[...] = dh1_ref[...] + _rms_bwd(du * g_ref[...], n, r)

    tile = pl.BlockSpec((TM, D), lambda i: (i, 0))
    return pl.pallas_call(
        body, name="inproj_bwd", grid=(t_pad // TM,),
        in_specs=[pl.BlockSpec((TM, D_IN), lambda i: (i, 0)), _const((N_DEV, D, WIN_B)), tile, tile, _full((1, D))],
        out_specs=[tile, _full((1, D))],
        out_shape=[_S((t_pad, D), _F32), _S((1, D), _F32)],
        compiler_params=_cp(("arbitrary",)),
    )(dp, w_in, h0, dh1, g_mix)


def _wgrad(name, a, b, a_spec, b_spec, n_blocks, out_block):
    def body(a_ref, b_ref, o_ref):
        av = a_ref[0] if len(a_ref.shape) == 3 else a_ref[...]
        bv = b_ref[0] if len(b_ref.shape) == 3 else b_ref[...]
        o_ref[0] = _dot_tn(av, bv).astype(_BF)

    return pl.pallas_call(
        body, name=name, grid=(n_blocks,),
        in_specs=[a_spec, b_spec],
        out_specs=pl.BlockSpec((1,) + out_block, lambda j: (j, 0, 0)),
        out_shape=_S((n_blocks,) + out_block, _BF),
        compiler_params=_cp(("arbitrary",)),
    )(a, b)


def _coords():
    return lax.axis_index("x"), lax.axis_index("y"), lax.axis_index("c")


def _sem_shapes(na):
    return [pltpu.SemaphoreType.DMA((7 * na,)), pltpu.SemaphoreType.DMA((7 * na,)), pltpu.SemaphoreType.DMA((na,))]


class _Gather:
    def __init__(self, srcs, outs, sems):
        self.srcs, self.outs = srcs, outs
        self.send_sems, self.recv_sems, self.local_sems = sems
        self.na = len(srcs)
        x, y, c = _coords()
        self.pos = (x, y, c)
        self.me = 4 * x + 2 * y + c
        self.sibling = (x, y, 1 - c)
        self.chips = [(1 - x, y), (x, 1 - y), (1 - x, 1 - y)]

    @staticmethod
    def _slot(px, py, pc):
        return 4 * px + 2 * py + pc

    def _copy(self, a, k, block, to, own=False):
        return pltpu.make_async_remote_copy(
            src_ref=self.srcs[a] if own else self.outs[a].at[block], dst_ref=self.outs[a].at[block],
            send_sem=self.send_sems.at[7 * a + k], recv_sem=self.recv_sems.at[7 * a + k],
            device_id=to, device_id_type=_MESH)

    def _mine(self, a):
        return pltpu.make_async_copy(self.srcs[a], self.outs[a].at[self.me], self.local_sems.at[a])

    def _first(self):
        c = self.pos[2]
        cps = []
        for a in range(self.na):
            cps.append(self._copy(a, 0, self.me, self.sibling, own=True))
            cps += [self._copy(a, 1 + j, self.me, (*chip, c), own=True) for j, chip in enumerate(self.chips)]
        return cps

    def _passed(self):
        c = self.pos[2]
        return [self._copy(a, 4 + j, self._slot(*chip, c), self.sibling)
                for j, chip in enumerate(self.chips) for a in range(self.na)]

    def start(self):
        for a in range(self.na):
            self._mine(a).start()
        for cp in self._first():
            cp.start()

    def forward(self):
        c = self.pos[2]
        for j, chip in enumerate(self.chips):
            for a in range(self.na):
                self._copy(a, 1 + j, self._slot(*chip, c), self.pos).wait_recv()
                self._copy(a, 4 + j, self._slot(*chip, c), self.sibling).start()

    def finish(self):
        x, y, c = self.pos
        for a in range(self.na):
            self._copy(a, 0, self._slot(x, y, 1 - c), self.pos).wait_recv()
        for j, chip in enumerate(self.chips):
            for a in range(self.na):
                self._copy(a, 4 + j, self._slot(*chip, 1 - c), self.pos).wait_recv()
        for cp in self._first() + self._passed():
            cp.wait_send()
        for a in range(self.na):
            self._mine(a).wait()


class _Exchange:
    def __init__(self, scatter, gather, outs, sems):
        self.ins = list(scatter) + list(gather)
        self.ns, self.na = len(scatter), len(scatter) + len(gather)
        self.outs = outs
        self.send_sems, self.recv_sems, self.local_sems = sems
        x, y, c = _coords()
        self.pos = (x, y, c)
        self.me = 4 * x + 2 * y + c

    def _peer(self, r):
        x, y, c = self.pos
        return x ^ (r >> 2), y ^ ((r >> 1) & 1), c ^ (r & 1)

    def _src(self, a, block):
        return self.ins[a].at[block] if a < self.ns else self.ins[a]

    def _local(self, a):
        return pltpu.make_async_copy(self._src(a, self.me), self.outs[a].at[self.me], self.local_sems.at[a])

    def _send(self, a, r):
        px, py, pc = self._peer(r)
        return pltpu.make_async_remote_copy(
            src_ref=self._src(a, 4 * px + 2 * py + pc), dst_ref=self.outs[a].at[self.me],
            send_sem=self.send_sems.at[7 * a + r - 1], recv_sem=self.recv_sems.at[7 * a + r - 1],
            device_id=(px, py, pc), device_id_type=_MESH)

    def _recv(self, a, r):
        px, py, pc = self._peer(r)
        return pltpu.make_async_remote_copy(
            src_ref=self._src(a, self.me), dst_ref=self.outs[a].at[4 * px + 2 * py + pc],
            send_sem=self.send_sems.at[7 * a + r - 1], recv_sem=self.recv_sems.at[7 * a + r - 1],
            device_id=(px, py, pc), device_id_type=_MESH)

    def start(self):
        for a in range(self.na):
            self._local(a).start()
        for r in range(1, N_DEV):
            for a in range(self.na):
                self._send(a, r).start()

    def finish(self):
        for r in range(1, N_DEV):
            for a in range(self.na):
                self._recv(a, r).wait_recv()
        for r in range(1, N_DEV):
            for a in range(self.na):
                self._send(a, r).wait_send()
        for a in range(self.na):
            self._local(a).wait()


def _allgather_first(gather_f32, cast_f32, gather_dtypes):
    ng, nc = len(gather_f32), len(cast_f32)

    def body(*refs):
        ins, cins = refs[:ng], refs[ng:ng + nc]
        outs, couts = refs[ng + nc:2 * ng + nc], refs[2 * ng + nc:2 * ng + 2 * nc]
        stage = refs[2 * ng + 2 * nc:3 * ng + 2 * nc]
        sems = refs[3 * ng + 2 * nc:]
        for a in range(ng):
            stage[a][...] = ins[a][...].astype(gather_dtypes[a])
        g = _Gather(stage, outs, sems)
        g.start()
        for a in range(nc):
            couts[a][...] = cins[a][...].astype(_BF)
        g.forward()
        g.finish()

    vm = pl.BlockSpec(memory_space=pltpu.VMEM)
    return pl.pallas_call(
        body, name="allgather_first",
        in_specs=[vm] * (ng + nc),
        out_specs=[pl.BlockSpec(memory_space=pl.ANY)] * ng + [vm] * nc,
        out_shape=[_S((N_DEV,) + l.shape, dt) for l, dt in zip(gather_f32, gather_dtypes)]
        + [_S(l.shape, _BF) for l in cast_f32],
        scratch_shapes=[pltpu.VMEM(l.shape, dt) for l, dt in zip(gather_f32, gather_dtypes)] + _sem_shapes(ng),
        compiler_params=pltpu.CompilerParams(vmem_limit_bytes=VMEM_LIMIT),
    )(*gather_f32, *cast_f32)


def _exchange(scatter, gather):
    na = len(scatter) + len(gather)

    def body(*refs):
        ex = _Exchange(refs[:len(scatter)], refs[len(scatter):na], refs[na:2 * na], refs[2 * na:])
        ex.start()
        ex.finish()

    return pl.pallas_call(
        body, name="exchange_last",
        in_specs=[pl.BlockSpec(memory_space=pl.ANY)] * na,
        out_specs=[pl.BlockSpec(memory_space=pl.ANY)] * na,
        out_shape=[_S(s.shape, s.dtype) for s in scatter] + [_S((N_DEV,) + g.shape, g.dtype) for g in gather],
        scratch_shapes=_sem_shapes(na),
    )(*scatter, *gather)


def _adamw_math(w, g, m, v):
    m2 = ADAM_B1 * m + (1.0 - ADAM_B1) * g
    v2 = ADAM_B2 * v + (1.0 - ADAM_B2) * (g * g)
    m_hat = m2 / (1.0 - ADAM_B1 ** ADAM_STEP)
    v_hat = v2 / (1.0 - ADAM_B2 ** ADAM_STEP)
    delta = -ADAM_LR * (m_hat / (jnp.sqrt(v_hat) + ADAM_EPS) + ADAM_WD * w)
    return delta, m2, v2


def _adamw_big(name, recv, w, m, v, rows):
    r_all, c_all = w.shape

    def body(r_ref, w_ref, m_ref, v_ref, g_out, d_out, m_out, v_out):
        g = r_ref[0].astype(_F32)
        for k in range(1, N_DEV):
            g = g + r_ref[k].astype(_F32)
        delta, m2, v2 = _adamw_math(w_ref[...], g, m_ref[...], v_ref[...])
        g_out[...] = g
        d_out[...] = delta
        m_out[...] = m2
        v_out[...] = v2

    tile = pl.BlockSpec((rows, c_all), lambda i: (i, 0))
    return pl.pallas_call(
        body, name=name, grid=(r_all // rows,),
        in_specs=[pl.BlockSpec((N_DEV, rows, c_all), lambda i: (0, i, 0)), tile, tile, tile],
        out_specs=[tile] * 4,
        out_shape=[_S(w.shape, _F32)] * 4,
        compiler_params=_cp(("arbitrary",)),
    )(recv, w, m, v)


def _adamw_small(gathered, slices, wmv):
    ng, npar = len(gathered), len(slices)

    def body(*refs):
        g_refs = refs[:ng]
        wmv_refs = refs[ng:ng + 3 * npar]
        outs = refs[ng + 3 * npar:]
        for i, (ai, r0, nr, ncol) in enumerate(slices):
            g = g_refs[ai][0, pl.ds(r0, nr), pl.ds(0, ncol)]
            for k in range(1, N_DEV):
                g = g + g_refs[ai][k, pl.ds(r0, nr), pl.ds(0, ncol)]
            w_ref, m_ref, v_ref = wmv_refs[3 * i:3 * i + 3]
            delta, m2, v2 = _adamw_math(w_ref[...], g, m_ref[...], v_ref[...])
            outs[4 * i][...] = g
            outs[4 * i + 1][...] = delta
            outs[4 * i + 2][...] = m2
            outs[4 * i + 3][...] = v2

    flat = [t for trip in wmv for t in trip]
    out_shape = []
    for w, _, _ in wmv:
        out_shape += [_S(w.shape, _F32)] * 4
    return pl.pallas_call(
        body, name="adamw_small", out_shape=out_shape,
        compiler_params=pltpu.CompilerParams(vmem_limit_bytes=VMEM_LIMIT),
    )(*gathered, *flat)


def _block_diag(w):
    eye = jnp.eye(8, dtype=w.dtype)
    return (w[:, :, None, :] * eye[:, None, :, None]).reshape(D_RG, D_RG)


def _diag_blocks(g):
    return jnp.concatenate([g[64 * h:64 * (h + 1), 64 * h:64 * (h + 1)] for h in range(8)], axis=0)


def _local_step(x, tgt, meta, g_mix, w_in, vec, wr, wi, hb, g_hg, w_out_l, g_ffn, w_gu_l, w_down_l, g_fin):
    seq = x.shape[0]
    n_valid = N_META + seq
    t_pad = -(-n_valid // TM) * TM
    h0 = jnp.concatenate([meta, x, jnp.zeros((t_pad - n_valid, D), _F32)], axis=0)
    tgt_p = jnp.concatenate([jnp.zeros((N_META, D), _F32), tgt, jnp.zeros((t_pad - n_valid, D), _F32)], axis=0)

    p, u = _inproj(h0, g_mix, w_in)
    y, hs, o, sc, w_gu, w_out, w_down = _mixer_fwd(p, wr, wi, vec, hb, g_hg, [w_gu_l, w_out_l, w_down_l])
    w_out = w_out.reshape(D, D)
    w_down = w_down.reshape(4, FFB, D)
    h1, v = _outproj(h0, y, w_out, g_ffn)
    gu, act, dh2, dh2b, loss, gfin = _ffn_loss(v, h1, w_gu, w_down, g_fin, tgt_p, n_valid)

    dgu, dh1, dh1b, dy, gffn = _ffn_bwd(dh2, dh2b, gu, h1, g_ffn, w_gu, w_down, w_out)
    g_wdown = _wgrad("wgrad_down", act, dh2b, pl.BlockSpec((1, t_pad, FFB), lambda j: (j, 0, 0)),
                     pl.BlockSpec((t_pad, D), lambda j: (0, 0)), 4, (FFB, D))
    g_wgu = _wgrad("wgrad_gate_up", v, dgu, pl.BlockSpec((t_pad, D), lambda j: (0, 0)),
                   pl.BlockSpec((1, t_pad, FFB), lambda j: (j, 0, 0)), N_DEV, (D, FFB))
    g_wout = _wgrad("wgrad_out", y, dh1b, pl.BlockSpec((t_pad, D // N_DEV), lambda j: (0, j)),
                    pl.BlockSpec((t_pad, D), lambda j: (0, 0)), N_DEV, (D // N_DEV, D))
    dp, gvec, gw, r_wgu, r_wout, r_wdown = _mixer_bwd(
        p, hs, o, sc, dy, wr, wi, vec, hb, g_hg, [g_wgu, g_wout, g_wdown.reshape(N_DEV, D_FF // N_DEV, D)])
    dh0, gmix = _inproj_bwd(dp, w_in, h0, dh1, g_mix)
    g_win = _wgrad("wgrad_in", u, dp, pl.BlockSpec((t_pad, D), lambda j: (0, 0)),
                   pl.BlockSpec((t_pad, WIN_B), lambda j: (0, j)), N_DEV, (D, WIN_B))
    return loss, dh0, g_win, (r_wgu, r_wout, r_wdown), (gmix, gffn, gfin, gvec, gw)


def kernel(x, meta_tokens, mix_norm_g, w_in, conv_w, conv_b, w_rgate, b_rgate, w_igate, b_igate, lru_lambda, rg_norm_g, hg_lower_bound, hg_norm_g, w_out, ffn_norm_g, w_gate_up, w_down, final_norm_g, loss_target, m_meta_tokens, m_mix_norm_g, m_w_in, m_conv_w, m_conv_b, m_w_rgate, m_b_rgate, m_w_igate, m_b_igate, m_lru_lambda, m_rg_norm_g, m_hg_lower_bound, m_hg_norm_g, m_w_out, m_ffn_norm_g, m_w_gate_up, m_w_down, m_final_norm_g, v_meta_tokens, v_mix_norm_g, v_w_in, v_conv_w, v_conv_b, v_w_rgate, v_b_rgate, v_w_igate, v_b_igate, v_lru_lambda, v_rg_norm_g, v_hg_lower_bound, v_hg_norm_g, v_w_out, v_ffn_norm_g, v_w_gate_up, v_w_down, v_final_norm_g):
    seq = x.shape[1]
    me = 4 * lax.axis_index("x") + 2 * lax.axis_index("y") + lax.axis_index("c")

    small_l = jnp.concatenate([meta_tokens, jnp.pad(conv_w[0], ((0, 4), (0, 64)))], axis=0)
    w_in_g, small_g, w_gu_l, w_out_l, w_down_l = _allgather_first(
        [w_in[0], small_l], [w_gate_up[0], w_out[0], w_down[0]], [_BF, _F32])
    meta_full = jnp.transpose(small_g[:, :N_META, :], (1, 0, 2)).reshape(N_META, D)
    conv_w_full = jnp.transpose(small_g[:, N_META:N_META + 4, :64], (1, 0, 2)).reshape(4, D_RG)
    vec = jnp.concatenate([conv_b, b_rgate, b_igate, lru_lambda, rg_norm_g, jnp.zeros((3, D_RG), _F32),
                           conv_w_full, jnp.zeros((4, D_RG), _F32)], axis=0)
    wr = _block_diag(w_rgate[0]).astype(_BF)
    wi = _block_diag(w_igate[0]).astype(_BF)

    loss, dh0, g_win, (r_wgu, r_wout, r_wdown), small = _local_step(
        x[0], loss_target[0], meta_full, mix_norm_g, w_in_g, vec, wr, wi, hg_lower_bound, hg_norm_g,
        w_out_l, ffn_norm_g, w_gu_l, w_down_l, final_norm_g.reshape(1, D))
    gmix, gffn, gfin, gvec, gw = small
    grad_x = dh0[N_META:N_META + seq][None]
    pack_a = jnp.concatenate([gmix, gffn, gfin, jnp.zeros((5, D), _F32), dh0[:N_META]], axis=0)
    pack_c = jnp.concatenate([_diag_blocks(gw[0]), _diag_blocks(gw[1])], axis=0)

    r_win, all_a, all_b, all_c = _exchange([g_win], [pack_a, gvec, pack_c])

    outs = {}
    outs["w_in"] = _adamw_big("adamw_w_in", r_win, w_in[0], m_w_in[0], v_w_in[0], 256)
    outs["w_gate_up"] = _adamw_big("adamw_w_gate_up", r_wgu, w_gate_up[0], m_w_gate_up[0], v_w_gate_up[0], 256)
    outs["w_out"] = _adamw_big("adamw_w_out", r_wout, w_out[0], m_w_out[0], v_w_out[0], 128)
    outs["w_down"] = _adamw_big("adamw_w_down", r_wdown, w_down[0], m_w_down[0], v_w_down[0], 176)

    meta_part = lax.dynamic_slice_in_dim(all_a[:, R_META:R_META + N_META, :], me * 128, 128, axis=2)
    convw_part = lax.dynamic_slice_in_dim(all_b[:, R_CONVW:R_CONVW + 4, :], me * 64, 64, axis=2)
    gathered = [all_a, all_b, all_c, meta_part, convw_part]
    small_params = [
        ("meta_tokens", (3, 0, N_META, 128), (meta_tokens, m_meta_tokens, v_meta_tokens), (N_META, 128)),
        ("mix_norm_g", (0, R_GMIX, 1, D), (mix_norm_g, m_mix_norm_g, v_mix_norm_g), (1, D)),
        ("conv_w", (4, 0, 4, 64), (conv_w, m_conv_w, v_conv_w), (4, 64)),
        ("conv_b", (1, R_CONVB, 1, D_RG), (conv_b, m_conv_b, v_conv_b), (1, D_RG)),
        ("w_rgate", (2, 0, 512, 64), (w_rgate, m_w_rgate, v_w_rgate), (512, 64)),
        ("b_rgate", (1, R_BR, 1, D_RG), (b_rgate, m_b_rgate, v_b_rgate), (1, D_RG)),
        ("w_igate", (2, 512, 512, 64), (w_igate, m_w_igate, v_w_igate), (512, 64)),
        ("b_igate", (1, R_BI, 1, D_RG), (b_igate, m_b_igate, v_b_igate), (1, D_RG)),
        ("lru_lambda", (1, R_LAM, 1, D_RG), (lru_lambda, m_lru_lambda, v_lru_lambda), (1, D_RG)),
        ("rg_norm_g", (1, R_GRG, 1, D_RG), (rg_norm_g, m_rg_norm_g, v_rg_norm_g), (1, D_RG)),
        ("hg_lower_bound", (1, R_HB0, 2, D_HG), (hg_lower_bound, m_hg_lower_bound, v_hg_lower_bound), (2, D_HG)),
        ("hg_norm_g", (1, R_GHG, 1, HD), (hg_norm_g, m_hg_norm_g, v_hg_norm_g), (1, HD)),
        ("ffn_norm_g", (0, R_GFFN, 1, D), (ffn_norm_g, m_ffn_norm_g, v_ffn_norm_g), (1, D)),
        ("final_norm_g", (0, R_GFIN, 1, D), (final_norm_g, m_final_norm_g, v_final_norm_g), (1, D)),
    ]
    res = _adamw_small(gathered, [s[1] for s in small_params],
                       [tuple(t.reshape(s[3]) for t in s[2]) for s in small_params])
    for i, s in enumerate(small_params):
        outs[s[0]] = [r.reshape(s[2][0].shape) for r in res[4 * i:4 * i + 4]]
    for n, ref in (("w_in", w_in), ("w_gate_up", w_gate_up), ("w_out", w_out), ("w_down", w_down)):
        outs[n] = [r.reshape(ref.shape) for r in outs[n]]

    loss_all = lax.psum(loss[0, 0], ("x", "y", "c"))
    order = ["meta_tokens", "mix_norm_g", "w_in", "conv_w", "conv_b", "w_rgate", "b_rgate", "w_igate", "b_igate",
             "lru_lambda", "rg_norm_g", "hg_lower_bound", "hg_norm_g", "w_out", "ffn_norm_g", "w_gate_up", "w_down",
             "final_norm_g"]
    return (loss_all, grad_x, *[outs[n][0] for n in order], *[outs[n][1] for n in order],
            *[outs[n][2] for n in order], *[outs[n][3] for n in order])
```

```python
import functools

import jax
import jax.numpy as jnp
from jax import lax
from jax.experimental import pallas as pl
from jax.experimental.pallas import tpu as pltpu

_BF = jnp.bfloat16
_F32 = jnp.float32
_S = jax.ShapeDtypeStruct
_MESH = pl.DeviceIdType.MESH

N_DEV = 8
N_META = 16
D = 1024
D_RG = 512
D_HG = 512
HD = 128
NH = D_HG // HD
D_IN = 3072
D_FF = 2816
FFB = D_FF // 4
WIN_B = D_IN // N_DEV
EPS = 1e-6
LRU_C = 8.0
TM = 256
HC = 64
VMEM_LIMIT = 56 * 1024 * 1024

ADAM_LR = 0.001
ADAM_B1 = 0.9
ADAM_B2 = 0.999
ADAM_EPS = 1e-08
ADAM_WD = 0.01
ADAM_STEP = 10

_SEND_ORDER = (6, 7, 4, 5, 2, 3, 1, 0)

R_CONVB, R_BR, R_BI, R_LAM, R_GRG, R_HB0, R_HB1, R_GHG, R_CONVW = 0, 1, 2, 3, 4, 5, 6, 7, 8
R_GMIX, R_GFFN, R_GFIN, R_META = 0, 1, 2, 8


def _cp(sem=None, **kw):
    return pltpu.CompilerParams(dimension_semantics=sem, vmem_limit_bytes=VMEM_LIMIT, **kw)


def _dot(a, b):
    return jnp.dot(a, b, preferred_element_type=_F32)


def _dot_nt(a, b):
    return lax.dot_general(a, b, (((1,), (1,)), ((), ())), preferred_element_type=_F32)


def _dot_tn(a, b):
    return lax.dot_general(a, b, (((0,), (0,)), ((), ())), preferred_element_type=_F32)


def _dot_hi(a, b):
    return jnp.dot(a, b, preferred_element_type=_F32, precision=lax.Precision.HIGHEST)


def _sigmoid(x):
    return jax.nn.sigmoid(x)


def _dsilu(x, s):
    return s * (1.0 + x * (1.0 - s))


_GELU_C = 0.7978845608028654


def _gelu_parts(x):
    t = jnp.tanh(_GELU_C * (x + 0.044715 * (x * x * x)))
    g = 0.5 * x * (1.0 + t)
    dg = 0.5 * (1.0 + t) + 0.5 * x * (1.0 - t * t) * (_GELU_C * (1.0 + 3.0 * 0.044715 * (x * x)))
    return g, dg


def _softplus(z):
    e = jnp.exp(-jnp.abs(z))
    w = 1.0 + e
    l1p = jnp.where(w == 1.0, e, jnp.log(w) * e / jnp.where(w == 1.0, 1.0, w - 1.0))
    return jnp.maximum(z, 0.0) + l1p


def _rms_fwd(x):
    r = lax.rsqrt(jnp.mean(x * x, axis=-1, keepdims=True) + EPS)
    return x * r, r


def _rms_bwd(dyg, n, r):
    return r * (dyg - n * jnp.mean(dyg * n, axis=-1, keepdims=True))


def _full(shape):
    nd = len(shape)
    return pl.BlockSpec(shape, lambda i: (0,) * nd)


def _const(shape):
    nd = len(shape)
    return pl.BlockSpec(shape, lambda i: (0,) * nd, pipeline_mode=pl.Buffered(1))


def _inproj(h0, g_mix, w_in):
    t_pad = h0.shape[0]

    def body(h_ref, g_ref, w_ref, p_ref, u_ref):
        n, _ = _rms_fwd(h_ref[...])
        u = (n * g_ref[...]).astype(_BF)
        u_ref[...] = u
        for j in range(N_DEV):
            p_ref[:, WIN_B * j:WIN_B * (j + 1)] = _dot(u, w_ref[j])

    return pl.pallas_call(
        body, name="inproj", grid=(t_pad // TM,),
        in_specs=[pl.BlockSpec((TM, D), lambda i: (i, 0)), _full((1, D)), _const((N_DEV, D, WIN_B))],
        out_specs=[pl.BlockSpec((TM, D_IN), lambda i: (i, 0)), pl.BlockSpec((TM, D), lambda i: (i, 0))],
        out_shape=[_S((t_pad, D_IN), _F32), _S((t_pad, D), _BF)],
        compiler_params=_cp(("arbitrary",)),
    )(h0, g_mix, w_in)


def _rg_gates(xc, wr_ref, wi_ref, vec_ref):
    xcb = xc.astype(_BF)
    r = _sigmoid(_dot(xcb, wr_ref[...]) + vec_ref[R_BR:R_BR + 1, :])
    ig = _sigmoid(_dot(xcb, wi_ref[...]) + vec_ref[R_BI:R_BI + 1, :])
    nsp8 = -LRU_C * _softplus(-vec_ref[R_LAM:R_LAM + 1, :])
    la = nsp8 * r
    a = jnp.exp(la)
    th = jnp.tanh(la)
    s = jnp.sqrt(-2.0 * th / (1.0 - th))
    return r, ig, a, s, nsp8


def _conv(xbuf, vec_ref):
    acc = vec_ref[R_CONVW:R_CONVW + 1, :] * xbuf[pl.ds(5, TM), :]
    for j in range(1, 4):
        acc = acc + vec_ref[R_CONVW + j:R_CONVW + j + 1, :] * xbuf[pl.ds(5 + j, TM), :]
    return vec_ref[R_CONVB:R_CONVB + 1, :] + acc


def _hg_chunk_fwd(p_ref, rows, h, lbh, tri):
    hq = p_ref[rows, pl.ds(2 * D_RG + HD * h, HD)]
    hf = p_ref[rows, pl.ds(2 * D_RG + D_HG + HD * h, HD)]
    v = p_ref[rows, pl.ds(2 * D_RG + 2 * D_HG + HD * h, HD)]
    sq = _sigmoid(hq)
    q = hq * sq
    sg = _sigmoid(hf)
    f = lbh + (1.0 - lbh) * sg
    k = 1.0 - f
    b = _dot_hi(tri, jnp.log(f))
    bm = b[HC // 2 - 1:HC // 2, :]
    bl = b[HC - 1:HC, :]
    e_q = jnp.exp(b - bm)
    e_k = jnp.exp(bm - b)
    e_b = jnp.exp(b)
    e_l = jnp.exp(bl - b)
    return dict(hq=hq, sq=sq, q=q, sg=sg, f=f, k=k, v=v, e_q=e_q, e_k=e_k, e_b=e_b, e_l=e_l,
                qd=q * e_q, kd=k * e_k, qe=q * e_b, ke=k * e_l, e_end=jnp.exp(bl))


def _mixer_fwd(p, wr, wi, vec, hb, g_hg, shards):
    t_pad = p.shape[0]
    nt = t_pad // TM
    nc_t = TM // HC
    nsh = len(shards)

    def body(p_ref, wr_ref, wi_ref, vec_ref, hb_ref, ghg_ref, *rest):
        sh_refs, rest = rest[:nsh], rest[nsh:]
        y_ref, hs_ref, o_ref, sc_ref = rest[:4]
        gath_refs, rest = rest[4:4 + nsh], rest[4 + nsh:]
        xbuf, a_s, b_s, hcar, st = rest[:5]
        gather = _Gather(sh_refs, gath_refs, rest[5:])
        i = pl.program_id(0)

        @pl.when(i == 0)
        def _():
            gather.start()
            xbuf[pl.ds(0, 8), :] = jnp.zeros((8, D_RG), _F32)
            hcar[...] = jnp.zeros_like(hcar)
            st[...] = jnp.zeros_like(st)

        x = p_ref[:, pl.ds(0, D_RG)]
        xbuf[pl.ds(8, TM), :] = x
        xc = _conv(xbuf, vec_ref)
        xbuf[pl.ds(0, 8), :] = x[TM - 8:, :]
        r, ig, a, s, _ = _rg_gates(xc, wr_ref, wi_ref, vec_ref)
        a_s[...] = a
        b_s[...] = s * (ig * xc)

        def step(t, h):
            h = a_s[pl.ds(t, 1), :] * h + b_s[pl.ds(t, 1), :]
            hs_ref[pl.ds(t, 1), :] = h
            return h

        hcar[pl.ds(0, 1), :] = lax.fori_loop(0, TM, step, hcar[pl.ds(0, 1), :], unroll=8)
        gel, _ = _gelu_parts(p_ref[:, pl.ds(D_RG, D_RG)])
        n, _ = _rms_fwd(gel * hs_ref[...])
        y_ref[:, pl.ds(0, D_RG)] = (n * vec_ref[R_GRG:R_GRG + 1, :]).astype(_BF)

        lb = _sigmoid(hb_ref[0:1, :] - hb_ref[1:2, :])
        ti = lax.broadcasted_iota(jnp.int32, (HC, HC), 0)
        si = lax.broadcasted_iota(jnp.int32, (HC, HC), 1)
        causal = ti >= si
        tri = causal.astype(_F32)

        def chunk(c, carry):
            rows = pl.ds(pl.multiple_of(c * HC, HC), HC)
            for h in range(NH):
                q = _hg_chunk_fwd(p_ref, rows, h, lb[:, HD * h:HD * (h + 1)], tri)
                s0 = st[h]
                sc_ref[c, h] = s0
                amat = jnp.where(causal, _dot_nt(q["qd"].astype(_BF), q["kd"].astype(_BF)), 0.0)
                vb = q["v"].astype(_BF)
                o = _dot_nt(q["qe"].astype(_BF), s0.astype(_BF)) + _dot(amat.astype(_BF), vb)
                st[h] = q["e_end"] * s0 + _dot_tn(vb, q["ke"].astype(_BF))
                o_ref[rows, pl.ds(HD * h, HD)] = o
                n_o, _ = _rms_fwd(o)
                hg = p_ref[rows, pl.ds(2 * D_RG + 3 * D_HG + HD * h, HD)]
                yh = (n_o * ghg_ref[...]) * (hg * _sigmoid(hg))
                y_ref[rows, pl.ds(D_RG + HD * h, HD)] = yh.astype(_BF)
            return carry

        lax.fori_loop(0, nc_t, chunk, 0)

        for j in range(3):
            @pl.when(i == min(nt // 2 + 2 * j, nt - 1))
            def _(j=j):
                gather.forward(j)

        @pl.when(i == nt - 1)
        def _():
            gather.finish()

    hbm = pl.BlockSpec(memory_space=pl.ANY)
    return pl.pallas_call(
        body, name="mixer_fwd", grid=(nt,),
        in_specs=[pl.BlockSpec((TM, D_IN), lambda i: (i, 0)), _full((D_RG, D_RG)), _full((D_RG, D_RG)),
                  _full((16, D_RG)), _full((2, D_HG)), _full((1, HD))] + [hbm] * nsh,
        out_specs=[pl.BlockSpec((TM, D), lambda i: (i, 0)), pl.BlockSpec((TM, D_RG), lambda i: (i, 0)),
                   pl.BlockSpec((TM, D_HG), lambda i: (i, 0)),
                   pl.BlockSpec((nc_t, NH, HD, HD), lambda i: (i, 0, 0, 0))] + [hbm] * nsh,
        out_shape=[_S((t_pad, D), _BF), _S((t_pad, D_RG), _F32), _S((t_pad, D_HG), _F32),
                   _S((t_pad // HC, NH, HD, HD), _F32)] + [_S((N_DEV,) + s.shape, s.dtype) for s in shards],
        scratch_shapes=[pltpu.VMEM((TM + 8, D_RG), _F32), pltpu.VMEM((TM, D_RG), _F32),
                        pltpu.VMEM((TM, D_RG), _F32), pltpu.VMEM((8, D_RG), _F32),
                        pltpu.VMEM((NH, HD, HD), _F32)] + _sem_shapes(nsh),
        compiler_params=_cp(("arbitrary",)),
    )(p, wr, wi, vec, hb, g_hg, *shards)


def _outproj(h0, y, w_out, g_ffn):
    t_pad = h0.shape[0]

    def body(h_ref, y_ref, w_ref, g_ref, h1_ref, v_ref):
        h1 = h_ref[...] + _dot(y_ref[...], w_ref[...])
        h1_ref[...] = h1
        n, _ = _rms_fwd(h1)
        v_ref[...] = (n * g_ref[...]).astype(_BF)

    return pl.pallas_call(
        body, name="outproj", grid=(t_pad // TM,),
        in_specs=[pl.BlockSpec((TM, D), lambda i: (i, 0)), pl.BlockSpec((TM, D), lambda i: (i, 0)),
                  _full((D, D)), _full((1, D))],
        out_specs=[pl.BlockSpec((TM, D), lambda i: (i, 0)), pl.BlockSpec((TM, D), lambda i: (i, 0))],
        out_shape=[_S((t_pad, D), _F32), _S((t_pad, D), _BF)],
        compiler_params=_cp(("arbitrary",)),
    )(h0, y, w_out, g_ffn)


def _ffn_loss(v, h1, w_gu, w_down, g_fin, tgt, n_valid):
    t_pad = v.shape[0]

    def body(v_ref, h1_ref, wgu_ref, wd_ref, g_ref, t_ref, gu_ref, act_ref, dh2_ref, dh2b_ref, loss_ref, gfin_ref):
        i = pl.program_id(0)

        @pl.when(i == 0)
        def _():
            loss_ref[...] = jnp.zeros_like(loss_ref)
            gfin_ref[...] = jnp.zeros_like(gfin_ref)

        vb = v_ref[...]
        h2 = h1_ref[...]
        for b in range(4):
            gate = _dot(vb, wgu_ref[b])
            up = _dot(vb, wgu_ref[4 + b])
            gu_ref[b] = gate
            gu_ref[4 + b] = up
            act = ((gate * _sigmoid(gate)) * up).astype(_BF)
            act_ref[b] = act
            h2 = h2 + _dot(act, wd_ref[b])
        n, r = _rms_fwd(h2)
        out = n * g_ref[...]
        row = i * TM + lax.broadcasted_iota(jnp.int32, (TM, 1), 0)
        valid = (row >= N_META) & (row < n_valid)
        err = jnp.where(valid, out - t_ref[...], 0.0)
        loss_ref[...] += (0.5 / D) * jnp.sum(err * err)
        dout = err * (1.0 / D)
        gfin_ref[...] += jnp.sum(dout * n, axis=0, keepdims=True)
        dh2 = _rms_bwd(dout * g_ref[...], n, r)
        dh2_ref[...] = dh2
        dh2b_ref[...] = dh2.astype(_BF)

    return pl.pallas_call(
        body, name="ffn_loss", grid=(t_pad // TM,),
        in_specs=[pl.BlockSpec((TM, D), lambda i: (i, 0)), pl.BlockSpec((TM, D), lambda i: (i, 0)),
                  _const((N_DEV, D, FFB)), _const((4, FFB, D)), _full((1, D)),
                  pl.BlockSpec((TM, D), lambda i: (i, 0))],
        out_specs=[pl.BlockSpec((N_DEV, TM, FFB), lambda i: (0, i, 0)), pl.BlockSpec((4, TM, FFB), lambda i: (0, i, 0)),
                   pl.BlockSpec((TM, D), lambda i: (i, 0)), pl.BlockSpec((TM, D), lambda i: (i, 0)),
                   _full((8, 128)), _full((1, D))],
        out_shape=[_S((N_DEV, t_pad, FFB), _F32), _S((4, t_pad, FFB), _BF), _S((t_pad, D), _F32),
                   _S((t_pad, D), _BF), _S((8, 128), _F32), _S((1, D), _F32)],
        compiler_params=_cp(("arbitrary",)),
    )(v, h1, w_gu, w_down, g_fin, tgt)


def _ffn_bwd(dh2, dh2b, gu, h1, g_ffn, w_gu, w_down, w_out):
    t_pad = dh2.shape[0]

    def body(dh2_ref, dh2b_ref, gu_ref, h1_ref, g_ref, wgu_ref, wd_ref, wo_ref,
             dgu_ref, dh1_ref, dh1b_ref, dy_ref, gffn_ref):
        i = pl.program_id(0)

        @pl.when(i == 0)
        def _():
            gffn_ref[...] = jnp.zeros_like(gffn_ref)

        db = dh2b_ref[...]
        dv = jnp.zeros((TM, D), _F32)
        for b in range(4):
            dact = _dot_nt(db, wd_ref[b])
            gate = gu_ref[b]
            up = gu_ref[4 + b]
            sg = _sigmoid(gate)
            dgate = ((dact * up) * _dsilu(gate, sg)).astype(_BF)
            dup = (dact * (gate * sg)).astype(_BF)
            dgu_ref[b] = dgate
            dgu_ref[4 + b] = dup
            dv = dv + _dot_nt(dgate, wgu_ref[b]) + _dot_nt(dup, wgu_ref[4 + b])
        n, r = _rms_fwd(h1_ref[...])
        gffn_ref[...] += jnp.sum(dv * n, axis=0, keepdims=True)
        dh1 = dh2_ref[...] + _rms_bwd(dv * g_ref[...], n, r)
        dh1_ref[...] = dh1
        dh1b = dh1.astype(_BF)
        dh1b_ref[...] = dh1b
        dy_ref[...] = _dot_nt(dh1b, wo_ref[...])

    tile = pl.BlockSpec((TM, D), lambda i: (i, 0))
    return pl.pallas_call(
        body, name="ffn_bwd", grid=(t_pad // TM,),
        in_specs=[tile, tile, pl.BlockSpec((N_DEV, TM, FFB), lambda i: (0, i, 0)), tile, _full((1, D)),
                  _const((N_DEV, D, FFB)), _const((4, FFB, D)), _const((D, D))],
        out_specs=[pl.BlockSpec((N_DEV, TM, FFB), lambda i: (0, i, 0)), tile, tile, tile, _full((1, D))],
        out_shape=[_S((N_DEV, t_pad, FFB), _BF), _S((t_pad, D), _F32), _S((t_pad, D), _BF),
                   _S((t_pad, D), _F32), _S((1, D), _F32)],
        compiler_params=_cp(("arbitrary",)),
    )(dh2, dh2b, gu, h1, g_ffn, w_gu, w_down, w_out)


def _mixer_bwd(p, hs, o, sc, dy, wr, wi, vec, hb, g_hg, scatter):
    t_pad = p.shape[0]
    nt = t_pad // TM
    nc_t = TM // HC
    nsc = len(scatter)

    def rev(i):
        return nt - 1 - i

    def body(p_ref, pprev_ref, hs_ref, hprev_ref, o_ref, sc_ref, dy_ref, wr_ref, wi_ref, vec_ref, hb_ref, ghg_ref,
             *rest):
        send_refs, rest = rest[:nsc], rest[nsc:]
        dp_ref, gvec_ref, gw_ref = rest[:3]
        recv_refs, rest = rest[3:3 + nsc], rest[3 + nsc:]
        xbuf, hbuf, dbuf, a_s, g_s, ccar, dst = rest[:7]
        exchange = _Exchange(send_refs, [], recv_refs, rest[7:])
        i = pl.program_id(0)
        first_tile = i == nt - 1

        @pl.when(i == 0)
        def _():
            exchange.start()
            gvec_ref[...] = jnp.zeros_like(gvec_ref)
            gw_ref[...] = jnp.zeros_like(gw_ref)
            dbuf[pl.ds(TM, 8), :] = jnp.zeros((8, D_RG), _F32)
            ccar[...] = jnp.zeros_like(ccar)
            dst[...] = jnp.zeros_like(dst)

        def acc(row, val):
            gvec_ref[row:row + 1, :] += jnp.sum(val, axis=0, keepdims=True)

        keep = jnp.where(first_tile, 0.0, 1.0)
        x = p_ref[:, pl.ds(0, D_RG)]
        xbuf[pl.ds(0, 8), :] = pprev_ref[...] * keep
        xbuf[pl.ds(8, TM), :] = x
        xc = _conv(xbuf, vec_ref)
        r, ig, a, s, nsp8 = _rg_gates(xc, wr_ref, wi_ref, vec_ref)
        h = hs_ref[...]
        hbuf[pl.ds(0, 8), :] = hprev_ref[...] * keep
        hbuf[pl.ds(8, TM), :] = h
        hm1 = hbuf[pl.ds(7, TM), :]
        gr = p_ref[:, pl.ds(D_RG, D_RG)]
        gel, dgel = _gelu_parts(gr)
        n, rr = _rms_fwd(gel * h)
        dyn = dy_ref[:, pl.ds(0, D_RG)]
        acc(R_GRG, dyn * n)
        dpre = _rms_bwd(dyn * vec_ref[R_GRG:R_GRG + 1, :], n, rr)
        dp_ref[:, pl.ds(D_RG, D_RG)] = ((dpre * h) * dgel).astype(_BF)
        a_s[...] = a
        g_s[...] = dpre * gel

        def step(k, c):
            t = TM - 1 - k
            g = g_s[pl.ds(t, 1), :] + c
            g_s[pl.ds(t, 1), :] = g
            return a_s[pl.ds(t, 1), :] * g

        ccar[pl.ds(0, 1), :] = lax.fori_loop(0, TM, step, ccar[pl.ds(0, 1), :], unroll=8)
        gt = g_s[...]
        da = gt * hm1
        ixc = ig * xc
        ds = gt * ixc
        dig = (gt * s) * xc
        dxc = (gt * s) * ig
        dla = da * a - ds * ((a * a) / s)
        lam = vec_ref[R_LAM:R_LAM + 1, :]
        gvec_ref[R_LAM:R_LAM + 1, :] += jnp.sum(dla * r, axis=0, keepdims=True) * (LRU_C * _sigmoid(-lam))
        dzr = (dla * nsp8) * (r * (1.0 - r))
        dzi = dig * (ig * (1.0 - ig))
        acc(R_BR, dzr)
        acc(R_BI, dzi)
        xcb = xc.astype(_BF)
        dzrb = dzr.astype(_BF)
        dzib = dzi.astype(_BF)
        gw_ref[0] += _dot_tn(xcb, dzrb)
        gw_ref[1] += _dot_tn(xcb, dzib)
        dxc = dxc + _dot_nt(dzrb, wr_ref[...]) + _dot_nt(dzib, wi_ref[...])
        acc(R_CONVB, dxc)
        for j in range(4):
            acc(R_CONVW + j, dxc * xbuf[pl.ds(5 + j, TM), :])
        dbuf[pl.ds(0, TM), :] = dxc
        dx = vec_ref[R_CONVW + 3:R_CONVW + 4, :] * dxc
        for j in range(3):
            dx = dx + vec_ref[R_CONVW + j:R_CONVW + j + 1, :] * dbuf[pl.ds(3 - j, TM), :]
        dbuf[pl.ds(TM, 8), :] = dxc[0:8, :]
        dp_ref[:, pl.ds(0, D_RG)] = dx.astype(_BF)

        lb = _sigmoid(hb_ref[0:1, :] - hb_ref[1:2, :])
        ti = lax.broadcasted_iota(jnp.int32, (HC, HC), 0)
        si = lax.broadcasted_iota(jnp.int32, (HC, HC), 1)
        causal = ti >= si
        tri = causal.astype(_F32)
        tri_u = (si >= ti).astype(_F32)
        last_row = lax.broadcasted_iota(jnp.int32, (HC, 1), 0) == HC - 1
        ghg = ghg_ref[...]

        def chunk(cc, carry):
            c = nc_t - 1 - cc
            rows = pl.ds(pl.multiple_of(c * HC, HC), HC)
            for hh in range(NH):
                lbh = lb[:, HD * hh:HD * (hh + 1)]
                q = _hg_chunk_fwd(p_ref, rows, hh, lbh, tri)
                s0 = sc_ref[c, hh]
                s0b = s0.astype(_BF)
                dst_h = dst[hh]
                dstb = dst_h.astype(_BF)
                qdb, kdb = q["qd"].astype(_BF), q["kd"].astype(_BF)
                qeb, keb = q["qe"].astype(_BF), q["ke"].astype(_BF)
                vb = q["v"].astype(_BF)
                amat = jnp.where(causal, _dot_nt(qdb, kdb), 0.0)
                o = o_ref[rows, pl.ds(HD * hh, HD)]
                hg = p_ref[rows, pl.ds(2 * D_RG + 3 * D_HG + HD * hh, HD)]
                sh = _sigmoid(hg)
                n_o, r_o = _rms_fwd(o)
                dyh = dy_ref[rows, pl.ds(D_RG + HD * hh, HD)]
                dp_ref[rows, pl.ds(2 * D_RG + 3 * D_HG + HD * hh, HD)] = (
                    (dyh * (n_o * ghg)) * _dsilu(hg, sh)).astype(_BF)
                dn = dyh * (hg * sh)
                gvec_ref[R_GHG:R_GHG + 1, pl.ds(0, HD)] += jnp.sum(dn * n_o, axis=0, keepdims=True)
                dob = _rms_bwd(dn * ghg, n_o, r_o).astype(_BF)
                da_m = jnp.where(causal, _dot_nt(dob, vb), 0.0).astype(_BF)
                dqd = _dot(da_m, kdb)
                dkd = _dot_tn(da_m, qdb)
                dqe = _dot(dob, s0b)
                dke = _dot(vb, dstb)
                dv = _dot_tn(amat.astype(_BF), dob) + _dot_nt(keb, dstb)
                d_end = jnp.sum(s0 * dst_h, axis=0, keepdims=True)
                dst[hh] = _dot_tn(dob, qeb) + q["e_end"] * dst_h
                dq = dqd * q["e_q"] + dqe * q["e_b"]
                dk = dkd * q["e_k"] + dke * q["e_l"]
                dkeke = dke * q["ke"]
                db = dqd * qdb.astype(_F32) - dkd * kdb.astype(_F32) + dqe * q["qe"] - dkeke
                extra = jnp.sum(dkeke, axis=0, keepdims=True) + d_end * q["e_end"]
                db = db + jnp.where(last_row, extra, 0.0)
                dlf = _dot_hi(tri_u, db)
                df = dlf / q["f"] - dk
                sg = q["sg"]
                gvec_ref[R_HB0:R_HB0 + 1, pl.ds(HD * hh, HD)] += jnp.sum(df * (1.0 - sg), axis=0, keepdims=True)
                dp_ref[rows, pl.ds(2 * D_RG + HD * hh, HD)] = (dq * _dsilu(q["hq"], q["sq"])).astype(_BF)
                dp_ref[rows, pl.ds(2 * D_RG + D_HG + HD * hh, HD)] = (
                    (df * (1.0 - lbh)) * (sg * (1.0 - sg))).astype(_BF)
                dp_ref[rows, pl.ds(2 * D_RG + 2 * D_HG + HD * hh, HD)] = dv.astype(_BF)
            return carry

        lax.fori_loop(0, nc_t, chunk, 0)

        @pl.when(i == nt - 1)
        def _():
            glb = gvec_ref[R_HB0:R_HB0 + 1, :] * (lb * (1.0 - lb))
            gvec_ref[R_HB0:R_HB0 + 1, :] = glb
            gvec_ref[R_HB1:R_HB1 + 1, :] = -glb
            exchange.finish()

    hbm = pl.BlockSpec(memory_space=pl.ANY)
    return pl.pallas_call(
        body, name="mixer_bwd", grid=(nt,),
        in_specs=[pl.BlockSpec((TM, D_IN), lambda i: (rev(i), 0)),
                  pl.BlockSpec((8, D_RG), lambda i: (jnp.maximum(rev(i) * (TM // 8) - 1, 0), 0)),
                  pl.BlockSpec((TM, D_RG), lambda i: (rev(i), 0)),
                  pl.BlockSpec((8, D_RG), lambda i: (jnp.maximum(rev(i) * (TM // 8) - 1, 0), 0)),
                  pl.BlockSpec((TM, D_HG), lambda i: (rev(i), 0)),
                  pl.BlockSpec((nc_t, NH, HD, HD), lambda i: (rev(i), 0, 0, 0)),
                  pl.BlockSpec((TM, D), lambda i: (rev(i), 0)),
                  _full((D_RG, D_RG)), _full((D_RG, D_RG)), _full((16, D_RG)), _full((2, D_HG)), _full((1, HD))]
        + [hbm] * nsc,
        out_specs=[pl.BlockSpec((TM, D_IN), lambda i: (rev(i), 0)), _full((16, D_RG)), _full((2, D_RG, D_RG))]
        + [hbm] * nsc,
        out_shape=[_S((t_pad, D_IN), _BF), _S((16, D_RG), _F32), _S((2, D_RG, D_RG), _F32)]
        + [_S(s.shape, s.dtype) for s in scatter],
        scratch_shapes=[pltpu.VMEM((TM + 8, D_RG), _F32), pltpu.VMEM((TM + 8, D_RG), _F32),
                        pltpu.VMEM((TM + 8, D_RG), _F32), pltpu.VMEM((TM, D_RG), _F32),
                        pltpu.VMEM((TM, D_RG), _F32), pltpu.VMEM((8, D_RG), _F32),
                        pltpu.VMEM((NH, HD, HD), _F32)] + _sem_shapes(nsc),
        compiler_params=_cp(("arbitrary",)),
    )(p, p, hs, hs, o, sc, dy, wr, wi, vec, hb, g_hg, *scatter)


def _inproj_bwd(dp, w_in, h0, dh1, g_mix, to_all):
    t_pad = dp.shape[0]
    nt = t_pad // TM
    na = len(to_all)

    def body(dp_ref, w_ref, h_ref, dh1_ref, g_ref, *rest):
        dh0_ref, gmix_ref = rest[na:na + 2]
        exchange = _Exchange([], rest[:na], rest[na + 2:2 * na + 2], rest[2 * na + 2:])
        i = pl.program_id(0)

        @pl.when(i == 0)
        def _():
            exchange.start()
            gmix_ref[...] = jnp.zeros_like(gmix_ref)

        du = jnp.zeros((TM, D), _F32)
        for j in range(N_DEV):
            du = du + _dot_nt(dp_ref[:, WIN_B * j:WIN_B * (j + 1)], w_ref[j])
        n, r = _rms_fwd(h_ref[...])
        gmix_ref[...] += jnp.sum(du * n, axis=0, keepdims=True)
        dh0_ref[...] = dh1_ref[...] + _rms_bwd(du * g_ref[...], n, r)

        @pl.when(i == nt - 1)
        def _():
            exchange.finish()

    tile = pl.BlockSpec((TM, D), lambda i: (i, 0))
    hbm = pl.BlockSpec(memory_space=pl.ANY)
    return pl.pallas_call(
        body, name="inproj_bwd", grid=(nt,),
        in_specs=[pl.BlockSpec((TM, D_IN), lambda i: (i, 0)), _const((N_DEV, D, WIN_B)), tile, tile, _full((1, D))]
        + [hbm] * na,
        out_specs=[tile, _full((1, D))] + [hbm] * na,
        out_shape=[_S((t_pad, D), _F32), _S((1, D), _F32)] + [_S((N_DEV,) + g.shape, g.dtype) for g in to_all],
        scratch_shapes=_sem_shapes(na),
        compiler_params=_cp(("arbitrary",)),
    )(dp, w_in, h0, dh1, g_mix, *to_all)


def _wgrad_send(u, dp, order, to_all):
    t_pad = u.shape[0]
    na = len(to_all)

    def body(order_ref, u_ref, dp_ref, *rest):
        all_in = rest[:na]
        recv_ref = rest[na]
        all_out = rest[na + 1:2 * na + 1]
        buf, blk_send, blk_recv, blk_local = rest[2 * na + 1:2 * na + 5]
        exchange = _Exchange([], all_in, all_out, rest[2 * na + 5:])
        s = pl.program_id(0)
        x, y, c = _coords()
        me = 4 * x + 2 * y + c
        slot = s % 2

        def send(step):
            r = _SEND_ORDER[step]
            return pltpu.make_async_remote_copy(
                src_ref=buf.at[step % 2], dst_ref=recv_ref.at[me], send_sem=blk_send.at[step], recv_sem=blk_recv.at[r - 1],
                device_id=(x ^ (r >> 2), y ^ ((r >> 1) & 1), c ^ (r & 1)), device_id_type=_MESH)

        @pl.when(s == 0)
        def _():
            exchange.start()

        for step in range(2, N_DEV):
            @pl.when(s == step)
            def _(step=step):
                send(step - 2).wait_send()

        buf[slot] = _dot_tn(u_ref[...], dp_ref[...]).astype(_BF)

        for step in range(N_DEV - 1):
            @pl.when(s == step)
            def _(step=step):
                send(step).start()

        @pl.when(s == N_DEV - 1)
        def _():
            mine = pltpu.make_async_copy(buf.at[(N_DEV - 1) % 2], recv_ref.at[me], blk_local.at[0])
            mine.start()
            send(N_DEV - 2).wait_send()
            for r in range(1, N_DEV):
                px, py, pc = x ^ (r >> 2), y ^ ((r >> 1) & 1), c ^ (r & 1)
                pltpu.make_async_remote_copy(
                    src_ref=buf.at[0], dst_ref=recv_ref.at[4 * px + 2 * py + pc], send_sem=blk_send.at[0],
                    recv_sem=blk_recv.at[r - 1], device_id=(px, py, pc), device_id_type=_MESH).wait_recv()
            mine.wait()
            exchange.finish()

    hbm = pl.BlockSpec(memory_space=pl.ANY)
    return pl.pallas_call(
        body, name="wgrad_in_send",
        grid_spec=pltpu.PrefetchScalarGridSpec(
            num_scalar_prefetch=1, grid=(N_DEV,),
            in_specs=[pl.BlockSpec((t_pad, D), lambda s, order: (0, 0)),
                      pl.BlockSpec((t_pad, WIN_B), lambda s, order: (0, order[s]))] + [hbm] * na,
            out_specs=[hbm] * (na + 1),
            scratch_shapes=[pltpu.VMEM((2, D, WIN_B), _BF), pltpu.SemaphoreType.DMA((N_DEV - 1,)),
                            pltpu.SemaphoreType.DMA((N_DEV - 1,)), pltpu.SemaphoreType.DMA((1,))] + _sem_shapes(na)),
        out_shape=[_S((N_DEV, D, WIN_B), _BF)] + [_S((N_DEV,) + g.shape, g.dtype) for g in to_all],
        compiler_params=_cp(("arbitrary",)),
    )(order, u, dp, *to_all)


def _wgrad(name, a, b, a_spec, b_spec, n_blocks, out_block):
    def body(a_ref, b_ref, o_ref):
        av = a_ref[0] if len(a_ref.shape) == 3 else a_ref[...]
        bv = b_ref[0] if len(b_ref.shape) == 3 else b_ref[...]
        o_ref[0] = _dot_tn(av, bv).astype(_BF)

    return pl.pallas_call(
        body, name=name, grid=(n_blocks,),
        in_specs=[a_spec, b_spec],
        out_specs=pl.BlockSpec((1,) + out_block, lambda j: (j, 0, 0)),
        out_shape=_S((n_blocks,) + out_block, _BF),
        compiler_params=_cp(("arbitrary",)),
    )(a, b)


def _coords():
    return lax.axis_index("x"), lax.axis_index("y"), lax.axis_index("c")


def _sem_shapes(na):
    return [pltpu.SemaphoreType.DMA((7 * na,)), pltpu.SemaphoreType.DMA((7 * na,)), pltpu.SemaphoreType.DMA((na,))]


class _Gather:
    def __init__(self, srcs, outs, sems):
        self.srcs, self.outs = srcs, outs
        self.send_sems, self.recv_sems, self.local_sems = sems
        self.na = len(srcs)
        x, y, c = _coords()
        self.pos = (x, y, c)
        self.me = 4 * x + 2 * y + c
        self.sibling = (x, y, 1 - c)
        self.chips = [(1 - x, y), (x, 1 - y), (1 - x, 1 - y)]

    @staticmethod
    def _slot(px, py, pc):
        return 4 * px + 2 * py + pc

    def _copy(self, a, k, block, to, own=False):
        return pltpu.make_async_remote_copy(
            src_ref=self.srcs[a] if own else self.outs[a].at[block], dst_ref=self.outs[a].at[block],
            send_sem=self.send_sems.at[7 * a + k], recv_sem=self.recv_sems.at[7 * a + k],
            device_id=to, device_id_type=_MESH)

    def _mine(self, a):
        return pltpu.make_async_copy(self.srcs[a], self.outs[a].at[self.me], self.local_sems.at[a])

    def _first(self):
        c = self.pos[2]
        cps = []
        for a in range(self.na):
            cps.append(self._copy(a, 0, self.me, self.sibling, own=True))
            cps += [self._copy(a, 1 + j, self.me, (*chip, c), own=True) for j, chip in enumerate(self.chips)]
        return cps

    def _passed(self):
        c = self.pos[2]
        return [self._copy(a, 4 + j, self._slot(*chip, c), self.sibling)
                for j, chip in enumerate(self.chips) for a in range(self.na)]

    def start(self):
        for a in range(self.na):
            self._mine(a).start()
        for cp in self._first():
            cp.start()

    def forward(self, j):
        c = self.pos[2]
        chip = self.chips[j]
        for a in range(self.na):
            self._copy(a, 1 + j, self._slot(*chip, c), self.pos).wait_recv()
            self._copy(a, 4 + j, self._slot(*chip, c), self.sibling).start()

    def finish(self):
        x, y, c = self.pos
        for a in range(self.na):
            self._copy(a, 0, self._slot(x, y, 1 - c), self.pos).wait_recv()
        for j, chip in enumerate(self.chips):
            for a in range(self.na):
                self._copy(a, 4 + j, self._slot(*chip, 1 - c), self.pos).wait_recv()
        for cp in self._first() + self._passed():
            cp.wait_send()
        for a in range(self.na):
            self._mine(a).wait()


class _Exchange:
    def __init__(self, scatter, gather, outs, sems):
        self.ins = list(scatter) + list(gather)
        self.ns, self.na = len(scatter), len(scatter) + len(gather)
        self.outs = outs
        self.send_sems, self.recv_sems, self.local_sems = sems
        x, y, c = _coords()
        self.pos = (x, y, c)
        self.me = 4 * x + 2 * y + c

    def _peer(self, r):
        x, y, c = self.pos
        return x ^ (r >> 2), y ^ ((r >> 1) & 1), c ^ (r & 1)

    def _src(self, a, block):
        return self.ins[a].at[block] if a < self.ns else self.ins[a]

    def _local(self, a):
        return pltpu.make_async_copy(self._src(a, self.me), self.outs[a].at[self.me], self.local_sems.at[a])

    def _send(self, a, r):
        px, py, pc = self._peer(r)
        return pltpu.make_async_remote_copy(
            src_ref=self._src(a, 4 * px + 2 * py + pc), dst_ref=self.outs[a].at[self.me],
            send_sem=self.send_sems.at[7 * a + r - 1], recv_sem=self.recv_sems.at[7 * a + r - 1],
            device_id=(px, py, pc), device_id_type=_MESH)

    def _recv(self, a, r):
        px, py, pc = self._peer(r)
        return pltpu.make_async_remote_copy(
            src_ref=self._src(a, self.me), dst_ref=self.outs[a].at[4 * px + 2 * py + pc],
            send_sem=self.send_sems.at[7 * a + r - 1], recv_sem=self.recv_sems.at[7 * a + r - 1],
            device_id=(px, py, pc), device_id_type=_MESH)

    def start(self):
        for a in range(self.na):
            self._local(a).start()
        for r in range(1, N_DEV):
            for a in range(self.na):
                self._send(a, r).start()

    def finish(self):
        for r in range(1, N_DEV):
            for a in range(self.na):
                self._recv(a, r).wait_recv()
        for r in range(1, N_DEV):
            for a in range(self.na):
                self._send(a, r).wait_send()
        for a in range(self.na):
            self._local(a).wait()


def _allgather_first(gather_f32, cast_f32, gather_dtypes):
    ng, nc = len(gather_f32), len(cast_f32)

    def body(*refs):
        ins, cins = refs[:ng], refs[ng:ng + nc]
        outs, couts = refs[ng + nc:2 * ng + nc], refs[2 * ng + nc:2 * ng + 2 * nc]
        stage = refs[2 * ng + 2 * nc:3 * ng + 2 * nc]
        sems = refs[3 * ng + 2 * nc:]
        for a in range(ng):
            stage[a][...] = ins[a][...].astype(gather_dtypes[a])
        g = _Gather(stage, outs, sems)
        g.start()
        for a in range(nc):
            couts[a][...] = cins[a][...].astype(_BF)
        for j in range(3):
            g.forward(j)
        g.finish()

    vm = pl.BlockSpec(memory_space=pltpu.VMEM)
    return pl.pallas_call(
        body, name="allgather_first",
        in_specs=[vm] * (ng + nc),
        out_specs=[pl.BlockSpec(memory_space=pl.ANY)] * ng + [vm] * nc,
        out_shape=[_S((N_DEV,) + l.shape, dt) for l, dt in zip(gather_f32, gather_dtypes)]
        + [_S(l.shape, _BF) for l in cast_f32],
        scratch_shapes=[pltpu.VMEM(l.shape, dt) for l, dt in zip(gather_f32, gather_dtypes)] + _sem_shapes(ng),
        compiler_params=pltpu.CompilerParams(vmem_limit_bytes=VMEM_LIMIT),
    )(*gather_f32, *cast_f32)


def _adamw_math(w, g, m, v):
    m2 = ADAM_B1 * m + (1.0 - ADAM_B1) * g
    v2 = ADAM_B2 * v + (1.0 - ADAM_B2) * (g * g)
    m_hat = m2 / (1.0 - ADAM_B1 ** ADAM_STEP)
    v_hat = v2 / (1.0 - ADAM_B2 ** ADAM_STEP)
    delta = -ADAM_LR * (m_hat / (jnp.sqrt(v_hat) + ADAM_EPS) + ADAM_WD * w)
    return delta, m2, v2


def _adamw_big(name, recv, w, m, v, rows):
    r_all, c_all = w.shape

    def body(r_ref, w_ref, m_ref, v_ref, g_out, d_out, m_out, v_out):
        g = r_ref[0].astype(_F32)
        for k in range(1, N_DEV):
            g = g + r_ref[k].astype(_F32)
        delta, m2, v2 = _adamw_math(w_ref[...], g, m_ref[...], v_ref[...])
        g_out[...] = g
        d_out[...] = delta
        m_out[...] = m2
        v_out[...] = v2

    tile = pl.BlockSpec((rows, c_all), lambda i: (i, 0))
    return pl.pallas_call(
        body, name=name, grid=(r_all // rows,),
        in_specs=[pl.BlockSpec((N_DEV, rows, c_all), lambda i: (0, i, 0)), tile, tile, tile],
        out_specs=[tile] * 4,
        out_shape=[_S(w.shape, _F32)] * 4,
        compiler_params=_cp(("arbitrary",)),
    )(recv, w, m, v)


def _adamw_small(gathered, slices, wmv):
    ng, npar = len(gathered), len(slices)

    def body(*refs):
        g_refs = refs[:ng]
        wmv_refs = refs[ng:ng + 3 * npar]
        outs = refs[ng + 3 * npar:]
        for i, (ai, r0, nr, ncol) in enumerate(slices):
            g = g_refs[ai][0, pl.ds(r0, nr), pl.ds(0, ncol)]
            for k in range(1, N_DEV):
                g = g + g_refs[ai][k, pl.ds(r0, nr), pl.ds(0, ncol)]
            w_ref, m_ref, v_ref = wmv_refs[3 * i:3 * i + 3]
            delta, m2, v2 = _adamw_math(w_ref[...], g, m_ref[...], v_ref[...])
            outs[4 * i][...] = g
            outs[4 * i + 1][...] = delta
            outs[4 * i + 2][...] = m2
            outs[4 * i + 3][...] = v2

    flat = [t for trip in wmv for t in trip]
    out_shape = []
    for w, _, _ in wmv:
        out_shape += [_S(w.shape, _F32)] * 4
    return pl.pallas_call(
        body, name="adamw_small", out_shape=out_shape,
        compiler_params=pltpu.CompilerParams(vmem_limit_bytes=VMEM_LIMIT),
    )(*gathered, *flat)


def _block_diag(w):
    eye = jnp.eye(8, dtype=w.dtype)
    return (w[:, :, None, :] * eye[:, None, :, None]).reshape(D_RG, D_RG)


def _diag_blocks(g):
    return jnp.concatenate([g[64 * h:64 * (h + 1), 64 * h:64 * (h + 1)] for h in range(8)], axis=0)


def _local_step(x, tgt, meta, g_mix, w_in, vec, wr, wi, hb, g_hg, w_out_l, g_ffn, w_gu_l, w_down_l, g_fin):
    seq = x.shape[0]
    n_valid = N_META + seq
    t_pad = -(-n_valid // TM) * TM
    h0 = jnp.concatenate([meta, x, jnp.zeros((t_pad - n_valid, D), _F32)], axis=0)
    tgt_p = jnp.concatenate([jnp.zeros((N_META, D), _F32), tgt, jnp.zeros((t_pad - n_valid, D), _F32)], axis=0)

    p, u = _inproj(h0, g_mix, w_in)
    y, hs, o, sc, w_gu, w_out, w_down = _mixer_fwd(p, wr, wi, vec, hb, g_hg, [w_gu_l, w_out_l, w_down_l])
    w_out = w_out.reshape(D, D)
    w_down = w_down.reshape(4, FFB, D)
    h1, v = _outproj(h0, y, w_out, g_ffn)
    gu, act, dh2, dh2b, loss, gfin = _ffn_loss(v, h1, w_gu, w_down, g_fin, tgt_p, n_valid)

    dgu, dh1, dh1b, dy, gffn = _ffn_bwd(dh2, dh2b, gu, h1, g_ffn, w_gu, w_down, w_out)
    g_wdown = _wgrad("wgrad_down", act, dh2b, pl.BlockSpec((1, t_pad, FFB), lambda j: (j, 0, 0)),
                     pl.BlockSpec((t_pad, D), lambda j: (0, 0)), 4, (FFB, D))
    g_wgu = _wgrad("wgrad_gate_up", v, dgu, pl.BlockSpec((t_pad, D), lambda j: (0, 0)),
                   pl.BlockSpec((1, t_pad, FFB), lambda j: (j, 0, 0)), N_DEV, (D, FFB))
    g_wout = _wgrad("wgrad_out", y, dh1b, pl.BlockSpec((t_pad, D // N_DEV), lambda j: (0, j)),
                    pl.BlockSpec((t_pad, D), lambda j: (0, 0)), N_DEV, (D // N_DEV, D))
    dp, gvec, gw, r_wgu, r_wout, r_wdown = _mixer_bwd(
        p, hs, o, sc, dy, wr, wi, vec, hb, g_hg, [g_wgu, g_wout, g_wdown.reshape(N_DEV, D_FF // N_DEV, D)])
    pack_c = jnp.concatenate([_diag_blocks(gw[0]), _diag_blocks(gw[1])], axis=0)
    dh0, gmix, all_b, all_c = _inproj_bwd(dp, w_in, h0, dh1, g_mix, [gvec, pack_c])
    pack_a = jnp.concatenate([gmix, gffn, gfin, jnp.zeros((5, D), _F32), dh0[:N_META]], axis=0)
    me = 4 * lax.axis_index("x") + 2 * lax.axis_index("y") + lax.axis_index("c")
    order = (me ^ jnp.array(_SEND_ORDER, jnp.int32)).astype(jnp.int32)
    r_win, all_a = _wgrad_send(u, dp, order, [pack_a])
    return loss, dh0, (r_win, r_wgu, r_wout, r_wdown), (all_a, all_b, all_c)


def kernel(x, meta_tokens, mix_norm_g, w_in, conv_w, conv_b, w_rgate, b_rgate, w_igate, b_igate, lru_lambda, rg_norm_g, hg_lower_bound, hg_norm_g, w_out, ffn_norm_g, w_gate_up, w_down, final_norm_g, loss_target, m_meta_tokens, m_mix_norm_g, m_w_in, m_conv_w, m_conv_b, m_w_rgate, m_b_rgate, m_w_igate, m_b_igate, m_lru_lambda, m_rg_norm_g, m_hg_lower_bound, m_hg_norm_g, m_w_out, m_ffn_norm_g, m_w_gate_up, m_w_down, m_final_norm_g, v_meta_tokens, v_mix_norm_g, v_w_in, v_conv_w, v_conv_b, v_w_rgate, v_b_rgate, v_w_igate, v_b_igate, v_lru_lambda, v_rg_norm_g, v_hg_lower_bound, v_hg_norm_g, v_w_out, v_ffn_norm_g, v_w_gate_up, v_w_down, v_final_norm_g):
    seq = x.shape[1]
    me = 4 * lax.axis_index("x") + 2 * lax.axis_index("y") + lax.axis_index("c")

    small_l = jnp.concatenate([meta_tokens, jnp.pad(conv_w[0], ((0, 4), (0, 64)))], axis=0)
    w_in_g, small_g, w_gu_l, w_out_l, w_down_l = _allgather_first(
        [w_in[0], small_l], [w_gate_up[0], w_out[0], w_down[0]], [_BF, _F32])
    meta_full = jnp.transpose(small_g[:, :N_META, :], (1, 0, 2)).reshape(N_META, D)
    conv_w_full = jnp.transpose(small_g[:, N_META:N_META + 4, :64], (1, 0, 2)).reshape(4, D_RG)
    vec = jnp.concatenate([conv_b, b_rgate, b_igate, lru_lambda, rg_norm_g, jnp.zeros((3, D_RG), _F32),
                           conv_w_full, jnp.zeros((4, D_RG), _F32)], axis=0)
    wr = _block_diag(w_rgate[0]).astype(_BF)
    wi = _block_diag(w_igate[0]).astype(_BF)

    loss, dh0, (r_win, r_wgu, r_wout, r_wdown), (all_a, all_b, all_c) = _local_step(
        x[0], loss_target[0], meta_full, mix_norm_g, w_in_g, vec, wr, wi, hg_lower_bound, hg_norm_g,
        w_out_l, ffn_norm_g, w_gu_l, w_down_l, final_norm_g.reshape(1, D))
    grad_x = dh0[N_META:N_META + seq][None]

    outs = {}
    outs["w_in"] = _adamw_big("adamw_w_in", r_win, w_in[0], m_w_in[0], v_w_in[0], 256)
    outs["w_gate_up"] = _adamw_big("adamw_w_gate_up", r_wgu, w_gate_up[0], m_w_gate_up[0], v_w_gate_up[0], 256)
    outs["w_out"] = _adamw_big("adamw_w_out", r_wout, w_out[0], m_w_out[0], v_w_out[0], 128)
    outs["w_down"] = _adamw_big("adamw_w_down", r_wdown, w_down[0], m_w_down[0], v_w_down[0], 176)

    meta_part = lax.dynamic_slice_in_dim(all_a[:, R_META:R_META + N_META, :], me * 128, 128, axis=2)
    convw_part = lax.dynamic_slice_in_dim(all_b[:, R_CONVW:R_CONVW + 4, :], me * 64, 64, axis=2)
    gathered = [all_a, all_b, all_c, meta_part, convw_part]
    small_params = [
        ("meta_tokens", (3, 0, N_META, 128), (meta_tokens, m_meta_tokens, v_meta_tokens), (N_META, 128)),
        ("mix_norm_g", (0, R_GMIX, 1, D), (mix_norm_g, m_mix_norm_g, v_mix_norm_g), (1, D)),
        ("conv_w", (4, 0, 4, 64), (conv_w, m_conv_w, v_conv_w), (4, 64)),
        ("conv_b", (1, R_CONVB, 1, D_RG), (conv_b, m_conv_b, v_conv_b), (1, D_RG)),
        ("w_rgate", (2, 0, 512, 64), (w_rgate, m_w_rgate, v_w_rgate), (512, 64)),
        ("b_rgate", (1, R_BR, 1, D_RG), (b_rgate, m_b_rgate, v_b_rgate), (1, D_RG)),
        ("w_igate", (2, 512, 512, 64), (w_igate, m_w_igate, v_w_igate), (512, 64)),
        ("b_igate", (1, R_BI, 1, D_RG), (b_igate, m_b_igate, v_b_igate), (1, D_RG)),
        ("lru_lambda", (1, R_LAM, 1, D_RG), (lru_lambda, m_lru_lambda, v_lru_lambda), (1, D_RG)),
        ("rg_norm_g", (1, R_GRG, 1, D_RG), (rg_norm_g, m_rg_norm_g, v_rg_norm_g), (1, D_RG)),
        ("hg_lower_bound", (1, R_HB0, 2, D_HG), (hg_lower_bound, m_hg_lower_bound, v_hg_lower_bound), (2, D_HG)),
        ("hg_norm_g", (1, R_GHG, 1, HD), (hg_norm_g, m_hg_norm_g, v_hg_norm_g), (1, HD)),
        ("ffn_norm_g", (0, R_GFFN, 1, D), (ffn_norm_g, m_ffn_norm_g, v_ffn_norm_g), (1, D)),
        ("final_norm_g", (0, R_GFIN, 1, D), (final_norm_g, m_final_norm_g, v_final_norm_g), (1, D)),
    ]
    res = _adamw_small(gathered, [s[1] for s in small_params],
                       [tuple(t.reshape(s[3]) for t in s[2]) for s in small_params])
    for i, s in enumerate(small_params):
        outs[s[0]] = [r.reshape(s[2][0].shape) for r in res[4 * i:4 * i + 4]]
    for n, ref in (("w_in", w_in), ("w_gate_up", w_gate_up), ("w_out", w_out), ("w_down", w_down)):
        outs[n] = [r.reshape(ref.shape) for r in outs[n]]

    loss_all = lax.psum(loss[0, 0], ("x", "y", "c"))
    order = ["meta_tokens", "mix_norm_g", "w_in", "conv_w", "conv_b", "w_rgate", "b_rgate", "w_igate", "b_igate",
             "lru_lambda", "rg_norm_g", "hg_lower_bound", "hg_norm_g", "w_out", "ffn_norm_g", "w_gate_up", "w_down",
             "final_norm_g"]
    return (loss_all, grad_x, *[outs[n][0] for n in order], *[outs[n][1] for n in order],
            *[outs[n][2] for n in order], *[outs[n][3] for n in order])
```

```python
import functools

import jax
import jax.numpy as jnp
from jax import lax
from jax.experimental import pallas as pl
from jax.experimental.pallas import tpu as pltpu

_BF = jnp.bfloat16
_F32 = jnp.float32
_S = jax.ShapeDtypeStruct
_MESH = pl.DeviceIdType.MESH

N_DEV = 8
N_META = 16
D = 1024
D_RG = 512
D_HG = 512
HD = 128
NH = D_HG // HD
D_IN = 3072
D_FF = 2816
FFB = D_FF // 4
WIN_B = D_IN // N_DEV
EPS = 1e-6
LRU_C = 8.0
TM = 256
HC = 64
VMEM_LIMIT = 56 * 1024 * 1024

ADAM_LR = 0.001
ADAM_B1 = 0.9
ADAM_B2 = 0.999
ADAM_EPS = 1e-08
ADAM_WD = 0.01
ADAM_STEP = 10

_SEND_ORDER = (6, 7, 4, 5, 2, 3, 1, 0)

R_CONVB, R_BR, R_BI, R_LAM, R_GRG, R_HB0, R_HB1, R_GHG, R_CONVW = 0, 1, 2, 3, 4, 5, 6, 7, 8
R_GMIX, R_GFFN, R_GFIN, R_META = 0, 1, 2, 8


def _cp(sem=None, **kw):
    return pltpu.CompilerParams(dimension_semantics=sem, vmem_limit_bytes=VMEM_LIMIT, **kw)


def _dot(a, b):
    return jnp.dot(a, b, preferred_element_type=_F32)


def _dot_nt(a, b):
    return lax.dot_general(a, b, (((1,), (1,)), ((), ())), preferred_element_type=_F32)


def _dot_tn(a, b):
    return lax.dot_general(a, b, (((0,), (0,)), ((), ())), preferred_element_type=_F32)


def _sigmoid(x):
    return jax.nn.sigmoid(x)


def _dsilu(x, s):
    return s * (1.0 + x * (1.0 - s))


_GELU_C = 0.7978845608028654


def _gelu_parts(x):
    t = jnp.tanh(_GELU_C * (x + 0.044715 * (x * x * x)))
    g = 0.5 * x * (1.0 + t)
    dg = 0.5 * (1.0 + t) + 0.5 * x * (1.0 - t * t) * (_GELU_C * (1.0 + 3.0 * 0.044715 * (x * x)))
    return g, dg


def _softplus(z):
    e = jnp.exp(-jnp.abs(z))
    w = 1.0 + e
    l1p = jnp.where(w == 1.0, e, jnp.log(w) * e / jnp.where(w == 1.0, 1.0, w - 1.0))
    return jnp.maximum(z, 0.0) + l1p


def _rms_fwd(x):
    r = lax.rsqrt(jnp.mean(x * x, axis=-1, keepdims=True) + EPS)
    return x * r, r


def _rms_bwd(dyg, n, r):
    return r * (dyg - n * jnp.mean(dyg * n, axis=-1, keepdims=True))


def _full(shape):
    nd = len(shape)
    return pl.BlockSpec(shape, lambda i: (0,) * nd)


def _const(shape):
    nd = len(shape)
    return pl.BlockSpec(shape, lambda i: (0,) * nd, pipeline_mode=pl.Buffered(1))


def _inproj(h0, g_mix, w_in):
    t_pad = h0.shape[0]

    def body(h_ref, g_ref, w_ref, p_ref, u_ref):
        n, _ = _rms_fwd(h_ref[...])
        u = (n * g_ref[...]).astype(_BF)
        u_ref[...] = u
        for j in range(N_DEV):
            p_ref[:, WIN_B * j:WIN_B * (j + 1)] = _dot(u, w_ref[j])

    return pl.pallas_call(
        body, name="inproj", grid=(t_pad // TM,),
        in_specs=[pl.BlockSpec((TM, D), lambda i: (i, 0)), _full((1, D)), _const((N_DEV, D, WIN_B))],
        out_specs=[pl.BlockSpec((TM, D_IN), lambda i: (i, 0)), pl.BlockSpec((TM, D), lambda i: (i, 0))],
        out_shape=[_S((t_pad, D_IN), _F32), _S((t_pad, D), _BF)],
        compiler_params=_cp(("arbitrary",)),
    )(h0, g_mix, w_in)


def _rg_gates(xc, wr_ref, wi_ref, vec_ref):
    xcb = xc.astype(_BF)
    r = _sigmoid(_dot(xcb, wr_ref[...]) + vec_ref[R_BR:R_BR + 1, :])
    ig = _sigmoid(_dot(xcb, wi_ref[...]) + vec_ref[R_BI:R_BI + 1, :])
    nsp8 = -LRU_C * _softplus(-vec_ref[R_LAM:R_LAM + 1, :])
    la = nsp8 * r
    a = jnp.exp(la)
    th = jnp.tanh(la)
    s = jnp.sqrt(-2.0 * th / (1.0 - th))
    return r, ig, a, s, nsp8


def _conv(xbuf, vec_ref):
    acc = vec_ref[R_CONVW:R_CONVW + 1, :] * xbuf[pl.ds(5, TM), :]
    for j in range(1, 4):
        acc = acc + vec_ref[R_CONVW + j:R_CONVW + j + 1, :] * xbuf[pl.ds(5 + j, TM), :]
    return vec_ref[R_CONVB:R_CONVB + 1, :] + acc


def _dot3(m01, x):
    hi = x.astype(_BF)
    r1 = x - hi.astype(_F32)
    mid = r1.astype(_BF)
    lo = (r1 - mid.astype(_F32)).astype(_BF)
    return (_dot(m01, lo) + _dot(m01, mid)) + _dot(m01, hi)


def _chunk_masks():
    row = lax.broadcasted_iota(jnp.int32, (TM, TM), 0)
    col = lax.broadcasted_iota(jnp.int32, (TM, TM), 1)
    shift = HC.bit_length() - 1
    same = lax.shift_right_logical(row, shift) == lax.shift_right_logical(col, shift)
    return same, same & (row >= col), same & (col >= row)


def _per_chunk_rows(x, r):
    return jnp.concatenate([jnp.broadcast_to(x[HC * c + r:HC * c + r + 1, :], (HC, x.shape[1]))
                            for c in range(TM // HC)], axis=0)


def _hg_prep(p_ref, lb, tri_blk):
    hq = p_ref[:, pl.ds(2 * D_RG, D_HG)]
    hf = p_ref[:, pl.ds(2 * D_RG + D_HG, D_HG)]
    sq = _sigmoid(hq)
    q = hq * sq
    sg = _sigmoid(hf)
    f = lb + (1.0 - lb) * sg
    k = 1.0 - f
    b = _dot3(tri_blk, jnp.log(f))
    bm = _per_chunk_rows(b, HC // 2 - 1)
    bl = _per_chunk_rows(b, HC - 1)
    e_q = jnp.exp(b - bm)
    e_k = jnp.exp(bm - b)
    e_b = jnp.exp(b)
    e_l = jnp.exp(bl - b)
    return dict(hq=hq, sq=sq, q=q, sg=sg, f=f, k=k, e_q=e_q, e_k=e_k, e_b=e_b, e_l=e_l,
                qd=q * e_q, kd=k * e_k, qe=q * e_b, ke=k * e_l, e_end=jnp.exp(bl))


def _mixer_fwd(p, wr, wi, vec, hb, g_hg, shards):
    t_pad = p.shape[0]
    nt = t_pad // TM
    nc_t = TM // HC
    nsh = len(shards)

    def body(p_ref, wr_ref, wi_ref, vec_ref, hb_ref, ghg_ref, *rest):
        sh_refs, rest = rest[:nsh], rest[nsh:]
        y_ref, hs_ref, o_ref, sc_ref = rest[:4]
        gath_refs, rest = rest[4:4 + nsh], rest[4 + nsh:]
        xbuf, a_s, b_s, hcar, st, qd_s, kd_s, qe_s, ke_s, v_s, u_s = rest[:11]
        gather = _Gather(sh_refs, gath_refs, rest[11:])
        i = pl.program_id(0)

        @pl.when(i == 0)
        def _():
            gather.start()
            xbuf[pl.ds(0, 8), :] = jnp.zeros((8, D_RG), _F32)
            hcar[...] = jnp.zeros_like(hcar)
            st[...] = jnp.zeros_like(st)

        x = p_ref[:, pl.ds(0, D_RG)]
        xbuf[pl.ds(8, TM), :] = x
        xc = _conv(xbuf, vec_ref)
        xbuf[pl.ds(0, 8), :] = x[TM - 8:, :]
        r, ig, a, s, _ = _rg_gates(xc, wr_ref, wi_ref, vec_ref)
        a_s[...] = a
        b_s[...] = s * (ig * xc)

        def step(t, h):
            h = a_s[pl.ds(t, 1), :] * h + b_s[pl.ds(t, 1), :]
            hs_ref[pl.ds(t, 1), :] = h
            return h

        hcar[pl.ds(0, 1), :] = lax.fori_loop(0, TM, step, hcar[pl.ds(0, 1), :], unroll=8)
        gel, _ = _gelu_parts(p_ref[:, pl.ds(D_RG, D_RG)])
        n, _ = _rms_fwd(gel * hs_ref[...])
        y_ref[:, pl.ds(0, D_RG)] = (n * vec_ref[R_GRG:R_GRG + 1, :]).astype(_BF)

        lb = _sigmoid(hb_ref[0:1, :] - hb_ref[1:2, :])
        _, tri_blk, _ = _chunk_masks()
        q = _hg_prep(p_ref, lb, tri_blk.astype(_BF))
        for name, ref in (("qd", qd_s), ("kd", kd_s), ("qe", qe_s), ("ke", ke_s)):
            ref[...] = q[name].astype(_BF)
        v_s[...] = p_ref[:, pl.ds(2 * D_RG + 2 * D_HG, D_HG)].astype(_BF)
        e_end = q["e_end"]
        causal = (lax.broadcasted_iota(jnp.int32, (HC, HC), 0) >= lax.broadcasted_iota(jnp.int32, (HC, HC), 1))
        for c in range(nc_t):
            for h in range(NH):
                rs, cs = pl.ds(HC * c, HC), pl.ds(HD * h, HD)
                amat = jnp.where(causal, _dot_nt(qd_s[rs, cs], kd_s[rs, cs]), 0.0)
                o_ref[rs, cs] = _dot(amat.astype(_BF), v_s[rs, cs])
                u_s[NH * c + h] = _dot_tn(v_s[rs, cs], ke_s[rs, cs])
        for h in range(NH):
            cs = pl.ds(HD * h, HD)
            s_run = st[h]
            for c in range(nc_t):
                rs = pl.ds(HC * c, HC)
                sc_ref[c, h] = s_run
                o_ref[rs, cs] += _dot_nt(qe_s[rs, cs], s_run.astype(_BF))
                s_run = e_end[HC * c:HC * c + 1, HD * h:HD * (h + 1)] * s_run + u_s[NH * c + h]
            st[h] = s_run
        for h in range(NH):
            cs = pl.ds(HD * h, HD)
            n_o, _ = _rms_fwd(o_ref[:, cs])
            hg = p_ref[:, pl.ds(2 * D_RG + 3 * D_HG + HD * h, HD)]
            y_ref[:, pl.ds(D_RG + HD * h, HD)] = ((n_o * ghg_ref[...]) * (hg * _sigmoid(hg))).astype(_BF)

        for j in range(3):
            @pl.when(i == min(nt // 2 + 2 * j, nt - 1))
            def _(j=j):
                gather.forward(j)

        @pl.when(i == nt - 1)
        def _():
            gather.finish()

    hbm = pl.BlockSpec(memory_space=pl.ANY)
    return pl.pallas_call(
        body, name="mixer_fwd", grid=(nt,),
        in_specs=[pl.BlockSpec((TM, D_IN), lambda i: (i, 0)), _full((D_RG, D_RG)), _full((D_RG, D_RG)),
                  _full((16, D_RG)), _full((2, D_HG)), _full((1, HD))] + [hbm] * nsh,
        out_specs=[pl.BlockSpec((TM, D), lambda i: (i, 0)), pl.BlockSpec((TM, D_RG), lambda i: (i, 0)),
                   pl.BlockSpec((TM, D_HG), lambda i: (i, 0)),
                   pl.BlockSpec((nc_t, NH, HD, HD), lambda i: (i, 0, 0, 0))] + [hbm] * nsh,
        out_shape=[_S((t_pad, D), _BF), _S((t_pad, D_RG), _F32), _S((t_pad, D_HG), _F32),
                   _S((t_pad // HC, NH, HD, HD), _F32)] + [_S((N_DEV,) + s.shape, s.dtype) for s in shards],
        scratch_shapes=[pltpu.VMEM((TM + 8, D_RG), _F32), pltpu.VMEM((TM, D_RG), _F32),
                        pltpu.VMEM((TM, D_RG), _F32), pltpu.VMEM((8, D_RG), _F32),
                        pltpu.VMEM((NH, HD, HD), _F32)] + [pltpu.VMEM((TM, D_HG), _BF) for _ in range(5)]
        + [pltpu.VMEM((nc_t * NH, HD, HD), _F32)] + _sem_shapes(nsh),
        compiler_params=_cp(("arbitrary",)),
    )(p, wr, wi, vec, hb, g_hg, *shards)


def _outproj(h0, y, w_out, g_ffn):
    t_pad = h0.shape[0]

    def body(h_ref, y_ref, w_ref, g_ref, h1_ref, v_ref):
        h1 = h_ref[...] + _dot(y_ref[...], w_ref[...])
        h1_ref[...] = h1
        n, _ = _rms_fwd(h1)
        v_ref[...] = (n * g_ref[...]).astype(_BF)

    return pl.pallas_call(
        body, name="outproj", grid=(t_pad // TM,),
        in_specs=[pl.BlockSpec((TM, D), lambda i: (i, 0)), pl.BlockSpec((TM, D), lambda i: (i, 0)),
                  _full((D, D)), _full((1, D))],
        out_specs=[pl.BlockSpec((TM, D), lambda i: (i, 0)), pl.BlockSpec((TM, D), lambda i: (i, 0))],
        out_shape=[_S((t_pad, D), _F32), _S((t_pad, D), _BF)],
        compiler_params=_cp(("arbitrary",)),
    )(h0, y, w_out, g_ffn)


def _ffn_loss(v, h1, w_gu, w_down, g_fin, tgt, n_valid):
    t_pad = v.shape[0]

    def body(v_ref, h1_ref, wgu_ref, wd_ref, g_ref, t_ref, gu_ref, act_ref, dh2_ref, dh2b_ref, loss_ref, gfin_ref):
        i = pl.program_id(0)

        @pl.when(i == 0)
        def _():
            loss_ref[...] = jnp.zeros_like(loss_ref)
            gfin_ref[...] = jnp.zeros_like(gfin_ref)

        vb = v_ref[...]
        h2 = h1_ref[...]
        for b in range(4):
            gate = _dot(vb, wgu_ref[b])
            up = _dot(vb, wgu_ref[4 + b])
            gu_ref[b] = gate
            gu_ref[4 + b] = up
            act = ((gate * _sigmoid(gate)) * up).astype(_BF)
            act_ref[b] = act
            h2 = h2 + _dot(act, wd_ref[b])
        n, r = _rms_fwd(h2)
        out = n * g_ref[...]
        row = i * TM + lax.broadcasted_iota(jnp.int32, (TM, 1), 0)
        valid = (row >= N_META) & (row < n_valid)
        err = jnp.where(valid, out - t_ref[...], 0.0)
        loss_ref[...] += (0.5 / D) * jnp.sum(err * err)
        dout = err * (1.0 / D)
        gfin_ref[...] += jnp.sum(dout * n, axis=0, keepdims=True)
        dh2 = _rms_bwd(dout * g_ref[...], n, r)
        dh2_ref[...] = dh2
        dh2b_ref[...] = dh2.astype(_BF)

    return pl.pallas_call(
        body, name="ffn_loss", grid=(t_pad // TM,),
        in_specs=[pl.BlockSpec((TM, D), lambda i: (i, 0)), pl.BlockSpec((TM, D), lambda i: (i, 0)),
                  _const((N_DEV, D, FFB)), _const((4, FFB, D)), _full((1, D)),
                  pl.BlockSpec((TM, D), lambda i: (i, 0))],
        out_specs=[pl.BlockSpec((N_DEV, TM, FFB), lambda i: (0, i, 0)), pl.BlockSpec((4, TM, FFB), lambda i: (0, i, 0)),
                   pl.BlockSpec((TM, D), lambda i: (i, 0)), pl.BlockSpec((TM, D), lambda i: (i, 0)),
                   _full((8, 128)), _full((1, D))],
        out_shape=[_S((N_DEV, t_pad, FFB), _F32), _S((4, t_pad, FFB), _BF), _S((t_pad, D), _F32),
                   _S((t_pad, D), _BF), _S((8, 128), _F32), _S((1, D), _F32)],
        compiler_params=_cp(("arbitrary",)),
    )(v, h1, w_gu, w_down, g_fin, tgt)


def _ffn_bwd(dh2, dh2b, gu, h1, g_ffn, w_gu, w_down, w_out):
    t_pad = dh2.shape[0]

    def body(dh2_ref, dh2b_ref, gu_ref, h1_ref, g_ref, wgu_ref, wd_ref, wo_ref,
             dgu_ref, dh1_ref, dh1b_ref, dy_ref, gffn_ref):
        i = pl.program_id(0)

        @pl.when(i == 0)
        def _():
            gffn_ref[...] = jnp.zeros_like(gffn_ref)

        db = dh2b_ref[...]
        dv = jnp.zeros((TM, D), _F32)
        for b in range(4):
            dact = _dot_nt(db, wd_ref[b])
            gate = gu_ref[b]
            up = gu_ref[4 + b]
            sg = _sigmoid(gate)
            dgate = ((dact * up) * _dsilu(gate, sg)).astype(_BF)
            dup = (dact * (gate * sg)).astype(_BF)
            dgu_ref[b] = dgate
            dgu_ref[4 + b] = dup
            dv = dv + _dot_nt(dgate, wgu_ref[b]) + _dot_nt(dup, wgu_ref[4 + b])
        n, r = _rms_fwd(h1_ref[...])
        gffn_ref[...] += jnp.sum(dv * n, axis=0, keepdims=True)
        dh1 = dh2_ref[...] + _rms_bwd(dv * g_ref[...], n, r)
        dh1_ref[...] = dh1
        dh1b = dh1.astype(_BF)
        dh1b_ref[...] = dh1b
        dy_ref[...] = _dot_nt(dh1b, wo_ref[...])

    tile = pl.BlockSpec((TM, D), lambda i: (i, 0))
    return pl.pallas_call(
        body, name="ffn_bwd", grid=(t_pad // TM,),
        in_specs=[tile, tile, pl.BlockSpec((N_DEV, TM, FFB), lambda i: (0, i, 0)), tile, _full((1, D)),
                  _const((N_DEV, D, FFB)), _const((4, FFB, D)), _const((D, D))],
        out_specs=[pl.BlockSpec((N_DEV, TM, FFB), lambda i: (0, i, 0)), tile, tile, tile, _full((1, D))],
        out_shape=[_S((N_DEV, t_pad, FFB), _BF), _S((t_pad, D), _F32), _S((t_pad, D), _BF),
                   _S((t_pad, D), _F32), _S((1, D), _F32)],
        compiler_params=_cp(("arbitrary",)),
    )(dh2, dh2b, gu, h1, g_ffn, w_gu, w_down, w_out)


def _mixer_bwd(p, hs, o, sc, dy, wr, wi, vec, hb, g_hg, scatter):
    t_pad = p.shape[0]
    nt = t_pad // TM
    nc_t = TM // HC
    nsc = len(scatter)

    def rev(i):
        return nt - 1 - i

    def body(p_ref, pprev_ref, hs_ref, hprev_ref, o_ref, sc_ref, dy_ref, wr_ref, wi_ref, vec_ref, hb_ref, ghg_ref,
             *rest):
        send_refs, rest = rest[:nsc], rest[nsc:]
        dp_ref, gvec_ref, gw_ref = rest[:3]
        recv_refs, rest = rest[3:3 + nsc], rest[3 + nsc:]
        xbuf, hbuf, dbuf, a_s, g_s, ccar, dst = rest[:7]
        qd_s, kd_s, qe_s, ke_s, v_s, do_s, dqd_s, dkd_s, dqe_s, dke_s, dv_s, w_s, dend_s = rest[7:20]
        exchange = _Exchange(send_refs, [], recv_refs, rest[20:])
        i = pl.program_id(0)
        first_tile = i == nt - 1

        @pl.when(i == 0)
        def _():
            exchange.start()
            gvec_ref[...] = jnp.zeros_like(gvec_ref)
            gw_ref[...] = jnp.zeros_like(gw_ref)
            dbuf[pl.ds(TM, 8), :] = jnp.zeros((8, D_RG), _F32)
            ccar[...] = jnp.zeros_like(ccar)
            dst[...] = jnp.zeros_like(dst)

        def acc(row, val):
            gvec_ref[row:row + 1, :] += jnp.sum(val, axis=0, keepdims=True)

        keep = jnp.where(first_tile, 0.0, 1.0)
        x = p_ref[:, pl.ds(0, D_RG)]
        xbuf[pl.ds(0, 8), :] = pprev_ref[...] * keep
        xbuf[pl.ds(8, TM), :] = x
        xc = _conv(xbuf, vec_ref)
        r, ig, a, s, nsp8 = _rg_gates(xc, wr_ref, wi_ref, vec_ref)
        h = hs_ref[...]
        hbuf[pl.ds(0, 8), :] = hprev_ref[...] * keep
        hbuf[pl.ds(8, TM), :] = h
        hm1 = hbuf[pl.ds(7, TM), :]
        gr = p_ref[:, pl.ds(D_RG, D_RG)]
        gel, dgel = _gelu_parts(gr)
        n, rr = _rms_fwd(gel * h)
        dyn = dy_ref[:, pl.ds(0, D_RG)]
        acc(R_GRG, dyn * n)
        dpre = _rms_bwd(dyn * vec_ref[R_GRG:R_GRG + 1, :], n, rr)
        dp_ref[:, pl.ds(D_RG, D_RG)] = ((dpre * h) * dgel).astype(_BF)
        a_s[...] = a
        g_s[...] = dpre * gel

        def step(k, c):
            t = TM - 1 - k
            g = g_s[pl.ds(t, 1), :] + c
            g_s[pl.ds(t, 1), :] = g
            return a_s[pl.ds(t, 1), :] * g

        ccar[pl.ds(0, 1), :] = lax.fori_loop(0, TM, step, ccar[pl.ds(0, 1), :], unroll=8)
        gt = g_s[...]
        da = gt * hm1
        ixc = ig * xc
        ds = gt * ixc
        dig = (gt * s) * xc
        dxc = (gt * s) * ig
        dla = da * a - ds * ((a * a) / s)
        lam = vec_ref[R_LAM:R_LAM + 1, :]
        gvec_ref[R_LAM:R_LAM + 1, :] += jnp.sum(dla * r, axis=0, keepdims=True) * (LRU_C * _sigmoid(-lam))
        dzr = (dla * nsp8) * (r * (1.0 - r))
        dzi = dig * (ig * (1.0 - ig))
        acc(R_BR, dzr)
        acc(R_BI, dzi)
        xcb = xc.astype(_BF)
        dzrb = dzr.astype(_BF)
        dzib = dzi.astype(_BF)
        gw_ref[0] += _dot_tn(xcb, dzrb)
        gw_ref[1] += _dot_tn(xcb, dzib)
        dxc = dxc + _dot_nt(dzrb, wr_ref[...]) + _dot_nt(dzib, wi_ref[...])
        acc(R_CONVB, dxc)
        for j in range(4):
            acc(R_CONVW + j, dxc * xbuf[pl.ds(5 + j, TM), :])
        dbuf[pl.ds(0, TM), :] = dxc
        dx = vec_ref[R_CONVW + 3:R_CONVW + 4, :] * dxc
        for j in range(3):
            dx = dx + vec_ref[R_CONVW + j:R_CONVW + j + 1, :] * dbuf[pl.ds(3 - j, TM), :]
        dbuf[pl.ds(TM, 8), :] = dxc[0:8, :]
        dp_ref[:, pl.ds(0, D_RG)] = dx.astype(_BF)

        lb = _sigmoid(hb_ref[0:1, :] - hb_ref[1:2, :])
        same, tri_blk, triu_blk = _chunk_masks()
        q = _hg_prep(p_ref, lb, tri_blk.astype(_BF))
        qdb, kdb = q["qd"].astype(_BF), q["kd"].astype(_BF)
        qd_s[...] = qdb
        kd_s[...] = kdb
        qe_s[...] = q["qe"].astype(_BF)
        ke_s[...] = q["ke"].astype(_BF)
        v_s[...] = p_ref[:, pl.ds(2 * D_RG + 2 * D_HG, D_HG)].astype(_BF)
        e_end = q["e_end"]
        ghg = ghg_ref[...]
        for h in range(NH):
            cs = pl.ds(HD * h, HD)
            hg = p_ref[:, pl.ds(2 * D_RG + 3 * D_HG + HD * h, HD)]
            sh = _sigmoid(hg)
            n_o, r_o = _rms_fwd(o_ref[:, cs])
            dyh = dy_ref[:, pl.ds(D_RG + HD * h, HD)]
            dp_ref[:, pl.ds(2 * D_RG + 3 * D_HG + HD * h, HD)] = ((dyh * (n_o * ghg)) * _dsilu(hg, sh)).astype(_BF)
            dn = dyh * (hg * sh)
            gvec_ref[R_GHG:R_GHG + 1, pl.ds(0, HD)] += jnp.sum(dn * n_o, axis=0, keepdims=True)
            do_s[:, cs] = _rms_bwd(dn * ghg, n_o, r_o).astype(_BF)
        causal = (lax.broadcasted_iota(jnp.int32, (HC, HC), 0) >= lax.broadcasted_iota(jnp.int32, (HC, HC), 1))
        for c in range(nc_t):
            for h in range(NH):
                rs, cs = pl.ds(HC * c, HC), pl.ds(HD * h, HD)
                qd_c, kd_c, do_c = qd_s[rs, cs], kd_s[rs, cs], do_s[rs, cs]
                amat = jnp.where(causal, _dot_nt(qd_c, kd_c), 0.0).astype(_BF)
                da_m = jnp.where(causal, _dot_nt(do_c, v_s[rs, cs]), 0.0).astype(_BF)
                dqd_s[rs, cs] = _dot(da_m, kd_c)
                dkd_s[rs, cs] = _dot_tn(da_m, qd_c)
                dqe_s[rs, cs] = _dot(do_c, sc_ref[c, h].astype(_BF))
                dv_s[rs, cs] = _dot_tn(amat, do_c)
                w_s[NH * c + h] = _dot_tn(do_c, qe_s[rs, cs])
        for h in range(NH):
            cs = pl.ds(HD * h, HD)
            d_run = dst[h]
            for c in reversed(range(nc_t)):
                rs = pl.ds(HC * c, HC)
                d_b = d_run.astype(_BF)
                dke_s[rs, cs] = _dot(v_s[rs, cs], d_b)
                dp_ref[rs, pl.ds(2 * D_RG + 2 * D_HG + HD * h, HD)] = (
                    dv_s[rs, cs] + _dot_nt(ke_s[rs, cs], d_b)).astype(_BF)
                dend_s[pl.ds(c, 1), cs] = jnp.sum(sc_ref[c, h] * d_run, axis=0, keepdims=True)
                d_run = w_s[NH * c + h] + e_end[HC * c:HC * c + 1, HD * h:HD * (h + 1)] * d_run
            dst[h] = d_run
        dqd, dkd, dqe, dke = dqd_s[...], dkd_s[...], dqe_s[...], dke_s[...]
        dq = dqd * q["e_q"] + dqe * q["e_b"]
        dk = dkd * q["e_k"] + dke * q["e_l"]
        dkeke = dke * q["ke"]
        db = dqd * qdb.astype(_F32) - dkd * kdb.astype(_F32) + dqe * q["qe"] - dkeke
        d_end = jnp.concatenate([jnp.broadcast_to(dend_s[pl.ds(c, 1), :], (HC, D_HG)) for c in range(nc_t)], axis=0)
        dlf = _dot3(triu_blk.astype(_BF), db) + _dot3(same.astype(_BF), dkeke) + d_end * e_end
        df = dlf / q["f"] - dk
        sg = q["sg"]
        gvec_ref[R_HB0:R_HB0 + 1, :] += jnp.sum(df * (1.0 - sg), axis=0, keepdims=True)
        dp_ref[:, pl.ds(2 * D_RG, D_HG)] = (dq * _dsilu(q["hq"], q["sq"])).astype(_BF)
        dp_ref[:, pl.ds(2 * D_RG + D_HG, D_HG)] = ((df * (1.0 - lb)) * (sg * (1.0 - sg))).astype(_BF)

        @pl.when(i == nt - 1)
        def _():
            glb = gvec_ref[R_HB0:R_HB0 + 1, :] * (lb * (1.0 - lb))
            gvec_ref[R_HB0:R_HB0 + 1, :] = glb
            gvec_ref[R_HB1:R_HB1 + 1, :] = -glb
            exchange.finish()

    hbm = pl.BlockSpec(memory_space=pl.ANY)
    return pl.pallas_call(
        body, name="mixer_bwd", grid=(nt,),
        in_specs=[pl.BlockSpec((TM, D_IN), lambda i: (rev(i), 0)),
                  pl.BlockSpec((8, D_RG), lambda i: (jnp.maximum(rev(i) * (TM // 8) - 1, 0), 0)),
                  pl.BlockSpec((TM, D_RG), lambda i: (rev(i), 0)),
                  pl.BlockSpec((8, D_RG), lambda i: (jnp.maximum(rev(i) * (TM // 8) - 1, 0), 0)),
                  pl.BlockSpec((TM, D_HG), lambda i: (rev(i), 0)),
                  pl.BlockSpec((nc_t, NH, HD, HD), lambda i: (rev(i), 0, 0, 0)),
                  pl.BlockSpec((TM, D), lambda i: (rev(i), 0)),
                  _full((D_RG, D_RG)), _full((D_RG, D_RG)), _full((16, D_RG)), _full((2, D_HG)), _full((1, HD))]
        + [hbm] * nsc,
        out_specs=[pl.BlockSpec((TM, D_IN), lambda i: (rev(i), 0)), _full((16, D_RG)), _full((2, D_RG, D_RG))]
        + [hbm] * nsc,
        out_shape=[_S((t_pad, D_IN), _BF), _S((16, D_RG), _F32), _S((2, D_RG, D_RG), _F32)]
        + [_S(s.shape, s.dtype) for s in scatter],
        scratch_shapes=[pltpu.VMEM((TM + 8, D_RG), _F32), pltpu.VMEM((TM + 8, D_RG), _F32),
                        pltpu.VMEM((TM + 8, D_RG), _F32), pltpu.VMEM((TM, D_RG), _F32),
                        pltpu.VMEM((TM, D_RG), _F32), pltpu.VMEM((8, D_RG), _F32),
                        pltpu.VMEM((NH, HD, HD), _F32)]
        + [pltpu.VMEM((TM, D_HG), _BF) for _ in range(6)] + [pltpu.VMEM((TM, D_HG), _F32) for _ in range(5)]
        + [pltpu.VMEM((nc_t * NH, HD, HD), _F32), pltpu.VMEM((8, D_HG), _F32)] + _sem_shapes(nsc),
        compiler_params=_cp(("arbitrary",)),
    )(p, p, hs, hs, o, sc, dy, wr, wi, vec, hb, g_hg, *scatter)


def _inproj_bwd(dp, w_in, h0, dh1, g_mix, to_all):
    t_pad = dp.shape[0]
    nt = t_pad // TM
    na = len(to_all)

    def body(dp_ref, w_ref, h_ref, dh1_ref, g_ref, *rest):
        dh0_ref, gmix_ref = rest[na:na + 2]
        exchange = _Exchange([], rest[:na], rest[na + 2:2 * na + 2], rest[2 * na + 2:])
        i = pl.program_id(0)

        @pl.when(i == 0)
        def _():
            exchange.start()
            gmix_ref[...] = jnp.zeros_like(gmix_ref)

        du = jnp.zeros((TM, D), _F32)
        for j in range(N_DEV):
            du = du + _dot_nt(dp_ref[:, WIN_B * j:WIN_B * (j + 1)], w_ref[j])
        n, r = _rms_fwd(h_ref[...])
        gmix_ref[...] += jnp.sum(du * n, axis=0, keepdims=True)
        dh0_ref[...] = dh1_ref[...] + _rms_bwd(du * g_ref[...], n, r)

        @pl.when(i == nt - 1)
        def _():
            exchange.finish()

    tile = pl.BlockSpec((TM, D), lambda i: (i, 0))
    hbm = pl.BlockSpec(memory_space=pl.ANY)
    return pl.pallas_call(
        body, name="inproj_bwd", grid=(nt,),
        in_specs=[pl.BlockSpec((TM, D_IN), lambda i: (i, 0)), _const((N_DEV, D, WIN_B)), tile, tile, _full((1, D))]
        + [hbm] * na,
        out_specs=[tile, _full((1, D))] + [hbm] * na,
        out_shape=[_S((t_pad, D), _F32), _S((1, D), _F32)] + [_S((N_DEV,) + g.shape, g.dtype) for g in to_all],
        scratch_shapes=_sem_shapes(na),
        compiler_params=_cp(("arbitrary",)),
    )(dp, w_in, h0, dh1, g_mix, *to_all)


def _wgrad_send(u, dp, order, to_all):
    t_pad = u.shape[0]
    na = len(to_all)

    def body(order_ref, u_ref, dp_ref, *rest):
        all_in = rest[:na]
        recv_ref = rest[na]
        all_out = rest[na + 1:2 * na + 1]
        buf, blk_send, blk_recv, blk_local = rest[2 * na + 1:2 * na + 5]
        exchange = _Exchange([], all_in, all_out, rest[2 * na + 5:])
        s = pl.program_id(0)
        x, y, c = _coords()
        me = 4 * x + 2 * y + c
        slot = s % 2

        def send(step):
            r = _SEND_ORDER[step]
            return pltpu.make_async_remote_copy(
                src_ref=buf.at[step % 2], dst_ref=recv_ref.at[me], send_sem=blk_send.at[step], recv_sem=blk_recv.at[r - 1],
                device_id=(x ^ (r >> 2), y ^ ((r >> 1) & 1), c ^ (r & 1)), device_id_type=_MESH)

        @pl.when(s == 0)
        def _():
            exchange.start()

        for step in range(2, N_DEV):
            @pl.when(s == step)
            def _(step=step):
                send(step - 2).wait_send()

        buf[slot] = _dot_tn(u_ref[...], dp_ref[...]).astype(_BF)

        for step in range(N_DEV - 1):
            @pl.when(s == step)
            def _(step=step):
                send(step).start()

        @pl.when(s == N_DEV - 1)
        def _():
            mine = pltpu.make_async_copy(buf.at[(N_DEV - 1) % 2], recv_ref.at[me], blk_local.at[0])
            mine.start()
            send(N_DEV - 2).wait_send()
            for r in range(1, N_DEV):
                px, py, pc = x ^ (r >> 2), y ^ ((r >> 1) & 1), c ^ (r & 1)
                pltpu.make_async_remote_copy(
                    src_ref=buf.at[0], dst_ref=recv_ref.at[4 * px + 2 * py + pc], send_sem=blk_send.at[0],
                    recv_sem=blk_recv.at[r - 1], device_id=(px, py, pc), device_id_type=_MESH).wait_recv()
            mine.wait()
            exchange.finish()

    hbm = pl.BlockSpec(memory_space=pl.ANY)
    return pl.pallas_call(
        body, name="wgrad_in_send",
        grid_spec=pltpu.PrefetchScalarGridSpec(
            num_scalar_prefetch=1, grid=(N_DEV,),
            in_specs=[pl.BlockSpec((t_pad, D), lambda s, order: (0, 0)),
                      pl.BlockSpec((t_pad, WIN_B), lambda s, order: (0, order[s]))] + [hbm] * na,
            out_specs=[hbm] * (na + 1),
            scratch_shapes=[pltpu.VMEM((2, D, WIN_B), _BF), pltpu.SemaphoreType.DMA((N_DEV - 1,)),
                            pltpu.SemaphoreType.DMA((N_DEV - 1,)), pltpu.SemaphoreType.DMA((1,))] + _sem_shapes(na)),
        out_shape=[_S((N_DEV, D, WIN_B), _BF)] + [_S((N_DEV,) + g.shape, g.dtype) for g in to_all],
        compiler_params=_cp(("arbitrary",)),
    )(order, u, dp, *to_all)


def _wgrad(name, a, b, a_spec, b_spec, n_blocks, out_block):
    def body(a_ref, b_ref, o_ref):
        av = a_ref[0] if len(a_ref.shape) == 3 else a_ref[...]
        bv = b_ref[0] if len(b_ref.shape) == 3 else b_ref[...]
        o_ref[0] = _dot_tn(av, bv).astype(_BF)

    return pl.pallas_call(
        body, name=name, grid=(n_blocks,),
        in_specs=[a_spec, b_spec],
        out_specs=pl.BlockSpec((1,) + out_block, lambda j: (j, 0, 0)),
        out_shape=_S((n_blocks,) + out_block, _BF),
        compiler_params=_cp(("arbitrary",)),
    )(a, b)


def _coords():
    return lax.axis_index("x"), lax.axis_index("y"), lax.axis_index("c")


def _sem_shapes(na):
    return [pltpu.SemaphoreType.DMA((7 * na,)), pltpu.SemaphoreType.DMA((7 * na,)), pltpu.SemaphoreType.DMA((na,))]


class _Gather:
    def __init__(self, srcs, outs, sems):
        self.srcs, self.outs = srcs, outs
        self.send_sems, self.recv_sems, self.local_sems = sems
        self.na = len(srcs)
        x, y, c = _coords()
        self.pos = (x, y, c)
        self.me = 4 * x + 2 * y + c
        self.sibling = (x, y, 1 - c)
        self.chips = [(1 - x, y), (x, 1 - y), (1 - x, 1 - y)]

    @staticmethod
    def _slot(px, py, pc):
        return 4 * px + 2 * py + pc

    def _copy(self, a, k, block, to, own=False):
        return pltpu.make_async_remote_copy(
            src_ref=self.srcs[a] if own else self.outs[a].at[block], dst_ref=self.outs[a].at[block],
            send_sem=self.send_sems.at[7 * a + k], recv_sem=self.recv_sems.at[7 * a + k],
            device_id=to, device_id_type=_MESH)

    def _mine(self, a):
        return pltpu.make_async_copy(self.srcs[a], self.outs[a].at[self.me], self.local_sems.at[a])

    def _first(self):
        c = self.pos[2]
        cps = []
        for a in range(self.na):
            cps.append(self._copy(a, 0, self.me, self.sibling, own=True))
            cps += [self._copy(a, 1 + j, self.me, (*chip, c), own=True) for j, chip in enumerate(self.chips)]
        return cps

    def _passed(self):
        c = self.pos[2]
        return [self._copy(a, 4 + j, self._slot(*chip, c), self.sibling)
                for j, chip in enumerate(self.chips) for a in range(self.na)]

    def start(self):
        for a in range(self.na):
            self._mine(a).start()
        for cp in self._first():
            cp.start()

    def forward(self, j):
        c = self.pos[2]
        chip = self.chips[j]
        for a in range(self.na):
            self._copy(a, 1 + j, self._slot(*chip, c), self.pos).wait_recv()
            self._copy(a, 4 + j, self._slot(*chip, c), self.sibling).start()

    def finish(self):
        x, y, c = self.pos
        for a in range(self.na):
            self._copy(a, 0, self._slot(x, y, 1 - c), self.pos).wait_recv()
        for j, chip in enumerate(self.chips):
            for a in range(self.na):
                self._copy(a, 4 + j, self._slot(*chip, 1 - c), self.pos).wait_recv()
        for cp in self._first() + self._passed():
            cp.wait_send()
        for a in range(self.na):
            self._mine(a).wait()


class _Exchange:
    def __init__(self, scatter, gather, outs, sems):
        self.ins = list(scatter) + list(gather)
        self.ns, self.na = len(scatter), len(scatter) + len(gather)
        self.outs = outs
        self.send_sems, self.recv_sems, self.local_sems = sems
        x, y, c = _coords()
        self.pos = (x, y, c)
        self.me = 4 * x + 2 * y + c

    def _peer(self, r):
        x, y, c = self.pos
        return x ^ (r >> 2), y ^ ((r >> 1) & 1), c ^ (r & 1)

    def _src(self, a, block):
        return self.ins[a].at[block] if a < self.ns else self.ins[a]

    def _local(self, a):
        return pltpu.make_async_copy(self._src(a, self.me), self.outs[a].at[self.me], self.local_sems.at[a])

    def _send(self, a, r):
        px, py, pc = self._peer(r)
        return pltpu.make_async_remote_copy(
            src_ref=self._src(a, 4 * px + 2 * py + pc), dst_ref=self.outs[a].at[self.me],
            send_sem=self.send_sems.at[7 * a + r - 1], recv_sem=self.recv_sems.at[7 * a + r - 1],
            device_id=(px, py, pc), device_id_type=_MESH)

    def _recv(self, a, r):
        px, py, pc = self._peer(r)
        return pltpu.make_async_remote_copy(
            src_ref=self._src(a, self.me), dst_ref=self.outs[a].at[4 * px + 2 * py + pc],
            send_sem=self.send_sems.at[7 * a + r - 1], recv_sem=self.recv_sems.at[7 * a + r - 1],
            device_id=(px, py, pc), device_id_type=_MESH)

    def start(self):
        for a in range(self.na):
            self._local(a).start()
        for r in range(1, N_DEV):
            for a in range(self.na):
                self._send(a, r).start()

    def finish(self):
        for r in range(1, N_DEV):
            for a in range(self.na):
                self._recv(a, r).wait_recv()
        for r in range(1, N_DEV):
            for a in range(self.na):
                self._send(a, r).wait_send()
        for a in range(self.na):
            self._local(a).wait()


def _allgather_first(gather_f32, cast_f32, gather_dtypes):
    ng, nc = len(gather_f32), len(cast_f32)

    def body(*refs):
        ins, cins = refs[:ng], refs[ng:ng + nc]
        outs, couts = refs[ng + nc:2 * ng + nc], refs[2 * ng + nc:2 * ng + 2 * nc]
        stage = refs[2 * ng + 2 * nc:3 * ng + 2 * nc]
        sems = refs[3 * ng + 2 * nc:]
        for a in range(ng):
            stage[a][...] = ins[a][...].astype(gather_dtypes[a])
        g = _Gather(stage, outs, sems)
        g.start()
        for a in range(nc):
            couts[a][...] = cins[a][...].astype(_BF)
        for j in range(3):
            g.forward(j)
        g.finish()

    vm = pl.BlockSpec(memory_space=pltpu.VMEM)
    return pl.pallas_call(
        body, name="allgather_first",
        in_specs=[vm] * (ng + nc),
        out_specs=[pl.BlockSpec(memory_space=pl.ANY)] * ng + [vm] * nc,
        out_shape=[_S((N_DEV,) + l.shape, dt) for l, dt in zip(gather_f32, gather_dtypes)]
        + [_S(l.shape, _BF) for l in cast_f32],
        scratch_shapes=[pltpu.VMEM(l.shape, dt) for l, dt in zip(gather_f32, gather_dtypes)] + _sem_shapes(ng),
        compiler_params=pltpu.CompilerParams(vmem_limit_bytes=VMEM_LIMIT),
    )(*gather_f32, *cast_f32)


def _adamw_math(w, g, m, v):
    m2 = ADAM_B1 * m + (1.0 - ADAM_B1) * g
    v2 = ADAM_B2 * v + (1.0 - ADAM_B2) * (g * g)
    m_hat = m2 / (1.0 - ADAM_B1 ** ADAM_STEP)
    v_hat = v2 / (1.0 - ADAM_B2 ** ADAM_STEP)
    delta = -ADAM_LR * (m_hat / (jnp.sqrt(v_hat) + ADAM_EPS) + ADAM_WD * w)
    return delta, m2, v2


def _adamw_big(name, recv, w, m, v, rows):
    r_all, c_all = w.shape

    def body(r_ref, w_ref, m_ref, v_ref, g_out, d_out, m_out, v_out):
        g = r_ref[0].astype(_F32)
        for k in range(1, N_DEV):
            g = g + r_ref[k].astype(_F32)
        delta, m2, v2 = _adamw_math(w_ref[...], g, m_ref[...], v_ref[...])
        g_out[...] = g
        d_out[...] = delta
        m_out[...] = m2
        v_out[...] = v2

    tile = pl.BlockSpec((rows, c_all), lambda i: (i, 0))
    return pl.pallas_call(
        body, name=name, grid=(r_all // rows,),
        in_specs=[pl.BlockSpec((N_DEV, rows, c_all), lambda i: (0, i, 0)), tile, tile, tile],
        out_specs=[tile] * 4,
        out_shape=[_S(w.shape, _F32)] * 4,
        compiler_params=_cp(("arbitrary",)),
    )(recv, w, m, v)


def _adamw_small(gathered, slices, wmv):
    ng, npar = len(gathered), len(slices)

    def body(*refs):
        g_refs = refs[:ng]
        wmv_refs = refs[ng:ng + 3 * npar]
        outs = refs[ng + 3 * npar:]
        for i, (ai, r0, nr, ncol) in enumerate(slices):
            g = g_refs[ai][0, pl.ds(r0, nr), pl.ds(0, ncol)]
            for k in range(1, N_DEV):
                g = g + g_refs[ai][k, pl.ds(r0, nr), pl.ds(0, ncol)]
            w_ref, m_ref, v_ref = wmv_refs[3 * i:3 * i + 3]
            delta, m2, v2 = _adamw_math(w_ref[...], g, m_ref[...], v_ref[...])
            outs[4 * i][...] = g
            outs[4 * i + 1][...] = delta
            outs[4 * i + 2][...] = m2
            outs[4 * i + 3][...] = v2

    flat = [t for trip in wmv for t in trip]
    out_shape = []
    for w, _, _ in wmv:
        out_shape += [_S(w.shape, _F32)] * 4
    return pl.pallas_call(
        body, name="adamw_small", out_shape=out_shape,
        compiler_params=pltpu.CompilerParams(vmem_limit_bytes=VMEM_LIMIT),
    )(*gathered, *flat)


def _block_diag(w):
    eye = jnp.eye(8, dtype=w.dtype)
    return (w[:, :, None, :] * eye[:, None, :, None]).reshape(D_RG, D_RG)


def _diag_blocks(g):
    return jnp.concatenate([g[64 * h:64 * (h + 1), 64 * h:64 * (h + 1)] for h in range(8)], axis=0)


def _local_step(x, tgt, meta, g_mix, w_in, vec, wr, wi, hb, g_hg, w_out_l, g_ffn, w_gu_l, w_down_l, g_fin):
    seq = x.shape[0]
    n_valid = N_META + seq
    t_pad = -(-n_valid // TM) * TM
    h0 = jnp.concatenate([meta, x, jnp.zeros((t_pad - n_valid, D), _F32)], axis=0)
    tgt_p = jnp.concatenate([jnp.zeros((N_META, D), _F32), tgt, jnp.zeros((t_pad - n_valid, D), _F32)], axis=0)

    p, u = _inproj(h0, g_mix, w_in)
    y, hs, o, sc, w_gu, w_out, w_down = _mixer_fwd(p, wr, wi, vec, hb, g_hg, [w_gu_l, w_out_l, w_down_l])
    w_out = w_out.reshape(D, D)
    w_down = w_down.reshape(4, FFB, D)
    h1, v = _outproj(h0, y, w_out, g_ffn)
    gu, act, dh2, dh2b, loss, gfin = _ffn_loss(v, h1, w_gu, w_down, g_fin, tgt_p, n_valid)

    dgu, dh1, dh1b, dy, gffn = _ffn_bwd(dh2, dh2b, gu, h1, g_ffn, w_gu, w_down, w_out)
    g_wdown = _wgrad("wgrad_down", act, dh2b, pl.BlockSpec((1, t_pad, FFB), lambda j: (j, 0, 0)),
                     pl.BlockSpec((t_pad, D), lambda j: (0, 0)), 4, (FFB, D))
    g_wgu = _wgrad("wgrad_gate_up", v, dgu, pl.BlockSpec((t_pad, D), lambda j: (0, 0)),
                   pl.BlockSpec((1, t_pad, FFB), lambda j: (j, 0, 0)), N_DEV, (D, FFB))
    g_wout = _wgrad("wgrad_out", y, dh1b, pl.BlockSpec((t_pad, D // N_DEV), lambda j: (0, j)),
                    pl.BlockSpec((t_pad, D), lambda j: (0, 0)), N_DEV, (D // N_DEV, D))
    dp, gvec, gw, r_wgu, r_wout, r_wdown = _mixer_bwd(
        p, hs, o, sc, dy, wr, wi, vec, hb, g_hg, [g_wgu, g_wout, g_wdown.reshape(N_DEV, D_FF // N_DEV, D)])
    pack_c = jnp.concatenate([_diag_blocks(gw[0]), _diag_blocks(gw[1])], axis=0)
    dh0, gmix, all_b, all_c = _inproj_bwd(dp, w_in, h0, dh1, g_mix, [gvec, pack_c])
    pack_a = jnp.concatenate([gmix, gffn, gfin, jnp.zeros((5, D), _F32), dh0[:N_META]], axis=0)
    me = 4 * lax.axis_index("x") + 2 * lax.axis_index("y") + lax.axis_index("c")
    order = (me ^ jnp.array(_SEND_ORDER, jnp.int32)).astype(jnp.int32)
    r_win, all_a = _wgrad_send(u, dp, order, [pack_a])
    return loss, dh0, (r_win, r_wgu, r_wout, r_wdown), (all_a, all_b, all_c)


def kernel(x, meta_tokens, mix_norm_g, w_in, conv_w, conv_b, w_rgate, b_rgate, w_igate, b_igate, lru_lambda, rg_norm_g, hg_lower_bound, hg_norm_g, w_out, ffn_norm_g, w_gate_up, w_down, final_norm_g, loss_target, m_meta_tokens, m_mix_norm_g, m_w_in, m_conv_w, m_conv_b, m_w_rgate, m_b_rgate, m_w_igate, m_b_igate, m_lru_lambda, m_rg_norm_g, m_hg_lower_bound, m_hg_norm_g, m_w_out, m_ffn_norm_g, m_w_gate_up, m_w_down, m_final_norm_g, v_meta_tokens, v_mix_norm_g, v_w_in, v_conv_w, v_conv_b, v_w_rgate, v_b_rgate, v_w_igate, v_b_igate, v_lru_lambda, v_rg_norm_g, v_hg_lower_bound, v_hg_norm_g, v_w_out, v_ffn_norm_g, v_w_gate_up, v_w_down, v_final_norm_g):
    seq = x.shape[1]
    me = 4 * lax.axis_index("x") + 2 * lax.axis_index("y") + lax.axis_index("c")

    small_l = jnp.concatenate([meta_tokens, jnp.pad(conv_w[0], ((0, 4), (0, 64)))], axis=0)
    w_in_g, small_g, w_gu_l, w_out_l, w_down_l = _allgather_first(
        [w_in[0], small_l], [w_gate_up[0], w_out[0], w_down[0]], [_BF, _F32])
    meta_full = jnp.transpose(small_g[:, :N_META, :], (1, 0, 2)).reshape(N_META, D)
    conv_w_full = jnp.transpose(small_g[:, N_META:N_META + 4, :64], (1, 0, 2)).reshape(4, D_RG)
    vec = jnp.concatenate([conv_b, b_rgate, b_igate, lru_lambda, rg_norm_g, jnp.zeros((3, D_RG), _F32),
                           conv_w_full, jnp.zeros((4, D_RG), _F32)], axis=0)
    wr = _block_diag(w_rgate[0]).astype(_BF)
    wi = _block_diag(w_igate[0]).astype(_BF)

    loss, dh0, (r_win, r_wgu, r_wout, r_wdown), (all_a, all_b, all_c) = _local_step(
        x[0], loss_target[0], meta_full, mix_norm_g, w_in_g, vec, wr, wi, hg_lower_bound, hg_norm_g,
        w_out_l, ffn_norm_g, w_gu_l, w_down_l, final_norm_g.reshape(1, D))
    grad_x = dh0[N_META:N_META + seq][None]

    outs = {}
    outs["w_in"] = _adamw_big("adamw_w_in", r_win, w_in[0], m_w_in[0], v_w_in[0], 256)
    outs["w_gate_up"] = _adamw_big("adamw_w_gate_up", r_wgu, w_gate_up[0], m_w_gate_up[0], v_w_gate_up[0], 256)
    outs["w_out"] = _adamw_big("adamw_w_out", r_wout, w_out[0], m_w_out[0], v_w_out[0], 128)
    outs["w_down"] = _adamw_big("adamw_w_down", r_wdown, w_down[0], m_w_down[0], v_w_down[0], 176)

    meta_part = lax.dynamic_slice_in_dim(all_a[:, R_META:R_META + N_META, :], me * 128, 128, axis=2)
    convw_part = lax.dynamic_slice_in_dim(all_b[:, R_CONVW:R_CONVW + 4, :], me * 64, 64, axis=2)
    gathered = [all_a, all_b, all_c, meta_part, convw_part]
    small_params = [
        ("meta_tokens", (3, 0, N_META, 128), (meta_tokens, m_meta_tokens, v_meta_tokens), (N_META, 128)),
        ("mix_norm_g", (0, R_GMIX, 1, D), (mix_norm_g, m_mix_norm_g, v_mix_norm_g), (1, D)),
        ("conv_w", (4, 0, 4, 64), (conv_w, m_conv_w, v_conv_w), (4, 64)),
        ("conv_b", (1, R_CONVB, 1, D_RG), (conv_b, m_conv_b, v_conv_b), (1, D_RG)),
        ("w_rgate", (2, 0, 512, 64), (w_rgate, m_w_rgate, v_w_rgate), (512, 64)),
        ("b_rgate", (1, R_BR, 1, D_RG), (b_rgate, m_b_rgate, v_b_rgate), (1, D_RG)),
        ("w_igate", (2, 512, 512, 64), (w_igate, m_w_igate, v_w_igate), (512, 64)),
        ("b_igate", (1, R_BI, 1, D_RG), (b_igate, m_b_igate, v_b_igate), (1, D_RG)),
        ("lru_lambda", (1, R_LAM, 1, D_RG), (lru_lambda, m_lru_lambda, v_lru_lambda), (1, D_RG)),
        ("rg_norm_g", (1, R_GRG, 1, D_RG), (rg_norm_g, m_rg_norm_g, v_rg_norm_g), (1, D_RG)),
        ("hg_lower_bound", (1, R_HB0, 2, D_HG), (hg_lower_bound, m_hg_lower_bound, v_hg_lower_bound), (2, D_HG)),
        ("hg_norm_g", (1, R_GHG, 1, HD), (hg_norm_g, m_hg_norm_g, v_hg_norm_g), (1, HD)),
        ("ffn_norm_g", (0, R_GFFN, 1, D), (ffn_norm_g, m_ffn_norm_g, v_ffn_norm_g), (1, D)),
        ("final_norm_g", (0, R_GFIN, 1, D), (final_norm_g, m_final_norm_g, v_final_norm_g), (1, D)),
    ]
    res = _adamw_small(gathered, [s[1] for s in small_params],
                       [tuple(t.reshape(s[3]) for t in s[2]) for s in small_params])
    for i, s in enumerate(small_params):
        outs[s[0]] = [r.reshape(s[2][0].shape) for r in res[4 * i:4 * i + 4]]
    for n, ref in (("w_in", w_in), ("w_gate_up", w_gate_up), ("w_out", w_out), ("w_down", w_down)):
        outs[n] = [r.reshape(ref.shape) for r in outs[n]]

    loss_all = lax.psum(loss[0, 0], ("x", "y", "c"))
    order = ["meta_tokens", "mix_norm_g", "w_in", "conv_w", "conv_b", "w_rgate", "b_rgate", "w_igate", "b_igate",
             "lru_lambda", "rg_norm_g", "hg_lower_bound", "hg_norm_g", "w_out", "ffn_norm_g", "w_gate_up", "w_down",
             "final_norm_g"]
    return (loss_all, grad_x, *[outs[n][0] for n in order], *[outs[n][1] for n in order],
            *[outs[n][2] for n in order], *[outs[n][3] for n in order])
```

```python
import functools

import jax
import jax.numpy as jnp
from jax import lax
from jax.experimental import pallas as pl
from jax.experimental.pallas import tpu as pltpu

_BF = jnp.bfloat16
_F32 = jnp.float32
_S = jax.ShapeDtypeStruct
_MESH = pl.DeviceIdType.MESH

N_DEV = 8
N_META = 16
D = 1024
D_RG = 512
D_HG = 512
HD = 128
NH = D_HG // HD
D_IN = 3072
D_FF = 2816
FFB = D_FF // 4
WIN_B = D_IN // N_DEV
EPS = 1e-6
LRU_C = 8.0
TM = 256
HC = 64
VMEM_LIMIT = 56 * 1024 * 1024

ADAM_LR = 0.001
ADAM_B1 = 0.9
ADAM_B2 = 0.999
ADAM_EPS = 1e-08
ADAM_WD = 0.01
ADAM_STEP = 10

_SEND_ORDER = (6, 7, 4, 5, 2, 3, 1, 0)

R_CONVB, R_BR, R_BI, R_LAM, R_GRG, R_HB0, R_HB1, R_GHG, R_CONVW = 0, 1, 2, 3, 4, 5, 6, 7, 8
R_GMIX, R_GFFN, R_GFIN, R_META = 0, 1, 2, 8


def _cp(sem=None, **kw):
    return pltpu.CompilerParams(dimension_semantics=sem, vmem_limit_bytes=VMEM_LIMIT, **kw)


def _dot(a, b):
    return jnp.dot(a, b, preferred_element_type=_F32)


def _dot_nt(a, b):
    return lax.dot_general(a, b, (((1,), (1,)), ((), ())), preferred_element_type=_F32)


def _dot_tn(a, b):
    return lax.dot_general(a, b, (((0,), (0,)), ((), ())), preferred_element_type=_F32)


def _sigmoid(x):
    return jax.nn.sigmoid(x)


def _dsilu(x, s):
    return s * (1.0 + x * (1.0 - s))


_GELU_C = 0.7978845608028654


def _gelu_parts(x):
    t = jnp.tanh(_GELU_C * (x + 0.044715 * (x * x * x)))
    g = 0.5 * x * (1.0 + t)
    dg = 0.5 * (1.0 + t) + 0.5 * x * (1.0 - t * t) * (_GELU_C * (1.0 + 3.0 * 0.044715 * (x * x)))
    return g, dg


def _softplus(z):
    e = jnp.exp(-jnp.abs(z))
    w = 1.0 + e
    l1p = jnp.where(w == 1.0, e, jnp.log(w) * e / jnp.where(w == 1.0, 1.0, w - 1.0))
    return jnp.maximum(z, 0.0) + l1p


def _rms_fwd(x):
    r = lax.rsqrt(jnp.mean(x * x, axis=-1, keepdims=True) + EPS)
    return x * r, r


def _rms_bwd(dyg, n, r):
    return r * (dyg - n * jnp.mean(dyg * n, axis=-1, keepdims=True))


def _full(shape):
    nd = len(shape)
    return pl.BlockSpec(shape, lambda i: (0,) * nd)


def _const(shape):
    nd = len(shape)
    return pl.BlockSpec(shape, lambda i: (0,) * nd, pipeline_mode=pl.Buffered(1))


def _carry_gather(gather, i, nt):
    @pl.when(i == 0)
    def _():
        gather.start()

    def tail():
        for j in range(3):
            @pl.when(i == max(nt - 3 + j, 0))
            def _(j=j):
                gather.forward(j)

        @pl.when(i == nt - 1)
        def _():
            gather.finish()

    return tail


def _inproj(h0, g_mix, w_in, shards):
    t_pad = h0.shape[0]
    nt = t_pad // TM
    nsh = len(shards)

    def body(h_ref, g_ref, w_ref, *rest):
        p_ref, u_ref = rest[nsh:nsh + 2]
        tail = _carry_gather(_Gather(rest[:nsh], rest[nsh + 2:2 * nsh + 2], rest[2 * nsh + 2:]), pl.program_id(0), nt)
        n, _ = _rms_fwd(h_ref[...])
        u = (n * g_ref[...]).astype(_BF)
        u_ref[...] = u
        for j in range(N_DEV):
            p_ref[:, WIN_B * j:WIN_B * (j + 1)] = _dot(u, w_ref[j])
        tail()

    hbm = pl.BlockSpec(memory_space=pl.ANY)
    return pl.pallas_call(
        body, name="inproj", grid=(nt,),
        in_specs=[pl.BlockSpec((TM, D), lambda i: (i, 0)), _full((1, D)), _const((N_DEV, D, WIN_B))] + [hbm] * nsh,
        out_specs=[pl.BlockSpec((TM, D_IN), lambda i: (i, 0)), pl.BlockSpec((TM, D), lambda i: (i, 0))] + [hbm] * nsh,
        out_shape=[_S((t_pad, D_IN), _F32), _S((t_pad, D), _BF)] + [_S((N_DEV,) + s.shape, s.dtype) for s in shards],
        scratch_shapes=_sem_shapes(nsh),
        compiler_params=_cp(("arbitrary",)),
    )(h0, g_mix, w_in, *shards)


def _rg_gates(xc, wr_ref, wi_ref, vec_ref):
    xcb = xc.astype(_BF)
    r = _sigmoid(_dot(xcb, wr_ref[...]) + vec_ref[R_BR:R_BR + 1, :])
    ig = _sigmoid(_dot(xcb, wi_ref[...]) + vec_ref[R_BI:R_BI + 1, :])
    nsp8 = -LRU_C * _softplus(-vec_ref[R_LAM:R_LAM + 1, :])
    la = nsp8 * r
    a = jnp.exp(la)
    th = jnp.tanh(la)
    s = jnp.sqrt(-2.0 * th / (1.0 - th))
    return r, ig, a, s, nsp8


def _conv(xbuf, vec_ref):
    acc = vec_ref[R_CONVW:R_CONVW + 1, :] * xbuf[pl.ds(5, TM), :]
    for j in range(1, 4):
        acc = acc + vec_ref[R_CONVW + j:R_CONVW + j + 1, :] * xbuf[pl.ds(5 + j, TM), :]
    return vec_ref[R_CONVB:R_CONVB + 1, :] + acc


def _dot3(m01, x):
    hi = x.astype(_BF)
    r1 = x - hi.astype(_F32)
    mid = r1.astype(_BF)
    lo = (r1 - mid.astype(_F32)).astype(_BF)
    return (_dot(m01, lo) + _dot(m01, mid)) + _dot(m01, hi)


def _chunk_masks():
    row = lax.broadcasted_iota(jnp.int32, (TM, TM), 0)
    col = lax.broadcasted_iota(jnp.int32, (TM, TM), 1)
    shift = HC.bit_length() - 1
    same = lax.shift_right_logical(row, shift) == lax.shift_right_logical(col, shift)
    return same, same & (row >= col), same & (col >= row)


def _per_chunk_rows(x, r):
    return jnp.concatenate([jnp.broadcast_to(x[HC * c + r:HC * c + r + 1, :], (HC, x.shape[1]))
                            for c in range(TM // HC)], axis=0)


def _hg_prep(p_ref, lb, tri_blk):
    hq = p_ref[:, pl.ds(2 * D_RG, D_HG)]
    hf = p_ref[:, pl.ds(2 * D_RG + D_HG, D_HG)]
    sq = _sigmoid(hq)
    q = hq * sq
    sg = _sigmoid(hf)
    f = lb + (1.0 - lb) * sg
    k = 1.0 - f
    b = _dot3(tri_blk, jnp.log(f))
    bm = _per_chunk_rows(b, HC // 2 - 1)
    bl = _per_chunk_rows(b, HC - 1)
    e_q = jnp.exp(b - bm)
    e_k = jnp.exp(bm - b)
    e_b = jnp.exp(b)
    e_l = jnp.exp(bl - b)
    return dict(hq=hq, sq=sq, q=q, sg=sg, f=f, k=k, e_q=e_q, e_k=e_k, e_b=e_b, e_l=e_l,
                qd=q * e_q, kd=k * e_k, qe=q * e_b, ke=k * e_l, e_end=jnp.exp(bl))


def _mixer_fwd(p, wr, wi, vec, hb, g_hg, shards):
    t_pad = p.shape[0]
    nt = t_pad // TM
    nc_t = TM // HC
    nsh = len(shards)

    def body(p_ref, wr_ref, wi_ref, vec_ref, hb_ref, ghg_ref, *rest):
        sh_refs, rest = rest[:nsh], rest[nsh:]
        y_ref, hs_ref, o_ref, sc_ref = rest[:4]
        gath_refs, rest = rest[4:4 + nsh], rest[4 + nsh:]
        xbuf, a_s, b_s, hcar, st, qd_s, kd_s, qe_s, ke_s, v_s, u_s = rest[:11]
        i = pl.program_id(0)
        tail = _carry_gather(_Gather(sh_refs, gath_refs, rest[11:]), i, nt)

        @pl.when(i == 0)
        def _():
            xbuf[pl.ds(0, 8), :] = jnp.zeros((8, D_RG), _F32)
            hcar[...] = jnp.zeros_like(hcar)
            st[...] = jnp.zeros_like(st)

        x = p_ref[:, pl.ds(0, D_RG)]
        xbuf[pl.ds(8, TM), :] = x
        xc = _conv(xbuf, vec_ref)
        xbuf[pl.ds(0, 8), :] = x[TM - 8:, :]
        r, ig, a, s, _ = _rg_gates(xc, wr_ref, wi_ref, vec_ref)
        a_s[...] = a
        b_s[...] = s * (ig * xc)

        def step(t, h):
            h = a_s[pl.ds(t, 1), :] * h + b_s[pl.ds(t, 1), :]
            hs_ref[pl.ds(t, 1), :] = h
            return h

        hcar[pl.ds(0, 1), :] = lax.fori_loop(0, TM, step, hcar[pl.ds(0, 1), :], unroll=8)
        gel, _ = _gelu_parts(p_ref[:, pl.ds(D_RG, D_RG)])
        n, _ = _rms_fwd(gel * hs_ref[...])
        y_ref[:, pl.ds(0, D_RG)] = (n * vec_ref[R_GRG:R_GRG + 1, :]).astype(_BF)

        lb = _sigmoid(hb_ref[0:1, :] - hb_ref[1:2, :])
        _, tri_blk, _ = _chunk_masks()
        q = _hg_prep(p_ref, lb, tri_blk.astype(_BF))
        for name, ref in (("qd", qd_s), ("kd", kd_s), ("qe", qe_s), ("ke", ke_s)):
            ref[...] = q[name].astype(_BF)
        v_s[...] = p_ref[:, pl.ds(2 * D_RG + 2 * D_HG, D_HG)].astype(_BF)
        e_end = q["e_end"]
        causal = (lax.broadcasted_iota(jnp.int32, (HC, HC), 0) >= lax.broadcasted_iota(jnp.int32, (HC, HC), 1))
        for c in range(nc_t):
            for h in range(NH):
                rs, cs = pl.ds(HC * c, HC), pl.ds(HD * h, HD)
                amat = jnp.where(causal, _dot_nt(qd_s[rs, cs], kd_s[rs, cs]), 0.0)
                o_ref[rs, cs] = _dot(amat.astype(_BF), v_s[rs, cs])
                u_s[NH * c + h] = _dot_tn(v_s[rs, cs], ke_s[rs, cs])
        for h in range(NH):
            cs = pl.ds(HD * h, HD)
            s_run = st[h]
            for c in range(nc_t):
                rs = pl.ds(HC * c, HC)
                sc_ref[c, h] = s_run
                o_ref[rs, cs] += _dot_nt(qe_s[rs, cs], s_run.astype(_BF))
                s_run = e_end[HC * c:HC * c + 1, HD * h:HD * (h + 1)] * s_run + u_s[NH * c + h]
            st[h] = s_run
        for h in range(NH):
            cs = pl.ds(HD * h, HD)
            n_o, _ = _rms_fwd(o_ref[:, cs])
            hg = p_ref[:, pl.ds(2 * D_RG + 3 * D_HG + HD * h, HD)]
            y_ref[:, pl.ds(D_RG + HD * h, HD)] = ((n_o * ghg_ref[...]) * (hg * _sigmoid(hg))).astype(_BF)

        tail()

    hbm = pl.BlockSpec(memory_space=pl.ANY)
    return pl.pallas_call(
        body, name="mixer_fwd", grid=(nt,),
        in_specs=[pl.BlockSpec((TM, D_IN), lambda i: (i, 0)), _full((D_RG, D_RG)), _full((D_RG, D_RG)),
                  _full((16, D_RG)), _full((2, D_HG)), _full((1, HD))] + [hbm] * nsh,
        out_specs=[pl.BlockSpec((TM, D), lambda i: (i, 0)), pl.BlockSpec((TM, D_RG), lambda i: (i, 0)),
                   pl.BlockSpec((TM, D_HG), lambda i: (i, 0)),
                   pl.BlockSpec((nc_t, NH, HD, HD), lambda i: (i, 0, 0, 0))] + [hbm] * nsh,
        out_shape=[_S((t_pad, D), _BF), _S((t_pad, D_RG), _F32), _S((t_pad, D_HG), _F32),
                   _S((t_pad // HC, NH, HD, HD), _F32)] + [_S((N_DEV,) + s.shape, s.dtype) for s in shards],
        scratch_shapes=[pltpu.VMEM((TM + 8, D_RG), _F32), pltpu.VMEM((TM, D_RG), _F32),
                        pltpu.VMEM((TM, D_RG), _F32), pltpu.VMEM((8, D_RG), _F32),
                        pltpu.VMEM((NH, HD, HD), _F32)] + [pltpu.VMEM((TM, D_HG), _BF) for _ in range(5)]
        + [pltpu.VMEM((nc_t * NH, HD, HD), _F32)] + _sem_shapes(nsh),
        compiler_params=_cp(("arbitrary",)),
    )(p, wr, wi, vec, hb, g_hg, *shards)


def _outproj(h0, y, w_out, g_ffn):
    t_pad = h0.shape[0]

    def body(h_ref, y_ref, w_ref, g_ref, h1_ref, v_ref):
        h1 = h_ref[...] + _dot(y_ref[...], w_ref[...])
        h1_ref[...] = h1
        n, _ = _rms_fwd(h1)
        v_ref[...] = (n * g_ref[...]).astype(_BF)

    return pl.pallas_call(
        body, name="outproj", grid=(t_pad // TM,),
        in_specs=[pl.BlockSpec((TM, D), lambda i: (i, 0)), pl.BlockSpec((TM, D), lambda i: (i, 0)),
                  _full((D, D)), _full((1, D))],
        out_specs=[pl.BlockSpec((TM, D), lambda i: (i, 0)), pl.BlockSpec((TM, D), lambda i: (i, 0))],
        out_shape=[_S((t_pad, D), _F32), _S((t_pad, D), _BF)],
        compiler_params=_cp(("arbitrary",)),
    )(h0, y, w_out, g_ffn)


def _ffn_loss(v, h1, w_gu, w_down, g_fin, tgt, n_valid):
    t_pad = v.shape[0]

    def body(v_ref, h1_ref, wgu_ref, wd_ref, g_ref, t_ref, gu_ref, act_ref, dh2_ref, dh2b_ref, loss_ref, gfin_ref):
        i = pl.program_id(0)

        @pl.when(i == 0)
        def _():
            loss_ref[...] = jnp.zeros_like(loss_ref)
            gfin_ref[...] = jnp.zeros_like(gfin_ref)

        vb = v_ref[...]
        h2 = h1_ref[...]
        for b in range(4):
            gate = _dot(vb, wgu_ref[b])
            up = _dot(vb, wgu_ref[4 + b])
            gu_ref[b] = gate
            gu_ref[4 + b] = up
            act = ((gate * _sigmoid(gate)) * up).astype(_BF)
            act_ref[b] = act
            h2 = h2 + _dot(act, wd_ref[b])
        n, r = _rms_fwd(h2)
        out = n * g_ref[...]
        row = i * TM + lax.broadcasted_iota(jnp.int32, (TM, 1), 0)
        valid = (row >= N_META) & (row < n_valid)
        err = jnp.where(valid, out - t_ref[...], 0.0)
        loss_ref[...] += (0.5 / D) * jnp.sum(err * err)
        dout = err * (1.0 / D)
        gfin_ref[...] += jnp.sum(dout * n, axis=0, keepdims=True)
        dh2 = _rms_bwd(dout * g_ref[...], n, r)
        dh2_ref[...] = dh2
        dh2b_ref[...] = dh2.astype(_BF)

    return pl.pallas_call(
        body, name="ffn_loss", grid=(t_pad // TM,),
        in_specs=[pl.BlockSpec((TM, D), lambda i: (i, 0)), pl.BlockSpec((TM, D), lambda i: (i, 0)),
                  _const((N_DEV, D, FFB)), _const((4, FFB, D)), _full((1, D)),
                  pl.BlockSpec((TM, D), lambda i: (i, 0))],
        out_specs=[pl.BlockSpec((N_DEV, TM, FFB), lambda i: (0, i, 0)), pl.BlockSpec((4, TM, FFB), lambda i: (0, i, 0)),
                   pl.BlockSpec((TM, D), lambda i: (i, 0)), pl.BlockSpec((TM, D), lambda i: (i, 0)),
                   _full((8, 128)), _full((1, D))],
        out_shape=[_S((N_DEV, t_pad, FFB), _F32), _S((4, t_pad, FFB), _BF), _S((t_pad, D), _F32),
                   _S((t_pad, D), _BF), _S((8, 128), _F32), _S((1, D), _F32)],
        compiler_params=_cp(("arbitrary",)),
    )(v, h1, w_gu, w_down, g_fin, tgt)


def _ffn_bwd(dh2, dh2b, gu, h1, g_ffn, w_gu, w_down, w_out):
    t_pad = dh2.shape[0]

    def body(dh2_ref, dh2b_ref, gu_ref, h1_ref, g_ref, wgu_ref, wd_ref, wo_ref,
             dgu_ref, dh1_ref, dh1b_ref, dy_ref, gffn_ref):
        i = pl.program_id(0)

        @pl.when(i == 0)
        def _():
            gffn_ref[...] = jnp.zeros_like(gffn_ref)

        db = dh2b_ref[...]
        dv = jnp.zeros((TM, D), _F32)
        for b in range(4):
            dact = _dot_nt(db, wd_ref[b])
            gate = gu_ref[b]
            up = gu_ref[4 + b]
            sg = _sigmoid(gate)
            dgate = ((dact * up) * _dsilu(gate, sg)).astype(_BF)
            dup = (dact * (gate * sg)).astype(_BF)
            dgu_ref[b] = dgate
            dgu_ref[4 + b] = dup
            dv = dv + _dot_nt(dgate, wgu_ref[b]) + _dot_nt(dup, wgu_ref[4 + b])
        n, r = _rms_fwd(h1_ref[...])
        gffn_ref[...] += jnp.sum(dv * n, axis=0, keepdims=True)
        dh1 = dh2_ref[...] + _rms_bwd(dv * g_ref[...], n, r)
        dh1_ref[...] = dh1
        dh1b = dh1.astype(_BF)
        dh1b_ref[...] = dh1b
        dy_ref[...] = _dot_nt(dh1b, wo_ref[...])

    tile = pl.BlockSpec((TM, D), lambda i: (i, 0))
    return pl.pallas_call(
        body, name="ffn_bwd", grid=(t_pad // TM,),
        in_specs=[tile, tile, pl.BlockSpec((N_DEV, TM, FFB), lambda i: (0, i, 0)), tile, _full((1, D)),
                  _const((N_DEV, D, FFB)), _const((4, FFB, D)), _const((D, D))],
        out_specs=[pl.BlockSpec((N_DEV, TM, FFB), lambda i: (0, i, 0)), tile, tile, tile, _full((1, D))],
        out_shape=[_S((N_DEV, t_pad, FFB), _BF), _S((t_pad, D), _F32), _S((t_pad, D), _BF),
                   _S((t_pad, D), _F32), _S((1, D), _F32)],
        compiler_params=_cp(("arbitrary",)),
    )(dh2, dh2b, gu, h1, g_ffn, w_gu, w_down, w_out)


def _mixer_bwd(p, hs, o, sc, dy, wr, wi, vec, hb, g_hg, scatter):
    t_pad = p.shape[0]
    nt = t_pad // TM
    nc_t = TM // HC
    nsc = len(scatter)

    def rev(i):
        return nt - 1 - i

    def body(p_ref, pprev_ref, hs_ref, hprev_ref, o_ref, sc_ref, dy_ref, wr_ref, wi_ref, vec_ref, hb_ref, ghg_ref,
             *rest):
        send_refs, rest = rest[:nsc], rest[nsc:]
        dp_ref, gvec_ref, gw_ref = rest[:3]
        recv_refs, rest = rest[3:3 + nsc], rest[3 + nsc:]
        xbuf, hbuf, dbuf, a_s, g_s, ccar, dst = rest[:7]
        qd_s, kd_s, qe_s, ke_s, v_s, do_s, dqd_s, dkd_s, dqe_s, dke_s, dv_s, w_s, dend_s = rest[7:20]
        exchange = _Exchange(send_refs, [], recv_refs, rest[20:])
        i = pl.program_id(0)
        first_tile = i == nt - 1

        @pl.when(i == 0)
        def _():
            exchange.start()
            gvec_ref[...] = jnp.zeros_like(gvec_ref)
            gw_ref[...] = jnp.zeros_like(gw_ref)
            dbuf[pl.ds(TM, 8), :] = jnp.zeros((8, D_RG), _F32)
            ccar[...] = jnp.zeros_like(ccar)
            dst[...] = jnp.zeros_like(dst)

        def acc(row, val):
            gvec_ref[row:row + 1, :] += jnp.sum(val, axis=0, keepdims=True)

        keep = jnp.where(first_tile, 0.0, 1.0)
        x = p_ref[:, pl.ds(0, D_RG)]
        xbuf[pl.ds(0, 8), :] = pprev_ref[...] * keep
        xbuf[pl.ds(8, TM), :] = x
        xc = _conv(xbuf, vec_ref)
        r, ig, a, s, nsp8 = _rg_gates(xc, wr_ref, wi_ref, vec_ref)
        h = hs_ref[...]
        hbuf[pl.ds(0, 8), :] = hprev_ref[...] * keep
        hbuf[pl.ds(8, TM), :] = h
        hm1 = hbuf[pl.ds(7, TM), :]
        gr = p_ref[:, pl.ds(D_RG, D_RG)]
        gel, dgel = _gelu_parts(gr)
        n, rr = _rms_fwd(gel * h)
        dyn = dy_ref[:, pl.ds(0, D_RG)]
        acc(R_GRG, dyn * n)
        dpre = _rms_bwd(dyn * vec_ref[R_GRG:R_GRG + 1, :], n, rr)
        dp_ref[:, pl.ds(D_RG, D_RG)] = ((dpre * h) * dgel).astype(_BF)
        a_s[...] = a
        g_s[...] = dpre * gel

        def step(k, c):
            t = TM - 1 - k
            g = g_s[pl.ds(t, 1), :] + c
            g_s[pl.ds(t, 1), :] = g
            return a_s[pl.ds(t, 1), :] * g

        ccar[pl.ds(0, 1), :] = lax.fori_loop(0, TM, step, ccar[pl.ds(0, 1), :], unroll=8)
        gt = g_s[...]
        da = gt * hm1
        ixc = ig * xc
        ds = gt * ixc
        dig = (gt * s) * xc
        dxc = (gt * s) * ig
        dla = da * a - ds * ((a * a) / s)
        lam = vec_ref[R_LAM:R_LAM + 1, :]
        gvec_ref[R_LAM:R_LAM + 1, :] += jnp.sum(dla * r, axis=0, keepdims=True) * (LRU_C * _sigmoid(-lam))
        dzr = (dla * nsp8) * (r * (1.0 - r))
        dzi = dig * (ig * (1.0 - ig))
        acc(R_BR, dzr)
        acc(R_BI, dzi)
        xcb = xc.astype(_BF)
        dzrb = dzr.astype(_BF)
        dzib = dzi.astype(_BF)
        gw_ref[0] += _dot_tn(xcb, dzrb)
        gw_ref[1] += _dot_tn(xcb, dzib)
        dxc = dxc + _dot_nt(dzrb, wr_ref[...]) + _dot_nt(dzib, wi_ref[...])
        acc(R_CONVB, dxc)
        for j in range(4):
            acc(R_CONVW + j, dxc * xbuf[pl.ds(5 + j, TM), :])
        dbuf[pl.ds(0, TM), :] = dxc
        dx = vec_ref[R_CONVW + 3:R_CONVW + 4, :] * dxc
        for j in range(3):
            dx = dx + vec_ref[R_CONVW + j:R_CONVW + j + 1, :] * dbuf[pl.ds(3 - j, TM), :]
        dbuf[pl.ds(TM, 8), :] = dxc[0:8, :]
        dp_ref[:, pl.ds(0, D_RG)] = dx.astype(_BF)

        lb = _sigmoid(hb_ref[0:1, :] - hb_ref[1:2, :])
        same, tri_blk, triu_blk = _chunk_masks()
        q = _hg_prep(p_ref, lb, tri_blk.astype(_BF))
        qdb, kdb = q["qd"].astype(_BF), q["kd"].astype(_BF)
        qd_s[...] = qdb
        kd_s[...] = kdb
        qe_s[...] = q["qe"].astype(_BF)
        ke_s[...] = q["ke"].astype(_BF)
        v_s[...] = p_ref[:, pl.ds(2 * D_RG + 2 * D_HG, D_HG)].astype(_BF)
        e_end = q["e_end"]
        ghg = ghg_ref[...]
        for h in range(NH):
            cs = pl.ds(HD * h, HD)
            hg = p_ref[:, pl.ds(2 * D_RG + 3 * D_HG + HD * h, HD)]
            sh = _sigmoid(hg)
            n_o, r_o = _rms_fwd(o_ref[:, cs])
            dyh = dy_ref[:, pl.ds(D_RG + HD * h, HD)]
            dp_ref[:, pl.ds(2 * D_RG + 3 * D_HG + HD * h, HD)] = ((dyh * (n_o * ghg)) * _dsilu(hg, sh)).astype(_BF)
            dn = dyh * (hg * sh)
            gvec_ref[R_GHG:R_GHG + 1, pl.ds(0, HD)] += jnp.sum(dn * n_o, axis=0, keepdims=True)
            do_s[:, cs] = _rms_bwd(dn * ghg, n_o, r_o).astype(_BF)
        causal = (lax.broadcasted_iota(jnp.int32, (HC, HC), 0) >= lax.broadcasted_iota(jnp.int32, (HC, HC), 1))
        for c in range(nc_t):
            for h in range(NH):
                rs, cs = pl.ds(HC * c, HC), pl.ds(HD * h, HD)
                qd_c, kd_c, do_c = qd_s[rs, cs], kd_s[rs, cs], do_s[rs, cs]
                amat = jnp.where(causal, _dot_nt(qd_c, kd_c), 0.0).astype(_BF)
                da_m = jnp.where(causal, _dot_nt(do_c, v_s[rs, cs]), 0.0).astype(_BF)
                dqd_s[rs, cs] = _dot(da_m, kd_c)
                dkd_s[rs, cs] = _dot_tn(da_m, qd_c)
                dqe_s[rs, cs] = _dot(do_c, sc_ref[c, h].astype(_BF))
                dv_s[rs, cs] = _dot_tn(amat, do_c)
                w_s[NH * c + h] = _dot_tn(do_c, qe_s[rs, cs])
        for h in range(NH):
            cs = pl.ds(HD * h, HD)
            d_run = dst[h]
            for c in reversed(range(nc_t)):
                rs = pl.ds(HC * c, HC)
                d_b = d_run.astype(_BF)
                dke_s[rs, cs] = _dot(v_s[rs, cs], d_b)
                dp_ref[rs, pl.ds(2 * D_RG + 2 * D_HG + HD * h, HD)] = (
                    dv_s[rs, cs] + _dot_nt(ke_s[rs, cs], d_b)).astype(_BF)
                dend_s[pl.ds(c, 1), cs] = jnp.sum(sc_ref[c, h] * d_run, axis=0, keepdims=True)
                d_run = w_s[NH * c + h] + e_end[HC * c:HC * c + 1, HD * h:HD * (h + 1)] * d_run
            dst[h] = d_run
        dqd, dkd, dqe, dke = dqd_s[...], dkd_s[...], dqe_s[...], dke_s[...]
        dq = dqd * q["e_q"] + dqe * q["e_b"]
        dk = dkd * q["e_k"] + dke * q["e_l"]
        dkeke = dke * q["ke"]
        db = dqd * qdb.astype(_F32) - dkd * kdb.astype(_F32) + dqe * q["qe"] - dkeke
        d_end = jnp.concatenate([jnp.broadcast_to(dend_s[pl.ds(c, 1), :], (HC, D_HG)) for c in range(nc_t)], axis=0)
        dlf = _dot3(triu_blk.astype(_BF), db) + _dot3(same.astype(_BF), dkeke) + d_end * e_end
        df = dlf / q["f"] - dk
        sg = q["sg"]
        gvec_ref[R_HB0:R_HB0 + 1, :] += jnp.sum(df * (1.0 - sg), axis=0, keepdims=True)
        dp_ref[:, pl.ds(2 * D_RG, D_HG)] = (dq * _dsilu(q["hq"], q["sq"])).astype(_BF)
        dp_ref[:, pl.ds(2 * D_RG + D_HG, D_HG)] = ((df * (1.0 - lb)) * (sg * (1.0 - sg))).astype(_BF)

        @pl.when(i == nt - 1)
        def _():
            glb = gvec_ref[R_HB0:R_HB0 + 1, :] * (lb * (1.0 - lb))
            gvec_ref[R_HB0:R_HB0 + 1, :] = glb
            gvec_ref[R_HB1:R_HB1 + 1, :] = -glb
            exchange.finish()

    hbm = pl.BlockSpec(memory_space=pl.ANY)
    return pl.pallas_call(
        body, name="mixer_bwd", grid=(nt,),
        in_specs=[pl.BlockSpec((TM, D_IN), lambda i: (rev(i), 0)),
                  pl.BlockSpec((8, D_RG), lambda i: (jnp.maximum(rev(i) * (TM // 8) - 1, 0), 0)),
                  pl.BlockSpec((TM, D_RG), lambda i: (rev(i), 0)),
                  pl.BlockSpec((8, D_RG), lambda i: (jnp.maximum(rev(i) * (TM // 8) - 1, 0), 0)),
                  pl.BlockSpec((TM, D_HG), lambda i: (rev(i), 0)),
                  pl.BlockSpec((nc_t, NH, HD, HD), lambda i: (rev(i), 0, 0, 0)),
                  pl.BlockSpec((TM, D), lambda i: (rev(i), 0)),
                  _full((D_RG, D_RG)), _full((D_RG, D_RG)), _full((16, D_RG)), _full((2, D_HG)), _full((1, HD))]
        + [hbm] * nsc,
        out_specs=[pl.BlockSpec((TM, D_IN), lambda i: (rev(i), 0)), _full((16, D_RG)), _full((2, D_RG, D_RG))]
        + [hbm] * nsc,
        out_shape=[_S((t_pad, D_IN), _BF), _S((16, D_RG), _F32), _S((2, D_RG, D_RG), _F32)]
        + [_S(s.shape, s.dtype) for s in scatter],
        scratch_shapes=[pltpu.VMEM((TM + 8, D_RG), _F32), pltpu.VMEM((TM + 8, D_RG), _F32),
                        pltpu.VMEM((TM + 8, D_RG), _F32), pltpu.VMEM((TM, D_RG), _F32),
                        pltpu.VMEM((TM, D_RG), _F32), pltpu.VMEM((8, D_RG), _F32),
                        pltpu.VMEM((NH, HD, HD), _F32)]
        + [pltpu.VMEM((TM, D_HG), _BF) for _ in range(6)] + [pltpu.VMEM((TM, D_HG), _F32) for _ in range(5)]
        + [pltpu.VMEM((nc_t * NH, HD, HD), _F32), pltpu.VMEM((8, D_HG), _F32)] + _sem_shapes(nsc),
        compiler_params=_cp(("arbitrary",)),
    )(p, p, hs, hs, o, sc, dy, wr, wi, vec, hb, g_hg, *scatter)


def _inproj_bwd(dp, w_in, h0, dh1, g_mix, to_all):
    t_pad = dp.shape[0]
    nt = t_pad // TM
    na = len(to_all)

    def body(dp_ref, w_ref, h_ref, dh1_ref, g_ref, *rest):
        dh0_ref, gmix_ref = rest[na:na + 2]
        exchange = _Exchange([], rest[:na], rest[na + 2:2 * na + 2], rest[2 * na + 2:])
        i = pl.program_id(0)

        @pl.when(i == 0)
        def _():
            exchange.start()
            gmix_ref[...] = jnp.zeros_like(gmix_ref)

        du = jnp.zeros((TM, D), _F32)
        for j in range(N_DEV):
            du = du + _dot_nt(dp_ref[:, WIN_B * j:WIN_B * (j + 1)], w_ref[j])
        n, r = _rms_fwd(h_ref[...])
        gmix_ref[...] += jnp.sum(du * n, axis=0, keepdims=True)
        dh0_ref[...] = dh1_ref[...] + _rms_bwd(du * g_ref[...], n, r)

        @pl.when(i == nt - 1)
        def _():
            exchange.finish()

    tile = pl.BlockSpec((TM, D), lambda i: (i, 0))
    hbm = pl.BlockSpec(memory_space=pl.ANY)
    return pl.pallas_call(
        body, name="inproj_bwd", grid=(nt,),
        in_specs=[pl.BlockSpec((TM, D_IN), lambda i: (i, 0)), _const((N_DEV, D, WIN_B)), tile, tile, _full((1, D))]
        + [hbm] * na,
        out_specs=[tile, _full((1, D))] + [hbm] * na,
        out_shape=[_S((t_pad, D), _F32), _S((1, D), _F32)] + [_S((N_DEV,) + g.shape, g.dtype) for g in to_all],
        scratch_shapes=_sem_shapes(na),
        compiler_params=_cp(("arbitrary",)),
    )(dp, w_in, h0, dh1, g_mix, *to_all)


def _wgrad_send(u, dp, order, to_all):
    t_pad = u.shape[0]
    na = len(to_all)

    def body(order_ref, u_ref, dp_ref, *rest):
        all_in = rest[:na]
        recv_ref = rest[na]
        all_out = rest[na + 1:2 * na + 1]
        buf, blk_send, blk_recv, blk_local = rest[2 * na + 1:2 * na + 5]
        exchange = _Exchange([], all_in, all_out, rest[2 * na + 5:])
        s = pl.program_id(0)
        x, y, c = _coords()
        me = 4 * x + 2 * y + c
        slot = s % 2

        def send(step):
            r = _SEND_ORDER[step]
            return pltpu.make_async_remote_copy(
                src_ref=buf.at[step % 2], dst_ref=recv_ref.at[me], send_sem=blk_send.at[step], recv_sem=blk_recv.at[r - 1],
                device_id=(x ^ (r >> 2), y ^ ((r >> 1) & 1), c ^ (r & 1)), device_id_type=_MESH)

        @pl.when(s == 0)
        def _():
            exchange.start()

        for step in range(2, N_DEV):
            @pl.when(s == step)
            def _(step=step):
                send(step - 2).wait_send()

        buf[slot] = _dot_tn(u_ref[...], dp_ref[...]).astype(_BF)

        for step in range(N_DEV - 1):
            @pl.when(s == step)
            def _(step=step):
                send(step).start()

        @pl.when(s == N_DEV - 1)
        def _():
            mine = pltpu.make_async_copy(buf.at[(N_DEV - 1) % 2], recv_ref.at[me], blk_local.at[0])
            mine.start()
            send(N_DEV - 2).wait_send()
            for r in range(1, N_DEV):
                px, py, pc = x ^ (r >> 2), y ^ ((r >> 1) & 1), c ^ (r & 1)
                pltpu.make_async_remote_copy(
                    src_ref=buf.at[0], dst_ref=recv_ref.at[4 * px + 2 * py + pc], send_sem=blk_send.at[0],
                    recv_sem=blk_recv.at[r - 1], device_id=(px, py, pc), device_id_type=_MESH).wait_recv()
            mine.wait()
            exchange.finish()

    hbm = pl.BlockSpec(memory_space=pl.ANY)
    return pl.pallas_call(
        body, name="wgrad_in_send",
        grid_spec=pltpu.PrefetchScalarGridSpec(
            num_scalar_prefetch=1, grid=(N_DEV,),
            in_specs=[pl.BlockSpec((t_pad, D), lambda s, order: (0, 0)),
                      pl.BlockSpec((t_pad, WIN_B), lambda s, order: (0, order[s]))] + [hbm] * na,
            out_specs=[hbm] * (na + 1),
            scratch_shapes=[pltpu.VMEM((2, D, WIN_B), _BF), pltpu.SemaphoreType.DMA((N_DEV - 1,)),
                            pltpu.SemaphoreType.DMA((N_DEV - 1,)), pltpu.SemaphoreType.DMA((1,))] + _sem_shapes(na)),
        out_shape=[_S((N_DEV, D, WIN_B), _BF)] + [_S((N_DEV,) + g.shape, g.dtype) for g in to_all],
        compiler_params=_cp(("arbitrary",)),
    )(order, u, dp, *to_all)


def _wgrad(name, a, b, a_spec, b_spec, n_blocks, out_block, scatter=()):
    nsc = len(scatter)

    def body(a_ref, b_ref, *rest):
        o_ref = rest[nsc]
        j = pl.program_id(0)
        if nsc:
            exchange = _Exchange(rest[:nsc], [], rest[nsc + 1:2 * nsc + 1], rest[2 * nsc + 1:])

            @pl.when(j == 0)
            def _():
                exchange.start()

        av = a_ref[0] if len(a_ref.shape) == 3 else a_ref[...]
        bv = b_ref[0] if len(b_ref.shape) == 3 else b_ref[...]
        o_ref[0] = _dot_tn(av, bv).astype(_BF)

        if nsc:
            @pl.when(j == n_blocks - 1)
            def _():
                exchange.finish()

    hbm = pl.BlockSpec(memory_space=pl.ANY)
    res = pl.pallas_call(
        body, name=name, grid=(n_blocks,),
        in_specs=[a_spec, b_spec] + [hbm] * nsc,
        out_specs=[pl.BlockSpec((1,) + out_block, lambda j: (j, 0, 0))] + [hbm] * nsc,
        out_shape=[_S((n_blocks,) + out_block, _BF)] + [_S(s.shape, s.dtype) for s in scatter],
        scratch_shapes=_sem_shapes(nsc) if nsc else [],
        compiler_params=_cp(("arbitrary",)),
    )(a, b, *scatter)
    return res if nsc else res[0]


def _coords():
    return lax.axis_index("x"), lax.axis_index("y"), lax.axis_index("c")


def _sem_shapes(na):
    return [pltpu.SemaphoreType.DMA((7 * na,)), pltpu.SemaphoreType.DMA((7 * na,)), pltpu.SemaphoreType.DMA((na,))]


class _Gather:
    def __init__(self, srcs, outs, sems):
        self.srcs, self.outs = srcs, outs
        self.send_sems, self.recv_sems, self.local_sems = sems
        self.na = len(srcs)
        x, y, c = _coords()
        self.pos = (x, y, c)
        self.me = 4 * x + 2 * y + c
        self.sibling = (x, y, 1 - c)
        self.chips = [(1 - x, y), (x, 1 - y), (1 - x, 1 - y)]

    @staticmethod
    def _slot(px, py, pc):
        return 4 * px + 2 * py + pc

    def _copy(self, a, k, block, to, own=False):
        return pltpu.make_async_remote_copy(
            src_ref=self.srcs[a] if own else self.outs[a].at[block], dst_ref=self.outs[a].at[block],
            send_sem=self.send_sems.at[7 * a + k], recv_sem=self.recv_sems.at[7 * a + k],
            device_id=to, device_id_type=_MESH)

    def _mine(self, a):
        return pltpu.make_async_copy(self.srcs[a], self.outs[a].at[self.me], self.local_sems.at[a])

    def _first(self):
        c = self.pos[2]
        cps = []
        for a in range(self.na):
            cps.append(self._copy(a, 0, self.me, self.sibling, own=True))
            cps += [self._copy(a, 1 + j, self.me, (*chip, c), own=True) for j, chip in enumerate(self.chips)]
        return cps

    def _passed(self):
        c = self.pos[2]
        return [self._copy(a, 4 + j, self._slot(*chip, c), self.sibling)
                for j, chip in enumerate(self.chips) for a in range(self.na)]

    def start(self):
        for a in range(self.na):
            self._mine(a).start()
        for cp in self._first():
            cp.start()

    def forward(self, j):
        c = self.pos[2]
        chip = self.chips[j]
        for a in range(self.na):
            self._copy(a, 1 + j, self._slot(*chip, c), self.pos).wait_recv()
            self._copy(a, 4 + j, self._slot(*chip, c), self.sibling).start()

    def finish(self):
        x, y, c = self.pos
        for a in range(self.na):
            self._copy(a, 0, self._slot(x, y, 1 - c), self.pos).wait_recv()
        for j, chip in enumerate(self.chips):
            for a in range(self.na):
                self._copy(a, 4 + j, self._slot(*chip, 1 - c), self.pos).wait_recv()
        for cp in self._first() + self._passed():
            cp.wait_send()
        for a in range(self.na):
            self._mine(a).wait()


class _Exchange:
    def __init__(self, scatter, gather, outs, sems):
        self.ins = list(scatter) + list(gather)
        self.ns, self.na = len(scatter), len(scatter) + len(gather)
        self.outs = outs
        self.send_sems, self.recv_sems, self.local_sems = sems
        x, y, c = _coords()
        self.pos = (x, y, c)
        self.me = 4 * x + 2 * y + c

    def _peer(self, r):
        x, y, c = self.pos
        return x ^ (r >> 2), y ^ ((r >> 1) & 1), c ^ (r & 1)

    def _src(self, a, block):
        return self.ins[a].at[block] if a < self.ns else self.ins[a]

    def _local(self, a):
        return pltpu.make_async_copy(self._src(a, self.me), self.outs[a].at[self.me], self.local_sems.at[a])

    def _send(self, a, r):
        px, py, pc = self._peer(r)
        return pltpu.make_async_remote_copy(
            src_ref=self._src(a, 4 * px + 2 * py + pc), dst_ref=self.outs[a].at[self.me],
            send_sem=self.send_sems.at[7 * a + r - 1], recv_sem=self.recv_sems.at[7 * a + r - 1],
            device_id=(px, py, pc), device_id_type=_MESH)

    def _recv(self, a, r):
        px, py, pc = self._peer(r)
        return pltpu.make_async_remote_copy(
            src_ref=self._src(a, self.me), dst_ref=self.outs[a].at[4 * px + 2 * py + pc],
            send_sem=self.send_sems.at[7 * a + r - 1], recv_sem=self.recv_sems.at[7 * a + r - 1],
            device_id=(px, py, pc), device_id_type=_MESH)

    def start(self):
        for a in range(self.na):
            self._local(a).start()
        for r in range(1, N_DEV):
            for a in range(self.na):
                self._send(a, r).start()

    def finish(self):
        for r in range(1, N_DEV):
            for a in range(self.na):
                self._recv(a, r).wait_recv()
        for r in range(1, N_DEV):
            for a in range(self.na):
                self._send(a, r).wait_send()
        for a in range(self.na):
            self._local(a).wait()


def _allgather_first(gather_f32, cast_f32, gather_dtypes):
    ng, nc = len(gather_f32), len(cast_f32)

    def body(*refs):
        ins, cins = refs[:ng], refs[ng:ng + nc]
        outs, couts = refs[ng + nc:2 * ng + nc], refs[2 * ng + nc:2 * ng + 2 * nc]
        stage = refs[2 * ng + 2 * nc:3 * ng + 2 * nc]
        sems = refs[3 * ng + 2 * nc:]
        for a in range(ng):
            stage[a][...] = ins[a][...].astype(gather_dtypes[a])
        g = _Gather(stage, outs, sems)
        g.start()
        for a in range(nc):
            couts[a][...] = cins[a][...].astype(_BF)
        for j in range(3):
            g.forward(j)
        g.finish()

    vm = pl.BlockSpec(memory_space=pltpu.VMEM)
    return pl.pallas_call(
        body, name="allgather_first",
        in_specs=[vm] * (ng + nc),
        out_specs=[pl.BlockSpec(memory_space=pl.ANY)] * ng + [vm] * nc,
        out_shape=[_S((N_DEV,) + l.shape, dt) for l, dt in zip(gather_f32, gather_dtypes)]
        + [_S(l.shape, _BF) for l in cast_f32],
        scratch_shapes=[pltpu.VMEM(l.shape, dt) for l, dt in zip(gather_f32, gather_dtypes)] + _sem_shapes(ng),
        compiler_params=pltpu.CompilerParams(vmem_limit_bytes=VMEM_LIMIT),
    )(*gather_f32, *cast_f32)


def _adamw_math(w, g, m, v):
    m2 = ADAM_B1 * m + (1.0 - ADAM_B1) * g
    v2 = ADAM_B2 * v + (1.0 - ADAM_B2) * (g * g)
    m_hat = m2 / (1.0 - ADAM_B1 ** ADAM_STEP)
    v_hat = v2 / (1.0 - ADAM_B2 ** ADAM_STEP)
    delta = -ADAM_LR * (m_hat / (jnp.sqrt(v_hat) + ADAM_EPS) + ADAM_WD * w)
    return delta, m2, v2


def _adamw_big(name, recv, w, m, v, rows):
    r_all, c_all = w.shape

    def body(r_ref, w_ref, m_ref, v_ref, g_out, d_out, m_out, v_out):
        g = r_ref[0].astype(_F32)
        for k in range(1, N_DEV):
            g = g + r_ref[k].astype(_F32)
        delta, m2, v2 = _adamw_math(w_ref[...], g, m_ref[...], v_ref[...])
        g_out[...] = g
        d_out[...] = delta
        m_out[...] = m2
        v_out[...] = v2

    tile = pl.BlockSpec((rows, c_all), lambda i: (i, 0))
    return pl.pallas_call(
        body, name=name, grid=(r_all // rows,),
        in_specs=[pl.BlockSpec((N_DEV, rows, c_all), lambda i: (0, i, 0)), tile, tile, tile],
        out_specs=[tile] * 4,
        out_shape=[_S(w.shape, _F32)] * 4,
        compiler_params=_cp(("arbitrary",)),
    )(recv, w, m, v)


def _adamw_small(gathered, slices, wmv):
    ng, npar = len(gathered), len(slices)

    def body(*refs):
        g_refs = refs[:ng]
        wmv_refs = refs[ng:ng + 3 * npar]
        outs = refs[ng + 3 * npar:]
        for i, (ai, r0, nr, ncol) in enumerate(slices):
            g = g_refs[ai][0, pl.ds(r0, nr), pl.ds(0, ncol)]
            for k in range(1, N_DEV):
                g = g + g_refs[ai][k, pl.ds(r0, nr), pl.ds(0, ncol)]
            w_ref, m_ref, v_ref = wmv_refs[3 * i:3 * i + 3]
            delta, m2, v2 = _adamw_math(w_ref[...], g, m_ref[...], v_ref[...])
            outs[4 * i][...] = g
            outs[4 * i + 1][...] = delta
            outs[4 * i + 2][...] = m2
            outs[4 * i + 3][...] = v2

    flat = [t for trip in wmv for t in trip]
    out_shape = []
    for w, _, _ in wmv:
        out_shape += [_S(w.shape, _F32)] * 4
    return pl.pallas_call(
        body, name="adamw_small", out_shape=out_shape,
        compiler_params=pltpu.CompilerParams(vmem_limit_bytes=VMEM_LIMIT),
    )(*gathered, *flat)


def _block_diag(w):
    eye = jnp.eye(8, dtype=w.dtype)
    return (w[:, :, None, :] * eye[:, None, :, None]).reshape(D_RG, D_RG)


def _diag_blocks(g):
    return jnp.concatenate([g[64 * h:64 * (h + 1), 64 * h:64 * (h + 1)] for h in range(8)], axis=0)


def _local_step(x, tgt, meta, g_mix, w_in, vec, wr, wi, hb, g_hg, w_out_l, g_ffn, w_gu_l, w_down_l, g_fin):
    seq = x.shape[0]
    n_valid = N_META + seq
    t_pad = -(-n_valid // TM) * TM
    h0 = jnp.concatenate([meta, x, jnp.zeros((t_pad - n_valid, D), _F32)], axis=0)
    tgt_p = jnp.concatenate([jnp.zeros((N_META, D), _F32), tgt, jnp.zeros((t_pad - n_valid, D), _F32)], axis=0)

    p, u, w_out, w_down = _inproj(h0, g_mix, w_in, [w_out_l, w_down_l])
    y, hs, o, sc, w_gu = _mixer_fwd(p, wr, wi, vec, hb, g_hg, [w_gu_l])
    w_out = w_out.reshape(D, D)
    w_down = w_down.reshape(4, FFB, D)
    h1, v = _outproj(h0, y, w_out, g_ffn)
    gu, act, dh2, dh2b, loss, gfin = _ffn_loss(v, h1, w_gu, w_down, g_fin, tgt_p, n_valid)

    dgu, dh1, dh1b, dy, gffn = _ffn_bwd(dh2, dh2b, gu, h1, g_ffn, w_gu, w_down, w_out)
    g_wdown = _wgrad("wgrad_down", act, dh2b, pl.BlockSpec((1, t_pad, FFB), lambda j: (j, 0, 0)),
                     pl.BlockSpec((t_pad, D), lambda j: (0, 0)), 4, (FFB, D))
    g_wgu, r_wdown = _wgrad("wgrad_gate_up", v, dgu, pl.BlockSpec((t_pad, D), lambda j: (0, 0)),
                            pl.BlockSpec((1, t_pad, FFB), lambda j: (j, 0, 0)), N_DEV, (D, FFB),
                            scatter=[g_wdown.reshape(N_DEV, D_FF // N_DEV, D)])
    g_wout = _wgrad("wgrad_out", y, dh1b, pl.BlockSpec((t_pad, D // N_DEV), lambda j: (0, j)),
                    pl.BlockSpec((t_pad, D), lambda j: (0, 0)), N_DEV, (D // N_DEV, D))
    dp, gvec, gw, r_wgu, r_wout = _mixer_bwd(p, hs, o, sc, dy, wr, wi, vec, hb, g_hg, [g_wgu, g_wout])
    pack_c = jnp.concatenate([_diag_blocks(gw[0]), _diag_blocks(gw[1])], axis=0)
    dh0, gmix, all_b, all_c = _inproj_bwd(dp, w_in, h0, dh1, g_mix, [gvec, pack_c])
    pack_a = jnp.concatenate([gmix, gffn, gfin, jnp.zeros((5, D), _F32), dh0[:N_META]], axis=0)
    me = 4 * lax.axis_index("x") + 2 * lax.axis_index("y") + lax.axis_index("c")
    order = (me ^ jnp.array(_SEND_ORDER, jnp.int32)).astype(jnp.int32)
    r_win, all_a = _wgrad_send(u, dp, order, [pack_a])
    return loss, dh0, (r_win, r_wgu, r_wout, r_wdown), (all_a, all_b, all_c)


def kernel(x, meta_tokens, mix_norm_g, w_in, conv_w, conv_b, w_rgate, b_rgate, w_igate, b_igate, lru_lambda, rg_norm_g, hg_lower_bound, hg_norm_g, w_out, ffn_norm_g, w_gate_up, w_down, final_norm_g, loss_target, m_meta_tokens, m_mix_norm_g, m_w_in, m_conv_w, m_conv_b, m_w_rgate, m_b_rgate, m_w_igate, m_b_igate, m_lru_lambda, m_rg_norm_g, m_hg_lower_bound, m_hg_norm_g, m_w_out, m_ffn_norm_g, m_w_gate_up, m_w_down, m_final_norm_g, v_meta_tokens, v_mix_norm_g, v_w_in, v_conv_w, v_conv_b, v_w_rgate, v_b_rgate, v_w_igate, v_b_igate, v_lru_lambda, v_rg_norm_g, v_hg_lower_bound, v_hg_norm_g, v_w_out, v_ffn_norm_g, v_w_gate_up, v_w_down, v_final_norm_g):
    seq = x.shape[1]
    me = 4 * lax.axis_index("x") + 2 * lax.axis_index("y") + lax.axis_index("c")

    small_l = jnp.concatenate([meta_tokens, jnp.pad(conv_w[0], ((0, 4), (0, 64)))], axis=0)
    w_in_g, small_g, w_gu_l, w_out_l, w_down_l = _allgather_first(
        [w_in[0], small_l], [w_gate_up[0], w_out[0], w_down[0]], [_BF, _F32])
    meta_full = jnp.transpose(small_g[:, :N_META, :], (1, 0, 2)).reshape(N_META, D)
    conv_w_full = jnp.transpose(small_g[:, N_META:N_META + 4, :64], (1, 0, 2)).reshape(4, D_RG)
    vec = jnp.concatenate([conv_b, b_rgate, b_igate, lru_lambda, rg_norm_g, jnp.zeros((3, D_RG), _F32),
                           conv_w_full, jnp.zeros((4, D_RG), _F32)], axis=0)
    wr = _block_diag(w_rgate[0]).astype(_BF)
    wi = _block_diag(w_igate[0]).astype(_BF)

    loss, dh0, (r_win, r_wgu, r_wout, r_wdown), (all_a, all_b, all_c) = _local_step(
        x[0], loss_target[0], meta_full, mix_norm_g, w_in_g, vec, wr, wi, hg_lower_bound, hg_norm_g,
        w_out_l, ffn_norm_g, w_gu_l, w_down_l, final_norm_g.reshape(1, D))
    grad_x = dh0[N_META:N_META + seq][None]

    outs = {}
    outs["w_in"] = _adamw_big("adamw_w_in", r_win, w_in[0], m_w_in[0], v_w_in[0], 256)
    outs["w_gate_up"] = _adamw_big("adamw_w_gate_up", r_wgu, w_gate_up[0], m_w_gate_up[0], v_w_gate_up[0], 256)
    outs["w_out"] = _adamw_big("adamw_w_out", r_wout, w_out[0], m_w_out[0], v_w_out[0], 128)
    outs["w_down"] = _adamw_big("adamw_w_down", r_wdown, w_down[0], m_w_down[0], v_w_down[0], 176)

    meta_part = lax.dynamic_slice_in_dim(all_a[:, R_META:R_META + N_META, :], me * 128, 128, axis=2)
    convw_part = lax.dynamic_slice_in_dim(all_b[:, R_CONVW:R_CONVW + 4, :], me * 64, 64, axis=2)
    gathered = [all_a, all_b, all_c, meta_part, convw_part]
    small_params = [
        ("meta_tokens", (3, 0, N_META, 128), (meta_tokens, m_meta_tokens, v_meta_tokens), (N_META, 128)),
        ("mix_norm_g", (0, R_GMIX, 1, D), (mix_norm_g, m_mix_norm_g, v_mix_norm_g), (1, D)),
        ("conv_w", (4, 0, 4, 64), (conv_w, m_conv_w, v_conv_w), (4, 64)),
        ("conv_b", (1, R_CONVB, 1, D_RG), (conv_b, m_conv_b, v_conv_b), (1, D_RG)),
        ("w_rgate", (2, 0, 512, 64), (w_rgate, m_w_rgate, v_w_rgate), (512, 64)),
        ("b_rgate", (1, R_BR, 1, D_RG), (b_rgate, m_b_rgate, v_b_rgate), (1, D_RG)),
        ("w_igate", (2, 512, 512, 64), (w_igate, m_w_igate, v_w_igate), (512, 64)),
        ("b_igate", (1, R_BI, 1, D_RG), (b_igate, m_b_igate, v_b_igate), (1, D_RG)),
        ("lru_lambda", (1, R_LAM, 1, D_RG), (lru_lambda, m_lru_lambda, v_lru_lambda), (1, D_RG)),
        ("rg_norm_g", (1, R_GRG, 1, D_RG), (rg_norm_g, m_rg_norm_g, v_rg_norm_g), (1, D_RG)),
        ("hg_lower_bound", (1, R_HB0, 2, D_HG), (hg_lower_bound, m_hg_lower_bound, v_hg_lower_bound), (2, D_HG)),
        ("hg_norm_g", (1, R_GHG, 1, HD), (hg_norm_g, m_hg_norm_g, v_hg_norm_g), (1, HD)),
        ("ffn_norm_g", (0, R_GFFN, 1, D), (ffn_norm_g, m_ffn_norm_g, v_ffn_norm_g), (1, D)),
        ("final_norm_g", (0, R_GFIN, 1, D), (final_norm_g, m_final_norm_g, v_final_norm_g), (1, D)),
    ]
    res = _adamw_small(gathered, [s[1] for s in small_params],
                       [tuple(t.reshape(s[3]) for t in s[2]) for s in small_params])
    for i, s in enumerate(small_params):
        outs[s[0]] = [r.reshape(s[2][0].shape) for r in res[4 * i:4 * i + 4]]
    for n, ref in (("w_in", w_in), ("w_gate_up", w_gate_up), ("w_out", w_out), ("w_down", w_down)):
        outs[n] = [r.reshape(ref.shape) for r in outs[n]]

    loss_all = lax.psum(loss[0, 0], ("x", "y", "c"))
    order = ["meta_tokens", "mix_norm_g", "w_in", "conv_w", "conv_b", "w_rgate", "b_rgate", "w_igate", "b_igate",
             "lru_lambda", "rg_norm_g", "hg_lower_bound", "hg_norm_g", "w_out", "ffn_norm_g", "w_gate_up", "w_down",
             "final_norm_g"]
    return (loss_all, grad_x, *[outs[n][0] for n in order], *[outs[n][1] for n in order],
            *[outs[n][2] for n in order], *[outs[n][3] for n in order])
```

```python
import functools

import jax
import jax.numpy as jnp
from jax import lax
from jax.experimental import pallas as pl
from jax.experimental.pallas import tpu as pltpu

_BF = jnp.bfloat16
_F32 = jnp.float32
_S = jax.ShapeDtypeStruct
_MESH = pl.DeviceIdType.MESH

N_DEV = 8
N_META = 16
D = 1024
D_RG = 512
D_HG = 512
HD = 128
NH = D_HG // HD
D_IN = 3072
D_FF = 2816
FFB = D_FF // 4
WIN_B = D_IN // N_DEV
EPS = 1e-6
LRU_C = 8.0
TM = 256
HC = 64
VMEM_LIMIT = 56 * 1024 * 1024

ADAM_LR = 0.001
ADAM_B1 = 0.9
ADAM_B2 = 0.999
ADAM_EPS = 1e-08
ADAM_WD = 0.01
ADAM_STEP = 10

_SEND_ORDER = (6, 7, 4, 5, 2, 3, 1, 0)

R_CONVB, R_BR, R_BI, R_LAM, R_GRG, R_HB0, R_HB1, R_GHG, R_CONVW = 0, 1, 2, 3, 4, 5, 6, 7, 8
R_GMIX, R_GFFN, R_GFIN, R_META = 0, 1, 2, 8


def _cp(sem=None, **kw):
    return pltpu.CompilerParams(dimension_semantics=sem, vmem_limit_bytes=VMEM_LIMIT, **kw)


def _dot(a, b):
    return jnp.dot(a, b, preferred_element_type=_F32)


def _dot_nt(a, b):
    return lax.dot_general(a, b, (((1,), (1,)), ((), ())), preferred_element_type=_F32)


def _dot_tn(a, b):
    return lax.dot_general(a, b, (((0,), (0,)), ((), ())), preferred_element_type=_F32)


def _sigmoid(x):
    return jax.nn.sigmoid(x)


def _dsilu(x, s):
    return s * (1.0 + x * (1.0 - s))


_GELU_C = 0.7978845608028654


def _gelu_parts(x):
    t = jnp.tanh(_GELU_C * (x + 0.044715 * (x * x * x)))
    g = 0.5 * x * (1.0 + t)
    dg = 0.5 * (1.0 + t) + 0.5 * x * (1.0 - t * t) * (_GELU_C * (1.0 + 3.0 * 0.044715 * (x * x)))
    return g, dg


def _softplus(z):
    e = jnp.exp(-jnp.abs(z))
    w = 1.0 + e
    l1p = jnp.where(w == 1.0, e, jnp.log(w) * e / jnp.where(w == 1.0, 1.0, w - 1.0))
    return jnp.maximum(z, 0.0) + l1p


def _rms_fwd(x):
    r = lax.rsqrt(jnp.mean(x * x, axis=-1, keepdims=True) + EPS)
    return x * r, r


def _rms_bwd(dyg, n, r):
    return r * (dyg - n * jnp.mean(dyg * n, axis=-1, keepdims=True))


def _full(shape):
    nd = len(shape)
    return pl.BlockSpec(shape, lambda i: (0,) * nd)


def _const(shape):
    nd = len(shape)
    return pl.BlockSpec(shape, lambda i: (0,) * nd, pipeline_mode=pl.Buffered(1))


def _carry_gather(gather, i, nt):
    @pl.when(i == 0)
    def _():
        gather.start()

    def tail():
        for j in range(3):
            @pl.when(i == max(nt - 3 + j, 0))
            def _(j=j):
                gather.forward(j)

        @pl.when(i == nt - 1)
        def _():
            gather.finish()

    return tail


def _inproj(h0, g_mix, w_in, shards):
    t_pad = h0.shape[0]
    nt = t_pad // TM
    nsh = len(shards)

    def body(h_ref, g_ref, w_ref, *rest):
        p_ref, u_ref = rest[nsh:nsh + 2]
        tail = _carry_gather(_Gather(rest[:nsh], rest[nsh + 2:2 * nsh + 2], rest[2 * nsh + 2:]), pl.program_id(0), nt)
        n, _ = _rms_fwd(h_ref[...])
        u = (n * g_ref[...]).astype(_BF)
        u_ref[...] = u
        for j in range(N_DEV):
            p_ref[:, WIN_B * j:WIN_B * (j + 1)] = _dot(u, w_ref[j])
        tail()

    hbm = pl.BlockSpec(memory_space=pl.ANY)
    return pl.pallas_call(
        body, name="inproj", grid=(nt,),
        in_specs=[pl.BlockSpec((TM, D), lambda i: (i, 0)), _full((1, D)), _const((N_DEV, D, WIN_B))] + [hbm] * nsh,
        out_specs=[pl.BlockSpec((TM, D_IN), lambda i: (i, 0)), pl.BlockSpec((TM, D), lambda i: (i, 0))] + [hbm] * nsh,
        out_shape=[_S((t_pad, D_IN), _F32), _S((t_pad, D), _BF)] + [_S((N_DEV,) + s.shape, s.dtype) for s in shards],
        scratch_shapes=_sem_shapes(nsh),
        compiler_params=_cp(("arbitrary",)),
    )(h0, g_mix, w_in, *shards)


def _rg_gates(xc, wr_ref, wi_ref, vec_ref):
    xcb = xc.astype(_BF)
    r = _sigmoid(_dot(xcb, wr_ref[...]) + vec_ref[R_BR:R_BR + 1, :])
    ig = _sigmoid(_dot(xcb, wi_ref[...]) + vec_ref[R_BI:R_BI + 1, :])
    nsp8 = -LRU_C * _softplus(-vec_ref[R_LAM:R_LAM + 1, :])
    la = nsp8 * r
    a = jnp.exp(la)
    th = jnp.tanh(la)
    s = jnp.sqrt(-2.0 * th / (1.0 - th))
    return r, ig, a, s, nsp8


def _conv(xbuf, vec_ref):
    acc = vec_ref[R_CONVW:R_CONVW + 1, :] * xbuf[pl.ds(5, TM), :]
    for j in range(1, 4):
        acc = acc + vec_ref[R_CONVW + j:R_CONVW + j + 1, :] * xbuf[pl.ds(5 + j, TM), :]
    return vec_ref[R_CONVB:R_CONVB + 1, :] + acc


def _dot3(m01, x):
    hi = x.astype(_BF)
    r1 = x - hi.astype(_F32)
    mid = r1.astype(_BF)
    lo = (r1 - mid.astype(_F32)).astype(_BF)
    return (_dot(m01, lo) + _dot(m01, mid)) + _dot(m01, hi)


def _chunk_masks():
    row = lax.broadcasted_iota(jnp.int32, (TM, TM), 0)
    col = lax.broadcasted_iota(jnp.int32, (TM, TM), 1)
    shift = HC.bit_length() - 1
    same = lax.shift_right_logical(row, shift) == lax.shift_right_logical(col, shift)
    return same, same & (row >= col), same & (col >= row)


def _per_chunk_rows(x, r):
    return jnp.concatenate([jnp.broadcast_to(x[HC * c + r:HC * c + r + 1, :], (HC, x.shape[1]))
                            for c in range(TM // HC)], axis=0)


def _hg_prep(p_ref, lb, tri_blk):
    hq = p_ref[:, pl.ds(2 * D_RG, D_HG)]
    hf = p_ref[:, pl.ds(2 * D_RG + D_HG, D_HG)]
    sq = _sigmoid(hq)
    q = hq * sq
    sg = _sigmoid(hf)
    f = lb + (1.0 - lb) * sg
    k = 1.0 - f
    b = _dot3(tri_blk, jnp.log(f))
    bm = _per_chunk_rows(b, HC // 2 - 1)
    bl = _per_chunk_rows(b, HC - 1)
    e_q = jnp.exp(b - bm)
    e_k = jnp.exp(bm - b)
    e_b = jnp.exp(b)
    e_l = jnp.exp(bl - b)
    return dict(hq=hq, sq=sq, q=q, sg=sg, f=f, k=k, e_q=e_q, e_k=e_k, e_b=e_b, e_l=e_l,
                qd=q * e_q, kd=k * e_k, qe=q * e_b, ke=k * e_l, e_end=jnp.exp(bl))


def _mixer_fwd(p, wr, wi, vec, hb, g_hg, shards):
    t_pad = p.shape[0]
    nt = t_pad // TM
    nc_t = TM // HC
    nsh = len(shards)

    def body(p_ref, wr_ref, wi_ref, vec_ref, hb_ref, ghg_ref, *rest):
        sh_refs, rest = rest[:nsh], rest[nsh:]
        y_ref, hs_ref, o_ref, sc_ref = rest[:4]
        gath_refs, rest = rest[4:4 + nsh], rest[4 + nsh:]
        xbuf, a_s, b_s, hcar, st, qd_s, kd_s, qe_s, ke_s, v_s, u_s = rest[:11]
        i = pl.program_id(0)
        tail = _carry_gather(_Gather(sh_refs, gath_refs, rest[11:]), i, nt)

        @pl.when(i == 0)
        def _():
            xbuf[pl.ds(0, 8), :] = jnp.zeros((8, D_RG), _F32)
            hcar[...] = jnp.zeros_like(hcar)
            st[...] = jnp.zeros_like(st)

        x = p_ref[:, pl.ds(0, D_RG)]
        xbuf[pl.ds(8, TM), :] = x
        xc = _conv(xbuf, vec_ref)
        xbuf[pl.ds(0, 8), :] = x[TM - 8:, :]
        r, ig, a, s, _ = _rg_gates(xc, wr_ref, wi_ref, vec_ref)
        a_s[...] = a
        b_s[...] = s * (ig * xc)

        def step(t, h):
            h = a_s[pl.ds(t, 1), :] * h + b_s[pl.ds(t, 1), :]
            hs_ref[pl.ds(t, 1), :] = h
            return h

        hcar[pl.ds(0, 1), :] = lax.fori_loop(0, TM, step, hcar[pl.ds(0, 1), :], unroll=8)
        gel, _ = _gelu_parts(p_ref[:, pl.ds(D_RG, D_RG)])
        n, _ = _rms_fwd(gel * hs_ref[...])
        y_ref[:, pl.ds(0, D_RG)] = (n * vec_ref[R_GRG:R_GRG + 1, :]).astype(_BF)

        lb = _sigmoid(hb_ref[0:1, :] - hb_ref[1:2, :])
        _, tri_blk, _ = _chunk_masks()
        q = _hg_prep(p_ref, lb, tri_blk.astype(_BF))
        for name, ref in (("qd", qd_s), ("kd", kd_s), ("qe", qe_s), ("ke", ke_s)):
            ref[...] = q[name].astype(_BF)
        v_s[...] = p_ref[:, pl.ds(2 * D_RG + 2 * D_HG, D_HG)].astype(_BF)
        e_end = q["e_end"]
        causal = (lax.broadcasted_iota(jnp.int32, (HC, HC), 0) >= lax.broadcasted_iota(jnp.int32, (HC, HC), 1))
        for c in range(nc_t):
            for h in range(NH):
                rs, cs = pl.ds(HC * c, HC), pl.ds(HD * h, HD)
                amat = jnp.where(causal, _dot_nt(qd_s[rs, cs], kd_s[rs, cs]), 0.0)
                o_ref[rs, cs] = _dot(amat.astype(_BF), v_s[rs, cs])
                u_s[NH * c + h] = _dot_tn(v_s[rs, cs], ke_s[rs, cs])
        for h in range(NH):
            cs = pl.ds(HD * h, HD)
            s_run = st[h]
            for c in range(nc_t):
                rs = pl.ds(HC * c, HC)
                sc_ref[c, h] = s_run
                o_ref[rs, cs] += _dot_nt(qe_s[rs, cs], s_run.astype(_BF))
                s_run = e_end[HC * c:HC * c + 1, HD * h:HD * (h + 1)] * s_run + u_s[NH * c + h]
            st[h] = s_run
        for h in range(NH):
            cs = pl.ds(HD * h, HD)
            n_o, _ = _rms_fwd(o_ref[:, cs])
            hg = p_ref[:, pl.ds(2 * D_RG + 3 * D_HG + HD * h, HD)]
            y_ref[:, pl.ds(D_RG + HD * h, HD)] = ((n_o * ghg_ref[...]) * (hg * _sigmoid(hg))).astype(_BF)

        tail()

    hbm = pl.BlockSpec(memory_space=pl.ANY)
    return pl.pallas_call(
        body, name="mixer_fwd", grid=(nt,),
        in_specs=[pl.BlockSpec((TM, D_IN), lambda i: (i, 0)), _full((D_RG, D_RG)), _full((D_RG, D_RG)),
                  _full((16, D_RG)), _full((2, D_HG)), _full((1, HD))] + [hbm] * nsh,
        out_specs=[pl.BlockSpec((TM, D), lambda i: (i, 0)), pl.BlockSpec((TM, D_RG), lambda i: (i, 0)),
                   pl.BlockSpec((TM, D_HG), lambda i: (i, 0)),
                   pl.BlockSpec((nc_t, NH, HD, HD), lambda i: (i, 0, 0, 0))] + [hbm] * nsh,
        out_shape=[_S((t_pad, D), _BF), _S((t_pad, D_RG), _F32), _S((t_pad, D_HG), _F32),
                   _S((t_pad // HC, NH, HD, HD), _F32)] + [_S((N_DEV,) + s.shape, s.dtype) for s in shards],
        scratch_shapes=[pltpu.VMEM((TM + 8, D_RG), _F32), pltpu.VMEM((TM, D_RG), _F32),
                        pltpu.VMEM((TM, D_RG), _F32), pltpu.VMEM((8, D_RG), _F32),
                        pltpu.VMEM((NH, HD, HD), _F32)] + [pltpu.VMEM((TM, D_HG), _BF) for _ in range(5)]
        + [pltpu.VMEM((nc_t * NH, HD, HD), _F32)] + _sem_shapes(nsh),
        compiler_params=_cp(("arbitrary",)),
    )(p, wr, wi, vec, hb, g_hg, *shards)


def _outproj(h0, y, w_out, g_ffn):
    t_pad = h0.shape[0]

    def body(h_ref, y_ref, w_ref, g_ref, h1_ref, v_ref):
        h1 = h_ref[...] + _dot(y_ref[...], w_ref[...])
        h1_ref[...] = h1
        n, _ = _rms_fwd(h1)
        v_ref[...] = (n * g_ref[...]).astype(_BF)

    return pl.pallas_call(
        body, name="outproj", grid=(t_pad // TM,),
        in_specs=[pl.BlockSpec((TM, D), lambda i: (i, 0)), pl.BlockSpec((TM, D), lambda i: (i, 0)),
                  _full((D, D)), _full((1, D))],
        out_specs=[pl.BlockSpec((TM, D), lambda i: (i, 0)), pl.BlockSpec((TM, D), lambda i: (i, 0))],
        out_shape=[_S((t_pad, D), _F32), _S((t_pad, D), _BF)],
        compiler_params=_cp(("arbitrary",)),
    )(h0, y, w_out, g_ffn)


def _ffn_loss(v, h1, w_gu, w_down, g_fin, tgt, n_valid):
    t_pad = v.shape[0]

    def body(v_ref, h1_ref, wgu_ref, wd_ref, g_ref, t_ref, gu_ref, act_ref, dh2_ref, dh2b_ref, loss_ref, gfin_ref):
        i = pl.program_id(0)

        @pl.when(i == 0)
        def _():
            loss_ref[...] = jnp.zeros_like(loss_ref)
            gfin_ref[...] = jnp.zeros_like(gfin_ref)

        vb = v_ref[...]
        h2 = h1_ref[...]
        for b in range(4):
            gate = _dot(vb, wgu_ref[b])
            up = _dot(vb, wgu_ref[4 + b])
            gu_ref[b] = gate
            gu_ref[4 + b] = up
            act = ((gate * _sigmoid(gate)) * up).astype(_BF)
            act_ref[b] = act
            h2 = h2 + _dot(act, wd_ref[b])
        n, r = _rms_fwd(h2)
        out = n * g_ref[...]
        row = i * TM + lax.broadcasted_iota(jnp.int32, (TM, 1), 0)
        valid = (row >= N_META) & (row < n_valid)
        err = jnp.where(valid, out - t_ref[...], 0.0)
        loss_ref[...] += (0.5 / D) * jnp.sum(err * err)
        dout = err * (1.0 / D)
        gfin_ref[...] += jnp.sum(dout * n, axis=0, keepdims=True)
        dh2 = _rms_bwd(dout * g_ref[...], n, r)
        dh2_ref[...] = dh2
        dh2b_ref[...] = dh2.astype(_BF)

    return pl.pallas_call(
        body, name="ffn_loss", grid=(t_pad // TM,),
        in_specs=[pl.BlockSpec((TM, D), lambda i: (i, 0)), pl.BlockSpec((TM, D), lambda i: (i, 0)),
                  _const((N_DEV, D, FFB)), _const((4, FFB, D)), _full((1, D)),
                  pl.BlockSpec((TM, D), lambda i: (i, 0))],
        out_specs=[pl.BlockSpec((N_DEV, TM, FFB), lambda i: (0, i, 0)), pl.BlockSpec((4, TM, FFB), lambda i: (0, i, 0)),
                   pl.BlockSpec((TM, D), lambda i: (i, 0)), pl.BlockSpec((TM, D), lambda i: (i, 0)),
                   _full((8, 128)), _full((1, D))],
        out_shape=[_S((N_DEV, t_pad, FFB), _F32), _S((4, t_pad, FFB), _BF), _S((t_pad, D), _F32),
                   _S((t_pad, D), _BF), _S((8, 128), _F32), _S((1, D), _F32)],
        compiler_params=_cp(("arbitrary",)),
    )(v, h1, w_gu, w_down, g_fin, tgt)


def _ffn_bwd(dh2, dh2b, gu, h1, g_ffn, w_gu, w_down, w_out):
    t_pad = dh2.shape[0]

    def body(dh2_ref, dh2b_ref, gu_ref, h1_ref, g_ref, wgu_ref, wd_ref, wo_ref,
             dgu_ref, dh1_ref, dh1b_ref, dy_ref, gffn_ref):
        i = pl.program_id(0)

        @pl.when(i == 0)
        def _():
            gffn_ref[...] = jnp.zeros_like(gffn_ref)

        db = dh2b_ref[...]
        dv = jnp.zeros((TM, D), _F32)
        for b in range(4):
            dact = _dot_nt(db, wd_ref[b])
            gate = gu_ref[b]
            up = gu_ref[4 + b]
            sg = _sigmoid(gate)
            dgate = ((dact * up) * _dsilu(gate, sg)).astype(_BF)
            dup = (dact * (gate * sg)).astype(_BF)
            dgu_ref[b] = dgate
            dgu_ref[4 + b] = dup
            dv = dv + _dot_nt(dgate, wgu_ref[b]) + _dot_nt(dup, wgu_ref[4 + b])
        n, r = _rms_fwd(h1_ref[...])
        gffn_ref[...] += jnp.sum(dv * n, axis=0, keepdims=True)
        dh1 = dh2_ref[...] + _rms_bwd(dv * g_ref[...], n, r)
        dh1_ref[...] = dh1
        dh1b = dh1.astype(_BF)
        dh1b_ref[...] = dh1b
        dy_ref[...] = _dot_nt(dh1b, wo_ref[...])

    tile = pl.BlockSpec((TM, D), lambda i: (i, 0))
    return pl.pallas_call(
        body, name="ffn_bwd", grid=(t_pad // TM,),
        in_specs=[tile, tile, pl.BlockSpec((N_DEV, TM, FFB), lambda i: (0, i, 0)), tile, _full((1, D)),
                  _const((N_DEV, D, FFB)), _const((4, FFB, D)), _const((D, D))],
        out_specs=[pl.BlockSpec((N_DEV, TM, FFB), lambda i: (0, i, 0)), tile, tile, tile, _full((1, D))],
        out_shape=[_S((N_DEV, t_pad, FFB), _BF), _S((t_pad, D), _F32), _S((t_pad, D), _BF),
                   _S((t_pad, D), _F32), _S((1, D), _F32)],
        compiler_params=_cp(("arbitrary",)),
    )(dh2, dh2b, gu, h1, g_ffn, w_gu, w_down, w_out)


def _mixer_bwd(p, hs, o, sc, dy, wr, wi, vec, hb, g_hg, scatter):
    t_pad = p.shape[0]
    nt = t_pad // TM
    nc_t = TM // HC
    nsc = len(scatter)

    def rev(i):
        return nt - 1 - i

    def body(p_ref, pprev_ref, hs_ref, hprev_ref, o_ref, sc_ref, dy_ref, wr_ref, wi_ref, vec_ref, hb_ref, ghg_ref,
             *rest):
        send_refs, rest = rest[:nsc], rest[nsc:]
        dp_ref, gvec_ref, gw_ref = rest[:3]
        recv_refs, rest = rest[3:3 + nsc], rest[3 + nsc:]
        xbuf, hbuf, dbuf, a_s, g_s, ccar, dst = rest[:7]
        qd_s, kd_s, qe_s, ke_s, v_s, do_s, dqd_s, dkd_s, dqe_s, dke_s, dv_s, w_s, dend_s = rest[7:20]
        exchange = _Exchange(send_refs, [], recv_refs, rest[20:])
        i = pl.program_id(0)
        first_tile = i == nt - 1

        @pl.when(i == 0)
        def _():
            exchange.start()
            gvec_ref[...] = jnp.zeros_like(gvec_ref)
            gw_ref[...] = jnp.zeros_like(gw_ref)
            dbuf[pl.ds(TM, 8), :] = jnp.zeros((8, D_RG), _F32)
            ccar[...] = jnp.zeros_like(ccar)
            dst[...] = jnp.zeros_like(dst)

        def acc(row, val):
            gvec_ref[row:row + 1, :] += jnp.sum(val, axis=0, keepdims=True)

        keep = jnp.where(first_tile, 0.0, 1.0)
        x = p_ref[:, pl.ds(0, D_RG)]
        xbuf[pl.ds(0, 8), :] = pprev_ref[...] * keep
        xbuf[pl.ds(8, TM), :] = x
        xc = _conv(xbuf, vec_ref)
        r, ig, a, s, nsp8 = _rg_gates(xc, wr_ref, wi_ref, vec_ref)
        h = hs_ref[...]
        hbuf[pl.ds(0, 8), :] = hprev_ref[...] * keep
        hbuf[pl.ds(8, TM), :] = h
        hm1 = hbuf[pl.ds(7, TM), :]
        gr = p_ref[:, pl.ds(D_RG, D_RG)]
        gel, dgel = _gelu_parts(gr)
        n, rr = _rms_fwd(gel * h)
        dyn = dy_ref[:, pl.ds(0, D_RG)]
        acc(R_GRG, dyn * n)
        dpre = _rms_bwd(dyn * vec_ref[R_GRG:R_GRG + 1, :], n, rr)
        dp_ref[:, pl.ds(D_RG, D_RG)] = ((dpre * h) * dgel).astype(_BF)
        a_s[...] = a
        g_s[...] = dpre * gel

        def step(k, c):
            t = TM - 1 - k
            g = g_s[pl.ds(t, 1), :] + c
            g_s[pl.ds(t, 1), :] = g
            return a_s[pl.ds(t, 1), :] * g

        ccar[pl.ds(0, 1), :] = lax.fori_loop(0, TM, step, ccar[pl.ds(0, 1), :], unroll=8)
        gt = g_s[...]
        da = gt * hm1
        ixc = ig * xc
        ds = gt * ixc
        dig = (gt * s) * xc
        dxc = (gt * s) * ig
        dla = da * a - ds * ((a * a) / s)
        lam = vec_ref[R_LAM:R_LAM + 1, :]
        gvec_ref[R_LAM:R_LAM + 1, :] += jnp.sum(dla * r, axis=0, keepdims=True) * (LRU_C * _sigmoid(-lam))
        dzr = (dla * nsp8) * (r * (1.0 - r))
        dzi = dig * (ig * (1.0 - ig))
        acc(R_BR, dzr)
        acc(R_BI, dzi)
        xcb = xc.astype(_BF)
        dzrb = dzr.astype(_BF)
        dzib = dzi.astype(_BF)
        gw_ref[0] += _dot_tn(xcb, dzrb)
        gw_ref[1] += _dot_tn(xcb, dzib)
        dxc = dxc + _dot_nt(dzrb, wr_ref[...]) + _dot_nt(dzib, wi_ref[...])
        acc(R_CONVB, dxc)
        for j in range(4):
            acc(R_CONVW + j, dxc * xbuf[pl.ds(5 + j, TM), :])
        dbuf[pl.ds(0, TM), :] = dxc
        dx = vec_ref[R_CONVW + 3:R_CONVW + 4, :] * dxc
        for j in range(3):
            dx = dx + vec_ref[R_CONVW + j:R_CONVW + j + 1, :] * dbuf[pl.ds(3 - j, TM), :]
        dbuf[pl.ds(TM, 8), :] = dxc[0:8, :]
        dp_ref[:, pl.ds(0, D_RG)] = dx.astype(_BF)

        lb = _sigmoid(hb_ref[0:1, :] - hb_ref[1:2, :])
        same, tri_blk, triu_blk = _chunk_masks()
        q = _hg_prep(p_ref, lb, tri_blk.astype(_BF))
        qdb, kdb = q["qd"].astype(_BF), q["kd"].astype(_BF)
        qd_s[...] = qdb
        kd_s[...] = kdb
        qe_s[...] = q["qe"].astype(_BF)
        ke_s[...] = q["ke"].astype(_BF)
        v_s[...] = p_ref[:, pl.ds(2 * D_RG + 2 * D_HG, D_HG)].astype(_BF)
        e_end = q["e_end"]
        ghg = ghg_ref[...]
        for h in range(NH):
            cs = pl.ds(HD * h, HD)
            hg = p_ref[:, pl.ds(2 * D_RG + 3 * D_HG + HD * h, HD)]
            sh = _sigmoid(hg)
            n_o, r_o = _rms_fwd(o_ref[:, cs])
            dyh = dy_ref[:, pl.ds(D_RG + HD * h, HD)]
            dp_ref[:, pl.ds(2 * D_RG + 3 * D_HG + HD * h, HD)] = ((dyh * (n_o * ghg)) * _dsilu(hg, sh)).astype(_BF)
            dn = dyh * (hg * sh)
            gvec_ref[R_GHG:R_GHG + 1, pl.ds(0, HD)] += jnp.sum(dn * n_o, axis=0, keepdims=True)
            do_s[:, cs] = _rms_bwd(dn * ghg, n_o, r_o).astype(_BF)
        causal = (lax.broadcasted_iota(jnp.int32, (HC, HC), 0) >= lax.broadcasted_iota(jnp.int32, (HC, HC), 1))
        for c in range(nc_t):
            for h in range(NH):
                rs, cs = pl.ds(HC * c, HC), pl.ds(HD * h, HD)
                qd_c, kd_c, do_c = qd_s[rs, cs], kd_s[rs, cs], do_s[rs, cs]
                amat = jnp.where(causal, _dot_nt(qd_c, kd_c), 0.0).astype(_BF)
                da_m = jnp.where(causal, _dot_nt(do_c, v_s[rs, cs]), 0.0).astype(_BF)
                dqd_s[rs, cs] = _dot(da_m, kd_c)
                dkd_s[rs, cs] = _dot_tn(da_m, qd_c)
                dqe_s[rs, cs] = _dot(do_c, sc_ref[c, h].astype(_BF))
                dv_s[rs, cs] = _dot_tn(amat, do_c)
                w_s[NH * c + h] = _dot_tn(do_c, qe_s[rs, cs])
        for h in range(NH):
            cs = pl.ds(HD * h, HD)
            d_run = dst[h]
            for c in reversed(range(nc_t)):
                rs = pl.ds(HC * c, HC)
                d_b = d_run.astype(_BF)
                dke_s[rs, cs] = _dot(v_s[rs, cs], d_b)
                dp_ref[rs, pl.ds(2 * D_RG + 2 * D_HG + HD * h, HD)] = (
                    dv_s[rs, cs] + _dot_nt(ke_s[rs, cs], d_b)).astype(_BF)
                dend_s[pl.ds(c, 1), cs] = jnp.sum(sc_ref[c, h] * d_run, axis=0, keepdims=True)
                d_run = w_s[NH * c + h] + e_end[HC * c:HC * c + 1, HD * h:HD * (h + 1)] * d_run
            dst[h] = d_run
        dqd, dkd, dqe, dke = dqd_s[...], dkd_s[...], dqe_s[...], dke_s[...]
        dq = dqd * q["e_q"] + dqe * q["e_b"]
        dk = dkd * q["e_k"] + dke * q["e_l"]
        dkeke = dke * q["ke"]
        db = dqd * qdb.astype(_F32) - dkd * kdb.astype(_F32) + dqe * q["qe"] - dkeke
        d_end = jnp.concatenate([jnp.broadcast_to(dend_s[pl.ds(c, 1), :], (HC, D_HG)) for c in range(nc_t)], axis=0)
        dlf = _dot3(triu_blk.astype(_BF), db) + _dot3(same.astype(_BF), dkeke) + d_end * e_end
        df = dlf / q["f"] - dk
        sg = q["sg"]
        gvec_ref[R_HB0:R_HB0 + 1, :] += jnp.sum(df * (1.0 - sg), axis=0, keepdims=True)
        dp_ref[:, pl.ds(2 * D_RG, D_HG)] = (dq * _dsilu(q["hq"], q["sq"])).astype(_BF)
        dp_ref[:, pl.ds(2 * D_RG + D_HG, D_HG)] = ((df * (1.0 - lb)) * (sg * (1.0 - sg))).astype(_BF)

        @pl.when(i == nt - 1)
        def _():
            glb = gvec_ref[R_HB0:R_HB0 + 1, :] * (lb * (1.0 - lb))
            gvec_ref[R_HB0:R_HB0 + 1, :] = glb
            gvec_ref[R_HB1:R_HB1 + 1, :] = -glb
            exchange.finish()

    hbm = pl.BlockSpec(memory_space=pl.ANY)
    return pl.pallas_call(
        body, name="mixer_bwd", grid=(nt,),
        in_specs=[pl.BlockSpec((TM, D_IN), lambda i: (rev(i), 0)),
                  pl.BlockSpec((8, D_RG), lambda i: (jnp.maximum(rev(i) * (TM // 8) - 1, 0), 0)),
                  pl.BlockSpec((TM, D_RG), lambda i: (rev(i), 0)),
                  pl.BlockSpec((8, D_RG), lambda i: (jnp.maximum(rev(i) * (TM // 8) - 1, 0), 0)),
                  pl.BlockSpec((TM, D_HG), lambda i: (rev(i), 0)),
                  pl.BlockSpec((nc_t, NH, HD, HD), lambda i: (rev(i), 0, 0, 0)),
                  pl.BlockSpec((TM, D), lambda i: (rev(i), 0)),
                  _full((D_RG, D_RG)), _full((D_RG, D_RG)), _full((16, D_RG)), _full((2, D_HG)), _full((1, HD))]
        + [hbm] * nsc,
        out_specs=[pl.BlockSpec((TM, D_IN), lambda i: (rev(i), 0)), _full((16, D_RG)), _full((2, D_RG, D_RG))]
        + [hbm] * nsc,
        out_shape=[_S((t_pad, D_IN), _BF), _S((16, D_RG), _F32), _S((2, D_RG, D_RG), _F32)]
        + [_S(s.shape, s.dtype) for s in scatter],
        scratch_shapes=[pltpu.VMEM((TM + 8, D_RG), _F32), pltpu.VMEM((TM + 8, D_RG), _F32),
                        pltpu.VMEM((TM + 8, D_RG), _F32), pltpu.VMEM((TM, D_RG), _F32),
                        pltpu.VMEM((TM, D_RG), _F32), pltpu.VMEM((8, D_RG), _F32),
                        pltpu.VMEM((NH, HD, HD), _F32)]
        + [pltpu.VMEM((TM, D_HG), _BF) for _ in range(6)] + [pltpu.VMEM((TM, D_HG), _F32) for _ in range(5)]
        + [pltpu.VMEM((nc_t * NH, HD, HD), _F32), pltpu.VMEM((8, D_HG), _F32)] + _sem_shapes(nsc),
        compiler_params=_cp(("arbitrary",)),
    )(p, p, hs, hs, o, sc, dy, wr, wi, vec, hb, g_hg, *scatter)


def _inproj_bwd_send(dp, w_in, h0, dh1, g_mix, u, order, gffn, gfin, to_all):
    t_pad = dp.shape[0]
    rb = t_pad // (2 * N_DEV)
    na = len(to_all)

    def body(order_ref, dpc_ref, dpr_ref, u_ref, w_ref, h_ref, dh1_ref, g_ref, gffn_ref, gfin_ref, *rest):
        all_in = rest[:na]
        dh0_ref, recv_ref = rest[na:na + 2]
        all_out = rest[na + 2:2 * na + 2]
        alla_ref = rest[2 * na + 2]
        buf, pack, blk_send, blk_recv, blk_local = rest[2 * na + 3:2 * na + 8]
        exchange = _Exchange([], all_in, all_out, rest[2 * na + 8:2 * na + 11])
        last = _Exchange([], [pack], [alla_ref], rest[2 * na + 11:])
        s, k = pl.program_id(0), pl.program_id(1)
        x, y, c = _coords()
        me = 4 * x + 2 * y + c

        def send(step):
            r = _SEND_ORDER[step]
            return pltpu.make_async_remote_copy(
                src_ref=buf.at[step % 2], dst_ref=recv_ref.at[me], send_sem=blk_send.at[step], recv_sem=blk_recv.at[r - 1],
                device_id=(x ^ (r >> 2), y ^ ((r >> 1) & 1), c ^ (r & 1)), device_id_type=_MESH)

        @pl.when((s == 0) & (k == 0))
        def _():
            exchange.start()
            pack[...] = jnp.zeros_like(pack)

        @pl.when(k == 0)
        def _():
            for step in range(2, N_DEV):
                @pl.when(s == step)
                def _(step=step):
                    send(step - 2).wait_send()

            buf[s % 2] = _dot_tn(u_ref[...], dpc_ref[...]).astype(_BF)

            for step in range(N_DEV - 1):
                @pl.when(s == step)
                def _(step=step):
                    send(step).start()

        du = jnp.zeros((rb, D), _F32)
        for j in range(N_DEV):
            du = du + _dot_nt(dpr_ref[:, WIN_B * j:WIN_B * (j + 1)], w_ref[j])
        n, r = _rms_fwd(h_ref[...])
        pack[R_GMIX:R_GMIX + 1, :] += jnp.sum(du * n, axis=0, keepdims=True)
        dh0 = dh1_ref[...] + _rms_bwd(du * g_ref[...], n, r)
        dh0_ref[...] = dh0

        @pl.when((s == 0) & (k == 0))
        def _():
            pack[R_META:R_META + N_META, :] = dh0[0:N_META, :]

        @pl.when((s == N_DEV - 1) & (k == 1))
        def _():
            pack[R_GFFN:R_GFFN + 1, :] = gffn_ref[...]
            pack[R_GFIN:R_GFIN + 1, :] = gfin_ref[...]
            last.start()
            mine = pltpu.make_async_copy(buf.at[(N_DEV - 1) % 2], recv_ref.at[me], blk_local.at[0])
            mine.start()
            send(N_DEV - 2).wait_send()
            for r in range(1, N_DEV):
                px, py, pc = x ^ (r >> 2), y ^ ((r >> 1) & 1), c ^ (r & 1)
                pltpu.make_async_remote_copy(
                    src_ref=buf.at[0], dst_ref=recv_ref.at[4 * px + 2 * py + pc], send_sem=blk_send.at[0],
                    recv_sem=blk_recv.at[r - 1], device_id=(px, py, pc), device_id_type=_MESH).wait_recv()
            mine.wait()
            exchange.finish()
            last.finish()

    hbm = pl.BlockSpec(memory_space=pl.ANY)
    rows = pl.BlockSpec((rb, D), lambda s, k, order: (2 * s + k, 0))
    one = pl.BlockSpec((1, D), lambda s, k, order: (0, 0))
    res = pl.pallas_call(
        body, name="inproj_bwd_send",
        grid_spec=pltpu.PrefetchScalarGridSpec(
            num_scalar_prefetch=1, grid=(N_DEV, 2),
            in_specs=[pl.BlockSpec((t_pad, WIN_B), lambda s, k, order: (0, order[s])),
                      pl.BlockSpec((rb, D_IN), lambda s, k, order: (2 * s + k, 0)),
                      pl.BlockSpec((t_pad, D), lambda s, k, order: (0, 0), pipeline_mode=pl.Buffered(1)),
                      pl.BlockSpec((N_DEV, D, WIN_B), lambda s, k, order: (0, 0, 0), pipeline_mode=pl.Buffered(1)),
                      rows, rows, one, one, one] + [hbm] * na,
            out_specs=[rows] + [hbm] * (na + 2),
            scratch_shapes=[pltpu.VMEM((2, D, WIN_B), _BF), pltpu.VMEM((24, D), _F32),
                            pltpu.SemaphoreType.DMA((N_DEV - 1,)), pltpu.SemaphoreType.DMA((N_DEV - 1,)),
                            pltpu.SemaphoreType.DMA((1,))] + _sem_shapes(na) + _sem_shapes(1)),
        out_shape=[_S((t_pad, D), _F32), _S((N_DEV, D, WIN_B), _BF)]
        + [_S((N_DEV,) + g.shape, g.dtype) for g in to_all] + [_S((N_DEV, 24, D), _F32)],
        compiler_params=_cp(("arbitrary", "arbitrary")),
    )(order, dp, dp, u, w_in, h0, dh1, g_mix, gffn, gfin, *to_all)
    return res


def _wgrad(name, a, b, a_spec, b_spec, n_blocks, out_block, scatter=()):
    nsc = len(scatter)

    def body(a_ref, b_ref, *rest):
        o_ref = rest[nsc]
        j = pl.program_id(0)
        if nsc:
            exchange = _Exchange(rest[:nsc], [], rest[nsc + 1:2 * nsc + 1], rest[2 * nsc + 1:])

            @pl.when(j == 0)
            def _():
                exchange.start()

        av = a_ref[0] if len(a_ref.shape) == 3 else a_ref[...]
        bv = b_ref[0] if len(b_ref.shape) == 3 else b_ref[...]
        o_ref[0] = _dot_tn(av, bv).astype(_BF)

        if nsc:
            @pl.when(j == n_blocks - 1)
            def _():
                exchange.finish()

    hbm = pl.BlockSpec(memory_space=pl.ANY)
    res = pl.pallas_call(
        body, name=name, grid=(n_blocks,),
        in_specs=[a_spec, b_spec] + [hbm] * nsc,
        out_specs=[pl.BlockSpec((1,) + out_block, lambda j: (j, 0, 0))] + [hbm] * nsc,
        out_shape=[_S((n_blocks,) + out_block, _BF)] + [_S(s.shape, s.dtype) for s in scatter],
        scratch_shapes=_sem_shapes(nsc) if nsc else [],
        compiler_params=_cp(("arbitrary",)),
    )(a, b, *scatter)
    return res if nsc else res[0]


def _coords():
    return lax.axis_index("x"), lax.axis_index("y"), lax.axis_index("c")


def _sem_shapes(na):
    return [pltpu.SemaphoreType.DMA((7 * na,)), pltpu.SemaphoreType.DMA((7 * na,)), pltpu.SemaphoreType.DMA((na,))]


class _Gather:
    def __init__(self, srcs, outs, sems):
        self.srcs, self.outs = srcs, outs
        self.send_sems, self.recv_sems, self.local_sems = sems
        self.na = len(srcs)
        x, y, c = _coords()
        self.pos = (x, y, c)
        self.me = 4 * x + 2 * y + c
        self.sibling = (x, y, 1 - c)
        self.chips = [(1 - x, y), (x, 1 - y), (1 - x, 1 - y)]

    @staticmethod
    def _slot(px, py, pc):
        return 4 * px + 2 * py + pc

    def _copy(self, a, k, block, to, own=False):
        return pltpu.make_async_remote_copy(
            src_ref=self.srcs[a] if own else self.outs[a].at[block], dst_ref=self.outs[a].at[block],
            send_sem=self.send_sems.at[7 * a + k], recv_sem=self.recv_sems.at[7 * a + k],
            device_id=to, device_id_type=_MESH)

    def _mine(self, a):
        return pltpu.make_async_copy(self.srcs[a], self.outs[a].at[self.me], self.local_sems.at[a])

    def _first(self):
        c = self.pos[2]
        cps = []
        for a in range(self.na):
            cps.append(self._copy(a, 0, self.me, self.sibling, own=True))
            cps += [self._copy(a, 1 + j, self.me, (*chip, c), own=True) for j, chip in enumerate(self.chips)]
        return cps

    def _passed(self):
        c = self.pos[2]
        return [self._copy(a, 4 + j, self._slot(*chip, c), self.sibling)
                for j, chip in enumerate(self.chips) for a in range(self.na)]

    def start(self):
        for a in range(self.na):
            self._mine(a).start()
        for cp in self._first():
            cp.start()

    def forward(self, j):
        c = self.pos[2]
        chip = self.chips[j]
        for a in range(self.na):
            self._copy(a, 1 + j, self._slot(*chip, c), self.pos).wait_recv()
            self._copy(a, 4 + j, self._slot(*chip, c), self.sibling).start()

    def finish(self):
        x, y, c = self.pos
        for a in range(self.na):
            self._copy(a, 0, self._slot(x, y, 1 - c), self.pos).wait_recv()
        for j, chip in enumerate(self.chips):
            for a in range(self.na):
                self._copy(a, 4 + j, self._slot(*chip, 1 - c), self.pos).wait_recv()
        for cp in self._first() + self._passed():
            cp.wait_send()
        for a in range(self.na):
            self._mine(a).wait()


class _Exchange:
    def __init__(self, scatter, gather, outs, sems):
        self.ins = list(scatter) + list(gather)
        self.ns, self.na = len(scatter), len(scatter) + len(gather)
        self.outs = outs
        self.send_sems, self.recv_sems, self.local_sems = sems
        x, y, c = _coords()
        self.pos = (x, y, c)
        self.me = 4 * x + 2 * y + c

    def _peer(self, r):
        x, y, c = self.pos
        return x ^ (r >> 2), y ^ ((r >> 1) & 1), c ^ (r & 1)

    def _src(self, a, block):
        return self.ins[a].at[block] if a < self.ns else self.ins[a]

    def _local(self, a):
        return pltpu.make_async_copy(self._src(a, self.me), self.outs[a].at[self.me], self.local_sems.at[a])

    def _send(self, a, r):
        px, py, pc = self._peer(r)
        return pltpu.make_async_remote_copy(
            src_ref=self._src(a, 4 * px + 2 * py + pc), dst_ref=self.outs[a].at[self.me],
            send_sem=self.send_sems.at[7 * a + r - 1], recv_sem=self.recv_sems.at[7 * a + r - 1],
            device_id=(px, py, pc), device_id_type=_MESH)

    def _recv(self, a, r):
        px, py, pc = self._peer(r)
        return pltpu.make_async_remote_copy(
            src_ref=self._src(a, self.me), dst_ref=self.outs[a].at[4 * px + 2 * py + pc],
            send_sem=self.send_sems.at[7 * a + r - 1], recv_sem=self.recv_sems.at[7 * a + r - 1],
            device_id=(px, py, pc), device_id_type=_MESH)

    def start(self):
        for a in range(self.na):
            self._local(a).start()
        for r in range(1, N_DEV):
            for a in range(self.na):
                self._send(a, r).start()

    def finish(self):
        for r in range(1, N_DEV):
            for a in range(self.na):
                self._recv(a, r).wait_recv()
        for r in range(1, N_DEV):
            for a in range(self.na):
                self._send(a, r).wait_send()
        for a in range(self.na):
            self._local(a).wait()


def _allgather_first(gather_f32, cast_f32, gather_dtypes):
    ng, nc = len(gather_f32), len(cast_f32)

    def body(*refs):
        ins, cins = refs[:ng], refs[ng:ng + nc]
        outs, couts = refs[ng + nc:2 * ng + nc], refs[2 * ng + nc:2 * ng + 2 * nc]
        stage = refs[2 * ng + 2 * nc:3 * ng + 2 * nc]
        sems = refs[3 * ng + 2 * nc:]
        for a in range(ng):
            stage[a][...] = ins[a][...].astype(gather_dtypes[a])
        g = _Gather(stage, outs, sems)
        g.start()
        for a in range(nc):
            couts[a][...] = cins[a][...].astype(_BF)
        for j in range(3):
            g.forward(j)
        g.finish()

    vm = pl.BlockSpec(memory_space=pltpu.VMEM)
    return pl.pallas_call(
        body, name="allgather_first",
        in_specs=[vm] * (ng + nc),
        out_specs=[pl.BlockSpec(memory_space=pl.ANY)] * ng + [vm] * nc,
        out_shape=[_S((N_DEV,) + l.shape, dt) for l, dt in zip(gather_f32, gather_dtypes)]
        + [_S(l.shape, _BF) for l in cast_f32],
        scratch_shapes=[pltpu.VMEM(l.shape, dt) for l, dt in zip(gather_f32, gather_dtypes)] + _sem_shapes(ng),
        compiler_params=pltpu.CompilerParams(vmem_limit_bytes=VMEM_LIMIT),
    )(*gather_f32, *cast_f32)


def _adamw_math(w, g, m, v):
    m2 = ADAM_B1 * m + (1.0 - ADAM_B1) * g
    v2 = ADAM_B2 * v + (1.0 - ADAM_B2) * (g * g)
    m_hat = m2 / (1.0 - ADAM_B1 ** ADAM_STEP)
    v_hat = v2 / (1.0 - ADAM_B2 ** ADAM_STEP)
    delta = -ADAM_LR * (m_hat / (jnp.sqrt(v_hat) + ADAM_EPS) + ADAM_WD * w)
    return delta, m2, v2


def _adamw_big(name, recv, w, m, v, rows):
    r_all, c_all = w.shape

    def body(r_ref, w_ref, m_ref, v_ref, g_out, d_out, m_out, v_out):
        g = r_ref[0].astype(_F32)
        for k in range(1, N_DEV):
            g = g + r_ref[k].astype(_F32)
        delta, m2, v2 = _adamw_math(w_ref[...], g, m_ref[...], v_ref[...])
        g_out[...] = g
        d_out[...] = delta
        m_out[...] = m2
        v_out[...] = v2

    tile = pl.BlockSpec((rows, c_all), lambda i: (i, 0))
    return pl.pallas_call(
        body, name=name, grid=(r_all // rows,),
        in_specs=[pl.BlockSpec((N_DEV, rows, c_all), lambda i: (0, i, 0)), tile, tile, tile],
        out_specs=[tile] * 4,
        out_shape=[_S(w.shape, _F32)] * 4,
        compiler_params=_cp(("arbitrary",)),
    )(recv, w, m, v)


def _adamw_small(gathered, slices, wmv):
    ng, npar = len(gathered), len(slices)

    def body(*refs):
        g_refs = refs[:ng]
        wmv_refs = refs[ng:ng + 3 * npar]
        outs = refs[ng + 3 * npar:]
        for i, (ai, r0, nr, ncol) in enumerate(slices):
            g = g_refs[ai][0, pl.ds(r0, nr), pl.ds(0, ncol)]
            for k in range(1, N_DEV):
                g = g + g_refs[ai][k, pl.ds(r0, nr), pl.ds(0, ncol)]
            w_ref, m_ref, v_ref = wmv_refs[3 * i:3 * i + 3]
            delta, m2, v2 = _adamw_math(w_ref[...], g, m_ref[...], v_ref[...])
            outs[4 * i][...] = g
            outs[4 * i + 1][...] = delta
            outs[4 * i + 2][...] = m2
            outs[4 * i + 3][...] = v2

    flat = [t for trip in wmv for t in trip]
    out_shape = []
    for w, _, _ in wmv:
        out_shape += [_S(w.shape, _F32)] * 4
    return pl.pallas_call(
        body, name="adamw_small", out_shape=out_shape,
        compiler_params=pltpu.CompilerParams(vmem_limit_bytes=VMEM_LIMIT),
    )(*gathered, *flat)


def _block_diag(w):
    eye = jnp.eye(8, dtype=w.dtype)
    return (w[:, :, None, :] * eye[:, None, :, None]).reshape(D_RG, D_RG)


def _diag_blocks(g):
    return jnp.concatenate([g[64 * h:64 * (h + 1), 64 * h:64 * (h + 1)] for h in range(8)], axis=0)


def _local_step(x, tgt, meta, g_mix, w_in, vec, wr, wi, hb, g_hg, w_out_l, g_ffn, w_gu_l, w_down_l, g_fin):
    seq = x.shape[0]
    n_valid = N_META + seq
    t_pad = -(-n_valid // TM) * TM
    h0 = jnp.concatenate([meta, x, jnp.zeros((t_pad - n_valid, D), _F32)], axis=0)
    tgt_p = jnp.concatenate([jnp.zeros((N_META, D), _F32), tgt, jnp.zeros((t_pad - n_valid, D), _F32)], axis=0)

    p, u, w_out, w_down = _inproj(h0, g_mix, w_in, [w_out_l, w_down_l])
    y, hs, o, sc, w_gu = _mixer_fwd(p, wr, wi, vec, hb, g_hg, [w_gu_l])
    w_out = w_out.reshape(D, D)
    w_down = w_down.reshape(4, FFB, D)
    h1, v = _outproj(h0, y, w_out, g_ffn)
    gu, act, dh2, dh2b, loss, gfin = _ffn_loss(v, h1, w_gu, w_down, g_fin, tgt_p, n_valid)

    dgu, dh1, dh1b, dy, gffn = _ffn_bwd(dh2, dh2b, gu, h1, g_ffn, w_gu, w_down, w_out)
    g_wdown = _wgrad("wgrad_down", act, dh2b, pl.BlockSpec((1, t_pad, FFB), lambda j: (j, 0, 0)),
                     pl.BlockSpec((t_pad, D), lambda j: (0, 0)), 4, (FFB, D))
    g_wgu, r_wdown = _wgrad("wgrad_gate_up", v, dgu, pl.BlockSpec((t_pad, D), lambda j: (0, 0)),
                            pl.BlockSpec((1, t_pad, FFB), lambda j: (j, 0, 0)), N_DEV, (D, FFB),
                            scatter=[g_wdown.reshape(N_DEV, D_FF // N_DEV, D)])
    g_wout = _wgrad("wgrad_out", y, dh1b, pl.BlockSpec((t_pad, D // N_DEV), lambda j: (0, j)),
                    pl.BlockSpec((t_pad, D), lambda j: (0, 0)), N_DEV, (D // N_DEV, D))
    dp, gvec, gw, r_wgu, r_wout = _mixer_bwd(p, hs, o, sc, dy, wr, wi, vec, hb, g_hg, [g_wgu, g_wout])
    pack_c = jnp.concatenate([_diag_blocks(gw[0]), _diag_blocks(gw[1])], axis=0)
    me = 4 * lax.axis_index("x") + 2 * lax.axis_index("y") + lax.axis_index("c")
    order = (me ^ jnp.array(_SEND_ORDER, jnp.int32)).astype(jnp.int32)
    dh0, r_win, all_b, all_c, all_a = _inproj_bwd_send(dp, w_in, h0, dh1, g_mix, u, order, gffn, gfin, [gvec, pack_c])
    return loss, dh0, (r_win, r_wgu, r_wout, r_wdown), (all_a, all_b, all_c)


def kernel(x, meta_tokens, mix_norm_g, w_in, conv_w, conv_b, w_rgate, b_rgate, w_igate, b_igate, lru_lambda, rg_norm_g, hg_lower_bound, hg_norm_g, w_out, ffn_norm_g, w_gate_up, w_down, final_norm_g, loss_target, m_meta_tokens, m_mix_norm_g, m_w_in, m_conv_w, m_conv_b, m_w_rgate, m_b_rgate, m_w_igate, m_b_igate, m_lru_lambda, m_rg_norm_g, m_hg_lower_bound, m_hg_norm_g, m_w_out, m_ffn_norm_g, m_w_gate_up, m_w_down, m_final_norm_g, v_meta_tokens, v_mix_norm_g, v_w_in, v_conv_w, v_conv_b, v_w_rgate, v_b_rgate, v_w_igate, v_b_igate, v_lru_lambda, v_rg_norm_g, v_hg_lower_bound, v_hg_norm_g, v_w_out, v_ffn_norm_g, v_w_gate_up, v_w_down, v_final_norm_g):
    seq = x.shape[1]
    me = 4 * lax.axis_index("x") + 2 * lax.axis_index("y") + lax.axis_index("c")

    small_l = jnp.concatenate([meta_tokens, jnp.pad(conv_w[0], ((0, 4), (0, 64)))], axis=0)
    w_in_g, small_g, w_gu_l, w_out_l, w_down_l = _allgather_first(
        [w_in[0], small_l], [w_gate_up[0], w_out[0], w_down[0]], [_BF, _F32])
    meta_full = jnp.transpose(small_g[:, :N_META, :], (1, 0, 2)).reshape(N_META, D)
    conv_w_full = jnp.transpose(small_g[:, N_META:N_META + 4, :64], (1, 0, 2)).reshape(4, D_RG)
    vec = jnp.concatenate([conv_b, b_rgate, b_igate, lru_lambda, rg_norm_g, jnp.zeros((3, D_RG), _F32),
                           conv_w_full, jnp.zeros((4, D_RG), _F32)], axis=0)
    wr = _block_diag(w_rgate[0]).astype(_BF)
    wi = _block_diag(w_igate[0]).astype(_BF)

    loss, dh0, (r_win, r_wgu, r_wout, r_wdown), (all_a, all_b, all_c) = _local_step(
        x[0], loss_target[0], meta_full, mix_norm_g, w_in_g, vec, wr, wi, hg_lower_bound, hg_norm_g,
        w_out_l, ffn_norm_g, w_gu_l, w_down_l, final_norm_g.reshape(1, D))
    grad_x = dh0[N_META:N_META + seq][None]

    outs = {}
    outs["w_in"] = _adamw_big("adamw_w_in", r_win, w_in[0], m_w_in[0], v_w_in[0], 256)
    outs["w_gate_up"] = _adamw_big("adamw_w_gate_up", r_wgu, w_gate_up[0], m_w_gate_up[0], v_w_gate_up[0], 256)
    outs["w_out"] = _adamw_big("adamw_w_out", r_wout, w_out[0], m_w_out[0], v_w_out[0], 128)
    outs["w_down"] = _adamw_big("adamw_w_down", r_wdown, w_down[0], m_w_down[0], v_w_down[0], 176)

    meta_part = lax.dynamic_slice_in_dim(all_a[:, R_META:R_META + N_META, :], me * 128, 128, axis=2)
    convw_part = lax.dynamic_slice_in_dim(all_b[:, R_CONVW:R_CONVW + 4, :], me * 64, 64, axis=2)
    gathered = [all_a, all_b, all_c, meta_part, convw_part]
    small_params = [
        ("meta_tokens", (3, 0, N_META, 128), (meta_tokens, m_meta_tokens, v_meta_tokens), (N_META, 128)),
        ("mix_norm_g", (0, R_GMIX, 1, D), (mix_norm_g, m_mix_norm_g, v_mix_norm_g), (1, D)),
        ("conv_w", (4, 0, 4, 64), (conv_w, m_conv_w, v_conv_w), (4, 64)),
        ("conv_b", (1, R_CONVB, 1, D_RG), (conv_b, m_conv_b, v_conv_b), (1, D_RG)),
        ("w_rgate", (2, 0, 512, 64), (w_rgate, m_w_rgate, v_w_rgate), (512, 64)),
        ("b_rgate", (1, R_BR, 1, D_RG), (b_rgate, m_b_rgate, v_b_rgate), (1, D_RG)),
        ("w_igate", (2, 512, 512, 64), (w_igate, m_w_igate, v_w_igate), (512, 64)),
        ("b_igate", (1, R_BI, 1, D_RG), (b_igate, m_b_igate, v_b_igate), (1, D_RG)),
        ("lru_lambda", (1, R_LAM, 1, D_RG), (lru_lambda, m_lru_lambda, v_lru_lambda), (1, D_RG)),
        ("rg_norm_g", (1, R_GRG, 1, D_RG), (rg_norm_g, m_rg_norm_g, v_rg_norm_g), (1, D_RG)),
        ("hg_lower_bound", (1, R_HB0, 2, D_HG), (hg_lower_bound, m_hg_lower_bound, v_hg_lower_bound), (2, D_HG)),
        ("hg_norm_g", (1, R_GHG, 1, HD), (hg_norm_g, m_hg_norm_g, v_hg_norm_g), (1, HD)),
        ("ffn_norm_g", (0, R_GFFN, 1, D), (ffn_norm_g, m_ffn_norm_g, v_ffn_norm_g), (1, D)),
        ("final_norm_g", (0, R_GFIN, 1, D), (final_norm_g, m_final_norm_g, v_final_norm_g), (1, D)),
    ]
    res = _adamw_small(gathered, [s[1] for s in small_params],
                       [tuple(t.reshape(s[3]) for t in s[2]) for s in small_params])
    for i, s in enumerate(small_params):
        outs[s[0]] = [r.reshape(s[2][0].shape) for r in res[4 * i:4 * i + 4]]
    for n, ref in (("w_in", w_in), ("w_gate_up", w_gate_up), ("w_out", w_out), ("w_down", w_down)):
        outs[n] = [r.reshape(ref.shape) for r in outs[n]]

    loss_all = lax.psum(loss[0, 0], ("x", "y", "c"))
    order = ["meta_tokens", "mix_norm_g", "w_in", "conv_w", "conv_b", "w_rgate", "b_rgate", "w_igate", "b_igate",
             "lru_lambda", "rg_norm_g", "hg_lower_bound", "hg_norm_g", "w_out", "ffn_norm_g", "w_gate_up", "w_down",
             "final_norm_g"]
    return (loss_all, grad_x, *[outs[n][0] for n in order], *[outs[n][1] for n in order],
            *[outs[n][2] for n in order], *[outs[n][3] for n in order])
```

```python
import functools

import jax
import jax.numpy as jnp
from jax import lax
from jax.experimental import pallas as pl
from jax.experimental.pallas import tpu as pltpu

_BF = jnp.bfloat16
_F32 = jnp.float32
_S = jax.ShapeDtypeStruct
_MESH = pl.DeviceIdType.MESH

N_DEV = 8
N_META = 16
D = 1024
D_RG = 512
D_HG = 512
HD = 128
NH = D_HG // HD
D_IN = 3072
D_FF = 2816
FFB = D_FF // 4
WIN_B = D_IN // N_DEV
EPS = 1e-6
LRU_C = 8.0
TM = 256
HC = 64
VMEM_LIMIT = 56 * 1024 * 1024

ADAM_LR = 0.001
ADAM_B1 = 0.9
ADAM_B2 = 0.999
ADAM_EPS = 1e-08
ADAM_WD = 0.01
ADAM_STEP = 10

_SEND_ORDER = (6, 4, 2, 7, 5, 3, 1, 0)

R_CONVB, R_BR, R_BI, R_LAM, R_GRG, R_HB0, R_HB1, R_GHG, R_CONVW = 0, 1, 2, 3, 4, 5, 6, 7, 8
R_GMIX, R_GFFN, R_GFIN, R_META = 0, 1, 2, 8


def _cp(sem=None, **kw):
    return pltpu.CompilerParams(dimension_semantics=sem, vmem_limit_bytes=VMEM_LIMIT, **kw)


def _dot(a, b):
    return jnp.dot(a, b, preferred_element_type=_F32)


def _dot_nt(a, b):
    return lax.dot_general(a, b, (((1,), (1,)), ((), ())), preferred_element_type=_F32)


def _dot_tn(a, b):
    return lax.dot_general(a, b, (((0,), (0,)), ((), ())), preferred_element_type=_F32)


def _sigmoid(x):
    return jax.nn.sigmoid(x)


def _dsilu(x, s):
    return s * (1.0 + x * (1.0 - s))


_GELU_C = 0.7978845608028654


def _gelu_parts(x):
    t = jnp.tanh(_GELU_C * (x + 0.044715 * (x * x * x)))
    g = 0.5 * x * (1.0 + t)
    dg = 0.5 * (1.0 + t) + 0.5 * x * (1.0 - t * t) * (_GELU_C * (1.0 + 3.0 * 0.044715 * (x * x)))
    return g, dg


def _softplus(z):
    e = jnp.exp(-jnp.abs(z))
    w = 1.0 + e
    l1p = jnp.where(w == 1.0, e, jnp.log(w) * e / jnp.where(w == 1.0, 1.0, w - 1.0))
    return jnp.maximum(z, 0.0) + l1p


def _rms_fwd(x):
    r = lax.rsqrt(jnp.mean(x * x, axis=-1, keepdims=True) + EPS)
    return x * r, r


def _rms_bwd(dyg, n, r):
    return r * (dyg - n * jnp.mean(dyg * n, axis=-1, keepdims=True))


def _full(shape):
    nd = len(shape)
    return pl.BlockSpec(shape, lambda i: (0,) * nd)


def _const(shape):
    nd = len(shape)
    return pl.BlockSpec(shape, lambda i: (0,) * nd, pipeline_mode=pl.Buffered(1))


def _carry_gather(gather, i, nt):
    @pl.when(i == 0)
    def _():
        gather.start()

    def tail():
        for j in range(3):
            @pl.when(i == max(nt - 3 + j, 0))
            def _(j=j):
                gather.forward(j)

        @pl.when(i == nt - 1)
        def _():
            gather.finish()

    return tail


def _inproj(h0, g_mix, w_in, shards):
    t_pad = h0.shape[0]
    nt = t_pad // TM
    nsh = len(shards)

    def body(h_ref, g_ref, w_ref, *rest):
        p_ref, u_ref = rest[nsh:nsh + 2]
        tail = _carry_gather(_Gather(rest[:nsh], rest[nsh + 2:2 * nsh + 2], rest[2 * nsh + 2:]), pl.program_id(0), nt)
        n, _ = _rms_fwd(h_ref[...])
        u = (n * g_ref[...]).astype(_BF)
        u_ref[...] = u
        for j in range(N_DEV):
            p_ref[:, WIN_B * j:WIN_B * (j + 1)] = _dot(u, w_ref[j])
        tail()

    hbm = pl.BlockSpec(memory_space=pl.ANY)
    return pl.pallas_call(
        body, name="inproj", grid=(nt,),
        in_specs=[pl.BlockSpec((TM, D), lambda i: (i, 0)), _full((1, D)), _const((N_DEV, D, WIN_B))] + [hbm] * nsh,
        out_specs=[pl.BlockSpec((TM, D_IN), lambda i: (i, 0)), pl.BlockSpec((TM, D), lambda i: (i, 0))] + [hbm] * nsh,
        out_shape=[_S((t_pad, D_IN), _F32), _S((t_pad, D), _BF)] + [_S((N_DEV,) + s.shape, s.dtype) for s in shards],
        scratch_shapes=_sem_shapes(nsh),
        compiler_params=_cp(("arbitrary",)),
    )(h0, g_mix, w_in, *shards)


def _rg_gates(xc, wr_ref, wi_ref, vec_ref):
    xcb = xc.astype(_BF)
    r = _sigmoid(_dot(xcb, wr_ref[...]) + vec_ref[R_BR:R_BR + 1, :])
    ig = _sigmoid(_dot(xcb, wi_ref[...]) + vec_ref[R_BI:R_BI + 1, :])
    nsp8 = -LRU_C * _softplus(-vec_ref[R_LAM:R_LAM + 1, :])
    la = nsp8 * r
    a = jnp.exp(la)
    th = jnp.tanh(la)
    s = jnp.sqrt(-2.0 * th / (1.0 - th))
    return r, ig, a, s, nsp8


def _conv(xbuf, vec_ref):
    acc = vec_ref[R_CONVW:R_CONVW + 1, :] * xbuf[pl.ds(5, TM), :]
    for j in range(1, 4):
        acc = acc + vec_ref[R_CONVW + j:R_CONVW + j + 1, :] * xbuf[pl.ds(5 + j, TM), :]
    return vec_ref[R_CONVB:R_CONVB + 1, :] + acc


def _dot3(m01, x):
    hi = x.astype(_BF)
    r1 = x - hi.astype(_F32)
    mid = r1.astype(_BF)
    lo = (r1 - mid.astype(_F32)).astype(_BF)
    return (_dot(m01, lo) + _dot(m01, mid)) + _dot(m01, hi)


def _chunk_masks():
    row = lax.broadcasted_iota(jnp.int32, (TM, TM), 0)
    col = lax.broadcasted_iota(jnp.int32, (TM, TM), 1)
    shift = HC.bit_length() - 1
    same = lax.shift_right_logical(row, shift) == lax.shift_right_logical(col, shift)
    return same, same & (row >= col), same & (col >= row)


def _per_chunk_rows(x, r):
    return jnp.concatenate([jnp.broadcast_to(x[HC * c + r:HC * c + r + 1, :], (HC, x.shape[1]))
                            for c in range(TM // HC)], axis=0)


def _hg_prep(p_ref, lb, tri_blk):
    hq = p_ref[:, pl.ds(2 * D_RG, D_HG)]
    hf = p_ref[:, pl.ds(2 * D_RG + D_HG, D_HG)]
    sq = _sigmoid(hq)
    q = hq * sq
    sg = _sigmoid(hf)
    f = lb + (1.0 - lb) * sg
    k = 1.0 - f
    b = _dot3(tri_blk, jnp.log(f))
    bm = _per_chunk_rows(b, HC // 2 - 1)
    bl = _per_chunk_rows(b, HC - 1)
    e_q = jnp.exp(b - bm)
    e_k = jnp.exp(bm - b)
    e_b = jnp.exp(b)
    e_l = jnp.exp(bl - b)
    return dict(hq=hq, sq=sq, q=q, sg=sg, f=f, k=k, e_q=e_q, e_k=e_k, e_b=e_b, e_l=e_l,
                qd=q * e_q, kd=k * e_k, qe=q * e_b, ke=k * e_l, e_end=jnp.exp(bl))


def _mixer_fwd(p, wr, wi, vec, hb, g_hg, shards):
    t_pad = p.shape[0]
    nt = t_pad // TM
    nc_t = TM // HC
    nsh = len(shards)

    def body(p_ref, wr_ref, wi_ref, vec_ref, hb_ref, ghg_ref, *rest):
        sh_refs, rest = rest[:nsh], rest[nsh:]
        y_ref, hs_ref, o_ref, sc_ref = rest[:4]
        gath_refs, rest = rest[4:4 + nsh], rest[4 + nsh:]
        xbuf, a_s, b_s, hcar, st, qd_s, kd_s, qe_s, ke_s, v_s, u_s = rest[:11]
        i = pl.program_id(0)
        tail = _carry_gather(_Gather(sh_refs, gath_refs, rest[11:]), i, nt)

        @pl.when(i == 0)
        def _():
            xbuf[pl.ds(0, 8), :] = jnp.zeros((8, D_RG), _F32)
            hcar[...] = jnp.zeros_like(hcar)
            st[...] = jnp.zeros_like(st)

        x = p_ref[:, pl.ds(0, D_RG)]
        xbuf[pl.ds(8, TM), :] = x
        xc = _conv(xbuf, vec_ref)
        xbuf[pl.ds(0, 8), :] = x[TM - 8:, :]
        r, ig, a, s, _ = _rg_gates(xc, wr_ref, wi_ref, vec_ref)
        a_s[...] = a
        b_s[...] = s * (ig * xc)

        def step(t, h):
            h = a_s[pl.ds(t, 1), :] * h + b_s[pl.ds(t, 1), :]
            hs_ref[pl.ds(t, 1), :] = h
            return h

        hcar[pl.ds(0, 1), :] = lax.fori_loop(0, TM, step, hcar[pl.ds(0, 1), :], unroll=8)
        gel, _ = _gelu_parts(p_ref[:, pl.ds(D_RG, D_RG)])
        n, _ = _rms_fwd(gel * hs_ref[...])
        y_ref[:, pl.ds(0, D_RG)] = (n * vec_ref[R_GRG:R_GRG + 1, :]).astype(_BF)

        lb = _sigmoid(hb_ref[0:1, :] - hb_ref[1:2, :])
        _, tri_blk, _ = _chunk_masks()
        q = _hg_prep(p_ref, lb, tri_blk.astype(_BF))
        for name, ref in (("qd", qd_s), ("kd", kd_s), ("qe", qe_s), ("ke", ke_s)):
            ref[...] = q[name].astype(_BF)
        v_s[...] = p_ref[:, pl.ds(2 * D_RG + 2 * D_HG, D_HG)].astype(_BF)
        e_end = q["e_end"]
        causal = (lax.broadcasted_iota(jnp.int32, (HC, HC), 0) >= lax.broadcasted_iota(jnp.int32, (HC, HC), 1))
        for c in range(nc_t):
            for h in range(NH):
                rs, cs = pl.ds(HC * c, HC), pl.ds(HD * h, HD)
                amat = jnp.where(causal, _dot_nt(qd_s[rs, cs], kd_s[rs, cs]), 0.0)
                o_ref[rs, cs] = _dot(amat.astype(_BF), v_s[rs, cs])
                u_s[NH * c + h] = _dot_tn(v_s[rs, cs], ke_s[rs, cs])
        for h in range(NH):
            cs = pl.ds(HD * h, HD)
            s_run = st[h]
            for c in range(nc_t):
                rs = pl.ds(HC * c, HC)
                sc_ref[c, h] = s_run
                o_ref[rs, cs] += _dot_nt(qe_s[rs, cs], s_run.astype(_BF))
                s_run = e_end[HC * c:HC * c + 1, HD * h:HD * (h + 1)] * s_run + u_s[NH * c + h]
            st[h] = s_run
        for h in range(NH):
            cs = pl.ds(HD * h, HD)
            n_o, _ = _rms_fwd(o_ref[:, cs])
            hg = p_ref[:, pl.ds(2 * D_RG + 3 * D_HG + HD * h, HD)]
            y_ref[:, pl.ds(D_RG + HD * h, HD)] = ((n_o * ghg_ref[...]) * (hg * _sigmoid(hg))).astype(_BF)

        tail()

    hbm = pl.BlockSpec(memory_space=pl.ANY)
    return pl.pallas_call(
        body, name="mixer_fwd", grid=(nt,),
        in_specs=[pl.BlockSpec((TM, D_IN), lambda i: (i, 0)), _full((D_RG, D_RG)), _full((D_RG, D_RG)),
                  _full((16, D_RG)), _full((2, D_HG)), _full((1, HD))] + [hbm] * nsh,
        out_specs=[pl.BlockSpec((TM, D), lambda i: (i, 0)), pl.BlockSpec((TM, D_RG), lambda i: (i, 0)),
                   pl.BlockSpec((TM, D_HG), lambda i: (i, 0)),
                   pl.BlockSpec((nc_t, NH, HD, HD), lambda i: (i, 0, 0, 0))] + [hbm] * nsh,
        out_shape=[_S((t_pad, D), _BF), _S((t_pad, D_RG), _F32), _S((t_pad, D_HG), _F32),
                   _S((t_pad // HC, NH, HD, HD), _F32)] + [_S((N_DEV,) + s.shape, s.dtype) for s in shards],
        scratch_shapes=[pltpu.VMEM((TM + 8, D_RG), _F32), pltpu.VMEM((TM, D_RG), _F32),
                        pltpu.VMEM((TM, D_RG), _F32), pltpu.VMEM((8, D_RG), _F32),
                        pltpu.VMEM((NH, HD, HD), _F32)] + [pltpu.VMEM((TM, D_HG), _BF) for _ in range(5)]
        + [pltpu.VMEM((nc_t * NH, HD, HD), _F32)] + _sem_shapes(nsh),
        compiler_params=_cp(("arbitrary",)),
    )(p, wr, wi, vec, hb, g_hg, *shards)


def _outproj(h0, y, w_out, g_ffn):
    t_pad = h0.shape[0]

    def body(h_ref, y_ref, w_ref, g_ref, h1_ref, v_ref):
        h1 = h_ref[...] + _dot(y_ref[...], w_ref[...])
        h1_ref[...] = h1
        n, _ = _rms_fwd(h1)
        v_ref[...] = (n * g_ref[...]).astype(_BF)

    return pl.pallas_call(
        body, name="outproj", grid=(t_pad // TM,),
        in_specs=[pl.BlockSpec((TM, D), lambda i: (i, 0)), pl.BlockSpec((TM, D), lambda i: (i, 0)),
                  _full((D, D)), _full((1, D))],
        out_specs=[pl.BlockSpec((TM, D), lambda i: (i, 0)), pl.BlockSpec((TM, D), lambda i: (i, 0))],
        out_shape=[_S((t_pad, D), _F32), _S((t_pad, D), _BF)],
        compiler_params=_cp(("arbitrary",)),
    )(h0, y, w_out, g_ffn)


def _ffn_loss(v, h1, w_gu, w_down, g_fin, tgt, n_valid):
    t_pad = v.shape[0]

    def body(v_ref, h1_ref, wgu_ref, wd_ref, g_ref, t_ref, gu_ref, act_ref, dh2_ref, dh2b_ref, loss_ref, gfin_ref):
        i = pl.program_id(0)

        @pl.when(i == 0)
        def _():
            loss_ref[...] = jnp.zeros_like(loss_ref)
            gfin_ref[...] = jnp.zeros_like(gfin_ref)

        vb = v_ref[...]
        h2 = h1_ref[...]
        for b in range(4):
            gate = _dot(vb, wgu_ref[b])
            up = _dot(vb, wgu_ref[4 + b])
            gu_ref[b] = gate
            gu_ref[4 + b] = up
            act = ((gate * _sigmoid(gate)) * up).astype(_BF)
            act_ref[b] = act
            h2 = h2 + _dot(act, wd_ref[b])
        n, r = _rms_fwd(h2)
        out = n * g_ref[...]
        row = i * TM + lax.broadcasted_iota(jnp.int32, (TM, 1), 0)
        valid = (row >= N_META) & (row < n_valid)
        err = jnp.where(valid, out - t_ref[...], 0.0)
        loss_ref[...] += (0.5 / D) * jnp.sum(err * err)
        dout = err * (1.0 / D)
        gfin_ref[...] += jnp.sum(dout * n, axis=0, keepdims=True)
        dh2 = _rms_bwd(dout * g_ref[...], n, r)
        dh2_ref[...] = dh2
        dh2b_ref[...] = dh2.astype(_BF)

    return pl.pallas_call(
        body, name="ffn_loss", grid=(t_pad // TM,),
        in_specs=[pl.BlockSpec((TM, D), lambda i: (i, 0)), pl.BlockSpec((TM, D), lambda i: (i, 0)),
                  _const((N_DEV, D, FFB)), _const((4, FFB, D)), _full((1, D)),
                  pl.BlockSpec((TM, D), lambda i: (i, 0))],
        out_specs=[pl.BlockSpec((N_DEV, TM, FFB), lambda i: (0, i, 0)), pl.BlockSpec((4, TM, FFB), lambda i: (0, i, 0)),
                   pl.BlockSpec((TM, D), lambda i: (i, 0)), pl.BlockSpec((TM, D), lambda i: (i, 0)),
                   _full((8, 128)), _full((1, D))],
        out_shape=[_S((N_DEV, t_pad, FFB), _F32), _S((4, t_pad, FFB), _BF), _S((t_pad, D), _F32),
                   _S((t_pad, D), _BF), _S((8, 128), _F32), _S((1, D), _F32)],
        compiler_params=_cp(("arbitrary",)),
    )(v, h1, w_gu, w_down, g_fin, tgt)


def _ffn_bwd(dh2, dh2b, gu, h1, g_ffn, w_gu, w_down, w_out):
    t_pad = dh2.shape[0]

    def body(dh2_ref, dh2b_ref, gu_ref, h1_ref, g_ref, wgu_ref, wd_ref, wo_ref,
             dgu_ref, dh1_ref, dh1b_ref, dy_ref, gffn_ref):
        i = pl.program_id(0)

        @pl.when(i == 0)
        def _():
            gffn_ref[...] = jnp.zeros_like(gffn_ref)

        db = dh2b_ref[...]
        dv = jnp.zeros((TM, D), _F32)
        for b in range(4):
            dact = _dot_nt(db, wd_ref[b])
            gate = gu_ref[b]
            up = gu_ref[4 + b]
            sg = _sigmoid(gate)
            dgate = ((dact * up) * _dsilu(gate, sg)).astype(_BF)
            dup = (dact * (gate * sg)).astype(_BF)
            dgu_ref[b] = dgate
            dgu_ref[4 + b] = dup
            dv = dv + _dot_nt(dgate, wgu_ref[b]) + _dot_nt(dup, wgu_ref[4 + b])
        n, r = _rms_fwd(h1_ref[...])
        gffn_ref[...] += jnp.sum(dv * n, axis=0, keepdims=True)
        dh1 = dh2_ref[...] + _rms_bwd(dv * g_ref[...], n, r)
        dh1_ref[...] = dh1
        dh1b = dh1.astype(_BF)
        dh1b_ref[...] = dh1b
        dy_ref[...] = _dot_nt(dh1b, wo_ref[...])

    tile = pl.BlockSpec((TM, D), lambda i: (i, 0))
    return pl.pallas_call(
        body, name="ffn_bwd", grid=(t_pad // TM,),
        in_specs=[tile, tile, pl.BlockSpec((N_DEV, TM, FFB), lambda i: (0, i, 0)), tile, _full((1, D)),
                  _const((N_DEV, D, FFB)), _const((4, FFB, D)), _const((D, D))],
        out_specs=[pl.BlockSpec((N_DEV, TM, FFB), lambda i: (0, i, 0)), tile, tile, tile, _full((1, D))],
        out_shape=[_S((N_DEV, t_pad, FFB), _BF), _S((t_pad, D), _F32), _S((t_pad, D), _BF),
                   _S((t_pad, D), _F32), _S((1, D), _F32)],
        compiler_params=_cp(("arbitrary",)),
    )(dh2, dh2b, gu, h1, g_ffn, w_gu, w_down, w_out)


def _mixer_bwd(p, hs, o, sc, dy, wr, wi, vec, hb, g_hg, scatter):
    t_pad = p.shape[0]
    nt = t_pad // TM
    nc_t = TM // HC
    nsc = len(scatter)

    def rev(i):
        return nt - 1 - i

    def body(p_ref, pprev_ref, hs_ref, hprev_ref, o_ref, sc_ref, dy_ref, wr_ref, wi_ref, vec_ref, hb_ref, ghg_ref,
             *rest):
        send_refs, rest = rest[:nsc], rest[nsc:]
        dp_ref, gvec_ref, gw_ref = rest[:3]
        recv_refs, rest = rest[3:3 + nsc], rest[3 + nsc:]
        xbuf, hbuf, dbuf, a_s, g_s, ccar, dst = rest[:7]
        qd_s, kd_s, qe_s, ke_s, v_s, do_s, dqd_s, dkd_s, dqe_s, dke_s, dv_s, w_s, dend_s = rest[7:20]
        exchange = _Exchange(send_refs, [], recv_refs, rest[20:])
        i = pl.program_id(0)
        first_tile = i == nt - 1

        @pl.when(i == 0)
        def _():
            exchange.start()
            gvec_ref[...] = jnp.zeros_like(gvec_ref)
            gw_ref[...] = jnp.zeros_like(gw_ref)
            dbuf[pl.ds(TM, 8), :] = jnp.zeros((8, D_RG), _F32)
            ccar[...] = jnp.zeros_like(ccar)
            dst[...] = jnp.zeros_like(dst)

        def acc(row, val):
            gvec_ref[row:row + 1, :] += jnp.sum(val, axis=0, keepdims=True)

        keep = jnp.where(first_tile, 0.0, 1.0)
        x = p_ref[:, pl.ds(0, D_RG)]
        xbuf[pl.ds(0, 8), :] = pprev_ref[...] * keep
        xbuf[pl.ds(8, TM), :] = x
        xc = _conv(xbuf, vec_ref)
        r, ig, a, s, nsp8 = _rg_gates(xc, wr_ref, wi_ref, vec_ref)
        h = hs_ref[...]
        hbuf[pl.ds(0, 8), :] = hprev_ref[...] * keep
        hbuf[pl.ds(8, TM), :] = h
        hm1 = hbuf[pl.ds(7, TM), :]
        gr = p_ref[:, pl.ds(D_RG, D_RG)]
        gel, dgel = _gelu_parts(gr)
        n, rr = _rms_fwd(gel * h)
        dyn = dy_ref[:, pl.ds(0, D_RG)]
        acc(R_GRG, dyn * n)
        dpre = _rms_bwd(dyn * vec_ref[R_GRG:R_GRG + 1, :], n, rr)
        dp_ref[:, pl.ds(D_RG, D_RG)] = ((dpre * h) * dgel).astype(_BF)
        a_s[...] = a
        g_s[...] = dpre * gel

        def step(k, c):
            t = TM - 1 - k
            g = g_s[pl.ds(t, 1), :] + c
            g_s[pl.ds(t, 1), :] = g
            return a_s[pl.ds(t, 1), :] * g

        ccar[pl.ds(0, 1), :] = lax.fori_loop(0, TM, step, ccar[pl.ds(0, 1), :], unroll=8)
        gt = g_s[...]
        da = gt * hm1
        ixc = ig * xc
        ds = gt * ixc
        dig = (gt * s) * xc
        dxc = (gt * s) * ig
        dla = da * a - ds * ((a * a) / s)
        lam = vec_ref[R_LAM:R_LAM + 1, :]
        gvec_ref[R_LAM:R_LAM + 1, :] += jnp.sum(dla * r, axis=0, keepdims=True) * (LRU_C * _sigmoid(-lam))
        dzr = (dla * nsp8) * (r * (1.0 - r))
        dzi = dig * (ig * (1.0 - ig))
        acc(R_BR, dzr)
        acc(R_BI, dzi)
        xcb = xc.astype(_BF)
        dzrb = dzr.astype(_BF)
        dzib = dzi.astype(_BF)
        gw_ref[0] += _dot_tn(xcb, dzrb)
        gw_ref[1] += _dot_tn(xcb, dzib)
        dxc = dxc + _dot_nt(dzrb, wr_ref[...]) + _dot_nt(dzib, wi_ref[...])
        acc(R_CONVB, dxc)
        for j in range(4):
            acc(R_CONVW + j, dxc * xbuf[pl.ds(5 + j, TM), :])
        dbuf[pl.ds(0, TM), :] = dxc
        dx = vec_ref[R_CONVW + 3:R_CONVW + 4, :] * dxc
        for j in range(3):
            dx = dx + vec_ref[R_CONVW + j:R_CONVW + j + 1, :] * dbuf[pl.ds(3 - j, TM), :]
        dbuf[pl.ds(TM, 8), :] = dxc[0:8, :]
        dp_ref[:, pl.ds(0, D_RG)] = dx.astype(_BF)

        lb = _sigmoid(hb_ref[0:1, :] - hb_ref[1:2, :])
        same, tri_blk, triu_blk = _chunk_masks()
        q = _hg_prep(p_ref, lb, tri_blk.astype(_BF))
        qdb, kdb = q["qd"].astype(_BF), q["kd"].astype(_BF)
        qd_s[...] = qdb
        kd_s[...] = kdb
        qe_s[...] = q["qe"].astype(_BF)
        ke_s[...] = q["ke"].astype(_BF)
        v_s[...] = p_ref[:, pl.ds(2 * D_RG + 2 * D_HG, D_HG)].astype(_BF)
        e_end = q["e_end"]
        ghg = ghg_ref[...]
        for h in range(NH):
            cs = pl.ds(HD * h, HD)
            hg = p_ref[:, pl.ds(2 * D_RG + 3 * D_HG + HD * h, HD)]
            sh = _sigmoid(hg)
            n_o, r_o = _rms_fwd(o_ref[:, cs])
            dyh = dy_ref[:, pl.ds(D_RG + HD * h, HD)]
            dp_ref[:, pl.ds(2 * D_RG + 3 * D_HG + HD * h, HD)] = ((dyh * (n_o * ghg)) * _dsilu(hg, sh)).astype(_BF)
            dn = dyh * (hg * sh)
            gvec_ref[R_GHG:R_GHG + 1, pl.ds(0, HD)] += jnp.sum(dn * n_o, axis=0, keepdims=True)
            do_s[:, cs] = _rms_bwd(dn * ghg, n_o, r_o).astype(_BF)
        causal = (lax.broadcasted_iota(jnp.int32, (HC, HC), 0) >= lax.broadcasted_iota(jnp.int32, (HC, HC), 1))
        for c in range(nc_t):
            for h in range(NH):
                rs, cs = pl.ds(HC * c, HC), pl.ds(HD * h, HD)
                qd_c, kd_c, do_c = qd_s[rs, cs], kd_s[rs, cs], do_s[rs, cs]
                amat = jnp.where(causal, _dot_nt(qd_c, kd_c), 0.0).astype(_BF)
                da_m = jnp.where(causal, _dot_nt(do_c, v_s[rs, cs]), 0.0).astype(_BF)
                dqd_s[rs, cs] = _dot(da_m, kd_c)
                dkd_s[rs, cs] = _dot_tn(da_m, qd_c)
                dqe_s[rs, cs] = _dot(do_c, sc_ref[c, h].astype(_BF))
                dv_s[rs, cs] = _dot_tn(amat, do_c)
                w_s[NH * c + h] = _dot_tn(do_c, qe_s[rs, cs])
        for h in range(NH):
            cs = pl.ds(HD * h, HD)
            d_run = dst[h]
            for c in reversed(range(nc_t)):
                rs = pl.ds(HC * c, HC)
                d_b = d_run.astype(_BF)
                dke_s[rs, cs] = _dot(v_s[rs, cs], d_b)
                dp_ref[rs, pl.ds(2 * D_RG + 2 * D_HG + HD * h, HD)] = (
                    dv_s[rs, cs] + _dot_nt(ke_s[rs, cs], d_b)).astype(_BF)
                dend_s[pl.ds(c, 1), cs] = jnp.sum(sc_ref[c, h] * d_run, axis=0, keepdims=True)
                d_run = w_s[NH * c + h] + e_end[HC * c:HC * c + 1, HD * h:HD * (h + 1)] * d_run
            dst[h] = d_run
        dqd, dkd, dqe, dke = dqd_s[...], dkd_s[...], dqe_s[...], dke_s[...]
        dq = dqd * q["e_q"] + dqe * q["e_b"]
        dk = dkd * q["e_k"] + dke * q["e_l"]
        dkeke = dke * q["ke"]
        db = dqd * qdb.astype(_F32) - dkd * kdb.astype(_F32) + dqe * q["qe"] - dkeke
        d_end = jnp.concatenate([jnp.broadcast_to(dend_s[pl.ds(c, 1), :], (HC, D_HG)) for c in range(nc_t)], axis=0)
        dlf = _dot3(triu_blk.astype(_BF), db) + _dot3(same.astype(_BF), dkeke) + d_end * e_end
        df = dlf / q["f"] - dk
        sg = q["sg"]
        gvec_ref[R_HB0:R_HB0 + 1, :] += jnp.sum(df * (1.0 - sg), axis=0, keepdims=True)
        dp_ref[:, pl.ds(2 * D_RG, D_HG)] = (dq * _dsilu(q["hq"], q["sq"])).astype(_BF)
        dp_ref[:, pl.ds(2 * D_RG + D_HG, D_HG)] = ((df * (1.0 - lb)) * (sg * (1.0 - sg))).astype(_BF)

        @pl.when(i == nt - 1)
        def _():
            glb = gvec_ref[R_HB0:R_HB0 + 1, :] * (lb * (1.0 - lb))
            gvec_ref[R_HB0:R_HB0 + 1, :] = glb
            gvec_ref[R_HB1:R_HB1 + 1, :] = -glb
            exchange.finish()

    hbm = pl.BlockSpec(memory_space=pl.ANY)
    return pl.pallas_call(
        body, name="mixer_bwd", grid=(nt,),
        in_specs=[pl.BlockSpec((TM, D_IN), lambda i: (rev(i), 0)),
                  pl.BlockSpec((8, D_RG), lambda i: (jnp.maximum(rev(i) * (TM // 8) - 1, 0), 0)),
                  pl.BlockSpec((TM, D_RG), lambda i: (rev(i), 0)),
                  pl.BlockSpec((8, D_RG), lambda i: (jnp.maximum(rev(i) * (TM // 8) - 1, 0), 0)),
                  pl.BlockSpec((TM, D_HG), lambda i: (rev(i), 0)),
                  pl.BlockSpec((nc_t, NH, HD, HD), lambda i: (rev(i), 0, 0, 0)),
                  pl.BlockSpec((TM, D), lambda i: (rev(i), 0)),
                  _full((D_RG, D_RG)), _full((D_RG, D_RG)), _full((16, D_RG)), _full((2, D_HG)), _full((1, HD))]
        + [hbm] * nsc,
        out_specs=[pl.BlockSpec((TM, D_IN), lambda i: (rev(i), 0)), _full((16, D_RG)), _full((2, D_RG, D_RG))]
        + [hbm] * nsc,
        out_shape=[_S((t_pad, D_IN), _BF), _S((16, D_RG), _F32), _S((2, D_RG, D_RG), _F32)]
        + [_S(s.shape, s.dtype) for s in scatter],
        scratch_shapes=[pltpu.VMEM((TM + 8, D_RG), _F32), pltpu.VMEM((TM + 8, D_RG), _F32),
                        pltpu.VMEM((TM + 8, D_RG), _F32), pltpu.VMEM((TM, D_RG), _F32),
                        pltpu.VMEM((TM, D_RG), _F32), pltpu.VMEM((8, D_RG), _F32),
                        pltpu.VMEM((NH, HD, HD), _F32)]
        + [pltpu.VMEM((TM, D_HG), _BF) for _ in range(6)] + [pltpu.VMEM((TM, D_HG), _F32) for _ in range(5)]
        + [pltpu.VMEM((nc_t * NH, HD, HD), _F32), pltpu.VMEM((8, D_HG), _F32)] + _sem_shapes(nsc),
        compiler_params=_cp(("arbitrary",)),
    )(p, p, hs, hs, o, sc, dy, wr, wi, vec, hb, g_hg, *scatter)


def _inproj_bwd_send(dp, w_in, h0, dh1, g_mix, u, order, gffn, gfin, to_all):
    t_pad = dp.shape[0]
    rb = t_pad // (2 * N_DEV)
    na = len(to_all)

    def body(order_ref, dpc_ref, dpr_ref, u_ref, w_ref, h_ref, dh1_ref, g_ref, gffn_ref, gfin_ref, *rest):
        all_in = rest[:na]
        dh0_ref, recv_ref = rest[na:na + 2]
        all_out = rest[na + 2:2 * na + 2]
        alla_ref = rest[2 * na + 2]
        buf, pack, blk_send, blk_recv, blk_local = rest[2 * na + 3:2 * na + 8]
        exchange = _Exchange([], all_in, all_out, rest[2 * na + 8:2 * na + 11])
        last = _Exchange([], [pack], [alla_ref], rest[2 * na + 11:])
        s, k = pl.program_id(0), pl.program_id(1)
        x, y, c = _coords()
        me = 4 * x + 2 * y + c

        def send(step):
            r = _SEND_ORDER[step]
            return pltpu.make_async_remote_copy(
                src_ref=buf.at[step], dst_ref=recv_ref.at[me], send_sem=blk_send.at[step], recv_sem=blk_recv.at[r - 1],
                device_id=(x ^ (r >> 2), y ^ ((r >> 1) & 1), c ^ (r & 1)), device_id_type=_MESH)

        @pl.when((s == 0) & (k == 0))
        def _():
            exchange.start()
            pack[...] = jnp.zeros_like(pack)

        @pl.when(k == 0)
        def _():
            buf[s] = _dot_tn(u_ref[...], dpc_ref[...]).astype(_BF)

            for step in range(N_DEV - 1):
                @pl.when(s == step)
                def _(step=step):
                    send(step).start()

        du = jnp.zeros((rb, D), _F32)
        for j in range(N_DEV):
            du = du + _dot_nt(dpr_ref[:, WIN_B * j:WIN_B * (j + 1)], w_ref[j])
        n, r = _rms_fwd(h_ref[...])
        pack[R_GMIX:R_GMIX + 1, :] += jnp.sum(du * n, axis=0, keepdims=True)
        dh0 = dh1_ref[...] + _rms_bwd(du * g_ref[...], n, r)
        dh0_ref[...] = dh0

        @pl.when((s == 0) & (k == 0))
        def _():
            pack[R_META:R_META + N_META, :] = dh0[0:N_META, :]

        @pl.when((s == N_DEV - 1) & (k == 1))
        def _():
            pack[R_GFFN:R_GFFN + 1, :] = gffn_ref[...]
            pack[R_GFIN:R_GFIN + 1, :] = gfin_ref[...]
            last.start()
            mine = pltpu.make_async_copy(buf.at[N_DEV - 1], recv_ref.at[me], blk_local.at[0])
            mine.start()
            for step in range(N_DEV - 1):
                send(step).wait_send()
            for r in range(1, N_DEV):
                px, py, pc = x ^ (r >> 2), y ^ ((r >> 1) & 1), c ^ (r & 1)
                pltpu.make_async_remote_copy(
                    src_ref=buf.at[0], dst_ref=recv_ref.at[4 * px + 2 * py + pc], send_sem=blk_send.at[0],
                    recv_sem=blk_recv.at[r - 1], device_id=(px, py, pc), device_id_type=_MESH).wait_recv()
            mine.wait()
            exchange.finish()
            last.finish()

    hbm = pl.BlockSpec(memory_space=pl.ANY)
    rows = pl.BlockSpec((rb, D), lambda s, k, order: (2 * s + k, 0))
    one = pl.BlockSpec((1, D), lambda s, k, order: (0, 0))
    res = pl.pallas_call(
        body, name="inproj_bwd_send",
        grid_spec=pltpu.PrefetchScalarGridSpec(
            num_scalar_prefetch=1, grid=(N_DEV, 2),
            in_specs=[pl.BlockSpec((t_pad, WIN_B), lambda s, k, order: (0, order[s])),
                      pl.BlockSpec((rb, D_IN), lambda s, k, order: (2 * s + k, 0)),
                      pl.BlockSpec((t_pad, D), lambda s, k, order: (0, 0), pipeline_mode=pl.Buffered(1)),
                      pl.BlockSpec((N_DEV, D, WIN_B), lambda s, k, order: (0, 0, 0), pipeline_mode=pl.Buffered(1)),
                      rows, rows, one, one, one] + [hbm] * na,
            out_specs=[rows] + [hbm] * (na + 2),
            scratch_shapes=[pltpu.VMEM((N_DEV, D, WIN_B), _BF), pltpu.VMEM((24, D), _F32),
                            pltpu.SemaphoreType.DMA((N_DEV - 1,)), pltpu.SemaphoreType.DMA((N_DEV - 1,)),
                            pltpu.SemaphoreType.DMA((1,))] + _sem_shapes(na) + _sem_shapes(1)),
        out_shape=[_S((t_pad, D), _F32), _S((N_DEV, D, WIN_B), _BF)]
        + [_S((N_DEV,) + g.shape, g.dtype) for g in to_all] + [_S((N_DEV, 24, D), _F32)],
        compiler_params=_cp(("arbitrary", "arbitrary")),
    )(order, dp, dp, u, w_in, h0, dh1, g_mix, gffn, gfin, *to_all)
    return res


def _wgrad(name, a, b, a_spec, b_spec, n_blocks, out_block, scatter=()):
    nsc = len(scatter)

    def body(a_ref, b_ref, *rest):
        o_ref = rest[nsc]
        j = pl.program_id(0)
        if nsc:
            exchange = _Exchange(rest[:nsc], [], rest[nsc + 1:2 * nsc + 1], rest[2 * nsc + 1:])

            @pl.when(j == 0)
            def _():
                exchange.start()

        av = a_ref[0] if len(a_ref.shape) == 3 else a_ref[...]
        bv = b_ref[0] if len(b_ref.shape) == 3 else b_ref[...]
        o_ref[0] = _dot_tn(av, bv).astype(_BF)

        if nsc:
            @pl.when(j == n_blocks - 1)
            def _():
                exchange.finish()

    hbm = pl.BlockSpec(memory_space=pl.ANY)
    res = pl.pallas_call(
        body, name=name, grid=(n_blocks,),
        in_specs=[a_spec, b_spec] + [hbm] * nsc,
        out_specs=[pl.BlockSpec((1,) + out_block, lambda j: (j, 0, 0))] + [hbm] * nsc,
        out_shape=[_S((n_blocks,) + out_block, _BF)] + [_S(s.shape, s.dtype) for s in scatter],
        scratch_shapes=_sem_shapes(nsc) if nsc else [],
        compiler_params=_cp(("arbitrary",)),
    )(a, b, *scatter)
    return res if nsc else res[0]


def _coords():
    return lax.axis_index("x"), lax.axis_index("y"), lax.axis_index("c")


def _sem_shapes(na):
    return [pltpu.SemaphoreType.DMA((7 * na,)), pltpu.SemaphoreType.DMA((7 * na,)), pltpu.SemaphoreType.DMA((na,))]


class _Gather:
    def __init__(self, srcs, outs, sems):
        self.srcs, self.outs = srcs, outs
        self.send_sems, self.recv_sems, self.local_sems = sems
        self.na = len(srcs)
        x, y, c = _coords()
        self.pos = (x, y, c)
        self.me = 4 * x + 2 * y + c
        self.sibling = (x, y, 1 - c)
        self.chips = [(1 - x, y), (x, 1 - y), (1 - x, 1 - y)]

    @staticmethod
    def _slot(px, py, pc):
        return 4 * px + 2 * py + pc

    def _copy(self, a, k, block, to, own=False):
        return pltpu.make_async_remote_copy(
            src_ref=self.srcs[a] if own else self.outs[a].at[block], dst_ref=self.outs[a].at[block],
            send_sem=self.send_sems.at[7 * a + k], recv_sem=self.recv_sems.at[7 * a + k],
            device_id=to, device_id_type=_MESH)

    def _mine(self, a):
        return pltpu.make_async_copy(self.srcs[a], self.outs[a].at[self.me], self.local_sems.at[a])

    def _first(self):
        c = self.pos[2]
        cps = []
        for a in range(self.na):
            cps.append(self._copy(a, 0, self.me, self.sibling, own=True))
            cps += [self._copy(a, 1 + j, self.me, (*chip, c), own=True) for j, chip in enumerate(self.chips)]
        return cps

    def _passed(self):
        c = self.pos[2]
        return [self._copy(a, 4 + j, self._slot(*chip, c), self.sibling)
                for j, chip in enumerate(self.chips) for a in range(self.na)]

    def start(self):
        for a in range(self.na):
            self._mine(a).start()
        for cp in self._first():
            cp.start()

    def forward(self, j):
        c = self.pos[2]
        chip = self.chips[j]
        for a in range(self.na):
            self._copy(a, 1 + j, self._slot(*chip, c), self.pos).wait_recv()
            self._copy(a, 4 + j, self._slot(*chip, c), self.sibling).start()

    def finish(self):
        x, y, c = self.pos
        for a in range(self.na):
            self._copy(a, 0, self._slot(x, y, 1 - c), self.pos).wait_recv()
        for j, chip in enumerate(self.chips):
            for a in range(self.na):
                self._copy(a, 4 + j, self._slot(*chip, 1 - c), self.pos).wait_recv()
        for cp in self._first() + self._passed():
            cp.wait_send()
        for a in range(self.na):
            self._mine(a).wait()


class _Exchange:
    def __init__(self, scatter, gather, outs, sems):
        self.ins = list(scatter) + list(gather)
        self.ns, self.na = len(scatter), len(scatter) + len(gather)
        self.outs = outs
        self.send_sems, self.recv_sems, self.local_sems = sems
        x, y, c = _coords()
        self.pos = (x, y, c)
        self.me = 4 * x + 2 * y + c

    def _peer(self, r):
        x, y, c = self.pos
        return x ^ (r >> 2), y ^ ((r >> 1) & 1), c ^ (r & 1)

    def _src(self, a, block):
        return self.ins[a].at[block] if a < self.ns else self.ins[a]

    def _local(self, a):
        return pltpu.make_async_copy(self._src(a, self.me), self.outs[a].at[self.me], self.local_sems.at[a])

    def _send(self, a, r):
        px, py, pc = self._peer(r)
        return pltpu.make_async_remote_copy(
            src_ref=self._src(a, 4 * px + 2 * py + pc), dst_ref=self.outs[a].at[self.me],
            send_sem=self.send_sems.at[7 * a + r - 1], recv_sem=self.recv_sems.at[7 * a + r - 1],
            device_id=(px, py, pc), device_id_type=_MESH)

    def _recv(self, a, r):
        px, py, pc = self._peer(r)
        return pltpu.make_async_remote_copy(
            src_ref=self._src(a, self.me), dst_ref=self.outs[a].at[4 * px + 2 * py + pc],
            send_sem=self.send_sems.at[7 * a + r - 1], recv_sem=self.recv_sems.at[7 * a + r - 1],
            device_id=(px, py, pc), device_id_type=_MESH)

    def start(self):
        for a in range(self.na):
            self._local(a).start()
        for r in range(1, N_DEV):
            for a in range(self.na):
                self._send(a, r).start()

    def finish(self):
        for r in range(1, N_DEV):
            for a in range(self.na):
                self._recv(a, r).wait_recv()
        for r in range(1, N_DEV):
            for a in range(self.na):
                self._send(a, r).wait_send()
        for a in range(self.na):
            self._local(a).wait()


def _allgather_first(gather_f32, cast_f32, gather_dtypes):
    ng, nc = len(gather_f32), len(cast_f32)

    def body(*refs):
        ins, cins = refs[:ng], refs[ng:ng + nc]
        outs, couts = refs[ng + nc:2 * ng + nc], refs[2 * ng + nc:2 * ng + 2 * nc]
        stage = refs[2 * ng + 2 * nc:3 * ng + 2 * nc]
        sems = refs[3 * ng + 2 * nc:]
        for a in range(ng):
            stage[a][...] = ins[a][...].astype(gather_dtypes[a])
        g = _Gather(stage, outs, sems)
        g.start()
        for a in range(nc):
            couts[a][...] = cins[a][...].astype(_BF)
        for j in range(3):
            g.forward(j)
        g.finish()

    vm = pl.BlockSpec(memory_space=pltpu.VMEM)
    return pl.pallas_call(
        body, name="allgather_first",
        in_specs=[vm] * (ng + nc),
        out_specs=[pl.BlockSpec(memory_space=pl.ANY)] * ng + [vm] * nc,
        out_shape=[_S((N_DEV,) + l.shape, dt) for l, dt in zip(gather_f32, gather_dtypes)]
        + [_S(l.shape, _BF) for l in cast_f32],
        scratch_shapes=[pltpu.VMEM(l.shape, dt) for l, dt in zip(gather_f32, gather_dtypes)] + _sem_shapes(ng),
        compiler_params=pltpu.CompilerParams(vmem_limit_bytes=VMEM_LIMIT),
    )(*gather_f32, *cast_f32)


def _adamw_math(w, g, m, v):
    m2 = ADAM_B1 * m + (1.0 - ADAM_B1) * g
    v2 = ADAM_B2 * v + (1.0 - ADAM_B2) * (g * g)
    m_hat = m2 / (1.0 - ADAM_B1 ** ADAM_STEP)
    v_hat = v2 / (1.0 - ADAM_B2 ** ADAM_STEP)
    delta = -ADAM_LR * (m_hat / (jnp.sqrt(v_hat) + ADAM_EPS) + ADAM_WD * w)
    return delta, m2, v2


def _adamw_big(name, recv, w, m, v, rows):
    r_all, c_all = w.shape

    def body(r_ref, w_ref, m_ref, v_ref, g_out, d_out, m_out, v_out):
        g = r_ref[0].astype(_F32)
        for k in range(1, N_DEV):
            g = g + r_ref[k].astype(_F32)
        delta, m2, v2 = _adamw_math(w_ref[...], g, m_ref[...], v_ref[...])
        g_out[...] = g
        d_out[...] = delta
        m_out[...] = m2
        v_out[...] = v2

    tile = pl.BlockSpec((rows, c_all), lambda i: (i, 0))
    return pl.pallas_call(
        body, name=name, grid=(r_all // rows,),
        in_specs=[pl.BlockSpec((N_DEV, rows, c_all), lambda i: (0, i, 0)), tile, tile, tile],
        out_specs=[tile] * 4,
        out_shape=[_S(w.shape, _F32)] * 4,
        compiler_params=_cp(("arbitrary",)),
    )(recv, w, m, v)


def _adamw_small(gathered, slices, wmv):
    ng, npar = len(gathered), len(slices)

    def body(*refs):
        g_refs = refs[:ng]
        wmv_refs = refs[ng:ng + 3 * npar]
        outs = refs[ng + 3 * npar:]
        for i, (ai, r0, nr, ncol) in enumerate(slices):
            g = g_refs[ai][0, pl.ds(r0, nr), pl.ds(0, ncol)]
            for k in range(1, N_DEV):
                g = g + g_refs[ai][k, pl.ds(r0, nr), pl.ds(0, ncol)]
            w_ref, m_ref, v_ref = wmv_refs[3 * i:3 * i + 3]
            delta, m2, v2 = _adamw_math(w_ref[...], g, m_ref[...], v_ref[...])
            outs[4 * i][...] = g
            outs[4 * i + 1][...] = delta
            outs[4 * i + 2][...] = m2
            outs[4 * i + 3][...] = v2

    flat = [t for trip in wmv for t in trip]
    out_shape = []
    for w, _, _ in wmv:
        out_shape += [_S(w.shape, _F32)] * 4
    return pl.pallas_call(
        body, name="adamw_small", out_shape=out_shape,
        compiler_params=pltpu.CompilerParams(vmem_limit_bytes=VMEM_LIMIT),
    )(*gathered, *flat)


def _block_diag(w):
    eye = jnp.eye(8, dtype=w.dtype)
    return (w[:, :, None, :] * eye[:, None, :, None]).reshape(D_RG, D_RG)


def _diag_blocks(g):
    return jnp.concatenate([g[64 * h:64 * (h + 1), 64 * h:64 * (h + 1)] for h in range(8)], axis=0)


def _local_step(x, tgt, meta, g_mix, w_in, vec, wr, wi, hb, g_hg, w_out_l, g_ffn, w_gu_l, w_down_l, g_fin):
    seq = x.shape[0]
    n_valid = N_META + seq
    t_pad = -(-n_valid // TM) * TM
    h0 = jnp.concatenate([meta, x, jnp.zeros((t_pad - n_valid, D), _F32)], axis=0)
    tgt_p = jnp.concatenate([jnp.zeros((N_META, D), _F32), tgt, jnp.zeros((t_pad - n_valid, D), _F32)], axis=0)

    p, u, w_out, w_down = _inproj(h0, g_mix, w_in, [w_out_l, w_down_l])
    y, hs, o, sc, w_gu = _mixer_fwd(p, wr, wi, vec, hb, g_hg, [w_gu_l])
    w_out = w_out.reshape(D, D)
    w_down = w_down.reshape(4, FFB, D)
    h1, v = _outproj(h0, y, w_out, g_ffn)
    gu, act, dh2, dh2b, loss, gfin = _ffn_loss(v, h1, w_gu, w_down, g_fin, tgt_p, n_valid)

    dgu, dh1, dh1b, dy, gffn = _ffn_bwd(dh2, dh2b, gu, h1, g_ffn, w_gu, w_down, w_out)
    g_wdown = _wgrad("wgrad_down", act, dh2b, pl.BlockSpec((1, t_pad, FFB), lambda j: (j, 0, 0)),
                     pl.BlockSpec((t_pad, D), lambda j: (0, 0)), 4, (FFB, D))
    g_wgu, r_wdown = _wgrad("wgrad_gate_up", v, dgu, pl.BlockSpec((t_pad, D), lambda j: (0, 0)),
                            pl.BlockSpec((1, t_pad, FFB), lambda j: (j, 0, 0)), N_DEV, (D, FFB),
                            scatter=[g_wdown.reshape(N_DEV, D_FF // N_DEV, D)])
    g_wout = _wgrad("wgrad_out", y, dh1b, pl.BlockSpec((t_pad, D // N_DEV), lambda j: (0, j)),
                    pl.BlockSpec((t_pad, D), lambda j: (0, 0)), N_DEV, (D // N_DEV, D))
    dp, gvec, gw, r_wgu, r_wout = _mixer_bwd(p, hs, o, sc, dy, wr, wi, vec, hb, g_hg, [g_wgu, g_wout])
    pack_c = jnp.concatenate([_diag_blocks(gw[0]), _diag_blocks(gw[1])], axis=0)
    me = 4 * lax.axis_index("x") + 2 * lax.axis_index("y") + lax.axis_index("c")
    order = (me ^ jnp.array(_SEND_ORDER, jnp.int32)).astype(jnp.int32)
    dh0, r_win, all_b, all_c, all_a = _inproj_bwd_send(dp, w_in, h0, dh1, g_mix, u, order, gffn, gfin, [gvec, pack_c])
    return loss, dh0, (r_win, r_wgu, r_wout, r_wdown), (all_a, all_b, all_c)


def kernel(x, meta_tokens, mix_norm_g, w_in, conv_w, conv_b, w_rgate, b_rgate, w_igate, b_igate, lru_lambda, rg_norm_g, hg_lower_bound, hg_norm_g, w_out, ffn_norm_g, w_gate_up, w_down, final_norm_g, loss_target, m_meta_tokens, m_mix_norm_g, m_w_in, m_conv_w, m_conv_b, m_w_rgate, m_b_rgate, m_w_igate, m_b_igate, m_lru_lambda, m_rg_norm_g, m_hg_lower_bound, m_hg_norm_g, m_w_out, m_ffn_norm_g, m_w_gate_up, m_w_down, m_final_norm_g, v_meta_tokens, v_mix_norm_g, v_w_in, v_conv_w, v_conv_b, v_w_rgate, v_b_rgate, v_w_igate, v_b_igate, v_lru_lambda, v_rg_norm_g, v_hg_lower_bound, v_hg_norm_g, v_w_out, v_ffn_norm_g, v_w_gate_up, v_w_down, v_final_norm_g):
    seq = x.shape[1]
    me = 4 * lax.axis_index("x") + 2 * lax.axis_index("y") + lax.axis_index("c")

    small_l = jnp.concatenate([meta_tokens, jnp.pad(conv_w[0], ((0, 4), (0, 64)))], axis=0)
    w_in_g, small_g, w_gu_l, w_out_l, w_down_l = _allgather_first(
        [w_in[0], small_l], [w_gate_up[0], w_out[0], w_down[0]], [_BF, _F32])
    meta_full = jnp.transpose(small_g[:, :N_META, :], (1, 0, 2)).reshape(N_META, D)
    conv_w_full = jnp.transpose(small_g[:, N_META:N_META + 4, :64], (1, 0, 2)).reshape(4, D_RG)
    vec = jnp.concatenate([conv_b, b_rgate, b_igate, lru_lambda, rg_norm_g, jnp.zeros((3, D_RG), _F32),
                           conv_w_full, jnp.zeros((4, D_RG), _F32)], axis=0)
    wr = _block_diag(w_rgate[0]).astype(_BF)
    wi = _block_diag(w_igate[0]).astype(_BF)

    loss, dh0, (r_win, r_wgu, r_wout, r_wdown), (all_a, all_b, all_c) = _local_step(
        x[0], loss_target[0], meta_full, mix_norm_g, w_in_g, vec, wr, wi, hg_lower_bound, hg_norm_g,
        w_out_l, ffn_norm_g, w_gu_l, w_down_l, final_norm_g.reshape(1, D))
    grad_x = dh0[N_META:N_META + seq][None]

    outs = {}
    outs["w_in"] = _adamw_big("adamw_w_in", r_win, w_in[0], m_w_in[0], v_w_in[0], 256)
    outs["w_gate_up"] = _adamw_big("adamw_w_gate_up", r_wgu, w_gate_up[0], m_w_gate_up[0], v_w_gate_up[0], 256)
    outs["w_out"] = _adamw_big("adamw_w_out", r_wout, w_out[0], m_w_out[0], v_w_out[0], 128)
    outs["w_down"] = _adamw_big("adamw_w_down", r_wdown, w_down[0], m_w_down[0], v_w_down[0], 176)

    meta_part = lax.dynamic_slice_in_dim(all_a[:, R_META:R_META + N_META, :], me * 128, 128, axis=2)
    convw_part = lax.dynamic_slice_in_dim(all_b[:, R_CONVW:R_CONVW + 4, :], me * 64, 64, axis=2)
    gathered = [all_a, all_b, all_c, meta_part, convw_part]
    small_params = [
        ("meta_tokens", (3, 0, N_META, 128), (meta_tokens, m_meta_tokens, v_meta_tokens), (N_META, 128)),
        ("mix_norm_g", (0, R_GMIX, 1, D), (mix_norm_g, m_mix_norm_g, v_mix_norm_g), (1, D)),
        ("conv_w", (4, 0, 4, 64), (conv_w, m_conv_w, v_conv_w), (4, 64)),
        ("conv_b", (1, R_CONVB, 1, D_RG), (conv_b, m_conv_b, v_conv_b), (1, D_RG)),
        ("w_rgate", (2, 0, 512, 64), (w_rgate, m_w_rgate, v_w_rgate), (512, 64)),
        ("b_rgate", (1, R_BR, 1, D_RG), (b_rgate, m_b_rgate, v_b_rgate), (1, D_RG)),
        ("w_igate", (2, 512, 512, 64), (w_igate, m_w_igate, v_w_igate), (512, 64)),
        ("b_igate", (1, R_BI, 1, D_RG), (b_igate, m_b_igate, v_b_igate), (1, D_RG)),
        ("lru_lambda", (1, R_LAM, 1, D_RG), (lru_lambda, m_lru_lambda, v_lru_lambda), (1, D_RG)),
        ("rg_norm_g", (1, R_GRG, 1, D_RG), (rg_norm_g, m_rg_norm_g, v_rg_norm_g), (1, D_RG)),
        ("hg_lower_bound", (1, R_HB0, 2, D_HG), (hg_lower_bound, m_hg_lower_bound, v_hg_lower_bound), (2, D_HG)),
        ("hg_norm_g", (1, R_GHG, 1, HD), (hg_norm_g, m_hg_norm_g, v_hg_norm_g), (1, HD)),
        ("ffn_norm_g", (0, R_GFFN, 1, D), (ffn_norm_g, m_ffn_norm_g, v_ffn_norm_g), (1, D)),
        ("final_norm_g", (0, R_GFIN, 1, D), (final_norm_g, m_final_norm_g, v_final_norm_g), (1, D)),
    ]
    res = _adamw_small(gathered, [s[1] for s in small_params],
                       [tuple(t.reshape(s[3]) for t in s[2]) for s in small_params])
    for i, s in enumerate(small_params):
        outs[s[0]] = [r.reshape(s[2][0].shape) for r in res[4 * i:4 * i + 4]]
    for n, ref in (("w_in", w_in), ("w_gate_up", w_gate_up), ("w_out", w_out), ("w_down", w_down)):
        outs[n] = [r.reshape(ref.shape) for r in outs[n]]

    loss_all = lax.psum(loss[0, 0], ("x", "y", "c"))
    order = ["meta_tokens", "mix_norm_g", "w_in", "conv_w", "conv_b", "w_rgate", "b_rgate", "w_igate", "b_igate",
             "lru_lambda", "rg_norm_g", "hg_lower_bound", "hg_norm_g", "w_out", "ffn_norm_g", "w_gate_up", "w_down",
             "final_norm_g"]
    return (loss_all, grad_x, *[outs[n][0] for n in order], *[outs[n][1] for n in order],
            *[outs[n][2] for n in order], *[outs[n][3] for n in order])
```

```python
import functools

import jax
import jax.numpy as jnp
from jax import lax
from jax.experimental import pallas as pl
from jax.experimental.pallas import tpu as pltpu

_BF = jnp.bfloat16
_F32 = jnp.float32
_S = jax.ShapeDtypeStruct
_MESH = pl.DeviceIdType.MESH

N_DEV = 8
N_META = 16
D = 1024
D_RG = 512
D_HG = 512
HD = 128
NH = D_HG // HD
D_IN = 3072
D_FF = 2816
FFB = D_FF // 4
WIN_B = D_IN // N_DEV
EPS = 1e-6
LRU_C = 8.0
TM = 256
HC = 64
VMEM_LIMIT = 56 * 1024 * 1024

ADAM_LR = 0.001
ADAM_B1 = 0.9
ADAM_B2 = 0.999
ADAM_EPS = 1e-08
ADAM_WD = 0.01
ADAM_STEP = 10

_SEND_ORDER = (6, 4, 2, 7, 5, 3, 1, 0)

R_CONVB, R_BR, R_BI, R_LAM, R_GRG, R_HB0, R_HB1, R_GHG, R_CONVW = 0, 1, 2, 3, 4, 5, 6, 7, 8
R_GMIX, R_GFFN, R_GFIN, R_META = 0, 1, 2, 8


def _cp(sem=None, **kw):
    return pltpu.CompilerParams(dimension_semantics=sem, vmem_limit_bytes=VMEM_LIMIT, **kw)


def _dot(a, b):
    return jnp.dot(a, b, preferred_element_type=_F32)


def _dot_nt(a, b):
    return lax.dot_general(a, b, (((1,), (1,)), ((), ())), preferred_element_type=_F32)


def _dot_tn(a, b):
    return lax.dot_general(a, b, (((0,), (0,)), ((), ())), preferred_element_type=_F32)


def _sigmoid(x):
    return jax.nn.sigmoid(x)


def _dsilu(x, s):
    return s * (1.0 + x * (1.0 - s))


_GELU_C = 0.7978845608028654


def _gelu_parts(x):
    t = jnp.tanh(_GELU_C * (x + 0.044715 * (x * x * x)))
    g = 0.5 * x * (1.0 + t)
    dg = 0.5 * (1.0 + t) + 0.5 * x * (1.0 - t * t) * (_GELU_C * (1.0 + 3.0 * 0.044715 * (x * x)))
    return g, dg


def _softplus(z):
    e = jnp.exp(-jnp.abs(z))
    w = 1.0 + e
    l1p = jnp.where(w == 1.0, e, jnp.log(w) * e / jnp.where(w == 1.0, 1.0, w - 1.0))
    return jnp.maximum(z, 0.0) + l1p


def _rms_fwd(x):
    r = lax.rsqrt(jnp.mean(x * x, axis=-1, keepdims=True) + EPS)
    return x * r, r


def _rms_bwd(dyg, n, r):
    return r * (dyg - n * jnp.mean(dyg * n, axis=-1, keepdims=True))


def _full(shape):
    nd = len(shape)
    return pl.BlockSpec(shape, lambda i: (0,) * nd)


def _const(shape):
    nd = len(shape)
    return pl.BlockSpec(shape, lambda i: (0,) * nd, pipeline_mode=pl.Buffered(1))


def _carry_gather(gather, i, nt):
    @pl.when(i == 0)
    def _():
        gather.start()

    def tail():
        for j in range(3):
            @pl.when(i == max(nt - 3 + j, 0))
            def _(j=j):
                gather.forward(j)

        @pl.when(i == nt - 1)
        def _():
            gather.finish()

    return tail


def _inproj(h0, g_mix, w_in, shards):
    t_pad = h0.shape[0]
    nt = t_pad // TM
    nsh = len(shards)

    def body(h_ref, g_ref, w_ref, *rest):
        p_ref, u_ref = rest[nsh:nsh + 2]
        tail = _carry_gather(_Gather(rest[:nsh], rest[nsh + 2:2 * nsh + 2], rest[2 * nsh + 2:]), pl.program_id(0), nt)
        n, _ = _rms_fwd(h_ref[...])
        u = (n * g_ref[...]).astype(_BF)
        u_ref[...] = u
        for j in range(N_DEV):
            p_ref[:, WIN_B * j:WIN_B * (j + 1)] = _dot(u, w_ref[j])
        tail()

    hbm = pl.BlockSpec(memory_space=pl.ANY)
    return pl.pallas_call(
        body, name="inproj", grid=(nt,),
        in_specs=[pl.BlockSpec((TM, D), lambda i: (i, 0)), _full((1, D)), _const((N_DEV, D, WIN_B))] + [hbm] * nsh,
        out_specs=[pl.BlockSpec((TM, D_IN), lambda i: (i, 0)), pl.BlockSpec((TM, D), lambda i: (i, 0))] + [hbm] * nsh,
        out_shape=[_S((t_pad, D_IN), _F32), _S((t_pad, D), _BF)] + [_S((N_DEV,) + s.shape, s.dtype) for s in shards],
        scratch_shapes=_sem_shapes(nsh),
        compiler_params=_cp(("arbitrary",)),
    )(h0, g_mix, w_in, *shards)


def _rg_gates(xc, wr_ref, wi_ref, vec_ref):
    xcb = xc.astype(_BF)
    r = _sigmoid(_dot(xcb, wr_ref[...]) + vec_ref[R_BR:R_BR + 1, :])
    ig = _sigmoid(_dot(xcb, wi_ref[...]) + vec_ref[R_BI:R_BI + 1, :])
    nsp8 = -LRU_C * _softplus(-vec_ref[R_LAM:R_LAM + 1, :])
    la = nsp8 * r
    a = jnp.exp(la)
    th = jnp.tanh(la)
    s = jnp.sqrt(-2.0 * th / (1.0 - th))
    return r, ig, a, s, nsp8


def _conv(xbuf, vec_ref):
    acc = vec_ref[R_CONVW:R_CONVW + 1, :] * xbuf[pl.ds(5, TM), :]
    for j in range(1, 4):
        acc = acc + vec_ref[R_CONVW + j:R_CONVW + j + 1, :] * xbuf[pl.ds(5 + j, TM), :]
    return vec_ref[R_CONVB:R_CONVB + 1, :] + acc


def _dot3(m01, x):
    hi = x.astype(_BF)
    r1 = x - hi.astype(_F32)
    mid = r1.astype(_BF)
    lo = (r1 - mid.astype(_F32)).astype(_BF)
    return (_dot(m01, lo) + _dot(m01, mid)) + _dot(m01, hi)


def _chunk_masks():
    row = lax.broadcasted_iota(jnp.int32, (TM, TM), 0)
    col = lax.broadcasted_iota(jnp.int32, (TM, TM), 1)
    shift = HC.bit_length() - 1
    same = lax.shift_right_logical(row, shift) == lax.shift_right_logical(col, shift)
    return same, same & (row >= col), same & (col >= row)


def _per_chunk_rows(x, r):
    return jnp.concatenate([jnp.broadcast_to(x[HC * c + r:HC * c + r + 1, :], (HC, x.shape[1]))
                            for c in range(TM // HC)], axis=0)


def _hg_prep(p_ref, lb, tri_blk):
    hq = p_ref[:, pl.ds(2 * D_RG, D_HG)]
    hf = p_ref[:, pl.ds(2 * D_RG + D_HG, D_HG)]
    sq = _sigmoid(hq)
    q = hq * sq
    sg = _sigmoid(hf)
    f = lb + (1.0 - lb) * sg
    k = 1.0 - f
    b = _dot3(tri_blk, jnp.log(f))
    bm = _per_chunk_rows(b, HC // 2 - 1)
    bl = _per_chunk_rows(b, HC - 1)
    e_q = jnp.exp(b - bm)
    e_k = jnp.exp(bm - b)
    e_b = jnp.exp(b)
    e_l = jnp.exp(bl - b)
    return dict(hq=hq, sq=sq, q=q, sg=sg, f=f, k=k, e_q=e_q, e_k=e_k, e_b=e_b, e_l=e_l,
                qd=q * e_q, kd=k * e_k, qe=q * e_b, ke=k * e_l, e_end=jnp.exp(bl))


def _mixer_fwd(p, wr, wi, vec, hb, g_hg, shards):
    t_pad = p.shape[0]
    nt = t_pad // TM
    nc_t = TM // HC
    nsh = len(shards)

    def body(p_ref, wr_ref, wi_ref, vec_ref, hb_ref, ghg_ref, *rest):
        sh_refs, rest = rest[:nsh], rest[nsh:]
        y_ref, hs_ref, o_ref, sc_ref = rest[:4]
        gath_refs, rest = rest[4:4 + nsh], rest[4 + nsh:]
        xbuf, a_s, b_s, hcar, st, qd_s, kd_s, qe_s, ke_s, v_s, u_s = rest[:11]
        i = pl.program_id(0)
        tail = _carry_gather(_Gather(sh_refs, gath_refs, rest[11:]), i, nt)

        @pl.when(i == 0)
        def _():
            xbuf[pl.ds(0, 8), :] = jnp.zeros((8, D_RG), _F32)
            hcar[...] = jnp.zeros_like(hcar)
            st[...] = jnp.zeros_like(st)

        x = p_ref[:, pl.ds(0, D_RG)]
        xbuf[pl.ds(8, TM), :] = x
        xc = _conv(xbuf, vec_ref)
        xbuf[pl.ds(0, 8), :] = x[TM - 8:, :]
        r, ig, a, s, _ = _rg_gates(xc, wr_ref, wi_ref, vec_ref)
        a_s[...] = a
        b_s[...] = s * (ig * xc)

        def step(t, h):
            h = a_s[pl.ds(t, 1), :] * h + b_s[pl.ds(t, 1), :]
            hs_ref[pl.ds(t, 1), :] = h
            return h

        hcar[pl.ds(0, 1), :] = lax.fori_loop(0, TM, step, hcar[pl.ds(0, 1), :], unroll=8)
        gel, _ = _gelu_parts(p_ref[:, pl.ds(D_RG, D_RG)])
        n, _ = _rms_fwd(gel * hs_ref[...])
        y_ref[:, pl.ds(0, D_RG)] = (n * vec_ref[R_GRG:R_GRG + 1, :]).astype(_BF)

        lb = _sigmoid(hb_ref[0:1, :] - hb_ref[1:2, :])
        _, tri_blk, _ = _chunk_masks()
        q = _hg_prep(p_ref, lb, tri_blk.astype(_BF))
        for name, ref in (("qd", qd_s), ("kd", kd_s), ("qe", qe_s), ("ke", ke_s)):
            ref[...] = q[name].astype(_BF)
        v_s[...] = p_ref[:, pl.ds(2 * D_RG + 2 * D_HG, D_HG)].astype(_BF)
        e_end = q["e_end"]
        causal = (lax.broadcasted_iota(jnp.int32, (HC, HC), 0) >= lax.broadcasted_iota(jnp.int32, (HC, HC), 1))
        for c in range(nc_t):
            for h in range(NH):
                rs, cs = pl.ds(HC * c, HC), pl.ds(HD * h, HD)
                amat = jnp.where(causal, _dot_nt(qd_s[rs, cs], kd_s[rs, cs]), 0.0)
                o_ref[rs, cs] = _dot(amat.astype(_BF), v_s[rs, cs])
                u_s[NH * c + h] = _dot_tn(v_s[rs, cs], ke_s[rs, cs])
        for h in range(NH):
            cs = pl.ds(HD * h, HD)
            s_run = st[h]
            for c in range(nc_t):
                rs = pl.ds(HC * c, HC)
                sc_ref[c, h] = s_run
                o_ref[rs, cs] += _dot_nt(qe_s[rs, cs], s_run.astype(_BF))
                s_run = e_end[HC * c:HC * c + 1, HD * h:HD * (h + 1)] * s_run + u_s[NH * c + h]
            st[h] = s_run
        for h in range(NH):
            cs = pl.ds(HD * h, HD)
            n_o, _ = _rms_fwd(o_ref[:, cs])
            hg = p_ref[:, pl.ds(2 * D_RG + 3 * D_HG + HD * h, HD)]
            y_ref[:, pl.ds(D_RG + HD * h, HD)] = ((n_o * ghg_ref[...]) * (hg * _sigmoid(hg))).astype(_BF)

        tail()

    hbm = pl.BlockSpec(memory_space=pl.ANY)
    return pl.pallas_call(
        body, name="mixer_fwd", grid=(nt,),
        in_specs=[pl.BlockSpec((TM, D_IN), lambda i: (i, 0)), _full((D_RG, D_RG)), _full((D_RG, D_RG)),
                  _full((16, D_RG)), _full((2, D_HG)), _full((1, HD))] + [hbm] * nsh,
        out_specs=[pl.BlockSpec((TM, D), lambda i: (i, 0)), pl.BlockSpec((TM, D_RG), lambda i: (i, 0)),
                   pl.BlockSpec((TM, D_HG), lambda i: (i, 0)),
                   pl.BlockSpec((nc_t, NH, HD, HD), lambda i: (i, 0, 0, 0))] + [hbm] * nsh,
        out_shape=[_S((t_pad, D), _BF), _S((t_pad, D_RG), _F32), _S((t_pad, D_HG), _F32),
                   _S((t_pad // HC, NH, HD, HD), _F32)] + [_S((N_DEV,) + s.shape, s.dtype) for s in shards],
        scratch_shapes=[pltpu.VMEM((TM + 8, D_RG), _F32), pltpu.VMEM((TM, D_RG), _F32),
                        pltpu.VMEM((TM, D_RG), _F32), pltpu.VMEM((8, D_RG), _F32),
                        pltpu.VMEM((NH, HD, HD), _F32)] + [pltpu.VMEM((TM, D_HG), _BF) for _ in range(5)]
        + [pltpu.VMEM((nc_t * NH, HD, HD), _F32)] + _sem_shapes(nsh),
        compiler_params=_cp(("arbitrary",)),
    )(p, wr, wi, vec, hb, g_hg, *shards)


def _outproj(h0, y, w_out, g_ffn):
    t_pad = h0.shape[0]

    def body(h_ref, y_ref, w_ref, g_ref, h1_ref, v_ref):
        h1 = h_ref[...] + _dot(y_ref[...], w_ref[...])
        h1_ref[...] = h1
        n, _ = _rms_fwd(h1)
        v_ref[...] = (n * g_ref[...]).astype(_BF)

    return pl.pallas_call(
        body, name="outproj", grid=(t_pad // TM,),
        in_specs=[pl.BlockSpec((TM, D), lambda i: (i, 0)), pl.BlockSpec((TM, D), lambda i: (i, 0)),
                  _full((D, D)), _full((1, D))],
        out_specs=[pl.BlockSpec((TM, D), lambda i: (i, 0)), pl.BlockSpec((TM, D), lambda i: (i, 0))],
        out_shape=[_S((t_pad, D), _F32), _S((t_pad, D), _BF)],
        compiler_params=_cp(("arbitrary",)),
    )(h0, y, w_out, g_ffn)


def _ffn_loss(v, h1, w_gu, w_down, g_fin, tgt, n_valid):
    t_pad = v.shape[0]

    def body(v_ref, h1_ref, wgu_ref, wd_ref, g_ref, t_ref, gu_ref, act_ref, dh2_ref, dh2b_ref, loss_ref, gfin_ref):
        i = pl.program_id(0)

        @pl.when(i == 0)
        def _():
            loss_ref[...] = jnp.zeros_like(loss_ref)
            gfin_ref[...] = jnp.zeros_like(gfin_ref)

        vb = v_ref[...]
        h2 = h1_ref[...]
        for b in range(4):
            gate = _dot(vb, wgu_ref[b])
            up = _dot(vb, wgu_ref[4 + b])
            gu_ref[b] = gate
            gu_ref[4 + b] = up
            act = ((gate * _sigmoid(gate)) * up).astype(_BF)
            act_ref[b] = act
            h2 = h2 + _dot(act, wd_ref[b])
        n, r = _rms_fwd(h2)
        out = n * g_ref[...]
        row = i * TM + lax.broadcasted_iota(jnp.int32, (TM, 1), 0)
        valid = (row >= N_META) & (row < n_valid)
        err = jnp.where(valid, out - t_ref[...], 0.0)
        loss_ref[...] += (0.5 / D) * jnp.sum(err * err)
        dout = err * (1.0 / D)
        gfin_ref[...] += jnp.sum(dout * n, axis=0, keepdims=True)
        dh2 = _rms_bwd(dout * g_ref[...], n, r)
        dh2_ref[...] = dh2
        dh2b_ref[...] = dh2.astype(_BF)

    return pl.pallas_call(
        body, name="ffn_loss", grid=(t_pad // TM,),
        in_specs=[pl.BlockSpec((TM, D), lambda i: (i, 0)), pl.BlockSpec((TM, D), lambda i: (i, 0)),
                  _const((N_DEV, D, FFB)), _const((4, FFB, D)), _full((1, D)),
                  pl.BlockSpec((TM, D), lambda i: (i, 0))],
        out_specs=[pl.BlockSpec((N_DEV, TM, FFB), lambda i: (0, i, 0)), pl.BlockSpec((4, TM, FFB), lambda i: (0, i, 0)),
                   pl.BlockSpec((TM, D), lambda i: (i, 0)), pl.BlockSpec((TM, D), lambda i: (i, 0)),
                   _full((8, 128)), _full((1, D))],
        out_shape=[_S((N_DEV, t_pad, FFB), _F32), _S((4, t_pad, FFB), _BF), _S((t_pad, D), _F32),
                   _S((t_pad, D), _BF), _S((8, 128), _F32), _S((1, D), _F32)],
        compiler_params=_cp(("arbitrary",)),
    )(v, h1, w_gu, w_down, g_fin, tgt)


def _ffn_bwd(dh2, dh2b, gu, h1, g_ffn, w_gu, w_down, w_out):
    t_pad = dh2.shape[0]

    def body(dh2_ref, dh2b_ref, gu_ref, h1_ref, g_ref, wgu_ref, wd_ref, wo_ref,
             dgu_ref, dh1_ref, dh1b_ref, dy_ref, gffn_ref):
        i = pl.program_id(0)

        @pl.when(i == 0)
        def _():
            gffn_ref[...] = jnp.zeros_like(gffn_ref)

        db = dh2b_ref[...]
        dv = jnp.zeros((TM, D), _F32)
        for b in range(4):
            dact = _dot_nt(db, wd_ref[b])
            gate = gu_ref[b]
            up = gu_ref[4 + b]
            sg = _sigmoid(gate)
            dgate = ((dact * up) * _dsilu(gate, sg)).astype(_BF)
            dup = (dact * (gate * sg)).astype(_BF)
            dgu_ref[b] = dgate
            dgu_ref[4 + b] = dup
            dv = dv + _dot_nt(dgate, wgu_ref[b]) + _dot_nt(dup, wgu_ref[4 + b])
        n, r = _rms_fwd(h1_ref[...])
        gffn_ref[...] += jnp.sum(dv * n, axis=0, keepdims=True)
        dh1 = dh2_ref[...] + _rms_bwd(dv * g_ref[...], n, r)
        dh1_ref[...] = dh1
        dh1b = dh1.astype(_BF)
        dh1b_ref[...] = dh1b
        dy_ref[...] = _dot_nt(dh1b, wo_ref[...])

    tile = pl.BlockSpec((TM, D), lambda i: (i, 0))
    return pl.pallas_call(
        body, name="ffn_bwd", grid=(t_pad // TM,),
        in_specs=[tile, tile, pl.BlockSpec((N_DEV, TM, FFB), lambda i: (0, i, 0)), tile, _full((1, D)),
                  _const((N_DEV, D, FFB)), _const((4, FFB, D)), _const((D, D))],
        out_specs=[pl.BlockSpec((N_DEV, TM, FFB), lambda i: (0, i, 0)), tile, tile, tile, _full((1, D))],
        out_shape=[_S((N_DEV, t_pad, FFB), _BF), _S((t_pad, D), _F32), _S((t_pad, D), _BF),
                   _S((t_pad, D), _F32), _S((1, D), _F32)],
        compiler_params=_cp(("arbitrary",)),
    )(dh2, dh2b, gu, h1, g_ffn, w_gu, w_down, w_out)


def _mixer_bwd(p, hs, o, sc, dy, wr, wi, vec, hb, g_hg, scatter):
    t_pad = p.shape[0]
    nt = t_pad // TM
    nc_t = TM // HC
    nsc = len(scatter)

    def rev(i):
        return nt - 1 - i

    def body(p_ref, pprev_ref, hs_ref, hprev_ref, o_ref, sc_ref, dy_ref, wr_ref, wi_ref, vec_ref, hb_ref, ghg_ref,
             *rest):
        send_refs, rest = rest[:nsc], rest[nsc:]
        dp_ref, gvec_ref, gw_ref = rest[:3]
        recv_refs, rest = rest[3:3 + nsc], rest[3 + nsc:]
        xbuf, hbuf, dbuf, a_s, g_s, ccar, dst = rest[:7]
        qd_s, kd_s, qe_s, ke_s, v_s, do_s, dqd_s, dkd_s, dqe_s, dke_s, dv_s, w_s, dend_s = rest[7:20]
        exchange = _Exchange(send_refs, [], recv_refs, rest[20:])
        i = pl.program_id(0)
        first_tile = i == nt - 1

        @pl.when(i == 0)
        def _():
            exchange.start()
            gvec_ref[...] = jnp.zeros_like(gvec_ref)
            gw_ref[...] = jnp.zeros_like(gw_ref)
            dbuf[pl.ds(TM, 8), :] = jnp.zeros((8, D_RG), _F32)
            ccar[...] = jnp.zeros_like(ccar)
            dst[...] = jnp.zeros_like(dst)

        def acc(row, val):
            gvec_ref[row:row + 1, :] += jnp.sum(val, axis=0, keepdims=True)

        keep = jnp.where(first_tile, 0.0, 1.0)
        x = p_ref[:, pl.ds(0, D_RG)]
        xbuf[pl.ds(0, 8), :] = pprev_ref[...] * keep
        xbuf[pl.ds(8, TM), :] = x
        xc = _conv(xbuf, vec_ref)
        r, ig, a, s, nsp8 = _rg_gates(xc, wr_ref, wi_ref, vec_ref)
        h = hs_ref[...]
        hbuf[pl.ds(0, 8), :] = hprev_ref[...] * keep
        hbuf[pl.ds(8, TM), :] = h
        hm1 = hbuf[pl.ds(7, TM), :]
        gr = p_ref[:, pl.ds(D_RG, D_RG)]
        gel, dgel = _gelu_parts(gr)
        n, rr = _rms_fwd(gel * h)
        dyn = dy_ref[:, pl.ds(0, D_RG)]
        acc(R_GRG, dyn * n)
        dpre = _rms_bwd(dyn * vec_ref[R_GRG:R_GRG + 1, :], n, rr)
        dp_ref[:, pl.ds(D_RG, D_RG)] = ((dpre * h) * dgel).astype(_BF)
        a_s[...] = a
        g_s[...] = dpre * gel

        def step(k, c):
            t = TM - 1 - k
            g = g_s[pl.ds(t, 1), :] + c
            g_s[pl.ds(t, 1), :] = g
            return a_s[pl.ds(t, 1), :] * g

        ccar[pl.ds(0, 1), :] = lax.fori_loop(0, TM, step, ccar[pl.ds(0, 1), :], unroll=8)
        gt = g_s[...]
        da = gt * hm1
        ixc = ig * xc
        ds = gt * ixc
        dig = (gt * s) * xc
        dxc = (gt * s) * ig
        dla = da * a - ds * ((a * a) / s)
        lam = vec_ref[R_LAM:R_LAM + 1, :]
        gvec_ref[R_LAM:R_LAM + 1, :] += jnp.sum(dla * r, axis=0, keepdims=True) * (LRU_C * _sigmoid(-lam))
        dzr = (dla * nsp8) * (r * (1.0 - r))
        dzi = dig * (ig * (1.0 - ig))
        acc(R_BR, dzr)
        acc(R_BI, dzi)
        xcb = xc.astype(_BF)
        dzrb = dzr.astype(_BF)
        dzib = dzi.astype(_BF)
        gw_ref[0] += _dot_tn(xcb, dzrb)
        gw_ref[1] += _dot_tn(xcb, dzib)
        dxc = dxc + _dot_nt(dzrb, wr_ref[...]) + _dot_nt(dzib, wi_ref[...])
        acc(R_CONVB, dxc)
        for j in range(4):
            acc(R_CONVW + j, dxc * xbuf[pl.ds(5 + j, TM), :])
        dbuf[pl.ds(0, TM), :] = dxc
        dx = vec_ref[R_CONVW + 3:R_CONVW + 4, :] * dxc
        for j in range(3):
            dx = dx + vec_ref[R_CONVW + j:R_CONVW + j + 1, :] * dbuf[pl.ds(3 - j, TM), :]
        dbuf[pl.ds(TM, 8), :] = dxc[0:8, :]
        dp_ref[:, pl.ds(0, D_RG)] = dx.astype(_BF)

        lb = _sigmoid(hb_ref[0:1, :] - hb_ref[1:2, :])
        same, tri_blk, triu_blk = _chunk_masks()
        q = _hg_prep(p_ref, lb, tri_blk.astype(_BF))
        qdb, kdb = q["qd"].astype(_BF), q["kd"].astype(_BF)
        qd_s[...] = qdb
        kd_s[...] = kdb
        qe_s[...] = q["qe"].astype(_BF)
        ke_s[...] = q["ke"].astype(_BF)
        v_s[...] = p_ref[:, pl.ds(2 * D_RG + 2 * D_HG, D_HG)].astype(_BF)
        e_end = q["e_end"]
        ghg = ghg_ref[...]
        for h in range(NH):
            cs = pl.ds(HD * h, HD)
            hg = p_ref[:, pl.ds(2 * D_RG + 3 * D_HG + HD * h, HD)]
            sh = _sigmoid(hg)
            n_o, r_o = _rms_fwd(o_ref[:, cs])
            dyh = dy_ref[:, pl.ds(D_RG + HD * h, HD)]
            dp_ref[:, pl.ds(2 * D_RG + 3 * D_HG + HD * h, HD)] = ((dyh * (n_o * ghg)) * _dsilu(hg, sh)).astype(_BF)
            dn = dyh * (hg * sh)
            gvec_ref[R_GHG:R_GHG + 1, pl.ds(0, HD)] += jnp.sum(dn * n_o, axis=0, keepdims=True)
            do_s[:, cs] = _rms_bwd(dn * ghg, n_o, r_o).astype(_BF)
        causal = (lax.broadcasted_iota(jnp.int32, (HC, HC), 0) >= lax.broadcasted_iota(jnp.int32, (HC, HC), 1))
        for c in range(nc_t):
            for h in range(NH):
                rs, cs = pl.ds(HC * c, HC), pl.ds(HD * h, HD)
                qd_c, kd_c, do_c = qd_s[rs, cs], kd_s[rs, cs], do_s[rs, cs]
                amat = jnp.where(causal, _dot_nt(qd_c, kd_c), 0.0).astype(_BF)
                da_m = jnp.where(causal, _dot_nt(do_c, v_s[rs, cs]), 0.0).astype(_BF)
                dqd_s[rs, cs] = _dot(da_m, kd_c)
                dkd_s[rs, cs] = _dot_tn(da_m, qd_c)
                dqe_s[rs, cs] = _dot(do_c, sc_ref[c, h].astype(_BF))
                dv_s[rs, cs] = _dot_tn(amat, do_c)
                w_s[NH * c + h] = _dot_tn(do_c, qe_s[rs, cs])
        for h in range(NH):
            cs = pl.ds(HD * h, HD)
            d_run = dst[h]
            for c in reversed(range(nc_t)):
                rs = pl.ds(HC * c, HC)
                d_b = d_run.astype(_BF)
                dke_s[rs, cs] = _dot(v_s[rs, cs], d_b)
                dp_ref[rs, pl.ds(2 * D_RG + 2 * D_HG + HD * h, HD)] = (
                    dv_s[rs, cs] + _dot_nt(ke_s[rs, cs], d_b)).astype(_BF)
                dend_s[pl.ds(c, 1), cs] = jnp.sum(sc_ref[c, h] * d_run, axis=0, keepdims=True)
                d_run = w_s[NH * c + h] + e_end[HC * c:HC * c + 1, HD * h:HD * (h + 1)] * d_run
            dst[h] = d_run
        dqd, dkd, dqe, dke = dqd_s[...], dkd_s[...], dqe_s[...], dke_s[...]
        dq = dqd * q["e_q"] + dqe * q["e_b"]
        dk = dkd * q["e_k"] + dke * q["e_l"]
        dkeke = dke * q["ke"]
        db = dqd * qdb.astype(_F32) - dkd * kdb.astype(_F32) + dqe * q["qe"] - dkeke
        d_end = jnp.concatenate([jnp.broadcast_to(dend_s[pl.ds(c, 1), :], (HC, D_HG)) for c in range(nc_t)], axis=0)
        dlf = _dot3(triu_blk.astype(_BF), db) + _dot3(same.astype(_BF), dkeke) + d_end * e_end
        df = dlf / q["f"] - dk
        sg = q["sg"]
        gvec_ref[R_HB0:R_HB0 + 1, :] += jnp.sum(df * (1.0 - sg), axis=0, keepdims=True)
        dp_ref[:, pl.ds(2 * D_RG, D_HG)] = (dq * _dsilu(q["hq"], q["sq"])).astype(_BF)
        dp_ref[:, pl.ds(2 * D_RG + D_HG, D_HG)] = ((df * (1.0 - lb)) * (sg * (1.0 - sg))).astype(_BF)

        @pl.when(i == nt - 1)
        def _():
            glb = gvec_ref[R_HB0:R_HB0 + 1, :] * (lb * (1.0 - lb))
            gvec_ref[R_HB0:R_HB0 + 1, :] = glb
            gvec_ref[R_HB1:R_HB1 + 1, :] = -glb
            exchange.finish()

    hbm = pl.BlockSpec(memory_space=pl.ANY)
    return pl.pallas_call(
        body, name="mixer_bwd", grid=(nt,),
        in_specs=[pl.BlockSpec((TM, D_IN), lambda i: (rev(i), 0)),
                  pl.BlockSpec((8, D_RG), lambda i: (jnp.maximum(rev(i) * (TM // 8) - 1, 0), 0)),
                  pl.BlockSpec((TM, D_RG), lambda i: (rev(i), 0)),
                  pl.BlockSpec((8, D_RG), lambda i: (jnp.maximum(rev(i) * (TM // 8) - 1, 0), 0)),
                  pl.BlockSpec((TM, D_HG), lambda i: (rev(i), 0)),
                  pl.BlockSpec((nc_t, NH, HD, HD), lambda i: (rev(i), 0, 0, 0)),
                  pl.BlockSpec((TM, D), lambda i: (rev(i), 0)),
                  _full((D_RG, D_RG)), _full((D_RG, D_RG)), _full((16, D_RG)), _full((2, D_HG)), _full((1, HD))]
        + [hbm] * nsc,
        out_specs=[pl.BlockSpec((TM, D_IN), lambda i: (rev(i), 0)), _full((16, D_RG)), _full((2, D_RG, D_RG))]
        + [hbm] * nsc,
        out_shape=[_S((t_pad, D_IN), _BF), _S((16, D_RG), _F32), _S((2, D_RG, D_RG), _F32)]
        + [_S(s.shape, s.dtype) for s in scatter],
        scratch_shapes=[pltpu.VMEM((TM + 8, D_RG), _F32), pltpu.VMEM((TM + 8, D_RG), _F32),
                        pltpu.VMEM((TM + 8, D_RG), _F32), pltpu.VMEM((TM, D_RG), _F32),
                        pltpu.VMEM((TM, D_RG), _F32), pltpu.VMEM((8, D_RG), _F32),
                        pltpu.VMEM((NH, HD, HD), _F32)]
        + [pltpu.VMEM((TM, D_HG), _BF) for _ in range(6)] + [pltpu.VMEM((TM, D_HG), _F32) for _ in range(5)]
        + [pltpu.VMEM((nc_t * NH, HD, HD), _F32), pltpu.VMEM((8, D_HG), _F32)] + _sem_shapes(nsc),
        compiler_params=_cp(("arbitrary",)),
    )(p, p, hs, hs, o, sc, dy, wr, wi, vec, hb, g_hg, *scatter)


def _inproj_bwd_send(dp, w_in, h0, dh1, g_mix, u, order, gffn, gfin, to_all):
    t_pad = dp.shape[0]
    rb = t_pad // (2 * N_DEV)
    n_steps = N_DEV + 2 * N_DEV
    na = len(to_all)

    def body(order_ref, dpc_ref, dpr_ref, u_ref, w_ref, h_ref, dh1_ref, g_ref, gffn_ref, gfin_ref, *rest):
        all_in = rest[:na]
        dh0_ref, recv_ref = rest[na:na + 2]
        all_out = rest[na + 2:2 * na + 2]
        alla_ref = rest[2 * na + 2]
        buf, pack, blk_send, blk_recv, blk_local = rest[2 * na + 3:2 * na + 8]
        exchange = _Exchange([], all_in, all_out, rest[2 * na + 8:2 * na + 11])
        last = _Exchange([], [pack], [alla_ref], rest[2 * na + 11:])
        s = pl.program_id(0)
        x, y, c = _coords()
        me = 4 * x + 2 * y + c

        def send(step):
            r = _SEND_ORDER[step]
            return pltpu.make_async_remote_copy(
                src_ref=buf.at[step], dst_ref=recv_ref.at[me], send_sem=blk_send.at[step], recv_sem=blk_recv.at[r - 1],
                device_id=(x ^ (r >> 2), y ^ ((r >> 1) & 1), c ^ (r & 1)), device_id_type=_MESH)

        @pl.when(s == 0)
        def _():
            exchange.start()
            pack[...] = jnp.zeros_like(pack)

        @pl.when(s < N_DEV)
        def _():
            buf[s] = _dot_tn(u_ref[...], dpc_ref[...]).astype(_BF)

            for step in range(N_DEV - 1):
                @pl.when(s == step)
                def _(step=step):
                    send(step).start()

        @pl.when(s >= N_DEV)
        def _():
            du = jnp.zeros((rb, D), _F32)
            for j in range(N_DEV):
                du = du + _dot_nt(dpr_ref[:, WIN_B * j:WIN_B * (j + 1)], w_ref[j])
            n, r = _rms_fwd(h_ref[...])
            pack[R_GMIX:R_GMIX + 1, :] += jnp.sum(du * n, axis=0, keepdims=True)
            dh0 = dh1_ref[...] + _rms_bwd(du * g_ref[...], n, r)
            dh0_ref[...] = dh0

            @pl.when(s == N_DEV)
            def _():
                pack[R_META:R_META + N_META, :] = dh0[0:N_META, :]

        @pl.when(s == n_steps - 1)
        def _():
            pack[R_GFFN:R_GFFN + 1, :] = gffn_ref[...]
            pack[R_GFIN:R_GFIN + 1, :] = gfin_ref[...]
            last.start()
            mine = pltpu.make_async_copy(buf.at[N_DEV - 1], recv_ref.at[me], blk_local.at[0])
            mine.start()
            for step in range(N_DEV - 1):
                send(step).wait_send()
            for r in range(1, N_DEV):
                px, py, pc = x ^ (r >> 2), y ^ ((r >> 1) & 1), c ^ (r & 1)
                pltpu.make_async_remote_copy(
                    src_ref=buf.at[0], dst_ref=recv_ref.at[4 * px + 2 * py + pc], send_sem=blk_send.at[0],
                    recv_sem=blk_recv.at[r - 1], device_id=(px, py, pc), device_id_type=_MESH).wait_recv()
            mine.wait()
            exchange.finish()
            last.finish()

    hbm = pl.BlockSpec(memory_space=pl.ANY)
    rows = pl.BlockSpec((rb, D), lambda s, order: (jnp.maximum(s - N_DEV, 0), 0))
    one = pl.BlockSpec((1, D), lambda s, order: (0, 0))
    res = pl.pallas_call(
        body, name="inproj_bwd_send",
        grid_spec=pltpu.PrefetchScalarGridSpec(
            num_scalar_prefetch=1, grid=(n_steps,),
            in_specs=[pl.BlockSpec((t_pad, WIN_B), lambda s, order: (0, order[jnp.minimum(s, N_DEV - 1)])),
                      pl.BlockSpec((rb, D_IN), lambda s, order: (jnp.maximum(s - N_DEV, 0), 0)),
                      pl.BlockSpec((t_pad, D), lambda s, order: (0, 0), pipeline_mode=pl.Buffered(1)),
                      pl.BlockSpec((N_DEV, D, WIN_B), lambda s, order: (0, 0, 0), pipeline_mode=pl.Buffered(1)),
                      rows, rows, one, one, one] + [hbm] * na,
            out_specs=[rows] + [hbm] * (na + 2),
            scratch_shapes=[pltpu.VMEM((N_DEV, D, WIN_B), _BF), pltpu.VMEM((24, D), _F32),
                            pltpu.SemaphoreType.DMA((N_DEV - 1,)), pltpu.SemaphoreType.DMA((N_DEV - 1,)),
                            pltpu.SemaphoreType.DMA((1,))] + _sem_shapes(na) + _sem_shapes(1)),
        out_shape=[_S((t_pad, D), _F32), _S((N_DEV, D, WIN_B), _BF)]
        + [_S((N_DEV,) + g.shape, g.dtype) for g in to_all] + [_S((N_DEV, 24, D), _F32)],
        compiler_params=_cp(("arbitrary",)),
    )(order, dp, dp, u, w_in, h0, dh1, g_mix, gffn, gfin, *to_all)
    return res


def _wgrad(name, a, b, a_spec, b_spec, n_blocks, out_block, scatter=()):
    nsc = len(scatter)

    def body(a_ref, b_ref, *rest):
        o_ref = rest[nsc]
        j = pl.program_id(0)
        if nsc:
            exchange = _Exchange(rest[:nsc], [], rest[nsc + 1:2 * nsc + 1], rest[2 * nsc + 1:])

            @pl.when(j == 0)
            def _():
                exchange.start()

        av = a_ref[0] if len(a_ref.shape) == 3 else a_ref[...]
        bv = b_ref[0] if len(b_ref.shape) == 3 else b_ref[...]
        o_ref[0] = _dot_tn(av, bv).astype(_BF)

        if nsc:
            @pl.when(j == n_blocks - 1)
            def _():
                exchange.finish()

    hbm = pl.BlockSpec(memory_space=pl.ANY)
    res = pl.pallas_call(
        body, name=name, grid=(n_blocks,),
        in_specs=[a_spec, b_spec] + [hbm] * nsc,
        out_specs=[pl.BlockSpec((1,) + out_block, lambda j: (j, 0, 0))] + [hbm] * nsc,
        out_shape=[_S((n_blocks,) + out_block, _BF)] + [_S(s.shape, s.dtype) for s in scatter],
        scratch_shapes=_sem_shapes(nsc) if nsc else [],
        compiler_params=_cp(("arbitrary",)),
    )(a, b, *scatter)
    return res if nsc else res[0]


def _coords():
    return lax.axis_index("x"), lax.axis_index("y"), lax.axis_index("c")


def _sem_shapes(na):
    return [pltpu.SemaphoreType.DMA((7 * na,)), pltpu.SemaphoreType.DMA((7 * na,)), pltpu.SemaphoreType.DMA((na,))]


class _Gather:
    def __init__(self, srcs, outs, sems):
        self.srcs, self.outs = srcs, outs
        self.send_sems, self.recv_sems, self.local_sems = sems
        self.na = len(srcs)
        x, y, c = _coords()
        self.pos = (x, y, c)
        self.me = 4 * x + 2 * y + c
        self.sibling = (x, y, 1 - c)
        self.chips = [(1 - x, y), (x, 1 - y), (1 - x, 1 - y)]

    @staticmethod
    def _slot(px, py, pc):
        return 4 * px + 2 * py + pc

    def _copy(self, a, k, block, to, own=False):
        return pltpu.make_async_remote_copy(
            src_ref=self.srcs[a] if own else self.outs[a].at[block], dst_ref=self.outs[a].at[block],
            send_sem=self.send_sems.at[7 * a + k], recv_sem=self.recv_sems.at[7 * a + k],
            device_id=to, device_id_type=_MESH)

    def _mine(self, a):
        return pltpu.make_async_copy(self.srcs[a], self.outs[a].at[self.me], self.local_sems.at[a])

    def _first(self):
        c = self.pos[2]
        cps = []
        for a in range(self.na):
            cps.append(self._copy(a, 0, self.me, self.sibling, own=True))
            cps += [self._copy(a, 1 + j, self.me, (*chip, c), own=True) for j, chip in enumerate(self.chips)]
        return cps

    def _passed(self):
        c = self.pos[2]
        return [self._copy(a, 4 + j, self._slot(*chip, c), self.sibling)
                for j, chip in enumerate(self.chips) for a in range(self.na)]

    def start(self):
        for a in range(self.na):
            self._mine(a).start()
        for cp in self._first():
            cp.start()

    def forward(self, j):
        c = self.pos[2]
        chip = self.chips[j]
        for a in range(self.na):
            self._copy(a, 1 + j, self._slot(*chip, c), self.pos).wait_recv()
            self._copy(a, 4 + j, self._slot(*chip, c), self.sibling).start()

    def finish(self):
        x, y, c = self.pos
        for a in range(self.na):
            self._copy(a, 0, self._slot(x, y, 1 - c), self.pos).wait_recv()
        for j, chip in enumerate(self.chips):
            for a in range(self.na):
                self._copy(a, 4 + j, self._slot(*chip, 1 - c), self.pos).wait_recv()
        for cp in self._first() + self._passed():
            cp.wait_send()
        for a in range(self.na):
            self._mine(a).wait()


class _Exchange:
    def __init__(self, scatter, gather, outs, sems):
        self.ins = list(scatter) + list(gather)
        self.ns, self.na = len(scatter), len(scatter) + len(gather)
        self.outs = outs
        self.send_sems, self.recv_sems, self.local_sems = sems
        x, y, c = _coords()
        self.pos = (x, y, c)
        self.me = 4 * x + 2 * y + c

    def _peer(self, r):
        x, y, c = self.pos
        return x ^ (r >> 2), y ^ ((r >> 1) & 1), c ^ (r & 1)

    def _src(self, a, block):
        return self.ins[a].at[block] if a < self.ns else self.ins[a]

    def _local(self, a):
        return pltpu.make_async_copy(self._src(a, self.me), self.outs[a].at[self.me], self.local_sems.at[a])

    def _send(self, a, r):
        px, py, pc = self._peer(r)
        return pltpu.make_async_remote_copy(
            src_ref=self._src(a, 4 * px + 2 * py + pc), dst_ref=self.outs[a].at[self.me],
            send_sem=self.send_sems.at[7 * a + r - 1], recv_sem=self.recv_sems.at[7 * a + r - 1],
            device_id=(px, py, pc), device_id_type=_MESH)

    def _recv(self, a, r):
        px, py, pc = self._peer(r)
        return pltpu.make_async_remote_copy(
            src_ref=self._src(a, self.me), dst_ref=self.outs[a].at[4 * px + 2 * py + pc],
            send_sem=self.send_sems.at[7 * a + r - 1], recv_sem=self.recv_sems.at[7 * a + r - 1],
            device_id=(px, py, pc), device_id_type=_MESH)

    def start(self):
        for a in range(self.na):
            self._local(a).start()
        for r in range(1, N_DEV):
            for a in range(self.na):
                self._send(a, r).start()

    def finish(self):
        for r in range(1, N_DEV):
            for a in range(self.na):
                self._recv(a, r).wait_recv()
        for r in range(1, N_DEV):
            for a in range(self.na):
                self._send(a, r).wait_send()
        for a in range(self.na):
            self._local(a).wait()


def _allgather_first(gather_f32, cast_f32, gather_dtypes):
    ng, nc = len(gather_f32), len(cast_f32)

    def body(*refs):
        ins, cins = refs[:ng], refs[ng:ng + nc]
        outs, couts = refs[ng + nc:2 * ng + nc], refs[2 * ng + nc:2 * ng + 2 * nc]
        stage = refs[2 * ng + 2 * nc:3 * ng + 2 * nc]
        sems = refs[3 * ng + 2 * nc:]
        for a in range(ng):
            stage[a][...] = ins[a][...].astype(gather_dtypes[a])
        g = _Gather(stage, outs, sems)
        g.start()
        for a in range(nc):
            couts[a][...] = cins[a][...].astype(_BF)
        for j in range(3):
            g.forward(j)
        g.finish()

    vm = pl.BlockSpec(memory_space=pltpu.VMEM)
    return pl.pallas_call(
        body, name="allgather_first",
        in_specs=[vm] * (ng + nc),
        out_specs=[pl.BlockSpec(memory_space=pl.ANY)] * ng + [vm] * nc,
        out_shape=[_S((N_DEV,) + l.shape, dt) for l, dt in zip(gather_f32, gather_dtypes)]
        + [_S(l.shape, _BF) for l in cast_f32],
        scratch_shapes=[pltpu.VMEM(l.shape, dt) for l, dt in zip(gather_f32, gather_dtypes)] + _sem_shapes(ng),
        compiler_params=pltpu.CompilerParams(vmem_limit_bytes=VMEM_LIMIT),
    )(*gather_f32, *cast_f32)


def _adamw_math(w, g, m, v):
    m2 = ADAM_B1 * m + (1.0 - ADAM_B1) * g
    v2 = ADAM_B2 * v + (1.0 - ADAM_B2) * (g * g)
    m_hat = m2 / (1.0 - ADAM_B1 ** ADAM_STEP)
    v_hat = v2 / (1.0 - ADAM_B2 ** ADAM_STEP)
    delta = -ADAM_LR * (m_hat / (jnp.sqrt(v_hat) + ADAM_EPS) + ADAM_WD * w)
    return delta, m2, v2


def _adamw_big(name, recv, w, m, v, rows):
    r_all, c_all = w.shape

    def body(r_ref, w_ref, m_ref, v_ref, g_out, d_out, m_out, v_out):
        g = r_ref[0].astype(_F32)
        for k in range(1, N_DEV):
            g = g + r_ref[k].astype(_F32)
        delta, m2, v2 = _adamw_math(w_ref[...], g, m_ref[...], v_ref[...])
        g_out[...] = g
        d_out[...] = delta
        m_out[...] = m2
        v_out[...] = v2

    tile = pl.BlockSpec((rows, c_all), lambda i: (i, 0))
    return pl.pallas_call(
        body, name=name, grid=(r_all // rows,),
        in_specs=[pl.BlockSpec((N_DEV, rows, c_all), lambda i: (0, i, 0)), tile, tile, tile],
        out_specs=[tile] * 4,
        out_shape=[_S(w.shape, _F32)] * 4,
        compiler_params=_cp(("arbitrary",)),
    )(recv, w, m, v)


def _adamw_small(gathered, slices, wmv):
    ng, npar = len(gathered), len(slices)

    def body(*refs):
        g_refs = refs[:ng]
        wmv_refs = refs[ng:ng + 3 * npar]
        outs = refs[ng + 3 * npar:]
        for i, (ai, r0, nr, ncol) in enumerate(slices):
            g = g_refs[ai][0, pl.ds(r0, nr), pl.ds(0, ncol)]
            for k in range(1, N_DEV):
                g = g + g_refs[ai][k, pl.ds(r0, nr), pl.ds(0, ncol)]
            w_ref, m_ref, v_ref = wmv_refs[3 * i:3 * i + 3]
            delta, m2, v2 = _adamw_math(w_ref[...], g, m_ref[...], v_ref[...])
            outs[4 * i][...] = g
            outs[4 * i + 1][...] = delta
            outs[4 * i + 2][...] = m2
            outs[4 * i + 3][...] = v2

    flat = [t for trip in wmv for t in trip]
    out_shape = []
    for w, _, _ in wmv:
        out_shape += [_S(w.shape, _F32)] * 4
    return pl.pallas_call(
        body, name="adamw_small", out_shape=out_shape,
        compiler_params=pltpu.CompilerParams(vmem_limit_bytes=VMEM_LIMIT),
    )(*gathered, *flat)


def _block_diag(w):
    eye = jnp.eye(8, dtype=w.dtype)
    return (w[:, :, None, :] * eye[:, None, :, None]).reshape(D_RG, D_RG)


def _diag_blocks(g):
    return jnp.concatenate([g[64 * h:64 * (h + 1), 64 * h:64 * (h + 1)] for h in range(8)], axis=0)


def _local_step(x, tgt, meta, g_mix, w_in, vec, wr, wi, hb, g_hg, w_out_l, g_ffn, w_gu_l, w_down_l, g_fin):
    seq = x.shape[0]
    n_valid = N_META + seq
    t_pad = -(-n_valid // TM) * TM
    h0 = jnp.concatenate([meta, x, jnp.zeros((t_pad - n_valid, D), _F32)], axis=0)
    tgt_p = jnp.concatenate([jnp.zeros((N_META, D), _F32), tgt, jnp.zeros((t_pad - n_valid, D), _F32)], axis=0)

    p, u, w_out, w_down = _inproj(h0, g_mix, w_in, [w_out_l, w_down_l])
    y, hs, o, sc, w_gu = _mixer_fwd(p, wr, wi, vec, hb, g_hg, [w_gu_l])
    w_out = w_out.reshape(D, D)
    w_down = w_down.reshape(4, FFB, D)
    h1, v = _outproj(h0, y, w_out, g_ffn)
    gu, act, dh2, dh2b, loss, gfin = _ffn_loss(v, h1, w_gu, w_down, g_fin, tgt_p, n_valid)

    dgu, dh1, dh1b, dy, gffn = _ffn_bwd(dh2, dh2b, gu, h1, g_ffn, w_gu, w_down, w_out)
    g_wdown = _wgrad("wgrad_down", act, dh2b, pl.BlockSpec((1, t_pad, FFB), lambda j: (j, 0, 0)),
                     pl.BlockSpec((t_pad, D), lambda j: (0, 0)), 4, (FFB, D))
    g_wgu, r_wdown = _wgrad("wgrad_gate_up", v, dgu, pl.BlockSpec((t_pad, D), lambda j: (0, 0)),
                            pl.BlockSpec((1, t_pad, FFB), lambda j: (j, 0, 0)), N_DEV, (D, FFB),
                            scatter=[g_wdown.reshape(N_DEV, D_FF // N_DEV, D)])
    g_wout = _wgrad("wgrad_out", y, dh1b, pl.BlockSpec((t_pad, D // N_DEV), lambda j: (0, j)),
                    pl.BlockSpec((t_pad, D), lambda j: (0, 0)), N_DEV, (D // N_DEV, D))
    dp, gvec, gw, r_wgu, r_wout = _mixer_bwd(p, hs, o, sc, dy, wr, wi, vec, hb, g_hg, [g_wgu, g_wout])
    pack_c = jnp.concatenate([_diag_blocks(gw[0]), _diag_blocks(gw[1])], axis=0)
    me = 4 * lax.axis_index("x") + 2 * lax.axis_index("y") + lax.axis_index("c")
    order = (me ^ jnp.array(_SEND_ORDER, jnp.int32)).astype(jnp.int32)
    dh0, r_win, all_b, all_c, all_a = _inproj_bwd_send(dp, w_in, h0, dh1, g_mix, u, order, gffn, gfin, [gvec, pack_c])
    return loss, dh0, (r_win, r_wgu, r_wout, r_wdown), (all_a, all_b, all_c)


def kernel(x, meta_tokens, mix_norm_g, w_in, conv_w, conv_b, w_rgate, b_rgate, w_igate, b_igate, lru_lambda, rg_norm_g, hg_lower_bound, hg_norm_g, w_out, ffn_norm_g, w_gate_up, w_down, final_norm_g, loss_target, m_meta_tokens, m_mix_norm_g, m_w_in, m_conv_w, m_conv_b, m_w_rgate, m_b_rgate, m_w_igate, m_b_igate, m_lru_lambda, m_rg_norm_g, m_hg_lower_bound, m_hg_norm_g, m_w_out, m_ffn_norm_g, m_w_gate_up, m_w_down, m_final_norm_g, v_meta_tokens, v_mix_norm_g, v_w_in, v_conv_w, v_conv_b, v_w_rgate, v_b_rgate, v_w_igate, v_b_igate, v_lru_lambda, v_rg_norm_g, v_hg_lower_bound, v_hg_norm_g, v_w_out, v_ffn_norm_g, v_w_gate_up, v_w_down, v_final_norm_g):
    seq = x.shape[1]
    me = 4 * lax.axis_index("x") + 2 * lax.axis_index("y") + lax.axis_index("c")

    small_l = jnp.concatenate([meta_tokens, jnp.pad(conv_w[0], ((0, 4), (0, 64)))], axis=0)
    w_in_g, small_g, w_gu_l, w_out_l, w_down_l = _allgather_first(
        [w_in[0], small_l], [w_gate_up[0], w_out[0], w_down[0]], [_BF, _F32])
    meta_full = jnp.transpose(small_g[:, :N_META, :], (1, 0, 2)).reshape(N_META, D)
    conv_w_full = jnp.transpose(small_g[:, N_META:N_META + 4, :64], (1, 0, 2)).reshape(4, D_RG)
    vec = jnp.concatenate([conv_b, b_rgate, b_igate, lru_lambda, rg_norm_g, jnp.zeros((3, D_RG), _F32),
                           conv_w_full, jnp.zeros((4, D_RG), _F32)], axis=0)
    wr = _block_diag(w_rgate[0]).astype(_BF)
    wi = _block_diag(w_igate[0]).astype(_BF)

    loss, dh0, (r_win, r_wgu, r_wout, r_wdown), (all_a, all_b, all_c) = _local_step(
        x[0], loss_target[0], meta_full, mix_norm_g, w_in_g, vec, wr, wi, hg_lower_bound, hg_norm_g,
        w_out_l, ffn_norm_g, w_gu_l, w_down_l, final_norm_g.reshape(1, D))
    grad_x = dh0[N_META:N_META + seq][None]

    outs = {}
    outs["w_in"] = _adamw_big("adamw_w_in", r_win, w_in[0], m_w_in[0], v_w_in[0], 256)
    outs["w_gate_up"] = _adamw_big("adamw_w_gate_up", r_wgu, w_gate_up[0], m_w_gate_up[0], v_w_gate_up[0], 256)
    outs["w_out"] = _adamw_big("adamw_w_out", r_wout, w_out[0], m_w_out[0], v_w_out[0], 128)
    outs["w_down"] = _adamw_big("adamw_w_down", r_wdown, w_down[0], m_w_down[0], v_w_down[0], 176)

    meta_part = lax.dynamic_slice_in_dim(all_a[:, R_META:R_META + N_META, :], me * 128, 128, axis=2)
    convw_part = lax.dynamic_slice_in_dim(all_b[:, R_CONVW:R_CONVW + 4, :], me * 64, 64, axis=2)
    gathered = [all_a, all_b, all_c, meta_part, convw_part]
    small_params = [
        ("meta_tokens", (3, 0, N_META, 128), (meta_tokens, m_meta_tokens, v_meta_tokens), (N_META, 128)),
        ("mix_norm_g", (0, R_GMIX, 1, D), (mix_norm_g, m_mix_norm_g, v_mix_norm_g), (1, D)),
        ("conv_w", (4, 0, 4, 64), (conv_w, m_conv_w, v_conv_w), (4, 64)),
        ("conv_b", (1, R_CONVB, 1, D_RG), (conv_b, m_conv_b, v_conv_b), (1, D_RG)),
        ("w_rgate", (2, 0, 512, 64), (w_rgate, m_w_rgate, v_w_rgate), (512, 64)),
        ("b_rgate", (1, R_BR, 1, D_RG), (b_rgate, m_b_rgate, v_b_rgate), (1, D_RG)),
        ("w_igate", (2, 512, 512, 64), (w_igate, m_w_igate, v_w_igate), (512, 64)),
        ("b_igate", (1, R_BI, 1, D_RG), (b_igate, m_b_igate, v_b_igate), (1, D_RG)),
        ("lru_lambda", (1, R_LAM, 1, D_RG), (lru_lambda, m_lru_lambda, v_lru_lambda), (1, D_RG)),
        ("rg_norm_g", (1, R_GRG, 1, D_RG), (rg_norm_g, m_rg_norm_g, v_rg_norm_g), (1, D_RG)),
        ("hg_lower_bound", (1, R_HB0, 2, D_HG), (hg_lower_bound, m_hg_lower_bound, v_hg_lower_bound), (2, D_HG)),
        ("hg_norm_g", (1, R_GHG, 1, HD), (hg_norm_g, m_hg_norm_g, v_hg_norm_g), (1, HD)),
        ("ffn_norm_g", (0, R_GFFN, 1, D), (ffn_norm_g, m_ffn_norm_g, v_ffn_norm_g), (1, D)),
        ("final_norm_g", (0, R_GFIN, 1, D), (final_norm_g, m_final_norm_g, v_final_norm_g), (1, D)),
    ]
    res = _adamw_small(gathered, [s[1] for s in small_params],
                       [tuple(t.reshape(s[3]) for t in s[2]) for s in small_params])
    for i, s in enumerate(small_params):
        outs[s[0]] = [r.reshape(s[2][0].shape) for r in res[4 * i:4 * i + 4]]
    for n, ref in (("w_in", w_in), ("w_gate_up", w_gate_up), ("w_out", w_out), ("w_down", w_down)):
        outs[n] = [r.reshape(ref.shape) for r in outs[n]]

    loss_all = lax.psum(loss[0, 0], ("x", "y", "c"))
    order = ["meta_tokens", "mix_norm_g", "w_in", "conv_w", "conv_b", "w_rgate", "b_rgate", "w_igate", "b_igate",
             "lru_lambda", "rg_norm_g", "hg_lower_bound", "hg_norm_g", "w_out", "ffn_norm_g", "w_gate_up", "w_down",
             "final_norm_g"]
    return (loss_all, grad_x, *[outs[n][0] for n in order], *[outs[n][1] for n in order],
            *[outs[n][2] for n in order], *[outs[n][3] for n in order])
```

```python
import functools

import jax
import jax.numpy as jnp
from jax import lax
from jax.experimental import pallas as pl
from jax.experimental.pallas import tpu as pltpu

_BF = jnp.bfloat16
_F32 = jnp.float32
_S = jax.ShapeDtypeStruct
_MESH = pl.DeviceIdType.MESH

N_DEV = 8
N_META = 16
D = 1024
D_RG = 512
D_HG = 512
HD = 128
NH = D_HG // HD
D_IN = 3072
D_FF = 2816
FFB = D_FF // 4
WIN_B = D_IN // N_DEV
EPS = 1e-6
LRU_C = 8.0
TM = 256
HC = 64
VMEM_LIMIT = 56 * 1024 * 1024

ADAM_LR = 0.001
ADAM_B1 = 0.9
ADAM_B2 = 0.999
ADAM_EPS = 1e-08
ADAM_WD = 0.01
ADAM_STEP = 10

_SEND_ORDER = (6, 4, 2, 7, 5, 3, 1, 0)

R_CONVB, R_BR, R_BI, R_LAM, R_GRG, R_HB0, R_HB1, R_GHG, R_CONVW = 0, 1, 2, 3, 4, 5, 6, 7, 8
R_GMIX, R_GFFN, R_GFIN, R_META = 0, 1, 2, 8


def _cp(sem=None, **kw):
    return pltpu.CompilerParams(dimension_semantics=sem, vmem_limit_bytes=VMEM_LIMIT, **kw)


def _dot(a, b):
    return jnp.dot(a, b, preferred_element_type=_F32)


def _dot_nt(a, b):
    return lax.dot_general(a, b, (((1,), (1,)), ((), ())), preferred_element_type=_F32)


def _dot_tn(a, b):
    return lax.dot_general(a, b, (((0,), (0,)), ((), ())), preferred_element_type=_F32)


def _sigmoid(x):
    return jax.nn.sigmoid(x)


def _dsilu(x, s):
    return s * (1.0 + x * (1.0 - s))


_GELU_C = 0.7978845608028654


def _gelu_parts(x):
    t = jnp.tanh(_GELU_C * (x + 0.044715 * (x * x * x)))
    g = 0.5 * x * (1.0 + t)
    dg = 0.5 * (1.0 + t) + 0.5 * x * (1.0 - t * t) * (_GELU_C * (1.0 + 3.0 * 0.044715 * (x * x)))
    return g, dg


def _softplus(z):
    e = jnp.exp(-jnp.abs(z))
    w = 1.0 + e
    l1p = jnp.where(w == 1.0, e, jnp.log(w) * e / jnp.where(w == 1.0, 1.0, w - 1.0))
    return jnp.maximum(z, 0.0) + l1p


def _rms_fwd(x):
    r = lax.rsqrt(jnp.mean(x * x, axis=-1, keepdims=True) + EPS)
    return x * r, r


def _rms_bwd(dyg, n, r):
    return r * (dyg - n * jnp.mean(dyg * n, axis=-1, keepdims=True))


def _full(shape):
    nd = len(shape)
    return pl.BlockSpec(shape, lambda i: (0,) * nd)


def _const(shape):
    nd = len(shape)
    return pl.BlockSpec(shape, lambda i: (0,) * nd, pipeline_mode=pl.Buffered(1))


def _carry_gather(gather, i, nt):
    @pl.when(i == 0)
    def _():
        gather.start()

    def tail():
        for j in range(3):
            @pl.when(i == max(nt - 3 + j, 0))
            def _(j=j):
                gather.forward(j)

        @pl.when(i == nt - 1)
        def _():
            gather.finish()

    return tail


def _inproj(h0, g_mix, w_in, shards):
    t_pad = h0.shape[0]
    nt = t_pad // TM
    nsh = len(shards)

    def body(h_ref, g_ref, w_ref, *rest):
        p_ref, u_ref = rest[nsh:nsh + 2]
        tail = _carry_gather(_Gather(rest[:nsh], rest[nsh + 2:2 * nsh + 2], rest[2 * nsh + 2:]), pl.program_id(0), nt)
        n, _ = _rms_fwd(h_ref[...])
        u = (n * g_ref[...]).astype(_BF)
        u_ref[...] = u
        for j in range(N_DEV):
            p_ref[:, WIN_B * j:WIN_B * (j + 1)] = _dot(u, w_ref[j])
        tail()

    hbm = pl.BlockSpec(memory_space=pl.ANY)
    return pl.pallas_call(
        body, name="inproj", grid=(nt,),
        in_specs=[pl.BlockSpec((TM, D), lambda i: (i, 0)), _full((1, D)), _const((N_DEV, D, WIN_B))] + [hbm] * nsh,
        out_specs=[pl.BlockSpec((TM, D_IN), lambda i: (i, 0)), pl.BlockSpec((TM, D), lambda i: (i, 0))] + [hbm] * nsh,
        out_shape=[_S((t_pad, D_IN), _F32), _S((t_pad, D), _BF)] + [_S((N_DEV,) + s.shape, s.dtype) for s in shards],
        scratch_shapes=_sem_shapes(nsh),
        compiler_params=_cp(("arbitrary",)),
    )(h0, g_mix, w_in, *shards)


def _rg_gates(xc, wr_ref, wi_ref, vec_ref):
    xcb = xc.astype(_BF)
    r = _sigmoid(_dot(xcb, wr_ref[...]) + vec_ref[R_BR:R_BR + 1, :])
    ig = _sigmoid(_dot(xcb, wi_ref[...]) + vec_ref[R_BI:R_BI + 1, :])
    nsp8 = -LRU_C * _softplus(-vec_ref[R_LAM:R_LAM + 1, :])
    la = nsp8 * r
    a = jnp.exp(la)
    th = jnp.tanh(la)
    s = jnp.sqrt(-2.0 * th / (1.0 - th))
    return r, ig, a, s, nsp8


def _conv(xbuf, vec_ref):
    acc = vec_ref[R_CONVW:R_CONVW + 1, :] * xbuf[pl.ds(5, TM), :]
    for j in range(1, 4):
        acc = acc + vec_ref[R_CONVW + j:R_CONVW + j + 1, :] * xbuf[pl.ds(5 + j, TM), :]
    return vec_ref[R_CONVB:R_CONVB + 1, :] + acc


def _dot3(m01, x):
    hi = x.astype(_BF)
    r1 = x - hi.astype(_F32)
    mid = r1.astype(_BF)
    lo = (r1 - mid.astype(_F32)).astype(_BF)
    return (_dot(m01, lo) + _dot(m01, mid)) + _dot(m01, hi)


def _chunk_masks():
    row = lax.broadcasted_iota(jnp.int32, (TM, TM), 0)
    col = lax.broadcasted_iota(jnp.int32, (TM, TM), 1)
    shift = HC.bit_length() - 1
    same = lax.shift_right_logical(row, shift) == lax.shift_right_logical(col, shift)
    return same, same & (row >= col), same & (col >= row)


def _per_chunk_rows(x, r):
    return jnp.concatenate([jnp.broadcast_to(x[HC * c + r:HC * c + r + 1, :], (HC, x.shape[1]))
                            for c in range(TM // HC)], axis=0)


def _hg_prep(p_ref, lb, tri_blk):
    hq = p_ref[:, pl.ds(2 * D_RG, D_HG)]
    hf = p_ref[:, pl.ds(2 * D_RG + D_HG, D_HG)]
    sq = _sigmoid(hq)
    q = hq * sq
    sg = _sigmoid(hf)
    f = lb + (1.0 - lb) * sg
    k = 1.0 - f
    b = _dot3(tri_blk, jnp.log(f))
    bm = _per_chunk_rows(b, HC // 2 - 1)
    bl = _per_chunk_rows(b, HC - 1)
    e_q = jnp.exp(b - bm)
    e_k = jnp.exp(bm - b)
    e_b = jnp.exp(b)
    e_l = jnp.exp(bl - b)
    return dict(hq=hq, sq=sq, q=q, sg=sg, f=f, k=k, e_q=e_q, e_k=e_k, e_b=e_b, e_l=e_l,
                qd=q * e_q, kd=k * e_k, qe=q * e_b, ke=k * e_l, e_end=jnp.exp(bl))


def _mixer_fwd(p, wr, wi, vec, hb, g_hg, shards):
    t_pad = p.shape[0]
    nt = t_pad // TM
    nc_t = TM // HC
    nsh = len(shards)

    def body(p_ref, wr_ref, wi_ref, vec_ref, hb_ref, ghg_ref, *rest):
        sh_refs, rest = rest[:nsh], rest[nsh:]
        y_ref, hs_ref, o_ref, sc_ref = rest[:4]
        gath_refs, rest = rest[4:4 + nsh], rest[4 + nsh:]
        xbuf, a_s, b_s, hcar, st, qd_s, kd_s, qe_s, ke_s, v_s, u_s = rest[:11]
        i = pl.program_id(0)
        tail = _carry_gather(_Gather(sh_refs, gath_refs, rest[11:]), i, nt)

        @pl.when(i == 0)
        def _():
            xbuf[pl.ds(0, 8), :] = jnp.zeros((8, D_RG), _F32)
            hcar[...] = jnp.zeros_like(hcar)
            st[...] = jnp.zeros_like(st)

        x = p_ref[:, pl.ds(0, D_RG)]
        xbuf[pl.ds(8, TM), :] = x
        xc = _conv(xbuf, vec_ref)
        xbuf[pl.ds(0, 8), :] = x[TM - 8:, :]
        r, ig, a, s, _ = _rg_gates(xc, wr_ref, wi_ref, vec_ref)
        a_s[...] = a
        b_s[...] = s * (ig * xc)

        def step(t, h):
            h = a_s[pl.ds(t, 1), :] * h + b_s[pl.ds(t, 1), :]
            hs_ref[pl.ds(t, 1), :] = h
            return h

        hcar[pl.ds(0, 1), :] = lax.fori_loop(0, TM, step, hcar[pl.ds(0, 1), :], unroll=8)
        gel, _ = _gelu_parts(p_ref[:, pl.ds(D_RG, D_RG)])
        n, _ = _rms_fwd(gel * hs_ref[...])
        y_ref[:, pl.ds(0, D_RG)] = (n * vec_ref[R_GRG:R_GRG + 1, :]).astype(_BF)

        lb = _sigmoid(hb_ref[0:1, :] - hb_ref[1:2, :])
        _, tri_blk, _ = _chunk_masks()
        q = _hg_prep(p_ref, lb, tri_blk.astype(_BF))
        for name, ref in (("qd", qd_s), ("kd", kd_s), ("qe", qe_s), ("ke", ke_s)):
            ref[...] = q[name].astype(_BF)
        v_s[...] = p_ref[:, pl.ds(2 * D_RG + 2 * D_HG, D_HG)].astype(_BF)
        e_end = q["e_end"]
        causal = (lax.broadcasted_iota(jnp.int32, (HC, HC), 0) >= lax.broadcasted_iota(jnp.int32, (HC, HC), 1))
        for c in range(nc_t):
            for h in range(NH):
                rs, cs = pl.ds(HC * c, HC), pl.ds(HD * h, HD)
                amat = jnp.where(causal, _dot_nt(qd_s[rs, cs], kd_s[rs, cs]), 0.0)
                o_ref[rs, cs] = _dot(amat.astype(_BF), v_s[rs, cs])
                u_s[NH * c + h] = _dot_tn(v_s[rs, cs], ke_s[rs, cs])
        for h in range(NH):
            cs = pl.ds(HD * h, HD)
            s_run = st[h]
            for c in range(nc_t):
                rs = pl.ds(HC * c, HC)
                sc_ref[c, h] = s_run
                o_ref[rs, cs] += _dot_nt(qe_s[rs, cs], s_run.astype(_BF))
                s_run = e_end[HC * c:HC * c + 1, HD * h:HD * (h + 1)] * s_run + u_s[NH * c + h]
            st[h] = s_run
        for h in range(NH):
            cs = pl.ds(HD * h, HD)
            n_o, _ = _rms_fwd(o_ref[:, cs])
            hg = p_ref[:, pl.ds(2 * D_RG + 3 * D_HG + HD * h, HD)]
            y_ref[:, pl.ds(D_RG + HD * h, HD)] = ((n_o * ghg_ref[...]) * (hg * _sigmoid(hg))).astype(_BF)

        tail()

    hbm = pl.BlockSpec(memory_space=pl.ANY)
    return pl.pallas_call(
        body, name="mixer_fwd", grid=(nt,),
        in_specs=[pl.BlockSpec((TM, D_IN), lambda i: (i, 0)), _full((D_RG, D_RG)), _full((D_RG, D_RG)),
                  _full((16, D_RG)), _full((2, D_HG)), _full((1, HD))] + [hbm] * nsh,
        out_specs=[pl.BlockSpec((TM, D), lambda i: (i, 0)), pl.BlockSpec((TM, D_RG), lambda i: (i, 0)),
                   pl.BlockSpec((TM, D_HG), lambda i: (i, 0)),
                   pl.BlockSpec((nc_t, NH, HD, HD), lambda i: (i, 0, 0, 0))] + [hbm] * nsh,
        out_shape=[_S((t_pad, D), _BF), _S((t_pad, D_RG), _F32), _S((t_pad, D_HG), _F32),
                   _S((t_pad // HC, NH, HD, HD), _F32)] + [_S((N_DEV,) + s.shape, s.dtype) for s in shards],
        scratch_shapes=[pltpu.VMEM((TM + 8, D_RG), _F32), pltpu.VMEM((TM, D_RG), _F32),
                        pltpu.VMEM((TM, D_RG), _F32), pltpu.VMEM((8, D_RG), _F32),
                        pltpu.VMEM((NH, HD, HD), _F32)] + [pltpu.VMEM((TM, D_HG), _BF) for _ in range(5)]
        + [pltpu.VMEM((nc_t * NH, HD, HD), _F32)] + _sem_shapes(nsh),
        compiler_params=_cp(("arbitrary",)),
    )(p, wr, wi, vec, hb, g_hg, *shards)


def _outproj(h0, y, w_out, g_ffn):
    t_pad = h0.shape[0]

    def body(h_ref, y_ref, w_ref, g_ref, h1_ref, v_ref):
        h1 = h_ref[...] + _dot(y_ref[...], w_ref[...])
        h1_ref[...] = h1
        n, _ = _rms_fwd(h1)
        v_ref[...] = (n * g_ref[...]).astype(_BF)

    return pl.pallas_call(
        body, name="outproj", grid=(t_pad // TM,),
        in_specs=[pl.BlockSpec((TM, D), lambda i: (i, 0)), pl.BlockSpec((TM, D), lambda i: (i, 0)),
                  _full((D, D)), _full((1, D))],
        out_specs=[pl.BlockSpec((TM, D), lambda i: (i, 0)), pl.BlockSpec((TM, D), lambda i: (i, 0))],
        out_shape=[_S((t_pad, D), _F32), _S((t_pad, D), _BF)],
        compiler_params=_cp(("arbitrary",)),
    )(h0, y, w_out, g_ffn)


def _ffn_loss(v, h1, w_gu, w_down, g_fin, tgt, n_valid):
    t_pad = v.shape[0]

    def body(v_ref, h1_ref, wgu_ref, wd_ref, g_ref, t_ref, gu_ref, act_ref, dh2_ref, dh2b_ref, loss_ref, gfin_ref):
        i = pl.program_id(0)

        @pl.when(i == 0)
        def _():
            loss_ref[...] = jnp.zeros_like(loss_ref)
            gfin_ref[...] = jnp.zeros_like(gfin_ref)

        vb = v_ref[...]
        h2 = h1_ref[...]
        for b in range(4):
            gate = _dot(vb, wgu_ref[b])
            up = _dot(vb, wgu_ref[4 + b])
            gu_ref[b] = gate
            gu_ref[4 + b] = up
            act = ((gate * _sigmoid(gate)) * up).astype(_BF)
            act_ref[b] = act
            h2 = h2 + _dot(act, wd_ref[b])
        n, r = _rms_fwd(h2)
        out = n * g_ref[...]
        row = i * TM + lax.broadcasted_iota(jnp.int32, (TM, 1), 0)
        valid = (row >= N_META) & (row < n_valid)
        err = jnp.where(valid, out - t_ref[...], 0.0)
        loss_ref[...] += (0.5 / D) * jnp.sum(err * err)
        dout = err * (1.0 / D)
        gfin_ref[...] += jnp.sum(dout * n, axis=0, keepdims=True)
        dh2 = _rms_bwd(dout * g_ref[...], n, r)
        dh2_ref[...] = dh2
        dh2b_ref[...] = dh2.astype(_BF)

    return pl.pallas_call(
        body, name="ffn_loss", grid=(t_pad // TM,),
        in_specs=[pl.BlockSpec((TM, D), lambda i: (i, 0)), pl.BlockSpec((TM, D), lambda i: (i, 0)),
                  _const((N_DEV, D, FFB)), _const((4, FFB, D)), _full((1, D)),
                  pl.BlockSpec((TM, D), lambda i: (i, 0))],
        out_specs=[pl.BlockSpec((N_DEV, TM, FFB), lambda i: (0, i, 0)), pl.BlockSpec((4, TM, FFB), lambda i: (0, i, 0)),
                   pl.BlockSpec((TM, D), lambda i: (i, 0)), pl.BlockSpec((TM, D), lambda i: (i, 0)),
                   _full((8, 128)), _full((1, D))],
        out_shape=[_S((N_DEV, t_pad, FFB), _F32), _S((4, t_pad, FFB), _BF), _S((t_pad, D), _F32),
                   _S((t_pad, D), _BF), _S((8, 128), _F32), _S((1, D), _F32)],
        compiler_params=_cp(("arbitrary",)),
    )(v, h1, w_gu, w_down, g_fin, tgt)


def _ffn_bwd(dh2, dh2b, gu, h1, g_ffn, w_gu, w_down, w_out):
    t_pad = dh2.shape[0]

    def body(dh2_ref, dh2b_ref, gu_ref, h1_ref, g_ref, wgu_ref, wd_ref, wo_ref,
             dgu_ref, dh1_ref, dh1b_ref, dy_ref, gffn_ref):
        i = pl.program_id(0)

        @pl.when(i == 0)
        def _():
            gffn_ref[...] = jnp.zeros_like(gffn_ref)

        db = dh2b_ref[...]
        dv = jnp.zeros((TM, D), _F32)
        for b in range(4):
            dact = _dot_nt(db, wd_ref[b])
            gate = gu_ref[b]
            up = gu_ref[4 + b]
            sg = _sigmoid(gate)
            dgate = ((dact * up) * _dsilu(gate, sg)).astype(_BF)
            dup = (dact * (gate * sg)).astype(_BF)
            dgu_ref[b] = dgate
            dgu_ref[4 + b] = dup
            dv = dv + _dot_nt(dgate, wgu_ref[b]) + _dot_nt(dup, wgu_ref[4 + b])
        n, r = _rms_fwd(h1_ref[...])
        gffn_ref[...] += jnp.sum(dv * n, axis=0, keepdims=True)
        dh1 = dh2_ref[...] + _rms_bwd(dv * g_ref[...], n, r)
        dh1_ref[...] = dh1
        dh1b = dh1.astype(_BF)
        dh1b_ref[...] = dh1b
        dy_ref[...] = _dot_nt(dh1b, wo_ref[...])

    tile = pl.BlockSpec((TM, D), lambda i: (i, 0))
    return pl.pallas_call(
        body, name="ffn_bwd", grid=(t_pad // TM,),
        in_specs=[tile, tile, pl.BlockSpec((N_DEV, TM, FFB), lambda i: (0, i, 0)), tile, _full((1, D)),
                  _const((N_DEV, D, FFB)), _const((4, FFB, D)), _const((D, D))],
        out_specs=[pl.BlockSpec((N_DEV, TM, FFB), lambda i: (0, i, 0)), tile, tile, tile, _full((1, D))],
        out_shape=[_S((N_DEV, t_pad, FFB), _BF), _S((t_pad, D), _F32), _S((t_pad, D), _BF),
                   _S((t_pad, D), _F32), _S((1, D), _F32)],
        compiler_params=_cp(("arbitrary",)),
    )(dh2, dh2b, gu, h1, g_ffn, w_gu, w_down, w_out)


def _mixer_bwd(p, hs, o, sc, dy, wr, wi, vec, hb, g_hg, scatter):
    t_pad = p.shape[0]
    nt = t_pad // TM
    nc_t = TM // HC
    nsc = len(scatter)

    def rev(i):
        return nt - 1 - i

    def body(p_ref, pprev_ref, hs_ref, hprev_ref, o_ref, sc_ref, dy_ref, wr_ref, wi_ref, vec_ref, hb_ref, ghg_ref,
             *rest):
        send_refs, rest = rest[:nsc], rest[nsc:]
        dp_ref, gvec_ref, gw_ref = rest[:3]
        recv_refs, rest = rest[3:3 + nsc], rest[3 + nsc:]
        xbuf, hbuf, dbuf, a_s, g_s, ccar, dst = rest[:7]
        qd_s, kd_s, qe_s, ke_s, v_s, do_s, dqd_s, dkd_s, dqe_s, dke_s, dv_s, w_s, dend_s = rest[7:20]
        exchange = _Exchange(send_refs, [], recv_refs, rest[20:])
        i = pl.program_id(0)
        first_tile = i == nt - 1

        @pl.when(i == 0)
        def _():
            exchange.start()
            gvec_ref[...] = jnp.zeros_like(gvec_ref)
            gw_ref[...] = jnp.zeros_like(gw_ref)
            dbuf[pl.ds(TM, 8), :] = jnp.zeros((8, D_RG), _F32)
            ccar[...] = jnp.zeros_like(ccar)
            dst[...] = jnp.zeros_like(dst)

        def acc(row, val):
            gvec_ref[row:row + 1, :] += jnp.sum(val, axis=0, keepdims=True)

        keep = jnp.where(first_tile, 0.0, 1.0)
        x = p_ref[:, pl.ds(0, D_RG)]
        xbuf[pl.ds(0, 8), :] = pprev_ref[...] * keep
        xbuf[pl.ds(8, TM), :] = x
        xc = _conv(xbuf, vec_ref)
        r, ig, a, s, nsp8 = _rg_gates(xc, wr_ref, wi_ref, vec_ref)
        h = hs_ref[...]
        hbuf[pl.ds(0, 8), :] = hprev_ref[...] * keep
        hbuf[pl.ds(8, TM), :] = h
        hm1 = hbuf[pl.ds(7, TM), :]
        gr = p_ref[:, pl.ds(D_RG, D_RG)]
        gel, dgel = _gelu_parts(gr)
        n, rr = _rms_fwd(gel * h)
        dyn = dy_ref[:, pl.ds(0, D_RG)]
        acc(R_GRG, dyn * n)
        dpre = _rms_bwd(dyn * vec_ref[R_GRG:R_GRG + 1, :], n, rr)
        dp_ref[:, pl.ds(D_RG, D_RG)] = ((dpre * h) * dgel).astype(_BF)
        a_s[...] = a
        g_s[...] = dpre * gel

        def step(k, c):
            t = TM - 1 - k
            g = g_s[pl.ds(t, 1), :] + c
            g_s[pl.ds(t, 1), :] = g
            return a_s[pl.ds(t, 1), :] * g

        ccar[pl.ds(0, 1), :] = lax.fori_loop(0, TM, step, ccar[pl.ds(0, 1), :], unroll=8)
        gt = g_s[...]
        da = gt * hm1
        ixc = ig * xc
        ds = gt * ixc
        dig = (gt * s) * xc
        dxc = (gt * s) * ig
        dla = da * a - ds * ((a * a) / s)
        lam = vec_ref[R_LAM:R_LAM + 1, :]
        gvec_ref[R_LAM:R_LAM + 1, :] += jnp.sum(dla * r, axis=0, keepdims=True) * (LRU_C * _sigmoid(-lam))
        dzr = (dla * nsp8) * (r * (1.0 - r))
        dzi = dig * (ig * (1.0 - ig))
        acc(R_BR, dzr)
        acc(R_BI, dzi)
        xcb = xc.astype(_BF)
        dzrb = dzr.astype(_BF)
        dzib = dzi.astype(_BF)
        gw_ref[0] += _dot_tn(xcb, dzrb)
        gw_ref[1] += _dot_tn(xcb, dzib)
        dxc = dxc + _dot_nt(dzrb, wr_ref[...]) + _dot_nt(dzib, wi_ref[...])
        acc(R_CONVB, dxc)
        for j in range(4):
            acc(R_CONVW + j, dxc * xbuf[pl.ds(5 + j, TM), :])
        dbuf[pl.ds(0, TM), :] = dxc
        dx = vec_ref[R_CONVW + 3:R_CONVW + 4, :] * dxc
        for j in range(3):
            dx = dx + vec_ref[R_CONVW + j:R_CONVW + j + 1, :] * dbuf[pl.ds(3 - j, TM), :]
        dbuf[pl.ds(TM, 8), :] = dxc[0:8, :]
        dp_ref[:, pl.ds(0, D_RG)] = dx.astype(_BF)

        lb = _sigmoid(hb_ref[0:1, :] - hb_ref[1:2, :])
        same, tri_blk, triu_blk = _chunk_masks()
        q = _hg_prep(p_ref, lb, tri_blk.astype(_BF))
        qdb, kdb = q["qd"].astype(_BF), q["kd"].astype(_BF)
        qd_s[...] = qdb
        kd_s[...] = kdb
        qe_s[...] = q["qe"].astype(_BF)
        ke_s[...] = q["ke"].astype(_BF)
        v_s[...] = p_ref[:, pl.ds(2 * D_RG + 2 * D_HG, D_HG)].astype(_BF)
        e_end = q["e_end"]
        ghg = ghg_ref[...]
        for h in range(NH):
            cs = pl.ds(HD * h, HD)
            hg = p_ref[:, pl.ds(2 * D_RG + 3 * D_HG + HD * h, HD)]
            sh = _sigmoid(hg)
            n_o, r_o = _rms_fwd(o_ref[:, cs])
            dyh = dy_ref[:, pl.ds(D_RG + HD * h, HD)]
            dp_ref[:, pl.ds(2 * D_RG + 3 * D_HG + HD * h, HD)] = ((dyh * (n_o * ghg)) * _dsilu(hg, sh)).astype(_BF)
            dn = dyh * (hg * sh)
            gvec_ref[R_GHG:R_GHG + 1, pl.ds(0, HD)] += jnp.sum(dn * n_o, axis=0, keepdims=True)
            do_s[:, cs] = _rms_bwd(dn * ghg, n_o, r_o).astype(_BF)
        causal = (lax.broadcasted_iota(jnp.int32, (HC, HC), 0) >= lax.broadcasted_iota(jnp.int32, (HC, HC), 1))
        for c in range(nc_t):
            for h in range(NH):
                rs, cs = pl.ds(HC * c, HC), pl.ds(HD * h, HD)
                qd_c, kd_c, do_c = qd_s[rs, cs], kd_s[rs, cs], do_s[rs, cs]
                amat = jnp.where(causal, _dot_nt(qd_c, kd_c), 0.0).astype(_BF)
                da_m = jnp.where(causal, _dot_nt(do_c, v_s[rs, cs]), 0.0).astype(_BF)
                dqd_s[rs, cs] = _dot(da_m, kd_c)
                dkd_s[rs, cs] = _dot_tn(da_m, qd_c)
                dqe_s[rs, cs] = _dot(do_c, sc_ref[c, h].astype(_BF))
                dv_s[rs, cs] = _dot_tn(amat, do_c)
                w_s[NH * c + h] = _dot_tn(do_c, qe_s[rs, cs])
        for h in range(NH):
            cs = pl.ds(HD * h, HD)
            d_run = dst[h]
            for c in reversed(range(nc_t)):
                rs = pl.ds(HC * c, HC)
                d_b = d_run.astype(_BF)
                dke_s[rs, cs] = _dot(v_s[rs, cs], d_b)
                dp_ref[rs, pl.ds(2 * D_RG + 2 * D_HG + HD * h, HD)] = (
                    dv_s[rs, cs] + _dot_nt(ke_s[rs, cs], d_b)).astype(_BF)
                dend_s[pl.ds(c, 1), cs] = jnp.sum(sc_ref[c, h] * d_run, axis=0, keepdims=True)
                d_run = w_s[NH * c + h] + e_end[HC * c:HC * c + 1, HD * h:HD * (h + 1)] * d_run
            dst[h] = d_run
        dqd, dkd, dqe, dke = dqd_s[...], dkd_s[...], dqe_s[...], dke_s[...]
        dq = dqd * q["e_q"] + dqe * q["e_b"]
        dk = dkd * q["e_k"] + dke * q["e_l"]
        dkeke = dke * q["ke"]
        db = dqd * qdb.astype(_F32) - dkd * kdb.astype(_F32) + dqe * q["qe"] - dkeke
        d_end = jnp.concatenate([jnp.broadcast_to(dend_s[pl.ds(c, 1), :], (HC, D_HG)) for c in range(nc_t)], axis=0)
        dlf = _dot3(triu_blk.astype(_BF), db) + _dot3(same.astype(_BF), dkeke) + d_end * e_end
        df = dlf / q["f"] - dk
        sg = q["sg"]
        gvec_ref[R_HB0:R_HB0 + 1, :] += jnp.sum(df * (1.0 - sg), axis=0, keepdims=True)
        dp_ref[:, pl.ds(2 * D_RG, D_HG)] = (dq * _dsilu(q["hq"], q["sq"])).astype(_BF)
        dp_ref[:, pl.ds(2 * D_RG + D_HG, D_HG)] = ((df * (1.0 - lb)) * (sg * (1.0 - sg))).astype(_BF)

        @pl.when(i == nt - 1)
        def _():
            glb = gvec_ref[R_HB0:R_HB0 + 1, :] * (lb * (1.0 - lb))
            gvec_ref[R_HB0:R_HB0 + 1, :] = glb
            gvec_ref[R_HB1:R_HB1 + 1, :] = -glb
            exchange.finish()

    hbm = pl.BlockSpec(memory_space=pl.ANY)
    return pl.pallas_call(
        body, name="mixer_bwd", grid=(nt,),
        in_specs=[pl.BlockSpec((TM, D_IN), lambda i: (rev(i), 0)),
                  pl.BlockSpec((8, D_RG), lambda i: (jnp.maximum(rev(i) * (TM // 8) - 1, 0), 0)),
                  pl.BlockSpec((TM, D_RG), lambda i: (rev(i), 0)),
                  pl.BlockSpec((8, D_RG), lambda i: (jnp.maximum(rev(i) * (TM // 8) - 1, 0), 0)),
                  pl.BlockSpec((TM, D_HG), lambda i: (rev(i), 0)),
                  pl.BlockSpec((nc_t, NH, HD, HD), lambda i: (rev(i), 0, 0, 0)),
                  pl.BlockSpec((TM, D), lambda i: (rev(i), 0)),
                  _full((D_RG, D_RG)), _full((D_RG, D_RG)), _full((16, D_RG)), _full((2, D_HG)), _full((1, HD))]
        + [hbm] * nsc,
        out_specs=[pl.BlockSpec((TM, D_IN), lambda i: (rev(i), 0)), _full((16, D_RG)), _full((2, D_RG, D_RG))]
        + [hbm] * nsc,
        out_shape=[_S((t_pad, D_IN), _BF), _S((16, D_RG), _F32), _S((2, D_RG, D_RG), _F32)]
        + [_S(s.shape, s.dtype) for s in scatter],
        scratch_shapes=[pltpu.VMEM((TM + 8, D_RG), _F32), pltpu.VMEM((TM + 8, D_RG), _F32),
                        pltpu.VMEM((TM + 8, D_RG), _F32), pltpu.VMEM((TM, D_RG), _F32),
                        pltpu.VMEM((TM, D_RG), _F32), pltpu.VMEM((8, D_RG), _F32),
                        pltpu.VMEM((NH, HD, HD), _F32)]
        + [pltpu.VMEM((TM, D_HG), _BF) for _ in range(6)] + [pltpu.VMEM((TM, D_HG), _F32) for _ in range(5)]
        + [pltpu.VMEM((nc_t * NH, HD, HD), _F32), pltpu.VMEM((8, D_HG), _F32)] + _sem_shapes(nsc),
        compiler_params=_cp(("arbitrary",)),
    )(p, p, hs, hs, o, sc, dy, wr, wi, vec, hb, g_hg, *scatter)


def _inproj_bwd_send(dp, w_in, h0, dh1, g_mix, u, order, gffn, gfin, to_all):
    t_pad = dp.shape[0]
    rb = t_pad // (2 * N_DEV)
    n_steps = N_DEV + 2 * N_DEV
    na = len(to_all)

    def body(order_ref, dpc_ref, dpr_ref, u_ref, w_ref, h_ref, dh1_ref, g_ref, gffn_ref, gfin_ref, *rest):
        all_in = rest[:na]
        dh0_ref, recv_ref = rest[na:na + 2]
        all_out = rest[na + 2:2 * na + 2]
        alla_ref = rest[2 * na + 2]
        buf, pack, blk_send, blk_recv, blk_local = rest[2 * na + 3:2 * na + 8]
        exchange = _Exchange([], all_in, all_out, rest[2 * na + 8:2 * na + 11])
        last = _Exchange([], [pack], [alla_ref], rest[2 * na + 11:])
        s = pl.program_id(0)
        x, y, c = _coords()
        me = 4 * x + 2 * y + c

        def send(step):
            r = _SEND_ORDER[step]
            return pltpu.make_async_remote_copy(
                src_ref=buf.at[step], dst_ref=recv_ref.at[me], send_sem=blk_send.at[step], recv_sem=blk_recv.at[r - 1],
                device_id=(x ^ (r >> 2), y ^ ((r >> 1) & 1), c ^ (r & 1)), device_id_type=_MESH)

        @pl.when(s == 0)
        def _():
            exchange.start()
            pack[...] = jnp.zeros_like(pack)

        @pl.when(s < N_DEV)
        def _():
            buf[s] = _dot_tn(u_ref[...], dpc_ref[...]).astype(_BF)

            for step in range(N_DEV - 1):
                @pl.when(s == step)
                def _(step=step):
                    send(step).start()

        @pl.when(s >= N_DEV)
        def _():
            du = jnp.zeros((rb, D), _F32)
            for j in range(N_DEV):
                du = du + _dot_nt(dpr_ref[:, WIN_B * j:WIN_B * (j + 1)], w_ref[j])
            n, r = _rms_fwd(h_ref[...])
            pack[R_GMIX:R_GMIX + 1, :] += jnp.sum(du * n, axis=0, keepdims=True)
            dh0 = dh1_ref[...] + _rms_bwd(du * g_ref[...], n, r)
            dh0_ref[...] = dh0

            @pl.when(s == N_DEV)
            def _():
                pack[R_META:R_META + N_META, :] = dh0[0:N_META, :]

        @pl.when(s == n_steps - 1)
        def _():
            pack[R_GFFN:R_GFFN + 1, :] = gffn_ref[...]
            pack[R_GFIN:R_GFIN + 1, :] = gfin_ref[...]
            last.start()
            mine = pltpu.make_async_copy(buf.at[N_DEV - 1], recv_ref.at[me], blk_local.at[0])
            mine.start()
            for step in range(N_DEV - 1):
                send(step).wait_send()
            for r in range(1, N_DEV):
                px, py, pc = x ^ (r >> 2), y ^ ((r >> 1) & 1), c ^ (r & 1)
                pltpu.make_async_remote_copy(
                    src_ref=buf.at[0], dst_ref=recv_ref.at[4 * px + 2 * py + pc], send_sem=blk_send.at[0],
                    recv_sem=blk_recv.at[r - 1], device_id=(px, py, pc), device_id_type=_MESH).wait_recv()
            mine.wait()
            exchange.finish()
            last.finish()

    hbm = pl.BlockSpec(memory_space=pl.ANY)
    rows = pl.BlockSpec((rb, D), lambda s, order: (jnp.maximum(s - N_DEV, 0), 0))
    one = pl.BlockSpec((1, D), lambda s, order: (0, 0))
    res = pl.pallas_call(
        body, name="inproj_bwd_send",
        grid_spec=pltpu.PrefetchScalarGridSpec(
            num_scalar_prefetch=1, grid=(n_steps,),
            in_specs=[pl.BlockSpec((t_pad, WIN_B), lambda s, order: (0, order[jnp.minimum(s, N_DEV - 1)])),
                      pl.BlockSpec((rb, D_IN), lambda s, order: (jnp.maximum(s - N_DEV, 0), 0)),
                      pl.BlockSpec((t_pad, D), lambda s, order: (0, 0), pipeline_mode=pl.Buffered(1)),
                      pl.BlockSpec((N_DEV, D, WIN_B), lambda s, order: (0, 0, 0), pipeline_mode=pl.Buffered(1)),
                      rows, rows, one, one, one] + [hbm] * na,
            out_specs=[rows] + [hbm] * (na + 2),
            scratch_shapes=[pltpu.VMEM((N_DEV, D, WIN_B), _BF), pltpu.VMEM((24, D), _F32),
                            pltpu.SemaphoreType.DMA((N_DEV - 1,)), pltpu.SemaphoreType.DMA((N_DEV - 1,)),
                            pltpu.SemaphoreType.DMA((1,))] + _sem_shapes(na) + _sem_shapes(1)),
        out_shape=[_S((t_pad, D), _F32), _S((N_DEV, D, WIN_B), _BF)]
        + [_S((N_DEV,) + g.shape, g.dtype) for g in to_all] + [_S((N_DEV, 24, D), _F32)],
        compiler_params=_cp(("arbitrary",)),
    )(order, dp, dp, u, w_in, h0, dh1, g_mix, gffn, gfin, *to_all)
    return res


def _wgrad(name, a, b, a_spec, b_spec, n_blocks, out_block, scatter=()):
    nsc = len(scatter)

    def body(a_ref, b_ref, *rest):
        o_ref = rest[nsc]
        j = pl.program_id(0)
        if nsc:
            exchange = _Exchange(rest[:nsc], [], rest[nsc + 1:2 * nsc + 1], rest[2 * nsc + 1:])

            @pl.when(j == 0)
            def _():
                exchange.start()

        av = a_ref[0] if len(a_ref.shape) == 3 else a_ref[...]
        bv = b_ref[0] if len(b_ref.shape) == 3 else b_ref[...]
        o_ref[0] = _dot_tn(av, bv).astype(_BF)

        if nsc:
            @pl.when(j == n_blocks - 1)
            def _():
                exchange.finish()

    hbm = pl.BlockSpec(memory_space=pl.ANY)
    res = pl.pallas_call(
        body, name=name, grid=(n_blocks,),
        in_specs=[a_spec, b_spec] + [hbm] * nsc,
        out_specs=[pl.BlockSpec((1,) + out_block, lambda j: (j, 0, 0))] + [hbm] * nsc,
        out_shape=[_S((n_blocks,) + out_block, _BF)] + [_S(s.shape, s.dtype) for s in scatter],
        scratch_shapes=_sem_shapes(nsc) if nsc else [],
        compiler_params=_cp(("arbitrary",)),
    )(a, b, *scatter)
    return res if nsc else res[0]


def _coords():
    return lax.axis_index("x"), lax.axis_index("y"), lax.axis_index("c")


def _sem_shapes(na):
    return [pltpu.SemaphoreType.DMA((7 * na,)), pltpu.SemaphoreType.DMA((7 * na,)), pltpu.SemaphoreType.DMA((na,))]


class _Gather:
    def __init__(self, srcs, outs, sems):
        self.srcs, self.outs = srcs, outs
        self.send_sems, self.recv_sems, self.local_sems = sems
        self.na = len(srcs)
        x, y, c = _coords()
        self.pos = (x, y, c)
        self.me = 4 * x + 2 * y + c
        self.sibling = (x, y, 1 - c)
        self.chips = [(1 - x, y), (x, 1 - y), (1 - x, 1 - y)]

    @staticmethod
    def _slot(px, py, pc):
        return 4 * px + 2 * py + pc

    def _copy(self, a, k, block, to, own=False):
        return pltpu.make_async_remote_copy(
            src_ref=self.srcs[a] if own else self.outs[a].at[block], dst_ref=self.outs[a].at[block],
            send_sem=self.send_sems.at[7 * a + k], recv_sem=self.recv_sems.at[7 * a + k],
            device_id=to, device_id_type=_MESH)

    def _mine(self, a):
        return pltpu.make_async_copy(self.srcs[a], self.outs[a].at[self.me], self.local_sems.at[a])

    def _first(self):
        c = self.pos[2]
        cps = []
        for a in range(self.na):
            cps.append(self._copy(a, 0, self.me, self.sibling, own=True))
            cps += [self._copy(a, 1 + j, self.me, (*chip, c), own=True) for j, chip in enumerate(self.chips)]
        return cps

    def _passed(self):
        c = self.pos[2]
        return [self._copy(a, 4 + j, self._slot(*chip, c), self.sibling)
                for j, chip in enumerate(self.chips) for a in range(self.na)]

    def start(self):
        for a in range(self.na):
            self._mine(a).start()
        for cp in self._first():
            cp.start()

    def forward(self, j):
        c = self.pos[2]
        chip = self.chips[j]
        for a in range(self.na):
            self._copy(a, 1 + j, self._slot(*chip, c), self.pos).wait_recv()
            self._copy(a, 4 + j, self._slot(*chip, c), self.sibling).start()

    def finish(self):
        x, y, c = self.pos
        for a in range(self.na):
            self._copy(a, 0, self._slot(x, y, 1 - c), self.pos).wait_recv()
        for j, chip in enumerate(self.chips):
            for a in range(self.na):
                self._copy(a, 4 + j, self._slot(*chip, 1 - c), self.pos).wait_recv()
        for cp in self._first() + self._passed():
            cp.wait_send()
        for a in range(self.na):
            self._mine(a).wait()


class _Exchange:
    def __init__(self, scatter, gather, outs, sems):
        self.ins = list(scatter) + list(gather)
        self.ns, self.na = len(scatter), len(scatter) + len(gather)
        self.outs = outs
        self.send_sems, self.recv_sems, self.local_sems = sems
        x, y, c = _coords()
        self.pos = (x, y, c)
        self.me = 4 * x + 2 * y + c

    def _peer(self, r):
        x, y, c = self.pos
        return x ^ (r >> 2), y ^ ((r >> 1) & 1), c ^ (r & 1)

    def _src(self, a, block):
        return self.ins[a].at[block] if a < self.ns else self.ins[a]

    def _local(self, a):
        return pltpu.make_async_copy(self._src(a, self.me), self.outs[a].at[self.me], self.local_sems.at[a])

    def _send(self, a, r):
        px, py, pc = self._peer(r)
        return pltpu.make_async_remote_copy(
            src_ref=self._src(a, 4 * px + 2 * py + pc), dst_ref=self.outs[a].at[self.me],
            send_sem=self.send_sems.at[7 * a + r - 1], recv_sem=self.recv_sems.at[7 * a + r - 1],
            device_id=(px, py, pc), device_id_type=_MESH)

    def _recv(self, a, r):
        px, py, pc = self._peer(r)
        return pltpu.make_async_remote_copy(
            src_ref=self._src(a, self.me), dst_ref=self.outs[a].at[4 * px + 2 * py + pc],
            send_sem=self.send_sems.at[7 * a + r - 1], recv_sem=self.recv_sems.at[7 * a + r - 1],
            device_id=(px, py, pc), device_id_type=_MESH)

    def start(self):
        for a in range(self.na):
            self._local(a).start()
        for r in range(1, N_DEV):
            for a in range(self.na):
                self._send(a, r).start()

    def finish(self):
        for r in range(1, N_DEV):
            for a in range(self.na):
                self._recv(a, r).wait_recv()
        for r in range(1, N_DEV):
            for a in range(self.na):
                self._send(a, r).wait_send()
        for a in range(self.na):
            self._local(a).wait()


def _allgather_first(gather_f32, cast_f32, gather_dtypes):
    ng, nc = len(gather_f32), len(cast_f32)

    def body(*refs):
        ins, cins = refs[:ng], refs[ng:ng + nc]
        outs, couts = refs[ng + nc:2 * ng + nc], refs[2 * ng + nc:2 * ng + 2 * nc]
        stage = refs[2 * ng + 2 * nc:3 * ng + 2 * nc]
        sems = refs[3 * ng + 2 * nc:]
        for a in range(ng):
            stage[a][...] = ins[a][...].astype(gather_dtypes[a])
        g = _Gather(stage, outs, sems)
        g.start()
        for a in range(nc):
            couts[a][...] = cins[a][...].astype(_BF)
        for j in range(3):
            g.forward(j)
        g.finish()

    vm = pl.BlockSpec(memory_space=pltpu.VMEM)
    return pl.pallas_call(
        body, name="allgather_first",
        in_specs=[vm] * (ng + nc),
        out_specs=[pl.BlockSpec(memory_space=pl.ANY)] * ng + [vm] * nc,
        out_shape=[_S((N_DEV,) + l.shape, dt) for l, dt in zip(gather_f32, gather_dtypes)]
        + [_S(l.shape, _BF) for l in cast_f32],
        scratch_shapes=[pltpu.VMEM(l.shape, dt) for l, dt in zip(gather_f32, gather_dtypes)] + _sem_shapes(ng),
        compiler_params=pltpu.CompilerParams(vmem_limit_bytes=VMEM_LIMIT),
    )(*gather_f32, *cast_f32)


def _adamw_math(w, g, m, v):
    m2 = ADAM_B1 * m + (1.0 - ADAM_B1) * g
    v2 = ADAM_B2 * v + (1.0 - ADAM_B2) * (g * g)
    m_hat = m2 / (1.0 - ADAM_B1 ** ADAM_STEP)
    v_hat = v2 / (1.0 - ADAM_B2 ** ADAM_STEP)
    delta = -ADAM_LR * (m_hat / (jnp.sqrt(v_hat) + ADAM_EPS) + ADAM_WD * w)
    return delta, m2, v2


def _adamw_big(name, recv, w, m, v, rows):
    r_all, c_all = w.shape

    def body(r_ref, w_ref, m_ref, v_ref, g_out, d_out, m_out, v_out):
        g = r_ref[0].astype(_F32)
        for k in range(1, N_DEV):
            g = g + r_ref[k].astype(_F32)
        delta, m2, v2 = _adamw_math(w_ref[...], g, m_ref[...], v_ref[...])
        g_out[...] = g
        d_out[...] = delta
        m_out[...] = m2
        v_out[...] = v2

    tile = pl.BlockSpec((rows, c_all), lambda i: (i, 0))
    return pl.pallas_call(
        body, name=name, grid=(r_all // rows,),
        in_specs=[pl.BlockSpec((N_DEV, rows, c_all), lambda i: (0, i, 0)), tile, tile, tile],
        out_specs=[tile] * 4,
        out_shape=[_S(w.shape, _F32)] * 4,
        compiler_params=_cp(("arbitrary",)),
    )(recv, w, m, v)


def _adamw_small(gathered, slices, wmv):
    ng, npar = len(gathered), len(slices)

    def body(*refs):
        g_refs = refs[:ng]
        wmv_refs = refs[ng:ng + 3 * npar]
        outs = refs[ng + 3 * npar:]
        for i, (ai, r0, nr, c0, ncol) in enumerate(slices):
            g = g_refs[ai][0, pl.ds(r0, nr), pl.ds(c0, ncol)].astype(_F32)
            for k in range(1, N_DEV):
                g = g + g_refs[ai][k, pl.ds(r0, nr), pl.ds(c0, ncol)].astype(_F32)
            w_ref, m_ref, v_ref = wmv_refs[3 * i:3 * i + 3]
            delta, m2, v2 = _adamw_math(w_ref[...], g, m_ref[...], v_ref[...])
            outs[4 * i][...] = g
            outs[4 * i + 1][...] = delta
            outs[4 * i + 2][...] = m2
            outs[4 * i + 3][...] = v2

    flat = [t for trip in wmv for t in trip]
    out_shape = []
    for w, _, _ in wmv:
        out_shape += [_S(w.shape, _F32)] * 4
    return pl.pallas_call(
        body, name="adamw_small", out_shape=out_shape,
        compiler_params=pltpu.CompilerParams(vmem_limit_bytes=VMEM_LIMIT),
    )(*gathered, *flat)


def _block_diag(w):
    eye = jnp.eye(8, dtype=w.dtype)
    return (w[:, :, None, :] * eye[:, None, :, None]).reshape(D_RG, D_RG)


def _diag_blocks(g):
    return jnp.concatenate([g[64 * h:64 * (h + 1), 64 * h:64 * (h + 1)] for h in range(8)], axis=0)


def _local_step(x, tgt, meta, g_mix, w_in, vec, wr, wi, hb, g_hg, w_out_l, g_ffn, w_gu_l, w_down_l, g_fin):
    seq = x.shape[0]
    n_valid = N_META + seq
    t_pad = -(-n_valid // TM) * TM
    h0 = jnp.concatenate([meta, x, jnp.zeros((t_pad - n_valid, D), _F32)], axis=0)
    tgt_p = jnp.concatenate([jnp.zeros((N_META, D), _F32), tgt, jnp.zeros((t_pad - n_valid, D), _F32)], axis=0)

    p, u, w_out, w_down = _inproj(h0, g_mix, w_in, [w_out_l, w_down_l])
    y, hs, o, sc, w_gu = _mixer_fwd(p, wr, wi, vec, hb, g_hg, [w_gu_l])
    w_out = w_out.reshape(D, D)
    w_down = w_down.reshape(4, FFB, D)
    h1, v = _outproj(h0, y, w_out, g_ffn)
    gu, act, dh2, dh2b, loss, gfin = _ffn_loss(v, h1, w_gu, w_down, g_fin, tgt_p, n_valid)

    dgu, dh1, dh1b, dy, gffn = _ffn_bwd(dh2, dh2b, gu, h1, g_ffn, w_gu, w_down, w_out)
    g_wdown = _wgrad("wgrad_down", act, dh2b, pl.BlockSpec((1, t_pad, FFB), lambda j: (j, 0, 0)),
                     pl.BlockSpec((t_pad, D), lambda j: (0, 0)), 4, (FFB, D))
    g_wgu, r_wdown = _wgrad("wgrad_gate_up", v, dgu, pl.BlockSpec((t_pad, D), lambda j: (0, 0)),
                            pl.BlockSpec((1, t_pad, FFB), lambda j: (j, 0, 0)), N_DEV, (D, FFB),
                            scatter=[g_wdown.reshape(N_DEV, D_FF // N_DEV, D)])
    g_wout = _wgrad("wgrad_out", y, dh1b, pl.BlockSpec((t_pad, D // N_DEV), lambda j: (0, j)),
                    pl.BlockSpec((t_pad, D), lambda j: (0, 0)), N_DEV, (D // N_DEV, D))
    dp, gvec, gw, r_wgu, r_wout = _mixer_bwd(p, hs, o, sc, dy, wr, wi, vec, hb, g_hg, [g_wgu, g_wout])
    pack_c = jnp.concatenate([_diag_blocks(gw[0]), _diag_blocks(gw[1])], axis=1).astype(_BF)
    me = 4 * lax.axis_index("x") + 2 * lax.axis_index("y") + lax.axis_index("c")
    order = (me ^ jnp.array(_SEND_ORDER, jnp.int32)).astype(jnp.int32)
    dh0, r_win, all_b, all_c, all_a = _inproj_bwd_send(dp, w_in, h0, dh1, g_mix, u, order, gffn, gfin, [gvec, pack_c])
    return loss, dh0, (r_win, r_wgu, r_wout, r_wdown), (all_a, all_b, all_c)


def kernel(x, meta_tokens, mix_norm_g, w_in, conv_w, conv_b, w_rgate, b_rgate, w_igate, b_igate, lru_lambda, rg_norm_g, hg_lower_bound, hg_norm_g, w_out, ffn_norm_g, w_gate_up, w_down, final_norm_g, loss_target, m_meta_tokens, m_mix_norm_g, m_w_in, m_conv_w, m_conv_b, m_w_rgate, m_b_rgate, m_w_igate, m_b_igate, m_lru_lambda, m_rg_norm_g, m_hg_lower_bound, m_hg_norm_g, m_w_out, m_ffn_norm_g, m_w_gate_up, m_w_down, m_final_norm_g, v_meta_tokens, v_mix_norm_g, v_w_in, v_conv_w, v_conv_b, v_w_rgate, v_b_rgate, v_w_igate, v_b_igate, v_lru_lambda, v_rg_norm_g, v_hg_lower_bound, v_hg_norm_g, v_w_out, v_ffn_norm_g, v_w_gate_up, v_w_down, v_final_norm_g):
    seq = x.shape[1]
    me = 4 * lax.axis_index("x") + 2 * lax.axis_index("y") + lax.axis_index("c")

    small_l = jnp.concatenate([meta_tokens, jnp.pad(conv_w[0], ((0, 4), (0, 64)))], axis=0)
    w_in_g, small_g, w_gu_l, w_out_l, w_down_l = _allgather_first(
        [w_in[0], small_l], [w_gate_up[0], w_out[0], w_down[0]], [_BF, _F32])
    meta_full = jnp.transpose(small_g[:, :N_META, :], (1, 0, 2)).reshape(N_META, D)
    conv_w_full = jnp.transpose(small_g[:, N_META:N_META + 4, :64], (1, 0, 2)).reshape(4, D_RG)
    vec = jnp.concatenate([conv_b, b_rgate, b_igate, lru_lambda, rg_norm_g, jnp.zeros((3, D_RG), _F32),
                           conv_w_full, jnp.zeros((4, D_RG), _F32)], axis=0)
    wr = _block_diag(w_rgate[0]).astype(_BF)
    wi = _block_diag(w_igate[0]).astype(_BF)

    loss, dh0, (r_win, r_wgu, r_wout, r_wdown), (all_a, all_b, all_c) = _local_step(
        x[0], loss_target[0], meta_full, mix_norm_g, w_in_g, vec, wr, wi, hg_lower_bound, hg_norm_g,
        w_out_l, ffn_norm_g, w_gu_l, w_down_l, final_norm_g.reshape(1, D))
    grad_x = dh0[N_META:N_META + seq][None]

    outs = {}
    outs["w_in"] = _adamw_big("adamw_w_in", r_win, w_in[0], m_w_in[0], v_w_in[0], 256)
    outs["w_gate_up"] = _adamw_big("adamw_w_gate_up", r_wgu, w_gate_up[0], m_w_gate_up[0], v_w_gate_up[0], 256)
    outs["w_out"] = _adamw_big("adamw_w_out", r_wout, w_out[0], m_w_out[0], v_w_out[0], 128)
    outs["w_down"] = _adamw_big("adamw_w_down", r_wdown, w_down[0], m_w_down[0], v_w_down[0], 176)

    meta_part = lax.dynamic_slice_in_dim(all_a[:, R_META:R_META + N_META, :], me * 128, 128, axis=2)
    convw_part = lax.dynamic_slice_in_dim(all_b[:, R_CONVW:R_CONVW + 4, :], me * 64, 64, axis=2)
    gathered = [all_a, all_b, all_c, meta_part, convw_part]
    small_params = [
        ("meta_tokens", (3, 0, N_META, 0, 128), (meta_tokens, m_meta_tokens, v_meta_tokens), (N_META, 128)),
        ("mix_norm_g", (0, R_GMIX, 1, 0, D), (mix_norm_g, m_mix_norm_g, v_mix_norm_g), (1, D)),
        ("conv_w", (4, 0, 4, 0, 64), (conv_w, m_conv_w, v_conv_w), (4, 64)),
        ("conv_b", (1, R_CONVB, 1, 0, D_RG), (conv_b, m_conv_b, v_conv_b), (1, D_RG)),
        ("w_rgate", (2, 0, 512, 0, 64), (w_rgate, m_w_rgate, v_w_rgate), (512, 64)),
        ("b_rgate", (1, R_BR, 1, 0, D_RG), (b_rgate, m_b_rgate, v_b_rgate), (1, D_RG)),
        ("w_igate", (2, 0, 512, 64, 64), (w_igate, m_w_igate, v_w_igate), (512, 64)),
        ("b_igate", (1, R_BI, 1, 0, D_RG), (b_igate, m_b_igate, v_b_igate), (1, D_RG)),
        ("lru_lambda", (1, R_LAM, 1, 0, D_RG), (lru_lambda, m_lru_lambda, v_lru_lambda), (1, D_RG)),
        ("rg_norm_g", (1, R_GRG, 1, 0, D_RG), (rg_norm_g, m_rg_norm_g, v_rg_norm_g), (1, D_RG)),
        ("hg_lower_bound", (1, R_HB0, 2, 0, D_HG), (hg_lower_bound, m_hg_lower_bound, v_hg_lower_bound), (2, D_HG)),
        ("hg_norm_g", (1, R_GHG, 1, 0, HD), (hg_norm_g, m_hg_norm_g, v_hg_norm_g), (1, HD)),
        ("ffn_norm_g", (0, R_GFFN, 1, 0, D), (ffn_norm_g, m_ffn_norm_g, v_ffn_norm_g), (1, D)),
        ("final_norm_g", (0, R_GFIN, 1, 0, D), (final_norm_g, m_final_norm_g, v_final_norm_g), (1, D)),
    ]
    res = _adamw_small(gathered, [s[1] for s in small_params],
                       [tuple(t.reshape(s[3]) for t in s[2]) for s in small_params])
    for i, s in enumerate(small_params):
        outs[s[0]] = [r.reshape(s[2][0].shape) for r in res[4 * i:4 * i + 4]]
    for n, ref in (("w_in", w_in), ("w_gate_up", w_gate_up), ("w_out", w_out), ("w_down", w_down)):
        outs[n] = [r.reshape(ref.shape) for r in outs[n]]

    loss_all = lax.psum(loss[0, 0], ("x", "y", "c"))
    order = ["meta_tokens", "mix_norm_g", "w_in", "conv_w", "conv_b", "w_rgate", "b_rgate", "w_igate", "b_igate",
             "lru_lambda", "rg_norm_g", "hg_lower_bound", "hg_norm_g", "w_out", "ffn_norm_g", "w_gate_up", "w_down",
             "final_norm_g"]
    return (loss_all, grad_x, *[outs[n][0] for n in order], *[outs[n][1] for n in order],
            *[outs[n][2] for n in order], *[outs[n][3] for n in order])
```

```python
import functools

import jax
import jax.numpy as jnp
from jax import lax
from jax.experimental import pallas as pl
from jax.experimental.pallas import tpu as pltpu

_BF = jnp.bfloat16
_F32 = jnp.float32
_S = jax.ShapeDtypeStruct
_MESH = pl.DeviceIdType.MESH

N_DEV = 8
N_META = 16
D = 1024
D_RG = 512
D_HG = 512
HD = 128
NH = D_HG // HD
D_IN = 3072
D_FF = 2816
FFB = D_FF // 4
WIN_B = D_IN // N_DEV
EPS = 1e-6
LRU_C = 8.0
TM = 256
HC = 64
VMEM_LIMIT = 56 * 1024 * 1024

ADAM_LR = 0.001
ADAM_B1 = 0.9
ADAM_B2 = 0.999
ADAM_EPS = 1e-08
ADAM_WD = 0.01
ADAM_STEP = 10

_SEND_ORDER = (6, 4, 2, 7, 5, 3, 1, 0)

R_CONVB, R_BR, R_BI, R_LAM, R_GRG, R_HB0, R_HB1, R_GHG, R_CONVW = 0, 1, 2, 3, 4, 5, 6, 7, 8
R_GMIX, R_GFFN, R_GFIN, R_LOSS, R_META = 0, 1, 2, 3, 8


def _cp(sem=None, **kw):
    return pltpu.CompilerParams(dimension_semantics=sem, vmem_limit_bytes=VMEM_LIMIT, **kw)


def _dot(a, b):
    return jnp.dot(a, b, preferred_element_type=_F32)


def _dot_nt(a, b):
    return lax.dot_general(a, b, (((1,), (1,)), ((), ())), preferred_element_type=_F32)


def _dot_tn(a, b):
    return lax.dot_general(a, b, (((0,), (0,)), ((), ())), preferred_element_type=_F32)


def _sigmoid(x):
    return jax.nn.sigmoid(x)


def _dsilu(x, s):
    return s * (1.0 + x * (1.0 - s))


_GELU_C = 0.7978845608028654


def _gelu_parts(x):
    t = jnp.tanh(_GELU_C * (x + 0.044715 * (x * x * x)))
    g = 0.5 * x * (1.0 + t)
    dg = 0.5 * (1.0 + t) + 0.5 * x * (1.0 - t * t) * (_GELU_C * (1.0 + 3.0 * 0.044715 * (x * x)))
    return g, dg


def _softplus(z):
    e = jnp.exp(-jnp.abs(z))
    w = 1.0 + e
    l1p = jnp.where(w == 1.0, e, jnp.log(w) * e / jnp.where(w == 1.0, 1.0, w - 1.0))
    return jnp.maximum(z, 0.0) + l1p


def _rms_fwd(x):
    r = lax.rsqrt(jnp.mean(x * x, axis=-1, keepdims=True) + EPS)
    return x * r, r


def _rms_bwd(dyg, n, r):
    return r * (dyg - n * jnp.mean(dyg * n, axis=-1, keepdims=True))


def _full(shape):
    nd = len(shape)
    return pl.BlockSpec(shape, lambda i: (0,) * nd)


def _const(shape):
    nd = len(shape)
    return pl.BlockSpec(shape, lambda i: (0,) * nd, pipeline_mode=pl.Buffered(1))


def _carry_gather(gather, i, nt):
    @pl.when(i == 0)
    def _():
        gather.start()

    def tail():
        for j in range(3):
            @pl.when(i == max(nt - 3 + j, 0))
            def _(j=j):
                gather.forward(j)

        @pl.when(i == nt - 1)
        def _():
            gather.finish()

    return tail


def _inproj(h0, g_mix, w_in, shards):
    t_pad = h0.shape[0]
    nt = t_pad // TM
    nsh = len(shards)

    def body(h_ref, g_ref, w_ref, *rest):
        p_ref, u_ref = rest[nsh:nsh + 2]
        tail = _carry_gather(_Gather(rest[:nsh], rest[nsh + 2:2 * nsh + 2], rest[2 * nsh + 2:]), pl.program_id(0), nt)
        n, _ = _rms_fwd(h_ref[...])
        u = (n * g_ref[...]).astype(_BF)
        u_ref[...] = u
        for j in range(N_DEV):
            p_ref[:, WIN_B * j:WIN_B * (j + 1)] = _dot(u, w_ref[j])
        tail()

    hbm = pl.BlockSpec(memory_space=pl.ANY)
    return pl.pallas_call(
        body, name="inproj", grid=(nt,),
        in_specs=[pl.BlockSpec((TM, D), lambda i: (i, 0)), _full((1, D)), _const((N_DEV, D, WIN_B))] + [hbm] * nsh,
        out_specs=[pl.BlockSpec((TM, D_IN), lambda i: (i, 0)), pl.BlockSpec((TM, D), lambda i: (i, 0))] + [hbm] * nsh,
        out_shape=[_S((t_pad, D_IN), _F32), _S((t_pad, D), _BF)] + [_S((N_DEV,) + s.shape, s.dtype) for s in shards],
        scratch_shapes=_sem_shapes(nsh),
        compiler_params=_cp(("arbitrary",)),
    )(h0, g_mix, w_in, *shards)


def _rg_gates(xc, wr_ref, wi_ref, vec_ref):
    xcb = xc.astype(_BF)
    r = _sigmoid(_dot(xcb, wr_ref[...]) + vec_ref[R_BR:R_BR + 1, :])
    ig = _sigmoid(_dot(xcb, wi_ref[...]) + vec_ref[R_BI:R_BI + 1, :])
    nsp8 = -LRU_C * _softplus(-vec_ref[R_LAM:R_LAM + 1, :])
    la = nsp8 * r
    a = jnp.exp(la)
    th = jnp.tanh(la)
    s = jnp.sqrt(-2.0 * th / (1.0 - th))
    return r, ig, a, s, nsp8


def _conv(xbuf, vec_ref):
    acc = vec_ref[R_CONVW:R_CONVW + 1, :] * xbuf[pl.ds(5, TM), :]
    for j in range(1, 4):
        acc = acc + vec_ref[R_CONVW + j:R_CONVW + j + 1, :] * xbuf[pl.ds(5 + j, TM), :]
    return vec_ref[R_CONVB:R_CONVB + 1, :] + acc


def _dot3(m01, x):
    hi = x.astype(_BF)
    r1 = x - hi.astype(_F32)
    mid = r1.astype(_BF)
    lo = (r1 - mid.astype(_F32)).astype(_BF)
    return (_dot(m01, lo) + _dot(m01, mid)) + _dot(m01, hi)


def _chunk_masks():
    row = lax.broadcasted_iota(jnp.int32, (TM, TM), 0)
    col = lax.broadcasted_iota(jnp.int32, (TM, TM), 1)
    shift = HC.bit_length() - 1
    same = lax.shift_right_logical(row, shift) == lax.shift_right_logical(col, shift)
    return same, same & (row >= col), same & (col >= row)


def _per_chunk_rows(x, r):
    return jnp.concatenate([jnp.broadcast_to(x[HC * c + r:HC * c + r + 1, :], (HC, x.shape[1]))
                            for c in range(TM // HC)], axis=0)


def _hg_prep(p_ref, lb, tri_blk):
    hq = p_ref[:, pl.ds(2 * D_RG, D_HG)]
    hf = p_ref[:, pl.ds(2 * D_RG + D_HG, D_HG)]
    sq = _sigmoid(hq)
    q = hq * sq
    sg = _sigmoid(hf)
    f = lb + (1.0 - lb) * sg
    k = 1.0 - f
    b = _dot3(tri_blk, jnp.log(f))
    bm = _per_chunk_rows(b, HC // 2 - 1)
    bl = _per_chunk_rows(b, HC - 1)
    e_q = jnp.exp(b - bm)
    e_k = jnp.exp(bm - b)
    e_b = jnp.exp(b)
    e_l = jnp.exp(bl - b)
    return dict(hq=hq, sq=sq, q=q, sg=sg, f=f, k=k, e_q=e_q, e_k=e_k, e_b=e_b, e_l=e_l,
                qd=q * e_q, kd=k * e_k, qe=q * e_b, ke=k * e_l, e_end=jnp.exp(bl))


def _mixer_fwd(p, wr, wi, vec, hb, g_hg, shards):
    t_pad = p.shape[0]
    nt = t_pad // TM
    nc_t = TM // HC
    nsh = len(shards)

    def body(p_ref, wr_ref, wi_ref, vec_ref, hb_ref, ghg_ref, *rest):
        sh_refs, rest = rest[:nsh], rest[nsh:]
        y_ref, hs_ref, o_ref, sc_ref = rest[:4]
        gath_refs, rest = rest[4:4 + nsh], rest[4 + nsh:]
        xbuf, a_s, b_s, hcar, st, qd_s, kd_s, qe_s, ke_s, v_s, u_s = rest[:11]
        i = pl.program_id(0)
        tail = _carry_gather(_Gather(sh_refs, gath_refs, rest[11:]), i, nt)

        @pl.when(i == 0)
        def _():
            xbuf[pl.ds(0, 8), :] = jnp.zeros((8, D_RG), _F32)
            hcar[...] = jnp.zeros_like(hcar)
            st[...] = jnp.zeros_like(st)

        x = p_ref[:, pl.ds(0, D_RG)]
        xbuf[pl.ds(8, TM), :] = x
        xc = _conv(xbuf, vec_ref)
        xbuf[pl.ds(0, 8), :] = x[TM - 8:, :]
        r, ig, a, s, _ = _rg_gates(xc, wr_ref, wi_ref, vec_ref)
        a_s[...] = a
        b_s[...] = s * (ig * xc)

        def step(t, h):
            h = a_s[pl.ds(t, 1), :] * h + b_s[pl.ds(t, 1), :]
            hs_ref[pl.ds(t, 1), :] = h
            return h

        hcar[pl.ds(0, 1), :] = lax.fori_loop(0, TM, step, hcar[pl.ds(0, 1), :], unroll=8)
        gel, _ = _gelu_parts(p_ref[:, pl.ds(D_RG, D_RG)])
        n, _ = _rms_fwd(gel * hs_ref[...])
        y_ref[:, pl.ds(0, D_RG)] = (n * vec_ref[R_GRG:R_GRG + 1, :]).astype(_BF)

        lb = _sigmoid(hb_ref[0:1, :] - hb_ref[1:2, :])
        _, tri_blk, _ = _chunk_masks()
        q = _hg_prep(p_ref, lb, tri_blk.astype(_BF))
        for name, ref in (("qd", qd_s), ("kd", kd_s), ("qe", qe_s), ("ke", ke_s)):
            ref[...] = q[name].astype(_BF)
        v_s[...] = p_ref[:, pl.ds(2 * D_RG + 2 * D_HG, D_HG)].astype(_BF)
        e_end = q["e_end"]
        causal = (lax.broadcasted_iota(jnp.int32, (HC, HC), 0) >= lax.broadcasted_iota(jnp.int32, (HC, HC), 1))
        for c in range(nc_t):
            for h in range(NH):
                rs, cs = pl.ds(HC * c, HC), pl.ds(HD * h, HD)
                amat = jnp.where(causal, _dot_nt(qd_s[rs, cs], kd_s[rs, cs]), 0.0)
                o_ref[rs, cs] = _dot(amat.astype(_BF), v_s[rs, cs])
                u_s[NH * c + h] = _dot_tn(v_s[rs, cs], ke_s[rs, cs])
        for h in range(NH):
            cs = pl.ds(HD * h, HD)
            s_run = st[h]
            for c in range(nc_t):
                rs = pl.ds(HC * c, HC)
                sc_ref[c, h] = s_run
                o_ref[rs, cs] += _dot_nt(qe_s[rs, cs], s_run.astype(_BF))
                s_run = e_end[HC * c:HC * c + 1, HD * h:HD * (h + 1)] * s_run + u_s[NH * c + h]
            st[h] = s_run
        for h in range(NH):
            cs = pl.ds(HD * h, HD)
            n_o, _ = _rms_fwd(o_ref[:, cs])
            hg = p_ref[:, pl.ds(2 * D_RG + 3 * D_HG + HD * h, HD)]
            y_ref[:, pl.ds(D_RG + HD * h, HD)] = ((n_o * ghg_ref[...]) * (hg * _sigmoid(hg))).astype(_BF)

        tail()

    hbm = pl.BlockSpec(memory_space=pl.ANY)
    return pl.pallas_call(
        body, name="mixer_fwd", grid=(nt,),
        in_specs=[pl.BlockSpec((TM, D_IN), lambda i: (i, 0)), _full((D_RG, D_RG)), _full((D_RG, D_RG)),
                  _full((16, D_RG)), _full((2, D_HG)), _full((1, HD))] + [hbm] * nsh,
        out_specs=[pl.BlockSpec((TM, D), lambda i: (i, 0)), pl.BlockSpec((TM, D_RG), lambda i: (i, 0)),
                   pl.BlockSpec((TM, D_HG), lambda i: (i, 0)),
                   pl.BlockSpec((nc_t, NH, HD, HD), lambda i: (i, 0, 0, 0))] + [hbm] * nsh,
        out_shape=[_S((t_pad, D), _BF), _S((t_pad, D_RG), _F32), _S((t_pad, D_HG), _F32),
                   _S((t_pad // HC, NH, HD, HD), _F32)] + [_S((N_DEV,) + s.shape, s.dtype) for s in shards],
        scratch_shapes=[pltpu.VMEM((TM + 8, D_RG), _F32), pltpu.VMEM((TM, D_RG), _F32),
                        pltpu.VMEM((TM, D_RG), _F32), pltpu.VMEM((8, D_RG), _F32),
                        pltpu.VMEM((NH, HD, HD), _F32)] + [pltpu.VMEM((TM, D_HG), _BF) for _ in range(5)]
        + [pltpu.VMEM((nc_t * NH, HD, HD), _F32)] + _sem_shapes(nsh),
        compiler_params=_cp(("arbitrary",)),
    )(p, wr, wi, vec, hb, g_hg, *shards)


def _outproj(h0, y, w_out, g_ffn):
    t_pad = h0.shape[0]

    def body(h_ref, y_ref, w_ref, g_ref, h1_ref, v_ref):
        h1 = h_ref[...] + _dot(y_ref[...], w_ref[...])
        h1_ref[...] = h1
        n, _ = _rms_fwd(h1)
        v_ref[...] = (n * g_ref[...]).astype(_BF)

    return pl.pallas_call(
        body, name="outproj", grid=(t_pad // TM,),
        in_specs=[pl.BlockSpec((TM, D), lambda i: (i, 0)), pl.BlockSpec((TM, D), lambda i: (i, 0)),
                  _full((D, D)), _full((1, D))],
        out_specs=[pl.BlockSpec((TM, D), lambda i: (i, 0)), pl.BlockSpec((TM, D), lambda i: (i, 0))],
        out_shape=[_S((t_pad, D), _F32), _S((t_pad, D), _BF)],
        compiler_params=_cp(("arbitrary",)),
    )(h0, y, w_out, g_ffn)


def _ffn_loss(v, h1, w_gu, w_down, g_fin, tgt, n_valid):
    t_pad = v.shape[0]

    def body(v_ref, h1_ref, wgu_ref, wd_ref, g_ref, t_ref, gu_ref, act_ref, dh2_ref, dh2b_ref, loss_ref, gfin_ref):
        i = pl.program_id(0)

        @pl.when(i == 0)
        def _():
            loss_ref[...] = jnp.zeros_like(loss_ref)
            gfin_ref[...] = jnp.zeros_like(gfin_ref)

        vb = v_ref[...]
        h2 = h1_ref[...]
        for b in range(4):
            gate = _dot_nt(vb, wgu_ref[b])
            up = _dot_nt(vb, wgu_ref[4 + b])
            gu_ref[b] = gate
            gu_ref[4 + b] = up
            act = ((gate * _sigmoid(gate)) * up).astype(_BF)
            act_ref[b] = act
            h2 = h2 + _dot(act, wd_ref[b])
        n, r = _rms_fwd(h2)
        out = n * g_ref[...]
        row = i * TM + lax.broadcasted_iota(jnp.int32, (TM, 1), 0)
        valid = (row >= N_META) & (row < n_valid)
        err = jnp.where(valid, out - t_ref[...], 0.0)
        loss_ref[...] += (0.5 / D) * jnp.sum(err * err)
        dout = err * (1.0 / D)
        gfin_ref[...] += jnp.sum(dout * n, axis=0, keepdims=True)
        dh2 = _rms_bwd(dout * g_ref[...], n, r)
        dh2_ref[...] = dh2
        dh2b_ref[...] = dh2.astype(_BF)

    return pl.pallas_call(
        body, name="ffn_loss", grid=(t_pad // TM,),
        in_specs=[pl.BlockSpec((TM, D), lambda i: (i, 0)), pl.BlockSpec((TM, D), lambda i: (i, 0)),
                  _const((N_DEV, FFB, D)), _const((4, FFB, D)), _full((1, D)),
                  pl.BlockSpec((TM, D), lambda i: (i, 0))],
        out_specs=[pl.BlockSpec((N_DEV, TM, FFB), lambda i: (0, i, 0)), pl.BlockSpec((4, TM, FFB), lambda i: (0, i, 0)),
                   pl.BlockSpec((TM, D), lambda i: (i, 0)), pl.BlockSpec((TM, D), lambda i: (i, 0)),
                   _full((8, 128)), _full((1, D))],
        out_shape=[_S((N_DEV, t_pad, FFB), _F32), _S((4, t_pad, FFB), _BF), _S((t_pad, D), _F32),
                   _S((t_pad, D), _BF), _S((8, 128), _F32), _S((1, D), _F32)],
        compiler_params=_cp(("arbitrary",)),
    )(v, h1, w_gu, w_down, g_fin, tgt)


def _ffn_bwd(dh2, dh2b, gu, h1, g_ffn, w_gu, w_down, w_out):
    t_pad = dh2.shape[0]

    def body(dh2_ref, dh2b_ref, gu_ref, h1_ref, g_ref, wgu_ref, wd_ref, wo_ref,
             dgu_ref, dh1_ref, dh1b_ref, dy_ref, gffn_ref):
        i = pl.program_id(0)

        @pl.when(i == 0)
        def _():
            gffn_ref[...] = jnp.zeros_like(gffn_ref)

        db = dh2b_ref[...]
        dv = jnp.zeros((TM, D), _F32)
        for b in range(4):
            dact = _dot_nt(db, wd_ref[b])
            gate = gu_ref[b]
            up = gu_ref[4 + b]
            sg = _sigmoid(gate)
            dgate = ((dact * up) * _dsilu(gate, sg)).astype(_BF)
            dup = (dact * (gate * sg)).astype(_BF)
            dgu_ref[b] = dgate
            dgu_ref[4 + b] = dup
            dv = dv + _dot(dgate, wgu_ref[b]) + _dot(dup, wgu_ref[4 + b])
        n, r = _rms_fwd(h1_ref[...])
        gffn_ref[...] += jnp.sum(dv * n, axis=0, keepdims=True)
        dh1 = dh2_ref[...] + _rms_bwd(dv * g_ref[...], n, r)
        dh1_ref[...] = dh1
        dh1b = dh1.astype(_BF)
        dh1b_ref[...] = dh1b
        dy_ref[...] = _dot_nt(dh1b, wo_ref[...])

    tile = pl.BlockSpec((TM, D), lambda i: (i, 0))
    return pl.pallas_call(
        body, name="ffn_bwd", grid=(t_pad // TM,),
        in_specs=[tile, tile, pl.BlockSpec((N_DEV, TM, FFB), lambda i: (0, i, 0)), tile, _full((1, D)),
                  _const((N_DEV, FFB, D)), _const((4, FFB, D)), _const((D, D))],
        out_specs=[pl.BlockSpec((N_DEV, TM, FFB), lambda i: (0, i, 0)), tile, tile, tile, _full((1, D))],
        out_shape=[_S((N_DEV, t_pad, FFB), _BF), _S((t_pad, D), _F32), _S((t_pad, D), _BF),
                   _S((t_pad, D), _F32), _S((1, D), _F32)],
        compiler_params=_cp(("arbitrary",)),
    )(dh2, dh2b, gu, h1, g_ffn, w_gu, w_down, w_out)


def _mixer_bwd(p, hs, o, sc, dy, wr, wi, vec, hb, g_hg, scatter):
    t_pad = p.shape[0]
    nt = t_pad // TM
    nc_t = TM // HC
    nsc = len(scatter)

    def rev(i):
        return nt - 1 - i

    def body(p_ref, pprev_ref, hs_ref, hprev_ref, o_ref, sc_ref, dy_ref, wr_ref, wi_ref, vec_ref, hb_ref, ghg_ref,
             *rest):
        send_refs, rest = rest[:nsc], rest[nsc:]
        dp_ref, gvec_ref, gw_ref = rest[:3]
        recv_refs, rest = rest[3:3 + nsc], rest[3 + nsc:]
        xbuf, hbuf, dbuf, a_s, g_s, ccar, dst = rest[:7]
        qd_s, kd_s, qe_s, ke_s, v_s, do_s, dqd_s, dkd_s, dqe_s, dke_s, dv_s, w_s, dend_s = rest[7:20]
        exchange = _Exchange(send_refs, [], recv_refs, rest[20:])
        i = pl.program_id(0)
        first_tile = i == nt - 1

        @pl.when(i == 0)
        def _():
            exchange.start()
            gvec_ref[...] = jnp.zeros_like(gvec_ref)
            gw_ref[...] = jnp.zeros_like(gw_ref)
            dbuf[pl.ds(TM, 8), :] = jnp.zeros((8, D_RG), _F32)
            ccar[...] = jnp.zeros_like(ccar)
            dst[...] = jnp.zeros_like(dst)

        def acc(row, val):
            gvec_ref[row:row + 1, :] += jnp.sum(val, axis=0, keepdims=True)

        keep = jnp.where(first_tile, 0.0, 1.0)
        x = p_ref[:, pl.ds(0, D_RG)]
        xbuf[pl.ds(0, 8), :] = pprev_ref[...] * keep
        xbuf[pl.ds(8, TM), :] = x
        xc = _conv(xbuf, vec_ref)
        r, ig, a, s, nsp8 = _rg_gates(xc, wr_ref, wi_ref, vec_ref)
        h = hs_ref[...]
        hbuf[pl.ds(0, 8), :] = hprev_ref[...] * keep
        hbuf[pl.ds(8, TM), :] = h
        hm1 = hbuf[pl.ds(7, TM), :]
        gr = p_ref[:, pl.ds(D_RG, D_RG)]
        gel, dgel = _gelu_parts(gr)
        n, rr = _rms_fwd(gel * h)
        dyn = dy_ref[:, pl.ds(0, D_RG)]
        acc(R_GRG, dyn * n)
        dpre = _rms_bwd(dyn * vec_ref[R_GRG:R_GRG + 1, :], n, rr)
        dp_ref[:, pl.ds(D_RG, D_RG)] = ((dpre * h) * dgel).astype(_BF)
        a_s[...] = a
        g_s[...] = dpre * gel

        def step(k, c):
            t = TM - 1 - k
            g = g_s[pl.ds(t, 1), :] + c
            g_s[pl.ds(t, 1), :] = g
            return a_s[pl.ds(t, 1), :] * g

        ccar[pl.ds(0, 1), :] = lax.fori_loop(0, TM, step, ccar[pl.ds(0, 1), :], unroll=8)
        gt = g_s[...]
        da = gt * hm1
        ixc = ig * xc
        ds = gt * ixc
        dig = (gt * s) * xc
        dxc = (gt * s) * ig
        dla = da * a - ds * ((a * a) / s)
        lam = vec_ref[R_LAM:R_LAM + 1, :]
        gvec_ref[R_LAM:R_LAM + 1, :] += jnp.sum(dla * r, axis=0, keepdims=True) * (LRU_C * _sigmoid(-lam))
        dzr = (dla * nsp8) * (r * (1.0 - r))
        dzi = dig * (ig * (1.0 - ig))
        acc(R_BR, dzr)
        acc(R_BI, dzi)
        xcb = xc.astype(_BF)
        dzrb = dzr.astype(_BF)
        dzib = dzi.astype(_BF)
        gw_ref[0] += _dot_tn(xcb, dzrb)
        gw_ref[1] += _dot_tn(xcb, dzib)
        dxc = dxc + _dot_nt(dzrb, wr_ref[...]) + _dot_nt(dzib, wi_ref[...])
        acc(R_CONVB, dxc)
        for j in range(4):
            acc(R_CONVW + j, dxc * xbuf[pl.ds(5 + j, TM), :])
        dbuf[pl.ds(0, TM), :] = dxc
        dx = vec_ref[R_CONVW + 3:R_CONVW + 4, :] * dxc
        for j in range(3):
            dx = dx + vec_ref[R_CONVW + j:R_CONVW + j + 1, :] * dbuf[pl.ds(3 - j, TM), :]
        dbuf[pl.ds(TM, 8), :] = dxc[0:8, :]
        dp_ref[:, pl.ds(0, D_RG)] = dx.astype(_BF)

        lb = _sigmoid(hb_ref[0:1, :] - hb_ref[1:2, :])
        same, tri_blk, triu_blk = _chunk_masks()
        q = _hg_prep(p_ref, lb, tri_blk.astype(_BF))
        qdb, kdb = q["qd"].astype(_BF), q["kd"].astype(_BF)
        qd_s[...] = qdb
        kd_s[...] = kdb
        qe_s[...] = q["qe"].astype(_BF)
        ke_s[...] = q["ke"].astype(_BF)
        v_s[...] = p_ref[:, pl.ds(2 * D_RG + 2 * D_HG, D_HG)].astype(_BF)
        e_end = q["e_end"]
        ghg = ghg_ref[...]
        for h in range(NH):
            cs = pl.ds(HD * h, HD)
            hg = p_ref[:, pl.ds(2 * D_RG + 3 * D_HG + HD * h, HD)]
            sh = _sigmoid(hg)
            n_o, r_o = _rms_fwd(o_ref[:, cs])
            dyh = dy_ref[:, pl.ds(D_RG + HD * h, HD)]
            dp_ref[:, pl.ds(2 * D_RG + 3 * D_HG + HD * h, HD)] = ((dyh * (n_o * ghg)) * _dsilu(hg, sh)).astype(_BF)
            dn = dyh * (hg * sh)
            gvec_ref[R_GHG:R_GHG + 1, pl.ds(0, HD)] += jnp.sum(dn * n_o, axis=0, keepdims=True)
            do_s[:, cs] = _rms_bwd(dn * ghg, n_o, r_o).astype(_BF)
        causal = (lax.broadcasted_iota(jnp.int32, (HC, HC), 0) >= lax.broadcasted_iota(jnp.int32, (HC, HC), 1))
        for c in range(nc_t):
            for h in range(NH):
                rs, cs = pl.ds(HC * c, HC), pl.ds(HD * h, HD)
                qd_c, kd_c, do_c = qd_s[rs, cs], kd_s[rs, cs], do_s[rs, cs]
                amat = jnp.where(causal, _dot_nt(qd_c, kd_c), 0.0).astype(_BF)
                da_m = jnp.where(causal, _dot_nt(do_c, v_s[rs, cs]), 0.0).astype(_BF)
                dqd_s[rs, cs] = _dot(da_m, kd_c)
                dkd_s[rs, cs] = _dot_tn(da_m, qd_c)
                dqe_s[rs, cs] = _dot(do_c, sc_ref[c, h].astype(_BF))
                dv_s[rs, cs] = _dot_tn(amat, do_c)
                w_s[NH * c + h] = _dot_tn(do_c, qe_s[rs, cs])
        for h in range(NH):
            cs = pl.ds(HD * h, HD)
            d_run = dst[h]
            for c in reversed(range(nc_t)):
                rs = pl.ds(HC * c, HC)
                d_b = d_run.astype(_BF)
                dke_s[rs, cs] = _dot(v_s[rs, cs], d_b)
                dp_ref[rs, pl.ds(2 * D_RG + 2 * D_HG + HD * h, HD)] = (
                    dv_s[rs, cs] + _dot_nt(ke_s[rs, cs], d_b)).astype(_BF)
                dend_s[pl.ds(c, 1), cs] = jnp.sum(sc_ref[c, h] * d_run, axis=0, keepdims=True)
                d_run = w_s[NH * c + h] + e_end[HC * c:HC * c + 1, HD * h:HD * (h + 1)] * d_run
            dst[h] = d_run
        dqd, dkd, dqe, dke = dqd_s[...], dkd_s[...], dqe_s[...], dke_s[...]
        dq = dqd * q["e_q"] + dqe * q["e_b"]
        dk = dkd * q["e_k"] + dke * q["e_l"]
        dkeke = dke * q["ke"]
        db = dqd * qdb.astype(_F32) - dkd * kdb.astype(_F32) + dqe * q["qe"] - dkeke
        d_end = jnp.concatenate([jnp.broadcast_to(dend_s[pl.ds(c, 1), :], (HC, D_HG)) for c in range(nc_t)], axis=0)
        dlf = _dot3(triu_blk.astype(_BF), db) + _dot3(same.astype(_BF), dkeke) + d_end * e_end
        df = dlf / q["f"] - dk
        sg = q["sg"]
        gvec_ref[R_HB0:R_HB0 + 1, :] += jnp.sum(df * (1.0 - sg), axis=0, keepdims=True)
        dp_ref[:, pl.ds(2 * D_RG, D_HG)] = (dq * _dsilu(q["hq"], q["sq"])).astype(_BF)
        dp_ref[:, pl.ds(2 * D_RG + D_HG, D_HG)] = ((df * (1.0 - lb)) * (sg * (1.0 - sg))).astype(_BF)

        @pl.when(i == nt - 1)
        def _():
            glb = gvec_ref[R_HB0:R_HB0 + 1, :] * (lb * (1.0 - lb))
            gvec_ref[R_HB0:R_HB0 + 1, :] = glb
            gvec_ref[R_HB1:R_HB1 + 1, :] = -glb
            exchange.finish()

    hbm = pl.BlockSpec(memory_space=pl.ANY)
    return pl.pallas_call(
        body, name="mixer_bwd", grid=(nt,),
        in_specs=[pl.BlockSpec((TM, D_IN), lambda i: (rev(i), 0)),
                  pl.BlockSpec((8, D_RG), lambda i: (jnp.maximum(rev(i) * (TM // 8) - 1, 0), 0)),
                  pl.BlockSpec((TM, D_RG), lambda i: (rev(i), 0)),
                  pl.BlockSpec((8, D_RG), lambda i: (jnp.maximum(rev(i) * (TM // 8) - 1, 0), 0)),
                  pl.BlockSpec((TM, D_HG), lambda i: (rev(i), 0)),
                  pl.BlockSpec((nc_t, NH, HD, HD), lambda i: (rev(i), 0, 0, 0)),
                  pl.BlockSpec((TM, D), lambda i: (rev(i), 0)),
                  _full((D_RG, D_RG)), _full((D_RG, D_RG)), _full((16, D_RG)), _full((2, D_HG)), _full((1, HD))]
        + [hbm] * nsc,
        out_specs=[pl.BlockSpec((TM, D_IN), lambda i: (rev(i), 0)), _full((16, D_RG)), _full((2, D_RG, D_RG))]
        + [hbm] * nsc,
        out_shape=[_S((t_pad, D_IN), _BF), _S((16, D_RG), _F32), _S((2, D_RG, D_RG), _F32)]
        + [_S(s.shape, s.dtype) for s in scatter],
        scratch_shapes=[pltpu.VMEM((TM + 8, D_RG), _F32), pltpu.VMEM((TM + 8, D_RG), _F32),
                        pltpu.VMEM((TM + 8, D_RG), _F32), pltpu.VMEM((TM, D_RG), _F32),
                        pltpu.VMEM((TM, D_RG), _F32), pltpu.VMEM((8, D_RG), _F32),
                        pltpu.VMEM((NH, HD, HD), _F32)]
        + [pltpu.VMEM((TM, D_HG), _BF) for _ in range(6)] + [pltpu.VMEM((TM, D_HG), _F32) for _ in range(5)]
        + [pltpu.VMEM((nc_t * NH, HD, HD), _F32), pltpu.VMEM((8, D_HG), _F32)] + _sem_shapes(nsc),
        compiler_params=_cp(("arbitrary",)),
    )(p, p, hs, hs, o, sc, dy, wr, wi, vec, hb, g_hg, *scatter)


def _inproj_bwd_send(dp, w_in, h0, dh1, g_mix, u, order, gffn, gfin, loss, to_all):
    t_pad = dp.shape[0]
    rb = t_pad // (2 * N_DEV)
    n_steps = N_DEV + 2 * N_DEV
    na = len(to_all)

    def body(order_ref, dpc_ref, dpr_ref, u_ref, w_ref, h_ref, dh1_ref, g_ref, gffn_ref, gfin_ref, loss_ref, *rest):
        all_in = rest[:na]
        dh0_ref, recv_ref = rest[na:na + 2]
        all_out = rest[na + 2:2 * na + 2]
        alla_ref = rest[2 * na + 2]
        buf, pack, blk_send, blk_recv, blk_local = rest[2 * na + 3:2 * na + 8]
        exchange = _Exchange([], all_in, all_out, rest[2 * na + 8:2 * na + 11])
        last = _Exchange([], [pack], [alla_ref], rest[2 * na + 11:])
        s = pl.program_id(0)
        x, y, c = _coords()
        me = 4 * x + 2 * y + c

        def send(step):
            r = _SEND_ORDER[step]
            return pltpu.make_async_remote_copy(
                src_ref=buf.at[step], dst_ref=recv_ref.at[me], send_sem=blk_send.at[step], recv_sem=blk_recv.at[r - 1],
                device_id=(x ^ (r >> 2), y ^ ((r >> 1) & 1), c ^ (r & 1)), device_id_type=_MESH)

        @pl.when(s == 0)
        def _():
            exchange.start()
            pack[...] = jnp.zeros_like(pack)

        @pl.when(s < N_DEV)
        def _():
            buf[s] = _dot_tn(u_ref[...], dpc_ref[...]).astype(_BF)

            for step in range(N_DEV - 1):
                @pl.when(s == step)
                def _(step=step):
                    send(step).start()

        @pl.when(s >= N_DEV)
        def _():
            du = jnp.zeros((rb, D), _F32)
            for j in range(N_DEV):
                du = du + _dot_nt(dpr_ref[:, WIN_B * j:WIN_B * (j + 1)], w_ref[j])
            n, r = _rms_fwd(h_ref[...])
            pack[R_GMIX:R_GMIX + 1, :] += jnp.sum(du * n, axis=0, keepdims=True)
            dh0 = dh1_ref[...] + _rms_bwd(du * g_ref[...], n, r)
            dh0_ref[...] = dh0

            @pl.when(s == N_DEV)
            def _():
                pack[R_META:R_META + N_META, :] = dh0[0:N_META, :]

        @pl.when(s == n_steps - 1)
        def _():
            pack[R_GFFN:R_GFFN + 1, :] = gffn_ref[...]
            pack[R_GFIN:R_GFIN + 1, :] = gfin_ref[...]
            pack[R_LOSS:R_LOSS + 1, pl.ds(0, 128)] = loss_ref[0:1, :]
            last.start()
            mine = pltpu.make_async_copy(buf.at[N_DEV - 1], recv_ref.at[me], blk_local.at[0])
            mine.start()
            for step in range(N_DEV - 1):
                send(step).wait_send()
            for r in range(1, N_DEV):
                px, py, pc = x ^ (r >> 2), y ^ ((r >> 1) & 1), c ^ (r & 1)
                pltpu.make_async_remote_copy(
                    src_ref=buf.at[0], dst_ref=recv_ref.at[4 * px + 2 * py + pc], send_sem=blk_send.at[0],
                    recv_sem=blk_recv.at[r - 1], device_id=(px, py, pc), device_id_type=_MESH).wait_recv()
            mine.wait()
            exchange.finish()
            last.finish()

    hbm = pl.BlockSpec(memory_space=pl.ANY)
    rows = pl.BlockSpec((rb, D), lambda s, order: (jnp.maximum(s - N_DEV, 0), 0))
    one = pl.BlockSpec((1, D), lambda s, order: (0, 0))
    res = pl.pallas_call(
        body, name="inproj_bwd_send",
        grid_spec=pltpu.PrefetchScalarGridSpec(
            num_scalar_prefetch=1, grid=(n_steps,),
            in_specs=[pl.BlockSpec((t_pad, WIN_B), lambda s, order: (0, order[jnp.minimum(s, N_DEV - 1)])),
                      pl.BlockSpec((rb, D_IN), lambda s, order: (jnp.maximum(s - N_DEV, 0), 0)),
                      pl.BlockSpec((t_pad, D), lambda s, order: (0, 0), pipeline_mode=pl.Buffered(1)),
                      pl.BlockSpec((N_DEV, D, WIN_B), lambda s, order: (0, 0, 0), pipeline_mode=pl.Buffered(1)),
                      rows, rows, one, one, one, pl.BlockSpec((8, 128), lambda s, order: (0, 0))] + [hbm] * na,
            out_specs=[rows] + [hbm] * (na + 2),
            scratch_shapes=[pltpu.VMEM((N_DEV, D, WIN_B), _BF), pltpu.VMEM((24, D), _F32),
                            pltpu.SemaphoreType.DMA((N_DEV - 1,)), pltpu.SemaphoreType.DMA((N_DEV - 1,)),
                            pltpu.SemaphoreType.DMA((1,))] + _sem_shapes(na) + _sem_shapes(1)),
        out_shape=[_S((t_pad, D), _F32), _S((N_DEV, D, WIN_B), _BF)]
        + [_S((N_DEV,) + g.shape, g.dtype) for g in to_all] + [_S((N_DEV, 24, D), _F32)],
        compiler_params=_cp(("arbitrary",)),
    )(order, dp, dp, u, w_in, h0, dh1, g_mix, gffn, gfin, loss, *to_all)
    return res


def _wgrad(name, a, b, a_spec, b_spec, n_blocks, out_block, scatter=()):
    nsc = len(scatter)

    def body(a_ref, b_ref, *rest):
        o_ref = rest[nsc]
        j = pl.program_id(0)
        if nsc:
            exchange = _Exchange(rest[:nsc], [], rest[nsc + 1:2 * nsc + 1], rest[2 * nsc + 1:])

            @pl.when(j == 0)
            def _():
                exchange.start()

        av = a_ref[0] if len(a_ref.shape) == 3 else a_ref[...]
        bv = b_ref[0] if len(b_ref.shape) == 3 else b_ref[...]
        o_ref[0] = _dot_tn(av, bv).astype(_BF)

        if nsc:
            @pl.when(j == n_blocks - 1)
            def _():
                exchange.finish()

    hbm = pl.BlockSpec(memory_space=pl.ANY)
    res = pl.pallas_call(
        body, name=name, grid=(n_blocks,),
        in_specs=[a_spec, b_spec] + [hbm] * nsc,
        out_specs=[pl.BlockSpec((1,) + out_block, lambda j: (j, 0, 0))] + [hbm] * nsc,
        out_shape=[_S((n_blocks,) + out_block, _BF)] + [_S(s.shape, s.dtype) for s in scatter],
        scratch_shapes=_sem_shapes(nsc) if nsc else [],
        compiler_params=_cp(("arbitrary",)),
    )(a, b, *scatter)
    return res if nsc else res[0]


def _coords():
    return lax.axis_index("x"), lax.axis_index("y"), lax.axis_index("c")


def _sem_shapes(na):
    return [pltpu.SemaphoreType.DMA((7 * na,)), pltpu.SemaphoreType.DMA((7 * na,)), pltpu.SemaphoreType.DMA((na,))]


class _Gather:
    def __init__(self, srcs, outs, sems):
        self.srcs, self.outs = srcs, outs
        self.send_sems, self.recv_sems, self.local_sems = sems
        self.na = len(srcs)
        x, y, c = _coords()
        self.pos = (x, y, c)
        self.me = 4 * x + 2 * y + c
        self.sibling = (x, y, 1 - c)
        self.chips = [(1 - x, y), (x, 1 - y), (1 - x, 1 - y)]

    @staticmethod
    def _slot(px, py, pc):
        return 4 * px + 2 * py + pc

    def _copy(self, a, k, block, to, own=False):
        return pltpu.make_async_remote_copy(
            src_ref=self.srcs[a] if own else self.outs[a].at[block], dst_ref=self.outs[a].at[block],
            send_sem=self.send_sems.at[7 * a + k], recv_sem=self.recv_sems.at[7 * a + k],
            device_id=to, device_id_type=_MESH)

    def _mine(self, a):
        return pltpu.make_async_copy(self.srcs[a], self.outs[a].at[self.me], self.local_sems.at[a])

    def _first(self):
        c = self.pos[2]
        cps = []
        for a in range(self.na):
            cps.append(self._copy(a, 0, self.me, self.sibling, own=True))
            cps += [self._copy(a, 1 + j, self.me, (*chip, c), own=True) for j, chip in enumerate(self.chips)]
        return cps

    def _passed(self):
        c = self.pos[2]
        return [self._copy(a, 4 + j, self._slot(*chip, c), self.sibling)
                for j, chip in enumerate(self.chips) for a in range(self.na)]

    def start(self):
        for a in range(self.na):
            self._mine(a).start()
        for cp in self._first():
            cp.start()

    def forward(self, j):
        c = self.pos[2]
        chip = self.chips[j]
        for a in range(self.na):
            self._copy(a, 1 + j, self._slot(*chip, c), self.pos).wait_recv()
            self._copy(a, 4 + j, self._slot(*chip, c), self.sibling).start()

    def finish(self):
        x, y, c = self.pos
        for a in range(self.na):
            self._copy(a, 0, self._slot(x, y, 1 - c), self.pos).wait_recv()
        for j, chip in enumerate(self.chips):
            for a in range(self.na):
                self._copy(a, 4 + j, self._slot(*chip, 1 - c), self.pos).wait_recv()
        for cp in self._first() + self._passed():
            cp.wait_send()
        for a in range(self.na):
            self._mine(a).wait()


class _Exchange:
    def __init__(self, scatter, gather, outs, sems):
        self.ins = list(scatter) + list(gather)
        self.ns, self.na = len(scatter), len(scatter) + len(gather)
        self.outs = outs
        self.send_sems, self.recv_sems, self.local_sems = sems
        x, y, c = _coords()
        self.pos = (x, y, c)
        self.me = 4 * x + 2 * y + c

    def _peer(self, r):
        x, y, c = self.pos
        return x ^ (r >> 2), y ^ ((r >> 1) & 1), c ^ (r & 1)

    def _src(self, a, block):
        return self.ins[a].at[block] if a < self.ns else self.ins[a]

    def _local(self, a):
        return pltpu.make_async_copy(self._src(a, self.me), self.outs[a].at[self.me], self.local_sems.at[a])

    def _send(self, a, r):
        px, py, pc = self._peer(r)
        return pltpu.make_async_remote_copy(
            src_ref=self._src(a, 4 * px + 2 * py + pc), dst_ref=self.outs[a].at[self.me],
            send_sem=self.send_sems.at[7 * a + r - 1], recv_sem=self.recv_sems.at[7 * a + r - 1],
            device_id=(px, py, pc), device_id_type=_MESH)

    def _recv(self, a, r):
        px, py, pc = self._peer(r)
        return pltpu.make_async_remote_copy(
            src_ref=self._src(a, self.me), dst_ref=self.outs[a].at[4 * px + 2 * py + pc],
            send_sem=self.send_sems.at[7 * a + r - 1], recv_sem=self.recv_sems.at[7 * a + r - 1],
            device_id=(px, py, pc), device_id_type=_MESH)

    def start(self):
        for a in range(self.na):
            self._local(a).start()
        for r in range(1, N_DEV):
            for a in range(self.na):
                self._send(a, r).start()

    def finish(self):
        for r in range(1, N_DEV):
            for a in range(self.na):
                self._recv(a, r).wait_recv()
        for r in range(1, N_DEV):
            for a in range(self.na):
                self._send(a, r).wait_send()
        for a in range(self.na):
            self._local(a).wait()


def _allgather_first(gather_f32, cast_f32, gather_dtypes):
    ng, nc = len(gather_f32), len(cast_f32)

    def body(*refs):
        ins, cins = refs[:ng], refs[ng:ng + nc]
        outs, couts = refs[ng + nc:2 * ng + nc], refs[2 * ng + nc:2 * ng + 2 * nc]
        stage = refs[2 * ng + 2 * nc:3 * ng + 2 * nc]
        sems = refs[3 * ng + 2 * nc:]
        for a in range(ng):
            stage[a][...] = ins[a][...].astype(gather_dtypes[a])
        g = _Gather(stage, outs, sems)
        g.start()
        for a in range(nc):
            couts[a][...] = cins[a][...].astype(_BF)
        for j in range(3):
            g.forward(j)
        g.finish()

    vm = pl.BlockSpec(memory_space=pltpu.VMEM)
    return pl.pallas_call(
        body, name="allgather_first",
        in_specs=[vm] * (ng + nc),
        out_specs=[pl.BlockSpec(memory_space=pl.ANY)] * ng + [vm] * nc,
        out_shape=[_S((N_DEV,) + l.shape, dt) for l, dt in zip(gather_f32, gather_dtypes)]
        + [_S(l.shape, _BF) for l in cast_f32],
        scratch_shapes=[pltpu.VMEM(l.shape, dt) for l, dt in zip(gather_f32, gather_dtypes)] + _sem_shapes(ng),
        compiler_params=pltpu.CompilerParams(vmem_limit_bytes=VMEM_LIMIT),
    )(*gather_f32, *cast_f32)


def _adamw_math(w, g, m, v):
    m2 = ADAM_B1 * m + (1.0 - ADAM_B1) * g
    v2 = ADAM_B2 * v + (1.0 - ADAM_B2) * (g * g)
    m_hat = m2 / (1.0 - ADAM_B1 ** ADAM_STEP)
    v_hat = v2 / (1.0 - ADAM_B2 ** ADAM_STEP)
    delta = -ADAM_LR * (m_hat / (jnp.sqrt(v_hat) + ADAM_EPS) + ADAM_WD * w)
    return delta, m2, v2


def _adamw_big(name, recv, w, m, v, rows):
    r_all, c_all = w.shape

    def body(r_ref, w_ref, m_ref, v_ref, g_out, d_out, m_out, v_out):
        g = r_ref[0].astype(_F32)
        for k in range(1, N_DEV):
            g = g + r_ref[k].astype(_F32)
        delta, m2, v2 = _adamw_math(w_ref[...], g, m_ref[...], v_ref[...])
        g_out[...] = g
        d_out[...] = delta
        m_out[...] = m2
        v_out[...] = v2

    tile = pl.BlockSpec((rows, c_all), lambda i: (i, 0))
    return pl.pallas_call(
        body, name=name, grid=(r_all // rows,),
        in_specs=[pl.BlockSpec((N_DEV, rows, c_all), lambda i: (0, i, 0)), tile, tile, tile],
        out_specs=[tile] * 4,
        out_shape=[_S(w.shape, _F32)] * 4,
        compiler_params=_cp(("arbitrary",)),
    )(recv, w, m, v)


def _adamw_small(gathered, slices, wmv):
    ng, npar = len(gathered), len(slices)

    def body(*refs):
        g_refs = refs[:ng]
        wmv_refs = refs[ng:ng + 3 * npar]
        outs = refs[ng + 3 * npar:]
        for i, (ai, r0, nr, c0, ncol) in enumerate(slices):
            g = g_refs[ai][0, pl.ds(r0, nr), pl.ds(c0, ncol)].astype(_F32)
            for k in range(1, N_DEV):
                g = g + g_refs[ai][k, pl.ds(r0, nr), pl.ds(c0, ncol)].astype(_F32)
            w_ref, m_ref, v_ref = wmv_refs[3 * i:3 * i + 3]
            delta, m2, v2 = _adamw_math(w_ref[...], g, m_ref[...], v_ref[...])
            outs[4 * i][...] = g
            outs[4 * i + 1][...] = delta
            outs[4 * i + 2][...] = m2
            outs[4 * i + 3][...] = v2
        total = g_refs[0][0, pl.ds(R_LOSS, 1), pl.ds(0, 128)]
        for k in range(1, N_DEV):
            total = total + g_refs[0][k, pl.ds(R_LOSS, 1), pl.ds(0, 128)]
        outs[4 * npar][...] = total

    flat = [t for trip in wmv for t in trip]
    out_shape = []
    for w, _, _ in wmv:
        out_shape += [_S(w.shape, _F32)] * 4
    out_shape.append(_S((1, 128), _F32))
    return pl.pallas_call(
        body, name="adamw_small", out_shape=out_shape,
        compiler_params=pltpu.CompilerParams(vmem_limit_bytes=VMEM_LIMIT),
    )(*gathered, *flat)


def _block_diag(w):
    eye = jnp.eye(8, dtype=w.dtype)
    return (w[:, :, None, :] * eye[:, None, :, None]).reshape(D_RG, D_RG)


def _diag_blocks(g):
    return jnp.concatenate([g[64 * h:64 * (h + 1), 64 * h:64 * (h + 1)] for h in range(8)], axis=0)


def _local_step(x, tgt, meta, g_mix, w_in, vec, wr, wi, hb, g_hg, w_out_l, g_ffn, w_gu_l, w_down_l, g_fin):
    seq = x.shape[0]
    n_valid = N_META + seq
    t_pad = -(-n_valid // TM) * TM
    h0 = jnp.concatenate([meta, x, jnp.zeros((t_pad - n_valid, D), _F32)], axis=0)
    tgt_p = jnp.concatenate([jnp.zeros((N_META, D), _F32), tgt, jnp.zeros((t_pad - n_valid, D), _F32)], axis=0)

    p, u, w_out, w_down = _inproj(h0, g_mix, w_in, [w_out_l, w_down_l])
    y, hs, o, sc, w_gu = _mixer_fwd(p, wr, wi, vec, hb, g_hg, [w_gu_l])
    w_out = w_out.reshape(D, D)
    w_down = w_down.reshape(4, FFB, D)
    h1, v = _outproj(h0, y, w_out, g_ffn)
    gu, act, dh2, dh2b, loss, gfin = _ffn_loss(v, h1, w_gu, w_down, g_fin, tgt_p, n_valid)

    dgu, dh1, dh1b, dy, gffn = _ffn_bwd(dh2, dh2b, gu, h1, g_ffn, w_gu, w_down, w_out)
    g_wdown = _wgrad("wgrad_down", act, dh2b, pl.BlockSpec((1, t_pad, FFB), lambda j: (j, 0, 0)),
                     pl.BlockSpec((t_pad, D), lambda j: (0, 0)), 4, (FFB, D))
    g_wgu, r_wdown = _wgrad("wgrad_gate_up", dgu, v, pl.BlockSpec((1, t_pad, FFB), lambda j: (j, 0, 0)),
                            pl.BlockSpec((t_pad, D), lambda j: (0, 0)), N_DEV, (FFB, D),
                            scatter=[g_wdown.reshape(N_DEV, D_FF // N_DEV, D)])
    g_wout = _wgrad("wgrad_out", y, dh1b, pl.BlockSpec((t_pad, D // N_DEV), lambda j: (0, j)),
                    pl.BlockSpec((t_pad, D), lambda j: (0, 0)), N_DEV, (D // N_DEV, D))
    dp, gvec, gw, r_wgu, r_wout = _mixer_bwd(p, hs, o, sc, dy, wr, wi, vec, hb, g_hg, [g_wgu, g_wout])
    pack_c = jnp.concatenate([_diag_blocks(gw[0]), _diag_blocks(gw[1])], axis=1).astype(_BF)
    me = 4 * lax.axis_index("x") + 2 * lax.axis_index("y") + lax.axis_index("c")
    order = (me ^ jnp.array(_SEND_ORDER, jnp.int32)).astype(jnp.int32)
    dh0, r_win, all_b, all_c, all_a = _inproj_bwd_send(dp, w_in, h0, dh1, g_mix, u, order, gffn, gfin, loss,
                                                       [gvec, pack_c])
    return dh0, (r_win, r_wgu, r_wout, r_wdown), (all_a, all_b, all_c)


def kernel(x, meta_tokens, mix_norm_g, w_in, conv_w, conv_b, w_rgate, b_rgate, w_igate, b_igate, lru_lambda, rg_norm_g, hg_lower_bound, hg_norm_g, w_out, ffn_norm_g, w_gate_up, w_down, final_norm_g, loss_target, m_meta_tokens, m_mix_norm_g, m_w_in, m_conv_w, m_conv_b, m_w_rgate, m_b_rgate, m_w_igate, m_b_igate, m_lru_lambda, m_rg_norm_g, m_hg_lower_bound, m_hg_norm_g, m_w_out, m_ffn_norm_g, m_w_gate_up, m_w_down, m_final_norm_g, v_meta_tokens, v_mix_norm_g, v_w_in, v_conv_w, v_conv_b, v_w_rgate, v_b_rgate, v_w_igate, v_b_igate, v_lru_lambda, v_rg_norm_g, v_hg_lower_bound, v_hg_norm_g, v_w_out, v_ffn_norm_g, v_w_gate_up, v_w_down, v_final_norm_g):
    seq = x.shape[1]
    me = 4 * lax.axis_index("x") + 2 * lax.axis_index("y") + lax.axis_index("c")

    small_l = jnp.concatenate([meta_tokens, jnp.pad(conv_w[0], ((0, 4), (0, 64)))], axis=0)
    w_in_g, small_g, w_gu_l, w_out_l, w_down_l = _allgather_first(
        [w_in[0], small_l], [w_gate_up[0].T, w_out[0], w_down[0]], [_BF, _F32])
    meta_full = jnp.transpose(small_g[:, :N_META, :], (1, 0, 2)).reshape(N_META, D)
    conv_w_full = jnp.transpose(small_g[:, N_META:N_META + 4, :64], (1, 0, 2)).reshape(4, D_RG)
    vec = jnp.concatenate([conv_b, b_rgate, b_igate, lru_lambda, rg_norm_g, jnp.zeros((3, D_RG), _F32),
                           conv_w_full, jnp.zeros((4, D_RG), _F32)], axis=0)
    wr = _block_diag(w_rgate[0]).astype(_BF)
    wi = _block_diag(w_igate[0]).astype(_BF)

    dh0, (r_win, r_wgu, r_wout, r_wdown), (all_a, all_b, all_c) = _local_step(
        x[0], loss_target[0], meta_full, mix_norm_g, w_in_g, vec, wr, wi, hg_lower_bound, hg_norm_g,
        w_out_l, ffn_norm_g, w_gu_l, w_down_l, final_norm_g.reshape(1, D))
    grad_x = dh0[N_META:N_META + seq][None]

    outs = {}
    outs["w_in"] = _adamw_big("adamw_w_in", r_win, w_in[0], m_w_in[0], v_w_in[0], 256)
    outs["w_gate_up"] = [r.T for r in _adamw_big("adamw_w_gate_up", r_wgu, w_gate_up[0].T, m_w_gate_up[0].T,
                                                 v_w_gate_up[0].T, 176)]
    outs["w_out"] = _adamw_big("adamw_w_out", r_wout, w_out[0], m_w_out[0], v_w_out[0], 128)
    outs["w_down"] = _adamw_big("adamw_w_down", r_wdown, w_down[0], m_w_down[0], v_w_down[0], 176)

    meta_part = lax.dynamic_slice_in_dim(all_a[:, R_META:R_META + N_META, :], me * 128, 128, axis=2)
    convw_part = lax.dynamic_slice_in_dim(all_b[:, R_CONVW:R_CONVW + 4, :], me * 64, 64, axis=2)
    gathered = [all_a, all_b, all_c, meta_part, convw_part]
    small_params = [
        ("meta_tokens", (3, 0, N_META, 0, 128), (meta_tokens, m_meta_tokens, v_meta_tokens), (N_META, 128)),
        ("mix_norm_g", (0, R_GMIX, 1, 0, D), (mix_norm_g, m_mix_norm_g, v_mix_norm_g), (1, D)),
        ("conv_w", (4, 0, 4, 0, 64), (conv_w, m_conv_w, v_conv_w), (4, 64)),
        ("conv_b", (1, R_CONVB, 1, 0, D_RG), (conv_b, m_conv_b, v_conv_b), (1, D_RG)),
        ("w_rgate", (2, 0, 512, 0, 64), (w_rgate, m_w_rgate, v_w_rgate), (512, 64)),
        ("b_rgate", (1, R_BR, 1, 0, D_RG), (b_rgate, m_b_rgate, v_b_rgate), (1, D_RG)),
        ("w_igate", (2, 0, 512, 64, 64), (w_igate, m_w_igate, v_w_igate), (512, 64)),
        ("b_igate", (1, R_BI, 1, 0, D_RG), (b_igate, m_b_igate, v_b_igate), (1, D_RG)),
        ("lru_lambda", (1, R_LAM, 1, 0, D_RG), (lru_lambda, m_lru_lambda, v_lru_lambda), (1, D_RG)),
        ("rg_norm_g", (1, R_GRG, 1, 0, D_RG), (rg_norm_g, m_rg_norm_g, v_rg_norm_g), (1, D_RG)),
        ("hg_lower_bound", (1, R_HB0, 2, 0, D_HG), (hg_lower_bound, m_hg_lower_bound, v_hg_lower_bound), (2, D_HG)),
        ("hg_norm_g", (1, R_GHG, 1, 0, HD), (hg_norm_g, m_hg_norm_g, v_hg_norm_g), (1, HD)),
        ("ffn_norm_g", (0, R_GFFN, 1, 0, D), (ffn_norm_g, m_ffn_norm_g, v_ffn_norm_g), (1, D)),
        ("final_norm_g", (0, R_GFIN, 1, 0, D), (final_norm_g, m_final_norm_g, v_final_norm_g), (1, D)),
    ]
    res = _adamw_small(gathered, [s[1] for s in small_params],
                       [tuple(t.reshape(s[3]) for t in s[2]) for s in small_params])
    for i, s in enumerate(small_params):
        outs[s[0]] = [r.reshape(s[2][0].shape) for r in res[4 * i:4 * i + 4]]
    for n, ref in (("w_in", w_in), ("w_gate_up", w_gate_up), ("w_out", w_out), ("w_down", w_down)):
        outs[n] = [r.reshape(ref.shape) for r in outs[n]]

    loss_all = res[4 * len(small_params)][0, 0]
    order = ["meta_tokens", "mix_norm_g", "w_in", "conv_w", "conv_b", "w_rgate", "b_rgate", "w_igate", "b_igate",
             "lru_lambda", "rg_norm_g", "hg_lower_bound", "hg_norm_g", "w_out", "ffn_norm_g", "w_gate_up", "w_down",
             "final_norm_g"]
    return (loss_all, grad_x, *[outs[n][0] for n in order], *[outs[n][1] for n in order],
            *[outs[n][2] for n in order], *[outs[n][3] for n in order])
```

```python
import functools

import jax
import jax.numpy as jnp
from jax import lax
from jax.experimental import pallas as pl
from jax.experimental.pallas import tpu as pltpu

_BF = jnp.bfloat16
_F32 = jnp.float32
_S = jax.ShapeDtypeStruct
_MESH = pl.DeviceIdType.MESH

N_DEV = 8
N_META = 16
D = 1024
D_RG = 512
D_HG = 512
HD = 128
NH = D_HG // HD
D_IN = 3072
D_FF = 2816
FFB = D_FF // 4
WIN_B = D_IN // N_DEV
EPS = 1e-6
LRU_C = 8.0
TM = 256
HC = 64
VMEM_LIMIT = 56 * 1024 * 1024

ADAM_LR = 0.001
ADAM_B1 = 0.9
ADAM_B2 = 0.999
ADAM_EPS = 1e-08
ADAM_WD = 0.01
ADAM_STEP = 10

_SEND_ORDER = (6, 4, 2, 7, 5, 3, 1, 0)

_GATHER_REL = (0, 1, 4, 2, 6, 5, 3, 7)

R_CONVB, R_BR, R_BI, R_LAM, R_GRG, R_HB0, R_HB1, R_GHG, R_CONVW = 0, 1, 2, 3, 4, 5, 6, 7, 8
R_GMIX, R_GFFN, R_GFIN, R_LOSS, R_META = 0, 1, 2, 3, 8


def _cp(sem=None, **kw):
    return pltpu.CompilerParams(dimension_semantics=sem, vmem_limit_bytes=VMEM_LIMIT, **kw)


def _dot(a, b):
    return jnp.dot(a, b, preferred_element_type=_F32)


def _dot_nt(a, b):
    return lax.dot_general(a, b, (((1,), (1,)), ((), ())), preferred_element_type=_F32)


def _dot_tn(a, b):
    return lax.dot_general(a, b, (((0,), (0,)), ((), ())), preferred_element_type=_F32)


def _sigmoid(x):
    return jax.nn.sigmoid(x)


def _dsilu(x, s):
    return s * (1.0 + x * (1.0 - s))


_GELU_C = 0.7978845608028654


def _gelu_parts(x):
    t = jnp.tanh(_GELU_C * (x + 0.044715 * (x * x * x)))
    g = 0.5 * x * (1.0 + t)
    dg = 0.5 * (1.0 + t) + 0.5 * x * (1.0 - t * t) * (_GELU_C * (1.0 + 3.0 * 0.044715 * (x * x)))
    return g, dg


def _softplus(z):
    e = jnp.exp(-jnp.abs(z))
    w = 1.0 + e
    l1p = jnp.where(w == 1.0, e, jnp.log(w) * e / jnp.where(w == 1.0, 1.0, w - 1.0))
    return jnp.maximum(z, 0.0) + l1p


def _rms_fwd(x):
    r = lax.rsqrt(jnp.mean(x * x, axis=-1, keepdims=True) + EPS)
    return x * r, r


def _rms_bwd(dyg, n, r):
    return r * (dyg - n * jnp.mean(dyg * n, axis=-1, keepdims=True))


def _full(shape):
    nd = len(shape)
    return pl.BlockSpec(shape, lambda i: (0,) * nd)


def _const(shape):
    nd = len(shape)
    return pl.BlockSpec(shape, lambda i: (0,) * nd, pipeline_mode=pl.Buffered(1))


def _carry_gather(gather, i, nt):
    @pl.when(i == 0)
    def _():
        gather.start()

    def tail():
        for j in range(3):
            @pl.when(i == max(nt - 3 + j, 0))
            def _(j=j):
                gather.forward(j)

        @pl.when(i == nt - 1)
        def _():
            gather.finish()

    return tail


def _inproj(h0, g_mix, order, w_in_l, shards):
    t_pad = h0.shape[0]
    nt = t_pad // TM
    nsh = len(shards)

    def body(order_ref, h_ref, g_ref, wl_ref, *rest):
        sh_refs = rest[:nsh]
        p_ref, u_ref, wg_ref = rest[nsh:nsh + 3]
        gath_refs = rest[nsh + 3:2 * nsh + 3]
        u_s, wbuf, wsem = rest[2 * nsh + 3:2 * nsh + 6]
        g_w = _Gather([wl_ref], [wg_ref], rest[2 * nsh + 6:2 * nsh + 9])
        g_sh = _Gather(sh_refs, gath_refs, rest[2 * nsh + 9:])
        j, i = pl.program_id(0), pl.program_id(1)

        @pl.when((j == 0) & (i == 0))
        def _():
            g_w.start()
            g_sh.start()

        @pl.when(i == 0)
        def _():
            for step in range(N_DEV):
                @pl.when(j == step)
                def _(step=step):
                    if step == 0:
                        src = wl_ref
                    else:
                        if step == 1:
                            g_w.wait_sibling()
                        elif step <= 4:
                            g_w.forward(step - 2)
                        else:
                            g_w.wait_passed(step - 5)
                        src = wg_ref.at[order_ref[step]]
                    cp = pltpu.make_async_copy(src, wbuf, wsem.at[0])
                    cp.start()
                    cp.wait()

        @pl.when(j == 0)
        def _():
            n, _ = _rms_fwd(h_ref[...])
            u = (n * g_ref[...]).astype(_BF)
            u_s[i] = u
            u_ref[...] = u

        p_ref[...] = _dot(u_s[i], wbuf[...])

        @pl.when((j == N_DEV - 1) & (i == nt - 1))
        def _():
            g_w.finish_sends()
            for jj in range(3):
                g_sh.forward(jj)
            g_sh.finish()

    hbm = pl.BlockSpec(memory_space=pl.ANY)
    return pl.pallas_call(
        body, name="inproj",
        grid_spec=pltpu.PrefetchScalarGridSpec(
            num_scalar_prefetch=1, grid=(N_DEV, nt),
            in_specs=[pl.BlockSpec((TM, D), lambda j, i, order: (jnp.where(j == 0, i, 0), 0)),
                      pl.BlockSpec((1, D), lambda j, i, order: (0, 0)), hbm] + [hbm] * nsh,
            out_specs=[pl.BlockSpec((TM, WIN_B), lambda j, i, order: (i, order[j])),
                       pl.BlockSpec((TM, D), lambda j, i, order: (jnp.where(j == 0, i, nt - 1), 0)), hbm] + [hbm] * nsh,
            scratch_shapes=[pltpu.VMEM((nt, TM, D), _BF), pltpu.VMEM((D, WIN_B), _BF), pltpu.SemaphoreType.DMA((1,))]
            + _sem_shapes(1) + _sem_shapes(nsh)),
        out_shape=[_S((t_pad, D_IN), _F32), _S((t_pad, D), _BF), _S((N_DEV, D, WIN_B), _BF)]
        + [_S((N_DEV,) + s.shape, s.dtype) for s in shards],
        compiler_params=_cp(("arbitrary", "arbitrary")),
    )(order, h0, g_mix, w_in_l, *shards)


def _rg_gates(xc, wr_ref, wi_ref, vec_ref):
    xcb = xc.astype(_BF)
    r = _sigmoid(_dot(xcb, wr_ref[...]) + vec_ref[R_BR:R_BR + 1, :])
    ig = _sigmoid(_dot(xcb, wi_ref[...]) + vec_ref[R_BI:R_BI + 1, :])
    nsp8 = -LRU_C * _softplus(-vec_ref[R_LAM:R_LAM + 1, :])
    la = nsp8 * r
    a = jnp.exp(la)
    th = jnp.tanh(la)
    s = jnp.sqrt(-2.0 * th / (1.0 - th))
    return r, ig, a, s, nsp8


def _conv(xbuf, vec_ref):
    acc = vec_ref[R_CONVW:R_CONVW + 1, :] * xbuf[pl.ds(5, TM), :]
    for j in range(1, 4):
        acc = acc + vec_ref[R_CONVW + j:R_CONVW + j + 1, :] * xbuf[pl.ds(5 + j, TM), :]
    return vec_ref[R_CONVB:R_CONVB + 1, :] + acc


def _dot3(m01, x):
    hi = x.astype(_BF)
    r1 = x - hi.astype(_F32)
    mid = r1.astype(_BF)
    lo = (r1 - mid.astype(_F32)).astype(_BF)
    return (_dot(m01, lo) + _dot(m01, mid)) + _dot(m01, hi)


def _chunk_masks():
    row = lax.broadcasted_iota(jnp.int32, (TM, TM), 0)
    col = lax.broadcasted_iota(jnp.int32, (TM, TM), 1)
    shift = HC.bit_length() - 1
    same = lax.shift_right_logical(row, shift) == lax.shift_right_logical(col, shift)
    return same, same & (row >= col), same & (col >= row)


def _per_chunk_rows(x, r):
    return jnp.concatenate([jnp.broadcast_to(x[HC * c + r:HC * c + r + 1, :], (HC, x.shape[1]))
                            for c in range(TM // HC)], axis=0)


def _hg_prep(p_ref, lb, tri_blk):
    hq = p_ref[:, pl.ds(2 * D_RG, D_HG)]
    hf = p_ref[:, pl.ds(2 * D_RG + D_HG, D_HG)]
    sq = _sigmoid(hq)
    q = hq * sq
    sg = _sigmoid(hf)
    f = lb + (1.0 - lb) * sg
    k = 1.0 - f
    b = _dot3(tri_blk, jnp.log(f))
    bm = _per_chunk_rows(b, HC // 2 - 1)
    bl = _per_chunk_rows(b, HC - 1)
    e_q = jnp.exp(b - bm)
    e_k = jnp.exp(bm - b)
    e_b = jnp.exp(b)
    e_l = jnp.exp(bl - b)
    return dict(hq=hq, sq=sq, q=q, sg=sg, f=f, k=k, e_q=e_q, e_k=e_k, e_b=e_b, e_l=e_l,
                qd=q * e_q, kd=k * e_k, qe=q * e_b, ke=k * e_l, e_end=jnp.exp(bl))


def _mixer_fwd(p, wr, wi, vec, hb, g_hg, shards):
    t_pad = p.shape[0]
    nt = t_pad // TM
    nc_t = TM // HC
    nsh = len(shards)

    def body(p_ref, wr_ref, wi_ref, vec_ref, hb_ref, ghg_ref, *rest):
        sh_refs, rest = rest[:nsh], rest[nsh:]
        y_ref, hs_ref, o_ref, sc_ref = rest[:4]
        gath_refs, rest = rest[4:4 + nsh], rest[4 + nsh:]
        xbuf, a_s, b_s, hcar, st, qd_s, kd_s, qe_s, ke_s, v_s, u_s = rest[:11]
        i = pl.program_id(0)
        tail = _carry_gather(_Gather(sh_refs, gath_refs, rest[11:]), i, nt)

        @pl.when(i == 0)
        def _():
            xbuf[pl.ds(0, 8), :] = jnp.zeros((8, D_RG), _F32)
            hcar[...] = jnp.zeros_like(hcar)
            st[...] = jnp.zeros_like(st)

        x = p_ref[:, pl.ds(0, D_RG)]
        xbuf[pl.ds(8, TM), :] = x
        xc = _conv(xbuf, vec_ref)
        xbuf[pl.ds(0, 8), :] = x[TM - 8:, :]
        r, ig, a, s, _ = _rg_gates(xc, wr_ref, wi_ref, vec_ref)
        a_s[...] = a
        b_s[...] = s * (ig * xc)

        def step(t, h):
            h = a_s[pl.ds(t, 1), :] * h + b_s[pl.ds(t, 1), :]
            hs_ref[pl.ds(t, 1), :] = h
            return h

        hcar[pl.ds(0, 1), :] = lax.fori_loop(0, TM, step, hcar[pl.ds(0, 1), :], unroll=8)
        gel, _ = _gelu_parts(p_ref[:, pl.ds(D_RG, D_RG)])
        n, _ = _rms_fwd(gel * hs_ref[...])
        y_ref[:, pl.ds(0, D_RG)] = (n * vec_ref[R_GRG:R_GRG + 1, :]).astype(_BF)

        lb = _sigmoid(hb_ref[0:1, :] - hb_ref[1:2, :])
        _, tri_blk, _ = _chunk_masks()
        q = _hg_prep(p_ref, lb, tri_blk.astype(_BF))
        for name, ref in (("qd", qd_s), ("kd", kd_s), ("qe", qe_s), ("ke", ke_s)):
            ref[...] = q[name].astype(_BF)
        v_s[...] = p_ref[:, pl.ds(2 * D_RG + 2 * D_HG, D_HG)].astype(_BF)
        e_end = q["e_end"]
        causal = (lax.broadcasted_iota(jnp.int32, (HC, HC), 0) >= lax.broadcasted_iota(jnp.int32, (HC, HC), 1))
        for c in range(nc_t):
            for h in range(NH):
                rs, cs = pl.ds(HC * c, HC), pl.ds(HD * h, HD)
                amat = jnp.where(causal, _dot_nt(qd_s[rs, cs], kd_s[rs, cs]), 0.0)
                o_ref[rs, cs] = _dot(amat.astype(_BF), v_s[rs, cs])
                u_s[NH * c + h] = _dot_tn(v_s[rs, cs], ke_s[rs, cs])
        for h in range(NH):
            cs = pl.ds(HD * h, HD)
            s_run = st[h]
            for c in range(nc_t):
                rs = pl.ds(HC * c, HC)
                sc_ref[c, h] = s_run
                o_ref[rs, cs] += _dot_nt(qe_s[rs, cs], s_run.astype(_BF))
                s_run = e_end[HC * c:HC * c + 1, HD * h:HD * (h + 1)] * s_run + u_s[NH * c + h]
            st[h] = s_run
        for h in range(NH):
            cs = pl.ds(HD * h, HD)
            n_o, _ = _rms_fwd(o_ref[:, cs])
            hg = p_ref[:, pl.ds(2 * D_RG + 3 * D_HG + HD * h, HD)]
            y_ref[:, pl.ds(D_RG + HD * h, HD)] = ((n_o * ghg_ref[...]) * (hg * _sigmoid(hg))).astype(_BF)

        tail()

    hbm = pl.BlockSpec(memory_space=pl.ANY)
    return pl.pallas_call(
        body, name="mixer_fwd", grid=(nt,),
        in_specs=[pl.BlockSpec((TM, D_IN), lambda i: (i, 0)), _full((D_RG, D_RG)), _full((D_RG, D_RG)),
                  _full((16, D_RG)), _full((2, D_HG)), _full((1, HD))] + [hbm] * nsh,
        out_specs=[pl.BlockSpec((TM, D), lambda i: (i, 0)), pl.BlockSpec((TM, D_RG), lambda i: (i, 0)),
                   pl.BlockSpec((TM, D_HG), lambda i: (i, 0)),
                   pl.BlockSpec((nc_t, NH, HD, HD), lambda i: (i, 0, 0, 0))] + [hbm] * nsh,
        out_shape=[_S((t_pad, D), _BF), _S((t_pad, D_RG), _F32), _S((t_pad, D_HG), _F32),
                   _S((t_pad // HC, NH, HD, HD), _F32)] + [_S((N_DEV,) + s.shape, s.dtype) for s in shards],
        scratch_shapes=[pltpu.VMEM((TM + 8, D_RG), _F32), pltpu.VMEM((TM, D_RG), _F32),
                        pltpu.VMEM((TM, D_RG), _F32), pltpu.VMEM((8, D_RG), _F32),
                        pltpu.VMEM((NH, HD, HD), _F32)] + [pltpu.VMEM((TM, D_HG), _BF) for _ in range(5)]
        + [pltpu.VMEM((nc_t * NH, HD, HD), _F32)] + _sem_shapes(nsh),
        compiler_params=_cp(("arbitrary",)),
    )(p, wr, wi, vec, hb, g_hg, *shards)


def _outproj(h0, y, w_out, g_ffn):
    t_pad = h0.shape[0]

    def body(h_ref, y_ref, w_ref, g_ref, h1_ref, v_ref):
        h1 = h_ref[...] + _dot(y_ref[...], w_ref[...])
        h1_ref[...] = h1
        n, _ = _rms_fwd(h1)
        v_ref[...] = (n * g_ref[...]).astype(_BF)

    return pl.pallas_call(
        body, name="outproj", grid=(t_pad // TM,),
        in_specs=[pl.BlockSpec((TM, D), lambda i: (i, 0)), pl.BlockSpec((TM, D), lambda i: (i, 0)),
                  _full((D, D)), _full((1, D))],
        out_specs=[pl.BlockSpec((TM, D), lambda i: (i, 0)), pl.BlockSpec((TM, D), lambda i: (i, 0))],
        out_shape=[_S((t_pad, D), _F32), _S((t_pad, D), _BF)],
        compiler_params=_cp(("arbitrary",)),
    )(h0, y, w_out, g_ffn)


def _ffn_loss(v, h1, w_gu, w_down, g_fin, tgt, n_valid):
    t_pad = v.shape[0]

    def body(v_ref, h1_ref, wgu_ref, wd_ref, g_ref, t_ref, gu_ref, act_ref, dh2_ref, dh2b_ref, loss_ref, gfin_ref):
        i = pl.program_id(0)

        @pl.when(i == 0)
        def _():
            loss_ref[...] = jnp.zeros_like(loss_ref)
            gfin_ref[...] = jnp.zeros_like(gfin_ref)

        vb = v_ref[...]
        h2 = h1_ref[...]
        for b in range(4):
            gate = _dot_nt(vb, wgu_ref[b])
            up = _dot_nt(vb, wgu_ref[4 + b])
            gu_ref[b] = gate
            gu_ref[4 + b] = up
            act = ((gate * _sigmoid(gate)) * up).astype(_BF)
            act_ref[b] = act
            h2 = h2 + _dot(act, wd_ref[b])
        n, r = _rms_fwd(h2)
        out = n * g_ref[...]
        row = i * TM + lax.broadcasted_iota(jnp.int32, (TM, 1), 0)
        valid = (row >= N_META) & (row < n_valid)
        err = jnp.where(valid, out - t_ref[...], 0.0)
        loss_ref[...] += (0.5 / D) * jnp.sum(err * err)
        dout = err * (1.0 / D)
        gfin_ref[...] += jnp.sum(dout * n, axis=0, keepdims=True)
        dh2 = _rms_bwd(dout * g_ref[...], n, r)
        dh2_ref[...] = dh2
        dh2b_ref[...] = dh2.astype(_BF)

    return pl.pallas_call(
        body, name="ffn_loss", grid=(t_pad // TM,),
        in_specs=[pl.BlockSpec((TM, D), lambda i: (i, 0)), pl.BlockSpec((TM, D), lambda i: (i, 0)),
                  _const((N_DEV, FFB, D)), _const((4, FFB, D)), _full((1, D)),
                  pl.BlockSpec((TM, D), lambda i: (i, 0))],
        out_specs=[pl.BlockSpec((N_DEV, TM, FFB), lambda i: (0, i, 0)), pl.BlockSpec((4, TM, FFB), lambda i: (0, i, 0)),
                   pl.BlockSpec((TM, D), lambda i: (i, 0)), pl.BlockSpec((TM, D), lambda i: (i, 0)),
                   _full((8, 128)), _full((1, D))],
        out_shape=[_S((N_DEV, t_pad, FFB), _F32), _S((4, t_pad, FFB), _BF), _S((t_pad, D), _F32),
                   _S((t_pad, D), _BF), _S((8, 128), _F32), _S((1, D), _F32)],
        compiler_params=_cp(("arbitrary",)),
    )(v, h1, w_gu, w_down, g_fin, tgt)


def _ffn_bwd(dh2, dh2b, gu, h1, g_ffn, w_gu, w_down, w_out):
    t_pad = dh2.shape[0]

    def body(dh2_ref, dh2b_ref, gu_ref, h1_ref, g_ref, wgu_ref, wd_ref, wo_ref,
             dgu_ref, dh1_ref, dh1b_ref, dy_ref, gffn_ref):
        i = pl.program_id(0)

        @pl.when(i == 0)
        def _():
            gffn_ref[...] = jnp.zeros_like(gffn_ref)

        db = dh2b_ref[...]
        dv = jnp.zeros((TM, D), _F32)
        for b in range(4):
            dact = _dot_nt(db, wd_ref[b])
            gate = gu_ref[b]
            up = gu_ref[4 + b]
            sg = _sigmoid(gate)
            dgate = ((dact * up) * _dsilu(gate, sg)).astype(_BF)
            dup = (dact * (gate * sg)).astype(_BF)
            dgu_ref[b] = dgate
            dgu_ref[4 + b] = dup
            dv = dv + _dot(dgate, wgu_ref[b]) + _dot(dup, wgu_ref[4 + b])
        n, r = _rms_fwd(h1_ref[...])
        gffn_ref[...] += jnp.sum(dv * n, axis=0, keepdims=True)
        dh1 = dh2_ref[...] + _rms_bwd(dv * g_ref[...], n, r)
        dh1_ref[...] = dh1
        dh1b = dh1.astype(_BF)
        dh1b_ref[...] = dh1b
        dy_ref[...] = _dot_nt(dh1b, wo_ref[...])

    tile = pl.BlockSpec((TM, D), lambda i: (i, 0))
    return pl.pallas_call(
        body, name="ffn_bwd", grid=(t_pad // TM,),
        in_specs=[tile, tile, pl.BlockSpec((N_DEV, TM, FFB), lambda i: (0, i, 0)), tile, _full((1, D)),
                  _const((N_DEV, FFB, D)), _const((4, FFB, D)), _const((D, D))],
        out_specs=[pl.BlockSpec((N_DEV, TM, FFB), lambda i: (0, i, 0)), tile, tile, tile, _full((1, D))],
        out_shape=[_S((N_DEV, t_pad, FFB), _BF), _S((t_pad, D), _F32), _S((t_pad, D), _BF),
                   _S((t_pad, D), _F32), _S((1, D), _F32)],
        compiler_params=_cp(("arbitrary",)),
    )(dh2, dh2b, gu, h1, g_ffn, w_gu, w_down, w_out)


def _mixer_bwd(p, hs, o, sc, dy, wr, wi, vec, hb, g_hg, scatter):
    t_pad = p.shape[0]
    nt = t_pad // TM
    nc_t = TM // HC
    nsc = len(scatter)

    def rev(i):
        return nt - 1 - i

    def body(p_ref, pprev_ref, hs_ref, hprev_ref, o_ref, sc_ref, dy_ref, wr_ref, wi_ref, vec_ref, hb_ref, ghg_ref,
             *rest):
        send_refs, rest = rest[:nsc], rest[nsc:]
        dp_ref, gvec_ref, gw_ref = rest[:3]
        recv_refs, rest = rest[3:3 + nsc], rest[3 + nsc:]
        xbuf, hbuf, dbuf, a_s, g_s, ccar, dst = rest[:7]
        qd_s, kd_s, qe_s, ke_s, v_s, do_s, dqd_s, dkd_s, dqe_s, dke_s, dv_s, w_s, dend_s = rest[7:20]
        exchange = _Exchange(send_refs, [], recv_refs, rest[20:])
        i = pl.program_id(0)
        first_tile = i == nt - 1

        @pl.when(i == 0)
        def _():
            exchange.start()
            gvec_ref[...] = jnp.zeros_like(gvec_ref)
            gw_ref[...] = jnp.zeros_like(gw_ref)
            dbuf[pl.ds(TM, 8), :] = jnp.zeros((8, D_RG), _F32)
            ccar[...] = jnp.zeros_like(ccar)
            dst[...] = jnp.zeros_like(dst)

        def acc(row, val):
            gvec_ref[row:row + 1, :] += jnp.sum(val, axis=0, keepdims=True)

        keep = jnp.where(first_tile, 0.0, 1.0)
        x = p_ref[:, pl.ds(0, D_RG)]
        xbuf[pl.ds(0, 8), :] = pprev_ref[...] * keep
        xbuf[pl.ds(8, TM), :] = x
        xc = _conv(xbuf, vec_ref)
        r, ig, a, s, nsp8 = _rg_gates(xc, wr_ref, wi_ref, vec_ref)
        h = hs_ref[...]
        hbuf[pl.ds(0, 8), :] = hprev_ref[...] * keep
        hbuf[pl.ds(8, TM), :] = h
        hm1 = hbuf[pl.ds(7, TM), :]
        gr = p_ref[:, pl.ds(D_RG, D_RG)]
        gel, dgel = _gelu_parts(gr)
        n, rr = _rms_fwd(gel * h)
        dyn = dy_ref[:, pl.ds(0, D_RG)]
        acc(R_GRG, dyn * n)
        dpre = _rms_bwd(dyn * vec_ref[R_GRG:R_GRG + 1, :], n, rr)
        dp_ref[:, pl.ds(D_RG, D_RG)] = ((dpre * h) * dgel).astype(_BF)
        a_s[...] = a
        g_s[...] = dpre * gel

        def step(k, c):
            t = TM - 1 - k
            g = g_s[pl.ds(t, 1), :] + c
            g_s[pl.ds(t, 1), :] = g
            return a_s[pl.ds(t, 1), :] * g

        ccar[pl.ds(0, 1), :] = lax.fori_loop(0, TM, step, ccar[pl.ds(0, 1), :], unroll=8)
        gt = g_s[...]
        da = gt * hm1
        ixc = ig * xc
        ds = gt * ixc
        dig = (gt * s) * xc
        dxc = (gt * s) * ig
        dla = da * a - ds * ((a * a) / s)
        lam = vec_ref[R_LAM:R_LAM + 1, :]
        gvec_ref[R_LAM:R_LAM + 1, :] += jnp.sum(dla * r, axis=0, keepdims=True) * (LRU_C * _sigmoid(-lam))
        dzr = (dla * nsp8) * (r * (1.0 - r))
        dzi = dig * (ig * (1.0 - ig))
        acc(R_BR, dzr)
        acc(R_BI, dzi)
        xcb = xc.astype(_BF)
        dzrb = dzr.astype(_BF)
        dzib = dzi.astype(_BF)
        gw_ref[0] += _dot_tn(xcb, dzrb)
        gw_ref[1] += _dot_tn(xcb, dzib)
        dxc = dxc + _dot_nt(dzrb, wr_ref[...]) + _dot_nt(dzib, wi_ref[...])
        acc(R_CONVB, dxc)
        for j in range(4):
            acc(R_CONVW + j, dxc * xbuf[pl.ds(5 + j, TM), :])
        dbuf[pl.ds(0, TM), :] = dxc
        dx = vec_ref[R_CONVW + 3:R_CONVW + 4, :] * dxc
        for j in range(3):
            dx = dx + vec_ref[R_CONVW + j:R_CONVW + j + 1, :] * dbuf[pl.ds(3 - j, TM), :]
        dbuf[pl.ds(TM, 8), :] = dxc[0:8, :]
        dp_ref[:, pl.ds(0, D_RG)] = dx.astype(_BF)

        lb = _sigmoid(hb_ref[0:1, :] - hb_ref[1:2, :])
        same, tri_blk, triu_blk = _chunk_masks()
        q = _hg_prep(p_ref, lb, tri_blk.astype(_BF))
        qdb, kdb = q["qd"].astype(_BF), q["kd"].astype(_BF)
        qd_s[...] = qdb
        kd_s[...] = kdb
        qe_s[...] = q["qe"].astype(_BF)
        ke_s[...] = q["ke"].astype(_BF)
        v_s[...] = p_ref[:, pl.ds(2 * D_RG + 2 * D_HG, D_HG)].astype(_BF)
        e_end = q["e_end"]
        ghg = ghg_ref[...]
        for h in range(NH):
            cs = pl.ds(HD * h, HD)
            hg = p_ref[:, pl.ds(2 * D_RG + 3 * D_HG + HD * h, HD)]
            sh = _sigmoid(hg)
            n_o, r_o = _rms_fwd(o_ref[:, cs])
            dyh = dy_ref[:, pl.ds(D_RG + HD * h, HD)]
            dp_ref[:, pl.ds(2 * D_RG + 3 * D_HG + HD * h, HD)] = ((dyh * (n_o * ghg)) * _dsilu(hg, sh)).astype(_BF)
            dn = dyh * (hg * sh)
            gvec_ref[R_GHG:R_GHG + 1, pl.ds(0, HD)] += jnp.sum(dn * n_o, axis=0, keepdims=True)
            do_s[:, cs] = _rms_bwd(dn * ghg, n_o, r_o).astype(_BF)
        causal = (lax.broadcasted_iota(jnp.int32, (HC, HC), 0) >= lax.broadcasted_iota(jnp.int32, (HC, HC), 1))
        for c in range(nc_t):
            for h in range(NH):
                rs, cs = pl.ds(HC * c, HC), pl.ds(HD * h, HD)
                qd_c, kd_c, do_c = qd_s[rs, cs], kd_s[rs, cs], do_s[rs, cs]
                amat = jnp.where(causal, _dot_nt(qd_c, kd_c), 0.0).astype(_BF)
                da_m = jnp.where(causal, _dot_nt(do_c, v_s[rs, cs]), 0.0).astype(_BF)
                dqd_s[rs, cs] = _dot(da_m, kd_c)
                dkd_s[rs, cs] = _dot_tn(da_m, qd_c)
                dqe_s[rs, cs] = _dot(do_c, sc_ref[c, h].astype(_BF))
                dv_s[rs, cs] = _dot_tn(amat, do_c)
                w_s[NH * c + h] = _dot_tn(do_c, qe_s[rs, cs])
        for h in range(NH):
            cs = pl.ds(HD * h, HD)
            d_run = dst[h]
            for c in reversed(range(nc_t)):
                rs = pl.ds(HC * c, HC)
                d_b = d_run.astype(_BF)
                dke_s[rs, cs] = _dot(v_s[rs, cs], d_b)
                dp_ref[rs, pl.ds(2 * D_RG + 2 * D_HG + HD * h, HD)] = (
                    dv_s[rs, cs] + _dot_nt(ke_s[rs, cs], d_b)).astype(_BF)
                dend_s[pl.ds(c, 1), cs] = jnp.sum(sc_ref[c, h] * d_run, axis=0, keepdims=True)
                d_run = w_s[NH * c + h] + e_end[HC * c:HC * c + 1, HD * h:HD * (h + 1)] * d_run
            dst[h] = d_run
        dqd, dkd, dqe, dke = dqd_s[...], dkd_s[...], dqe_s[...], dke_s[...]
        dq = dqd * q["e_q"] + dqe * q["e_b"]
        dk = dkd * q["e_k"] + dke * q["e_l"]
        dkeke = dke * q["ke"]
        db = dqd * qdb.astype(_F32) - dkd * kdb.astype(_F32) + dqe * q["qe"] - dkeke
        d_end = jnp.concatenate([jnp.broadcast_to(dend_s[pl.ds(c, 1), :], (HC, D_HG)) for c in range(nc_t)], axis=0)
        dlf = _dot3(triu_blk.astype(_BF), db) + _dot3(same.astype(_BF), dkeke) + d_end * e_end
        df = dlf / q["f"] - dk
        sg = q["sg"]
        gvec_ref[R_HB0:R_HB0 + 1, :] += jnp.sum(df * (1.0 - sg), axis=0, keepdims=True)
        dp_ref[:, pl.ds(2 * D_RG, D_HG)] = (dq * _dsilu(q["hq"], q["sq"])).astype(_BF)
        dp_ref[:, pl.ds(2 * D_RG + D_HG, D_HG)] = ((df * (1.0 - lb)) * (sg * (1.0 - sg))).astype(_BF)

        @pl.when(i == nt - 1)
        def _():
            glb = gvec_ref[R_HB0:R_HB0 + 1, :] * (lb * (1.0 - lb))
            gvec_ref[R_HB0:R_HB0 + 1, :] = glb
            gvec_ref[R_HB1:R_HB1 + 1, :] = -glb
            exchange.finish()

    hbm = pl.BlockSpec(memory_space=pl.ANY)
    return pl.pallas_call(
        body, name="mixer_bwd", grid=(nt,),
        in_specs=[pl.BlockSpec((TM, D_IN), lambda i: (rev(i), 0)),
                  pl.BlockSpec((8, D_RG), lambda i: (jnp.maximum(rev(i) * (TM // 8) - 1, 0), 0)),
                  pl.BlockSpec((TM, D_RG), lambda i: (rev(i), 0)),
                  pl.BlockSpec((8, D_RG), lambda i: (jnp.maximum(rev(i) * (TM // 8) - 1, 0), 0)),
                  pl.BlockSpec((TM, D_HG), lambda i: (rev(i), 0)),
                  pl.BlockSpec((nc_t, NH, HD, HD), lambda i: (rev(i), 0, 0, 0)),
                  pl.BlockSpec((TM, D), lambda i: (rev(i), 0)),
                  _full((D_RG, D_RG)), _full((D_RG, D_RG)), _full((16, D_RG)), _full((2, D_HG)), _full((1, HD))]
        + [hbm] * nsc,
        out_specs=[pl.BlockSpec((TM, D_IN), lambda i: (rev(i), 0)), _full((16, D_RG)), _full((2, D_RG, D_RG))]
        + [hbm] * nsc,
        out_shape=[_S((t_pad, D_IN), _BF), _S((16, D_RG), _F32), _S((2, D_RG, D_RG), _F32)]
        + [_S(s.shape, s.dtype) for s in scatter],
        scratch_shapes=[pltpu.VMEM((TM + 8, D_RG), _F32), pltpu.VMEM((TM + 8, D_RG), _F32),
                        pltpu.VMEM((TM + 8, D_RG), _F32), pltpu.VMEM((TM, D_RG), _F32),
                        pltpu.VMEM((TM, D_RG), _F32), pltpu.VMEM((8, D_RG), _F32),
                        pltpu.VMEM((NH, HD, HD), _F32)]
        + [pltpu.VMEM((TM, D_HG), _BF) for _ in range(6)] + [pltpu.VMEM((TM, D_HG), _F32) for _ in range(5)]
        + [pltpu.VMEM((nc_t * NH, HD, HD), _F32), pltpu.VMEM((8, D_HG), _F32)] + _sem_shapes(nsc),
        compiler_params=_cp(("arbitrary",)),
    )(p, p, hs, hs, o, sc, dy, wr, wi, vec, hb, g_hg, *scatter)


def _inproj_bwd_send(dp, w_in, h0, dh1, g_mix, u, order, gffn, gfin, loss, to_all):
    t_pad = dp.shape[0]
    rb = t_pad // (2 * N_DEV)
    n_steps = N_DEV + 2 * N_DEV
    na = len(to_all)

    def body(order_ref, dpc_ref, dpr_ref, u_ref, w_ref, h_ref, dh1_ref, g_ref, gffn_ref, gfin_ref, loss_ref, *rest):
        all_in = rest[:na]
        dh0_ref, recv_ref = rest[na:na + 2]
        all_out = rest[na + 2:2 * na + 2]
        alla_ref = rest[2 * na + 2]
        buf, pack, blk_send, blk_recv, blk_local = rest[2 * na + 3:2 * na + 8]
        exchange = _Exchange([], all_in, all_out, rest[2 * na + 8:2 * na + 11])
        last = _Exchange([], [pack], [alla_ref], rest[2 * na + 11:])
        s = pl.program_id(0)
        x, y, c = _coords()
        me = 4 * x + 2 * y + c

        def send(step):
            r = _SEND_ORDER[step]
            return pltpu.make_async_remote_copy(
                src_ref=buf.at[step], dst_ref=recv_ref.at[me], send_sem=blk_send.at[step], recv_sem=blk_recv.at[r - 1],
                device_id=(x ^ (r >> 2), y ^ ((r >> 1) & 1), c ^ (r & 1)), device_id_type=_MESH)

        @pl.when(s == 0)
        def _():
            exchange.start()
            pack[...] = jnp.zeros_like(pack)

        @pl.when(s < N_DEV)
        def _():
            buf[s] = _dot_tn(u_ref[...], dpc_ref[...]).astype(_BF)

            for step in range(N_DEV - 1):
                @pl.when(s == step)
                def _(step=step):
                    send(step).start()

        @pl.when(s >= N_DEV)
        def _():
            du = jnp.zeros((rb, D), _F32)
            for j in range(N_DEV):
                du = du + _dot_nt(dpr_ref[:, WIN_B * j:WIN_B * (j + 1)], w_ref[j])
            n, r = _rms_fwd(h_ref[...])
            pack[R_GMIX:R_GMIX + 1, :] += jnp.sum(du * n, axis=0, keepdims=True)
            dh0 = dh1_ref[...] + _rms_bwd(du * g_ref[...], n, r)
            dh0_ref[...] = dh0

            @pl.when(s == N_DEV)
            def _():
                pack[R_META:R_META + N_META, :] = dh0[0:N_META, :]

        @pl.when(s == n_steps - 1)
        def _():
            pack[R_GFFN:R_GFFN + 1, :] = gffn_ref[...]
            pack[R_GFIN:R_GFIN + 1, :] = gfin_ref[...]
            pack[R_LOSS:R_LOSS + 1, pl.ds(0, 128)] = loss_ref[0:1, :]
            last.start()
            mine = pltpu.make_async_copy(buf.at[N_DEV - 1], recv_ref.at[me], blk_local.at[0])
            mine.start()
            for step in range(N_DEV - 1):
                send(step).wait_send()
            for r in range(1, N_DEV):
                px, py, pc = x ^ (r >> 2), y ^ ((r >> 1) & 1), c ^ (r & 1)
                pltpu.make_async_remote_copy(
                    src_ref=buf.at[0], dst_ref=recv_ref.at[4 * px + 2 * py + pc], send_sem=blk_send.at[0],
                    recv_sem=blk_recv.at[r - 1], device_id=(px, py, pc), device_id_type=_MESH).wait_recv()
            mine.wait()
            exchange.finish()
            last.finish()

    hbm = pl.BlockSpec(memory_space=pl.ANY)
    rows = pl.BlockSpec((rb, D), lambda s, order: (jnp.maximum(s - N_DEV, 0), 0))
    one = pl.BlockSpec((1, D), lambda s, order: (0, 0))
    res = pl.pallas_call(
        body, name="inproj_bwd_send",
        grid_spec=pltpu.PrefetchScalarGridSpec(
            num_scalar_prefetch=1, grid=(n_steps,),
            in_specs=[pl.BlockSpec((t_pad, WIN_B), lambda s, order: (0, order[jnp.minimum(s, N_DEV - 1)])),
                      pl.BlockSpec((rb, D_IN), lambda s, order: (jnp.maximum(s - N_DEV, 0), 0)),
                      pl.BlockSpec((t_pad, D), lambda s, order: (0, 0), pipeline_mode=pl.Buffered(1)),
                      pl.BlockSpec((N_DEV, D, WIN_B), lambda s, order: (0, 0, 0), pipeline_mode=pl.Buffered(1)),
                      rows, rows, one, one, one, pl.BlockSpec((8, 128), lambda s, order: (0, 0))] + [hbm] * na,
            out_specs=[rows] + [hbm] * (na + 2),
            scratch_shapes=[pltpu.VMEM((N_DEV, D, WIN_B), _BF), pltpu.VMEM((24, D), _F32),
                            pltpu.SemaphoreType.DMA((N_DEV - 1,)), pltpu.SemaphoreType.DMA((N_DEV - 1,)),
                            pltpu.SemaphoreType.DMA((1,))] + _sem_shapes(na) + _sem_shapes(1)),
        out_shape=[_S((t_pad, D), _F32), _S((N_DEV, D, WIN_B), _BF)]
        + [_S((N_DEV,) + g.shape, g.dtype) for g in to_all] + [_S((N_DEV, 24, D), _F32)],
        compiler_params=_cp(("arbitrary",)),
    )(order, dp, dp, u, w_in, h0, dh1, g_mix, gffn, gfin, loss, *to_all)
    return res


def _wgrad(name, a, b, a_spec, b_spec, n_blocks, out_block, scatter=()):
    nsc = len(scatter)

    def body(a_ref, b_ref, *rest):
        o_ref = rest[nsc]
        j = pl.program_id(0)
        if nsc:
            exchange = _Exchange(rest[:nsc], [], rest[nsc + 1:2 * nsc + 1], rest[2 * nsc + 1:])

            @pl.when(j == 0)
            def _():
                exchange.start()

        av = a_ref[0] if len(a_ref.shape) == 3 else a_ref[...]
        bv = b_ref[0] if len(b_ref.shape) == 3 else b_ref[...]
        o_ref[0] = _dot_tn(av, bv).astype(_BF)

        if nsc:
            @pl.when(j == n_blocks - 1)
            def _():
                exchange.finish()

    hbm = pl.BlockSpec(memory_space=pl.ANY)
    res = pl.pallas_call(
        body, name=name, grid=(n_blocks,),
        in_specs=[a_spec, b_spec] + [hbm] * nsc,
        out_specs=[pl.BlockSpec((1,) + out_block, lambda j: (j, 0, 0))] + [hbm] * nsc,
        out_shape=[_S((n_blocks,) + out_block, _BF)] + [_S(s.shape, s.dtype) for s in scatter],
        scratch_shapes=_sem_shapes(nsc) if nsc else [],
        compiler_params=_cp(("arbitrary",)),
    )(a, b, *scatter)
    return res if nsc else res[0]


def _coords():
    return lax.axis_index("x"), lax.axis_index("y"), lax.axis_index("c")


def _sem_shapes(na):
    return [pltpu.SemaphoreType.DMA((7 * na,)), pltpu.SemaphoreType.DMA((7 * na,)), pltpu.SemaphoreType.DMA((na,))]


class _Gather:
    def __init__(self, srcs, outs, sems):
        self.srcs, self.outs = srcs, outs
        self.send_sems, self.recv_sems, self.local_sems = sems
        self.na = len(srcs)
        x, y, c = _coords()
        self.pos = (x, y, c)
        self.me = 4 * x + 2 * y + c
        self.sibling = (x, y, 1 - c)
        self.chips = [(1 - x, y), (x, 1 - y), (1 - x, 1 - y)]

    @staticmethod
    def _slot(px, py, pc):
        return 4 * px + 2 * py + pc

    def _copy(self, a, k, block, to, own=False):
        return pltpu.make_async_remote_copy(
            src_ref=self.srcs[a] if own else self.outs[a].at[block], dst_ref=self.outs[a].at[block],
            send_sem=self.send_sems.at[7 * a + k], recv_sem=self.recv_sems.at[7 * a + k],
            device_id=to, device_id_type=_MESH)

    def _mine(self, a):
        return pltpu.make_async_copy(self.srcs[a], self.outs[a].at[self.me], self.local_sems.at[a])

    def _first(self):
        c = self.pos[2]
        cps = []
        for a in range(self.na):
            cps.append(self._copy(a, 0, self.me, self.sibling, own=True))
            cps += [self._copy(a, 1 + j, self.me, (*chip, c), own=True) for j, chip in enumerate(self.chips)]
        return cps

    def _passed(self):
        c = self.pos[2]
        return [self._copy(a, 4 + j, self._slot(*chip, c), self.sibling)
                for j, chip in enumerate(self.chips) for a in range(self.na)]

    def start(self):
        for a in range(self.na):
            self._mine(a).start()
        for cp in self._first():
            cp.start()

    def forward(self, j):
        c = self.pos[2]
        chip = self.chips[j]
        for a in range(self.na):
            self._copy(a, 1 + j, self._slot(*chip, c), self.pos).wait_recv()
            self._copy(a, 4 + j, self._slot(*chip, c), self.sibling).start()

    def wait_sibling(self):
        x, y, c = self.pos
        for a in range(self.na):
            self._copy(a, 0, self._slot(x, y, 1 - c), self.pos).wait_recv()

    def wait_passed(self, j):
        c = self.pos[2]
        for a in range(self.na):
            self._copy(a, 4 + j, self._slot(*self.chips[j], 1 - c), self.pos).wait_recv()

    def finish_sends(self):
        for cp in self._first() + self._passed():
            cp.wait_send()
        for a in range(self.na):
            self._mine(a).wait()

    def finish(self):
        self.wait_sibling()
        for j in range(3):
            self.wait_passed(j)
        self.finish_sends()


class _Exchange:
    def __init__(self, scatter, gather, outs, sems):
        self.ins = list(scatter) + list(gather)
        self.ns, self.na = len(scatter), len(scatter) + len(gather)
        self.outs = outs
        self.send_sems, self.recv_sems, self.local_sems = sems
        x, y, c = _coords()
        self.pos = (x, y, c)
        self.me = 4 * x + 2 * y + c

    def _peer(self, r):
        x, y, c = self.pos
        return x ^ (r >> 2), y ^ ((r >> 1) & 1), c ^ (r & 1)

    def _src(self, a, block):
        return self.ins[a].at[block] if a < self.ns else self.ins[a]

    def _local(self, a):
        return pltpu.make_async_copy(self._src(a, self.me), self.outs[a].at[self.me], self.local_sems.at[a])

    def _send(self, a, r):
        px, py, pc = self._peer(r)
        return pltpu.make_async_remote_copy(
            src_ref=self._src(a, 4 * px + 2 * py + pc), dst_ref=self.outs[a].at[self.me],
            send_sem=self.send_sems.at[7 * a + r - 1], recv_sem=self.recv_sems.at[7 * a + r - 1],
            device_id=(px, py, pc), device_id_type=_MESH)

    def _recv(self, a, r):
        px, py, pc = self._peer(r)
        return pltpu.make_async_remote_copy(
            src_ref=self._src(a, self.me), dst_ref=self.outs[a].at[4 * px + 2 * py + pc],
            send_sem=self.send_sems.at[7 * a + r - 1], recv_sem=self.recv_sems.at[7 * a + r - 1],
            device_id=(px, py, pc), device_id_type=_MESH)

    def start(self):
        for a in range(self.na):
            self._local(a).start()
        for r in range(1, N_DEV):
            for a in range(self.na):
                self._send(a, r).start()

    def finish(self):
        for r in range(1, N_DEV):
            for a in range(self.na):
                self._recv(a, r).wait_recv()
        for r in range(1, N_DEV):
            for a in range(self.na):
                self._send(a, r).wait_send()
        for a in range(self.na):
            self._local(a).wait()


def _allgather_first(gather_f32, cast_f32, gather_dtypes):
    ng, nc = len(gather_f32), len(cast_f32)

    def body(*refs):
        ins, cins = refs[:ng], refs[ng:ng + nc]
        outs, couts = refs[ng + nc:2 * ng + nc], refs[2 * ng + nc:2 * ng + 2 * nc]
        stage = refs[2 * ng + 2 * nc:3 * ng + 2 * nc]
        sems = refs[3 * ng + 2 * nc:]
        for a in range(ng):
            stage[a][...] = ins[a][...].astype(gather_dtypes[a])
        g = _Gather(stage, outs, sems)
        g.start()
        for a in range(nc):
            couts[a][...] = cins[a][...].astype(_BF)
        for j in range(3):
            g.forward(j)
        g.finish()

    vm = pl.BlockSpec(memory_space=pltpu.VMEM)
    return pl.pallas_call(
        body, name="allgather_first",
        in_specs=[vm] * (ng + nc),
        out_specs=[pl.BlockSpec(memory_space=pl.ANY)] * ng + [vm] * nc,
        out_shape=[_S((N_DEV,) + l.shape, dt) for l, dt in zip(gather_f32, gather_dtypes)]
        + [_S(l.shape, _BF) for l in cast_f32],
        scratch_shapes=[pltpu.VMEM(l.shape, dt) for l, dt in zip(gather_f32, gather_dtypes)] + _sem_shapes(ng),
        compiler_params=pltpu.CompilerParams(vmem_limit_bytes=VMEM_LIMIT),
    )(*gather_f32, *cast_f32)


def _adamw_math(w, g, m, v):
    m2 = ADAM_B1 * m + (1.0 - ADAM_B1) * g
    v2 = ADAM_B2 * v + (1.0 - ADAM_B2) * (g * g)
    m_hat = m2 / (1.0 - ADAM_B1 ** ADAM_STEP)
    v_hat = v2 / (1.0 - ADAM_B2 ** ADAM_STEP)
    delta = -ADAM_LR * (m_hat / (jnp.sqrt(v_hat) + ADAM_EPS) + ADAM_WD * w)
    return delta, m2, v2


def _adamw_big(name, recv, w, m, v, rows):
    r_all, c_all = w.shape

    def body(r_ref, w_ref, m_ref, v_ref, g_out, d_out, m_out, v_out):
        g = r_ref[0].astype(_F32)
        for k in range(1, N_DEV):
            g = g + r_ref[k].astype(_F32)
        delta, m2, v2 = _adamw_math(w_ref[...], g, m_ref[...], v_ref[...])
        g_out[...] = g
        d_out[...] = delta
        m_out[...] = m2
        v_out[...] = v2

    tile = pl.BlockSpec((rows, c_all), lambda i: (i, 0))
    return pl.pallas_call(
        body, name=name, grid=(r_all // rows,),
        in_specs=[pl.BlockSpec((N_DEV, rows, c_all), lambda i: (0, i, 0)), tile, tile, tile],
        out_specs=[tile] * 4,
        out_shape=[_S(w.shape, _F32)] * 4,
        compiler_params=_cp(("arbitrary",)),
    )(recv, w, m, v)


def _adamw_small(gathered, slices, wmv):
    ng, npar = len(gathered), len(slices)

    def body(*refs):
        g_refs = refs[:ng]
        wmv_refs = refs[ng:ng + 3 * npar]
        outs = refs[ng + 3 * npar:]
        for i, (ai, r0, nr, c0, ncol) in enumerate(slices):
            g = g_refs[ai][0, pl.ds(r0, nr), pl.ds(c0, ncol)].astype(_F32)
            for k in range(1, N_DEV):
                g = g + g_refs[ai][k, pl.ds(r0, nr), pl.ds(c0, ncol)].astype(_F32)
            w_ref, m_ref, v_ref = wmv_refs[3 * i:3 * i + 3]
            delta, m2, v2 = _adamw_math(w_ref[...], g, m_ref[...], v_ref[...])
            outs[4 * i][...] = g
            outs[4 * i + 1][...] = delta
            outs[4 * i + 2][...] = m2
            outs[4 * i + 3][...] = v2
        total = g_refs[0][0, pl.ds(R_LOSS, 1), pl.ds(0, 128)]
        for k in range(1, N_DEV):
            total = total + g_refs[0][k, pl.ds(R_LOSS, 1), pl.ds(0, 128)]
        outs[4 * npar][...] = total

    flat = [t for trip in wmv for t in trip]
    out_shape = []
    for w, _, _ in wmv:
        out_shape += [_S(w.shape, _F32)] * 4
    out_shape.append(_S((1, 128), _F32))
    return pl.pallas_call(
        body, name="adamw_small", out_shape=out_shape,
        compiler_params=pltpu.CompilerParams(vmem_limit_bytes=VMEM_LIMIT),
    )(*gathered, *flat)


def _block_diag(w):
    eye = jnp.eye(8, dtype=w.dtype)
    return (w[:, :, None, :] * eye[:, None, :, None]).reshape(D_RG, D_RG)


def _diag_blocks(g):
    return jnp.concatenate([g[64 * h:64 * (h + 1), 64 * h:64 * (h + 1)] for h in range(8)], axis=0)


def _local_step(x, tgt, meta, g_mix, w_in_l, vec, wr, wi, hb, g_hg, w_out_l, g_ffn, w_gu_l, w_down_l, g_fin):
    seq = x.shape[0]
    n_valid = N_META + seq
    t_pad = -(-n_valid // TM) * TM
    h0 = jnp.concatenate([meta, x, jnp.zeros((t_pad - n_valid, D), _F32)], axis=0)
    tgt_p = jnp.concatenate([jnp.zeros((N_META, D), _F32), tgt, jnp.zeros((t_pad - n_valid, D), _F32)], axis=0)

    me = 4 * lax.axis_index("x") + 2 * lax.axis_index("y") + lax.axis_index("c")
    p, u, w_in, w_out = _inproj(h0, g_mix, (me ^ jnp.array(_GATHER_REL, jnp.int32)).astype(jnp.int32), w_in_l,
                                [w_out_l])
    y, hs, o, sc, w_gu, w_down = _mixer_fwd(p, wr, wi, vec, hb, g_hg, [w_gu_l, w_down_l])
    w_out = w_out.reshape(D, D)
    w_down = w_down.reshape(4, FFB, D)
    h1, v = _outproj(h0, y, w_out, g_ffn)
    gu, act, dh2, dh2b, loss, gfin = _ffn_loss(v, h1, w_gu, w_down, g_fin, tgt_p, n_valid)

    dgu, dh1, dh1b, dy, gffn = _ffn_bwd(dh2, dh2b, gu, h1, g_ffn, w_gu, w_down, w_out)
    g_wdown = _wgrad("wgrad_down", act, dh2b, pl.BlockSpec((1, t_pad, FFB), lambda j: (j, 0, 0)),
                     pl.BlockSpec((t_pad, D), lambda j: (0, 0)), 4, (FFB, D))
    g_wgu, r_wdown = _wgrad("wgrad_gate_up", dgu, v, pl.BlockSpec((1, t_pad, FFB), lambda j: (j, 0, 0)),
                            pl.BlockSpec((t_pad, D), lambda j: (0, 0)), N_DEV, (FFB, D),
                            scatter=[g_wdown.reshape(N_DEV, D_FF // N_DEV, D)])
    g_wout = _wgrad("wgrad_out", y, dh1b, pl.BlockSpec((t_pad, D // N_DEV), lambda j: (0, j)),
                    pl.BlockSpec((t_pad, D), lambda j: (0, 0)), N_DEV, (D // N_DEV, D))
    dp, gvec, gw, r_wgu, r_wout = _mixer_bwd(p, hs, o, sc, dy, wr, wi, vec, hb, g_hg, [g_wgu, g_wout])
    pack_c = jnp.concatenate([_diag_blocks(gw[0]), _diag_blocks(gw[1])], axis=1).astype(_BF)
    order = (me ^ jnp.array(_SEND_ORDER, jnp.int32)).astype(jnp.int32)
    dh0, r_win, all_b, all_c, all_a = _inproj_bwd_send(dp, w_in, h0, dh1, g_mix, u, order, gffn, gfin, loss,
                                                       [gvec, pack_c])
    return dh0, (r_win, r_wgu, r_wout, r_wdown), (all_a, all_b, all_c)


def kernel(x, meta_tokens, mix_norm_g, w_in, conv_w, conv_b, w_rgate, b_rgate, w_igate, b_igate, lru_lambda, rg_norm_g, hg_lower_bound, hg_norm_g, w_out, ffn_norm_g, w_gate_up, w_down, final_norm_g, loss_target, m_meta_tokens, m_mix_norm_g, m_w_in, m_conv_w, m_conv_b, m_w_rgate, m_b_rgate, m_w_igate, m_b_igate, m_lru_lambda, m_rg_norm_g, m_hg_lower_bound, m_hg_norm_g, m_w_out, m_ffn_norm_g, m_w_gate_up, m_w_down, m_final_norm_g, v_meta_tokens, v_mix_norm_g, v_w_in, v_conv_w, v_conv_b, v_w_rgate, v_b_rgate, v_w_igate, v_b_igate, v_lru_lambda, v_rg_norm_g, v_hg_lower_bound, v_hg_norm_g, v_w_out, v_ffn_norm_g, v_w_gate_up, v_w_down, v_final_norm_g):
    seq = x.shape[1]
    me = 4 * lax.axis_index("x") + 2 * lax.axis_index("y") + lax.axis_index("c")

    small_l = jnp.concatenate([meta_tokens, jnp.pad(conv_w[0], ((0, 4), (0, 64)))], axis=0)
    small_g, w_in_l, w_gu_l, w_out_l, w_down_l = _allgather_first(
        [small_l], [w_in[0], w_gate_up[0].T, w_out[0], w_down[0]], [_F32])
    meta_full = jnp.transpose(small_g[:, :N_META, :], (1, 0, 2)).reshape(N_META, D)
    conv_w_full = jnp.transpose(small_g[:, N_META:N_META + 4, :64], (1, 0, 2)).reshape(4, D_RG)
    vec = jnp.concatenate([conv_b, b_rgate, b_igate, lru_lambda, rg_norm_g, jnp.zeros((3, D_RG), _F32),
                           conv_w_full, jnp.zeros((4, D_RG), _F32)], axis=0)
    wr = _block_diag(w_rgate[0]).astype(_BF)
    wi = _block_diag(w_igate[0]).astype(_BF)

    dh0, (r_win, r_wgu, r_wout, r_wdown), (all_a, all_b, all_c) = _local_step(
        x[0], loss_target[0], meta_full, mix_norm_g, w_in_l, vec, wr, wi, hg_lower_bound, hg_norm_g,
        w_out_l, ffn_norm_g, w_gu_l, w_down_l, final_norm_g.reshape(1, D))
    grad_x = dh0[N_META:N_META + seq][None]

    outs = {}
    outs["w_in"] = _adamw_big("adamw_w_in", r_win, w_in[0], m_w_in[0], v_w_in[0], 256)
    outs["w_gate_up"] = [r.T for r in _adamw_big("adamw_w_gate_up", r_wgu, w_gate_up[0].T, m_w_gate_up[0].T,
                                                 v_w_gate_up[0].T, 176)]
    outs["w_out"] = _adamw_big("adamw_w_out", r_wout, w_out[0], m_w_out[0], v_w_out[0], 128)
    outs["w_down"] = _adamw_big("adamw_w_down", r_wdown, w_down[0], m_w_down[0], v_w_down[0], 176)

    meta_part = lax.dynamic_slice_in_dim(all_a[:, R_META:R_META + N_META, :], me * 128, 128, axis=2)
    convw_part = lax.dynamic_slice_in_dim(all_b[:, R_CONVW:R_CONVW + 4, :], me * 64, 64, axis=2)
    gathered = [all_a, all_b, all_c, meta_part, convw_part]
    small_params = [
        ("meta_tokens", (3, 0, N_META, 0, 128), (meta_tokens, m_meta_tokens, v_meta_tokens), (N_META, 128)),
        ("mix_norm_g", (0, R_GMIX, 1, 0, D), (mix_norm_g, m_mix_norm_g, v_mix_norm_g), (1, D)),
        ("conv_w", (4, 0, 4, 0, 64), (conv_w, m_conv_w, v_conv_w), (4, 64)),
        ("conv_b", (1, R_CONVB, 1, 0, D_RG), (conv_b, m_conv_b, v_conv_b), (1, D_RG)),
        ("w_rgate", (2, 0, 512, 0, 64), (w_rgate, m_w_rgate, v_w_rgate), (512, 64)),
        ("b_rgate", (1, R_BR, 1, 0, D_RG), (b_rgate, m_b_rgate, v_b_rgate), (1, D_RG)),
        ("w_igate", (2, 0, 512, 64, 64), (w_igate, m_w_igate, v_w_igate), (512, 64)),
        ("b_igate", (1, R_BI, 1, 0, D_RG), (b_igate, m_b_igate, v_b_igate), (1, D_RG)),
        ("lru_lambda", (1, R_LAM, 1, 0, D_RG), (lru_lambda, m_lru_lambda, v_lru_lambda), (1, D_RG)),
        ("rg_norm_g", (1, R_GRG, 1, 0, D_RG), (rg_norm_g, m_rg_norm_g, v_rg_norm_g), (1, D_RG)),
        ("hg_lower_bound", (1, R_HB0, 2, 0, D_HG), (hg_lower_bound, m_hg_lower_bound, v_hg_lower_bound), (2, D_HG)),
        ("hg_norm_g", (1, R_GHG, 1, 0, HD), (hg_norm_g, m_hg_norm_g, v_hg_norm_g), (1, HD)),
        ("ffn_norm_g", (0, R_GFFN, 1, 0, D), (ffn_norm_g, m_ffn_norm_g, v_ffn_norm_g), (1, D)),
        ("final_norm_g", (0, R_GFIN, 1, 0, D), (final_norm_g, m_final_norm_g, v_final_norm_g), (1, D)),
    ]
    res = _adamw_small(gathered, [s[1] for s in small_params],
                       [tuple(t.reshape(s[3]) for t in s[2]) for s in small_params])
    for i, s in enumerate(small_params):
        outs[s[0]] = [r.reshape(s[2][0].shape) for r in res[4 * i:4 * i + 4]]
    for n, ref in (("w_in", w_in), ("w_gate_up", w_gate_up), ("w_out", w_out), ("w_down", w_down)):
        outs[n] = [r.reshape(ref.shape) for r in outs[n]]

    loss_all = res[4 * len(small_params)][0, 0]
    order = ["meta_tokens", "mix_norm_g", "w_in", "conv_w", "conv_b", "w_rgate", "b_rgate", "w_igate", "b_igate",
             "lru_lambda", "rg_norm_g", "hg_lower_bound", "hg_norm_g", "w_out", "ffn_norm_g", "w_gate_up", "w_down",
             "final_norm_g"]
    return (loss_all, grad_x, *[outs[n][0] for n in order], *[outs[n][1] for n in order],
            *[outs[n][2] for n in order], *[outs[n][3] for n in order])
```

```python
import functools

import jax
import jax.numpy as jnp
from jax import lax
from jax.experimental import pallas as pl
from jax.experimental.pallas import tpu as pltpu

_BF = jnp.bfloat16
_F32 = jnp.float32
_S = jax.ShapeDtypeStruct
_MESH = pl.DeviceIdType.MESH

N_DEV = 8
N_META = 16
D = 1024
D_RG = 512
D_HG = 512
HD = 128
NH = D_HG // HD
D_IN = 3072
D_FF = 2816
FFB = D_FF // 4
WIN_B = D_IN // N_DEV
EPS = 1e-6
LRU_C = 8.0
TM = 256
HC = 64
VMEM_LIMIT = 56 * 1024 * 1024

ADAM_LR = 0.001
ADAM_B1 = 0.9
ADAM_B2 = 0.999
ADAM_EPS = 1e-08
ADAM_WD = 0.01
ADAM_STEP = 10

_SEND_ORDER = (6, 4, 2, 7, 5, 3, 1, 0)

_GATHER_REL = (0, 1, 4, 2, 6, 5, 3, 7)

R_CONVB, R_BR, R_BI, R_LAM, R_GRG, R_HB0, R_HB1, R_GHG, R_CONVW = 0, 1, 2, 3, 4, 5, 6, 7, 8
R_GMIX, R_GFFN, R_GFIN, R_LOSS, R_META = 0, 1, 2, 3, 8


def _cp(sem=None, **kw):
    return pltpu.CompilerParams(dimension_semantics=sem, vmem_limit_bytes=VMEM_LIMIT, **kw)


def _dot(a, b):
    return jnp.dot(a, b, preferred_element_type=_F32)


def _dot_nt(a, b):
    return lax.dot_general(a, b, (((1,), (1,)), ((), ())), preferred_element_type=_F32)


def _dot_tn(a, b):
    return lax.dot_general(a, b, (((0,), (0,)), ((), ())), preferred_element_type=_F32)


def _sigmoid(x):
    return jax.nn.sigmoid(x)


def _dsilu(x, s):
    return s * (1.0 + x * (1.0 - s))


_GELU_C = 0.7978845608028654


def _gelu_parts(x):
    t = jnp.tanh(_GELU_C * (x + 0.044715 * (x * x * x)))
    g = 0.5 * x * (1.0 + t)
    dg = 0.5 * (1.0 + t) + 0.5 * x * (1.0 - t * t) * (_GELU_C * (1.0 + 3.0 * 0.044715 * (x * x)))
    return g, dg


def _softplus(z):
    e = jnp.exp(-jnp.abs(z))
    w = 1.0 + e
    l1p = jnp.where(w == 1.0, e, jnp.log(w) * e / jnp.where(w == 1.0, 1.0, w - 1.0))
    return jnp.maximum(z, 0.0) + l1p


def _rms_fwd(x):
    r = lax.rsqrt(jnp.mean(x * x, axis=-1, keepdims=True) + EPS)
    return x * r, r


def _rms_bwd(dyg, n, r):
    return r * (dyg - n * jnp.mean(dyg * n, axis=-1, keepdims=True))


def _full(shape):
    nd = len(shape)
    return pl.BlockSpec(shape, lambda i: (0,) * nd)


def _const(shape):
    nd = len(shape)
    return pl.BlockSpec(shape, lambda i: (0,) * nd, pipeline_mode=pl.Buffered(1))


def _carry_gather(gather, i, nt):
    @pl.when(i == 0)
    def _():
        gather.start()

    def tail():
        for j in range(3):
            @pl.when(i == max(nt - 3 + j, 0))
            def _(j=j):
                gather.forward(j)

        @pl.when(i == nt - 1)
        def _():
            gather.finish()

    return tail


def _inproj(h0, g_mix, order, w_in_l, shards):
    t_pad = h0.shape[0]
    nt = 4
    tmi = t_pad // nt
    nsh = len(shards)

    def body(order_ref, h_ref, g_ref, wl_ref, *rest):
        sh_refs = rest[:nsh]
        p_ref, u_ref, wg_ref = rest[nsh:nsh + 3]
        gath_refs = rest[nsh + 3:2 * nsh + 3]
        u_s, wbuf, wsem = rest[2 * nsh + 3:2 * nsh + 6]
        g_w = _Gather([wl_ref], [wg_ref], rest[2 * nsh + 6:2 * nsh + 9])
        g_sh = _Gather(sh_refs, gath_refs, rest[2 * nsh + 9:])
        j, i = pl.program_id(0), pl.program_id(1)

        @pl.when((j == 0) & (i == 0))
        def _():
            g_w.start()
            g_sh.start()

        @pl.when(i == 0)
        def _():
            for step in range(N_DEV):
                @pl.when(j == step)
                def _(step=step):
                    if step == 0:
                        src = wl_ref
                    else:
                        if step == 1:
                            g_w.wait_sibling()
                        elif step <= 4:
                            g_w.forward(step - 2)
                        else:
                            g_w.wait_passed(step - 5)
                        src = wg_ref.at[order_ref[step]]
                    cp = pltpu.make_async_copy(src, wbuf, wsem.at[0])
                    cp.start()
                    cp.wait()

        @pl.when(j == 0)
        def _():
            n, _ = _rms_fwd(h_ref[...])
            u = (n * g_ref[...]).astype(_BF)
            u_s[i] = u
            u_ref[...] = u

        p_ref[...] = _dot(u_s[i], wbuf[...])

        @pl.when((j == N_DEV - 1) & (i == nt - 1))
        def _():
            g_w.finish_sends()
            for jj in range(3):
                g_sh.forward(jj)
            g_sh.finish()

    hbm = pl.BlockSpec(memory_space=pl.ANY)
    return pl.pallas_call(
        body, name="inproj",
        grid_spec=pltpu.PrefetchScalarGridSpec(
            num_scalar_prefetch=1, grid=(N_DEV, nt),
            in_specs=[pl.BlockSpec((tmi, D), lambda j, i, order: (jnp.where(j == 0, i, 0), 0)),
                      pl.BlockSpec((1, D), lambda j, i, order: (0, 0)), hbm] + [hbm] * nsh,
            out_specs=[pl.BlockSpec((tmi, WIN_B), lambda j, i, order: (i, order[j])),
                       pl.BlockSpec((tmi, D), lambda j, i, order: (jnp.where(j == 0, i, nt - 1), 0)), hbm] + [hbm] * nsh,
            scratch_shapes=[pltpu.VMEM((nt, tmi, D), _BF), pltpu.VMEM((D, WIN_B), _BF), pltpu.SemaphoreType.DMA((1,))]
            + _sem_shapes(1) + _sem_shapes(nsh)),
        out_shape=[_S((t_pad, D_IN), _F32), _S((t_pad, D), _BF), _S((N_DEV, D, WIN_B), _BF)]
        + [_S((N_DEV,) + s.shape, s.dtype) for s in shards],
        compiler_params=_cp(("arbitrary", "arbitrary")),
    )(order, h0, g_mix, w_in_l, *shards)


def _rg_gates(xc, wr_ref, wi_ref, vec_ref):
    xcb = xc.astype(_BF)
    r = _sigmoid(_dot(xcb, wr_ref[...]) + vec_ref[R_BR:R_BR + 1, :])
    ig = _sigmoid(_dot(xcb, wi_ref[...]) + vec_ref[R_BI:R_BI + 1, :])
    nsp8 = -LRU_C * _softplus(-vec_ref[R_LAM:R_LAM + 1, :])
    la = nsp8 * r
    a = jnp.exp(la)
    th = jnp.tanh(la)
    s = jnp.sqrt(-2.0 * th / (1.0 - th))
    return r, ig, a, s, nsp8


def _conv(xbuf, vec_ref):
    acc = vec_ref[R_CONVW:R_CONVW + 1, :] * xbuf[pl.ds(5, TM), :]
    for j in range(1, 4):
        acc = acc + vec_ref[R_CONVW + j:R_CONVW + j + 1, :] * xbuf[pl.ds(5 + j, TM), :]
    return vec_ref[R_CONVB:R_CONVB + 1, :] + acc


def _dot3(m01, x):
    hi = x.astype(_BF)
    r1 = x - hi.astype(_F32)
    mid = r1.astype(_BF)
    lo = (r1 - mid.astype(_F32)).astype(_BF)
    return (_dot(m01, lo) + _dot(m01, mid)) + _dot(m01, hi)


def _chunk_masks():
    row = lax.broadcasted_iota(jnp.int32, (TM, TM), 0)
    col = lax.broadcasted_iota(jnp.int32, (TM, TM), 1)
    shift = HC.bit_length() - 1
    same = lax.shift_right_logical(row, shift) == lax.shift_right_logical(col, shift)
    return same, same & (row >= col), same & (col >= row)


def _per_chunk_rows(x, r):
    return jnp.concatenate([jnp.broadcast_to(x[HC * c + r:HC * c + r + 1, :], (HC, x.shape[1]))
                            for c in range(TM // HC)], axis=0)


def _hg_prep(p_ref, lb, tri_blk):
    hq = p_ref[:, pl.ds(2 * D_RG, D_HG)]
    hf = p_ref[:, pl.ds(2 * D_RG + D_HG, D_HG)]
    sq = _sigmoid(hq)
    q = hq * sq
    sg = _sigmoid(hf)
    f = lb + (1.0 - lb) * sg
    k = 1.0 - f
    b = _dot3(tri_blk, jnp.log(f))
    bm = _per_chunk_rows(b, HC // 2 - 1)
    bl = _per_chunk_rows(b, HC - 1)
    e_q = jnp.exp(b - bm)
    e_k = jnp.exp(bm - b)
    e_b = jnp.exp(b)
    e_l = jnp.exp(bl - b)
    return dict(hq=hq, sq=sq, q=q, sg=sg, f=f, k=k, e_q=e_q, e_k=e_k, e_b=e_b, e_l=e_l,
                qd=q * e_q, kd=k * e_k, qe=q * e_b, ke=k * e_l, e_end=jnp.exp(bl))


def _mixer_fwd(p, wr, wi, vec, hb, g_hg, shards):
    t_pad = p.shape[0]
    nt = t_pad // TM
    nc_t = TM // HC
    nsh = len(shards)

    def body(p_ref, wr_ref, wi_ref, vec_ref, hb_ref, ghg_ref, *rest):
        sh_refs, rest = rest[:nsh], rest[nsh:]
        y_ref, hs_ref, o_ref, sc_ref = rest[:4]
        gath_refs, rest = rest[4:4 + nsh], rest[4 + nsh:]
        xbuf, a_s, b_s, hcar, st, qd_s, kd_s, qe_s, ke_s, v_s, u_s = rest[:11]
        i = pl.program_id(0)
        tail = _carry_gather(_Gather(sh_refs, gath_refs, rest[11:]), i, nt)

        @pl.when(i == 0)
        def _():
            xbuf[pl.ds(0, 8), :] = jnp.zeros((8, D_RG), _F32)
            hcar[...] = jnp.zeros_like(hcar)
            st[...] = jnp.zeros_like(st)

        x = p_ref[:, pl.ds(0, D_RG)]
        xbuf[pl.ds(8, TM), :] = x
        xc = _conv(xbuf, vec_ref)
        xbuf[pl.ds(0, 8), :] = x[TM - 8:, :]
        r, ig, a, s, _ = _rg_gates(xc, wr_ref, wi_ref, vec_ref)
        a_s[...] = a
        b_s[...] = s * (ig * xc)

        def step(t, h):
            h = a_s[pl.ds(t, 1), :] * h + b_s[pl.ds(t, 1), :]
            hs_ref[pl.ds(t, 1), :] = h
            return h

        hcar[pl.ds(0, 1), :] = lax.fori_loop(0, TM, step, hcar[pl.ds(0, 1), :], unroll=8)
        gel, _ = _gelu_parts(p_ref[:, pl.ds(D_RG, D_RG)])
        n, _ = _rms_fwd(gel * hs_ref[...])
        y_ref[:, pl.ds(0, D_RG)] = (n * vec_ref[R_GRG:R_GRG + 1, :]).astype(_BF)

        lb = _sigmoid(hb_ref[0:1, :] - hb_ref[1:2, :])
        _, tri_blk, _ = _chunk_masks()
        q = _hg_prep(p_ref, lb, tri_blk.astype(_BF))
        for name, ref in (("qd", qd_s), ("kd", kd_s), ("qe", qe_s), ("ke", ke_s)):
            ref[...] = q[name].astype(_BF)
        v_s[...] = p_ref[:, pl.ds(2 * D_RG + 2 * D_HG, D_HG)].astype(_BF)
        e_end = q["e_end"]
        causal = (lax.broadcasted_iota(jnp.int32, (HC, HC), 0) >= lax.broadcasted_iota(jnp.int32, (HC, HC), 1))
        for c in range(nc_t):
            for h in range(NH):
                rs, cs = pl.ds(HC * c, HC), pl.ds(HD * h, HD)
                amat = jnp.where(causal, _dot_nt(qd_s[rs, cs], kd_s[rs, cs]), 0.0)
                o_ref[rs, cs] = _dot(amat.astype(_BF), v_s[rs, cs])
                u_s[NH * c + h] = _dot_tn(v_s[rs, cs], ke_s[rs, cs])
        for h in range(NH):
            cs = pl.ds(HD * h, HD)
            s_run = st[h]
            for c in range(nc_t):
                rs = pl.ds(HC * c, HC)
                sc_ref[c, h] = s_run
                o_ref[rs, cs] += _dot_nt(qe_s[rs, cs], s_run.astype(_BF))
                s_run = e_end[HC * c:HC * c + 1, HD * h:HD * (h + 1)] * s_run + u_s[NH * c + h]
            st[h] = s_run
        for h in range(NH):
            cs = pl.ds(HD * h, HD)
            n_o, _ = _rms_fwd(o_ref[:, cs])
            hg = p_ref[:, pl.ds(2 * D_RG + 3 * D_HG + HD * h, HD)]
            y_ref[:, pl.ds(D_RG + HD * h, HD)] = ((n_o * ghg_ref[...]) * (hg * _sigmoid(hg))).astype(_BF)

        tail()

    hbm = pl.BlockSpec(memory_space=pl.ANY)
    return pl.pallas_call(
        body, name="mixer_fwd", grid=(nt,),
        in_specs=[pl.BlockSpec((TM, D_IN), lambda i: (i, 0)), _full((D_RG, D_RG)), _full((D_RG, D_RG)),
                  _full((16, D_RG)), _full((2, D_HG)), _full((1, HD))] + [hbm] * nsh,
        out_specs=[pl.BlockSpec((TM, D), lambda i: (i, 0)), pl.BlockSpec((TM, D_RG), lambda i: (i, 0)),
                   pl.BlockSpec((TM, D_HG), lambda i: (i, 0)),
                   pl.BlockSpec((nc_t, NH, HD, HD), lambda i: (i, 0, 0, 0))] + [hbm] * nsh,
        out_shape=[_S((t_pad, D), _BF), _S((t_pad, D_RG), _F32), _S((t_pad, D_HG), _F32),
                   _S((t_pad // HC, NH, HD, HD), _F32)] + [_S((N_DEV,) + s.shape, s.dtype) for s in shards],
        scratch_shapes=[pltpu.VMEM((TM + 8, D_RG), _F32), pltpu.VMEM((TM, D_RG), _F32),
                        pltpu.VMEM((TM, D_RG), _F32), pltpu.VMEM((8, D_RG), _F32),
                        pltpu.VMEM((NH, HD, HD), _F32)] + [pltpu.VMEM((TM, D_HG), _BF) for _ in range(5)]
        + [pltpu.VMEM((nc_t * NH, HD, HD), _F32)] + _sem_shapes(nsh),
        compiler_params=_cp(("arbitrary",)),
    )(p, wr, wi, vec, hb, g_hg, *shards)


def _outproj(h0, y, w_out, g_ffn):
    t_pad = h0.shape[0]

    def body(h_ref, y_ref, w_ref, g_ref, h1_ref, v_ref):
        h1 = h_ref[...] + _dot(y_ref[...], w_ref[...])
        h1_ref[...] = h1
        n, _ = _rms_fwd(h1)
        v_ref[...] = (n * g_ref[...]).astype(_BF)

    return pl.pallas_call(
        body, name="outproj", grid=(t_pad // TM,),
        in_specs=[pl.BlockSpec((TM, D), lambda i: (i, 0)), pl.BlockSpec((TM, D), lambda i: (i, 0)),
                  _full((D, D)), _full((1, D))],
        out_specs=[pl.BlockSpec((TM, D), lambda i: (i, 0)), pl.BlockSpec((TM, D), lambda i: (i, 0))],
        out_shape=[_S((t_pad, D), _F32), _S((t_pad, D), _BF)],
        compiler_params=_cp(("arbitrary",)),
    )(h0, y, w_out, g_ffn)


def _ffn_loss(v, h1, w_gu, w_down, g_fin, tgt, n_valid):
    t_pad = v.shape[0]

    def body(v_ref, h1_ref, wgu_ref, wd_ref, g_ref, t_ref, gu_ref, act_ref, dh2_ref, dh2b_ref, loss_ref, gfin_ref):
        i = pl.program_id(0)

        @pl.when(i == 0)
        def _():
            loss_ref[...] = jnp.zeros_like(loss_ref)
            gfin_ref[...] = jnp.zeros_like(gfin_ref)

        vb = v_ref[...]
        h2 = h1_ref[...]
        for b in range(4):
            gate = _dot_nt(vb, wgu_ref[b])
            up = _dot_nt(vb, wgu_ref[4 + b])
            gu_ref[b] = gate
            gu_ref[4 + b] = up
            act = ((gate * _sigmoid(gate)) * up).astype(_BF)
            act_ref[b] = act
            h2 = h2 + _dot(act, wd_ref[b])
        n, r = _rms_fwd(h2)
        out = n * g_ref[...]
        row = i * TM + lax.broadcasted_iota(jnp.int32, (TM, 1), 0)
        valid = (row >= N_META) & (row < n_valid)
        err = jnp.where(valid, out - t_ref[...], 0.0)
        loss_ref[...] += (0.5 / D) * jnp.sum(err * err)
        dout = err * (1.0 / D)
        gfin_ref[...] += jnp.sum(dout * n, axis=0, keepdims=True)
        dh2 = _rms_bwd(dout * g_ref[...], n, r)
        dh2_ref[...] = dh2
        dh2b_ref[...] = dh2.astype(_BF)

    return pl.pallas_call(
        body, name="ffn_loss", grid=(t_pad // TM,),
        in_specs=[pl.BlockSpec((TM, D), lambda i: (i, 0)), pl.BlockSpec((TM, D), lambda i: (i, 0)),
                  _const((N_DEV, FFB, D)), _const((4, FFB, D)), _full((1, D)),
                  pl.BlockSpec((TM, D), lambda i: (i, 0))],
        out_specs=[pl.BlockSpec((N_DEV, TM, FFB), lambda i: (0, i, 0)), pl.BlockSpec((4, TM, FFB), lambda i: (0, i, 0)),
                   pl.BlockSpec((TM, D), lambda i: (i, 0)), pl.BlockSpec((TM, D), lambda i: (i, 0)),
                   _full((8, 128)), _full((1, D))],
        out_shape=[_S((N_DEV, t_pad, FFB), _F32), _S((4, t_pad, FFB), _BF), _S((t_pad, D), _F32),
                   _S((t_pad, D), _BF), _S((8, 128), _F32), _S((1, D), _F32)],
        compiler_params=_cp(("arbitrary",)),
    )(v, h1, w_gu, w_down, g_fin, tgt)


def _ffn_bwd(dh2, dh2b, gu, h1, g_ffn, w_gu, w_down, w_out):
    t_pad = dh2.shape[0]

    def body(dh2_ref, dh2b_ref, gu_ref, h1_ref, g_ref, wgu_ref, wd_ref, wo_ref,
             dgu_ref, dh1_ref, dh1b_ref, dy_ref, gffn_ref):
        i = pl.program_id(0)

        @pl.when(i == 0)
        def _():
            gffn_ref[...] = jnp.zeros_like(gffn_ref)

        db = dh2b_ref[...]
        dv = jnp.zeros((TM, D), _F32)
        for b in range(4):
            dact = _dot_nt(db, wd_ref[b])
            gate = gu_ref[b]
            up = gu_ref[4 + b]
            sg = _sigmoid(gate)
            dgate = ((dact * up) * _dsilu(gate, sg)).astype(_BF)
            dup = (dact * (gate * sg)).astype(_BF)
            dgu_ref[b] = dgate
            dgu_ref[4 + b] = dup
            dv = dv + _dot(dgate, wgu_ref[b]) + _dot(dup, wgu_ref[4 + b])
        n, r = _rms_fwd(h1_ref[...])
        gffn_ref[...] += jnp.sum(dv * n, axis=0, keepdims=True)
        dh1 = dh2_ref[...] + _rms_bwd(dv * g_ref[...], n, r)
        dh1_ref[...] = dh1
        dh1b = dh1.astype(_BF)
        dh1b_ref[...] = dh1b
        dy_ref[...] = _dot_nt(dh1b, wo_ref[...])

    tile = pl.BlockSpec((TM, D), lambda i: (i, 0))
    return pl.pallas_call(
        body, name="ffn_bwd", grid=(t_pad // TM,),
        in_specs=[tile, tile, pl.BlockSpec((N_DEV, TM, FFB), lambda i: (0, i, 0)), tile, _full((1, D)),
                  _const((N_DEV, FFB, D)), _const((4, FFB, D)), _const((D, D))],
        out_specs=[pl.BlockSpec((N_DEV, TM, FFB), lambda i: (0, i, 0)), tile, tile, tile, _full((1, D))],
        out_shape=[_S((N_DEV, t_pad, FFB), _BF), _S((t_pad, D), _F32), _S((t_pad, D), _BF),
                   _S((t_pad, D), _F32), _S((1, D), _F32)],
        compiler_params=_cp(("arbitrary",)),
    )(dh2, dh2b, gu, h1, g_ffn, w_gu, w_down, w_out)


def _mixer_bwd(p, hs, o, sc, dy, wr, wi, vec, hb, g_hg, scatter):
    t_pad = p.shape[0]
    nt = t_pad // TM
    nc_t = TM // HC
    nsc = len(scatter)

    def rev(i):
        return nt - 1 - i

    def body(p_ref, pprev_ref, hs_ref, hprev_ref, o_ref, sc_ref, dy_ref, wr_ref, wi_ref, vec_ref, hb_ref, ghg_ref,
             *rest):
        send_refs, rest = rest[:nsc], rest[nsc:]
        dp_ref, gvec_ref, gw_ref = rest[:3]
        recv_refs, rest = rest[3:3 + nsc], rest[3 + nsc:]
        xbuf, hbuf, dbuf, a_s, g_s, ccar, dst = rest[:7]
        qd_s, kd_s, qe_s, ke_s, v_s, do_s, dqd_s, dkd_s, dqe_s, dke_s, dv_s, w_s, dend_s = rest[7:20]
        exchange = _Exchange(send_refs, [], recv_refs, rest[20:])
        i = pl.program_id(0)
        first_tile = i == nt - 1

        @pl.when(i == 0)
        def _():
            exchange.start()
            gvec_ref[...] = jnp.zeros_like(gvec_ref)
            gw_ref[...] = jnp.zeros_like(gw_ref)
            dbuf[pl.ds(TM, 8), :] = jnp.zeros((8, D_RG), _F32)
            ccar[...] = jnp.zeros_like(ccar)
            dst[...] = jnp.zeros_like(dst)

        def acc(row, val):
            gvec_ref[row:row + 1, :] += jnp.sum(val, axis=0, keepdims=True)

        keep = jnp.where(first_tile, 0.0, 1.0)
        x = p_ref[:, pl.ds(0, D_RG)]
        xbuf[pl.ds(0, 8), :] = pprev_ref[...] * keep
        xbuf[pl.ds(8, TM), :] = x
        xc = _conv(xbuf, vec_ref)
        r, ig, a, s, nsp8 = _rg_gates(xc, wr_ref, wi_ref, vec_ref)
        h = hs_ref[...]
        hbuf[pl.ds(0, 8), :] = hprev_ref[...] * keep
        hbuf[pl.ds(8, TM), :] = h
        hm1 = hbuf[pl.ds(7, TM), :]
        gr = p_ref[:, pl.ds(D_RG, D_RG)]
        gel, dgel = _gelu_parts(gr)
        n, rr = _rms_fwd(gel * h)
        dyn = dy_ref[:, pl.ds(0, D_RG)]
        acc(R_GRG, dyn * n)
        dpre = _rms_bwd(dyn * vec_ref[R_GRG:R_GRG + 1, :], n, rr)
        dp_ref[:, pl.ds(D_RG, D_RG)] = ((dpre * h) * dgel).astype(_BF)
        a_s[...] = a
        g_s[...] = dpre * gel

        def step(k, c):
            t = TM - 1 - k
            g = g_s[pl.ds(t, 1), :] + c
            g_s[pl.ds(t, 1), :] = g
            return a_s[pl.ds(t, 1), :] * g

        ccar[pl.ds(0, 1), :] = lax.fori_loop(0, TM, step, ccar[pl.ds(0, 1), :], unroll=8)
        gt = g_s[...]
        da = gt * hm1
        ixc = ig * xc
        ds = gt * ixc
        dig = (gt * s) * xc
        dxc = (gt * s) * ig
        dla = da * a - ds * ((a * a) / s)
        lam = vec_ref[R_LAM:R_LAM + 1, :]
        gvec_ref[R_LAM:R_LAM + 1, :] += jnp.sum(dla * r, axis=0, keepdims=True) * (LRU_C * _sigmoid(-lam))
        dzr = (dla * nsp8) * (r * (1.0 - r))
        dzi = dig * (ig * (1.0 - ig))
        acc(R_BR, dzr)
        acc(R_BI, dzi)
        xcb = xc.astype(_BF)
        dzrb = dzr.astype(_BF)
        dzib = dzi.astype(_BF)
        gw_ref[0] += _dot_tn(xcb, dzrb)
        gw_ref[1] += _dot_tn(xcb, dzib)
        dxc = dxc + _dot_nt(dzrb, wr_ref[...]) + _dot_nt(dzib, wi_ref[...])
        acc(R_CONVB, dxc)
        for j in range(4):
            acc(R_CONVW + j, dxc * xbuf[pl.ds(5 + j, TM), :])
        dbuf[pl.ds(0, TM), :] = dxc
        dx = vec_ref[R_CONVW + 3:R_CONVW + 4, :] * dxc
        for j in range(3):
            dx = dx + vec_ref[R_CONVW + j:R_CONVW + j + 1, :] * dbuf[pl.ds(3 - j, TM), :]
        dbuf[pl.ds(TM, 8), :] = dxc[0:8, :]
        dp_ref[:, pl.ds(0, D_RG)] = dx.astype(_BF)

        lb = _sigmoid(hb_ref[0:1, :] - hb_ref[1:2, :])
        same, tri_blk, triu_blk = _chunk_masks()
        q = _hg_prep(p_ref, lb, tri_blk.astype(_BF))
        qdb, kdb = q["qd"].astype(_BF), q["kd"].astype(_BF)
        qd_s[...] = qdb
        kd_s[...] = kdb
        qe_s[...] = q["qe"].astype(_BF)
        ke_s[...] = q["ke"].astype(_BF)
        v_s[...] = p_ref[:, pl.ds(2 * D_RG + 2 * D_HG, D_HG)].astype(_BF)
        e_end = q["e_end"]
        ghg = ghg_ref[...]
        for h in range(NH):
            cs = pl.ds(HD * h, HD)
            hg = p_ref[:, pl.ds(2 * D_RG + 3 * D_HG + HD * h, HD)]
            sh = _sigmoid(hg)
            n_o, r_o = _rms_fwd(o_ref[:, cs])
            dyh = dy_ref[:, pl.ds(D_RG + HD * h, HD)]
            dp_ref[:, pl.ds(2 * D_RG + 3 * D_HG + HD * h, HD)] = ((dyh * (n_o * ghg)) * _dsilu(hg, sh)).astype(_BF)
            dn = dyh * (hg * sh)
            gvec_ref[R_GHG:R_GHG + 1, pl.ds(0, HD)] += jnp.sum(dn * n_o, axis=0, keepdims=True)
            do_s[:, cs] = _rms_bwd(dn * ghg, n_o, r_o).astype(_BF)
        causal = (lax.broadcasted_iota(jnp.int32, (HC, HC), 0) >= lax.broadcasted_iota(jnp.int32, (HC, HC), 1))
        for c in range(nc_t):
            for h in range(NH):
                rs, cs = pl.ds(HC * c, HC), pl.ds(HD * h, HD)
                qd_c, kd_c, do_c = qd_s[rs, cs], kd_s[rs, cs], do_s[rs, cs]
                amat = jnp.where(causal, _dot_nt(qd_c, kd_c), 0.0).astype(_BF)
                da_m = jnp.where(causal, _dot_nt(do_c, v_s[rs, cs]), 0.0).astype(_BF)
                dqd_s[rs, cs] = _dot(da_m, kd_c)
                dkd_s[rs, cs] = _dot_tn(da_m, qd_c)
                dqe_s[rs, cs] = _dot(do_c, sc_ref[c, h].astype(_BF))
                dv_s[rs, cs] = _dot_tn(amat, do_c)
                w_s[NH * c + h] = _dot_tn(do_c, qe_s[rs, cs])
        for h in range(NH):
            cs = pl.ds(HD * h, HD)
            d_run = dst[h]
            for c in reversed(range(nc_t)):
                rs = pl.ds(HC * c, HC)
                d_b = d_run.astype(_BF)
                dke_s[rs, cs] = _dot(v_s[rs, cs], d_b)
                dp_ref[rs, pl.ds(2 * D_RG + 2 * D_HG + HD * h, HD)] = (
                    dv_s[rs, cs] + _dot_nt(ke_s[rs, cs], d_b)).astype(_BF)
                dend_s[pl.ds(c, 1), cs] = jnp.sum(sc_ref[c, h] * d_run, axis=0, keepdims=True)
                d_run = w_s[NH * c + h] + e_end[HC * c:HC * c + 1, HD * h:HD * (h + 1)] * d_run
            dst[h] = d_run
        dqd, dkd, dqe, dke = dqd_s[...], dkd_s[...], dqe_s[...], dke_s[...]
        dq = dqd * q["e_q"] + dqe * q["e_b"]
        dk = dkd * q["e_k"] + dke * q["e_l"]
        dkeke = dke * q["ke"]
        db = dqd * qdb.astype(_F32) - dkd * kdb.astype(_F32) + dqe * q["qe"] - dkeke
        d_end = jnp.concatenate([jnp.broadcast_to(dend_s[pl.ds(c, 1), :], (HC, D_HG)) for c in range(nc_t)], axis=0)
        dlf = _dot3(triu_blk.astype(_BF), db) + _dot3(same.astype(_BF), dkeke) + d_end * e_end
        df = dlf / q["f"] - dk
        sg = q["sg"]
        gvec_ref[R_HB0:R_HB0 + 1, :] += jnp.sum(df * (1.0 - sg), axis=0, keepdims=True)
        dp_ref[:, pl.ds(2 * D_RG, D_HG)] = (dq * _dsilu(q["hq"], q["sq"])).astype(_BF)
        dp_ref[:, pl.ds(2 * D_RG + D_HG, D_HG)] = ((df * (1.0 - lb)) * (sg * (1.0 - sg))).astype(_BF)

        @pl.when(i == nt - 1)
        def _():
            glb = gvec_ref[R_HB0:R_HB0 + 1, :] * (lb * (1.0 - lb))
            gvec_ref[R_HB0:R_HB0 + 1, :] = glb
            gvec_ref[R_HB1:R_HB1 + 1, :] = -glb
            exchange.finish()

    hbm = pl.BlockSpec(memory_space=pl.ANY)
    return pl.pallas_call(
        body, name="mixer_bwd", grid=(nt,),
        in_specs=[pl.BlockSpec((TM, D_IN), lambda i: (rev(i), 0)),
                  pl.BlockSpec((8, D_RG), lambda i: (jnp.maximum(rev(i) * (TM // 8) - 1, 0), 0)),
                  pl.BlockSpec((TM, D_RG), lambda i: (rev(i), 0)),
                  pl.BlockSpec((8, D_RG), lambda i: (jnp.maximum(rev(i) * (TM // 8) - 1, 0), 0)),
                  pl.BlockSpec((TM, D_HG), lambda i: (rev(i), 0)),
                  pl.BlockSpec((nc_t, NH, HD, HD), lambda i: (rev(i), 0, 0, 0)),
                  pl.BlockSpec((TM, D), lambda i: (rev(i), 0)),
                  _full((D_RG, D_RG)), _full((D_RG, D_RG)), _full((16, D_RG)), _full((2, D_HG)), _full((1, HD))]
        + [hbm] * nsc,
        out_specs=[pl.BlockSpec((TM, D_IN), lambda i: (rev(i), 0)), _full((16, D_RG)), _full((2, D_RG, D_RG))]
        + [hbm] * nsc,
        out_shape=[_S((t_pad, D_IN), _BF), _S((16, D_RG), _F32), _S((2, D_RG, D_RG), _F32)]
        + [_S(s.shape, s.dtype) for s in scatter],
        scratch_shapes=[pltpu.VMEM((TM + 8, D_RG), _F32), pltpu.VMEM((TM + 8, D_RG), _F32),
                        pltpu.VMEM((TM + 8, D_RG), _F32), pltpu.VMEM((TM, D_RG), _F32),
                        pltpu.VMEM((TM, D_RG), _F32), pltpu.VMEM((8, D_RG), _F32),
                        pltpu.VMEM((NH, HD, HD), _F32)]
        + [pltpu.VMEM((TM, D_HG), _BF) for _ in range(6)] + [pltpu.VMEM((TM, D_HG), _F32) for _ in range(5)]
        + [pltpu.VMEM((nc_t * NH, HD, HD), _F32), pltpu.VMEM((8, D_HG), _F32)] + _sem_shapes(nsc),
        compiler_params=_cp(("arbitrary",)),
    )(p, p, hs, hs, o, sc, dy, wr, wi, vec, hb, g_hg, *scatter)


def _inproj_bwd_send(dp, w_in, h0, dh1, g_mix, u, order, gffn, gfin, loss, to_all):
    t_pad = dp.shape[0]
    rb = t_pad // (2 * N_DEV)
    n_steps = N_DEV + 2 * N_DEV
    na = len(to_all)

    def body(order_ref, dpc_ref, dpr_ref, u_ref, w_ref, h_ref, dh1_ref, g_ref, gffn_ref, gfin_ref, loss_ref, *rest):
        all_in = rest[:na]
        dh0_ref, recv_ref = rest[na:na + 2]
        all_out = rest[na + 2:2 * na + 2]
        alla_ref = rest[2 * na + 2]
        buf, pack, blk_send, blk_recv, blk_local = rest[2 * na + 3:2 * na + 8]
        exchange = _Exchange([], all_in, all_out, rest[2 * na + 8:2 * na + 11])
        last = _Exchange([], [pack], [alla_ref], rest[2 * na + 11:])
        s = pl.program_id(0)
        x, y, c = _coords()
        me = 4 * x + 2 * y + c

        def send(step):
            r = _SEND_ORDER[step]
            return pltpu.make_async_remote_copy(
                src_ref=buf.at[step], dst_ref=recv_ref.at[me], send_sem=blk_send.at[step], recv_sem=blk_recv.at[r - 1],
                device_id=(x ^ (r >> 2), y ^ ((r >> 1) & 1), c ^ (r & 1)), device_id_type=_MESH)

        @pl.when(s == 0)
        def _():
            exchange.start()
            pack[...] = jnp.zeros_like(pack)

        @pl.when(s < N_DEV)
        def _():
            buf[s] = _dot_tn(u_ref[...], dpc_ref[...]).astype(_BF)

            for step in range(N_DEV - 1):
                @pl.when(s == step)
                def _(step=step):
                    send(step).start()

        @pl.when(s >= N_DEV)
        def _():
            du = jnp.zeros((rb, D), _F32)
            for j in range(N_DEV):
                du = du + _dot_nt(dpr_ref[:, WIN_B * j:WIN_B * (j + 1)], w_ref[j])
            n, r = _rms_fwd(h_ref[...])
            pack[R_GMIX:R_GMIX + 1, :] += jnp.sum(du * n, axis=0, keepdims=True)
            dh0 = dh1_ref[...] + _rms_bwd(du * g_ref[...], n, r)
            dh0_ref[...] = dh0

            @pl.when(s == N_DEV)
            def _():
                pack[R_META:R_META + N_META, :] = dh0[0:N_META, :]

        @pl.when(s == n_steps - 1)
        def _():
            pack[R_GFFN:R_GFFN + 1, :] = gffn_ref[...]
            pack[R_GFIN:R_GFIN + 1, :] = gfin_ref[...]
            pack[R_LOSS:R_LOSS + 1, pl.ds(0, 128)] = loss_ref[0:1, :]
            last.start()
            mine = pltpu.make_async_copy(buf.at[N_DEV - 1], recv_ref.at[me], blk_local.at[0])
            mine.start()
            for step in range(N_DEV - 1):
                send(step).wait_send()
            for r in range(1, N_DEV):
                px, py, pc = x ^ (r >> 2), y ^ ((r >> 1) & 1), c ^ (r & 1)
                pltpu.make_async_remote_copy(
                    src_ref=buf.at[0], dst_ref=recv_ref.at[4 * px + 2 * py + pc], send_sem=blk_send.at[0],
                    recv_sem=blk_recv.at[r - 1], device_id=(px, py, pc), device_id_type=_MESH).wait_recv()
            mine.wait()
            exchange.finish()
            last.finish()

    hbm = pl.BlockSpec(memory_space=pl.ANY)
    rows = pl.BlockSpec((rb, D), lambda s, order: (jnp.maximum(s - N_DEV, 0), 0))
    one = pl.BlockSpec((1, D), lambda s, order: (0, 0))
    res = pl.pallas_call(
        body, name="inproj_bwd_send",
        grid_spec=pltpu.PrefetchScalarGridSpec(
            num_scalar_prefetch=1, grid=(n_steps,),
            in_specs=[pl.BlockSpec((t_pad, WIN_B), lambda s, order: (0, order[jnp.minimum(s, N_DEV - 1)])),
                      pl.BlockSpec((rb, D_IN), lambda s, order: (jnp.maximum(s - N_DEV, 0), 0)),
                      pl.BlockSpec((t_pad, D), lambda s, order: (0, 0), pipeline_mode=pl.Buffered(1)),
                      pl.BlockSpec((N_DEV, D, WIN_B), lambda s, order: (0, 0, 0), pipeline_mode=pl.Buffered(1)),
                      rows, rows, one, one, one, pl.BlockSpec((8, 128), lambda s, order: (0, 0))] + [hbm] * na,
            out_specs=[rows] + [hbm] * (na + 2),
            scratch_shapes=[pltpu.VMEM((N_DEV, D, WIN_B), _BF), pltpu.VMEM((24, D), _F32),
                            pltpu.SemaphoreType.DMA((N_DEV - 1,)), pltpu.SemaphoreType.DMA((N_DEV - 1,)),
                            pltpu.SemaphoreType.DMA((1,))] + _sem_shapes(na) + _sem_shapes(1)),
        out_shape=[_S((t_pad, D), _F32), _S((N_DEV, D, WIN_B), _BF)]
        + [_S((N_DEV,) + g.shape, g.dtype) for g in to_all] + [_S((N_DEV, 24, D), _F32)],
        compiler_params=_cp(("arbitrary",)),
    )(order, dp, dp, u, w_in, h0, dh1, g_mix, gffn, gfin, loss, *to_all)
    return res


def _wgrad(name, a, b, a_spec, b_spec, n_blocks, out_block, scatter=()):
    nsc = len(scatter)

    def body(a_ref, b_ref, *rest):
        o_ref = rest[nsc]
        j = pl.program_id(0)
        if nsc:
            exchange = _Exchange(rest[:nsc], [], rest[nsc + 1:2 * nsc + 1], rest[2 * nsc + 1:])

            @pl.when(j == 0)
            def _():
                exchange.start()

        av = a_ref[0] if len(a_ref.shape) == 3 else a_ref[...]
        bv = b_ref[0] if len(b_ref.shape) == 3 else b_ref[...]
        o_ref[0] = _dot_tn(av, bv).astype(_BF)

        if nsc:
            @pl.when(j == n_blocks - 1)
            def _():
                exchange.finish()

    hbm = pl.BlockSpec(memory_space=pl.ANY)
    res = pl.pallas_call(
        body, name=name, grid=(n_blocks,),
        in_specs=[a_spec, b_spec] + [hbm] * nsc,
        out_specs=[pl.BlockSpec((1,) + out_block, lambda j: (j, 0, 0))] + [hbm] * nsc,
        out_shape=[_S((n_blocks,) + out_block, _BF)] + [_S(s.shape, s.dtype) for s in scatter],
        scratch_shapes=_sem_shapes(nsc) if nsc else [],
        compiler_params=_cp(("arbitrary",)),
    )(a, b, *scatter)
    return res if nsc else res[0]


def _coords():
    return lax.axis_index("x"), lax.axis_index("y"), lax.axis_index("c")


def _sem_shapes(na):
    return [pltpu.SemaphoreType.DMA((7 * na,)), pltpu.SemaphoreType.DMA((7 * na,)), pltpu.SemaphoreType.DMA((na,))]


class _Gather:
    def __init__(self, srcs, outs, sems):
        self.srcs, self.outs = srcs, outs
        self.send_sems, self.recv_sems, self.local_sems = sems
        self.na = len(srcs)
        x, y, c = _coords()
        self.pos = (x, y, c)
        self.me = 4 * x + 2 * y + c
        self.sibling = (x, y, 1 - c)
        self.chips = [(1 - x, y), (x, 1 - y), (1 - x, 1 - y)]

    @staticmethod
    def _slot(px, py, pc):
        return 4 * px + 2 * py + pc

    def _copy(self, a, k, block, to, own=False):
        return pltpu.make_async_remote_copy(
            src_ref=self.srcs[a] if own else self.outs[a].at[block], dst_ref=self.outs[a].at[block],
            send_sem=self.send_sems.at[7 * a + k], recv_sem=self.recv_sems.at[7 * a + k],
            device_id=to, device_id_type=_MESH)

    def _mine(self, a):
        return pltpu.make_async_copy(self.srcs[a], self.outs[a].at[self.me], self.local_sems.at[a])

    def _first(self):
        c = self.pos[2]
        cps = []
        for a in range(self.na):
            cps.append(self._copy(a, 0, self.me, self.sibling, own=True))
            cps += [self._copy(a, 1 + j, self.me, (*chip, c), own=True) for j, chip in enumerate(self.chips)]
        return cps

    def _passed(self):
        c = self.pos[2]
        return [self._copy(a, 4 + j, self._slot(*chip, c), self.sibling)
                for j, chip in enumerate(self.chips) for a in range(self.na)]

    def start(self):
        for a in range(self.na):
            self._mine(a).start()
        for cp in self._first():
            cp.start()

    def forward(self, j):
        c = self.pos[2]
        chip = self.chips[j]
        for a in range(self.na):
            self._copy(a, 1 + j, self._slot(*chip, c), self.pos).wait_recv()
            self._copy(a, 4 + j, self._slot(*chip, c), self.sibling).start()

    def wait_sibling(self):
        x, y, c = self.pos
        for a in range(self.na):
            self._copy(a, 0, self._slot(x, y, 1 - c), self.pos).wait_recv()

    def wait_passed(self, j):
        c = self.pos[2]
        for a in range(self.na):
            self._copy(a, 4 + j, self._slot(*self.chips[j], 1 - c), self.pos).wait_recv()

    def finish_sends(self):
        for cp in self._first() + self._passed():
            cp.wait_send()
        for a in range(self.na):
            self._mine(a).wait()

    def finish(self):
        self.wait_sibling()
        for j in range(3):
            self.wait_passed(j)
        self.finish_sends()


class _Exchange:
    def __init__(self, scatter, gather, outs, sems):
        self.ins = list(scatter) + list(gather)
        self.ns, self.na = len(scatter), len(scatter) + len(gather)
        self.outs = outs
        self.send_sems, self.recv_sems, self.local_sems = sems
        x, y, c = _coords()
        self.pos = (x, y, c)
        self.me = 4 * x + 2 * y + c

    def _peer(self, r):
        x, y, c = self.pos
        return x ^ (r >> 2), y ^ ((r >> 1) & 1), c ^ (r & 1)

    def _src(self, a, block):
        return self.ins[a].at[block] if a < self.ns else self.ins[a]

    def _local(self, a):
        return pltpu.make_async_copy(self._src(a, self.me), self.outs[a].at[self.me], self.local_sems.at[a])

    def _send(self, a, r):
        px, py, pc = self._peer(r)
        return pltpu.make_async_remote_copy(
            src_ref=self._src(a, 4 * px + 2 * py + pc), dst_ref=self.outs[a].at[self.me],
            send_sem=self.send_sems.at[7 * a + r - 1], recv_sem=self.recv_sems.at[7 * a + r - 1],
            device_id=(px, py, pc), device_id_type=_MESH)

    def _recv(self, a, r):
        px, py, pc = self._peer(r)
        return pltpu.make_async_remote_copy(
            src_ref=self._src(a, self.me), dst_ref=self.outs[a].at[4 * px + 2 * py + pc],
            send_sem=self.send_sems.at[7 * a + r - 1], recv_sem=self.recv_sems.at[7 * a + r - 1],
            device_id=(px, py, pc), device_id_type=_MESH)

    def start(self):
        for a in range(self.na):
            self._local(a).start()
        for r in range(1, N_DEV):
            for a in range(self.na):
                self._send(a, r).start()

    def finish(self):
        for r in range(1, N_DEV):
            for a in range(self.na):
                self._recv(a, r).wait_recv()
        for r in range(1, N_DEV):
            for a in range(self.na):
                self._send(a, r).wait_send()
        for a in range(self.na):
            self._local(a).wait()


def _allgather_first(gather_f32, cast_f32, gather_dtypes):
    ng, nc = len(gather_f32), len(cast_f32)

    def body(*refs):
        ins, cins = refs[:ng], refs[ng:ng + nc]
        outs, couts = refs[ng + nc:2 * ng + nc], refs[2 * ng + nc:2 * ng + 2 * nc]
        stage = refs[2 * ng + 2 * nc:3 * ng + 2 * nc]
        sems = refs[3 * ng + 2 * nc:]
        for a in range(ng):
            stage[a][...] = ins[a][...].astype(gather_dtypes[a])
        g = _Gather(stage, outs, sems)
        g.start()
        for a in range(nc):
            couts[a][...] = cins[a][...].astype(_BF)
        for j in range(3):
            g.forward(j)
        g.finish()

    vm = pl.BlockSpec(memory_space=pltpu.VMEM)
    return pl.pallas_call(
        body, name="allgather_first",
        in_specs=[vm] * (ng + nc),
        out_specs=[pl.BlockSpec(memory_space=pl.ANY)] * ng + [vm] * nc,
        out_shape=[_S((N_DEV,) + l.shape, dt) for l, dt in zip(gather_f32, gather_dtypes)]
        + [_S(l.shape, _BF) for l in cast_f32],
        scratch_shapes=[pltpu.VMEM(l.shape, dt) for l, dt in zip(gather_f32, gather_dtypes)] + _sem_shapes(ng),
        compiler_params=pltpu.CompilerParams(vmem_limit_bytes=VMEM_LIMIT),
    )(*gather_f32, *cast_f32)


def _adamw_math(w, g, m, v):
    m2 = ADAM_B1 * m + (1.0 - ADAM_B1) * g
    v2 = ADAM_B2 * v + (1.0 - ADAM_B2) * (g * g)
    m_hat = m2 / (1.0 - ADAM_B1 ** ADAM_STEP)
    v_hat = v2 / (1.0 - ADAM_B2 ** ADAM_STEP)
    delta = -ADAM_LR * (m_hat / (jnp.sqrt(v_hat) + ADAM_EPS) + ADAM_WD * w)
    return delta, m2, v2


def _adamw_big(name, recv, w, m, v, rows):
    r_all, c_all = w.shape

    def body(r_ref, w_ref, m_ref, v_ref, g_out, d_out, m_out, v_out):
        g = r_ref[0].astype(_F32)
        for k in range(1, N_DEV):
            g = g + r_ref[k].astype(_F32)
        delta, m2, v2 = _adamw_math(w_ref[...], g, m_ref[...], v_ref[...])
        g_out[...] = g
        d_out[...] = delta
        m_out[...] = m2
        v_out[...] = v2

    tile = pl.BlockSpec((rows, c_all), lambda i: (i, 0))
    return pl.pallas_call(
        body, name=name, grid=(r_all // rows,),
        in_specs=[pl.BlockSpec((N_DEV, rows, c_all), lambda i: (0, i, 0)), tile, tile, tile],
        out_specs=[tile] * 4,
        out_shape=[_S(w.shape, _F32)] * 4,
        compiler_params=_cp(("arbitrary",)),
    )(recv, w, m, v)


def _adamw_small(gathered, slices, wmv):
    ng, npar = len(gathered), len(slices)

    def body(*refs):
        g_refs = refs[:ng]
        wmv_refs = refs[ng:ng + 3 * npar]
        outs = refs[ng + 3 * npar:]
        for i, (ai, r0, nr, c0, ncol) in enumerate(slices):
            g = g_refs[ai][0, pl.ds(r0, nr), pl.ds(c0, ncol)].astype(_F32)
            for k in range(1, N_DEV):
                g = g + g_refs[ai][k, pl.ds(r0, nr), pl.ds(c0, ncol)].astype(_F32)
            w_ref, m_ref, v_ref = wmv_refs[3 * i:3 * i + 3]
            delta, m2, v2 = _adamw_math(w_ref[...], g, m_ref[...], v_ref[...])
            outs[4 * i][...] = g
            outs[4 * i + 1][...] = delta
            outs[4 * i + 2][...] = m2
            outs[4 * i + 3][...] = v2
        total = g_refs[0][0, pl.ds(R_LOSS, 1), pl.ds(0, 128)]
        for k in range(1, N_DEV):
            total = total + g_refs[0][k, pl.ds(R_LOSS, 1), pl.ds(0, 128)]
        outs[4 * npar][...] = total

    flat = [t for trip in wmv for t in trip]
    out_shape = []
    for w, _, _ in wmv:
        out_shape += [_S(w.shape, _F32)] * 4
    out_shape.append(_S((1, 128), _F32))
    return pl.pallas_call(
        body, name="adamw_small", out_shape=out_shape,
        compiler_params=pltpu.CompilerParams(vmem_limit_bytes=VMEM_LIMIT),
    )(*gathered, *flat)


def _block_diag(w):
    eye = jnp.eye(8, dtype=w.dtype)
    return (w[:, :, None, :] * eye[:, None, :, None]).reshape(D_RG, D_RG)


def _diag_blocks(g):
    return jnp.concatenate([g[64 * h:64 * (h + 1), 64 * h:64 * (h + 1)] for h in range(8)], axis=0)


def _local_step(x, tgt, meta, g_mix, w_in_l, vec, wr, wi, hb, g_hg, w_out_l, g_ffn, w_gu_l, w_down_l, g_fin):
    seq = x.shape[0]
    n_valid = N_META + seq
    t_pad = -(-n_valid // TM) * TM
    h0 = jnp.concatenate([meta, x, jnp.zeros((t_pad - n_valid, D), _F32)], axis=0)
    tgt_p = jnp.concatenate([jnp.zeros((N_META, D), _F32), tgt, jnp.zeros((t_pad - n_valid, D), _F32)], axis=0)

    me = 4 * lax.axis_index("x") + 2 * lax.axis_index("y") + lax.axis_index("c")
    p, u, w_in, w_out = _inproj(h0, g_mix, (me ^ jnp.array(_GATHER_REL, jnp.int32)).astype(jnp.int32), w_in_l,
                                [w_out_l])
    y, hs, o, sc, w_gu, w_down = _mixer_fwd(p, wr, wi, vec, hb, g_hg, [w_gu_l, w_down_l])
    w_out = w_out.reshape(D, D)
    w_down = w_down.reshape(4, FFB, D)
    h1, v = _outproj(h0, y, w_out, g_ffn)
    gu, act, dh2, dh2b, loss, gfin = _ffn_loss(v, h1, w_gu, w_down, g_fin, tgt_p, n_valid)

    dgu, dh1, dh1b, dy, gffn = _ffn_bwd(dh2, dh2b, gu, h1, g_ffn, w_gu, w_down, w_out)
    g_wdown = _wgrad("wgrad_down", act, dh2b, pl.BlockSpec((1, t_pad, FFB), lambda j: (j, 0, 0)),
                     pl.BlockSpec((t_pad, D), lambda j: (0, 0)), 4, (FFB, D))
    g_wgu, r_wdown = _wgrad("wgrad_gate_up", dgu, v, pl.BlockSpec((1, t_pad, FFB), lambda j: (j, 0, 0)),
                            pl.BlockSpec((t_pad, D), lambda j: (0, 0)), N_DEV, (FFB, D),
                            scatter=[g_wdown.reshape(N_DEV, D_FF // N_DEV, D)])
    g_wout = _wgrad("wgrad_out", y, dh1b, pl.BlockSpec((t_pad, D // N_DEV), lambda j: (0, j)),
                    pl.BlockSpec((t_pad, D), lambda j: (0, 0)), N_DEV, (D // N_DEV, D))
    dp, gvec, gw, r_wgu, r_wout = _mixer_bwd(p, hs, o, sc, dy, wr, wi, vec, hb, g_hg, [g_wgu, g_wout])
    pack_c = jnp.concatenate([_diag_blocks(gw[0]), _diag_blocks(gw[1])], axis=1).astype(_BF)
    order = (me ^ jnp.array(_SEND_ORDER, jnp.int32)).astype(jnp.int32)
    dh0, r_win, all_b, all_c, all_a = _inproj_bwd_send(dp, w_in, h0, dh1, g_mix, u, order, gffn, gfin, loss,
                                                       [gvec, pack_c])
    return dh0, (r_win, r_wgu, r_wout, r_wdown), (all_a, all_b, all_c)


def kernel(x, meta_tokens, mix_norm_g, w_in, conv_w, conv_b, w_rgate, b_rgate, w_igate, b_igate, lru_lambda, rg_norm_g, hg_lower_bound, hg_norm_g, w_out, ffn_norm_g, w_gate_up, w_down, final_norm_g, loss_target, m_meta_tokens, m_mix_norm_g, m_w_in, m_conv_w, m_conv_b, m_w_rgate, m_b_rgate, m_w_igate, m_b_igate, m_lru_lambda, m_rg_norm_g, m_hg_lower_bound, m_hg_norm_g, m_w_out, m_ffn_norm_g, m_w_gate_up, m_w_down, m_final_norm_g, v_meta_tokens, v_mix_norm_g, v_w_in, v_conv_w, v_conv_b, v_w_rgate, v_b_rgate, v_w_igate, v_b_igate, v_lru_lambda, v_rg_norm_g, v_hg_lower_bound, v_hg_norm_g, v_w_out, v_ffn_norm_g, v_w_gate_up, v_w_down, v_final_norm_g):
    seq = x.shape[1]
    me = 4 * lax.axis_index("x") + 2 * lax.axis_index("y") + lax.axis_index("c")

    small_l = jnp.concatenate([meta_tokens, jnp.pad(conv_w[0], ((0, 4), (0, 64)))], axis=0)
    small_g, w_in_l, w_gu_l, w_out_l, w_down_l = _allgather_first(
        [small_l], [w_in[0], w_gate_up[0].T, w_out[0], w_down[0]], [_F32])
    meta_full = jnp.transpose(small_g[:, :N_META, :], (1, 0, 2)).reshape(N_META, D)
    conv_w_full = jnp.transpose(small_g[:, N_META:N_META + 4, :64], (1, 0, 2)).reshape(4, D_RG)
    vec = jnp.concatenate([conv_b, b_rgate, b_igate, lru_lambda, rg_norm_g, jnp.zeros((3, D_RG), _F32),
                           conv_w_full, jnp.zeros((4, D_RG), _F32)], axis=0)
    wr = _block_diag(w_rgate[0]).astype(_BF)
    wi = _block_diag(w_igate[0]).astype(_BF)

    dh0, (r_win, r_wgu, r_wout, r_wdown), (all_a, all_b, all_c) = _local_step(
        x[0], loss_target[0], meta_full, mix_norm_g, w_in_l, vec, wr, wi, hg_lower_bound, hg_norm_g,
        w_out_l, ffn_norm_g, w_gu_l, w_down_l, final_norm_g.reshape(1, D))
    grad_x = dh0[N_META:N_META + seq][None]

    outs = {}
    outs["w_in"] = _adamw_big("adamw_w_in", r_win, w_in[0], m_w_in[0], v_w_in[0], 256)
    outs["w_gate_up"] = [r.T for r in _adamw_big("adamw_w_gate_up", r_wgu, w_gate_up[0].T, m_w_gate_up[0].T,
                                                 v_w_gate_up[0].T, 176)]
    outs["w_out"] = _adamw_big("adamw_w_out", r_wout, w_out[0], m_w_out[0], v_w_out[0], 128)
    outs["w_down"] = _adamw_big("adamw_w_down", r_wdown, w_down[0], m_w_down[0], v_w_down[0], 176)

    meta_part = lax.dynamic_slice_in_dim(all_a[:, R_META:R_META + N_META, :], me * 128, 128, axis=2)
    convw_part = lax.dynamic_slice_in_dim(all_b[:, R_CONVW:R_CONVW + 4, :], me * 64, 64, axis=2)
    gathered = [all_a, all_b, all_c, meta_part, convw_part]
    small_params = [
        ("meta_tokens", (3, 0, N_META, 0, 128), (meta_tokens, m_meta_tokens, v_meta_tokens), (N_META, 128)),
        ("mix_norm_g", (0, R_GMIX, 1, 0, D), (mix_norm_g, m_mix_norm_g, v_mix_norm_g), (1, D)),
        ("conv_w", (4, 0, 4, 0, 64), (conv_w, m_conv_w, v_conv_w), (4, 64)),
        ("conv_b", (1, R_CONVB, 1, 0, D_RG), (conv_b, m_conv_b, v_conv_b), (1, D_RG)),
        ("w_rgate", (2, 0, 512, 0, 64), (w_rgate, m_w_rgate, v_w_rgate), (512, 64)),
        ("b_rgate", (1, R_BR, 1, 0, D_RG), (b_rgate, m_b_rgate, v_b_rgate), (1, D_RG)),
        ("w_igate", (2, 0, 512, 64, 64), (w_igate, m_w_igate, v_w_igate), (512, 64)),
        ("b_igate", (1, R_BI, 1, 0, D_RG), (b_igate, m_b_igate, v_b_igate), (1, D_RG)),
        ("lru_lambda", (1, R_LAM, 1, 0, D_RG), (lru_lambda, m_lru_lambda, v_lru_lambda), (1, D_RG)),
        ("rg_norm_g", (1, R_GRG, 1, 0, D_RG), (rg_norm_g, m_rg_norm_g, v_rg_norm_g), (1, D_RG)),
        ("hg_lower_bound", (1, R_HB0, 2, 0, D_HG), (hg_lower_bound, m_hg_lower_bound, v_hg_lower_bound), (2, D_HG)),
        ("hg_norm_g", (1, R_GHG, 1, 0, HD), (hg_norm_g, m_hg_norm_g, v_hg_norm_g), (1, HD)),
        ("ffn_norm_g", (0, R_GFFN, 1, 0, D), (ffn_norm_g, m_ffn_norm_g, v_ffn_norm_g), (1, D)),
        ("final_norm_g", (0, R_GFIN, 1, 0, D), (final_norm_g, m_final_norm_g, v_final_norm_g), (1, D)),
    ]
    res = _adamw_small(gathered, [s[1] for s in small_params],
                       [tuple(t.reshape(s[3]) for t in s[2]) for s in small_params])
    for i, s in enumerate(small_params):
        outs[s[0]] = [r.reshape(s[2][0].shape) for r in res[4 * i:4 * i + 4]]
    for n, ref in (("w_in", w_in), ("w_gate_up", w_gate_up), ("w_out", w_out), ("w_down", w_down)):
        outs[n] = [r.reshape(ref.shape) for r in outs[n]]

    loss_all = res[4 * len(small_params)][0, 0]
    order = ["meta_tokens", "mix_norm_g", "w_in", "conv_w", "conv_b", "w_rgate", "b_rgate", "w_igate", "b_igate",
             "lru_lambda", "rg_norm_g", "hg_lower_bound", "hg_norm_g", "w_out", "ffn_norm_g", "w_gate_up", "w_down",
             "final_norm_g"]
    return (loss_all, grad_x, *[outs[n][0] for n in order], *[outs[n][1] for n in order],
            *[outs[n][2] for n in order], *[outs[n][3] for n in order])
```

```python
import functools

import jax
import jax.numpy as jnp
from jax import lax
from jax.experimental import pallas as pl
from jax.experimental.pallas import tpu as pltpu

_BF = jnp.bfloat16
_F32 = jnp.float32
_S = jax.ShapeDtypeStruct
_MESH = pl.DeviceIdType.MESH

N_DEV = 8
N_META = 16
D = 1024
D_RG = 512
D_HG = 512
HD = 128
NH = D_HG // HD
D_IN = 3072
D_FF = 2816
FFB = D_FF // 4
WIN_B = D_IN // N_DEV
WIN_P = 2 * WIN_B
EPS = 1e-6
LRU_C = 8.0
TM = 256
HC = 64
VMEM_LIMIT = 56 * 1024 * 1024

ADAM_LR = 0.001
ADAM_B1 = 0.9
ADAM_B2 = 0.999
ADAM_EPS = 1e-08
ADAM_WD = 0.01
ADAM_STEP = 10

_SEND_ORDER = (6, 4, 2, 7, 5, 3, 1, 0)

_CHIP_ORDER = (0, 2, 1, 3)

R_CONVB, R_BR, R_BI, R_LAM, R_GRG, R_HB0, R_HB1, R_GHG, R_CONVW = 0, 1, 2, 3, 4, 5, 6, 7, 8
R_GMIX, R_GFFN, R_GFIN, R_LOSS, R_META = 0, 1, 2, 3, 8


def _cp(sem=None, **kw):
    return pltpu.CompilerParams(dimension_semantics=sem, vmem_limit_bytes=VMEM_LIMIT, **kw)


def _dot(a, b):
    return jnp.dot(a, b, preferred_element_type=_F32)


def _dot_nt(a, b):
    return lax.dot_general(a, b, (((1,), (1,)), ((), ())), preferred_element_type=_F32)


def _dot_tn(a, b):
    return lax.dot_general(a, b, (((0,), (0,)), ((), ())), preferred_element_type=_F32)


def _sigmoid(x):
    return jax.nn.sigmoid(x)


def _dsilu(x, s):
    return s * (1.0 + x * (1.0 - s))


_GELU_C = 0.7978845608028654


def _gelu_parts(x):
    t = jnp.tanh(_GELU_C * (x + 0.044715 * (x * x * x)))
    g = 0.5 * x * (1.0 + t)
    dg = 0.5 * (1.0 + t) + 0.5 * x * (1.0 - t * t) * (_GELU_C * (1.0 + 3.0 * 0.044715 * (x * x)))
    return g, dg


def _softplus(z):
    e = jnp.exp(-jnp.abs(z))
    w = 1.0 + e
    l1p = jnp.where(w == 1.0, e, jnp.log(w) * e / jnp.where(w == 1.0, 1.0, w - 1.0))
    return jnp.maximum(z, 0.0) + l1p


def _rms_fwd(x):
    r = lax.rsqrt(jnp.mean(x * x, axis=-1, keepdims=True) + EPS)
    return x * r, r


def _rms_bwd(dyg, n, r):
    return r * (dyg - n * jnp.mean(dyg * n, axis=-1, keepdims=True))


def _full(shape):
    nd = len(shape)
    return pl.BlockSpec(shape, lambda i: (0,) * nd)


def _const(shape):
    nd = len(shape)
    return pl.BlockSpec(shape, lambda i: (0,) * nd, pipeline_mode=pl.Buffered(1))


def _carry_gather(gather, i, nt):
    @pl.when(i == 0)
    def _():
        gather.start()

    def tail():
        for j in range(3):
            @pl.when(i == max(nt - 3 + j, 0))
            def _(j=j):
                gather.forward(j)

        @pl.when(i == nt - 1)
        def _():
            gather.finish()

    return tail


def _pair_place(ref, block):
    return ref.at[block // 2, :, pl.ds(pl.multiple_of((block % 2) * WIN_B, WIN_B), WIN_B)]


def _inproj(h0, g_mix, order, w_in_l, shards):
    t_pad = h0.shape[0]
    nt = 4
    tmi = t_pad // nt
    nsh = len(shards)

    def body(order_ref, h_ref, g_ref, wl_ref, *rest):
        sh_refs = rest[:nsh]
        p_ref, u_ref, wg_ref = rest[nsh:nsh + 3]
        gath_refs = rest[nsh + 3:2 * nsh + 3]
        u_s, wbuf, wsem = rest[2 * nsh + 3:2 * nsh + 6]
        g_w = _Gather([wl_ref], [wg_ref], rest[2 * nsh + 6:2 * nsh + 9], place=_pair_place)
        g_sh = _Gather(sh_refs, gath_refs, rest[2 * nsh + 9:])
        j, i = pl.program_id(0), pl.program_id(1)

        @pl.when((j == 0) & (i == 0))
        def _():
            g_w.start()
            g_sh.start()

        @pl.when(i == 0)
        def _():
            for step in range(4):
                @pl.when(j == step)
                def _(step=step):
                    if step == 0:
                        g_w.wait_mine()
                        g_w.wait_sibling()
                    else:
                        g_w.forward(step - 1)
                        g_w.wait_passed(step - 1)
                    cp = pltpu.make_async_copy(wg_ref.at[order_ref[step]], wbuf, wsem.at[0])
                    cp.start()
                    cp.wait()

        @pl.when(j == 0)
        def _():
            n, _ = _rms_fwd(h_ref[...])
            u = (n * g_ref[...]).astype(_BF)
            u_s[i] = u
            u_ref[...] = u

        p_ref[...] = _dot(u_s[i], wbuf[...])

        @pl.when((j == 3) & (i == nt - 1))
        def _():
            g_w.finish_sends(mine=False)
            for jj in range(3):
                g_sh.forward(jj)
            g_sh.finish()

    hbm = pl.BlockSpec(memory_space=pl.ANY)
    return pl.pallas_call(
        body, name="inproj",
        grid_spec=pltpu.PrefetchScalarGridSpec(
            num_scalar_prefetch=1, grid=(4, nt),
            in_specs=[pl.BlockSpec((tmi, D), lambda j, i, order: (jnp.where(j == 0, i, 0), 0)),
                      pl.BlockSpec((1, D), lambda j, i, order: (0, 0)), hbm] + [hbm] * nsh,
            out_specs=[pl.BlockSpec((tmi, WIN_P), lambda j, i, order: (i, order[j])),
                       pl.BlockSpec((tmi, D), lambda j, i, order: (jnp.where(j == 0, i, nt - 1), 0)), hbm] + [hbm] * nsh,
            scratch_shapes=[pltpu.VMEM((nt, tmi, D), _BF), pltpu.VMEM((D, WIN_P), _BF), pltpu.SemaphoreType.DMA((1,))]
            + _sem_shapes(1) + _sem_shapes(nsh)),
        out_shape=[_S((t_pad, D_IN), _F32), _S((t_pad, D), _BF), _S((4, D, WIN_P), _BF)]
        + [_S((N_DEV,) + s.shape, s.dtype) for s in shards],
        compiler_params=_cp(("arbitrary", "arbitrary")),
    )(order, h0, g_mix, w_in_l, *shards)


def _rg_gates(xc, wr_ref, wi_ref, vec_ref):
    xcb = xc.astype(_BF)
    r = _sigmoid(_dot(xcb, wr_ref[...]) + vec_ref[R_BR:R_BR + 1, :])
    ig = _sigmoid(_dot(xcb, wi_ref[...]) + vec_ref[R_BI:R_BI + 1, :])
    nsp8 = -LRU_C * _softplus(-vec_ref[R_LAM:R_LAM + 1, :])
    la = nsp8 * r
    a = jnp.exp(la)
    th = jnp.tanh(la)
    s = jnp.sqrt(-2.0 * th / (1.0 - th))
    return r, ig, a, s, nsp8


def _conv(xbuf, vec_ref):
    acc = vec_ref[R_CONVW:R_CONVW + 1, :] * xbuf[pl.ds(5, TM), :]
    for j in range(1, 4):
        acc = acc + vec_ref[R_CONVW + j:R_CONVW + j + 1, :] * xbuf[pl.ds(5 + j, TM), :]
    return vec_ref[R_CONVB:R_CONVB + 1, :] + acc


def _dot3(m01, x):
    hi = x.astype(_BF)
    r1 = x - hi.astype(_F32)
    mid = r1.astype(_BF)
    lo = (r1 - mid.astype(_F32)).astype(_BF)
    return (_dot(m01, lo) + _dot(m01, mid)) + _dot(m01, hi)


def _chunk_masks():
    row = lax.broadcasted_iota(jnp.int32, (TM, TM), 0)
    col = lax.broadcasted_iota(jnp.int32, (TM, TM), 1)
    shift = HC.bit_length() - 1
    same = lax.shift_right_logical(row, shift) == lax.shift_right_logical(col, shift)
    return same, same & (row >= col), same & (col >= row)


def _per_chunk_rows(x, r):
    return jnp.concatenate([jnp.broadcast_to(x[HC * c + r:HC * c + r + 1, :], (HC, x.shape[1]))
                            for c in range(TM // HC)], axis=0)


def _hg_prep(p_ref, lb, tri_blk):
    hq = p_ref[:, pl.ds(2 * D_RG, D_HG)]
    hf = p_ref[:, pl.ds(2 * D_RG + D_HG, D_HG)]
    sq = _sigmoid(hq)
    q = hq * sq
    sg = _sigmoid(hf)
    f = lb + (1.0 - lb) * sg
    k = 1.0 - f
    b = _dot3(tri_blk, jnp.log(f))
    bm = _per_chunk_rows(b, HC // 2 - 1)
    bl = _per_chunk_rows(b, HC - 1)
    e_q = jnp.exp(b - bm)
    e_k = jnp.exp(bm - b)
    e_b = jnp.exp(b)
    e_l = jnp.exp(bl - b)
    return dict(hq=hq, sq=sq, q=q, sg=sg, f=f, k=k, e_q=e_q, e_k=e_k, e_b=e_b, e_l=e_l,
                qd=q * e_q, kd=k * e_k, qe=q * e_b, ke=k * e_l, e_end=jnp.exp(bl))


def _mixer_fwd(p, wr, wi, vec, hb, g_hg, shards):
    t_pad = p.shape[0]
    nt = t_pad // TM
    nc_t = TM // HC
    nsh = len(shards)

    def body(p_ref, wr_ref, wi_ref, vec_ref, hb_ref, ghg_ref, *rest):
        sh_refs, rest = rest[:nsh], rest[nsh:]
        y_ref, hs_ref, o_ref, sc_ref = rest[:4]
        gath_refs, rest = rest[4:4 + nsh], rest[4 + nsh:]
        xbuf, a_s, b_s, hcar, st, qd_s, kd_s, qe_s, ke_s, v_s, u_s = rest[:11]
        i = pl.program_id(0)
        tail = _carry_gather(_Gather(sh_refs, gath_refs, rest[11:]), i, nt)

        @pl.when(i == 0)
        def _():
            xbuf[pl.ds(0, 8), :] = jnp.zeros((8, D_RG), _F32)
            hcar[...] = jnp.zeros_like(hcar)
            st[...] = jnp.zeros_like(st)

        x = p_ref[:, pl.ds(0, D_RG)]
        xbuf[pl.ds(8, TM), :] = x
        xc = _conv(xbuf, vec_ref)
        xbuf[pl.ds(0, 8), :] = x[TM - 8:, :]
        r, ig, a, s, _ = _rg_gates(xc, wr_ref, wi_ref, vec_ref)
        a_s[...] = a
        b_s[...] = s * (ig * xc)

        def step(t, h):
            h = a_s[pl.ds(t, 1), :] * h + b_s[pl.ds(t, 1), :]
            hs_ref[pl.ds(t, 1), :] = h
            return h

        hcar[pl.ds(0, 1), :] = lax.fori_loop(0, TM, step, hcar[pl.ds(0, 1), :], unroll=8)
        gel, _ = _gelu_parts(p_ref[:, pl.ds(D_RG, D_RG)])
        n, _ = _rms_fwd(gel * hs_ref[...])
        y_ref[:, pl.ds(0, D_RG)] = (n * vec_ref[R_GRG:R_GRG + 1, :]).astype(_BF)

        lb = _sigmoid(hb_ref[0:1, :] - hb_ref[1:2, :])
        _, tri_blk, _ = _chunk_masks()
        q = _hg_prep(p_ref, lb, tri_blk.astype(_BF))
        for name, ref in (("qd", qd_s), ("kd", kd_s), ("qe", qe_s), ("ke", ke_s)):
            ref[...] = q[name].astype(_BF)
        v_s[...] = p_ref[:, pl.ds(2 * D_RG + 2 * D_HG, D_HG)].astype(_BF)
        e_end = q["e_end"]
        causal = (lax.broadcasted_iota(jnp.int32, (HC, HC), 0) >= lax.broadcasted_iota(jnp.int32, (HC, HC), 1))
        for c in range(nc_t):
            for h in range(NH):
                rs, cs = pl.ds(HC * c, HC), pl.ds(HD * h, HD)
                amat = jnp.where(causal, _dot_nt(qd_s[rs, cs], kd_s[rs, cs]), 0.0)
                o_ref[rs, cs] = _dot(amat.astype(_BF), v_s[rs, cs])
                u_s[NH * c + h] = _dot_tn(v_s[rs, cs], ke_s[rs, cs])
        for h in range(NH):
            cs = pl.ds(HD * h, HD)
            s_run = st[h]
            for c in range(nc_t):
                rs = pl.ds(HC * c, HC)
                sc_ref[c, h] = s_run
                o_ref[rs, cs] += _dot_nt(qe_s[rs, cs], s_run.astype(_BF))
                s_run = e_end[HC * c:HC * c + 1, HD * h:HD * (h + 1)] * s_run + u_s[NH * c + h]
            st[h] = s_run
        for h in range(NH):
            cs = pl.ds(HD * h, HD)
            n_o, _ = _rms_fwd(o_ref[:, cs])
            hg = p_ref[:, pl.ds(2 * D_RG + 3 * D_HG + HD * h, HD)]
            y_ref[:, pl.ds(D_RG + HD * h, HD)] = ((n_o * ghg_ref[...]) * (hg * _sigmoid(hg))).astype(_BF)

        tail()

    hbm = pl.BlockSpec(memory_space=pl.ANY)
    return pl.pallas_call(
        body, name="mixer_fwd", grid=(nt,),
        in_specs=[pl.BlockSpec((TM, D_IN), lambda i: (i, 0)), _full((D_RG, D_RG)), _full((D_RG, D_RG)),
                  _full((16, D_RG)), _full((2, D_HG)), _full((1, HD))] + [hbm] * nsh,
        out_specs=[pl.BlockSpec((TM, D), lambda i: (i, 0)), pl.BlockSpec((TM, D_RG), lambda i: (i, 0)),
                   pl.BlockSpec((TM, D_HG), lambda i: (i, 0)),
                   pl.BlockSpec((nc_t, NH, HD, HD), lambda i: (i, 0, 0, 0))] + [hbm] * nsh,
        out_shape=[_S((t_pad, D), _BF), _S((t_pad, D_RG), _F32), _S((t_pad, D_HG), _F32),
                   _S((t_pad // HC, NH, HD, HD), _F32)] + [_S((N_DEV,) + s.shape, s.dtype) for s in shards],
        scratch_shapes=[pltpu.VMEM((TM + 8, D_RG), _F32), pltpu.VMEM((TM, D_RG), _F32),
                        pltpu.VMEM((TM, D_RG), _F32), pltpu.VMEM((8, D_RG), _F32),
                        pltpu.VMEM((NH, HD, HD), _F32)] + [pltpu.VMEM((TM, D_HG), _BF) for _ in range(5)]
        + [pltpu.VMEM((nc_t * NH, HD, HD), _F32)] + _sem_shapes(nsh),
        compiler_params=_cp(("arbitrary",)),
    )(p, wr, wi, vec, hb, g_hg, *shards)


def _outproj(h0, y, w_out, g_ffn):
    t_pad = h0.shape[0]

    def body(h_ref, y_ref, w_ref, g_ref, h1_ref, v_ref):
        h1 = h_ref[...] + _dot(y_ref[...], w_ref[...])
        h1_ref[...] = h1
        n, _ = _rms_fwd(h1)
        v_ref[...] = (n * g_ref[...]).astype(_BF)

    return pl.pallas_call(
        body, name="outproj", grid=(t_pad // TM,),
        in_specs=[pl.BlockSpec((TM, D), lambda i: (i, 0)), pl.BlockSpec((TM, D), lambda i: (i, 0)),
                  _full((D, D)), _full((1, D))],
        out_specs=[pl.BlockSpec((TM, D), lambda i: (i, 0)), pl.BlockSpec((TM, D), lambda i: (i, 0))],
        out_shape=[_S((t_pad, D), _F32), _S((t_pad, D), _BF)],
        compiler_params=_cp(("arbitrary",)),
    )(h0, y, w_out, g_ffn)


def _ffn_loss(v, h1, w_gu, w_down, g_fin, tgt, n_valid):
    t_pad = v.shape[0]

    def body(v_ref, h1_ref, wgu_ref, wd_ref, g_ref, t_ref, gu_ref, act_ref, dh2_ref, dh2b_ref, loss_ref, gfin_ref):
        i = pl.program_id(0)

        @pl.when(i == 0)
        def _():
            loss_ref[...] = jnp.zeros_like(loss_ref)
            gfin_ref[...] = jnp.zeros_like(gfin_ref)

        vb = v_ref[...]
        h2 = h1_ref[...]
        for b in range(4):
            gate = _dot_nt(vb, wgu_ref[b])
            up = _dot_nt(vb, wgu_ref[4 + b])
            gu_ref[b] = gate
            gu_ref[4 + b] = up
            act = ((gate * _sigmoid(gate)) * up).astype(_BF)
            act_ref[b] = act
            h2 = h2 + _dot(act, wd_ref[b])
        n, r = _rms_fwd(h2)
        out = n * g_ref[...]
        row = i * TM + lax.broadcasted_iota(jnp.int32, (TM, 1), 0)
        valid = (row >= N_META) & (row < n_valid)
        err = jnp.where(valid, out - t_ref[...], 0.0)
        loss_ref[...] += (0.5 / D) * jnp.sum(err * err)
        dout = err * (1.0 / D)
        gfin_ref[...] += jnp.sum(dout * n, axis=0, keepdims=True)
        dh2 = _rms_bwd(dout * g_ref[...], n, r)
        dh2_ref[...] = dh2
        dh2b_ref[...] = dh2.astype(_BF)

    return pl.pallas_call(
        body, name="ffn_loss", grid=(t_pad // TM,),
        in_specs=[pl.BlockSpec((TM, D), lambda i: (i, 0)), pl.BlockSpec((TM, D), lambda i: (i, 0)),
                  _const((N_DEV, FFB, D)), _const((4, FFB, D)), _full((1, D)),
                  pl.BlockSpec((TM, D), lambda i: (i, 0))],
        out_specs=[pl.BlockSpec((N_DEV, TM, FFB), lambda i: (0, i, 0)), pl.BlockSpec((4, TM, FFB), lambda i: (0, i, 0)),
                   pl.BlockSpec((TM, D), lambda i: (i, 0)), pl.BlockSpec((TM, D), lambda i: (i, 0)),
                   _full((8, 128)), _full((1, D))],
        out_shape=[_S((N_DEV, t_pad, FFB), _F32), _S((4, t_pad, FFB), _BF), _S((t_pad, D), _F32),
                   _S((t_pad, D), _BF), _S((8, 128), _F32), _S((1, D), _F32)],
        compiler_params=_cp(("arbitrary",)),
    )(v, h1, w_gu, w_down, g_fin, tgt)


def _ffn_bwd(dh2, dh2b, gu, h1, g_ffn, w_gu, w_down, w_out):
    t_pad = dh2.shape[0]

    def body(dh2_ref, dh2b_ref, gu_ref, h1_ref, g_ref, wgu_ref, wd_ref, wo_ref,
             dgu_ref, dh1_ref, dh1b_ref, dy_ref, gffn_ref):
        i = pl.program_id(0)

        @pl.when(i == 0)
        def _():
            gffn_ref[...] = jnp.zeros_like(gffn_ref)

        db = dh2b_ref[...]
        dv = jnp.zeros((TM, D), _F32)
        for b in range(4):
            dact = _dot_nt(db, wd_ref[b])
            gate = gu_ref[b]
            up = gu_ref[4 + b]
            sg = _sigmoid(gate)
            dgate = ((dact * up) * _dsilu(gate, sg)).astype(_BF)
            dup = (dact * (gate * sg)).astype(_BF)
            dgu_ref[b] = dgate
            dgu_ref[4 + b] = dup
            dv = dv + _dot(dgate, wgu_ref[b]) + _dot(dup, wgu_ref[4 + b])
        n, r = _rms_fwd(h1_ref[...])
        gffn_ref[...] += jnp.sum(dv * n, axis=0, keepdims=True)
        dh1 = dh2_ref[...] + _rms_bwd(dv * g_ref[...], n, r)
        dh1_ref[...] = dh1
        dh1b = dh1.astype(_BF)
        dh1b_ref[...] = dh1b
        dy_ref[...] = _dot_nt(dh1b, wo_ref[...])

    tile = pl.BlockSpec((TM, D), lambda i: (i, 0))
    return pl.pallas_call(
        body, name="ffn_bwd", grid=(t_pad // TM,),
        in_specs=[tile, tile, pl.BlockSpec((N_DEV, TM, FFB), lambda i: (0, i, 0)), tile, _full((1, D)),
                  _const((N_DEV, FFB, D)), _const((4, FFB, D)), _const((D, D))],
        out_specs=[pl.BlockSpec((N_DEV, TM, FFB), lambda i: (0, i, 0)), tile, tile, tile, _full((1, D))],
        out_shape=[_S((N_DEV, t_pad, FFB), _BF), _S((t_pad, D), _F32), _S((t_pad, D), _BF),
                   _S((t_pad, D), _F32), _S((1, D), _F32)],
        compiler_params=_cp(("arbitrary",)),
    )(dh2, dh2b, gu, h1, g_ffn, w_gu, w_down, w_out)


def _mixer_bwd(p, hs, o, sc, dy, wr, wi, vec, hb, g_hg, scatter):
    t_pad = p.shape[0]
    nt = t_pad // TM
    nc_t = TM // HC
    nsc = len(scatter)

    def rev(i):
        return nt - 1 - i

    def body(p_ref, pprev_ref, hs_ref, hprev_ref, o_ref, sc_ref, dy_ref, wr_ref, wi_ref, vec_ref, hb_ref, ghg_ref,
             *rest):
        send_refs, rest = rest[:nsc], rest[nsc:]
        dp_ref, gvec_ref, gw_ref = rest[:3]
        recv_refs, rest = rest[3:3 + nsc], rest[3 + nsc:]
        xbuf, hbuf, dbuf, a_s, g_s, ccar, dst = rest[:7]
        qd_s, kd_s, qe_s, ke_s, v_s, do_s, dqd_s, dkd_s, dqe_s, dke_s, dv_s, w_s, dend_s = rest[7:20]
        exchange = _Exchange(send_refs, [], recv_refs, rest[20:])
        i = pl.program_id(0)
        first_tile = i == nt - 1

        @pl.when(i == 0)
        def _():
            exchange.start()
            gvec_ref[...] = jnp.zeros_like(gvec_ref)
            gw_ref[...] = jnp.zeros_like(gw_ref)
            dbuf[pl.ds(TM, 8), :] = jnp.zeros((8, D_RG), _F32)
            ccar[...] = jnp.zeros_like(ccar)
            dst[...] = jnp.zeros_like(dst)

        def acc(row, val):
            gvec_ref[row:row + 1, :] += jnp.sum(val, axis=0, keepdims=True)

        keep = jnp.where(first_tile, 0.0, 1.0)
        x = p_ref[:, pl.ds(0, D_RG)]
        xbuf[pl.ds(0, 8), :] = pprev_ref[...] * keep
        xbuf[pl.ds(8, TM), :] = x
        xc = _conv(xbuf, vec_ref)
        r, ig, a, s, nsp8 = _rg_gates(xc, wr_ref, wi_ref, vec_ref)
        h = hs_ref[...]
        hbuf[pl.ds(0, 8), :] = hprev_ref[...] * keep
        hbuf[pl.ds(8, TM), :] = h
        hm1 = hbuf[pl.ds(7, TM), :]
        gr = p_ref[:, pl.ds(D_RG, D_RG)]
        gel, dgel = _gelu_parts(gr)
        n, rr = _rms_fwd(gel * h)
        dyn = dy_ref[:, pl.ds(0, D_RG)]
        acc(R_GRG, dyn * n)
        dpre = _rms_bwd(dyn * vec_ref[R_GRG:R_GRG + 1, :], n, rr)
        dp_ref[:, pl.ds(D_RG, D_RG)] = ((dpre * h) * dgel).astype(_BF)
        a_s[...] = a
        g_s[...] = dpre * gel

        def step(k, c):
            t = TM - 1 - k
            g = g_s[pl.ds(t, 1), :] + c
            g_s[pl.ds(t, 1), :] = g
            return a_s[pl.ds(t, 1), :] * g

        ccar[pl.ds(0, 1), :] = lax.fori_loop(0, TM, step, ccar[pl.ds(0, 1), :], unroll=8)
        gt = g_s[...]
        da = gt * hm1
        ixc = ig * xc
        ds = gt * ixc
        dig = (gt * s) * xc
        dxc = (gt * s) * ig
        dla = da * a - ds * ((a * a) / s)
        lam = vec_ref[R_LAM:R_LAM + 1, :]
        gvec_ref[R_LAM:R_LAM + 1, :] += jnp.sum(dla * r, axis=0, keepdims=True) * (LRU_C * _sigmoid(-lam))
        dzr = (dla * nsp8) * (r * (1.0 - r))
        dzi = dig * (ig * (1.0 - ig))
        acc(R_BR, dzr)
        acc(R_BI, dzi)
        xcb = xc.astype(_BF)
        dzrb = dzr.astype(_BF)
        dzib = dzi.astype(_BF)
        gw_ref[0] += _dot_tn(xcb, dzrb)
        gw_ref[1] += _dot_tn(xcb, dzib)
        dxc = dxc + _dot_nt(dzrb, wr_ref[...]) + _dot_nt(dzib, wi_ref[...])
        acc(R_CONVB, dxc)
        for j in range(4):
            acc(R_CONVW + j, dxc * xbuf[pl.ds(5 + j, TM), :])
        dbuf[pl.ds(0, TM), :] = dxc
        dx = vec_ref[R_CONVW + 3:R_CONVW + 4, :] * dxc
        for j in range(3):
            dx = dx + vec_ref[R_CONVW + j:R_CONVW + j + 1, :] * dbuf[pl.ds(3 - j, TM), :]
        dbuf[pl.ds(TM, 8), :] = dxc[0:8, :]
        dp_ref[:, pl.ds(0, D_RG)] = dx.astype(_BF)

        lb = _sigmoid(hb_ref[0:1, :] - hb_ref[1:2, :])
        same, tri_blk, triu_blk = _chunk_masks()
        q = _hg_prep(p_ref, lb, tri_blk.astype(_BF))
        qdb, kdb = q["qd"].astype(_BF), q["kd"].astype(_BF)
        qd_s[...] = qdb
        kd_s[...] = kdb
        qe_s[...] = q["qe"].astype(_BF)
        ke_s[...] = q["ke"].astype(_BF)
        v_s[...] = p_ref[:, pl.ds(2 * D_RG + 2 * D_HG, D_HG)].astype(_BF)
        e_end = q["e_end"]
        ghg = ghg_ref[...]
        for h in range(NH):
            cs = pl.ds(HD * h, HD)
            hg = p_ref[:, pl.ds(2 * D_RG + 3 * D_HG + HD * h, HD)]
            sh = _sigmoid(hg)
            n_o, r_o = _rms_fwd(o_ref[:, cs])
            dyh = dy_ref[:, pl.ds(D_RG + HD * h, HD)]
            dp_ref[:, pl.ds(2 * D_RG + 3 * D_HG + HD * h, HD)] = ((dyh * (n_o * ghg)) * _dsilu(hg, sh)).astype(_BF)
            dn = dyh * (hg * sh)
            gvec_ref[R_GHG:R_GHG + 1, pl.ds(0, HD)] += jnp.sum(dn * n_o, axis=0, keepdims=True)
            do_s[:, cs] = _rms_bwd(dn * ghg, n_o, r_o).astype(_BF)
        causal = (lax.broadcasted_iota(jnp.int32, (HC, HC), 0) >= lax.broadcasted_iota(jnp.int32, (HC, HC), 1))
        for c in range(nc_t):
            for h in range(NH):
                rs, cs = pl.ds(HC * c, HC), pl.ds(HD * h, HD)
                qd_c, kd_c, do_c = qd_s[rs, cs], kd_s[rs, cs], do_s[rs, cs]
                amat = jnp.where(causal, _dot_nt(qd_c, kd_c), 0.0).astype(_BF)
                da_m = jnp.where(causal, _dot_nt(do_c, v_s[rs, cs]), 0.0).astype(_BF)
                dqd_s[rs, cs] = _dot(da_m, kd_c)
                dkd_s[rs, cs] = _dot_tn(da_m, qd_c)
                dqe_s[rs, cs] = _dot(do_c, sc_ref[c, h].astype(_BF))
                dv_s[rs, cs] = _dot_tn(amat, do_c)
                w_s[NH * c + h] = _dot_tn(do_c, qe_s[rs, cs])
        for h in range(NH):
            cs = pl.ds(HD * h, HD)
            d_run = dst[h]
            for c in reversed(range(nc_t)):
                rs = pl.ds(HC * c, HC)
                d_b = d_run.astype(_BF)
                dke_s[rs, cs] = _dot(v_s[rs, cs], d_b)
                dp_ref[rs, pl.ds(2 * D_RG + 2 * D_HG + HD * h, HD)] = (
                    dv_s[rs, cs] + _dot_nt(ke_s[rs, cs], d_b)).astype(_BF)
                dend_s[pl.ds(c, 1), cs] = jnp.sum(sc_ref[c, h] * d_run, axis=0, keepdims=True)
                d_run = w_s[NH * c + h] + e_end[HC * c:HC * c + 1, HD * h:HD * (h + 1)] * d_run
            dst[h] = d_run
        dqd, dkd, dqe, dke = dqd_s[...], dkd_s[...], dqe_s[...], dke_s[...]
        dq = dqd * q["e_q"] + dqe * q["e_b"]
        dk = dkd * q["e_k"] + dke * q["e_l"]
        dkeke = dke * q["ke"]
        db = dqd * qdb.astype(_F32) - dkd * kdb.astype(_F32) + dqe * q["qe"] - dkeke
        d_end = jnp.concatenate([jnp.broadcast_to(dend_s[pl.ds(c, 1), :], (HC, D_HG)) for c in range(nc_t)], axis=0)
        dlf = _dot3(triu_blk.astype(_BF), db) + _dot3(same.astype(_BF), dkeke) + d_end * e_end
        df = dlf / q["f"] - dk
        sg = q["sg"]
        gvec_ref[R_HB0:R_HB0 + 1, :] += jnp.sum(df * (1.0 - sg), axis=0, keepdims=True)
        dp_ref[:, pl.ds(2 * D_RG, D_HG)] = (dq * _dsilu(q["hq"], q["sq"])).astype(_BF)
        dp_ref[:, pl.ds(2 * D_RG + D_HG, D_HG)] = ((df * (1.0 - lb)) * (sg * (1.0 - sg))).astype(_BF)

        @pl.when(i == nt - 1)
        def _():
            glb = gvec_ref[R_HB0:R_HB0 + 1, :] * (lb * (1.0 - lb))
            gvec_ref[R_HB0:R_HB0 + 1, :] = glb
            gvec_ref[R_HB1:R_HB1 + 1, :] = -glb
            exchange.finish()

    hbm = pl.BlockSpec(memory_space=pl.ANY)
    return pl.pallas_call(
        body, name="mixer_bwd", grid=(nt,),
        in_specs=[pl.BlockSpec((TM, D_IN), lambda i: (rev(i), 0)),
                  pl.BlockSpec((8, D_RG), lambda i: (jnp.maximum(rev(i) * (TM // 8) - 1, 0), 0)),
                  pl.BlockSpec((TM, D_RG), lambda i: (rev(i), 0)),
                  pl.BlockSpec((8, D_RG), lambda i: (jnp.maximum(rev(i) * (TM // 8) - 1, 0), 0)),
                  pl.BlockSpec((TM, D_HG), lambda i: (rev(i), 0)),
                  pl.BlockSpec((nc_t, NH, HD, HD), lambda i: (rev(i), 0, 0, 0)),
                  pl.BlockSpec((TM, D), lambda i: (rev(i), 0)),
                  _full((D_RG, D_RG)), _full((D_RG, D_RG)), _full((16, D_RG)), _full((2, D_HG)), _full((1, HD))]
        + [hbm] * nsc,
        out_specs=[pl.BlockSpec((TM, D_IN), lambda i: (rev(i), 0)), _full((16, D_RG)), _full((2, D_RG, D_RG))]
        + [hbm] * nsc,
        out_shape=[_S((t_pad, D_IN), _BF), _S((16, D_RG), _F32), _S((2, D_RG, D_RG), _F32)]
        + [_S(s.shape, s.dtype) for s in scatter],
        scratch_shapes=[pltpu.VMEM((TM + 8, D_RG), _F32), pltpu.VMEM((TM + 8, D_RG), _F32),
                        pltpu.VMEM((TM + 8, D_RG), _F32), pltpu.VMEM((TM, D_RG), _F32),
                        pltpu.VMEM((TM, D_RG), _F32), pltpu.VMEM((8, D_RG), _F32),
                        pltpu.VMEM((NH, HD, HD), _F32)]
        + [pltpu.VMEM((TM, D_HG), _BF) for _ in range(6)] + [pltpu.VMEM((TM, D_HG), _F32) for _ in range(5)]
        + [pltpu.VMEM((nc_t * NH, HD, HD), _F32), pltpu.VMEM((8, D_HG), _F32)] + _sem_shapes(nsc),
        compiler_params=_cp(("arbitrary",)),
    )(p, p, hs, hs, o, sc, dy, wr, wi, vec, hb, g_hg, *scatter)


def _inproj_bwd_send(dp, w_in, h0, dh1, g_mix, u, order, gffn, gfin, loss, to_all):
    t_pad = dp.shape[0]
    rb = t_pad // (2 * N_DEV)
    n_steps = N_DEV + 2 * N_DEV
    na = len(to_all)

    def body(order_ref, dpc_ref, dpr_ref, u_ref, w_ref, h_ref, dh1_ref, g_ref, gffn_ref, gfin_ref, loss_ref, *rest):
        all_in = rest[:na]
        dh0_ref, recv_ref = rest[na:na + 2]
        all_out = rest[na + 2:2 * na + 2]
        alla_ref = rest[2 * na + 2]
        buf, pack, blk_send, blk_recv, blk_local = rest[2 * na + 3:2 * na + 8]
        exchange = _Exchange([], all_in, all_out, rest[2 * na + 8:2 * na + 11])
        last = _Exchange([], [pack], [alla_ref], rest[2 * na + 11:])
        s = pl.program_id(0)
        x, y, c = _coords()
        me = 4 * x + 2 * y + c

        def send(step):
            r = _SEND_ORDER[step]
            return pltpu.make_async_remote_copy(
                src_ref=buf.at[step], dst_ref=recv_ref.at[me], send_sem=blk_send.at[step], recv_sem=blk_recv.at[r - 1],
                device_id=(x ^ (r >> 2), y ^ ((r >> 1) & 1), c ^ (r & 1)), device_id_type=_MESH)

        @pl.when(s == 0)
        def _():
            exchange.start()
            pack[...] = jnp.zeros_like(pack)

        @pl.when(s < N_DEV)
        def _():
            buf[s] = _dot_tn(u_ref[...], dpc_ref[...]).astype(_BF)

            for step in range(N_DEV - 1):
                @pl.when(s == step)
                def _(step=step):
                    send(step).start()

        @pl.when(s >= N_DEV)
        def _():
            du = jnp.zeros((rb, D), _F32)
            for j in range(4):
                du = du + _dot_nt(dpr_ref[:, WIN_P * j:WIN_P * (j + 1)], w_ref[j])
            n, r = _rms_fwd(h_ref[...])
            pack[R_GMIX:R_GMIX + 1, :] += jnp.sum(du * n, axis=0, keepdims=True)
            dh0 = dh1_ref[...] + _rms_bwd(du * g_ref[...], n, r)
            dh0_ref[...] = dh0

            @pl.when(s == N_DEV)
            def _():
                pack[R_META:R_META + N_META, :] = dh0[0:N_META, :]

        @pl.when(s == n_steps - 1)
        def _():
            pack[R_GFFN:R_GFFN + 1, :] = gffn_ref[...]
            pack[R_GFIN:R_GFIN + 1, :] = gfin_ref[...]
            pack[R_LOSS:R_LOSS + 1, pl.ds(0, 128)] = loss_ref[0:1, :]
            last.start()
            mine = pltpu.make_async_copy(buf.at[N_DEV - 1], recv_ref.at[me], blk_local.at[0])
            mine.start()
            for step in range(N_DEV - 1):
                send(step).wait_send()
            for r in range(1, N_DEV):
                px, py, pc = x ^ (r >> 2), y ^ ((r >> 1) & 1), c ^ (r & 1)
                pltpu.make_async_remote_copy(
                    src_ref=buf.at[0], dst_ref=recv_ref.at[4 * px + 2 * py + pc], send_sem=blk_send.at[0],
                    recv_sem=blk_recv.at[r - 1], device_id=(px, py, pc), device_id_type=_MESH).wait_recv()
            mine.wait()
            exchange.finish()
            last.finish()

    hbm = pl.BlockSpec(memory_space=pl.ANY)
    rows = pl.BlockSpec((rb, D), lambda s, order: (jnp.maximum(s - N_DEV, 0), 0))
    one = pl.BlockSpec((1, D), lambda s, order: (0, 0))
    res = pl.pallas_call(
        body, name="inproj_bwd_send",
        grid_spec=pltpu.PrefetchScalarGridSpec(
            num_scalar_prefetch=1, grid=(n_steps,),
            in_specs=[pl.BlockSpec((t_pad, WIN_B), lambda s, order: (0, order[jnp.minimum(s, N_DEV - 1)])),
                      pl.BlockSpec((rb, D_IN), lambda s, order: (jnp.maximum(s - N_DEV, 0), 0)),
                      pl.BlockSpec((t_pad, D), lambda s, order: (0, 0), pipeline_mode=pl.Buffered(1)),
                      pl.BlockSpec((4, D, WIN_P), lambda s, order: (0, 0, 0), pipeline_mode=pl.Buffered(1)),
                      rows, rows, one, one, one, pl.BlockSpec((8, 128), lambda s, order: (0, 0))] + [hbm] * na,
            out_specs=[rows] + [hbm] * (na + 2),
            scratch_shapes=[pltpu.VMEM((N_DEV, D, WIN_B), _BF), pltpu.VMEM((24, D), _F32),
                            pltpu.SemaphoreType.DMA((N_DEV - 1,)), pltpu.SemaphoreType.DMA((N_DEV - 1,)),
                            pltpu.SemaphoreType.DMA((1,))] + _sem_shapes(na) + _sem_shapes(1)),
        out_shape=[_S((t_pad, D), _F32), _S((N_DEV, D, WIN_B), _BF)]
        + [_S((N_DEV,) + g.shape, g.dtype) for g in to_all] + [_S((N_DEV, 24, D), _F32)],
        compiler_params=_cp(("arbitrary",)),
    )(order, dp, dp, u, w_in, h0, dh1, g_mix, gffn, gfin, loss, *to_all)
    return res


def _wgrad(name, a, b, a_spec, b_spec, n_blocks, out_block, scatter=()):
    nsc = len(scatter)

    def body(a_ref, b_ref, *rest):
        o_ref = rest[nsc]
        j = pl.program_id(0)
        if nsc:
            exchange = _Exchange(rest[:nsc], [], rest[nsc + 1:2 * nsc + 1], rest[2 * nsc + 1:])

            @pl.when(j == 0)
            def _():
                exchange.start()

        av = a_ref[0] if len(a_ref.shape) == 3 else a_ref[...]
        bv = b_ref[0] if len(b_ref.shape) == 3 else b_ref[...]
        o_ref[0] = _dot_tn(av, bv).astype(_BF)

        if nsc:
            @pl.when(j == n_blocks - 1)
            def _():
                exchange.finish()

    hbm = pl.BlockSpec(memory_space=pl.ANY)
    res = pl.pallas_call(
        body, name=name, grid=(n_blocks,),
        in_specs=[a_spec, b_spec] + [hbm] * nsc,
        out_specs=[pl.BlockSpec((1,) + out_block, lambda j: (j, 0, 0))] + [hbm] * nsc,
        out_shape=[_S((n_blocks,) + out_block, _BF)] + [_S(s.shape, s.dtype) for s in scatter],
        scratch_shapes=_sem_shapes(nsc) if nsc else [],
        compiler_params=_cp(("arbitrary",)),
    )(a, b, *scatter)
    return res if nsc else res[0]


def _coords():
    return lax.axis_index("x"), lax.axis_index("y"), lax.axis_index("c")


def _sem_shapes(na):
    return [pltpu.SemaphoreType.DMA((7 * na,)), pltpu.SemaphoreType.DMA((7 * na,)), pltpu.SemaphoreType.DMA((na,))]


class _Gather:
    def __init__(self, srcs, outs, sems, place=None):
        self.srcs, self.outs = srcs, outs
        self.send_sems, self.recv_sems, self.local_sems = sems
        self.place = place if place is not None else (lambda ref, block: ref.at[block])
        self.na = len(srcs)
        x, y, c = _coords()
        self.pos = (x, y, c)
        self.me = 4 * x + 2 * y + c
        self.sibling = (x, y, 1 - c)
        self.chips = [(1 - x, y), (x, 1 - y), (1 - x, 1 - y)]

    @staticmethod
    def _slot(px, py, pc):
        return 4 * px + 2 * py + pc

    def _copy(self, a, k, block, to, own=False):
        dst = self.place(self.outs[a], block)
        return pltpu.make_async_remote_copy(
            src_ref=self.srcs[a] if own else dst, dst_ref=dst,
            send_sem=self.send_sems.at[7 * a + k], recv_sem=self.recv_sems.at[7 * a + k],
            device_id=to, device_id_type=_MESH)

    def _mine(self, a):
        return pltpu.make_async_copy(self.srcs[a], self.place(self.outs[a], self.me), self.local_sems.at[a])

    def _first(self):
        c = self.pos[2]
        cps = []
        for a in range(self.na):
            cps.append(self._copy(a, 0, self.me, self.sibling, own=True))
            cps += [self._copy(a, 1 + j, self.me, (*chip, c), own=True) for j, chip in enumerate(self.chips)]
        return cps

    def _passed(self):
        c = self.pos[2]
        return [self._copy(a, 4 + j, self._slot(*chip, c), self.sibling)
                for j, chip in enumerate(self.chips) for a in range(self.na)]

    def start(self):
        for a in range(self.na):
            self._mine(a).start()
        for cp in self._first():
            cp.start()

    def forward(self, j):
        c = self.pos[2]
        chip = self.chips[j]
        for a in range(self.na):
            self._copy(a, 1 + j, self._slot(*chip, c), self.pos).wait_recv()
            self._copy(a, 4 + j, self._slot(*chip, c), self.sibling).start()

    def wait_sibling(self):
        x, y, c = self.pos
        for a in range(self.na):
            self._copy(a, 0, self._slot(x, y, 1 - c), self.pos).wait_recv()

    def wait_passed(self, j):
        c = self.pos[2]
        for a in range(self.na):
            self._copy(a, 4 + j, self._slot(*self.chips[j], 1 - c), self.pos).wait_recv()

    def wait_mine(self):
        for a in range(self.na):
            self._mine(a).wait()

    def finish_sends(self, mine=True):
        for cp in self._first() + self._passed():
            cp.wait_send()
        if mine:
            self.wait_mine()

    def finish(self):
        self.wait_sibling()
        for j in range(3):
            self.wait_passed(j)
        self.finish_sends()


class _Exchange:
    def __init__(self, scatter, gather, outs, sems):
        self.ins = list(scatter) + list(gather)
        self.ns, self.na = len(scatter), len(scatter) + len(gather)
        self.outs = outs
        self.send_sems, self.recv_sems, self.local_sems = sems
        x, y, c = _coords()
        self.pos = (x, y, c)
        self.me = 4 * x + 2 * y + c

    def _peer(self, r):
        x, y, c = self.pos
        return x ^ (r >> 2), y ^ ((r >> 1) & 1), c ^ (r & 1)

    def _src(self, a, block):
        return self.ins[a].at[block] if a < self.ns else self.ins[a]

    def _local(self, a):
        return pltpu.make_async_copy(self._src(a, self.me), self.outs[a].at[self.me], self.local_sems.at[a])

    def _send(self, a, r):
        px, py, pc = self._peer(r)
        return pltpu.make_async_remote_copy(
            src_ref=self._src(a, 4 * px + 2 * py + pc), dst_ref=self.outs[a].at[self.me],
            send_sem=self.send_sems.at[7 * a + r - 1], recv_sem=self.recv_sems.at[7 * a + r - 1],
            device_id=(px, py, pc), device_id_type=_MESH)

    def _recv(self, a, r):
        px, py, pc = self._peer(r)
        return pltpu.make_async_remote_copy(
            src_ref=self._src(a, self.me), dst_ref=self.outs[a].at[4 * px + 2 * py + pc],
            send_sem=self.send_sems.at[7 * a + r - 1], recv_sem=self.recv_sems.at[7 * a + r - 1],
            device_id=(px, py, pc), device_id_type=_MESH)

    def start(self):
        for a in range(self.na):
            self._local(a).start()
        for r in range(1, N_DEV):
            for a in range(self.na):
                self._send(a, r).start()

    def finish(self):
        for r in range(1, N_DEV):
            for a in range(self.na):
                self._recv(a, r).wait_recv()
        for r in range(1, N_DEV):
            for a in range(self.na):
                self._send(a, r).wait_send()
        for a in range(self.na):
            self._local(a).wait()


def _allgather_first(gather_f32, cast_f32, gather_dtypes):
    ng, nc = len(gather_f32), len(cast_f32)

    def body(*refs):
        ins, cins = refs[:ng], refs[ng:ng + nc]
        outs, couts = refs[ng + nc:2 * ng + nc], refs[2 * ng + nc:2 * ng + 2 * nc]
        stage = refs[2 * ng + 2 * nc:3 * ng + 2 * nc]
        sems = refs[3 * ng + 2 * nc:]
        for a in range(ng):
            stage[a][...] = ins[a][...].astype(gather_dtypes[a])
        g = _Gather(stage, outs, sems)
        g.start()
        for a in range(nc):
            couts[a][...] = cins[a][...].astype(_BF)
        for j in range(3):
            g.forward(j)
        g.finish()

    vm = pl.BlockSpec(memory_space=pltpu.VMEM)
    return pl.pallas_call(
        body, name="allgather_first",
        in_specs=[vm] * (ng + nc),
        out_specs=[pl.BlockSpec(memory_space=pl.ANY)] * ng + [vm] * nc,
        out_shape=[_S((N_DEV,) + l.shape, dt) for l, dt in zip(gather_f32, gather_dtypes)]
        + [_S(l.shape, _BF) for l in cast_f32],
        scratch_shapes=[pltpu.VMEM(l.shape, dt) for l, dt in zip(gather_f32, gather_dtypes)] + _sem_shapes(ng),
        compiler_params=pltpu.CompilerParams(vmem_limit_bytes=VMEM_LIMIT),
    )(*gather_f32, *cast_f32)


def _adamw_math(w, g, m, v):
    m2 = ADAM_B1 * m + (1.0 - ADAM_B1) * g
    v2 = ADAM_B2 * v + (1.0 - ADAM_B2) * (g * g)
    m_hat = m2 / (1.0 - ADAM_B1 ** ADAM_STEP)
    v_hat = v2 / (1.0 - ADAM_B2 ** ADAM_STEP)
    delta = -ADAM_LR * (m_hat / (jnp.sqrt(v_hat) + ADAM_EPS) + ADAM_WD * w)
    return delta, m2, v2


def _adamw_big(name, recv, w, m, v, rows):
    r_all, c_all = w.shape

    def body(r_ref, w_ref, m_ref, v_ref, g_out, d_out, m_out, v_out):
        g = r_ref[0].astype(_F32)
        for k in range(1, N_DEV):
            g = g + r_ref[k].astype(_F32)
        delta, m2, v2 = _adamw_math(w_ref[...], g, m_ref[...], v_ref[...])
        g_out[...] = g
        d_out[...] = delta
        m_out[...] = m2
        v_out[...] = v2

    tile = pl.BlockSpec((rows, c_all), lambda i: (i, 0))
    return pl.pallas_call(
        body, name=name, grid=(r_all // rows,),
        in_specs=[pl.BlockSpec((N_DEV, rows, c_all), lambda i: (0, i, 0)), tile, tile, tile],
        out_specs=[tile] * 4,
        out_shape=[_S(w.shape, _F32)] * 4,
        compiler_params=_cp(("arbitrary",)),
    )(recv, w, m, v)


def _adamw_small(gathered, slices, wmv):
    ng, npar = len(gathered), len(slices)

    def body(*refs):
        g_refs = refs[:ng]
        wmv_refs = refs[ng:ng + 3 * npar]
        outs = refs[ng + 3 * npar:]
        for i, (ai, r0, nr, c0, ncol) in enumerate(slices):
            g = g_refs[ai][0, pl.ds(r0, nr), pl.ds(c0, ncol)].astype(_F32)
            for k in range(1, N_DEV):
                g = g + g_refs[ai][k, pl.ds(r0, nr), pl.ds(c0, ncol)].astype(_F32)
            w_ref, m_ref, v_ref = wmv_refs[3 * i:3 * i + 3]
            delta, m2, v2 = _adamw_math(w_ref[...], g, m_ref[...], v_ref[...])
            outs[4 * i][...] = g
            outs[4 * i + 1][...] = delta
            outs[4 * i + 2][...] = m2
            outs[4 * i + 3][...] = v2
        total = g_refs[0][0, pl.ds(R_LOSS, 1), pl.ds(0, 128)]
        for k in range(1, N_DEV):
            total = total + g_refs[0][k, pl.ds(R_LOSS, 1), pl.ds(0, 128)]
        outs[4 * npar][...] = total

    flat = [t for trip in wmv for t in trip]
    out_shape = []
    for w, _, _ in wmv:
        out_shape += [_S(w.shape, _F32)] * 4
    out_shape.append(_S((1, 128), _F32))
    return pl.pallas_call(
        body, name="adamw_small", out_shape=out_shape,
        compiler_params=pltpu.CompilerParams(vmem_limit_bytes=VMEM_LIMIT),
    )(*gathered, *flat)


def _block_diag(w):
    eye = jnp.eye(8, dtype=w.dtype)
    return (w[:, :, None, :] * eye[:, None, :, None]).reshape(D_RG, D_RG)


def _diag_blocks(g):
    return jnp.concatenate([g[64 * h:64 * (h + 1), 64 * h:64 * (h + 1)] for h in range(8)], axis=0)


def _local_step(x, tgt, meta, g_mix, w_in_l, vec, wr, wi, hb, g_hg, w_out_l, g_ffn, w_gu_l, w_down_l, g_fin):
    seq = x.shape[0]
    n_valid = N_META + seq
    t_pad = -(-n_valid // TM) * TM
    h0 = jnp.concatenate([meta, x, jnp.zeros((t_pad - n_valid, D), _F32)], axis=0)
    tgt_p = jnp.concatenate([jnp.zeros((N_META, D), _F32), tgt, jnp.zeros((t_pad - n_valid, D), _F32)], axis=0)

    me = 4 * lax.axis_index("x") + 2 * lax.axis_index("y") + lax.axis_index("c")
    p, u, w_in, w_out = _inproj(h0, g_mix, ((me >> 1) ^ jnp.array(_CHIP_ORDER, jnp.int32)).astype(jnp.int32), w_in_l,
                                [w_out_l])
    y, hs, o, sc, w_gu, w_down = _mixer_fwd(p, wr, wi, vec, hb, g_hg, [w_gu_l, w_down_l])
    w_out = w_out.reshape(D, D)
    w_down = w_down.reshape(4, FFB, D)
    h1, v = _outproj(h0, y, w_out, g_ffn)
    gu, act, dh2, dh2b, loss, gfin = _ffn_loss(v, h1, w_gu, w_down, g_fin, tgt_p, n_valid)

    dgu, dh1, dh1b, dy, gffn = _ffn_bwd(dh2, dh2b, gu, h1, g_ffn, w_gu, w_down, w_out)
    g_wdown = _wgrad("wgrad_down", act, dh2b, pl.BlockSpec((1, t_pad, FFB), lambda j: (j, 0, 0)),
                     pl.BlockSpec((t_pad, D), lambda j: (0, 0)), 4, (FFB, D))
    g_wgu, r_wdown = _wgrad("wgrad_gate_up", dgu, v, pl.BlockSpec((1, t_pad, FFB), lambda j: (j, 0, 0)),
                            pl.BlockSpec((t_pad, D), lambda j: (0, 0)), N_DEV, (FFB, D),
                            scatter=[g_wdown.reshape(N_DEV, D_FF // N_DEV, D)])
    g_wout = _wgrad("wgrad_out", y, dh1b, pl.BlockSpec((t_pad, D // N_DEV), lambda j: (0, j)),
                    pl.BlockSpec((t_pad, D), lambda j: (0, 0)), N_DEV, (D // N_DEV, D))
    dp, gvec, gw, r_wgu, r_wout = _mixer_bwd(p, hs, o, sc, dy, wr, wi, vec, hb, g_hg, [g_wgu, g_wout])
    pack_c = jnp.concatenate([_diag_blocks(gw[0]), _diag_blocks(gw[1])], axis=1).astype(_BF)
    order = (me ^ jnp.array(_SEND_ORDER, jnp.int32)).astype(jnp.int32)
    dh0, r_win, all_b, all_c, all_a = _inproj_bwd_send(dp, w_in, h0, dh1, g_mix, u, order, gffn, gfin, loss,
                                                       [gvec, pack_c])
    return dh0, (r_win, r_wgu, r_wout, r_wdown), (all_a, all_b, all_c)


def kernel(x, meta_tokens, mix_norm_g, w_in, conv_w, conv_b, w_rgate, b_rgate, w_igate, b_igate, lru_lambda, rg_norm_g, hg_lower_bound, hg_norm_g, w_out, ffn_norm_g, w_gate_up, w_down, final_norm_g, loss_target, m_meta_tokens, m_mix_norm_g, m_w_in, m_conv_w, m_conv_b, m_w_rgate, m_b_rgate, m_w_igate, m_b_igate, m_lru_lambda, m_rg_norm_g, m_hg_lower_bound, m_hg_norm_g, m_w_out, m_ffn_norm_g, m_w_gate_up, m_w_down, m_final_norm_g, v_meta_tokens, v_mix_norm_g, v_w_in, v_conv_w, v_conv_b, v_w_rgate, v_b_rgate, v_w_igate, v_b_igate, v_lru_lambda, v_rg_norm_g, v_hg_lower_bound, v_hg_norm_g, v_w_out, v_ffn_norm_g, v_w_gate_up, v_w_down, v_final_norm_g):
    seq = x.shape[1]
    me = 4 * lax.axis_index("x") + 2 * lax.axis_index("y") + lax.axis_index("c")

    small_l = jnp.concatenate([meta_tokens, jnp.pad(conv_w[0], ((0, 4), (0, 64)))], axis=0)
    small_g, w_in_l, w_gu_l, w_out_l, w_down_l = _allgather_first(
        [small_l], [w_in[0], w_gate_up[0].T, w_out[0], w_down[0]], [_F32])
    meta_full = jnp.transpose(small_g[:, :N_META, :], (1, 0, 2)).reshape(N_META, D)
    conv_w_full = jnp.transpose(small_g[:, N_META:N_META + 4, :64], (1, 0, 2)).reshape(4, D_RG)
    vec = jnp.concatenate([conv_b, b_rgate, b_igate, lru_lambda, rg_norm_g, jnp.zeros((3, D_RG), _F32),
                           conv_w_full, jnp.zeros((4, D_RG), _F32)], axis=0)
    wr = _block_diag(w_rgate[0]).astype(_BF)
    wi = _block_diag(w_igate[0]).astype(_BF)

    dh0, (r_win, r_wgu, r_wout, r_wdown), (all_a, all_b, all_c) = _local_step(
        x[0], loss_target[0], meta_full, mix_norm_g, w_in_l, vec, wr, wi, hg_lower_bound, hg_norm_g,
        w_out_l, ffn_norm_g, w_gu_l, w_down_l, final_norm_g.reshape(1, D))
    grad_x = dh0[N_META:N_META + seq][None]

    outs = {}
    outs["w_in"] = _adamw_big("adamw_w_in", r_win, w_in[0], m_w_in[0], v_w_in[0], 256)
    outs["w_gate_up"] = [r.T for r in _adamw_big("adamw_w_gate_up", r_wgu, w_gate_up[0].T, m_w_gate_up[0].T,
                                                 v_w_gate_up[0].T, 176)]
    outs["w_out"] = _adamw_big("adamw_w_out", r_wout, w_out[0], m_w_out[0], v_w_out[0], 128)
    outs["w_down"] = _adamw_big("adamw_w_down", r_wdown, w_down[0], m_w_down[0], v_w_down[0], 176)

    meta_part = lax.dynamic_slice_in_dim(all_a[:, R_META:R_META + N_META, :], me * 128, 128, axis=2)
    convw_part = lax.dynamic_slice_in_dim(all_b[:, R_CONVW:R_CONVW + 4, :], me * 64, 64, axis=2)
    gathered = [all_a, all_b, all_c, meta_part, convw_part]
    small_params = [
        ("meta_tokens", (3, 0, N_META, 0, 128), (meta_tokens, m_meta_tokens, v_meta_tokens), (N_META, 128)),
        ("mix_norm_g", (0, R_GMIX, 1, 0, D), (mix_norm_g, m_mix_norm_g, v_mix_norm_g), (1, D)),
        ("conv_w", (4, 0, 4, 0, 64), (conv_w, m_conv_w, v_conv_w), (4, 64)),
        ("conv_b", (1, R_CONVB, 1, 0, D_RG), (conv_b, m_conv_b, v_conv_b), (1, D_RG)),
        ("w_rgate", (2, 0, 512, 0, 64), (w_rgate, m_w_rgate, v_w_rgate), (512, 64)),
        ("b_rgate", (1, R_BR, 1, 0, D_RG), (b_rgate, m_b_rgate, v_b_rgate), (1, D_RG)),
        ("w_igate", (2, 0, 512, 64, 64), (w_igate, m_w_igate, v_w_igate), (512, 64)),
        ("b_igate", (1, R_BI, 1, 0, D_RG), (b_igate, m_b_igate, v_b_igate), (1, D_RG)),
        ("lru_lambda", (1, R_LAM, 1, 0, D_RG), (lru_lambda, m_lru_lambda, v_lru_lambda), (1, D_RG)),
        ("rg_norm_g", (1, R_GRG, 1, 0, D_RG), (rg_norm_g, m_rg_norm_g, v_rg_norm_g), (1, D_RG)),
        ("hg_lower_bound", (1, R_HB0, 2, 0, D_HG), (hg_lower_bound, m_hg_lower_bound, v_hg_lower_bound), (2, D_HG)),
        ("hg_norm_g", (1, R_GHG, 1, 0, HD), (hg_norm_g, m_hg_norm_g, v_hg_norm_g), (1, HD)),
        ("ffn_norm_g", (0, R_GFFN, 1, 0, D), (ffn_norm_g, m_ffn_norm_g, v_ffn_norm_g), (1, D)),
        ("final_norm_g", (0, R_GFIN, 1, 0, D), (final_norm_g, m_final_norm_g, v_final_norm_g), (1, D)),
    ]
    res = _adamw_small(gathered, [s[1] for s in small_params],
                       [tuple(t.reshape(s[3]) for t in s[2]) for s in small_params])
    for i, s in enumerate(small_params):
        outs[s[0]] = [r.reshape(s[2][0].shape) for r in res[4 * i:4 * i + 4]]
    for n, ref in (("w_in", w_in), ("w_gate_up", w_gate_up), ("w_out", w_out), ("w_down", w_down)):
        outs[n] = [r.reshape(ref.shape) for r in outs[n]]

    loss_all = res[4 * len(small_params)][0, 0]
    order = ["meta_tokens", "mix_norm_g", "w_in", "conv_w", "conv_b", "w_rgate", "b_rgate", "w_igate", "b_igate",
             "lru_lambda", "rg_norm_g", "hg_lower_bound", "hg_norm_g", "w_out", "ffn_norm_g", "w_gate_up", "w_down",
             "final_norm_g"]
    return (loss_all, grad_x, *[outs[n][0] for n in order], *[outs[n][1] for n in order],
            *[outs[n][2] for n in order], *[outs[n][3] for n in order])
```

```python
import functools

import jax
import jax.numpy as jnp
from jax import lax
from jax.experimental import pallas as pl
from jax.experimental.pallas import tpu as pltpu

_BF = jnp.bfloat16
_F32 = jnp.float32
_S = jax.ShapeDtypeStruct
_MESH = pl.DeviceIdType.MESH

N_DEV = 8
N_META = 16
D = 1024
D_RG = 512
D_HG = 512
HD = 128
NH = D_HG // HD
D_IN = 3072
D_FF = 2816
FFB = D_FF // 4
WIN_B = D_IN // N_DEV
WIN_P = 2 * WIN_B
EPS = 1e-6
LRU_C = 8.0
TM = 256
HC = 64
VMEM_LIMIT = 56 * 1024 * 1024

ADAM_LR = 0.001
ADAM_B1 = 0.9
ADAM_B2 = 0.999
ADAM_EPS = 1e-08
ADAM_WD = 0.01
ADAM_STEP = 10

_SEND_ORDER = (6, 4, 2, 7, 5, 3, 1, 0)

_CHIP_ORDER = (0, 2, 1, 3)

R_CONVB, R_BR, R_BI, R_LAM, R_GRG, R_HB0, R_HB1, R_GHG, R_CONVW = 0, 1, 2, 3, 4, 5, 6, 7, 8
R_GMIX, R_GFFN, R_GFIN, R_LOSS, R_META = 0, 1, 2, 3, 8


def _cp(sem=None, **kw):
    return pltpu.CompilerParams(dimension_semantics=sem, vmem_limit_bytes=VMEM_LIMIT, **kw)


def _dot(a, b):
    return jnp.dot(a, b, preferred_element_type=_F32)


def _dot_nt(a, b):
    return lax.dot_general(a, b, (((1,), (1,)), ((), ())), preferred_element_type=_F32)


def _dot_tn(a, b):
    return lax.dot_general(a, b, (((0,), (0,)), ((), ())), preferred_element_type=_F32)


def _sigmoid(x):
    return jax.nn.sigmoid(x)


def _dsilu(x, s):
    return s * (1.0 + x * (1.0 - s))


_GELU_C = 0.7978845608028654


def _gelu_parts(x):
    t = jnp.tanh(_GELU_C * (x + 0.044715 * (x * x * x)))
    g = 0.5 * x * (1.0 + t)
    dg = 0.5 * (1.0 + t) + 0.5 * x * (1.0 - t * t) * (_GELU_C * (1.0 + 3.0 * 0.044715 * (x * x)))
    return g, dg


def _softplus(z):
    e = jnp.exp(-jnp.abs(z))
    w = 1.0 + e
    l1p = jnp.where(w == 1.0, e, jnp.log(w) * e / jnp.where(w == 1.0, 1.0, w - 1.0))
    return jnp.maximum(z, 0.0) + l1p


def _rms_fwd(x):
    r = lax.rsqrt(jnp.mean(x * x, axis=-1, keepdims=True) + EPS)
    return x * r, r


def _rms_bwd(dyg, n, r):
    return r * (dyg - n * jnp.mean(dyg * n, axis=-1, keepdims=True))


def _full(shape):
    nd = len(shape)
    return pl.BlockSpec(shape, lambda i: (0,) * nd)


def _const(shape):
    nd = len(shape)
    return pl.BlockSpec(shape, lambda i: (0,) * nd, pipeline_mode=pl.Buffered(1))


def _carry_gather(gather, i, nt):
    @pl.when(i == 0)
    def _():
        gather.start()

    def tail():
        for j in range(3):
            @pl.when(i == max(nt - 3 + j, 0))
            def _(j=j):
                gather.forward(j)

        @pl.when(i == nt - 1)
        def _():
            gather.finish()

    return tail


def _pair_place(ref, block):
    return ref.at[block // 2, :, pl.ds(pl.multiple_of((block % 2) * WIN_B, WIN_B), WIN_B)]


def _inproj(h0, g_mix, order, w_in_l):
    t_pad = h0.shape[0]
    nt = 4
    tmi = t_pad // nt

    def body(order_ref, h_ref, g_ref, wl_ref, p_ref, u_ref, wg_ref, u_s, wbuf, wsem, *sems):
        g_w = _Gather([wl_ref], [wg_ref], sems, place=_pair_place)
        j, i = pl.program_id(0), pl.program_id(1)

        @pl.when((j == 0) & (i == 0))
        def _():
            g_w.start()

        @pl.when(i == 0)
        def _():
            for step in range(4):
                @pl.when(j == step)
                def _(step=step):
                    if step == 0:
                        g_w.wait_mine()
                        g_w.wait_sibling()
                    else:
                        g_w.forward(step - 1)
                        g_w.wait_passed(step - 1)
                    cp = pltpu.make_async_copy(wg_ref.at[order_ref[step]], wbuf, wsem.at[0])
                    cp.start()
                    cp.wait()

        @pl.when(j == 0)
        def _():
            n, _ = _rms_fwd(h_ref[...])
            u = (n * g_ref[...]).astype(_BF)
            u_s[i] = u
            u_ref[...] = u

        p_ref[...] = _dot(u_s[i], wbuf[...])

        @pl.when((j == 3) & (i == nt - 1))
        def _():
            g_w.finish_sends(mine=False)

    hbm = pl.BlockSpec(memory_space=pl.ANY)
    return pl.pallas_call(
        body, name="inproj",
        grid_spec=pltpu.PrefetchScalarGridSpec(
            num_scalar_prefetch=1, grid=(4, nt),
            in_specs=[pl.BlockSpec((tmi, D), lambda j, i, order: (jnp.where(j == 0, i, 0), 0)),
                      pl.BlockSpec((1, D), lambda j, i, order: (0, 0)), hbm],
            out_specs=[pl.BlockSpec((tmi, WIN_P), lambda j, i, order: (i, order[j])),
                       pl.BlockSpec((tmi, D), lambda j, i, order: (jnp.where(j == 0, i, nt - 1), 0)), hbm],
            scratch_shapes=[pltpu.VMEM((nt, tmi, D), _BF), pltpu.VMEM((D, WIN_P), _BF), pltpu.SemaphoreType.DMA((1,))]
            + _sem_shapes(1)),
        out_shape=[_S((t_pad, D_IN), _F32), _S((t_pad, D), _BF), _S((4, D, WIN_P), _BF)],
        compiler_params=_cp(("arbitrary", "arbitrary")),
    )(order, h0, g_mix, w_in_l)


def _rg_gates(xc, wr_ref, wi_ref, vec_ref):
    xcb = xc.astype(_BF)
    r = _sigmoid(_dot(xcb, wr_ref[...]) + vec_ref[R_BR:R_BR + 1, :])
    ig = _sigmoid(_dot(xcb, wi_ref[...]) + vec_ref[R_BI:R_BI + 1, :])
    nsp8 = -LRU_C * _softplus(-vec_ref[R_LAM:R_LAM + 1, :])
    la = nsp8 * r
    a = jnp.exp(la)
    th = jnp.tanh(la)
    s = jnp.sqrt(-2.0 * th / (1.0 - th))
    return r, ig, a, s, nsp8


def _conv(xbuf, vec_ref):
    acc = vec_ref[R_CONVW:R_CONVW + 1, :] * xbuf[pl.ds(5, TM), :]
    for j in range(1, 4):
        acc = acc + vec_ref[R_CONVW + j:R_CONVW + j + 1, :] * xbuf[pl.ds(5 + j, TM), :]
    return vec_ref[R_CONVB:R_CONVB + 1, :] + acc


def _dot3(m01, x):
    hi = x.astype(_BF)
    r1 = x - hi.astype(_F32)
    mid = r1.astype(_BF)
    lo = (r1 - mid.astype(_F32)).astype(_BF)
    return (_dot(m01, lo) + _dot(m01, mid)) + _dot(m01, hi)


def _chunk_masks():
    row = lax.broadcasted_iota(jnp.int32, (TM, TM), 0)
    col = lax.broadcasted_iota(jnp.int32, (TM, TM), 1)
    shift = HC.bit_length() - 1
    same = lax.shift_right_logical(row, shift) == lax.shift_right_logical(col, shift)
    return same, same & (row >= col), same & (col >= row)


def _per_chunk_rows(x, r):
    return jnp.concatenate([jnp.broadcast_to(x[HC * c + r:HC * c + r + 1, :], (HC, x.shape[1]))
                            for c in range(TM // HC)], axis=0)


def _hg_prep(p_ref, lb, tri_blk):
    hq = p_ref[:, pl.ds(2 * D_RG, D_HG)]
    hf = p_ref[:, pl.ds(2 * D_RG + D_HG, D_HG)]
    sq = _sigmoid(hq)
    q = hq * sq
    sg = _sigmoid(hf)
    f = lb + (1.0 - lb) * sg
    k = 1.0 - f
    b = _dot3(tri_blk, jnp.log(f))
    bm = _per_chunk_rows(b, HC // 2 - 1)
    bl = _per_chunk_rows(b, HC - 1)
    e_q = jnp.exp(b - bm)
    e_k = jnp.exp(bm - b)
    e_b = jnp.exp(b)
    e_l = jnp.exp(bl - b)
    return dict(hq=hq, sq=sq, q=q, sg=sg, f=f, k=k, e_q=e_q, e_k=e_k, e_b=e_b, e_l=e_l,
                qd=q * e_q, kd=k * e_k, qe=q * e_b, ke=k * e_l, e_end=jnp.exp(bl))


def _mixer_fwd(p, wr, wi, vec, hb, g_hg, shards):
    t_pad = p.shape[0]
    nt = t_pad // TM
    nc_t = TM // HC
    nsh = len(shards)

    def body(p_ref, wr_ref, wi_ref, vec_ref, hb_ref, ghg_ref, *rest):
        sh_refs, rest = rest[:nsh], rest[nsh:]
        y_ref, hs_ref, o_ref, sc_ref = rest[:4]
        gath_refs, rest = rest[4:4 + nsh], rest[4 + nsh:]
        xbuf, a_s, b_s, hcar, st, qd_s, kd_s, qe_s, ke_s, v_s, u_s = rest[:11]
        i = pl.program_id(0)
        tail = _carry_gather(_Gather(sh_refs, gath_refs, rest[11:]), i, nt)

        @pl.when(i == 0)
        def _():
            xbuf[pl.ds(0, 8), :] = jnp.zeros((8, D_RG), _F32)
            hcar[...] = jnp.zeros_like(hcar)
            st[...] = jnp.zeros_like(st)

        x = p_ref[:, pl.ds(0, D_RG)]
        xbuf[pl.ds(8, TM), :] = x
        xc = _conv(xbuf, vec_ref)
        xbuf[pl.ds(0, 8), :] = x[TM - 8:, :]
        r, ig, a, s, _ = _rg_gates(xc, wr_ref, wi_ref, vec_ref)
        a_s[...] = a
        b_s[...] = s * (ig * xc)

        def step(t, h):
            h = a_s[pl.ds(t, 1), :] * h + b_s[pl.ds(t, 1), :]
            hs_ref[pl.ds(t, 1), :] = h
            return h

        hcar[pl.ds(0, 1), :] = lax.fori_loop(0, TM, step, hcar[pl.ds(0, 1), :], unroll=8)
        gel, _ = _gelu_parts(p_ref[:, pl.ds(D_RG, D_RG)])
        n, _ = _rms_fwd(gel * hs_ref[...])
        y_ref[:, pl.ds(0, D_RG)] = (n * vec_ref[R_GRG:R_GRG + 1, :]).astype(_BF)

        lb = _sigmoid(hb_ref[0:1, :] - hb_ref[1:2, :])
        _, tri_blk, _ = _chunk_masks()
        q = _hg_prep(p_ref, lb, tri_blk.astype(_BF))
        for name, ref in (("qd", qd_s), ("kd", kd_s), ("qe", qe_s), ("ke", ke_s)):
            ref[...] = q[name].astype(_BF)
        v_s[...] = p_ref[:, pl.ds(2 * D_RG + 2 * D_HG, D_HG)].astype(_BF)
        e_end = q["e_end"]
        causal = (lax.broadcasted_iota(jnp.int32, (HC, HC), 0) >= lax.broadcasted_iota(jnp.int32, (HC, HC), 1))
        for c in range(nc_t):
            for h in range(NH):
                rs, cs = pl.ds(HC * c, HC), pl.ds(HD * h, HD)
                amat = jnp.where(causal, _dot_nt(qd_s[rs, cs], kd_s[rs, cs]), 0.0)
                o_ref[rs, cs] = _dot(amat.astype(_BF), v_s[rs, cs])
                u_s[NH * c + h] = _dot_tn(v_s[rs, cs], ke_s[rs, cs])
        for h in range(NH):
            cs = pl.ds(HD * h, HD)
            s_run = st[h]
            for c in range(nc_t):
                rs = pl.ds(HC * c, HC)
                sc_ref[c, h] = s_run
                o_ref[rs, cs] += _dot_nt(qe_s[rs, cs], s_run.astype(_BF))
                s_run = e_end[HC * c:HC * c + 1, HD * h:HD * (h + 1)] * s_run + u_s[NH * c + h]
            st[h] = s_run
        for h in range(NH):
            cs = pl.ds(HD * h, HD)
            n_o, _ = _rms_fwd(o_ref[:, cs])
            hg = p_ref[:, pl.ds(2 * D_RG + 3 * D_HG + HD * h, HD)]
            y_ref[:, pl.ds(D_RG + HD * h, HD)] = ((n_o * ghg_ref[...]) * (hg * _sigmoid(hg))).astype(_BF)

        tail()

    hbm = pl.BlockSpec(memory_space=pl.ANY)
    return pl.pallas_call(
        body, name="mixer_fwd", grid=(nt,),
        in_specs=[pl.BlockSpec((TM, D_IN), lambda i: (i, 0)), _full((D_RG, D_RG)), _full((D_RG, D_RG)),
                  _full((16, D_RG)), _full((2, D_HG)), _full((1, HD))] + [hbm] * nsh,
        out_specs=[pl.BlockSpec((TM, D), lambda i: (i, 0)), pl.BlockSpec((TM, D_RG), lambda i: (i, 0)),
                   pl.BlockSpec((TM, D_HG), lambda i: (i, 0)),
                   pl.BlockSpec((nc_t, NH, HD, HD), lambda i: (i, 0, 0, 0))] + [hbm] * nsh,
        out_shape=[_S((t_pad, D), _BF), _S((t_pad, D_RG), _F32), _S((t_pad, D_HG), _F32),
                   _S((t_pad // HC, NH, HD, HD), _F32)] + [_S((N_DEV,) + s.shape, s.dtype) for s in shards],
        scratch_shapes=[pltpu.VMEM((TM + 8, D_RG), _F32), pltpu.VMEM((TM, D_RG), _F32),
                        pltpu.VMEM((TM, D_RG), _F32), pltpu.VMEM((8, D_RG), _F32),
                        pltpu.VMEM((NH, HD, HD), _F32)] + [pltpu.VMEM((TM, D_HG), _BF) for _ in range(5)]
        + [pltpu.VMEM((nc_t * NH, HD, HD), _F32)] + _sem_shapes(nsh),
        compiler_params=_cp(("arbitrary",)),
    )(p, wr, wi, vec, hb, g_hg, *shards)


def _outproj(h0, y, w_out, g_ffn):
    t_pad = h0.shape[0]

    def body(h_ref, y_ref, w_ref, g_ref, h1_ref, v_ref):
        h1 = h_ref[...] + _dot(y_ref[...], w_ref[...])
        h1_ref[...] = h1
        n, _ = _rms_fwd(h1)
        v_ref[...] = (n * g_ref[...]).astype(_BF)

    return pl.pallas_call(
        body, name="outproj", grid=(t_pad // TM,),
        in_specs=[pl.BlockSpec((TM, D), lambda i: (i, 0)), pl.BlockSpec((TM, D), lambda i: (i, 0)),
                  _full((D, D)), _full((1, D))],
        out_specs=[pl.BlockSpec((TM, D), lambda i: (i, 0)), pl.BlockSpec((TM, D), lambda i: (i, 0))],
        out_shape=[_S((t_pad, D), _F32), _S((t_pad, D), _BF)],
        compiler_params=_cp(("arbitrary",)),
    )(h0, y, w_out, g_ffn)


def _ffn_loss(v, h1, w_gu, w_down, g_fin, tgt, n_valid):
    t_pad = v.shape[0]

    def body(v_ref, h1_ref, wgu_ref, wd_ref, g_ref, t_ref, gu_ref, act_ref, dh2_ref, dh2b_ref, loss_ref, gfin_ref):
        i = pl.program_id(0)

        @pl.when(i == 0)
        def _():
            loss_ref[...] = jnp.zeros_like(loss_ref)
            gfin_ref[...] = jnp.zeros_like(gfin_ref)

        vb = v_ref[...]
        h2 = h1_ref[...]
        for b in range(4):
            gate = _dot_nt(vb, wgu_ref[b])
            up = _dot_nt(vb, wgu_ref[4 + b])
            gu_ref[b] = gate
            gu_ref[4 + b] = up
            act = ((gate * _sigmoid(gate)) * up).astype(_BF)
            act_ref[b] = act
            h2 = h2 + _dot(act, wd_ref[b])
        n, r = _rms_fwd(h2)
        out = n * g_ref[...]
        row = i * TM + lax.broadcasted_iota(jnp.int32, (TM, 1), 0)
        valid = (row >= N_META) & (row < n_valid)
        err = jnp.where(valid, out - t_ref[...], 0.0)
        loss_ref[...] += (0.5 / D) * jnp.sum(err * err)
        dout = err * (1.0 / D)
        gfin_ref[...] += jnp.sum(dout * n, axis=0, keepdims=True)
        dh2 = _rms_bwd(dout * g_ref[...], n, r)
        dh2_ref[...] = dh2
        dh2b_ref[...] = dh2.astype(_BF)

    return pl.pallas_call(
        body, name="ffn_loss", grid=(t_pad // TM,),
        in_specs=[pl.BlockSpec((TM, D), lambda i: (i, 0)), pl.BlockSpec((TM, D), lambda i: (i, 0)),
                  _const((N_DEV, FFB, D)), _const((4, FFB, D)), _full((1, D)),
                  pl.BlockSpec((TM, D), lambda i: (i, 0))],
        out_specs=[pl.BlockSpec((N_DEV, TM, FFB), lambda i: (0, i, 0)), pl.BlockSpec((4, TM, FFB), lambda i: (0, i, 0)),
                   pl.BlockSpec((TM, D), lambda i: (i, 0)), pl.BlockSpec((TM, D), lambda i: (i, 0)),
                   _full((8, 128)), _full((1, D))],
        out_shape=[_S((N_DEV, t_pad, FFB), _F32), _S((4, t_pad, FFB), _BF), _S((t_pad, D), _F32),
                   _S((t_pad, D), _BF), _S((8, 128), _F32), _S((1, D), _F32)],
        compiler_params=_cp(("arbitrary",)),
    )(v, h1, w_gu, w_down, g_fin, tgt)


def _ffn_bwd(dh2, dh2b, gu, h1, g_ffn, w_gu, w_down, w_out):
    t_pad = dh2.shape[0]

    def body(dh2_ref, dh2b_ref, gu_ref, h1_ref, g_ref, wgu_ref, wd_ref, wo_ref,
             dgu_ref, dh1_ref, dh1b_ref, dy_ref, gffn_ref):
        i = pl.program_id(0)

        @pl.when(i == 0)
        def _():
            gffn_ref[...] = jnp.zeros_like(gffn_ref)

        db = dh2b_ref[...]
        dv = jnp.zeros((TM, D), _F32)
        for b in range(4):
            dact = _dot_nt(db, wd_ref[b])
            gate = gu_ref[b]
            up = gu_ref[4 + b]
            sg = _sigmoid(gate)
            dgate = ((dact * up) * _dsilu(gate, sg)).astype(_BF)
            dup = (dact * (gate * sg)).astype(_BF)
            dgu_ref[b] = dgate
            dgu_ref[4 + b] = dup
            dv = dv + _dot(dgate, wgu_ref[b]) + _dot(dup, wgu_ref[4 + b])
        n, r = _rms_fwd(h1_ref[...])
        gffn_ref[...] += jnp.sum(dv * n, axis=0, keepdims=True)
        dh1 = dh2_ref[...] + _rms_bwd(dv * g_ref[...], n, r)
        dh1_ref[...] = dh1
        dh1b = dh1.astype(_BF)
        dh1b_ref[...] = dh1b
        dy_ref[...] = _dot_nt(dh1b, wo_ref[...])

    tile = pl.BlockSpec((TM, D), lambda i: (i, 0))
    return pl.pallas_call(
        body, name="ffn_bwd", grid=(t_pad // TM,),
        in_specs=[tile, tile, pl.BlockSpec((N_DEV, TM, FFB), lambda i: (0, i, 0)), tile, _full((1, D)),
                  _const((N_DEV, FFB, D)), _const((4, FFB, D)), _const((D, D))],
        out_specs=[pl.BlockSpec((N_DEV, TM, FFB), lambda i: (0, i, 0)), tile, tile, tile, _full((1, D))],
        out_shape=[_S((N_DEV, t_pad, FFB), _BF), _S((t_pad, D), _F32), _S((t_pad, D), _BF),
                   _S((t_pad, D), _F32), _S((1, D), _F32)],
        compiler_params=_cp(("arbitrary",)),
    )(dh2, dh2b, gu, h1, g_ffn, w_gu, w_down, w_out)


def _mixer_bwd(p, hs, o, sc, dy, wr, wi, vec, hb, g_hg, scatter):
    t_pad = p.shape[0]
    nt = t_pad // TM
    nc_t = TM // HC
    nsc = len(scatter)

    def rev(i):
        return nt - 1 - i

    def body(p_ref, pprev_ref, hs_ref, hprev_ref, o_ref, sc_ref, dy_ref, wr_ref, wi_ref, vec_ref, hb_ref, ghg_ref,
             *rest):
        send_refs, rest = rest[:nsc], rest[nsc:]
        dp_ref, gvec_ref, gw_ref = rest[:3]
        recv_refs, rest = rest[3:3 + nsc], rest[3 + nsc:]
        xbuf, hbuf, dbuf, a_s, g_s, ccar, dst = rest[:7]
        qd_s, kd_s, qe_s, ke_s, v_s, do_s, dqd_s, dkd_s, dqe_s, dke_s, dv_s, w_s, dend_s = rest[7:20]
        exchange = _Exchange(send_refs, [], recv_refs, rest[20:])
        i = pl.program_id(0)
        first_tile = i == nt - 1

        @pl.when(i == 0)
        def _():
            exchange.start()
            gvec_ref[...] = jnp.zeros_like(gvec_ref)
            gw_ref[...] = jnp.zeros_like(gw_ref)
            dbuf[pl.ds(TM, 8), :] = jnp.zeros((8, D_RG), _F32)
            ccar[...] = jnp.zeros_like(ccar)
            dst[...] = jnp.zeros_like(dst)

        def acc(row, val):
            gvec_ref[row:row + 1, :] += jnp.sum(val, axis=0, keepdims=True)

        keep = jnp.where(first_tile, 0.0, 1.0)
        x = p_ref[:, pl.ds(0, D_RG)]
        xbuf[pl.ds(0, 8), :] = pprev_ref[...] * keep
        xbuf[pl.ds(8, TM), :] = x
        xc = _conv(xbuf, vec_ref)
        r, ig, a, s, nsp8 = _rg_gates(xc, wr_ref, wi_ref, vec_ref)
        h = hs_ref[...]
        hbuf[pl.ds(0, 8), :] = hprev_ref[...] * keep
        hbuf[pl.ds(8, TM), :] = h
        hm1 = hbuf[pl.ds(7, TM), :]
        gr = p_ref[:, pl.ds(D_RG, D_RG)]
        gel, dgel = _gelu_parts(gr)
        n, rr = _rms_fwd(gel * h)
        dyn = dy_ref[:, pl.ds(0, D_RG)]
        acc(R_GRG, dyn * n)
        dpre = _rms_bwd(dyn * vec_ref[R_GRG:R_GRG + 1, :], n, rr)
        dp_ref[:, pl.ds(D_RG, D_RG)] = ((dpre * h) * dgel).astype(_BF)
        a_s[...] = a
        g_s[...] = dpre * gel

        def step(k, c):
            t = TM - 1 - k
            g = g_s[pl.ds(t, 1), :] + c
            g_s[pl.ds(t, 1), :] = g
            return a_s[pl.ds(t, 1), :] * g

        ccar[pl.ds(0, 1), :] = lax.fori_loop(0, TM, step, ccar[pl.ds(0, 1), :], unroll=8)
        gt = g_s[...]
        da = gt * hm1
        ixc = ig * xc
        ds = gt * ixc
        dig = (gt * s) * xc
        dxc = (gt * s) * ig
        dla = da * a - ds * ((a * a) / s)
        lam = vec_ref[R_LAM:R_LAM + 1, :]
        gvec_ref[R_LAM:R_LAM + 1, :] += jnp.sum(dla * r, axis=0, keepdims=True) * (LRU_C * _sigmoid(-lam))
        dzr = (dla * nsp8) * (r * (1.0 - r))
        dzi = dig * (ig * (1.0 - ig))
        acc(R_BR, dzr)
        acc(R_BI, dzi)
        xcb = xc.astype(_BF)
        dzrb = dzr.astype(_BF)
        dzib = dzi.astype(_BF)
        gw_ref[0] += _dot_tn(xcb, dzrb)
        gw_ref[1] += _dot_tn(xcb, dzib)
        dxc = dxc + _dot_nt(dzrb, wr_ref[...]) + _dot_nt(dzib, wi_ref[...])
        acc(R_CONVB, dxc)
        for j in range(4):
            acc(R_CONVW + j, dxc * xbuf[pl.ds(5 + j, TM), :])
        dbuf[pl.ds(0, TM), :] = dxc
        dx = vec_ref[R_CONVW + 3:R_CONVW + 4, :] * dxc
        for j in range(3):
            dx = dx + vec_ref[R_CONVW + j:R_CONVW + j + 1, :] * dbuf[pl.ds(3 - j, TM), :]
        dbuf[pl.ds(TM, 8), :] = dxc[0:8, :]
        dp_ref[:, pl.ds(0, D_RG)] = dx.astype(_BF)

        lb = _sigmoid(hb_ref[0:1, :] - hb_ref[1:2, :])
        same, tri_blk, triu_blk = _chunk_masks()
        q = _hg_prep(p_ref, lb, tri_blk.astype(_BF))
        qdb, kdb = q["qd"].astype(_BF), q["kd"].astype(_BF)
        qd_s[...] = qdb
        kd_s[...] = kdb
        qe_s[...] = q["qe"].astype(_BF)
        ke_s[...] = q["ke"].astype(_BF)
        v_s[...] = p_ref[:, pl.ds(2 * D_RG + 2 * D_HG, D_HG)].astype(_BF)
        e_end = q["e_end"]
        ghg = ghg_ref[...]
        for h in range(NH):
            cs = pl.ds(HD * h, HD)
            hg = p_ref[:, pl.ds(2 * D_RG + 3 * D_HG + HD * h, HD)]
            sh = _sigmoid(hg)
            n_o, r_o = _rms_fwd(o_ref[:, cs])
            dyh = dy_ref[:, pl.ds(D_RG + HD * h, HD)]
            dp_ref[:, pl.ds(2 * D_RG + 3 * D_HG + HD * h, HD)] = ((dyh * (n_o * ghg)) * _dsilu(hg, sh)).astype(_BF)
            dn = dyh * (hg * sh)
            gvec_ref[R_GHG:R_GHG + 1, pl.ds(0, HD)] += jnp.sum(dn * n_o, axis=0, keepdims=True)
            do_s[:, cs] = _rms_bwd(dn * ghg, n_o, r_o).astype(_BF)
        causal = (lax.broadcasted_iota(jnp.int32, (HC, HC), 0) >= lax.broadcasted_iota(jnp.int32, (HC, HC), 1))
        for c in range(nc_t):
            for h in range(NH):
                rs, cs = pl.ds(HC * c, HC), pl.ds(HD * h, HD)
                qd_c, kd_c, do_c = qd_s[rs, cs], kd_s[rs, cs], do_s[rs, cs]
                amat = jnp.where(causal, _dot_nt(qd_c, kd_c), 0.0).astype(_BF)
                da_m = jnp.where(causal, _dot_nt(do_c, v_s[rs, cs]), 0.0).astype(_BF)
                dqd_s[rs, cs] = _dot(da_m, kd_c)
                dkd_s[rs, cs] = _dot_tn(da_m, qd_c)
                dqe_s[rs, cs] = _dot(do_c, sc_ref[c, h].astype(_BF))
                dv_s[rs, cs] = _dot_tn(amat, do_c)
                w_s[NH * c + h] = _dot_tn(do_c, qe_s[rs, cs])
        for h in range(NH):
            cs = pl.ds(HD * h, HD)
            d_run = dst[h]
            for c in reversed(range(nc_t)):
                rs = pl.ds(HC * c, HC)
                d_b = d_run.astype(_BF)
                dke_s[rs, cs] = _dot(v_s[rs, cs], d_b)
                dp_ref[rs, pl.ds(2 * D_RG + 2 * D_HG + HD * h, HD)] = (
                    dv_s[rs, cs] + _dot_nt(ke_s[rs, cs], d_b)).astype(_BF)
                dend_s[pl.ds(c, 1), cs] = jnp.sum(sc_ref[c, h] * d_run, axis=0, keepdims=True)
                d_run = w_s[NH * c + h] + e_end[HC * c:HC * c + 1, HD * h:HD * (h + 1)] * d_run
            dst[h] = d_run
        dqd, dkd, dqe, dke = dqd_s[...], dkd_s[...], dqe_s[...], dke_s[...]
        dq = dqd * q["e_q"] + dqe * q["e_b"]
        dk = dkd * q["e_k"] + dke * q["e_l"]
        dkeke = dke * q["ke"]
        db = dqd * qdb.astype(_F32) - dkd * kdb.astype(_F32) + dqe * q["qe"] - dkeke
        d_end = jnp.concatenate([jnp.broadcast_to(dend_s[pl.ds(c, 1), :], (HC, D_HG)) for c in range(nc_t)], axis=0)
        dlf = _dot3(triu_blk.astype(_BF), db) + _dot3(same.astype(_BF), dkeke) + d_end * e_end
        df = dlf / q["f"] - dk
        sg = q["sg"]
        gvec_ref[R_HB0:R_HB0 + 1, :] += jnp.sum(df * (1.0 - sg), axis=0, keepdims=True)
        dp_ref[:, pl.ds(2 * D_RG, D_HG)] = (dq * _dsilu(q["hq"], q["sq"])).astype(_BF)
        dp_ref[:, pl.ds(2 * D_RG + D_HG, D_HG)] = ((df * (1.0 - lb)) * (sg * (1.0 - sg))).astype(_BF)

        @pl.when(i == nt - 1)
        def _():
            glb = gvec_ref[R_HB0:R_HB0 + 1, :] * (lb * (1.0 - lb))
            gvec_ref[R_HB0:R_HB0 + 1, :] = glb
            gvec_ref[R_HB1:R_HB1 + 1, :] = -glb
            exchange.finish()

    hbm = pl.BlockSpec(memory_space=pl.ANY)
    return pl.pallas_call(
        body, name="mixer_bwd", grid=(nt,),
        in_specs=[pl.BlockSpec((TM, D_IN), lambda i: (rev(i), 0)),
                  pl.BlockSpec((8, D_RG), lambda i: (jnp.maximum(rev(i) * (TM // 8) - 1, 0), 0)),
                  pl.BlockSpec((TM, D_RG), lambda i: (rev(i), 0)),
                  pl.BlockSpec((8, D_RG), lambda i: (jnp.maximum(rev(i) * (TM // 8) - 1, 0), 0)),
                  pl.BlockSpec((TM, D_HG), lambda i: (rev(i), 0)),
                  pl.BlockSpec((nc_t, NH, HD, HD), lambda i: (rev(i), 0, 0, 0)),
                  pl.BlockSpec((TM, D), lambda i: (rev(i), 0)),
                  _full((D_RG, D_RG)), _full((D_RG, D_RG)), _full((16, D_RG)), _full((2, D_HG)), _full((1, HD))]
        + [hbm] * nsc,
        out_specs=[pl.BlockSpec((TM, D_IN), lambda i: (rev(i), 0)), _full((16, D_RG)), _full((2, D_RG, D_RG))]
        + [hbm] * nsc,
        out_shape=[_S((t_pad, D_IN), _BF), _S((16, D_RG), _F32), _S((2, D_RG, D_RG), _F32)]
        + [_S(s.shape, s.dtype) for s in scatter],
        scratch_shapes=[pltpu.VMEM((TM + 8, D_RG), _F32), pltpu.VMEM((TM + 8, D_RG), _F32),
                        pltpu.VMEM((TM + 8, D_RG), _F32), pltpu.VMEM((TM, D_RG), _F32),
                        pltpu.VMEM((TM, D_RG), _F32), pltpu.VMEM((8, D_RG), _F32),
                        pltpu.VMEM((NH, HD, HD), _F32)]
        + [pltpu.VMEM((TM, D_HG), _BF) for _ in range(6)] + [pltpu.VMEM((TM, D_HG), _F32) for _ in range(5)]
        + [pltpu.VMEM((nc_t * NH, HD, HD), _F32), pltpu.VMEM((8, D_HG), _F32)] + _sem_shapes(nsc),
        compiler_params=_cp(("arbitrary",)),
    )(p, p, hs, hs, o, sc, dy, wr, wi, vec, hb, g_hg, *scatter)


def _inproj_bwd_send(dp, w_in, h0, dh1, g_mix, u, order, gffn, gfin, loss, to_all):
    t_pad = dp.shape[0]
    rb = t_pad // (2 * N_DEV)
    n_steps = N_DEV + 2 * N_DEV
    na = len(to_all)

    def body(order_ref, dpc_ref, dpr_ref, u_ref, w_ref, h_ref, dh1_ref, g_ref, gffn_ref, gfin_ref, loss_ref, *rest):
        all_in = rest[:na]
        dh0_ref, recv_ref = rest[na:na + 2]
        all_out = rest[na + 2:2 * na + 2]
        alla_ref = rest[2 * na + 2]
        buf, pack, blk_send, blk_recv, blk_local = rest[2 * na + 3:2 * na + 8]
        exchange = _Exchange([], all_in, all_out, rest[2 * na + 8:2 * na + 11])
        last = _Exchange([], [pack], [alla_ref], rest[2 * na + 11:])
        s = pl.program_id(0)
        x, y, c = _coords()
        me = 4 * x + 2 * y + c

        def send(step):
            r = _SEND_ORDER[step]
            return pltpu.make_async_remote_copy(
                src_ref=buf.at[step], dst_ref=recv_ref.at[me], send_sem=blk_send.at[step], recv_sem=blk_recv.at[r - 1],
                device_id=(x ^ (r >> 2), y ^ ((r >> 1) & 1), c ^ (r & 1)), device_id_type=_MESH)

        @pl.when(s == 0)
        def _():
            exchange.start()
            pack[...] = jnp.zeros_like(pack)

        @pl.when(s < N_DEV)
        def _():
            buf[s] = _dot_tn(u_ref[...], dpc_ref[...]).astype(_BF)

            for step in range(N_DEV - 1):
                @pl.when(s == step)
                def _(step=step):
                    send(step).start()

        @pl.when(s >= N_DEV)
        def _():
            du = jnp.zeros((rb, D), _F32)
            for j in range(4):
                du = du + _dot_nt(dpr_ref[:, WIN_P * j:WIN_P * (j + 1)], w_ref[j])
            n, r = _rms_fwd(h_ref[...])
            pack[R_GMIX:R_GMIX + 1, :] += jnp.sum(du * n, axis=0, keepdims=True)
            dh0 = dh1_ref[...] + _rms_bwd(du * g_ref[...], n, r)
            dh0_ref[...] = dh0

            @pl.when(s == N_DEV)
            def _():
                pack[R_META:R_META + N_META, :] = dh0[0:N_META, :]

        @pl.when(s == n_steps - 1)
        def _():
            pack[R_GFFN:R_GFFN + 1, :] = gffn_ref[...]
            pack[R_GFIN:R_GFIN + 1, :] = gfin_ref[...]
            pack[R_LOSS:R_LOSS + 1, pl.ds(0, 128)] = loss_ref[0:1, :]
            last.start()
            mine = pltpu.make_async_copy(buf.at[N_DEV - 1], recv_ref.at[me], blk_local.at[0])
            mine.start()
            for step in range(N_DEV - 1):
                send(step).wait_send()
            for r in range(1, N_DEV):
                px, py, pc = x ^ (r >> 2), y ^ ((r >> 1) & 1), c ^ (r & 1)
                pltpu.make_async_remote_copy(
                    src_ref=buf.at[0], dst_ref=recv_ref.at[4 * px + 2 * py + pc], send_sem=blk_send.at[0],
                    recv_sem=blk_recv.at[r - 1], device_id=(px, py, pc), device_id_type=_MESH).wait_recv()
            mine.wait()
            exchange.finish()
            last.finish()

    hbm = pl.BlockSpec(memory_space=pl.ANY)
    rows = pl.BlockSpec((rb, D), lambda s, order: (jnp.maximum(s - N_DEV, 0), 0))
    one = pl.BlockSpec((1, D), lambda s, order: (0, 0))
    res = pl.pallas_call(
        body, name="inproj_bwd_send",
        grid_spec=pltpu.PrefetchScalarGridSpec(
            num_scalar_prefetch=1, grid=(n_steps,),
            in_specs=[pl.BlockSpec((t_pad, WIN_B), lambda s, order: (0, order[jnp.minimum(s, N_DEV - 1)])),
                      pl.BlockSpec((rb, D_IN), lambda s, order: (jnp.maximum(s - N_DEV, 0), 0)),
                      pl.BlockSpec((t_pad, D), lambda s, order: (0, 0), pipeline_mode=pl.Buffered(1)),
                      pl.BlockSpec((4, D, WIN_P), lambda s, order: (0, 0, 0), pipeline_mode=pl.Buffered(1)),
                      rows, rows, one, one, one, pl.BlockSpec((8, 128), lambda s, order: (0, 0))] + [hbm] * na,
            out_specs=[rows] + [hbm] * (na + 2),
            scratch_shapes=[pltpu.VMEM((N_DEV, D, WIN_B), _BF), pltpu.VMEM((24, D), _F32),
                            pltpu.SemaphoreType.DMA((N_DEV - 1,)), pltpu.SemaphoreType.DMA((N_DEV - 1,)),
                            pltpu.SemaphoreType.DMA((1,))] + _sem_shapes(na) + _sem_shapes(1)),
        out_shape=[_S((t_pad, D), _F32), _S((N_DEV, D, WIN_B), _BF)]
        + [_S((N_DEV,) + g.shape, g.dtype) for g in to_all] + [_S((N_DEV, 24, D), _F32)],
        compiler_params=_cp(("arbitrary",)),
    )(order, dp, dp, u, w_in, h0, dh1, g_mix, gffn, gfin, loss, *to_all)
    return res


def _wgrad(name, a, b, a_spec, b_spec, n_blocks, out_block, scatter=()):
    nsc = len(scatter)

    def body(a_ref, b_ref, *rest):
        o_ref = rest[nsc]
        j = pl.program_id(0)
        if nsc:
            exchange = _Exchange(rest[:nsc], [], rest[nsc + 1:2 * nsc + 1], rest[2 * nsc + 1:])

            @pl.when(j == 0)
            def _():
                exchange.start()

        av = a_ref[0] if len(a_ref.shape) == 3 else a_ref[...]
        bv = b_ref[0] if len(b_ref.shape) == 3 else b_ref[...]
        o_ref[0] = _dot_tn(av, bv).astype(_BF)

        if nsc:
            @pl.when(j == n_blocks - 1)
            def _():
                exchange.finish()

    hbm = pl.BlockSpec(memory_space=pl.ANY)
    res = pl.pallas_call(
        body, name=name, grid=(n_blocks,),
        in_specs=[a_spec, b_spec] + [hbm] * nsc,
        out_specs=[pl.BlockSpec((1,) + out_block, lambda j: (j, 0, 0))] + [hbm] * nsc,
        out_shape=[_S((n_blocks,) + out_block, _BF)] + [_S(s.shape, s.dtype) for s in scatter],
        scratch_shapes=_sem_shapes(nsc) if nsc else [],
        compiler_params=_cp(("arbitrary",)),
    )(a, b, *scatter)
    return res if nsc else res[0]


def _coords():
    return lax.axis_index("x"), lax.axis_index("y"), lax.axis_index("c")


def _sem_shapes(na):
    return [pltpu.SemaphoreType.DMA((7 * na,)), pltpu.SemaphoreType.DMA((7 * na,)), pltpu.SemaphoreType.DMA((na,))]


class _Gather:
    def __init__(self, srcs, outs, sems, place=None):
        self.srcs, self.outs = srcs, outs
        self.send_sems, self.recv_sems, self.local_sems = sems
        self.place = place if place is not None else (lambda ref, block: ref.at[block])
        self.na = len(srcs)
        x, y, c = _coords()
        self.pos = (x, y, c)
        self.me = 4 * x + 2 * y + c
        self.sibling = (x, y, 1 - c)
        self.chips = [(1 - x, y), (x, 1 - y), (1 - x, 1 - y)]

    @staticmethod
    def _slot(px, py, pc):
        return 4 * px + 2 * py + pc

    def _copy(self, a, k, block, to, own=False):
        dst = self.place(self.outs[a], block)
        return pltpu.make_async_remote_copy(
            src_ref=self.srcs[a] if own else dst, dst_ref=dst,
            send_sem=self.send_sems.at[7 * a + k], recv_sem=self.recv_sems.at[7 * a + k],
            device_id=to, device_id_type=_MESH)

    def _mine(self, a):
        return pltpu.make_async_copy(self.srcs[a], self.place(self.outs[a], self.me), self.local_sems.at[a])

    def _first(self):
        c = self.pos[2]
        cps = []
        for a in range(self.na):
            cps.append(self._copy(a, 0, self.me, self.sibling, own=True))
            cps += [self._copy(a, 1 + j, self.me, (*chip, c), own=True) for j, chip in enumerate(self.chips)]
        return cps

    def _passed(self):
        c = self.pos[2]
        return [self._copy(a, 4 + j, self._slot(*chip, c), self.sibling)
                for j, chip in enumerate(self.chips) for a in range(self.na)]

    def start(self):
        for a in range(self.na):
            self._mine(a).start()
        for cp in self._first():
            cp.start()

    def forward(self, j):
        c = self.pos[2]
        chip = self.chips[j]
        for a in range(self.na):
            self._copy(a, 1 + j, self._slot(*chip, c), self.pos).wait_recv()
            self._copy(a, 4 + j, self._slot(*chip, c), self.sibling).start()

    def wait_sibling(self):
        x, y, c = self.pos
        for a in range(self.na):
            self._copy(a, 0, self._slot(x, y, 1 - c), self.pos).wait_recv()

    def wait_passed(self, j):
        c = self.pos[2]
        for a in range(self.na):
            self._copy(a, 4 + j, self._slot(*self.chips[j], 1 - c), self.pos).wait_recv()

    def wait_mine(self):
        for a in range(self.na):
            self._mine(a).wait()

    def finish_sends(self, mine=True):
        for cp in self._first() + self._passed():
            cp.wait_send()
        if mine:
            self.wait_mine()

    def finish(self):
        self.wait_sibling()
        for j in range(3):
            self.wait_passed(j)
        self.finish_sends()


class _Exchange:
    def __init__(self, scatter, gather, outs, sems):
        self.ins = list(scatter) + list(gather)
        self.ns, self.na = len(scatter), len(scatter) + len(gather)
        self.outs = outs
        self.send_sems, self.recv_sems, self.local_sems = sems
        x, y, c = _coords()
        self.pos = (x, y, c)
        self.me = 4 * x + 2 * y + c

    def _peer(self, r):
        x, y, c = self.pos
        return x ^ (r >> 2), y ^ ((r >> 1) & 1), c ^ (r & 1)

    def _src(self, a, block):
        return self.ins[a].at[block] if a < self.ns else self.ins[a]

    def _local(self, a):
        return pltpu.make_async_copy(self._src(a, self.me), self.outs[a].at[self.me], self.local_sems.at[a])

    def _send(self, a, r):
        px, py, pc = self._peer(r)
        return pltpu.make_async_remote_copy(
            src_ref=self._src(a, 4 * px + 2 * py + pc), dst_ref=self.outs[a].at[self.me],
            send_sem=self.send_sems.at[7 * a + r - 1], recv_sem=self.recv_sems.at[7 * a + r - 1],
            device_id=(px, py, pc), device_id_type=_MESH)

    def _recv(self, a, r):
        px, py, pc = self._peer(r)
        return pltpu.make_async_remote_copy(
            src_ref=self._src(a, self.me), dst_ref=self.outs[a].at[4 * px + 2 * py + pc],
            send_sem=self.send_sems.at[7 * a + r - 1], recv_sem=self.recv_sems.at[7 * a + r - 1],
            device_id=(px, py, pc), device_id_type=_MESH)

    def start(self):
        for a in range(self.na):
            self._local(a).start()
        for r in range(1, N_DEV):
            for a in range(self.na):
                self._send(a, r).start()

    def finish(self):
        for r in range(1, N_DEV):
            for a in range(self.na):
                self._recv(a, r).wait_recv()
        for r in range(1, N_DEV):
            for a in range(self.na):
                self._send(a, r).wait_send()
        for a in range(self.na):
            self._local(a).wait()


def _allgather_first(gather_f32, cast_f32, gather_dtypes):
    ng, nc = len(gather_f32), len(cast_f32)

    def body(*refs):
        ins, cins = refs[:ng], refs[ng:ng + nc]
        outs, couts = refs[ng + nc:2 * ng + nc], refs[2 * ng + nc:2 * ng + 2 * nc]
        stage = refs[2 * ng + 2 * nc:3 * ng + 2 * nc]
        sems = refs[3 * ng + 2 * nc:]
        for a in range(ng):
            stage[a][...] = ins[a][...].astype(gather_dtypes[a])
        g = _Gather(stage, outs, sems)
        g.start()
        for a in range(nc):
            couts[a][...] = cins[a][...].astype(_BF)
        for j in range(3):
            g.forward(j)
        g.finish()

    vm = pl.BlockSpec(memory_space=pltpu.VMEM)
    return pl.pallas_call(
        body, name="allgather_first",
        in_specs=[vm] * (ng + nc),
        out_specs=[pl.BlockSpec(memory_space=pl.ANY)] * ng + [vm] * nc,
        out_shape=[_S((N_DEV,) + l.shape, dt) for l, dt in zip(gather_f32, gather_dtypes)]
        + [_S(l.shape, _BF) for l in cast_f32],
        scratch_shapes=[pltpu.VMEM(l.shape, dt) for l, dt in zip(gather_f32, gather_dtypes)] + _sem_shapes(ng),
        compiler_params=pltpu.CompilerParams(vmem_limit_bytes=VMEM_LIMIT),
    )(*gather_f32, *cast_f32)


def _adamw_math(w, g, m, v):
    m2 = ADAM_B1 * m + (1.0 - ADAM_B1) * g
    v2 = ADAM_B2 * v + (1.0 - ADAM_B2) * (g * g)
    m_hat = m2 / (1.0 - ADAM_B1 ** ADAM_STEP)
    v_hat = v2 / (1.0 - ADAM_B2 ** ADAM_STEP)
    delta = -ADAM_LR * (m_hat / (jnp.sqrt(v_hat) + ADAM_EPS) + ADAM_WD * w)
    return delta, m2, v2


def _adamw_big(name, recv, w, m, v, rows):
    r_all, c_all = w.shape

    def body(r_ref, w_ref, m_ref, v_ref, g_out, d_out, m_out, v_out):
        g = r_ref[0].astype(_F32)
        for k in range(1, N_DEV):
            g = g + r_ref[k].astype(_F32)
        delta, m2, v2 = _adamw_math(w_ref[...], g, m_ref[...], v_ref[...])
        g_out[...] = g
        d_out[...] = delta
        m_out[...] = m2
        v_out[...] = v2

    tile = pl.BlockSpec((rows, c_all), lambda i: (i, 0))
    return pl.pallas_call(
        body, name=name, grid=(r_all // rows,),
        in_specs=[pl.BlockSpec((N_DEV, rows, c_all), lambda i: (0, i, 0)), tile, tile, tile],
        out_specs=[tile] * 4,
        out_shape=[_S(w.shape, _F32)] * 4,
        compiler_params=_cp(("arbitrary",)),
    )(recv, w, m, v)


def _adamw_small(gathered, slices, wmv):
    ng, npar = len(gathered), len(slices)

    def body(*refs):
        g_refs = refs[:ng]
        wmv_refs = refs[ng:ng + 3 * npar]
        outs = refs[ng + 3 * npar:]
        for i, (ai, r0, nr, c0, ncol) in enumerate(slices):
            g = g_refs[ai][0, pl.ds(r0, nr), pl.ds(c0, ncol)].astype(_F32)
            for k in range(1, N_DEV):
                g = g + g_refs[ai][k, pl.ds(r0, nr), pl.ds(c0, ncol)].astype(_F32)
            w_ref, m_ref, v_ref = wmv_refs[3 * i:3 * i + 3]
            delta, m2, v2 = _adamw_math(w_ref[...], g, m_ref[...], v_ref[...])
            outs[4 * i][...] = g
            outs[4 * i + 1][...] = delta
            outs[4 * i + 2][...] = m2
            outs[4 * i + 3][...] = v2
        total = g_refs[0][0, pl.ds(R_LOSS, 1), pl.ds(0, 128)]
        for k in range(1, N_DEV):
            total = total + g_refs[0][k, pl.ds(R_LOSS, 1), pl.ds(0, 128)]
        outs[4 * npar][...] = total

    flat = [t for trip in wmv for t in trip]
    out_shape = []
    for w, _, _ in wmv:
        out_shape += [_S(w.shape, _F32)] * 4
    out_shape.append(_S((1, 128), _F32))
    return pl.pallas_call(
        body, name="adamw_small", out_shape=out_shape,
        compiler_params=pltpu.CompilerParams(vmem_limit_bytes=VMEM_LIMIT),
    )(*gathered, *flat)


def _block_diag(w):
    eye = jnp.eye(8, dtype=w.dtype)
    return (w[:, :, None, :] * eye[:, None, :, None]).reshape(D_RG, D_RG)


def _diag_blocks(g):
    return jnp.concatenate([g[64 * h:64 * (h + 1), 64 * h:64 * (h + 1)] for h in range(8)], axis=0)


def _local_step(x, tgt, meta, g_mix, w_in_l, vec, wr, wi, hb, g_hg, w_out_l, g_ffn, w_gu_l, w_down_l, g_fin):
    seq = x.shape[0]
    n_valid = N_META + seq
    t_pad = -(-n_valid // TM) * TM
    h0 = jnp.concatenate([meta, x, jnp.zeros((t_pad - n_valid, D), _F32)], axis=0)
    tgt_p = jnp.concatenate([jnp.zeros((N_META, D), _F32), tgt, jnp.zeros((t_pad - n_valid, D), _F32)], axis=0)

    me = 4 * lax.axis_index("x") + 2 * lax.axis_index("y") + lax.axis_index("c")
    p, u, w_in = _inproj(h0, g_mix, ((me >> 1) ^ jnp.array(_CHIP_ORDER, jnp.int32)).astype(jnp.int32), w_in_l)
    y, hs, o, sc, w_out, w_gu, w_down = _mixer_fwd(p, wr, wi, vec, hb, g_hg, [w_out_l, w_gu_l, w_down_l])
    w_out = w_out.reshape(D, D)
    w_down = w_down.reshape(4, FFB, D)
    h1, v = _outproj(h0, y, w_out, g_ffn)
    gu, act, dh2, dh2b, loss, gfin = _ffn_loss(v, h1, w_gu, w_down, g_fin, tgt_p, n_valid)

    dgu, dh1, dh1b, dy, gffn = _ffn_bwd(dh2, dh2b, gu, h1, g_ffn, w_gu, w_down, w_out)
    g_wdown = _wgrad("wgrad_down", act, dh2b, pl.BlockSpec((1, t_pad, FFB), lambda j: (j, 0, 0)),
                     pl.BlockSpec((t_pad, D), lambda j: (0, 0)), 4, (FFB, D))
    g_wgu, r_wdown = _wgrad("wgrad_gate_up", dgu, v, pl.BlockSpec((1, t_pad, FFB), lambda j: (j, 0, 0)),
                            pl.BlockSpec((t_pad, D), lambda j: (0, 0)), N_DEV, (FFB, D),
                            scatter=[g_wdown.reshape(N_DEV, D_FF // N_DEV, D)])
    g_wout = _wgrad("wgrad_out", y, dh1b, pl.BlockSpec((t_pad, D // N_DEV), lambda j: (0, j)),
                    pl.BlockSpec((t_pad, D), lambda j: (0, 0)), N_DEV, (D // N_DEV, D))
    dp, gvec, gw, r_wgu, r_wout = _mixer_bwd(p, hs, o, sc, dy, wr, wi, vec, hb, g_hg, [g_wgu, g_wout])
    pack_c = jnp.concatenate([_diag_blocks(gw[0]), _diag_blocks(gw[1])], axis=1).astype(_BF)
    order = (me ^ jnp.array(_SEND_ORDER, jnp.int32)).astype(jnp.int32)
    dh0, r_win, all_b, all_c, all_a = _inproj_bwd_send(dp, w_in, h0, dh1, g_mix, u, order, gffn, gfin, loss,
                                                       [gvec, pack_c])
    return dh0, (r_win, r_wgu, r_wout, r_wdown), (all_a, all_b, all_c)


def kernel(x, meta_tokens, mix_norm_g, w_in, conv_w, conv_b, w_rgate, b_rgate, w_igate, b_igate, lru_lambda, rg_norm_g, hg_lower_bound, hg_norm_g, w_out, ffn_norm_g, w_gate_up, w_down, final_norm_g, loss_target, m_meta_tokens, m_mix_norm_g, m_w_in, m_conv_w, m_conv_b, m_w_rgate, m_b_rgate, m_w_igate, m_b_igate, m_lru_lambda, m_rg_norm_g, m_hg_lower_bound, m_hg_norm_g, m_w_out, m_ffn_norm_g, m_w_gate_up, m_w_down, m_final_norm_g, v_meta_tokens, v_mix_norm_g, v_w_in, v_conv_w, v_conv_b, v_w_rgate, v_b_rgate, v_w_igate, v_b_igate, v_lru_lambda, v_rg_norm_g, v_hg_lower_bound, v_hg_norm_g, v_w_out, v_ffn_norm_g, v_w_gate_up, v_w_down, v_final_norm_g):
    seq = x.shape[1]
    me = 4 * lax.axis_index("x") + 2 * lax.axis_index("y") + lax.axis_index("c")

    small_l = jnp.concatenate([meta_tokens, jnp.pad(conv_w[0], ((0, 4), (0, 64)))], axis=0)
    small_g, w_in_l, w_gu_l, w_out_l, w_down_l = _allgather_first(
        [small_l], [w_in[0], w_gate_up[0].T, w_out[0], w_down[0]], [_F32])
    meta_full = jnp.transpose(small_g[:, :N_META, :], (1, 0, 2)).reshape(N_META, D)
    conv_w_full = jnp.transpose(small_g[:, N_META:N_META + 4, :64], (1, 0, 2)).reshape(4, D_RG)
    vec = jnp.concatenate([conv_b, b_rgate, b_igate, lru_lambda, rg_norm_g, jnp.zeros((3, D_RG), _F32),
                           conv_w_full, jnp.zeros((4, D_RG), _F32)], axis=0)
    wr = _block_diag(w_rgate[0]).astype(_BF)
    wi = _block_diag(w_igate[0]).astype(_BF)

    dh0, (r_win, r_wgu, r_wout, r_wdown), (all_a, all_b, all_c) = _local_step(
        x[0], loss_target[0], meta_full, mix_norm_g, w_in_l, vec, wr, wi, hg_lower_bound, hg_norm_g,
        w_out_l, ffn_norm_g, w_gu_l, w_down_l, final_norm_g.reshape(1, D))
    grad_x = dh0[N_META:N_META + seq][None]

    outs = {}
    outs["w_in"] = _adamw_big("adamw_w_in", r_win, w_in[0], m_w_in[0], v_w_in[0], 256)
    outs["w_gate_up"] = [r.T for r in _adamw_big("adamw_w_gate_up", r_wgu, w_gate_up[0].T, m_w_gate_up[0].T,
                                                 v_w_gate_up[0].T, 176)]
    outs["w_out"] = _adamw_big("adamw_w_out", r_wout, w_out[0], m_w_out[0], v_w_out[0], 128)
    outs["w_down"] = _adamw_big("adamw_w_down", r_wdown, w_down[0], m_w_down[0], v_w_down[0], 176)

    meta_part = lax.dynamic_slice_in_dim(all_a[:, R_META:R_META + N_META, :], me * 128, 128, axis=2)
    convw_part = lax.dynamic_slice_in_dim(all_b[:, R_CONVW:R_CONVW + 4, :], me * 64, 64, axis=2)
    gathered = [all_a, all_b, all_c, meta_part, convw_part]
    small_params = [
        ("meta_tokens", (3, 0, N_META, 0, 128), (meta_tokens, m_meta_tokens, v_meta_tokens), (N_META, 128)),
        ("mix_norm_g", (0, R_GMIX, 1, 0, D), (mix_norm_g, m_mix_norm_g, v_mix_norm_g), (1, D)),
        ("conv_w", (4, 0, 4, 0, 64), (conv_w, m_conv_w, v_conv_w), (4, 64)),
        ("conv_b", (1, R_CONVB, 1, 0, D_RG), (conv_b, m_conv_b, v_conv_b), (1, D_RG)),
        ("w_rgate", (2, 0, 512, 0, 64), (w_rgate, m_w_rgate, v_w_rgate), (512, 64)),
        ("b_rgate", (1, R_BR, 1, 0, D_RG), (b_rgate, m_b_rgate, v_b_rgate), (1, D_RG)),
        ("w_igate", (2, 0, 512, 64, 64), (w_igate, m_w_igate, v_w_igate), (512, 64)),
        ("b_igate", (1, R_BI, 1, 0, D_RG), (b_igate, m_b_igate, v_b_igate), (1, D_RG)),
        ("lru_lambda", (1, R_LAM, 1, 0, D_RG), (lru_lambda, m_lru_lambda, v_lru_lambda), (1, D_RG)),
        ("rg_norm_g", (1, R_GRG, 1, 0, D_RG), (rg_norm_g, m_rg_norm_g, v_rg_norm_g), (1, D_RG)),
        ("hg_lower_bound", (1, R_HB0, 2, 0, D_HG), (hg_lower_bound, m_hg_lower_bound, v_hg_lower_bound), (2, D_HG)),
        ("hg_norm_g", (1, R_GHG, 1, 0, HD), (hg_norm_g, m_hg_norm_g, v_hg_norm_g), (1, HD)),
        ("ffn_norm_g", (0, R_GFFN, 1, 0, D), (ffn_norm_g, m_ffn_norm_g, v_ffn_norm_g), (1, D)),
        ("final_norm_g", (0, R_GFIN, 1, 0, D), (final_norm_g, m_final_norm_g, v_final_norm_g), (1, D)),
    ]
    res = _adamw_small(gathered, [s[1] for s in small_params],
                       [tuple(t.reshape(s[3]) for t in s[2]) for s in small_params])
    for i, s in enumerate(small_params):
        outs[s[0]] = [r.reshape(s[2][0].shape) for r in res[4 * i:4 * i + 4]]
    for n, ref in (("w_in", w_in), ("w_gate_up", w_gate_up), ("w_out", w_out), ("w_down", w_down)):
        outs[n] = [r.reshape(ref.shape) for r in outs[n]]

    loss_all = res[4 * len(small_params)][0, 0]
    order = ["meta_tokens", "mix_norm_g", "w_in", "conv_w", "conv_b", "w_rgate", "b_rgate", "w_igate", "b_igate",
             "lru_lambda", "rg_norm_g", "hg_lower_bound", "hg_norm_g", "w_out", "ffn_norm_g", "w_gate_up", "w_down",
             "final_norm_g"]
    return (loss_all, grad_x, *[outs[n][0] for n in order], *[outs[n][1] for n in order],
            *[outs[n][2] for n in order], *[outs[n][3] for n in order])
```

```python
import functools

import jax
import jax.numpy as jnp
from jax import lax
from jax.experimental import pallas as pl
from jax.experimental.pallas import tpu as pltpu

_BF = jnp.bfloat16
_F32 = jnp.float32
_S = jax.ShapeDtypeStruct
_MESH = pl.DeviceIdType.MESH

N_DEV = 8
N_META = 16
D = 1024
D_RG = 512
D_HG = 512
HD = 128
NH = D_HG // HD
D_IN = 3072
D_FF = 2816
FFB = D_FF // 4
WIN_B = D_IN // N_DEV
WIN_P = 2 * WIN_B
EPS = 1e-6
LRU_C = 8.0
TM = 256
HC = 64
VMEM_LIMIT = 56 * 1024 * 1024

ADAM_LR = 0.001
ADAM_B1 = 0.9
ADAM_B2 = 0.999
ADAM_EPS = 1e-08
ADAM_WD = 0.01
ADAM_STEP = 10

_SEND_ORDER = (6, 4, 2, 7, 5, 3, 1, 0)

R_CONVB, R_BR, R_BI, R_LAM, R_GRG, R_HB0, R_HB1, R_GHG, R_CONVW = 0, 1, 2, 3, 4, 5, 6, 7, 8
R_GMIX, R_GFFN, R_GFIN, R_LOSS, R_META = 0, 1, 2, 3, 8


def _cp(sem=None, **kw):
    return pltpu.CompilerParams(dimension_semantics=sem, vmem_limit_bytes=VMEM_LIMIT, **kw)


def _dot(a, b):
    return jnp.dot(a, b, preferred_element_type=_F32)


def _dot_nt(a, b):
    return lax.dot_general(a, b, (((1,), (1,)), ((), ())), preferred_element_type=_F32)


def _dot_tn(a, b):
    return lax.dot_general(a, b, (((0,), (0,)), ((), ())), preferred_element_type=_F32)


def _sigmoid(x):
    return jax.nn.sigmoid(x)


def _dsilu(x, s):
    return s * (1.0 + x * (1.0 - s))


_GELU_C = 0.7978845608028654


def _gelu_parts(x):
    t = jnp.tanh(_GELU_C * (x + 0.044715 * (x * x * x)))
    g = 0.5 * x * (1.0 + t)
    dg = 0.5 * (1.0 + t) + 0.5 * x * (1.0 - t * t) * (_GELU_C * (1.0 + 3.0 * 0.044715 * (x * x)))
    return g, dg


def _softplus(z):
    e = jnp.exp(-jnp.abs(z))
    w = 1.0 + e
    l1p = jnp.where(w == 1.0, e, jnp.log(w) * e / jnp.where(w == 1.0, 1.0, w - 1.0))
    return jnp.maximum(z, 0.0) + l1p


def _rms_fwd(x):
    r = lax.rsqrt(jnp.mean(x * x, axis=-1, keepdims=True) + EPS)
    return x * r, r


def _rms_bwd(dyg, n, r):
    return r * (dyg - n * jnp.mean(dyg * n, axis=-1, keepdims=True))


def _full(shape):
    nd = len(shape)
    return pl.BlockSpec(shape, lambda i: (0,) * nd)


def _const(shape):
    nd = len(shape)
    return pl.BlockSpec(shape, lambda i: (0,) * nd, pipeline_mode=pl.Buffered(1))


def _carry_gather(gather, i, nt):
    @pl.when(i == 0)
    def _():
        gather.start()

    def tail():
        for j in range(3):
            @pl.when(i == max(nt - 3 + j, 0))
            def _(j=j):
                gather.forward(j)

        @pl.when(i == nt - 1)
        def _():
            gather.finish()

    return tail


def _pair_place(ref, block):
    return ref.at[block // 2, :, pl.ds(pl.multiple_of((block % 2) * WIN_B, WIN_B), WIN_B)]


def _inproj(h0, g_mix, w_in, shards):
    t_pad = h0.shape[0]
    nt = t_pad // TM
    nsh = len(shards)

    def body(h_ref, g_ref, w_ref, *rest):
        p_ref, u_ref = rest[nsh:nsh + 2]
        tail = _carry_gather(_Gather(rest[:nsh], rest[nsh + 2:2 * nsh + 2], rest[2 * nsh + 2:]), pl.program_id(0), nt)
        n, _ = _rms_fwd(h_ref[...])
        u = (n * g_ref[...]).astype(_BF)
        u_ref[...] = u
        for j in range(4):
            p_ref[:, WIN_P * j:WIN_P * (j + 1)] = _dot(u, w_ref[j])
        tail()

    hbm = pl.BlockSpec(memory_space=pl.ANY)
    return pl.pallas_call(
        body, name="inproj", grid=(nt,),
        in_specs=[pl.BlockSpec((TM, D), lambda i: (i, 0)), _full((1, D)), _const((4, D, WIN_P))] + [hbm] * nsh,
        out_specs=[pl.BlockSpec((TM, D_IN), lambda i: (i, 0)), pl.BlockSpec((TM, D), lambda i: (i, 0))] + [hbm] * nsh,
        out_shape=[_S((t_pad, D_IN), _F32), _S((t_pad, D), _BF)] + [_S((N_DEV,) + s.shape, s.dtype) for s in shards],
        scratch_shapes=_sem_shapes(nsh),
        compiler_params=_cp(("arbitrary",)),
    )(h0, g_mix, w_in, *shards)


def _rg_gates(xc, wr_ref, wi_ref, vec_ref):
    xcb = xc.astype(_BF)
    r = _sigmoid(_dot(xcb, wr_ref[...]) + vec_ref[R_BR:R_BR + 1, :])
    ig = _sigmoid(_dot(xcb, wi_ref[...]) + vec_ref[R_BI:R_BI + 1, :])
    nsp8 = -LRU_C * _softplus(-vec_ref[R_LAM:R_LAM + 1, :])
    la = nsp8 * r
    a = jnp.exp(la)
    th = jnp.tanh(la)
    s = jnp.sqrt(-2.0 * th / (1.0 - th))
    return r, ig, a, s, nsp8


def _conv(xbuf, vec_ref):
    acc = vec_ref[R_CONVW:R_CONVW + 1, :] * xbuf[pl.ds(5, TM), :]
    for j in range(1, 4):
        acc = acc + vec_ref[R_CONVW + j:R_CONVW + j + 1, :] * xbuf[pl.ds(5 + j, TM), :]
    return vec_ref[R_CONVB:R_CONVB + 1, :] + acc


def _dot3(m01, x):
    hi = x.astype(_BF)
    r1 = x - hi.astype(_F32)
    mid = r1.astype(_BF)
    lo = (r1 - mid.astype(_F32)).astype(_BF)
    return (_dot(m01, lo) + _dot(m01, mid)) + _dot(m01, hi)


def _chunk_masks():
    row = lax.broadcasted_iota(jnp.int32, (TM, TM), 0)
    col = lax.broadcasted_iota(jnp.int32, (TM, TM), 1)
    shift = HC.bit_length() - 1
    same = lax.shift_right_logical(row, shift) == lax.shift_right_logical(col, shift)
    return same, same & (row >= col), same & (col >= row)


def _per_chunk_rows(x, r):
    return jnp.concatenate([jnp.broadcast_to(x[HC * c + r:HC * c + r + 1, :], (HC, x.shape[1]))
                            for c in range(TM // HC)], axis=0)


def _hg_prep(p_ref, lb, tri_blk):
    hq = p_ref[:, pl.ds(2 * D_RG, D_HG)]
    hf = p_ref[:, pl.ds(2 * D_RG + D_HG, D_HG)]
    sq = _sigmoid(hq)
    q = hq * sq
    sg = _sigmoid(hf)
    f = lb + (1.0 - lb) * sg
    k = 1.0 - f
    b = _dot3(tri_blk, jnp.log(f))
    bm = _per_chunk_rows(b, HC // 2 - 1)
    bl = _per_chunk_rows(b, HC - 1)
    e_q = jnp.exp(b - bm)
    e_k = jnp.exp(bm - b)
    e_b = jnp.exp(b)
    e_l = jnp.exp(bl - b)
    return dict(hq=hq, sq=sq, q=q, sg=sg, f=f, k=k, e_q=e_q, e_k=e_k, e_b=e_b, e_l=e_l,
                qd=q * e_q, kd=k * e_k, qe=q * e_b, ke=k * e_l, e_end=jnp.exp(bl))


def _mixer_fwd(p, wr, wi, vec, hb, g_hg, shards):
    t_pad = p.shape[0]
    nt = t_pad // TM
    nc_t = TM // HC
    nsh = len(shards)

    def body(p_ref, wr_ref, wi_ref, vec_ref, hb_ref, ghg_ref, *rest):
        sh_refs, rest = rest[:nsh], rest[nsh:]
        y_ref, hs_ref, o_ref, sc_ref = rest[:4]
        gath_refs, rest = rest[4:4 + nsh], rest[4 + nsh:]
        xbuf, a_s, b_s, hcar, st, qd_s, kd_s, qe_s, ke_s, v_s, u_s = rest[:11]
        i = pl.program_id(0)
        tail = _carry_gather(_Gather(sh_refs, gath_refs, rest[11:]), i, nt)

        @pl.when(i == 0)
        def _():
            xbuf[pl.ds(0, 8), :] = jnp.zeros((8, D_RG), _F32)
            hcar[...] = jnp.zeros_like(hcar)
            st[...] = jnp.zeros_like(st)

        x = p_ref[:, pl.ds(0, D_RG)]
        xbuf[pl.ds(8, TM), :] = x
        xc = _conv(xbuf, vec_ref)
        xbuf[pl.ds(0, 8), :] = x[TM - 8:, :]
        r, ig, a, s, _ = _rg_gates(xc, wr_ref, wi_ref, vec_ref)
        a_s[...] = a
        b_s[...] = s * (ig * xc)

        def step(t, h):
            h = a_s[pl.ds(t, 1), :] * h + b_s[pl.ds(t, 1), :]
            hs_ref[pl.ds(t, 1), :] = h
            return h

        hcar[pl.ds(0, 1), :] = lax.fori_loop(0, TM, step, hcar[pl.ds(0, 1), :], unroll=8)
        gel, _ = _gelu_parts(p_ref[:, pl.ds(D_RG, D_RG)])
        n, _ = _rms_fwd(gel * hs_ref[...])
        y_ref[:, pl.ds(0, D_RG)] = (n * vec_ref[R_GRG:R_GRG + 1, :]).astype(_BF)

        lb = _sigmoid(hb_ref[0:1, :] - hb_ref[1:2, :])
        _, tri_blk, _ = _chunk_masks()
        q = _hg_prep(p_ref, lb, tri_blk.astype(_BF))
        for name, ref in (("qd", qd_s), ("kd", kd_s), ("qe", qe_s), ("ke", ke_s)):
            ref[...] = q[name].astype(_BF)
        v_s[...] = p_ref[:, pl.ds(2 * D_RG + 2 * D_HG, D_HG)].astype(_BF)
        e_end = q["e_end"]
        causal = (lax.broadcasted_iota(jnp.int32, (HC, HC), 0) >= lax.broadcasted_iota(jnp.int32, (HC, HC), 1))
        for c in range(nc_t):
            for h in range(NH):
                rs, cs = pl.ds(HC * c, HC), pl.ds(HD * h, HD)
                amat = jnp.where(causal, _dot_nt(qd_s[rs, cs], kd_s[rs, cs]), 0.0)
                o_ref[rs, cs] = _dot(amat.astype(_BF), v_s[rs, cs])
                u_s[NH * c + h] = _dot_tn(v_s[rs, cs], ke_s[rs, cs])
        for h in range(NH):
            cs = pl.ds(HD * h, HD)
            s_run = st[h]
            for c in range(nc_t):
                rs = pl.ds(HC * c, HC)
                sc_ref[c, h] = s_run
                o_ref[rs, cs] += _dot_nt(qe_s[rs, cs], s_run.astype(_BF))
                s_run = e_end[HC * c:HC * c + 1, HD * h:HD * (h + 1)] * s_run + u_s[NH * c + h]
            st[h] = s_run
        for h in range(NH):
            cs = pl.ds(HD * h, HD)
            n_o, _ = _rms_fwd(o_ref[:, cs])
            hg = p_ref[:, pl.ds(2 * D_RG + 3 * D_HG + HD * h, HD)]
            y_ref[:, pl.ds(D_RG + HD * h, HD)] = ((n_o * ghg_ref[...]) * (hg * _sigmoid(hg))).astype(_BF)

        tail()

    hbm = pl.BlockSpec(memory_space=pl.ANY)
    return pl.pallas_call(
        body, name="mixer_fwd", grid=(nt,),
        in_specs=[pl.BlockSpec((TM, D_IN), lambda i: (i, 0)), _full((D_RG, D_RG)), _full((D_RG, D_RG)),
                  _full((16, D_RG)), _full((2, D_HG)), _full((1, HD))] + [hbm] * nsh,
        out_specs=[pl.BlockSpec((TM, D), lambda i: (i, 0)), pl.BlockSpec((TM, D_RG), lambda i: (i, 0)),
                   pl.BlockSpec((TM, D_HG), lambda i: (i, 0)),
                   pl.BlockSpec((nc_t, NH, HD, HD), lambda i: (i, 0, 0, 0))] + [hbm] * nsh,
        out_shape=[_S((t_pad, D), _BF), _S((t_pad, D_RG), _F32), _S((t_pad, D_HG), _F32),
                   _S((t_pad // HC, NH, HD, HD), _F32)] + [_S((N_DEV,) + s.shape, s.dtype) for s in shards],
        scratch_shapes=[pltpu.VMEM((TM + 8, D_RG), _F32), pltpu.VMEM((TM, D_RG), _F32),
                        pltpu.VMEM((TM, D_RG), _F32), pltpu.VMEM((8, D_RG), _F32),
                        pltpu.VMEM((NH, HD, HD), _F32)] + [pltpu.VMEM((TM, D_HG), _BF) for _ in range(5)]
        + [pltpu.VMEM((nc_t * NH, HD, HD), _F32)] + _sem_shapes(nsh),
        compiler_params=_cp(("arbitrary",)),
    )(p, wr, wi, vec, hb, g_hg, *shards)


def _outproj(h0, y, w_out, g_ffn):
    t_pad = h0.shape[0]

    def body(h_ref, y_ref, w_ref, g_ref, h1_ref, v_ref):
        h1 = h_ref[...] + _dot(y_ref[...], w_ref[...])
        h1_ref[...] = h1
        n, _ = _rms_fwd(h1)
        v_ref[...] = (n * g_ref[...]).astype(_BF)

    return pl.pallas_call(
        body, name="outproj", grid=(t_pad // TM,),
        in_specs=[pl.BlockSpec((TM, D), lambda i: (i, 0)), pl.BlockSpec((TM, D), lambda i: (i, 0)),
                  _full((D, D)), _full((1, D))],
        out_specs=[pl.BlockSpec((TM, D), lambda i: (i, 0)), pl.BlockSpec((TM, D), lambda i: (i, 0))],
        out_shape=[_S((t_pad, D), _F32), _S((t_pad, D), _BF)],
        compiler_params=_cp(("arbitrary",)),
    )(h0, y, w_out, g_ffn)


def _ffn_loss(v, h1, w_gu, w_down, g_fin, tgt, n_valid):
    t_pad = v.shape[0]

    def body(v_ref, h1_ref, wgu_ref, wd_ref, g_ref, t_ref, gu_ref, act_ref, dh2_ref, dh2b_ref, loss_ref, gfin_ref):
        i = pl.program_id(0)

        @pl.when(i == 0)
        def _():
            loss_ref[...] = jnp.zeros_like(loss_ref)
            gfin_ref[...] = jnp.zeros_like(gfin_ref)

        vb = v_ref[...]
        h2 = h1_ref[...]
        for b in range(4):
            gate = _dot_nt(vb, wgu_ref[b])
            up = _dot_nt(vb, wgu_ref[4 + b])
            gu_ref[b] = gate
            gu_ref[4 + b] = up
            act = ((gate * _sigmoid(gate)) * up).astype(_BF)
            act_ref[b] = act
            h2 = h2 + _dot(act, wd_ref[b])
        n, r = _rms_fwd(h2)
        out = n * g_ref[...]
        row = i * TM + lax.broadcasted_iota(jnp.int32, (TM, 1), 0)
        valid = (row >= N_META) & (row < n_valid)
        err = jnp.where(valid, out - t_ref[...], 0.0)
        loss_ref[...] += (0.5 / D) * jnp.sum(err * err)
        dout = err * (1.0 / D)
        gfin_ref[...] += jnp.sum(dout * n, axis=0, keepdims=True)
        dh2 = _rms_bwd(dout * g_ref[...], n, r)
        dh2_ref[...] = dh2
        dh2b_ref[...] = dh2.astype(_BF)

    return pl.pallas_call(
        body, name="ffn_loss", grid=(t_pad // TM,),
        in_specs=[pl.BlockSpec((TM, D), lambda i: (i, 0)), pl.BlockSpec((TM, D), lambda i: (i, 0)),
                  _const((N_DEV, FFB, D)), _const((4, FFB, D)), _full((1, D)),
                  pl.BlockSpec((TM, D), lambda i: (i, 0))],
        out_specs=[pl.BlockSpec((N_DEV, TM, FFB), lambda i: (0, i, 0)), pl.BlockSpec((4, TM, FFB), lambda i: (0, i, 0)),
                   pl.BlockSpec((TM, D), lambda i: (i, 0)), pl.BlockSpec((TM, D), lambda i: (i, 0)),
                   _full((8, 128)), _full((1, D))],
        out_shape=[_S((N_DEV, t_pad, FFB), _F32), _S((4, t_pad, FFB), _BF), _S((t_pad, D), _F32),
                   _S((t_pad, D), _BF), _S((8, 128), _F32), _S((1, D), _F32)],
        compiler_params=_cp(("arbitrary",)),
    )(v, h1, w_gu, w_down, g_fin, tgt)


def _ffn_bwd(dh2, dh2b, gu, h1, g_ffn, w_gu, w_down, w_out):
    t_pad = dh2.shape[0]

    def body(dh2_ref, dh2b_ref, gu_ref, h1_ref, g_ref, wgu_ref, wd_ref, wo_ref,
             dgu_ref, dh1_ref, dh1b_ref, dy_ref, gffn_ref):
        i = pl.program_id(0)

        @pl.when(i == 0)
        def _():
            gffn_ref[...] = jnp.zeros_like(gffn_ref)

        db = dh2b_ref[...]
        dv = jnp.zeros((TM, D), _F32)
        for b in range(4):
            dact = _dot_nt(db, wd_ref[b])
            gate = gu_ref[b]
            up = gu_ref[4 + b]
            sg = _sigmoid(gate)
            dgate = ((dact * up) * _dsilu(gate, sg)).astype(_BF)
            dup = (dact * (gate * sg)).astype(_BF)
            dgu_ref[b] = dgate
            dgu_ref[4 + b] = dup
            dv = dv + _dot(dgate, wgu_ref[b]) + _dot(dup, wgu_ref[4 + b])
        n, r = _rms_fwd(h1_ref[...])
        gffn_ref[...] += jnp.sum(dv * n, axis=0, keepdims=True)
        dh1 = dh2_ref[...] + _rms_bwd(dv * g_ref[...], n, r)
        dh1_ref[...] = dh1
        dh1b = dh1.astype(_BF)
        dh1b_ref[...] = dh1b
        dy_ref[...] = _dot_nt(dh1b, wo_ref[...])

    tile = pl.BlockSpec((TM, D), lambda i: (i, 0))
    return pl.pallas_call(
        body, name="ffn_bwd", grid=(t_pad // TM,),
        in_specs=[tile, tile, pl.BlockSpec((N_DEV, TM, FFB), lambda i: (0, i, 0)), tile, _full((1, D)),
                  _const((N_DEV, FFB, D)), _const((4, FFB, D)), _const((D, D))],
        out_specs=[pl.BlockSpec((N_DEV, TM, FFB), lambda i: (0, i, 0)), tile, tile, tile, _full((1, D))],
        out_shape=[_S((N_DEV, t_pad, FFB), _BF), _S((t_pad, D), _F32), _S((t_pad, D), _BF),
                   _S((t_pad, D), _F32), _S((1, D), _F32)],
        compiler_params=_cp(("arbitrary",)),
    )(dh2, dh2b, gu, h1, g_ffn, w_gu, w_down, w_out)


def _mixer_bwd(p, hs, o, sc, dy, wr, wi, vec, hb, g_hg, scatter):
    t_pad = p.shape[0]
    nt = t_pad // TM
    nc_t = TM // HC
    nsc = len(scatter)

    def rev(i):
        return nt - 1 - i

    def body(p_ref, pprev_ref, hs_ref, hprev_ref, o_ref, sc_ref, dy_ref, wr_ref, wi_ref, vec_ref, hb_ref, ghg_ref,
             *rest):
        send_refs, rest = rest[:nsc], rest[nsc:]
        dp_ref, gvec_ref, gw_ref = rest[:3]
        recv_refs, rest = rest[3:3 + nsc], rest[3 + nsc:]
        xbuf, hbuf, dbuf, a_s, g_s, ccar, dst = rest[:7]
        qd_s, kd_s, qe_s, ke_s, v_s, do_s, dqd_s, dkd_s, dqe_s, dke_s, dv_s, w_s, dend_s = rest[7:20]
        exchange = _Exchange(send_refs, [], recv_refs, rest[20:])
        i = pl.program_id(0)
        first_tile = i == nt - 1

        @pl.when(i == 0)
        def _():
            exchange.start()
            gvec_ref[...] = jnp.zeros_like(gvec_ref)
            gw_ref[...] = jnp.zeros_like(gw_ref)
            dbuf[pl.ds(TM, 8), :] = jnp.zeros((8, D_RG), _F32)
            ccar[...] = jnp.zeros_like(ccar)
            dst[...] = jnp.zeros_like(dst)

        def acc(row, val):
            gvec_ref[row:row + 1, :] += jnp.sum(val, axis=0, keepdims=True)

        keep = jnp.where(first_tile, 0.0, 1.0)
        x = p_ref[:, pl.ds(0, D_RG)]
        xbuf[pl.ds(0, 8), :] = pprev_ref[...] * keep
        xbuf[pl.ds(8, TM), :] = x
        xc = _conv(xbuf, vec_ref)
        r, ig, a, s, nsp8 = _rg_gates(xc, wr_ref, wi_ref, vec_ref)
        h = hs_ref[...]
        hbuf[pl.ds(0, 8), :] = hprev_ref[...] * keep
        hbuf[pl.ds(8, TM), :] = h
        hm1 = hbuf[pl.ds(7, TM), :]
        gr = p_ref[:, pl.ds(D_RG, D_RG)]
        gel, dgel = _gelu_parts(gr)
        n, rr = _rms_fwd(gel * h)
        dyn = dy_ref[:, pl.ds(0, D_RG)]
        acc(R_GRG, dyn * n)
        dpre = _rms_bwd(dyn * vec_ref[R_GRG:R_GRG + 1, :], n, rr)
        dp_ref[:, pl.ds(D_RG, D_RG)] = ((dpre * h) * dgel).astype(_BF)
        a_s[...] = a
        g_s[...] = dpre * gel

        def step(k, c):
            t = TM - 1 - k
            g = g_s[pl.ds(t, 1), :] + c
            g_s[pl.ds(t, 1), :] = g
            return a_s[pl.ds(t, 1), :] * g

        ccar[pl.ds(0, 1), :] = lax.fori_loop(0, TM, step, ccar[pl.ds(0, 1), :], unroll=8)
        gt = g_s[...]
        da = gt * hm1
        ixc = ig * xc
        ds = gt * ixc
        dig = (gt * s) * xc
        dxc = (gt * s) * ig
        dla = da * a - ds * ((a * a) / s)
        lam = vec_ref[R_LAM:R_LAM + 1, :]
        gvec_ref[R_LAM:R_LAM + 1, :] += jnp.sum(dla * r, axis=0, keepdims=True) * (LRU_C * _sigmoid(-lam))
        dzr = (dla * nsp8) * (r * (1.0 - r))
        dzi = dig * (ig * (1.0 - ig))
        acc(R_BR, dzr)
        acc(R_BI, dzi)
        xcb = xc.astype(_BF)
        dzrb = dzr.astype(_BF)
        dzib = dzi.astype(_BF)
        gw_ref[0] += _dot_tn(xcb, dzrb)
        gw_ref[1] += _dot_tn(xcb, dzib)
        dxc = dxc + _dot_nt(dzrb, wr_ref[...]) + _dot_nt(dzib, wi_ref[...])
        acc(R_CONVB, dxc)
        for j in range(4):
            acc(R_CONVW + j, dxc * xbuf[pl.ds(5 + j, TM), :])
        dbuf[pl.ds(0, TM), :] = dxc
        dx = vec_ref[R_CONVW + 3:R_CONVW + 4, :] * dxc
        for j in range(3):
            dx = dx + vec_ref[R_CONVW + j:R_CONVW + j + 1, :] * dbuf[pl.ds(3 - j, TM), :]
        dbuf[pl.ds(TM, 8), :] = dxc[0:8, :]
        dp_ref[:, pl.ds(0, D_RG)] = dx.astype(_BF)

        lb = _sigmoid(hb_ref[0:1, :] - hb_ref[1:2, :])
        same, tri_blk, triu_blk = _chunk_masks()
        q = _hg_prep(p_ref, lb, tri_blk.astype(_BF))
        qdb, kdb = q["qd"].astype(_BF), q["kd"].astype(_BF)
        qd_s[...] = qdb
        kd_s[...] = kdb
        qe_s[...] = q["qe"].astype(_BF)
        ke_s[...] = q["ke"].astype(_BF)
        v_s[...] = p_ref[:, pl.ds(2 * D_RG + 2 * D_HG, D_HG)].astype(_BF)
        e_end = q["e_end"]
        ghg = ghg_ref[...]
        for h in range(NH):
            cs = pl.ds(HD * h, HD)
            hg = p_ref[:, pl.ds(2 * D_RG + 3 * D_HG + HD * h, HD)]
            sh = _sigmoid(hg)
            n_o, r_o = _rms_fwd(o_ref[:, cs])
            dyh = dy_ref[:, pl.ds(D_RG + HD * h, HD)]
            dp_ref[:, pl.ds(2 * D_RG + 3 * D_HG + HD * h, HD)] = ((dyh * (n_o * ghg)) * _dsilu(hg, sh)).astype(_BF)
            dn = dyh * (hg * sh)
            gvec_ref[R_GHG:R_GHG + 1, pl.ds(0, HD)] += jnp.sum(dn * n_o, axis=0, keepdims=True)
            do_s[:, cs] = _rms_bwd(dn * ghg, n_o, r_o).astype(_BF)
        causal = (lax.broadcasted_iota(jnp.int32, (HC, HC), 0) >= lax.broadcasted_iota(jnp.int32, (HC, HC), 1))
        for c in range(nc_t):
            for h in range(NH):
                rs, cs = pl.ds(HC * c, HC), pl.ds(HD * h, HD)
                qd_c, kd_c, do_c = qd_s[rs, cs], kd_s[rs, cs], do_s[rs, cs]
                amat = jnp.where(causal, _dot_nt(qd_c, kd_c), 0.0).astype(_BF)
                da_m = jnp.where(causal, _dot_nt(do_c, v_s[rs, cs]), 0.0).astype(_BF)
                dqd_s[rs, cs] = _dot(da_m, kd_c)
                dkd_s[rs, cs] = _dot_tn(da_m, qd_c)
                dqe_s[rs, cs] = _dot(do_c, sc_ref[c, h].astype(_BF))
                dv_s[rs, cs] = _dot_tn(amat, do_c)
                w_s[NH * c + h] = _dot_tn(do_c, qe_s[rs, cs])
        for h in range(NH):
            cs = pl.ds(HD * h, HD)
            d_run = dst[h]
            for c in reversed(range(nc_t)):
                rs = pl.ds(HC * c, HC)
                d_b = d_run.astype(_BF)
                dke_s[rs, cs] = _dot(v_s[rs, cs], d_b)
                dp_ref[rs, pl.ds(2 * D_RG + 2 * D_HG + HD * h, HD)] = (
                    dv_s[rs, cs] + _dot_nt(ke_s[rs, cs], d_b)).astype(_BF)
                dend_s[pl.ds(c, 1), cs] = jnp.sum(sc_ref[c, h] * d_run, axis=0, keepdims=True)
                d_run = w_s[NH * c + h] + e_end[HC * c:HC * c + 1, HD * h:HD * (h + 1)] * d_run
            dst[h] = d_run
        dqd, dkd, dqe, dke = dqd_s[...], dkd_s[...], dqe_s[...], dke_s[...]
        dq = dqd * q["e_q"] + dqe * q["e_b"]
        dk = dkd * q["e_k"] + dke * q["e_l"]
        dkeke = dke * q["ke"]
        db = dqd * qdb.astype(_F32) - dkd * kdb.astype(_F32) + dqe * q["qe"] - dkeke
        d_end = jnp.concatenate([jnp.broadcast_to(dend_s[pl.ds(c, 1), :], (HC, D_HG)) for c in range(nc_t)], axis=0)
        dlf = _dot3(triu_blk.astype(_BF), db) + _dot3(same.astype(_BF), dkeke) + d_end * e_end
        df = dlf / q["f"] - dk
        sg = q["sg"]
        gvec_ref[R_HB0:R_HB0 + 1, :] += jnp.sum(df * (1.0 - sg), axis=0, keepdims=True)
        dp_ref[:, pl.ds(2 * D_RG, D_HG)] = (dq * _dsilu(q["hq"], q["sq"])).astype(_BF)
        dp_ref[:, pl.ds(2 * D_RG + D_HG, D_HG)] = ((df * (1.0 - lb)) * (sg * (1.0 - sg))).astype(_BF)

        @pl.when(i == nt - 1)
        def _():
            glb = gvec_ref[R_HB0:R_HB0 + 1, :] * (lb * (1.0 - lb))
            gvec_ref[R_HB0:R_HB0 + 1, :] = glb
            gvec_ref[R_HB1:R_HB1 + 1, :] = -glb
            exchange.finish()

    hbm = pl.BlockSpec(memory_space=pl.ANY)
    return pl.pallas_call(
        body, name="mixer_bwd", grid=(nt,),
        in_specs=[pl.BlockSpec((TM, D_IN), lambda i: (rev(i), 0)),
                  pl.BlockSpec((8, D_RG), lambda i: (jnp.maximum(rev(i) * (TM // 8) - 1, 0), 0)),
                  pl.BlockSpec((TM, D_RG), lambda i: (rev(i), 0)),
                  pl.BlockSpec((8, D_RG), lambda i: (jnp.maximum(rev(i) * (TM // 8) - 1, 0), 0)),
                  pl.BlockSpec((TM, D_HG), lambda i: (rev(i), 0)),
                  pl.BlockSpec((nc_t, NH, HD, HD), lambda i: (rev(i), 0, 0, 0)),
                  pl.BlockSpec((TM, D), lambda i: (rev(i), 0)),
                  _full((D_RG, D_RG)), _full((D_RG, D_RG)), _full((16, D_RG)), _full((2, D_HG)), _full((1, HD))]
        + [hbm] * nsc,
        out_specs=[pl.BlockSpec((TM, D_IN), lambda i: (rev(i), 0)), _full((16, D_RG)), _full((2, D_RG, D_RG))]
        + [hbm] * nsc,
        out_shape=[_S((t_pad, D_IN), _BF), _S((16, D_RG), _F32), _S((2, D_RG, D_RG), _F32)]
        + [_S(s.shape, s.dtype) for s in scatter],
        scratch_shapes=[pltpu.VMEM((TM + 8, D_RG), _F32), pltpu.VMEM((TM + 8, D_RG), _F32),
                        pltpu.VMEM((TM + 8, D_RG), _F32), pltpu.VMEM((TM, D_RG), _F32),
                        pltpu.VMEM((TM, D_RG), _F32), pltpu.VMEM((8, D_RG), _F32),
                        pltpu.VMEM((NH, HD, HD), _F32)]
        + [pltpu.VMEM((TM, D_HG), _BF) for _ in range(6)] + [pltpu.VMEM((TM, D_HG), _F32) for _ in range(5)]
        + [pltpu.VMEM((nc_t * NH, HD, HD), _F32), pltpu.VMEM((8, D_HG), _F32)] + _sem_shapes(nsc),
        compiler_params=_cp(("arbitrary",)),
    )(p, p, hs, hs, o, sc, dy, wr, wi, vec, hb, g_hg, *scatter)


def _inproj_bwd_send(dp, w_in, h0, dh1, g_mix, u, order, gffn, gfin, loss, to_all):
    t_pad = dp.shape[0]
    rb = t_pad // (2 * N_DEV)
    n_steps = N_DEV + 2 * N_DEV
    na = len(to_all)

    def body(order_ref, dpc_ref, dpr_ref, u_ref, w_ref, h_ref, dh1_ref, g_ref, gffn_ref, gfin_ref, loss_ref, *rest):
        all_in = rest[:na]
        dh0_ref, recv_ref = rest[na:na + 2]
        all_out = rest[na + 2:2 * na + 2]
        alla_ref = rest[2 * na + 2]
        buf, pack, blk_send, blk_recv, blk_local = rest[2 * na + 3:2 * na + 8]
        exchange = _Exchange([], all_in, all_out, rest[2 * na + 8:2 * na + 11])
        last = _Exchange([], [pack], [alla_ref], rest[2 * na + 11:])
        s = pl.program_id(0)
        x, y, c = _coords()
        me = 4 * x + 2 * y + c

        def send(step):
            r = _SEND_ORDER[step]
            return pltpu.make_async_remote_copy(
                src_ref=buf.at[step], dst_ref=recv_ref.at[me], send_sem=blk_send.at[step], recv_sem=blk_recv.at[r - 1],
                device_id=(x ^ (r >> 2), y ^ ((r >> 1) & 1), c ^ (r & 1)), device_id_type=_MESH)

        @pl.when(s == 0)
        def _():
            exchange.start()
            pack[...] = jnp.zeros_like(pack)

        @pl.when(s < N_DEV)
        def _():
            buf[s] = _dot_tn(u_ref[...], dpc_ref[...]).astype(_BF)

            for step in range(N_DEV - 1):
                @pl.when(s == step)
                def _(step=step):
                    send(step).start()

        @pl.when(s >= N_DEV)
        def _():
            du = jnp.zeros((rb, D), _F32)
            for j in range(4):
                du = du + _dot_nt(dpr_ref[:, WIN_P * j:WIN_P * (j + 1)], w_ref[j])
            n, r = _rms_fwd(h_ref[...])
            pack[R_GMIX:R_GMIX + 1, :] += jnp.sum(du * n, axis=0, keepdims=True)
            dh0 = dh1_ref[...] + _rms_bwd(du * g_ref[...], n, r)
            dh0_ref[...] = dh0

            @pl.when(s == N_DEV)
            def _():
                pack[R_META:R_META + N_META, :] = dh0[0:N_META, :]

        @pl.when(s == n_steps - 1)
        def _():
            pack[R_GFFN:R_GFFN + 1, :] = gffn_ref[...]
            pack[R_GFIN:R_GFIN + 1, :] = gfin_ref[...]
            pack[R_LOSS:R_LOSS + 1, pl.ds(0, 128)] = loss_ref[0:1, :]
            last.start()
            mine = pltpu.make_async_copy(buf.at[N_DEV - 1], recv_ref.at[me], blk_local.at[0])
            mine.start()
            for step in range(N_DEV - 1):
                send(step).wait_send()
            for r in range(1, N_DEV):
                px, py, pc = x ^ (r >> 2), y ^ ((r >> 1) & 1), c ^ (r & 1)
                pltpu.make_async_remote_copy(
                    src_ref=buf.at[0], dst_ref=recv_ref.at[4 * px + 2 * py + pc], send_sem=blk_send.at[0],
                    recv_sem=blk_recv.at[r - 1], device_id=(px, py, pc), device_id_type=_MESH).wait_recv()
            mine.wait()
            exchange.finish()
            last.finish()

    hbm = pl.BlockSpec(memory_space=pl.ANY)
    rows = pl.BlockSpec((rb, D), lambda s, order: (jnp.maximum(s - N_DEV, 0), 0))
    one = pl.BlockSpec((1, D), lambda s, order: (0, 0))
    res = pl.pallas_call(
        body, name="inproj_bwd_send",
        grid_spec=pltpu.PrefetchScalarGridSpec(
            num_scalar_prefetch=1, grid=(n_steps,),
            in_specs=[pl.BlockSpec((t_pad, WIN_B), lambda s, order: (0, order[jnp.minimum(s, N_DEV - 1)])),
                      pl.BlockSpec((rb, D_IN), lambda s, order: (jnp.maximum(s - N_DEV, 0), 0)),
                      pl.BlockSpec((t_pad, D), lambda s, order: (0, 0), pipeline_mode=pl.Buffered(1)),
                      pl.BlockSpec((4, D, WIN_P), lambda s, order: (0, 0, 0), pipeline_mode=pl.Buffered(1)),
                      rows, rows, one, one, one, pl.BlockSpec((8, 128), lambda s, order: (0, 0))] + [hbm] * na,
            out_specs=[rows] + [hbm] * (na + 2),
            scratch_shapes=[pltpu.VMEM((N_DEV, D, WIN_B), _BF), pltpu.VMEM((24, D), _F32),
                            pltpu.SemaphoreType.DMA((N_DEV - 1,)), pltpu.SemaphoreType.DMA((N_DEV - 1,)),
                            pltpu.SemaphoreType.DMA((1,))] + _sem_shapes(na) + _sem_shapes(1)),
        out_shape=[_S((t_pad, D), _F32), _S((N_DEV, D, WIN_B), _BF)]
        + [_S((N_DEV,) + g.shape, g.dtype) for g in to_all] + [_S((N_DEV, 24, D), _F32)],
        compiler_params=_cp(("arbitrary",)),
    )(order, dp, dp, u, w_in, h0, dh1, g_mix, gffn, gfin, loss, *to_all)
    return res


def _wgrad(name, a, b, a_spec, b_spec, n_blocks, out_block, scatter=()):
    nsc = len(scatter)

    def body(a_ref, b_ref, *rest):
        o_ref = rest[nsc]
        j = pl.program_id(0)
        if nsc:
            exchange = _Exchange(rest[:nsc], [], rest[nsc + 1:2 * nsc + 1], rest[2 * nsc + 1:])

            @pl.when(j == 0)
            def _():
                exchange.start()

        av = a_ref[0] if len(a_ref.shape) == 3 else a_ref[...]
        bv = b_ref[0] if len(b_ref.shape) == 3 else b_ref[...]
        o_ref[0] = _dot_tn(av, bv).astype(_BF)

        if nsc:
            @pl.when(j == n_blocks - 1)
            def _():
                exchange.finish()

    hbm = pl.BlockSpec(memory_space=pl.ANY)
    res = pl.pallas_call(
        body, name=name, grid=(n_blocks,),
        in_specs=[a_spec, b_spec] + [hbm] * nsc,
        out_specs=[pl.BlockSpec((1,) + out_block, lambda j: (j, 0, 0))] + [hbm] * nsc,
        out_shape=[_S((n_blocks,) + out_block, _BF)] + [_S(s.shape, s.dtype) for s in scatter],
        scratch_shapes=_sem_shapes(nsc) if nsc else [],
        compiler_params=_cp(("arbitrary",)),
    )(a, b, *scatter)
    return res if nsc else res[0]


def _coords():
    return lax.axis_index("x"), lax.axis_index("y"), lax.axis_index("c")


def _sem_shapes(na):
    return [pltpu.SemaphoreType.DMA((7 * na,)), pltpu.SemaphoreType.DMA((7 * na,)), pltpu.SemaphoreType.DMA((na,))]


class _Gather:
    def __init__(self, srcs, outs, sems, place=None):
        self.srcs, self.outs = srcs, outs
        self.send_sems, self.recv_sems, self.local_sems = sems
        self.place = place if place is not None else (lambda ref, block: ref.at[block])
        self.na = len(srcs)
        x, y, c = _coords()
        self.pos = (x, y, c)
        self.me = 4 * x + 2 * y + c
        self.sibling = (x, y, 1 - c)
        self.chips = [(1 - x, y), (x, 1 - y), (1 - x, 1 - y)]

    @staticmethod
    def _slot(px, py, pc):
        return 4 * px + 2 * py + pc

    def _copy(self, a, k, block, to, own=False):
        dst = self.place(self.outs[a], block)
        return pltpu.make_async_remote_copy(
            src_ref=self.srcs[a] if own else dst, dst_ref=dst,
            send_sem=self.send_sems.at[7 * a + k], recv_sem=self.recv_sems.at[7 * a + k],
            device_id=to, device_id_type=_MESH)

    def _mine(self, a):
        return pltpu.make_async_copy(self.srcs[a], self.place(self.outs[a], self.me), self.local_sems.at[a])

    def _first(self):
        c = self.pos[2]
        cps = []
        for a in range(self.na):
            cps.append(self._copy(a, 0, self.me, self.sibling, own=True))
            cps += [self._copy(a, 1 + j, self.me, (*chip, c), own=True) for j, chip in enumerate(self.chips)]
        return cps

    def _passed(self):
        c = self.pos[2]
        return [self._copy(a, 4 + j, self._slot(*chip, c), self.sibling)
                for j, chip in enumerate(self.chips) for a in range(self.na)]

    def start(self):
        for a in range(self.na):
            self._mine(a).start()
        for cp in self._first():
            cp.start()

    def forward(self, j):
        c = self.pos[2]
        chip = self.chips[j]
        for a in range(self.na):
            self._copy(a, 1 + j, self._slot(*chip, c), self.pos).wait_recv()
            self._copy(a, 4 + j, self._slot(*chip, c), self.sibling).start()

    def wait_sibling(self):
        x, y, c = self.pos
        for a in range(self.na):
            self._copy(a, 0, self._slot(x, y, 1 - c), self.pos).wait_recv()

    def wait_passed(self, j):
        c = self.pos[2]
        for a in range(self.na):
            self._copy(a, 4 + j, self._slot(*self.chips[j], 1 - c), self.pos).wait_recv()

    def finish_sends(self):
        for cp in self._first() + self._passed():
            cp.wait_send()
        for a in range(self.na):
            self._mine(a).wait()

    def finish(self):
        self.wait_sibling()
        for j in range(3):
            self.wait_passed(j)
        self.finish_sends()


class _Exchange:
    def __init__(self, scatter, gather, outs, sems):
        self.ins = list(scatter) + list(gather)
        self.ns, self.na = len(scatter), len(scatter) + len(gather)
        self.outs = outs
        self.send_sems, self.recv_sems, self.local_sems = sems
        x, y, c = _coords()
        self.pos = (x, y, c)
        self.me = 4 * x + 2 * y + c

    def _peer(self, r):
        x, y, c = self.pos
        return x ^ (r >> 2), y ^ ((r >> 1) & 1), c ^ (r & 1)

    def _src(self, a, block):
        return self.ins[a].at[block] if a < self.ns else self.ins[a]

    def _local(self, a):
        return pltpu.make_async_copy(self._src(a, self.me), self.outs[a].at[self.me], self.local_sems.at[a])

    def _send(self, a, r):
        px, py, pc = self._peer(r)
        return pltpu.make_async_remote_copy(
            src_ref=self._src(a, 4 * px + 2 * py + pc), dst_ref=self.outs[a].at[self.me],
            send_sem=self.send_sems.at[7 * a + r - 1], recv_sem=self.recv_sems.at[7 * a + r - 1],
            device_id=(px, py, pc), device_id_type=_MESH)

    def _recv(self, a, r):
        px, py, pc = self._peer(r)
        return pltpu.make_async_remote_copy(
            src_ref=self._src(a, self.me), dst_ref=self.outs[a].at[4 * px + 2 * py + pc],
            send_sem=self.send_sems.at[7 * a + r - 1], recv_sem=self.recv_sems.at[7 * a + r - 1],
            device_id=(px, py, pc), device_id_type=_MESH)

    def start(self):
        for a in range(self.na):
            self._local(a).start()
        for r in range(1, N_DEV):
            for a in range(self.na):
                self._send(a, r).start()

    def finish(self):
        for r in range(1, N_DEV):
            for a in range(self.na):
                self._recv(a, r).wait_recv()
        for r in range(1, N_DEV):
            for a in range(self.na):
                self._send(a, r).wait_send()
        for a in range(self.na):
            self._local(a).wait()


def _prologue(x, tgt, small_l, w_in_l, cast_f32, t_pad):
    seq = x.shape[0]
    nc = len(cast_f32)
    tail = t_pad - N_META - seq
    n_local = 5 + N_DEV

    def body(x_ref, t_ref, s_ref, w_ref, *rest):
        cins = rest[:nc]
        small_ref, wg_ref, h0_ref, tp_ref = rest[nc:nc + 4]
        couts = rest[nc + 4:2 * nc + 4]
        s_stage, w_stage, zeros, lsem = rest[2 * nc + 4:2 * nc + 8]
        g_s = _Gather([s_stage], [small_ref], rest[2 * nc + 8:2 * nc + 11])
        g_w = _Gather([w_stage], [wg_ref], rest[2 * nc + 11:], place=_pair_place)
        s_stage[...] = s_ref[...]
        w_stage[...] = w_ref[...].astype(_BF)
        g_s.start()
        g_w.start()
        zeros[...] = jnp.zeros_like(zeros)
        local = [pltpu.make_async_copy(x_ref, h0_ref.at[pl.ds(N_META, seq)], lsem.at[0]),
                 pltpu.make_async_copy(t_ref, tp_ref.at[pl.ds(N_META, seq)], lsem.at[1]),
                 pltpu.make_async_copy(zeros.at[pl.ds(0, N_META)], tp_ref.at[pl.ds(0, N_META)], lsem.at[4])]
        if tail:
            local += [
                pltpu.make_async_copy(zeros.at[pl.ds(0, tail)], h0_ref.at[pl.ds(N_META + seq, tail)], lsem.at[2]),
                pltpu.make_async_copy(zeros.at[pl.ds(0, tail)], tp_ref.at[pl.ds(N_META + seq, tail)], lsem.at[3])]
        for cp in local:
            cp.start()
        for a in range(nc):
            couts[a][...] = cins[a][...].astype(_BF)
        for j in range(3):
            g_s.forward(j)
        g_s.finish()
        meta = [pltpu.make_async_copy(small_ref.at[k, pl.ds(0, N_META), :],
                                      h0_ref.at[pl.ds(0, N_META), pl.ds(128 * k, 128)], lsem.at[5 + k])
                for k in range(N_DEV)]
        for cp in meta:
            cp.start()
        for j in range(3):
            g_w.forward(j)
        g_w.finish()
        for cp in local + meta:
            cp.wait()

    vm = pl.BlockSpec(memory_space=pltpu.VMEM)
    hbm = pl.BlockSpec(memory_space=pl.ANY)
    return pl.pallas_call(
        body, name="prologue",
        in_specs=[hbm, hbm, vm, vm] + [vm] * nc,
        out_specs=[hbm] * 4 + [vm] * nc,
        out_shape=[_S((N_DEV,) + small_l.shape, _F32), _S((4, D, WIN_P), _BF), _S((t_pad, D), _F32),
                   _S((t_pad, D), _F32)] + [_S(l.shape, _BF) for l in cast_f32],
        scratch_shapes=[pltpu.VMEM(small_l.shape, _F32), pltpu.VMEM(w_in_l.shape, _BF),
                        pltpu.VMEM((max(tail, N_META), D), _F32), pltpu.SemaphoreType.DMA((n_local,))]
        + _sem_shapes(1) + _sem_shapes(1),
        compiler_params=pltpu.CompilerParams(vmem_limit_bytes=VMEM_LIMIT),
    )(x, tgt, small_l, w_in_l, *cast_f32)


def _adamw_math(w, g, m, v):
    m2 = ADAM_B1 * m + (1.0 - ADAM_B1) * g
    v2 = ADAM_B2 * v + (1.0 - ADAM_B2) * (g * g)
    m_hat = m2 / (1.0 - ADAM_B1 ** ADAM_STEP)
    v_hat = v2 / (1.0 - ADAM_B2 ** ADAM_STEP)
    delta = -ADAM_LR * (m_hat / (jnp.sqrt(v_hat) + ADAM_EPS) + ADAM_WD * w)
    return delta, m2, v2


def _adamw_big(name, recv, w, m, v, rows):
    r_all, c_all = w.shape

    def body(r_ref, w_ref, m_ref, v_ref, g_out, d_out, m_out, v_out):
        g = r_ref[0].astype(_F32)
        for k in range(1, N_DEV):
            g = g + r_ref[k].astype(_F32)
        delta, m2, v2 = _adamw_math(w_ref[...], g, m_ref[...], v_ref[...])
        g_out[...] = g
        d_out[...] = delta
        m_out[...] = m2
        v_out[...] = v2

    tile = pl.BlockSpec((rows, c_all), lambda i: (i, 0))
    return pl.pallas_call(
        body, name=name, grid=(r_all // rows,),
        in_specs=[pl.BlockSpec((N_DEV, rows, c_all), lambda i: (0, i, 0)), tile, tile, tile],
        out_specs=[tile] * 4,
        out_shape=[_S(w.shape, _F32)] * 4,
        compiler_params=_cp(("arbitrary",)),
    )(recv, w, m, v)


def _adamw_small(gathered, slices, wmv):
    ng, npar = len(gathered), len(slices)

    def body(*refs):
        g_refs = refs[:ng]
        wmv_refs = refs[ng:ng + 3 * npar]
        outs = refs[ng + 3 * npar:]
        for i, (ai, r0, nr, c0, ncol) in enumerate(slices):
            g = g_refs[ai][0, pl.ds(r0, nr), pl.ds(c0, ncol)].astype(_F32)
            for k in range(1, N_DEV):
                g = g + g_refs[ai][k, pl.ds(r0, nr), pl.ds(c0, ncol)].astype(_F32)
            w_ref, m_ref, v_ref = wmv_refs[3 * i:3 * i + 3]
            delta, m2, v2 = _adamw_math(w_ref[...], g, m_ref[...], v_ref[...])
            outs[4 * i][...] = g
            outs[4 * i + 1][...] = delta
            outs[4 * i + 2][...] = m2
            outs[4 * i + 3][...] = v2
        total = g_refs[0][0, pl.ds(R_LOSS, 1), pl.ds(0, 128)]
        for k in range(1, N_DEV):
            total = total + g_refs[0][k, pl.ds(R_LOSS, 1), pl.ds(0, 128)]
        outs[4 * npar][...] = total

    flat = [t for trip in wmv for t in trip]
    out_shape = []
    for w, _, _ in wmv:
        out_shape += [_S(w.shape, _F32)] * 4
    out_shape.append(_S((1, 128), _F32))
    return pl.pallas_call(
        body, name="adamw_small", out_shape=out_shape,
        compiler_params=pltpu.CompilerParams(vmem_limit_bytes=VMEM_LIMIT),
    )(*gathered, *flat)


def _block_diag(w):
    eye = jnp.eye(8, dtype=w.dtype)
    return (w[:, :, None, :] * eye[:, None, :, None]).reshape(D_RG, D_RG)


def _diag_blocks(g):
    return jnp.concatenate([g[64 * h:64 * (h + 1), 64 * h:64 * (h + 1)] for h in range(8)], axis=0)


def _local_step(h0, tgt_p, n_valid, g_mix, w_in, vec, wr, wi, hb, g_hg, w_out_l, g_ffn, w_gu_l, w_down_l, g_fin):
    t_pad = h0.shape[0]
    me = 4 * lax.axis_index("x") + 2 * lax.axis_index("y") + lax.axis_index("c")
    p, u, w_out, w_down = _inproj(h0, g_mix, w_in, [w_out_l, w_down_l])
    y, hs, o, sc, w_gu = _mixer_fwd(p, wr, wi, vec, hb, g_hg, [w_gu_l])
    w_out = w_out.reshape(D, D)
    w_down = w_down.reshape(4, FFB, D)
    h1, v = _outproj(h0, y, w_out, g_ffn)
    gu, act, dh2, dh2b, loss, gfin = _ffn_loss(v, h1, w_gu, w_down, g_fin, tgt_p, n_valid)

    dgu, dh1, dh1b, dy, gffn = _ffn_bwd(dh2, dh2b, gu, h1, g_ffn, w_gu, w_down, w_out)
    g_wdown = _wgrad("wgrad_down", act, dh2b, pl.BlockSpec((1, t_pad, FFB), lambda j: (j, 0, 0)),
                     pl.BlockSpec((t_pad, D), lambda j: (0, 0)), 4, (FFB, D))
    g_wgu, r_wdown = _wgrad("wgrad_gate_up", dgu, v, pl.BlockSpec((1, t_pad, FFB), lambda j: (j, 0, 0)),
                            pl.BlockSpec((t_pad, D), lambda j: (0, 0)), N_DEV, (FFB, D),
                            scatter=[g_wdown.reshape(N_DEV, D_FF // N_DEV, D)])
    g_wout = _wgrad("wgrad_out", y, dh1b, pl.BlockSpec((t_pad, D // N_DEV), lambda j: (0, j)),
                    pl.BlockSpec((t_pad, D), lambda j: (0, 0)), N_DEV, (D // N_DEV, D))
    dp, gvec, gw, r_wgu, r_wout = _mixer_bwd(p, hs, o, sc, dy, wr, wi, vec, hb, g_hg, [g_wgu, g_wout])
    pack_c = jnp.concatenate([_diag_blocks(gw[0]), _diag_blocks(gw[1])], axis=1).astype(_BF)
    order = (me ^ jnp.array(_SEND_ORDER, jnp.int32)).astype(jnp.int32)
    dh0, r_win, all_b, all_c, all_a = _inproj_bwd_send(dp, w_in, h0, dh1, g_mix, u, order, gffn, gfin, loss,
                                                       [gvec, pack_c])
    return dh0, (r_win, r_wgu, r_wout, r_wdown), (all_a, all_b, all_c)


def kernel(x, meta_tokens, mix_norm_g, w_in, conv_w, conv_b, w_rgate, b_rgate, w_igate, b_igate, lru_lambda, rg_norm_g, hg_lower_bound, hg_norm_g, w_out, ffn_norm_g, w_gate_up, w_down, final_norm_g, loss_target, m_meta_tokens, m_mix_norm_g, m_w_in, m_conv_w, m_conv_b, m_w_rgate, m_b_rgate, m_w_igate, m_b_igate, m_lru_lambda, m_rg_norm_g, m_hg_lower_bound, m_hg_norm_g, m_w_out, m_ffn_norm_g, m_w_gate_up, m_w_down, m_final_norm_g, v_meta_tokens, v_mix_norm_g, v_w_in, v_conv_w, v_conv_b, v_w_rgate, v_b_rgate, v_w_igate, v_b_igate, v_lru_lambda, v_rg_norm_g, v_hg_lower_bound, v_hg_norm_g, v_w_out, v_ffn_norm_g, v_w_gate_up, v_w_down, v_final_norm_g):
    seq = x.shape[1]
    me = 4 * lax.axis_index("x") + 2 * lax.axis_index("y") + lax.axis_index("c")

    n_valid = N_META + seq
    t_pad = -(-n_valid // TM) * TM
    small_l = jnp.concatenate([meta_tokens, jnp.pad(conv_w[0], ((0, 4), (0, 64)))], axis=0)
    small_g, w_in_g, h0, tgt_p, w_gu_l, w_out_l, w_down_l = _prologue(
        x[0], loss_target[0], small_l, w_in[0], [w_gate_up[0].T, w_out[0], w_down[0]], t_pad)
    conv_w_full = jnp.transpose(small_g[:, N_META:N_META + 4, :64], (1, 0, 2)).reshape(4, D_RG)
    vec = jnp.concatenate([conv_b, b_rgate, b_igate, lru_lambda, rg_norm_g, jnp.zeros((3, D_RG), _F32),
                           conv_w_full, jnp.zeros((4, D_RG), _F32)], axis=0)
    wr = _block_diag(w_rgate[0]).astype(_BF)
    wi = _block_diag(w_igate[0]).astype(_BF)

    dh0, (r_win, r_wgu, r_wout, r_wdown), (all_a, all_b, all_c) = _local_step(
        h0, tgt_p, n_valid, mix_norm_g, w_in_g, vec, wr, wi, hg_lower_bound, hg_norm_g,
        w_out_l, ffn_norm_g, w_gu_l, w_down_l, final_norm_g.reshape(1, D))
    grad_x = dh0[N_META:N_META + seq][None]

    outs = {}
    outs["w_in"] = _adamw_big("adamw_w_in", r_win, w_in[0], m_w_in[0], v_w_in[0], 256)
    outs["w_gate_up"] = [r.T for r in _adamw_big("adamw_w_gate_up", r_wgu, w_gate_up[0].T, m_w_gate_up[0].T,
                                                 v_w_gate_up[0].T, 176)]
    outs["w_out"] = _adamw_big("adamw_w_out", r_wout, w_out[0], m_w_out[0], v_w_out[0], 128)
    outs["w_down"] = _adamw_big("adamw_w_down", r_wdown, w_down[0], m_w_down[0], v_w_down[0], 176)

    meta_part = lax.dynamic_slice_in_dim(all_a[:, R_META:R_META + N_META, :], me * 128, 128, axis=2)
    convw_part = lax.dynamic_slice_in_dim(all_b[:, R_CONVW:R_CONVW + 4, :], me * 64, 64, axis=2)
    gathered = [all_a, all_b, all_c, meta_part, convw_part]
    small_params = [
        ("meta_tokens", (3, 0, N_META, 0, 128), (meta_tokens, m_meta_tokens, v_meta_tokens), (N_META, 128)),
        ("mix_norm_g", (0, R_GMIX, 1, 0, D), (mix_norm_g, m_mix_norm_g, v_mix_norm_g), (1, D)),
        ("conv_w", (4, 0, 4, 0, 64), (conv_w, m_conv_w, v_conv_w), (4, 64)),
        ("conv_b", (1, R_CONVB, 1, 0, D_RG), (conv_b, m_conv_b, v_conv_b), (1, D_RG)),
        ("w_rgate", (2, 0, 512, 0, 64), (w_rgate, m_w_rgate, v_w_rgate), (512, 64)),
        ("b_rgate", (1, R_BR, 1, 0, D_RG), (b_rgate, m_b_rgate, v_b_rgate), (1, D_RG)),
        ("w_igate", (2, 0, 512, 64, 64), (w_igate, m_w_igate, v_w_igate), (512, 64)),
        ("b_igate", (1, R_BI, 1, 0, D_RG), (b_igate, m_b_igate, v_b_igate), (1, D_RG)),
        ("lru_lambda", (1, R_LAM, 1, 0, D_RG), (lru_lambda, m_lru_lambda, v_lru_lambda), (1, D_RG)),
        ("rg_norm_g", (1, R_GRG, 1, 0, D_RG), (rg_norm_g, m_rg_norm_g, v_rg_norm_g), (1, D_RG)),
        ("hg_lower_bound", (1, R_HB0, 2, 0, D_HG), (hg_lower_bound, m_hg_lower_bound, v_hg_lower_bound), (2, D_HG)),
        ("hg_norm_g", (1, R_GHG, 1, 0, HD), (hg_norm_g, m_hg_norm_g, v_hg_norm_g), (1, HD)),
        ("ffn_norm_g", (0, R_GFFN, 1, 0, D), (ffn_norm_g, m_ffn_norm_g, v_ffn_norm_g), (1, D)),
        ("final_norm_g", (0, R_GFIN, 1, 0, D), (final_norm_g, m_final_norm_g, v_final_norm_g), (1, D)),
    ]
    res = _adamw_small(gathered, [s[1] for s in small_params],
                       [tuple(t.reshape(s[3]) for t in s[2]) for s in small_params])
    for i, s in enumerate(small_params):
        outs[s[0]] = [r.reshape(s[2][0].shape) for r in res[4 * i:4 * i + 4]]
    for n, ref in (("w_in", w_in), ("w_gate_up", w_gate_up), ("w_out", w_out), ("w_down", w_down)):
        outs[n] = [r.reshape(ref.shape) for r in outs[n]]

    loss_all = res[4 * len(small_params)][0, 0]
    order = ["meta_tokens", "mix_norm_g", "w_in", "conv_w", "conv_b", "w_rgate", "b_rgate", "w_igate", "b_igate",
             "lru_lambda", "rg_norm_g", "hg_lower_bound", "hg_norm_g", "w_out", "ffn_norm_g", "w_gate_up", "w_down",
             "final_norm_g"]
    return (loss_all, grad_x, *[outs[n][0] for n in order], *[outs[n][1] for n in order],
            *[outs[n][2] for n in order], *[outs[n][3] for n in order])
```

```python
import functools

import jax
import jax.numpy as jnp
from jax import lax
from jax.experimental import pallas as pl
from jax.experimental.pallas import tpu as pltpu

_BF = jnp.bfloat16
_F32 = jnp.float32
_S = jax.ShapeDtypeStruct
_MESH = pl.DeviceIdType.MESH

N_DEV = 8
N_META = 16
D = 1024
D_RG = 512
D_HG = 512
HD = 128
NH = D_HG // HD
D_IN = 3072
D_FF = 2816
FFB = D_FF // 4
WIN_B = D_IN // N_DEV
WIN_P = 2 * WIN_B
EPS = 1e-6
LRU_C = 8.0
TM = 256
HC = 64
VMEM_LIMIT = 56 * 1024 * 1024

ADAM_LR = 0.001
ADAM_B1 = 0.9
ADAM_B2 = 0.999
ADAM_EPS = 1e-08
ADAM_WD = 0.01
ADAM_STEP = 10

_SEND_ORDER = (6, 4, 2, 7, 5, 3, 1, 0)

R_CONVB, R_BR, R_BI, R_LAM, R_GRG, R_HB0, R_HB1, R_GHG, R_CONVW = 0, 1, 2, 3, 4, 5, 6, 7, 8
R_GMIX, R_GFFN, R_GFIN, R_LOSS, R_META = 0, 1, 2, 3, 8


def _cp(sem=None, **kw):
    return pltpu.CompilerParams(dimension_semantics=sem, vmem_limit_bytes=VMEM_LIMIT, **kw)


def _dot(a, b):
    return jnp.dot(a, b, preferred_element_type=_F32)


def _dot_nt(a, b):
    return lax.dot_general(a, b, (((1,), (1,)), ((), ())), preferred_element_type=_F32)


def _dot_tn(a, b):
    return lax.dot_general(a, b, (((0,), (0,)), ((), ())), preferred_element_type=_F32)


def _sigmoid(x):
    return jax.nn.sigmoid(x)


def _dsilu(x, s):
    return s * (1.0 + x * (1.0 - s))


_GELU_C = 0.7978845608028654


def _gelu_parts(x):
    t = jnp.tanh(_GELU_C * (x + 0.044715 * (x * x * x)))
    g = 0.5 * x * (1.0 + t)
    dg = 0.5 * (1.0 + t) + 0.5 * x * (1.0 - t * t) * (_GELU_C * (1.0 + 3.0 * 0.044715 * (x * x)))
    return g, dg


def _softplus(z):
    e = jnp.exp(-jnp.abs(z))
    w = 1.0 + e
    l1p = jnp.where(w == 1.0, e, jnp.log(w) * e / jnp.where(w == 1.0, 1.0, w - 1.0))
    return jnp.maximum(z, 0.0) + l1p


def _rms_fwd(x):
    r = lax.rsqrt(jnp.mean(x * x, axis=-1, keepdims=True) + EPS)
    return x * r, r


def _rms_bwd(dyg, n, r):
    return r * (dyg - n * jnp.mean(dyg * n, axis=-1, keepdims=True))


def _full(shape):
    nd = len(shape)
    return pl.BlockSpec(shape, lambda i: (0,) * nd)


def _const(shape):
    nd = len(shape)
    return pl.BlockSpec(shape, lambda i: (0,) * nd, pipeline_mode=pl.Buffered(1))


def _carry_gather(gather, i, nt):
    @pl.when(i == 0)
    def _():
        gather.start()

    def tail():
        for j in range(3):
            @pl.when(i == max(nt - 3 + j, 0))
            def _(j=j):
                gather.forward(j)

        @pl.when(i == nt - 1)
        def _():
            gather.finish()

    return tail


def _pair_place(ref, block):
    return ref.at[block // 2, :, pl.ds(pl.multiple_of((block % 2) * WIN_B, WIN_B), WIN_B)]


def _inproj(h0, g_mix, w_in, shards):
    t_pad = h0.shape[0]
    nt = t_pad // TM
    nsh = len(shards)

    def body(h_ref, g_ref, w_ref, *rest):
        p_ref, u_ref = rest[nsh:nsh + 2]
        tail = _carry_gather(_Gather(rest[:nsh], rest[nsh + 2:2 * nsh + 2], rest[2 * nsh + 2:]), pl.program_id(0), nt)
        n, _ = _rms_fwd(h_ref[...])
        u = (n * g_ref[...]).astype(_BF)
        u_ref[...] = u
        for j in range(4):
            p_ref[:, WIN_P * j:WIN_P * (j + 1)] = _dot(u, w_ref[j])
        tail()

    hbm = pl.BlockSpec(memory_space=pl.ANY)
    return pl.pallas_call(
        body, name="inproj", grid=(nt,),
        in_specs=[pl.BlockSpec((TM, D), lambda i: (i, 0)), _full((1, D)), _const((4, D, WIN_P))] + [hbm] * nsh,
        out_specs=[pl.BlockSpec((TM, D_IN), lambda i: (i, 0)), pl.BlockSpec((TM, D), lambda i: (i, 0))] + [hbm] * nsh,
        out_shape=[_S((t_pad, D_IN), _F32), _S((t_pad, D), _BF)] + [_S((N_DEV,) + s.shape, s.dtype) for s in shards],
        scratch_shapes=_sem_shapes(nsh),
        compiler_params=_cp(("arbitrary",)),
    )(h0, g_mix, w_in, *shards)


def _rg_gates(xc, wr_ref, wi_ref, vec_ref):
    xcb = xc.astype(_BF)
    r = _sigmoid(_dot(xcb, wr_ref[...]) + vec_ref[R_BR:R_BR + 1, :])
    ig = _sigmoid(_dot(xcb, wi_ref[...]) + vec_ref[R_BI:R_BI + 1, :])
    nsp8 = -LRU_C * _softplus(-vec_ref[R_LAM:R_LAM + 1, :])
    la = nsp8 * r
    a = jnp.exp(la)
    th = jnp.tanh(la)
    s = jnp.sqrt(-2.0 * th / (1.0 - th))
    return r, ig, a, s, nsp8


def _conv(xbuf, vec_ref):
    acc = vec_ref[R_CONVW:R_CONVW + 1, :] * xbuf[pl.ds(5, TM), :]
    for j in range(1, 4):
        acc = acc + vec_ref[R_CONVW + j:R_CONVW + j + 1, :] * xbuf[pl.ds(5 + j, TM), :]
    return vec_ref[R_CONVB:R_CONVB + 1, :] + acc


def _dot3(m01, x):
    hi = x.astype(_BF)
    r1 = x - hi.astype(_F32)
    mid = r1.astype(_BF)
    lo = (r1 - mid.astype(_F32)).astype(_BF)
    return (_dot(m01, lo) + _dot(m01, mid)) + _dot(m01, hi)


def _chunk_masks():
    row = lax.broadcasted_iota(jnp.int32, (TM, TM), 0)
    col = lax.broadcasted_iota(jnp.int32, (TM, TM), 1)
    shift = HC.bit_length() - 1
    same = lax.shift_right_logical(row, shift) == lax.shift_right_logical(col, shift)
    return same, same & (row >= col), same & (col >= row)


def _per_chunk_rows(x, r):
    return jnp.concatenate([jnp.broadcast_to(x[HC * c + r:HC * c + r + 1, :], (HC, x.shape[1]))
                            for c in range(TM // HC)], axis=0)


def _hg_prep(p_ref, lb, tri_blk):
    hq = p_ref[:, pl.ds(2 * D_RG, D_HG)]
    hf = p_ref[:, pl.ds(2 * D_RG + D_HG, D_HG)]
    sq = _sigmoid(hq)
    q = hq * sq
    sg = _sigmoid(hf)
    f = lb + (1.0 - lb) * sg
    k = 1.0 - f
    b = _dot3(tri_blk, jnp.log(f))
    bm = _per_chunk_rows(b, HC // 2 - 1)
    bl = _per_chunk_rows(b, HC - 1)
    e_q = jnp.exp(b - bm)
    e_k = jnp.exp(bm - b)
    e_b = jnp.exp(b)
    e_l = jnp.exp(bl - b)
    return dict(hq=hq, sq=sq, q=q, sg=sg, f=f, k=k, e_q=e_q, e_k=e_k, e_b=e_b, e_l=e_l,
                qd=q * e_q, kd=k * e_k, qe=q * e_b, ke=k * e_l, e_end=jnp.exp(bl))


def _mixer_fwd(p, wr, wi, vec, hb, g_hg, shards):
    t_pad = p.shape[0]
    nt = t_pad // TM
    nc_t = TM // HC
    nsh = len(shards)

    def body(p_ref, wr_ref, wi_ref, vec_ref, hb_ref, ghg_ref, *rest):
        sh_refs, rest = rest[:nsh], rest[nsh:]
        y_ref, hs_ref, o_ref, sc_ref = rest[:4]
        gath_refs, rest = rest[4:4 + nsh], rest[4 + nsh:]
        xbuf, a_s, b_s, hcar, st, qd_s, kd_s, qe_s, ke_s, v_s, u_s = rest[:11]
        i = pl.program_id(0)
        tail = _carry_gather(_Gather(sh_refs, gath_refs, rest[11:]), i, nt)

        @pl.when(i == 0)
        def _():
            xbuf[pl.ds(0, 8), :] = jnp.zeros((8, D_RG), _F32)
            hcar[...] = jnp.zeros_like(hcar)
            st[...] = jnp.zeros_like(st)

        x = p_ref[:, pl.ds(0, D_RG)]
        xbuf[pl.ds(8, TM), :] = x
        xc = _conv(xbuf, vec_ref)
        xbuf[pl.ds(0, 8), :] = x[TM - 8:, :]
        r, ig, a, s, _ = _rg_gates(xc, wr_ref, wi_ref, vec_ref)
        a_s[...] = a
        b_s[...] = s * (ig * xc)

        def step(t, h):
            h = a_s[pl.ds(t, 1), :] * h + b_s[pl.ds(t, 1), :]
            hs_ref[pl.ds(t, 1), :] = h
            return h

        hcar[pl.ds(0, 1), :] = lax.fori_loop(0, TM, step, hcar[pl.ds(0, 1), :], unroll=8)
        gel, _ = _gelu_parts(p_ref[:, pl.ds(D_RG, D_RG)])
        n, _ = _rms_fwd(gel * hs_ref[...])
        y_ref[:, pl.ds(0, D_RG)] = (n * vec_ref[R_GRG:R_GRG + 1, :]).astype(_BF)

        lb = _sigmoid(hb_ref[0:1, :] - hb_ref[1:2, :])
        _, tri_blk, _ = _chunk_masks()
        q = _hg_prep(p_ref, lb, tri_blk.astype(_BF))
        for name, ref in (("qd", qd_s), ("kd", kd_s), ("qe", qe_s), ("ke", ke_s)):
            ref[...] = q[name].astype(_BF)
        v_s[...] = p_ref[:, pl.ds(2 * D_RG + 2 * D_HG, D_HG)].astype(_BF)
        e_end = q["e_end"]
        causal = (lax.broadcasted_iota(jnp.int32, (HC, HC), 0) >= lax.broadcasted_iota(jnp.int32, (HC, HC), 1))
        for c in range(nc_t):
            for h in range(NH):
                rs, cs = pl.ds(HC * c, HC), pl.ds(HD * h, HD)
                amat = jnp.where(causal, _dot_nt(qd_s[rs, cs], kd_s[rs, cs]), 0.0)
                o_ref[rs, cs] = _dot(amat.astype(_BF), v_s[rs, cs])
                u_s[NH * c + h] = _dot_tn(v_s[rs, cs], ke_s[rs, cs])
        for h in range(NH):
            cs = pl.ds(HD * h, HD)
            s_run = st[h]
            for c in range(nc_t):
                rs = pl.ds(HC * c, HC)
                sc_ref[c, h] = s_run
                o_ref[rs, cs] += _dot_nt(qe_s[rs, cs], s_run.astype(_BF))
                s_run = e_end[HC * c:HC * c + 1, HD * h:HD * (h + 1)] * s_run + u_s[NH * c + h]
            st[h] = s_run
        for h in range(NH):
            cs = pl.ds(HD * h, HD)
            n_o, _ = _rms_fwd(o_ref[:, cs])
            hg = p_ref[:, pl.ds(2 * D_RG + 3 * D_HG + HD * h, HD)]
            y_ref[:, pl.ds(D_RG + HD * h, HD)] = ((n_o * ghg_ref[...]) * (hg * _sigmoid(hg))).astype(_BF)

        tail()

    hbm = pl.BlockSpec(memory_space=pl.ANY)
    return pl.pallas_call(
        body, name="mixer_fwd", grid=(nt,),
        in_specs=[pl.BlockSpec((TM, D_IN), lambda i: (i, 0)), _full((D_RG, D_RG)), _full((D_RG, D_RG)),
                  _full((16, D_RG)), _full((2, D_HG)), _full((1, HD))] + [hbm] * nsh,
        out_specs=[pl.BlockSpec((TM, D), lambda i: (i, 0)), pl.BlockSpec((TM, D_RG), lambda i: (i, 0)),
                   pl.BlockSpec((TM, D_HG), lambda i: (i, 0)),
                   pl.BlockSpec((nc_t, NH, HD, HD), lambda i: (i, 0, 0, 0))] + [hbm] * nsh,
        out_shape=[_S((t_pad, D), _BF), _S((t_pad, D_RG), _F32), _S((t_pad, D_HG), _F32),
                   _S((t_pad // HC, NH, HD, HD), _F32)] + [_S((N_DEV,) + s.shape, s.dtype) for s in shards],
        scratch_shapes=[pltpu.VMEM((TM + 8, D_RG), _F32), pltpu.VMEM((TM, D_RG), _F32),
                        pltpu.VMEM((TM, D_RG), _F32), pltpu.VMEM((8, D_RG), _F32),
                        pltpu.VMEM((NH, HD, HD), _F32)] + [pltpu.VMEM((TM, D_HG), _BF) for _ in range(5)]
        + [pltpu.VMEM((nc_t * NH, HD, HD), _F32)] + _sem_shapes(nsh),
        compiler_params=_cp(("arbitrary",)),
    )(p, wr, wi, vec, hb, g_hg, *shards)


def _outproj(h0, y, w_out, g_ffn):
    t_pad = h0.shape[0]

    def body(h_ref, y_ref, w_ref, g_ref, h1_ref, v_ref):
        h1 = h_ref[...] + _dot(y_ref[...], w_ref[...])
        h1_ref[...] = h1
        n, _ = _rms_fwd(h1)
        v_ref[...] = (n * g_ref[...]).astype(_BF)

    return pl.pallas_call(
        body, name="outproj", grid=(t_pad // TM,),
        in_specs=[pl.BlockSpec((TM, D), lambda i: (i, 0)), pl.BlockSpec((TM, D), lambda i: (i, 0)),
                  _full((D, D)), _full((1, D))],
        out_specs=[pl.BlockSpec((TM, D), lambda i: (i, 0)), pl.BlockSpec((TM, D), lambda i: (i, 0))],
        out_shape=[_S((t_pad, D), _F32), _S((t_pad, D), _BF)],
        compiler_params=_cp(("arbitrary",)),
    )(h0, y, w_out, g_ffn)


def _ffn_loss(v, h1, w_gu, w_down, g_fin, tgt, n_valid):
    t_pad = v.shape[0]

    def body(v_ref, h1_ref, wgu_ref, wd_ref, g_ref, t_ref, gu_ref, act_ref, dh2_ref, dh2b_ref, loss_ref, gfin_ref):
        i = pl.program_id(0)

        @pl.when(i == 0)
        def _():
            loss_ref[...] = jnp.zeros_like(loss_ref)
            gfin_ref[...] = jnp.zeros_like(gfin_ref)

        vb = v_ref[...]
        h2 = h1_ref[...]
        for b in range(4):
            gate = _dot_nt(vb, wgu_ref[b])
            up = _dot_nt(vb, wgu_ref[4 + b])
            gu_ref[b] = gate
            gu_ref[4 + b] = up
            act = ((gate * _sigmoid(gate)) * up).astype(_BF)
            act_ref[b] = act
            h2 = h2 + _dot(act, wd_ref[b])
        n, r = _rms_fwd(h2)
        out = n * g_ref[...]
        row = i * TM + lax.broadcasted_iota(jnp.int32, (TM, 1), 0)
        valid = (row >= N_META) & (row < n_valid)
        err = jnp.where(valid, out - t_ref[...], 0.0)
        loss_ref[...] += (0.5 / D) * jnp.sum(err * err)
        dout = err * (1.0 / D)
        gfin_ref[...] += jnp.sum(dout * n, axis=0, keepdims=True)
        dh2 = _rms_bwd(dout * g_ref[...], n, r)
        dh2_ref[...] = dh2
        dh2b_ref[...] = dh2.astype(_BF)

    return pl.pallas_call(
        body, name="ffn_loss", grid=(t_pad // TM,),
        in_specs=[pl.BlockSpec((TM, D), lambda i: (i, 0)), pl.BlockSpec((TM, D), lambda i: (i, 0)),
                  _const((N_DEV, FFB, D)), _const((4, FFB, D)), _full((1, D)),
                  pl.BlockSpec((TM, D), lambda i: (i, 0))],
        out_specs=[pl.BlockSpec((N_DEV, TM, FFB), lambda i: (0, i, 0)), pl.BlockSpec((4, TM, FFB), lambda i: (0, i, 0)),
                   pl.BlockSpec((TM, D), lambda i: (i, 0)), pl.BlockSpec((TM, D), lambda i: (i, 0)),
                   _full((8, 128)), _full((1, D))],
        out_shape=[_S((N_DEV, t_pad, FFB), _F32), _S((4, t_pad, FFB), _BF), _S((t_pad, D), _F32),
                   _S((t_pad, D), _BF), _S((8, 128), _F32), _S((1, D), _F32)],
        compiler_params=_cp(("arbitrary",)),
    )(v, h1, w_gu, w_down, g_fin, tgt)


def _ffn_bwd(dh2, dh2b, gu, h1, g_ffn, w_gu, w_down, w_out):
    t_pad = dh2.shape[0]

    def body(dh2_ref, dh2b_ref, gu_ref, h1_ref, g_ref, wgu_ref, wd_ref, wo_ref,
             dgu_ref, dh1_ref, dh1b_ref, dy_ref, gffn_ref):
        i = pl.program_id(0)

        @pl.when(i == 0)
        def _():
            gffn_ref[...] = jnp.zeros_like(gffn_ref)

        db = dh2b_ref[...]
        dv = jnp.zeros((TM, D), _F32)
        for b in range(4):
            dact = _dot_nt(db, wd_ref[b])
            gate = gu_ref[b]
            up = gu_ref[4 + b]
            sg = _sigmoid(gate)
            dgate = ((dact * up) * _dsilu(gate, sg)).astype(_BF)
            dup = (dact * (gate * sg)).astype(_BF)
            dgu_ref[b] = dgate
            dgu_ref[4 + b] = dup
            dv = dv + _dot(dgate, wgu_ref[b]) + _dot(dup, wgu_ref[4 + b])
        n, r = _rms_fwd(h1_ref[...])
        gffn_ref[...] += jnp.sum(dv * n, axis=0, keepdims=True)
        dh1 = dh2_ref[...] + _rms_bwd(dv * g_ref[...], n, r)
        dh1_ref[...] = dh1
        dh1b = dh1.astype(_BF)
        dh1b_ref[...] = dh1b
        dy_ref[...] = _dot_nt(dh1b, wo_ref[...])

    tile = pl.BlockSpec((TM, D), lambda i: (i, 0))
    return pl.pallas_call(
        body, name="ffn_bwd", grid=(t_pad // TM,),
        in_specs=[tile, tile, pl.BlockSpec((N_DEV, TM, FFB), lambda i: (0, i, 0)), tile, _full((1, D)),
                  _const((N_DEV, FFB, D)), _const((4, FFB, D)), _const((D, D))],
        out_specs=[pl.BlockSpec((N_DEV, TM, FFB), lambda i: (0, i, 0)), tile, tile, tile, _full((1, D))],
        out_shape=[_S((N_DEV, t_pad, FFB), _BF), _S((t_pad, D), _F32), _S((t_pad, D), _BF),
                   _S((t_pad, D), _F32), _S((1, D), _F32)],
        compiler_params=_cp(("arbitrary",)),
    )(dh2, dh2b, gu, h1, g_ffn, w_gu, w_down, w_out)


def _mixer_bwd(p, hs, o, sc, dy, wr, wi, vec, hb, g_hg, scatter):
    t_pad = p.shape[0]
    nt = t_pad // TM
    nc_t = TM // HC
    nsc = len(scatter)

    def rev(i):
        return nt - 1 - i

    def body(p_ref, pprev_ref, hs_ref, hprev_ref, o_ref, sc_ref, dy_ref, wr_ref, wi_ref, vec_ref, hb_ref, ghg_ref,
             *rest):
        send_refs, rest = rest[:nsc], rest[nsc:]
        dp_ref, gvec_ref, gw_ref = rest[:3]
        recv_refs, rest = rest[3:3 + nsc], rest[3 + nsc:]
        xbuf, hbuf, dbuf, a_s, g_s, ccar, dst = rest[:7]
        qd_s, kd_s, qe_s, ke_s, v_s, do_s, dqd_s, dkd_s, dqe_s, dke_s, dv_s, w_s, dend_s = rest[7:20]
        exchange = _Exchange(send_refs, [], recv_refs, rest[20:])
        i = pl.program_id(0)
        first_tile = i == nt - 1

        @pl.when(i == 0)
        def _():
            exchange.start()
            gvec_ref[...] = jnp.zeros_like(gvec_ref)
            gw_ref[...] = jnp.zeros_like(gw_ref)
            dbuf[pl.ds(TM, 8), :] = jnp.zeros((8, D_RG), _F32)
            ccar[...] = jnp.zeros_like(ccar)
            dst[...] = jnp.zeros_like(dst)

        def acc(row, val):
            gvec_ref[row:row + 1, :] += jnp.sum(val, axis=0, keepdims=True)

        keep = jnp.where(first_tile, 0.0, 1.0)
        x = p_ref[:, pl.ds(0, D_RG)]
        xbuf[pl.ds(0, 8), :] = pprev_ref[...] * keep
        xbuf[pl.ds(8, TM), :] = x
        xc = _conv(xbuf, vec_ref)
        r, ig, a, s, nsp8 = _rg_gates(xc, wr_ref, wi_ref, vec_ref)
        h = hs_ref[...]
        hbuf[pl.ds(0, 8), :] = hprev_ref[...] * keep
        hbuf[pl.ds(8, TM), :] = h
        hm1 = hbuf[pl.ds(7, TM), :]
        gr = p_ref[:, pl.ds(D_RG, D_RG)]
        gel, dgel = _gelu_parts(gr)
        n, rr = _rms_fwd(gel * h)
        dyn = dy_ref[:, pl.ds(0, D_RG)]
        acc(R_GRG, dyn * n)
        dpre = _rms_bwd(dyn * vec_ref[R_GRG:R_GRG + 1, :], n, rr)
        dp_ref[:, pl.ds(D_RG, D_RG)] = ((dpre * h) * dgel).astype(_BF)
        a_s[...] = a
        g_s[...] = dpre * gel

        def step(k, c):
            t = TM - 1 - k
            g = g_s[pl.ds(t, 1), :] + c
            g_s[pl.ds(t, 1), :] = g
            return a_s[pl.ds(t, 1), :] * g

        ccar[pl.ds(0, 1), :] = lax.fori_loop(0, TM, step, ccar[pl.ds(0, 1), :], unroll=8)
        gt = g_s[...]
        da = gt * hm1
        ixc = ig * xc
        ds = gt * ixc
        dig = (gt * s) * xc
        dxc = (gt * s) * ig
        dla = da * a - ds * ((a * a) / s)
        lam = vec_ref[R_LAM:R_LAM + 1, :]
        gvec_ref[R_LAM:R_LAM + 1, :] += jnp.sum(dla * r, axis=0, keepdims=True) * (LRU_C * _sigmoid(-lam))
        dzr = (dla * nsp8) * (r * (1.0 - r))
        dzi = dig * (ig * (1.0 - ig))
        acc(R_BR, dzr)
        acc(R_BI, dzi)
        xcb = xc.astype(_BF)
        dzrb = dzr.astype(_BF)
        dzib = dzi.astype(_BF)
        gw_ref[0] += _dot_tn(xcb, dzrb)
        gw_ref[1] += _dot_tn(xcb, dzib)
        dxc = dxc + _dot_nt(dzrb, wr_ref[...]) + _dot_nt(dzib, wi_ref[...])
        acc(R_CONVB, dxc)
        for j in range(4):
            acc(R_CONVW + j, dxc * xbuf[pl.ds(5 + j, TM), :])
        dbuf[pl.ds(0, TM), :] = dxc
        dx = vec_ref[R_CONVW + 3:R_CONVW + 4, :] * dxc
        for j in range(3):
            dx = dx + vec_ref[R_CONVW + j:R_CONVW + j + 1, :] * dbuf[pl.ds(3 - j, TM), :]
        dbuf[pl.ds(TM, 8), :] = dxc[0:8, :]
        dp_ref[:, pl.ds(0, D_RG)] = dx.astype(_BF)

        lb = _sigmoid(hb_ref[0:1, :] - hb_ref[1:2, :])
        same, tri_blk, triu_blk = _chunk_masks()
        q = _hg_prep(p_ref, lb, tri_blk.astype(_BF))
        qdb, kdb = q["qd"].astype(_BF), q["kd"].astype(_BF)
        qd_s[...] = qdb
        kd_s[...] = kdb
        qe_s[...] = q["qe"].astype(_BF)
        ke_s[...] = q["ke"].astype(_BF)
        v_s[...] = p_ref[:, pl.ds(2 * D_RG + 2 * D_HG, D_HG)].astype(_BF)
        e_end = q["e_end"]
        ghg = ghg_ref[...]
        for h in range(NH):
            cs = pl.ds(HD * h, HD)
            hg = p_ref[:, pl.ds(2 * D_RG + 3 * D_HG + HD * h, HD)]
            sh = _sigmoid(hg)
            n_o, r_o = _rms_fwd(o_ref[:, cs])
            dyh = dy_ref[:, pl.ds(D_RG + HD * h, HD)]
            dp_ref[:, pl.ds(2 * D_RG + 3 * D_HG + HD * h, HD)] = ((dyh * (n_o * ghg)) * _dsilu(hg, sh)).astype(_BF)
            dn = dyh * (hg * sh)
            gvec_ref[R_GHG:R_GHG + 1, pl.ds(0, HD)] += jnp.sum(dn * n_o, axis=0, keepdims=True)
            do_s[:, cs] = _rms_bwd(dn * ghg, n_o, r_o).astype(_BF)
        causal = (lax.broadcasted_iota(jnp.int32, (HC, HC), 0) >= lax.broadcasted_iota(jnp.int32, (HC, HC), 1))
        for c in range(nc_t):
            for h in range(NH):
                rs, cs = pl.ds(HC * c, HC), pl.ds(HD * h, HD)
                qd_c, kd_c, do_c = qd_s[rs, cs], kd_s[rs, cs], do_s[rs, cs]
                amat = jnp.where(causal, _dot_nt(qd_c, kd_c), 0.0).astype(_BF)
                da_m = jnp.where(causal, _dot_nt(do_c, v_s[rs, cs]), 0.0).astype(_BF)
                dqd_s[rs, cs] = _dot(da_m, kd_c)
                dkd_s[rs, cs] = _dot_tn(da_m, qd_c)
                dqe_s[rs, cs] = _dot(do_c, sc_ref[c, h].astype(_BF))
                dv_s[rs, cs] = _dot_tn(amat, do_c)
                w_s[NH * c + h] = _dot_tn(do_c, qe_s[rs, cs])
        for h in range(NH):
            cs = pl.ds(HD * h, HD)
            d_run = dst[h]
            for c in reversed(range(nc_t)):
                rs = pl.ds(HC * c, HC)
                d_b = d_run.astype(_BF)
                dke_s[rs, cs] = _dot(v_s[rs, cs], d_b)
                dp_ref[rs, pl.ds(2 * D_RG + 2 * D_HG + HD * h, HD)] = (
                    dv_s[rs, cs] + _dot_nt(ke_s[rs, cs], d_b)).astype(_BF)
                dend_s[pl.ds(c, 1), cs] = jnp.sum(sc_ref[c, h] * d_run, axis=0, keepdims=True)
                d_run = w_s[NH * c + h] + e_end[HC * c:HC * c + 1, HD * h:HD * (h + 1)] * d_run
            dst[h] = d_run
        dqd, dkd, dqe, dke = dqd_s[...], dkd_s[...], dqe_s[...], dke_s[...]
        dq = dqd * q["e_q"] + dqe * q["e_b"]
        dk = dkd * q["e_k"] + dke * q["e_l"]
        dkeke = dke * q["ke"]
        db = dqd * qdb.astype(_F32) - dkd * kdb.astype(_F32) + dqe * q["qe"] - dkeke
        d_end = jnp.concatenate([jnp.broadcast_to(dend_s[pl.ds(c, 1), :], (HC, D_HG)) for c in range(nc_t)], axis=0)
        dlf = _dot3(triu_blk.astype(_BF), db) + _dot3(same.astype(_BF), dkeke) + d_end * e_end
        df = dlf / q["f"] - dk
        sg = q["sg"]
        gvec_ref[R_HB0:R_HB0 + 1, :] += jnp.sum(df * (1.0 - sg), axis=0, keepdims=True)
        dp_ref[:, pl.ds(2 * D_RG, D_HG)] = (dq * _dsilu(q["hq"], q["sq"])).astype(_BF)
        dp_ref[:, pl.ds(2 * D_RG + D_HG, D_HG)] = ((df * (1.0 - lb)) * (sg * (1.0 - sg))).astype(_BF)

        @pl.when(i == nt - 1)
        def _():
            glb = gvec_ref[R_HB0:R_HB0 + 1, :] * (lb * (1.0 - lb))
            gvec_ref[R_HB0:R_HB0 + 1, :] = glb
            gvec_ref[R_HB1:R_HB1 + 1, :] = -glb
            exchange.finish()

    hbm = pl.BlockSpec(memory_space=pl.ANY)
    return pl.pallas_call(
        body, name="mixer_bwd", grid=(nt,),
        in_specs=[pl.BlockSpec((TM, D_IN), lambda i: (rev(i), 0)),
                  pl.BlockSpec((8, D_RG), lambda i: (jnp.maximum(rev(i) * (TM // 8) - 1, 0), 0)),
                  pl.BlockSpec((TM, D_RG), lambda i: (rev(i), 0)),
                  pl.BlockSpec((8, D_RG), lambda i: (jnp.maximum(rev(i) * (TM // 8) - 1, 0), 0)),
                  pl.BlockSpec((TM, D_HG), lambda i: (rev(i), 0)),
                  pl.BlockSpec((nc_t, NH, HD, HD), lambda i: (rev(i), 0, 0, 0)),
                  pl.BlockSpec((TM, D), lambda i: (rev(i), 0)),
                  _full((D_RG, D_RG)), _full((D_RG, D_RG)), _full((16, D_RG)), _full((2, D_HG)), _full((1, HD))]
        + [hbm] * nsc,
        out_specs=[pl.BlockSpec((TM, D_IN), lambda i: (rev(i), 0)), _full((16, D_RG)), _full((2, D_RG, D_RG))]
        + [hbm] * nsc,
        out_shape=[_S((t_pad, D_IN), _BF), _S((16, D_RG), _F32), _S((2, D_RG, D_RG), _F32)]
        + [_S(s.shape, s.dtype) for s in scatter],
        scratch_shapes=[pltpu.VMEM((TM + 8, D_RG), _F32), pltpu.VMEM((TM + 8, D_RG), _F32),
                        pltpu.VMEM((TM + 8, D_RG), _F32), pltpu.VMEM((TM, D_RG), _F32),
                        pltpu.VMEM((TM, D_RG), _F32), pltpu.VMEM((8, D_RG), _F32),
                        pltpu.VMEM((NH, HD, HD), _F32)]
        + [pltpu.VMEM((TM, D_HG), _BF) for _ in range(6)] + [pltpu.VMEM((TM, D_HG), _F32) for _ in range(5)]
        + [pltpu.VMEM((nc_t * NH, HD, HD), _F32), pltpu.VMEM((8, D_HG), _F32)] + _sem_shapes(nsc),
        compiler_params=_cp(("arbitrary",)),
    )(p, p, hs, hs, o, sc, dy, wr, wi, vec, hb, g_hg, *scatter)


def _inproj_bwd_send(dp, w_in, h0, dh1, g_mix, u, order, gffn, gfin, loss, to_all):
    t_pad = dp.shape[0]
    rb = t_pad // (2 * N_DEV)
    n_steps = N_DEV + 2 * N_DEV
    na = len(to_all)

    def body(order_ref, dpc_ref, dpr_ref, u_ref, w_ref, h_ref, dh1_ref, g_ref, gffn_ref, gfin_ref, loss_ref, *rest):
        all_in = rest[:na]
        dh0_ref, recv_ref = rest[na:na + 2]
        all_out = rest[na + 2:2 * na + 2]
        alla_ref = rest[2 * na + 2]
        buf, pack, blk_send, blk_recv, blk_local = rest[2 * na + 3:2 * na + 8]
        exchange = _Exchange([], all_in, all_out, rest[2 * na + 8:2 * na + 11])
        last = _Exchange([], [pack], [alla_ref], rest[2 * na + 11:])
        s = pl.program_id(0)
        x, y, c = _coords()
        me = 4 * x + 2 * y + c

        def send(step):
            r = _SEND_ORDER[step]
            return pltpu.make_async_remote_copy(
                src_ref=buf.at[step], dst_ref=recv_ref.at[me], send_sem=blk_send.at[step], recv_sem=blk_recv.at[r - 1],
                device_id=(x ^ (r >> 2), y ^ ((r >> 1) & 1), c ^ (r & 1)), device_id_type=_MESH)

        @pl.when(s == 0)
        def _():
            exchange.start()
            pack[...] = jnp.zeros_like(pack)

        @pl.when(s < N_DEV)
        def _():
            buf[s] = _dot_tn(u_ref[...], dpc_ref[...]).astype(_BF)

            for step in range(N_DEV - 1):
                @pl.when(s == step)
                def _(step=step):
                    send(step).start()

        @pl.when(s >= N_DEV)
        def _():
            du = jnp.zeros((rb, D), _F32)
            for j in range(4):
                du = du + _dot_nt(dpr_ref[:, WIN_P * j:WIN_P * (j + 1)], w_ref[j])
            n, r = _rms_fwd(h_ref[...])
            pack[R_GMIX:R_GMIX + 1, :] += jnp.sum(du * n, axis=0, keepdims=True)
            dh0 = dh1_ref[...] + _rms_bwd(du * g_ref[...], n, r)
            dh0_ref[...] = dh0

            @pl.when(s == N_DEV)
            def _():
                pack[R_META:R_META + N_META, :] = dh0[0:N_META, :]

        @pl.when(s == n_steps - 1)
        def _():
            pack[R_GFFN:R_GFFN + 1, :] = gffn_ref[...]
            pack[R_GFIN:R_GFIN + 1, :] = gfin_ref[...]
            pack[R_LOSS:R_LOSS + 1, pl.ds(0, 128)] = loss_ref[0:1, :]
            last.start()
            mine = pltpu.make_async_copy(buf.at[N_DEV - 1], recv_ref.at[me], blk_local.at[0])
            mine.start()
            for step in range(N_DEV - 1):
                send(step).wait_send()
            for r in range(1, N_DEV):
                px, py, pc = x ^ (r >> 2), y ^ ((r >> 1) & 1), c ^ (r & 1)
                pltpu.make_async_remote_copy(
                    src_ref=buf.at[0], dst_ref=recv_ref.at[4 * px + 2 * py + pc], send_sem=blk_send.at[0],
                    recv_sem=blk_recv.at[r - 1], device_id=(px, py, pc), device_id_type=_MESH).wait_recv()
            mine.wait()
            exchange.finish()
            last.finish()

    hbm = pl.BlockSpec(memory_space=pl.ANY)
    rows = pl.BlockSpec((rb, D), lambda s, order: (jnp.maximum(s - N_DEV, 0), 0))
    one = pl.BlockSpec((1, D), lambda s, order: (0, 0))
    res = pl.pallas_call(
        body, name="inproj_bwd_send",
        grid_spec=pltpu.PrefetchScalarGridSpec(
            num_scalar_prefetch=1, grid=(n_steps,),
            in_specs=[pl.BlockSpec((t_pad, WIN_B), lambda s, order: (0, order[jnp.minimum(s, N_DEV - 1)])),
                      pl.BlockSpec((rb, D_IN), lambda s, order: (jnp.maximum(s - N_DEV, 0), 0)),
                      pl.BlockSpec((t_pad, D), lambda s, order: (0, 0), pipeline_mode=pl.Buffered(1)),
                      pl.BlockSpec((4, D, WIN_P), lambda s, order: (0, 0, 0), pipeline_mode=pl.Buffered(1)),
                      rows, rows, one, one, one, pl.BlockSpec((8, 128), lambda s, order: (0, 0))] + [hbm] * na,
            out_specs=[rows] + [hbm] * (na + 2),
            scratch_shapes=[pltpu.VMEM((N_DEV, D, WIN_B), _BF), pltpu.VMEM((24, D), _F32),
                            pltpu.SemaphoreType.DMA((N_DEV - 1,)), pltpu.SemaphoreType.DMA((N_DEV - 1,)),
                            pltpu.SemaphoreType.DMA((1,))] + _sem_shapes(na) + _sem_shapes(1)),
        out_shape=[_S((t_pad, D), _F32), _S((N_DEV, D, WIN_B), _BF)]
        + [_S((N_DEV,) + g.shape, g.dtype) for g in to_all] + [_S((N_DEV, 24, D), _F32)],
        compiler_params=_cp(("arbitrary",)),
    )(order, dp, dp, u, w_in, h0, dh1, g_mix, gffn, gfin, loss, *to_all)
    return res


def _wgrad(name, a, b, a_spec, b_spec, n_blocks, out_block, scatter=()):
    nsc = len(scatter)

    def body(a_ref, b_ref, *rest):
        o_ref = rest[nsc]
        j = pl.program_id(0)
        if nsc:
            exchange = _Exchange(rest[:nsc], [], rest[nsc + 1:2 * nsc + 1], rest[2 * nsc + 1:])

            @pl.when(j == 0)
            def _():
                exchange.start()

        av = a_ref[0] if len(a_ref.shape) == 3 else a_ref[...]
        bv = b_ref[0] if len(b_ref.shape) == 3 else b_ref[...]
        o_ref[0] = _dot_tn(av, bv).astype(_BF)

        if nsc:
            @pl.when(j == n_blocks - 1)
            def _():
                exchange.finish()

    hbm = pl.BlockSpec(memory_space=pl.ANY)
    res = pl.pallas_call(
        body, name=name, grid=(n_blocks,),
        in_specs=[a_spec, b_spec] + [hbm] * nsc,
        out_specs=[pl.BlockSpec((1,) + out_block, lambda j: (j, 0, 0))] + [hbm] * nsc,
        out_shape=[_S((n_blocks,) + out_block, _BF)] + [_S(s.shape, s.dtype) for s in scatter],
        scratch_shapes=_sem_shapes(nsc) if nsc else [],
        compiler_params=_cp(("arbitrary",)),
    )(a, b, *scatter)
    return res if nsc else res[0]


def _coords():
    return lax.axis_index("x"), lax.axis_index("y"), lax.axis_index("c")


def _sem_shapes(na):
    return [pltpu.SemaphoreType.DMA((7 * na,)), pltpu.SemaphoreType.DMA((7 * na,)), pltpu.SemaphoreType.DMA((na,))]


class _Gather:
    def __init__(self, srcs, outs, sems, place=None):
        self.srcs, self.outs = srcs, outs
        self.send_sems, self.recv_sems, self.local_sems = sems
        self.place = place if place is not None else (lambda ref, block: ref.at[block])
        self.na = len(srcs)
        x, y, c = _coords()
        self.pos = (x, y, c)
        self.me = 4 * x + 2 * y + c
        self.sibling = (x, y, 1 - c)
        self.chips = [(1 - x, y), (x, 1 - y), (1 - x, 1 - y)]

    @staticmethod
    def _slot(px, py, pc):
        return 4 * px + 2 * py + pc

    def _copy(self, a, k, block, to, own=False):
        dst = self.place(self.outs[a], block)
        return pltpu.make_async_remote_copy(
            src_ref=self.srcs[a] if own else dst, dst_ref=dst,
            send_sem=self.send_sems.at[7 * a + k], recv_sem=self.recv_sems.at[7 * a + k],
            device_id=to, device_id_type=_MESH)

    def _mine(self, a):
        return pltpu.make_async_copy(self.srcs[a], self.place(self.outs[a], self.me), self.local_sems.at[a])

    def _first(self):
        c = self.pos[2]
        cps = []
        for a in range(self.na):
            cps.append(self._copy(a, 0, self.me, self.sibling, own=True))
            cps += [self._copy(a, 1 + j, self.me, (*chip, c), own=True) for j, chip in enumerate(self.chips)]
        return cps

    def _passed(self):
        c = self.pos[2]
        return [self._copy(a, 4 + j, self._slot(*chip, c), self.sibling)
                for j, chip in enumerate(self.chips) for a in range(self.na)]

    def start(self):
        for a in range(self.na):
            self._mine(a).start()
        for cp in self._first():
            cp.start()

    def forward(self, j):
        c = self.pos[2]
        chip = self.chips[j]
        for a in range(self.na):
            self._copy(a, 1 + j, self._slot(*chip, c), self.pos).wait_recv()
            self._copy(a, 4 + j, self._slot(*chip, c), self.sibling).start()

    def wait_sibling(self):
        x, y, c = self.pos
        for a in range(self.na):
            self._copy(a, 0, self._slot(x, y, 1 - c), self.pos).wait_recv()

    def wait_passed(self, j):
        c = self.pos[2]
        for a in range(self.na):
            self._copy(a, 4 + j, self._slot(*self.chips[j], 1 - c), self.pos).wait_recv()

    def finish_sends(self):
        for cp in self._first() + self._passed():
            cp.wait_send()
        for a in range(self.na):
            self._mine(a).wait()

    def finish(self):
        self.wait_sibling()
        for j in range(3):
            self.wait_passed(j)
        self.finish_sends()


class _Exchange:
    def __init__(self, scatter, gather, outs, sems):
        self.ins = list(scatter) + list(gather)
        self.ns, self.na = len(scatter), len(scatter) + len(gather)
        self.outs = outs
        self.send_sems, self.recv_sems, self.local_sems = sems
        x, y, c = _coords()
        self.pos = (x, y, c)
        self.me = 4 * x + 2 * y + c

    def _peer(self, r):
        x, y, c = self.pos
        return x ^ (r >> 2), y ^ ((r >> 1) & 1), c ^ (r & 1)

    def _src(self, a, block):
        return self.ins[a].at[block] if a < self.ns else self.ins[a]

    def _local(self, a):
        return pltpu.make_async_copy(self._src(a, self.me), self.outs[a].at[self.me], self.local_sems.at[a])

    def _send(self, a, r):
        px, py, pc = self._peer(r)
        return pltpu.make_async_remote_copy(
            src_ref=self._src(a, 4 * px + 2 * py + pc), dst_ref=self.outs[a].at[self.me],
            send_sem=self.send_sems.at[7 * a + r - 1], recv_sem=self.recv_sems.at[7 * a + r - 1],
            device_id=(px, py, pc), device_id_type=_MESH)

    def _recv(self, a, r):
        px, py, pc = self._peer(r)
        return pltpu.make_async_remote_copy(
            src_ref=self._src(a, self.me), dst_ref=self.outs[a].at[4 * px + 2 * py + pc],
            send_sem=self.send_sems.at[7 * a + r - 1], recv_sem=self.recv_sems.at[7 * a + r - 1],
            device_id=(px, py, pc), device_id_type=_MESH)

    def start(self):
        for a in range(self.na):
            self._local(a).start()
        for r in range(1, N_DEV):
            for a in range(self.na):
                self._send(a, r).start()

    def finish(self):
        for r in range(1, N_DEV):
            for a in range(self.na):
                self._recv(a, r).wait_recv()
        for r in range(1, N_DEV):
            for a in range(self.na):
                self._send(a, r).wait_send()
        for a in range(self.na):
            self._local(a).wait()


def _prologue(x, tgt, small_l, w_in_l, cast_f32):
    seq = x.shape[0]
    assert seq % TM == 0
    nx = seq // TM
    nt = nx + 1
    nc = len(cast_f32)
    body_rows = TM - N_META

    def body(xm_ref, xp_ref, tm_ref, tp_ref, s_ref, w_ref, *rest):
        cins = rest[:nc]
        h0_ref, tgt_ref, small_ref, wg_ref = rest[nc:nc + 4]
        couts = rest[nc + 4:2 * nc + 4]
        s_stage, w_stage, meta, msem = rest[2 * nc + 4:2 * nc + 8]
        g_s = _Gather([s_stage], [small_ref], rest[2 * nc + 8:2 * nc + 11])
        g_w = _Gather([w_stage], [wg_ref], rest[2 * nc + 11:], place=_pair_place)
        s = pl.program_id(0)
        i = (s + 1) % nt

        @pl.when(s == 0)
        def _():
            s_stage[...] = s_ref[...]
            w_stage[...] = w_ref[...].astype(_BF)
            g_s.start()
            g_w.start()
            meta[...] = jnp.zeros_like(meta)
            for a in range(nc):
                couts[a][...] = cins[a][...].astype(_BF)

        @pl.when(s == nt - 1)
        def _():
            for j in range(3):
                g_s.forward(j)
            g_s.finish()
            cps = [pltpu.make_async_copy(small_ref.at[k, pl.ds(0, N_META), :], meta.at[:, pl.ds(128 * k, 128)],
                                         msem.at[k]) for k in range(N_DEV)]
            for cp in cps:
                cp.start()
            for cp in cps:
                cp.wait()
            for j in range(3):
                g_w.forward(j)
            g_w.finish()

        has_x = i < nx
        h0_ref[pl.ds(0, N_META), :] = jnp.where(i == 0, meta[...], xp_ref[...])
        h0_ref[pl.ds(N_META, body_rows), :] = jnp.where(has_x, xm_ref[pl.ds(0, body_rows), :], 0.0)
        tgt_ref[pl.ds(0, N_META), :] = jnp.where(i == 0, 0.0, tp_ref[...])
        tgt_ref[pl.ds(N_META, body_rows), :] = jnp.where(has_x, tm_ref[pl.ds(0, body_rows), :], 0.0)

    def tile_of(s):
        return (s + 1) % nt

    hbm = pl.BlockSpec(memory_space=pl.ANY)
    main = pl.BlockSpec((TM, D), lambda s: (jnp.minimum(tile_of(s), nx - 1), 0))
    prev = pl.BlockSpec((N_META, D), lambda s: (jnp.maximum(tile_of(s) * (TM // N_META) - 1, 0), 0))
    tile = pl.BlockSpec((TM, D), lambda s: (tile_of(s), 0))
    return pl.pallas_call(
        body, name="prologue", grid=(nt,),
        in_specs=[main, prev, main, prev, _const(small_l.shape), _const(w_in_l.shape)]
        + [_const(l.shape) for l in cast_f32],
        out_specs=[tile, tile, hbm, hbm] + [_full(l.shape) for l in cast_f32],
        out_shape=[_S((nt * TM, D), _F32), _S((nt * TM, D), _F32), _S((N_DEV,) + small_l.shape, _F32),
                   _S((4, D, WIN_P), _BF)] + [_S(l.shape, _BF) for l in cast_f32],
        scratch_shapes=[pltpu.VMEM(small_l.shape, _F32), pltpu.VMEM(w_in_l.shape, _BF), pltpu.VMEM((N_META, D), _F32),
                        pltpu.SemaphoreType.DMA((N_DEV,))] + _sem_shapes(1) + _sem_shapes(1),
        compiler_params=_cp(("arbitrary",)),
    )(x, x, tgt, tgt, small_l, w_in_l, *cast_f32)


def _adamw_math(w, g, m, v):
    m2 = ADAM_B1 * m + (1.0 - ADAM_B1) * g
    v2 = ADAM_B2 * v + (1.0 - ADAM_B2) * (g * g)
    m_hat = m2 / (1.0 - ADAM_B1 ** ADAM_STEP)
    v_hat = v2 / (1.0 - ADAM_B2 ** ADAM_STEP)
    delta = -ADAM_LR * (m_hat / (jnp.sqrt(v_hat) + ADAM_EPS) + ADAM_WD * w)
    return delta, m2, v2


def _adamw_big(name, recv, w, m, v, rows):
    r_all, c_all = w.shape

    def body(r_ref, w_ref, m_ref, v_ref, g_out, d_out, m_out, v_out):
        g = r_ref[0].astype(_F32)
        for k in range(1, N_DEV):
            g = g + r_ref[k].astype(_F32)
        delta, m2, v2 = _adamw_math(w_ref[...], g, m_ref[...], v_ref[...])
        g_out[...] = g
        d_out[...] = delta
        m_out[...] = m2
        v_out[...] = v2

    tile = pl.BlockSpec((rows, c_all), lambda i: (i, 0))
    return pl.pallas_call(
        body, name=name, grid=(r_all // rows,),
        in_specs=[pl.BlockSpec((N_DEV, rows, c_all), lambda i: (0, i, 0)), tile, tile, tile],
        out_specs=[tile] * 4,
        out_shape=[_S(w.shape, _F32)] * 4,
        compiler_params=_cp(("arbitrary",)),
    )(recv, w, m, v)


def _adamw_small(gathered, slices, wmv):
    ng, npar = len(gathered), len(slices)

    def body(*refs):
        g_refs = refs[:ng]
        wmv_refs = refs[ng:ng + 3 * npar]
        outs = refs[ng + 3 * npar:]
        for i, (ai, r0, nr, c0, ncol) in enumerate(slices):
            g = g_refs[ai][0, pl.ds(r0, nr), pl.ds(c0, ncol)].astype(_F32)
            for k in range(1, N_DEV):
                g = g + g_refs[ai][k, pl.ds(r0, nr), pl.ds(c0, ncol)].astype(_F32)
            w_ref, m_ref, v_ref = wmv_refs[3 * i:3 * i + 3]
            delta, m2, v2 = _adamw_math(w_ref[...], g, m_ref[...], v_ref[...])
            outs[4 * i][...] = g
            outs[4 * i + 1][...] = delta
            outs[4 * i + 2][...] = m2
            outs[4 * i + 3][...] = v2
        total = g_refs[0][0, pl.ds(R_LOSS, 1), pl.ds(0, 128)]
        for k in range(1, N_DEV):
            total = total + g_refs[0][k, pl.ds(R_LOSS, 1), pl.ds(0, 128)]
        outs[4 * npar][...] = total

    flat = [t for trip in wmv for t in trip]
    out_shape = []
    for w, _, _ in wmv:
        out_shape += [_S(w.shape, _F32)] * 4
    out_shape.append(_S((1, 128), _F32))
    return pl.pallas_call(
        body, name="adamw_small", out_shape=out_shape,
        compiler_params=pltpu.CompilerParams(vmem_limit_bytes=VMEM_LIMIT),
    )(*gathered, *flat)


def _block_diag(w):
    eye = jnp.eye(8, dtype=w.dtype)
    return (w[:, :, None, :] * eye[:, None, :, None]).reshape(D_RG, D_RG)


def _diag_blocks(g):
    return jnp.concatenate([g[64 * h:64 * (h + 1), 64 * h:64 * (h + 1)] for h in range(8)], axis=0)


def _local_step(h0, tgt_p, n_valid, g_mix, w_in, vec, wr, wi, hb, g_hg, w_out_l, g_ffn, w_gu_l, w_down_l, g_fin):
    t_pad = h0.shape[0]
    me = 4 * lax.axis_index("x") + 2 * lax.axis_index("y") + lax.axis_index("c")
    p, u, w_out, w_down = _inproj(h0, g_mix, w_in, [w_out_l, w_down_l])
    y, hs, o, sc, w_gu = _mixer_fwd(p, wr, wi, vec, hb, g_hg, [w_gu_l])
    w_out = w_out.reshape(D, D)
    w_down = w_down.reshape(4, FFB, D)
    h1, v = _outproj(h0, y, w_out, g_ffn)
    gu, act, dh2, dh2b, loss, gfin = _ffn_loss(v, h1, w_gu, w_down, g_fin, tgt_p, n_valid)

    dgu, dh1, dh1b, dy, gffn = _ffn_bwd(dh2, dh2b, gu, h1, g_ffn, w_gu, w_down, w_out)
    g_wdown = _wgrad("wgrad_down", act, dh2b, pl.BlockSpec((1, t_pad, FFB), lambda j: (j, 0, 0)),
                     pl.BlockSpec((t_pad, D), lambda j: (0, 0)), 4, (FFB, D))
    g_wgu, r_wdown = _wgrad("wgrad_gate_up", dgu, v, pl.BlockSpec((1, t_pad, FFB), lambda j: (j, 0, 0)),
                            pl.BlockSpec((t_pad, D), lambda j: (0, 0)), N_DEV, (FFB, D),
                            scatter=[g_wdown.reshape(N_DEV, D_FF // N_DEV, D)])
    g_wout = _wgrad("wgrad_out", y, dh1b, pl.BlockSpec((t_pad, D // N_DEV), lambda j: (0, j)),
                    pl.BlockSpec((t_pad, D), lambda j: (0, 0)), N_DEV, (D // N_DEV, D))
    dp, gvec, gw, r_wgu, r_wout = _mixer_bwd(p, hs, o, sc, dy, wr, wi, vec, hb, g_hg, [g_wgu, g_wout])
    pack_c = jnp.concatenate([_diag_blocks(gw[0]), _diag_blocks(gw[1])], axis=1).astype(_BF)
    order = (me ^ jnp.array(_SEND_ORDER, jnp.int32)).astype(jnp.int32)
    dh0, r_win, all_b, all_c, all_a = _inproj_bwd_send(dp, w_in, h0, dh1, g_mix, u, order, gffn, gfin, loss,
                                                       [gvec, pack_c])
    return dh0, (r_win, r_wgu, r_wout, r_wdown), (all_a, all_b, all_c)


def kernel(x, meta_tokens, mix_norm_g, w_in, conv_w, conv_b, w_rgate, b_rgate, w_igate, b_igate, lru_lambda, rg_norm_g, hg_lower_bound, hg_norm_g, w_out, ffn_norm_g, w_gate_up, w_down, final_norm_g, loss_target, m_meta_tokens, m_mix_norm_g, m_w_in, m_conv_w, m_conv_b, m_w_rgate, m_b_rgate, m_w_igate, m_b_igate, m_lru_lambda, m_rg_norm_g, m_hg_lower_bound, m_hg_norm_g, m_w_out, m_ffn_norm_g, m_w_gate_up, m_w_down, m_final_norm_g, v_meta_tokens, v_mix_norm_g, v_w_in, v_conv_w, v_conv_b, v_w_rgate, v_b_rgate, v_w_igate, v_b_igate, v_lru_lambda, v_rg_norm_g, v_hg_lower_bound, v_hg_norm_g, v_w_out, v_ffn_norm_g, v_w_gate_up, v_w_down, v_final_norm_g):
    seq = x.shape[1]
    me = 4 * lax.axis_index("x") + 2 * lax.axis_index("y") + lax.axis_index("c")

    n_valid = N_META + seq
    small_l = jnp.concatenate([meta_tokens, jnp.pad(conv_w[0], ((0, 4), (0, 64)))], axis=0)
    h0, tgt_p, small_g, w_in_g, w_gu_l, w_out_l, w_down_l = _prologue(
        x[0], loss_target[0], small_l, w_in[0], [w_gate_up[0].T, w_out[0], w_down[0]])
    conv_w_full = jnp.transpose(small_g[:, N_META:N_META + 4, :64], (1, 0, 2)).reshape(4, D_RG)
    vec = jnp.concatenate([conv_b, b_rgate, b_igate, lru_lambda, rg_norm_g, jnp.zeros((3, D_RG), _F32),
                           conv_w_full, jnp.zeros((4, D_RG), _F32)], axis=0)
    wr = _block_diag(w_rgate[0]).astype(_BF)
    wi = _block_diag(w_igate[0]).astype(_BF)

    dh0, (r_win, r_wgu, r_wout, r_wdown), (all_a, all_b, all_c) = _local_step(
        h0, tgt_p, n_valid, mix_norm_g, w_in_g, vec, wr, wi, hg_lower_bound, hg_norm_g,
        w_out_l, ffn_norm_g, w_gu_l, w_down_l, final_norm_g.reshape(1, D))
    grad_x = dh0[N_META:N_META + seq][None]

    outs = {}
    outs["w_in"] = _adamw_big("adamw_w_in", r_win, w_in[0], m_w_in[0], v_w_in[0], 256)
    outs["w_gate_up"] = [r.T for r in _adamw_big("adamw_w_gate_up", r_wgu, w_gate_up[0].T, m_w_gate_up[0].T,
                                                 v_w_gate_up[0].T, 176)]
    outs["w_out"] = _adamw_big("adamw_w_out", r_wout, w_out[0], m_w_out[0], v_w_out[0], 128)
    outs["w_down"] = _adamw_big("adamw_w_down", r_wdown, w_down[0], m_w_down[0], v_w_down[0], 176)

    meta_part = lax.dynamic_slice_in_dim(all_a[:, R_META:R_META + N_META, :], me * 128, 128, axis=2)
    convw_part = lax.dynamic_slice_in_dim(all_b[:, R_CONVW:R_CONVW + 4, :], me * 64, 64, axis=2)
    gathered = [all_a, all_b, all_c, meta_part, convw_part]
    small_params = [
        ("meta_tokens", (3, 0, N_META, 0, 128), (meta_tokens, m_meta_tokens, v_meta_tokens), (N_META, 128)),
        ("mix_norm_g", (0, R_GMIX, 1, 0, D), (mix_norm_g, m_mix_norm_g, v_mix_norm_g), (1, D)),
        ("conv_w", (4, 0, 4, 0, 64), (conv_w, m_conv_w, v_conv_w), (4, 64)),
        ("conv_b", (1, R_CONVB, 1, 0, D_RG), (conv_b, m_conv_b, v_conv_b), (1, D_RG)),
        ("w_rgate", (2, 0, 512, 0, 64), (w_rgate, m_w_rgate, v_w_rgate), (512, 64)),
        ("b_rgate", (1, R_BR, 1, 0, D_RG), (b_rgate, m_b_rgate, v_b_rgate), (1, D_RG)),
        ("w_igate", (2, 0, 512, 64, 64), (w_igate, m_w_igate, v_w_igate), (512, 64)),
        ("b_igate", (1, R_BI, 1, 0, D_RG), (b_igate, m_b_igate, v_b_igate), (1, D_RG)),
        ("lru_lambda", (1, R_LAM, 1, 0, D_RG), (lru_lambda, m_lru_lambda, v_lru_lambda), (1, D_RG)),
        ("rg_norm_g", (1, R_GRG, 1, 0, D_RG), (rg_norm_g, m_rg_norm_g, v_rg_norm_g), (1, D_RG)),
        ("hg_lower_bound", (1, R_HB0, 2, 0, D_HG), (hg_lower_bound, m_hg_lower_bound, v_hg_lower_bound), (2, D_HG)),
        ("hg_norm_g", (1, R_GHG, 1, 0, HD), (hg_norm_g, m_hg_norm_g, v_hg_norm_g), (1, HD)),
        ("ffn_norm_g", (0, R_GFFN, 1, 0, D), (ffn_norm_g, m_ffn_norm_g, v_ffn_norm_g), (1, D)),
        ("final_norm_g", (0, R_GFIN, 1, 0, D), (final_norm_g, m_final_norm_g, v_final_norm_g), (1, D)),
    ]
    res = _adamw_small(gathered, [s[1] for s in small_params],
                       [tuple(t.reshape(s[3]) for t in s[2]) for s in small_params])
    for i, s in enumerate(small_params):
        outs[s[0]] = [r.reshape(s[2][0].shape) for r in res[4 * i:4 * i + 4]]
    for n, ref in (("w_in", w_in), ("w_gate_up", w_gate_up), ("w_out", w_out), ("w_down", w_down)):
        outs[n] = [r.reshape(ref.shape) for r in outs[n]]

    loss_all = res[4 * len(small_params)][0, 0]
    order = ["meta_tokens", "mix_norm_g", "w_in", "conv_w", "conv_b", "w_rgate", "b_rgate", "w_igate", "b_igate",
             "lru_lambda", "rg_norm_g", "hg_lower_bound", "hg_norm_g", "w_out", "ffn_norm_g", "w_gate_up", "w_down",
             "final_norm_g"]
    return (loss_all, grad_x, *[outs[n][0] for n in order], *[outs[n][1] for n in order],
            *[outs[n][2] for n in order], *[outs[n][3] for n in order])
```

```python
import functools

import jax
import jax.numpy as jnp
from jax import lax
from jax.experimental import pallas as pl
from jax.experimental.pallas import tpu as pltpu

_BF = jnp.bfloat16
_F32 = jnp.float32
_S = jax.ShapeDtypeStruct
_MESH = pl.DeviceIdType.MESH

N_DEV = 8
N_META = 16
D = 1024
D_RG = 512
D_HG = 512
HD = 128
NH = D_HG // HD
D_IN = 3072
D_FF = 2816
FFB = D_FF // 4
WIN_B = D_IN // N_DEV
WIN_P = 2 * WIN_B
EPS = 1e-6
LRU_C = 8.0
TM = 256
HC = 64
VMEM_LIMIT = 56 * 1024 * 1024

ADAM_LR = 0.001
ADAM_B1 = 0.9
ADAM_B2 = 0.999
ADAM_EPS = 1e-08
ADAM_WD = 0.01
ADAM_STEP = 10

_SEND_ORDER = (6, 4, 2, 7, 5, 3, 1, 0)

R_CONVB, R_BR, R_BI, R_LAM, R_GRG, R_HB0, R_HB1, R_GHG, R_CONVW = 0, 1, 2, 3, 4, 5, 6, 7, 8
R_GMIX, R_GFFN, R_GFIN, R_LOSS, R_META = 0, 1, 2, 3, 8


def _cp(sem=None, **kw):
    return pltpu.CompilerParams(dimension_semantics=sem, vmem_limit_bytes=VMEM_LIMIT, **kw)


def _dot(a, b):
    return jnp.dot(a, b, preferred_element_type=_F32)


def _dot_nt(a, b):
    return lax.dot_general(a, b, (((1,), (1,)), ((), ())), preferred_element_type=_F32)


def _dot_tn(a, b):
    return lax.dot_general(a, b, (((0,), (0,)), ((), ())), preferred_element_type=_F32)


def _sigmoid(x):
    return jax.nn.sigmoid(x)


def _dsilu(x, s):
    return s * (1.0 + x * (1.0 - s))


_GELU_C = 0.7978845608028654


def _gelu_parts(x):
    t = jnp.tanh(_GELU_C * (x + 0.044715 * (x * x * x)))
    g = 0.5 * x * (1.0 + t)
    dg = 0.5 * (1.0 + t) + 0.5 * x * (1.0 - t * t) * (_GELU_C * (1.0 + 3.0 * 0.044715 * (x * x)))
    return g, dg


def _softplus(z):
    e = jnp.exp(-jnp.abs(z))
    w = 1.0 + e
    l1p = jnp.where(w == 1.0, e, jnp.log(w) * e / jnp.where(w == 1.0, 1.0, w - 1.0))
    return jnp.maximum(z, 0.0) + l1p


def _rms_fwd(x):
    r = lax.rsqrt(jnp.mean(x * x, axis=-1, keepdims=True) + EPS)
    return x * r, r


def _rms_bwd(dyg, n, r):
    return r * (dyg - n * jnp.mean(dyg * n, axis=-1, keepdims=True))


def _full(shape):
    nd = len(shape)
    return pl.BlockSpec(shape, lambda i: (0,) * nd)


def _const(shape):
    nd = len(shape)
    return pl.BlockSpec(shape, lambda i: (0,) * nd, pipeline_mode=pl.Buffered(1))


def _carry_gather(gather, i, nt):
    @pl.when(i == 0)
    def _():
        gather.start()

    def tail():
        for j in range(3):
            @pl.when(i == max(nt - 3 + j, 0))
            def _(j=j):
                gather.forward(j)

        @pl.when(i == nt - 1)
        def _():
            gather.finish()

    return tail


def _pair_place(ref, block):
    return ref.at[block // 2, :, pl.ds(pl.multiple_of((block % 2) * WIN_B, WIN_B), WIN_B)]


def _inproj(h0, g_mix, w_in, shards):
    t_pad = h0.shape[0]
    nt = t_pad // TM
    nsh = len(shards)

    def body(h_ref, g_ref, w_ref, *rest):
        p_ref, u_ref = rest[nsh:nsh + 2]
        tail = _carry_gather(_Gather(rest[:nsh], rest[nsh + 2:2 * nsh + 2], rest[2 * nsh + 2:]), pl.program_id(0), nt)
        n, _ = _rms_fwd(h_ref[...])
        u = (n * g_ref[...]).astype(_BF)
        u_ref[...] = u
        for j in range(4):
            p_ref[:, WIN_P * j:WIN_P * (j + 1)] = _dot(u, w_ref[j])
        tail()

    hbm = pl.BlockSpec(memory_space=pl.ANY)
    return pl.pallas_call(
        body, name="inproj", grid=(nt,),
        in_specs=[pl.BlockSpec((TM, D), lambda i: (i, 0)), _full((1, D)), _const((4, D, WIN_P))] + [hbm] * nsh,
        out_specs=[pl.BlockSpec((TM, D_IN), lambda i: (i, 0)), pl.BlockSpec((TM, D), lambda i: (i, 0))] + [hbm] * nsh,
        out_shape=[_S((t_pad, D_IN), _F32), _S((t_pad, D), _BF)] + [_S((N_DEV,) + s.shape, s.dtype) for s in shards],
        scratch_shapes=_sem_shapes(nsh),
        compiler_params=_cp(("arbitrary",)),
    )(h0, g_mix, w_in, *shards)


def _rg_gates(xc, wr_ref, wi_ref, vec_ref):
    xcb = xc.astype(_BF)
    r = _sigmoid(_dot(xcb, wr_ref[...]) + vec_ref[R_BR:R_BR + 1, :])
    ig = _sigmoid(_dot(xcb, wi_ref[...]) + vec_ref[R_BI:R_BI + 1, :])
    nsp8 = -LRU_C * _softplus(-vec_ref[R_LAM:R_LAM + 1, :])
    la = nsp8 * r
    a = jnp.exp(la)
    th = jnp.tanh(la)
    s = jnp.sqrt(-2.0 * th / (1.0 - th))
    return r, ig, a, s, nsp8


def _conv(xbuf, vec_ref):
    acc = vec_ref[R_CONVW:R_CONVW + 1, :] * xbuf[pl.ds(5, TM), :]
    for j in range(1, 4):
        acc = acc + vec_ref[R_CONVW + j:R_CONVW + j + 1, :] * xbuf[pl.ds(5 + j, TM), :]
    return vec_ref[R_CONVB:R_CONVB + 1, :] + acc


def _dot3(m01, x):
    hi = x.astype(_BF)
    r1 = x - hi.astype(_F32)
    mid = r1.astype(_BF)
    lo = (r1 - mid.astype(_F32)).astype(_BF)
    return (_dot(m01, lo) + _dot(m01, mid)) + _dot(m01, hi)


def _chunk_masks():
    row = lax.broadcasted_iota(jnp.int32, (TM, TM), 0)
    col = lax.broadcasted_iota(jnp.int32, (TM, TM), 1)
    shift = HC.bit_length() - 1
    same = lax.shift_right_logical(row, shift) == lax.shift_right_logical(col, shift)
    return same, same & (row >= col), same & (col >= row)


def _per_chunk_rows(x, r):
    return jnp.concatenate([jnp.broadcast_to(x[HC * c + r:HC * c + r + 1, :], (HC, x.shape[1]))
                            for c in range(TM // HC)], axis=0)


def _hg_prep(p_ref, lb, tri_blk):
    hq = p_ref[:, pl.ds(2 * D_RG, D_HG)]
    hf = p_ref[:, pl.ds(2 * D_RG + D_HG, D_HG)]
    sq = _sigmoid(hq)
    q = hq * sq
    sg = _sigmoid(hf)
    f = lb + (1.0 - lb) * sg
    k = 1.0 - f
    b = _dot3(tri_blk, jnp.log(f))
    bm = _per_chunk_rows(b, HC // 2 - 1)
    bl = _per_chunk_rows(b, HC - 1)
    e_q = jnp.exp(b - bm)
    e_k = jnp.exp(bm - b)
    e_b = jnp.exp(b)
    e_l = jnp.exp(bl - b)
    return dict(hq=hq, sq=sq, q=q, sg=sg, f=f, k=k, e_q=e_q, e_k=e_k, e_b=e_b, e_l=e_l,
                qd=q * e_q, kd=k * e_k, qe=q * e_b, ke=k * e_l, e_end=jnp.exp(bl))


def _mixer_fwd(h0, g_mix, w_in, wr, wi, vec, hb, g_hg, shards):
    t_pad = h0.shape[0]
    nt = t_pad // TM
    nc_t = TM // HC
    nsh = len(shards)

    def body(h_ref, gmix_ref, win_ref, wr_ref, wi_ref, vec_ref, hb_ref, ghg_ref, *rest):
        sh_refs, rest = rest[:nsh], rest[nsh:]
        pout_ref, uout_ref, y_ref, hs_ref, o_ref, sc_ref = rest[:6]
        gath_refs, rest = rest[6:6 + nsh], rest[6 + nsh:]
        xbuf, a_s, b_s, hcar, st, qd_s, kd_s, qe_s, ke_s, v_s, u_s, p_s, p_ref = rest[:13]
        i = pl.program_id(0)
        tail = _carry_gather(_Gather(sh_refs, gath_refs, rest[13:]), i, nt + 1)

        @pl.when(i == 0)
        def _():
            p_s[...] = jnp.zeros_like(p_s)

        p_ref[...] = p_s[...]

        @pl.when(i <= 1)
        def _():
            xbuf[pl.ds(0, 8), :] = jnp.zeros((8, D_RG), _F32)
            hcar[...] = jnp.zeros_like(hcar)
            st[...] = jnp.zeros_like(st)

        n_h, _ = _rms_fwd(h_ref[...])
        u = (n_h * gmix_ref[...]).astype(_BF)
        uout_ref[...] = u
        pieces = [(j, k) for j in range(4) for k in range(WIN_P // 256)]

        def project(count):
            for _ in range(count):
                j, k = pieces.pop(0)
                blk = _dot(u, win_ref[j, :, pl.ds(256 * k, 256)])
                p_s[:, pl.ds(WIN_P * j + 256 * k, 256)] = blk
                pout_ref[:, pl.ds(WIN_P * j + 256 * k, 256)] = blk

        x = p_ref[:, pl.ds(0, D_RG)]
        xbuf[pl.ds(8, TM), :] = x
        xc = _conv(xbuf, vec_ref)
        xbuf[pl.ds(0, 8), :] = x[TM - 8:, :]
        r, ig, a, s, _ = _rg_gates(xc, wr_ref, wi_ref, vec_ref)
        a_s[...] = a
        b_s[...] = s * (ig * xc)

        def step(t, h):
            h = a_s[pl.ds(t, 1), :] * h + b_s[pl.ds(t, 1), :]
            hs_ref[pl.ds(t, 1), :] = h
            return h

        hcar[pl.ds(0, 1), :] = lax.fori_loop(0, TM, step, hcar[pl.ds(0, 1), :], unroll=8)
        gel, _ = _gelu_parts(p_ref[:, pl.ds(D_RG, D_RG)])
        n, _ = _rms_fwd(gel * hs_ref[...])
        y_ref[:, pl.ds(0, D_RG)] = (n * vec_ref[R_GRG:R_GRG + 1, :]).astype(_BF)

        lb = _sigmoid(hb_ref[0:1, :] - hb_ref[1:2, :])
        _, tri_blk, _ = _chunk_masks()
        q = _hg_prep(p_ref, lb, tri_blk.astype(_BF))
        for name, ref in (("qd", qd_s), ("kd", kd_s), ("qe", qe_s), ("ke", ke_s)):
            ref[...] = q[name].astype(_BF)
        v_s[...] = p_ref[:, pl.ds(2 * D_RG + 2 * D_HG, D_HG)].astype(_BF)
        e_end = q["e_end"]
        causal = (lax.broadcasted_iota(jnp.int32, (HC, HC), 0) >= lax.broadcasted_iota(jnp.int32, (HC, HC), 1))
        for c in range(nc_t):
            for h in range(NH):
                rs, cs = pl.ds(HC * c, HC), pl.ds(HD * h, HD)
                amat = jnp.where(causal, _dot_nt(qd_s[rs, cs], kd_s[rs, cs]), 0.0)
                o_ref[rs, cs] = _dot(amat.astype(_BF), v_s[rs, cs])
                u_s[NH * c + h] = _dot_tn(v_s[rs, cs], ke_s[rs, cs])
                if pieces:
                    project(1)
        assert not pieces
        for h in range(NH):
            cs = pl.ds(HD * h, HD)
            s_run = st[h]
            for c in range(nc_t):
                rs = pl.ds(HC * c, HC)
                sc_ref[c, h] = s_run
                o_ref[rs, cs] += _dot_nt(qe_s[rs, cs], s_run.astype(_BF))
                s_run = e_end[HC * c:HC * c + 1, HD * h:HD * (h + 1)] * s_run + u_s[NH * c + h]
            st[h] = s_run
        for h in range(NH):
            cs = pl.ds(HD * h, HD)
            n_o, _ = _rms_fwd(o_ref[:, cs])
            hg = p_ref[:, pl.ds(2 * D_RG + 3 * D_HG + HD * h, HD)]
            y_ref[:, pl.ds(D_RG + HD * h, HD)] = ((n_o * ghg_ref[...]) * (hg * _sigmoid(hg))).astype(_BF)

        tail()

    hbm = pl.BlockSpec(memory_space=pl.ANY)

    def proj(i):
        return jnp.minimum(i, nt - 1)

    def mixed(i):
        return jnp.maximum(i - 1, 0)

    return pl.pallas_call(
        body, name="mixer_fwd", grid=(nt + 1,),
        in_specs=[pl.BlockSpec((TM, D), lambda i: (proj(i), 0)), _full((1, D)), _const((4, D, WIN_P)),
                  _full((D_RG, D_RG)), _full((D_RG, D_RG)),
                  _full((16, D_RG)), _full((2, D_HG)), _full((1, HD))] + [hbm] * nsh,
        out_specs=[pl.BlockSpec((TM, D_IN), lambda i: (proj(i), 0)), pl.BlockSpec((TM, D), lambda i: (proj(i), 0)),
                   pl.BlockSpec((TM, D), lambda i: (mixed(i), 0)), pl.BlockSpec((TM, D_RG), lambda i: (mixed(i), 0)),
                   pl.BlockSpec((TM, D_HG), lambda i: (mixed(i), 0)),
                   pl.BlockSpec((nc_t, NH, HD, HD), lambda i: (mixed(i), 0, 0, 0))] + [hbm] * nsh,
        out_shape=[_S((t_pad, D_IN), _F32), _S((t_pad, D), _BF),
                   _S((t_pad, D), _BF), _S((t_pad, D_RG), _F32), _S((t_pad, D_HG), _F32),
                   _S((t_pad // HC, NH, HD, HD), _F32)] + [_S((N_DEV,) + s.shape, s.dtype) for s in shards],
        scratch_shapes=[pltpu.VMEM((TM + 8, D_RG), _F32), pltpu.VMEM((TM, D_RG), _F32),
                        pltpu.VMEM((TM, D_RG), _F32), pltpu.VMEM((8, D_RG), _F32),
                        pltpu.VMEM((NH, HD, HD), _F32)] + [pltpu.VMEM((TM, D_HG), _BF) for _ in range(5)]
        + [pltpu.VMEM((nc_t * NH, HD, HD), _F32), pltpu.VMEM((TM, D_IN), _F32), pltpu.VMEM((TM, D_IN), _F32)]
        + _sem_shapes(nsh),
        compiler_params=_cp(("arbitrary",)),
    )(h0, g_mix, w_in, wr, wi, vec, hb, g_hg, *shards)


def _outproj(h0, y, w_out, g_ffn):
    t_pad = h0.shape[0]

    def body(h_ref, y_ref, w_ref, g_ref, h1_ref, v_ref):
        h1 = h_ref[...] + _dot(y_ref[...], w_ref[...])
        h1_ref[...] = h1
        n, _ = _rms_fwd(h1)
        v_ref[...] = (n * g_ref[...]).astype(_BF)

    return pl.pallas_call(
        body, name="outproj", grid=(t_pad // TM,),
        in_specs=[pl.BlockSpec((TM, D), lambda i: (i, 0)), pl.BlockSpec((TM, D), lambda i: (i, 0)),
                  _full((D, D)), _full((1, D))],
        out_specs=[pl.BlockSpec((TM, D), lambda i: (i, 0)), pl.BlockSpec((TM, D), lambda i: (i, 0))],
        out_shape=[_S((t_pad, D), _F32), _S((t_pad, D), _BF)],
        compiler_params=_cp(("arbitrary",)),
    )(h0, y, w_out, g_ffn)


def _ffn_loss(v, h1, w_gu, w_down, g_fin, tgt, n_valid):
    t_pad = v.shape[0]

    def body(v_ref, h1_ref, wgu_ref, wd_ref, g_ref, t_ref, gu_ref, act_ref, dh2_ref, dh2b_ref, loss_ref, gfin_ref):
        i = pl.program_id(0)

        @pl.when(i == 0)
        def _():
            loss_ref[...] = jnp.zeros_like(loss_ref)
            gfin_ref[...] = jnp.zeros_like(gfin_ref)

        vb = v_ref[...]
        h2 = h1_ref[...]
        for b in range(4):
            gate = _dot_nt(vb, wgu_ref[b])
            up = _dot_nt(vb, wgu_ref[4 + b])
            gu_ref[b] = gate
            gu_ref[4 + b] = up
            act = ((gate * _sigmoid(gate)) * up).astype(_BF)
            act_ref[b] = act
            h2 = h2 + _dot(act, wd_ref[b])
        n, r = _rms_fwd(h2)
        out = n * g_ref[...]
        row = i * TM + lax.broadcasted_iota(jnp.int32, (TM, 1), 0)
        valid = (row >= N_META) & (row < n_valid)
        err = jnp.where(valid, out - t_ref[...], 0.0)
        loss_ref[...] += (0.5 / D) * jnp.sum(err * err)
        dout = err * (1.0 / D)
        gfin_ref[...] += jnp.sum(dout * n, axis=0, keepdims=True)
        dh2 = _rms_bwd(dout * g_ref[...], n, r)
        dh2_ref[...] = dh2
        dh2b_ref[...] = dh2.astype(_BF)

    return pl.pallas_call(
        body, name="ffn_loss", grid=(t_pad // TM,),
        in_specs=[pl.BlockSpec((TM, D), lambda i: (i, 0)), pl.BlockSpec((TM, D), lambda i: (i, 0)),
                  _const((N_DEV, FFB, D)), _const((4, FFB, D)), _full((1, D)),
                  pl.BlockSpec((TM, D), lambda i: (i, 0))],
        out_specs=[pl.BlockSpec((N_DEV, TM, FFB), lambda i: (0, i, 0)), pl.BlockSpec((4, TM, FFB), lambda i: (0, i, 0)),
                   pl.BlockSpec((TM, D), lambda i: (i, 0)), pl.BlockSpec((TM, D), lambda i: (i, 0)),
                   _full((8, 128)), _full((1, D))],
        out_shape=[_S((N_DEV, t_pad, FFB), _F32), _S((4, t_pad, FFB), _BF), _S((t_pad, D), _F32),
                   _S((t_pad, D), _BF), _S((8, 128), _F32), _S((1, D), _F32)],
        compiler_params=_cp(("arbitrary",)),
    )(v, h1, w_gu, w_down, g_fin, tgt)


def _ffn_bwd(dh2, dh2b, gu, h1, g_ffn, w_gu, w_down, w_out):
    t_pad = dh2.shape[0]

    def body(dh2_ref, dh2b_ref, gu_ref, h1_ref, g_ref, wgu_ref, wd_ref, wo_ref,
             dgu_ref, dh1_ref, dh1b_ref, dy_ref, gffn_ref):
        i = pl.program_id(0)

        @pl.when(i == 0)
        def _():
            gffn_ref[...] = jnp.zeros_like(gffn_ref)

        db = dh2b_ref[...]
        dv = jnp.zeros((TM, D), _F32)
        for b in range(4):
            dact = _dot_nt(db, wd_ref[b])
            gate = gu_ref[b]
            up = gu_ref[4 + b]
            sg = _sigmoid(gate)
            dgate = ((dact * up) * _dsilu(gate, sg)).astype(_BF)
            dup = (dact * (gate * sg)).astype(_BF)
            dgu_ref[b] = dgate
            dgu_ref[4 + b] = dup
            dv = dv + _dot(dgate, wgu_ref[b]) + _dot(dup, wgu_ref[4 + b])
        n, r = _rms_fwd(h1_ref[...])
        gffn_ref[...] += jnp.sum(dv * n, axis=0, keepdims=True)
        dh1 = dh2_ref[...] + _rms_bwd(dv * g_ref[...], n, r)
        dh1_ref[...] = dh1
        dh1b = dh1.astype(_BF)
        dh1b_ref[...] = dh1b
        dy_ref[...] = _dot_nt(dh1b, wo_ref[...])

    tile = pl.BlockSpec((TM, D), lambda i: (i, 0))
    return pl.pallas_call(
        body, name="ffn_bwd", grid=(t_pad // TM,),
        in_specs=[tile, tile, pl.BlockSpec((N_DEV, TM, FFB), lambda i: (0, i, 0)), tile, _full((1, D)),
                  _const((N_DEV, FFB, D)), _const((4, FFB, D)), _const((D, D))],
        out_specs=[pl.BlockSpec((N_DEV, TM, FFB), lambda i: (0, i, 0)), tile, tile, tile, _full((1, D))],
        out_shape=[_S((N_DEV, t_pad, FFB), _BF), _S((t_pad, D), _F32), _S((t_pad, D), _BF),
                   _S((t_pad, D), _F32), _S((1, D), _F32)],
        compiler_params=_cp(("arbitrary",)),
    )(dh2, dh2b, gu, h1, g_ffn, w_gu, w_down, w_out)


def _mixer_bwd(p, hs, o, sc, dy, wr, wi, vec, hb, g_hg, scatter):
    t_pad = p.shape[0]
    nt = t_pad // TM
    nc_t = TM // HC
    nsc = len(scatter)

    def rev(i):
        return nt - 1 - i

    def body(p_ref, pprev_ref, hs_ref, hprev_ref, o_ref, sc_ref, dy_ref, wr_ref, wi_ref, vec_ref, hb_ref, ghg_ref,
             *rest):
        send_refs, rest = rest[:nsc], rest[nsc:]
        dp_ref, gvec_ref, gw_ref = rest[:3]
        recv_refs, rest = rest[3:3 + nsc], rest[3 + nsc:]
        xbuf, hbuf, dbuf, a_s, g_s, ccar, dst = rest[:7]
        qd_s, kd_s, qe_s, ke_s, v_s, do_s, dqd_s, dkd_s, dqe_s, dke_s, dv_s, w_s, dend_s = rest[7:20]
        exchange = _Exchange(send_refs, [], recv_refs, rest[20:])
        i = pl.program_id(0)
        first_tile = i == nt - 1

        @pl.when(i == 0)
        def _():
            exchange.start()
            gvec_ref[...] = jnp.zeros_like(gvec_ref)
            gw_ref[...] = jnp.zeros_like(gw_ref)
            dbuf[pl.ds(TM, 8), :] = jnp.zeros((8, D_RG), _F32)
            ccar[...] = jnp.zeros_like(ccar)
            dst[...] = jnp.zeros_like(dst)

        def acc(row, val):
            gvec_ref[row:row + 1, :] += jnp.sum(val, axis=0, keepdims=True)

        keep = jnp.where(first_tile, 0.0, 1.0)
        x = p_ref[:, pl.ds(0, D_RG)]
        xbuf[pl.ds(0, 8), :] = pprev_ref[...] * keep
        xbuf[pl.ds(8, TM), :] = x
        xc = _conv(xbuf, vec_ref)
        r, ig, a, s, nsp8 = _rg_gates(xc, wr_ref, wi_ref, vec_ref)
        h = hs_ref[...]
        hbuf[pl.ds(0, 8), :] = hprev_ref[...] * keep
        hbuf[pl.ds(8, TM), :] = h
        hm1 = hbuf[pl.ds(7, TM), :]
        gr = p_ref[:, pl.ds(D_RG, D_RG)]
        gel, dgel = _gelu_parts(gr)
        n, rr = _rms_fwd(gel * h)
        dyn = dy_ref[:, pl.ds(0, D_RG)]
        acc(R_GRG, dyn * n)
        dpre = _rms_bwd(dyn * vec_ref[R_GRG:R_GRG + 1, :], n, rr)
        dp_ref[:, pl.ds(D_RG, D_RG)] = ((dpre * h) * dgel).astype(_BF)
        a_s[...] = a
        g_s[...] = dpre * gel

        def step(k, c):
            t = TM - 1 - k
            g = g_s[pl.ds(t, 1), :] + c
            g_s[pl.ds(t, 1), :] = g
            return a_s[pl.ds(t, 1), :] * g

        ccar[pl.ds(0, 1), :] = lax.fori_loop(0, TM, step, ccar[pl.ds(0, 1), :], unroll=8)
        gt = g_s[...]
        da = gt * hm1
        ixc = ig * xc
        ds = gt * ixc
        dig = (gt * s) * xc
        dxc = (gt * s) * ig
        dla = da * a - ds * ((a * a) / s)
        lam = vec_ref[R_LAM:R_LAM + 1, :]
        gvec_ref[R_LAM:R_LAM + 1, :] += jnp.sum(dla * r, axis=0, keepdims=True) * (LRU_C * _sigmoid(-lam))
        dzr = (dla * nsp8) * (r * (1.0 - r))
        dzi = dig * (ig * (1.0 - ig))
        acc(R_BR, dzr)
        acc(R_BI, dzi)
        xcb = xc.astype(_BF)
        dzrb = dzr.astype(_BF)
        dzib = dzi.astype(_BF)
        gw_ref[0] += _dot_tn(xcb, dzrb)
        gw_ref[1] += _dot_tn(xcb, dzib)
        dxc = dxc + _dot_nt(dzrb, wr_ref[...]) + _dot_nt(dzib, wi_ref[...])
        acc(R_CONVB, dxc)
        for j in range(4):
            acc(R_CONVW + j, dxc * xbuf[pl.ds(5 + j, TM), :])
        dbuf[pl.ds(0, TM), :] = dxc
        dx = vec_ref[R_CONVW + 3:R_CONVW + 4, :] * dxc
        for j in range(3):
            dx = dx + vec_ref[R_CONVW + j:R_CONVW + j + 1, :] * dbuf[pl.ds(3 - j, TM), :]
        dbuf[pl.ds(TM, 8), :] = dxc[0:8, :]
        dp_ref[:, pl.ds(0, D_RG)] = dx.astype(_BF)

        lb = _sigmoid(hb_ref[0:1, :] - hb_ref[1:2, :])
        same, tri_blk, triu_blk = _chunk_masks()
        q = _hg_prep(p_ref, lb, tri_blk.astype(_BF))
        qdb, kdb = q["qd"].astype(_BF), q["kd"].astype(_BF)
        qd_s[...] = qdb
        kd_s[...] = kdb
        qe_s[...] = q["qe"].astype(_BF)
        ke_s[...] = q["ke"].astype(_BF)
        v_s[...] = p_ref[:, pl.ds(2 * D_RG + 2 * D_HG, D_HG)].astype(_BF)
        e_end = q["e_end"]
        ghg = ghg_ref[...]
        for h in range(NH):
            cs = pl.ds(HD * h, HD)
            hg = p_ref[:, pl.ds(2 * D_RG + 3 * D_HG + HD * h, HD)]
            sh = _sigmoid(hg)
            n_o, r_o = _rms_fwd(o_ref[:, cs])
            dyh = dy_ref[:, pl.ds(D_RG + HD * h, HD)]
            dp_ref[:, pl.ds(2 * D_RG + 3 * D_HG + HD * h, HD)] = ((dyh * (n_o * ghg)) * _dsilu(hg, sh)).astype(_BF)
            dn = dyh * (hg * sh)
            gvec_ref[R_GHG:R_GHG + 1, pl.ds(0, HD)] += jnp.sum(dn * n_o, axis=0, keepdims=True)
            do_s[:, cs] = _rms_bwd(dn * ghg, n_o, r_o).astype(_BF)
        causal = (lax.broadcasted_iota(jnp.int32, (HC, HC), 0) >= lax.broadcasted_iota(jnp.int32, (HC, HC), 1))
        for c in range(nc_t):
            for h in range(NH):
                rs, cs = pl.ds(HC * c, HC), pl.ds(HD * h, HD)
                qd_c, kd_c, do_c = qd_s[rs, cs], kd_s[rs, cs], do_s[rs, cs]
                amat = jnp.where(causal, _dot_nt(qd_c, kd_c), 0.0).astype(_BF)
                da_m = jnp.where(causal, _dot_nt(do_c, v_s[rs, cs]), 0.0).astype(_BF)
                dqd_s[rs, cs] = _dot(da_m, kd_c)
                dkd_s[rs, cs] = _dot_tn(da_m, qd_c)
                dqe_s[rs, cs] = _dot(do_c, sc_ref[c, h].astype(_BF))
                dv_s[rs, cs] = _dot_tn(amat, do_c)
                w_s[NH * c + h] = _dot_tn(do_c, qe_s[rs, cs])
        for h in range(NH):
            cs = pl.ds(HD * h, HD)
            d_run = dst[h]
            for c in reversed(range(nc_t)):
                rs = pl.ds(HC * c, HC)
                d_b = d_run.astype(_BF)
                dke_s[rs, cs] = _dot(v_s[rs, cs], d_b)
                dp_ref[rs, pl.ds(2 * D_RG + 2 * D_HG + HD * h, HD)] = (
                    dv_s[rs, cs] + _dot_nt(ke_s[rs, cs], d_b)).astype(_BF)
                dend_s[pl.ds(c, 1), cs] = jnp.sum(sc_ref[c, h] * d_run, axis=0, keepdims=True)
                d_run = w_s[NH * c + h] + e_end[HC * c:HC * c + 1, HD * h:HD * (h + 1)] * d_run
            dst[h] = d_run
        dqd, dkd, dqe, dke = dqd_s[...], dkd_s[...], dqe_s[...], dke_s[...]
        dq = dqd * q["e_q"] + dqe * q["e_b"]
        dk = dkd * q["e_k"] + dke * q["e_l"]
        dkeke = dke * q["ke"]
        db = dqd * qdb.astype(_F32) - dkd * kdb.astype(_F32) + dqe * q["qe"] - dkeke
        d_end = jnp.concatenate([jnp.broadcast_to(dend_s[pl.ds(c, 1), :], (HC, D_HG)) for c in range(nc_t)], axis=0)
        dlf = _dot3(triu_blk.astype(_BF), db) + _dot3(same.astype(_BF), dkeke) + d_end * e_end
        df = dlf / q["f"] - dk
        sg = q["sg"]
        gvec_ref[R_HB0:R_HB0 + 1, :] += jnp.sum(df * (1.0 - sg), axis=0, keepdims=True)
        dp_ref[:, pl.ds(2 * D_RG, D_HG)] = (dq * _dsilu(q["hq"], q["sq"])).astype(_BF)
        dp_ref[:, pl.ds(2 * D_RG + D_HG, D_HG)] = ((df * (1.0 - lb)) * (sg * (1.0 - sg))).astype(_BF)

        @pl.when(i == nt - 1)
        def _():
            glb = gvec_ref[R_HB0:R_HB0 + 1, :] * (lb * (1.0 - lb))
            gvec_ref[R_HB0:R_HB0 + 1, :] = glb
            gvec_ref[R_HB1:R_HB1 + 1, :] = -glb
            exchange.finish()

    hbm = pl.BlockSpec(memory_space=pl.ANY)
    return pl.pallas_call(
        body, name="mixer_bwd", grid=(nt,),
        in_specs=[pl.BlockSpec((TM, D_IN), lambda i: (rev(i), 0)),
                  pl.BlockSpec((8, D_RG), lambda i: (jnp.maximum(rev(i) * (TM // 8) - 1, 0), 0)),
                  pl.BlockSpec((TM, D_RG), lambda i: (rev(i), 0)),
                  pl.BlockSpec((8, D_RG), lambda i: (jnp.maximum(rev(i) * (TM // 8) - 1, 0), 0)),
                  pl.BlockSpec((TM, D_HG), lambda i: (rev(i), 0)),
                  pl.BlockSpec((nc_t, NH, HD, HD), lambda i: (rev(i), 0, 0, 0)),
                  pl.BlockSpec((TM, D), lambda i: (rev(i), 0)),
                  _full((D_RG, D_RG)), _full((D_RG, D_RG)), _full((16, D_RG)), _full((2, D_HG)), _full((1, HD))]
        + [hbm] * nsc,
        out_specs=[pl.BlockSpec((TM, D_IN), lambda i: (rev(i), 0)), _full((16, D_RG)), _full((2, D_RG, D_RG))]
        + [hbm] * nsc,
        out_shape=[_S((t_pad, D_IN), _BF), _S((16, D_RG), _F32), _S((2, D_RG, D_RG), _F32)]
        + [_S(s.shape, s.dtype) for s in scatter],
        scratch_shapes=[pltpu.VMEM((TM + 8, D_RG), _F32), pltpu.VMEM((TM + 8, D_RG), _F32),
                        pltpu.VMEM((TM + 8, D_RG), _F32), pltpu.VMEM((TM, D_RG), _F32),
                        pltpu.VMEM((TM, D_RG), _F32), pltpu.VMEM((8, D_RG), _F32),
                        pltpu.VMEM((NH, HD, HD), _F32)]
        + [pltpu.VMEM((TM, D_HG), _BF) for _ in range(6)] + [pltpu.VMEM((TM, D_HG), _F32) for _ in range(5)]
        + [pltpu.VMEM((nc_t * NH, HD, HD), _F32), pltpu.VMEM((8, D_HG), _F32)] + _sem_shapes(nsc),
        compiler_params=_cp(("arbitrary",)),
    )(p, p, hs, hs, o, sc, dy, wr, wi, vec, hb, g_hg, *scatter)


def _inproj_bwd_send(dp, w_in, h0, dh1, g_mix, u, order, gffn, gfin, loss, to_all):
    t_pad = dp.shape[0]
    rb = t_pad // (2 * N_DEV)
    n_steps = N_DEV + 2 * N_DEV
    na = len(to_all)

    def body(order_ref, dpc_ref, dpr_ref, u_ref, w_ref, h_ref, dh1_ref, g_ref, gffn_ref, gfin_ref, loss_ref, *rest):
        all_in = rest[:na]
        dh0_ref, recv_ref = rest[na:na + 2]
        all_out = rest[na + 2:2 * na + 2]
        alla_ref = rest[2 * na + 2]
        buf, pack, blk_send, blk_recv, blk_local = rest[2 * na + 3:2 * na + 8]
        exchange = _Exchange([], all_in, all_out, rest[2 * na + 8:2 * na + 11])
        last = _Exchange([], [pack], [alla_ref], rest[2 * na + 11:])
        s = pl.program_id(0)
        x, y, c = _coords()
        me = 4 * x + 2 * y + c

        def send(step):
            r = _SEND_ORDER[step]
            return pltpu.make_async_remote_copy(
                src_ref=buf.at[step], dst_ref=recv_ref.at[me], send_sem=blk_send.at[step], recv_sem=blk_recv.at[r - 1],
                device_id=(x ^ (r >> 2), y ^ ((r >> 1) & 1), c ^ (r & 1)), device_id_type=_MESH)

        @pl.when(s == 0)
        def _():
            exchange.start()
            pack[...] = jnp.zeros_like(pack)

        @pl.when(s < N_DEV)
        def _():
            buf[s] = _dot_tn(u_ref[...], dpc_ref[...]).astype(_BF)

            for step in range(N_DEV - 1):
                @pl.when(s == step)
                def _(step=step):
                    send(step).start()

        @pl.when(s >= N_DEV)
        def _():
            du = jnp.zeros((rb, D), _F32)
            for j in range(4):
                du = du + _dot_nt(dpr_ref[:, WIN_P * j:WIN_P * (j + 1)], w_ref[j])
            n, r = _rms_fwd(h_ref[...])
            pack[R_GMIX:R_GMIX + 1, :] += jnp.sum(du * n, axis=0, keepdims=True)
            dh0 = dh1_ref[...] + _rms_bwd(du * g_ref[...], n, r)
            dh0_ref[...] = dh0

            @pl.when(s == N_DEV)
            def _():
                pack[R_META:R_META + N_META, :] = dh0[0:N_META, :]

        @pl.when(s == n_steps - 1)
        def _():
            pack[R_GFFN:R_GFFN + 1, :] = gffn_ref[...]
            pack[R_GFIN:R_GFIN + 1, :] = gfin_ref[...]
            pack[R_LOSS:R_LOSS + 1, pl.ds(0, 128)] = loss_ref[0:1, :]
            last.start()
            mine = pltpu.make_async_copy(buf.at[N_DEV - 1], recv_ref.at[me], blk_local.at[0])
            mine.start()
            for step in range(N_DEV - 1):
                send(step).wait_send()
            for r in range(1, N_DEV):
                px, py, pc = x ^ (r >> 2), y ^ ((r >> 1) & 1), c ^ (r & 1)
                pltpu.make_async_remote_copy(
                    src_ref=buf.at[0], dst_ref=recv_ref.at[4 * px + 2 * py + pc], send_sem=blk_send.at[0],
                    recv_sem=blk_recv.at[r - 1], device_id=(px, py, pc), device_id_type=_MESH).wait_recv()
            mine.wait()
            exchange.finish()
            last.finish()

    hbm = pl.BlockSpec(memory_space=pl.ANY)
    rows = pl.BlockSpec((rb, D), lambda s, order: (jnp.maximum(s - N_DEV, 0), 0))
    one = pl.BlockSpec((1, D), lambda s, order: (0, 0))
    res = pl.pallas_call(
        body, name="inproj_bwd_send",
        grid_spec=pltpu.PrefetchScalarGridSpec(
            num_scalar_prefetch=1, grid=(n_steps,),
            in_specs=[pl.BlockSpec((t_pad, WIN_B), lambda s, order: (0, order[jnp.minimum(s, N_DEV - 1)])),
                      pl.BlockSpec((rb, D_IN), lambda s, order: (jnp.maximum(s - N_DEV, 0), 0)),
                      pl.BlockSpec((t_pad, D), lambda s, order: (0, 0), pipeline_mode=pl.Buffered(1)),
                      pl.BlockSpec((4, D, WIN_P), lambda s, order: (0, 0, 0), pipeline_mode=pl.Buffered(1)),
                      rows, rows, one, one, one, pl.BlockSpec((8, 128), lambda s, order: (0, 0))] + [hbm] * na,
            out_specs=[rows] + [hbm] * (na + 2),
            scratch_shapes=[pltpu.VMEM((N_DEV, D, WIN_B), _BF), pltpu.VMEM((24, D), _F32),
                            pltpu.SemaphoreType.DMA((N_DEV - 1,)), pltpu.SemaphoreType.DMA((N_DEV - 1,)),
                            pltpu.SemaphoreType.DMA((1,))] + _sem_shapes(na) + _sem_shapes(1)),
        out_shape=[_S((t_pad, D), _F32), _S((N_DEV, D, WIN_B), _BF)]
        + [_S((N_DEV,) + g.shape, g.dtype) for g in to_all] + [_S((N_DEV, 24, D), _F32)],
        compiler_params=_cp(("arbitrary",)),
    )(order, dp, dp, u, w_in, h0, dh1, g_mix, gffn, gfin, loss, *to_all)
    return res


def _wgrad(name, a, b, a_spec, b_spec, n_blocks, out_block, scatter=()):
    nsc = len(scatter)

    def body(a_ref, b_ref, *rest):
        o_ref = rest[nsc]
        j = pl.program_id(0)
        if nsc:
            exchange = _Exchange(rest[:nsc], [], rest[nsc + 1:2 * nsc + 1], rest[2 * nsc + 1:])

            @pl.when(j == 0)
            def _():
                exchange.start()

        av = a_ref[0] if len(a_ref.shape) == 3 else a_ref[...]
        bv = b_ref[0] if len(b_ref.shape) == 3 else b_ref[...]
        o_ref[0] = _dot_tn(av, bv).astype(_BF)

        if nsc:
            @pl.when(j == n_blocks - 1)
            def _():
                exchange.finish()

    hbm = pl.BlockSpec(memory_space=pl.ANY)
    res = pl.pallas_call(
        body, name=name, grid=(n_blocks,),
        in_specs=[a_spec, b_spec] + [hbm] * nsc,
        out_specs=[pl.BlockSpec((1,) + out_block, lambda j: (j, 0, 0))] + [hbm] * nsc,
        out_shape=[_S((n_blocks,) + out_block, _BF)] + [_S(s.shape, s.dtype) for s in scatter],
        scratch_shapes=_sem_shapes(nsc) if nsc else [],
        compiler_params=_cp(("arbitrary",)),
    )(a, b, *scatter)
    return res if nsc else res[0]


def _coords():
    return lax.axis_index("x"), lax.axis_index("y"), lax.axis_index("c")


def _sem_shapes(na):
    return [pltpu.SemaphoreType.DMA((7 * na,)), pltpu.SemaphoreType.DMA((7 * na,)), pltpu.SemaphoreType.DMA((na,))]


class _Gather:
    def __init__(self, srcs, outs, sems, place=None):
        self.srcs, self.outs = srcs, outs
        self.send_sems, self.recv_sems, self.local_sems = sems
        self.place = place if place is not None else (lambda ref, block: ref.at[block])
        self.na = len(srcs)
        x, y, c = _coords()
        self.pos = (x, y, c)
        self.me = 4 * x + 2 * y + c
        self.sibling = (x, y, 1 - c)
        self.chips = [(1 - x, y), (x, 1 - y), (1 - x, 1 - y)]

    @staticmethod
    def _slot(px, py, pc):
        return 4 * px + 2 * py + pc

    def _copy(self, a, k, block, to, own=False):
        dst = self.place(self.outs[a], block)
        return pltpu.make_async_remote_copy(
            src_ref=self.srcs[a] if own else dst, dst_ref=dst,
            send_sem=self.send_sems.at[7 * a + k], recv_sem=self.recv_sems.at[7 * a + k],
            device_id=to, device_id_type=_MESH)

    def _mine(self, a):
        return pltpu.make_async_copy(self.srcs[a], self.place(self.outs[a], self.me), self.local_sems.at[a])

    def _first(self):
        c = self.pos[2]
        cps = []
        for a in range(self.na):
            cps.append(self._copy(a, 0, self.me, self.sibling, own=True))
            cps += [self._copy(a, 1 + j, self.me, (*chip, c), own=True) for j, chip in enumerate(self.chips)]
        return cps

    def _passed(self):
        c = self.pos[2]
        return [self._copy(a, 4 + j, self._slot(*chip, c), self.sibling)
                for j, chip in enumerate(self.chips) for a in range(self.na)]

    def start(self):
        for a in range(self.na):
            self._mine(a).start()
        for cp in self._first():
            cp.start()

    def forward(self, j):
        c = self.pos[2]
        chip = self.chips[j]
        for a in range(self.na):
            self._copy(a, 1 + j, self._slot(*chip, c), self.pos).wait_recv()
            self._copy(a, 4 + j, self._slot(*chip, c), self.sibling).start()

    def wait_sibling(self):
        x, y, c = self.pos
        for a in range(self.na):
            self._copy(a, 0, self._slot(x, y, 1 - c), self.pos).wait_recv()

    def wait_passed(self, j):
        c = self.pos[2]
        for a in range(self.na):
            self._copy(a, 4 + j, self._slot(*self.chips[j], 1 - c), self.pos).wait_recv()

    def finish_sends(self):
        for cp in self._first() + self._passed():
            cp.wait_send()
        for a in range(self.na):
            self._mine(a).wait()

    def finish(self):
        self.wait_sibling()
        for j in range(3):
            self.wait_passed(j)
        self.finish_sends()


class _Exchange:
    def __init__(self, scatter, gather, outs, sems):
        self.ins = list(scatter) + list(gather)
        self.ns, self.na = len(scatter), len(scatter) + len(gather)
        self.outs = outs
        self.send_sems, self.recv_sems, self.local_sems = sems
        x, y, c = _coords()
        self.pos = (x, y, c)
        self.me = 4 * x + 2 * y + c

    def _peer(self, r):
        x, y, c = self.pos
        return x ^ (r >> 2), y ^ ((r >> 1) & 1), c ^ (r & 1)

    def _src(self, a, block):
        return self.ins[a].at[block] if a < self.ns else self.ins[a]

    def _local(self, a):
        return pltpu.make_async_copy(self._src(a, self.me), self.outs[a].at[self.me], self.local_sems.at[a])

    def _send(self, a, r):
        px, py, pc = self._peer(r)
        return pltpu.make_async_remote_copy(
            src_ref=self._src(a, 4 * px + 2 * py + pc), dst_ref=self.outs[a].at[self.me],
            send_sem=self.send_sems.at[7 * a + r - 1], recv_sem=self.recv_sems.at[7 * a + r - 1],
            device_id=(px, py, pc), device_id_type=_MESH)

    def _recv(self, a, r):
        px, py, pc = self._peer(r)
        return pltpu.make_async_remote_copy(
            src_ref=self._src(a, self.me), dst_ref=self.outs[a].at[4 * px + 2 * py + pc],
            send_sem=self.send_sems.at[7 * a + r - 1], recv_sem=self.recv_sems.at[7 * a + r - 1],
            device_id=(px, py, pc), device_id_type=_MESH)

    def start(self):
        for a in range(self.na):
            self._local(a).start()
        for r in range(1, N_DEV):
            for a in range(self.na):
                self._send(a, r).start()

    def finish(self):
        for r in range(1, N_DEV):
            for a in range(self.na):
                self._recv(a, r).wait_recv()
        for r in range(1, N_DEV):
            for a in range(self.na):
                self._send(a, r).wait_send()
        for a in range(self.na):
            self._local(a).wait()


def _prologue(x, tgt, small_l, w_in_l, cast_f32):
    seq = x.shape[0]
    assert seq % TM == 0
    nx = seq // TM
    nt = nx + 1
    nc = len(cast_f32)
    body_rows = TM - N_META

    def body(xm_ref, xp_ref, tm_ref, tp_ref, s_ref, w_ref, *rest):
        cins = rest[:nc]
        h0_ref, tgt_ref, small_ref, wg_ref = rest[nc:nc + 4]
        couts = rest[nc + 4:2 * nc + 4]
        s_stage, w_stage, meta, msem = rest[2 * nc + 4:2 * nc + 8]
        g_s = _Gather([s_stage], [small_ref], rest[2 * nc + 8:2 * nc + 11])
        g_w = _Gather([w_stage], [wg_ref], rest[2 * nc + 11:], place=_pair_place)
        s = pl.program_id(0)
        i = (s + 1) % nt

        @pl.when(s == 0)
        def _():
            s_stage[...] = s_ref[...]
            w_stage[...] = w_ref[...].astype(_BF)
            g_s.start()
            g_w.start()
            meta[...] = jnp.zeros_like(meta)
            for a in range(nc):
                couts[a][...] = cins[a][...].astype(_BF)

        @pl.when(s == nt - 1)
        def _():
            for j in range(3):
                g_s.forward(j)
            g_s.finish()
            cps = [pltpu.make_async_copy(small_ref.at[k, pl.ds(0, N_META), :], meta.at[:, pl.ds(128 * k, 128)],
                                         msem.at[k]) for k in range(N_DEV)]
            for cp in cps:
                cp.start()
            for cp in cps:
                cp.wait()
            for j in range(3):
                g_w.forward(j)
            g_w.finish()

        has_x = i < nx
        h0_ref[pl.ds(0, N_META), :] = jnp.where(i == 0, meta[...], xp_ref[...])
        h0_ref[pl.ds(N_META, body_rows), :] = jnp.where(has_x, xm_ref[pl.ds(0, body_rows), :], 0.0)
        tgt_ref[pl.ds(0, N_META), :] = jnp.where(i == 0, 0.0, tp_ref[...])
        tgt_ref[pl.ds(N_META, body_rows), :] = jnp.where(has_x, tm_ref[pl.ds(0, body_rows), :], 0.0)

    def tile_of(s):
        return (s + 1) % nt

    hbm = pl.BlockSpec(memory_space=pl.ANY)
    main = pl.BlockSpec((TM, D), lambda s: (jnp.minimum(tile_of(s), nx - 1), 0))
    prev = pl.BlockSpec((N_META, D), lambda s: (jnp.maximum(tile_of(s) * (TM // N_META) - 1, 0), 0))
    tile = pl.BlockSpec((TM, D), lambda s: (tile_of(s), 0))
    return pl.pallas_call(
        body, name="prologue", grid=(nt,),
        in_specs=[main, prev, main, prev, _const(small_l.shape), _const(w_in_l.shape)]
        + [_const(l.shape) for l in cast_f32],
        out_specs=[tile, tile, hbm, hbm] + [_full(l.shape) for l in cast_f32],
        out_shape=[_S((nt * TM, D), _F32), _S((nt * TM, D), _F32), _S((N_DEV,) + small_l.shape, _F32),
                   _S((4, D, WIN_P), _BF)] + [_S(l.shape, _BF) for l in cast_f32],
        scratch_shapes=[pltpu.VMEM(small_l.shape, _F32), pltpu.VMEM(w_in_l.shape, _BF), pltpu.VMEM((N_META, D), _F32),
                        pltpu.SemaphoreType.DMA((N_DEV,))] + _sem_shapes(1) + _sem_shapes(1),
        compiler_params=_cp(("arbitrary",)),
    )(x, x, tgt, tgt, small_l, w_in_l, *cast_f32)


def _adamw_math(w, g, m, v):
    m2 = ADAM_B1 * m + (1.0 - ADAM_B1) * g
    v2 = ADAM_B2 * v + (1.0 - ADAM_B2) * (g * g)
    m_hat = m2 / (1.0 - ADAM_B1 ** ADAM_STEP)
    v_hat = v2 / (1.0 - ADAM_B2 ** ADAM_STEP)
    delta = -ADAM_LR * (m_hat / (jnp.sqrt(v_hat) + ADAM_EPS) + ADAM_WD * w)
    return delta, m2, v2


def _adamw_big(name, recv, w, m, v, rows):
    r_all, c_all = w.shape

    def body(r_ref, w_ref, m_ref, v_ref, g_out, d_out, m_out, v_out):
        g = r_ref[0].astype(_F32)
        for k in range(1, N_DEV):
            g = g + r_ref[k].astype(_F32)
        delta, m2, v2 = _adamw_math(w_ref[...], g, m_ref[...], v_ref[...])
        g_out[...] = g
        d_out[...] = delta
        m_out[...] = m2
        v_out[...] = v2

    tile = pl.BlockSpec((rows, c_all), lambda i: (i, 0))
    return pl.pallas_call(
        body, name=name, grid=(r_all // rows,),
        in_specs=[pl.BlockSpec((N_DEV, rows, c_all), lambda i: (0, i, 0)), tile, tile, tile],
        out_specs=[tile] * 4,
        out_shape=[_S(w.shape, _F32)] * 4,
        compiler_params=_cp(("arbitrary",)),
    )(recv, w, m, v)


def _adamw_small(gathered, slices, wmv):
    ng, npar = len(gathered), len(slices)

    def body(*refs):
        g_refs = refs[:ng]
        wmv_refs = refs[ng:ng + 3 * npar]
        outs = refs[ng + 3 * npar:]
        for i, (ai, r0, nr, c0, ncol) in enumerate(slices):
            g = g_refs[ai][0, pl.ds(r0, nr), pl.ds(c0, ncol)].astype(_F32)
            for k in range(1, N_DEV):
                g = g + g_refs[ai][k, pl.ds(r0, nr), pl.ds(c0, ncol)].astype(_F32)
            w_ref, m_ref, v_ref = wmv_refs[3 * i:3 * i + 3]
            delta, m2, v2 = _adamw_math(w_ref[...], g, m_ref[...], v_ref[...])
            outs[4 * i][...] = g
            outs[4 * i + 1][...] = delta
            outs[4 * i + 2][...] = m2
            outs[4 * i + 3][...] = v2
        total = g_refs[0][0, pl.ds(R_LOSS, 1), pl.ds(0, 128)]
        for k in range(1, N_DEV):
            total = total + g_refs[0][k, pl.ds(R_LOSS, 1), pl.ds(0, 128)]
        outs[4 * npar][...] = total

    flat = [t for trip in wmv for t in trip]
    out_shape = []
    for w, _, _ in wmv:
        out_shape += [_S(w.shape, _F32)] * 4
    out_shape.append(_S((1, 128), _F32))
    return pl.pallas_call(
        body, name="adamw_small", out_shape=out_shape,
        compiler_params=pltpu.CompilerParams(vmem_limit_bytes=VMEM_LIMIT),
    )(*gathered, *flat)


def _block_diag(w):
    eye = jnp.eye(8, dtype=w.dtype)
    return (w[:, :, None, :] * eye[:, None, :, None]).reshape(D_RG, D_RG)


def _diag_blocks(g):
    return jnp.concatenate([g[64 * h:64 * (h + 1), 64 * h:64 * (h + 1)] for h in range(8)], axis=0)


def _local_step(h0, tgt_p, n_valid, g_mix, w_in, vec, wr, wi, hb, g_hg, w_out_l, g_ffn, w_gu_l, w_down_l, g_fin):
    t_pad = h0.shape[0]
    me = 4 * lax.axis_index("x") + 2 * lax.axis_index("y") + lax.axis_index("c")
    p, u, y, hs, o, sc, w_out, w_gu, w_down = _mixer_fwd(h0, g_mix, w_in, wr, wi, vec, hb, g_hg,
                                                         [w_out_l, w_gu_l, w_down_l])
    w_out = w_out.reshape(D, D)
    w_down = w_down.reshape(4, FFB, D)
    h1, v = _outproj(h0, y, w_out, g_ffn)
    gu, act, dh2, dh2b, loss, gfin = _ffn_loss(v, h1, w_gu, w_down, g_fin, tgt_p, n_valid)

    dgu, dh1, dh1b, dy, gffn = _ffn_bwd(dh2, dh2b, gu, h1, g_ffn, w_gu, w_down, w_out)
    g_wdown = _wgrad("wgrad_down", act, dh2b, pl.BlockSpec((1, t_pad, FFB), lambda j: (j, 0, 0)),
                     pl.BlockSpec((t_pad, D), lambda j: (0, 0)), 4, (FFB, D))
    g_wgu, r_wdown = _wgrad("wgrad_gate_up", dgu, v, pl.BlockSpec((1, t_pad, FFB), lambda j: (j, 0, 0)),
                            pl.BlockSpec((t_pad, D), lambda j: (0, 0)), N_DEV, (FFB, D),
                            scatter=[g_wdown.reshape(N_DEV, D_FF // N_DEV, D)])
    g_wout = _wgrad("wgrad_out", y, dh1b, pl.BlockSpec((t_pad, D // N_DEV), lambda j: (0, j)),
                    pl.BlockSpec((t_pad, D), lambda j: (0, 0)), N_DEV, (D // N_DEV, D))
    dp, gvec, gw, r_wgu, r_wout = _mixer_bwd(p, hs, o, sc, dy, wr, wi, vec, hb, g_hg, [g_wgu, g_wout])
    pack_c = jnp.concatenate([_diag_blocks(gw[0]), _diag_blocks(gw[1])], axis=1).astype(_BF)
    order = (me ^ jnp.array(_SEND_ORDER, jnp.int32)).astype(jnp.int32)
    dh0, r_win, all_b, all_c, all_a = _inproj_bwd_send(dp, w_in, h0, dh1, g_mix, u, order, gffn, gfin, loss,
                                                       [gvec, pack_c])
    return dh0, (r_win, r_wgu, r_wout, r_wdown), (all_a, all_b, all_c)


def kernel(x, meta_tokens, mix_norm_g, w_in, conv_w, conv_b, w_rgate, b_rgate, w_igate, b_igate, lru_lambda, rg_norm_g, hg_lower_bound, hg_norm_g, w_out, ffn_norm_g, w_gate_up, w_down, final_norm_g, loss_target, m_meta_tokens, m_mix_norm_g, m_w_in, m_conv_w, m_conv_b, m_w_rgate, m_b_rgate, m_w_igate, m_b_igate, m_lru_lambda, m_rg_norm_g, m_hg_lower_bound, m_hg_norm_g, m_w_out, m_ffn_norm_g, m_w_gate_up, m_w_down, m_final_norm_g, v_meta_tokens, v_mix_norm_g, v_w_in, v_conv_w, v_conv_b, v_w_rgate, v_b_rgate, v_w_igate, v_b_igate, v_lru_lambda, v_rg_norm_g, v_hg_lower_bound, v_hg_norm_g, v_w_out, v_ffn_norm_g, v_w_gate_up, v_w_down, v_final_norm_g):
    seq = x.shape[1]
    me = 4 * lax.axis_index("x") + 2 * lax.axis_index("y") + lax.axis_index("c")

    n_valid = N_META + seq
    small_l = jnp.concatenate([meta_tokens, jnp.pad(conv_w[0], ((0, 4), (0, 64)))], axis=0)
    h0, tgt_p, small_g, w_in_g, w_gu_l, w_out_l, w_down_l = _prologue(
        x[0], loss_target[0], small_l, w_in[0], [w_gate_up[0].T, w_out[0], w_down[0]])
    conv_w_full = jnp.transpose(small_g[:, N_META:N_META + 4, :64], (1, 0, 2)).reshape(4, D_RG)
    vec = jnp.concatenate([conv_b, b_rgate, b_igate, lru_lambda, rg_norm_g, jnp.zeros((3, D_RG), _F32),
                           conv_w_full, jnp.zeros((4, D_RG), _F32)], axis=0)
    wr = _block_diag(w_rgate[0]).astype(_BF)
    wi = _block_diag(w_igate[0]).astype(_BF)

    dh0, (r_win, r_wgu, r_wout, r_wdown), (all_a, all_b, all_c) = _local_step(
        h0, tgt_p, n_valid, mix_norm_g, w_in_g, vec, wr, wi, hg_lower_bound, hg_norm_g,
        w_out_l, ffn_norm_g, w_gu_l, w_down_l, final_norm_g.reshape(1, D))
    grad_x = dh0[N_META:N_META + seq][None]

    outs = {}
    outs["w_in"] = _adamw_big("adamw_w_in", r_win, w_in[0], m_w_in[0], v_w_in[0], 256)
    outs["w_gate_up"] = [r.T for r in _adamw_big("adamw_w_gate_up", r_wgu, w_gate_up[0].T, m_w_gate_up[0].T,
                                                 v_w_gate_up[0].T, 176)]
    outs["w_out"] = _adamw_big("adamw_w_out", r_wout, w_out[0], m_w_out[0], v_w_out[0], 128)
    outs["w_down"] = _adamw_big("adamw_w_down", r_wdown, w_down[0], m_w_down[0], v_w_down[0], 176)

    meta_part = lax.dynamic_slice_in_dim(all_a[:, R_META:R_META + N_META, :], me * 128, 128, axis=2)
    convw_part = lax.dynamic_slice_in_dim(all_b[:, R_CONVW:R_CONVW + 4, :], me * 64, 64, axis=2)
    gathered = [all_a, all_b, all_c, meta_part, convw_part]
    small_params = [
        ("meta_tokens", (3, 0, N_META, 0, 128), (meta_tokens, m_meta_tokens, v_meta_tokens), (N_META, 128)),
        ("mix_norm_g", (0, R_GMIX, 1, 0, D), (mix_norm_g, m_mix_norm_g, v_mix_norm_g), (1, D)),
        ("conv_w", (4, 0, 4, 0, 64), (conv_w, m_conv_w, v_conv_w), (4, 64)),
        ("conv_b", (1, R_CONVB, 1, 0, D_RG), (conv_b, m_conv_b, v_conv_b), (1, D_RG)),
        ("w_rgate", (2, 0, 512, 0, 64), (w_rgate, m_w_rgate, v_w_rgate), (512, 64)),
        ("b_rgate", (1, R_BR, 1, 0, D_RG), (b_rgate, m_b_rgate, v_b_rgate), (1, D_RG)),
        ("w_igate", (2, 0, 512, 64, 64), (w_igate, m_w_igate, v_w_igate), (512, 64)),
        ("b_igate", (1, R_BI, 1, 0, D_RG), (b_igate, m_b_igate, v_b_igate), (1, D_RG)),
        ("lru_lambda", (1, R_LAM, 1, 0, D_RG), (lru_lambda, m_lru_lambda, v_lru_lambda), (1, D_RG)),
        ("rg_norm_g", (1, R_GRG, 1, 0, D_RG), (rg_norm_g, m_rg_norm_g, v_rg_norm_g), (1, D_RG)),
        ("hg_lower_bound", (1, R_HB0, 2, 0, D_HG), (hg_lower_bound, m_hg_lower_bound, v_hg_lower_bound), (2, D_HG)),
        ("hg_norm_g", (1, R_GHG, 1, 0, HD), (hg_norm_g, m_hg_norm_g, v_hg_norm_g), (1, HD)),
        ("ffn_norm_g", (0, R_GFFN, 1, 0, D), (ffn_norm_g, m_ffn_norm_g, v_ffn_norm_g), (1, D)),
        ("final_norm_g", (0, R_GFIN, 1, 0, D), (final_norm_g, m_final_norm_g, v_final_norm_g), (1, D)),
    ]
    res = _adamw_small(gathered, [s[1] for s in small_params],
                       [tuple(t.reshape(s[3]) for t in s[2]) for s in small_params])
    for i, s in enumerate(small_params):
        outs[s[0]] = [r.reshape(s[2][0].shape) for r in res[4 * i:4 * i + 4]]
    for n, ref in (("w_in", w_in), ("w_gate_up", w_gate_up), ("w_out", w_out), ("w_down", w_down)):
        outs[n] = [r.reshape(ref.shape) for r in outs[n]]

    loss_all = res[4 * len(small_params)][0, 0]
    order = ["meta_tokens", "mix_norm_g", "w_in", "conv_w", "conv_b", "w_rgate", "b_rgate", "w_igate", "b_igate",
             "lru_lambda", "rg_norm_g", "hg_lower_bound", "hg_norm_g", "w_out", "ffn_norm_g", "w_gate_up", "w_down",
             "final_norm_g"]
    return (loss_all, grad_x, *[outs[n][0] for n in order], *[outs[n][1] for n in order],
            *[outs[n][2] for n in order], *[outs[n][3] for n in order])
```

```python
import functools

import jax
import jax.numpy as jnp
from jax import lax
from jax.experimental import pallas as pl
from jax.experimental.pallas import tpu as pltpu

_BF = jnp.bfloat16
_F32 = jnp.float32
_S = jax.ShapeDtypeStruct
_MESH = pl.DeviceIdType.MESH

N_DEV = 8
N_META = 16
D = 1024
D_RG = 512
D_HG = 512
HD = 128
NH = D_HG // HD
D_IN = 3072
D_FF = 2816
FFB = D_FF // 4
WIN_B = D_IN // N_DEV
WIN_P = 2 * WIN_B
EPS = 1e-6
LRU_C = 8.0
TM = 256
HC = 64
VMEM_LIMIT = 56 * 1024 * 1024

ADAM_LR = 0.001
ADAM_B1 = 0.9
ADAM_B2 = 0.999
ADAM_EPS = 1e-08
ADAM_WD = 0.01
ADAM_STEP = 10

_SEND_ORDER = (6, 4, 2, 7, 5, 3, 1, 0)

R_CONVB, R_BR, R_BI, R_LAM, R_GRG, R_HB0, R_HB1, R_GHG, R_CONVW = 0, 1, 2, 3, 4, 5, 6, 7, 8
R_GMIX, R_GFFN, R_GFIN, R_LOSS, R_META = 0, 1, 2, 3, 8


def _cp(sem=None, **kw):
    return pltpu.CompilerParams(dimension_semantics=sem, vmem_limit_bytes=VMEM_LIMIT, **kw)


def _dot(a, b):
    return jnp.dot(a, b, preferred_element_type=_F32)


def _dot_nt(a, b):
    return lax.dot_general(a, b, (((1,), (1,)), ((), ())), preferred_element_type=_F32)


def _dot_tn(a, b):
    return lax.dot_general(a, b, (((0,), (0,)), ((), ())), preferred_element_type=_F32)


def _sigmoid(x):
    return 0.5 * jnp.tanh(0.5 * x) + 0.5


def _dsilu(x, s):
    return s * (1.0 + x * (1.0 - s))


_GELU_C = 0.7978845608028654


def _gelu_parts(x):
    t = jnp.tanh(_GELU_C * (x + 0.044715 * (x * x * x)))
    g = 0.5 * x * (1.0 + t)
    dg = 0.5 * (1.0 + t) + 0.5 * x * (1.0 - t * t) * (_GELU_C * (1.0 + 3.0 * 0.044715 * (x * x)))
    return g, dg


def _softplus(z):
    e = jnp.exp(-jnp.abs(z))
    w = 1.0 + e
    l1p = jnp.where(w == 1.0, e, jnp.log(w) * e / jnp.where(w == 1.0, 1.0, w - 1.0))
    return jnp.maximum(z, 0.0) + l1p


def _rms_fwd(x):
    r = lax.rsqrt(jnp.mean(x * x, axis=-1, keepdims=True) + EPS)
    return x * r, r


def _rms_bwd(dyg, n, r):
    return r * (dyg - n * jnp.mean(dyg * n, axis=-1, keepdims=True))


def _full(shape):
    nd = len(shape)
    return pl.BlockSpec(shape, lambda i: (0,) * nd)


def _const(shape):
    nd = len(shape)
    return pl.BlockSpec(shape, lambda i: (0,) * nd, pipeline_mode=pl.Buffered(1))


def _carry_gather(gather, i, nt):
    @pl.when(i == 0)
    def _():
        gather.start()

    def tail():
        for j in range(3):
            @pl.when(i == max(nt - 3 + j, 0))
            def _(j=j):
                gather.forward(j)

        @pl.when(i == nt - 1)
        def _():
            gather.finish()

    return tail


def _pair_place(ref, block):
    return ref.at[block // 2, :, pl.ds(pl.multiple_of((block % 2) * WIN_B, WIN_B), WIN_B)]


def _rg_gates(xc, wr_ref, wi_ref, vec_ref):
    xcb = xc.astype(_BF)
    r = _sigmoid(_dot(xcb, wr_ref[...]) + vec_ref[R_BR:R_BR + 1, :])
    ig = _sigmoid(_dot(xcb, wi_ref[...]) + vec_ref[R_BI:R_BI + 1, :])
    nsp8 = -LRU_C * _softplus(-vec_ref[R_LAM:R_LAM + 1, :])
    la = nsp8 * r
    a = jnp.exp(la)
    th = jnp.tanh(la)
    s = jnp.sqrt(-2.0 * th / (1.0 - th))
    return r, ig, a, s, nsp8


def _conv(xbuf, vec_ref):
    acc = vec_ref[R_CONVW:R_CONVW + 1, :] * xbuf[pl.ds(5, TM), :]
    for j in range(1, 4):
        acc = acc + vec_ref[R_CONVW + j:R_CONVW + j + 1, :] * xbuf[pl.ds(5 + j, TM), :]
    return vec_ref[R_CONVB:R_CONVB + 1, :] + acc


def _dot3(m01, x):
    hi = x.astype(_BF)
    r1 = x - hi.astype(_F32)
    mid = r1.astype(_BF)
    lo = (r1 - mid.astype(_F32)).astype(_BF)
    return (_dot(m01, lo) + _dot(m01, mid)) + _dot(m01, hi)


def _chunk_masks():
    row = lax.broadcasted_iota(jnp.int32, (TM, TM), 0)
    col = lax.broadcasted_iota(jnp.int32, (TM, TM), 1)
    shift = HC.bit_length() - 1
    same = lax.shift_right_logical(row, shift) == lax.shift_right_logical(col, shift)
    return same, same & (row >= col), same & (col >= row)


def _per_chunk_rows(x, r):
    return jnp.concatenate([jnp.broadcast_to(x[HC * c + r:HC * c + r + 1, :], (HC, x.shape[1]))
                            for c in range(TM // HC)], axis=0)


def _hg_prep(p_ref, lb, tri_blk):
    hq = p_ref[:, pl.ds(2 * D_RG, D_HG)]
    hf = p_ref[:, pl.ds(2 * D_RG + D_HG, D_HG)]
    sq = _sigmoid(hq)
    q = hq * sq
    sg = _sigmoid(hf)
    f = lb + (1.0 - lb) * sg
    k = 1.0 - f
    b = _dot3(tri_blk, jnp.log(f))
    bm = _per_chunk_rows(b, HC // 2 - 1)
    bl = _per_chunk_rows(b, HC - 1)
    e_q = jnp.exp(b - bm)
    e_k = jnp.exp(bm - b)
    e_b = jnp.exp(b)
    e_l = jnp.exp(bl - b)
    return dict(hq=hq, sq=sq, q=q, sg=sg, f=f, k=k, e_q=e_q, e_k=e_k, e_b=e_b, e_l=e_l,
                qd=q * e_q, kd=k * e_k, qe=q * e_b, ke=k * e_l, e_end=jnp.exp(bl))


def _mixer_fwd(h0, g_mix, w_in, wr, wi, vec, hb, g_hg, shards):
    t_pad = h0.shape[0]
    nt = t_pad // TM
    nc_t = TM // HC
    nsh = len(shards)

    def body(h_ref, gmix_ref, win_ref, wr_ref, wi_ref, vec_ref, hb_ref, ghg_ref, *rest):
        sh_refs, rest = rest[:nsh], rest[nsh:]
        pout_ref, uout_ref, y_ref, hs_ref, o_ref, sc_ref = rest[:6]
        gath_refs, rest = rest[6:6 + nsh], rest[6 + nsh:]
        xbuf, a_s, b_s, hcar, st, qd_s, kd_s, qe_s, ke_s, v_s, u_s, p_s, p_ref = rest[:13]
        i = pl.program_id(0)
        tail = _carry_gather(_Gather(sh_refs, gath_refs, rest[13:]), i, nt + 1)

        @pl.when(i == 0)
        def _():
            p_s[...] = jnp.zeros_like(p_s)

        p_ref[...] = p_s[...]

        @pl.when(i <= 1)
        def _():
            xbuf[pl.ds(0, 8), :] = jnp.zeros((8, D_RG), _F32)
            hcar[...] = jnp.zeros_like(hcar)
            st[...] = jnp.zeros_like(st)

        n_h, _ = _rms_fwd(h_ref[...])
        u = (n_h * gmix_ref[...]).astype(_BF)
        uout_ref[...] = u
        pieces = [(j, k) for j in range(4) for k in range(WIN_P // 256)]

        def project(count):
            for _ in range(count):
                j, k = pieces.pop(0)
                blk = _dot(u, win_ref[j, :, pl.ds(256 * k, 256)])
                p_s[:, pl.ds(WIN_P * j + 256 * k, 256)] = blk
                pout_ref[:, pl.ds(WIN_P * j + 256 * k, 256)] = blk

        x = p_ref[:, pl.ds(0, D_RG)]
        xbuf[pl.ds(8, TM), :] = x
        xc = _conv(xbuf, vec_ref)
        xbuf[pl.ds(0, 8), :] = x[TM - 8:, :]
        r, ig, a, s, _ = _rg_gates(xc, wr_ref, wi_ref, vec_ref)
        a_s[...] = a
        b_s[...] = s * (ig * xc)

        def step(t, h):
            h = a_s[pl.ds(t, 1), :] * h + b_s[pl.ds(t, 1), :]
            hs_ref[pl.ds(t, 1), :] = h
            return h

        hcar[pl.ds(0, 1), :] = lax.fori_loop(0, TM, step, hcar[pl.ds(0, 1), :], unroll=8)
        gel, _ = _gelu_parts(p_ref[:, pl.ds(D_RG, D_RG)])
        n, _ = _rms_fwd(gel * hs_ref[...])
        y_ref[:, pl.ds(0, D_RG)] = (n * vec_ref[R_GRG:R_GRG + 1, :]).astype(_BF)

        lb = _sigmoid(hb_ref[0:1, :] - hb_ref[1:2, :])
        _, tri_blk, _ = _chunk_masks()
        q = _hg_prep(p_ref, lb, tri_blk.astype(_BF))
        for name, ref in (("qd", qd_s), ("kd", kd_s), ("qe", qe_s), ("ke", ke_s)):
            ref[...] = q[name].astype(_BF)
        v_s[...] = p_ref[:, pl.ds(2 * D_RG + 2 * D_HG, D_HG)].astype(_BF)
        e_end = q["e_end"]
        causal = (lax.broadcasted_iota(jnp.int32, (HC, HC), 0) >= lax.broadcasted_iota(jnp.int32, (HC, HC), 1))
        for c in range(nc_t):
            for h in range(NH):
                rs, cs = pl.ds(HC * c, HC), pl.ds(HD * h, HD)
                amat = jnp.where(causal, _dot_nt(qd_s[rs, cs], kd_s[rs, cs]), 0.0)
                o_ref[rs, cs] = _dot(amat.astype(_BF), v_s[rs, cs])
                u_s[NH * c + h] = _dot_tn(v_s[rs, cs], ke_s[rs, cs])
                if pieces:
                    project(1)
        assert not pieces
        for h in range(NH):
            cs = pl.ds(HD * h, HD)
            s_run = st[h]
            for c in range(nc_t):
                rs = pl.ds(HC * c, HC)
                sc_ref[c, h] = s_run
                o_ref[rs, cs] += _dot_nt(qe_s[rs, cs], s_run.astype(_BF))
                s_run = e_end[HC * c:HC * c + 1, HD * h:HD * (h + 1)] * s_run + u_s[NH * c + h]
            st[h] = s_run
        for h in range(NH):
            cs = pl.ds(HD * h, HD)
            n_o, _ = _rms_fwd(o_ref[:, cs])
            hg = p_ref[:, pl.ds(2 * D_RG + 3 * D_HG + HD * h, HD)]
            y_ref[:, pl.ds(D_RG + HD * h, HD)] = ((n_o * ghg_ref[...]) * (hg * _sigmoid(hg))).astype(_BF)

        tail()

    hbm = pl.BlockSpec(memory_space=pl.ANY)

    def proj(i):
        return jnp.minimum(i, nt - 1)

    def mixed(i):
        return jnp.maximum(i - 1, 0)

    return pl.pallas_call(
        body, name="mixer_fwd", grid=(nt + 1,),
        in_specs=[pl.BlockSpec((TM, D), lambda i: (proj(i), 0)), _full((1, D)), _const((4, D, WIN_P)),
                  _full((D_RG, D_RG)), _full((D_RG, D_RG)),
                  _full((16, D_RG)), _full((2, D_HG)), _full((1, HD))] + [hbm] * nsh,
        out_specs=[pl.BlockSpec((TM, D_IN), lambda i: (proj(i), 0)), pl.BlockSpec((TM, D), lambda i: (proj(i), 0)),
                   pl.BlockSpec((TM, D), lambda i: (mixed(i), 0)), pl.BlockSpec((TM, D_RG), lambda i: (mixed(i), 0)),
                   pl.BlockSpec((TM, D_HG), lambda i: (mixed(i), 0)),
                   pl.BlockSpec((nc_t, NH, HD, HD), lambda i: (mixed(i), 0, 0, 0))] + [hbm] * nsh,
        out_shape=[_S((t_pad, D_IN), _F32), _S((t_pad, D), _BF),
                   _S((t_pad, D), _BF), _S((t_pad, D_RG), _F32), _S((t_pad, D_HG), _F32),
                   _S((t_pad // HC, NH, HD, HD), _F32)] + [_S((N_DEV,) + s.shape, s.dtype) for s in shards],
        scratch_shapes=[pltpu.VMEM((TM + 8, D_RG), _F32), pltpu.VMEM((TM, D_RG), _F32),
                        pltpu.VMEM((TM, D_RG), _F32), pltpu.VMEM((8, D_RG), _F32),
                        pltpu.VMEM((NH, HD, HD), _F32)] + [pltpu.VMEM((TM, D_HG), _BF) for _ in range(5)]
        + [pltpu.VMEM((nc_t * NH, HD, HD), _F32), pltpu.VMEM((TM, D_IN), _F32), pltpu.VMEM((TM, D_IN), _F32)]
        + _sem_shapes(nsh),
        compiler_params=_cp(("arbitrary",)),
    )(h0, g_mix, w_in, wr, wi, vec, hb, g_hg, *shards)


def _outproj(h0, y, w_out, g_ffn):
    t_pad = h0.shape[0]

    def body(h_ref, y_ref, w_ref, g_ref, h1_ref, v_ref):
        h1 = h_ref[...] + _dot(y_ref[...], w_ref[...])
        h1_ref[...] = h1
        n, _ = _rms_fwd(h1)
        v_ref[...] = (n * g_ref[...]).astype(_BF)

    return pl.pallas_call(
        body, name="outproj", grid=(t_pad // TM,),
        in_specs=[pl.BlockSpec((TM, D), lambda i: (i, 0)), pl.BlockSpec((TM, D), lambda i: (i, 0)),
                  _full((D, D)), _full((1, D))],
        out_specs=[pl.BlockSpec((TM, D), lambda i: (i, 0)), pl.BlockSpec((TM, D), lambda i: (i, 0))],
        out_shape=[_S((t_pad, D), _F32), _S((t_pad, D), _BF)],
        compiler_params=_cp(("arbitrary",)),
    )(h0, y, w_out, g_ffn)


def _ffn_loss(v, h1, w_gu, w_down, g_fin, tgt, n_valid):
    t_pad = v.shape[0]

    def body(v_ref, h1_ref, wgu_ref, wd_ref, g_ref, t_ref, gu_ref, act_ref, dh2_ref, dh2b_ref, loss_ref, gfin_ref):
        i = pl.program_id(0)

        @pl.when(i == 0)
        def _():
            loss_ref[...] = jnp.zeros_like(loss_ref)
            gfin_ref[...] = jnp.zeros_like(gfin_ref)

        vb = v_ref[...]
        h2 = h1_ref[...]
        for b in range(4):
            gate = _dot_nt(vb, wgu_ref[b])
            up = _dot_nt(vb, wgu_ref[4 + b])
            gu_ref[b] = gate
            gu_ref[4 + b] = up
            act = ((gate * _sigmoid(gate)) * up).astype(_BF)
            act_ref[b] = act
            h2 = h2 + _dot(act, wd_ref[b])
        n, r = _rms_fwd(h2)
        out = n * g_ref[...]
        row = i * TM + lax.broadcasted_iota(jnp.int32, (TM, 1), 0)
        valid = (row >= N_META) & (row < n_valid)
        err = jnp.where(valid, out - t_ref[...], 0.0)
        loss_ref[...] += (0.5 / D) * jnp.sum(err * err)
        dout = err * (1.0 / D)
        gfin_ref[...] += jnp.sum(dout * n, axis=0, keepdims=True)
        dh2 = _rms_bwd(dout * g_ref[...], n, r)
        dh2_ref[...] = dh2
        dh2b_ref[...] = dh2.astype(_BF)

    return pl.pallas_call(
        body, name="ffn_loss", grid=(t_pad // TM,),
        in_specs=[pl.BlockSpec((TM, D), lambda i: (i, 0)), pl.BlockSpec((TM, D), lambda i: (i, 0)),
                  _const((N_DEV, FFB, D)), _const((4, FFB, D)), _full((1, D)),
                  pl.BlockSpec((TM, D), lambda i: (i, 0))],
        out_specs=[pl.BlockSpec((N_DEV, TM, FFB), lambda i: (0, i, 0)), pl.BlockSpec((4, TM, FFB), lambda i: (0, i, 0)),
                   pl.BlockSpec((TM, D), lambda i: (i, 0)), pl.BlockSpec((TM, D), lambda i: (i, 0)),
                   _full((8, 128)), _full((1, D))],
        out_shape=[_S((N_DEV, t_pad, FFB), _F32), _S((4, t_pad, FFB), _BF), _S((t_pad, D), _F32),
                   _S((t_pad, D), _BF), _S((8, 128), _F32), _S((1, D), _F32)],
        compiler_params=_cp(("arbitrary",)),
    )(v, h1, w_gu, w_down, g_fin, tgt)


def _ffn_bwd(dh2, dh2b, gu, h1, g_ffn, w_gu, w_down, w_out):
    t_pad = dh2.shape[0]

    def body(dh2_ref, dh2b_ref, gu_ref, h1_ref, g_ref, wgu_ref, wd_ref, wo_ref,
             dgu_ref, dh1_ref, dh1b_ref, dy_ref, gffn_ref):
        i = pl.program_id(0)

        @pl.when(i == 0)
        def _():
            gffn_ref[...] = jnp.zeros_like(gffn_ref)

        db = dh2b_ref[...]
        dv = jnp.zeros((TM, D), _F32)
        for b in range(4):
            dact = _dot_nt(db, wd_ref[b])
            gate = gu_ref[b]
            up = gu_ref[4 + b]
            sg = _sigmoid(gate)
            dgate = ((dact * up) * _dsilu(gate, sg)).astype(_BF)
            dup = (dact * (gate * sg)).astype(_BF)
            dgu_ref[b] = dgate
            dgu_ref[4 + b] = dup
            dv = dv + _dot(dgate, wgu_ref[b]) + _dot(dup, wgu_ref[4 + b])
        n, r = _rms_fwd(h1_ref[...])
        gffn_ref[...] += jnp.sum(dv * n, axis=0, keepdims=True)
        dh1 = dh2_ref[...] + _rms_bwd(dv * g_ref[...], n, r)
        dh1_ref[...] = dh1
        dh1b = dh1.astype(_BF)
        dh1b_ref[...] = dh1b
        dy_ref[...] = _dot_nt(dh1b, wo_ref[...])

    tile = pl.BlockSpec((TM, D), lambda i: (i, 0))
    return pl.pallas_call(
        body, name="ffn_bwd", grid=(t_pad // TM,),
        in_specs=[tile, tile, pl.BlockSpec((N_DEV, TM, FFB), lambda i: (0, i, 0)), tile, _full((1, D)),
                  _const((N_DEV, FFB, D)), _const((4, FFB, D)), _const((D, D))],
        out_specs=[pl.BlockSpec((N_DEV, TM, FFB), lambda i: (0, i, 0)), tile, tile, tile, _full((1, D))],
        out_shape=[_S((N_DEV, t_pad, FFB), _BF), _S((t_pad, D), _F32), _S((t_pad, D), _BF),
                   _S((t_pad, D), _F32), _S((1, D), _F32)],
        compiler_params=_cp(("arbitrary",)),
    )(dh2, dh2b, gu, h1, g_ffn, w_gu, w_down, w_out)


def _mixer_bwd(p, hs, o, sc, dy, wr, wi, vec, hb, g_hg, scatter):
    t_pad = p.shape[0]
    nt = t_pad // TM
    nc_t = TM // HC
    nsc = len(scatter)

    def rev(i):
        return nt - 1 - i

    def body(p_ref, pprev_ref, hs_ref, hprev_ref, o_ref, sc_ref, dy_ref, wr_ref, wi_ref, vec_ref, hb_ref, ghg_ref,
             *rest):
        send_refs, rest = rest[:nsc], rest[nsc:]
        dp_ref, gvec_ref, gw_ref = rest[:3]
        recv_refs, rest = rest[3:3 + nsc], rest[3 + nsc:]
        xbuf, hbuf, dbuf, a_s, g_s, ccar, dst = rest[:7]
        qd_s, kd_s, qe_s, ke_s, v_s, do_s, dqd_s, dkd_s, dqe_s, dke_s, dv_s, w_s, dend_s = rest[7:20]
        exchange = _Exchange(send_refs, [], recv_refs, rest[20:])
        i = pl.program_id(0)
        first_tile = i == nt - 1

        @pl.when(i == 0)
        def _():
            exchange.start()
            gvec_ref[...] = jnp.zeros_like(gvec_ref)
            gw_ref[...] = jnp.zeros_like(gw_ref)
            dbuf[pl.ds(TM, 8), :] = jnp.zeros((8, D_RG), _F32)
            ccar[...] = jnp.zeros_like(ccar)
            dst[...] = jnp.zeros_like(dst)

        def acc(row, val):
            gvec_ref[row:row + 1, :] += jnp.sum(val, axis=0, keepdims=True)

        keep = jnp.where(first_tile, 0.0, 1.0)
        x = p_ref[:, pl.ds(0, D_RG)]
        xbuf[pl.ds(0, 8), :] = pprev_ref[...] * keep
        xbuf[pl.ds(8, TM), :] = x
        xc = _conv(xbuf, vec_ref)
        r, ig, a, s, nsp8 = _rg_gates(xc, wr_ref, wi_ref, vec_ref)
        h = hs_ref[...]
        hbuf[pl.ds(0, 8), :] = hprev_ref[...] * keep
        hbuf[pl.ds(8, TM), :] = h
        hm1 = hbuf[pl.ds(7, TM), :]
        gr = p_ref[:, pl.ds(D_RG, D_RG)]
        gel, dgel = _gelu_parts(gr)
        n, rr = _rms_fwd(gel * h)
        dyn = dy_ref[:, pl.ds(0, D_RG)]
        acc(R_GRG, dyn * n)
        dpre = _rms_bwd(dyn * vec_ref[R_GRG:R_GRG + 1, :], n, rr)
        dp_ref[:, pl.ds(D_RG, D_RG)] = ((dpre * h) * dgel).astype(_BF)
        a_s[...] = a
        g_s[...] = dpre * gel

        def step(k, c):
            t = TM - 1 - k
            g = g_s[pl.ds(t, 1), :] + c
            g_s[pl.ds(t, 1), :] = g
            return a_s[pl.ds(t, 1), :] * g

        ccar[pl.ds(0, 1), :] = lax.fori_loop(0, TM, step, ccar[pl.ds(0, 1), :], unroll=8)
        gt = g_s[...]
        da = gt * hm1
        ixc = ig * xc
        ds = gt * ixc
        dig = (gt * s) * xc
        dxc = (gt * s) * ig
        dla = da * a - ds * ((a * a) / s)
        lam = vec_ref[R_LAM:R_LAM + 1, :]
        gvec_ref[R_LAM:R_LAM + 1, :] += jnp.sum(dla * r, axis=0, keepdims=True) * (LRU_C * _sigmoid(-lam))
        dzr = (dla * nsp8) * (r * (1.0 - r))
        dzi = dig * (ig * (1.0 - ig))
        acc(R_BR, dzr)
        acc(R_BI, dzi)
        xcb = xc.astype(_BF)
        dzrb = dzr.astype(_BF)
        dzib = dzi.astype(_BF)
        gw_ref[0] += _dot_tn(xcb, dzrb)
        gw_ref[1] += _dot_tn(xcb, dzib)
        dxc = dxc + _dot_nt(dzrb, wr_ref[...]) + _dot_nt(dzib, wi_ref[...])
        acc(R_CONVB, dxc)
        for j in range(4):
            acc(R_CONVW + j, dxc * xbuf[pl.ds(5 + j, TM), :])
        dbuf[pl.ds(0, TM), :] = dxc
        dx = vec_ref[R_CONVW + 3:R_CONVW + 4, :] * dxc
        for j in range(3):
            dx = dx + vec_ref[R_CONVW + j:R_CONVW + j + 1, :] * dbuf[pl.ds(3 - j, TM), :]
        dbuf[pl.ds(TM, 8), :] = dxc[0:8, :]
        dp_ref[:, pl.ds(0, D_RG)] = dx.astype(_BF)

        lb = _sigmoid(hb_ref[0:1, :] - hb_ref[1:2, :])
        same, tri_blk, triu_blk = _chunk_masks()
        q = _hg_prep(p_ref, lb, tri_blk.astype(_BF))
        qdb, kdb = q["qd"].astype(_BF), q["kd"].astype(_BF)
        qd_s[...] = qdb
        kd_s[...] = kdb
        qe_s[...] = q["qe"].astype(_BF)
        ke_s[...] = q["ke"].astype(_BF)
        v_s[...] = p_ref[:, pl.ds(2 * D_RG + 2 * D_HG, D_HG)].astype(_BF)
        e_end = q["e_end"]
        ghg = ghg_ref[...]
        for h in range(NH):
            cs = pl.ds(HD * h, HD)
            hg = p_ref[:, pl.ds(2 * D_RG + 3 * D_HG + HD * h, HD)]
            sh = _sigmoid(hg)
            n_o, r_o = _rms_fwd(o_ref[:, cs])
            dyh = dy_ref[:, pl.ds(D_RG + HD * h, HD)]
            dp_ref[:, pl.ds(2 * D_RG + 3 * D_HG + HD * h, HD)] = ((dyh * (n_o * ghg)) * _dsilu(hg, sh)).astype(_BF)
            dn = dyh * (hg * sh)
            gvec_ref[R_GHG:R_GHG + 1, pl.ds(0, HD)] += jnp.sum(dn * n_o, axis=0, keepdims=True)
            do_s[:, cs] = _rms_bwd(dn * ghg, n_o, r_o).astype(_BF)
        causal = (lax.broadcasted_iota(jnp.int32, (HC, HC), 0) >= lax.broadcasted_iota(jnp.int32, (HC, HC), 1))
        for c in range(nc_t):
            for h in range(NH):
                rs, cs = pl.ds(HC * c, HC), pl.ds(HD * h, HD)
                qd_c, kd_c, do_c = qd_s[rs, cs], kd_s[rs, cs], do_s[rs, cs]
                amat = jnp.where(causal, _dot_nt(qd_c, kd_c), 0.0).astype(_BF)
                da_m = jnp.where(causal, _dot_nt(do_c, v_s[rs, cs]), 0.0).astype(_BF)
                dqd_s[rs, cs] = _dot(da_m, kd_c)
                dkd_s[rs, cs] = _dot_tn(da_m, qd_c)
                dqe_s[rs, cs] = _dot(do_c, sc_ref[c, h].astype(_BF))
                dv_s[rs, cs] = _dot_tn(amat, do_c)
                w_s[NH * c + h] = _dot_tn(do_c, qe_s[rs, cs])
        for h in range(NH):
            cs = pl.ds(HD * h, HD)
            d_run = dst[h]
            for c in reversed(range(nc_t)):
                rs = pl.ds(HC * c, HC)
                d_b = d_run.astype(_BF)
                dke_s[rs, cs] = _dot(v_s[rs, cs], d_b)
                dp_ref[rs, pl.ds(2 * D_RG + 2 * D_HG + HD * h, HD)] = (
                    dv_s[rs, cs] + _dot_nt(ke_s[rs, cs], d_b)).astype(_BF)
                dend_s[pl.ds(c, 1), cs] = jnp.sum(sc_ref[c, h] * d_run, axis=0, keepdims=True)
                d_run = w_s[NH * c + h] + e_end[HC * c:HC * c + 1, HD * h:HD * (h + 1)] * d_run
            dst[h] = d_run
        dqd, dkd, dqe, dke = dqd_s[...], dkd_s[...], dqe_s[...], dke_s[...]
        dq = dqd * q["e_q"] + dqe * q["e_b"]
        dk = dkd * q["e_k"] + dke * q["e_l"]
        dkeke = dke * q["ke"]
        db = dqd * qdb.astype(_F32) - dkd * kdb.astype(_F32) + dqe * q["qe"] - dkeke
        d_end = jnp.concatenate([jnp.broadcast_to(dend_s[pl.ds(c, 1), :], (HC, D_HG)) for c in range(nc_t)], axis=0)
        dlf = _dot3(triu_blk.astype(_BF), db) + _dot3(same.astype(_BF), dkeke) + d_end * e_end
        df = dlf / q["f"] - dk
        sg = q["sg"]
        gvec_ref[R_HB0:R_HB0 + 1, :] += jnp.sum(df * (1.0 - sg), axis=0, keepdims=True)
        dp_ref[:, pl.ds(2 * D_RG, D_HG)] = (dq * _dsilu(q["hq"], q["sq"])).astype(_BF)
        dp_ref[:, pl.ds(2 * D_RG + D_HG, D_HG)] = ((df * (1.0 - lb)) * (sg * (1.0 - sg))).astype(_BF)

        @pl.when(i == nt - 1)
        def _():
            glb = gvec_ref[R_HB0:R_HB0 + 1, :] * (lb * (1.0 - lb))
            gvec_ref[R_HB0:R_HB0 + 1, :] = glb
            gvec_ref[R_HB1:R_HB1 + 1, :] = -glb
            exchange.finish()

    hbm = pl.BlockSpec(memory_space=pl.ANY)
    return pl.pallas_call(
        body, name="mixer_bwd", grid=(nt,),
        in_specs=[pl.BlockSpec((TM, D_IN), lambda i: (rev(i), 0)),
                  pl.BlockSpec((8, D_RG), lambda i: (jnp.maximum(rev(i) * (TM // 8) - 1, 0), 0)),
                  pl.BlockSpec((TM, D_RG), lambda i: (rev(i), 0)),
                  pl.BlockSpec((8, D_RG), lambda i: (jnp.maximum(rev(i) * (TM // 8) - 1, 0), 0)),
                  pl.BlockSpec((TM, D_HG), lambda i: (rev(i), 0)),
                  pl.BlockSpec((nc_t, NH, HD, HD), lambda i: (rev(i), 0, 0, 0)),
                  pl.BlockSpec((TM, D), lambda i: (rev(i), 0)),
                  _full((D_RG, D_RG)), _full((D_RG, D_RG)), _full((16, D_RG)), _full((2, D_HG)), _full((1, HD))]
        + [hbm] * nsc,
        out_specs=[pl.BlockSpec((TM, D_IN), lambda i: (rev(i), 0)), _full((16, D_RG)), _full((2, D_RG, D_RG))]
        + [hbm] * nsc,
        out_shape=[_S((t_pad, D_IN), _BF), _S((16, D_RG), _F32), _S((2, D_RG, D_RG), _F32)]
        + [_S(s.shape, s.dtype) for s in scatter],
        scratch_shapes=[pltpu.VMEM((TM + 8, D_RG), _F32), pltpu.VMEM((TM + 8, D_RG), _F32),
                        pltpu.VMEM((TM + 8, D_RG), _F32), pltpu.VMEM((TM, D_RG), _F32),
                        pltpu.VMEM((TM, D_RG), _F32), pltpu.VMEM((8, D_RG), _F32),
                        pltpu.VMEM((NH, HD, HD), _F32)]
        + [pltpu.VMEM((TM, D_HG), _BF) for _ in range(6)] + [pltpu.VMEM((TM, D_HG), _F32) for _ in range(5)]
        + [pltpu.VMEM((nc_t * NH, HD, HD), _F32), pltpu.VMEM((8, D_HG), _F32)] + _sem_shapes(nsc),
        compiler_params=_cp(("arbitrary",)),
    )(p, p, hs, hs, o, sc, dy, wr, wi, vec, hb, g_hg, *scatter)


def _inproj_bwd_send(dp, w_in, h0, dh1, g_mix, u, order, gffn, gfin, loss, to_all):
    t_pad = dp.shape[0]
    rb = t_pad // (2 * N_DEV)
    n_steps = N_DEV + 2 * N_DEV
    na = len(to_all)

    def body(order_ref, dpc_ref, dpr_ref, u_ref, w_ref, h_ref, dh1_ref, g_ref, gffn_ref, gfin_ref, loss_ref, *rest):
        all_in = rest[:na]
        dh0_ref, recv_ref = rest[na:na + 2]
        all_out = rest[na + 2:2 * na + 2]
        alla_ref = rest[2 * na + 2]
        buf, pack, blk_send, blk_recv, blk_local = rest[2 * na + 3:2 * na + 8]
        exchange = _Exchange([], all_in, all_out, rest[2 * na + 8:2 * na + 11])
        last = _Exchange([], [pack], [alla_ref], rest[2 * na + 11:])
        s = pl.program_id(0)
        x, y, c = _coords()
        me = 4 * x + 2 * y + c

        def send(step):
            r = _SEND_ORDER[step]
            return pltpu.make_async_remote_copy(
                src_ref=buf.at[step], dst_ref=recv_ref.at[me], send_sem=blk_send.at[step], recv_sem=blk_recv.at[r - 1],
                device_id=(x ^ (r >> 2), y ^ ((r >> 1) & 1), c ^ (r & 1)), device_id_type=_MESH)

        @pl.when(s == 0)
        def _():
            exchange.start()
            pack[...] = jnp.zeros_like(pack)

        @pl.when(s < N_DEV)
        def _():
            buf[s] = _dot_tn(u_ref[...], dpc_ref[...]).astype(_BF)

            for step in range(N_DEV - 1):
                @pl.when(s == step)
                def _(step=step):
                    send(step).start()

        @pl.when(s >= N_DEV)
        def _():
            du = jnp.zeros((rb, D), _F32)
            for j in range(4):
                du = du + _dot_nt(dpr_ref[:, WIN_P * j:WIN_P * (j + 1)], w_ref[j])
            n, r = _rms_fwd(h_ref[...])
            pack[R_GMIX:R_GMIX + 1, :] += jnp.sum(du * n, axis=0, keepdims=True)
            dh0 = dh1_ref[...] + _rms_bwd(du * g_ref[...], n, r)
            dh0_ref[...] = dh0

            @pl.when(s == N_DEV)
            def _():
                pack[R_META:R_META + N_META, :] = dh0[0:N_META, :]

        @pl.when(s == n_steps - 1)
        def _():
            pack[R_GFFN:R_GFFN + 1, :] = gffn_ref[...]
            pack[R_GFIN:R_GFIN + 1, :] = gfin_ref[...]
            pack[R_LOSS:R_LOSS + 1, pl.ds(0, 128)] = loss_ref[0:1, :]
            last.start()
            mine = pltpu.make_async_copy(buf.at[N_DEV - 1], recv_ref.at[me], blk_local.at[0])
            mine.start()
            for step in range(N_DEV - 1):
                send(step).wait_send()
            for r in range(1, N_DEV):
                px, py, pc = x ^ (r >> 2), y ^ ((r >> 1) & 1), c ^ (r & 1)
                pltpu.make_async_remote_copy(
                    src_ref=buf.at[0], dst_ref=recv_ref.at[4 * px + 2 * py + pc], send_sem=blk_send.at[0],
                    recv_sem=blk_recv.at[r - 1], device_id=(px, py, pc), device_id_type=_MESH).wait_recv()
            mine.wait()
            exchange.finish()
            last.finish()

    hbm = pl.BlockSpec(memory_space=pl.ANY)
    rows = pl.BlockSpec((rb, D), lambda s, order: (jnp.maximum(s - N_DEV, 0), 0))
    one = pl.BlockSpec((1, D), lambda s, order: (0, 0))
    res = pl.pallas_call(
        body, name="inproj_bwd_send",
        grid_spec=pltpu.PrefetchScalarGridSpec(
            num_scalar_prefetch=1, grid=(n_steps,),
            in_specs=[pl.BlockSpec((t_pad, WIN_B), lambda s, order: (0, order[jnp.minimum(s, N_DEV - 1)])),
                      pl.BlockSpec((rb, D_IN), lambda s, order: (jnp.maximum(s - N_DEV, 0), 0)),
                      pl.BlockSpec((t_pad, D), lambda s, order: (0, 0), pipeline_mode=pl.Buffered(1)),
                      pl.BlockSpec((4, D, WIN_P), lambda s, order: (0, 0, 0), pipeline_mode=pl.Buffered(1)),
                      rows, rows, one, one, one, pl.BlockSpec((8, 128), lambda s, order: (0, 0))] + [hbm] * na,
            out_specs=[rows] + [hbm] * (na + 2),
            scratch_shapes=[pltpu.VMEM((N_DEV, D, WIN_B), _BF), pltpu.VMEM((24, D), _F32),
                            pltpu.SemaphoreType.DMA((N_DEV - 1,)), pltpu.SemaphoreType.DMA((N_DEV - 1,)),
                            pltpu.SemaphoreType.DMA((1,))] + _sem_shapes(na) + _sem_shapes(1)),
        out_shape=[_S((t_pad, D), _F32), _S((N_DEV, D, WIN_B), _BF)]
        + [_S((N_DEV,) + g.shape, g.dtype) for g in to_all] + [_S((N_DEV, 24, D), _F32)],
        compiler_params=_cp(("arbitrary",)),
    )(order, dp, dp, u, w_in, h0, dh1, g_mix, gffn, gfin, loss, *to_all)
    return res


def _wgrad(name, a, b, a_spec, b_spec, n_blocks, out_block, scatter=()):
    nsc = len(scatter)

    def body(a_ref, b_ref, *rest):
        o_ref = rest[nsc]
        j = pl.program_id(0)
        if nsc:
            exchange = _Exchange(rest[:nsc], [], rest[nsc + 1:2 * nsc + 1], rest[2 * nsc + 1:])

            @pl.when(j == 0)
            def _():
                exchange.start()

        av = a_ref[0] if len(a_ref.shape) == 3 else a_ref[...]
        bv = b_ref[0] if len(b_ref.shape) == 3 else b_ref[...]
        o_ref[0] = _dot_tn(av, bv).astype(_BF)

        if nsc:
            @pl.when(j == n_blocks - 1)
            def _():
                exchange.finish()

    hbm = pl.BlockSpec(memory_space=pl.ANY)
    res = pl.pallas_call(
        body, name=name, grid=(n_blocks,),
        in_specs=[a_spec, b_spec] + [hbm] * nsc,
        out_specs=[pl.BlockSpec((1,) + out_block, lambda j: (j, 0, 0))] + [hbm] * nsc,
        out_shape=[_S((n_blocks,) + out_block, _BF)] + [_S(s.shape, s.dtype) for s in scatter],
        scratch_shapes=_sem_shapes(nsc) if nsc else [],
        compiler_params=_cp(("arbitrary",)),
    )(a, b, *scatter)
    return res if nsc else res[0]


def _coords():
    return lax.axis_index("x"), lax.axis_index("y"), lax.axis_index("c")


def _sem_shapes(na):
    return [pltpu.SemaphoreType.DMA((7 * na,)), pltpu.SemaphoreType.DMA((7 * na,)), pltpu.SemaphoreType.DMA((na,))]


class _Gather:
    def __init__(self, srcs, outs, sems, place=None):
        self.srcs, self.outs = srcs, outs
        self.send_sems, self.recv_sems, self.local_sems = sems
        self.place = place if place is not None else (lambda ref, block: ref.at[block])
        self.na = len(srcs)
        x, y, c = _coords()
        self.pos = (x, y, c)
        self.me = 4 * x + 2 * y + c
        self.sibling = (x, y, 1 - c)
        self.chips = [(1 - x, y), (x, 1 - y), (1 - x, 1 - y)]

    @staticmethod
    def _slot(px, py, pc):
        return 4 * px + 2 * py + pc

    def _copy(self, a, k, block, to, own=False):
        dst = self.place(self.outs[a], block)
        return pltpu.make_async_remote_copy(
            src_ref=self.srcs[a] if own else dst, dst_ref=dst,
            send_sem=self.send_sems.at[7 * a + k], recv_sem=self.recv_sems.at[7 * a + k],
            device_id=to, device_id_type=_MESH)

    def _mine(self, a):
        return pltpu.make_async_copy(self.srcs[a], self.place(self.outs[a], self.me), self.local_sems.at[a])

    def _first(self):
        c = self.pos[2]
        cps = []
        for a in range(self.na):
            cps.append(self._copy(a, 0, self.me, self.sibling, own=True))
            cps += [self._copy(a, 1 + j, self.me, (*chip, c), own=True) for j, chip in enumerate(self.chips)]
        return cps

    def _passed(self):
        c = self.pos[2]
        return [self._copy(a, 4 + j, self._slot(*chip, c), self.sibling)
                for j, chip in enumerate(self.chips) for a in range(self.na)]

    def start(self):
        for a in range(self.na):
            self._mine(a).start()
        for cp in self._first():
            cp.start()

    def forward(self, j):
        c = self.pos[2]
        chip = self.chips[j]
        for a in range(self.na):
            self._copy(a, 1 + j, self._slot(*chip, c), self.pos).wait_recv()
            self._copy(a, 4 + j, self._slot(*chip, c), self.sibling).start()

    def wait_sibling(self):
        x, y, c = self.pos
        for a in range(self.na):
            self._copy(a, 0, self._slot(x, y, 1 - c), self.pos).wait_recv()

    def wait_passed(self, j):
        c = self.pos[2]
        for a in range(self.na):
            self._copy(a, 4 + j, self._slot(*self.chips[j], 1 - c), self.pos).wait_recv()

    def finish_sends(self):
        for cp in self._first() + self._passed():
            cp.wait_send()
        for a in range(self.na):
            self._mine(a).wait()

    def finish(self):
        self.wait_sibling()
        for j in range(3):
            self.wait_passed(j)
        self.finish_sends()


class _Exchange:
    def __init__(self, scatter, gather, outs, sems):
        self.ins = list(scatter) + list(gather)
        self.ns, self.na = len(scatter), len(scatter) + len(gather)
        self.outs = outs
        self.send_sems, self.recv_sems, self.local_sems = sems
        x, y, c = _coords()
        self.pos = (x, y, c)
        self.me = 4 * x + 2 * y + c

    def _peer(self, r):
        x, y, c = self.pos
        return x ^ (r >> 2), y ^ ((r >> 1) & 1), c ^ (r & 1)

    def _src(self, a, block):
        return self.ins[a].at[block] if a < self.ns else self.ins[a]

    def _local(self, a):
        return pltpu.make_async_copy(self._src(a, self.me), self.outs[a].at[self.me], self.local_sems.at[a])

    def _send(self, a, r):
        px, py, pc = self._peer(r)
        return pltpu.make_async_remote_copy(
            src_ref=self._src(a, 4 * px + 2 * py + pc), dst_ref=self.outs[a].at[self.me],
            send_sem=self.send_sems.at[7 * a + r - 1], recv_sem=self.recv_sems.at[7 * a + r - 1],
            device_id=(px, py, pc), device_id_type=_MESH)

    def _recv(self, a, r):
        px, py, pc = self._peer(r)
        return pltpu.make_async_remote_copy(
            src_ref=self._src(a, self.me), dst_ref=self.outs[a].at[4 * px + 2 * py + pc],
            send_sem=self.send_sems.at[7 * a + r - 1], recv_sem=self.recv_sems.at[7 * a + r - 1],
            device_id=(px, py, pc), device_id_type=_MESH)

    def start(self):
        for a in range(self.na):
            self._local(a).start()
        for r in range(1, N_DEV):
            for a in range(self.na):
                self._send(a, r).start()

    def finish(self):
        for r in range(1, N_DEV):
            for a in range(self.na):
                self._recv(a, r).wait_recv()
        for r in range(1, N_DEV):
            for a in range(self.na):
                self._send(a, r).wait_send()
        for a in range(self.na):
            self._local(a).wait()


def _prologue(x, tgt, small_l, w_in_l, cast_f32):
    seq = x.shape[0]
    assert seq % TM == 0
    nx = seq // TM
    nt = nx + 1
    nc = len(cast_f32)
    body_rows = TM - N_META

    def body(xm_ref, xp_ref, tm_ref, tp_ref, s_ref, w_ref, *rest):
        cins = rest[:nc]
        h0_ref, tgt_ref, small_ref, wg_ref = rest[nc:nc + 4]
        couts = rest[nc + 4:2 * nc + 4]
        s_stage, w_stage, meta, msem = rest[2 * nc + 4:2 * nc + 8]
        g_s = _Gather([s_stage], [small_ref], rest[2 * nc + 8:2 * nc + 11])
        g_w = _Gather([w_stage], [wg_ref], rest[2 * nc + 11:], place=_pair_place)
        s = pl.program_id(0)
        i = (s + 1) % nt

        @pl.when(s == 0)
        def _():
            s_stage[...] = s_ref[...]
            w_stage[...] = w_ref[...].astype(_BF)
            g_s.start()
            g_w.start()
            meta[...] = jnp.zeros_like(meta)
            for a in range(nc):
                couts[a][...] = cins[a][...].astype(_BF)

        @pl.when(s == nt - 1)
        def _():
            for j in range(3):
                g_s.forward(j)
            g_s.finish()
            cps = [pltpu.make_async_copy(small_ref.at[k, pl.ds(0, N_META), :], meta.at[:, pl.ds(128 * k, 128)],
                                         msem.at[k]) for k in range(N_DEV)]
            for cp in cps:
                cp.start()
            for cp in cps:
                cp.wait()
            for j in range(3):
                g_w.forward(j)
            g_w.finish()

        has_x = i < nx
        h0_ref[pl.ds(0, N_META), :] = jnp.where(i == 0, meta[...], xp_ref[...])
        h0_ref[pl.ds(N_META, body_rows), :] = jnp.where(has_x, xm_ref[pl.ds(0, body_rows), :], 0.0)
        tgt_ref[pl.ds(0, N_META), :] = jnp.where(i == 0, 0.0, tp_ref[...])
        tgt_ref[pl.ds(N_META, body_rows), :] = jnp.where(has_x, tm_ref[pl.ds(0, body_rows), :], 0.0)

    def tile_of(s):
        return (s + 1) % nt

    hbm = pl.BlockSpec(memory_space=pl.ANY)
    main = pl.BlockSpec((TM, D), lambda s: (jnp.minimum(tile_of(s), nx - 1), 0))
    prev = pl.BlockSpec((N_META, D), lambda s: (jnp.maximum(tile_of(s) * (TM // N_META) - 1, 0), 0))
    tile = pl.BlockSpec((TM, D), lambda s: (tile_of(s), 0))
    return pl.pallas_call(
        body, name="prologue", grid=(nt,),
        in_specs=[main, prev, main, prev, _const(small_l.shape), _const(w_in_l.shape)]
        + [_const(l.shape) for l in cast_f32],
        out_specs=[tile, tile, hbm, hbm] + [_full(l.shape) for l in cast_f32],
        out_shape=[_S((nt * TM, D), _F32), _S((nt * TM, D), _F32), _S((N_DEV,) + small_l.shape, _F32),
                   _S((4, D, WIN_P), _BF)] + [_S(l.shape, _BF) for l in cast_f32],
        scratch_shapes=[pltpu.VMEM(small_l.shape, _F32), pltpu.VMEM(w_in_l.shape, _BF), pltpu.VMEM((N_META, D), _F32),
                        pltpu.SemaphoreType.DMA((N_DEV,))] + _sem_shapes(1) + _sem_shapes(1),
        compiler_params=_cp(("arbitrary",)),
    )(x, x, tgt, tgt, small_l, w_in_l, *cast_f32)


def _adamw_math(w, g, m, v):
    m2 = ADAM_B1 * m + (1.0 - ADAM_B1) * g
    v2 = ADAM_B2 * v + (1.0 - ADAM_B2) * (g * g)
    m_hat = m2 / (1.0 - ADAM_B1 ** ADAM_STEP)
    v_hat = v2 / (1.0 - ADAM_B2 ** ADAM_STEP)
    delta = -ADAM_LR * (m_hat / (jnp.sqrt(v_hat) + ADAM_EPS) + ADAM_WD * w)
    return delta, m2, v2


def _adamw_big(name, recv, w, m, v, rows):
    r_all, c_all = w.shape

    def body(r_ref, w_ref, m_ref, v_ref, g_out, d_out, m_out, v_out):
        g = r_ref[0].astype(_F32)
        for k in range(1, N_DEV):
            g = g + r_ref[k].astype(_F32)
        delta, m2, v2 = _adamw_math(w_ref[...], g, m_ref[...], v_ref[...])
        g_out[...] = g
        d_out[...] = delta
        m_out[...] = m2
        v_out[...] = v2

    tile = pl.BlockSpec((rows, c_all), lambda i: (i, 0))
    return pl.pallas_call(
        body, name=name, grid=(r_all // rows,),
        in_specs=[pl.BlockSpec((N_DEV, rows, c_all), lambda i: (0, i, 0)), tile, tile, tile],
        out_specs=[tile] * 4,
        out_shape=[_S(w.shape, _F32)] * 4,
        compiler_params=_cp(("arbitrary",)),
    )(recv, w, m, v)


def _adamw_small(gathered, slices, wmv):
    ng, npar = len(gathered), len(slices)

    def body(*refs):
        g_refs = refs[:ng]
        wmv_refs = refs[ng:ng + 3 * npar]
        outs = refs[ng + 3 * npar:]
        for i, (ai, r0, nr, c0, ncol) in enumerate(slices):
            g = g_refs[ai][0, pl.ds(r0, nr), pl.ds(c0, ncol)].astype(_F32)
            for k in range(1, N_DEV):
                g = g + g_refs[ai][k, pl.ds(r0, nr), pl.ds(c0, ncol)].astype(_F32)
            w_ref, m_ref, v_ref = wmv_refs[3 * i:3 * i + 3]
            delta, m2, v2 = _adamw_math(w_ref[...], g, m_ref[...], v_ref[...])
            outs[4 * i][...] = g
            outs[4 * i + 1][...] = delta
            outs[4 * i + 2][...] = m2
            outs[4 * i + 3][...] = v2
        total = g_refs[0][0, pl.ds(R_LOSS, 1), pl.ds(0, 128)]
        for k in range(1, N_DEV):
            total = total + g_refs[0][k, pl.ds(R_LOSS, 1), pl.ds(0, 128)]
        outs[4 * npar][...] = total

    flat = [t for trip in wmv for t in trip]
    out_shape = []
    for w, _, _ in wmv:
        out_shape += [_S(w.shape, _F32)] * 4
    out_shape.append(_S((1, 128), _F32))
    return pl.pallas_call(
        body, name="adamw_small", out_shape=out_shape,
        compiler_params=pltpu.CompilerParams(vmem_limit_bytes=VMEM_LIMIT),
    )(*gathered, *flat)


def _block_diag(w):
    eye = jnp.eye(8, dtype=w.dtype)
    return (w[:, :, None, :] * eye[:, None, :, None]).reshape(D_RG, D_RG)


def _diag_blocks(g):
    return jnp.concatenate([g[64 * h:64 * (h + 1), 64 * h:64 * (h + 1)] for h in range(8)], axis=0)


def _local_step(h0, tgt_p, n_valid, g_mix, w_in, vec, wr, wi, hb, g_hg, w_out_l, g_ffn, w_gu_l, w_down_l, g_fin):
    t_pad = h0.shape[0]
    me = 4 * lax.axis_index("x") + 2 * lax.axis_index("y") + lax.axis_index("c")
    p, u, y, hs, o, sc, w_out, w_gu, w_down = _mixer_fwd(h0, g_mix, w_in, wr, wi, vec, hb, g_hg,
                                                         [w_out_l, w_gu_l, w_down_l])
    w_out = w_out.reshape(D, D)
    w_down = w_down.reshape(4, FFB, D)
    h1, v = _outproj(h0, y, w_out, g_ffn)
    gu, act, dh2, dh2b, loss, gfin = _ffn_loss(v, h1, w_gu, w_down, g_fin, tgt_p, n_valid)

    dgu, dh1, dh1b, dy, gffn = _ffn_bwd(dh2, dh2b, gu, h1, g_ffn, w_gu, w_down, w_out)
    g_wdown = _wgrad("wgrad_down", act, dh2b, pl.BlockSpec((1, t_pad, FFB), lambda j: (j, 0, 0)),
                     pl.BlockSpec((t_pad, D), lambda j: (0, 0)), 4, (FFB, D))
    g_wgu, r_wdown = _wgrad("wgrad_gate_up", dgu, v, pl.BlockSpec((1, t_pad, FFB), lambda j: (j, 0, 0)),
                            pl.BlockSpec((t_pad, D), lambda j: (0, 0)), N_DEV, (FFB, D),
                            scatter=[g_wdown.reshape(N_DEV, D_FF // N_DEV, D)])
    g_wout = _wgrad("wgrad_out", y, dh1b, pl.BlockSpec((t_pad, D // N_DEV), lambda j: (0, j)),
                    pl.BlockSpec((t_pad, D), lambda j: (0, 0)), N_DEV, (D // N_DEV, D))
    dp, gvec, gw, r_wgu, r_wout = _mixer_bwd(p, hs, o, sc, dy, wr, wi, vec, hb, g_hg, [g_wgu, g_wout])
    pack_c = jnp.concatenate([_diag_blocks(gw[0]), _diag_blocks(gw[1])], axis=1).astype(_BF)
    order = (me ^ jnp.array(_SEND_ORDER, jnp.int32)).astype(jnp.int32)
    dh0, r_win, all_b, all_c, all_a = _inproj_bwd_send(dp, w_in, h0, dh1, g_mix, u, order, gffn, gfin, loss,
                                                       [gvec, pack_c])
    return dh0, (r_win, r_wgu, r_wout, r_wdown), (all_a, all_b, all_c)


def kernel(x, meta_tokens, mix_norm_g, w_in, conv_w, conv_b, w_rgate, b_rgate, w_igate, b_igate, lru_lambda, rg_norm_g, hg_lower_bound, hg_norm_g, w_out, ffn_norm_g, w_gate_up, w_down, final_norm_g, loss_target, m_meta_tokens, m_mix_norm_g, m_w_in, m_conv_w, m_conv_b, m_w_rgate, m_b_rgate, m_w_igate, m_b_igate, m_lru_lambda, m_rg_norm_g, m_hg_lower_bound, m_hg_norm_g, m_w_out, m_ffn_norm_g, m_w_gate_up, m_w_down, m_final_norm_g, v_meta_tokens, v_mix_norm_g, v_w_in, v_conv_w, v_conv_b, v_w_rgate, v_b_rgate, v_w_igate, v_b_igate, v_lru_lambda, v_rg_norm_g, v_hg_lower_bound, v_hg_norm_g, v_w_out, v_ffn_norm_g, v_w_gate_up, v_w_down, v_final_norm_g):
    seq = x.shape[1]
    me = 4 * lax.axis_index("x") + 2 * lax.axis_index("y") + lax.axis_index("c")

    n_valid = N_META + seq
    small_l = jnp.concatenate([meta_tokens, jnp.pad(conv_w[0], ((0, 4), (0, 64)))], axis=0)
    h0, tgt_p, small_g, w_in_g, w_gu_l, w_out_l, w_down_l = _prologue(
        x[0], loss_target[0], small_l, w_in[0], [w_gate_up[0].T, w_out[0], w_down[0]])
    conv_w_full = jnp.transpose(small_g[:, N_META:N_META + 4, :64], (1, 0, 2)).reshape(4, D_RG)
    vec = jnp.concatenate([conv_b, b_rgate, b_igate, lru_lambda, rg_norm_g, jnp.zeros((3, D_RG), _F32),
                           conv_w_full, jnp.zeros((4, D_RG), _F32)], axis=0)
    wr = _block_diag(w_rgate[0]).astype(_BF)
    wi = _block_diag(w_igate[0]).astype(_BF)

    dh0, (r_win, r_wgu, r_wout, r_wdown), (all_a, all_b, all_c) = _local_step(
        h0, tgt_p, n_valid, mix_norm_g, w_in_g, vec, wr, wi, hg_lower_bound, hg_norm_g,
        w_out_l, ffn_norm_g, w_gu_l, w_down_l, final_norm_g.reshape(1, D))
    grad_x = dh0[N_META:N_META + seq][None]

    outs = {}
    outs["w_in"] = _adamw_big("adamw_w_in", r_win, w_in[0], m_w_in[0], v_w_in[0], 256)
    outs["w_gate_up"] = [r.T for r in _adamw_big("adamw_w_gate_up", r_wgu, w_gate_up[0].T, m_w_gate_up[0].T,
                                                 v_w_gate_up[0].T, 176)]
    outs["w_out"] = _adamw_big("adamw_w_out", r_wout, w_out[0], m_w_out[0], v_w_out[0], 128)
    outs["w_down"] = _adamw_big("adamw_w_down", r_wdown, w_down[0], m_w_down[0], v_w_down[0], 176)

    meta_part = lax.dynamic_slice_in_dim(all_a[:, R_META:R_META + N_META, :], me * 128, 128, axis=2)
    convw_part = lax.dynamic_slice_in_dim(all_b[:, R_CONVW:R_CONVW + 4, :], me * 64, 64, axis=2)
    gathered = [all_a, all_b, all_c, meta_part, convw_part]
    small_params = [
        ("meta_tokens", (3, 0, N_META, 0, 128), (meta_tokens, m_meta_tokens, v_meta_tokens), (N_META, 128)),
        ("mix_norm_g", (0, R_GMIX, 1, 0, D), (mix_norm_g, m_mix_norm_g, v_mix_norm_g), (1, D)),
        ("conv_w", (4, 0, 4, 0, 64), (conv_w, m_conv_w, v_conv_w), (4, 64)),
        ("conv_b", (1, R_CONVB, 1, 0, D_RG), (conv_b, m_conv_b, v_conv_b), (1, D_RG)),
        ("w_rgate", (2, 0, 512, 0, 64), (w_rgate, m_w_rgate, v_w_rgate), (512, 64)),
        ("b_rgate", (1, R_BR, 1, 0, D_RG), (b_rgate, m_b_rgate, v_b_rgate), (1, D_RG)),
        ("w_igate", (2, 0, 512, 64, 64), (w_igate, m_w_igate, v_w_igate), (512, 64)),
        ("b_igate", (1, R_BI, 1, 0, D_RG), (b_igate, m_b_igate, v_b_igate), (1, D_RG)),
        ("lru_lambda", (1, R_LAM, 1, 0, D_RG), (lru_lambda, m_lru_lambda, v_lru_lambda), (1, D_RG)),
        ("rg_norm_g", (1, R_GRG, 1, 0, D_RG), (rg_norm_g, m_rg_norm_g, v_rg_norm_g), (1, D_RG)),
        ("hg_lower_bound", (1, R_HB0, 2, 0, D_HG), (hg_lower_bound, m_hg_lower_bound, v_hg_lower_bound), (2, D_HG)),
        ("hg_norm_g", (1, R_GHG, 1, 0, HD), (hg_norm_g, m_hg_norm_g, v_hg_norm_g), (1, HD)),
        ("ffn_norm_g", (0, R_GFFN, 1, 0, D), (ffn_norm_g, m_ffn_norm_g, v_ffn_norm_g), (1, D)),
        ("final_norm_g", (0, R_GFIN, 1, 0, D), (final_norm_g, m_final_norm_g, v_final_norm_g), (1, D)),
    ]
    res = _adamw_small(gathered, [s[1] for s in small_params],
                       [tuple(t.reshape(s[3]) for t in s[2]) for s in small_params])
    for i, s in enumerate(small_params):
        outs[s[0]] = [r.reshape(s[2][0].shape) for r in res[4 * i:4 * i + 4]]
    for n, ref in (("w_in", w_in), ("w_gate_up", w_gate_up), ("w_out", w_out), ("w_down", w_down)):
        outs[n] = [r.reshape(ref.shape) for r in outs[n]]

    loss_all = res[4 * len(small_params)][0, 0]
    order = ["meta_tokens", "mix_norm_g", "w_in", "conv_w", "conv_b", "w_rgate", "b_rgate", "w_igate", "b_igate",
             "lru_lambda", "rg_norm_g", "hg_lower_bound", "hg_norm_g", "w_out", "ffn_norm_g", "w_gate_up", "w_down",
             "final_norm_g"]
    return (loss_all, grad_x, *[outs[n][0] for n in order], *[outs[n][1] for n in order],
            *[outs[n][2] for n in order], *[outs[n][3] for n in order])
```

```python
import functools

import jax
import jax.numpy as jnp
from jax import lax
from jax.experimental import pallas as pl
from jax.experimental.pallas import tpu as pltpu

_BF = jnp.bfloat16
_F32 = jnp.float32
_S = jax.ShapeDtypeStruct
_MESH = pl.DeviceIdType.MESH

N_DEV = 8
N_META = 16
D = 1024
D_RG = 512
D_HG = 512
HD = 128
NH = D_HG // HD
D_IN = 3072
D_FF = 2816
FFB = D_FF // 4
WIN_B = D_IN // N_DEV
WIN_P = 2 * WIN_B
EPS = 1e-6
LRU_C = 8.0
TM = 256
HC = 64
VMEM_LIMIT = 56 * 1024 * 1024

ADAM_LR = 0.001
ADAM_B1 = 0.9
ADAM_B2 = 0.999
ADAM_EPS = 1e-08
ADAM_WD = 0.01
ADAM_STEP = 10

_SEND_ORDER = (6, 4, 2, 7, 5, 3, 1, 0)

R_CONVB, R_BR, R_BI, R_LAM, R_GRG, R_HB0, R_HB1, R_GHG, R_CONVW = 0, 1, 2, 3, 4, 5, 6, 7, 8
R_GMIX, R_GFFN, R_GFIN, R_LOSS, R_META = 0, 1, 2, 3, 8


def _cp(sem=None, **kw):
    return pltpu.CompilerParams(dimension_semantics=sem, vmem_limit_bytes=VMEM_LIMIT, **kw)


def _dot(a, b):
    return jnp.dot(a, b, preferred_element_type=_F32)


def _dot_nt(a, b):
    return lax.dot_general(a, b, (((1,), (1,)), ((), ())), preferred_element_type=_F32)


def _dot_tn(a, b):
    return lax.dot_general(a, b, (((0,), (0,)), ((), ())), preferred_element_type=_F32)


def _sigmoid(x):
    return 0.5 * jnp.tanh(0.5 * x) + 0.5


def _dsilu(x, s):
    return s * (1.0 + x * (1.0 - s))


_GELU_C = 0.7978845608028654


def _gelu_parts(x):
    t = jnp.tanh(_GELU_C * (x + 0.044715 * (x * x * x)))
    g = 0.5 * x * (1.0 + t)
    dg = 0.5 * (1.0 + t) + 0.5 * x * (1.0 - t * t) * (_GELU_C * (1.0 + 3.0 * 0.044715 * (x * x)))
    return g, dg


def _softplus(z):
    e = jnp.exp(-jnp.abs(z))
    w = 1.0 + e
    l1p = jnp.where(w == 1.0, e, jnp.log(w) * e / jnp.where(w == 1.0, 1.0, w - 1.0))
    return jnp.maximum(z, 0.0) + l1p


def _rms_fwd(x):
    r = lax.rsqrt(jnp.mean(x * x, axis=-1, keepdims=True) + EPS)
    return x * r, r


def _rms_bwd(dyg, n, r):
    return r * (dyg - n * jnp.mean(dyg * n, axis=-1, keepdims=True))


def _full(shape):
    nd = len(shape)
    return pl.BlockSpec(shape, lambda i: (0,) * nd)


def _const(shape):
    nd = len(shape)
    return pl.BlockSpec(shape, lambda i: (0,) * nd, pipeline_mode=pl.Buffered(1))


def _carry_gather(gather, i, nt):
    @pl.when(i == 0)
    def _():
        gather.start()

    def tail():
        for j in range(3):
            @pl.when(i == max(nt - 4 + j, 0))
            def _(j=j):
                gather.forward(j)

        @pl.when(i == nt - 1)
        def _():
            gather.finish()

    return tail


def _pair_place(ref, block):
    return ref.at[block // 2, :, pl.ds(pl.multiple_of((block % 2) * WIN_B, WIN_B), WIN_B)]


def _rg_gates(xc, wr_ref, wi_ref, vec_ref):
    xcb = xc.astype(_BF)
    r = _sigmoid(_dot(xcb, wr_ref[...]) + vec_ref[R_BR:R_BR + 1, :])
    ig = _sigmoid(_dot(xcb, wi_ref[...]) + vec_ref[R_BI:R_BI + 1, :])
    nsp8 = -LRU_C * _softplus(-vec_ref[R_LAM:R_LAM + 1, :])
    la = nsp8 * r
    a = jnp.exp(la)
    th = jnp.tanh(la)
    s = jnp.sqrt(-2.0 * th / (1.0 - th))
    return r, ig, a, s, nsp8


def _conv(xbuf, vec_ref):
    acc = vec_ref[R_CONVW:R_CONVW + 1, :] * xbuf[pl.ds(5, TM), :]
    for j in range(1, 4):
        acc = acc + vec_ref[R_CONVW + j:R_CONVW + j + 1, :] * xbuf[pl.ds(5 + j, TM), :]
    return vec_ref[R_CONVB:R_CONVB + 1, :] + acc


def _dot3(m01, x):
    hi = x.astype(_BF)
    r1 = x - hi.astype(_F32)
    mid = r1.astype(_BF)
    lo = (r1 - mid.astype(_F32)).astype(_BF)
    return (_dot(m01, lo) + _dot(m01, mid)) + _dot(m01, hi)


def _chunk_masks():
    row = lax.broadcasted_iota(jnp.int32, (TM, TM), 0)
    col = lax.broadcasted_iota(jnp.int32, (TM, TM), 1)
    shift = HC.bit_length() - 1
    same = lax.shift_right_logical(row, shift) == lax.shift_right_logical(col, shift)
    return same, same & (row >= col), same & (col >= row)


def _per_chunk_rows(x, r):
    return jnp.concatenate([jnp.broadcast_to(x[HC * c + r:HC * c + r + 1, :], (HC, x.shape[1]))
                            for c in range(TM // HC)], axis=0)


def _hg_prep(p_ref, lb, tri_blk):
    hq = p_ref[:, pl.ds(2 * D_RG, D_HG)]
    hf = p_ref[:, pl.ds(2 * D_RG + D_HG, D_HG)]
    sq = _sigmoid(hq)
    q = hq * sq
    sg = _sigmoid(hf)
    f = lb + (1.0 - lb) * sg
    k = 1.0 - f
    b = _dot3(tri_blk, jnp.log(f))
    bm = _per_chunk_rows(b, HC // 2 - 1)
    bl = _per_chunk_rows(b, HC - 1)
    e_q = jnp.exp(b - bm)
    e_k = jnp.exp(bm - b)
    e_b = jnp.exp(b)
    e_l = jnp.exp(bl - b)
    return dict(hq=hq, sq=sq, q=q, sg=sg, f=f, k=k, e_q=e_q, e_k=e_k, e_b=e_b, e_l=e_l,
                qd=q * e_q, kd=k * e_k, qe=q * e_b, ke=k * e_l, e_end=jnp.exp(bl))


def _mixer_fwd(h0, g_mix, w_in, wr, wi, vec, hb, g_hg, shards):
    t_pad = h0.shape[0]
    nt = t_pad // TM
    nc_t = TM // HC
    nsh = len(shards)

    def body(h_ref, gmix_ref, win_ref, wr_ref, wi_ref, vec_ref, hb_ref, ghg_ref, *rest):
        sh_refs, rest = rest[:nsh], rest[nsh:]
        pout_ref, uout_ref, y_ref, hs_ref, o_ref, sc_ref = rest[:6]
        gath_refs, rest = rest[6:6 + nsh], rest[6 + nsh:]
        xbuf, a_s, b_s, hcar, st, qd_s, kd_s, qe_s, ke_s, v_s, u_s, p_s, p_ref = rest[:13]
        i = pl.program_id(0)
        tail = _carry_gather(_Gather(sh_refs, gath_refs, rest[13:]), i, nt + 1)

        @pl.when(i == 0)
        def _():
            p_s[...] = jnp.zeros_like(p_s)

        p_ref[...] = p_s[...]

        @pl.when(i <= 1)
        def _():
            xbuf[pl.ds(0, 8), :] = jnp.zeros((8, D_RG), _F32)
            hcar[...] = jnp.zeros_like(hcar)
            st[...] = jnp.zeros_like(st)

        n_h, _ = _rms_fwd(h_ref[...])
        u = (n_h * gmix_ref[...]).astype(_BF)
        uout_ref[...] = u
        pieces = [(j, k) for j in range(4) for k in range(WIN_P // 256)]

        def project(count):
            for _ in range(count):
                j, k = pieces.pop(0)
                blk = _dot(u, win_ref[j, :, pl.ds(256 * k, 256)])
                p_s[:, pl.ds(WIN_P * j + 256 * k, 256)] = blk
                pout_ref[:, pl.ds(WIN_P * j + 256 * k, 256)] = blk

        x = p_ref[:, pl.ds(0, D_RG)]
        xbuf[pl.ds(8, TM), :] = x
        xc = _conv(xbuf, vec_ref)
        xbuf[pl.ds(0, 8), :] = x[TM - 8:, :]
        r, ig, a, s, _ = _rg_gates(xc, wr_ref, wi_ref, vec_ref)
        a_s[...] = a
        b_s[...] = s * (ig * xc)

        def step(t, h):
            h = a_s[pl.ds(t, 1), :] * h + b_s[pl.ds(t, 1), :]
            hs_ref[pl.ds(t, 1), :] = h
            return h

        hcar[pl.ds(0, 1), :] = lax.fori_loop(0, TM, step, hcar[pl.ds(0, 1), :], unroll=8)
        gel, _ = _gelu_parts(p_ref[:, pl.ds(D_RG, D_RG)])
        n, _ = _rms_fwd(gel * hs_ref[...])
        y_ref[:, pl.ds(0, D_RG)] = (n * vec_ref[R_GRG:R_GRG + 1, :]).astype(_BF)

        lb = _sigmoid(hb_ref[0:1, :] - hb_ref[1:2, :])
        _, tri_blk, _ = _chunk_masks()
        q = _hg_prep(p_ref, lb, tri_blk.astype(_BF))
        for name, ref in (("qd", qd_s), ("kd", kd_s), ("qe", qe_s), ("ke", ke_s)):
            ref[...] = q[name].astype(_BF)
        v_s[...] = p_ref[:, pl.ds(2 * D_RG + 2 * D_HG, D_HG)].astype(_BF)
        e_end = q["e_end"]
        causal = (lax.broadcasted_iota(jnp.int32, (HC, HC), 0) >= lax.broadcasted_iota(jnp.int32, (HC, HC), 1))
        for c in range(nc_t):
            for h in range(NH):
                rs, cs = pl.ds(HC * c, HC), pl.ds(HD * h, HD)
                amat = jnp.where(causal, _dot_nt(qd_s[rs, cs], kd_s[rs, cs]), 0.0)
                o_ref[rs, cs] = _dot(amat.astype(_BF), v_s[rs, cs])
                u_s[NH * c + h] = _dot_tn(v_s[rs, cs], ke_s[rs, cs])
                if pieces:
                    project(1)
        assert not pieces
        for h in range(NH):
            cs = pl.ds(HD * h, HD)
            s_run = st[h]
            for c in range(nc_t):
                rs = pl.ds(HC * c, HC)
                sc_ref[c, h] = s_run
                o_ref[rs, cs] += _dot_nt(qe_s[rs, cs], s_run.astype(_BF))
                s_run = e_end[HC * c:HC * c + 1, HD * h:HD * (h + 1)] * s_run + u_s[NH * c + h]
            st[h] = s_run
        for h in range(NH):
            cs = pl.ds(HD * h, HD)
            n_o, _ = _rms_fwd(o_ref[:, cs])
            hg = p_ref[:, pl.ds(2 * D_RG + 3 * D_HG + HD * h, HD)]
            y_ref[:, pl.ds(D_RG + HD * h, HD)] = ((n_o * ghg_ref[...]) * (hg * _sigmoid(hg))).astype(_BF)

        tail()

    hbm = pl.BlockSpec(memory_space=pl.ANY)

    def proj(i):
        return jnp.minimum(i, nt - 1)

    def mixed(i):
        return jnp.maximum(i - 1, 0)

    return pl.pallas_call(
        body, name="mixer_fwd", grid=(nt + 1,),
        in_specs=[pl.BlockSpec((TM, D), lambda i: (proj(i), 0)), _full((1, D)), _const((4, D, WIN_P)),
                  _full((D_RG, D_RG)), _full((D_RG, D_RG)),
                  _full((16, D_RG)), _full((2, D_HG)), _full((1, HD))] + [hbm] * nsh,
        out_specs=[pl.BlockSpec((TM, D_IN), lambda i: (proj(i), 0)), pl.BlockSpec((TM, D), lambda i: (proj(i), 0)),
                   pl.BlockSpec((TM, D), lambda i: (mixed(i), 0)), pl.BlockSpec((TM, D_RG), lambda i: (mixed(i), 0)),
                   pl.BlockSpec((TM, D_HG), lambda i: (mixed(i), 0)),
                   pl.BlockSpec((nc_t, NH, HD, HD), lambda i: (mixed(i), 0, 0, 0))] + [hbm] * nsh,
        out_shape=[_S((t_pad, D_IN), _F32), _S((t_pad, D), _BF),
                   _S((t_pad, D), _BF), _S((t_pad, D_RG), _F32), _S((t_pad, D_HG), _F32),
                   _S((t_pad // HC, NH, HD, HD), _F32)] + [_S((N_DEV,) + s.shape, s.dtype) for s in shards],
        scratch_shapes=[pltpu.VMEM((TM + 8, D_RG), _F32), pltpu.VMEM((TM, D_RG), _F32),
                        pltpu.VMEM((TM, D_RG), _F32), pltpu.VMEM((8, D_RG), _F32),
                        pltpu.VMEM((NH, HD, HD), _F32)] + [pltpu.VMEM((TM, D_HG), _BF) for _ in range(5)]
        + [pltpu.VMEM((nc_t * NH, HD, HD), _F32), pltpu.VMEM((TM, D_IN), _F32), pltpu.VMEM((TM, D_IN), _F32)]
        + _sem_shapes(nsh),
        compiler_params=_cp(("arbitrary",)),
    )(h0, g_mix, w_in, wr, wi, vec, hb, g_hg, *shards)


def _ffn_loss(h0, y, w_out, g_ffn, w_gu, w_down, g_fin, tgt, n_valid):
    t_pad = h0.shape[0]

    def body(h_ref, y_ref, wo_ref, gffn_ref, wgu_ref, wd_ref, g_ref, t_ref,
             h1_ref, v_ref, gu_ref, act_ref, dh2_ref, dh2b_ref, loss_ref, gfin_ref):
        i = pl.program_id(0)

        @pl.when(i == 0)
        def _():
            loss_ref[...] = jnp.zeros_like(loss_ref)
            gfin_ref[...] = jnp.zeros_like(gfin_ref)

        h1 = h_ref[...] + _dot(y_ref[...], wo_ref[...])
        h1_ref[...] = h1
        n1, _ = _rms_fwd(h1)
        vb = (n1 * gffn_ref[...]).astype(_BF)
        v_ref[...] = vb
        h2 = h1
        for b in range(4):
            gate = _dot_nt(vb, wgu_ref[b])
            up = _dot_nt(vb, wgu_ref[4 + b])
            gu_ref[b] = gate
            gu_ref[4 + b] = up
            act = ((gate * _sigmoid(gate)) * up).astype(_BF)
            act_ref[b] = act
            h2 = h2 + _dot(act, wd_ref[b])
        n, r = _rms_fwd(h2)
        out = n * g_ref[...]
        row = i * TM + lax.broadcasted_iota(jnp.int32, (TM, 1), 0)
        valid = (row >= N_META) & (row < n_valid)
        err = jnp.where(valid, out - t_ref[...], 0.0)
        loss_ref[...] += (0.5 / D) * jnp.sum(err * err)
        dout = err * (1.0 / D)
        gfin_ref[...] += jnp.sum(dout * n, axis=0, keepdims=True)
        dh2 = _rms_bwd(dout * g_ref[...], n, r)
        dh2_ref[...] = dh2
        dh2b_ref[...] = dh2.astype(_BF)

    tile = pl.BlockSpec((TM, D), lambda i: (i, 0))
    return pl.pallas_call(
        body, name="ffn_loss", grid=(t_pad // TM,),
        in_specs=[tile, tile, _const((D, D)), _full((1, D)),
                  _const((N_DEV, FFB, D)), _const((4, FFB, D)), _full((1, D)), tile],
        out_specs=[tile, tile,
                   pl.BlockSpec((N_DEV, TM, FFB), lambda i: (0, i, 0)), pl.BlockSpec((4, TM, FFB), lambda i: (0, i, 0)),
                   tile, tile, _full((8, 128)), _full((1, D))],
        out_shape=[_S((t_pad, D), _F32), _S((t_pad, D), _BF),
                   _S((N_DEV, t_pad, FFB), _F32), _S((4, t_pad, FFB), _BF), _S((t_pad, D), _F32),
                   _S((t_pad, D), _BF), _S((8, 128), _F32), _S((1, D), _F32)],
        compiler_params=_cp(("arbitrary",)),
    )(h0, y, w_out, g_ffn, w_gu, w_down, g_fin, tgt)


def _ffn_bwd(dh2, dh2b, gu, h1, g_ffn, w_gu, w_down, w_out):
    t_pad = dh2.shape[0]

    def body(dh2_ref, dh2b_ref, gu_ref, h1_ref, g_ref, wgu_ref, wd_ref, wo_ref,
             dgu_ref, dh1_ref, dh1b_ref, dy_ref, gffn_ref):
        i = pl.program_id(0)

        @pl.when(i == 0)
        def _():
            gffn_ref[...] = jnp.zeros_like(gffn_ref)

        db = dh2b_ref[...]
        dv = jnp.zeros((TM, D), _F32)
        for b in range(4):
            dact = _dot_nt(db, wd_ref[b])
            gate = gu_ref[b]
            up = gu_ref[4 + b]
            sg = _sigmoid(gate)
            dgate = ((dact * up) * _dsilu(gate, sg)).astype(_BF)
            dup = (dact * (gate * sg)).astype(_BF)
            dgu_ref[b] = dgate
            dgu_ref[4 + b] = dup
            dv = dv + _dot(dgate, wgu_ref[b]) + _dot(dup, wgu_ref[4 + b])
        n, r = _rms_fwd(h1_ref[...])
        gffn_ref[...] += jnp.sum(dv * n, axis=0, keepdims=True)
        dh1 = dh2_ref[...] + _rms_bwd(dv * g_ref[...], n, r)
        dh1_ref[...] = dh1
        dh1b = dh1.astype(_BF)
        dh1b_ref[...] = dh1b
        dy_ref[...] = _dot_nt(dh1b, wo_ref[...])

    tile = pl.BlockSpec((TM, D), lambda i: (i, 0))
    return pl.pallas_call(
        body, name="ffn_bwd", grid=(t_pad // TM,),
        in_specs=[tile, tile, pl.BlockSpec((N_DEV, TM, FFB), lambda i: (0, i, 0)), tile, _full((1, D)),
                  _const((N_DEV, FFB, D)), _const((4, FFB, D)), _const((D, D))],
        out_specs=[pl.BlockSpec((N_DEV, TM, FFB), lambda i: (0, i, 0)), tile, tile, tile, _full((1, D))],
        out_shape=[_S((N_DEV, t_pad, FFB), _BF), _S((t_pad, D), _F32), _S((t_pad, D), _BF),
                   _S((t_pad, D), _F32), _S((1, D), _F32)],
        compiler_params=_cp(("arbitrary",)),
    )(dh2, dh2b, gu, h1, g_ffn, w_gu, w_down, w_out)


def _mixer_bwd(p, hs, o, sc, dy, wr, wi, vec, hb, g_hg, scatter):
    t_pad = p.shape[0]
    nt = t_pad // TM
    nc_t = TM // HC
    nsc = len(scatter)

    def rev(i):
        return nt - 1 - i

    def body(p_ref, pprev_ref, hs_ref, hprev_ref, o_ref, sc_ref, dy_ref, wr_ref, wi_ref, vec_ref, hb_ref, ghg_ref,
             *rest):
        send_refs, rest = rest[:nsc], rest[nsc:]
        dp_ref, gvec_ref, gw_ref = rest[:3]
        recv_refs, rest = rest[3:3 + nsc], rest[3 + nsc:]
        xbuf, hbuf, dbuf, a_s, g_s, ccar, dst = rest[:7]
        qd_s, kd_s, qe_s, ke_s, v_s, do_s, dqd_s, dkd_s, dqe_s, dke_s, dv_s, w_s, dend_s = rest[7:20]
        exchange = _Exchange(send_refs, [], recv_refs, rest[20:])
        i = pl.program_id(0)
        first_tile = i == nt - 1

        @pl.when(i == 0)
        def _():
            exchange.start()
            gvec_ref[...] = jnp.zeros_like(gvec_ref)
            gw_ref[...] = jnp.zeros_like(gw_ref)
            dbuf[pl.ds(TM, 8), :] = jnp.zeros((8, D_RG), _F32)
            ccar[...] = jnp.zeros_like(ccar)
            dst[...] = jnp.zeros_like(dst)

        def acc(row, val):
            gvec_ref[row:row + 1, :] += jnp.sum(val, axis=0, keepdims=True)

        keep = jnp.where(first_tile, 0.0, 1.0)
        x = p_ref[:, pl.ds(0, D_RG)]
        xbuf[pl.ds(0, 8), :] = pprev_ref[...] * keep
        xbuf[pl.ds(8, TM), :] = x
        xc = _conv(xbuf, vec_ref)
        r, ig, a, s, nsp8 = _rg_gates(xc, wr_ref, wi_ref, vec_ref)
        h = hs_ref[...]
        hbuf[pl.ds(0, 8), :] = hprev_ref[...] * keep
        hbuf[pl.ds(8, TM), :] = h
        hm1 = hbuf[pl.ds(7, TM), :]
        gr = p_ref[:, pl.ds(D_RG, D_RG)]
        gel, dgel = _gelu_parts(gr)
        n, rr = _rms_fwd(gel * h)
        dyn = dy_ref[:, pl.ds(0, D_RG)]
        acc(R_GRG, dyn * n)
        dpre = _rms_bwd(dyn * vec_ref[R_GRG:R_GRG + 1, :], n, rr)
        dp_ref[:, pl.ds(D_RG, D_RG)] = ((dpre * h) * dgel).astype(_BF)
        a_s[...] = a
        g_s[...] = dpre * gel

        def step(k, c):
            t = TM - 1 - k
            g = g_s[pl.ds(t, 1), :] + c
            g_s[pl.ds(t, 1), :] = g
            return a_s[pl.ds(t, 1), :] * g

        ccar[pl.ds(0, 1), :] = lax.fori_loop(0, TM, step, ccar[pl.ds(0, 1), :], unroll=8)
        gt = g_s[...]
        da = gt * hm1
        ixc = ig * xc
        ds = gt * ixc
        dig = (gt * s) * xc
        dxc = (gt * s) * ig
        dla = da * a - ds * ((a * a) / s)
        lam = vec_ref[R_LAM:R_LAM + 1, :]
        gvec_ref[R_LAM:R_LAM + 1, :] += jnp.sum(dla * r, axis=0, keepdims=True) * (LRU_C * _sigmoid(-lam))
        dzr = (dla * nsp8) * (r * (1.0 - r))
        dzi = dig * (ig * (1.0 - ig))
        acc(R_BR, dzr)
        acc(R_BI, dzi)
        xcb = xc.astype(_BF)
        dzrb = dzr.astype(_BF)
        dzib = dzi.astype(_BF)
        gw_ref[0] += _dot_tn(xcb, dzrb)
        gw_ref[1] += _dot_tn(xcb, dzib)
        dxc = dxc + _dot_nt(dzrb, wr_ref[...]) + _dot_nt(dzib, wi_ref[...])
        acc(R_CONVB, dxc)
        for j in range(4):
            acc(R_CONVW + j, dxc * xbuf[pl.ds(5 + j, TM), :])
        dbuf[pl.ds(0, TM), :] = dxc
        dx = vec_ref[R_CONVW + 3:R_CONVW + 4, :] * dxc
        for j in range(3):
            dx = dx + vec_ref[R_CONVW + j:R_CONVW + j + 1, :] * dbuf[pl.ds(3 - j, TM), :]
        dbuf[pl.ds(TM, 8), :] = dxc[0:8, :]
        dp_ref[:, pl.ds(0, D_RG)] = dx.astype(_BF)

        lb = _sigmoid(hb_ref[0:1, :] - hb_ref[1:2, :])
        same, tri_blk, triu_blk = _chunk_masks()
        q = _hg_prep(p_ref, lb, tri_blk.astype(_BF))
        qdb, kdb = q["qd"].astype(_BF), q["kd"].astype(_BF)
        qd_s[...] = qdb
        kd_s[...] = kdb
        qe_s[...] = q["qe"].astype(_BF)
        ke_s[...] = q["ke"].astype(_BF)
        v_s[...] = p_ref[:, pl.ds(2 * D_RG + 2 * D_HG, D_HG)].astype(_BF)
        e_end = q["e_end"]
        ghg = ghg_ref[...]
        for h in range(NH):
            cs = pl.ds(HD * h, HD)
            hg = p_ref[:, pl.ds(2 * D_RG + 3 * D_HG + HD * h, HD)]
            sh = _sigmoid(hg)
            n_o, r_o = _rms_fwd(o_ref[:, cs])
            dyh = dy_ref[:, pl.ds(D_RG + HD * h, HD)]
            dp_ref[:, pl.ds(2 * D_RG + 3 * D_HG + HD * h, HD)] = ((dyh * (n_o * ghg)) * _dsilu(hg, sh)).astype(_BF)
            dn = dyh * (hg * sh)
            gvec_ref[R_GHG:R_GHG + 1, pl.ds(0, HD)] += jnp.sum(dn * n_o, axis=0, keepdims=True)
            do_s[:, cs] = _rms_bwd(dn * ghg, n_o, r_o).astype(_BF)
        causal = (lax.broadcasted_iota(jnp.int32, (HC, HC), 0) >= lax.broadcasted_iota(jnp.int32, (HC, HC), 1))
        for c in range(nc_t):
            for h in range(NH):
                rs, cs = pl.ds(HC * c, HC), pl.ds(HD * h, HD)
                qd_c, kd_c, do_c = qd_s[rs, cs], kd_s[rs, cs], do_s[rs, cs]
                amat = jnp.where(causal, _dot_nt(qd_c, kd_c), 0.0).astype(_BF)
                da_m = jnp.where(causal, _dot_nt(do_c, v_s[rs, cs]), 0.0).astype(_BF)
                dqd_s[rs, cs] = _dot(da_m, kd_c)
                dkd_s[rs, cs] = _dot_tn(da_m, qd_c)
                dqe_s[rs, cs] = _dot(do_c, sc_ref[c, h].astype(_BF))
                dv_s[rs, cs] = _dot_tn(amat, do_c)
                w_s[NH * c + h] = _dot_tn(do_c, qe_s[rs, cs])
        for h in range(NH):
            cs = pl.ds(HD * h, HD)
            d_run = dst[h]
            for c in reversed(range(nc_t)):
                rs = pl.ds(HC * c, HC)
                d_b = d_run.astype(_BF)
                dke_s[rs, cs] = _dot(v_s[rs, cs], d_b)
                dp_ref[rs, pl.ds(2 * D_RG + 2 * D_HG + HD * h, HD)] = (
                    dv_s[rs, cs] + _dot_nt(ke_s[rs, cs], d_b)).astype(_BF)
                dend_s[pl.ds(c, 1), cs] = jnp.sum(sc_ref[c, h] * d_run, axis=0, keepdims=True)
                d_run = w_s[NH * c + h] + e_end[HC * c:HC * c + 1, HD * h:HD * (h + 1)] * d_run
            dst[h] = d_run
        dqd, dkd, dqe, dke = dqd_s[...], dkd_s[...], dqe_s[...], dke_s[...]
        dq = dqd * q["e_q"] + dqe * q["e_b"]
        dk = dkd * q["e_k"] + dke * q["e_l"]
        dkeke = dke * q["ke"]
        db = dqd * qdb.astype(_F32) - dkd * kdb.astype(_F32) + dqe * q["qe"] - dkeke
        d_end = jnp.concatenate([jnp.broadcast_to(dend_s[pl.ds(c, 1), :], (HC, D_HG)) for c in range(nc_t)], axis=0)
        dlf = _dot3(triu_blk.astype(_BF), db) + _dot3(same.astype(_BF), dkeke) + d_end * e_end
        df = dlf / q["f"] - dk
        sg = q["sg"]
        gvec_ref[R_HB0:R_HB0 + 1, :] += jnp.sum(df * (1.0 - sg), axis=0, keepdims=True)
        dp_ref[:, pl.ds(2 * D_RG, D_HG)] = (dq * _dsilu(q["hq"], q["sq"])).astype(_BF)
        dp_ref[:, pl.ds(2 * D_RG + D_HG, D_HG)] = ((df * (1.0 - lb)) * (sg * (1.0 - sg))).astype(_BF)

        @pl.when(i == nt - 1)
        def _():
            glb = gvec_ref[R_HB0:R_HB0 + 1, :] * (lb * (1.0 - lb))
            gvec_ref[R_HB0:R_HB0 + 1, :] = glb
            gvec_ref[R_HB1:R_HB1 + 1, :] = -glb
            exchange.finish()

    hbm = pl.BlockSpec(memory_space=pl.ANY)
    return pl.pallas_call(
        body, name="mixer_bwd", grid=(nt,),
        in_specs=[pl.BlockSpec((TM, D_IN), lambda i: (rev(i), 0)),
                  pl.BlockSpec((8, D_RG), lambda i: (jnp.maximum(rev(i) * (TM // 8) - 1, 0), 0)),
                  pl.BlockSpec((TM, D_RG), lambda i: (rev(i), 0)),
                  pl.BlockSpec((8, D_RG), lambda i: (jnp.maximum(rev(i) * (TM // 8) - 1, 0), 0)),
                  pl.BlockSpec((TM, D_HG), lambda i: (rev(i), 0)),
                  pl.BlockSpec((nc_t, NH, HD, HD), lambda i: (rev(i), 0, 0, 0)),
                  pl.BlockSpec((TM, D), lambda i: (rev(i), 0)),
                  _full((D_RG, D_RG)), _full((D_RG, D_RG)), _full((16, D_RG)), _full((2, D_HG)), _full((1, HD))]
        + [hbm] * nsc,
        out_specs=[pl.BlockSpec((TM, D_IN), lambda i: (rev(i), 0)), _full((16, D_RG)), _full((2, D_RG, D_RG))]
        + [hbm] * nsc,
        out_shape=[_S((t_pad, D_IN), _BF), _S((16, D_RG), _F32), _S((2, D_RG, D_RG), _F32)]
        + [_S(s.shape, s.dtype) for s in scatter],
        scratch_shapes=[pltpu.VMEM((TM + 8, D_RG), _F32), pltpu.VMEM((TM + 8, D_RG), _F32),
                        pltpu.VMEM((TM + 8, D_RG), _F32), pltpu.VMEM((TM, D_RG), _F32),
                        pltpu.VMEM((TM, D_RG), _F32), pltpu.VMEM((8, D_RG), _F32),
                        pltpu.VMEM((NH, HD, HD), _F32)]
        + [pltpu.VMEM((TM, D_HG), _BF) for _ in range(6)] + [pltpu.VMEM((TM, D_HG), _F32) for _ in range(5)]
        + [pltpu.VMEM((nc_t * NH, HD, HD), _F32), pltpu.VMEM((8, D_HG), _F32)] + _sem_shapes(nsc),
        compiler_params=_cp(("arbitrary",)),
    )(p, p, hs, hs, o, sc, dy, wr, wi, vec, hb, g_hg, *scatter)


def _inproj_bwd_send(dp, w_in, h0, dh1, g_mix, u, order, gffn, gfin, loss, to_all):
    t_pad = dp.shape[0]
    rb = t_pad // (2 * N_DEV)
    n_steps = N_DEV + 2 * N_DEV
    na = len(to_all)

    def body(order_ref, dpc_ref, dpr_ref, u_ref, w_ref, h_ref, dh1_ref, g_ref, gffn_ref, gfin_ref, loss_ref, *rest):
        all_in = rest[:na]
        dh0_ref, recv_ref = rest[na:na + 2]
        all_out = rest[na + 2:2 * na + 2]
        alla_ref = rest[2 * na + 2]
        buf, pack, blk_send, blk_recv, blk_local = rest[2 * na + 3:2 * na + 8]
        exchange = _Exchange([], all_in, all_out, rest[2 * na + 8:2 * na + 11])
        last = _Exchange([], [pack], [alla_ref], rest[2 * na + 11:])
        s = pl.program_id(0)
        x, y, c = _coords()
        me = 4 * x + 2 * y + c

        def send(step):
            r = _SEND_ORDER[step]
            return pltpu.make_async_remote_copy(
                src_ref=buf.at[step], dst_ref=recv_ref.at[me], send_sem=blk_send.at[step], recv_sem=blk_recv.at[r - 1],
                device_id=(x ^ (r >> 2), y ^ ((r >> 1) & 1), c ^ (r & 1)), device_id_type=_MESH)

        @pl.when(s == 0)
        def _():
            exchange.start()
            pack[...] = jnp.zeros_like(pack)

        @pl.when(s < N_DEV)
        def _():
            buf[s] = _dot_tn(u_ref[...], dpc_ref[...]).astype(_BF)

            for step in range(N_DEV - 1):
                @pl.when(s == step)
                def _(step=step):
                    send(step).start()

        @pl.when(s >= N_DEV)
        def _():
            du = jnp.zeros((rb, D), _F32)
            for j in range(4):
                du = du + _dot_nt(dpr_ref[:, WIN_P * j:WIN_P * (j + 1)], w_ref[j])
            n, r = _rms_fwd(h_ref[...])
            pack[R_GMIX:R_GMIX + 1, :] += jnp.sum(du * n, axis=0, keepdims=True)
            dh0 = dh1_ref[...] + _rms_bwd(du * g_ref[...], n, r)
            dh0_ref[...] = dh0

            @pl.when(s == N_DEV)
            def _():
                pack[R_META:R_META + N_META, :] = dh0[0:N_META, :]

        @pl.when(s == n_steps - 1)
        def _():
            pack[R_GFFN:R_GFFN + 1, :] = gffn_ref[...]
            pack[R_GFIN:R_GFIN + 1, :] = gfin_ref[...]
            pack[R_LOSS:R_LOSS + 1, pl.ds(0, 128)] = loss_ref[0:1, :]
            last.start()
            mine = pltpu.make_async_copy(buf.at[N_DEV - 1], recv_ref.at[me], blk_local.at[0])
            mine.start()
            for step in range(N_DEV - 1):
                send(step).wait_send()
            for r in range(1, N_DEV):
                px, py, pc = x ^ (r >> 2), y ^ ((r >> 1) & 1), c ^ (r & 1)
                pltpu.make_async_remote_copy(
                    src_ref=buf.at[0], dst_ref=recv_ref.at[4 * px + 2 * py + pc], send_sem=blk_send.at[0],
                    recv_sem=blk_recv.at[r - 1], device_id=(px, py, pc), device_id_type=_MESH).wait_recv()
            mine.wait()
            exchange.finish()
            last.finish()

    hbm = pl.BlockSpec(memory_space=pl.ANY)
    rows = pl.BlockSpec((rb, D), lambda s, order: (jnp.maximum(s - N_DEV, 0), 0))
    one = pl.BlockSpec((1, D), lambda s, order: (0, 0))
    res = pl.pallas_call(
        body, name="inproj_bwd_send",
        grid_spec=pltpu.PrefetchScalarGridSpec(
            num_scalar_prefetch=1, grid=(n_steps,),
            in_specs=[pl.BlockSpec((t_pad, WIN_B), lambda s, order: (0, order[jnp.minimum(s, N_DEV - 1)])),
                      pl.BlockSpec((rb, D_IN), lambda s, order: (jnp.maximum(s - N_DEV, 0), 0)),
                      pl.BlockSpec((t_pad, D), lambda s, order: (0, 0), pipeline_mode=pl.Buffered(1)),
                      pl.BlockSpec((4, D, WIN_P), lambda s, order: (0, 0, 0), pipeline_mode=pl.Buffered(1)),
                      rows, rows, one, one, one, pl.BlockSpec((8, 128), lambda s, order: (0, 0))] + [hbm] * na,
            out_specs=[rows] + [hbm] * (na + 2),
            scratch_shapes=[pltpu.VMEM((N_DEV, D, WIN_B), _BF), pltpu.VMEM((24, D), _F32),
                            pltpu.SemaphoreType.DMA((N_DEV - 1,)), pltpu.SemaphoreType.DMA((N_DEV - 1,)),
                            pltpu.SemaphoreType.DMA((1,))] + _sem_shapes(na) + _sem_shapes(1)),
        out_shape=[_S((t_pad, D), _F32), _S((N_DEV, D, WIN_B), _BF)]
        + [_S((N_DEV,) + g.shape, g.dtype) for g in to_all] + [_S((N_DEV, 24, D), _F32)],
        compiler_params=_cp(("arbitrary",)),
    )(order, dp, dp, u, w_in, h0, dh1, g_mix, gffn, gfin, loss, *to_all)
    return res


def _wgrad(name, a, b, a_spec, b_spec, n_blocks, out_block, scatter=()):
    nsc = len(scatter)

    def body(a_ref, b_ref, *rest):
        o_ref = rest[nsc]
        j = pl.program_id(0)
        if nsc:
            exchange = _Exchange(rest[:nsc], [], rest[nsc + 1:2 * nsc + 1], rest[2 * nsc + 1:])

            @pl.when(j == 0)
            def _():
                exchange.start()

        av = a_ref[0] if len(a_ref.shape) == 3 else a_ref[...]
        bv = b_ref[0] if len(b_ref.shape) == 3 else b_ref[...]
        o_ref[0] = _dot_tn(av, bv).astype(_BF)

        if nsc:
            @pl.when(j == n_blocks - 1)
            def _():
                exchange.finish()

    hbm = pl.BlockSpec(memory_space=pl.ANY)
    res = pl.pallas_call(
        body, name=name, grid=(n_blocks,),
        in_specs=[a_spec, b_spec] + [hbm] * nsc,
        out_specs=[pl.BlockSpec((1,) + out_block, lambda j: (j, 0, 0))] + [hbm] * nsc,
        out_shape=[_S((n_blocks,) + out_block, _BF)] + [_S(s.shape, s.dtype) for s in scatter],
        scratch_shapes=_sem_shapes(nsc) if nsc else [],
        compiler_params=_cp(("arbitrary",)),
    )(a, b, *scatter)
    return res if nsc else res[0]


def _coords():
    return lax.axis_index("x"), lax.axis_index("y"), lax.axis_index("c")


def _sem_shapes(na):
    return [pltpu.SemaphoreType.DMA((7 * na,)), pltpu.SemaphoreType.DMA((7 * na,)), pltpu.SemaphoreType.DMA((na,))]


class _Gather:
    def __init__(self, srcs, outs, sems, place=None):
        self.srcs, self.outs = srcs, outs
        self.send_sems, self.recv_sems, self.local_sems = sems
        self.place = place if place is not None else (lambda ref, block: ref.at[block])
        self.na = len(srcs)
        x, y, c = _coords()
        self.pos = (x, y, c)
        self.me = 4 * x + 2 * y + c
        self.sibling = (x, y, 1 - c)
        self.chips = [(1 - x, y), (x, 1 - y), (1 - x, 1 - y)]

    @staticmethod
    def _slot(px, py, pc):
        return 4 * px + 2 * py + pc

    def _copy(self, a, k, block, to, own=False):
        dst = self.place(self.outs[a], block)
        return pltpu.make_async_remote_copy(
            src_ref=self.srcs[a] if own else dst, dst_ref=dst,
            send_sem=self.send_sems.at[7 * a + k], recv_sem=self.recv_sems.at[7 * a + k],
            device_id=to, device_id_type=_MESH)

    def _mine(self, a):
        return pltpu.make_async_copy(self.srcs[a], self.place(self.outs[a], self.me), self.local_sems.at[a])

    def _first(self):
        c = self.pos[2]
        cps = []
        for a in range(self.na):
            cps.append(self._copy(a, 0, self.me, self.sibling, own=True))
            cps += [self._copy(a, 1 + j, self.me, (*chip, c), own=True) for j, chip in enumerate(self.chips)]
        return cps

    def _passed(self):
        c = self.pos[2]
        return [self._copy(a, 4 + j, self._slot(*chip, c), self.sibling)
                for j, chip in enumerate(self.chips) for a in range(self.na)]

    def start(self):
        for a in range(self.na):
            self._mine(a).start()
        for cp in self._first():
            cp.start()

    def forward(self, j):
        c = self.pos[2]
        chip = self.chips[j]
        for a in range(self.na):
            self._copy(a, 1 + j, self._slot(*chip, c), self.pos).wait_recv()
            self._copy(a, 4 + j, self._slot(*chip, c), self.sibling).start()

    def wait_sibling(self):
        x, y, c = self.pos
        for a in range(self.na):
            self._copy(a, 0, self._slot(x, y, 1 - c), self.pos).wait_recv()

    def wait_passed(self, j):
        c = self.pos[2]
        for a in range(self.na):
            self._copy(a, 4 + j, self._slot(*self.chips[j], 1 - c), self.pos).wait_recv()

    def finish_sends(self):
        for cp in self._first() + self._passed():
            cp.wait_send()
        for a in range(self.na):
            self._mine(a).wait()

    def finish(self):
        self.wait_sibling()
        for j in range(3):
            self.wait_passed(j)
        self.finish_sends()


class _Exchange:
    def __init__(self, scatter, gather, outs, sems):
        self.ins = list(scatter) + list(gather)
        self.ns, self.na = len(scatter), len(scatter) + len(gather)
        self.outs = outs
        self.send_sems, self.recv_sems, self.local_sems = sems
        x, y, c = _coords()
        self.pos = (x, y, c)
        self.me = 4 * x + 2 * y + c

    def _peer(self, r):
        x, y, c = self.pos
        return x ^ (r >> 2), y ^ ((r >> 1) & 1), c ^ (r & 1)

    def _src(self, a, block):
        return self.ins[a].at[block] if a < self.ns else self.ins[a]

    def _local(self, a):
        return pltpu.make_async_copy(self._src(a, self.me), self.outs[a].at[self.me], self.local_sems.at[a])

    def _send(self, a, r):
        px, py, pc = self._peer(r)
        return pltpu.make_async_remote_copy(
            src_ref=self._src(a, 4 * px + 2 * py + pc), dst_ref=self.outs[a].at[self.me],
            send_sem=self.send_sems.at[7 * a + r - 1], recv_sem=self.recv_sems.at[7 * a + r - 1],
            device_id=(px, py, pc), device_id_type=_MESH)

    def _recv(self, a, r):
        px, py, pc = self._peer(r)
        return pltpu.make_async_remote_copy(
            src_ref=self._src(a, self.me), dst_ref=self.outs[a].at[4 * px + 2 * py + pc],
            send_sem=self.send_sems.at[7 * a + r - 1], recv_sem=self.recv_sems.at[7 * a + r - 1],
            device_id=(px, py, pc), device_id_type=_MESH)

    def start(self):
        for a in range(self.na):
            self._local(a).start()
        for r in range(1, N_DEV):
            for a in range(self.na):
                self._send(a, r).start()

    def finish(self):
        for r in range(1, N_DEV):
            for a in range(self.na):
                self._recv(a, r).wait_recv()
        for r in range(1, N_DEV):
            for a in range(self.na):
                self._send(a, r).wait_send()
        for a in range(self.na):
            self._local(a).wait()


def _prologue(x, tgt, small_l, w_in_l, cast_f32):
    seq = x.shape[0]
    assert seq % TM == 0
    nx = seq // TM
    nt = nx + 1
    nc = len(cast_f32)
    body_rows = TM - N_META

    def body(xm_ref, xp_ref, tm_ref, tp_ref, s_ref, w_ref, *rest):
        cins = rest[:nc]
        h0_ref, tgt_ref, small_ref, wg_ref = rest[nc:nc + 4]
        couts = rest[nc + 4:2 * nc + 4]
        s_stage, w_stage, meta, msem = rest[2 * nc + 4:2 * nc + 8]
        g_s = _Gather([s_stage], [small_ref], rest[2 * nc + 8:2 * nc + 11])
        g_w = _Gather([w_stage], [wg_ref], rest[2 * nc + 11:], place=_pair_place)
        s = pl.program_id(0)
        i = (s + 1) % nt

        @pl.when(s == 0)
        def _():
            s_stage[...] = s_ref[...]
            w_stage[...] = w_ref[...].astype(_BF)
            g_s.start()
            g_w.start()
            meta[...] = jnp.zeros_like(meta)
            for a in range(nc):
                couts[a][...] = cins[a][...].astype(_BF)

        @pl.when(s == nt - 1)
        def _():
            for j in range(3):
                g_s.forward(j)
            g_s.finish()
            cps = [pltpu.make_async_copy(small_ref.at[k, pl.ds(0, N_META), :], meta.at[:, pl.ds(128 * k, 128)],
                                         msem.at[k]) for k in range(N_DEV)]
            for cp in cps:
                cp.start()
            for cp in cps:
                cp.wait()
            for j in range(3):
                g_w.forward(j)
            g_w.finish()

        has_x = i < nx
        h0_ref[pl.ds(0, N_META), :] = jnp.where(i == 0, meta[...], xp_ref[...])
        h0_ref[pl.ds(N_META, body_rows), :] = jnp.where(has_x, xm_ref[pl.ds(0, body_rows), :], 0.0)
        tgt_ref[pl.ds(0, N_META), :] = jnp.where(i == 0, 0.0, tp_ref[...])
        tgt_ref[pl.ds(N_META, body_rows), :] = jnp.where(has_x, tm_ref[pl.ds(0, body_rows), :], 0.0)

    def tile_of(s):
        return (s + 1) % nt

    hbm = pl.BlockSpec(memory_space=pl.ANY)
    main = pl.BlockSpec((TM, D), lambda s: (jnp.minimum(tile_of(s), nx - 1), 0))
    prev = pl.BlockSpec((N_META, D), lambda s: (jnp.maximum(tile_of(s) * (TM // N_META) - 1, 0), 0))
    tile = pl.BlockSpec((TM, D), lambda s: (tile_of(s), 0))
    return pl.pallas_call(
        body, name="prologue", grid=(nt,),
        in_specs=[main, prev, main, prev, _const(small_l.shape), _const(w_in_l.shape)]
        + [_const(l.shape) for l in cast_f32],
        out_specs=[tile, tile, hbm, hbm] + [_full(l.shape) for l in cast_f32],
        out_shape=[_S((nt * TM, D), _F32), _S((nt * TM, D), _F32), _S((N_DEV,) + small_l.shape, _F32),
                   _S((4, D, WIN_P), _BF)] + [_S(l.shape, _BF) for l in cast_f32],
        scratch_shapes=[pltpu.VMEM(small_l.shape, _F32), pltpu.VMEM(w_in_l.shape, _BF), pltpu.VMEM((N_META, D), _F32),
                        pltpu.SemaphoreType.DMA((N_DEV,))] + _sem_shapes(1) + _sem_shapes(1),
        compiler_params=_cp(("arbitrary",)),
    )(x, x, tgt, tgt, small_l, w_in_l, *cast_f32)


def _adamw_math(w, g, m, v):
    m2 = ADAM_B1 * m + (1.0 - ADAM_B1) * g
    v2 = ADAM_B2 * v + (1.0 - ADAM_B2) * (g * g)
    m_hat = m2 / (1.0 - ADAM_B1 ** ADAM_STEP)
    v_hat = v2 / (1.0 - ADAM_B2 ** ADAM_STEP)
    delta = -ADAM_LR * (m_hat / (jnp.sqrt(v_hat) + ADAM_EPS) + ADAM_WD * w)
    return delta, m2, v2


def _adamw_big(name, recv, w, m, v, rows):
    r_all, c_all = w.shape

    def body(r_ref, w_ref, m_ref, v_ref, g_out, d_out, m_out, v_out):
        g = r_ref[0].astype(_F32)
        for k in range(1, N_DEV):
            g = g + r_ref[k].astype(_F32)
        delta, m2, v2 = _adamw_math(w_ref[...], g, m_ref[...], v_ref[...])
        g_out[...] = g
        d_out[...] = delta
        m_out[...] = m2
        v_out[...] = v2

    tile = pl.BlockSpec((rows, c_all), lambda i: (i, 0))
    return pl.pallas_call(
        body, name=name, grid=(r_all // rows,),
        in_specs=[pl.BlockSpec((N_DEV, rows, c_all), lambda i: (0, i, 0)), tile, tile, tile],
        out_specs=[tile] * 4,
        out_shape=[_S(w.shape, _F32)] * 4,
        compiler_params=_cp(("arbitrary",)),
    )(recv, w, m, v)


def _adamw_small(gathered, slices, wmv):
    ng, npar = len(gathered), len(slices)

    def body(*refs):
        g_refs = refs[:ng]
        wmv_refs = refs[ng:ng + 3 * npar]
        outs = refs[ng + 3 * npar:]
        for i, (ai, r0, nr, c0, ncol) in enumerate(slices):
            g = g_refs[ai][0, pl.ds(r0, nr), pl.ds(c0, ncol)].astype(_F32)
            for k in range(1, N_DEV):
                g = g + g_refs[ai][k, pl.ds(r0, nr), pl.ds(c0, ncol)].astype(_F32)
            w_ref, m_ref, v_ref = wmv_refs[3 * i:3 * i + 3]
            delta, m2, v2 = _adamw_math(w_ref[...], g, m_ref[...], v_ref[...])
            outs[4 * i][...] = g
            outs[4 * i + 1][...] = delta
            outs[4 * i + 2][...] = m2
            outs[4 * i + 3][...] = v2
        total = g_refs[0][0, pl.ds(R_LOSS, 1), pl.ds(0, 128)]
        for k in range(1, N_DEV):
            total = total + g_refs[0][k, pl.ds(R_LOSS, 1), pl.ds(0, 128)]
        outs[4 * npar][...] = total

    flat = [t for trip in wmv for t in trip]
    out_shape = []
    for w, _, _ in wmv:
        out_shape += [_S(w.shape, _F32)] * 4
    out_shape.append(_S((1, 128), _F32))
    return pl.pallas_call(
        body, name="adamw_small", out_shape=out_shape,
        compiler_params=pltpu.CompilerParams(vmem_limit_bytes=VMEM_LIMIT),
    )(*gathered, *flat)


def _block_diag(w):
    eye = jnp.eye(8, dtype=w.dtype)
    return (w[:, :, None, :] * eye[:, None, :, None]).reshape(D_RG, D_RG)


def _diag_blocks(g):
    return jnp.concatenate([g[64 * h:64 * (h + 1), 64 * h:64 * (h + 1)] for h in range(8)], axis=0)


def _local_step(h0, tgt_p, n_valid, g_mix, w_in, vec, wr, wi, hb, g_hg, w_out_l, g_ffn, w_gu_l, w_down_l, g_fin):
    t_pad = h0.shape[0]
    me = 4 * lax.axis_index("x") + 2 * lax.axis_index("y") + lax.axis_index("c")
    p, u, y, hs, o, sc, w_out, w_gu, w_down = _mixer_fwd(h0, g_mix, w_in, wr, wi, vec, hb, g_hg,
                                                         [w_out_l, w_gu_l, w_down_l])
    w_out = w_out.reshape(D, D)
    w_down = w_down.reshape(4, FFB, D)
    h1, v, gu, act, dh2, dh2b, loss, gfin = _ffn_loss(h0, y, w_out, g_ffn, w_gu, w_down, g_fin, tgt_p, n_valid)

    dgu, dh1, dh1b, dy, gffn = _ffn_bwd(dh2, dh2b, gu, h1, g_ffn, w_gu, w_down, w_out)
    g_wdown = _wgrad("wgrad_down", act, dh2b, pl.BlockSpec((1, t_pad, FFB), lambda j: (j, 0, 0)),
                     pl.BlockSpec((t_pad, D), lambda j: (0, 0)), 4, (FFB, D))
    g_wgu, r_wdown = _wgrad("wgrad_gate_up", dgu, v, pl.BlockSpec((1, t_pad, FFB), lambda j: (j, 0, 0)),
                            pl.BlockSpec((t_pad, D), lambda j: (0, 0)), N_DEV, (FFB, D),
                            scatter=[g_wdown.reshape(N_DEV, D_FF // N_DEV, D)])
    g_wout = _wgrad("wgrad_out", y, dh1b, pl.BlockSpec((t_pad, D // N_DEV), lambda j: (0, j)),
                    pl.BlockSpec((t_pad, D), lambda j: (0, 0)), N_DEV, (D // N_DEV, D))
    dp, gvec, gw, r_wgu, r_wout = _mixer_bwd(p, hs, o, sc, dy, wr, wi, vec, hb, g_hg, [g_wgu, g_wout])
    pack_c = jnp.concatenate([_diag_blocks(gw[0]), _diag_blocks(gw[1])], axis=1).astype(_BF)
    order = (me ^ jnp.array(_SEND_ORDER, jnp.int32)).astype(jnp.int32)
    dh0, r_win, all_b, all_c, all_a = _inproj_bwd_send(dp, w_in, h0, dh1, g_mix, u, order, gffn, gfin, loss,
                                                       [gvec, pack_c])
    return dh0, (r_win, r_wgu, r_wout, r_wdown), (all_a, all_b, all_c)


def kernel(x, meta_tokens, mix_norm_g, w_in, conv_w, conv_b, w_rgate, b_rgate, w_igate, b_igate, lru_lambda, rg_norm_g, hg_lower_bound, hg_norm_g, w_out, ffn_norm_g, w_gate_up, w_down, final_norm_g, loss_target, m_meta_tokens, m_mix_norm_g, m_w_in, m_conv_w, m_conv_b, m_w_rgate, m_b_rgate, m_w_igate, m_b_igate, m_lru_lambda, m_rg_norm_g, m_hg_lower_bound, m_hg_norm_g, m_w_out, m_ffn_norm_g, m_w_gate_up, m_w_down, m_final_norm_g, v_meta_tokens, v_mix_norm_g, v_w_in, v_conv_w, v_conv_b, v_w_rgate, v_b_rgate, v_w_igate, v_b_igate, v_lru_lambda, v_rg_norm_g, v_hg_lower_bound, v_hg_norm_g, v_w_out, v_ffn_norm_g, v_w_gate_up, v_w_down, v_final_norm_g):
    seq = x.shape[1]
    me = 4 * lax.axis_index("x") + 2 * lax.axis_index("y") + lax.axis_index("c")

    n_valid = N_META + seq
    small_l = jnp.concatenate([meta_tokens, jnp.pad(conv_w[0], ((0, 4), (0, 64)))], axis=0)
    h0, tgt_p, small_g, w_in_g, w_gu_l, w_out_l, w_down_l = _prologue(
        x[0], loss_target[0], small_l, w_in[0], [w_gate_up[0].T, w_out[0], w_down[0]])
    conv_w_full = jnp.transpose(small_g[:, N_META:N_META + 4, :64], (1, 0, 2)).reshape(4, D_RG)
    vec = jnp.concatenate([conv_b, b_rgate, b_igate, lru_lambda, rg_norm_g, jnp.zeros((3, D_RG), _F32),
                           conv_w_full, jnp.zeros((4, D_RG), _F32)], axis=0)
    wr = _block_diag(w_rgate[0]).astype(_BF)
    wi = _block_diag(w_igate[0]).astype(_BF)

    dh0, (r_win, r_wgu, r_wout, r_wdown), (all_a, all_b, all_c) = _local_step(
        h0, tgt_p, n_valid, mix_norm_g, w_in_g, vec, wr, wi, hg_lower_bound, hg_norm_g,
        w_out_l, ffn_norm_g, w_gu_l, w_down_l, final_norm_g.reshape(1, D))
    grad_x = dh0[N_META:N_META + seq][None]

    outs = {}
    outs["w_in"] = _adamw_big("adamw_w_in", r_win, w_in[0], m_w_in[0], v_w_in[0], 256)
    outs["w_gate_up"] = [r.T for r in _adamw_big("adamw_w_gate_up", r_wgu, w_gate_up[0].T, m_w_gate_up[0].T,
                                                 v_w_gate_up[0].T, 176)]
    outs["w_out"] = _adamw_big("adamw_w_out", r_wout, w_out[0], m_w_out[0], v_w_out[0], 128)
    outs["w_down"] = _adamw_big("adamw_w_down", r_wdown, w_down[0], m_w_down[0], v_w_down[0], 176)

    meta_part = lax.dynamic_slice_in_dim(all_a[:, R_META:R_META + N_META, :], me * 128, 128, axis=2)
    convw_part = lax.dynamic_slice_in_dim(all_b[:, R_CONVW:R_CONVW + 4, :], me * 64, 64, axis=2)
    gathered = [all_a, all_b, all_c, meta_part, convw_part]
    small_params = [
        ("meta_tokens", (3, 0, N_META, 0, 128), (meta_tokens, m_meta_tokens, v_meta_tokens), (N_META, 128)),
        ("mix_norm_g", (0, R_GMIX, 1, 0, D), (mix_norm_g, m_mix_norm_g, v_mix_norm_g), (1, D)),
        ("conv_w", (4, 0, 4, 0, 64), (conv_w, m_conv_w, v_conv_w), (4, 64)),
        ("conv_b", (1, R_CONVB, 1, 0, D_RG), (conv_b, m_conv_b, v_conv_b), (1, D_RG)),
        ("w_rgate", (2, 0, 512, 0, 64), (w_rgate, m_w_rgate, v_w_rgate), (512, 64)),
        ("b_rgate", (1, R_BR, 1, 0, D_RG), (b_rgate, m_b_rgate, v_b_rgate), (1, D_RG)),
        ("w_igate", (2, 0, 512, 64, 64), (w_igate, m_w_igate, v_w_igate), (512, 64)),
        ("b_igate", (1, R_BI, 1, 0, D_RG), (b_igate, m_b_igate, v_b_igate), (1, D_RG)),
        ("lru_lambda", (1, R_LAM, 1, 0, D_RG), (lru_lambda, m_lru_lambda, v_lru_lambda), (1, D_RG)),
        ("rg_norm_g", (1, R_GRG, 1, 0, D_RG), (rg_norm_g, m_rg_norm_g, v_rg_norm_g), (1, D_RG)),
        ("hg_lower_bound", (1, R_HB0, 2, 0, D_HG), (hg_lower_bound, m_hg_lower_bound, v_hg_lower_bound), (2, D_HG)),
        ("hg_norm_g", (1, R_GHG, 1, 0, HD), (hg_norm_g, m_hg_norm_g, v_hg_norm_g), (1, HD)),
        ("ffn_norm_g", (0, R_GFFN, 1, 0, D), (ffn_norm_g, m_ffn_norm_g, v_ffn_norm_g), (1, D)),
        ("final_norm_g", (0, R_GFIN, 1, 0, D), (final_norm_g, m_final_norm_g, v_final_norm_g), (1, D)),
    ]
    res = _adamw_small(gathered, [s[1] for s in small_params],
                       [tuple(t.reshape(s[3]) for t in s[2]) for s in small_params])
    for i, s in enumerate(small_params):
        outs[s[0]] = [r.reshape(s[2][0].shape) for r in res[4 * i:4 * i + 4]]
    for n, ref in (("w_in", w_in), ("w_gate_up", w_gate_up), ("w_out", w_out), ("w_down", w_down)):
        outs[n] = [r.reshape(ref.shape) for r in outs[n]]

    loss_all = res[4 * len(small_params)][0, 0]
    order = ["meta_tokens", "mix_norm_g", "w_in", "conv_w", "conv_b", "w_rgate", "b_rgate", "w_igate", "b_igate",
             "lru_lambda", "rg_norm_g", "hg_lower_bound", "hg_norm_g", "w_out", "ffn_norm_g", "w_gate_up", "w_down",
             "final_norm_g"]
    return (loss_all, grad_x, *[outs[n][0] for n in order], *[outs[n][1] for n in order],
            *[outs[n][2] for n in order], *[outs[n][3] for n in order])
```

```python
import functools

import jax
import jax.numpy as jnp
from jax import lax
from jax.experimental import pallas as pl
from jax.experimental.pallas import tpu as pltpu

_BF = jnp.bfloat16
_F32 = jnp.float32
_S = jax.ShapeDtypeStruct
_MESH = pl.DeviceIdType.MESH

N_DEV = 8
N_META = 16
D = 1024
D_RG = 512
D_HG = 512
HD = 128
NH = D_HG // HD
D_IN = 3072
D_FF = 2816
FFB = D_FF // 4
WIN_B = D_IN // N_DEV
WIN_P = 2 * WIN_B
EPS = 1e-6
LRU_C = 8.0
TM = 256
HC = 64
VMEM_LIMIT = 56 * 1024 * 1024

ADAM_LR = 0.001
ADAM_B1 = 0.9
ADAM_B2 = 0.999
ADAM_EPS = 1e-08
ADAM_WD = 0.01
ADAM_STEP = 10

_SEND_ORDER = (6, 4, 2, 7, 5, 3, 1, 0)

R_CONVB, R_BR, R_BI, R_LAM, R_GRG, R_HB0, R_HB1, R_GHG, R_CONVW = 0, 1, 2, 3, 4, 5, 6, 7, 8
R_GMIX, R_GFFN, R_GFIN, R_LOSS, R_META = 0, 1, 2, 3, 8


def _cp(sem=None, **kw):
    return pltpu.CompilerParams(dimension_semantics=sem, vmem_limit_bytes=VMEM_LIMIT, **kw)


def _dot(a, b):
    return jnp.dot(a, b, preferred_element_type=_F32)


def _dot_nt(a, b):
    return lax.dot_general(a, b, (((1,), (1,)), ((), ())), preferred_element_type=_F32)


def _dot_tn(a, b):
    return lax.dot_general(a, b, (((0,), (0,)), ((), ())), preferred_element_type=_F32)


def _sigmoid(x):
    return 0.5 * jnp.tanh(0.5 * x) + 0.5


def _dsilu(x, s):
    return s * (1.0 + x * (1.0 - s))


_GELU_C = 0.7978845608028654


def _gelu_parts(x):
    t = jnp.tanh(_GELU_C * (x + 0.044715 * (x * x * x)))
    g = 0.5 * x * (1.0 + t)
    dg = 0.5 * (1.0 + t) + 0.5 * x * (1.0 - t * t) * (_GELU_C * (1.0 + 3.0 * 0.044715 * (x * x)))
    return g, dg


def _softplus(z):
    e = jnp.exp(-jnp.abs(z))
    w = 1.0 + e
    l1p = jnp.where(w == 1.0, e, jnp.log(w) * e / jnp.where(w == 1.0, 1.0, w - 1.0))
    return jnp.maximum(z, 0.0) + l1p


def _rms_fwd(x):
    r = lax.rsqrt(jnp.mean(x * x, axis=-1, keepdims=True) + EPS)
    return x * r, r


def _rms_bwd(dyg, n, r):
    return r * (dyg - n * jnp.mean(dyg * n, axis=-1, keepdims=True))


def _full(shape):
    nd = len(shape)
    return pl.BlockSpec(shape, lambda i: (0,) * nd)


def _const(shape):
    nd = len(shape)
    return pl.BlockSpec(shape, lambda i: (0,) * nd, pipeline_mode=pl.Buffered(1))


def _carry_gather(gather, i, nt):
    @pl.when(i == 0)
    def _():
        gather.start()

    def tail():
        for j in range(3):
            @pl.when(i == max(nt - 4 + j, 0))
            def _(j=j):
                gather.forward(j)

        @pl.when(i == nt - 1)
        def _():
            gather.finish()

    return tail


def _pair_place(ref, block):
    return ref.at[block // 2, :, pl.ds(pl.multiple_of((block % 2) * WIN_B, WIN_B), WIN_B)]


def _rg_gates(xc, wr_ref, wi_ref, vec_ref):
    xcb = xc.astype(_BF)
    r = _sigmoid(_dot(xcb, wr_ref[...]) + vec_ref[R_BR:R_BR + 1, :])
    ig = _sigmoid(_dot(xcb, wi_ref[...]) + vec_ref[R_BI:R_BI + 1, :])
    nsp8 = -LRU_C * _softplus(-vec_ref[R_LAM:R_LAM + 1, :])
    la = nsp8 * r
    a = jnp.exp(la)
    th = jnp.tanh(la)
    s = jnp.sqrt(-2.0 * th / (1.0 - th))
    return r, ig, a, s, nsp8


def _conv(xbuf, vec_ref):
    acc = vec_ref[R_CONVW:R_CONVW + 1, :] * xbuf[pl.ds(5, TM), :]
    for j in range(1, 4):
        acc = acc + vec_ref[R_CONVW + j:R_CONVW + j + 1, :] * xbuf[pl.ds(5 + j, TM), :]
    return vec_ref[R_CONVB:R_CONVB + 1, :] + acc


def _dot3(m01, x):
    hi = x.astype(_BF)
    r1 = x - hi.astype(_F32)
    mid = r1.astype(_BF)
    lo = (r1 - mid.astype(_F32)).astype(_BF)
    return (_dot(m01, lo) + _dot(m01, mid)) + _dot(m01, hi)


def _chunk_masks():
    row = lax.broadcasted_iota(jnp.int32, (TM, TM), 0)
    col = lax.broadcasted_iota(jnp.int32, (TM, TM), 1)
    shift = HC.bit_length() - 1
    same = lax.shift_right_logical(row, shift) == lax.shift_right_logical(col, shift)
    return same, same & (row >= col), same & (col >= row)


def _per_chunk_rows(x, r):
    return jnp.concatenate([jnp.broadcast_to(x[HC * c + r:HC * c + r + 1, :], (HC, x.shape[1]))
                            for c in range(TM // HC)], axis=0)


def _hg_prep(p_ref, lb, tri_blk):
    hq = p_ref[:, pl.ds(2 * D_RG, D_HG)]
    hf = p_ref[:, pl.ds(2 * D_RG + D_HG, D_HG)]
    sq = _sigmoid(hq)
    q = hq * sq
    sg = _sigmoid(hf)
    f = lb + (1.0 - lb) * sg
    k = 1.0 - f
    b = _dot3(tri_blk, jnp.log(f))
    bm = _per_chunk_rows(b, HC // 2 - 1)
    bl = _per_chunk_rows(b, HC - 1)
    e_q = jnp.exp(b - bm)
    e_k = jnp.exp(bm - b)
    e_b = jnp.exp(b)
    e_l = jnp.exp(bl - b)
    return dict(hq=hq, sq=sq, q=q, sg=sg, f=f, k=k, e_q=e_q, e_k=e_k, e_b=e_b, e_l=e_l,
                qd=q * e_q, kd=k * e_k, qe=q * e_b, ke=k * e_l, e_end=jnp.exp(bl))


def _mixer_fwd(h0, g_mix, w_in, wr, wi, vec, hb, g_hg, shards):
    t_pad = h0.shape[0]
    nt = t_pad // TM
    nc_t = TM // HC
    nsh = len(shards)

    def body(h_ref, gmix_ref, win_ref, wr_ref, wi_ref, vec_ref, hb_ref, ghg_ref, *rest):
        sh_refs, rest = rest[:nsh], rest[nsh:]
        pout_ref, uout_ref, y_ref, hs_ref, o_ref, sc_ref = rest[:6]
        gath_refs, rest = rest[6:6 + nsh], rest[6 + nsh:]
        xbuf, a_s, b_s, hcar, st, qd_s, kd_s, qe_s, ke_s, v_s, u_s, p_s, p_ref = rest[:13]
        i = pl.program_id(0)
        tail = _carry_gather(_Gather(sh_refs, gath_refs, rest[13:]), i, nt + 1)

        @pl.when(i == 0)
        def _():
            p_s[...] = jnp.zeros_like(p_s)

        p_ref[...] = p_s[...]

        @pl.when(i <= 1)
        def _():
            xbuf[pl.ds(0, 8), :] = jnp.zeros((8, D_RG), _F32)
            hcar[...] = jnp.zeros_like(hcar)
            st[...] = jnp.zeros_like(st)

        n_h, _ = _rms_fwd(h_ref[...])
        u = (n_h * gmix_ref[...]).astype(_BF)
        uout_ref[...] = u
        pieces = [(j, k) for j in range(4) for k in range(WIN_P // 256)]

        def project(count):
            for _ in range(count):
                j, k = pieces.pop(0)
                blk = _dot(u, win_ref[j, :, pl.ds(256 * k, 256)])
                p_s[:, pl.ds(WIN_P * j + 256 * k, 256)] = blk
                pout_ref[:, pl.ds(WIN_P * j + 256 * k, 256)] = blk

        x = p_ref[:, pl.ds(0, D_RG)]
        xbuf[pl.ds(8, TM), :] = x
        xc = _conv(xbuf, vec_ref)
        xbuf[pl.ds(0, 8), :] = x[TM - 8:, :]
        r, ig, a, s, _ = _rg_gates(xc, wr_ref, wi_ref, vec_ref)
        a_s[...] = a
        b_s[...] = s * (ig * xc)

        def step(t, h):
            h = a_s[pl.ds(t, 1), :] * h + b_s[pl.ds(t, 1), :]
            hs_ref[pl.ds(t, 1), :] = h
            return h

        hcar[pl.ds(0, 1), :] = lax.fori_loop(0, TM, step, hcar[pl.ds(0, 1), :], unroll=8)
        gel, _ = _gelu_parts(p_ref[:, pl.ds(D_RG, D_RG)])
        n, _ = _rms_fwd(gel * hs_ref[...])
        y_ref[:, pl.ds(0, D_RG)] = (n * vec_ref[R_GRG:R_GRG + 1, :]).astype(_BF)

        lb = _sigmoid(hb_ref[0:1, :] - hb_ref[1:2, :])
        _, tri_blk, _ = _chunk_masks()
        q = _hg_prep(p_ref, lb, tri_blk.astype(_BF))
        for name, ref in (("qd", qd_s), ("kd", kd_s), ("qe", qe_s), ("ke", ke_s)):
            ref[...] = q[name].astype(_BF)
        v_s[...] = p_ref[:, pl.ds(2 * D_RG + 2 * D_HG, D_HG)].astype(_BF)
        e_end = q["e_end"]
        causal = (lax.broadcasted_iota(jnp.int32, (HC, HC), 0) >= lax.broadcasted_iota(jnp.int32, (HC, HC), 1))
        for c in range(nc_t):
            for h in range(NH):
                rs, cs = pl.ds(HC * c, HC), pl.ds(HD * h, HD)
                amat = jnp.where(causal, _dot_nt(qd_s[rs, cs], kd_s[rs, cs]), 0.0)
                o_ref[rs, cs] = _dot(amat.astype(_BF), v_s[rs, cs])
                u_s[NH * c + h] = _dot_tn(v_s[rs, cs], ke_s[rs, cs])
                if pieces:
                    project(1)
        assert not pieces
        for h in range(NH):
            cs = pl.ds(HD * h, HD)
            s_run = st[h]
            for c in range(nc_t):
                rs = pl.ds(HC * c, HC)
                sc_ref[c, h] = s_run
                o_ref[rs, cs] += _dot_nt(qe_s[rs, cs], s_run.astype(_BF))
                s_run = e_end[HC * c:HC * c + 1, HD * h:HD * (h + 1)] * s_run + u_s[NH * c + h]
            st[h] = s_run
        for h in range(NH):
            cs = pl.ds(HD * h, HD)
            n_o, _ = _rms_fwd(o_ref[:, cs])
            hg = p_ref[:, pl.ds(2 * D_RG + 3 * D_HG + HD * h, HD)]
            y_ref[:, pl.ds(D_RG + HD * h, HD)] = ((n_o * ghg_ref[...]) * (hg * _sigmoid(hg))).astype(_BF)

        tail()

    hbm = pl.BlockSpec(memory_space=pl.ANY)

    def proj(i):
        return jnp.minimum(i, nt - 1)

    def mixed(i):
        return jnp.maximum(i - 1, 0)

    return pl.pallas_call(
        body, name="mixer_fwd", grid=(nt + 1,),
        in_specs=[pl.BlockSpec((TM, D), lambda i: (proj(i), 0)), _full((1, D)), _const((4, D, WIN_P)),
                  _full((D_RG, D_RG)), _full((D_RG, D_RG)),
                  _full((16, D_RG)), _full((2, D_HG)), _full((1, HD))] + [hbm] * nsh,
        out_specs=[pl.BlockSpec((TM, D_IN), lambda i: (proj(i), 0)), pl.BlockSpec((TM, D), lambda i: (proj(i), 0)),
                   pl.BlockSpec((TM, D), lambda i: (mixed(i), 0)), pl.BlockSpec((TM, D_RG), lambda i: (mixed(i), 0)),
                   pl.BlockSpec((TM, D_HG), lambda i: (mixed(i), 0)),
                   pl.BlockSpec((nc_t, NH, HD, HD), lambda i: (mixed(i), 0, 0, 0))] + [hbm] * nsh,
        out_shape=[_S((t_pad, D_IN), _F32), _S((t_pad, D), _BF),
                   _S((t_pad, D), _BF), _S((t_pad, D_RG), _F32), _S((t_pad, D_HG), _F32),
                   _S((t_pad // HC, NH, HD, HD), _F32)] + [_S((N_DEV,) + s.shape, s.dtype) for s in shards],
        scratch_shapes=[pltpu.VMEM((TM + 8, D_RG), _F32), pltpu.VMEM((TM, D_RG), _F32),
                        pltpu.VMEM((TM, D_RG), _F32), pltpu.VMEM((8, D_RG), _F32),
                        pltpu.VMEM((NH, HD, HD), _F32)] + [pltpu.VMEM((TM, D_HG), _BF) for _ in range(5)]
        + [pltpu.VMEM((nc_t * NH, HD, HD), _F32), pltpu.VMEM((TM, D_IN), _F32), pltpu.VMEM((TM, D_IN), _F32)]
        + _sem_shapes(nsh),
        compiler_params=_cp(("arbitrary",)),
    )(h0, g_mix, w_in, wr, wi, vec, hb, g_hg, *shards)


def _ffn_loss(h0, y, w_out, g_ffn, w_gu, w_down, g_fin, tgt, n_valid):
    t_pad = h0.shape[0]

    def body(h_ref, y_ref, wo_ref, gffn_ref, wgu_hbm, wd_hbm, g_ref, t_ref,
             h1_ref, v_ref, gu_ref, act_ref, dh2_ref, dh2b_ref, loss_ref, gfin_ref, wgu_ref, wd_ref, wsem):
        i = pl.program_id(0)

        def weights(b):
            return [pltpu.make_async_copy(wgu_hbm.at[b], wgu_ref.at[b], wsem.at[3 * b]),
                    pltpu.make_async_copy(wgu_hbm.at[4 + b], wgu_ref.at[4 + b], wsem.at[3 * b + 1]),
                    pltpu.make_async_copy(wd_hbm.at[b], wd_ref.at[b], wsem.at[3 * b + 2])]

        def one_tile(first):
            h1 = h_ref[...] + _dot(y_ref[...], wo_ref[...])
            h1_ref[...] = h1
            n1, _ = _rms_fwd(h1)
            vb = (n1 * gffn_ref[...]).astype(_BF)
            v_ref[...] = vb
            h2 = h1
            for b in range(4):
                if first:
                    for cp in weights(b):
                        cp.wait()
                gate = _dot_nt(vb, wgu_ref[b])
                up = _dot_nt(vb, wgu_ref[4 + b])
                gu_ref[b] = gate
                gu_ref[4 + b] = up
                act = ((gate * _sigmoid(gate)) * up).astype(_BF)
                act_ref[b] = act
                h2 = h2 + _dot(act, wd_ref[b])
            n, r = _rms_fwd(h2)
            out = n * g_ref[...]
            row = i * TM + lax.broadcasted_iota(jnp.int32, (TM, 1), 0)
            valid = (row >= N_META) & (row < n_valid)
            err = jnp.where(valid, out - t_ref[...], 0.0)
            loss_ref[...] += (0.5 / D) * jnp.sum(err * err)
            dout = err * (1.0 / D)
            gfin_ref[...] += jnp.sum(dout * n, axis=0, keepdims=True)
            dh2 = _rms_bwd(dout * g_ref[...], n, r)
            dh2_ref[...] = dh2
            dh2b_ref[...] = dh2.astype(_BF)

        @pl.when(i == 0)
        def _():
            for b in range(4):
                for cp in weights(b):
                    cp.start()
            loss_ref[...] = jnp.zeros_like(loss_ref)
            gfin_ref[...] = jnp.zeros_like(gfin_ref)
            one_tile(first=True)

        @pl.when(i > 0)
        def _():
            one_tile(first=False)

    tile = pl.BlockSpec((TM, D), lambda i: (i, 0))
    hbm = pl.BlockSpec(memory_space=pl.ANY)
    return pl.pallas_call(
        body, name="ffn_loss", grid=(t_pad // TM,),
        in_specs=[tile, tile, _const((D, D)), _full((1, D)), hbm, hbm, _full((1, D)), tile],
        out_specs=[tile, tile,
                   pl.BlockSpec((N_DEV, TM, FFB), lambda i: (0, i, 0)), pl.BlockSpec((4, TM, FFB), lambda i: (0, i, 0)),
                   tile, tile, _full((8, 128)), _full((1, D))],
        out_shape=[_S((t_pad, D), _F32), _S((t_pad, D), _BF),
                   _S((N_DEV, t_pad, FFB), _F32), _S((4, t_pad, FFB), _BF), _S((t_pad, D), _F32),
                   _S((t_pad, D), _BF), _S((8, 128), _F32), _S((1, D), _F32)],
        scratch_shapes=[pltpu.VMEM((N_DEV, FFB, D), _BF), pltpu.VMEM((4, FFB, D), _BF), pltpu.SemaphoreType.DMA((12,))],
        compiler_params=_cp(("arbitrary",)),
    )(h0, y, w_out, g_ffn, w_gu, w_down, g_fin, tgt)


def _ffn_bwd(dh2, dh2b, gu, h1, g_ffn, w_gu, w_down, w_out):
    t_pad = dh2.shape[0]

    def body(dh2_ref, dh2b_ref, gu_ref, h1_ref, g_ref, wgu_hbm, wd_hbm, wo_hbm,
             dgu_ref, dh1_ref, dh1b_ref, dy_ref, gffn_ref, wgu_ref, wd_ref, wo_ref, wsem):
        i = pl.program_id(0)

        def weights(b):
            return [pltpu.make_async_copy(wd_hbm.at[b], wd_ref.at[b], wsem.at[3 * b]),
                    pltpu.make_async_copy(wgu_hbm.at[b], wgu_ref.at[b], wsem.at[3 * b + 1]),
                    pltpu.make_async_copy(wgu_hbm.at[4 + b], wgu_ref.at[4 + b], wsem.at[3 * b + 2])]

        def out_weight():
            return pltpu.make_async_copy(wo_hbm, wo_ref, wsem.at[12])

        def one_tile(first):
            db = dh2b_ref[...]
            dv = jnp.zeros((TM, D), _F32)
            for b in range(4):
                if first:
                    for cp in weights(b):
                        cp.wait()
                dact = _dot_nt(db, wd_ref[b])
                gate = gu_ref[b]
                up = gu_ref[4 + b]
                sg = _sigmoid(gate)
                dgate = ((dact * up) * _dsilu(gate, sg)).astype(_BF)
                dup = (dact * (gate * sg)).astype(_BF)
                dgu_ref[b] = dgate
                dgu_ref[4 + b] = dup
                dv = dv + _dot(dgate, wgu_ref[b]) + _dot(dup, wgu_ref[4 + b])
            n, r = _rms_fwd(h1_ref[...])
            gffn_ref[...] += jnp.sum(dv * n, axis=0, keepdims=True)
            dh1 = dh2_ref[...] + _rms_bwd(dv * g_ref[...], n, r)
            dh1_ref[...] = dh1
            dh1b = dh1.astype(_BF)
            dh1b_ref[...] = dh1b
            if first:
                out_weight().wait()
            dy_ref[...] = _dot_nt(dh1b, wo_ref[...])

        @pl.when(i == 0)
        def _():
            for b in range(4):
                for cp in weights(b):
                    cp.start()
            out_weight().start()
            gffn_ref[...] = jnp.zeros_like(gffn_ref)
            one_tile(first=True)

        @pl.when(i > 0)
        def _():
            one_tile(first=False)

    tile = pl.BlockSpec((TM, D), lambda i: (i, 0))
    hbm = pl.BlockSpec(memory_space=pl.ANY)
    return pl.pallas_call(
        body, name="ffn_bwd", grid=(t_pad // TM,),
        in_specs=[tile, tile, pl.BlockSpec((N_DEV, TM, FFB), lambda i: (0, i, 0)), tile, _full((1, D)),
                  hbm, hbm, hbm],
        out_specs=[pl.BlockSpec((N_DEV, TM, FFB), lambda i: (0, i, 0)), tile, tile, tile, _full((1, D))],
        out_shape=[_S((N_DEV, t_pad, FFB), _BF), _S((t_pad, D), _F32), _S((t_pad, D), _BF),
                   _S((t_pad, D), _F32), _S((1, D), _F32)],
        scratch_shapes=[pltpu.VMEM((N_DEV, FFB, D), _BF), pltpu.VMEM((4, FFB, D), _BF), pltpu.VMEM((D, D), _BF),
                        pltpu.SemaphoreType.DMA((13,))],
        compiler_params=_cp(("arbitrary",)),
    )(dh2, dh2b, gu, h1, g_ffn, w_gu, w_down, w_out)


def _mixer_bwd(p, hs, o, sc, dy, wr, wi, vec, hb, g_hg, scatter):
    t_pad = p.shape[0]
    nt = t_pad // TM
    nc_t = TM // HC
    nsc = len(scatter)

    def rev(i):
        return nt - 1 - i

    def body(p_ref, pprev_ref, hs_ref, hprev_ref, o_ref, sc_ref, dy_ref, wr_ref, wi_ref, vec_ref, hb_ref, ghg_ref,
             *rest):
        send_refs, rest = rest[:nsc], rest[nsc:]
        dp_ref, gvec_ref, gw_ref = rest[:3]
        recv_refs, rest = rest[3:3 + nsc], rest[3 + nsc:]
        xbuf, hbuf, dbuf, a_s, g_s, ccar, dst = rest[:7]
        qd_s, kd_s, qe_s, ke_s, v_s, do_s, dqd_s, dkd_s, dqe_s, dke_s, dv_s, w_s, dend_s = rest[7:20]
        exchange = _Exchange(send_refs, [], recv_refs, rest[20:])
        i = pl.program_id(0)
        first_tile = i == nt - 1

        @pl.when(i == 0)
        def _():
            exchange.start()
            gvec_ref[...] = jnp.zeros_like(gvec_ref)
            gw_ref[...] = jnp.zeros_like(gw_ref)
            dbuf[pl.ds(TM, 8), :] = jnp.zeros((8, D_RG), _F32)
            ccar[...] = jnp.zeros_like(ccar)
            dst[...] = jnp.zeros_like(dst)

        def acc(row, val):
            gvec_ref[row:row + 1, :] += jnp.sum(val, axis=0, keepdims=True)

        keep = jnp.where(first_tile, 0.0, 1.0)
        x = p_ref[:, pl.ds(0, D_RG)]
        xbuf[pl.ds(0, 8), :] = pprev_ref[...] * keep
        xbuf[pl.ds(8, TM), :] = x
        xc = _conv(xbuf, vec_ref)
        r, ig, a, s, nsp8 = _rg_gates(xc, wr_ref, wi_ref, vec_ref)
        h = hs_ref[...]
        hbuf[pl.ds(0, 8), :] = hprev_ref[...] * keep
        hbuf[pl.ds(8, TM), :] = h
        hm1 = hbuf[pl.ds(7, TM), :]
        gr = p_ref[:, pl.ds(D_RG, D_RG)]
        gel, dgel = _gelu_parts(gr)
        n, rr = _rms_fwd(gel * h)
        dyn = dy_ref[:, pl.ds(0, D_RG)]
        acc(R_GRG, dyn * n)
        dpre = _rms_bwd(dyn * vec_ref[R_GRG:R_GRG + 1, :], n, rr)
        dp_ref[:, pl.ds(D_RG, D_RG)] = ((dpre * h) * dgel).astype(_BF)
        a_s[...] = a
        g_s[...] = dpre * gel

        def step(k, c):
            t = TM - 1 - k
            g = g_s[pl.ds(t, 1), :] + c
            g_s[pl.ds(t, 1), :] = g
            return a_s[pl.ds(t, 1), :] * g

        ccar[pl.ds(0, 1), :] = lax.fori_loop(0, TM, step, ccar[pl.ds(0, 1), :], unroll=8)
        gt = g_s[...]
        da = gt * hm1
        ixc = ig * xc
        ds = gt * ixc
        dig = (gt * s) * xc
        dxc = (gt * s) * ig
        dla = da * a - ds * ((a * a) / s)
        lam = vec_ref[R_LAM:R_LAM + 1, :]
        gvec_ref[R_LAM:R_LAM + 1, :] += jnp.sum(dla * r, axis=0, keepdims=True) * (LRU_C * _sigmoid(-lam))
        dzr = (dla * nsp8) * (r * (1.0 - r))
        dzi = dig * (ig * (1.0 - ig))
        acc(R_BR, dzr)
        acc(R_BI, dzi)
        xcb = xc.astype(_BF)
        dzrb = dzr.astype(_BF)
        dzib = dzi.astype(_BF)
        gw_ref[0] += _dot_tn(xcb, dzrb)
        gw_ref[1] += _dot_tn(xcb, dzib)
        dxc = dxc + _dot_nt(dzrb, wr_ref[...]) + _dot_nt(dzib, wi_ref[...])
        acc(R_CONVB, dxc)
        for j in range(4):
            acc(R_CONVW + j, dxc * xbuf[pl.ds(5 + j, TM), :])
        dbuf[pl.ds(0, TM), :] = dxc
        dx = vec_ref[R_CONVW + 3:R_CONVW + 4, :] * dxc
        for j in range(3):
            dx = dx + vec_ref[R_CONVW + j:R_CONVW + j + 1, :] * dbuf[pl.ds(3 - j, TM), :]
        dbuf[pl.ds(TM, 8), :] = dxc[0:8, :]
        dp_ref[:, pl.ds(0, D_RG)] = dx.astype(_BF)

        lb = _sigmoid(hb_ref[0:1, :] - hb_ref[1:2, :])
        same, tri_blk, triu_blk = _chunk_masks()
        q = _hg_prep(p_ref, lb, tri_blk.astype(_BF))
        qdb, kdb = q["qd"].astype(_BF), q["kd"].astype(_BF)
        qd_s[...] = qdb
        kd_s[...] = kdb
        qe_s[...] = q["qe"].astype(_BF)
        ke_s[...] = q["ke"].astype(_BF)
        v_s[...] = p_ref[:, pl.ds(2 * D_RG + 2 * D_HG, D_HG)].astype(_BF)
        e_end = q["e_end"]
        ghg = ghg_ref[...]
        for h in range(NH):
            cs = pl.ds(HD * h, HD)
            hg = p_ref[:, pl.ds(2 * D_RG + 3 * D_HG + HD * h, HD)]
            sh = _sigmoid(hg)
            n_o, r_o = _rms_fwd(o_ref[:, cs])
            dyh = dy_ref[:, pl.ds(D_RG + HD * h, HD)]
            dp_ref[:, pl.ds(2 * D_RG + 3 * D_HG + HD * h, HD)] = ((dyh * (n_o * ghg)) * _dsilu(hg, sh)).astype(_BF)
            dn = dyh * (hg * sh)
            gvec_ref[R_GHG:R_GHG + 1, pl.ds(0, HD)] += jnp.sum(dn * n_o, axis=0, keepdims=True)
            do_s[:, cs] = _rms_bwd(dn * ghg, n_o, r_o).astype(_BF)
        causal = (lax.broadcasted_iota(jnp.int32, (HC, HC), 0) >= lax.broadcasted_iota(jnp.int32, (HC, HC), 1))
        for c in range(nc_t):
            for h in range(NH):
                rs, cs = pl.ds(HC * c, HC), pl.ds(HD * h, HD)
                qd_c, kd_c, do_c = qd_s[rs, cs], kd_s[rs, cs], do_s[rs, cs]
                amat = jnp.where(causal, _dot_nt(qd_c, kd_c), 0.0).astype(_BF)
                da_m = jnp.where(causal, _dot_nt(do_c, v_s[rs, cs]), 0.0).astype(_BF)
                dqd_s[rs, cs] = _dot(da_m, kd_c)
                dkd_s[rs, cs] = _dot_tn(da_m, qd_c)
                dqe_s[rs, cs] = _dot(do_c, sc_ref[c, h].astype(_BF))
                dv_s[rs, cs] = _dot_tn(amat, do_c)
                w_s[NH * c + h] = _dot_tn(do_c, qe_s[rs, cs])
        for h in range(NH):
            cs = pl.ds(HD * h, HD)
            d_run = dst[h]
            for c in reversed(range(nc_t)):
                rs = pl.ds(HC * c, HC)
                d_b = d_run.astype(_BF)
                dke_s[rs, cs] = _dot(v_s[rs, cs], d_b)
                dp_ref[rs, pl.ds(2 * D_RG + 2 * D_HG + HD * h, HD)] = (
                    dv_s[rs, cs] + _dot_nt(ke_s[rs, cs], d_b)).astype(_BF)
                dend_s[pl.ds(c, 1), cs] = jnp.sum(sc_ref[c, h] * d_run, axis=0, keepdims=True)
                d_run = w_s[NH * c + h] + e_end[HC * c:HC * c + 1, HD * h:HD * (h + 1)] * d_run
            dst[h] = d_run
        dqd, dkd, dqe, dke = dqd_s[...], dkd_s[...], dqe_s[...], dke_s[...]
        dq = dqd * q["e_q"] + dqe * q["e_b"]
        dk = dkd * q["e_k"] + dke * q["e_l"]
        dkeke = dke * q["ke"]
        db = dqd * qdb.astype(_F32) - dkd * kdb.astype(_F32) + dqe * q["qe"] - dkeke
        d_end = jnp.concatenate([jnp.broadcast_to(dend_s[pl.ds(c, 1), :], (HC, D_HG)) for c in range(nc_t)], axis=0)
        dlf = _dot3(triu_blk.astype(_BF), db) + _dot3(same.astype(_BF), dkeke) + d_end * e_end
        df = dlf / q["f"] - dk
        sg = q["sg"]
        gvec_ref[R_HB0:R_HB0 + 1, :] += jnp.sum(df * (1.0 - sg), axis=0, keepdims=True)
        dp_ref[:, pl.ds(2 * D_RG, D_HG)] = (dq * _dsilu(q["hq"], q["sq"])).astype(_BF)
        dp_ref[:, pl.ds(2 * D_RG + D_HG, D_HG)] = ((df * (1.0 - lb)) * (sg * (1.0 - sg))).astype(_BF)

        @pl.when(i == nt - 1)
        def _():
            glb = gvec_ref[R_HB0:R_HB0 + 1, :] * (lb * (1.0 - lb))
            gvec_ref[R_HB0:R_HB0 + 1, :] = glb
            gvec_ref[R_HB1:R_HB1 + 1, :] = -glb
            exchange.finish()

    hbm = pl.BlockSpec(memory_space=pl.ANY)
    return pl.pallas_call(
        body, name="mixer_bwd", grid=(nt,),
        in_specs=[pl.BlockSpec((TM, D_IN), lambda i: (rev(i), 0)),
                  pl.BlockSpec((8, D_RG), lambda i: (jnp.maximum(rev(i) * (TM // 8) - 1, 0), 0)),
                  pl.BlockSpec((TM, D_RG), lambda i: (rev(i), 0)),
                  pl.BlockSpec((8, D_RG), lambda i: (jnp.maximum(rev(i) * (TM // 8) - 1, 0), 0)),
                  pl.BlockSpec((TM, D_HG), lambda i: (rev(i), 0)),
                  pl.BlockSpec((nc_t, NH, HD, HD), lambda i: (rev(i), 0, 0, 0)),
                  pl.BlockSpec((TM, D), lambda i: (rev(i), 0)),
                  _full((D_RG, D_RG)), _full((D_RG, D_RG)), _full((16, D_RG)), _full((2, D_HG)), _full((1, HD))]
        + [hbm] * nsc,
        out_specs=[pl.BlockSpec((TM, D_IN), lambda i: (rev(i), 0)), _full((16, D_RG)), _full((2, D_RG, D_RG))]
        + [hbm] * nsc,
        out_shape=[_S((t_pad, D_IN), _BF), _S((16, D_RG), _F32), _S((2, D_RG, D_RG), _F32)]
        + [_S(s.shape, s.dtype) for s in scatter],
        scratch_shapes=[pltpu.VMEM((TM + 8, D_RG), _F32), pltpu.VMEM((TM + 8, D_RG), _F32),
                        pltpu.VMEM((TM + 8, D_RG), _F32), pltpu.VMEM((TM, D_RG), _F32),
                        pltpu.VMEM((TM, D_RG), _F32), pltpu.VMEM((8, D_RG), _F32),
                        pltpu.VMEM((NH, HD, HD), _F32)]
        + [pltpu.VMEM((TM, D_HG), _BF) for _ in range(6)] + [pltpu.VMEM((TM, D_HG), _F32) for _ in range(5)]
        + [pltpu.VMEM((nc_t * NH, HD, HD), _F32), pltpu.VMEM((8, D_HG), _F32)] + _sem_shapes(nsc),
        compiler_params=_cp(("arbitrary",)),
    )(p, p, hs, hs, o, sc, dy, wr, wi, vec, hb, g_hg, *scatter)


def _inproj_bwd_send(dp, w_in, h0, dh1, g_mix, u, order, gffn, gfin, loss, to_all):
    t_pad = dp.shape[0]
    rb = t_pad // (2 * N_DEV)
    n_steps = N_DEV + 2 * N_DEV
    na = len(to_all)

    def body(order_ref, dpc_ref, dpr_ref, u_ref, w_ref, h_ref, dh1_ref, g_ref, gffn_ref, gfin_ref, loss_ref, *rest):
        all_in = rest[:na]
        dh0_ref, recv_ref = rest[na:na + 2]
        all_out = rest[na + 2:2 * na + 2]
        alla_ref = rest[2 * na + 2]
        buf, pack, blk_send, blk_recv, blk_local = rest[2 * na + 3:2 * na + 8]
        exchange = _Exchange([], all_in, all_out, rest[2 * na + 8:2 * na + 11])
        last = _Exchange([], [pack], [alla_ref], rest[2 * na + 11:])
        s = pl.program_id(0)
        x, y, c = _coords()
        me = 4 * x + 2 * y + c

        def send(step):
            r = _SEND_ORDER[step]
            return pltpu.make_async_remote_copy(
                src_ref=buf.at[step], dst_ref=recv_ref.at[me], send_sem=blk_send.at[step], recv_sem=blk_recv.at[r - 1],
                device_id=(x ^ (r >> 2), y ^ ((r >> 1) & 1), c ^ (r & 1)), device_id_type=_MESH)

        @pl.when(s == 0)
        def _():
            exchange.start()
            pack[...] = jnp.zeros_like(pack)

        @pl.when(s < N_DEV)
        def _():
            buf[s] = _dot_tn(u_ref[...], dpc_ref[...]).astype(_BF)

            for step in range(N_DEV - 1):
                @pl.when(s == step)
                def _(step=step):
                    send(step).start()

        @pl.when(s >= N_DEV)
        def _():
            du = jnp.zeros((rb, D), _F32)
            for j in range(4):
                du = du + _dot_nt(dpr_ref[:, WIN_P * j:WIN_P * (j + 1)], w_ref[j])
            n, r = _rms_fwd(h_ref[...])
            pack[R_GMIX:R_GMIX + 1, :] += jnp.sum(du * n, axis=0, keepdims=True)
            dh0 = dh1_ref[...] + _rms_bwd(du * g_ref[...], n, r)
            dh0_ref[...] = dh0

            @pl.when(s == N_DEV)
            def _():
                pack[R_META:R_META + N_META, :] = dh0[0:N_META, :]

        @pl.when(s == n_steps - 1)
        def _():
            pack[R_GFFN:R_GFFN + 1, :] = gffn_ref[...]
            pack[R_GFIN:R_GFIN + 1, :] = gfin_ref[...]
            pack[R_LOSS:R_LOSS + 1, pl.ds(0, 128)] = loss_ref[0:1, :]
            last.start()
            mine = pltpu.make_async_copy(buf.at[N_DEV - 1], recv_ref.at[me], blk_local.at[0])
            mine.start()
            for step in range(N_DEV - 1):
                send(step).wait_send()
            for r in range(1, N_DEV):
                px, py, pc = x ^ (r >> 2), y ^ ((r >> 1) & 1), c ^ (r & 1)
                pltpu.make_async_remote_copy(
                    src_ref=buf.at[0], dst_ref=recv_ref.at[4 * px + 2 * py + pc], send_sem=blk_send.at[0],
                    recv_sem=blk_recv.at[r - 1], device_id=(px, py, pc), device_id_type=_MESH).wait_recv()
            mine.wait()
            exchange.finish()
            last.finish()

    hbm = pl.BlockSpec(memory_space=pl.ANY)
    rows = pl.BlockSpec((rb, D), lambda s, order: (jnp.maximum(s - N_DEV, 0), 0))
    one = pl.BlockSpec((1, D), lambda s, order: (0, 0))
    res = pl.pallas_call(
        body, name="inproj_bwd_send",
        grid_spec=pltpu.PrefetchScalarGridSpec(
            num_scalar_prefetch=1, grid=(n_steps,),
            in_specs=[pl.BlockSpec((t_pad, WIN_B), lambda s, order: (0, order[jnp.minimum(s, N_DEV - 1)])),
                      pl.BlockSpec((rb, D_IN), lambda s, order: (jnp.maximum(s - N_DEV, 0), 0)),
                      pl.BlockSpec((t_pad, D), lambda s, order: (0, 0), pipeline_mode=pl.Buffered(1)),
                      pl.BlockSpec((4, D, WIN_P), lambda s, order: (0, 0, 0), pipeline_mode=pl.Buffered(1)),
                      rows, rows, one, one, one, pl.BlockSpec((8, 128), lambda s, order: (0, 0))] + [hbm] * na,
            out_specs=[rows] + [hbm] * (na + 2),
            scratch_shapes=[pltpu.VMEM((N_DEV, D, WIN_B), _BF), pltpu.VMEM((24, D), _F32),
                            pltpu.SemaphoreType.DMA((N_DEV - 1,)), pltpu.SemaphoreType.DMA((N_DEV - 1,)),
                            pltpu.SemaphoreType.DMA((1,))] + _sem_shapes(na) + _sem_shapes(1)),
        out_shape=[_S((t_pad, D), _F32), _S((N_DEV, D, WIN_B), _BF)]
        + [_S((N_DEV,) + g.shape, g.dtype) for g in to_all] + [_S((N_DEV, 24, D), _F32)],
        compiler_params=_cp(("arbitrary",)),
    )(order, dp, dp, u, w_in, h0, dh1, g_mix, gffn, gfin, loss, *to_all)
    return res


def _wgrad(name, a, b, a_spec, b_spec, n_blocks, out_block, scatter=()):
    nsc = len(scatter)

    def body(a_ref, b_ref, *rest):
        o_ref = rest[nsc]
        j = pl.program_id(0)
        if nsc:
            exchange = _Exchange(rest[:nsc], [], rest[nsc + 1:2 * nsc + 1], rest[2 * nsc + 1:])

            @pl.when(j == 0)
            def _():
                exchange.start()

        av = a_ref[0] if len(a_ref.shape) == 3 else a_ref[...]
        bv = b_ref[0] if len(b_ref.shape) == 3 else b_ref[...]
        o_ref[0] = _dot_tn(av, bv).astype(_BF)

        if nsc:
            @pl.when(j == n_blocks - 1)
            def _():
                exchange.finish()

    hbm = pl.BlockSpec(memory_space=pl.ANY)
    res = pl.pallas_call(
        body, name=name, grid=(n_blocks,),
        in_specs=[a_spec, b_spec] + [hbm] * nsc,
        out_specs=[pl.BlockSpec((1,) + out_block, lambda j: (j, 0, 0))] + [hbm] * nsc,
        out_shape=[_S((n_blocks,) + out_block, _BF)] + [_S(s.shape, s.dtype) for s in scatter],
        scratch_shapes=_sem_shapes(nsc) if nsc else [],
        compiler_params=_cp(("arbitrary",)),
    )(a, b, *scatter)
    return res if nsc else res[0]


def _coords():
    return lax.axis_index("x"), lax.axis_index("y"), lax.axis_index("c")


def _sem_shapes(na):
    return [pltpu.SemaphoreType.DMA((7 * na,)), pltpu.SemaphoreType.DMA((7 * na,)), pltpu.SemaphoreType.DMA((na,))]


class _Gather:
    def __init__(self, srcs, outs, sems, place=None):
        self.srcs, self.outs = srcs, outs
        self.send_sems, self.recv_sems, self.local_sems = sems
        self.place = place if place is not None else (lambda ref, block: ref.at[block])
        self.na = len(srcs)
        x, y, c = _coords()
        self.pos = (x, y, c)
        self.me = 4 * x + 2 * y + c
        self.sibling = (x, y, 1 - c)
        self.chips = [(1 - x, y), (x, 1 - y), (1 - x, 1 - y)]

    @staticmethod
    def _slot(px, py, pc):
        return 4 * px + 2 * py + pc

    def _copy(self, a, k, block, to, own=False):
        dst = self.place(self.outs[a], block)
        return pltpu.make_async_remote_copy(
            src_ref=self.srcs[a] if own else dst, dst_ref=dst,
            send_sem=self.send_sems.at[7 * a + k], recv_sem=self.recv_sems.at[7 * a + k],
            device_id=to, device_id_type=_MESH)

    def _mine(self, a):
        return pltpu.make_async_copy(self.srcs[a], self.place(self.outs[a], self.me), self.local_sems.at[a])

    def _first(self):
        c = self.pos[2]
        cps = []
        for a in range(self.na):
            cps.append(self._copy(a, 0, self.me, self.sibling, own=True))
            cps += [self._copy(a, 1 + j, self.me, (*chip, c), own=True) for j, chip in enumerate(self.chips)]
        return cps

    def _passed(self):
        c = self.pos[2]
        return [self._copy(a, 4 + j, self._slot(*chip, c), self.sibling)
                for j, chip in enumerate(self.chips) for a in range(self.na)]

    def start(self):
        for a in range(self.na):
            self._mine(a).start()
        for cp in self._first():
            cp.start()

    def forward(self, j):
        c = self.pos[2]
        chip = self.chips[j]
        for a in range(self.na):
            self._copy(a, 1 + j, self._slot(*chip, c), self.pos).wait_recv()
            self._copy(a, 4 + j, self._slot(*chip, c), self.sibling).start()

    def wait_sibling(self):
        x, y, c = self.pos
        for a in range(self.na):
            self._copy(a, 0, self._slot(x, y, 1 - c), self.pos).wait_recv()

    def wait_passed(self, j):
        c = self.pos[2]
        for a in range(self.na):
            self._copy(a, 4 + j, self._slot(*self.chips[j], 1 - c), self.pos).wait_recv()

    def finish_sends(self):
        for cp in self._first() + self._passed():
            cp.wait_send()
        for a in range(self.na):
            self._mine(a).wait()

    def finish(self):
        self.wait_sibling()
        for j in range(3):
            self.wait_passed(j)
        self.finish_sends()


class _Exchange:
    def __init__(self, scatter, gather, outs, sems):
        self.ins = list(scatter) + list(gather)
        self.ns, self.na = len(scatter), len(scatter) + len(gather)
        self.outs = outs
        self.send_sems, self.recv_sems, self.local_sems = sems
        x, y, c = _coords()
        self.pos = (x, y, c)
        self.me = 4 * x + 2 * y + c

    def _peer(self, r):
        x, y, c = self.pos
        return x ^ (r >> 2), y ^ ((r >> 1) & 1), c ^ (r & 1)

    def _src(self, a, block):
        return self.ins[a].at[block] if a < self.ns else self.ins[a]

    def _local(self, a):
        return pltpu.make_async_copy(self._src(a, self.me), self.outs[a].at[self.me], self.local_sems.at[a])

    def _send(self, a, r):
        px, py, pc = self._peer(r)
        return pltpu.make_async_remote_copy(
            src_ref=self._src(a, 4 * px + 2 * py + pc), dst_ref=self.outs[a].at[self.me],
            send_sem=self.send_sems.at[7 * a + r - 1], recv_sem=self.recv_sems.at[7 * a + r - 1],
            device_id=(px, py, pc), device_id_type=_MESH)

    def _recv(self, a, r):
        px, py, pc = self._peer(r)
        return pltpu.make_async_remote_copy(
            src_ref=self._src(a, self.me), dst_ref=self.outs[a].at[4 * px + 2 * py + pc],
            send_sem=self.send_sems.at[7 * a + r - 1], recv_sem=self.recv_sems.at[7 * a + r - 1],
            device_id=(px, py, pc), device_id_type=_MESH)

    def start(self):
        for a in range(self.na):
            self._local(a).start()
        for r in range(1, N_DEV):
            for a in range(self.na):
                self._send(a, r).start()

    def finish(self):
        for r in range(1, N_DEV):
            for a in range(self.na):
                self._recv(a, r).wait_recv()
        for r in range(1, N_DEV):
            for a in range(self.na):
                self._send(a, r).wait_send()
        for a in range(self.na):
            self._local(a).wait()


def _prologue(x, tgt, small_l, w_in_l, cast_f32):
    seq = x.shape[0]
    assert seq % TM == 0
    nx = seq // TM
    nt = nx + 1
    nc = len(cast_f32)
    body_rows = TM - N_META

    def body(xm_ref, xp_ref, tm_ref, tp_ref, s_ref, w_ref, *rest):
        cins = rest[:nc]
        h0_ref, tgt_ref, small_ref, wg_ref = rest[nc:nc + 4]
        couts = rest[nc + 4:2 * nc + 4]
        s_stage, w_stage, meta, msem = rest[2 * nc + 4:2 * nc + 8]
        g_s = _Gather([s_stage], [small_ref], rest[2 * nc + 8:2 * nc + 11])
        g_w = _Gather([w_stage], [wg_ref], rest[2 * nc + 11:], place=_pair_place)
        s = pl.program_id(0)
        i = (s + 1) % nt

        @pl.when(s == 0)
        def _():
            s_stage[...] = s_ref[...]
            w_stage[...] = w_ref[...].astype(_BF)
            g_s.start()
            g_w.start()
            meta[...] = jnp.zeros_like(meta)
            for a in range(nc):
                couts[a][...] = cins[a][...].astype(_BF)

        @pl.when(s == nt - 1)
        def _():
            for j in range(3):
                g_s.forward(j)
            g_s.finish()
            cps = [pltpu.make_async_copy(small_ref.at[k, pl.ds(0, N_META), :], meta.at[:, pl.ds(128 * k, 128)],
                                         msem.at[k]) for k in range(N_DEV)]
            for cp in cps:
                cp.start()
            for cp in cps:
                cp.wait()
            for j in range(3):
                g_w.forward(j)
            g_w.finish()

        has_x = i < nx
        h0_ref[pl.ds(0, N_META), :] = jnp.where(i == 0, meta[...], xp_ref[...])
        h0_ref[pl.ds(N_META, body_rows), :] = jnp.where(has_x, xm_ref[pl.ds(0, body_rows), :], 0.0)
        tgt_ref[pl.ds(0, N_META), :] = jnp.where(i == 0, 0.0, tp_ref[...])
        tgt_ref[pl.ds(N_META, body_rows), :] = jnp.where(has_x, tm_ref[pl.ds(0, body_rows), :], 0.0)

    def tile_of(s):
        return (s + 1) % nt

    hbm = pl.BlockSpec(memory_space=pl.ANY)
    main = pl.BlockSpec((TM, D), lambda s: (jnp.minimum(tile_of(s), nx - 1), 0))
    prev = pl.BlockSpec((N_META, D), lambda s: (jnp.maximum(tile_of(s) * (TM // N_META) - 1, 0), 0))
    tile = pl.BlockSpec((TM, D), lambda s: (tile_of(s), 0))
    return pl.pallas_call(
        body, name="prologue", grid=(nt,),
        in_specs=[main, prev, main, prev, _const(small_l.shape), _const(w_in_l.shape)]
        + [_const(l.shape) for l in cast_f32],
        out_specs=[tile, tile, hbm, hbm] + [_full(l.shape) for l in cast_f32],
        out_shape=[_S((nt * TM, D), _F32), _S((nt * TM, D), _F32), _S((N_DEV,) + small_l.shape, _F32),
                   _S((4, D, WIN_P), _BF)] + [_S(l.shape, _BF) for l in cast_f32],
        scratch_shapes=[pltpu.VMEM(small_l.shape, _F32), pltpu.VMEM(w_in_l.shape, _BF), pltpu.VMEM((N_META, D), _F32),
                        pltpu.SemaphoreType.DMA((N_DEV,))] + _sem_shapes(1) + _sem_shapes(1),
        compiler_params=_cp(("arbitrary",)),
    )(x, x, tgt, tgt, small_l, w_in_l, *cast_f32)


def _adamw_math(w, g, m, v):
    m2 = ADAM_B1 * m + (1.0 - ADAM_B1) * g
    v2 = ADAM_B2 * v + (1.0 - ADAM_B2) * (g * g)
    m_hat = m2 / (1.0 - ADAM_B1 ** ADAM_STEP)
    v_hat = v2 / (1.0 - ADAM_B2 ** ADAM_STEP)
    delta = -ADAM_LR * (m_hat / (jnp.sqrt(v_hat) + ADAM_EPS) + ADAM_WD * w)
    return delta, m2, v2


def _adamw_big(name, recv, w, m, v, rows):
    r_all, c_all = w.shape

    def body(r_ref, w_ref, m_ref, v_ref, g_out, d_out, m_out, v_out):
        g = r_ref[0].astype(_F32)
        for k in range(1, N_DEV):
            g = g + r_ref[k].astype(_F32)
        delta, m2, v2 = _adamw_math(w_ref[...], g, m_ref[...], v_ref[...])
        g_out[...] = g
        d_out[...] = delta
        m_out[...] = m2
        v_out[...] = v2

    tile = pl.BlockSpec((rows, c_all), lambda i: (i, 0))
    return pl.pallas_call(
        body, name=name, grid=(r_all // rows,),
        in_specs=[pl.BlockSpec((N_DEV, rows, c_all), lambda i: (0, i, 0)), tile, tile, tile],
        out_specs=[tile] * 4,
        out_shape=[_S(w.shape, _F32)] * 4,
        compiler_params=_cp(("arbitrary",)),
    )(recv, w, m, v)


def _adamw_small(gathered, slices, wmv):
    ng, npar = len(gathered), len(slices)

    def body(*refs):
        g_refs = refs[:ng]
        wmv_refs = refs[ng:ng + 3 * npar]
        outs = refs[ng + 3 * npar:]
        for i, (ai, r0, nr, c0, ncol) in enumerate(slices):
            g = g_refs[ai][0, pl.ds(r0, nr), pl.ds(c0, ncol)].astype(_F32)
            for k in range(1, N_DEV):
                g = g + g_refs[ai][k, pl.ds(r0, nr), pl.ds(c0, ncol)].astype(_F32)
            w_ref, m_ref, v_ref = wmv_refs[3 * i:3 * i + 3]
            delta, m2, v2 = _adamw_math(w_ref[...], g, m_ref[...], v_ref[...])
            outs[4 * i][...] = g
            outs[4 * i + 1][...] = delta
            outs[4 * i + 2][...] = m2
            outs[4 * i + 3][...] = v2
        total = g_refs[0][0, pl.ds(R_LOSS, 1), pl.ds(0, 128)]
        for k in range(1, N_DEV):
            total = total + g_refs[0][k, pl.ds(R_LOSS, 1), pl.ds(0, 128)]
        outs[4 * npar][...] = total

    flat = [t for trip in wmv for t in trip]
    out_shape = []
    for w, _, _ in wmv:
        out_shape += [_S(w.shape, _F32)] * 4
    out_shape.append(_S((1, 128), _F32))
    return pl.pallas_call(
        body, name="adamw_small", out_shape=out_shape,
        compiler_params=pltpu.CompilerParams(vmem_limit_bytes=VMEM_LIMIT),
    )(*gathered, *flat)


def _block_diag(w):
    eye = jnp.eye(8, dtype=w.dtype)
    return (w[:, :, None, :] * eye[:, None, :, None]).reshape(D_RG, D_RG)


def _diag_blocks(g):
    return jnp.concatenate([g[64 * h:64 * (h + 1), 64 * h:64 * (h + 1)] for h in range(8)], axis=0)


def _local_step(h0, tgt_p, n_valid, g_mix, w_in, vec, wr, wi, hb, g_hg, w_out_l, g_ffn, w_gu_l, w_down_l, g_fin):
    t_pad = h0.shape[0]
    me = 4 * lax.axis_index("x") + 2 * lax.axis_index("y") + lax.axis_index("c")
    p, u, y, hs, o, sc, w_out, w_gu, w_down = _mixer_fwd(h0, g_mix, w_in, wr, wi, vec, hb, g_hg,
                                                         [w_out_l, w_gu_l, w_down_l])
    w_out = w_out.reshape(D, D)
    w_down = w_down.reshape(4, FFB, D)
    h1, v, gu, act, dh2, dh2b, loss, gfin = _ffn_loss(h0, y, w_out, g_ffn, w_gu, w_down, g_fin, tgt_p, n_valid)

    dgu, dh1, dh1b, dy, gffn = _ffn_bwd(dh2, dh2b, gu, h1, g_ffn, w_gu, w_down, w_out)
    g_wdown = _wgrad("wgrad_down", act, dh2b, pl.BlockSpec((1, t_pad, FFB), lambda j: (j, 0, 0)),
                     pl.BlockSpec((t_pad, D), lambda j: (0, 0)), 4, (FFB, D))
    g_wgu, r_wdown = _wgrad("wgrad_gate_up", dgu, v, pl.BlockSpec((1, t_pad, FFB), lambda j: (j, 0, 0)),
                            pl.BlockSpec((t_pad, D), lambda j: (0, 0)), N_DEV, (FFB, D),
                            scatter=[g_wdown.reshape(N_DEV, D_FF // N_DEV, D)])
    g_wout = _wgrad("wgrad_out", y, dh1b, pl.BlockSpec((t_pad, D // N_DEV), lambda j: (0, j)),
                    pl.BlockSpec((t_pad, D), lambda j: (0, 0)), N_DEV, (D // N_DEV, D))
    dp, gvec, gw, r_wgu, r_wout = _mixer_bwd(p, hs, o, sc, dy, wr, wi, vec, hb, g_hg, [g_wgu, g_wout])
    pack_c = jnp.concatenate([_diag_blocks(gw[0]), _diag_blocks(gw[1])], axis=1).astype(_BF)
    order = (me ^ jnp.array(_SEND_ORDER, jnp.int32)).astype(jnp.int32)
    dh0, r_win, all_b, all_c, all_a = _inproj_bwd_send(dp, w_in, h0, dh1, g_mix, u, order, gffn, gfin, loss,
                                                       [gvec, pack_c])
    return dh0, (r_win, r_wgu, r_wout, r_wdown), (all_a, all_b, all_c)


def kernel(x, meta_tokens, mix_norm_g, w_in, conv_w, conv_b, w_rgate, b_rgate, w_igate, b_igate, lru_lambda, rg_norm_g, hg_lower_bound, hg_norm_g, w_out, ffn_norm_g, w_gate_up, w_down, final_norm_g, loss_target, m_meta_tokens, m_mix_norm_g, m_w_in, m_conv_w, m_conv_b, m_w_rgate, m_b_rgate, m_w_igate, m_b_igate, m_lru_lambda, m_rg_norm_g, m_hg_lower_bound, m_hg_norm_g, m_w_out, m_ffn_norm_g, m_w_gate_up, m_w_down, m_final_norm_g, v_meta_tokens, v_mix_norm_g, v_w_in, v_conv_w, v_conv_b, v_w_rgate, v_b_rgate, v_w_igate, v_b_igate, v_lru_lambda, v_rg_norm_g, v_hg_lower_bound, v_hg_norm_g, v_w_out, v_ffn_norm_g, v_w_gate_up, v_w_down, v_final_norm_g):
    seq = x.shape[1]
    me = 4 * lax.axis_index("x") + 2 * lax.axis_index("y") + lax.axis_index("c")

    n_valid = N_META + seq
    small_l = jnp.concatenate([meta_tokens, jnp.pad(conv_w[0], ((0, 4), (0, 64)))], axis=0)
    h0, tgt_p, small_g, w_in_g, w_gu_l, w_out_l, w_down_l = _prologue(
        x[0], loss_target[0], small_l, w_in[0], [w_gate_up[0].T, w_out[0], w_down[0]])
    conv_w_full = jnp.transpose(small_g[:, N_META:N_META + 4, :64], (1, 0, 2)).reshape(4, D_RG)
    vec = jnp.concatenate([conv_b, b_rgate, b_igate, lru_lambda, rg_norm_g, jnp.zeros((3, D_RG), _F32),
                           conv_w_full, jnp.zeros((4, D_RG), _F32)], axis=0)
    wr = _block_diag(w_rgate[0]).astype(_BF)
    wi = _block_diag(w_igate[0]).astype(_BF)

    dh0, (r_win, r_wgu, r_wout, r_wdown), (all_a, all_b, all_c) = _local_step(
        h0, tgt_p, n_valid, mix_norm_g, w_in_g, vec, wr, wi, hg_lower_bound, hg_norm_g,
        w_out_l, ffn_norm_g, w_gu_l, w_down_l, final_norm_g.reshape(1, D))
    grad_x = dh0[N_META:N_META + seq][None]

    outs = {}
    outs["w_in"] = _adamw_big("adamw_w_in", r_win, w_in[0], m_w_in[0], v_w_in[0], 256)
    outs["w_gate_up"] = [r.T for r in _adamw_big("adamw_w_gate_up", r_wgu, w_gate_up[0].T, m_w_gate_up[0].T,
                                                 v_w_gate_up[0].T, 176)]
    outs["w_out"] = _adamw_big("adamw_w_out", r_wout, w_out[0], m_w_out[0], v_w_out[0], 128)
    outs["w_down"] = _adamw_big("adamw_w_down", r_wdown, w_down[0], m_w_down[0], v_w_down[0], 176)

    meta_part = lax.dynamic_slice_in_dim(all_a[:, R_META:R_META + N_META, :], me * 128, 128, axis=2)
    convw_part = lax.dynamic_slice_in_dim(all_b[:, R_CONVW:R_CONVW + 4, :], me * 64, 64, axis=2)
    gathered = [all_a, all_b, all_c, meta_part, convw_part]
    small_params = [
        ("meta_tokens", (3, 0, N_META, 0, 128), (meta_tokens, m_meta_tokens, v_meta_tokens), (N_META, 128)),
        ("mix_norm_g", (0, R_GMIX, 1, 0, D), (mix_norm_g, m_mix_norm_g, v_mix_norm_g), (1, D)),
        ("conv_w", (4, 0, 4, 0, 64), (conv_w, m_conv_w, v_conv_w), (4, 64)),
        ("conv_b", (1, R_CONVB, 1, 0, D_RG), (conv_b, m_conv_b, v_conv_b), (1, D_RG)),
        ("w_rgate", (2, 0, 512, 0, 64), (w_rgate, m_w_rgate, v_w_rgate), (512, 64)),
        ("b_rgate", (1, R_BR, 1, 0, D_RG), (b_rgate, m_b_rgate, v_b_rgate), (1, D_RG)),
        ("w_igate", (2, 0, 512, 64, 64), (w_igate, m_w_igate, v_w_igate), (512, 64)),
        ("b_igate", (1, R_BI, 1, 0, D_RG), (b_igate, m_b_igate, v_b_igate), (1, D_RG)),
        ("lru_lambda", (1, R_LAM, 1, 0, D_RG), (lru_lambda, m_lru_lambda, v_lru_lambda), (1, D_RG)),
        ("rg_norm_g", (1, R_GRG, 1, 0, D_RG), (rg_norm_g, m_rg_norm_g, v_rg_norm_g), (1, D_RG)),
        ("hg_lower_bound", (1, R_HB0, 2, 0, D_HG), (hg_lower_bound, m_hg_lower_bound, v_hg_lower_bound), (2, D_HG)),
        ("hg_norm_g", (1, R_GHG, 1, 0, HD), (hg_norm_g, m_hg_norm_g, v_hg_norm_g), (1, HD)),
        ("ffn_norm_g", (0, R_GFFN, 1, 0, D), (ffn_norm_g, m_ffn_norm_g, v_ffn_norm_g), (1, D)),
        ("final_norm_g", (0, R_GFIN, 1, 0, D), (final_norm_g, m_final_norm_g, v_final_norm_g), (1, D)),
    ]
    res = _adamw_small(gathered, [s[1] for s in small_params],
                       [tuple(t.reshape(s[3]) for t in s[2]) for s in small_params])
    for i, s in enumerate(small_params):
        outs[s[0]] = [r.reshape(s[2][0].shape) for r in res[4 * i:4 * i + 4]]
    for n, ref in (("w_in", w_in), ("w_gate_up", w_gate_up), ("w_out", w_out), ("w_down", w_down)):
        outs[n] = [r.reshape(ref.shape) for r in outs[n]]

    loss_all = res[4 * len(small_params)][0, 0]
    order = ["meta_tokens", "mix_norm_g", "w_in", "conv_w", "conv_b", "w_rgate", "b_rgate", "w_igate", "b_igate",
             "lru_lambda", "rg_norm_g", "hg_lower_bound", "hg_norm_g", "w_out", "ffn_norm_g", "w_gate_up", "w_down",
             "final_norm_g"]
    return (loss_all, grad_x, *[outs[n][0] for n in order], *[outs[n][1] for n in order],
            *[outs[n][2] for n in order], *[outs[n][3] for n in order])
```

```python
import functools

import jax
import jax.numpy as jnp
from jax import lax
from jax.experimental import pallas as pl
from jax.experimental.pallas import tpu as pltpu

_BF = jnp.bfloat16
_F32 = jnp.float32
_S = jax.ShapeDtypeStruct
_MESH = pl.DeviceIdType.MESH

N_DEV = 8
N_META = 16
D = 1024
D_RG = 512
D_HG = 512
HD = 128
NH = D_HG // HD
D_IN = 3072
D_FF = 2816
FFB = D_FF // 4
WIN_B = D_IN // N_DEV
WIN_P = 2 * WIN_B
EPS = 1e-6
LRU_C = 8.0
TM = 320
HC = 64
VMEM_LIMIT = 62 * 1024 * 1024

ADAM_LR = 0.001
ADAM_B1 = 0.9
ADAM_B2 = 0.999
ADAM_EPS = 1e-08
ADAM_WD = 0.01
ADAM_STEP = 10

_SEND_ORDER = (6, 4, 2, 7, 5, 3, 1, 0)

R_CONVB, R_BR, R_BI, R_LAM, R_GRG, R_HB0, R_HB1, R_GHG, R_CONVW = 0, 1, 2, 3, 4, 5, 6, 7, 8
R_GMIX, R_GFFN, R_GFIN, R_LOSS, R_META = 0, 1, 2, 3, 8


def _cp(sem=None, **kw):
    return pltpu.CompilerParams(dimension_semantics=sem, vmem_limit_bytes=VMEM_LIMIT, **kw)


def _dot(a, b):
    return jnp.dot(a, b, preferred_element_type=_F32)


def _dot_nt(a, b):
    return lax.dot_general(a, b, (((1,), (1,)), ((), ())), preferred_element_type=_F32)


def _dot_tn(a, b):
    return lax.dot_general(a, b, (((0,), (0,)), ((), ())), preferred_element_type=_F32)


def _sigmoid(x):
    return 0.5 * jnp.tanh(0.5 * x) + 0.5


def _dsilu(x, s):
    return s * (1.0 + x * (1.0 - s))


_GELU_C = 0.7978845608028654


def _gelu_parts(x):
    t = jnp.tanh(_GELU_C * (x + 0.044715 * (x * x * x)))
    g = 0.5 * x * (1.0 + t)
    dg = 0.5 * (1.0 + t) + 0.5 * x * (1.0 - t * t) * (_GELU_C * (1.0 + 3.0 * 0.044715 * (x * x)))
    return g, dg


def _softplus(z):
    e = jnp.exp(-jnp.abs(z))
    w = 1.0 + e
    l1p = jnp.where(w == 1.0, e, jnp.log(w) * e / jnp.where(w == 1.0, 1.0, w - 1.0))
    return jnp.maximum(z, 0.0) + l1p


def _rms_fwd(x):
    r = lax.rsqrt(jnp.mean(x * x, axis=-1, keepdims=True) + EPS)
    return x * r, r


def _rms_bwd(dyg, n, r):
    return r * (dyg - n * jnp.mean(dyg * n, axis=-1, keepdims=True))


def _full(shape):
    nd = len(shape)
    return pl.BlockSpec(shape, lambda i: (0,) * nd)


def _const(shape):
    nd = len(shape)
    return pl.BlockSpec(shape, lambda i: (0,) * nd, pipeline_mode=pl.Buffered(1))


def _carry_gather(gather, i, nt):
    @pl.when(i == 0)
    def _():
        gather.start()

    def tail():
        for j in range(3):
            @pl.when(i == max(nt - 4 + j, 0))
            def _(j=j):
                gather.forward(j)

        @pl.when(i == nt - 1)
        def _():
            gather.finish()

    return tail


def _pair_place(ref, block):
    return ref.at[block // 2, :, pl.ds(pl.multiple_of((block % 2) * WIN_B, WIN_B), WIN_B)]


def _rg_gates(xc, wr_ref, wi_ref, vec_ref):
    xcb = xc.astype(_BF)
    r = _sigmoid(_dot(xcb, wr_ref[...]) + vec_ref[R_BR:R_BR + 1, :])
    ig = _sigmoid(_dot(xcb, wi_ref[...]) + vec_ref[R_BI:R_BI + 1, :])
    nsp8 = -LRU_C * _softplus(-vec_ref[R_LAM:R_LAM + 1, :])
    la = nsp8 * r
    a = jnp.exp(la)
    th = jnp.tanh(la)
    s = jnp.sqrt(-2.0 * th / (1.0 - th))
    return r, ig, a, s, nsp8


def _conv(xbuf, vec_ref):
    acc = vec_ref[R_CONVW:R_CONVW + 1, :] * xbuf[pl.ds(5, TM), :]
    for j in range(1, 4):
        acc = acc + vec_ref[R_CONVW + j:R_CONVW + j + 1, :] * xbuf[pl.ds(5 + j, TM), :]
    return vec_ref[R_CONVB:R_CONVB + 1, :] + acc


def _dot3(m01, x):
    hi = x.astype(_BF)
    r1 = x - hi.astype(_F32)
    mid = r1.astype(_BF)
    lo = (r1 - mid.astype(_F32)).astype(_BF)
    return (_dot(m01, lo) + _dot(m01, mid)) + _dot(m01, hi)


def _chunk_masks():
    row = lax.broadcasted_iota(jnp.int32, (TM, TM), 0)
    col = lax.broadcasted_iota(jnp.int32, (TM, TM), 1)
    shift = HC.bit_length() - 1
    same = lax.shift_right_logical(row, shift) == lax.shift_right_logical(col, shift)
    return same, same & (row >= col), same & (col >= row)


def _per_chunk_rows(x, r):
    return jnp.concatenate([jnp.broadcast_to(x[HC * c + r:HC * c + r + 1, :], (HC, x.shape[1]))
                            for c in range(TM // HC)], axis=0)


def _hg_prep(p_ref, lb, tri_blk):
    hq = p_ref[:, pl.ds(2 * D_RG, D_HG)]
    hf = p_ref[:, pl.ds(2 * D_RG + D_HG, D_HG)]
    sq = _sigmoid(hq)
    q = hq * sq
    sg = _sigmoid(hf)
    f = lb + (1.0 - lb) * sg
    k = 1.0 - f
    b = _dot3(tri_blk, jnp.log(f))
    bm = _per_chunk_rows(b, HC // 2 - 1)
    bl = _per_chunk_rows(b, HC - 1)
    e_q = jnp.exp(b - bm)
    e_k = jnp.exp(bm - b)
    e_b = jnp.exp(b)
    e_l = jnp.exp(bl - b)
    return dict(hq=hq, sq=sq, q=q, sg=sg, f=f, k=k, e_q=e_q, e_k=e_k, e_b=e_b, e_l=e_l,
                qd=q * e_q, kd=k * e_k, qe=q * e_b, ke=k * e_l, e_end=jnp.exp(bl))


def _mixer_fwd(h0, g_mix, w_in, wr, wi, vec, hb, g_hg, shards):
    t_pad = h0.shape[0]
    nt = t_pad // TM
    nc_t = TM // HC
    nsh = len(shards)

    def body(h_ref, gmix_ref, win_ref, wr_ref, wi_ref, vec_ref, hb_ref, ghg_ref, *rest):
        sh_refs, rest = rest[:nsh], rest[nsh:]
        pout_ref, uout_ref, y_ref, hs_ref, o_ref, sc_ref = rest[:6]
        gath_refs, rest = rest[6:6 + nsh], rest[6 + nsh:]
        xbuf, a_s, b_s, hcar, st, qd_s, kd_s, qe_s, ke_s, v_s, u_s, p_s, p_ref = rest[:13]
        i = pl.program_id(0)
        tail = _carry_gather(_Gather(sh_refs, gath_refs, rest[13:]), i, nt + 1)

        @pl.when(i == 0)
        def _():
            p_s[...] = jnp.zeros_like(p_s)

        p_ref[...] = p_s[...]

        @pl.when(i <= 1)
        def _():
            xbuf[pl.ds(0, 8), :] = jnp.zeros((8, D_RG), _F32)
            hcar[...] = jnp.zeros_like(hcar)
            st[...] = jnp.zeros_like(st)

        n_h, _ = _rms_fwd(h_ref[...])
        u = (n_h * gmix_ref[...]).astype(_BF)
        uout_ref[...] = u
        pieces = [(j, k) for j in range(4) for k in range(WIN_P // 256)]

        def project(count):
            for _ in range(count):
                j, k = pieces.pop(0)
                blk = _dot(u, win_ref[j, :, pl.ds(256 * k, 256)])
                p_s[:, pl.ds(WIN_P * j + 256 * k, 256)] = blk
                pout_ref[:, pl.ds(WIN_P * j + 256 * k, 256)] = blk

        x = p_ref[:, pl.ds(0, D_RG)]
        xbuf[pl.ds(8, TM), :] = x
        xc = _conv(xbuf, vec_ref)
        xbuf[pl.ds(0, 8), :] = x[TM - 8:, :]
        r, ig, a, s, _ = _rg_gates(xc, wr_ref, wi_ref, vec_ref)
        a_s[...] = a
        b_s[...] = s * (ig * xc)

        def step(t, h):
            h = a_s[pl.ds(t, 1), :] * h + b_s[pl.ds(t, 1), :]
            hs_ref[pl.ds(t, 1), :] = h
            return h

        hcar[pl.ds(0, 1), :] = lax.fori_loop(0, TM, step, hcar[pl.ds(0, 1), :], unroll=8)
        gel, _ = _gelu_parts(p_ref[:, pl.ds(D_RG, D_RG)])
        n, _ = _rms_fwd(gel * hs_ref[...])
        y_ref[:, pl.ds(0, D_RG)] = (n * vec_ref[R_GRG:R_GRG + 1, :]).astype(_BF)

        lb = _sigmoid(hb_ref[0:1, :] - hb_ref[1:2, :])
        _, tri_blk, _ = _chunk_masks()
        q = _hg_prep(p_ref, lb, tri_blk.astype(_BF))
        for name, ref in (("qd", qd_s), ("kd", kd_s), ("qe", qe_s), ("ke", ke_s)):
            ref[...] = q[name].astype(_BF)
        v_s[...] = p_ref[:, pl.ds(2 * D_RG + 2 * D_HG, D_HG)].astype(_BF)
        e_end = q["e_end"]
        causal = (lax.broadcasted_iota(jnp.int32, (HC, HC), 0) >= lax.broadcasted_iota(jnp.int32, (HC, HC), 1))
        for c in range(nc_t):
            for h in range(NH):
                rs, cs = pl.ds(HC * c, HC), pl.ds(HD * h, HD)
                amat = jnp.where(causal, _dot_nt(qd_s[rs, cs], kd_s[rs, cs]), 0.0)
                o_ref[rs, cs] = _dot(amat.astype(_BF), v_s[rs, cs])
                u_s[NH * c + h] = _dot_tn(v_s[rs, cs], ke_s[rs, cs])
                if pieces:
                    project(1)
        assert not pieces
        for h in range(NH):
            cs = pl.ds(HD * h, HD)
            s_run = st[h]
            for c in range(nc_t):
                rs = pl.ds(HC * c, HC)
                sc_ref[c, h] = s_run
                o_ref[rs, cs] += _dot_nt(qe_s[rs, cs], s_run.astype(_BF))
                s_run = e_end[HC * c:HC * c + 1, HD * h:HD * (h + 1)] * s_run + u_s[NH * c + h]
            st[h] = s_run
        for h in range(NH):
            cs = pl.ds(HD * h, HD)
            n_o, _ = _rms_fwd(o_ref[:, cs])
            hg = p_ref[:, pl.ds(2 * D_RG + 3 * D_HG + HD * h, HD)]
            y_ref[:, pl.ds(D_RG + HD * h, HD)] = ((n_o * ghg_ref[...]) * (hg * _sigmoid(hg))).astype(_BF)

        tail()

    hbm = pl.BlockSpec(memory_space=pl.ANY)

    def proj(i):
        return jnp.minimum(i, nt - 1)

    def mixed(i):
        return jnp.maximum(i - 1, 0)

    return pl.pallas_call(
        body, name="mixer_fwd", grid=(nt + 1,),
        in_specs=[pl.BlockSpec((TM, D), lambda i: (proj(i), 0)), _full((1, D)), _const((4, D, WIN_P)),
                  _full((D_RG, D_RG)), _full((D_RG, D_RG)),
                  _full((16, D_RG)), _full((2, D_HG)), _full((1, HD))] + [hbm] * nsh,
        out_specs=[pl.BlockSpec((TM, D_IN), lambda i: (proj(i), 0)), pl.BlockSpec((TM, D), lambda i: (proj(i), 0)),
                   pl.BlockSpec((TM, D), lambda i: (mixed(i), 0)), pl.BlockSpec((TM, D_RG), lambda i: (mixed(i), 0)),
                   pl.BlockSpec((TM, D_HG), lambda i: (mixed(i), 0)),
                   pl.BlockSpec((nc_t, NH, HD, HD), lambda i: (mixed(i), 0, 0, 0))] + [hbm] * nsh,
        out_shape=[_S((t_pad, D_IN), _F32), _S((t_pad, D), _BF),
                   _S((t_pad, D), _BF), _S((t_pad, D_RG), _F32), _S((t_pad, D_HG), _F32),
                   _S((t_pad // HC, NH, HD, HD), _F32)] + [_S((N_DEV,) + s.shape, s.dtype) for s in shards],
        scratch_shapes=[pltpu.VMEM((TM + 8, D_RG), _F32), pltpu.VMEM((TM, D_RG), _F32),
                        pltpu.VMEM((TM, D_RG), _F32), pltpu.VMEM((8, D_RG), _F32),
                        pltpu.VMEM((NH, HD, HD), _F32)] + [pltpu.VMEM((TM, D_HG), _BF) for _ in range(5)]
        + [pltpu.VMEM((nc_t * NH, HD, HD), _F32), pltpu.VMEM((TM, D_IN), _F32), pltpu.VMEM((TM, D_IN), _F32)]
        + _sem_shapes(nsh),
        compiler_params=_cp(("arbitrary",)),
    )(h0, g_mix, w_in, wr, wi, vec, hb, g_hg, *shards)


def _ffn_loss(h0, y, w_out, g_ffn, w_gu, w_down, g_fin, tgt, n_valid):
    t_pad = h0.shape[0]

    def body(h_ref, y_ref, wo_ref, gffn_ref, wgu_ref, wd_ref, g_ref, t_ref,
             h1_ref, v_ref, gu_ref, act_ref, dh2_ref, dh2b_ref, loss_ref, gfin_ref):
        i = pl.program_id(0)

        @pl.when(i == 0)
        def _():
            loss_ref[...] = jnp.zeros_like(loss_ref)
            gfin_ref[...] = jnp.zeros_like(gfin_ref)

        h1 = h_ref[...] + _dot(y_ref[...], wo_ref[...])
        h1_ref[...] = h1
        n1, _ = _rms_fwd(h1)
        vb = (n1 * gffn_ref[...]).astype(_BF)
        v_ref[...] = vb
        h2 = h1
        for b in range(4):
            gate = _dot_nt(vb, wgu_ref[b])
            up = _dot_nt(vb, wgu_ref[4 + b])
            gu_ref[b] = gate
            gu_ref[4 + b] = up
            act = ((gate * _sigmoid(gate)) * up).astype(_BF)
            act_ref[b] = act
            h2 = h2 + _dot(act, wd_ref[b])
        n, r = _rms_fwd(h2)
        out = n * g_ref[...]
        row = i * TM + lax.broadcasted_iota(jnp.int32, (TM, 1), 0)
        valid = (row >= N_META) & (row < n_valid)
        err = jnp.where(valid, out - t_ref[...], 0.0)
        loss_ref[...] += (0.5 / D) * jnp.sum(err * err)
        dout = err * (1.0 / D)
        gfin_ref[...] += jnp.sum(dout * n, axis=0, keepdims=True)
        dh2 = _rms_bwd(dout * g_ref[...], n, r)
        dh2_ref[...] = dh2
        dh2b_ref[...] = dh2.astype(_BF)

    tile = pl.BlockSpec((TM, D), lambda i: (i, 0))
    return pl.pallas_call(
        body, name="ffn_loss", grid=(t_pad // TM,),
        in_specs=[tile, tile, _const((D, D)), _full((1, D)),
                  _const((N_DEV, FFB, D)), _const((4, FFB, D)), _full((1, D)), tile],
        out_specs=[tile, tile,
                   pl.BlockSpec((N_DEV, TM, FFB), lambda i: (0, i, 0)), pl.BlockSpec((4, TM, FFB), lambda i: (0, i, 0)),
                   tile, tile, _full((8, 128)), _full((1, D))],
        out_shape=[_S((t_pad, D), _F32), _S((t_pad, D), _BF),
                   _S((N_DEV, t_pad, FFB), _F32), _S((4, t_pad, FFB), _BF), _S((t_pad, D), _F32),
                   _S((t_pad, D), _BF), _S((8, 128), _F32), _S((1, D), _F32)],
        compiler_params=_cp(("arbitrary",)),
    )(h0, y, w_out, g_ffn, w_gu, w_down, g_fin, tgt)


def _ffn_bwd(dh2, dh2b, gu, h1, g_ffn, w_gu, w_down, w_out):
    t_pad = dh2.shape[0]

    def body(dh2_ref, dh2b_ref, gu_ref, h1_ref, g_ref, wgu_ref, wd_ref, wo_ref,
             dgu_ref, dh1_ref, dh1b_ref, dy_ref, gffn_ref):
        i = pl.program_id(0)

        @pl.when(i == 0)
        def _():
            gffn_ref[...] = jnp.zeros_like(gffn_ref)

        db = dh2b_ref[...]
        dv = jnp.zeros((TM, D), _F32)
        for b in range(4):
            dact = _dot_nt(db, wd_ref[b])
            gate = gu_ref[b]
            up = gu_ref[4 + b]
            sg = _sigmoid(gate)
            dgate = ((dact * up) * _dsilu(gate, sg)).astype(_BF)
            dup = (dact * (gate * sg)).astype(_BF)
            dgu_ref[b] = dgate
            dgu_ref[4 + b] = dup
            dv = dv + _dot(dgate, wgu_ref[b]) + _dot(dup, wgu_ref[4 + b])
        n, r = _rms_fwd(h1_ref[...])
        gffn_ref[...] += jnp.sum(dv * n, axis=0, keepdims=True)
        dh1 = dh2_ref[...] + _rms_bwd(dv * g_ref[...], n, r)
        dh1_ref[...] = dh1
        dh1b = dh1.astype(_BF)
        dh1b_ref[...] = dh1b
        dy_ref[...] = _dot_nt(dh1b, wo_ref[...])

    tile = pl.BlockSpec((TM, D), lambda i: (i, 0))
    return pl.pallas_call(
        body, name="ffn_bwd", grid=(t_pad // TM,),
        in_specs=[tile, tile, pl.BlockSpec((N_DEV, TM, FFB), lambda i: (0, i, 0)), tile, _full((1, D)),
                  _const((N_DEV, FFB, D)), _const((4, FFB, D)), _const((D, D))],
        out_specs=[pl.BlockSpec((N_DEV, TM, FFB), lambda i: (0, i, 0)), tile, tile, tile, _full((1, D))],
        out_shape=[_S((N_DEV, t_pad, FFB), _BF), _S((t_pad, D), _F32), _S((t_pad, D), _BF),
                   _S((t_pad, D), _F32), _S((1, D), _F32)],
        compiler_params=_cp(("arbitrary",)),
    )(dh2, dh2b, gu, h1, g_ffn, w_gu, w_down, w_out)


def _mixer_bwd(p, hs, o, sc, dy, wr, wi, vec, hb, g_hg, scatter):
    t_pad = p.shape[0]
    nt = t_pad // TM
    nc_t = TM // HC
    nsc = len(scatter)

    def rev(i):
        return nt - 1 - i

    def body(p_ref, pprev_ref, hs_ref, hprev_ref, o_ref, sc_ref, dy_ref, wr_ref, wi_ref, vec_ref, hb_ref, ghg_ref,
             *rest):
        send_refs, rest = rest[:nsc], rest[nsc:]
        dp_ref, gvec_ref, gw_ref = rest[:3]
        recv_refs, rest = rest[3:3 + nsc], rest[3 + nsc:]
        xbuf, hbuf, dbuf, a_s, g_s, ccar, dst = rest[:7]
        qd_s, kd_s, qe_s, ke_s, v_s, do_s, dqd_s, dkd_s, dqe_s, dke_s, dv_s, w_s, dend_s = rest[7:20]
        exchange = _Exchange(send_refs, [], recv_refs, rest[20:])
        i = pl.program_id(0)
        first_tile = i == nt - 1

        @pl.when(i == 0)
        def _():
            exchange.start()
            gvec_ref[...] = jnp.zeros_like(gvec_ref)
            gw_ref[...] = jnp.zeros_like(gw_ref)
            dbuf[pl.ds(TM, 8), :] = jnp.zeros((8, D_RG), _F32)
            ccar[...] = jnp.zeros_like(ccar)
            dst[...] = jnp.zeros_like(dst)

        def acc(row, val):
            gvec_ref[row:row + 1, :] += jnp.sum(val, axis=0, keepdims=True)

        keep = jnp.where(first_tile, 0.0, 1.0)
        x = p_ref[:, pl.ds(0, D_RG)]
        xbuf[pl.ds(0, 8), :] = pprev_ref[...] * keep
        xbuf[pl.ds(8, TM), :] = x
        xc = _conv(xbuf, vec_ref)
        r, ig, a, s, nsp8 = _rg_gates(xc, wr_ref, wi_ref, vec_ref)
        h = hs_ref[...]
        hbuf[pl.ds(0, 8), :] = hprev_ref[...] * keep
        hbuf[pl.ds(8, TM), :] = h
        hm1 = hbuf[pl.ds(7, TM), :]
        gr = p_ref[:, pl.ds(D_RG, D_RG)]
        gel, dgel = _gelu_parts(gr)
        n, rr = _rms_fwd(gel * h)
        dyn = dy_ref[:, pl.ds(0, D_RG)]
        acc(R_GRG, dyn * n)
        dpre = _rms_bwd(dyn * vec_ref[R_GRG:R_GRG + 1, :], n, rr)
        dp_ref[:, pl.ds(D_RG, D_RG)] = ((dpre * h) * dgel).astype(_BF)
        a_s[...] = a
        g_s[...] = dpre * gel

        def step(k, c):
            t = TM - 1 - k
            g = g_s[pl.ds(t, 1), :] + c
            g_s[pl.ds(t, 1), :] = g
            return a_s[pl.ds(t, 1), :] * g

        ccar[pl.ds(0, 1), :] = lax.fori_loop(0, TM, step, ccar[pl.ds(0, 1), :], unroll=8)
        gt = g_s[...]
        da = gt * hm1
        ixc = ig * xc
        ds = gt * ixc
        dig = (gt * s) * xc
        dxc = (gt * s) * ig
        dla = da * a - ds * ((a * a) / s)
        lam = vec_ref[R_LAM:R_LAM + 1, :]
        gvec_ref[R_LAM:R_LAM + 1, :] += jnp.sum(dla * r, axis=0, keepdims=True) * (LRU_C * _sigmoid(-lam))
        dzr = (dla * nsp8) * (r * (1.0 - r))
        dzi = dig * (ig * (1.0 - ig))
        acc(R_BR, dzr)
        acc(R_BI, dzi)
        xcb = xc.astype(_BF)
        dzrb = dzr.astype(_BF)
        dzib = dzi.astype(_BF)
        gw_ref[0] += _dot_tn(xcb, dzrb)
        gw_ref[1] += _dot_tn(xcb, dzib)
        dxc = dxc + _dot_nt(dzrb, wr_ref[...]) + _dot_nt(dzib, wi_ref[...])
        acc(R_CONVB, dxc)
        for j in range(4):
            acc(R_CONVW + j, dxc * xbuf[pl.ds(5 + j, TM), :])
        dbuf[pl.ds(0, TM), :] = dxc
        dx = vec_ref[R_CONVW + 3:R_CONVW + 4, :] * dxc
        for j in range(3):
            dx = dx + vec_ref[R_CONVW + j:R_CONVW + j + 1, :] * dbuf[pl.ds(3 - j, TM), :]
        dbuf[pl.ds(TM, 8), :] = dxc[0:8, :]
        dp_ref[:, pl.ds(0, D_RG)] = dx.astype(_BF)

        lb = _sigmoid(hb_ref[0:1, :] - hb_ref[1:2, :])
        same, tri_blk, triu_blk = _chunk_masks()
        q = _hg_prep(p_ref, lb, tri_blk.astype(_BF))
        qdb, kdb = q["qd"].astype(_BF), q["kd"].astype(_BF)
        qd_s[...] = qdb
        kd_s[...] = kdb
        qe_s[...] = q["qe"].astype(_BF)
        ke_s[...] = q["ke"].astype(_BF)
        v_s[...] = p_ref[:, pl.ds(2 * D_RG + 2 * D_HG, D_HG)].astype(_BF)
        e_end = q["e_end"]
        ghg = ghg_ref[...]
        for h in range(NH):
            cs = pl.ds(HD * h, HD)
            hg = p_ref[:, pl.ds(2 * D_RG + 3 * D_HG + HD * h, HD)]
            sh = _sigmoid(hg)
            n_o, r_o = _rms_fwd(o_ref[:, cs])
            dyh = dy_ref[:, pl.ds(D_RG + HD * h, HD)]
            dp_ref[:, pl.ds(2 * D_RG + 3 * D_HG + HD * h, HD)] = ((dyh * (n_o * ghg)) * _dsilu(hg, sh)).astype(_BF)
            dn = dyh * (hg * sh)
            gvec_ref[R_GHG:R_GHG + 1, pl.ds(0, HD)] += jnp.sum(dn * n_o, axis=0, keepdims=True)
            do_s[:, cs] = _rms_bwd(dn * ghg, n_o, r_o).astype(_BF)
        causal = (lax.broadcasted_iota(jnp.int32, (HC, HC), 0) >= lax.broadcasted_iota(jnp.int32, (HC, HC), 1))
        for c in range(nc_t):
            for h in range(NH):
                rs, cs = pl.ds(HC * c, HC), pl.ds(HD * h, HD)
                qd_c, kd_c, do_c = qd_s[rs, cs], kd_s[rs, cs], do_s[rs, cs]
                amat = jnp.where(causal, _dot_nt(qd_c, kd_c), 0.0).astype(_BF)
                da_m = jnp.where(causal, _dot_nt(do_c, v_s[rs, cs]), 0.0).astype(_BF)
                dqd_s[rs, cs] = _dot(da_m, kd_c)
                dkd_s[rs, cs] = _dot_tn(da_m, qd_c)
                dqe_s[rs, cs] = _dot(do_c, sc_ref[c, h].astype(_BF))
                dv_s[rs, cs] = _dot_tn(amat, do_c)
                w_s[NH * c + h] = _dot_tn(do_c, qe_s[rs, cs])
        for h in range(NH):
            cs = pl.ds(HD * h, HD)
            d_run = dst[h]
            for c in reversed(range(nc_t)):
                rs = pl.ds(HC * c, HC)
                d_b = d_run.astype(_BF)
                dke_s[rs, cs] = _dot(v_s[rs, cs], d_b)
                dp_ref[rs, pl.ds(2 * D_RG + 2 * D_HG + HD * h, HD)] = (
                    dv_s[rs, cs] + _dot_nt(ke_s[rs, cs], d_b)).astype(_BF)
                dend_s[pl.ds(c, 1), cs] = jnp.sum(sc_ref[c, h] * d_run, axis=0, keepdims=True)
                d_run = w_s[NH * c + h] + e_end[HC * c:HC * c + 1, HD * h:HD * (h + 1)] * d_run
            dst[h] = d_run
        dqd, dkd, dqe, dke = dqd_s[...], dkd_s[...], dqe_s[...], dke_s[...]
        dq = dqd * q["e_q"] + dqe * q["e_b"]
        dk = dkd * q["e_k"] + dke * q["e_l"]
        dkeke = dke * q["ke"]
        db = dqd * qdb.astype(_F32) - dkd * kdb.astype(_F32) + dqe * q["qe"] - dkeke
        d_end = jnp.concatenate([jnp.broadcast_to(dend_s[pl.ds(c, 1), :], (HC, D_HG)) for c in range(nc_t)], axis=0)
        dlf = _dot3(triu_blk.astype(_BF), db) + _dot3(same.astype(_BF), dkeke) + d_end * e_end
        df = dlf / q["f"] - dk
        sg = q["sg"]
        gvec_ref[R_HB0:R_HB0 + 1, :] += jnp.sum(df * (1.0 - sg), axis=0, keepdims=True)
        dp_ref[:, pl.ds(2 * D_RG, D_HG)] = (dq * _dsilu(q["hq"], q["sq"])).astype(_BF)
        dp_ref[:, pl.ds(2 * D_RG + D_HG, D_HG)] = ((df * (1.0 - lb)) * (sg * (1.0 - sg))).astype(_BF)

        @pl.when(i == nt - 1)
        def _():
            glb = gvec_ref[R_HB0:R_HB0 + 1, :] * (lb * (1.0 - lb))
            gvec_ref[R_HB0:R_HB0 + 1, :] = glb
            gvec_ref[R_HB1:R_HB1 + 1, :] = -glb
            exchange.finish()

    hbm = pl.BlockSpec(memory_space=pl.ANY)
    return pl.pallas_call(
        body, name="mixer_bwd", grid=(nt,),
        in_specs=[pl.BlockSpec((TM, D_IN), lambda i: (rev(i), 0)),
                  pl.BlockSpec((8, D_RG), lambda i: (jnp.maximum(rev(i) * (TM // 8) - 1, 0), 0)),
                  pl.BlockSpec((TM, D_RG), lambda i: (rev(i), 0)),
                  pl.BlockSpec((8, D_RG), lambda i: (jnp.maximum(rev(i) * (TM // 8) - 1, 0), 0)),
                  pl.BlockSpec((TM, D_HG), lambda i: (rev(i), 0)),
                  pl.BlockSpec((nc_t, NH, HD, HD), lambda i: (rev(i), 0, 0, 0)),
                  pl.BlockSpec((TM, D), lambda i: (rev(i), 0)),
                  _full((D_RG, D_RG)), _full((D_RG, D_RG)), _full((16, D_RG)), _full((2, D_HG)), _full((1, HD))]
        + [hbm] * nsc,
        out_specs=[pl.BlockSpec((TM, D_IN), lambda i: (rev(i), 0)), _full((16, D_RG)), _full((2, D_RG, D_RG))]
        + [hbm] * nsc,
        out_shape=[_S((t_pad, D_IN), _BF), _S((16, D_RG), _F32), _S((2, D_RG, D_RG), _F32)]
        + [_S(s.shape, s.dtype) for s in scatter],
        scratch_shapes=[pltpu.VMEM((TM + 8, D_RG), _F32), pltpu.VMEM((TM + 8, D_RG), _F32),
                        pltpu.VMEM((TM + 8, D_RG), _F32), pltpu.VMEM((TM, D_RG), _F32),
                        pltpu.VMEM((TM, D_RG), _F32), pltpu.VMEM((8, D_RG), _F32),
                        pltpu.VMEM((NH, HD, HD), _F32)]
        + [pltpu.VMEM((TM, D_HG), _BF) for _ in range(6)] + [pltpu.VMEM((TM, D_HG), _F32) for _ in range(5)]
        + [pltpu.VMEM((nc_t * NH, HD, HD), _F32), pltpu.VMEM((8, D_HG), _F32)] + _sem_shapes(nsc),
        compiler_params=_cp(("arbitrary",)),
    )(p, p, hs, hs, o, sc, dy, wr, wi, vec, hb, g_hg, *scatter)


def _inproj_bwd_send(dp, w_in, h0, dh1, g_mix, u, order, gffn, gfin, loss, to_all):
    t_pad = dp.shape[0]
    rb = TM
    n_steps = N_DEV + t_pad // rb
    na = len(to_all)

    def body(order_ref, dpc_ref, dpr_ref, u_ref, w_ref, h_ref, dh1_ref, g_ref, gffn_ref, gfin_ref, loss_ref, *rest):
        all_in = rest[:na]
        dh0_ref, recv_ref = rest[na:na + 2]
        all_out = rest[na + 2:2 * na + 2]
        alla_ref = rest[2 * na + 2]
        buf, pack, blk_send, blk_recv, blk_local = rest[2 * na + 3:2 * na + 8]
        exchange = _Exchange([], all_in, all_out, rest[2 * na + 8:2 * na + 11])
        last = _Exchange([], [pack], [alla_ref], rest[2 * na + 11:])
        s = pl.program_id(0)
        x, y, c = _coords()
        me = 4 * x + 2 * y + c

        def send(step):
            r = _SEND_ORDER[step]
            return pltpu.make_async_remote_copy(
                src_ref=buf.at[step], dst_ref=recv_ref.at[me], send_sem=blk_send.at[step], recv_sem=blk_recv.at[r - 1],
                device_id=(x ^ (r >> 2), y ^ ((r >> 1) & 1), c ^ (r & 1)), device_id_type=_MESH)

        @pl.when(s == 0)
        def _():
            exchange.start()
            pack[...] = jnp.zeros_like(pack)

        @pl.when(s < N_DEV)
        def _():
            buf[s] = _dot_tn(u_ref[...], dpc_ref[...]).astype(_BF)

            for step in range(N_DEV - 1):
                @pl.when(s == step)
                def _(step=step):
                    send(step).start()

        @pl.when(s >= N_DEV)
        def _():
            du = jnp.zeros((rb, D), _F32)
            for j in range(4):
                du = du + _dot_nt(dpr_ref[:, WIN_P * j:WIN_P * (j + 1)], w_ref[j])
            n, r = _rms_fwd(h_ref[...])
            pack[R_GMIX:R_GMIX + 1, :] += jnp.sum(du * n, axis=0, keepdims=True)
            dh0 = dh1_ref[...] + _rms_bwd(du * g_ref[...], n, r)
            dh0_ref[...] = dh0

            @pl.when(s == N_DEV)
            def _():
                pack[R_META:R_META + N_META, :] = dh0[0:N_META, :]

        @pl.when(s == n_steps - 1)
        def _():
            pack[R_GFFN:R_GFFN + 1, :] = gffn_ref[...]
            pack[R_GFIN:R_GFIN + 1, :] = gfin_ref[...]
            pack[R_LOSS:R_LOSS + 1, pl.ds(0, 128)] = loss_ref[0:1, :]
            last.start()
            mine = pltpu.make_async_copy(buf.at[N_DEV - 1], recv_ref.at[me], blk_local.at[0])
            mine.start()
            for step in range(N_DEV - 1):
                send(step).wait_send()
            for r in range(1, N_DEV):
                px, py, pc = x ^ (r >> 2), y ^ ((r >> 1) & 1), c ^ (r & 1)
                pltpu.make_async_remote_copy(
                    src_ref=buf.at[0], dst_ref=recv_ref.at[4 * px + 2 * py + pc], send_sem=blk_send.at[0],
                    recv_sem=blk_recv.at[r - 1], device_id=(px, py, pc), device_id_type=_MESH).wait_recv()
            mine.wait()
            exchange.finish()
            last.finish()

    hbm = pl.BlockSpec(memory_space=pl.ANY)
    rows = pl.BlockSpec((rb, D), lambda s, order: (jnp.maximum(s - N_DEV, 0), 0))
    one = pl.BlockSpec((1, D), lambda s, order: (0, 0))
    res = pl.pallas_call(
        body, name="inproj_bwd_send",
        grid_spec=pltpu.PrefetchScalarGridSpec(
            num_scalar_prefetch=1, grid=(n_steps,),
            in_specs=[pl.BlockSpec((t_pad, WIN_B), lambda s, order: (0, order[jnp.minimum(s, N_DEV - 1)])),
                      pl.BlockSpec((rb, D_IN), lambda s, order: (jnp.maximum(s - N_DEV, 0), 0)),
                      pl.BlockSpec((t_pad, D), lambda s, order: (0, 0), pipeline_mode=pl.Buffered(1)),
                      pl.BlockSpec((4, D, WIN_P), lambda s, order: (0, 0, 0), pipeline_mode=pl.Buffered(1)),
                      rows, rows, one, one, one, pl.BlockSpec((8, 128), lambda s, order: (0, 0))] + [hbm] * na,
            out_specs=[rows] + [hbm] * (na + 2),
            scratch_shapes=[pltpu.VMEM((N_DEV, D, WIN_B), _BF), pltpu.VMEM((24, D), _F32),
                            pltpu.SemaphoreType.DMA((N_DEV - 1,)), pltpu.SemaphoreType.DMA((N_DEV - 1,)),
                            pltpu.SemaphoreType.DMA((1,))] + _sem_shapes(na) + _sem_shapes(1)),
        out_shape=[_S((t_pad, D), _F32), _S((N_DEV, D, WIN_B), _BF)]
        + [_S((N_DEV,) + g.shape, g.dtype) for g in to_all] + [_S((N_DEV, 24, D), _F32)],
        compiler_params=_cp(("arbitrary",)),
    )(order, dp, dp, u, w_in, h0, dh1, g_mix, gffn, gfin, loss, *to_all)
    return res


def _wgrad(name, a, b, a_spec, b_spec, n_blocks, out_block, scatter=()):
    nsc = len(scatter)

    def body(a_ref, b_ref, *rest):
        o_ref = rest[nsc]
        j = pl.program_id(0)
        if nsc:
            exchange = _Exchange(rest[:nsc], [], rest[nsc + 1:2 * nsc + 1], rest[2 * nsc + 1:])

            @pl.when(j == 0)
            def _():
                exchange.start()

        av = a_ref[0] if len(a_ref.shape) == 3 else a_ref[...]
        bv = b_ref[0] if len(b_ref.shape) == 3 else b_ref[...]
        o_ref[0] = _dot_tn(av, bv).astype(_BF)

        if nsc:
            @pl.when(j == n_blocks - 1)
            def _():
                exchange.finish()

    hbm = pl.BlockSpec(memory_space=pl.ANY)
    res = pl.pallas_call(
        body, name=name, grid=(n_blocks,),
        in_specs=[a_spec, b_spec] + [hbm] * nsc,
        out_specs=[pl.BlockSpec((1,) + out_block, lambda j: (j, 0, 0))] + [hbm] * nsc,
        out_shape=[_S((n_blocks,) + out_block, _BF)] + [_S(s.shape, s.dtype) for s in scatter],
        scratch_shapes=_sem_shapes(nsc) if nsc else [],
        compiler_params=_cp(("arbitrary",)),
    )(a, b, *scatter)
    return res if nsc else res[0]


def _coords():
    return lax.axis_index("x"), lax.axis_index("y"), lax.axis_index("c")


def _sem_shapes(na):
    return [pltpu.SemaphoreType.DMA((7 * na,)), pltpu.SemaphoreType.DMA((7 * na,)), pltpu.SemaphoreType.DMA((na,))]


class _Gather:
    def __init__(self, srcs, outs, sems, place=None):
        self.srcs, self.outs = srcs, outs
        self.send_sems, self.recv_sems, self.local_sems = sems
        self.place = place if place is not None else (lambda ref, block: ref.at[block])
        self.na = len(srcs)
        x, y, c = _coords()
        self.pos = (x, y, c)
        self.me = 4 * x + 2 * y + c
        self.sibling = (x, y, 1 - c)
        self.chips = [(1 - x, y), (x, 1 - y), (1 - x, 1 - y)]

    @staticmethod
    def _slot(px, py, pc):
        return 4 * px + 2 * py + pc

    def _copy(self, a, k, block, to, own=False):
        dst = self.place(self.outs[a], block)
        return pltpu.make_async_remote_copy(
            src_ref=self.srcs[a] if own else dst, dst_ref=dst,
            send_sem=self.send_sems.at[7 * a + k], recv_sem=self.recv_sems.at[7 * a + k],
            device_id=to, device_id_type=_MESH)

    def _mine(self, a):
        return pltpu.make_async_copy(self.srcs[a], self.place(self.outs[a], self.me), self.local_sems.at[a])

    def _first(self):
        c = self.pos[2]
        cps = []
        for a in range(self.na):
            cps.append(self._copy(a, 0, self.me, self.sibling, own=True))
            cps += [self._copy(a, 1 + j, self.me, (*chip, c), own=True) for j, chip in enumerate(self.chips)]
        return cps

    def _passed(self):
        c = self.pos[2]
        return [self._copy(a, 4 + j, self._slot(*chip, c), self.sibling)
                for j, chip in enumerate(self.chips) for a in range(self.na)]

    def start(self):
        for a in range(self.na):
            self._mine(a).start()
        for cp in self._first():
            cp.start()

    def forward(self, j):
        c = self.pos[2]
        chip = self.chips[j]
        for a in range(self.na):
            self._copy(a, 1 + j, self._slot(*chip, c), self.pos).wait_recv()
            self._copy(a, 4 + j, self._slot(*chip, c), self.sibling).start()

    def wait_sibling(self):
        x, y, c = self.pos
        for a in range(self.na):
            self._copy(a, 0, self._slot(x, y, 1 - c), self.pos).wait_recv()

    def wait_passed(self, j):
        c = self.pos[2]
        for a in range(self.na):
            self._copy(a, 4 + j, self._slot(*self.chips[j], 1 - c), self.pos).wait_recv()

    def finish_sends(self):
        for cp in self._first() + self._passed():
            cp.wait_send()
        for a in range(self.na):
            self._mine(a).wait()

    def finish(self):
        self.wait_sibling()
        for j in range(3):
            self.wait_passed(j)
        self.finish_sends()


class _Exchange:
    def __init__(self, scatter, gather, outs, sems):
        self.ins = list(scatter) + list(gather)
        self.ns, self.na = len(scatter), len(scatter) + len(gather)
        self.outs = outs
        self.send_sems, self.recv_sems, self.local_sems = sems
        x, y, c = _coords()
        self.pos = (x, y, c)
        self.me = 4 * x + 2 * y + c

    def _peer(self, r):
        x, y, c = self.pos
        return x ^ (r >> 2), y ^ ((r >> 1) & 1), c ^ (r & 1)

    def _src(self, a, block):
        return self.ins[a].at[block] if a < self.ns else self.ins[a]

    def _local(self, a):
        return pltpu.make_async_copy(self._src(a, self.me), self.outs[a].at[self.me], self.local_sems.at[a])

    def _send(self, a, r):
        px, py, pc = self._peer(r)
        return pltpu.make_async_remote_copy(
            src_ref=self._src(a, 4 * px + 2 * py + pc), dst_ref=self.outs[a].at[self.me],
            send_sem=self.send_sems.at[7 * a + r - 1], recv_sem=self.recv_sems.at[7 * a + r - 1],
            device_id=(px, py, pc), device_id_type=_MESH)

    def _recv(self, a, r):
        px, py, pc = self._peer(r)
        return pltpu.make_async_remote_copy(
            src_ref=self._src(a, self.me), dst_ref=self.outs[a].at[4 * px + 2 * py + pc],
            send_sem=self.send_sems.at[7 * a + r - 1], recv_sem=self.recv_sems.at[7 * a + r - 1],
            device_id=(px, py, pc), device_id_type=_MESH)

    def start(self):
        for a in range(self.na):
            self._local(a).start()
        for r in range(1, N_DEV):
            for a in range(self.na):
                self._send(a, r).start()

    def finish(self):
        for r in range(1, N_DEV):
            for a in range(self.na):
                self._recv(a, r).wait_recv()
        for r in range(1, N_DEV):
            for a in range(self.na):
                self._send(a, r).wait_send()
        for a in range(self.na):
            self._local(a).wait()


def _prologue(x, tgt, small_l, w_in_l, cast_f32):
    seq = x.shape[0]
    nx = seq // TM
    rest_rows = seq - nx * TM
    nt = nx + 1
    nc = len(cast_f32)
    body_rows = TM - N_META
    assert nx >= 1 and rest_rows % 8 == 0 and rest_rows <= body_rows
    x_rest, t_rest = x[nx * TM:], tgt[nx * TM:]

    def last_tile_body(rest_ref):
        parts = ([rest_ref[...]] if rest_rows else []) + (
            [jnp.zeros((body_rows - rest_rows, D), _F32)] if body_rows > rest_rows else [])
        return parts[0] if len(parts) == 1 else jnp.concatenate(parts, axis=0)

    def body(xm_ref, xp_ref, tm_ref, tp_ref, *rest):
        if rest_rows:
            xr_ref, tr_ref, rest = rest[0], rest[1], rest[2:]
        else:
            xr_ref = tr_ref = None
        s_ref, w_ref, rest = rest[0], rest[1], rest[2:]
        cins = rest[:nc]
        h0_ref, tgt_ref, small_ref, wg_ref = rest[nc:nc + 4]
        couts = rest[nc + 4:2 * nc + 4]
        s_stage, w_stage, meta, msem = rest[2 * nc + 4:2 * nc + 8]
        g_s = _Gather([s_stage], [small_ref], rest[2 * nc + 8:2 * nc + 11])
        g_w = _Gather([w_stage], [wg_ref], rest[2 * nc + 11:], place=_pair_place)
        s = pl.program_id(0)
        i = (s + 1) % nt

        @pl.when(s == 0)
        def _():
            s_stage[...] = s_ref[...]
            w_stage[...] = w_ref[...].astype(_BF)
            g_s.start()
            g_w.start()
            meta[...] = jnp.zeros_like(meta)
            for a in range(nc):
                couts[a][...] = cins[a][...].astype(_BF)

        @pl.when(s == nt - 1)
        def _():
            for j in range(3):
                g_s.forward(j)
            g_s.finish()
            cps = [pltpu.make_async_copy(small_ref.at[k, pl.ds(0, N_META), :], meta.at[:, pl.ds(128 * k, 128)],
                                         msem.at[k]) for k in range(N_DEV)]
            for cp in cps:
                cp.start()
            for cp in cps:
                cp.wait()
            for j in range(3):
                g_w.forward(j)
            g_w.finish()

        has_x = i < nx
        h0_ref[pl.ds(0, N_META), :] = jnp.where(i == 0, meta[...], xp_ref[...])
        h0_ref[pl.ds(N_META, body_rows), :] = jnp.where(has_x, xm_ref[pl.ds(0, body_rows), :], last_tile_body(xr_ref))
        tgt_ref[pl.ds(0, N_META), :] = jnp.where(i == 0, 0.0, tp_ref[...])
        tgt_ref[pl.ds(N_META, body_rows), :] = jnp.where(has_x, tm_ref[pl.ds(0, body_rows), :], last_tile_body(tr_ref))

    def tile_of(s):
        return (s + 1) % nt

    hbm = pl.BlockSpec(memory_space=pl.ANY)
    main = pl.BlockSpec((TM, D), lambda s: (jnp.minimum(tile_of(s), nx - 1), 0))
    prev = pl.BlockSpec((N_META, D), lambda s: (jnp.maximum(tile_of(s) * (TM // N_META) - 1, 0), 0))
    tile = pl.BlockSpec((TM, D), lambda s: (tile_of(s), 0))
    rests = [x_rest, t_rest] if rest_rows else []
    return pl.pallas_call(
        body, name="prologue", grid=(nt,),
        in_specs=[main, prev, main, prev] + [_const(r.shape) for r in rests]
        + [_const(small_l.shape), _const(w_in_l.shape)] + [_const(l.shape) for l in cast_f32],
        out_specs=[tile, tile, hbm, hbm] + [_full(l.shape) for l in cast_f32],
        out_shape=[_S((nt * TM, D), _F32), _S((nt * TM, D), _F32), _S((N_DEV,) + small_l.shape, _F32),
                   _S((4, D, WIN_P), _BF)] + [_S(l.shape, _BF) for l in cast_f32],
        scratch_shapes=[pltpu.VMEM(small_l.shape, _F32), pltpu.VMEM(w_in_l.shape, _BF), pltpu.VMEM((N_META, D), _F32),
                        pltpu.SemaphoreType.DMA((N_DEV,))] + _sem_shapes(1) + _sem_shapes(1),
        compiler_params=_cp(("arbitrary",)),
    )(x, x, tgt, tgt, *rests, small_l, w_in_l, *cast_f32)


def _adamw_math(w, g, m, v):
    m2 = ADAM_B1 * m + (1.0 - ADAM_B1) * g
    v2 = ADAM_B2 * v + (1.0 - ADAM_B2) * (g * g)
    m_hat = m2 / (1.0 - ADAM_B1 ** ADAM_STEP)
    v_hat = v2 / (1.0 - ADAM_B2 ** ADAM_STEP)
    delta = -ADAM_LR * (m_hat / (jnp.sqrt(v_hat) + ADAM_EPS) + ADAM_WD * w)
    return delta, m2, v2


def _adamw_big(name, recv, w, m, v, rows):
    r_all, c_all = w.shape

    def body(r_ref, w_ref, m_ref, v_ref, g_out, d_out, m_out, v_out):
        g = r_ref[0].astype(_F32)
        for k in range(1, N_DEV):
            g = g + r_ref[k].astype(_F32)
        delta, m2, v2 = _adamw_math(w_ref[...], g, m_ref[...], v_ref[...])
        g_out[...] = g
        d_out[...] = delta
        m_out[...] = m2
        v_out[...] = v2

    tile = pl.BlockSpec((rows, c_all), lambda i: (i, 0))
    return pl.pallas_call(
        body, name=name, grid=(r_all // rows,),
        in_specs=[pl.BlockSpec((N_DEV, rows, c_all), lambda i: (0, i, 0)), tile, tile, tile],
        out_specs=[tile] * 4,
        out_shape=[_S(w.shape, _F32)] * 4,
        compiler_params=_cp(("arbitrary",)),
    )(recv, w, m, v)


def _adamw_small(gathered, slices, wmv):
    ng, npar = len(gathered), len(slices)

    def body(*refs):
        g_refs = refs[:ng]
        wmv_refs = refs[ng:ng + 3 * npar]
        outs = refs[ng + 3 * npar:]
        for i, (ai, r0, nr, c0, ncol) in enumerate(slices):
            g = g_refs[ai][0, pl.ds(r0, nr), pl.ds(c0, ncol)].astype(_F32)
            for k in range(1, N_DEV):
                g = g + g_refs[ai][k, pl.ds(r0, nr), pl.ds(c0, ncol)].astype(_F32)
            w_ref, m_ref, v_ref = wmv_refs[3 * i:3 * i + 3]
            delta, m2, v2 = _adamw_math(w_ref[...], g, m_ref[...], v_ref[...])
            outs[4 * i][...] = g
            outs[4 * i + 1][...] = delta
            outs[4 * i + 2][...] = m2
            outs[4 * i + 3][...] = v2
        total = g_refs[0][0, pl.ds(R_LOSS, 1), pl.ds(0, 128)]
        for k in range(1, N_DEV):
            total = total + g_refs[0][k, pl.ds(R_LOSS, 1), pl.ds(0, 128)]
        outs[4 * npar][...] = total

    flat = [t for trip in wmv for t in trip]
    out_shape = []
    for w, _, _ in wmv:
        out_shape += [_S(w.shape, _F32)] * 4
    out_shape.append(_S((1, 128), _F32))
    return pl.pallas_call(
        body, name="adamw_small", out_shape=out_shape,
        compiler_params=pltpu.CompilerParams(vmem_limit_bytes=VMEM_LIMIT),
    )(*gathered, *flat)


def _block_diag(w):
    eye = jnp.eye(8, dtype=w.dtype)
    return (w[:, :, None, :] * eye[:, None, :, None]).reshape(D_RG, D_RG)


def _diag_blocks(g):
    return jnp.concatenate([g[64 * h:64 * (h + 1), 64 * h:64 * (h + 1)] for h in range(8)], axis=0)


def _local_step(h0, tgt_p, n_valid, g_mix, w_in, vec, wr, wi, hb, g_hg, w_out_l, g_ffn, w_gu_l, w_down_l, g_fin):
    t_pad = h0.shape[0]
    me = 4 * lax.axis_index("x") + 2 * lax.axis_index("y") + lax.axis_index("c")
    p, u, y, hs, o, sc, w_out, w_gu, w_down = _mixer_fwd(h0, g_mix, w_in, wr, wi, vec, hb, g_hg,
                                                         [w_out_l, w_gu_l, w_down_l])
    w_out = w_out.reshape(D, D)
    w_down = w_down.reshape(4, FFB, D)
    h1, v, gu, act, dh2, dh2b, loss, gfin = _ffn_loss(h0, y, w_out, g_ffn, w_gu, w_down, g_fin, tgt_p, n_valid)

    dgu, dh1, dh1b, dy, gffn = _ffn_bwd(dh2, dh2b, gu, h1, g_ffn, w_gu, w_down, w_out)
    g_wdown = _wgrad("wgrad_down", act, dh2b, pl.BlockSpec((1, t_pad, FFB), lambda j: (j, 0, 0)),
                     pl.BlockSpec((t_pad, D), lambda j: (0, 0)), 4, (FFB, D))
    g_wgu, r_wdown = _wgrad("wgrad_gate_up", dgu, v, pl.BlockSpec((1, t_pad, FFB), lambda j: (j, 0, 0)),
                            pl.BlockSpec((t_pad, D), lambda j: (0, 0)), N_DEV, (FFB, D),
                            scatter=[g_wdown.reshape(N_DEV, D_FF // N_DEV, D)])
    g_wout = _wgrad("wgrad_out", y, dh1b, pl.BlockSpec((t_pad, D // N_DEV), lambda j: (0, j)),
                    pl.BlockSpec((t_pad, D), lambda j: (0, 0)), N_DEV, (D // N_DEV, D))
    dp, gvec, gw, r_wgu, r_wout = _mixer_bwd(p, hs, o, sc, dy, wr, wi, vec, hb, g_hg, [g_wgu, g_wout])
    pack_c = jnp.concatenate([_diag_blocks(gw[0]), _diag_blocks(gw[1])], axis=1).astype(_BF)
    order = (me ^ jnp.array(_SEND_ORDER, jnp.int32)).astype(jnp.int32)
    dh0, r_win, all_b, all_c, all_a = _inproj_bwd_send(dp, w_in, h0, dh1, g_mix, u, order, gffn, gfin, loss,
                                                       [gvec, pack_c])
    return dh0, (r_win, r_wgu, r_wout, r_wdown), (all_a, all_b, all_c)


def kernel(x, meta_tokens, mix_norm_g, w_in, conv_w, conv_b, w_rgate, b_rgate, w_igate, b_igate, lru_lambda, rg_norm_g, hg_lower_bound, hg_norm_g, w_out, ffn_norm_g, w_gate_up, w_down, final_norm_g, loss_target, m_meta_tokens, m_mix_norm_g, m_w_in, m_conv_w, m_conv_b, m_w_rgate, m_b_rgate, m_w_igate, m_b_igate, m_lru_lambda, m_rg_norm_g, m_hg_lower_bound, m_hg_norm_g, m_w_out, m_ffn_norm_g, m_w_gate_up, m_w_down, m_final_norm_g, v_meta_tokens, v_mix_norm_g, v_w_in, v_conv_w, v_conv_b, v_w_rgate, v_b_rgate, v_w_igate, v_b_igate, v_lru_lambda, v_rg_norm_g, v_hg_lower_bound, v_hg_norm_g, v_w_out, v_ffn_norm_g, v_w_gate_up, v_w_down, v_final_norm_g):
    seq = x.shape[1]
    me = 4 * lax.axis_index("x") + 2 * lax.axis_index("y") + lax.axis_index("c")

    n_valid = N_META + seq
    small_l = jnp.concatenate([meta_tokens, jnp.pad(conv_w[0], ((0, 4), (0, 64)))], axis=0)
    h0, tgt_p, small_g, w_in_g, w_gu_l, w_out_l, w_down_l = _prologue(
        x[0], loss_target[0], small_l, w_in[0], [w_gate_up[0].T, w_out[0], w_down[0]])
    conv_w_full = jnp.transpose(small_g[:, N_META:N_META + 4, :64], (1, 0, 2)).reshape(4, D_RG)
    vec = jnp.concatenate([conv_b, b_rgate, b_igate, lru_lambda, rg_norm_g, jnp.zeros((3, D_RG), _F32),
                           conv_w_full, jnp.zeros((4, D_RG), _F32)], axis=0)
    wr = _block_diag(w_rgate[0]).astype(_BF)
    wi = _block_diag(w_igate[0]).astype(_BF)

    dh0, (r_win, r_wgu, r_wout, r_wdown), (all_a, all_b, all_c) = _local_step(
        h0, tgt_p, n_valid, mix_norm_g, w_in_g, vec, wr, wi, hg_lower_bound, hg_norm_g,
        w_out_l, ffn_norm_g, w_gu_l, w_down_l, final_norm_g.reshape(1, D))
    grad_x = dh0[N_META:N_META + seq][None]

    outs = {}
    outs["w_in"] = _adamw_big("adamw_w_in", r_win, w_in[0], m_w_in[0], v_w_in[0], 256)
    outs["w_gate_up"] = [r.T for r in _adamw_big("adamw_w_gate_up", r_wgu, w_gate_up[0].T, m_w_gate_up[0].T,
                                                 v_w_gate_up[0].T, 176)]
    outs["w_out"] = _adamw_big("adamw_w_out", r_wout, w_out[0], m_w_out[0], v_w_out[0], 128)
    outs["w_down"] = _adamw_big("adamw_w_down", r_wdown, w_down[0], m_w_down[0], v_w_down[0], 176)

    meta_part = lax.dynamic_slice_in_dim(all_a[:, R_META:R_META + N_META, :], me * 128, 128, axis=2)
    convw_part = lax.dynamic_slice_in_dim(all_b[:, R_CONVW:R_CONVW + 4, :], me * 64, 64, axis=2)
    gathered = [all_a, all_b, all_c, meta_part, convw_part]
    small_params = [
        ("meta_tokens", (3, 0, N_META, 0, 128), (meta_tokens, m_meta_tokens, v_meta_tokens), (N_META, 128)),
        ("mix_norm_g", (0, R_GMIX, 1, 0, D), (mix_norm_g, m_mix_norm_g, v_mix_norm_g), (1, D)),
        ("conv_w", (4, 0, 4, 0, 64), (conv_w, m_conv_w, v_conv_w), (4, 64)),
        ("conv_b", (1, R_CONVB, 1, 0, D_RG), (conv_b, m_conv_b, v_conv_b), (1, D_RG)),
        ("w_rgate", (2, 0, 512, 0, 64), (w_rgate, m_w_rgate, v_w_rgate), (512, 64)),
        ("b_rgate", (1, R_BR, 1, 0, D_RG), (b_rgate, m_b_rgate, v_b_rgate), (1, D_RG)),
        ("w_igate", (2, 0, 512, 64, 64), (w_igate, m_w_igate, v_w_igate), (512, 64)),
        ("b_igate", (1, R_BI, 1, 0, D_RG), (b_igate, m_b_igate, v_b_igate), (1, D_RG)),
        ("lru_lambda", (1, R_LAM, 1, 0, D_RG), (lru_lambda, m_lru_lambda, v_lru_lambda), (1, D_RG)),
        ("rg_norm_g", (1, R_GRG, 1, 0, D_RG), (rg_norm_g, m_rg_norm_g, v_rg_norm_g), (1, D_RG)),
        ("hg_lower_bound", (1, R_HB0, 2, 0, D_HG), (hg_lower_bound, m_hg_lower_bound, v_hg_lower_bound), (2, D_HG)),
        ("hg_norm_g", (1, R_GHG, 1, 0, HD), (hg_norm_g, m_hg_norm_g, v_hg_norm_g), (1, HD)),
        ("ffn_norm_g", (0, R_GFFN, 1, 0, D), (ffn_norm_g, m_ffn_norm_g, v_ffn_norm_g), (1, D)),
        ("final_norm_g", (0, R_GFIN, 1, 0, D), (final_norm_g, m_final_norm_g, v_final_norm_g), (1, D)),
    ]
    res = _adamw_small(gathered, [s[1] for s in small_params],
                       [tuple(t.reshape(s[3]) for t in s[2]) for s in small_params])
    for i, s in enumerate(small_params):
        outs[s[0]] = [r.reshape(s[2][0].shape) for r in res[4 * i:4 * i + 4]]
    for n, ref in (("w_in", w_in), ("w_gate_up", w_gate_up), ("w_out", w_out), ("w_down", w_down)):
        outs[n] = [r.reshape(ref.shape) for r in outs[n]]

    loss_all = res[4 * len(small_params)][0, 0]
    order = ["meta_tokens", "mix_norm_g", "w_in", "conv_w", "conv_b", "w_rgate", "b_rgate", "w_igate", "b_igate",
             "lru_lambda", "rg_norm_g", "hg_lower_bound", "hg_norm_g", "w_out", "ffn_norm_g", "w_gate_up", "w_down",
             "final_norm_g"]
    return (loss_all, grad_x, *[outs[n][0] for n in order], *[outs[n][1] for n in order],
            *[outs[n][2] for n in order], *[outs[n][3] for n in order])
```

```python
import functools

import jax
import jax.numpy as jnp
from jax import lax
from jax.experimental import pallas as pl
from jax.experimental.pallas import tpu as pltpu

_BF = jnp.bfloat16
_F32 = jnp.float32
_S = jax.ShapeDtypeStruct
_MESH = pl.DeviceIdType.MESH

N_DEV = 8
N_META = 16
D = 1024
D_RG = 512
D_HG = 512
HD = 128
NH = D_HG // HD
D_IN = 3072
D_FF = 2816
FFB = D_FF // 4
WIN_B = D_IN // N_DEV
WIN_P = 2 * WIN_B
EPS = 1e-6
LRU_C = 8.0
TM = 320
HC = 64
VMEM_LIMIT = 62 * 1024 * 1024

ADAM_LR = 0.001
ADAM_B1 = 0.9
ADAM_B2 = 0.999
ADAM_EPS = 1e-08
ADAM_WD = 0.01
ADAM_STEP = 10

_SEND_ORDER = (6, 4, 2, 7, 5, 3, 1, 0)

R_CONVB, R_BR, R_BI, R_LAM, R_GRG, R_HB0, R_HB1, R_GHG, R_CONVW = 0, 1, 2, 3, 4, 5, 6, 7, 8
R_GMIX, R_GFFN, R_GFIN, R_LOSS, R_META = 0, 1, 2, 3, 8


def _cp(sem=None, **kw):
    return pltpu.CompilerParams(dimension_semantics=sem, vmem_limit_bytes=VMEM_LIMIT, **kw)


def _dot(a, b):
    return jnp.dot(a, b, preferred_element_type=_F32)


def _dot_nt(a, b):
    return lax.dot_general(a, b, (((1,), (1,)), ((), ())), preferred_element_type=_F32)


def _dot_tn(a, b):
    return lax.dot_general(a, b, (((0,), (0,)), ((), ())), preferred_element_type=_F32)


def _sigmoid(x):
    return 0.5 * jnp.tanh(0.5 * x) + 0.5


def _dsilu(x, s):
    return s * (1.0 + x * (1.0 - s))


_GELU_C = 0.7978845608028654


def _gelu_parts(x):
    t = jnp.tanh(_GELU_C * (x + 0.044715 * (x * x * x)))
    g = 0.5 * x * (1.0 + t)
    dg = 0.5 * (1.0 + t) + 0.5 * x * (1.0 - t * t) * (_GELU_C * (1.0 + 3.0 * 0.044715 * (x * x)))
    return g, dg


def _softplus(z):
    e = jnp.exp(-jnp.abs(z))
    w = 1.0 + e
    l1p = jnp.where(w == 1.0, e, jnp.log(w) * e / jnp.where(w == 1.0, 1.0, w - 1.0))
    return jnp.maximum(z, 0.0) + l1p


def _rms_fwd(x):
    r = lax.rsqrt(jnp.mean(x * x, axis=-1, keepdims=True) + EPS)
    return x * r, r


def _rms_bwd(dyg, n, r):
    return r * (dyg - n * jnp.mean(dyg * n, axis=-1, keepdims=True))


def _full(shape):
    nd = len(shape)
    return pl.BlockSpec(shape, lambda i: (0,) * nd)


def _const(shape):
    nd = len(shape)
    return pl.BlockSpec(shape, lambda i: (0,) * nd, pipeline_mode=pl.Buffered(1))


def _carry_gather(gather, i, nt):
    @pl.when(i == 0)
    def _():
        gather.start()

    def tail():
        for j in range(3):
            @pl.when(i == max(nt - 4 + j, 0))
            def _(j=j):
                gather.forward(j)

        @pl.when(i == nt - 1)
        def _():
            gather.finish()

    return tail


def _pair_place(ref, block):
    return ref.at[block // 2, :, pl.ds(pl.multiple_of((block % 2) * WIN_B, WIN_B), WIN_B)]


def _rg_gates(xc, wr_ref, wi_ref, vec_ref):
    xcb = xc.astype(_BF)
    r = _sigmoid(_dot(xcb, wr_ref[...]) + vec_ref[R_BR:R_BR + 1, :])
    ig = _sigmoid(_dot(xcb, wi_ref[...]) + vec_ref[R_BI:R_BI + 1, :])
    nsp8 = -LRU_C * _softplus(-vec_ref[R_LAM:R_LAM + 1, :])
    la = nsp8 * r
    a = jnp.exp(la)
    th = jnp.tanh(la)
    s = jnp.sqrt(-2.0 * th / (1.0 - th))
    return r, ig, a, s, nsp8


def _conv(xbuf, vec_ref):
    acc = vec_ref[R_CONVW:R_CONVW + 1, :] * xbuf[pl.ds(5, TM), :]
    for j in range(1, 4):
        acc = acc + vec_ref[R_CONVW + j:R_CONVW + j + 1, :] * xbuf[pl.ds(5 + j, TM), :]
    return vec_ref[R_CONVB:R_CONVB + 1, :] + acc


def _dot3(m01, x):
    hi = x.astype(_BF)
    r1 = x - hi.astype(_F32)
    mid = r1.astype(_BF)
    lo = (r1 - mid.astype(_F32)).astype(_BF)
    return (_dot(m01, lo) + _dot(m01, mid)) + _dot(m01, hi)


def _chunk_dot3(m01, x):
    return jnp.concatenate([_dot3(m01, x[HC * c:HC * (c + 1), :]) for c in range(x.shape[0] // HC)], axis=0)


def _chunk_masks():
    row = lax.broadcasted_iota(jnp.int32, (HC, HC), 0)
    col = lax.broadcasted_iota(jnp.int32, (HC, HC), 1)
    return (row >= col).astype(_BF), (col >= row).astype(_BF), jnp.ones((HC, HC), _BF)


def _per_chunk_rows(x, r):
    return jnp.concatenate([jnp.broadcast_to(x[HC * c + r:HC * c + r + 1, :], (HC, x.shape[1]))
                            for c in range(TM // HC)], axis=0)


def _hg_prep(p_ref, lb, tri):
    hq = p_ref[:, pl.ds(2 * D_RG, D_HG)]
    hf = p_ref[:, pl.ds(2 * D_RG + D_HG, D_HG)]
    sq = _sigmoid(hq)
    q = hq * sq
    sg = _sigmoid(hf)
    f = lb + (1.0 - lb) * sg
    k = 1.0 - f
    b = _chunk_dot3(tri, jnp.log(f))
    bm = _per_chunk_rows(b, HC // 2 - 1)
    bl = _per_chunk_rows(b, HC - 1)
    e_q = jnp.exp(b - bm)
    e_k = jnp.exp(bm - b)
    e_b = jnp.exp(b)
    e_l = jnp.exp(bl - b)
    return dict(hq=hq, sq=sq, q=q, sg=sg, f=f, k=k, e_q=e_q, e_k=e_k, e_b=e_b, e_l=e_l,
                qd=q * e_q, kd=k * e_k, qe=q * e_b, ke=k * e_l, e_end=jnp.exp(bl))


def _mixer_fwd(h0, g_mix, w_in, wr, wi, vec, hb, g_hg, shards):
    t_pad = h0.shape[0]
    nt = t_pad // TM
    nc_t = TM // HC
    nsh = len(shards)

    def body(h_ref, gmix_ref, win_ref, wr_ref, wi_ref, vec_ref, hb_ref, ghg_ref, *rest):
        sh_refs, rest = rest[:nsh], rest[nsh:]
        pout_ref, uout_ref, y_ref, hs_ref, o_ref, sc_ref = rest[:6]
        gath_refs, rest = rest[6:6 + nsh], rest[6 + nsh:]
        xbuf, a_s, b_s, hcar, st, qd_s, kd_s, qe_s, ke_s, v_s, u_s, p_s, p_ref = rest[:13]
        i = pl.program_id(0)
        tail = _carry_gather(_Gather(sh_refs, gath_refs, rest[13:]), i, nt + 1)

        @pl.when(i == 0)
        def _():
            p_s[...] = jnp.zeros_like(p_s)

        p_ref[...] = p_s[...]

        @pl.when(i <= 1)
        def _():
            xbuf[pl.ds(0, 8), :] = jnp.zeros((8, D_RG), _F32)
            hcar[...] = jnp.zeros_like(hcar)
            st[...] = jnp.zeros_like(st)

        n_h, _ = _rms_fwd(h_ref[...])
        u = (n_h * gmix_ref[...]).astype(_BF)
        uout_ref[...] = u
        pieces = [(j, k) for j in range(4) for k in range(WIN_P // 256)]

        def project(count):
            for _ in range(count):
                j, k = pieces.pop(0)
                blk = _dot(u, win_ref[j, :, pl.ds(256 * k, 256)])
                p_s[:, pl.ds(WIN_P * j + 256 * k, 256)] = blk
                pout_ref[:, pl.ds(WIN_P * j + 256 * k, 256)] = blk

        x = p_ref[:, pl.ds(0, D_RG)]
        xbuf[pl.ds(8, TM), :] = x
        xc = _conv(xbuf, vec_ref)
        xbuf[pl.ds(0, 8), :] = x[TM - 8:, :]
        r, ig, a, s, _ = _rg_gates(xc, wr_ref, wi_ref, vec_ref)
        a_s[...] = a
        b_s[...] = s * (ig * xc)

        def step(t, h):
            h = a_s[pl.ds(t, 1), :] * h + b_s[pl.ds(t, 1), :]
            hs_ref[pl.ds(t, 1), :] = h
            return h

        hcar[pl.ds(0, 1), :] = lax.fori_loop(0, TM, step, hcar[pl.ds(0, 1), :], unroll=8)
        gel, _ = _gelu_parts(p_ref[:, pl.ds(D_RG, D_RG)])
        n, _ = _rms_fwd(gel * hs_ref[...])
        y_ref[:, pl.ds(0, D_RG)] = (n * vec_ref[R_GRG:R_GRG + 1, :]).astype(_BF)

        lb = _sigmoid(hb_ref[0:1, :] - hb_ref[1:2, :])
        tri, _, _ = _chunk_masks()
        q = _hg_prep(p_ref, lb, tri)
        for name, ref in (("qd", qd_s), ("kd", kd_s), ("qe", qe_s), ("ke", ke_s)):
            ref[...] = q[name].astype(_BF)
        v_s[...] = p_ref[:, pl.ds(2 * D_RG + 2 * D_HG, D_HG)].astype(_BF)
        e_end = q["e_end"]
        causal = (lax.broadcasted_iota(jnp.int32, (HC, HC), 0) >= lax.broadcasted_iota(jnp.int32, (HC, HC), 1))
        for c in range(nc_t):
            for h in range(NH):
                rs, cs = pl.ds(HC * c, HC), pl.ds(HD * h, HD)
                amat = jnp.where(causal, _dot_nt(qd_s[rs, cs], kd_s[rs, cs]), 0.0)
                o_ref[rs, cs] = _dot(amat.astype(_BF), v_s[rs, cs])
                u_s[NH * c + h] = _dot_tn(v_s[rs, cs], ke_s[rs, cs])
                if pieces:
                    project(1)
        assert not pieces
        for h in range(NH):
            cs = pl.ds(HD * h, HD)
            s_run = st[h]
            for c in range(nc_t):
                rs = pl.ds(HC * c, HC)
                sc_ref[c, h] = s_run
                o_ref[rs, cs] += _dot_nt(qe_s[rs, cs], s_run.astype(_BF))
                s_run = e_end[HC * c:HC * c + 1, HD * h:HD * (h + 1)] * s_run + u_s[NH * c + h]
            st[h] = s_run
        for h in range(NH):
            cs = pl.ds(HD * h, HD)
            n_o, _ = _rms_fwd(o_ref[:, cs])
            hg = p_ref[:, pl.ds(2 * D_RG + 3 * D_HG + HD * h, HD)]
            y_ref[:, pl.ds(D_RG + HD * h, HD)] = ((n_o * ghg_ref[...]) * (hg * _sigmoid(hg))).astype(_BF)

        tail()

    hbm = pl.BlockSpec(memory_space=pl.ANY)

    def proj(i):
        return jnp.minimum(i, nt - 1)

    def mixed(i):
        return jnp.maximum(i - 1, 0)

    return pl.pallas_call(
        body, name="mixer_fwd", grid=(nt + 1,),
        in_specs=[pl.BlockSpec((TM, D), lambda i: (proj(i), 0)), _full((1, D)), _const((4, D, WIN_P)),
                  _full((D_RG, D_RG)), _full((D_RG, D_RG)),
                  _full((16, D_RG)), _full((2, D_HG)), _full((1, HD))] + [hbm] * nsh,
        out_specs=[pl.BlockSpec((TM, D_IN), lambda i: (proj(i), 0)), pl.BlockSpec((TM, D), lambda i: (proj(i), 0)),
                   pl.BlockSpec((TM, D), lambda i: (mixed(i), 0)), pl.BlockSpec((TM, D_RG), lambda i: (mixed(i), 0)),
                   pl.BlockSpec((TM, D_HG), lambda i: (mixed(i), 0)),
                   pl.BlockSpec((nc_t, NH, HD, HD), lambda i: (mixed(i), 0, 0, 0))] + [hbm] * nsh,
        out_shape=[_S((t_pad, D_IN), _F32), _S((t_pad, D), _BF),
                   _S((t_pad, D), _BF), _S((t_pad, D_RG), _F32), _S((t_pad, D_HG), _F32),
                   _S((t_pad // HC, NH, HD, HD), _F32)] + [_S((N_DEV,) + s.shape, s.dtype) for s in shards],
        scratch_shapes=[pltpu.VMEM((TM + 8, D_RG), _F32), pltpu.VMEM((TM, D_RG), _F32),
                        pltpu.VMEM((TM, D_RG), _F32), pltpu.VMEM((8, D_RG), _F32),
                        pltpu.VMEM((NH, HD, HD), _F32)] + [pltpu.VMEM((TM, D_HG), _BF) for _ in range(5)]
        + [pltpu.VMEM((nc_t * NH, HD, HD), _F32), pltpu.VMEM((TM, D_IN), _F32), pltpu.VMEM((TM, D_IN), _F32)]
        + _sem_shapes(nsh),
        compiler_params=_cp(("arbitrary",)),
    )(h0, g_mix, w_in, wr, wi, vec, hb, g_hg, *shards)


def _ffn_loss(h0, y, w_out, g_ffn, w_gu, w_down, g_fin, tgt, n_valid):
    t_pad = h0.shape[0]

    def body(h_ref, y_ref, wo_ref, gffn_ref, wgu_ref, wd_ref, g_ref, t_ref,
             h1_ref, v_ref, gu_ref, act_ref, dh2_ref, dh2b_ref, loss_ref, gfin_ref):
        i = pl.program_id(0)

        @pl.when(i == 0)
        def _():
            loss_ref[...] = jnp.zeros_like(loss_ref)
            gfin_ref[...] = jnp.zeros_like(gfin_ref)

        h1 = h_ref[...] + _dot(y_ref[...], wo_ref[...])
        h1_ref[...] = h1
        n1, _ = _rms_fwd(h1)
        vb = (n1 * gffn_ref[...]).astype(_BF)
        v_ref[...] = vb
        h2 = h1
        for b in range(4):
            gate = _dot_nt(vb, wgu_ref[b])
            up = _dot_nt(vb, wgu_ref[4 + b])
            gu_ref[b] = gate
            gu_ref[4 + b] = up
            act = ((gate * _sigmoid(gate)) * up).astype(_BF)
            act_ref[b] = act
            h2 = h2 + _dot(act, wd_ref[b])
        n, r = _rms_fwd(h2)
        out = n * g_ref[...]
        row = i * TM + lax.broadcasted_iota(jnp.int32, (TM, 1), 0)
        valid = (row >= N_META) & (row < n_valid)
        err = jnp.where(valid, out - t_ref[...], 0.0)
        loss_ref[...] += (0.5 / D) * jnp.sum(err * err)
        dout = err * (1.0 / D)
        gfin_ref[...] += jnp.sum(dout * n, axis=0, keepdims=True)
        dh2 = _rms_bwd(dout * g_ref[...], n, r)
        dh2_ref[...] = dh2
        dh2b_ref[...] = dh2.astype(_BF)

    tile = pl.BlockSpec((TM, D), lambda i: (i, 0))
    return pl.pallas_call(
        body, name="ffn_loss", grid=(t_pad // TM,),
        in_specs=[tile, tile, _const((D, D)), _full((1, D)),
                  _const((N_DEV, FFB, D)), _const((4, FFB, D)), _full((1, D)), tile],
        out_specs=[tile, tile,
                   pl.BlockSpec((N_DEV, TM, FFB), lambda i: (0, i, 0)), pl.BlockSpec((4, TM, FFB), lambda i: (0, i, 0)),
                   tile, tile, _full((8, 128)), _full((1, D))],
        out_shape=[_S((t_pad, D), _F32), _S((t_pad, D), _BF),
                   _S((N_DEV, t_pad, FFB), _F32), _S((4, t_pad, FFB), _BF), _S((t_pad, D), _F32),
                   _S((t_pad, D), _BF), _S((8, 128), _F32), _S((1, D), _F32)],
        compiler_params=_cp(("arbitrary",)),
    )(h0, y, w_out, g_ffn, w_gu, w_down, g_fin, tgt)


def _ffn_bwd(dh2, dh2b, gu, h1, g_ffn, w_gu, w_down, w_out):
    t_pad = dh2.shape[0]

    def body(dh2_ref, dh2b_ref, gu_ref, h1_ref, g_ref, wgu_ref, wd_ref, wo_ref,
             dgu_ref, dh1_ref, dh1b_ref, dy_ref, gffn_ref):
        i = pl.program_id(0)

        @pl.when(i == 0)
        def _():
            gffn_ref[...] = jnp.zeros_like(gffn_ref)

        db = dh2b_ref[...]
        dv = jnp.zeros((TM, D), _F32)
        for b in range(4):
            dact = _dot_nt(db, wd_ref[b])
            gate = gu_ref[b]
            up = gu_ref[4 + b]
            sg = _sigmoid(gate)
            dgate = ((dact * up) * _dsilu(gate, sg)).astype(_BF)
            dup = (dact * (gate * sg)).astype(_BF)
            dgu_ref[b] = dgate
            dgu_ref[4 + b] = dup
            dv = dv + _dot(dgate, wgu_ref[b]) + _dot(dup, wgu_ref[4 + b])
        n, r = _rms_fwd(h1_ref[...])
        gffn_ref[...] += jnp.sum(dv * n, axis=0, keepdims=True)
        dh1 = dh2_ref[...] + _rms_bwd(dv * g_ref[...], n, r)
        dh1_ref[...] = dh1
        dh1b = dh1.astype(_BF)
        dh1b_ref[...] = dh1b
        dy_ref[...] = _dot_nt(dh1b, wo_ref[...])

    tile = pl.BlockSpec((TM, D), lambda i: (i, 0))
    return pl.pallas_call(
        body, name="ffn_bwd", grid=(t_pad // TM,),
        in_specs=[tile, tile, pl.BlockSpec((N_DEV, TM, FFB), lambda i: (0, i, 0)), tile, _full((1, D)),
                  _const((N_DEV, FFB, D)), _const((4, FFB, D)), _const((D, D))],
        out_specs=[pl.BlockSpec((N_DEV, TM, FFB), lambda i: (0, i, 0)), tile, tile, tile, _full((1, D))],
        out_shape=[_S((N_DEV, t_pad, FFB), _BF), _S((t_pad, D), _F32), _S((t_pad, D), _BF),
                   _S((t_pad, D), _F32), _S((1, D), _F32)],
        compiler_params=_cp(("arbitrary",)),
    )(dh2, dh2b, gu, h1, g_ffn, w_gu, w_down, w_out)


def _mixer_bwd(p, hs, o, sc, dy, wr, wi, vec, hb, g_hg, scatter):
    t_pad = p.shape[0]
    nt = t_pad // TM
    nc_t = TM // HC
    nsc = len(scatter)

    def rev(i):
        return nt - 1 - i

    def body(p_ref, pprev_ref, hs_ref, hprev_ref, o_ref, sc_ref, dy_ref, wr_ref, wi_ref, vec_ref, hb_ref, ghg_ref,
             *rest):
        send_refs, rest = rest[:nsc], rest[nsc:]
        dp_ref, gvec_ref, gw_ref = rest[:3]
        recv_refs, rest = rest[3:3 + nsc], rest[3 + nsc:]
        xbuf, hbuf, dbuf, a_s, g_s, ccar, dst = rest[:7]
        qd_s, kd_s, qe_s, ke_s, v_s, do_s, dqd_s, dkd_s, dqe_s, dke_s, dv_s, w_s, dend_s = rest[7:20]
        exchange = _Exchange(send_refs, [], recv_refs, rest[20:])
        i = pl.program_id(0)
        first_tile = i == nt - 1

        @pl.when(i == 0)
        def _():
            exchange.start()
            gvec_ref[...] = jnp.zeros_like(gvec_ref)
            gw_ref[...] = jnp.zeros_like(gw_ref)
            dbuf[pl.ds(TM, 8), :] = jnp.zeros((8, D_RG), _F32)
            ccar[...] = jnp.zeros_like(ccar)
            dst[...] = jnp.zeros_like(dst)

        def acc(row, val):
            gvec_ref[row:row + 1, :] += jnp.sum(val, axis=0, keepdims=True)

        keep = jnp.where(first_tile, 0.0, 1.0)
        x = p_ref[:, pl.ds(0, D_RG)]
        xbuf[pl.ds(0, 8), :] = pprev_ref[...] * keep
        xbuf[pl.ds(8, TM), :] = x
        xc = _conv(xbuf, vec_ref)
        r, ig, a, s, nsp8 = _rg_gates(xc, wr_ref, wi_ref, vec_ref)
        h = hs_ref[...]
        hbuf[pl.ds(0, 8), :] = hprev_ref[...] * keep
        hbuf[pl.ds(8, TM), :] = h
        hm1 = hbuf[pl.ds(7, TM), :]
        gr = p_ref[:, pl.ds(D_RG, D_RG)]
        gel, dgel = _gelu_parts(gr)
        n, rr = _rms_fwd(gel * h)
        dyn = dy_ref[:, pl.ds(0, D_RG)]
        acc(R_GRG, dyn * n)
        dpre = _rms_bwd(dyn * vec_ref[R_GRG:R_GRG + 1, :], n, rr)
        dp_ref[:, pl.ds(D_RG, D_RG)] = ((dpre * h) * dgel).astype(_BF)
        a_s[...] = a
        g_s[...] = dpre * gel

        def step(k, c):
            t = TM - 1 - k
            g = g_s[pl.ds(t, 1), :] + c
            g_s[pl.ds(t, 1), :] = g
            return a_s[pl.ds(t, 1), :] * g

        ccar[pl.ds(0, 1), :] = lax.fori_loop(0, TM, step, ccar[pl.ds(0, 1), :], unroll=8)
        gt = g_s[...]
        da = gt * hm1
        ixc = ig * xc
        ds = gt * ixc
        dig = (gt * s) * xc
        dxc = (gt * s) * ig
        dla = da * a - ds * ((a * a) / s)
        lam = vec_ref[R_LAM:R_LAM + 1, :]
        gvec_ref[R_LAM:R_LAM + 1, :] += jnp.sum(dla * r, axis=0, keepdims=True) * (LRU_C * _sigmoid(-lam))
        dzr = (dla * nsp8) * (r * (1.0 - r))
        dzi = dig * (ig * (1.0 - ig))
        acc(R_BR, dzr)
        acc(R_BI, dzi)
        xcb = xc.astype(_BF)
        dzrb = dzr.astype(_BF)
        dzib = dzi.astype(_BF)
        gw_ref[0] += _dot_tn(xcb, dzrb)
        gw_ref[1] += _dot_tn(xcb, dzib)
        dxc = dxc + _dot_nt(dzrb, wr_ref[...]) + _dot_nt(dzib, wi_ref[...])
        acc(R_CONVB, dxc)
        for j in range(4):
            acc(R_CONVW + j, dxc * xbuf[pl.ds(5 + j, TM), :])
        dbuf[pl.ds(0, TM), :] = dxc
        dx = vec_ref[R_CONVW + 3:R_CONVW + 4, :] * dxc
        for j in range(3):
            dx = dx + vec_ref[R_CONVW + j:R_CONVW + j + 1, :] * dbuf[pl.ds(3 - j, TM), :]
        dbuf[pl.ds(TM, 8), :] = dxc[0:8, :]
        dp_ref[:, pl.ds(0, D_RG)] = dx.astype(_BF)

        lb = _sigmoid(hb_ref[0:1, :] - hb_ref[1:2, :])
        tri, tri_rev, ones = _chunk_masks()
        q = _hg_prep(p_ref, lb, tri)
        qdb, kdb = q["qd"].astype(_BF), q["kd"].astype(_BF)
        qd_s[...] = qdb
        kd_s[...] = kdb
        qe_s[...] = q["qe"].astype(_BF)
        ke_s[...] = q["ke"].astype(_BF)
        v_s[...] = p_ref[:, pl.ds(2 * D_RG + 2 * D_HG, D_HG)].astype(_BF)
        e_end = q["e_end"]
        ghg = ghg_ref[...]
        for h in range(NH):
            cs = pl.ds(HD * h, HD)
            hg = p_ref[:, pl.ds(2 * D_RG + 3 * D_HG + HD * h, HD)]
            sh = _sigmoid(hg)
            n_o, r_o = _rms_fwd(o_ref[:, cs])
            dyh = dy_ref[:, pl.ds(D_RG + HD * h, HD)]
            dp_ref[:, pl.ds(2 * D_RG + 3 * D_HG + HD * h, HD)] = ((dyh * (n_o * ghg)) * _dsilu(hg, sh)).astype(_BF)
            dn = dyh * (hg * sh)
            gvec_ref[R_GHG:R_GHG + 1, pl.ds(0, HD)] += jnp.sum(dn * n_o, axis=0, keepdims=True)
            do_s[:, cs] = _rms_bwd(dn * ghg, n_o, r_o).astype(_BF)
        causal = (lax.broadcasted_iota(jnp.int32, (HC, HC), 0) >= lax.broadcasted_iota(jnp.int32, (HC, HC), 1))
        for c in range(nc_t):
            for h in range(NH):
                rs, cs = pl.ds(HC * c, HC), pl.ds(HD * h, HD)
                qd_c, kd_c, do_c = qd_s[rs, cs], kd_s[rs, cs], do_s[rs, cs]
                amat = jnp.where(causal, _dot_nt(qd_c, kd_c), 0.0).astype(_BF)
                da_m = jnp.where(causal, _dot_nt(do_c, v_s[rs, cs]), 0.0).astype(_BF)
                dqd_s[rs, cs] = _dot(da_m, kd_c)
                dkd_s[rs, cs] = _dot_tn(da_m, qd_c)
                dqe_s[rs, cs] = _dot(do_c, sc_ref[c, h].astype(_BF))
                dv_s[rs, cs] = _dot_tn(amat, do_c)
                w_s[NH * c + h] = _dot_tn(do_c, qe_s[rs, cs])
        for h in range(NH):
            cs = pl.ds(HD * h, HD)
            d_run = dst[h]
            for c in reversed(range(nc_t)):
                rs = pl.ds(HC * c, HC)
                d_b = d_run.astype(_BF)
                dke_s[rs, cs] = _dot(v_s[rs, cs], d_b)
                dp_ref[rs, pl.ds(2 * D_RG + 2 * D_HG + HD * h, HD)] = (
                    dv_s[rs, cs] + _dot_nt(ke_s[rs, cs], d_b)).astype(_BF)
                dend_s[pl.ds(c, 1), cs] = jnp.sum(sc_ref[c, h] * d_run, axis=0, keepdims=True)
                d_run = w_s[NH * c + h] + e_end[HC * c:HC * c + 1, HD * h:HD * (h + 1)] * d_run
            dst[h] = d_run
        dqd, dkd, dqe, dke = dqd_s[...], dkd_s[...], dqe_s[...], dke_s[...]
        dq = dqd * q["e_q"] + dqe * q["e_b"]
        dk = dkd * q["e_k"] + dke * q["e_l"]
        dkeke = dke * q["ke"]
        db = dqd * qdb.astype(_F32) - dkd * kdb.astype(_F32) + dqe * q["qe"] - dkeke
        d_end = jnp.concatenate([jnp.broadcast_to(dend_s[pl.ds(c, 1), :], (HC, D_HG)) for c in range(nc_t)], axis=0)
        dlf = _chunk_dot3(tri_rev, db) + _chunk_dot3(ones, dkeke) + d_end * e_end
        df = dlf / q["f"] - dk
        sg = q["sg"]
        gvec_ref[R_HB0:R_HB0 + 1, :] += jnp.sum(df * (1.0 - sg), axis=0, keepdims=True)
        dp_ref[:, pl.ds(2 * D_RG, D_HG)] = (dq * _dsilu(q["hq"], q["sq"])).astype(_BF)
        dp_ref[:, pl.ds(2 * D_RG + D_HG, D_HG)] = ((df * (1.0 - lb)) * (sg * (1.0 - sg))).astype(_BF)

        @pl.when(i == nt - 1)
        def _():
            glb = gvec_ref[R_HB0:R_HB0 + 1, :] * (lb * (1.0 - lb))
            gvec_ref[R_HB0:R_HB0 + 1, :] = glb
            gvec_ref[R_HB1:R_HB1 + 1, :] = -glb
            exchange.finish()

    hbm = pl.BlockSpec(memory_space=pl.ANY)
    return pl.pallas_call(
        body, name="mixer_bwd", grid=(nt,),
        in_specs=[pl.BlockSpec((TM, D_IN), lambda i: (rev(i), 0)),
                  pl.BlockSpec((8, D_RG), lambda i: (jnp.maximum(rev(i) * (TM // 8) - 1, 0), 0)),
                  pl.BlockSpec((TM, D_RG), lambda i: (rev(i), 0)),
                  pl.BlockSpec((8, D_RG), lambda i: (jnp.maximum(rev(i) * (TM // 8) - 1, 0), 0)),
                  pl.BlockSpec((TM, D_HG), lambda i: (rev(i), 0)),
                  pl.BlockSpec((nc_t, NH, HD, HD), lambda i: (rev(i), 0, 0, 0)),
                  pl.BlockSpec((TM, D), lambda i: (rev(i), 0)),
                  _full((D_RG, D_RG)), _full((D_RG, D_RG)), _full((16, D_RG)), _full((2, D_HG)), _full((1, HD))]
        + [hbm] * nsc,
        out_specs=[pl.BlockSpec((TM, D_IN), lambda i: (rev(i), 0)), _full((16, D_RG)), _full((2, D_RG, D_RG))]
        + [hbm] * nsc,
        out_shape=[_S((t_pad, D_IN), _BF), _S((16, D_RG), _F32), _S((2, D_RG, D_RG), _F32)]
        + [_S(s.shape, s.dtype) for s in scatter],
        scratch_shapes=[pltpu.VMEM((TM + 8, D_RG), _F32), pltpu.VMEM((TM + 8, D_RG), _F32),
                        pltpu.VMEM((TM + 8, D_RG), _F32), pltpu.VMEM((TM, D_RG), _F32),
                        pltpu.VMEM((TM, D_RG), _F32), pltpu.VMEM((8, D_RG), _F32),
                        pltpu.VMEM((NH, HD, HD), _F32)]
        + [pltpu.VMEM((TM, D_HG), _BF) for _ in range(6)] + [pltpu.VMEM((TM, D_HG), _F32) for _ in range(5)]
        + [pltpu.VMEM((nc_t * NH, HD, HD), _F32), pltpu.VMEM((8, D_HG), _F32)] + _sem_shapes(nsc),
        compiler_params=_cp(("arbitrary",)),
    )(p, p, hs, hs, o, sc, dy, wr, wi, vec, hb, g_hg, *scatter)


def _inproj_bwd_send(dp, w_in, h0, dh1, g_mix, u, order, gffn, gfin, loss, to_all):
    t_pad = dp.shape[0]
    rb = TM
    n_steps = N_DEV + t_pad // rb
    na = len(to_all)

    def body(order_ref, dpc_ref, dpr_ref, u_ref, w_ref, h_ref, dh1_ref, g_ref, gffn_ref, gfin_ref, loss_ref, *rest):
        all_in = rest[:na]
        dh0_ref, recv_ref = rest[na:na + 2]
        all_out = rest[na + 2:2 * na + 2]
        alla_ref = rest[2 * na + 2]
        buf, pack, blk_send, blk_recv, blk_local = rest[2 * na + 3:2 * na + 8]
        exchange = _Exchange([], all_in, all_out, rest[2 * na + 8:2 * na + 11])
        last = _Exchange([], [pack], [alla_ref], rest[2 * na + 11:])
        s = pl.program_id(0)
        x, y, c = _coords()
        me = 4 * x + 2 * y + c

        def send(step):
            r = _SEND_ORDER[step]
            return pltpu.make_async_remote_copy(
                src_ref=buf.at[step], dst_ref=recv_ref.at[me], send_sem=blk_send.at[step], recv_sem=blk_recv.at[r - 1],
                device_id=(x ^ (r >> 2), y ^ ((r >> 1) & 1), c ^ (r & 1)), device_id_type=_MESH)

        @pl.when(s == 0)
        def _():
            exchange.start()
            pack[...] = jnp.zeros_like(pack)

        @pl.when(s < N_DEV)
        def _():
            buf[s] = _dot_tn(u_ref[...], dpc_ref[...]).astype(_BF)

            for step in range(N_DEV - 1):
                @pl.when(s == step)
                def _(step=step):
                    send(step).start()

        @pl.when(s >= N_DEV)
        def _():
            du = jnp.zeros((rb, D), _F32)
            for j in range(4):
                du = du + _dot_nt(dpr_ref[:, WIN_P * j:WIN_P * (j + 1)], w_ref[j])
            n, r = _rms_fwd(h_ref[...])
            pack[R_GMIX:R_GMIX + 1, :] += jnp.sum(du * n, axis=0, keepdims=True)
            dh0 = dh1_ref[...] + _rms_bwd(du * g_ref[...], n, r)
            dh0_ref[...] = dh0

            @pl.when(s == N_DEV)
            def _():
                pack[R_META:R_META + N_META, :] = dh0[0:N_META, :]

        @pl.when(s == n_steps - 1)
        def _():
            pack[R_GFFN:R_GFFN + 1, :] = gffn_ref[...]
            pack[R_GFIN:R_GFIN + 1, :] = gfin_ref[...]
            pack[R_LOSS:R_LOSS + 1, pl.ds(0, 128)] = loss_ref[0:1, :]
            last.start()
            mine = pltpu.make_async_copy(buf.at[N_DEV - 1], recv_ref.at[me], blk_local.at[0])
            mine.start()
            for step in range(N_DEV - 1):
                send(step).wait_send()
            for r in range(1, N_DEV):
                px, py, pc = x ^ (r >> 2), y ^ ((r >> 1) & 1), c ^ (r & 1)
                pltpu.make_async_remote_copy(
                    src_ref=buf.at[0], dst_ref=recv_ref.at[4 * px + 2 * py + pc], send_sem=blk_send.at[0],
                    recv_sem=blk_recv.at[r - 1], device_id=(px, py, pc), device_id_type=_MESH).wait_recv()
            mine.wait()
            exchange.finish()
            last.finish()

    hbm = pl.BlockSpec(memory_space=pl.ANY)
    rows = pl.BlockSpec((rb, D), lambda s, order: (jnp.maximum(s - N_DEV, 0), 0))
    one = pl.BlockSpec((1, D), lambda s, order: (0, 0))
    res = pl.pallas_call(
        body, name="inproj_bwd_send",
        grid_spec=pltpu.PrefetchScalarGridSpec(
            num_scalar_prefetch=1, grid=(n_steps,),
            in_specs=[pl.BlockSpec((t_pad, WIN_B), lambda s, order: (0, order[jnp.minimum(s, N_DEV - 1)])),
                      pl.BlockSpec((rb, D_IN), lambda s, order: (jnp.maximum(s - N_DEV, 0), 0)),
                      pl.BlockSpec((t_pad, D), lambda s, order: (0, 0), pipeline_mode=pl.Buffered(1)),
                      pl.BlockSpec((4, D, WIN_P), lambda s, order: (0, 0, 0), pipeline_mode=pl.Buffered(1)),
                      rows, rows, one, one, one, pl.BlockSpec((8, 128), lambda s, order: (0, 0))] + [hbm] * na,
            out_specs=[rows] + [hbm] * (na + 2),
            scratch_shapes=[pltpu.VMEM((N_DEV, D, WIN_B), _BF), pltpu.VMEM((24, D), _F32),
                            pltpu.SemaphoreType.DMA((N_DEV - 1,)), pltpu.SemaphoreType.DMA((N_DEV - 1,)),
                            pltpu.SemaphoreType.DMA((1,))] + _sem_shapes(na) + _sem_shapes(1)),
        out_shape=[_S((t_pad, D), _F32), _S((N_DEV, D, WIN_B), _BF)]
        + [_S((N_DEV,) + g.shape, g.dtype) for g in to_all] + [_S((N_DEV, 24, D), _F32)],
        compiler_params=_cp(("arbitrary",)),
    )(order, dp, dp, u, w_in, h0, dh1, g_mix, gffn, gfin, loss, *to_all)
    return res


def _wgrad(name, a, b, a_spec, b_spec, n_blocks, out_block, scatter=()):
    nsc = len(scatter)

    def body(a_ref, b_ref, *rest):
        o_ref = rest[nsc]
        j = pl.program_id(0)
        if nsc:
            exchange = _Exchange(rest[:nsc], [], rest[nsc + 1:2 * nsc + 1], rest[2 * nsc + 1:])

            @pl.when(j == 0)
            def _():
                exchange.start()

        av = a_ref[0] if len(a_ref.shape) == 3 else a_ref[...]
        bv = b_ref[0] if len(b_ref.shape) == 3 else b_ref[...]
        o_ref[0] = _dot_tn(av, bv).astype(_BF)

        if nsc:
            @pl.when(j == n_blocks - 1)
            def _():
                exchange.finish()

    hbm = pl.BlockSpec(memory_space=pl.ANY)
    res = pl.pallas_call(
        body, name=name, grid=(n_blocks,),
        in_specs=[a_spec, b_spec] + [hbm] * nsc,
        out_specs=[pl.BlockSpec((1,) + out_block, lambda j: (j, 0, 0))] + [hbm] * nsc,
        out_shape=[_S((n_blocks,) + out_block, _BF)] + [_S(s.shape, s.dtype) for s in scatter],
        scratch_shapes=_sem_shapes(nsc) if nsc else [],
        compiler_params=_cp(("arbitrary",)),
    )(a, b, *scatter)
    return res if nsc else res[0]


def _coords():
    return lax.axis_index("x"), lax.axis_index("y"), lax.axis_index("c")


def _sem_shapes(na):
    return [pltpu.SemaphoreType.DMA((7 * na,)), pltpu.SemaphoreType.DMA((7 * na,)), pltpu.SemaphoreType.DMA((na,))]


class _Gather:
    def __init__(self, srcs, outs, sems, place=None):
        self.srcs, self.outs = srcs, outs
        self.send_sems, self.recv_sems, self.local_sems = sems
        self.place = place if place is not None else (lambda ref, block: ref.at[block])
        self.na = len(srcs)
        x, y, c = _coords()
        self.pos = (x, y, c)
        self.me = 4 * x + 2 * y + c
        self.sibling = (x, y, 1 - c)
        self.chips = [(1 - x, y), (x, 1 - y), (1 - x, 1 - y)]

    @staticmethod
    def _slot(px, py, pc):
        return 4 * px + 2 * py + pc

    def _copy(self, a, k, block, to, own=False):
        dst = self.place(self.outs[a], block)
        return pltpu.make_async_remote_copy(
            src_ref=self.srcs[a] if own else dst, dst_ref=dst,
            send_sem=self.send_sems.at[7 * a + k], recv_sem=self.recv_sems.at[7 * a + k],
            device_id=to, device_id_type=_MESH)

    def _mine(self, a):
        return pltpu.make_async_copy(self.srcs[a], self.place(self.outs[a], self.me), self.local_sems.at[a])

    def _first(self):
        c = self.pos[2]
        cps = []
        for a in range(self.na):
            cps.append(self._copy(a, 0, self.me, self.sibling, own=True))
            cps += [self._copy(a, 1 + j, self.me, (*chip, c), own=True) for j, chip in enumerate(self.chips)]
        return cps

    def _passed(self):
        c = self.pos[2]
        return [self._copy(a, 4 + j, self._slot(*chip, c), self.sibling)
                for j, chip in enumerate(self.chips) for a in range(self.na)]

    def start(self):
        for a in range(self.na):
            self._mine(a).start()
        for cp in self._first():
            cp.start()

    def forward(self, j):
        c = self.pos[2]
        chip = self.chips[j]
        for a in range(self.na):
            self._copy(a, 1 + j, self._slot(*chip, c), self.pos).wait_recv()
            self._copy(a, 4 + j, self._slot(*chip, c), self.sibling).start()

    def wait_sibling(self):
        x, y, c = self.pos
        for a in range(self.na):
            self._copy(a, 0, self._slot(x, y, 1 - c), self.pos).wait_recv()

    def wait_passed(self, j):
        c = self.pos[2]
        for a in range(self.na):
            self._copy(a, 4 + j, self._slot(*self.chips[j], 1 - c), self.pos).wait_recv()

    def finish_sends(self):
        for cp in self._first() + self._passed():
            cp.wait_send()
        for a in range(self.na):
            self._mine(a).wait()

    def finish(self):
        self.wait_sibling()
        for j in range(3):
            self.wait_passed(j)
        self.finish_sends()


class _Exchange:
    def __init__(self, scatter, gather, outs, sems):
        self.ins = list(scatter) + list(gather)
        self.ns, self.na = len(scatter), len(scatter) + len(gather)
        self.outs = outs
        self.send_sems, self.recv_sems, self.local_sems = sems
        x, y, c = _coords()
        self.pos = (x, y, c)
        self.me = 4 * x + 2 * y + c

    def _peer(self, r):
        x, y, c = self.pos
        return x ^ (r >> 2), y ^ ((r >> 1) & 1), c ^ (r & 1)

    def _src(self, a, block):
        return self.ins[a].at[block] if a < self.ns else self.ins[a]

    def _local(self, a):
        return pltpu.make_async_copy(self._src(a, self.me), self.outs[a].at[self.me], self.local_sems.at[a])

    def _send(self, a, r):
        px, py, pc = self._peer(r)
        return pltpu.make_async_remote_copy(
            src_ref=self._src(a, 4 * px + 2 * py + pc), dst_ref=self.outs[a].at[self.me],
            send_sem=self.send_sems.at[7 * a + r - 1], recv_sem=self.recv_sems.at[7 * a + r - 1],
            device_id=(px, py, pc), device_id_type=_MESH)

    def _recv(self, a, r):
        px, py, pc = self._peer(r)
        return pltpu.make_async_remote_copy(
            src_ref=self._src(a, self.me), dst_ref=self.outs[a].at[4 * px + 2 * py + pc],
            send_sem=self.send_sems.at[7 * a + r - 1], recv_sem=self.recv_sems.at[7 * a + r - 1],
            device_id=(px, py, pc), device_id_type=_MESH)

    def start(self):
        for a in range(self.na):
            self._local(a).start()
        for r in range(1, N_DEV):
            for a in range(self.na):
                self._send(a, r).start()

    def finish(self):
        for r in range(1, N_DEV):
            for a in range(self.na):
                self._recv(a, r).wait_recv()
        for r in range(1, N_DEV):
            for a in range(self.na):
                self._send(a, r).wait_send()
        for a in range(self.na):
            self._local(a).wait()


def _prologue(x, tgt, small_l, w_in_l, cast_f32):
    seq = x.shape[0]
    nx = seq // TM
    rest_rows = seq - nx * TM
    nt = nx + 1
    nc = len(cast_f32)
    body_rows = TM - N_META
    assert nx >= 1 and rest_rows % 8 == 0 and rest_rows <= body_rows
    x_rest, t_rest = x[nx * TM:], tgt[nx * TM:]

    def last_tile_body(rest_ref):
        parts = ([rest_ref[...]] if rest_rows else []) + (
            [jnp.zeros((body_rows - rest_rows, D), _F32)] if body_rows > rest_rows else [])
        return parts[0] if len(parts) == 1 else jnp.concatenate(parts, axis=0)

    def body(xm_ref, xp_ref, tm_ref, tp_ref, *rest):
        if rest_rows:
            xr_ref, tr_ref, rest = rest[0], rest[1], rest[2:]
        else:
            xr_ref = tr_ref = None
        s_ref, w_ref, rest = rest[0], rest[1], rest[2:]
        cins = rest[:nc]
        h0_ref, tgt_ref, small_ref, wg_ref = rest[nc:nc + 4]
        couts = rest[nc + 4:2 * nc + 4]
        s_stage, w_stage, meta, msem = rest[2 * nc + 4:2 * nc + 8]
        g_s = _Gather([s_stage], [small_ref], rest[2 * nc + 8:2 * nc + 11])
        g_w = _Gather([w_stage], [wg_ref], rest[2 * nc + 11:], place=_pair_place)
        s = pl.program_id(0)
        i = (s + 1) % nt

        @pl.when(s == 0)
        def _():
            s_stage[...] = s_ref[...]
            w_stage[...] = w_ref[...].astype(_BF)
            g_s.start()
            g_w.start()
            meta[...] = jnp.zeros_like(meta)
            for a in range(nc):
                couts[a][...] = cins[a][...].astype(_BF)

        @pl.when(s == nt - 1)
        def _():
            for j in range(3):
                g_s.forward(j)
            g_s.finish()
            cps = [pltpu.make_async_copy(small_ref.at[k, pl.ds(0, N_META), :], meta.at[:, pl.ds(128 * k, 128)],
                                         msem.at[k]) for k in range(N_DEV)]
            for cp in cps:
                cp.start()
            for cp in cps:
                cp.wait()
            for j in range(3):
                g_w.forward(j)
            g_w.finish()

        has_x = i < nx
        h0_ref[pl.ds(0, N_META), :] = jnp.where(i == 0, meta[...], xp_ref[...])
        h0_ref[pl.ds(N_META, body_rows), :] = jnp.where(has_x, xm_ref[pl.ds(0, body_rows), :], last_tile_body(xr_ref))
        tgt_ref[pl.ds(0, N_META), :] = jnp.where(i == 0, 0.0, tp_ref[...])
        tgt_ref[pl.ds(N_META, body_rows), :] = jnp.where(has_x, tm_ref[pl.ds(0, body_rows), :], last_tile_body(tr_ref))

    def tile_of(s):
        return (s + 1) % nt

    hbm = pl.BlockSpec(memory_space=pl.ANY)
    main = pl.BlockSpec((TM, D), lambda s: (jnp.minimum(tile_of(s), nx - 1), 0))
    prev = pl.BlockSpec((N_META, D), lambda s: (jnp.maximum(tile_of(s) * (TM // N_META) - 1, 0), 0))
    tile = pl.BlockSpec((TM, D), lambda s: (tile_of(s), 0))
    rests = [x_rest, t_rest] if rest_rows else []
    return pl.pallas_call(
        body, name="prologue", grid=(nt,),
        in_specs=[main, prev, main, prev] + [_const(r.shape) for r in rests]
        + [_const(small_l.shape), _const(w_in_l.shape)] + [_const(l.shape) for l in cast_f32],
        out_specs=[tile, tile, hbm, hbm] + [_full(l.shape) for l in cast_f32],
        out_shape=[_S((nt * TM, D), _F32), _S((nt * TM, D), _F32), _S((N_DEV,) + small_l.shape, _F32),
                   _S((4, D, WIN_P), _BF)] + [_S(l.shape, _BF) for l in cast_f32],
        scratch_shapes=[pltpu.VMEM(small_l.shape, _F32), pltpu.VMEM(w_in_l.shape, _BF), pltpu.VMEM((N_META, D), _F32),
                        pltpu.SemaphoreType.DMA((N_DEV,))] + _sem_shapes(1) + _sem_shapes(1),
        compiler_params=_cp(("arbitrary",)),
    )(x, x, tgt, tgt, *rests, small_l, w_in_l, *cast_f32)


def _adamw_math(w, g, m, v):
    m2 = ADAM_B1 * m + (1.0 - ADAM_B1) * g
    v2 = ADAM_B2 * v + (1.0 - ADAM_B2) * (g * g)
    m_hat = m2 / (1.0 - ADAM_B1 ** ADAM_STEP)
    v_hat = v2 / (1.0 - ADAM_B2 ** ADAM_STEP)
    delta = -ADAM_LR * (m_hat / (jnp.sqrt(v_hat) + ADAM_EPS) + ADAM_WD * w)
    return delta, m2, v2


def _adamw_big(name, recv, w, m, v, rows):
    r_all, c_all = w.shape

    def body(r_ref, w_ref, m_ref, v_ref, g_out, d_out, m_out, v_out):
        g = r_ref[0].astype(_F32)
        for k in range(1, N_DEV):
            g = g + r_ref[k].astype(_F32)
        delta, m2, v2 = _adamw_math(w_ref[...], g, m_ref[...], v_ref[...])
        g_out[...] = g
        d_out[...] = delta
        m_out[...] = m2
        v_out[...] = v2

    tile = pl.BlockSpec((rows, c_all), lambda i: (i, 0))
    return pl.pallas_call(
        body, name=name, grid=(r_all // rows,),
        in_specs=[pl.BlockSpec((N_DEV, rows, c_all), lambda i: (0, i, 0)), tile, tile, tile],
        out_specs=[tile] * 4,
        out_shape=[_S(w.shape, _F32)] * 4,
        compiler_params=_cp(("arbitrary",)),
    )(recv, w, m, v)


def _adamw_small(gathered, slices, wmv):
    ng, npar = len(gathered), len(slices)

    def body(*refs):
        g_refs = refs[:ng]
        wmv_refs = refs[ng:ng + 3 * npar]
        outs = refs[ng + 3 * npar:]
        for i, (ai, r0, nr, c0, ncol) in enumerate(slices):
            g = g_refs[ai][0, pl.ds(r0, nr), pl.ds(c0, ncol)].astype(_F32)
            for k in range(1, N_DEV):
                g = g + g_refs[ai][k, pl.ds(r0, nr), pl.ds(c0, ncol)].astype(_F32)
            w_ref, m_ref, v_ref = wmv_refs[3 * i:3 * i + 3]
            delta, m2, v2 = _adamw_math(w_ref[...], g, m_ref[...], v_ref[...])
            outs[4 * i][...] = g
            outs[4 * i + 1][...] = delta
            outs[4 * i + 2][...] = m2
            outs[4 * i + 3][...] = v2
        total = g_refs[0][0, pl.ds(R_LOSS, 1), pl.ds(0, 128)]
        for k in range(1, N_DEV):
            total = total + g_refs[0][k, pl.ds(R_LOSS, 1), pl.ds(0, 128)]
        outs[4 * npar][...] = total

    flat = [t for trip in wmv for t in trip]
    out_shape = []
    for w, _, _ in wmv:
        out_shape += [_S(w.shape, _F32)] * 4
    out_shape.append(_S((1, 128), _F32))
    return pl.pallas_call(
        body, name="adamw_small", out_shape=out_shape,
        compiler_params=pltpu.CompilerParams(vmem_limit_bytes=VMEM_LIMIT),
    )(*gathered, *flat)


def _block_diag(w):
    eye = jnp.eye(8, dtype=w.dtype)
    return (w[:, :, None, :] * eye[:, None, :, None]).reshape(D_RG, D_RG)


def _diag_blocks(g):
    return jnp.concatenate([g[64 * h:64 * (h + 1), 64 * h:64 * (h + 1)] for h in range(8)], axis=0)


def _local_step(h0, tgt_p, n_valid, g_mix, w_in, vec, wr, wi, hb, g_hg, w_out_l, g_ffn, w_gu_l, w_down_l, g_fin):
    t_pad = h0.shape[0]
    me = 4 * lax.axis_index("x") + 2 * lax.axis_index("y") + lax.axis_index("c")
    p, u, y, hs, o, sc, w_out, w_gu, w_down = _mixer_fwd(h0, g_mix, w_in, wr, wi, vec, hb, g_hg,
                                                         [w_out_l, w_gu_l, w_down_l])
    w_out = w_out.reshape(D, D)
    w_down = w_down.reshape(4, FFB, D)
    h1, v, gu, act, dh2, dh2b, loss, gfin = _ffn_loss(h0, y, w_out, g_ffn, w_gu, w_down, g_fin, tgt_p, n_valid)

    dgu, dh1, dh1b, dy, gffn = _ffn_bwd(dh2, dh2b, gu, h1, g_ffn, w_gu, w_down, w_out)
    g_wdown = _wgrad("wgrad_down", act, dh2b, pl.BlockSpec((1, t_pad, FFB), lambda j: (j, 0, 0)),
                     pl.BlockSpec((t_pad, D), lambda j: (0, 0)), 4, (FFB, D))
    g_wgu, r_wdown = _wgrad("wgrad_gate_up", dgu, v, pl.BlockSpec((1, t_pad, FFB), lambda j: (j, 0, 0)),
                            pl.BlockSpec((t_pad, D), lambda j: (0, 0)), N_DEV, (FFB, D),
                            scatter=[g_wdown.reshape(N_DEV, D_FF // N_DEV, D)])
    g_wout = _wgrad("wgrad_out", y, dh1b, pl.BlockSpec((t_pad, D // N_DEV), lambda j: (0, j)),
                    pl.BlockSpec((t_pad, D), lambda j: (0, 0)), N_DEV, (D // N_DEV, D))
    dp, gvec, gw, r_wgu, r_wout = _mixer_bwd(p, hs, o, sc, dy, wr, wi, vec, hb, g_hg, [g_wgu, g_wout])
    pack_c = jnp.concatenate([_diag_blocks(gw[0]), _diag_blocks(gw[1])], axis=1).astype(_BF)
    order = (me ^ jnp.array(_SEND_ORDER, jnp.int32)).astype(jnp.int32)
    dh0, r_win, all_b, all_c, all_a = _inproj_bwd_send(dp, w_in, h0, dh1, g_mix, u, order, gffn, gfin, loss,
                                                       [gvec, pack_c])
    return dh0, (r_win, r_wgu, r_wout, r_wdown), (all_a, all_b, all_c)


def kernel(x, meta_tokens, mix_norm_g, w_in, conv_w, conv_b, w_rgate, b_rgate, w_igate, b_igate, lru_lambda, rg_norm_g, hg_lower_bound, hg_norm_g, w_out, ffn_norm_g, w_gate_up, w_down, final_norm_g, loss_target, m_meta_tokens, m_mix_norm_g, m_w_in, m_conv_w, m_conv_b, m_w_rgate, m_b_rgate, m_w_igate, m_b_igate, m_lru_lambda, m_rg_norm_g, m_hg_lower_bound, m_hg_norm_g, m_w_out, m_ffn_norm_g, m_w_gate_up, m_w_down, m_final_norm_g, v_meta_tokens, v_mix_norm_g, v_w_in, v_conv_w, v_conv_b, v_w_rgate, v_b_rgate, v_w_igate, v_b_igate, v_lru_lambda, v_rg_norm_g, v_hg_lower_bound, v_hg_norm_g, v_w_out, v_ffn_norm_g, v_w_gate_up, v_w_down, v_final_norm_g):
    seq = x.shape[1]
    me = 4 * lax.axis_index("x") + 2 * lax.axis_index("y") + lax.axis_index("c")

    n_valid = N_META + seq
    small_l = jnp.concatenate([meta_tokens, jnp.pad(conv_w[0], ((0, 4), (0, 64)))], axis=0)
    h0, tgt_p, small_g, w_in_g, w_gu_l, w_out_l, w_down_l = _prologue(
        x[0], loss_target[0], small_l, w_in[0], [w_gate_up[0].T, w_out[0], w_down[0]])
    conv_w_full = jnp.transpose(small_g[:, N_META:N_META + 4, :64], (1, 0, 2)).reshape(4, D_RG)
    vec = jnp.concatenate([conv_b, b_rgate, b_igate, lru_lambda, rg_norm_g, jnp.zeros((3, D_RG), _F32),
                           conv_w_full, jnp.zeros((4, D_RG), _F32)], axis=0)
    wr = _block_diag(w_rgate[0]).astype(_BF)
    wi = _block_diag(w_igate[0]).astype(_BF)

    dh0, (r_win, r_wgu, r_wout, r_wdown), (all_a, all_b, all_c) = _local_step(
        h0, tgt_p, n_valid, mix_norm_g, w_in_g, vec, wr, wi, hg_lower_bound, hg_norm_g,
        w_out_l, ffn_norm_g, w_gu_l, w_down_l, final_norm_g.reshape(1, D))
    grad_x = dh0[N_META:N_META + seq][None]

    outs = {}
    outs["w_in"] = _adamw_big("adamw_w_in", r_win, w_in[0], m_w_in[0], v_w_in[0], 256)
    outs["w_gate_up"] = [r.T for r in _adamw_big("adamw_w_gate_up", r_wgu, w_gate_up[0].T, m_w_gate_up[0].T,
                                                 v_w_gate_up[0].T, 176)]
    outs["w_out"] = _adamw_big("adamw_w_out", r_wout, w_out[0], m_w_out[0], v_w_out[0], 128)
    outs["w_down"] = _adamw_big("adamw_w_down", r_wdown, w_down[0], m_w_down[0], v_w_down[0], 176)

    meta_part = lax.dynamic_slice_in_dim(all_a[:, R_META:R_META + N_META, :], me * 128, 128, axis=2)
    convw_part = lax.dynamic_slice_in_dim(all_b[:, R_CONVW:R_CONVW + 4, :], me * 64, 64, axis=2)
    gathered = [all_a, all_b, all_c, meta_part, convw_part]
    small_params = [
        ("meta_tokens", (3, 0, N_META, 0, 128), (meta_tokens, m_meta_tokens, v_meta_tokens), (N_META, 128)),
        ("mix_norm_g", (0, R_GMIX, 1, 0, D), (mix_norm_g, m_mix_norm_g, v_mix_norm_g), (1, D)),
        ("conv_w", (4, 0, 4, 0, 64), (conv_w, m_conv_w, v_conv_w), (4, 64)),
        ("conv_b", (1, R_CONVB, 1, 0, D_RG), (conv_b, m_conv_b, v_conv_b), (1, D_RG)),
        ("w_rgate", (2, 0, 512, 0, 64), (w_rgate, m_w_rgate, v_w_rgate), (512, 64)),
        ("b_rgate", (1, R_BR, 1, 0, D_RG), (b_rgate, m_b_rgate, v_b_rgate), (1, D_RG)),
        ("w_igate", (2, 0, 512, 64, 64), (w_igate, m_w_igate, v_w_igate), (512, 64)),
        ("b_igate", (1, R_BI, 1, 0, D_RG), (b_igate, m_b_igate, v_b_igate), (1, D_RG)),
        ("lru_lambda", (1, R_LAM, 1, 0, D_RG), (lru_lambda, m_lru_lambda, v_lru_lambda), (1, D_RG)),
        ("rg_norm_g", (1, R_GRG, 1, 0, D_RG), (rg_norm_g, m_rg_norm_g, v_rg_norm_g), (1, D_RG)),
        ("hg_lower_bound", (1, R_HB0, 2, 0, D_HG), (hg_lower_bound, m_hg_lower_bound, v_hg_lower_bound), (2, D_HG)),
        ("hg_norm_g", (1, R_GHG, 1, 0, HD), (hg_norm_g, m_hg_norm_g, v_hg_norm_g), (1, HD)),
        ("ffn_norm_g", (0, R_GFFN, 1, 0, D), (ffn_norm_g, m_ffn_norm_g, v_ffn_norm_g), (1, D)),
        ("final_norm_g", (0, R_GFIN, 1, 0, D), (final_norm_g, m_final_norm_g, v_final_norm_g), (1, D)),
    ]
    res = _adamw_small(gathered, [s[1] for s in small_params],
                       [tuple(t.reshape(s[3]) for t in s[2]) for s in small_params])
    for i, s in enumerate(small_params):
        outs[s[0]] = [r.reshape(s[2][0].shape) for r in res[4 * i:4 * i + 4]]
    for n, ref in (("w_in", w_in), ("w_gate_up", w_gate_up), ("w_out", w_out), ("w_down", w_down)):
        outs[n] = [r.reshape(ref.shape) for r in outs[n]]

    loss_all = res[4 * len(small_params)][0, 0]
    order = ["meta_tokens", "mix_norm_g", "w_in", "conv_w", "conv_b", "w_rgate", "b_rgate", "w_igate", "b_igate",
             "lru_lambda", "rg_norm_g", "hg_lower_bound", "hg_norm_g", "w_out", "ffn_norm_g", "w_gate_up", "w_down",
             "final_norm_g"]
    return (loss_all, grad_x, *[outs[n][0] for n in order], *[outs[n][1] for n in order],
            *[outs[n][2] for n in order], *[outs[n][3] for n in order])
```

```python
import functools

import jax
import jax.numpy as jnp
from jax import lax
from jax.experimental import pallas as pl
from jax.experimental.pallas import tpu as pltpu

_BF = jnp.bfloat16
_F32 = jnp.float32
_S = jax.ShapeDtypeStruct
_MESH = pl.DeviceIdType.MESH

N_DEV = 8
N_META = 16
D = 1024
D_RG = 512
D_HG = 512
HD = 128
NH = D_HG // HD
D_IN = 3072
D_FF = 2816
FFB = D_FF // 4
WIN_B = D_IN // N_DEV
WIN_P = 2 * WIN_B
EPS = 1e-6
LRU_C = 8.0
TM = 320
HC = 64
VMEM_LIMIT = 62 * 1024 * 1024

ADAM_LR = 0.001
ADAM_B1 = 0.9
ADAM_B2 = 0.999
ADAM_EPS = 1e-08
ADAM_WD = 0.01
ADAM_STEP = 10

_SEND_ORDER = (6, 4, 2, 7, 5, 3, 1, 0)

R_CONVB, R_BR, R_BI, R_LAM, R_GRG, R_HB0, R_HB1, R_GHG, R_CONVW = 0, 1, 2, 3, 4, 5, 6, 7, 8
R_GMIX, R_GFFN, R_GFIN, R_LOSS, R_META = 0, 1, 2, 3, 8


def _cp(sem=None, **kw):
    return pltpu.CompilerParams(dimension_semantics=sem, vmem_limit_bytes=VMEM_LIMIT, **kw)


def _dot(a, b):
    return jnp.dot(a, b, preferred_element_type=_F32)


def _dot_nt(a, b):
    return lax.dot_general(a, b, (((1,), (1,)), ((), ())), preferred_element_type=_F32)


def _dot_tn(a, b):
    return lax.dot_general(a, b, (((0,), (0,)), ((), ())), preferred_element_type=_F32)


def _sigmoid(x):
    return 0.5 * jnp.tanh(0.5 * x) + 0.5


def _dsilu(x, s):
    return s * (1.0 + x * (1.0 - s))


_GELU_C = 0.7978845608028654


def _gelu_parts(x):
    t = jnp.tanh(_GELU_C * (x + 0.044715 * (x * x * x)))
    g = 0.5 * x * (1.0 + t)
    dg = 0.5 * (1.0 + t) + 0.5 * x * (1.0 - t * t) * (_GELU_C * (1.0 + 3.0 * 0.044715 * (x * x)))
    return g, dg


def _softplus(z):
    e = jnp.exp(-jnp.abs(z))
    w = 1.0 + e
    l1p = jnp.where(w == 1.0, e, jnp.log(w) * e / jnp.where(w == 1.0, 1.0, w - 1.0))
    return jnp.maximum(z, 0.0) + l1p


def _rms_fwd(x):
    r = lax.rsqrt(jnp.mean(x * x, axis=-1, keepdims=True) + EPS)
    return x * r, r


def _rms_bwd(dyg, n, r):
    return r * (dyg - n * jnp.mean(dyg * n, axis=-1, keepdims=True))


def _full(shape):
    nd = len(shape)
    return pl.BlockSpec(shape, lambda i: (0,) * nd)


def _const(shape):
    nd = len(shape)
    return pl.BlockSpec(shape, lambda i: (0,) * nd, pipeline_mode=pl.Buffered(1))


def _carry_gather(gather, i, nt):
    @pl.when(i == 0)
    def _():
        gather.start()

    def tail():
        for j in range(3):
            @pl.when(i == max(nt - 3 + j, 0))
            def _(j=j):
                gather.forward(j)

        @pl.when(i == nt - 1)
        def _():
            gather.finish()

    return tail


def _pair_place(ref, block):
    return ref.at[block // 2, :, pl.ds(pl.multiple_of((block % 2) * WIN_B, WIN_B), WIN_B)]


def _rg_gates(xc, wr_ref, wi_ref, vec_ref):
    xcb = xc.astype(_BF)
    r = _sigmoid(_dot(xcb, wr_ref[...]) + vec_ref[R_BR:R_BR + 1, :])
    ig = _sigmoid(_dot(xcb, wi_ref[...]) + vec_ref[R_BI:R_BI + 1, :])
    nsp8 = -LRU_C * _softplus(-vec_ref[R_LAM:R_LAM + 1, :])
    la = nsp8 * r
    a = jnp.exp(la)
    th = jnp.tanh(la)
    s = jnp.sqrt(-2.0 * th / (1.0 - th))
    return r, ig, a, s, nsp8


def _conv(xbuf, vec_ref):
    acc = vec_ref[R_CONVW:R_CONVW + 1, :] * xbuf[pl.ds(5, TM), :]
    for j in range(1, 4):
        acc = acc + vec_ref[R_CONVW + j:R_CONVW + j + 1, :] * xbuf[pl.ds(5 + j, TM), :]
    return vec_ref[R_CONVB:R_CONVB + 1, :] + acc


def _dot3(m01, x):
    hi = x.astype(_BF)
    r1 = x - hi.astype(_F32)
    mid = r1.astype(_BF)
    lo = (r1 - mid.astype(_F32)).astype(_BF)
    return (_dot(m01, lo) + _dot(m01, mid)) + _dot(m01, hi)


def _chunk_dot3(m01, x):
    return jnp.concatenate([_dot3(m01, x[HC * c:HC * (c + 1), :]) for c in range(x.shape[0] // HC)], axis=0)


def _chunk_masks():
    row = lax.broadcasted_iota(jnp.int32, (HC, HC), 0)
    col = lax.broadcasted_iota(jnp.int32, (HC, HC), 1)
    return (row >= col).astype(_BF), (col >= row).astype(_BF), jnp.ones((HC, HC), _BF)


def _per_chunk_rows(x, r):
    return jnp.concatenate([jnp.broadcast_to(x[HC * c + r:HC * c + r + 1, :], (HC, x.shape[1]))
                            for c in range(TM // HC)], axis=0)


def _hg_prep(p_ref, lb, tri):
    hq = p_ref[:, pl.ds(2 * D_RG, D_HG)]
    hf = p_ref[:, pl.ds(2 * D_RG + D_HG, D_HG)]
    sq = _sigmoid(hq)
    q = hq * sq
    sg = _sigmoid(hf)
    f = lb + (1.0 - lb) * sg
    k = 1.0 - f
    b = _chunk_dot3(tri, jnp.log(f))
    bm = _per_chunk_rows(b, HC // 2 - 1)
    bl = _per_chunk_rows(b, HC - 1)
    e_q = jnp.exp(b - bm)
    e_k = jnp.exp(bm - b)
    e_b = jnp.exp(b)
    e_l = jnp.exp(bl - b)
    return dict(hq=hq, sq=sq, q=q, sg=sg, f=f, k=k, e_q=e_q, e_k=e_k, e_b=e_b, e_l=e_l,
                qd=q * e_q, kd=k * e_k, qe=q * e_b, ke=k * e_l, e_end=jnp.exp(bl))


def _mixer_fwd(h0, g_mix, w_in, wr, wi, vec, hb, g_hg, shards):
    t_pad = h0.shape[0]
    nt = t_pad // TM
    nc_t = TM // HC
    nsh = len(shards)

    def body(h_ref, gmix_ref, win_ref, wr_ref, wi_ref, vec_ref, hb_ref, ghg_ref, *rest):
        sh_refs, rest = rest[:nsh], rest[nsh:]
        pout_ref, uout_ref, y_ref, hs_ref, o_ref, sc_ref = rest[:6]
        gath_refs, rest = rest[6:6 + nsh], rest[6 + nsh:]
        xbuf, a_s, b_s, hcar, st, qd_s, kd_s, qe_s, ke_s, v_s, u_s, p_s, p_ref = rest[:13]
        i = pl.program_id(0)
        tail = _carry_gather(_Gather(sh_refs, gath_refs, rest[13:]), i, nt + 1)

        @pl.when(i == 0)
        def _():
            p_s[...] = jnp.zeros_like(p_s)

        p_ref[...] = p_s[...]

        @pl.when(i <= 1)
        def _():
            xbuf[pl.ds(0, 8), :] = jnp.zeros((8, D_RG), _F32)
            hcar[...] = jnp.zeros_like(hcar)
            st[...] = jnp.zeros_like(st)

        n_h, _ = _rms_fwd(h_ref[...])
        u = (n_h * gmix_ref[...]).astype(_BF)
        uout_ref[...] = u
        pieces = [(j, k) for j in range(4) for k in range(WIN_P // 256)]

        def project(count):
            for _ in range(count):
                j, k = pieces.pop(0)
                blk = _dot(u, win_ref[j, :, pl.ds(256 * k, 256)])
                p_s[:, pl.ds(WIN_P * j + 256 * k, 256)] = blk
                pout_ref[:, pl.ds(WIN_P * j + 256 * k, 256)] = blk

        x = p_ref[:, pl.ds(0, D_RG)]
        xbuf[pl.ds(8, TM), :] = x
        xc = _conv(xbuf, vec_ref)
        xbuf[pl.ds(0, 8), :] = x[TM - 8:, :]
        r, ig, a, s, _ = _rg_gates(xc, wr_ref, wi_ref, vec_ref)
        a_s[...] = a
        b_s[...] = s * (ig * xc)

        def step(t, h):
            h = a_s[pl.ds(t, 1), :] * h + b_s[pl.ds(t, 1), :]
            hs_ref[pl.ds(t, 1), :] = h
            return h

        hcar[pl.ds(0, 1), :] = lax.fori_loop(0, TM, step, hcar[pl.ds(0, 1), :], unroll=8)
        gel, _ = _gelu_parts(p_ref[:, pl.ds(D_RG, D_RG)])
        n, _ = _rms_fwd(gel * hs_ref[...])
        y_ref[:, pl.ds(0, D_RG)] = (n * vec_ref[R_GRG:R_GRG + 1, :]).astype(_BF)

        lb = _sigmoid(hb_ref[0:1, :] - hb_ref[1:2, :])
        tri, _, _ = _chunk_masks()
        q = _hg_prep(p_ref, lb, tri)
        for name, ref in (("qd", qd_s), ("kd", kd_s), ("qe", qe_s), ("ke", ke_s)):
            ref[...] = q[name].astype(_BF)
        v_s[...] = p_ref[:, pl.ds(2 * D_RG + 2 * D_HG, D_HG)].astype(_BF)
        e_end = q["e_end"]
        causal = (lax.broadcasted_iota(jnp.int32, (HC, HC), 0) >= lax.broadcasted_iota(jnp.int32, (HC, HC), 1))
        for c in range(nc_t):
            for h in range(NH):
                rs, cs = pl.ds(HC * c, HC), pl.ds(HD * h, HD)
                amat = jnp.where(causal, _dot_nt(qd_s[rs, cs], kd_s[rs, cs]), 0.0)
                o_ref[rs, cs] = _dot(amat.astype(_BF), v_s[rs, cs])
                u_s[NH * c + h] = _dot_tn(v_s[rs, cs], ke_s[rs, cs])
                if pieces:
                    project(1)
        assert not pieces
        for h in range(NH):
            cs = pl.ds(HD * h, HD)
            s_run = st[h]
            for c in range(nc_t):
                rs = pl.ds(HC * c, HC)
                sc_ref[c, h] = s_run
                o_ref[rs, cs] += _dot_nt(qe_s[rs, cs], s_run.astype(_BF))
                s_run = e_end[HC * c:HC * c + 1, HD * h:HD * (h + 1)] * s_run + u_s[NH * c + h]
            st[h] = s_run
        for h in range(NH):
            cs = pl.ds(HD * h, HD)
            n_o, _ = _rms_fwd(o_ref[:, cs])
            hg = p_ref[:, pl.ds(2 * D_RG + 3 * D_HG + HD * h, HD)]
            y_ref[:, pl.ds(D_RG + HD * h, HD)] = ((n_o * ghg_ref[...]) * (hg * _sigmoid(hg))).astype(_BF)

        tail()

    hbm = pl.BlockSpec(memory_space=pl.ANY)

    def proj(i):
        return jnp.minimum(i, nt - 1)

    def mixed(i):
        return jnp.maximum(i - 1, 0)

    return pl.pallas_call(
        body, name="mixer_fwd", grid=(nt + 1,),
        in_specs=[pl.BlockSpec((TM, D), lambda i: (proj(i), 0)), _full((1, D)), _const((4, D, WIN_P)),
                  _full((D_RG, D_RG)), _full((D_RG, D_RG)),
                  _full((16, D_RG)), _full((2, D_HG)), _full((1, HD))] + [hbm] * nsh,
        out_specs=[pl.BlockSpec((TM, D_IN), lambda i: (proj(i), 0)), pl.BlockSpec((TM, D), lambda i: (proj(i), 0)),
                   pl.BlockSpec((TM, D), lambda i: (mixed(i), 0)), pl.BlockSpec((TM, D_RG), lambda i: (mixed(i), 0)),
                   pl.BlockSpec((TM, D_HG), lambda i: (mixed(i), 0)),
                   pl.BlockSpec((nc_t, NH, HD, HD), lambda i: (mixed(i), 0, 0, 0))] + [hbm] * nsh,
        out_shape=[_S((t_pad, D_IN), _F32), _S((t_pad, D), _BF),
                   _S((t_pad, D), _BF), _S((t_pad, D_RG), _F32), _S((t_pad, D_HG), _F32),
                   _S((t_pad // HC, NH, HD, HD), _F32)] + [_S((N_DEV,) + s.shape, s.dtype) for s in shards],
        scratch_shapes=[pltpu.VMEM((TM + 8, D_RG), _F32), pltpu.VMEM((TM, D_RG), _F32),
                        pltpu.VMEM((TM, D_RG), _F32), pltpu.VMEM((8, D_RG), _F32),
                        pltpu.VMEM((NH, HD, HD), _F32)] + [pltpu.VMEM((TM, D_HG), _BF) for _ in range(5)]
        + [pltpu.VMEM((nc_t * NH, HD, HD), _F32), pltpu.VMEM((TM, D_IN), _F32), pltpu.VMEM((TM, D_IN), _F32)]
        + _sem_shapes(nsh),
        compiler_params=_cp(("arbitrary",)),
    )(h0, g_mix, w_in, wr, wi, vec, hb, g_hg, *shards)


def _ffn_loss(h0, y, w_out, g_ffn, w_gu, w_down, g_fin, tgt, n_valid):
    t_pad = h0.shape[0]

    def body(h_ref, y_ref, wo_ref, gffn_ref, wgu_ref, wd_ref, g_ref, t_ref,
             h1_ref, v_ref, gu_ref, act_ref, dh2_ref, dh2b_ref, loss_ref, gfin_ref):
        i = pl.program_id(0)

        @pl.when(i == 0)
        def _():
            loss_ref[...] = jnp.zeros_like(loss_ref)
            gfin_ref[...] = jnp.zeros_like(gfin_ref)

        h1 = h_ref[...] + _dot(y_ref[...], wo_ref[...])
        h1_ref[...] = h1
        n1, _ = _rms_fwd(h1)
        vb = (n1 * gffn_ref[...]).astype(_BF)
        v_ref[...] = vb
        h2 = h1
        for b in range(4):
            gate = _dot_nt(vb, wgu_ref[b])
            up = _dot_nt(vb, wgu_ref[4 + b])
            gu_ref[b] = gate
            gu_ref[4 + b] = up
            act = ((gate * _sigmoid(gate)) * up).astype(_BF)
            act_ref[b] = act
            h2 = h2 + _dot(act, wd_ref[b])
        n, r = _rms_fwd(h2)
        out = n * g_ref[...]
        row = i * TM + lax.broadcasted_iota(jnp.int32, (TM, 1), 0)
        valid = (row >= N_META) & (row < n_valid)
        err = jnp.where(valid, out - t_ref[...], 0.0)
        loss_ref[...] += (0.5 / D) * jnp.sum(err * err)
        dout = err * (1.0 / D)
        gfin_ref[...] += jnp.sum(dout * n, axis=0, keepdims=True)
        dh2 = _rms_bwd(dout * g_ref[...], n, r)
        dh2_ref[...] = dh2
        dh2b_ref[...] = dh2.astype(_BF)

    tile = pl.BlockSpec((TM, D), lambda i: (i, 0))
    return pl.pallas_call(
        body, name="ffn_loss", grid=(t_pad // TM,),
        in_specs=[tile, tile, _const((D, D)), _full((1, D)),
                  _const((N_DEV, FFB, D)), _const((4, FFB, D)), _full((1, D)), tile],
        out_specs=[tile, tile,
                   pl.BlockSpec((N_DEV, TM, FFB), lambda i: (0, i, 0)), pl.BlockSpec((4, TM, FFB), lambda i: (0, i, 0)),
                   tile, tile, _full((8, 128)), _full((1, D))],
        out_shape=[_S((t_pad, D), _F32), _S((t_pad, D), _BF),
                   _S((N_DEV, t_pad, FFB), _F32), _S((4, t_pad, FFB), _BF), _S((t_pad, D), _F32),
                   _S((t_pad, D), _BF), _S((8, 128), _F32), _S((1, D), _F32)],
        compiler_params=_cp(("arbitrary",)),
    )(h0, y, w_out, g_ffn, w_gu, w_down, g_fin, tgt)


def _ffn_bwd(dh2, dh2b, gu, h1, g_ffn, w_gu, w_down, w_out):
    t_pad = dh2.shape[0]

    def body(dh2_ref, dh2b_ref, gu_ref, h1_ref, g_ref, wgu_ref, wd_ref, wo_ref,
             dgu_ref, dh1_ref, dh1b_ref, dy_ref, gffn_ref):
        i = pl.program_id(0)

        @pl.when(i == 0)
        def _():
            gffn_ref[...] = jnp.zeros_like(gffn_ref)

        db = dh2b_ref[...]
        dv = jnp.zeros((TM, D), _F32)
        for b in range(4):
            dact = _dot_nt(db, wd_ref[b])
            gate = gu_ref[b]
            up = gu_ref[4 + b]
            sg = _sigmoid(gate)
            dgate = ((dact * up) * _dsilu(gate, sg)).astype(_BF)
            dup = (dact * (gate * sg)).astype(_BF)
            dgu_ref[b] = dgate
            dgu_ref[4 + b] = dup
            dv = dv + _dot(dgate, wgu_ref[b]) + _dot(dup, wgu_ref[4 + b])
        n, r = _rms_fwd(h1_ref[...])
        gffn_ref[...] += jnp.sum(dv * n, axis=0, keepdims=True)
        dh1 = dh2_ref[...] + _rms_bwd(dv * g_ref[...], n, r)
        dh1_ref[...] = dh1
        dh1b = dh1.astype(_BF)
        dh1b_ref[...] = dh1b
        dy_ref[...] = _dot_nt(dh1b, wo_ref[...])

    tile = pl.BlockSpec((TM, D), lambda i: (i, 0))
    return pl.pallas_call(
        body, name="ffn_bwd", grid=(t_pad // TM,),
        in_specs=[tile, tile, pl.BlockSpec((N_DEV, TM, FFB), lambda i: (0, i, 0)), tile, _full((1, D)),
                  _const((N_DEV, FFB, D)), _const((4, FFB, D)), _const((D, D))],
        out_specs=[pl.BlockSpec((N_DEV, TM, FFB), lambda i: (0, i, 0)), tile, tile, tile, _full((1, D))],
        out_shape=[_S((N_DEV, t_pad, FFB), _BF), _S((t_pad, D), _F32), _S((t_pad, D), _BF),
                   _S((t_pad, D), _F32), _S((1, D), _F32)],
        compiler_params=_cp(("arbitrary",)),
    )(dh2, dh2b, gu, h1, g_ffn, w_gu, w_down, w_out)


def _mixer_bwd(p, hs, o, sc, dy, wr, wi, vec, hb, g_hg, scatter):
    t_pad = p.shape[0]
    nt = t_pad // TM
    nc_t = TM // HC
    nsc = len(scatter)

    def rev(i):
        return nt - 1 - i

    def body(p_ref, pprev_ref, hs_ref, hprev_ref, o_ref, sc_ref, dy_ref, wr_ref, wi_ref, vec_ref, hb_ref, ghg_ref,
             *rest):
        send_refs, rest = rest[:nsc], rest[nsc:]
        dp_ref, gvec_ref, gw_ref = rest[:3]
        recv_refs, rest = rest[3:3 + nsc], rest[3 + nsc:]
        xbuf, hbuf, dbuf, a_s, g_s, ccar, dst = rest[:7]
        qd_s, kd_s, qe_s, ke_s, v_s, do_s, dqd_s, dkd_s, dqe_s, dke_s, dv_s, w_s, dend_s = rest[7:20]
        exchange = _Exchange(send_refs, [], recv_refs, rest[20:])
        i = pl.program_id(0)
        first_tile = i == nt - 1

        @pl.when(i == 0)
        def _():
            exchange.start()
            gvec_ref[...] = jnp.zeros_like(gvec_ref)
            gw_ref[...] = jnp.zeros_like(gw_ref)
            dbuf[pl.ds(TM, 8), :] = jnp.zeros((8, D_RG), _F32)
            ccar[...] = jnp.zeros_like(ccar)
            dst[...] = jnp.zeros_like(dst)

        def acc(row, val):
            gvec_ref[row:row + 1, :] += jnp.sum(val, axis=0, keepdims=True)

        keep = jnp.where(first_tile, 0.0, 1.0)
        x = p_ref[:, pl.ds(0, D_RG)]
        xbuf[pl.ds(0, 8), :] = pprev_ref[...] * keep
        xbuf[pl.ds(8, TM), :] = x
        xc = _conv(xbuf, vec_ref)
        r, ig, a, s, nsp8 = _rg_gates(xc, wr_ref, wi_ref, vec_ref)
        h = hs_ref[...]
        hbuf[pl.ds(0, 8), :] = hprev_ref[...] * keep
        hbuf[pl.ds(8, TM), :] = h
        hm1 = hbuf[pl.ds(7, TM), :]
        gr = p_ref[:, pl.ds(D_RG, D_RG)]
        gel, dgel = _gelu_parts(gr)
        n, rr = _rms_fwd(gel * h)
        dyn = dy_ref[:, pl.ds(0, D_RG)]
        acc(R_GRG, dyn * n)
        dpre = _rms_bwd(dyn * vec_ref[R_GRG:R_GRG + 1, :], n, rr)
        dp_ref[:, pl.ds(D_RG, D_RG)] = ((dpre * h) * dgel).astype(_BF)
        a_s[...] = a
        g_s[...] = dpre * gel

        def step(k, c):
            t = TM - 1 - k
            g = g_s[pl.ds(t, 1), :] + c
            g_s[pl.ds(t, 1), :] = g
            return a_s[pl.ds(t, 1), :] * g

        ccar[pl.ds(0, 1), :] = lax.fori_loop(0, TM, step, ccar[pl.ds(0, 1), :], unroll=8)
        gt = g_s[...]
        da = gt * hm1
        ixc = ig * xc
        ds = gt * ixc
        dig = (gt * s) * xc
        dxc = (gt * s) * ig
        dla = da * a - ds * ((a * a) / s)
        lam = vec_ref[R_LAM:R_LAM + 1, :]
        gvec_ref[R_LAM:R_LAM + 1, :] += jnp.sum(dla * r, axis=0, keepdims=True) * (LRU_C * _sigmoid(-lam))
        dzr = (dla * nsp8) * (r * (1.0 - r))
        dzi = dig * (ig * (1.0 - ig))
        acc(R_BR, dzr)
        acc(R_BI, dzi)
        xcb = xc.astype(_BF)
        dzrb = dzr.astype(_BF)
        dzib = dzi.astype(_BF)
        gw_ref[0] += _dot_tn(xcb, dzrb)
        gw_ref[1] += _dot_tn(xcb, dzib)
        dxc = dxc + _dot_nt(dzrb, wr_ref[...]) + _dot_nt(dzib, wi_ref[...])
        acc(R_CONVB, dxc)
        for j in range(4):
            acc(R_CONVW + j, dxc * xbuf[pl.ds(5 + j, TM), :])
        dbuf[pl.ds(0, TM), :] = dxc
        dx = vec_ref[R_CONVW + 3:R_CONVW + 4, :] * dxc
        for j in range(3):
            dx = dx + vec_ref[R_CONVW + j:R_CONVW + j + 1, :] * dbuf[pl.ds(3 - j, TM), :]
        dbuf[pl.ds(TM, 8), :] = dxc[0:8, :]
        dp_ref[:, pl.ds(0, D_RG)] = dx.astype(_BF)

        lb = _sigmoid(hb_ref[0:1, :] - hb_ref[1:2, :])
        tri, tri_rev, ones = _chunk_masks()
        q = _hg_prep(p_ref, lb, tri)
        qdb, kdb = q["qd"].astype(_BF), q["kd"].astype(_BF)
        qd_s[...] = qdb
        kd_s[...] = kdb
        qe_s[...] = q["qe"].astype(_BF)
        ke_s[...] = q["ke"].astype(_BF)
        v_s[...] = p_ref[:, pl.ds(2 * D_RG + 2 * D_HG, D_HG)].astype(_BF)
        e_end = q["e_end"]
        ghg = ghg_ref[...]
        for h in range(NH):
            cs = pl.ds(HD * h, HD)
            hg = p_ref[:, pl.ds(2 * D_RG + 3 * D_HG + HD * h, HD)]
            sh = _sigmoid(hg)
            n_o, r_o = _rms_fwd(o_ref[:, cs])
            dyh = dy_ref[:, pl.ds(D_RG + HD * h, HD)]
            dp_ref[:, pl.ds(2 * D_RG + 3 * D_HG + HD * h, HD)] = ((dyh * (n_o * ghg)) * _dsilu(hg, sh)).astype(_BF)
            dn = dyh * (hg * sh)
            gvec_ref[R_GHG:R_GHG + 1, pl.ds(0, HD)] += jnp.sum(dn * n_o, axis=0, keepdims=True)
            do_s[:, cs] = _rms_bwd(dn * ghg, n_o, r_o).astype(_BF)
        causal = (lax.broadcasted_iota(jnp.int32, (HC, HC), 0) >= lax.broadcasted_iota(jnp.int32, (HC, HC), 1))
        for c in range(nc_t):
            for h in range(NH):
                rs, cs = pl.ds(HC * c, HC), pl.ds(HD * h, HD)
                qd_c, kd_c, do_c = qd_s[rs, cs], kd_s[rs, cs], do_s[rs, cs]
                amat = jnp.where(causal, _dot_nt(qd_c, kd_c), 0.0).astype(_BF)
                da_m = jnp.where(causal, _dot_nt(do_c, v_s[rs, cs]), 0.0).astype(_BF)
                dqd_s[rs, cs] = _dot(da_m, kd_c)
                dkd_s[rs, cs] = _dot_tn(da_m, qd_c)
                dqe_s[rs, cs] = _dot(do_c, sc_ref[c, h].astype(_BF))
                dv_s[rs, cs] = _dot_tn(amat, do_c)
                w_s[NH * c + h] = _dot_tn(do_c, qe_s[rs, cs])
        for h in range(NH):
            cs = pl.ds(HD * h, HD)
            d_run = dst[h]
            for c in reversed(range(nc_t)):
                rs = pl.ds(HC * c, HC)
                d_b = d_run.astype(_BF)
                dke_s[rs, cs] = _dot(v_s[rs, cs], d_b)
                dp_ref[rs, pl.ds(2 * D_RG + 2 * D_HG + HD * h, HD)] = (
                    dv_s[rs, cs] + _dot_nt(ke_s[rs, cs], d_b)).astype(_BF)
                dend_s[pl.ds(c, 1), cs] = jnp.sum(sc_ref[c, h] * d_run, axis=0, keepdims=True)
                d_run = w_s[NH * c + h] + e_end[HC * c:HC * c + 1, HD * h:HD * (h + 1)] * d_run
            dst[h] = d_run
        dqd, dkd, dqe, dke = dqd_s[...], dkd_s[...], dqe_s[...], dke_s[...]
        dq = dqd * q["e_q"] + dqe * q["e_b"]
        dk = dkd * q["e_k"] + dke * q["e_l"]
        dkeke = dke * q["ke"]
        db = dqd * qdb.astype(_F32) - dkd * kdb.astype(_F32) + dqe * q["qe"] - dkeke
        d_end = jnp.concatenate([jnp.broadcast_to(dend_s[pl.ds(c, 1), :], (HC, D_HG)) for c in range(nc_t)], axis=0)
        dlf = _chunk_dot3(tri_rev, db) + _chunk_dot3(ones, dkeke) + d_end * e_end
        df = dlf / q["f"] - dk
        sg = q["sg"]
        gvec_ref[R_HB0:R_HB0 + 1, :] += jnp.sum(df * (1.0 - sg), axis=0, keepdims=True)
        dp_ref[:, pl.ds(2 * D_RG, D_HG)] = (dq * _dsilu(q["hq"], q["sq"])).astype(_BF)
        dp_ref[:, pl.ds(2 * D_RG + D_HG, D_HG)] = ((df * (1.0 - lb)) * (sg * (1.0 - sg))).astype(_BF)

        @pl.when(i == nt - 1)
        def _():
            glb = gvec_ref[R_HB0:R_HB0 + 1, :] * (lb * (1.0 - lb))
            gvec_ref[R_HB0:R_HB0 + 1, :] = glb
            gvec_ref[R_HB1:R_HB1 + 1, :] = -glb
            exchange.finish()

    hbm = pl.BlockSpec(memory_space=pl.ANY)
    return pl.pallas_call(
        body, name="mixer_bwd", grid=(nt,),
        in_specs=[pl.BlockSpec((TM, D_IN), lambda i: (rev(i), 0)),
                  pl.BlockSpec((8, D_RG), lambda i: (jnp.maximum(rev(i) * (TM // 8) - 1, 0), 0)),
                  pl.BlockSpec((TM, D_RG), lambda i: (rev(i), 0)),
                  pl.BlockSpec((8, D_RG), lambda i: (jnp.maximum(rev(i) * (TM // 8) - 1, 0), 0)),
                  pl.BlockSpec((TM, D_HG), lambda i: (rev(i), 0)),
                  pl.BlockSpec((nc_t, NH, HD, HD), lambda i: (rev(i), 0, 0, 0)),
                  pl.BlockSpec((TM, D), lambda i: (rev(i), 0)),
                  _full((D_RG, D_RG)), _full((D_RG, D_RG)), _full((16, D_RG)), _full((2, D_HG)), _full((1, HD))]
        + [hbm] * nsc,
        out_specs=[pl.BlockSpec((TM, D_IN), lambda i: (rev(i), 0)), _full((16, D_RG)), _full((2, D_RG, D_RG))]
        + [hbm] * nsc,
        out_shape=[_S((t_pad, D_IN), _BF), _S((16, D_RG), _F32), _S((2, D_RG, D_RG), _F32)]
        + [_S(s.shape, s.dtype) for s in scatter],
        scratch_shapes=[pltpu.VMEM((TM + 8, D_RG), _F32), pltpu.VMEM((TM + 8, D_RG), _F32),
                        pltpu.VMEM((TM + 8, D_RG), _F32), pltpu.VMEM((TM, D_RG), _F32),
                        pltpu.VMEM((TM, D_RG), _F32), pltpu.VMEM((8, D_RG), _F32),
                        pltpu.VMEM((NH, HD, HD), _F32)]
        + [pltpu.VMEM((TM, D_HG), _BF) for _ in range(6)] + [pltpu.VMEM((TM, D_HG), _F32) for _ in range(5)]
        + [pltpu.VMEM((nc_t * NH, HD, HD), _F32), pltpu.VMEM((8, D_HG), _F32)] + _sem_shapes(nsc),
        compiler_params=_cp(("arbitrary",)),
    )(p, p, hs, hs, o, sc, dy, wr, wi, vec, hb, g_hg, *scatter)


def _inproj_bwd_send(dp, w_in, h0, dh1, g_mix, u, order, gffn, gfin, loss, to_all):
    t_pad = dp.shape[0]
    rb = TM
    n_steps = N_DEV + t_pad // rb
    na = len(to_all)

    def body(order_ref, dpc_ref, dpr_ref, u_ref, w_ref, h_ref, dh1_ref, g_ref, gffn_ref, gfin_ref, loss_ref, *rest):
        all_in = rest[:na]
        dh0_ref, recv_ref = rest[na:na + 2]
        all_out = rest[na + 2:2 * na + 2]
        alla_ref = rest[2 * na + 2]
        buf, pack, blk_send, blk_recv, blk_local = rest[2 * na + 3:2 * na + 8]
        exchange = _Exchange([], all_in, all_out, rest[2 * na + 8:2 * na + 11])
        last = _Exchange([], [pack], [alla_ref], rest[2 * na + 11:])
        s = pl.program_id(0)
        x, y, c = _coords()
        me = 4 * x + 2 * y + c

        def send(step):
            r = _SEND_ORDER[step]
            return pltpu.make_async_remote_copy(
                src_ref=buf.at[step], dst_ref=recv_ref.at[me], send_sem=blk_send.at[step], recv_sem=blk_recv.at[r - 1],
                device_id=(x ^ (r >> 2), y ^ ((r >> 1) & 1), c ^ (r & 1)), device_id_type=_MESH)

        @pl.when(s == 0)
        def _():
            exchange.start()
            pack[...] = jnp.zeros_like(pack)

        @pl.when(s < N_DEV)
        def _():
            buf[s] = _dot_tn(u_ref[...], dpc_ref[...]).astype(_BF)

            for step in range(N_DEV - 1):
                @pl.when(s == step)
                def _(step=step):
                    send(step).start()

        @pl.when(s >= N_DEV)
        def _():
            du = jnp.zeros((rb, D), _F32)
            for j in range(4):
                du = du + _dot_nt(dpr_ref[:, WIN_P * j:WIN_P * (j + 1)], w_ref[j])
            n, r = _rms_fwd(h_ref[...])
            pack[R_GMIX:R_GMIX + 1, :] += jnp.sum(du * n, axis=0, keepdims=True)
            dh0 = dh1_ref[...] + _rms_bwd(du * g_ref[...], n, r)
            dh0_ref[...] = dh0

            @pl.when(s == N_DEV)
            def _():
                pack[R_META:R_META + N_META, :] = dh0[0:N_META, :]

        @pl.when(s == n_steps - 1)
        def _():
            pack[R_GFFN:R_GFFN + 1, :] = gffn_ref[...]
            pack[R_GFIN:R_GFIN + 1, :] = gfin_ref[...]
            pack[R_LOSS:R_LOSS + 1, pl.ds(0, 128)] = loss_ref[0:1, :]
            last.start()
            mine = pltpu.make_async_copy(buf.at[N_DEV - 1], recv_ref.at[me], blk_local.at[0])
            mine.start()
            for step in range(N_DEV - 1):
                send(step).wait_send()
            for r in range(1, N_DEV):
                px, py, pc = x ^ (r >> 2), y ^ ((r >> 1) & 1), c ^ (r & 1)
                pltpu.make_async_remote_copy(
                    src_ref=buf.at[0], dst_ref=recv_ref.at[4 * px + 2 * py + pc], send_sem=blk_send.at[0],
                    recv_sem=blk_recv.at[r - 1], device_id=(px, py, pc), device_id_type=_MESH).wait_recv()
            mine.wait()
            exchange.finish()
            last.finish()

    hbm = pl.BlockSpec(memory_space=pl.ANY)
    rows = pl.BlockSpec((rb, D), lambda s, order: (jnp.maximum(s - N_DEV, 0), 0))
    one = pl.BlockSpec((1, D), lambda s, order: (0, 0))
    res = pl.pallas_call(
        body, name="inproj_bwd_send",
        grid_spec=pltpu.PrefetchScalarGridSpec(
            num_scalar_prefetch=1, grid=(n_steps,),
            in_specs=[pl.BlockSpec((t_pad, WIN_B), lambda s, order: (0, order[jnp.minimum(s, N_DEV - 1)])),
                      pl.BlockSpec((rb, D_IN), lambda s, order: (jnp.maximum(s - N_DEV, 0), 0)),
                      pl.BlockSpec((t_pad, D), lambda s, order: (0, 0), pipeline_mode=pl.Buffered(1)),
                      pl.BlockSpec((4, D, WIN_P), lambda s, order: (0, 0, 0), pipeline_mode=pl.Buffered(1)),
                      rows, rows, one, one, one, pl.BlockSpec((8, 128), lambda s, order: (0, 0))] + [hbm] * na,
            out_specs=[rows] + [hbm] * (na + 2),
            scratch_shapes=[pltpu.VMEM((N_DEV, D, WIN_B), _BF), pltpu.VMEM((24, D), _F32),
                            pltpu.SemaphoreType.DMA((N_DEV - 1,)), pltpu.SemaphoreType.DMA((N_DEV - 1,)),
                            pltpu.SemaphoreType.DMA((1,))] + _sem_shapes(na) + _sem_shapes(1)),
        out_shape=[_S((t_pad, D), _F32), _S((N_DEV, D, WIN_B), _BF)]
        + [_S((N_DEV,) + g.shape, g.dtype) for g in to_all] + [_S((N_DEV, 24, D), _F32)],
        compiler_params=_cp(("arbitrary",)),
    )(order, dp, dp, u, w_in, h0, dh1, g_mix, gffn, gfin, loss, *to_all)
    return res


def _wgrad(name, a, b, a_spec, b_spec, n_blocks, out_block, scatter=()):
    nsc = len(scatter)

    def body(a_ref, b_ref, *rest):
        o_ref = rest[nsc]
        j = pl.program_id(0)
        if nsc:
            exchange = _Exchange(rest[:nsc], [], rest[nsc + 1:2 * nsc + 1], rest[2 * nsc + 1:])

            @pl.when(j == 0)
            def _():
                exchange.start()

        av = a_ref[0] if len(a_ref.shape) == 3 else a_ref[...]
        bv = b_ref[0] if len(b_ref.shape) == 3 else b_ref[...]
        o_ref[0] = _dot_tn(av, bv).astype(_BF)

        if nsc:
            @pl.when(j == n_blocks - 1)
            def _():
                exchange.finish()

    hbm = pl.BlockSpec(memory_space=pl.ANY)
    res = pl.pallas_call(
        body, name=name, grid=(n_blocks,),
        in_specs=[a_spec, b_spec] + [hbm] * nsc,
        out_specs=[pl.BlockSpec((1,) + out_block, lambda j: (j, 0, 0))] + [hbm] * nsc,
        out_shape=[_S((n_blocks,) + out_block, _BF)] + [_S(s.shape, s.dtype) for s in scatter],
        scratch_shapes=_sem_shapes(nsc) if nsc else [],
        compiler_params=_cp(("arbitrary",)),
    )(a, b, *scatter)
    return res if nsc else res[0]


def _coords():
    return lax.axis_index("x"), lax.axis_index("y"), lax.axis_index("c")


def _sem_shapes(na):
    return [pltpu.SemaphoreType.DMA((7 * na,)), pltpu.SemaphoreType.DMA((7 * na,)), pltpu.SemaphoreType.DMA((na,))]


class _Gather:
    def __init__(self, srcs, outs, sems, place=None):
        self.srcs, self.outs = srcs, outs
        self.send_sems, self.recv_sems, self.local_sems = sems
        self.place = place if place is not None else (lambda ref, block: ref.at[block])
        self.na = len(srcs)
        x, y, c = _coords()
        self.pos = (x, y, c)
        self.me = 4 * x + 2 * y + c
        self.sibling = (x, y, 1 - c)
        self.chips = [(1 - x, y), (x, 1 - y), (1 - x, 1 - y)]

    @staticmethod
    def _slot(px, py, pc):
        return 4 * px + 2 * py + pc

    def _copy(self, a, k, block, to, own=False):
        dst = self.place(self.outs[a], block)
        return pltpu.make_async_remote_copy(
            src_ref=self.srcs[a] if own else dst, dst_ref=dst,
            send_sem=self.send_sems.at[7 * a + k], recv_sem=self.recv_sems.at[7 * a + k],
            device_id=to, device_id_type=_MESH)

    def _mine(self, a):
        return pltpu.make_async_copy(self.srcs[a], self.place(self.outs[a], self.me), self.local_sems.at[a])

    def _first(self):
        c = self.pos[2]
        cps = []
        for a in range(self.na):
            cps.append(self._copy(a, 0, self.me, self.sibling, own=True))
            cps += [self._copy(a, 1 + j, self.me, (*chip, c), own=True) for j, chip in enumerate(self.chips)]
        return cps

    def _passed(self):
        c = self.pos[2]
        return [self._copy(a, 4 + j, self._slot(*chip, c), self.sibling)
                for j, chip in enumerate(self.chips) for a in range(self.na)]

    def start(self):
        for a in range(self.na):
            self._mine(a).start()
        for cp in self._first():
            cp.start()

    def forward(self, j):
        c = self.pos[2]
        chip = self.chips[j]
        for a in range(self.na):
            self._copy(a, 1 + j, self._slot(*chip, c), self.pos).wait_recv()
            self._copy(a, 4 + j, self._slot(*chip, c), self.sibling).start()

    def wait_sibling(self):
        x, y, c = self.pos
        for a in range(self.na):
            self._copy(a, 0, self._slot(x, y, 1 - c), self.pos).wait_recv()

    def wait_passed(self, j):
        c = self.pos[2]
        for a in range(self.na):
            self._copy(a, 4 + j, self._slot(*self.chips[j], 1 - c), self.pos).wait_recv()

    def finish_sends(self):
        for cp in self._first() + self._passed():
            cp.wait_send()
        for a in range(self.na):
            self._mine(a).wait()

    def finish(self):
        self.wait_sibling()
        for j in range(3):
            self.wait_passed(j)
        self.finish_sends()


class _Exchange:
    def __init__(self, scatter, gather, outs, sems):
        self.ins = list(scatter) + list(gather)
        self.ns, self.na = len(scatter), len(scatter) + len(gather)
        self.outs = outs
        self.send_sems, self.recv_sems, self.local_sems = sems
        x, y, c = _coords()
        self.pos = (x, y, c)
        self.me = 4 * x + 2 * y + c

    def _peer(self, r):
        x, y, c = self.pos
        return x ^ (r >> 2), y ^ ((r >> 1) & 1), c ^ (r & 1)

    def _src(self, a, block):
        return self.ins[a].at[block] if a < self.ns else self.ins[a]

    def _local(self, a):
        return pltpu.make_async_copy(self._src(a, self.me), self.outs[a].at[self.me], self.local_sems.at[a])

    def _send(self, a, r):
        px, py, pc = self._peer(r)
        return pltpu.make_async_remote_copy(
            src_ref=self._src(a, 4 * px + 2 * py + pc), dst_ref=self.outs[a].at[self.me],
            send_sem=self.send_sems.at[7 * a + r - 1], recv_sem=self.recv_sems.at[7 * a + r - 1],
            device_id=(px, py, pc), device_id_type=_MESH)

    def _recv(self, a, r):
        px, py, pc = self._peer(r)
        return pltpu.make_async_remote_copy(
            src_ref=self._src(a, self.me), dst_ref=self.outs[a].at[4 * px + 2 * py + pc],
            send_sem=self.send_sems.at[7 * a + r - 1], recv_sem=self.recv_sems.at[7 * a + r - 1],
            device_id=(px, py, pc), device_id_type=_MESH)

    def start(self):
        for a in range(self.na):
            self._local(a).start()
        for r in range(1, N_DEV):
            for a in range(self.na):
                self._send(a, r).start()

    def finish(self):
        for r in range(1, N_DEV):
            for a in range(self.na):
                self._recv(a, r).wait_recv()
        for r in range(1, N_DEV):
            for a in range(self.na):
                self._send(a, r).wait_send()
        for a in range(self.na):
            self._local(a).wait()


def _prologue(x, tgt, small_l, w_in_l, cast_f32):
    seq = x.shape[0]
    nx = seq // TM
    rest_rows = seq - nx * TM
    nt = nx + 1
    nc = len(cast_f32)
    body_rows = TM - N_META
    assert nx >= 1 and rest_rows % 8 == 0 and rest_rows <= body_rows
    x_rest, t_rest = x[nx * TM:], tgt[nx * TM:]

    def last_tile_body(rest_ref):
        parts = ([rest_ref[...]] if rest_rows else []) + (
            [jnp.zeros((body_rows - rest_rows, D), _F32)] if body_rows > rest_rows else [])
        return parts[0] if len(parts) == 1 else jnp.concatenate(parts, axis=0)

    def body(xm_ref, xp_ref, tm_ref, tp_ref, *rest):
        if rest_rows:
            xr_ref, tr_ref, rest = rest[0], rest[1], rest[2:]
        else:
            xr_ref = tr_ref = None
        s_ref, w_ref, rest = rest[0], rest[1], rest[2:]
        cins = rest[:nc]
        h0_ref, tgt_ref, small_ref, wg_ref = rest[nc:nc + 4]
        couts = rest[nc + 4:2 * nc + 4]
        s_stage, w_stage, meta, msem = rest[2 * nc + 4:2 * nc + 8]
        g_s = _Gather([s_stage], [small_ref], rest[2 * nc + 8:2 * nc + 11])
        g_w = _Gather([w_stage], [wg_ref], rest[2 * nc + 11:], place=_pair_place)
        s = pl.program_id(0)
        i = (s + 1) % nt

        @pl.when(s == 0)
        def _():
            s_stage[...] = s_ref[...]
            w_stage[...] = w_ref[...].astype(_BF)
            g_s.start()
            g_w.start()
            meta[...] = jnp.zeros_like(meta)
            for a in range(nc):
                couts[a][...] = cins[a][...].astype(_BF)

        @pl.when(s == nt - 1)
        def _():
            for j in range(3):
                g_s.forward(j)
            g_s.finish()
            cps = [pltpu.make_async_copy(small_ref.at[k, pl.ds(0, N_META), :], meta.at[:, pl.ds(128 * k, 128)],
                                         msem.at[k]) for k in range(N_DEV)]
            for cp in cps:
                cp.start()
            for cp in cps:
                cp.wait()
            for j in range(3):
                g_w.forward(j)
            g_w.finish()

        has_x = i < nx
        h0_ref[pl.ds(0, N_META), :] = jnp.where(i == 0, meta[...], xp_ref[...])
        h0_ref[pl.ds(N_META, body_rows), :] = jnp.where(has_x, xm_ref[pl.ds(0, body_rows), :], last_tile_body(xr_ref))
        tgt_ref[pl.ds(0, N_META), :] = jnp.where(i == 0, 0.0, tp_ref[...])
        tgt_ref[pl.ds(N_META, body_rows), :] = jnp.where(has_x, tm_ref[pl.ds(0, body_rows), :], last_tile_body(tr_ref))

    def tile_of(s):
        return (s + 1) % nt

    hbm = pl.BlockSpec(memory_space=pl.ANY)
    main = pl.BlockSpec((TM, D), lambda s: (jnp.minimum(tile_of(s), nx - 1), 0))
    prev = pl.BlockSpec((N_META, D), lambda s: (jnp.maximum(tile_of(s) * (TM // N_META) - 1, 0), 0))
    tile = pl.BlockSpec((TM, D), lambda s: (tile_of(s), 0))
    rests = [x_rest, t_rest] if rest_rows else []
    return pl.pallas_call(
        body, name="prologue", grid=(nt,),
        in_specs=[main, prev, main, prev] + [_const(r.shape) for r in rests]
        + [_const(small_l.shape), _const(w_in_l.shape)] + [_const(l.shape) for l in cast_f32],
        out_specs=[tile, tile, hbm, hbm] + [_full(l.shape) for l in cast_f32],
        out_shape=[_S((nt * TM, D), _F32), _S((nt * TM, D), _F32), _S((N_DEV,) + small_l.shape, _F32),
                   _S((4, D, WIN_P), _BF)] + [_S(l.shape, _BF) for l in cast_f32],
        scratch_shapes=[pltpu.VMEM(small_l.shape, _F32), pltpu.VMEM(w_in_l.shape, _BF), pltpu.VMEM((N_META, D), _F32),
                        pltpu.SemaphoreType.DMA((N_DEV,))] + _sem_shapes(1) + _sem_shapes(1),
        compiler_params=_cp(("arbitrary",)),
    )(x, x, tgt, tgt, *rests, small_l, w_in_l, *cast_f32)


def _adamw_math(w, g, m, v):
    m2 = ADAM_B1 * m + (1.0 - ADAM_B1) * g
    v2 = ADAM_B2 * v + (1.0 - ADAM_B2) * (g * g)
    m_hat = m2 / (1.0 - ADAM_B1 ** ADAM_STEP)
    v_hat = v2 / (1.0 - ADAM_B2 ** ADAM_STEP)
    delta = -ADAM_LR * (m_hat / (jnp.sqrt(v_hat) + ADAM_EPS) + ADAM_WD * w)
    return delta, m2, v2


def _adamw_big(name, recv, w, m, v, rows):
    r_all, c_all = w.shape

    def body(r_ref, w_ref, m_ref, v_ref, g_out, d_out, m_out, v_out):
        g = r_ref[0].astype(_F32)
        for k in range(1, N_DEV):
            g = g + r_ref[k].astype(_F32)
        delta, m2, v2 = _adamw_math(w_ref[...], g, m_ref[...], v_ref[...])
        g_out[...] = g
        d_out[...] = delta
        m_out[...] = m2
        v_out[...] = v2

    tile = pl.BlockSpec((rows, c_all), lambda i: (i, 0))
    return pl.pallas_call(
        body, name=name, grid=(r_all // rows,),
        in_specs=[pl.BlockSpec((N_DEV, rows, c_all), lambda i: (0, i, 0)), tile, tile, tile],
        out_specs=[tile] * 4,
        out_shape=[_S(w.shape, _F32)] * 4,
        compiler_params=_cp(("arbitrary",)),
    )(recv, w, m, v)


def _adamw_small(gathered, slices, wmv):
    ng, npar = len(gathered), len(slices)

    def body(*refs):
        g_refs = refs[:ng]
        wmv_refs = refs[ng:ng + 3 * npar]
        outs = refs[ng + 3 * npar:]
        for i, (ai, r0, nr, c0, ncol) in enumerate(slices):
            g = g_refs[ai][0, pl.ds(r0, nr), pl.ds(c0, ncol)].astype(_F32)
            for k in range(1, N_DEV):
                g = g + g_refs[ai][k, pl.ds(r0, nr), pl.ds(c0, ncol)].astype(_F32)
            w_ref, m_ref, v_ref = wmv_refs[3 * i:3 * i + 3]
            delta, m2, v2 = _adamw_math(w_ref[...], g, m_ref[...], v_ref[...])
            outs[4 * i][...] = g
            outs[4 * i + 1][...] = delta
            outs[4 * i + 2][...] = m2
            outs[4 * i + 3][...] = v2
        total = g_refs[0][0, pl.ds(R_LOSS, 1), pl.ds(0, 128)]
        for k in range(1, N_DEV):
            total = total + g_refs[0][k, pl.ds(R_LOSS, 1), pl.ds(0, 128)]
        outs[4 * npar][...] = total

    flat = [t for trip in wmv for t in trip]
    out_shape = []
    for w, _, _ in wmv:
        out_shape += [_S(w.shape, _F32)] * 4
    out_shape.append(_S((1, 128), _F32))
    return pl.pallas_call(
        body, name="adamw_small", out_shape=out_shape,
        compiler_params=pltpu.CompilerParams(vmem_limit_bytes=VMEM_LIMIT),
    )(*gathered, *flat)


def _block_diag(w):
    eye = jnp.eye(8, dtype=w.dtype)
    return (w[:, :, None, :] * eye[:, None, :, None]).reshape(D_RG, D_RG)


def _diag_blocks(g):
    return jnp.concatenate([g[64 * h:64 * (h + 1), 64 * h:64 * (h + 1)] for h in range(8)], axis=0)


def _local_step(h0, tgt_p, n_valid, g_mix, w_in, vec, wr, wi, hb, g_hg, w_out_l, g_ffn, w_gu_l, w_down_l, g_fin):
    t_pad = h0.shape[0]
    me = 4 * lax.axis_index("x") + 2 * lax.axis_index("y") + lax.axis_index("c")
    p, u, y, hs, o, sc, w_out, w_gu, w_down = _mixer_fwd(h0, g_mix, w_in, wr, wi, vec, hb, g_hg,
                                                         [w_out_l, w_gu_l, w_down_l])
    w_out = w_out.reshape(D, D)
    w_down = w_down.reshape(4, FFB, D)
    h1, v, gu, act, dh2, dh2b, loss, gfin = _ffn_loss(h0, y, w_out, g_ffn, w_gu, w_down, g_fin, tgt_p, n_valid)

    dgu, dh1, dh1b, dy, gffn = _ffn_bwd(dh2, dh2b, gu, h1, g_ffn, w_gu, w_down, w_out)
    g_wdown = _wgrad("wgrad_down", act, dh2b, pl.BlockSpec((1, t_pad, FFB), lambda j: (j, 0, 0)),
                     pl.BlockSpec((t_pad, D), lambda j: (0, 0)), 4, (FFB, D))
    g_wgu, r_wdown = _wgrad("wgrad_gate_up", dgu, v, pl.BlockSpec((1, t_pad, FFB), lambda j: (j, 0, 0)),
                            pl.BlockSpec((t_pad, D), lambda j: (0, 0)), N_DEV, (FFB, D),
                            scatter=[g_wdown.reshape(N_DEV, D_FF // N_DEV, D)])
    g_wout = _wgrad("wgrad_out", y, dh1b, pl.BlockSpec((t_pad, D // N_DEV), lambda j: (0, j)),
                    pl.BlockSpec((t_pad, D), lambda j: (0, 0)), N_DEV, (D // N_DEV, D))
    dp, gvec, gw, r_wgu, r_wout = _mixer_bwd(p, hs, o, sc, dy, wr, wi, vec, hb, g_hg, [g_wgu, g_wout])
    pack_c = jnp.concatenate([_diag_blocks(gw[0]), _diag_blocks(gw[1])], axis=1).astype(_BF)
    order = (me ^ jnp.array(_SEND_ORDER, jnp.int32)).astype(jnp.int32)
    dh0, r_win, all_b, all_c, all_a = _inproj_bwd_send(dp, w_in, h0, dh1, g_mix, u, order, gffn, gfin, loss,
                                                       [gvec, pack_c])
    return dh0, (r_win, r_wgu, r_wout, r_wdown), (all_a, all_b, all_c)


def kernel(x, meta_tokens, mix_norm_g, w_in, conv_w, conv_b, w_rgate, b_rgate, w_igate, b_igate, lru_lambda, rg_norm_g, hg_lower_bound, hg_norm_g, w_out, ffn_norm_g, w_gate_up, w_down, final_norm_g, loss_target, m_meta_tokens, m_mix_norm_g, m_w_in, m_conv_w, m_conv_b, m_w_rgate, m_b_rgate, m_w_igate, m_b_igate, m_lru_lambda, m_rg_norm_g, m_hg_lower_bound, m_hg_norm_g, m_w_out, m_ffn_norm_g, m_w_gate_up, m_w_down, m_final_norm_g, v_meta_tokens, v_mix_norm_g, v_w_in, v_conv_w, v_conv_b, v_w_rgate, v_b_rgate, v_w_igate, v_b_igate, v_lru_lambda, v_rg_norm_g, v_hg_lower_bound, v_hg_norm_g, v_w_out, v_ffn_norm_g, v_w_gate_up, v_w_down, v_final_norm_g):
    seq = x.shape[1]
    me = 4 * lax.axis_index("x") + 2 * lax.axis_index("y") + lax.axis_index("c")

    n_valid = N_META + seq
    small_l = jnp.concatenate([meta_tokens, jnp.pad(conv_w[0], ((0, 4), (0, 64)))], axis=0)
    h0, tgt_p, small_g, w_in_g, w_gu_l, w_out_l, w_down_l = _prologue(
        x[0], loss_target[0], small_l, w_in[0], [w_gate_up[0].T, w_out[0], w_down[0]])
    conv_w_full = jnp.transpose(small_g[:, N_META:N_META + 4, :64], (1, 0, 2)).reshape(4, D_RG)
    vec = jnp.concatenate([conv_b, b_rgate, b_igate, lru_lambda, rg_norm_g, jnp.zeros((3, D_RG), _F32),
                           conv_w_full, jnp.zeros((4, D_RG), _F32)], axis=0)
    wr = _block_diag(w_rgate[0]).astype(_BF)
    wi = _block_diag(w_igate[0]).astype(_BF)

    dh0, (r_win, r_wgu, r_wout, r_wdown), (all_a, all_b, all_c) = _local_step(
        h0, tgt_p, n_valid, mix_norm_g, w_in_g, vec, wr, wi, hg_lower_bound, hg_norm_g,
        w_out_l, ffn_norm_g, w_gu_l, w_down_l, final_norm_g.reshape(1, D))
    grad_x = dh0[N_META:N_META + seq][None]

    outs = {}
    outs["w_in"] = _adamw_big("adamw_w_in", r_win, w_in[0], m_w_in[0], v_w_in[0], 256)
    outs["w_gate_up"] = [r.T for r in _adamw_big("adamw_w_gate_up", r_wgu, w_gate_up[0].T, m_w_gate_up[0].T,
                                                 v_w_gate_up[0].T, 176)]
    outs["w_out"] = _adamw_big("adamw_w_out", r_wout, w_out[0], m_w_out[0], v_w_out[0], 128)
    outs["w_down"] = _adamw_big("adamw_w_down", r_wdown, w_down[0], m_w_down[0], v_w_down[0], 176)

    meta_part = lax.dynamic_slice_in_dim(all_a[:, R_META:R_META + N_META, :], me * 128, 128, axis=2)
    convw_part = lax.dynamic_slice_in_dim(all_b[:, R_CONVW:R_CONVW + 4, :], me * 64, 64, axis=2)
    gathered = [all_a, all_b, all_c, meta_part, convw_part]
    small_params = [
        ("meta_tokens", (3, 0, N_META, 0, 128), (meta_tokens, m_meta_tokens, v_meta_tokens), (N_META, 128)),
        ("mix_norm_g", (0, R_GMIX, 1, 0, D), (mix_norm_g, m_mix_norm_g, v_mix_norm_g), (1, D)),
        ("conv_w", (4, 0, 4, 0, 64), (conv_w, m_conv_w, v_conv_w), (4, 64)),
        ("conv_b", (1, R_CONVB, 1, 0, D_RG), (conv_b, m_conv_b, v_conv_b), (1, D_RG)),
        ("w_rgate", (2, 0, 512, 0, 64), (w_rgate, m_w_rgate, v_w_rgate), (512, 64)),
        ("b_rgate", (1, R_BR, 1, 0, D_RG), (b_rgate, m_b_rgate, v_b_rgate), (1, D_RG)),
        ("w_igate", (2, 0, 512, 64, 64), (w_igate, m_w_igate, v_w_igate), (512, 64)),
        ("b_igate", (1, R_BI, 1, 0, D_RG), (b_igate, m_b_igate, v_b_igate), (1, D_RG)),
        ("lru_lambda", (1, R_LAM, 1, 0, D_RG), (lru_lambda, m_lru_lambda, v_lru_lambda), (1, D_RG)),
        ("rg_norm_g", (1, R_GRG, 1, 0, D_RG), (rg_norm_g, m_rg_norm_g, v_rg_norm_g), (1, D_RG)),
        ("hg_lower_bound", (1, R_HB0, 2, 0, D_HG), (hg_lower_bound, m_hg_lower_bound, v_hg_lower_bound), (2, D_HG)),
        ("hg_norm_g", (1, R_GHG, 1, 0, HD), (hg_norm_g, m_hg_norm_g, v_hg_norm_g), (1, HD)),
        ("ffn_norm_g", (0, R_GFFN, 1, 0, D), (ffn_norm_g, m_ffn_norm_g, v_ffn_norm_g), (1, D)),
        ("final_norm_g", (0, R_GFIN, 1, 0, D), (final_norm_g, m_final_norm_g, v_final_norm_g), (1, D)),
    ]
    res = _adamw_small(gathered, [s[1] for s in small_params],
                       [tuple(t.reshape(s[3]) for t in s[2]) for s in small_params])
    for i, s in enumerate(small_params):
        outs[s[0]] = [r.reshape(s[2][0].shape) for r in res[4 * i:4 * i + 4]]
    for n, ref in (("w_in", w_in), ("w_gate_up", w_gate_up), ("w_out", w_out), ("w_down", w_down)):
        outs[n] = [r.reshape(ref.shape) for r in outs[n]]

    loss_all = res[4 * len(small_params)][0, 0]
    order = ["meta_tokens", "mix_norm_g", "w_in", "conv_w", "conv_b", "w_rgate", "b_rgate", "w_igate", "b_igate",
             "lru_lambda", "rg_norm_g", "hg_lower_bound", "hg_norm_g", "w_out", "ffn_norm_g", "w_gate_up", "w_down",
             "final_norm_g"]
    return (loss_all, grad_x, *[outs[n][0] for n in order], *[outs[n][1] for n in order],
            *[outs[n][2] for n in order], *[outs[n][3] for n in order])
```

```python
import functools

import jax
import jax.numpy as jnp
from jax import lax
from jax.experimental import pallas as pl
from jax.experimental.pallas import tpu as pltpu

_BF = jnp.bfloat16
_F32 = jnp.float32
_S = jax.ShapeDtypeStruct
_MESH = pl.DeviceIdType.MESH

N_DEV = 8
N_META = 16
D = 1024
D_RG = 512
D_HG = 512
HD = 128
NH = D_HG // HD
D_IN = 3072
D_FF = 2816
FFB = D_FF // 4
WIN_B = D_IN // N_DEV
WIN_P = 2 * WIN_B
WDOWN_A = 256
EPS = 1e-6
LRU_C = 8.0
TM = 320
HC = 64
VMEM_LIMIT = 62 * 1024 * 1024

ADAM_LR = 0.001
ADAM_B1 = 0.9
ADAM_B2 = 0.999
ADAM_EPS = 1e-08
ADAM_WD = 0.01
ADAM_STEP = 10

_SEND_ORDER = (6, 4, 2, 7, 5, 3, 1, 0)

R_CONVB, R_BR, R_BI, R_LAM, R_GRG, R_HB0, R_HB1, R_GHG, R_CONVW = 0, 1, 2, 3, 4, 5, 6, 7, 8
R_GMIX, R_GFFN, R_GFIN, R_LOSS, R_META = 0, 1, 2, 3, 8


def _cp(sem=None, **kw):
    return pltpu.CompilerParams(dimension_semantics=sem, vmem_limit_bytes=VMEM_LIMIT, **kw)


def _dot(a, b):
    return jnp.dot(a, b, preferred_element_type=_F32)


def _dot_nt(a, b):
    return lax.dot_general(a, b, (((1,), (1,)), ((), ())), preferred_element_type=_F32)


def _dot_tn(a, b):
    return lax.dot_general(a, b, (((0,), (0,)), ((), ())), preferred_element_type=_F32)


def _sigmoid(x):
    return 0.5 * jnp.tanh(0.5 * x) + 0.5


def _dsilu(x, s):
    return s * (1.0 + x * (1.0 - s))


_GELU_C = 0.7978845608028654


def _gelu_parts(x):
    t = jnp.tanh(_GELU_C * (x + 0.044715 * (x * x * x)))
    g = 0.5 * x * (1.0 + t)
    dg = 0.5 * (1.0 + t) + 0.5 * x * (1.0 - t * t) * (_GELU_C * (1.0 + 3.0 * 0.044715 * (x * x)))
    return g, dg


def _softplus(z):
    e = jnp.exp(-jnp.abs(z))
    w = 1.0 + e
    l1p = jnp.where(w == 1.0, e, jnp.log(w) * e / jnp.where(w == 1.0, 1.0, w - 1.0))
    return jnp.maximum(z, 0.0) + l1p


def _rms_fwd(x):
    r = lax.rsqrt(jnp.mean(x * x, axis=-1, keepdims=True) + EPS)
    return x * r, r


def _rms_bwd(dyg, n, r):
    return r * (dyg - n * jnp.mean(dyg * n, axis=-1, keepdims=True))


def _full(shape):
    nd = len(shape)
    return pl.BlockSpec(shape, lambda i: (0,) * nd)


def _const(shape):
    nd = len(shape)
    return pl.BlockSpec(shape, lambda i: (0,) * nd, pipeline_mode=pl.Buffered(1))


def _carry_gather(gather, i, nt):
    @pl.when(i == 0)
    def _():
        gather.start()

    def tail():
        for j in range(3):
            @pl.when(i == max(nt - 4 + j, 0))
            def _(j=j):
                gather.forward(j)

        @pl.when(i == nt - 1)
        def _():
            gather.finish()

    return tail


def _pair_place(ref, block):
    return ref.at[block // 2, :, pl.ds(pl.multiple_of((block % 2) * WIN_B, WIN_B), WIN_B)]


def _rg_gates(xc, wr_ref, wi_ref, vec_ref):
    xcb = xc.astype(_BF)
    r = _sigmoid(_dot(xcb, wr_ref[...]) + vec_ref[R_BR:R_BR + 1, :])
    ig = _sigmoid(_dot(xcb, wi_ref[...]) + vec_ref[R_BI:R_BI + 1, :])
    nsp8 = -LRU_C * _softplus(-vec_ref[R_LAM:R_LAM + 1, :])
    la = nsp8 * r
    a = jnp.exp(la)
    th = jnp.tanh(la)
    s = jnp.sqrt(-2.0 * th / (1.0 - th))
    return r, ig, a, s, nsp8


def _conv(xbuf, vec_ref):
    acc = vec_ref[R_CONVW:R_CONVW + 1, :] * xbuf[pl.ds(5, TM), :]
    for j in range(1, 4):
        acc = acc + vec_ref[R_CONVW + j:R_CONVW + j + 1, :] * xbuf[pl.ds(5 + j, TM), :]
    return vec_ref[R_CONVB:R_CONVB + 1, :] + acc


def _dot3(m01, x):
    hi = x.astype(_BF)
    r1 = x - hi.astype(_F32)
    mid = r1.astype(_BF)
    lo = (r1 - mid.astype(_F32)).astype(_BF)
    return (_dot(m01, lo) + _dot(m01, mid)) + _dot(m01, hi)


def _chunk_dot3(m01, x):
    return jnp.concatenate([_dot3(m01, x[HC * c:HC * (c + 1), :]) for c in range(x.shape[0] // HC)], axis=0)


def _chunk_masks():
    row = lax.broadcasted_iota(jnp.int32, (HC, HC), 0)
    col = lax.broadcasted_iota(jnp.int32, (HC, HC), 1)
    return (row >= col).astype(_BF), (col >= row).astype(_BF), jnp.ones((HC, HC), _BF)


def _per_chunk_rows(x, r):
    return jnp.concatenate([jnp.broadcast_to(x[HC * c + r:HC * c + r + 1, :], (HC, x.shape[1]))
                            for c in range(TM // HC)], axis=0)


def _hg_prep(p_ref, lb, tri):
    hq = p_ref[:, pl.ds(2 * D_RG, D_HG)]
    hf = p_ref[:, pl.ds(2 * D_RG + D_HG, D_HG)]
    sq = _sigmoid(hq)
    q = hq * sq
    sg = _sigmoid(hf)
    f = lb + (1.0 - lb) * sg
    k = 1.0 - f
    b = _chunk_dot3(tri, jnp.log(f))
    bm = _per_chunk_rows(b, HC // 2 - 1)
    bl = _per_chunk_rows(b, HC - 1)
    e_q = jnp.exp(b - bm)
    e_k = jnp.exp(bm - b)
    e_b = jnp.exp(b)
    e_l = jnp.exp(bl - b)
    return dict(hq=hq, sq=sq, q=q, sg=sg, f=f, k=k, e_q=e_q, e_k=e_k, e_b=e_b, e_l=e_l,
                qd=q * e_q, kd=k * e_k, qe=q * e_b, ke=k * e_l, e_end=jnp.exp(bl))


def _mixer_fwd(h0, g_mix, w_in, wr, wi, vec, hb, g_hg, shards):
    t_pad = h0.shape[0]
    nt = t_pad // TM
    nc_t = TM // HC
    nsh = len(shards)

    def body(h_ref, gmix_ref, win_ref, wr_ref, wi_ref, vec_ref, hb_ref, ghg_ref, *rest):
        sh_refs, rest = rest[:nsh], rest[nsh:]
        pout_ref, uout_ref, y_ref, hs_ref, o_ref, sc_ref = rest[:6]
        gath_refs, rest = rest[6:6 + nsh], rest[6 + nsh:]
        xbuf, a_s, b_s, hcar, st, qd_s, kd_s, qe_s, ke_s, v_s, u_s, p_s, p_ref = rest[:13]
        i = pl.program_id(0)
        tail = _carry_gather(_Gather(sh_refs, gath_refs, rest[13:]), i, nt + 1)

        @pl.when(i == 0)
        def _():
            p_s[...] = jnp.zeros_like(p_s)

        p_ref[...] = p_s[...]

        @pl.when(i <= 1)
        def _():
            xbuf[pl.ds(0, 8), :] = jnp.zeros((8, D_RG), _F32)
            hcar[...] = jnp.zeros_like(hcar)
            st[...] = jnp.zeros_like(st)

        n_h, _ = _rms_fwd(h_ref[...])
        u = (n_h * gmix_ref[...]).astype(_BF)
        uout_ref[...] = u
        pieces = [(j, k) for j in range(4) for k in range(WIN_P // 256)]

        def project(count):
            for _ in range(count):
                j, k = pieces.pop(0)
                blk = _dot(u, win_ref[j, :, pl.ds(256 * k, 256)])
                p_s[:, pl.ds(WIN_P * j + 256 * k, 256)] = blk
                pout_ref[:, pl.ds(WIN_P * j + 256 * k, 256)] = blk

        x = p_ref[:, pl.ds(0, D_RG)]
        xbuf[pl.ds(8, TM), :] = x
        xc = _conv(xbuf, vec_ref)
        xbuf[pl.ds(0, 8), :] = x[TM - 8:, :]
        r, ig, a, s, _ = _rg_gates(xc, wr_ref, wi_ref, vec_ref)
        a_s[...] = a
        b_s[...] = s * (ig * xc)

        def step(t, h):
            h = a_s[pl.ds(t, 1), :] * h + b_s[pl.ds(t, 1), :]
            hs_ref[pl.ds(t, 1), :] = h
            return h

        hcar[pl.ds(0, 1), :] = lax.fori_loop(0, TM, step, hcar[pl.ds(0, 1), :], unroll=8)
        gel, _ = _gelu_parts(p_ref[:, pl.ds(D_RG, D_RG)])
        n, _ = _rms_fwd(gel * hs_ref[...])
        y_ref[:, pl.ds(0, D_RG)] = (n * vec_ref[R_GRG:R_GRG + 1, :]).astype(_BF)

        lb = _sigmoid(hb_ref[0:1, :] - hb_ref[1:2, :])
        tri, _, _ = _chunk_masks()
        q = _hg_prep(p_ref, lb, tri)
        for name, ref in (("qd", qd_s), ("kd", kd_s), ("qe", qe_s), ("ke", ke_s)):
            ref[...] = q[name].astype(_BF)
        v_s[...] = p_ref[:, pl.ds(2 * D_RG + 2 * D_HG, D_HG)].astype(_BF)
        e_end = q["e_end"]
        causal = (lax.broadcasted_iota(jnp.int32, (HC, HC), 0) >= lax.broadcasted_iota(jnp.int32, (HC, HC), 1))
        for c in range(nc_t):
            for h in range(NH):
                rs, cs = pl.ds(HC * c, HC), pl.ds(HD * h, HD)
                amat = jnp.where(causal, _dot_nt(qd_s[rs, cs], kd_s[rs, cs]), 0.0)
                o_ref[rs, cs] = _dot(amat.astype(_BF), v_s[rs, cs])
                u_s[NH * c + h] = _dot_tn(v_s[rs, cs], ke_s[rs, cs])
                if pieces:
                    project(1)
        assert not pieces
        for h in range(NH):
            cs = pl.ds(HD * h, HD)
            s_run = st[h]
            for c in range(nc_t):
                rs = pl.ds(HC * c, HC)
                sc_ref[c, h] = s_run
                o_ref[rs, cs] += _dot_nt(qe_s[rs, cs], s_run.astype(_BF))
                s_run = e_end[HC * c:HC * c + 1, HD * h:HD * (h + 1)] * s_run + u_s[NH * c + h]
            st[h] = s_run
        for h in range(NH):
            cs = pl.ds(HD * h, HD)
            n_o, _ = _rms_fwd(o_ref[:, cs])
            hg = p_ref[:, pl.ds(2 * D_RG + 3 * D_HG + HD * h, HD)]
            y_ref[:, pl.ds(D_RG + HD * h, HD)] = ((n_o * ghg_ref[...]) * (hg * _sigmoid(hg))).astype(_BF)

        tail()

    hbm = pl.BlockSpec(memory_space=pl.ANY)

    def proj(i):
        return jnp.minimum(i, nt - 1)

    def mixed(i):
        return jnp.maximum(i - 1, 0)

    return pl.pallas_call(
        body, name="mixer_fwd", grid=(nt + 1,),
        in_specs=[pl.BlockSpec((TM, D), lambda i: (proj(i), 0)), _full((1, D)), _const((4, D, WIN_P)),
                  _full((D_RG, D_RG)), _full((D_RG, D_RG)),
                  _full((16, D_RG)), _full((2, D_HG)), _full((1, HD))] + [hbm] * nsh,
        out_specs=[pl.BlockSpec((TM, D_IN), lambda i: (proj(i), 0)), pl.BlockSpec((TM, D), lambda i: (proj(i), 0)),
                   pl.BlockSpec((TM, D), lambda i: (mixed(i), 0)), pl.BlockSpec((TM, D_RG), lambda i: (mixed(i), 0)),
                   pl.BlockSpec((TM, D_HG), lambda i: (mixed(i), 0)),
                   pl.BlockSpec((nc_t, NH, HD, HD), lambda i: (mixed(i), 0, 0, 0))] + [hbm] * nsh,
        out_shape=[_S((t_pad, D_IN), _F32), _S((t_pad, D), _BF),
                   _S((t_pad, D), _BF), _S((t_pad, D_RG), _F32), _S((t_pad, D_HG), _F32),
                   _S((t_pad // HC, NH, HD, HD), _F32)] + [_S((N_DEV,) + s.shape, s.dtype) for s in shards],
        scratch_shapes=[pltpu.VMEM((TM + 8, D_RG), _F32), pltpu.VMEM((TM, D_RG), _F32),
                        pltpu.VMEM((TM, D_RG), _F32), pltpu.VMEM((8, D_RG), _F32),
                        pltpu.VMEM((NH, HD, HD), _F32)] + [pltpu.VMEM((TM, D_HG), _BF) for _ in range(5)]
        + [pltpu.VMEM((nc_t * NH, HD, HD), _F32), pltpu.VMEM((TM, D_IN), _F32), pltpu.VMEM((TM, D_IN), _F32)]
        + _sem_shapes(nsh),
        compiler_params=_cp(("arbitrary",)),
    )(h0, g_mix, w_in, wr, wi, vec, hb, g_hg, *shards)


def _ffn_loss(h0, y, w_out, g_ffn, w_gu, w_down, g_fin, tgt, n_valid):
    t_pad = h0.shape[0]

    def body(h_ref, y_ref, wo_ref, gffn_ref, wgu_ref, wd_ref, g_ref, t_ref,
             h1_ref, v_ref, gu_ref, act_ref, dh2_ref, dh2b_ref, loss_ref, gfin_ref):
        i = pl.program_id(0)

        @pl.when(i == 0)
        def _():
            loss_ref[...] = jnp.zeros_like(loss_ref)
            gfin_ref[...] = jnp.zeros_like(gfin_ref)

        h1 = h_ref[...] + _dot(y_ref[...], wo_ref[...])
        h1_ref[...] = h1
        n1, _ = _rms_fwd(h1)
        vb = (n1 * gffn_ref[...]).astype(_BF)
        v_ref[...] = vb
        h2 = h1
        for b in range(4):
            gate = _dot_nt(vb, wgu_ref[b])
            up = _dot_nt(vb, wgu_ref[4 + b])
            gu_ref[b] = gate
            gu_ref[4 + b] = up
            act = ((gate * _sigmoid(gate)) * up).astype(_BF)
            act_ref[b] = act
            h2 = h2 + _dot(act, wd_ref[b])
        n, r = _rms_fwd(h2)
        out = n * g_ref[...]
        row = i * TM + lax.broadcasted_iota(jnp.int32, (TM, 1), 0)
        valid = (row >= N_META) & (row < n_valid)
        err = jnp.where(valid, out - t_ref[...], 0.0)
        loss_ref[...] += (0.5 / D) * jnp.sum(err * err)
        dout = err * (1.0 / D)
        gfin_ref[...] += jnp.sum(dout * n, axis=0, keepdims=True)
        dh2 = _rms_bwd(dout * g_ref[...], n, r)
        dh2_ref[...] = dh2
        dh2b_ref[...] = dh2.astype(_BF)

    tile = pl.BlockSpec((TM, D), lambda i: (i, 0))
    return pl.pallas_call(
        body, name="ffn_loss", grid=(t_pad // TM,),
        in_specs=[tile, tile, _const((D, D)), _full((1, D)),
                  _const((N_DEV, FFB, D)), _const((4, FFB, D)), _full((1, D)), tile],
        out_specs=[tile, tile,
                   pl.BlockSpec((N_DEV, TM, FFB), lambda i: (0, i, 0)), pl.BlockSpec((4, TM, FFB), lambda i: (0, i, 0)),
                   tile, tile, _full((8, 128)), _full((1, D))],
        out_shape=[_S((t_pad, D), _F32), _S((t_pad, D), _BF),
                   _S((N_DEV, t_pad, FFB), _F32), _S((4, t_pad, FFB), _BF), _S((t_pad, D), _F32),
                   _S((t_pad, D), _BF), _S((8, 128), _F32), _S((1, D), _F32)],
        compiler_params=_cp(("arbitrary",)),
    )(h0, y, w_out, g_ffn, w_gu, w_down, g_fin, tgt)


def _ffn_bwd(dh2, dh2b, gu, h1, g_ffn, w_gu, w_down, w_out):
    t_pad = dh2.shape[0]

    def body(dh2_ref, dh2b_ref, gu_ref, h1_ref, g_ref, wgu_ref, wd_ref, wo_ref,
             dgu_ref, dh1_ref, dh1b_ref, dy_ref, gffn_ref):
        i = pl.program_id(0)

        @pl.when(i == 0)
        def _():
            gffn_ref[...] = jnp.zeros_like(gffn_ref)

        db = dh2b_ref[...]
        dv = jnp.zeros((TM, D), _F32)
        for b in range(4):
            dact = _dot_nt(db, wd_ref[b])
            gate = gu_ref[b]
            up = gu_ref[4 + b]
            sg = _sigmoid(gate)
            dgate = ((dact * up) * _dsilu(gate, sg)).astype(_BF)
            dup = (dact * (gate * sg)).astype(_BF)
            dgu_ref[b] = dgate
            dgu_ref[4 + b] = dup
            dv = dv + _dot(dgate, wgu_ref[b]) + _dot(dup, wgu_ref[4 + b])
        n, r = _rms_fwd(h1_ref[...])
        gffn_ref[...] += jnp.sum(dv * n, axis=0, keepdims=True)
        dh1 = dh2_ref[...] + _rms_bwd(dv * g_ref[...], n, r)
        dh1_ref[...] = dh1
        dh1b = dh1.astype(_BF)
        dh1b_ref[...] = dh1b
        dy_ref[...] = _dot_nt(dh1b, wo_ref[...])

    tile = pl.BlockSpec((TM, D), lambda i: (i, 0))
    return pl.pallas_call(
        body, name="ffn_bwd", grid=(t_pad // TM,),
        in_specs=[tile, tile, pl.BlockSpec((N_DEV, TM, FFB), lambda i: (0, i, 0)), tile, _full((1, D)),
                  _const((N_DEV, FFB, D)), _const((4, FFB, D)), _const((D, D))],
        out_specs=[pl.BlockSpec((N_DEV, TM, FFB), lambda i: (0, i, 0)), tile, tile, tile, _full((1, D))],
        out_shape=[_S((N_DEV, t_pad, FFB), _BF), _S((t_pad, D), _F32), _S((t_pad, D), _BF),
                   _S((t_pad, D), _F32), _S((1, D), _F32)],
        compiler_params=_cp(("arbitrary",)),
    )(dh2, dh2b, gu, h1, g_ffn, w_gu, w_down, w_out)


def _mixer_bwd(p, hs, o, sc, dy, wr, wi, vec, hb, g_hg, scatter, windows):
    t_pad = p.shape[0]
    nt = t_pad // TM
    nc_t = TM // HC
    nsc = len(scatter)

    def rev(i):
        return nt - 1 - i

    def body(p_ref, pprev_ref, hs_ref, hprev_ref, o_ref, sc_ref, dy_ref, wr_ref, wi_ref, vec_ref, hb_ref, ghg_ref,
             *rest):
        send_refs, rest = rest[:nsc], rest[nsc:]
        dp_ref, gvec_ref, gw_ref = rest[:3]
        recv_refs, rest = rest[3:3 + nsc], rest[3 + nsc:]
        xbuf, hbuf, dbuf, a_s, g_s, ccar, dst = rest[:7]
        qd_s, kd_s, qe_s, ke_s, v_s, do_s, dqd_s, dkd_s, dqe_s, dke_s, dv_s, w_s, dend_s = rest[7:20]
        exchange = _Exchange(send_refs, [], recv_refs, rest[20:], windows)
        i = pl.program_id(0)
        first_tile = i == nt - 1

        @pl.when(i == 0)
        def _():
            exchange.start()
            gvec_ref[...] = jnp.zeros_like(gvec_ref)
            gw_ref[...] = jnp.zeros_like(gw_ref)
            dbuf[pl.ds(TM, 8), :] = jnp.zeros((8, D_RG), _F32)
            ccar[...] = jnp.zeros_like(ccar)
            dst[...] = jnp.zeros_like(dst)

        def acc(row, val):
            gvec_ref[row:row + 1, :] += jnp.sum(val, axis=0, keepdims=True)

        keep = jnp.where(first_tile, 0.0, 1.0)
        x = p_ref[:, pl.ds(0, D_RG)]
        xbuf[pl.ds(0, 8), :] = pprev_ref[...] * keep
        xbuf[pl.ds(8, TM), :] = x
        xc = _conv(xbuf, vec_ref)
        r, ig, a, s, nsp8 = _rg_gates(xc, wr_ref, wi_ref, vec_ref)
        h = hs_ref[...]
        hbuf[pl.ds(0, 8), :] = hprev_ref[...] * keep
        hbuf[pl.ds(8, TM), :] = h
        hm1 = hbuf[pl.ds(7, TM), :]
        gr = p_ref[:, pl.ds(D_RG, D_RG)]
        gel, dgel = _gelu_parts(gr)
        n, rr = _rms_fwd(gel * h)
        dyn = dy_ref[:, pl.ds(0, D_RG)]
        acc(R_GRG, dyn * n)
        dpre = _rms_bwd(dyn * vec_ref[R_GRG:R_GRG + 1, :], n, rr)
        dp_ref[:, pl.ds(D_RG, D_RG)] = ((dpre * h) * dgel).astype(_BF)
        a_s[...] = a
        g_s[...] = dpre * gel

        def step(k, c):
            t = TM - 1 - k
            g = g_s[pl.ds(t, 1), :] + c
            g_s[pl.ds(t, 1), :] = g
            return a_s[pl.ds(t, 1), :] * g

        ccar[pl.ds(0, 1), :] = lax.fori_loop(0, TM, step, ccar[pl.ds(0, 1), :], unroll=8)
        gt = g_s[...]
        da = gt * hm1
        ixc = ig * xc
        ds = gt * ixc
        dig = (gt * s) * xc
        dxc = (gt * s) * ig
        dla = da * a - ds * ((a * a) / s)
        lam = vec_ref[R_LAM:R_LAM + 1, :]
        gvec_ref[R_LAM:R_LAM + 1, :] += jnp.sum(dla * r, axis=0, keepdims=True) * (LRU_C * _sigmoid(-lam))
        dzr = (dla * nsp8) * (r * (1.0 - r))
        dzi = dig * (ig * (1.0 - ig))
        acc(R_BR, dzr)
        acc(R_BI, dzi)
        xcb = xc.astype(_BF)
        dzrb = dzr.astype(_BF)
        dzib = dzi.astype(_BF)
        gw_ref[0] += _dot_tn(xcb, dzrb)
        gw_ref[1] += _dot_tn(xcb, dzib)
        dxc = dxc + _dot_nt(dzrb, wr_ref[...]) + _dot_nt(dzib, wi_ref[...])
        acc(R_CONVB, dxc)
        for j in range(4):
            acc(R_CONVW + j, dxc * xbuf[pl.ds(5 + j, TM), :])
        dbuf[pl.ds(0, TM), :] = dxc
        dx = vec_ref[R_CONVW + 3:R_CONVW + 4, :] * dxc
        for j in range(3):
            dx = dx + vec_ref[R_CONVW + j:R_CONVW + j + 1, :] * dbuf[pl.ds(3 - j, TM), :]
        dbuf[pl.ds(TM, 8), :] = dxc[0:8, :]
        dp_ref[:, pl.ds(0, D_RG)] = dx.astype(_BF)

        lb = _sigmoid(hb_ref[0:1, :] - hb_ref[1:2, :])
        tri, tri_rev, ones = _chunk_masks()
        q = _hg_prep(p_ref, lb, tri)
        qdb, kdb = q["qd"].astype(_BF), q["kd"].astype(_BF)
        qd_s[...] = qdb
        kd_s[...] = kdb
        qe_s[...] = q["qe"].astype(_BF)
        ke_s[...] = q["ke"].astype(_BF)
        v_s[...] = p_ref[:, pl.ds(2 * D_RG + 2 * D_HG, D_HG)].astype(_BF)
        e_end = q["e_end"]
        ghg = ghg_ref[...]
        for h in range(NH):
            cs = pl.ds(HD * h, HD)
            hg = p_ref[:, pl.ds(2 * D_RG + 3 * D_HG + HD * h, HD)]
            sh = _sigmoid(hg)
            n_o, r_o = _rms_fwd(o_ref[:, cs])
            dyh = dy_ref[:, pl.ds(D_RG + HD * h, HD)]
            dp_ref[:, pl.ds(2 * D_RG + 3 * D_HG + HD * h, HD)] = ((dyh * (n_o * ghg)) * _dsilu(hg, sh)).astype(_BF)
            dn = dyh * (hg * sh)
            gvec_ref[R_GHG:R_GHG + 1, pl.ds(0, HD)] += jnp.sum(dn * n_o, axis=0, keepdims=True)
            do_s[:, cs] = _rms_bwd(dn * ghg, n_o, r_o).astype(_BF)
        causal = (lax.broadcasted_iota(jnp.int32, (HC, HC), 0) >= lax.broadcasted_iota(jnp.int32, (HC, HC), 1))
        for c in range(nc_t):
            for h in range(NH):
                rs, cs = pl.ds(HC * c, HC), pl.ds(HD * h, HD)
                qd_c, kd_c, do_c = qd_s[rs, cs], kd_s[rs, cs], do_s[rs, cs]
                amat = jnp.where(causal, _dot_nt(qd_c, kd_c), 0.0).astype(_BF)
                da_m = jnp.where(causal, _dot_nt(do_c, v_s[rs, cs]), 0.0).astype(_BF)
                dqd_s[rs, cs] = _dot(da_m, kd_c)
                dkd_s[rs, cs] = _dot_tn(da_m, qd_c)
                dqe_s[rs, cs] = _dot(do_c, sc_ref[c, h].astype(_BF))
                dv_s[rs, cs] = _dot_tn(amat, do_c)
                w_s[NH * c + h] = _dot_tn(do_c, qe_s[rs, cs])
        for h in range(NH):
            cs = pl.ds(HD * h, HD)
            d_run = dst[h]
            for c in reversed(range(nc_t)):
                rs = pl.ds(HC * c, HC)
                d_b = d_run.astype(_BF)
                dke_s[rs, cs] = _dot(v_s[rs, cs], d_b)
                dp_ref[rs, pl.ds(2 * D_RG + 2 * D_HG + HD * h, HD)] = (
                    dv_s[rs, cs] + _dot_nt(ke_s[rs, cs], d_b)).astype(_BF)
                dend_s[pl.ds(c, 1), cs] = jnp.sum(sc_ref[c, h] * d_run, axis=0, keepdims=True)
                d_run = w_s[NH * c + h] + e_end[HC * c:HC * c + 1, HD * h:HD * (h + 1)] * d_run
            dst[h] = d_run
        dqd, dkd, dqe, dke = dqd_s[...], dkd_s[...], dqe_s[...], dke_s[...]
        dq = dqd * q["e_q"] + dqe * q["e_b"]
        dk = dkd * q["e_k"] + dke * q["e_l"]
        dkeke = dke * q["ke"]
        db = dqd * qdb.astype(_F32) - dkd * kdb.astype(_F32) + dqe * q["qe"] - dkeke
        d_end = jnp.concatenate([jnp.broadcast_to(dend_s[pl.ds(c, 1), :], (HC, D_HG)) for c in range(nc_t)], axis=0)
        dlf = _chunk_dot3(tri_rev, db) + _chunk_dot3(ones, dkeke) + d_end * e_end
        df = dlf / q["f"] - dk
        sg = q["sg"]
        gvec_ref[R_HB0:R_HB0 + 1, :] += jnp.sum(df * (1.0 - sg), axis=0, keepdims=True)
        dp_ref[:, pl.ds(2 * D_RG, D_HG)] = (dq * _dsilu(q["hq"], q["sq"])).astype(_BF)
        dp_ref[:, pl.ds(2 * D_RG + D_HG, D_HG)] = ((df * (1.0 - lb)) * (sg * (1.0 - sg))).astype(_BF)

        @pl.when(i == nt - 1)
        def _():
            glb = gvec_ref[R_HB0:R_HB0 + 1, :] * (lb * (1.0 - lb))
            gvec_ref[R_HB0:R_HB0 + 1, :] = glb
            gvec_ref[R_HB1:R_HB1 + 1, :] = -glb
            exchange.finish()

    hbm = pl.BlockSpec(memory_space=pl.ANY)
    return pl.pallas_call(
        body, name="mixer_bwd", grid=(nt,),
        in_specs=[pl.BlockSpec((TM, D_IN), lambda i: (rev(i), 0)),
                  pl.BlockSpec((8, D_RG), lambda i: (jnp.maximum(rev(i) * (TM // 8) - 1, 0), 0)),
                  pl.BlockSpec((TM, D_RG), lambda i: (rev(i), 0)),
                  pl.BlockSpec((8, D_RG), lambda i: (jnp.maximum(rev(i) * (TM // 8) - 1, 0), 0)),
                  pl.BlockSpec((TM, D_HG), lambda i: (rev(i), 0)),
                  pl.BlockSpec((nc_t, NH, HD, HD), lambda i: (rev(i), 0, 0, 0)),
                  pl.BlockSpec((TM, D), lambda i: (rev(i), 0)),
                  _full((D_RG, D_RG)), _full((D_RG, D_RG)), _full((16, D_RG)), _full((2, D_HG)), _full((1, HD))]
        + [hbm] * nsc,
        out_specs=[pl.BlockSpec((TM, D_IN), lambda i: (rev(i), 0)), _full((16, D_RG)), _full((2, D_RG, D_RG))]
        + [hbm] * nsc,
        out_shape=[_S((t_pad, D_IN), _BF), _S((16, D_RG), _F32), _S((2, D_RG, D_RG), _F32)]
        + _recv_shapes(scatter, windows),
        scratch_shapes=[pltpu.VMEM((TM + 8, D_RG), _F32), pltpu.VMEM((TM + 8, D_RG), _F32),
                        pltpu.VMEM((TM + 8, D_RG), _F32), pltpu.VMEM((TM, D_RG), _F32),
                        pltpu.VMEM((TM, D_RG), _F32), pltpu.VMEM((8, D_RG), _F32),
                        pltpu.VMEM((NH, HD, HD), _F32)]
        + [pltpu.VMEM((TM, D_HG), _BF) for _ in range(6)] + [pltpu.VMEM((TM, D_HG), _F32) for _ in range(5)]
        + [pltpu.VMEM((nc_t * NH, HD, HD), _F32), pltpu.VMEM((8, D_HG), _F32)] + _sem_shapes(nsc),
        compiler_params=_cp(("arbitrary",)),
    )(p, p, hs, hs, o, sc, dy, wr, wi, vec, hb, g_hg, *scatter)


def _inproj_bwd_send(dp, w_in, h0, dh1, g_mix, u, order, gffn, gfin, loss, to_all):
    t_pad = dp.shape[0]
    rb = TM
    n_steps = N_DEV + t_pad // rb
    na = len(to_all)

    def body(order_ref, dpc_ref, dpr_ref, u_ref, w_ref, h_ref, dh1_ref, g_ref, gffn_ref, gfin_ref, loss_ref, *rest):
        all_in = rest[:na]
        dh0_ref, recv_ref = rest[na:na + 2]
        all_out = rest[na + 2:2 * na + 2]
        alla_ref = rest[2 * na + 2]
        buf, pack, blk_send, blk_recv, blk_local = rest[2 * na + 3:2 * na + 8]
        exchange = _Exchange([], all_in, all_out, rest[2 * na + 8:2 * na + 11])
        last = _Exchange([], [pack], [alla_ref], rest[2 * na + 11:])
        s = pl.program_id(0)
        x, y, c = _coords()
        me = 4 * x + 2 * y + c

        def send(step):
            r = _SEND_ORDER[step]
            return pltpu.make_async_remote_copy(
                src_ref=buf.at[step], dst_ref=recv_ref.at[me], send_sem=blk_send.at[step], recv_sem=blk_recv.at[r - 1],
                device_id=(x ^ (r >> 2), y ^ ((r >> 1) & 1), c ^ (r & 1)), device_id_type=_MESH)

        @pl.when(s == 0)
        def _():
            exchange.start()
            pack[...] = jnp.zeros_like(pack)

        @pl.when(s < N_DEV)
        def _():
            buf[s] = _dot_tn(u_ref[...], dpc_ref[...]).astype(_BF)

            for step in range(N_DEV - 1):
                @pl.when(s == step)
                def _(step=step):
                    send(step).start()

        @pl.when(s >= N_DEV)
        def _():
            du = jnp.zeros((rb, D), _F32)
            for j in range(4):
                du = du + _dot_nt(dpr_ref[:, WIN_P * j:WIN_P * (j + 1)], w_ref[j])
            n, r = _rms_fwd(h_ref[...])
            pack[R_GMIX:R_GMIX + 1, :] += jnp.sum(du * n, axis=0, keepdims=True)
            dh0 = dh1_ref[...] + _rms_bwd(du * g_ref[...], n, r)
            dh0_ref[...] = dh0

            @pl.when(s == N_DEV)
            def _():
                pack[R_META:R_META + N_META, :] = dh0[0:N_META, :]

        @pl.when(s == n_steps - 1)
        def _():
            pack[R_GFFN:R_GFFN + 1, :] = gffn_ref[...]
            pack[R_GFIN:R_GFIN + 1, :] = gfin_ref[...]
            pack[R_LOSS:R_LOSS + 1, pl.ds(0, 128)] = loss_ref[0:1, :]
            last.start()
            mine = pltpu.make_async_copy(buf.at[N_DEV - 1], recv_ref.at[me], blk_local.at[0])
            mine.start()
            for step in range(N_DEV - 1):
                send(step).wait_send()
            for r in range(1, N_DEV):
                px, py, pc = x ^ (r >> 2), y ^ ((r >> 1) & 1), c ^ (r & 1)
                pltpu.make_async_remote_copy(
                    src_ref=buf.at[0], dst_ref=recv_ref.at[4 * px + 2 * py + pc], send_sem=blk_send.at[0],
                    recv_sem=blk_recv.at[r - 1], device_id=(px, py, pc), device_id_type=_MESH).wait_recv()
            mine.wait()
            exchange.finish()
            last.finish()

    hbm = pl.BlockSpec(memory_space=pl.ANY)
    rows = pl.BlockSpec((rb, D), lambda s, order: (jnp.maximum(s - N_DEV, 0), 0))
    one = pl.BlockSpec((1, D), lambda s, order: (0, 0))
    res = pl.pallas_call(
        body, name="inproj_bwd_send",
        grid_spec=pltpu.PrefetchScalarGridSpec(
            num_scalar_prefetch=1, grid=(n_steps,),
            in_specs=[pl.BlockSpec((t_pad, WIN_B), lambda s, order: (0, order[jnp.minimum(s, N_DEV - 1)])),
                      pl.BlockSpec((rb, D_IN), lambda s, order: (jnp.maximum(s - N_DEV, 0), 0)),
                      pl.BlockSpec((t_pad, D), lambda s, order: (0, 0), pipeline_mode=pl.Buffered(1)),
                      pl.BlockSpec((4, D, WIN_P), lambda s, order: (0, 0, 0), pipeline_mode=pl.Buffered(1)),
                      rows, rows, one, one, one, pl.BlockSpec((8, 128), lambda s, order: (0, 0))] + [hbm] * na,
            out_specs=[rows] + [hbm] * (na + 2),
            scratch_shapes=[pltpu.VMEM((N_DEV, D, WIN_B), _BF), pltpu.VMEM((24, D), _F32),
                            pltpu.SemaphoreType.DMA((N_DEV - 1,)), pltpu.SemaphoreType.DMA((N_DEV - 1,)),
                            pltpu.SemaphoreType.DMA((1,))] + _sem_shapes(na) + _sem_shapes(1)),
        out_shape=[_S((t_pad, D), _F32), _S((N_DEV, D, WIN_B), _BF)]
        + [_S((N_DEV,) + g.shape, g.dtype) for g in to_all] + [_S((N_DEV, 24, D), _F32)],
        compiler_params=_cp(("arbitrary",)),
    )(order, dp, dp, u, w_in, h0, dh1, g_mix, gffn, gfin, loss, *to_all)
    return res


def _recv_shapes(scatter, windows):
    return [_S(s.shape if w is None else (s.shape[0], w[1]) + s.shape[2:], s.dtype) for s, w in zip(scatter, windows)]


def _wgrad(name, a, b, a_spec, b_spec, n_blocks, out_block, scatter=(), windows=None):
    nsc = len(scatter)
    windows = windows if windows is not None else [None] * nsc

    def body(a_ref, b_ref, *rest):
        o_ref = rest[nsc]
        j = pl.program_id(0)
        if nsc:
            exchange = _Exchange(rest[:nsc], [], rest[nsc + 1:2 * nsc + 1], rest[2 * nsc + 1:], windows)

            @pl.when(j == 0)
            def _():
                exchange.start()

        av = a_ref[0] if len(a_ref.shape) == 3 else a_ref[...]
        bv = b_ref[0] if len(b_ref.shape) == 3 else b_ref[...]
        o_ref[0] = _dot_tn(av, bv).astype(_BF)

        if nsc:
            @pl.when(j == n_blocks - 1)
            def _():
                exchange.finish()

    hbm = pl.BlockSpec(memory_space=pl.ANY)
    res = pl.pallas_call(
        body, name=name, grid=(n_blocks,),
        in_specs=[a_spec, b_spec] + [hbm] * nsc,
        out_specs=[pl.BlockSpec((1,) + out_block, lambda j: (j, 0, 0))] + [hbm] * nsc,
        out_shape=[_S((n_blocks,) + out_block, _BF)] + _recv_shapes(scatter, windows),
        scratch_shapes=_sem_shapes(nsc) if nsc else [],
        compiler_params=_cp(("arbitrary",)),
    )(a, b, *scatter)
    return res if nsc else res[0]


def _coords():
    return lax.axis_index("x"), lax.axis_index("y"), lax.axis_index("c")


def _sem_shapes(na):
    return [pltpu.SemaphoreType.DMA((7 * na,)), pltpu.SemaphoreType.DMA((7 * na,)), pltpu.SemaphoreType.DMA((na,))]


class _Gather:
    def __init__(self, srcs, outs, sems, place=None):
        self.srcs, self.outs = srcs, outs
        self.send_sems, self.recv_sems, self.local_sems = sems
        self.place = place if place is not None else (lambda ref, block: ref.at[block])
        self.na = len(srcs)
        x, y, c = _coords()
        self.pos = (x, y, c)
        self.me = 4 * x + 2 * y + c
        self.sibling = (x, y, 1 - c)
        self.chips = [(1 - x, y), (x, 1 - y), (1 - x, 1 - y)]

    @staticmethod
    def _slot(px, py, pc):
        return 4 * px + 2 * py + pc

    def _copy(self, a, k, block, to, own=False):
        dst = self.place(self.outs[a], block)
        return pltpu.make_async_remote_copy(
            src_ref=self.srcs[a] if own else dst, dst_ref=dst,
            send_sem=self.send_sems.at[7 * a + k], recv_sem=self.recv_sems.at[7 * a + k],
            device_id=to, device_id_type=_MESH)

    def _mine(self, a):
        return pltpu.make_async_copy(self.srcs[a], self.place(self.outs[a], self.me), self.local_sems.at[a])

    def _first(self):
        c = self.pos[2]
        cps = []
        for a in range(self.na):
            cps.append(self._copy(a, 0, self.me, self.sibling, own=True))
            cps += [self._copy(a, 1 + j, self.me, (*chip, c), own=True) for j, chip in enumerate(self.chips)]
        return cps

    def _passed(self):
        c = self.pos[2]
        return [self._copy(a, 4 + j, self._slot(*chip, c), self.sibling)
                for j, chip in enumerate(self.chips) for a in range(self.na)]

    def start(self):
        for a in range(self.na):
            self._mine(a).start()
        for cp in self._first():
            cp.start()

    def forward(self, j):
        c = self.pos[2]
        chip = self.chips[j]
        for a in range(self.na):
            self._copy(a, 1 + j, self._slot(*chip, c), self.pos).wait_recv()
            self._copy(a, 4 + j, self._slot(*chip, c), self.sibling).start()

    def wait_sibling(self):
        x, y, c = self.pos
        for a in range(self.na):
            self._copy(a, 0, self._slot(x, y, 1 - c), self.pos).wait_recv()

    def wait_passed(self, j):
        c = self.pos[2]
        for a in range(self.na):
            self._copy(a, 4 + j, self._slot(*self.chips[j], 1 - c), self.pos).wait_recv()

    def finish_sends(self):
        for cp in self._first() + self._passed():
            cp.wait_send()
        for a in range(self.na):
            self._mine(a).wait()

    def finish(self):
        self.wait_sibling()
        for j in range(3):
            self.wait_passed(j)
        self.finish_sends()


class _Exchange:
    def __init__(self, scatter, gather, outs, sems, windows=None):
        self.windows = windows if windows is not None else [None] * len(scatter)
        self.ins = list(scatter) + list(gather)
        self.ns, self.na = len(scatter), len(scatter) + len(gather)
        self.outs = outs
        self.send_sems, self.recv_sems, self.local_sems = sems
        x, y, c = _coords()
        self.pos = (x, y, c)
        self.me = 4 * x + 2 * y + c

    def _peer(self, r):
        x, y, c = self.pos
        return x ^ (r >> 2), y ^ ((r >> 1) & 1), c ^ (r & 1)

    def _src(self, a, block):
        if a >= self.ns:
            return self.ins[a]
        if self.windows[a] is None:
            return self.ins[a].at[block]
        row0, rows = self.windows[a]
        return self.ins[a].at[block, pl.ds(row0, rows)]

    def _local(self, a):
        return pltpu.make_async_copy(self._src(a, self.me), self.outs[a].at[self.me], self.local_sems.at[a])

    def _send(self, a, r):
        px, py, pc = self._peer(r)
        return pltpu.make_async_remote_copy(
            src_ref=self._src(a, 4 * px + 2 * py + pc), dst_ref=self.outs[a].at[self.me],
            send_sem=self.send_sems.at[7 * a + r - 1], recv_sem=self.recv_sems.at[7 * a + r - 1],
            device_id=(px, py, pc), device_id_type=_MESH)

    def _recv(self, a, r):
        px, py, pc = self._peer(r)
        return pltpu.make_async_remote_copy(
            src_ref=self._src(a, self.me), dst_ref=self.outs[a].at[4 * px + 2 * py + pc],
            send_sem=self.send_sems.at[7 * a + r - 1], recv_sem=self.recv_sems.at[7 * a + r - 1],
            device_id=(px, py, pc), device_id_type=_MESH)

    def start(self):
        for a in range(self.na):
            self._local(a).start()
        for r in range(1, N_DEV):
            for a in range(self.na):
                self._send(a, r).start()

    def finish(self):
        for r in range(1, N_DEV):
            for a in range(self.na):
                self._recv(a, r).wait_recv()
        for r in range(1, N_DEV):
            for a in range(self.na):
                self._send(a, r).wait_send()
        for a in range(self.na):
            self._local(a).wait()


def _prologue(x, tgt, small_l, w_in_l, cast_f32):
    seq = x.shape[0]
    nx = seq // TM
    rest_rows = seq - nx * TM
    nt = nx + 1
    nc = len(cast_f32)
    body_rows = TM - N_META
    assert nx >= 1 and rest_rows % 8 == 0 and rest_rows <= body_rows
    x_rest, t_rest = x[nx * TM:], tgt[nx * TM:]

    def last_tile_body(rest_ref):
        parts = ([rest_ref[...]] if rest_rows else []) + (
            [jnp.zeros((body_rows - rest_rows, D), _F32)] if body_rows > rest_rows else [])
        return parts[0] if len(parts) == 1 else jnp.concatenate(parts, axis=0)

    def body(xm_ref, xp_ref, tm_ref, tp_ref, *rest):
        if rest_rows:
            xr_ref, tr_ref, rest = rest[0], rest[1], rest[2:]
        else:
            xr_ref = tr_ref = None
        s_ref, w_ref, rest = rest[0], rest[1], rest[2:]
        cins = rest[:nc]
        h0_ref, tgt_ref, small_ref, wg_ref = rest[nc:nc + 4]
        couts = rest[nc + 4:2 * nc + 4]
        s_stage, w_stage, meta, msem = rest[2 * nc + 4:2 * nc + 8]
        g_s = _Gather([s_stage], [small_ref], rest[2 * nc + 8:2 * nc + 11])
        g_w = _Gather([w_stage], [wg_ref], rest[2 * nc + 11:], place=_pair_place)
        s = pl.program_id(0)
        i = (s + 1) % nt

        @pl.when(s == 0)
        def _():
            s_stage[...] = s_ref[...]
            w_stage[...] = w_ref[...].astype(_BF)
            g_s.start()
            g_w.start()
            meta[...] = jnp.zeros_like(meta)
            for a in range(nc):
                couts[a][...] = cins[a][...].astype(_BF)

        @pl.when(s == nt - 1)
        def _():
            for j in range(3):
                g_s.forward(j)
            g_s.finish()
            cps = [pltpu.make_async_copy(small_ref.at[k, pl.ds(0, N_META), :], meta.at[:, pl.ds(128 * k, 128)],
                                         msem.at[k]) for k in range(N_DEV)]
            for cp in cps:
                cp.start()
            for cp in cps:
                cp.wait()
            for j in range(3):
                g_w.forward(j)
            g_w.finish()

        has_x = i < nx
        h0_ref[pl.ds(0, N_META), :] = jnp.where(i == 0, meta[...], xp_ref[...])
        h0_ref[pl.ds(N_META, body_rows), :] = jnp.where(has_x, xm_ref[pl.ds(0, body_rows), :], last_tile_body(xr_ref))
        tgt_ref[pl.ds(0, N_META), :] = jnp.where(i == 0, 0.0, tp_ref[...])
        tgt_ref[pl.ds(N_META, body_rows), :] = jnp.where(has_x, tm_ref[pl.ds(0, body_rows), :], last_tile_body(tr_ref))

    def tile_of(s):
        return (s + 1) % nt

    hbm = pl.BlockSpec(memory_space=pl.ANY)
    main = pl.BlockSpec((TM, D), lambda s: (jnp.minimum(tile_of(s), nx - 1), 0))
    prev = pl.BlockSpec((N_META, D), lambda s: (jnp.maximum(tile_of(s) * (TM // N_META) - 1, 0), 0))
    tile = pl.BlockSpec((TM, D), lambda s: (tile_of(s), 0))
    rests = [x_rest, t_rest] if rest_rows else []
    return pl.pallas_call(
        body, name="prologue", grid=(nt,),
        in_specs=[main, prev, main, prev] + [_const(r.shape) for r in rests]
        + [_const(small_l.shape), _const(w_in_l.shape)] + [_const(l.shape) for l in cast_f32],
        out_specs=[tile, tile, hbm, hbm] + [_full(l.shape) for l in cast_f32],
        out_shape=[_S((nt * TM, D), _F32), _S((nt * TM, D), _F32), _S((N_DEV,) + small_l.shape, _F32),
                   _S((4, D, WIN_P), _BF)] + [_S(l.shape, _BF) for l in cast_f32],
        scratch_shapes=[pltpu.VMEM(small_l.shape, _F32), pltpu.VMEM(w_in_l.shape, _BF), pltpu.VMEM((N_META, D), _F32),
                        pltpu.SemaphoreType.DMA((N_DEV,))] + _sem_shapes(1) + _sem_shapes(1),
        compiler_params=_cp(("arbitrary",)),
    )(x, x, tgt, tgt, *rests, small_l, w_in_l, *cast_f32)


def _adamw_math(w, g, m, v):
    m2 = ADAM_B1 * m + (1.0 - ADAM_B1) * g
    v2 = ADAM_B2 * v + (1.0 - ADAM_B2) * (g * g)
    m_hat = m2 / (1.0 - ADAM_B1 ** ADAM_STEP)
    v_hat = v2 / (1.0 - ADAM_B2 ** ADAM_STEP)
    delta = -ADAM_LR * (m_hat / (jnp.sqrt(v_hat) + ADAM_EPS) + ADAM_WD * w)
    return delta, m2, v2


def _adamw_big(name, recv, w, m, v, rows):
    r_all, c_all = w.shape

    def body(r_ref, w_ref, m_ref, v_ref, g_out, d_out, m_out, v_out):
        g = r_ref[0].astype(_F32)
        for k in range(1, N_DEV):
            g = g + r_ref[k].astype(_F32)
        delta, m2, v2 = _adamw_math(w_ref[...], g, m_ref[...], v_ref[...])
        g_out[...] = g
        d_out[...] = delta
        m_out[...] = m2
        v_out[...] = v2

    tile = pl.BlockSpec((rows, c_all), lambda i: (i, 0))
    return pl.pallas_call(
        body, name=name, grid=(r_all // rows,),
        in_specs=[pl.BlockSpec((N_DEV, rows, c_all), lambda i: (0, i, 0)), tile, tile, tile],
        out_specs=[tile] * 4,
        out_shape=[_S(w.shape, _F32)] * 4,
        compiler_params=_cp(("arbitrary",)),
    )(recv, w, m, v)


def _adamw_parts(name, recvs, w, m, v):
    def body(*refs):
        r_refs = refs[:len(recvs)]
        w_ref, m_ref, v_ref, g_out, d_out, m_out, v_out = refs[len(recvs):]
        row0 = 0
        for r_ref in r_refs:
            rows = pl.ds(row0, r_ref.shape[1])
            g = r_ref[0].astype(_F32)
            for k in range(1, N_DEV):
                g = g + r_ref[k].astype(_F32)
            delta, m2, v2 = _adamw_math(w_ref[rows, :], g, m_ref[rows, :], v_ref[rows, :])
            g_out[rows, :] = g
            d_out[rows, :] = delta
            m_out[rows, :] = m2
            v_out[rows, :] = v2
            row0 += r_ref.shape[1]

    return pl.pallas_call(
        body, name=name, out_shape=[_S(w.shape, _F32)] * 4,
        compiler_params=pltpu.CompilerParams(vmem_limit_bytes=VMEM_LIMIT),
    )(*recvs, w, m, v)


def _adamw_small(gathered, slices, wmv):
    ng, npar = len(gathered), len(slices)

    def body(*refs):
        g_refs = refs[:ng]
        wmv_refs = refs[ng:ng + 3 * npar]
        outs = refs[ng + 3 * npar:]
        for i, (ai, r0, nr, c0, ncol) in enumerate(slices):
            g = g_refs[ai][0, pl.ds(r0, nr), pl.ds(c0, ncol)].astype(_F32)
            for k in range(1, N_DEV):
                g = g + g_refs[ai][k, pl.ds(r0, nr), pl.ds(c0, ncol)].astype(_F32)
            w_ref, m_ref, v_ref = wmv_refs[3 * i:3 * i + 3]
            delta, m2, v2 = _adamw_math(w_ref[...], g, m_ref[...], v_ref[...])
            outs[4 * i][...] = g
            outs[4 * i + 1][...] = delta
            outs[4 * i + 2][...] = m2
            outs[4 * i + 3][...] = v2
        total = g_refs[0][0, pl.ds(R_LOSS, 1), pl.ds(0, 128)]
        for k in range(1, N_DEV):
            total = total + g_refs[0][k, pl.ds(R_LOSS, 1), pl.ds(0, 128)]
        outs[4 * npar][...] = total

    flat = [t for trip in wmv for t in trip]
    out_shape = []
    for w, _, _ in wmv:
        out_shape += [_S(w.shape, _F32)] * 4
    out_shape.append(_S((1, 128), _F32))
    return pl.pallas_call(
        body, name="adamw_small", out_shape=out_shape,
        compiler_params=pltpu.CompilerParams(vmem_limit_bytes=VMEM_LIMIT),
    )(*gathered, *flat)


def _block_diag(w):
    eye = jnp.eye(8, dtype=w.dtype)
    return (w[:, :, None, :] * eye[:, None, :, None]).reshape(D_RG, D_RG)


def _diag_blocks(g):
    return jnp.concatenate([g[64 * h:64 * (h + 1), 64 * h:64 * (h + 1)] for h in range(8)], axis=0)


def _local_step(h0, tgt_p, n_valid, g_mix, w_in, vec, wr, wi, hb, g_hg, w_out_l, g_ffn, w_gu_l, w_down_l, g_fin):
    t_pad = h0.shape[0]
    me = 4 * lax.axis_index("x") + 2 * lax.axis_index("y") + lax.axis_index("c")
    p, u, y, hs, o, sc, w_out, w_gu, w_down = _mixer_fwd(h0, g_mix, w_in, wr, wi, vec, hb, g_hg,
                                                         [w_out_l, w_gu_l, w_down_l])
    w_out = w_out.reshape(D, D)
    w_down = w_down.reshape(4, FFB, D)
    h1, v, gu, act, dh2, dh2b, loss, gfin = _ffn_loss(h0, y, w_out, g_ffn, w_gu, w_down, g_fin, tgt_p, n_valid)

    dgu, dh1, dh1b, dy, gffn = _ffn_bwd(dh2, dh2b, gu, h1, g_ffn, w_gu, w_down, w_out)
    g_wdown = _wgrad("wgrad_down", act, dh2b, pl.BlockSpec((1, t_pad, FFB), lambda j: (j, 0, 0)),
                     pl.BlockSpec((t_pad, D), lambda j: (0, 0)), 4, (FFB, D))
    g_wdown = g_wdown.reshape(N_DEV, D_FF // N_DEV, D)
    g_wgu, r_wdown_a = _wgrad("wgrad_gate_up", dgu, v, pl.BlockSpec((1, t_pad, FFB), lambda j: (j, 0, 0)),
                              pl.BlockSpec((t_pad, D), lambda j: (0, 0)), N_DEV, (FFB, D),
                              scatter=[g_wdown], windows=[(0, WDOWN_A)])
    g_wout = _wgrad("wgrad_out", y, dh1b, pl.BlockSpec((t_pad, D // N_DEV), lambda j: (0, j)),
                    pl.BlockSpec((t_pad, D), lambda j: (0, 0)), N_DEV, (D // N_DEV, D))
    dp, gvec, gw, r_wgu, r_wout, r_wdown_b = _mixer_bwd(
        p, hs, o, sc, dy, wr, wi, vec, hb, g_hg, [g_wgu, g_wout, g_wdown],
        [None, None, (WDOWN_A, D_FF // N_DEV - WDOWN_A)])
    r_wdown = (r_wdown_a, r_wdown_b)
    pack_c = jnp.concatenate([_diag_blocks(gw[0]), _diag_blocks(gw[1])], axis=1).astype(_BF)
    order = (me ^ jnp.array(_SEND_ORDER, jnp.int32)).astype(jnp.int32)
    dh0, r_win, all_b, all_c, all_a = _inproj_bwd_send(dp, w_in, h0, dh1, g_mix, u, order, gffn, gfin, loss,
                                                       [gvec, pack_c])
    return dh0, (r_win, r_wgu, r_wout, r_wdown), (all_a, all_b, all_c)


def kernel(x, meta_tokens, mix_norm_g, w_in, conv_w, conv_b, w_rgate, b_rgate, w_igate, b_igate, lru_lambda, rg_norm_g, hg_lower_bound, hg_norm_g, w_out, ffn_norm_g, w_gate_up, w_down, final_norm_g, loss_target, m_meta_tokens, m_mix_norm_g, m_w_in, m_conv_w, m_conv_b, m_w_rgate, m_b_rgate, m_w_igate, m_b_igate, m_lru_lambda, m_rg_norm_g, m_hg_lower_bound, m_hg_norm_g, m_w_out, m_ffn_norm_g, m_w_gate_up, m_w_down, m_final_norm_g, v_meta_tokens, v_mix_norm_g, v_w_in, v_conv_w, v_conv_b, v_w_rgate, v_b_rgate, v_w_igate, v_b_igate, v_lru_lambda, v_rg_norm_g, v_hg_lower_bound, v_hg_norm_g, v_w_out, v_ffn_norm_g, v_w_gate_up, v_w_down, v_final_norm_g):
    seq = x.shape[1]
    me = 4 * lax.axis_index("x") + 2 * lax.axis_index("y") + lax.axis_index("c")

    n_valid = N_META + seq
    small_l = jnp.concatenate([meta_tokens, jnp.pad(conv_w[0], ((0, 4), (0, 64)))], axis=0)
    h0, tgt_p, small_g, w_in_g, w_gu_l, w_out_l, w_down_l = _prologue(
        x[0], loss_target[0], small_l, w_in[0], [w_gate_up[0].T, w_out[0], w_down[0]])
    conv_w_full = jnp.transpose(small_g[:, N_META:N_META + 4, :64], (1, 0, 2)).reshape(4, D_RG)
    vec = jnp.concatenate([conv_b, b_rgate, b_igate, lru_lambda, rg_norm_g, jnp.zeros((3, D_RG), _F32),
                           conv_w_full, jnp.zeros((4, D_RG), _F32)], axis=0)
    wr = _block_diag(w_rgate[0]).astype(_BF)
    wi = _block_diag(w_igate[0]).astype(_BF)

    dh0, (r_win, r_wgu, r_wout, r_wdown), (all_a, all_b, all_c) = _local_step(
        h0, tgt_p, n_valid, mix_norm_g, w_in_g, vec, wr, wi, hg_lower_bound, hg_norm_g,
        w_out_l, ffn_norm_g, w_gu_l, w_down_l, final_norm_g.reshape(1, D))
    grad_x = dh0[N_META:N_META + seq][None]

    outs = {}
    outs["w_in"] = _adamw_big("adamw_w_in", r_win, w_in[0], m_w_in[0], v_w_in[0], 256)
    outs["w_gate_up"] = [r.T for r in _adamw_big("adamw_w_gate_up", r_wgu, w_gate_up[0].T, m_w_gate_up[0].T,
                                                 v_w_gate_up[0].T, 176)]
    outs["w_out"] = _adamw_big("adamw_w_out", r_wout, w_out[0], m_w_out[0], v_w_out[0], 128)
    outs["w_down"] = _adamw_parts("adamw_w_down", r_wdown, w_down[0], m_w_down[0], v_w_down[0])

    meta_part = lax.dynamic_slice_in_dim(all_a[:, R_META:R_META + N_META, :], me * 128, 128, axis=2)
    convw_part = lax.dynamic_slice_in_dim(all_b[:, R_CONVW:R_CONVW + 4, :], me * 64, 64, axis=2)
    gathered = [all_a, all_b, all_c, meta_part, convw_part]
    small_params = [
        ("meta_tokens", (3, 0, N_META, 0, 128), (meta_tokens, m_meta_tokens, v_meta_tokens), (N_META, 128)),
        ("mix_norm_g", (0, R_GMIX, 1, 0, D), (mix_norm_g, m_mix_norm_g, v_mix_norm_g), (1, D)),
        ("conv_w", (4, 0, 4, 0, 64), (conv_w, m_conv_w, v_conv_w), (4, 64)),
        ("conv_b", (1, R_CONVB, 1, 0, D_RG), (conv_b, m_conv_b, v_conv_b), (1, D_RG)),
        ("w_rgate", (2, 0, 512, 0, 64), (w_rgate, m_w_rgate, v_w_rgate), (512, 64)),
        ("b_rgate", (1, R_BR, 1, 0, D_RG), (b_rgate, m_b_rgate, v_b_rgate), (1, D_RG)),
        ("w_igate", (2, 0, 512, 64, 64), (w_igate, m_w_igate, v_w_igate), (512, 64)),
        ("b_igate", (1, R_BI, 1, 0, D_RG), (b_igate, m_b_igate, v_b_igate), (1, D_RG)),
        ("lru_lambda", (1, R_LAM, 1, 0, D_RG), (lru_lambda, m_lru_lambda, v_lru_lambda), (1, D_RG)),
        ("rg_norm_g", (1, R_GRG, 1, 0, D_RG), (rg_norm_g, m_rg_norm_g, v_rg_norm_g), (1, D_RG)),
        ("hg_lower_bound", (1, R_HB0, 2, 0, D_HG), (hg_lower_bound, m_hg_lower_bound, v_hg_lower_bound), (2, D_HG)),
        ("hg_norm_g", (1, R_GHG, 1, 0, HD), (hg_norm_g, m_hg_norm_g, v_hg_norm_g), (1, HD)),
        ("ffn_norm_g", (0, R_GFFN, 1, 0, D), (ffn_norm_g, m_ffn_norm_g, v_ffn_norm_g), (1, D)),
        ("final_norm_g", (0, R_GFIN, 1, 0, D), (final_norm_g, m_final_norm_g, v_final_norm_g), (1, D)),
    ]
    res = _adamw_small(gathered, [s[1] for s in small_params],
                       [tuple(t.reshape(s[3]) for t in s[2]) for s in small_params])
    for i, s in enumerate(small_params):
        outs[s[0]] = [r.reshape(s[2][0].shape) for r in res[4 * i:4 * i + 4]]
    for n, ref in (("w_in", w_in), ("w_gate_up", w_gate_up), ("w_out", w_out), ("w_down", w_down)):
        outs[n] = [r.reshape(ref.shape) for r in outs[n]]

    loss_all = res[4 * len(small_params)][0, 0]
    order = ["meta_tokens", "mix_norm_g", "w_in", "conv_w", "conv_b", "w_rgate", "b_rgate", "w_igate", "b_igate",
             "lru_lambda", "rg_norm_g", "hg_lower_bound", "hg_norm_g", "w_out", "ffn_norm_g", "w_gate_up", "w_down",
             "final_norm_g"]
    return (loss_all, grad_x, *[outs[n][0] for n in order], *[outs[n][1] for n in order],
            *[outs[n][2] for n in order], *[outs[n][3] for n in order])
```

```python
import functools

import jax
import jax.numpy as jnp
from jax import lax
from jax.experimental import pallas as pl
from jax.experimental.pallas import tpu as pltpu

_BF = jnp.bfloat16
_F32 = jnp.float32
_S = jax.ShapeDtypeStruct
_MESH = pl.DeviceIdType.MESH

N_DEV = 8
N_META = 16
D = 1024
D_RG = 512
D_HG = 512
HD = 128
NH = D_HG // HD
D_IN = 3072
D_FF = 2816
FFB = D_FF // 4
WIN_B = D_IN // N_DEV
WIN_P = 2 * WIN_B
WDOWN_A = 256
EPS = 1e-6
LRU_C = 8.0
TM = 320
HC = 64
VMEM_LIMIT = 62 * 1024 * 1024

ADAM_LR = 0.001
ADAM_B1 = 0.9
ADAM_B2 = 0.999
ADAM_EPS = 1e-08
ADAM_WD = 0.01
ADAM_STEP = 10

_SEND_ORDER = (6, 4, 2, 7, 5, 3, 1, 0)

R_CONVB, R_BR, R_BI, R_LAM, R_GRG, R_HB0, R_HB1, R_GHG, R_CONVW = 0, 1, 2, 3, 4, 5, 6, 7, 8
R_GMIX, R_GFFN, R_GFIN, R_LOSS, R_META = 0, 1, 2, 3, 8


def _cp(sem=None, **kw):
    return pltpu.CompilerParams(dimension_semantics=sem, vmem_limit_bytes=VMEM_LIMIT, **kw)


def _dot(a, b):
    return jnp.dot(a, b, preferred_element_type=_F32)


def _dot_nt(a, b):
    return lax.dot_general(a, b, (((1,), (1,)), ((), ())), preferred_element_type=_F32)


def _dot_tn(a, b):
    return lax.dot_general(a, b, (((0,), (0,)), ((), ())), preferred_element_type=_F32)


def _sigmoid(x):
    return 0.5 * jnp.tanh(0.5 * x) + 0.5


def _dsilu(x, s):
    return s * (1.0 + x * (1.0 - s))


_GELU_C = 0.7978845608028654


def _gelu_parts(x):
    t = jnp.tanh(_GELU_C * (x + 0.044715 * (x * x * x)))
    g = 0.5 * x * (1.0 + t)
    dg = 0.5 * (1.0 + t) + 0.5 * x * (1.0 - t * t) * (_GELU_C * (1.0 + 3.0 * 0.044715 * (x * x)))
    return g, dg


def _softplus(z):
    e = jnp.exp(-jnp.abs(z))
    w = 1.0 + e
    l1p = jnp.where(w == 1.0, e, jnp.log(w) * e / jnp.where(w == 1.0, 1.0, w - 1.0))
    return jnp.maximum(z, 0.0) + l1p


def _rms_fwd(x):
    r = lax.rsqrt(jnp.mean(x * x, axis=-1, keepdims=True) + EPS)
    return x * r, r


def _rms_bwd(dyg, n, r):
    return r * (dyg - n * jnp.mean(dyg * n, axis=-1, keepdims=True))


def _full(shape):
    nd = len(shape)
    return pl.BlockSpec(shape, lambda i: (0,) * nd)


def _const(shape):
    nd = len(shape)
    return pl.BlockSpec(shape, lambda i: (0,) * nd, pipeline_mode=pl.Buffered(1))


def _carry_gather(gather, i, nt):
    @pl.when(i == 0)
    def _():
        gather.start()

    def tail():
        for j in range(3):
            @pl.when(i == max(nt - 5 + j, 0))
            def _(j=j):
                gather.forward(j)

        @pl.when(i == nt - 1)
        def _():
            gather.finish()

    return tail


def _pair_place(ref, block):
    return ref.at[block // 2, :, pl.ds(pl.multiple_of((block % 2) * WIN_B, WIN_B), WIN_B)]


def _rg_gates(xc, wr_ref, wi_ref, vec_ref):
    xcb = xc.astype(_BF)
    r = _sigmoid(_dot(xcb, wr_ref[...]) + vec_ref[R_BR:R_BR + 1, :])
    ig = _sigmoid(_dot(xcb, wi_ref[...]) + vec_ref[R_BI:R_BI + 1, :])
    nsp8 = -LRU_C * _softplus(-vec_ref[R_LAM:R_LAM + 1, :])
    la = nsp8 * r
    a = jnp.exp(la)
    th = jnp.tanh(la)
    s = jnp.sqrt(-2.0 * th / (1.0 - th))
    return r, ig, a, s, nsp8


def _conv(xbuf, vec_ref):
    acc = vec_ref[R_CONVW:R_CONVW + 1, :] * xbuf[pl.ds(5, TM), :]
    for j in range(1, 4):
        acc = acc + vec_ref[R_CONVW + j:R_CONVW + j + 1, :] * xbuf[pl.ds(5 + j, TM), :]
    return vec_ref[R_CONVB:R_CONVB + 1, :] + acc


def _dot3(m01, x):
    hi = x.astype(_BF)
    r1 = x - hi.astype(_F32)
    mid = r1.astype(_BF)
    lo = (r1 - mid.astype(_F32)).astype(_BF)
    return (_dot(m01, lo) + _dot(m01, mid)) + _dot(m01, hi)


def _chunk_dot3(m01, x):
    return jnp.concatenate([_dot3(m01, x[HC * c:HC * (c + 1), :]) for c in range(x.shape[0] // HC)], axis=0)


def _chunk_masks():
    row = lax.broadcasted_iota(jnp.int32, (HC, HC), 0)
    col = lax.broadcasted_iota(jnp.int32, (HC, HC), 1)
    return (row >= col).astype(_BF), (col >= row).astype(_BF), jnp.ones((HC, HC), _BF)


def _per_chunk_rows(x, r):
    return jnp.concatenate([jnp.broadcast_to(x[HC * c + r:HC * c + r + 1, :], (HC, x.shape[1]))
                            for c in range(TM // HC)], axis=0)


def _hg_prep(p_ref, lb, tri):
    hq = p_ref[:, pl.ds(2 * D_RG, D_HG)]
    hf = p_ref[:, pl.ds(2 * D_RG + D_HG, D_HG)]
    sq = _sigmoid(hq)
    q = hq * sq
    sg = _sigmoid(hf)
    f = lb + (1.0 - lb) * sg
    k = 1.0 - f
    b = _chunk_dot3(tri, jnp.log(f))
    bm = _per_chunk_rows(b, HC // 2 - 1)
    bl = _per_chunk_rows(b, HC - 1)
    e_q = jnp.exp(b - bm)
    e_k = jnp.exp(bm - b)
    e_b = jnp.exp(b)
    e_l = jnp.exp(bl - b)
    return dict(hq=hq, sq=sq, q=q, sg=sg, f=f, k=k, e_q=e_q, e_k=e_k, e_b=e_b, e_l=e_l,
                qd=q * e_q, kd=k * e_k, qe=q * e_b, ke=k * e_l, e_end=jnp.exp(bl))


def _mixer_fwd(h0, g_mix, w_in, wr, wi, vec, hb, g_hg, shards):
    t_pad = h0.shape[0]
    nt = t_pad // TM
    nc_t = TM // HC
    nsh = len(shards)

    def body(h_ref, gmix_ref, win_ref, wr_ref, wi_ref, vec_ref, hb_ref, ghg_ref, *rest):
        sh_refs, rest = rest[:nsh], rest[nsh:]
        pout_ref, uout_ref, y_ref, hs_ref, o_ref, sc_ref = rest[:6]
        gath_refs, rest = rest[6:6 + nsh], rest[6 + nsh:]
        xbuf, a_s, b_s, hcar, st, qd_s, kd_s, qe_s, ke_s, v_s, u_s, p_s, p_ref = rest[:13]
        i = pl.program_id(0)
        tail = _carry_gather(_Gather(sh_refs, gath_refs, rest[13:]), i, nt + 1)

        @pl.when(i == 0)
        def _():
            p_s[...] = jnp.zeros_like(p_s)

        p_ref[...] = p_s[...]

        @pl.when(i <= 1)
        def _():
            xbuf[pl.ds(0, 8), :] = jnp.zeros((8, D_RG), _F32)
            hcar[...] = jnp.zeros_like(hcar)
            st[...] = jnp.zeros_like(st)

        n_h, _ = _rms_fwd(h_ref[...])
        u = (n_h * gmix_ref[...]).astype(_BF)
        uout_ref[...] = u
        pieces = [(j, k) for j in range(4) for k in range(WIN_P // 256)]

        def project(count):
            for _ in range(count):
                j, k = pieces.pop(0)
                blk = _dot(u, win_ref[j, :, pl.ds(256 * k, 256)])
                p_s[:, pl.ds(WIN_P * j + 256 * k, 256)] = blk
                pout_ref[:, pl.ds(WIN_P * j + 256 * k, 256)] = blk

        x = p_ref[:, pl.ds(0, D_RG)]
        xbuf[pl.ds(8, TM), :] = x
        xc = _conv(xbuf, vec_ref)
        xbuf[pl.ds(0, 8), :] = x[TM - 8:, :]
        r, ig, a, s, _ = _rg_gates(xc, wr_ref, wi_ref, vec_ref)
        a_s[...] = a
        b_s[...] = s * (ig * xc)

        def step(t, h):
            h = a_s[pl.ds(t, 1), :] * h + b_s[pl.ds(t, 1), :]
            hs_ref[pl.ds(t, 1), :] = h
            return h

        hcar[pl.ds(0, 1), :] = lax.fori_loop(0, TM, step, hcar[pl.ds(0, 1), :], unroll=8)
        gel, _ = _gelu_parts(p_ref[:, pl.ds(D_RG, D_RG)])
        n, _ = _rms_fwd(gel * hs_ref[...])
        y_ref[:, pl.ds(0, D_RG)] = (n * vec_ref[R_GRG:R_GRG + 1, :]).astype(_BF)

        lb = _sigmoid(hb_ref[0:1, :] - hb_ref[1:2, :])
        tri, _, _ = _chunk_masks()
        q = _hg_prep(p_ref, lb, tri)
        for name, ref in (("qd", qd_s), ("kd", kd_s), ("qe", qe_s), ("ke", ke_s)):
            ref[...] = q[name].astype(_BF)
        v_s[...] = p_ref[:, pl.ds(2 * D_RG + 2 * D_HG, D_HG)].astype(_BF)
        e_end = q["e_end"]
        causal = (lax.broadcasted_iota(jnp.int32, (HC, HC), 0) >= lax.broadcasted_iota(jnp.int32, (HC, HC), 1))
        for c in range(nc_t):
            for h in range(NH):
                rs, cs = pl.ds(HC * c, HC), pl.ds(HD * h, HD)
                amat = jnp.where(causal, _dot_nt(qd_s[rs, cs], kd_s[rs, cs]), 0.0)
                o_ref[rs, cs] = _dot(amat.astype(_BF), v_s[rs, cs])
                u_s[NH * c + h] = _dot_tn(v_s[rs, cs], ke_s[rs, cs])
                if pieces:
                    project(1)
        assert not pieces
        for h in range(NH):
            cs = pl.ds(HD * h, HD)
            s_run = st[h]
            for c in range(nc_t):
                rs = pl.ds(HC * c, HC)
                sc_ref[c, h] = s_run
                o_ref[rs, cs] += _dot_nt(qe_s[rs, cs], s_run.astype(_BF))
                s_run = e_end[HC * c:HC * c + 1, HD * h:HD * (h + 1)] * s_run + u_s[NH * c + h]
            st[h] = s_run
        for h in range(NH):
            cs = pl.ds(HD * h, HD)
            n_o, _ = _rms_fwd(o_ref[:, cs])
            hg = p_ref[:, pl.ds(2 * D_RG + 3 * D_HG + HD * h, HD)]
            y_ref[:, pl.ds(D_RG + HD * h, HD)] = ((n_o * ghg_ref[...]) * (hg * _sigmoid(hg))).astype(_BF)

        tail()

    hbm = pl.BlockSpec(memory_space=pl.ANY)

    def proj(i):
        return jnp.minimum(i, nt - 1)

    def mixed(i):
        return jnp.maximum(i - 1, 0)

    return pl.pallas_call(
        body, name="mixer_fwd", grid=(nt + 1,),
        in_specs=[pl.BlockSpec((TM, D), lambda i: (proj(i), 0)), _full((1, D)), _const((4, D, WIN_P)),
                  _full((D_RG, D_RG)), _full((D_RG, D_RG)),
                  _full((16, D_RG)), _full((2, D_HG)), _full((1, HD))] + [hbm] * nsh,
        out_specs=[pl.BlockSpec((TM, D_IN), lambda i: (proj(i), 0)), pl.BlockSpec((TM, D), lambda i: (proj(i), 0)),
                   pl.BlockSpec((TM, D), lambda i: (mixed(i), 0)), pl.BlockSpec((TM, D_RG), lambda i: (mixed(i), 0)),
                   pl.BlockSpec((TM, D_HG), lambda i: (mixed(i), 0)),
                   pl.BlockSpec((nc_t, NH, HD, HD), lambda i: (mixed(i), 0, 0, 0))] + [hbm] * nsh,
        out_shape=[_S((t_pad, D_IN), _F32), _S((t_pad, D), _BF),
                   _S((t_pad, D), _BF), _S((t_pad, D_RG), _F32), _S((t_pad, D_HG), _F32),
                   _S((t_pad // HC, NH, HD, HD), _F32)] + [_S((N_DEV,) + s.shape, s.dtype) for s in shards],
        scratch_shapes=[pltpu.VMEM((TM + 8, D_RG), _F32), pltpu.VMEM((TM, D_RG), _F32),
                        pltpu.VMEM((TM, D_RG), _F32), pltpu.VMEM((8, D_RG), _F32),
                        pltpu.VMEM((NH, HD, HD), _F32)] + [pltpu.VMEM((TM, D_HG), _BF) for _ in range(5)]
        + [pltpu.VMEM((nc_t * NH, HD, HD), _F32), pltpu.VMEM((TM, D_IN), _F32), pltpu.VMEM((TM, D_IN), _F32)]
        + _sem_shapes(nsh),
        compiler_params=_cp(("arbitrary",)),
    )(h0, g_mix, w_in, wr, wi, vec, hb, g_hg, *shards)


def _ffn_loss(h0, y, w_out, g_ffn, w_gu, w_down, g_fin, tgt, n_valid):
    t_pad = h0.shape[0]

    def body(h_ref, y_ref, wo_ref, gffn_ref, wgu_ref, wd_ref, g_ref, t_ref,
             h1_ref, v_ref, gu_ref, act_ref, dh2_ref, dh2b_ref, loss_ref, gfin_ref):
        i = pl.program_id(0)

        @pl.when(i == 0)
        def _():
            loss_ref[...] = jnp.zeros_like(loss_ref)
            gfin_ref[...] = jnp.zeros_like(gfin_ref)

        h1 = h_ref[...] + _dot(y_ref[...], wo_ref[...])
        h1_ref[...] = h1
        n1, _ = _rms_fwd(h1)
        vb = (n1 * gffn_ref[...]).astype(_BF)
        v_ref[...] = vb
        h2 = h1
        for b in range(4):
            gate = _dot_nt(vb, wgu_ref[b])
            up = _dot_nt(vb, wgu_ref[4 + b])
            gu_ref[b] = gate
            gu_ref[4 + b] = up
            act = ((gate * _sigmoid(gate)) * up).astype(_BF)
            act_ref[b] = act
            h2 = h2 + _dot(act, wd_ref[b])
        n, r = _rms_fwd(h2)
        out = n * g_ref[...]
        row = i * TM + lax.broadcasted_iota(jnp.int32, (TM, 1), 0)
        valid = (row >= N_META) & (row < n_valid)
        err = jnp.where(valid, out - t_ref[...], 0.0)
        loss_ref[...] += (0.5 / D) * jnp.sum(err * err)
        dout = err * (1.0 / D)
        gfin_ref[...] += jnp.sum(dout * n, axis=0, keepdims=True)
        dh2 = _rms_bwd(dout * g_ref[...], n, r)
        dh2_ref[...] = dh2
        dh2b_ref[...] = dh2.astype(_BF)

    tile = pl.BlockSpec((TM, D), lambda i: (i, 0))
    return pl.pallas_call(
        body, name="ffn_loss", grid=(t_pad // TM,),
        in_specs=[tile, tile, _const((D, D)), _full((1, D)),
                  _const((N_DEV, FFB, D)), _const((4, FFB, D)), _full((1, D)), tile],
        out_specs=[tile, tile,
                   pl.BlockSpec((N_DEV, TM, FFB), lambda i: (0, i, 0)), pl.BlockSpec((4, TM, FFB), lambda i: (0, i, 0)),
                   tile, tile, _full((8, 128)), _full((1, D))],
        out_shape=[_S((t_pad, D), _F32), _S((t_pad, D), _BF),
                   _S((N_DEV, t_pad, FFB), _F32), _S((4, t_pad, FFB), _BF), _S((t_pad, D), _F32),
                   _S((t_pad, D), _BF), _S((8, 128), _F32), _S((1, D), _F32)],
        compiler_params=_cp(("arbitrary",)),
    )(h0, y, w_out, g_ffn, w_gu, w_down, g_fin, tgt)


def _ffn_bwd(dh2, dh2b, gu, h1, g_ffn, w_gu, w_down, w_out):
    t_pad = dh2.shape[0]

    def body(dh2_ref, dh2b_ref, gu_ref, h1_ref, g_ref, wgu_ref, wd_ref, wo_ref,
             dgu_ref, dh1_ref, dh1b_ref, dy_ref, gffn_ref):
        i = pl.program_id(0)

        @pl.when(i == 0)
        def _():
            gffn_ref[...] = jnp.zeros_like(gffn_ref)

        db = dh2b_ref[...]
        dv = jnp.zeros((TM, D), _F32)
        for b in range(4):
            dact = _dot_nt(db, wd_ref[b])
            gate = gu_ref[b]
            up = gu_ref[4 + b]
            sg = _sigmoid(gate)
            dgate = ((dact * up) * _dsilu(gate, sg)).astype(_BF)
            dup = (dact * (gate * sg)).astype(_BF)
            dgu_ref[b] = dgate
            dgu_ref[4 + b] = dup
            dv = dv + _dot(dgate, wgu_ref[b]) + _dot(dup, wgu_ref[4 + b])
        n, r = _rms_fwd(h1_ref[...])
        gffn_ref[...] += jnp.sum(dv * n, axis=0, keepdims=True)
        dh1 = dh2_ref[...] + _rms_bwd(dv * g_ref[...], n, r)
        dh1_ref[...] = dh1
        dh1b = dh1.astype(_BF)
        dh1b_ref[...] = dh1b
        dy_ref[...] = _dot_nt(dh1b, wo_ref[...])

    tile = pl.BlockSpec((TM, D), lambda i: (i, 0))
    return pl.pallas_call(
        body, name="ffn_bwd", grid=(t_pad // TM,),
        in_specs=[tile, tile, pl.BlockSpec((N_DEV, TM, FFB), lambda i: (0, i, 0)), tile, _full((1, D)),
                  _const((N_DEV, FFB, D)), _const((4, FFB, D)), _const((D, D))],
        out_specs=[pl.BlockSpec((N_DEV, TM, FFB), lambda i: (0, i, 0)), tile, tile, tile, _full((1, D))],
        out_shape=[_S((N_DEV, t_pad, FFB), _BF), _S((t_pad, D), _F32), _S((t_pad, D), _BF),
                   _S((t_pad, D), _F32), _S((1, D), _F32)],
        compiler_params=_cp(("arbitrary",)),
    )(dh2, dh2b, gu, h1, g_ffn, w_gu, w_down, w_out)


def _mixer_bwd(p, hs, o, sc, dy, wr, wi, vec, hb, g_hg, scatter, windows):
    t_pad = p.shape[0]
    nt = t_pad // TM
    nc_t = TM // HC
    nsc = len(scatter)

    def rev(i):
        return nt - 1 - i

    def body(p_ref, pprev_ref, hs_ref, hprev_ref, o_ref, sc_ref, dy_ref, wr_ref, wi_ref, vec_ref, hb_ref, ghg_ref,
             *rest):
        send_refs, rest = rest[:nsc], rest[nsc:]
        dp_ref, gvec_ref, gw_ref = rest[:3]
        recv_refs, rest = rest[3:3 + nsc], rest[3 + nsc:]
        xbuf, hbuf, dbuf, a_s, g_s, ccar, dst = rest[:7]
        qd_s, kd_s, qe_s, ke_s, v_s, do_s, dqd_s, dkd_s, dqe_s, dke_s, dv_s, w_s, dend_s = rest[7:20]
        exchange = _Exchange(send_refs, [], recv_refs, rest[20:], windows)
        i = pl.program_id(0)
        first_tile = i == nt - 1

        @pl.when(i == 0)
        def _():
            exchange.start()
            gvec_ref[...] = jnp.zeros_like(gvec_ref)
            gw_ref[...] = jnp.zeros_like(gw_ref)
            dbuf[pl.ds(TM, 8), :] = jnp.zeros((8, D_RG), _F32)
            ccar[...] = jnp.zeros_like(ccar)
            dst[...] = jnp.zeros_like(dst)

        def acc(row, val):
            gvec_ref[row:row + 1, :] += jnp.sum(val, axis=0, keepdims=True)

        keep = jnp.where(first_tile, 0.0, 1.0)
        x = p_ref[:, pl.ds(0, D_RG)]
        xbuf[pl.ds(0, 8), :] = pprev_ref[...] * keep
        xbuf[pl.ds(8, TM), :] = x
        xc = _conv(xbuf, vec_ref)
        r, ig, a, s, nsp8 = _rg_gates(xc, wr_ref, wi_ref, vec_ref)
        h = hs_ref[...]
        hbuf[pl.ds(0, 8), :] = hprev_ref[...] * keep
        hbuf[pl.ds(8, TM), :] = h
        hm1 = hbuf[pl.ds(7, TM), :]
        gr = p_ref[:, pl.ds(D_RG, D_RG)]
        gel, dgel = _gelu_parts(gr)
        n, rr = _rms_fwd(gel * h)
        dyn = dy_ref[:, pl.ds(0, D_RG)]
        acc(R_GRG, dyn * n)
        dpre = _rms_bwd(dyn * vec_ref[R_GRG:R_GRG + 1, :], n, rr)
        dp_ref[:, pl.ds(D_RG, D_RG)] = ((dpre * h) * dgel).astype(_BF)
        a_s[...] = a
        g_s[...] = dpre * gel

        def step(k, c):
            t = TM - 1 - k
            g = g_s[pl.ds(t, 1), :] + c
            g_s[pl.ds(t, 1), :] = g
            return a_s[pl.ds(t, 1), :] * g

        ccar[pl.ds(0, 1), :] = lax.fori_loop(0, TM, step, ccar[pl.ds(0, 1), :], unroll=8)
        gt = g_s[...]
        da = gt * hm1
        ixc = ig * xc
        ds = gt * ixc
        dig = (gt * s) * xc
        dxc = (gt * s) * ig
        dla = da * a - ds * ((a * a) / s)
        lam = vec_ref[R_LAM:R_LAM + 1, :]
        gvec_ref[R_LAM:R_LAM + 1, :] += jnp.sum(dla * r, axis=0, keepdims=True) * (LRU_C * _sigmoid(-lam))
        dzr = (dla * nsp8) * (r * (1.0 - r))
        dzi = dig * (ig * (1.0 - ig))
        acc(R_BR, dzr)
        acc(R_BI, dzi)
        xcb = xc.astype(_BF)
        dzrb = dzr.astype(_BF)
        dzib = dzi.astype(_BF)
        gw_ref[0] += _dot_tn(xcb, dzrb)
        gw_ref[1] += _dot_tn(xcb, dzib)
        dxc = dxc + _dot_nt(dzrb, wr_ref[...]) + _dot_nt(dzib, wi_ref[...])
        acc(R_CONVB, dxc)
        for j in range(4):
            acc(R_CONVW + j, dxc * xbuf[pl.ds(5 + j, TM), :])
        dbuf[pl.ds(0, TM), :] = dxc
        dx = vec_ref[R_CONVW + 3:R_CONVW + 4, :] * dxc
        for j in range(3):
            dx = dx + vec_ref[R_CONVW + j:R_CONVW + j + 1, :] * dbuf[pl.ds(3 - j, TM), :]
        dbuf[pl.ds(TM, 8), :] = dxc[0:8, :]
        dp_ref[:, pl.ds(0, D_RG)] = dx.astype(_BF)

        lb = _sigmoid(hb_ref[0:1, :] - hb_ref[1:2, :])
        tri, tri_rev, ones = _chunk_masks()
        q = _hg_prep(p_ref, lb, tri)
        qdb, kdb = q["qd"].astype(_BF), q["kd"].astype(_BF)
        qd_s[...] = qdb
        kd_s[...] = kdb
        qe_s[...] = q["qe"].astype(_BF)
        ke_s[...] = q["ke"].astype(_BF)
        v_s[...] = p_ref[:, pl.ds(2 * D_RG + 2 * D_HG, D_HG)].astype(_BF)
        e_end = q["e_end"]
        ghg = ghg_ref[...]
        for h in range(NH):
            cs = pl.ds(HD * h, HD)
            hg = p_ref[:, pl.ds(2 * D_RG + 3 * D_HG + HD * h, HD)]
            sh = _sigmoid(hg)
            n_o, r_o = _rms_fwd(o_ref[:, cs])
            dyh = dy_ref[:, pl.ds(D_RG + HD * h, HD)]
            dp_ref[:, pl.ds(2 * D_RG + 3 * D_HG + HD * h, HD)] = ((dyh * (n_o * ghg)) * _dsilu(hg, sh)).astype(_BF)
            dn = dyh * (hg * sh)
            gvec_ref[R_GHG:R_GHG + 1, pl.ds(0, HD)] += jnp.sum(dn * n_o, axis=0, keepdims=True)
            do_s[:, cs] = _rms_bwd(dn * ghg, n_o, r_o).astype(_BF)
        causal = (lax.broadcasted_iota(jnp.int32, (HC, HC), 0) >= lax.broadcasted_iota(jnp.int32, (HC, HC), 1))
        for c in range(nc_t):
            for h in range(NH):
                rs, cs = pl.ds(HC * c, HC), pl.ds(HD * h, HD)
                qd_c, kd_c, do_c = qd_s[rs, cs], kd_s[rs, cs], do_s[rs, cs]
                amat = jnp.where(causal, _dot_nt(qd_c, kd_c), 0.0).astype(_BF)
                da_m = jnp.where(causal, _dot_nt(do_c, v_s[rs, cs]), 0.0).astype(_BF)
                dqd_s[rs, cs] = _dot(da_m, kd_c)
                dkd_s[rs, cs] = _dot_tn(da_m, qd_c)
                dqe_s[rs, cs] = _dot(do_c, sc_ref[c, h].astype(_BF))
                dv_s[rs, cs] = _dot_tn(amat, do_c)
                w_s[NH * c + h] = _dot_tn(do_c, qe_s[rs, cs])
        for h in range(NH):
            cs = pl.ds(HD * h, HD)
            d_run = dst[h]
            for c in reversed(range(nc_t)):
                rs = pl.ds(HC * c, HC)
                d_b = d_run.astype(_BF)
                dke_s[rs, cs] = _dot(v_s[rs, cs], d_b)
                dp_ref[rs, pl.ds(2 * D_RG + 2 * D_HG + HD * h, HD)] = (
                    dv_s[rs, cs] + _dot_nt(ke_s[rs, cs], d_b)).astype(_BF)
                dend_s[pl.ds(c, 1), cs] = jnp.sum(sc_ref[c, h] * d_run, axis=0, keepdims=True)
                d_run = w_s[NH * c + h] + e_end[HC * c:HC * c + 1, HD * h:HD * (h + 1)] * d_run
            dst[h] = d_run
        dqd, dkd, dqe, dke = dqd_s[...], dkd_s[...], dqe_s[...], dke_s[...]
        dq = dqd * q["e_q"] + dqe * q["e_b"]
        dk = dkd * q["e_k"] + dke * q["e_l"]
        dkeke = dke * q["ke"]
        db = dqd * qdb.astype(_F32) - dkd * kdb.astype(_F32) + dqe * q["qe"] - dkeke
        d_end = jnp.concatenate([jnp.broadcast_to(dend_s[pl.ds(c, 1), :], (HC, D_HG)) for c in range(nc_t)], axis=0)
        dlf = _chunk_dot3(tri_rev, db) + _chunk_dot3(ones, dkeke) + d_end * e_end
        df = dlf / q["f"] - dk
        sg = q["sg"]
        gvec_ref[R_HB0:R_HB0 + 1, :] += jnp.sum(df * (1.0 - sg), axis=0, keepdims=True)
        dp_ref[:, pl.ds(2 * D_RG, D_HG)] = (dq * _dsilu(q["hq"], q["sq"])).astype(_BF)
        dp_ref[:, pl.ds(2 * D_RG + D_HG, D_HG)] = ((df * (1.0 - lb)) * (sg * (1.0 - sg))).astype(_BF)

        @pl.when(i == nt - 1)
        def _():
            glb = gvec_ref[R_HB0:R_HB0 + 1, :] * (lb * (1.0 - lb))
            gvec_ref[R_HB0:R_HB0 + 1, :] = glb
            gvec_ref[R_HB1:R_HB1 + 1, :] = -glb
            exchange.finish()

    hbm = pl.BlockSpec(memory_space=pl.ANY)
    return pl.pallas_call(
        body, name="mixer_bwd", grid=(nt,),
        in_specs=[pl.BlockSpec((TM, D_IN), lambda i: (rev(i), 0)),
                  pl.BlockSpec((8, D_RG), lambda i: (jnp.maximum(rev(i) * (TM // 8) - 1, 0), 0)),
                  pl.BlockSpec((TM, D_RG), lambda i: (rev(i), 0)),
                  pl.BlockSpec((8, D_RG), lambda i: (jnp.maximum(rev(i) * (TM // 8) - 1, 0), 0)),
                  pl.BlockSpec((TM, D_HG), lambda i: (rev(i), 0)),
                  pl.BlockSpec((nc_t, NH, HD, HD), lambda i: (rev(i), 0, 0, 0)),
                  pl.BlockSpec((TM, D), lambda i: (rev(i), 0)),
                  _full((D_RG, D_RG)), _full((D_RG, D_RG)), _full((16, D_RG)), _full((2, D_HG)), _full((1, HD))]
        + [hbm] * nsc,
        out_specs=[pl.BlockSpec((TM, D_IN), lambda i: (rev(i), 0)), _full((16, D_RG)), _full((2, D_RG, D_RG))]
        + [hbm] * nsc,
        out_shape=[_S((t_pad, D_IN), _BF), _S((16, D_RG), _F32), _S((2, D_RG, D_RG), _F32)]
        + _recv_shapes(scatter, windows),
        scratch_shapes=[pltpu.VMEM((TM + 8, D_RG), _F32), pltpu.VMEM((TM + 8, D_RG), _F32),
                        pltpu.VMEM((TM + 8, D_RG), _F32), pltpu.VMEM((TM, D_RG), _F32),
                        pltpu.VMEM((TM, D_RG), _F32), pltpu.VMEM((8, D_RG), _F32),
                        pltpu.VMEM((NH, HD, HD), _F32)]
        + [pltpu.VMEM((TM, D_HG), _BF) for _ in range(6)] + [pltpu.VMEM((TM, D_HG), _F32) for _ in range(5)]
        + [pltpu.VMEM((nc_t * NH, HD, HD), _F32), pltpu.VMEM((8, D_HG), _F32)] + _sem_shapes(nsc),
        compiler_params=_cp(("arbitrary",)),
    )(p, p, hs, hs, o, sc, dy, wr, wi, vec, hb, g_hg, *scatter)


def _inproj_bwd_send(dp, w_in, h0, dh1, g_mix, u, order, gffn, gfin, loss, to_all):
    t_pad = dp.shape[0]
    rb = TM
    n_steps = N_DEV + t_pad // rb
    na = len(to_all)

    def body(order_ref, dpc_ref, dpr_ref, u_ref, w_ref, h_ref, dh1_ref, g_ref, gffn_ref, gfin_ref, loss_ref, *rest):
        all_in = rest[:na]
        dh0_ref, recv_ref = rest[na:na + 2]
        all_out = rest[na + 2:2 * na + 2]
        alla_ref = rest[2 * na + 2]
        buf, pack, blk_send, blk_recv, blk_local = rest[2 * na + 3:2 * na + 8]
        exchange = _Exchange([], all_in, all_out, rest[2 * na + 8:2 * na + 11])
        last = _Exchange([], [pack], [alla_ref], rest[2 * na + 11:])
        s = pl.program_id(0)
        x, y, c = _coords()
        me = 4 * x + 2 * y + c

        def send(step):
            r = _SEND_ORDER[step]
            return pltpu.make_async_remote_copy(
                src_ref=buf.at[step], dst_ref=recv_ref.at[me], send_sem=blk_send.at[step], recv_sem=blk_recv.at[r - 1],
                device_id=(x ^ (r >> 2), y ^ ((r >> 1) & 1), c ^ (r & 1)), device_id_type=_MESH)

        @pl.when(s == 0)
        def _():
            exchange.start()
            pack[...] = jnp.zeros_like(pack)

        @pl.when(s < N_DEV)
        def _():
            buf[s] = _dot_tn(u_ref[...], dpc_ref[...]).astype(_BF)

            for step in range(N_DEV - 1):
                @pl.when(s == step)
                def _(step=step):
                    send(step).start()

        @pl.when(s >= N_DEV)
        def _():
            du = jnp.zeros((rb, D), _F32)
            for j in range(4):
                du = du + _dot_nt(dpr_ref[:, WIN_P * j:WIN_P * (j + 1)], w_ref[j])
            n, r = _rms_fwd(h_ref[...])
            pack[R_GMIX:R_GMIX + 1, :] += jnp.sum(du * n, axis=0, keepdims=True)
            dh0 = dh1_ref[...] + _rms_bwd(du * g_ref[...], n, r)
            dh0_ref[...] = dh0

            @pl.when(s == N_DEV)
            def _():
                pack[R_META:R_META + N_META, :] = dh0[0:N_META, :]

        @pl.when(s == n_steps - 1)
        def _():
            pack[R_GFFN:R_GFFN + 1, :] = gffn_ref[...]
            pack[R_GFIN:R_GFIN + 1, :] = gfin_ref[...]
            pack[R_LOSS:R_LOSS + 1, pl.ds(0, 128)] = loss_ref[0:1, :]
            last.start()
            mine = pltpu.make_async_copy(buf.at[N_DEV - 1], recv_ref.at[me], blk_local.at[0])
            mine.start()
            for step in range(N_DEV - 1):
                send(step).wait_send()
            for r in range(1, N_DEV):
                px, py, pc = x ^ (r >> 2), y ^ ((r >> 1) & 1), c ^ (r & 1)
                pltpu.make_async_remote_copy(
                    src_ref=buf.at[0], dst_ref=recv_ref.at[4 * px + 2 * py + pc], send_sem=blk_send.at[0],
                    recv_sem=blk_recv.at[r - 1], device_id=(px, py, pc), device_id_type=_MESH).wait_recv()
            mine.wait()
            exchange.finish()
            last.finish()

    hbm = pl.BlockSpec(memory_space=pl.ANY)
    rows = pl.BlockSpec((rb, D), lambda s, order: (jnp.maximum(s - N_DEV, 0), 0))
    one = pl.BlockSpec((1, D), lambda s, order: (0, 0))
    res = pl.pallas_call(
        body, name="inproj_bwd_send",
        grid_spec=pltpu.PrefetchScalarGridSpec(
            num_scalar_prefetch=1, grid=(n_steps,),
            in_specs=[pl.BlockSpec((t_pad, WIN_B), lambda s, order: (0, order[jnp.minimum(s, N_DEV - 1)])),
                      pl.BlockSpec((rb, D_IN), lambda s, order: (jnp.maximum(s - N_DEV, 0), 0)),
                      pl.BlockSpec((t_pad, D), lambda s, order: (0, 0), pipeline_mode=pl.Buffered(1)),
                      pl.BlockSpec((4, D, WIN_P), lambda s, order: (0, 0, 0), pipeline_mode=pl.Buffered(1)),
                      rows, rows, one, one, one, pl.BlockSpec((8, 128), lambda s, order: (0, 0))] + [hbm] * na,
            out_specs=[rows] + [hbm] * (na + 2),
            scratch_shapes=[pltpu.VMEM((N_DEV, D, WIN_B), _BF), pltpu.VMEM((24, D), _F32),
                            pltpu.SemaphoreType.DMA((N_DEV - 1,)), pltpu.SemaphoreType.DMA((N_DEV - 1,)),
                            pltpu.SemaphoreType.DMA((1,))] + _sem_shapes(na) + _sem_shapes(1)),
        out_shape=[_S((t_pad, D), _F32), _S((N_DEV, D, WIN_B), _BF)]
        + [_S((N_DEV,) + g.shape, g.dtype) for g in to_all] + [_S((N_DEV, 24, D), _F32)],
        compiler_params=_cp(("arbitrary",)),
    )(order, dp, dp, u, w_in, h0, dh1, g_mix, gffn, gfin, loss, *to_all)
    return res


def _recv_shapes(scatter, windows):
    return [_S(s.shape if w is None else (s.shape[0], w[1]) + s.shape[2:], s.dtype) for s, w in zip(scatter, windows)]


def _wgrad(name, a, b, a_spec, b_spec, n_blocks, out_block, scatter=(), windows=None):
    nsc = len(scatter)
    windows = windows if windows is not None else [None] * nsc

    def body(a_ref, b_ref, *rest):
        o_ref = rest[nsc]
        j = pl.program_id(0)
        if nsc:
            exchange = _Exchange(rest[:nsc], [], rest[nsc + 1:2 * nsc + 1], rest[2 * nsc + 1:], windows)

            @pl.when(j == 0)
            def _():
                exchange.start()

        av = a_ref[0] if len(a_ref.shape) == 3 else a_ref[...]
        bv = b_ref[0] if len(b_ref.shape) == 3 else b_ref[...]
        o_ref[0] = _dot_tn(av, bv).astype(_BF)

        if nsc:
            @pl.when(j == n_blocks - 1)
            def _():
                exchange.finish()

    hbm = pl.BlockSpec(memory_space=pl.ANY)
    res = pl.pallas_call(
        body, name=name, grid=(n_blocks,),
        in_specs=[a_spec, b_spec] + [hbm] * nsc,
        out_specs=[pl.BlockSpec((1,) + out_block, lambda j: (j, 0, 0))] + [hbm] * nsc,
        out_shape=[_S((n_blocks,) + out_block, _BF)] + _recv_shapes(scatter, windows),
        scratch_shapes=_sem_shapes(nsc) if nsc else [],
        compiler_params=_cp(("arbitrary",)),
    )(a, b, *scatter)
    return res if nsc else res[0]


def _coords():
    return lax.axis_index("x"), lax.axis_index("y"), lax.axis_index("c")


def _sem_shapes(na):
    return [pltpu.SemaphoreType.DMA((7 * na,)), pltpu.SemaphoreType.DMA((7 * na,)), pltpu.SemaphoreType.DMA((na,))]


class _Gather:
    def __init__(self, srcs, outs, sems, place=None):
        self.srcs, self.outs = srcs, outs
        self.send_sems, self.recv_sems, self.local_sems = sems
        self.place = place if place is not None else (lambda ref, block: ref.at[block])
        self.na = len(srcs)
        x, y, c = _coords()
        self.pos = (x, y, c)
        self.me = 4 * x + 2 * y + c
        self.sibling = (x, y, 1 - c)
        self.chips = [(1 - x, y), (x, 1 - y), (1 - x, 1 - y)]

    @staticmethod
    def _slot(px, py, pc):
        return 4 * px + 2 * py + pc

    def _copy(self, a, k, block, to, own=False):
        dst = self.place(self.outs[a], block)
        return pltpu.make_async_remote_copy(
            src_ref=self.srcs[a] if own else dst, dst_ref=dst,
            send_sem=self.send_sems.at[7 * a + k], recv_sem=self.recv_sems.at[7 * a + k],
            device_id=to, device_id_type=_MESH)

    def _mine(self, a):
        return pltpu.make_async_copy(self.srcs[a], self.place(self.outs[a], self.me), self.local_sems.at[a])

    def _first(self):
        c = self.pos[2]
        cps = []
        for a in range(self.na):
            cps.append(self._copy(a, 0, self.me, self.sibling, own=True))
            cps += [self._copy(a, 1 + j, self.me, (*chip, c), own=True) for j, chip in enumerate(self.chips)]
        return cps

    def _passed(self):
        c = self.pos[2]
        return [self._copy(a, 4 + j, self._slot(*chip, c), self.sibling)
                for j, chip in enumerate(self.chips) for a in range(self.na)]

    def start(self):
        for a in range(self.na):
            self._mine(a).start()
        for cp in self._first():
            cp.start()

    def forward(self, j):
        c = self.pos[2]
        chip = self.chips[j]
        for a in range(self.na):
            self._copy(a, 1 + j, self._slot(*chip, c), self.pos).wait_recv()
            self._copy(a, 4 + j, self._slot(*chip, c), self.sibling).start()

    def wait_sibling(self):
        x, y, c = self.pos
        for a in range(self.na):
            self._copy(a, 0, self._slot(x, y, 1 - c), self.pos).wait_recv()

    def wait_passed(self, j):
        c = self.pos[2]
        for a in range(self.na):
            self._copy(a, 4 + j, self._slot(*self.chips[j], 1 - c), self.pos).wait_recv()

    def finish_sends(self):
        for cp in self._first() + self._passed():
            cp.wait_send()
        for a in range(self.na):
            self._mine(a).wait()

    def finish(self):
        self.wait_sibling()
        for j in range(3):
            self.wait_passed(j)
        self.finish_sends()


class _Exchange:
    def __init__(self, scatter, gather, outs, sems, windows=None):
        self.windows = windows if windows is not None else [None] * len(scatter)
        self.ins = list(scatter) + list(gather)
        self.ns, self.na = len(scatter), len(scatter) + len(gather)
        self.outs = outs
        self.send_sems, self.recv_sems, self.local_sems = sems
        x, y, c = _coords()
        self.pos = (x, y, c)
        self.me = 4 * x + 2 * y + c

    def _peer(self, r):
        x, y, c = self.pos
        return x ^ (r >> 2), y ^ ((r >> 1) & 1), c ^ (r & 1)

    def _src(self, a, block):
        if a >= self.ns:
            return self.ins[a]
        if self.windows[a] is None:
            return self.ins[a].at[block]
        row0, rows = self.windows[a]
        return self.ins[a].at[block, pl.ds(row0, rows)]

    def _local(self, a):
        return pltpu.make_async_copy(self._src(a, self.me), self.outs[a].at[self.me], self.local_sems.at[a])

    def _send(self, a, r):
        px, py, pc = self._peer(r)
        return pltpu.make_async_remote_copy(
            src_ref=self._src(a, 4 * px + 2 * py + pc), dst_ref=self.outs[a].at[self.me],
            send_sem=self.send_sems.at[7 * a + r - 1], recv_sem=self.recv_sems.at[7 * a + r - 1],
            device_id=(px, py, pc), device_id_type=_MESH)

    def _recv(self, a, r):
        px, py, pc = self._peer(r)
        return pltpu.make_async_remote_copy(
            src_ref=self._src(a, self.me), dst_ref=self.outs[a].at[4 * px + 2 * py + pc],
            send_sem=self.send_sems.at[7 * a + r - 1], recv_sem=self.recv_sems.at[7 * a + r - 1],
            device_id=(px, py, pc), device_id_type=_MESH)

    def start(self):
        for a in range(self.na):
            self._local(a).start()
        for r in range(1, N_DEV):
            for a in range(self.na):
                self._send(a, r).start()

    def finish(self):
        for r in range(1, N_DEV):
            for a in range(self.na):
                self._recv(a, r).wait_recv()
        for r in range(1, N_DEV):
            for a in range(self.na):
                self._send(a, r).wait_send()
        for a in range(self.na):
            self._local(a).wait()


def _prologue(x, tgt, small_l, w_in_l, cast_f32):
    seq = x.shape[0]
    nx = seq // TM
    rest_rows = seq - nx * TM
    nt = nx + 1
    nc = len(cast_f32)
    body_rows = TM - N_META
    assert nx >= 1 and rest_rows % 8 == 0 and rest_rows <= body_rows
    x_rest, t_rest = x[nx * TM:], tgt[nx * TM:]

    def last_tile_body(rest_ref):
        parts = ([rest_ref[...]] if rest_rows else []) + (
            [jnp.zeros((body_rows - rest_rows, D), _F32)] if body_rows > rest_rows else [])
        return parts[0] if len(parts) == 1 else jnp.concatenate(parts, axis=0)

    def body(xm_ref, xp_ref, tm_ref, tp_ref, *rest):
        if rest_rows:
            xr_ref, tr_ref, rest = rest[0], rest[1], rest[2:]
        else:
            xr_ref = tr_ref = None
        s_ref, w_ref, rest = rest[0], rest[1], rest[2:]
        cins = rest[:nc]
        h0_ref, tgt_ref, small_ref, wg_ref = rest[nc:nc + 4]
        couts = rest[nc + 4:2 * nc + 4]
        s_stage, w_stage, meta, msem = rest[2 * nc + 4:2 * nc + 8]
        g_s = _Gather([s_stage], [small_ref], rest[2 * nc + 8:2 * nc + 11])
        g_w = _Gather([w_stage], [wg_ref], rest[2 * nc + 11:], place=_pair_place)
        s = pl.program_id(0)
        i = (s + 1) % nt

        @pl.when(s == 0)
        def _():
            s_stage[...] = s_ref[...]
            w_stage[...] = w_ref[...].astype(_BF)
            g_s.start()
            g_w.start()
            meta[...] = jnp.zeros_like(meta)
            for a in range(nc):
                couts[a][...] = cins[a][...].astype(_BF)

        @pl.when(s == nt - 1)
        def _():
            for j in range(3):
                g_s.forward(j)
            g_s.finish()
            cps = [pltpu.make_async_copy(small_ref.at[k, pl.ds(0, N_META), :], meta.at[:, pl.ds(128 * k, 128)],
                                         msem.at[k]) for k in range(N_DEV)]
            for cp in cps:
                cp.start()
            for cp in cps:
                cp.wait()
            for j in range(3):
                g_w.forward(j)
            g_w.finish()

        has_x = i < nx
        h0_ref[pl.ds(0, N_META), :] = jnp.where(i == 0, meta[...], xp_ref[...])
        h0_ref[pl.ds(N_META, body_rows), :] = jnp.where(has_x, xm_ref[pl.ds(0, body_rows), :], last_tile_body(xr_ref))
        tgt_ref[pl.ds(0, N_META), :] = jnp.where(i == 0, 0.0, tp_ref[...])
        tgt_ref[pl.ds(N_META, body_rows), :] = jnp.where(has_x, tm_ref[pl.ds(0, body_rows), :], last_tile_body(tr_ref))

    def tile_of(s):
        return (s + 1) % nt

    hbm = pl.BlockSpec(memory_space=pl.ANY)
    main = pl.BlockSpec((TM, D), lambda s: (jnp.minimum(tile_of(s), nx - 1), 0))
    prev = pl.BlockSpec((N_META, D), lambda s: (jnp.maximum(tile_of(s) * (TM // N_META) - 1, 0), 0))
    tile = pl.BlockSpec((TM, D), lambda s: (tile_of(s), 0))
    rests = [x_rest, t_rest] if rest_rows else []
    return pl.pallas_call(
        body, name="prologue", grid=(nt,),
        in_specs=[main, prev, main, prev] + [_const(r.shape) for r in rests]
        + [_const(small_l.shape), _const(w_in_l.shape)] + [_const(l.shape) for l in cast_f32],
        out_specs=[tile, tile, hbm, hbm] + [_full(l.shape) for l in cast_f32],
        out_shape=[_S((nt * TM, D), _F32), _S((nt * TM, D), _F32), _S((N_DEV,) + small_l.shape, _F32),
                   _S((4, D, WIN_P), _BF)] + [_S(l.shape, _BF) for l in cast_f32],
        scratch_shapes=[pltpu.VMEM(small_l.shape, _F32), pltpu.VMEM(w_in_l.shape, _BF), pltpu.VMEM((N_META, D), _F32),
                        pltpu.SemaphoreType.DMA((N_DEV,))] + _sem_shapes(1) + _sem_shapes(1),
        compiler_params=_cp(("arbitrary",)),
    )(x, x, tgt, tgt, *rests, small_l, w_in_l, *cast_f32)


def _adamw_math(w, g, m, v):
    m2 = ADAM_B1 * m + (1.0 - ADAM_B1) * g
    v2 = ADAM_B2 * v + (1.0 - ADAM_B2) * (g * g)
    m_hat = m2 / (1.0 - ADAM_B1 ** ADAM_STEP)
    v_hat = v2 / (1.0 - ADAM_B2 ** ADAM_STEP)
    delta = -ADAM_LR * (m_hat / (jnp.sqrt(v_hat) + ADAM_EPS) + ADAM_WD * w)
    return delta, m2, v2


def _adamw_big(name, recv, w, m, v, rows):
    r_all, c_all = w.shape

    def body(r_ref, w_ref, m_ref, v_ref, g_out, d_out, m_out, v_out):
        g = r_ref[0].astype(_F32)
        for k in range(1, N_DEV):
            g = g + r_ref[k].astype(_F32)
        delta, m2, v2 = _adamw_math(w_ref[...], g, m_ref[...], v_ref[...])
        g_out[...] = g
        d_out[...] = delta
        m_out[...] = m2
        v_out[...] = v2

    tile = pl.BlockSpec((rows, c_all), lambda i: (i, 0))
    return pl.pallas_call(
        body, name=name, grid=(r_all // rows,),
        in_specs=[pl.BlockSpec((N_DEV, rows, c_all), lambda i: (0, i, 0)), tile, tile, tile],
        out_specs=[tile] * 4,
        out_shape=[_S(w.shape, _F32)] * 4,
        compiler_params=_cp(("arbitrary",)),
    )(recv, w, m, v)


def _adamw_parts(name, recvs, w, m, v):
    def body(*refs):
        r_refs = refs[:len(recvs)]
        w_ref, m_ref, v_ref, g_out, d_out, m_out, v_out = refs[len(recvs):]
        row0 = 0
        for r_ref in r_refs:
            rows = pl.ds(row0, r_ref.shape[1])
            g = r_ref[0].astype(_F32)
            for k in range(1, N_DEV):
                g = g + r_ref[k].astype(_F32)
            delta, m2, v2 = _adamw_math(w_ref[rows, :], g, m_ref[rows, :], v_ref[rows, :])
            g_out[rows, :] = g
            d_out[rows, :] = delta
            m_out[rows, :] = m2
            v_out[rows, :] = v2
            row0 += r_ref.shape[1]

    return pl.pallas_call(
        body, name=name, out_shape=[_S(w.shape, _F32)] * 4,
        compiler_params=pltpu.CompilerParams(vmem_limit_bytes=VMEM_LIMIT),
    )(*recvs, w, m, v)


def _adamw_small(gathered, slices, wmv):
    ng, npar = len(gathered), len(slices)

    def body(*refs):
        g_refs = refs[:ng]
        wmv_refs = refs[ng:ng + 3 * npar]
        outs = refs[ng + 3 * npar:]
        for i, (ai, r0, nr, c0, ncol) in enumerate(slices):
            g = g_refs[ai][0, pl.ds(r0, nr), pl.ds(c0, ncol)].astype(_F32)
            for k in range(1, N_DEV):
                g = g + g_refs[ai][k, pl.ds(r0, nr), pl.ds(c0, ncol)].astype(_F32)
            w_ref, m_ref, v_ref = wmv_refs[3 * i:3 * i + 3]
            delta, m2, v2 = _adamw_math(w_ref[...], g, m_ref[...], v_ref[...])
            outs[4 * i][...] = g
            outs[4 * i + 1][...] = delta
            outs[4 * i + 2][...] = m2
            outs[4 * i + 3][...] = v2
        total = g_refs[0][0, pl.ds(R_LOSS, 1), pl.ds(0, 128)]
        for k in range(1, N_DEV):
            total = total + g_refs[0][k, pl.ds(R_LOSS, 1), pl.ds(0, 128)]
        outs[4 * npar][...] = total

    flat = [t for trip in wmv for t in trip]
    out_shape = []
    for w, _, _ in wmv:
        out_shape += [_S(w.shape, _F32)] * 4
    out_shape.append(_S((1, 128), _F32))
    return pl.pallas_call(
        body, name="adamw_small", out_shape=out_shape,
        compiler_params=pltpu.CompilerParams(vmem_limit_bytes=VMEM_LIMIT),
    )(*gathered, *flat)


def _block_diag(w):
    eye = jnp.eye(8, dtype=w.dtype)
    return (w[:, :, None, :] * eye[:, None, :, None]).reshape(D_RG, D_RG)


def _diag_blocks(g):
    return jnp.concatenate([g[64 * h:64 * (h + 1), 64 * h:64 * (h + 1)] for h in range(8)], axis=0)


def _local_step(h0, tgt_p, n_valid, g_mix, w_in, vec, wr, wi, hb, g_hg, w_out_l, g_ffn, w_gu_l, w_down_l, g_fin):
    t_pad = h0.shape[0]
    me = 4 * lax.axis_index("x") + 2 * lax.axis_index("y") + lax.axis_index("c")
    p, u, y, hs, o, sc, w_out, w_gu, w_down = _mixer_fwd(h0, g_mix, w_in, wr, wi, vec, hb, g_hg,
                                                         [w_out_l, w_gu_l, w_down_l])
    w_out = w_out.reshape(D, D)
    w_down = w_down.reshape(4, FFB, D)
    h1, v, gu, act, dh2, dh2b, loss, gfin = _ffn_loss(h0, y, w_out, g_ffn, w_gu, w_down, g_fin, tgt_p, n_valid)

    dgu, dh1, dh1b, dy, gffn = _ffn_bwd(dh2, dh2b, gu, h1, g_ffn, w_gu, w_down, w_out)
    g_wdown = _wgrad("wgrad_down", act, dh2b, pl.BlockSpec((1, t_pad, FFB), lambda j: (j, 0, 0)),
                     pl.BlockSpec((t_pad, D), lambda j: (0, 0)), 4, (FFB, D))
    g_wdown = g_wdown.reshape(N_DEV, D_FF // N_DEV, D)
    g_wgu, r_wdown_a = _wgrad("wgrad_gate_up", dgu, v, pl.BlockSpec((1, t_pad, FFB), lambda j: (j, 0, 0)),
                              pl.BlockSpec((t_pad, D), lambda j: (0, 0)), N_DEV, (FFB, D),
                              scatter=[g_wdown], windows=[(0, WDOWN_A)])
    g_wout = _wgrad("wgrad_out", y, dh1b, pl.BlockSpec((t_pad, D // N_DEV), lambda j: (0, j)),
                    pl.BlockSpec((t_pad, D), lambda j: (0, 0)), N_DEV, (D // N_DEV, D))
    dp, gvec, gw, r_wgu, r_wout, r_wdown_b = _mixer_bwd(
        p, hs, o, sc, dy, wr, wi, vec, hb, g_hg, [g_wgu, g_wout, g_wdown],
        [None, None, (WDOWN_A, D_FF // N_DEV - WDOWN_A)])
    r_wdown = (r_wdown_a, r_wdown_b)
    pack_c = jnp.concatenate([_diag_blocks(gw[0]), _diag_blocks(gw[1])], axis=1).astype(_BF)
    order = (me ^ jnp.array(_SEND_ORDER, jnp.int32)).astype(jnp.int32)
    dh0, r_win, all_b, all_c, all_a = _inproj_bwd_send(dp, w_in, h0, dh1, g_mix, u, order, gffn, gfin, loss,
                                                       [gvec, pack_c])
    return dh0, (r_win, r_wgu, r_wout, r_wdown), (all_a, all_b, all_c)


def kernel(x, meta_tokens, mix_norm_g, w_in, conv_w, conv_b, w_rgate, b_rgate, w_igate, b_igate, lru_lambda, rg_norm_g, hg_lower_bound, hg_norm_g, w_out, ffn_norm_g, w_gate_up, w_down, final_norm_g, loss_target, m_meta_tokens, m_mix_norm_g, m_w_in, m_conv_w, m_conv_b, m_w_rgate, m_b_rgate, m_w_igate, m_b_igate, m_lru_lambda, m_rg_norm_g, m_hg_lower_bound, m_hg_norm_g, m_w_out, m_ffn_norm_g, m_w_gate_up, m_w_down, m_final_norm_g, v_meta_tokens, v_mix_norm_g, v_w_in, v_conv_w, v_conv_b, v_w_rgate, v_b_rgate, v_w_igate, v_b_igate, v_lru_lambda, v_rg_norm_g, v_hg_lower_bound, v_hg_norm_g, v_w_out, v_ffn_norm_g, v_w_gate_up, v_w_down, v_final_norm_g):
    seq = x.shape[1]
    me = 4 * lax.axis_index("x") + 2 * lax.axis_index("y") + lax.axis_index("c")

    n_valid = N_META + seq
    small_l = jnp.concatenate([meta_tokens, jnp.pad(conv_w[0], ((0, 4), (0, 64)))], axis=0)
    h0, tgt_p, small_g, w_in_g, w_gu_l, w_out_l, w_down_l = _prologue(
        x[0], loss_target[0], small_l, w_in[0], [w_gate_up[0].T, w_out[0], w_down[0]])
    conv_w_full = jnp.transpose(small_g[:, N_META:N_META + 4, :64], (1, 0, 2)).reshape(4, D_RG)
    vec = jnp.concatenate([conv_b, b_rgate, b_igate, lru_lambda, rg_norm_g, jnp.zeros((3, D_RG), _F32),
                           conv_w_full, jnp.zeros((4, D_RG), _F32)], axis=0)
    wr = _block_diag(w_rgate[0]).astype(_BF)
    wi = _block_diag(w_igate[0]).astype(_BF)

    dh0, (r_win, r_wgu, r_wout, r_wdown), (all_a, all_b, all_c) = _local_step(
        h0, tgt_p, n_valid, mix_norm_g, w_in_g, vec, wr, wi, hg_lower_bound, hg_norm_g,
        w_out_l, ffn_norm_g, w_gu_l, w_down_l, final_norm_g.reshape(1, D))
    grad_x = dh0[N_META:N_META + seq][None]

    outs = {}
    outs["w_in"] = _adamw_big("adamw_w_in", r_win, w_in[0], m_w_in[0], v_w_in[0], 256)
    outs["w_gate_up"] = [r.T for r in _adamw_big("adamw_w_gate_up", r_wgu, w_gate_up[0].T, m_w_gate_up[0].T,
                                                 v_w_gate_up[0].T, 176)]
    outs["w_out"] = _adamw_big("adamw_w_out", r_wout, w_out[0], m_w_out[0], v_w_out[0], 128)
    outs["w_down"] = _adamw_parts("adamw_w_down", r_wdown, w_down[0], m_w_down[0], v_w_down[0])

    meta_part = lax.dynamic_slice_in_dim(all_a[:, R_META:R_META + N_META, :], me * 128, 128, axis=2)
    convw_part = lax.dynamic_slice_in_dim(all_b[:, R_CONVW:R_CONVW + 4, :], me * 64, 64, axis=2)
    gathered = [all_a, all_b, all_c, meta_part, convw_part]
    small_params = [
        ("meta_tokens", (3, 0, N_META, 0, 128), (meta_tokens, m_meta_tokens, v_meta_tokens), (N_META, 128)),
        ("mix_norm_g", (0, R_GMIX, 1, 0, D), (mix_norm_g, m_mix_norm_g, v_mix_norm_g), (1, D)),
        ("conv_w", (4, 0, 4, 0, 64), (conv_w, m_conv_w, v_conv_w), (4, 64)),
        ("conv_b", (1, R_CONVB, 1, 0, D_RG), (conv_b, m_conv_b, v_conv_b), (1, D_RG)),
        ("w_rgate", (2, 0, 512, 0, 64), (w_rgate, m_w_rgate, v_w_rgate), (512, 64)),
        ("b_rgate", (1, R_BR, 1, 0, D_RG), (b_rgate, m_b_rgate, v_b_rgate), (1, D_RG)),
        ("w_igate", (2, 0, 512, 64, 64), (w_igate, m_w_igate, v_w_igate), (512, 64)),
        ("b_igate", (1, R_BI, 1, 0, D_RG), (b_igate, m_b_igate, v_b_igate), (1, D_RG)),
        ("lru_lambda", (1, R_LAM, 1, 0, D_RG), (lru_lambda, m_lru_lambda, v_lru_lambda), (1, D_RG)),
        ("rg_norm_g", (1, R_GRG, 1, 0, D_RG), (rg_norm_g, m_rg_norm_g, v_rg_norm_g), (1, D_RG)),
        ("hg_lower_bound", (1, R_HB0, 2, 0, D_HG), (hg_lower_bound, m_hg_lower_bound, v_hg_lower_bound), (2, D_HG)),
        ("hg_norm_g", (1, R_GHG, 1, 0, HD), (hg_norm_g, m_hg_norm_g, v_hg_norm_g), (1, HD)),
        ("ffn_norm_g", (0, R_GFFN, 1, 0, D), (ffn_norm_g, m_ffn_norm_g, v_ffn_norm_g), (1, D)),
        ("final_norm_g", (0, R_GFIN, 1, 0, D), (final_norm_g, m_final_norm_g, v_final_norm_g), (1, D)),
    ]
    res = _adamw_small(gathered, [s[1] for s in small_params],
                       [tuple(t.reshape(s[3]) for t in s[2]) for s in small_params])
    for i, s in enumerate(small_params):
        outs[s[0]] = [r.reshape(s[2][0].shape) for r in res[4 * i:4 * i + 4]]
    for n, ref in (("w_in", w_in), ("w_gate_up", w_gate_up), ("w_out", w_out), ("w_down", w_down)):
        outs[n] = [r.reshape(ref.shape) for r in outs[n]]

    loss_all = res[4 * len(small_params)][0, 0]
    order = ["meta_tokens", "mix_norm_g", "w_in", "conv_w", "conv_b", "w_rgate", "b_rgate", "w_igate", "b_igate",
             "lru_lambda", "rg_norm_g", "hg_lower_bound", "hg_norm_g", "w_out", "ffn_norm_g", "w_gate_up", "w_down",
             "final_norm_g"]
    return (loss_all, grad_x, *[outs[n][0] for n in order], *[outs[n][1] for n in order],
            *[outs[n][2] for n in order], *[outs[n][3] for n in order])
```

```python
import functools

import jax
import jax.numpy as jnp
from jax import lax
from jax.experimental import pallas as pl
from jax.experimental.pallas import tpu as pltpu

_BF = jnp.bfloat16
_F32 = jnp.float32
_S = jax.ShapeDtypeStruct
_MESH = pl.DeviceIdType.MESH

N_DEV = 8
N_META = 16
D = 1024
D_RG = 512
D_HG = 512
HD = 128
NH = D_HG // HD
D_IN = 3072
D_FF = 2816
FFB = D_FF // 4
WIN_B = D_IN // N_DEV
WIN_P = 2 * WIN_B
WDOWN_A = 256
EPS = 1e-6
LRU_C = 8.0
TM = 320
HC = 64
VMEM_LIMIT = 62 * 1024 * 1024

ADAM_LR = 0.001
ADAM_B1 = 0.9
ADAM_B2 = 0.999
ADAM_EPS = 1e-08
ADAM_WD = 0.01
ADAM_STEP = 10

_SEND_ORDER = (6, 4, 2, 7, 5, 3, 1, 0)

R_CONVB, R_BR, R_BI, R_LAM, R_GRG, R_HB0, R_HB1, R_GHG, R_CONVW = 0, 1, 2, 3, 4, 5, 6, 7, 8
R_GMIX, R_GFFN, R_GFIN, R_LOSS, R_META = 0, 1, 2, 3, 8


def _cp(sem=None, **kw):
    return pltpu.CompilerParams(dimension_semantics=sem, vmem_limit_bytes=VMEM_LIMIT, **kw)


def _dot(a, b):
    return jnp.dot(a, b, preferred_element_type=_F32)


def _dot_nt(a, b):
    return lax.dot_general(a, b, (((1,), (1,)), ((), ())), preferred_element_type=_F32)


def _dot_tn(a, b):
    return lax.dot_general(a, b, (((0,), (0,)), ((), ())), preferred_element_type=_F32)


def _sigmoid(x):
    return 0.5 * jnp.tanh(0.5 * x) + 0.5


def _dsilu(x, s):
    return s * (1.0 + x * (1.0 - s))


_GELU_C = 0.7978845608028654


def _gelu_parts(x):
    t = jnp.tanh(_GELU_C * (x + 0.044715 * (x * x * x)))
    g = 0.5 * x * (1.0 + t)
    dg = 0.5 * (1.0 + t) + 0.5 * x * (1.0 - t * t) * (_GELU_C * (1.0 + 3.0 * 0.044715 * (x * x)))
    return g, dg


def _softplus(z):
    e = jnp.exp(-jnp.abs(z))
    w = 1.0 + e
    l1p = jnp.where(w == 1.0, e, jnp.log(w) * e / jnp.where(w == 1.0, 1.0, w - 1.0))
    return jnp.maximum(z, 0.0) + l1p


def _rms_fwd(x):
    r = lax.rsqrt(jnp.mean(x * x, axis=-1, keepdims=True) + EPS)
    return x * r, r


def _rms_bwd(dyg, n, r):
    return r * (dyg - n * jnp.mean(dyg * n, axis=-1, keepdims=True))


def _full(shape):
    nd = len(shape)
    return pl.BlockSpec(shape, lambda i: (0,) * nd)


def _const(shape):
    nd = len(shape)
    return pl.BlockSpec(shape, lambda i: (0,) * nd, pipeline_mode=pl.Buffered(1))


def _carry_gather(gather, i, nt, early=0):
    @pl.when(i == 0)
    def _():
        gather.start()

    def tail():
        for j in range(3):
            if early:
                @pl.when(i == min(nt // 3 + j, nt - 1))
                def _(j=j):
                    gather.forward(j, range(early))

            @pl.when(i == max(nt - 4 + j, 0))
            def _(j=j):
                gather.forward(j, range(early, gather.na))

        @pl.when(i == nt - 1)
        def _():
            gather.finish()

    return tail


def _pair_place(ref, block):
    return ref.at[block // 2, :, pl.ds(pl.multiple_of((block % 2) * WIN_B, WIN_B), WIN_B)]


def _rg_gates(xc, wr_ref, wi_ref, vec_ref):
    xcb = xc.astype(_BF)
    r = _sigmoid(_dot(xcb, wr_ref[...]) + vec_ref[R_BR:R_BR + 1, :])
    ig = _sigmoid(_dot(xcb, wi_ref[...]) + vec_ref[R_BI:R_BI + 1, :])
    nsp8 = -LRU_C * _softplus(-vec_ref[R_LAM:R_LAM + 1, :])
    la = nsp8 * r
    a = jnp.exp(la)
    th = jnp.tanh(la)
    s = jnp.sqrt(-2.0 * th / (1.0 - th))
    return r, ig, a, s, nsp8


def _conv(xbuf, vec_ref):
    acc = vec_ref[R_CONVW:R_CONVW + 1, :] * xbuf[pl.ds(5, TM), :]
    for j in range(1, 4):
        acc = acc + vec_ref[R_CONVW + j:R_CONVW + j + 1, :] * xbuf[pl.ds(5 + j, TM), :]
    return vec_ref[R_CONVB:R_CONVB + 1, :] + acc


def _dot3(m01, x):
    hi = x.astype(_BF)
    r1 = x - hi.astype(_F32)
    mid = r1.astype(_BF)
    lo = (r1 - mid.astype(_F32)).astype(_BF)
    return (_dot(m01, lo) + _dot(m01, mid)) + _dot(m01, hi)


def _chunk_dot3(m01, x):
    return jnp.concatenate([_dot3(m01, x[HC * c:HC * (c + 1), :]) for c in range(x.shape[0] // HC)], axis=0)


def _chunk_masks():
    row = lax.broadcasted_iota(jnp.int32, (HC, HC), 0)
    col = lax.broadcasted_iota(jnp.int32, (HC, HC), 1)
    return (row >= col).astype(_BF), (col >= row).astype(_BF), jnp.ones((HC, HC), _BF)


def _per_chunk_rows(x, r):
    return jnp.concatenate([jnp.broadcast_to(x[HC * c + r:HC * c + r + 1, :], (HC, x.shape[1]))
                            for c in range(TM // HC)], axis=0)


def _hg_prep(p_ref, lb, tri):
    hq = p_ref[:, pl.ds(2 * D_RG, D_HG)]
    hf = p_ref[:, pl.ds(2 * D_RG + D_HG, D_HG)]
    sq = _sigmoid(hq)
    q = hq * sq
    sg = _sigmoid(hf)
    f = lb + (1.0 - lb) * sg
    k = 1.0 - f
    b = _chunk_dot3(tri, jnp.log(f))
    bm = _per_chunk_rows(b, HC // 2 - 1)
    bl = _per_chunk_rows(b, HC - 1)
    e_q = jnp.exp(b - bm)
    e_k = jnp.exp(bm - b)
    e_b = jnp.exp(b)
    e_l = jnp.exp(bl - b)
    return dict(hq=hq, sq=sq, q=q, sg=sg, f=f, k=k, e_q=e_q, e_k=e_k, e_b=e_b, e_l=e_l,
                qd=q * e_q, kd=k * e_k, qe=q * e_b, ke=k * e_l, e_end=jnp.exp(bl))


def _mixer_fwd(h0, g_mix, w_in, wr, wi, vec, hb, g_hg, shards):
    t_pad = h0.shape[0]
    nt = t_pad // TM
    nc_t = TM // HC
    nsh = len(shards)

    def body(h_ref, gmix_ref, win_ref, wr_ref, wi_ref, vec_ref, hb_ref, ghg_ref, *rest):
        sh_refs, rest = rest[:nsh], rest[nsh:]
        pout_ref, uout_ref, y_ref, hs_ref, o_ref, sc_ref = rest[:6]
        gath_refs, rest = rest[6:6 + nsh], rest[6 + nsh:]
        xbuf, a_s, b_s, hcar, st, qd_s, kd_s, qe_s, ke_s, v_s, u_s, p_s, p_ref = rest[:13]
        i = pl.program_id(0)
        tail = _carry_gather(_Gather(sh_refs, gath_refs, rest[13:]), i, nt + 1, early=1)

        @pl.when(i == 0)
        def _():
            p_s[...] = jnp.zeros_like(p_s)

        p_ref[...] = p_s[...]

        @pl.when(i <= 1)
        def _():
            xbuf[pl.ds(0, 8), :] = jnp.zeros((8, D_RG), _F32)
            hcar[...] = jnp.zeros_like(hcar)
            st[...] = jnp.zeros_like(st)

        n_h, _ = _rms_fwd(h_ref[...])
        u = (n_h * gmix_ref[...]).astype(_BF)
        uout_ref[...] = u
        pieces = [(j, k) for j in range(4) for k in range(WIN_P // 256)]

        def project(count):
            for _ in range(count):
                j, k = pieces.pop(0)
                blk = _dot(u, win_ref[j, :, pl.ds(256 * k, 256)])
                p_s[:, pl.ds(WIN_P * j + 256 * k, 256)] = blk
                pout_ref[:, pl.ds(WIN_P * j + 256 * k, 256)] = blk

        x = p_ref[:, pl.ds(0, D_RG)]
        xbuf[pl.ds(8, TM), :] = x
        xc = _conv(xbuf, vec_ref)
        xbuf[pl.ds(0, 8), :] = x[TM - 8:, :]
        r, ig, a, s, _ = _rg_gates(xc, wr_ref, wi_ref, vec_ref)
        a_s[...] = a
        b_s[...] = s * (ig * xc)

        def step(t, h):
            h = a_s[pl.ds(t, 1), :] * h + b_s[pl.ds(t, 1), :]
            hs_ref[pl.ds(t, 1), :] = h
            return h

        hcar[pl.ds(0, 1), :] = lax.fori_loop(0, TM, step, hcar[pl.ds(0, 1), :], unroll=8)
        gel, _ = _gelu_parts(p_ref[:, pl.ds(D_RG, D_RG)])
        n, _ = _rms_fwd(gel * hs_ref[...])
        y_ref[:, pl.ds(0, D_RG)] = (n * vec_ref[R_GRG:R_GRG + 1, :]).astype(_BF)

        lb = _sigmoid(hb_ref[0:1, :] - hb_ref[1:2, :])
        tri, _, _ = _chunk_masks()
        q = _hg_prep(p_ref, lb, tri)
        for name, ref in (("qd", qd_s), ("kd", kd_s), ("qe", qe_s), ("ke", ke_s)):
            ref[...] = q[name].astype(_BF)
        v_s[...] = p_ref[:, pl.ds(2 * D_RG + 2 * D_HG, D_HG)].astype(_BF)
        e_end = q["e_end"]
        causal = (lax.broadcasted_iota(jnp.int32, (HC, HC), 0) >= lax.broadcasted_iota(jnp.int32, (HC, HC), 1))
        for c in range(nc_t):
            for h in range(NH):
                rs, cs = pl.ds(HC * c, HC), pl.ds(HD * h, HD)
                amat = jnp.where(causal, _dot_nt(qd_s[rs, cs], kd_s[rs, cs]), 0.0)
                o_ref[rs, cs] = _dot(amat.astype(_BF), v_s[rs, cs])
                u_s[NH * c + h] = _dot_tn(v_s[rs, cs], ke_s[rs, cs])
                if pieces:
                    project(1)
        assert not pieces
        for h in range(NH):
            cs = pl.ds(HD * h, HD)
            s_run = st[h]
            for c in range(nc_t):
                rs = pl.ds(HC * c, HC)
                sc_ref[c, h] = s_run
                o_ref[rs, cs] += _dot_nt(qe_s[rs, cs], s_run.astype(_BF))
                s_run = e_end[HC * c:HC * c + 1, HD * h:HD * (h + 1)] * s_run + u_s[NH * c + h]
            st[h] = s_run
        for h in range(NH):
            cs = pl.ds(HD * h, HD)
            n_o, _ = _rms_fwd(o_ref[:, cs])
            hg = p_ref[:, pl.ds(2 * D_RG + 3 * D_HG + HD * h, HD)]
            y_ref[:, pl.ds(D_RG + HD * h, HD)] = ((n_o * ghg_ref[...]) * (hg * _sigmoid(hg))).astype(_BF)

        tail()

    hbm = pl.BlockSpec(memory_space=pl.ANY)

    def proj(i):
        return jnp.minimum(i, nt - 1)

    def mixed(i):
        return jnp.maximum(i - 1, 0)

    return pl.pallas_call(
        body, name="mixer_fwd", grid=(nt + 1,),
        in_specs=[pl.BlockSpec((TM, D), lambda i: (proj(i), 0)), _full((1, D)), _const((4, D, WIN_P)),
                  _full((D_RG, D_RG)), _full((D_RG, D_RG)),
                  _full((16, D_RG)), _full((2, D_HG)), _full((1, HD))] + [hbm] * nsh,
        out_specs=[pl.BlockSpec((TM, D_IN), lambda i: (proj(i), 0)), pl.BlockSpec((TM, D), lambda i: (proj(i), 0)),
                   pl.BlockSpec((TM, D), lambda i: (mixed(i), 0)), pl.BlockSpec((TM, D_RG), lambda i: (mixed(i), 0)),
                   pl.BlockSpec((TM, D_HG), lambda i: (mixed(i), 0)),
                   pl.BlockSpec((nc_t, NH, HD, HD), lambda i: (mixed(i), 0, 0, 0))] + [hbm] * nsh,
        out_shape=[_S((t_pad, D_IN), _F32), _S((t_pad, D), _BF),
                   _S((t_pad, D), _BF), _S((t_pad, D_RG), _F32), _S((t_pad, D_HG), _F32),
                   _S((t_pad // HC, NH, HD, HD), _F32)] + [_S((N_DEV,) + s.shape, s.dtype) for s in shards],
        scratch_shapes=[pltpu.VMEM((TM + 8, D_RG), _F32), pltpu.VMEM((TM, D_RG), _F32),
                        pltpu.VMEM((TM, D_RG), _F32), pltpu.VMEM((8, D_RG), _F32),
                        pltpu.VMEM((NH, HD, HD), _F32)] + [pltpu.VMEM((TM, D_HG), _BF) for _ in range(5)]
        + [pltpu.VMEM((nc_t * NH, HD, HD), _F32), pltpu.VMEM((TM, D_IN), _F32), pltpu.VMEM((TM, D_IN), _F32)]
        + _sem_shapes(nsh),
        compiler_params=_cp(("arbitrary",)),
    )(h0, g_mix, w_in, wr, wi, vec, hb, g_hg, *shards)


def _ffn_loss(h0, y, w_out, g_ffn, w_gu, w_down, g_fin, tgt, n_valid):
    t_pad = h0.shape[0]

    def body(h_ref, y_ref, wo_ref, gffn_ref, wgu_ref, wd_ref, g_ref, t_ref,
             h1_ref, v_ref, gu_ref, act_ref, dh2_ref, dh2b_ref, loss_ref, gfin_ref):
        i = pl.program_id(0)

        @pl.when(i == 0)
        def _():
            loss_ref[...] = jnp.zeros_like(loss_ref)
            gfin_ref[...] = jnp.zeros_like(gfin_ref)

        h1 = h_ref[...] + _dot(y_ref[...], wo_ref[...])
        h1_ref[...] = h1
        n1, _ = _rms_fwd(h1)
        vb = (n1 * gffn_ref[...]).astype(_BF)
        v_ref[...] = vb
        h2 = h1
        for b in range(4):
            gate = _dot_nt(vb, wgu_ref[b])
            up = _dot_nt(vb, wgu_ref[4 + b])
            gu_ref[b] = gate
            gu_ref[4 + b] = up
            act = ((gate * _sigmoid(gate)) * up).astype(_BF)
            act_ref[b] = act
            h2 = h2 + _dot(act, wd_ref[b])
        n, r = _rms_fwd(h2)
        out = n * g_ref[...]
        row = i * TM + lax.broadcasted_iota(jnp.int32, (TM, 1), 0)
        valid = (row >= N_META) & (row < n_valid)
        err = jnp.where(valid, out - t_ref[...], 0.0)
        loss_ref[...] += (0.5 / D) * jnp.sum(err * err)
        dout = err * (1.0 / D)
        gfin_ref[...] += jnp.sum(dout * n, axis=0, keepdims=True)
        dh2 = _rms_bwd(dout * g_ref[...], n, r)
        dh2_ref[...] = dh2
        dh2b_ref[...] = dh2.astype(_BF)

    tile = pl.BlockSpec((TM, D), lambda i: (i, 0))
    return pl.pallas_call(
        body, name="ffn_loss", grid=(t_pad // TM,),
        in_specs=[tile, tile, _const((D, D)), _full((1, D)),
                  _const((N_DEV, FFB, D)), _const((4, FFB, D)), _full((1, D)), tile],
        out_specs=[tile, tile,
                   pl.BlockSpec((N_DEV, TM, FFB), lambda i: (0, i, 0)), pl.BlockSpec((4, TM, FFB), lambda i: (0, i, 0)),
                   tile, tile, _full((8, 128)), _full((1, D))],
        out_shape=[_S((t_pad, D), _F32), _S((t_pad, D), _BF),
                   _S((N_DEV, t_pad, FFB), _F32), _S((4, t_pad, FFB), _BF), _S((t_pad, D), _F32),
                   _S((t_pad, D), _BF), _S((8, 128), _F32), _S((1, D), _F32)],
        compiler_params=_cp(("arbitrary",)),
    )(h0, y, w_out, g_ffn, w_gu, w_down, g_fin, tgt)


def _ffn_bwd(dh2, dh2b, gu, h1, g_ffn, w_gu, w_down, w_out):
    t_pad = dh2.shape[0]

    def body(dh2_ref, dh2b_ref, gu_ref, h1_ref, g_ref, wgu_ref, wd_ref, wo_ref,
             dgu_ref, dh1_ref, dh1b_ref, dy_ref, gffn_ref):
        i = pl.program_id(0)

        @pl.when(i == 0)
        def _():
            gffn_ref[...] = jnp.zeros_like(gffn_ref)

        db = dh2b_ref[...]
        dv = jnp.zeros((TM, D), _F32)
        for b in range(4):
            dact = _dot_nt(db, wd_ref[b])
            gate = gu_ref[b]
            up = gu_ref[4 + b]
            sg = _sigmoid(gate)
            dgate = ((dact * up) * _dsilu(gate, sg)).astype(_BF)
            dup = (dact * (gate * sg)).astype(_BF)
            dgu_ref[b] = dgate
            dgu_ref[4 + b] = dup
            dv = dv + _dot(dgate, wgu_ref[b]) + _dot(dup, wgu_ref[4 + b])
        n, r = _rms_fwd(h1_ref[...])
        gffn_ref[...] += jnp.sum(dv * n, axis=0, keepdims=True)
        dh1 = dh2_ref[...] + _rms_bwd(dv * g_ref[...], n, r)
        dh1_ref[...] = dh1
        dh1b = dh1.astype(_BF)
        dh1b_ref[...] = dh1b
        dy_ref[...] = _dot_nt(dh1b, wo_ref[...])

    tile = pl.BlockSpec((TM, D), lambda i: (i, 0))
    return pl.pallas_call(
        body, name="ffn_bwd", grid=(t_pad // TM,),
        in_specs=[tile, tile, pl.BlockSpec((N_DEV, TM, FFB), lambda i: (0, i, 0)), tile, _full((1, D)),
                  _const((N_DEV, FFB, D)), _const((4, FFB, D)), _const((D, D))],
        out_specs=[pl.BlockSpec((N_DEV, TM, FFB), lambda i: (0, i, 0)), tile, tile, tile, _full((1, D))],
        out_shape=[_S((N_DEV, t_pad, FFB), _BF), _S((t_pad, D), _F32), _S((t_pad, D), _BF),
                   _S((t_pad, D), _F32), _S((1, D), _F32)],
        compiler_params=_cp(("arbitrary",)),
    )(dh2, dh2b, gu, h1, g_ffn, w_gu, w_down, w_out)


def _mixer_bwd(p, hs, o, sc, dy, wr, wi, vec, hb, g_hg, scatter, windows):
    t_pad = p.shape[0]
    nt = t_pad // TM
    nc_t = TM // HC
    nsc = len(scatter)

    def rev(i):
        return nt - 1 - i

    def body(p_ref, pprev_ref, hs_ref, hprev_ref, o_ref, sc_ref, dy_ref, wr_ref, wi_ref, vec_ref, hb_ref, ghg_ref,
             *rest):
        send_refs, rest = rest[:nsc], rest[nsc:]
        dp_ref, gvec_ref, gw_ref = rest[:3]
        recv_refs, rest = rest[3:3 + nsc], rest[3 + nsc:]
        xbuf, hbuf, dbuf, a_s, g_s, ccar, dst = rest[:7]
        qd_s, kd_s, qe_s, ke_s, v_s, do_s, dqd_s, dkd_s, dqe_s, dke_s, dv_s, w_s, dend_s = rest[7:20]
        exchange = _Exchange(send_refs, [], recv_refs, rest[20:], windows)
        i = pl.program_id(0)
        first_tile = i == nt - 1

        @pl.when(i == 0)
        def _():
            exchange.start()
            gvec_ref[...] = jnp.zeros_like(gvec_ref)
            gw_ref[...] = jnp.zeros_like(gw_ref)
            dbuf[pl.ds(TM, 8), :] = jnp.zeros((8, D_RG), _F32)
            ccar[...] = jnp.zeros_like(ccar)
            dst[...] = jnp.zeros_like(dst)

        def acc(row, val):
            gvec_ref[row:row + 1, :] += jnp.sum(val, axis=0, keepdims=True)

        keep = jnp.where(first_tile, 0.0, 1.0)
        x = p_ref[:, pl.ds(0, D_RG)]
        xbuf[pl.ds(0, 8), :] = pprev_ref[...] * keep
        xbuf[pl.ds(8, TM), :] = x
        xc = _conv(xbuf, vec_ref)
        r, ig, a, s, nsp8 = _rg_gates(xc, wr_ref, wi_ref, vec_ref)
        h = hs_ref[...]
        hbuf[pl.ds(0, 8), :] = hprev_ref[...] * keep
        hbuf[pl.ds(8, TM), :] = h
        hm1 = hbuf[pl.ds(7, TM), :]
        gr = p_ref[:, pl.ds(D_RG, D_RG)]
        gel, dgel = _gelu_parts(gr)
        n, rr = _rms_fwd(gel * h)
        dyn = dy_ref[:, pl.ds(0, D_RG)]
        acc(R_GRG, dyn * n)
        dpre = _rms_bwd(dyn * vec_ref[R_GRG:R_GRG + 1, :], n, rr)
        dp_ref[:, pl.ds(D_RG, D_RG)] = ((dpre * h) * dgel).astype(_BF)
        a_s[...] = a
        g_s[...] = dpre * gel

        def step(k, c):
            t = TM - 1 - k
            g = g_s[pl.ds(t, 1), :] + c
            g_s[pl.ds(t, 1), :] = g
            return a_s[pl.ds(t, 1), :] * g

        ccar[pl.ds(0, 1), :] = lax.fori_loop(0, TM, step, ccar[pl.ds(0, 1), :], unroll=8)
        gt = g_s[...]
        da = gt * hm1
        ixc = ig * xc
        ds = gt * ixc
        dig = (gt * s) * xc
        dxc = (gt * s) * ig
        dla = da * a - ds * ((a * a) / s)
        lam = vec_ref[R_LAM:R_LAM + 1, :]
        gvec_ref[R_LAM:R_LAM + 1, :] += jnp.sum(dla * r, axis=0, keepdims=True) * (LRU_C * _sigmoid(-lam))
        dzr = (dla * nsp8) * (r * (1.0 - r))
        dzi = dig * (ig * (1.0 - ig))
        acc(R_BR, dzr)
        acc(R_BI, dzi)
        xcb = xc.astype(_BF)
        dzrb = dzr.astype(_BF)
        dzib = dzi.astype(_BF)
        gw_ref[0] += _dot_tn(xcb, dzrb)
        gw_ref[1] += _dot_tn(xcb, dzib)
        dxc = dxc + _dot_nt(dzrb, wr_ref[...]) + _dot_nt(dzib, wi_ref[...])
        acc(R_CONVB, dxc)
        for j in range(4):
            acc(R_CONVW + j, dxc * xbuf[pl.ds(5 + j, TM), :])
        dbuf[pl.ds(0, TM), :] = dxc
        dx = vec_ref[R_CONVW + 3:R_CONVW + 4, :] * dxc
        for j in range(3):
            dx = dx + vec_ref[R_CONVW + j:R_CONVW + j + 1, :] * dbuf[pl.ds(3 - j, TM), :]
        dbuf[pl.ds(TM, 8), :] = dxc[0:8, :]
        dp_ref[:, pl.ds(0, D_RG)] = dx.astype(_BF)

        lb = _sigmoid(hb_ref[0:1, :] - hb_ref[1:2, :])
        tri, tri_rev, ones = _chunk_masks()
        q = _hg_prep(p_ref, lb, tri)
        qdb, kdb = q["qd"].astype(_BF), q["kd"].astype(_BF)
        qd_s[...] = qdb
        kd_s[...] = kdb
        qe_s[...] = q["qe"].astype(_BF)
        ke_s[...] = q["ke"].astype(_BF)
        v_s[...] = p_ref[:, pl.ds(2 * D_RG + 2 * D_HG, D_HG)].astype(_BF)
        e_end = q["e_end"]
        ghg = ghg_ref[...]
        for h in range(NH):
            cs = pl.ds(HD * h, HD)
            hg = p_ref[:, pl.ds(2 * D_RG + 3 * D_HG + HD * h, HD)]
            sh = _sigmoid(hg)
            n_o, r_o = _rms_fwd(o_ref[:, cs])
            dyh = dy_ref[:, pl.ds(D_RG + HD * h, HD)]
            dp_ref[:, pl.ds(2 * D_RG + 3 * D_HG + HD * h, HD)] = ((dyh * (n_o * ghg)) * _dsilu(hg, sh)).astype(_BF)
            dn = dyh * (hg * sh)
            gvec_ref[R_GHG:R_GHG + 1, pl.ds(0, HD)] += jnp.sum(dn * n_o, axis=0, keepdims=True)
            do_s[:, cs] = _rms_bwd(dn * ghg, n_o, r_o).astype(_BF)
        causal = (lax.broadcasted_iota(jnp.int32, (HC, HC), 0) >= lax.broadcasted_iota(jnp.int32, (HC, HC), 1))
        for c in range(nc_t):
            for h in range(NH):
                rs, cs = pl.ds(HC * c, HC), pl.ds(HD * h, HD)
                qd_c, kd_c, do_c = qd_s[rs, cs], kd_s[rs, cs], do_s[rs, cs]
                amat = jnp.where(causal, _dot_nt(qd_c, kd_c), 0.0).astype(_BF)
                da_m = jnp.where(causal, _dot_nt(do_c, v_s[rs, cs]), 0.0).astype(_BF)
                dqd_s[rs, cs] = _dot(da_m, kd_c)
                dkd_s[rs, cs] = _dot_tn(da_m, qd_c)
                dqe_s[rs, cs] = _dot(do_c, sc_ref[c, h].astype(_BF))
                dv_s[rs, cs] = _dot_tn(amat, do_c)
                w_s[NH * c + h] = _dot_tn(do_c, qe_s[rs, cs])
        for h in range(NH):
            cs = pl.ds(HD * h, HD)
            d_run = dst[h]
            for c in reversed(range(nc_t)):
                rs = pl.ds(HC * c, HC)
                d_b = d_run.astype(_BF)
                dke_s[rs, cs] = _dot(v_s[rs, cs], d_b)
                dp_ref[rs, pl.ds(2 * D_RG + 2 * D_HG + HD * h, HD)] = (
                    dv_s[rs, cs] + _dot_nt(ke_s[rs, cs], d_b)).astype(_BF)
                dend_s[pl.ds(c, 1), cs] = jnp.sum(sc_ref[c, h] * d_run, axis=0, keepdims=True)
                d_run = w_s[NH * c + h] + e_end[HC * c:HC * c + 1, HD * h:HD * (h + 1)] * d_run
            dst[h] = d_run
        dqd, dkd, dqe, dke = dqd_s[...], dkd_s[...], dqe_s[...], dke_s[...]
        dq = dqd * q["e_q"] + dqe * q["e_b"]
        dk = dkd * q["e_k"] + dke * q["e_l"]
        dkeke = dke * q["ke"]
        db = dqd * qdb.astype(_F32) - dkd * kdb.astype(_F32) + dqe * q["qe"] - dkeke
        d_end = jnp.concatenate([jnp.broadcast_to(dend_s[pl.ds(c, 1), :], (HC, D_HG)) for c in range(nc_t)], axis=0)
        dlf = _chunk_dot3(tri_rev, db) + _chunk_dot3(ones, dkeke) + d_end * e_end
        df = dlf / q["f"] - dk
        sg = q["sg"]
        gvec_ref[R_HB0:R_HB0 + 1, :] += jnp.sum(df * (1.0 - sg), axis=0, keepdims=True)
        dp_ref[:, pl.ds(2 * D_RG, D_HG)] = (dq * _dsilu(q["hq"], q["sq"])).astype(_BF)
        dp_ref[:, pl.ds(2 * D_RG + D_HG, D_HG)] = ((df * (1.0 - lb)) * (sg * (1.0 - sg))).astype(_BF)

        @pl.when(i == nt - 1)
        def _():
            glb = gvec_ref[R_HB0:R_HB0 + 1, :] * (lb * (1.0 - lb))
            gvec_ref[R_HB0:R_HB0 + 1, :] = glb
            gvec_ref[R_HB1:R_HB1 + 1, :] = -glb
            exchange.finish()

    hbm = pl.BlockSpec(memory_space=pl.ANY)
    return pl.pallas_call(
        body, name="mixer_bwd", grid=(nt,),
        in_specs=[pl.BlockSpec((TM, D_IN), lambda i: (rev(i), 0)),
                  pl.BlockSpec((8, D_RG), lambda i: (jnp.maximum(rev(i) * (TM // 8) - 1, 0), 0)),
                  pl.BlockSpec((TM, D_RG), lambda i: (rev(i), 0)),
                  pl.BlockSpec((8, D_RG), lambda i: (jnp.maximum(rev(i) * (TM // 8) - 1, 0), 0)),
                  pl.BlockSpec((TM, D_HG), lambda i: (rev(i), 0)),
                  pl.BlockSpec((nc_t, NH, HD, HD), lambda i: (rev(i), 0, 0, 0)),
                  pl.BlockSpec((TM, D), lambda i: (rev(i), 0)),
                  _full((D_RG, D_RG)), _full((D_RG, D_RG)), _full((16, D_RG)), _full((2, D_HG)), _full((1, HD))]
        + [hbm] * nsc,
        out_specs=[pl.BlockSpec((TM, D_IN), lambda i: (rev(i), 0)), _full((16, D_RG)), _full((2, D_RG, D_RG))]
        + [hbm] * nsc,
        out_shape=[_S((t_pad, D_IN), _BF), _S((16, D_RG), _F32), _S((2, D_RG, D_RG), _F32)]
        + _recv_shapes(scatter, windows),
        scratch_shapes=[pltpu.VMEM((TM + 8, D_RG), _F32), pltpu.VMEM((TM + 8, D_RG), _F32),
                        pltpu.VMEM((TM + 8, D_RG), _F32), pltpu.VMEM((TM, D_RG), _F32),
                        pltpu.VMEM((TM, D_RG), _F32), pltpu.VMEM((8, D_RG), _F32),
                        pltpu.VMEM((NH, HD, HD), _F32)]
        + [pltpu.VMEM((TM, D_HG), _BF) for _ in range(6)] + [pltpu.VMEM((TM, D_HG), _F32) for _ in range(5)]
        + [pltpu.VMEM((nc_t * NH, HD, HD), _F32), pltpu.VMEM((8, D_HG), _F32)] + _sem_shapes(nsc),
        compiler_params=_cp(("arbitrary",)),
    )(p, p, hs, hs, o, sc, dy, wr, wi, vec, hb, g_hg, *scatter)


def _inproj_bwd_send(dp, w_in, h0, dh1, g_mix, u, order, gffn, gfin, loss, to_all):
    t_pad = dp.shape[0]
    rb = TM
    n_steps = N_DEV + t_pad // rb
    na = len(to_all)

    def body(order_ref, dpc_ref, dpr_ref, u_ref, w_ref, h_ref, dh1_ref, g_ref, gffn_ref, gfin_ref, loss_ref, *rest):
        all_in = rest[:na]
        dh0_ref, recv_ref = rest[na:na + 2]
        all_out = rest[na + 2:2 * na + 2]
        alla_ref = rest[2 * na + 2]
        buf, pack, blk_send, blk_recv, blk_local = rest[2 * na + 3:2 * na + 8]
        exchange = _Exchange([], all_in, all_out, rest[2 * na + 8:2 * na + 11])
        last = _Exchange([], [pack], [alla_ref], rest[2 * na + 11:])
        s = pl.program_id(0)
        x, y, c = _coords()
        me = 4 * x + 2 * y + c

        def send(step):
            r = _SEND_ORDER[step]
            return pltpu.make_async_remote_copy(
                src_ref=buf.at[step], dst_ref=recv_ref.at[me], send_sem=blk_send.at[step], recv_sem=blk_recv.at[r - 1],
                device_id=(x ^ (r >> 2), y ^ ((r >> 1) & 1), c ^ (r & 1)), device_id_type=_MESH)

        @pl.when(s == 0)
        def _():
            exchange.start()
            pack[...] = jnp.zeros_like(pack)

        @pl.when(s < N_DEV)
        def _():
            buf[s] = _dot_tn(u_ref[...], dpc_ref[...]).astype(_BF)

            for step in range(N_DEV - 1):
                @pl.when(s == step)
                def _(step=step):
                    send(step).start()

        @pl.when(s >= N_DEV)
        def _():
            du = jnp.zeros((rb, D), _F32)
            for j in range(4):
                du = du + _dot_nt(dpr_ref[:, WIN_P * j:WIN_P * (j + 1)], w_ref[j])
            n, r = _rms_fwd(h_ref[...])
            pack[R_GMIX:R_GMIX + 1, :] += jnp.sum(du * n, axis=0, keepdims=True)
            dh0 = dh1_ref[...] + _rms_bwd(du * g_ref[...], n, r)
            dh0_ref[...] = dh0

            @pl.when(s == N_DEV)
            def _():
                pack[R_META:R_META + N_META, :] = dh0[0:N_META, :]

        @pl.when(s == n_steps - 1)
        def _():
            pack[R_GFFN:R_GFFN + 1, :] = gffn_ref[...]
            pack[R_GFIN:R_GFIN + 1, :] = gfin_ref[...]
            pack[R_LOSS:R_LOSS + 1, pl.ds(0, 128)] = loss_ref[0:1, :]
            last.start()
            mine = pltpu.make_async_copy(buf.at[N_DEV - 1], recv_ref.at[me], blk_local.at[0])
            mine.start()
            for step in range(N_DEV - 1):
                send(step).wait_send()
            for r in range(1, N_DEV):
                px, py, pc = x ^ (r >> 2), y ^ ((r >> 1) & 1), c ^ (r & 1)
                pltpu.make_async_remote_copy(
                    src_ref=buf.at[0], dst_ref=recv_ref.at[4 * px + 2 * py + pc], send_sem=blk_send.at[0],
                    recv_sem=blk_recv.at[r - 1], device_id=(px, py, pc), device_id_type=_MESH).wait_recv()
            mine.wait()
            exchange.finish()
            last.finish()

    hbm = pl.BlockSpec(memory_space=pl.ANY)
    rows = pl.BlockSpec((rb, D), lambda s, order: (jnp.maximum(s - N_DEV, 0), 0))
    one = pl.BlockSpec((1, D), lambda s, order: (0, 0))
    res = pl.pallas_call(
        body, name="inproj_bwd_send",
        grid_spec=pltpu.PrefetchScalarGridSpec(
            num_scalar_prefetch=1, grid=(n_steps,),
            in_specs=[pl.BlockSpec((t_pad, WIN_B), lambda s, order: (0, order[jnp.minimum(s, N_DEV - 1)])),
                      pl.BlockSpec((rb, D_IN), lambda s, order: (jnp.maximum(s - N_DEV, 0), 0)),
                      pl.BlockSpec((t_pad, D), lambda s, order: (0, 0), pipeline_mode=pl.Buffered(1)),
                      pl.BlockSpec((4, D, WIN_P), lambda s, order: (0, 0, 0), pipeline_mode=pl.Buffered(1)),
                      rows, rows, one, one, one, pl.BlockSpec((8, 128), lambda s, order: (0, 0))] + [hbm] * na,
            out_specs=[rows] + [hbm] * (na + 2),
            scratch_shapes=[pltpu.VMEM((N_DEV, D, WIN_B), _BF), pltpu.VMEM((24, D), _F32),
                            pltpu.SemaphoreType.DMA((N_DEV - 1,)), pltpu.SemaphoreType.DMA((N_DEV - 1,)),
                            pltpu.SemaphoreType.DMA((1,))] + _sem_shapes(na) + _sem_shapes(1)),
        out_shape=[_S((t_pad, D), _F32), _S((N_DEV, D, WIN_B), _BF)]
        + [_S((N_DEV,) + g.shape, g.dtype) for g in to_all] + [_S((N_DEV, 24, D), _F32)],
        compiler_params=_cp(("arbitrary",)),
    )(order, dp, dp, u, w_in, h0, dh1, g_mix, gffn, gfin, loss, *to_all)
    return res


def _recv_shapes(scatter, windows):
    return [_S(s.shape if w is None else (s.shape[0], w[1]) + s.shape[2:], s.dtype) for s, w in zip(scatter, windows)]


def _wgrad(name, a, b, a_spec, b_spec, n_blocks, out_block, scatter=(), windows=None):
    nsc = len(scatter)
    windows = windows if windows is not None else [None] * nsc

    def body(a_ref, b_ref, *rest):
        o_ref = rest[nsc]
        j = pl.program_id(0)
        if nsc:
            exchange = _Exchange(rest[:nsc], [], rest[nsc + 1:2 * nsc + 1], rest[2 * nsc + 1:], windows)

            @pl.when(j == 0)
            def _():
                exchange.start()

        av = a_ref[0] if len(a_ref.shape) == 3 else a_ref[...]
        bv = b_ref[0] if len(b_ref.shape) == 3 else b_ref[...]
        o_ref[0] = _dot_tn(av, bv).astype(_BF)

        if nsc:
            @pl.when(j == n_blocks - 1)
            def _():
                exchange.finish()

    hbm = pl.BlockSpec(memory_space=pl.ANY)
    res = pl.pallas_call(
        body, name=name, grid=(n_blocks,),
        in_specs=[a_spec, b_spec] + [hbm] * nsc,
        out_specs=[pl.BlockSpec((1,) + out_block, lambda j: (j, 0, 0))] + [hbm] * nsc,
        out_shape=[_S((n_blocks,) + out_block, _BF)] + _recv_shapes(scatter, windows),
        scratch_shapes=_sem_shapes(nsc) if nsc else [],
        compiler_params=_cp(("arbitrary",)),
    )(a, b, *scatter)
    return res if nsc else res[0]


def _coords():
    return lax.axis_index("x"), lax.axis_index("y"), lax.axis_index("c")


def _sem_shapes(na):
    return [pltpu.SemaphoreType.DMA((7 * na,)), pltpu.SemaphoreType.DMA((7 * na,)), pltpu.SemaphoreType.DMA((na,))]


class _Gather:
    def __init__(self, srcs, outs, sems, place=None):
        self.srcs, self.outs = srcs, outs
        self.send_sems, self.recv_sems, self.local_sems = sems
        self.place = place if place is not None else (lambda ref, block: ref.at[block])
        self.na = len(srcs)
        x, y, c = _coords()
        self.pos = (x, y, c)
        self.me = 4 * x + 2 * y + c
        self.sibling = (x, y, 1 - c)
        self.chips = [(1 - x, y), (x, 1 - y), (1 - x, 1 - y)]

    @staticmethod
    def _slot(px, py, pc):
        return 4 * px + 2 * py + pc

    def _copy(self, a, k, block, to, own=False):
        dst = self.place(self.outs[a], block)
        return pltpu.make_async_remote_copy(
            src_ref=self.srcs[a] if own else dst, dst_ref=dst,
            send_sem=self.send_sems.at[7 * a + k], recv_sem=self.recv_sems.at[7 * a + k],
            device_id=to, device_id_type=_MESH)

    def _mine(self, a):
        return pltpu.make_async_copy(self.srcs[a], self.place(self.outs[a], self.me), self.local_sems.at[a])

    def _first(self):
        c = self.pos[2]
        cps = []
        for a in range(self.na):
            cps.append(self._copy(a, 0, self.me, self.sibling, own=True))
            cps += [self._copy(a, 1 + j, self.me, (*chip, c), own=True) for j, chip in enumerate(self.chips)]
        return cps

    def _passed(self):
        c = self.pos[2]
        return [self._copy(a, 4 + j, self._slot(*chip, c), self.sibling)
                for j, chip in enumerate(self.chips) for a in range(self.na)]

    def start(self):
        for a in range(self.na):
            self._mine(a).start()
        for cp in self._first():
            cp.start()

    def forward(self, j, arrays=None):
        c = self.pos[2]
        chip = self.chips[j]
        for a in (range(self.na) if arrays is None else arrays):
            self._copy(a, 1 + j, self._slot(*chip, c), self.pos).wait_recv()
            self._copy(a, 4 + j, self._slot(*chip, c), self.sibling).start()

    def wait_sibling(self):
        x, y, c = self.pos
        for a in range(self.na):
            self._copy(a, 0, self._slot(x, y, 1 - c), self.pos).wait_recv()

    def wait_passed(self, j):
        c = self.pos[2]
        for a in range(self.na):
            self._copy(a, 4 + j, self._slot(*self.chips[j], 1 - c), self.pos).wait_recv()

    def finish_sends(self):
        for cp in self._first() + self._passed():
            cp.wait_send()
        for a in range(self.na):
            self._mine(a).wait()

    def finish(self):
        self.wait_sibling()
        for j in range(3):
            self.wait_passed(j)
        self.finish_sends()


class _Exchange:
    def __init__(self, scatter, gather, outs, sems, windows=None):
        self.windows = windows if windows is not None else [None] * len(scatter)
        self.ins = list(scatter) + list(gather)
        self.ns, self.na = len(scatter), len(scatter) + len(gather)
        self.outs = outs
        self.send_sems, self.recv_sems, self.local_sems = sems
        x, y, c = _coords()
        self.pos = (x, y, c)
        self.me = 4 * x + 2 * y + c

    def _peer(self, r):
        x, y, c = self.pos
        return x ^ (r >> 2), y ^ ((r >> 1) & 1), c ^ (r & 1)

    def _src(self, a, block):
        if a >= self.ns:
            return self.ins[a]
        if self.windows[a] is None:
            return self.ins[a].at[block]
        row0, rows = self.windows[a]
        return self.ins[a].at[block, pl.ds(row0, rows)]

    def _local(self, a):
        return pltpu.make_async_copy(self._src(a, self.me), self.outs[a].at[self.me], self.local_sems.at[a])

    def _send(self, a, r):
        px, py, pc = self._peer(r)
        return pltpu.make_async_remote_copy(
            src_ref=self._src(a, 4 * px + 2 * py + pc), dst_ref=self.outs[a].at[self.me],
            send_sem=self.send_sems.at[7 * a + r - 1], recv_sem=self.recv_sems.at[7 * a + r - 1],
            device_id=(px, py, pc), device_id_type=_MESH)

    def _recv(self, a, r):
        px, py, pc = self._peer(r)
        return pltpu.make_async_remote_copy(
            src_ref=self._src(a, self.me), dst_ref=self.outs[a].at[4 * px + 2 * py + pc],
            send_sem=self.send_sems.at[7 * a + r - 1], recv_sem=self.recv_sems.at[7 * a + r - 1],
            device_id=(px, py, pc), device_id_type=_MESH)

    def start(self):
        for a in range(self.na):
            self._local(a).start()
        for r in range(1, N_DEV):
            for a in range(self.na):
                self._send(a, r).start()

    def finish(self):
        for r in range(1, N_DEV):
            for a in range(self.na):
                self._recv(a, r).wait_recv()
        for r in range(1, N_DEV):
            for a in range(self.na):
                self._send(a, r).wait_send()
        for a in range(self.na):
            self._local(a).wait()


def _prologue(x, tgt, small_l, w_in_l, cast_f32):
    seq = x.shape[0]
    nx = seq // TM
    rest_rows = seq - nx * TM
    nt = nx + 1
    nc = len(cast_f32)
    body_rows = TM - N_META
    assert nx >= 1 and rest_rows % 8 == 0 and rest_rows <= body_rows
    x_rest, t_rest = x[nx * TM:], tgt[nx * TM:]

    def last_tile_body(rest_ref):
        parts = ([rest_ref[...]] if rest_rows else []) + (
            [jnp.zeros((body_rows - rest_rows, D), _F32)] if body_rows > rest_rows else [])
        return parts[0] if len(parts) == 1 else jnp.concatenate(parts, axis=0)

    def body(xm_ref, xp_ref, tm_ref, tp_ref, *rest):
        if rest_rows:
            xr_ref, tr_ref, rest = rest[0], rest[1], rest[2:]
        else:
            xr_ref = tr_ref = None
        s_ref, w_ref, rest = rest[0], rest[1], rest[2:]
        cins = rest[:nc]
        h0_ref, tgt_ref, small_ref, wg_ref = rest[nc:nc + 4]
        couts = rest[nc + 4:2 * nc + 4]
        s_stage, w_stage, meta, msem = rest[2 * nc + 4:2 * nc + 8]
        g_s = _Gather([s_stage], [small_ref], rest[2 * nc + 8:2 * nc + 11])
        g_w = _Gather([w_stage], [wg_ref], rest[2 * nc + 11:], place=_pair_place)
        s = pl.program_id(0)
        i = (s + 1) % nt

        @pl.when(s == 0)
        def _():
            s_stage[...] = s_ref[...]
            w_stage[...] = w_ref[...].astype(_BF)
            g_s.start()
            g_w.start()
            meta[...] = jnp.zeros_like(meta)
            for a in range(nc):
                couts[a][...] = cins[a][...].astype(_BF)

        @pl.when(s == nt - 1)
        def _():
            for j in range(3):
                g_s.forward(j)
            g_s.finish()
            cps = [pltpu.make_async_copy(small_ref.at[k, pl.ds(0, N_META), :], meta.at[:, pl.ds(128 * k, 128)],
                                         msem.at[k]) for k in range(N_DEV)]
            for cp in cps:
                cp.start()
            for cp in cps:
                cp.wait()
            for j in range(3):
                g_w.forward(j)
            g_w.finish()

        has_x = i < nx
        h0_ref[pl.ds(0, N_META), :] = jnp.where(i == 0, meta[...], xp_ref[...])
        h0_ref[pl.ds(N_META, body_rows), :] = jnp.where(has_x, xm_ref[pl.ds(0, body_rows), :], last_tile_body(xr_ref))
        tgt_ref[pl.ds(0, N_META), :] = jnp.where(i == 0, 0.0, tp_ref[...])
        tgt_ref[pl.ds(N_META, body_rows), :] = jnp.where(has_x, tm_ref[pl.ds(0, body_rows), :], last_tile_body(tr_ref))

    def tile_of(s):
        return (s + 1) % nt

    hbm = pl.BlockSpec(memory_space=pl.ANY)
    main = pl.BlockSpec((TM, D), lambda s: (jnp.minimum(tile_of(s), nx - 1), 0))
    prev = pl.BlockSpec((N_META, D), lambda s: (jnp.maximum(tile_of(s) * (TM // N_META) - 1, 0), 0))
    tile = pl.BlockSpec((TM, D), lambda s: (tile_of(s), 0))
    rests = [x_rest, t_rest] if rest_rows else []
    return pl.pallas_call(
        body, name="prologue", grid=(nt,),
        in_specs=[main, prev, main, prev] + [_const(r.shape) for r in rests]
        + [_const(small_l.shape), _const(w_in_l.shape)] + [_const(l.shape) for l in cast_f32],
        out_specs=[tile, tile, hbm, hbm] + [_full(l.shape) for l in cast_f32],
        out_shape=[_S((nt * TM, D), _F32), _S((nt * TM, D), _F32), _S((N_DEV,) + small_l.shape, _F32),
                   _S((4, D, WIN_P), _BF)] + [_S(l.shape, _BF) for l in cast_f32],
        scratch_shapes=[pltpu.VMEM(small_l.shape, _F32), pltpu.VMEM(w_in_l.shape, _BF), pltpu.VMEM((N_META, D), _F32),
                        pltpu.SemaphoreType.DMA((N_DEV,))] + _sem_shapes(1) + _sem_shapes(1),
        compiler_params=_cp(("arbitrary",)),
    )(x, x, tgt, tgt, *rests, small_l, w_in_l, *cast_f32)


def _adamw_math(w, g, m, v):
    m2 = ADAM_B1 * m + (1.0 - ADAM_B1) * g
    v2 = ADAM_B2 * v + (1.0 - ADAM_B2) * (g * g)
    m_hat = m2 / (1.0 - ADAM_B1 ** ADAM_STEP)
    v_hat = v2 / (1.0 - ADAM_B2 ** ADAM_STEP)
    delta = -ADAM_LR * (m_hat / (jnp.sqrt(v_hat) + ADAM_EPS) + ADAM_WD * w)
    return delta, m2, v2


def _adamw_big(name, recv, w, m, v, rows):
    r_all, c_all = w.shape

    def body(r_ref, w_ref, m_ref, v_ref, g_out, d_out, m_out, v_out):
        g = r_ref[0].astype(_F32)
        for k in range(1, N_DEV):
            g = g + r_ref[k].astype(_F32)
        delta, m2, v2 = _adamw_math(w_ref[...], g, m_ref[...], v_ref[...])
        g_out[...] = g
        d_out[...] = delta
        m_out[...] = m2
        v_out[...] = v2

    tile = pl.BlockSpec((rows, c_all), lambda i: (i, 0))
    return pl.pallas_call(
        body, name=name, grid=(r_all // rows,),
        in_specs=[pl.BlockSpec((N_DEV, rows, c_all), lambda i: (0, i, 0)), tile, tile, tile],
        out_specs=[tile] * 4,
        out_shape=[_S(w.shape, _F32)] * 4,
        compiler_params=_cp(("arbitrary",)),
    )(recv, w, m, v)


def _adamw_parts(name, recvs, w, m, v):
    def body(*refs):
        r_refs = refs[:len(recvs)]
        w_ref, m_ref, v_ref, g_out, d_out, m_out, v_out = refs[len(recvs):]
        row0 = 0
        for r_ref in r_refs:
            rows = pl.ds(row0, r_ref.shape[1])
            g = r_ref[0].astype(_F32)
            for k in range(1, N_DEV):
                g = g + r_ref[k].astype(_F32)
            delta, m2, v2 = _adamw_math(w_ref[rows, :], g, m_ref[rows, :], v_ref[rows, :])
            g_out[rows, :] = g
            d_out[rows, :] = delta
            m_out[rows, :] = m2
            v_out[rows, :] = v2
            row0 += r_ref.shape[1]

    return pl.pallas_call(
        body, name=name, out_shape=[_S(w.shape, _F32)] * 4,
        compiler_params=pltpu.CompilerParams(vmem_limit_bytes=VMEM_LIMIT),
    )(*recvs, w, m, v)


def _adamw_small(gathered, slices, wmv):
    ng, npar = len(gathered), len(slices)

    def body(*refs):
        g_refs = refs[:ng]
        wmv_refs = refs[ng:ng + 3 * npar]
        outs = refs[ng + 3 * npar:]
        for i, (ai, r0, nr, c0, ncol) in enumerate(slices):
            g = g_refs[ai][0, pl.ds(r0, nr), pl.ds(c0, ncol)].astype(_F32)
            for k in range(1, N_DEV):
                g = g + g_refs[ai][k, pl.ds(r0, nr), pl.ds(c0, ncol)].astype(_F32)
            w_ref, m_ref, v_ref = wmv_refs[3 * i:3 * i + 3]
            delta, m2, v2 = _adamw_math(w_ref[...], g, m_ref[...], v_ref[...])
            outs[4 * i][...] = g
            outs[4 * i + 1][...] = delta
            outs[4 * i + 2][...] = m2
            outs[4 * i + 3][...] = v2
        total = g_refs[0][0, pl.ds(R_LOSS, 1), pl.ds(0, 128)]
        for k in range(1, N_DEV):
            total = total + g_refs[0][k, pl.ds(R_LOSS, 1), pl.ds(0, 128)]
        outs[4 * npar][...] = total

    flat = [t for trip in wmv for t in trip]
    out_shape = []
    for w, _, _ in wmv:
        out_shape += [_S(w.shape, _F32)] * 4
    out_shape.append(_S((1, 128), _F32))
    return pl.pallas_call(
        body, name="adamw_small", out_shape=out_shape,
        compiler_params=pltpu.CompilerParams(vmem_limit_bytes=VMEM_LIMIT),
    )(*gathered, *flat)


def _block_diag(w):
    eye = jnp.eye(8, dtype=w.dtype)
    return (w[:, :, None, :] * eye[:, None, :, None]).reshape(D_RG, D_RG)


def _diag_blocks(g):
    return jnp.concatenate([g[64 * h:64 * (h + 1), 64 * h:64 * (h + 1)] for h in range(8)], axis=0)


def _local_step(h0, tgt_p, n_valid, g_mix, w_in, vec, wr, wi, hb, g_hg, w_out_l, g_ffn, w_gu_l, w_down_l, g_fin):
    t_pad = h0.shape[0]
    me = 4 * lax.axis_index("x") + 2 * lax.axis_index("y") + lax.axis_index("c")
    p, u, y, hs, o, sc, w_out, w_gu, w_down = _mixer_fwd(h0, g_mix, w_in, wr, wi, vec, hb, g_hg,
                                                         [w_out_l, w_gu_l, w_down_l])
    w_out = w_out.reshape(D, D)
    w_down = w_down.reshape(4, FFB, D)
    h1, v, gu, act, dh2, dh2b, loss, gfin = _ffn_loss(h0, y, w_out, g_ffn, w_gu, w_down, g_fin, tgt_p, n_valid)

    dgu, dh1, dh1b, dy, gffn = _ffn_bwd(dh2, dh2b, gu, h1, g_ffn, w_gu, w_down, w_out)
    g_wdown = _wgrad("wgrad_down", act, dh2b, pl.BlockSpec((1, t_pad, FFB), lambda j: (j, 0, 0)),
                     pl.BlockSpec((t_pad, D), lambda j: (0, 0)), 4, (FFB, D))
    g_wdown = g_wdown.reshape(N_DEV, D_FF // N_DEV, D)
    g_wgu, r_wdown_a = _wgrad("wgrad_gate_up", dgu, v, pl.BlockSpec((1, t_pad, FFB), lambda j: (j, 0, 0)),
                              pl.BlockSpec((t_pad, D), lambda j: (0, 0)), N_DEV, (FFB, D),
                              scatter=[g_wdown], windows=[(0, WDOWN_A)])
    g_wout = _wgrad("wgrad_out", y, dh1b, pl.BlockSpec((t_pad, D // N_DEV), lambda j: (0, j)),
                    pl.BlockSpec((t_pad, D), lambda j: (0, 0)), N_DEV, (D // N_DEV, D))
    dp, gvec, gw, r_wgu, r_wout, r_wdown_b = _mixer_bwd(
        p, hs, o, sc, dy, wr, wi, vec, hb, g_hg, [g_wgu, g_wout, g_wdown],
        [None, None, (WDOWN_A, D_FF // N_DEV - WDOWN_A)])
    r_wdown = (r_wdown_a, r_wdown_b)
    pack_c = jnp.concatenate([_diag_blocks(gw[0]), _diag_blocks(gw[1])], axis=1).astype(_BF)
    order = (me ^ jnp.array(_SEND_ORDER, jnp.int32)).astype(jnp.int32)
    dh0, r_win, all_b, all_c, all_a = _inproj_bwd_send(dp, w_in, h0, dh1, g_mix, u, order, gffn, gfin, loss,
                                                       [gvec, pack_c])
    return dh0, (r_win, r_wgu, r_wout, r_wdown), (all_a, all_b, all_c)


def kernel(x, meta_tokens, mix_norm_g, w_in, conv_w, conv_b, w_rgate, b_rgate, w_igate, b_igate, lru_lambda, rg_norm_g, hg_lower_bound, hg_norm_g, w_out, ffn_norm_g, w_gate_up, w_down, final_norm_g, loss_target, m_meta_tokens, m_mix_norm_g, m_w_in, m_conv_w, m_conv_b, m_w_rgate, m_b_rgate, m_w_igate, m_b_igate, m_lru_lambda, m_rg_norm_g, m_hg_lower_bound, m_hg_norm_g, m_w_out, m_ffn_norm_g, m_w_gate_up, m_w_down, m_final_norm_g, v_meta_tokens, v_mix_norm_g, v_w_in, v_conv_w, v_conv_b, v_w_rgate, v_b_rgate, v_w_igate, v_b_igate, v_lru_lambda, v_rg_norm_g, v_hg_lower_bound, v_hg_norm_g, v_w_out, v_ffn_norm_g, v_w_gate_up, v_w_down, v_final_norm_g):
    seq = x.shape[1]
    me = 4 * lax.axis_index("x") + 2 * lax.axis_index("y") + lax.axis_index("c")

    n_valid = N_META + seq
    small_l = jnp.concatenate([meta_tokens, jnp.pad(conv_w[0], ((0, 4), (0, 64)))], axis=0)
    h0, tgt_p, small_g, w_in_g, w_gu_l, w_out_l, w_down_l = _prologue(
        x[0], loss_target[0], small_l, w_in[0], [w_gate_up[0].T, w_out[0], w_down[0]])
    conv_w_full = jnp.transpose(small_g[:, N_META:N_META + 4, :64], (1, 0, 2)).reshape(4, D_RG)
    vec = jnp.concatenate([conv_b, b_rgate, b_igate, lru_lambda, rg_norm_g, jnp.zeros((3, D_RG), _F32),
                           conv_w_full, jnp.zeros((4, D_RG), _F32)], axis=0)
    wr = _block_diag(w_rgate[0]).astype(_BF)
    wi = _block_diag(w_igate[0]).astype(_BF)

    dh0, (r_win, r_wgu, r_wout, r_wdown), (all_a, all_b, all_c) = _local_step(
        h0, tgt_p, n_valid, mix_norm_g, w_in_g, vec, wr, wi, hg_lower_bound, hg_norm_g,
        w_out_l, ffn_norm_g, w_gu_l, w_down_l, final_norm_g.reshape(1, D))
    grad_x = dh0[N_META:N_META + seq][None]

    outs = {}
    outs["w_in"] = _adamw_big("adamw_w_in", r_win, w_in[0], m_w_in[0], v_w_in[0], 256)
    outs["w_gate_up"] = [r.T for r in _adamw_big("adamw_w_gate_up", r_wgu, w_gate_up[0].T, m_w_gate_up[0].T,
                                                 v_w_gate_up[0].T, 176)]
    outs["w_out"] = _adamw_big("adamw_w_out", r_wout, w_out[0], m_w_out[0], v_w_out[0], 128)
    outs["w_down"] = _adamw_parts("adamw_w_down", r_wdown, w_down[0], m_w_down[0], v_w_down[0])

    meta_part = lax.dynamic_slice_in_dim(all_a[:, R_META:R_META + N_META, :], me * 128, 128, axis=2)
    convw_part = lax.dynamic_slice_in_dim(all_b[:, R_CONVW:R_CONVW + 4, :], me * 64, 64, axis=2)
    gathered = [all_a, all_b, all_c, meta_part, convw_part]
    small_params = [
        ("meta_tokens", (3, 0, N_META, 0, 128), (meta_tokens, m_meta_tokens, v_meta_tokens), (N_META, 128)),
        ("mix_norm_g", (0, R_GMIX, 1, 0, D), (mix_norm_g, m_mix_norm_g, v_mix_norm_g), (1, D)),
        ("conv_w", (4, 0, 4, 0, 64), (conv_w, m_conv_w, v_conv_w), (4, 64)),
        ("conv_b", (1, R_CONVB, 1, 0, D_RG), (conv_b, m_conv_b, v_conv_b), (1, D_RG)),
        ("w_rgate", (2, 0, 512, 0, 64), (w_rgate, m_w_rgate, v_w_rgate), (512, 64)),
        ("b_rgate", (1, R_BR, 1, 0, D_RG), (b_rgate, m_b_rgate, v_b_rgate), (1, D_RG)),
        ("w_igate", (2, 0, 512, 64, 64), (w_igate, m_w_igate, v_w_igate), (512, 64)),
        ("b_igate", (1, R_BI, 1, 0, D_RG), (b_igate, m_b_igate, v_b_igate), (1, D_RG)),
        ("lru_lambda", (1, R_LAM, 1, 0, D_RG), (lru_lambda, m_lru_lambda, v_lru_lambda), (1, D_RG)),
        ("rg_norm_g", (1, R_GRG, 1, 0, D_RG), (rg_norm_g, m_rg_norm_g, v_rg_norm_g), (1, D_RG)),
        ("hg_lower_bound", (1, R_HB0, 2, 0, D_HG), (hg_lower_bound, m_hg_lower_bound, v_hg_lower_bound), (2, D_HG)),
        ("hg_norm_g", (1, R_GHG, 1, 0, HD), (hg_norm_g, m_hg_norm_g, v_hg_norm_g), (1, HD)),
        ("ffn_norm_g", (0, R_GFFN, 1, 0, D), (ffn_norm_g, m_ffn_norm_g, v_ffn_norm_g), (1, D)),
        ("final_norm_g", (0, R_GFIN, 1, 0, D), (final_norm_g, m_final_norm_g, v_final_norm_g), (1, D)),
    ]
    res = _adamw_small(gathered, [s[1] for s in small_params],
                       [tuple(t.reshape(s[3]) for t in s[2]) for s in small_params])
    for i, s in enumerate(small_params):
        outs[s[0]] = [r.reshape(s[2][0].shape) for r in res[4 * i:4 * i + 4]]
    for n, ref in (("w_in", w_in), ("w_gate_up", w_gate_up), ("w_out", w_out), ("w_down", w_down)):
        outs[n] = [r.reshape(ref.shape) for r in outs[n]]

    loss_all = res[4 * len(small_params)][0, 0]
    order = ["meta_tokens", "mix_norm_g", "w_in", "conv_w", "conv_b", "w_rgate", "b_rgate", "w_igate", "b_igate",
             "lru_lambda", "rg_norm_g", "hg_lower_bound", "hg_norm_g", "w_out", "ffn_norm_g", "w_gate_up", "w_down",
             "final_norm_g"]
    return (loss_all, grad_x, *[outs[n][0] for n in order], *[outs[n][1] for n in order],
            *[outs[n][2] for n in order], *[outs[n][3] for n in order])
```

```python
import functools

import jax
import jax.numpy as jnp
from jax import lax
from jax.experimental import pallas as pl
from jax.experimental.pallas import tpu as pltpu

_BF = jnp.bfloat16
_F32 = jnp.float32
_S = jax.ShapeDtypeStruct
_MESH = pl.DeviceIdType.MESH

N_DEV = 8
N_META = 16
D = 1024
D_RG = 512
D_HG = 512
HD = 128
NH = D_HG // HD
D_IN = 3072
D_FF = 2816
FFB = D_FF // 4
WIN_B = D_IN // N_DEV
WIN_P = 2 * WIN_B
WDOWN_A = 256
EPS = 1e-6
LRU_C = 8.0
TM = 320
HC = 64
VMEM_LIMIT = 62 * 1024 * 1024

ADAM_LR = 0.001
ADAM_B1 = 0.9
ADAM_B2 = 0.999
ADAM_EPS = 1e-08
ADAM_WD = 0.01
ADAM_STEP = 10

_SEND_ORDER = (6, 4, 2, 7, 5, 3, 1, 0)

R_CONVB, R_BR, R_BI, R_LAM, R_GRG, R_HB0, R_HB1, R_GHG, R_CONVW = 0, 1, 2, 3, 4, 5, 6, 7, 8
R_GMIX, R_GFFN, R_GFIN, R_LOSS, R_META = 0, 1, 2, 3, 8


def _cp(sem=None, **kw):
    return pltpu.CompilerParams(dimension_semantics=sem, vmem_limit_bytes=VMEM_LIMIT, **kw)


def _dot(a, b):
    return jnp.dot(a, b, preferred_element_type=_F32)


def _dot_nt(a, b):
    return lax.dot_general(a, b, (((1,), (1,)), ((), ())), preferred_element_type=_F32)


def _dot_tn(a, b):
    return lax.dot_general(a, b, (((0,), (0,)), ((), ())), preferred_element_type=_F32)


def _sigmoid(x):
    return 0.5 * jnp.tanh(0.5 * x) + 0.5


def _dsilu(x, s):
    return s * (1.0 + x * (1.0 - s))


_GELU_C = 0.7978845608028654


def _gelu_parts(x):
    t = jnp.tanh(_GELU_C * (x + 0.044715 * (x * x * x)))
    g = 0.5 * x * (1.0 + t)
    dg = 0.5 * (1.0 + t) + 0.5 * x * (1.0 - t * t) * (_GELU_C * (1.0 + 3.0 * 0.044715 * (x * x)))
    return g, dg


def _softplus(z):
    e = jnp.exp(-jnp.abs(z))
    w = 1.0 + e
    l1p = jnp.where(w == 1.0, e, jnp.log(w) * e / jnp.where(w == 1.0, 1.0, w - 1.0))
    return jnp.maximum(z, 0.0) + l1p


def _rms_fwd(x):
    r = lax.rsqrt(jnp.mean(x * x, axis=-1, keepdims=True) + EPS)
    return x * r, r


def _rms_bwd(dyg, n, r):
    return r * (dyg - n * jnp.mean(dyg * n, axis=-1, keepdims=True))


def _full(shape):
    nd = len(shape)
    return pl.BlockSpec(shape, lambda i: (0,) * nd)


def _const(shape):
    nd = len(shape)
    return pl.BlockSpec(shape, lambda i: (0,) * nd, pipeline_mode=pl.Buffered(1))


def _carry_gather(gather, i, nt, early=0):
    @pl.when(i == 0)
    def _():
        gather.start()

    def tail():
        for j in range(3):
            if early:
                @pl.when(i == min(nt // 3 + j, nt - 1))
                def _(j=j):
                    gather.forward(j, range(early))

            @pl.when(i == max(nt - 4 + j, 0))
            def _(j=j):
                gather.forward(j, range(early, gather.na))

        @pl.when(i == nt - 1)
        def _():
            gather.finish()

    return tail


def _pair_place(ref, block):
    return ref.at[block // 2, :, pl.ds(pl.multiple_of((block % 2) * WIN_B, WIN_B), WIN_B)]


def _rg_gates(xc, wr_ref, wi_ref, vec_ref):
    xcb = xc.astype(_BF)
    r = _sigmoid(_dot(xcb, wr_ref[...]) + vec_ref[R_BR:R_BR + 1, :])
    ig = _sigmoid(_dot(xcb, wi_ref[...]) + vec_ref[R_BI:R_BI + 1, :])
    nsp8 = -LRU_C * _softplus(-vec_ref[R_LAM:R_LAM + 1, :])
    la = nsp8 * r
    a = jnp.exp(la)
    th = jnp.tanh(la)
    s = jnp.sqrt(-2.0 * th / (1.0 - th))
    return r, ig, a, s, nsp8


def _conv(xbuf, vec_ref):
    acc = vec_ref[R_CONVW:R_CONVW + 1, :] * xbuf[pl.ds(5, TM), :]
    for j in range(1, 4):
        acc = acc + vec_ref[R_CONVW + j:R_CONVW + j + 1, :] * xbuf[pl.ds(5 + j, TM), :]
    return vec_ref[R_CONVB:R_CONVB + 1, :] + acc


def _dot3(m01, x):
    hi = x.astype(_BF)
    r1 = x - hi.astype(_F32)
    mid = r1.astype(_BF)
    lo = (r1 - mid.astype(_F32)).astype(_BF)
    return (_dot(m01, lo) + _dot(m01, mid)) + _dot(m01, hi)


def _chunk_dot3(m01, x):
    return jnp.concatenate([_dot3(m01, x[HC * c:HC * (c + 1), :]) for c in range(x.shape[0] // HC)], axis=0)


def _chunk_masks():
    row = lax.broadcasted_iota(jnp.int32, (HC, HC), 0)
    col = lax.broadcasted_iota(jnp.int32, (HC, HC), 1)
    return (row >= col).astype(_BF), (col >= row).astype(_BF), jnp.ones((HC, HC), _BF)


def _per_chunk_rows(x, r):
    return jnp.concatenate([jnp.broadcast_to(x[HC * c + r:HC * c + r + 1, :], (HC, x.shape[1]))
                            for c in range(TM // HC)], axis=0)


def _hg_prep(p_ref, lb, tri):
    hq = p_ref[:, pl.ds(2 * D_RG, D_HG)]
    hf = p_ref[:, pl.ds(2 * D_RG + D_HG, D_HG)]
    sq = _sigmoid(hq)
    q = hq * sq
    sg = _sigmoid(hf)
    f = lb + (1.0 - lb) * sg
    k = 1.0 - f
    b = _chunk_dot3(tri, jnp.log(f))
    bm = _per_chunk_rows(b, HC // 2 - 1)
    bl = _per_chunk_rows(b, HC - 1)
    e_q = jnp.exp(b - bm)
    e_k = jnp.exp(bm - b)
    e_b = jnp.exp(b)
    e_l = jnp.exp(bl - b)
    return dict(hq=hq, sq=sq, q=q, sg=sg, f=f, k=k, e_q=e_q, e_k=e_k, e_b=e_b, e_l=e_l,
                qd=q * e_q, kd=k * e_k, qe=q * e_b, ke=k * e_l, e_end=jnp.exp(bl))


def _mixer_fwd(h0, g_mix, w_in, wr, wi, vec, hb, g_hg, shards):
    t_pad = h0.shape[0]
    nt = t_pad // TM
    nc_t = TM // HC
    nsh = len(shards)

    def body(h_ref, gmix_ref, win_ref, wr_ref, wi_ref, vec_ref, hb_ref, ghg_ref, *rest):
        sh_refs, rest = rest[:nsh], rest[nsh:]
        pout_ref, uout_ref, y_ref, hs_ref, o_ref, sc_ref = rest[:6]
        gath_refs, rest = rest[6:6 + nsh], rest[6 + nsh:]
        xbuf, a_s, b_s, hcar, st, qd_s, kd_s, qe_s, ke_s, v_s, u_s, p_s, p_ref = rest[:13]
        i = pl.program_id(0)
        tail = _carry_gather(_Gather(sh_refs, gath_refs, rest[13:]), i, nt + 1, early=1)

        @pl.when(i == 0)
        def _():
            p_s[...] = jnp.zeros_like(p_s)

        p_ref[...] = p_s[...]

        @pl.when(i <= 1)
        def _():
            xbuf[pl.ds(0, 8), :] = jnp.zeros((8, D_RG), _F32)
            hcar[...] = jnp.zeros_like(hcar)
            st[...] = jnp.zeros_like(st)

        n_h, _ = _rms_fwd(h_ref[...])
        u = (n_h * gmix_ref[...]).astype(_BF)
        uout_ref[...] = u
        pieces = [(j, k) for j in range(4) for k in range(WIN_P // 256)]

        def project(count):
            for _ in range(count):
                j, k = pieces.pop(0)
                blk = _dot(u, win_ref[j, :, pl.ds(256 * k, 256)])
                p_s[:, pl.ds(WIN_P * j + 256 * k, 256)] = blk
                pout_ref[:, pl.ds(WIN_P * j + 256 * k, 256)] = blk

        x = p_ref[:, pl.ds(0, D_RG)]
        xbuf[pl.ds(8, TM), :] = x
        xc = _conv(xbuf, vec_ref)
        xbuf[pl.ds(0, 8), :] = x[TM - 8:, :]
        r, ig, a, s, _ = _rg_gates(xc, wr_ref, wi_ref, vec_ref)
        a_s[...] = a
        b_s[...] = s * (ig * xc)

        def step(t, h):
            h = a_s[pl.ds(t, 1), :] * h + b_s[pl.ds(t, 1), :]
            hs_ref[pl.ds(t, 1), :] = h
            return h

        hcar[pl.ds(0, 1), :] = lax.fori_loop(0, TM, step, hcar[pl.ds(0, 1), :], unroll=8)
        gel, _ = _gelu_parts(p_ref[:, pl.ds(D_RG, D_RG)])
        n, _ = _rms_fwd(gel * hs_ref[...])
        y_ref[:, pl.ds(0, D_RG)] = (n * vec_ref[R_GRG:R_GRG + 1, :]).astype(_BF)

        lb = _sigmoid(hb_ref[0:1, :] - hb_ref[1:2, :])
        tri, _, _ = _chunk_masks()
        q = _hg_prep(p_ref, lb, tri)
        for name, ref in (("qd", qd_s), ("kd", kd_s), ("qe", qe_s), ("ke", ke_s)):
            ref[...] = q[name].astype(_BF)
        v_s[...] = p_ref[:, pl.ds(2 * D_RG + 2 * D_HG, D_HG)].astype(_BF)
        e_end = q["e_end"]
        causal = (lax.broadcasted_iota(jnp.int32, (HC, HC), 0) >= lax.broadcasted_iota(jnp.int32, (HC, HC), 1))
        for c in range(nc_t):
            for h in range(NH):
                rs, cs = pl.ds(HC * c, HC), pl.ds(HD * h, HD)
                amat = jnp.where(causal, _dot_nt(qd_s[rs, cs], kd_s[rs, cs]), 0.0)
                o_ref[rs, cs] = _dot(amat.astype(_BF), v_s[rs, cs])
                u_s[NH * c + h] = _dot_tn(v_s[rs, cs], ke_s[rs, cs])
                if pieces:
                    project(1)
        assert not pieces
        for h in range(NH):
            cs = pl.ds(HD * h, HD)
            s_run = st[h]
            for c in range(nc_t):
                rs = pl.ds(HC * c, HC)
                sc_ref[c, h] = s_run
                o_ref[rs, cs] += _dot_nt(qe_s[rs, cs], s_run.astype(_BF))
                s_run = e_end[HC * c:HC * c + 1, HD * h:HD * (h + 1)] * s_run + u_s[NH * c + h]
            st[h] = s_run
        for h in range(NH):
            cs = pl.ds(HD * h, HD)
            n_o, _ = _rms_fwd(o_ref[:, cs])
            hg = p_ref[:, pl.ds(2 * D_RG + 3 * D_HG + HD * h, HD)]
            y_ref[:, pl.ds(D_RG + HD * h, HD)] = ((n_o * ghg_ref[...]) * (hg * _sigmoid(hg))).astype(_BF)

        tail()

    hbm = pl.BlockSpec(memory_space=pl.ANY)

    def proj(i):
        return jnp.minimum(i, nt - 1)

    def mixed(i):
        return jnp.maximum(i - 1, 0)

    return pl.pallas_call(
        body, name="mixer_fwd", grid=(nt + 1,),
        in_specs=[pl.BlockSpec((TM, D), lambda i: (proj(i), 0)), _full((1, D)), _const((4, D, WIN_P)),
                  _full((D_RG, D_RG)), _full((D_RG, D_RG)),
                  _full((16, D_RG)), _full((2, D_HG)), _full((1, HD))] + [hbm] * nsh,
        out_specs=[pl.BlockSpec((TM, D_IN), lambda i: (proj(i), 0)), pl.BlockSpec((TM, D), lambda i: (proj(i), 0)),
                   pl.BlockSpec((TM, D), lambda i: (mixed(i), 0)), pl.BlockSpec((TM, D_RG), lambda i: (mixed(i), 0)),
                   pl.BlockSpec((TM, D_HG), lambda i: (mixed(i), 0)),
                   pl.BlockSpec((nc_t, NH, HD, HD), lambda i: (mixed(i), 0, 0, 0))] + [hbm] * nsh,
        out_shape=[_S((t_pad, D_IN), _F32), _S((t_pad, D), _BF),
                   _S((t_pad, D), _BF), _S((t_pad, D_RG), _F32), _S((t_pad, D_HG), _F32),
                   _S((t_pad // HC, NH, HD, HD), _F32)] + [_S((N_DEV,) + s.shape, s.dtype) for s in shards],
        scratch_shapes=[pltpu.VMEM((TM + 8, D_RG), _F32), pltpu.VMEM((TM, D_RG), _F32),
                        pltpu.VMEM((TM, D_RG), _F32), pltpu.VMEM((8, D_RG), _F32),
                        pltpu.VMEM((NH, HD, HD), _F32)] + [pltpu.VMEM((TM, D_HG), _BF) for _ in range(5)]
        + [pltpu.VMEM((nc_t * NH, HD, HD), _F32), pltpu.VMEM((TM, D_IN), _F32), pltpu.VMEM((TM, D_IN), _F32)]
        + _sem_shapes(nsh),
        compiler_params=_cp(("arbitrary",)),
    )(h0, g_mix, w_in, wr, wi, vec, hb, g_hg, *shards)


def _ffn_loss(h0, y, w_out, g_ffn, w_gu, w_down, g_fin, tgt, n_valid):
    t_pad = h0.shape[0]

    def body(h_ref, y_ref, wo_ref, gffn_ref, wgu_ref, wd_ref, g_ref, t_ref,
             h1_ref, v_ref, gu_ref, act_ref, dh2_ref, dh2b_ref, loss_ref, gfin_ref):
        i = pl.program_id(0)

        @pl.when(i == 0)
        def _():
            loss_ref[...] = jnp.zeros_like(loss_ref)
            gfin_ref[...] = jnp.zeros_like(gfin_ref)

        h1 = h_ref[...] + _dot(y_ref[...], wo_ref[...])
        h1_ref[...] = h1
        n1, _ = _rms_fwd(h1)
        vb = (n1 * gffn_ref[...]).astype(_BF)
        v_ref[...] = vb
        h2 = h1
        for b in range(4):
            gate = _dot_nt(vb, wgu_ref[b])
            up = _dot_nt(vb, wgu_ref[4 + b])
            gu_ref[b] = gate
            gu_ref[4 + b] = up
            act = ((gate * _sigmoid(gate)) * up).astype(_BF)
            act_ref[b] = act
            h2 = h2 + _dot(act, wd_ref[b])
        n, r = _rms_fwd(h2)
        out = n * g_ref[...]
        row = i * TM + lax.broadcasted_iota(jnp.int32, (TM, 1), 0)
        valid = (row >= N_META) & (row < n_valid)
        err = jnp.where(valid, out - t_ref[...], 0.0)
        loss_ref[...] += (0.5 / D) * jnp.sum(err * err)
        dout = err * (1.0 / D)
        gfin_ref[...] += jnp.sum(dout * n, axis=0, keepdims=True)
        dh2 = _rms_bwd(dout * g_ref[...], n, r)
        dh2_ref[...] = dh2
        dh2b_ref[...] = dh2.astype(_BF)

    tile = pl.BlockSpec((TM, D), lambda i: (i, 0))
    return pl.pallas_call(
        body, name="ffn_loss", grid=(t_pad // TM,),
        in_specs=[tile, tile, _const((D, D)), _full((1, D)),
                  _const((N_DEV, FFB, D)), _const((4, FFB, D)), _full((1, D)), tile],
        out_specs=[tile, tile,
                   pl.BlockSpec((N_DEV, TM, FFB), lambda i: (0, i, 0)), pl.BlockSpec((4, TM, FFB), lambda i: (0, i, 0)),
                   tile, tile, _full((8, 128)), _full((1, D))],
        out_shape=[_S((t_pad, D), _F32), _S((t_pad, D), _BF),
                   _S((N_DEV, t_pad, FFB), _F32), _S((4, t_pad, FFB), _BF), _S((t_pad, D), _F32),
                   _S((t_pad, D), _BF), _S((8, 128), _F32), _S((1, D), _F32)],
        compiler_params=_cp(("arbitrary",)),
    )(h0, y, w_out, g_ffn, w_gu, w_down, g_fin, tgt)


def _ffn_bwd(dh2, dh2b, gu, h1, g_ffn, w_gu, w_down, w_out):
    t_pad = dh2.shape[0]

    def body(dh2_ref, dh2b_ref, gu_ref, h1_ref, g_ref, wgu_ref, wd_ref, wo_ref,
             dgu_ref, dh1_ref, dh1b_ref, dy_ref, gffn_ref):
        i = pl.program_id(0)

        @pl.when(i == 0)
        def _():
            gffn_ref[...] = jnp.zeros_like(gffn_ref)

        db = dh2b_ref[...]
        dv = jnp.zeros((TM, D), _F32)
        for b in range(4):
            dact = _dot_nt(db, wd_ref[b])
            gate = gu_ref[b]
            up = gu_ref[4 + b]
            sg = _sigmoid(gate)
            dgate = ((dact * up) * _dsilu(gate, sg)).astype(_BF)
            dup = (dact * (gate * sg)).astype(_BF)
            dgu_ref[b] = dgate
            dgu_ref[4 + b] = dup
            dv = dv + _dot(dgate, wgu_ref[b]) + _dot(dup, wgu_ref[4 + b])
        n, r = _rms_fwd(h1_ref[...])
        gffn_ref[...] += jnp.sum(dv * n, axis=0, keepdims=True)
        dh1 = dh2_ref[...] + _rms_bwd(dv * g_ref[...], n, r)
        dh1_ref[...] = dh1
        dh1b = dh1.astype(_BF)
        dh1b_ref[...] = dh1b
        dy_ref[...] = _dot_nt(dh1b, wo_ref[...])

    tile = pl.BlockSpec((TM, D), lambda i: (i, 0))
    return pl.pallas_call(
        body, name="ffn_bwd", grid=(t_pad // TM,),
        in_specs=[tile, tile, pl.BlockSpec((N_DEV, TM, FFB), lambda i: (0, i, 0)), tile, _full((1, D)),
                  _const((N_DEV, FFB, D)), _const((4, FFB, D)), _const((D, D))],
        out_specs=[pl.BlockSpec((N_DEV, TM, FFB), lambda i: (0, i, 0)), tile, tile, tile, _full((1, D))],
        out_shape=[_S((N_DEV, t_pad, FFB), _BF), _S((t_pad, D), _F32), _S((t_pad, D), _BF),
                   _S((t_pad, D), _F32), _S((1, D), _F32)],
        compiler_params=_cp(("arbitrary",)),
    )(dh2, dh2b, gu, h1, g_ffn, w_gu, w_down, w_out)


def _mixer_bwd(p, hs, o, sc, dy, wr, wi, vec, hb, g_hg, scatter, windows):
    t_pad = p.shape[0]
    nt = t_pad // TM
    nc_t = TM // HC
    nsc = len(scatter)

    def rev(i):
        return nt - 1 - i

    def body(p_ref, pprev_ref, hs_ref, hprev_ref, o_ref, sc_ref, dy_ref, wr_ref, wi_ref, vec_ref, hb_ref, ghg_ref,
             *rest):
        send_refs, rest = rest[:nsc], rest[nsc:]
        dp_ref, gvec_ref, gw_ref = rest[:3]
        recv_refs, rest = rest[3:3 + nsc], rest[3 + nsc:]
        xbuf, hbuf, dbuf, a_s, g_s, ccar, dst = rest[:7]
        qd_s, kd_s, qe_s, ke_s, v_s, do_s, dqd_s, dkd_s, dqe_s, dke_s, dv_s, w_s, dend_s = rest[7:20]
        exchange = _Exchange(send_refs, [], recv_refs, rest[20:], windows)
        i = pl.program_id(0)
        first_tile = i == nt - 1

        @pl.when(i == 0)
        def _():
            exchange.start()
            gvec_ref[...] = jnp.zeros_like(gvec_ref)
            gw_ref[...] = jnp.zeros_like(gw_ref)
            dbuf[pl.ds(TM, 8), :] = jnp.zeros((8, D_RG), _F32)
            ccar[...] = jnp.zeros_like(ccar)
            dst[...] = jnp.zeros_like(dst)

        def acc(row, val):
            gvec_ref[row:row + 1, :] += jnp.sum(val, axis=0, keepdims=True)

        keep = jnp.where(first_tile, 0.0, 1.0)
        x = p_ref[:, pl.ds(0, D_RG)]
        xbuf[pl.ds(0, 8), :] = pprev_ref[...] * keep
        xbuf[pl.ds(8, TM), :] = x
        xc = _conv(xbuf, vec_ref)
        r, ig, a, s, nsp8 = _rg_gates(xc, wr_ref, wi_ref, vec_ref)
        h = hs_ref[...]
        hbuf[pl.ds(0, 8), :] = hprev_ref[...] * keep
        hbuf[pl.ds(8, TM), :] = h
        hm1 = hbuf[pl.ds(7, TM), :]
        gr = p_ref[:, pl.ds(D_RG, D_RG)]
        gel, dgel = _gelu_parts(gr)
        n, rr = _rms_fwd(gel * h)
        dyn = dy_ref[:, pl.ds(0, D_RG)]
        acc(R_GRG, dyn * n)
        dpre = _rms_bwd(dyn * vec_ref[R_GRG:R_GRG + 1, :], n, rr)
        dp_ref[:, pl.ds(D_RG, D_RG)] = ((dpre * h) * dgel).astype(_BF)
        a_s[...] = a
        g_s[...] = dpre * gel

        def step(k, c):
            t = TM - 1 - k
            g = g_s[pl.ds(t, 1), :] + c
            g_s[pl.ds(t, 1), :] = g
            return a_s[pl.ds(t, 1), :] * g

        ccar[pl.ds(0, 1), :] = lax.fori_loop(0, TM, step, ccar[pl.ds(0, 1), :], unroll=8)
        gt = g_s[...]
        da = gt * hm1
        ixc = ig * xc
        ds = gt * ixc
        dig = (gt * s) * xc
        dxc = (gt * s) * ig
        dla = da * a - ds * ((a * a) / s)
        lam = vec_ref[R_LAM:R_LAM + 1, :]
        gvec_ref[R_LAM:R_LAM + 1, :] += jnp.sum(dla * r, axis=0, keepdims=True) * (LRU_C * _sigmoid(-lam))
        dzr = (dla * nsp8) * (r * (1.0 - r))
        dzi = dig * (ig * (1.0 - ig))
        acc(R_BR, dzr)
        acc(R_BI, dzi)
        xcb = xc.astype(_BF)
        dzrb = dzr.astype(_BF)
        dzib = dzi.astype(_BF)
        gw_ref[0] += _dot_tn(xcb, dzrb)
        gw_ref[1] += _dot_tn(xcb, dzib)
        dxc = dxc + _dot_nt(dzrb, wr_ref[...]) + _dot_nt(dzib, wi_ref[...])
        acc(R_CONVB, dxc)
        for j in range(4):
            acc(R_CONVW + j, dxc * xbuf[pl.ds(5 + j, TM), :])
        dbuf[pl.ds(0, TM), :] = dxc
        dx = vec_ref[R_CONVW + 3:R_CONVW + 4, :] * dxc
        for j in range(3):
            dx = dx + vec_ref[R_CONVW + j:R_CONVW + j + 1, :] * dbuf[pl.ds(3 - j, TM), :]
        dbuf[pl.ds(TM, 8), :] = dxc[0:8, :]
        dp_ref[:, pl.ds(0, D_RG)] = dx.astype(_BF)

        lb = _sigmoid(hb_ref[0:1, :] - hb_ref[1:2, :])
        tri, tri_rev, ones = _chunk_masks()
        q = _hg_prep(p_ref, lb, tri)
        qdb, kdb = q["qd"].astype(_BF), q["kd"].astype(_BF)
        qd_s[...] = qdb
        kd_s[...] = kdb
        qe_s[...] = q["qe"].astype(_BF)
        ke_s[...] = q["ke"].astype(_BF)
        v_s[...] = p_ref[:, pl.ds(2 * D_RG + 2 * D_HG, D_HG)].astype(_BF)
        e_end = q["e_end"]
        ghg = ghg_ref[...]
        for h in range(NH):
            cs = pl.ds(HD * h, HD)
            hg = p_ref[:, pl.ds(2 * D_RG + 3 * D_HG + HD * h, HD)]
            sh = _sigmoid(hg)
            n_o, r_o = _rms_fwd(o_ref[:, cs])
            dyh = dy_ref[:, pl.ds(D_RG + HD * h, HD)]
            dp_ref[:, pl.ds(2 * D_RG + 3 * D_HG + HD * h, HD)] = ((dyh * (n_o * ghg)) * _dsilu(hg, sh)).astype(_BF)
            dn = dyh * (hg * sh)
            gvec_ref[R_GHG:R_GHG + 1, pl.ds(0, HD)] += jnp.sum(dn * n_o, axis=0, keepdims=True)
            do_s[:, cs] = _rms_bwd(dn * ghg, n_o, r_o).astype(_BF)
        causal = (lax.broadcasted_iota(jnp.int32, (HC, HC), 0) >= lax.broadcasted_iota(jnp.int32, (HC, HC), 1))
        for c in range(nc_t):
            for h in range(NH):
                rs, cs = pl.ds(HC * c, HC), pl.ds(HD * h, HD)
                qd_c, kd_c, do_c = qd_s[rs, cs], kd_s[rs, cs], do_s[rs, cs]
                amat = jnp.where(causal, _dot_nt(qd_c, kd_c), 0.0).astype(_BF)
                da_m = jnp.where(causal, _dot_nt(do_c, v_s[rs, cs]), 0.0).astype(_BF)
                dqd_s[rs, cs] = _dot(da_m, kd_c)
                dkd_s[rs, cs] = _dot_tn(da_m, qd_c)
                dqe_s[rs, cs] = _dot(do_c, sc_ref[c, h].astype(_BF))
                dv_s[rs, cs] = _dot_tn(amat, do_c)
                w_s[NH * c + h] = _dot_tn(do_c, qe_s[rs, cs])
        for h in range(NH):
            cs = pl.ds(HD * h, HD)
            d_run = dst[h]
            for c in reversed(range(nc_t)):
                rs = pl.ds(HC * c, HC)
                d_b = d_run.astype(_BF)
                dke_s[rs, cs] = _dot(v_s[rs, cs], d_b)
                dp_ref[rs, pl.ds(2 * D_RG + 2 * D_HG + HD * h, HD)] = (
                    dv_s[rs, cs] + _dot_nt(ke_s[rs, cs], d_b)).astype(_BF)
                dend_s[pl.ds(c, 1), cs] = jnp.sum(sc_ref[c, h] * d_run, axis=0, keepdims=True)
                d_run = w_s[NH * c + h] + e_end[HC * c:HC * c + 1, HD * h:HD * (h + 1)] * d_run
            dst[h] = d_run
        dqd, dkd, dqe, dke = dqd_s[...], dkd_s[...], dqe_s[...], dke_s[...]
        dq = dqd * q["e_q"] + dqe * q["e_b"]
        dk = dkd * q["e_k"] + dke * q["e_l"]
        dkeke = dke * q["ke"]
        db = dqd * qdb.astype(_F32) - dkd * kdb.astype(_F32) + dqe * q["qe"] - dkeke
        d_end = jnp.concatenate([jnp.broadcast_to(dend_s[pl.ds(c, 1), :], (HC, D_HG)) for c in range(nc_t)], axis=0)
        dlf = _chunk_dot3(tri_rev, db) + _chunk_dot3(ones, dkeke) + d_end * e_end
        df = dlf / q["f"] - dk
        sg = q["sg"]
        gvec_ref[R_HB0:R_HB0 + 1, :] += jnp.sum(df * (1.0 - sg), axis=0, keepdims=True)
        dp_ref[:, pl.ds(2 * D_RG, D_HG)] = (dq * _dsilu(q["hq"], q["sq"])).astype(_BF)
        dp_ref[:, pl.ds(2 * D_RG + D_HG, D_HG)] = ((df * (1.0 - lb)) * (sg * (1.0 - sg))).astype(_BF)

        @pl.when(i == nt - 1)
        def _():
            glb = gvec_ref[R_HB0:R_HB0 + 1, :] * (lb * (1.0 - lb))
            gvec_ref[R_HB0:R_HB0 + 1, :] = glb
            gvec_ref[R_HB1:R_HB1 + 1, :] = -glb
            exchange.finish()

    hbm = pl.BlockSpec(memory_space=pl.ANY)
    return pl.pallas_call(
        body, name="mixer_bwd", grid=(nt,),
        in_specs=[pl.BlockSpec((TM, D_IN), lambda i: (rev(i), 0)),
                  pl.BlockSpec((8, D_RG), lambda i: (jnp.maximum(rev(i) * (TM // 8) - 1, 0), 0)),
                  pl.BlockSpec((TM, D_RG), lambda i: (rev(i), 0)),
                  pl.BlockSpec((8, D_RG), lambda i: (jnp.maximum(rev(i) * (TM // 8) - 1, 0), 0)),
                  pl.BlockSpec((TM, D_HG), lambda i: (rev(i), 0)),
                  pl.BlockSpec((nc_t, NH, HD, HD), lambda i: (rev(i), 0, 0, 0)),
                  pl.BlockSpec((TM, D), lambda i: (rev(i), 0)),
                  _full((D_RG, D_RG)), _full((D_RG, D_RG)), _full((16, D_RG)), _full((2, D_HG)), _full((1, HD))]
        + [hbm] * nsc,
        out_specs=[pl.BlockSpec((TM, D_IN), lambda i: (rev(i), 0)), _full((16, D_RG)), _full((2, D_RG, D_RG))]
        + [hbm] * nsc,
        out_shape=[_S((t_pad, D_IN), _BF), _S((16, D_RG), _F32), _S((2, D_RG, D_RG), _F32)]
        + _recv_shapes(scatter, windows),
        scratch_shapes=[pltpu.VMEM((TM + 8, D_RG), _F32), pltpu.VMEM((TM + 8, D_RG), _F32),
                        pltpu.VMEM((TM + 8, D_RG), _F32), pltpu.VMEM((TM, D_RG), _F32),
                        pltpu.VMEM((TM, D_RG), _F32), pltpu.VMEM((8, D_RG), _F32),
                        pltpu.VMEM((NH, HD, HD), _F32)]
        + [pltpu.VMEM((TM, D_HG), _BF) for _ in range(6)] + [pltpu.VMEM((TM, D_HG), _F32) for _ in range(5)]
        + [pltpu.VMEM((nc_t * NH, HD, HD), _F32), pltpu.VMEM((8, D_HG), _F32)] + _sem_shapes(nsc),
        compiler_params=_cp(("arbitrary",)),
    )(p, p, hs, hs, o, sc, dy, wr, wi, vec, hb, g_hg, *scatter)


def _inproj_bwd_send(dp, w_in, h0, dh1, g_mix, u, order, gffn, gfin, loss, to_all, n_valid):
    t_pad = dp.shape[0]
    rb = TM
    nr = t_pad // rb
    n_steps = N_DEV + nr
    n_last = n_valid - (nr - 1) * rb
    assert nr >= 2 and 0 < n_last <= rb and n_last % 8 == 0
    na = len(to_all)

    def body(order_ref, dpc_ref, dpr_ref, u_ref, w_ref, h_ref, dh1_ref, g_ref, gffn_ref, gfin_ref, loss_ref, *rest):
        all_in = rest[:na]
        gx_ref, recv_ref = rest[na:na + 2]
        all_out = rest[na + 2:2 * na + 2]
        alla_ref = rest[2 * na + 2]
        buf, pack, gx_buf, gx_sem, blk_send, blk_recv, blk_local = rest[2 * na + 3:2 * na + 10]
        exchange = _Exchange([], all_in, all_out, rest[2 * na + 10:2 * na + 13])
        last = _Exchange([], [pack], [alla_ref], rest[2 * na + 13:])
        s = pl.program_id(0)
        x, y, c = _coords()
        me = 4 * x + 2 * y + c

        def gx_copy(r, first=False, final=False):
            if first:
                src, dst = gx_buf.at[0, pl.ds(N_META, rb - N_META)], gx_ref.at[pl.ds(0, rb - N_META)]
            elif final:
                src = gx_buf.at[(nr - 1) % 2, pl.ds(0, n_last)]
                dst = gx_ref.at[pl.ds((nr - 1) * rb - N_META, n_last)]
            else:
                src, dst = gx_buf.at[r % 2], gx_ref.at[pl.ds(pl.multiple_of(r * rb - N_META, 8), rb)]
            return pltpu.make_async_copy(src, dst, gx_sem.at[r % 2])

        def send(step):
            r = _SEND_ORDER[step]
            return pltpu.make_async_remote_copy(
                src_ref=buf.at[step], dst_ref=recv_ref.at[me], send_sem=blk_send.at[step], recv_sem=blk_recv.at[r - 1],
                device_id=(x ^ (r >> 2), y ^ ((r >> 1) & 1), c ^ (r & 1)), device_id_type=_MESH)

        @pl.when(s == 0)
        def _():
            exchange.start()
            pack[...] = jnp.zeros_like(pack)

        @pl.when(s < N_DEV)
        def _():
            buf[s] = _dot_tn(u_ref[...], dpc_ref[...]).astype(_BF)

            for step in range(N_DEV - 1):
                @pl.when(s == step)
                def _(step=step):
                    send(step).start()

        @pl.when(s >= N_DEV)
        def _():
            du = jnp.zeros((rb, D), _F32)
            for j in range(4):
                du = du + _dot_nt(dpr_ref[:, WIN_P * j:WIN_P * (j + 1)], w_ref[j])
            n, r = _rms_fwd(h_ref[...])
            pack[R_GMIX:R_GMIX + 1, :] += jnp.sum(du * n, axis=0, keepdims=True)
            dh0 = dh1_ref[...] + _rms_bwd(du * g_ref[...], n, r)
            t = s - N_DEV

            @pl.when(t == 2)
            def _():
                gx_copy(0, first=True).wait()

            @pl.when(t > 2)
            def _():
                gx_copy(t - 2).wait()

            gx_buf[t % 2] = dh0

            @pl.when(t == 0)
            def _():
                pack[R_META:R_META + N_META, :] = dh0[0:N_META, :]
                gx_copy(0, first=True).start()

            @pl.when((t > 0) & (t < nr - 1))
            def _():
                gx_copy(t).start()

        @pl.when(s == n_steps - 1)
        def _():
            gx_copy(nr - 1, final=True).start()
            pack[R_GFFN:R_GFFN + 1, :] = gffn_ref[...]
            pack[R_GFIN:R_GFIN + 1, :] = gfin_ref[...]
            pack[R_LOSS:R_LOSS + 1, pl.ds(0, 128)] = loss_ref[0:1, :]
            last.start()
            mine = pltpu.make_async_copy(buf.at[N_DEV - 1], recv_ref.at[me], blk_local.at[0])
            mine.start()
            for step in range(N_DEV - 1):
                send(step).wait_send()
            for r in range(1, N_DEV):
                px, py, pc = x ^ (r >> 2), y ^ ((r >> 1) & 1), c ^ (r & 1)
                pltpu.make_async_remote_copy(
                    src_ref=buf.at[0], dst_ref=recv_ref.at[4 * px + 2 * py + pc], send_sem=blk_send.at[0],
                    recv_sem=blk_recv.at[r - 1], device_id=(px, py, pc), device_id_type=_MESH).wait_recv()
            mine.wait()
            gx_copy(nr - 2, first=nr == 2).wait()
            gx_copy(nr - 1, final=True).wait()
            exchange.finish()
            last.finish()

    hbm = pl.BlockSpec(memory_space=pl.ANY)
    rows = pl.BlockSpec((rb, D), lambda s, order: (jnp.maximum(s - N_DEV, 0), 0))
    one = pl.BlockSpec((1, D), lambda s, order: (0, 0))
    res = pl.pallas_call(
        body, name="inproj_bwd_send",
        grid_spec=pltpu.PrefetchScalarGridSpec(
            num_scalar_prefetch=1, grid=(n_steps,),
            in_specs=[pl.BlockSpec((t_pad, WIN_B), lambda s, order: (0, order[jnp.minimum(s, N_DEV - 1)])),
                      pl.BlockSpec((rb, D_IN), lambda s, order: (jnp.maximum(s - N_DEV, 0), 0)),
                      pl.BlockSpec((t_pad, D), lambda s, order: (0, 0), pipeline_mode=pl.Buffered(1)),
                      pl.BlockSpec((4, D, WIN_P), lambda s, order: (0, 0, 0), pipeline_mode=pl.Buffered(1)),
                      rows, rows, one, one, one, pl.BlockSpec((8, 128), lambda s, order: (0, 0))] + [hbm] * na,
            out_specs=[hbm] * (na + 3),
            scratch_shapes=[pltpu.VMEM((N_DEV, D, WIN_B), _BF), pltpu.VMEM((24, D), _F32),
                            pltpu.VMEM((2, rb, D), _F32), pltpu.SemaphoreType.DMA((2,)),
                            pltpu.SemaphoreType.DMA((N_DEV - 1,)), pltpu.SemaphoreType.DMA((N_DEV - 1,)),
                            pltpu.SemaphoreType.DMA((1,))] + _sem_shapes(na) + _sem_shapes(1)),
        out_shape=[_S((n_valid - N_META, D), _F32), _S((N_DEV, D, WIN_B), _BF)]
        + [_S((N_DEV,) + g.shape, g.dtype) for g in to_all] + [_S((N_DEV, 24, D), _F32)],
        compiler_params=_cp(("arbitrary",)),
    )(order, dp, dp, u, w_in, h0, dh1, g_mix, gffn, gfin, loss, *to_all)
    return res


def _recv_shapes(scatter, windows):
    return [_S(s.shape if w is None else (s.shape[0], w[1]) + s.shape[2:], s.dtype) for s, w in zip(scatter, windows)]


def _wgrad(name, a, b, a_spec, b_spec, n_blocks, out_block, scatter=(), windows=None):
    nsc = len(scatter)
    windows = windows if windows is not None else [None] * nsc

    def body(a_ref, b_ref, *rest):
        o_ref = rest[nsc]
        j = pl.program_id(0)
        if nsc:
            exchange = _Exchange(rest[:nsc], [], rest[nsc + 1:2 * nsc + 1], rest[2 * nsc + 1:], windows)

            @pl.when(j == 0)
            def _():
                exchange.start()

        av = a_ref[0] if len(a_ref.shape) == 3 else a_ref[...]
        bv = b_ref[0] if len(b_ref.shape) == 3 else b_ref[...]
        o_ref[0] = _dot_tn(av, bv).astype(_BF)

        if nsc:
            @pl.when(j == n_blocks - 1)
            def _():
                exchange.finish()

    hbm = pl.BlockSpec(memory_space=pl.ANY)
    res = pl.pallas_call(
        body, name=name, grid=(n_blocks,),
        in_specs=[a_spec, b_spec] + [hbm] * nsc,
        out_specs=[pl.BlockSpec((1,) + out_block, lambda j: (j, 0, 0))] + [hbm] * nsc,
        out_shape=[_S((n_blocks,) + out_block, _BF)] + _recv_shapes(scatter, windows),
        scratch_shapes=_sem_shapes(nsc) if nsc else [],
        compiler_params=_cp(("arbitrary",)),
    )(a, b, *scatter)
    return res if nsc else res[0]


def _coords():
    return lax.axis_index("x"), lax.axis_index("y"), lax.axis_index("c")


def _sem_shapes(na):
    return [pltpu.SemaphoreType.DMA((7 * na,)), pltpu.SemaphoreType.DMA((7 * na,)), pltpu.SemaphoreType.DMA((na,))]


class _Gather:
    def __init__(self, srcs, outs, sems, place=None):
        self.srcs, self.outs = srcs, outs
        self.send_sems, self.recv_sems, self.local_sems = sems
        self.place = place if place is not None else (lambda ref, block: ref.at[block])
        self.na = len(srcs)
        x, y, c = _coords()
        self.pos = (x, y, c)
        self.me = 4 * x + 2 * y + c
        self.sibling = (x, y, 1 - c)
        self.chips = [(1 - x, y), (x, 1 - y), (1 - x, 1 - y)]

    @staticmethod
    def _slot(px, py, pc):
        return 4 * px + 2 * py + pc

    def _copy(self, a, k, block, to, own=False):
        dst = self.place(self.outs[a], block)
        return pltpu.make_async_remote_copy(
            src_ref=self.srcs[a] if own else dst, dst_ref=dst,
            send_sem=self.send_sems.at[7 * a + k], recv_sem=self.recv_sems.at[7 * a + k],
            device_id=to, device_id_type=_MESH)

    def _mine(self, a):
        return pltpu.make_async_copy(self.srcs[a], self.place(self.outs[a], self.me), self.local_sems.at[a])

    def _first(self):
        c = self.pos[2]
        cps = []
        for a in range(self.na):
            cps.append(self._copy(a, 0, self.me, self.sibling, own=True))
            cps += [self._copy(a, 1 + j, self.me, (*chip, c), own=True) for j, chip in enumerate(self.chips)]
        return cps

    def _passed(self):
        c = self.pos[2]
        return [self._copy(a, 4 + j, self._slot(*chip, c), self.sibling)
                for j, chip in enumerate(self.chips) for a in range(self.na)]

    def start(self):
        for a in range(self.na):
            self._mine(a).start()
        for cp in self._first():
            cp.start()

    def forward(self, j, arrays=None):
        c = self.pos[2]
        chip = self.chips[j]
        for a in (range(self.na) if arrays is None else arrays):
            self._copy(a, 1 + j, self._slot(*chip, c), self.pos).wait_recv()
            self._copy(a, 4 + j, self._slot(*chip, c), self.sibling).start()

    def wait_sibling(self):
        x, y, c = self.pos
        for a in range(self.na):
            self._copy(a, 0, self._slot(x, y, 1 - c), self.pos).wait_recv()

    def wait_passed(self, j):
        c = self.pos[2]
        for a in range(self.na):
            self._copy(a, 4 + j, self._slot(*self.chips[j], 1 - c), self.pos).wait_recv()

    def finish_sends(self):
        for cp in self._first() + self._passed():
            cp.wait_send()
        for a in range(self.na):
            self._mine(a).wait()

    def finish(self):
        self.wait_sibling()
        for j in range(3):
            self.wait_passed(j)
        self.finish_sends()


class _Exchange:
    def __init__(self, scatter, gather, outs, sems, windows=None):
        self.windows = windows if windows is not None else [None] * len(scatter)
        self.ins = list(scatter) + list(gather)
        self.ns, self.na = len(scatter), len(scatter) + len(gather)
        self.outs = outs
        self.send_sems, self.recv_sems, self.local_sems = sems
        x, y, c = _coords()
        self.pos = (x, y, c)
        self.me = 4 * x + 2 * y + c

    def _peer(self, r):
        x, y, c = self.pos
        return x ^ (r >> 2), y ^ ((r >> 1) & 1), c ^ (r & 1)

    def _src(self, a, block):
        if a >= self.ns:
            return self.ins[a]
        if self.windows[a] is None:
            return self.ins[a].at[block]
        row0, rows = self.windows[a]
        return self.ins[a].at[block, pl.ds(row0, rows)]

    def _local(self, a):
        return pltpu.make_async_copy(self._src(a, self.me), self.outs[a].at[self.me], self.local_sems.at[a])

    def _send(self, a, r):
        px, py, pc = self._peer(r)
        return pltpu.make_async_remote_copy(
            src_ref=self._src(a, 4 * px + 2 * py + pc), dst_ref=self.outs[a].at[self.me],
            send_sem=self.send_sems.at[7 * a + r - 1], recv_sem=self.recv_sems.at[7 * a + r - 1],
            device_id=(px, py, pc), device_id_type=_MESH)

    def _recv(self, a, r):
        px, py, pc = self._peer(r)
        return pltpu.make_async_remote_copy(
            src_ref=self._src(a, self.me), dst_ref=self.outs[a].at[4 * px + 2 * py + pc],
            send_sem=self.send_sems.at[7 * a + r - 1], recv_sem=self.recv_sems.at[7 * a + r - 1],
            device_id=(px, py, pc), device_id_type=_MESH)

    def start(self):
        for a in range(self.na):
            self._local(a).start()
        for r in range(1, N_DEV):
            for a in range(self.na):
                self._send(a, r).start()

    def finish(self):
        for r in range(1, N_DEV):
            for a in range(self.na):
                self._recv(a, r).wait_recv()
        for r in range(1, N_DEV):
            for a in range(self.na):
                self._send(a, r).wait_send()
        for a in range(self.na):
            self._local(a).wait()


def _prologue(x, tgt, small_l, w_in_l, cast_f32):
    seq = x.shape[0]
    nx = seq // TM
    rest_rows = seq - nx * TM
    nt = nx + 1
    nc = len(cast_f32)
    body_rows = TM - N_META
    assert nx >= 1 and rest_rows % 8 == 0 and rest_rows <= body_rows
    x_rest, t_rest = x[nx * TM:], tgt[nx * TM:]

    def last_tile_body(rest_ref):
        parts = ([rest_ref[...]] if rest_rows else []) + (
            [jnp.zeros((body_rows - rest_rows, D), _F32)] if body_rows > rest_rows else [])
        return parts[0] if len(parts) == 1 else jnp.concatenate(parts, axis=0)

    def body(xm_ref, xp_ref, tm_ref, tp_ref, *rest):
        if rest_rows:
            xr_ref, tr_ref, rest = rest[0], rest[1], rest[2:]
        else:
            xr_ref = tr_ref = None
        s_ref, w_ref, rest = rest[0], rest[1], rest[2:]
        cins = rest[:nc]
        h0_ref, tgt_ref, small_ref, wg_ref = rest[nc:nc + 4]
        couts = rest[nc + 4:2 * nc + 4]
        s_stage, w_stage, meta, msem = rest[2 * nc + 4:2 * nc + 8]
        g_s = _Gather([s_stage], [small_ref], rest[2 * nc + 8:2 * nc + 11])
        g_w = _Gather([w_stage], [wg_ref], rest[2 * nc + 11:], place=_pair_place)
        s = pl.program_id(0)
        i = (s + 1) % nt

        @pl.when(s == 0)
        def _():
            s_stage[...] = s_ref[...]
            w_stage[...] = w_ref[...].astype(_BF)
            g_s.start()
            g_w.start()
            meta[...] = jnp.zeros_like(meta)
            for a in range(nc):
                couts[a][...] = cins[a][...].astype(_BF)

        @pl.when(s == nt - 1)
        def _():
            for j in range(3):
                g_s.forward(j)
            g_s.finish()
            cps = [pltpu.make_async_copy(small_ref.at[k, pl.ds(0, N_META), :], meta.at[:, pl.ds(128 * k, 128)],
                                         msem.at[k]) for k in range(N_DEV)]
            for cp in cps:
                cp.start()
            for cp in cps:
                cp.wait()
            for j in range(3):
                g_w.forward(j)
            g_w.finish()

        has_x = i < nx
        h0_ref[pl.ds(0, N_META), :] = jnp.where(i == 0, meta[...], xp_ref[...])
        h0_ref[pl.ds(N_META, body_rows), :] = jnp.where(has_x, xm_ref[pl.ds(0, body_rows), :], last_tile_body(xr_ref))
        tgt_ref[pl.ds(0, N_META), :] = jnp.where(i == 0, 0.0, tp_ref[...])
        tgt_ref[pl.ds(N_META, body_rows), :] = jnp.where(has_x, tm_ref[pl.ds(0, body_rows), :], last_tile_body(tr_ref))

    def tile_of(s):
        return (s + 1) % nt

    hbm = pl.BlockSpec(memory_space=pl.ANY)
    main = pl.BlockSpec((TM, D), lambda s: (jnp.minimum(tile_of(s), nx - 1), 0))
    prev = pl.BlockSpec((N_META, D), lambda s: (jnp.maximum(tile_of(s) * (TM // N_META) - 1, 0), 0))
    tile = pl.BlockSpec((TM, D), lambda s: (tile_of(s), 0))
    rests = [x_rest, t_rest] if rest_rows else []
    return pl.pallas_call(
        body, name="prologue", grid=(nt,),
        in_specs=[main, prev, main, prev] + [_const(r.shape) for r in rests]
        + [_const(small_l.shape), _const(w_in_l.shape)] + [_const(l.shape) for l in cast_f32],
        out_specs=[tile, tile, hbm, hbm] + [_full(l.shape) for l in cast_f32],
        out_shape=[_S((nt * TM, D), _F32), _S((nt * TM, D), _F32), _S((N_DEV,) + small_l.shape, _F32),
                   _S((4, D, WIN_P), _BF)] + [_S(l.shape, _BF) for l in cast_f32],
        scratch_shapes=[pltpu.VMEM(small_l.shape, _F32), pltpu.VMEM(w_in_l.shape, _BF), pltpu.VMEM((N_META, D), _F32),
                        pltpu.SemaphoreType.DMA((N_DEV,))] + _sem_shapes(1) + _sem_shapes(1),
        compiler_params=_cp(("arbitrary",)),
    )(x, x, tgt, tgt, *rests, small_l, w_in_l, *cast_f32)


def _adamw_math(w, g, m, v):
    m2 = ADAM_B1 * m + (1.0 - ADAM_B1) * g
    v2 = ADAM_B2 * v + (1.0 - ADAM_B2) * (g * g)
    m_hat = m2 / (1.0 - ADAM_B1 ** ADAM_STEP)
    v_hat = v2 / (1.0 - ADAM_B2 ** ADAM_STEP)
    delta = -ADAM_LR * (m_hat / (jnp.sqrt(v_hat) + ADAM_EPS) + ADAM_WD * w)
    return delta, m2, v2


def _adamw_big(name, recv, w, m, v, rows):
    r_all, c_all = w.shape

    def body(r_ref, w_ref, m_ref, v_ref, g_out, d_out, m_out, v_out):
        g = r_ref[0].astype(_F32)
        for k in range(1, N_DEV):
            g = g + r_ref[k].astype(_F32)
        delta, m2, v2 = _adamw_math(w_ref[...], g, m_ref[...], v_ref[...])
        g_out[...] = g
        d_out[...] = delta
        m_out[...] = m2
        v_out[...] = v2

    tile = pl.BlockSpec((rows, c_all), lambda i: (i, 0))
    return pl.pallas_call(
        body, name=name, grid=(r_all // rows,),
        in_specs=[pl.BlockSpec((N_DEV, rows, c_all), lambda i: (0, i, 0)), tile, tile, tile],
        out_specs=[tile] * 4,
        out_shape=[_S(w.shape, _F32)] * 4,
        compiler_params=_cp(("arbitrary",)),
    )(recv, w, m, v)


def _adamw_parts(name, recvs, w, m, v):
    def body(*refs):
        r_refs = refs[:len(recvs)]
        w_ref, m_ref, v_ref, g_out, d_out, m_out, v_out = refs[len(recvs):]
        row0 = 0
        for r_ref in r_refs:
            rows = pl.ds(row0, r_ref.shape[1])
            g = r_ref[0].astype(_F32)
            for k in range(1, N_DEV):
                g = g + r_ref[k].astype(_F32)
            delta, m2, v2 = _adamw_math(w_ref[rows, :], g, m_ref[rows, :], v_ref[rows, :])
            g_out[rows, :] = g
            d_out[rows, :] = delta
            m_out[rows, :] = m2
            v_out[rows, :] = v2
            row0 += r_ref.shape[1]

    return pl.pallas_call(
        body, name=name, out_shape=[_S(w.shape, _F32)] * 4,
        compiler_params=pltpu.CompilerParams(vmem_limit_bytes=VMEM_LIMIT),
    )(*recvs, w, m, v)


def _adamw_small(gathered, slices, wmv):
    ng, npar = len(gathered), len(slices)

    def body(*refs):
        g_refs = refs[:ng]
        wmv_refs = refs[ng:ng + 3 * npar]
        outs = refs[ng + 3 * npar:]
        for i, (ai, r0, nr, c0, ncol) in enumerate(slices):
            g = g_refs[ai][0, pl.ds(r0, nr), pl.ds(c0, ncol)].astype(_F32)
            for k in range(1, N_DEV):
                g = g + g_refs[ai][k, pl.ds(r0, nr), pl.ds(c0, ncol)].astype(_F32)
            w_ref, m_ref, v_ref = wmv_refs[3 * i:3 * i + 3]
            delta, m2, v2 = _adamw_math(w_ref[...], g, m_ref[...], v_ref[...])
            outs[4 * i][...] = g
            outs[4 * i + 1][...] = delta
            outs[4 * i + 2][...] = m2
            outs[4 * i + 3][...] = v2
        total = g_refs[0][0, pl.ds(R_LOSS, 1), pl.ds(0, 128)]
        for k in range(1, N_DEV):
            total = total + g_refs[0][k, pl.ds(R_LOSS, 1), pl.ds(0, 128)]
        outs[4 * npar][...] = total

    flat = [t for trip in wmv for t in trip]
    out_shape = []
    for w, _, _ in wmv:
        out_shape += [_S(w.shape, _F32)] * 4
    out_shape.append(_S((1, 128), _F32))
    return pl.pallas_call(
        body, name="adamw_small", out_shape=out_shape,
        compiler_params=pltpu.CompilerParams(vmem_limit_bytes=VMEM_LIMIT),
    )(*gathered, *flat)


def _block_diag(w):
    eye = jnp.eye(8, dtype=w.dtype)
    return (w[:, :, None, :] * eye[:, None, :, None]).reshape(D_RG, D_RG)


def _diag_blocks(g):
    return jnp.concatenate([g[64 * h:64 * (h + 1), 64 * h:64 * (h + 1)] for h in range(8)], axis=0)


def _local_step(h0, tgt_p, n_valid, g_mix, w_in, vec, wr, wi, hb, g_hg, w_out_l, g_ffn, w_gu_l, w_down_l, g_fin):
    t_pad = h0.shape[0]
    me = 4 * lax.axis_index("x") + 2 * lax.axis_index("y") + lax.axis_index("c")
    p, u, y, hs, o, sc, w_out, w_gu, w_down = _mixer_fwd(h0, g_mix, w_in, wr, wi, vec, hb, g_hg,
                                                         [w_out_l, w_gu_l, w_down_l])
    w_out = w_out.reshape(D, D)
    w_down = w_down.reshape(4, FFB, D)
    h1, v, gu, act, dh2, dh2b, loss, gfin = _ffn_loss(h0, y, w_out, g_ffn, w_gu, w_down, g_fin, tgt_p, n_valid)

    dgu, dh1, dh1b, dy, gffn = _ffn_bwd(dh2, dh2b, gu, h1, g_ffn, w_gu, w_down, w_out)
    g_wdown = _wgrad("wgrad_down", act, dh2b, pl.BlockSpec((1, t_pad, FFB), lambda j: (j, 0, 0)),
                     pl.BlockSpec((t_pad, D), lambda j: (0, 0)), 4, (FFB, D))
    g_wdown = g_wdown.reshape(N_DEV, D_FF // N_DEV, D)
    g_wgu, r_wdown_a = _wgrad("wgrad_gate_up", dgu, v, pl.BlockSpec((1, t_pad, FFB), lambda j: (j, 0, 0)),
                              pl.BlockSpec((t_pad, D), lambda j: (0, 0)), N_DEV, (FFB, D),
                              scatter=[g_wdown], windows=[(0, WDOWN_A)])
    g_wout = _wgrad("wgrad_out", y, dh1b, pl.BlockSpec((t_pad, D // N_DEV), lambda j: (0, j)),
                    pl.BlockSpec((t_pad, D), lambda j: (0, 0)), N_DEV, (D // N_DEV, D))
    dp, gvec, gw, r_wgu, r_wout, r_wdown_b = _mixer_bwd(
        p, hs, o, sc, dy, wr, wi, vec, hb, g_hg, [g_wgu, g_wout, g_wdown],
        [None, None, (WDOWN_A, D_FF // N_DEV - WDOWN_A)])
    r_wdown = (r_wdown_a, r_wdown_b)
    pack_c = jnp.concatenate([_diag_blocks(gw[0]), _diag_blocks(gw[1])], axis=1).astype(_BF)
    order = (me ^ jnp.array(_SEND_ORDER, jnp.int32)).astype(jnp.int32)
    grad_x, r_win, all_b, all_c, all_a = _inproj_bwd_send(dp, w_in, h0, dh1, g_mix, u, order, gffn, gfin, loss,
                                                          [gvec, pack_c], n_valid)
    return grad_x, (r_win, r_wgu, r_wout, r_wdown), (all_a, all_b, all_c)


def kernel(x, meta_tokens, mix_norm_g, w_in, conv_w, conv_b, w_rgate, b_rgate, w_igate, b_igate, lru_lambda, rg_norm_g, hg_lower_bound, hg_norm_g, w_out, ffn_norm_g, w_gate_up, w_down, final_norm_g, loss_target, m_meta_tokens, m_mix_norm_g, m_w_in, m_conv_w, m_conv_b, m_w_rgate, m_b_rgate, m_w_igate, m_b_igate, m_lru_lambda, m_rg_norm_g, m_hg_lower_bound, m_hg_norm_g, m_w_out, m_ffn_norm_g, m_w_gate_up, m_w_down, m_final_norm_g, v_meta_tokens, v_mix_norm_g, v_w_in, v_conv_w, v_conv_b, v_w_rgate, v_b_rgate, v_w_igate, v_b_igate, v_lru_lambda, v_rg_norm_g, v_hg_lower_bound, v_hg_norm_g, v_w_out, v_ffn_norm_g, v_w_gate_up, v_w_down, v_final_norm_g):
    seq = x.shape[1]
    me = 4 * lax.axis_index("x") + 2 * lax.axis_index("y") + lax.axis_index("c")

    n_valid = N_META + seq
    small_l = jnp.concatenate([meta_tokens, jnp.pad(conv_w[0], ((0, 4), (0, 64)))], axis=0)
    h0, tgt_p, small_g, w_in_g, w_gu_l, w_out_l, w_down_l = _prologue(
        x[0], loss_target[0], small_l, w_in[0], [w_gate_up[0].T, w_out[0], w_down[0]])
    conv_w_full = jnp.transpose(small_g[:, N_META:N_META + 4, :64], (1, 0, 2)).reshape(4, D_RG)
    vec = jnp.concatenate([conv_b, b_rgate, b_igate, lru_lambda, rg_norm_g, jnp.zeros((3, D_RG), _F32),
                           conv_w_full, jnp.zeros((4, D_RG), _F32)], axis=0)
    wr = _block_diag(w_rgate[0]).astype(_BF)
    wi = _block_diag(w_igate[0]).astype(_BF)

    grad_x, (r_win, r_wgu, r_wout, r_wdown), (all_a, all_b, all_c) = _local_step(
        h0, tgt_p, n_valid, mix_norm_g, w_in_g, vec, wr, wi, hg_lower_bound, hg_norm_g,
        w_out_l, ffn_norm_g, w_gu_l, w_down_l, final_norm_g.reshape(1, D))
    grad_x = grad_x[None]

    outs = {}
    outs["w_in"] = _adamw_big("adamw_w_in", r_win, w_in[0], m_w_in[0], v_w_in[0], 256)
    outs["w_gate_up"] = [r.T for r in _adamw_big("adamw_w_gate_up", r_wgu, w_gate_up[0].T, m_w_gate_up[0].T,
                                                 v_w_gate_up[0].T, 176)]
    outs["w_out"] = _adamw_big("adamw_w_out", r_wout, w_out[0], m_w_out[0], v_w_out[0], 128)
    outs["w_down"] = _adamw_parts("adamw_w_down", r_wdown, w_down[0], m_w_down[0], v_w_down[0])

    meta_part = lax.dynamic_slice_in_dim(all_a[:, R_META:R_META + N_META, :], me * 128, 128, axis=2)
    convw_part = lax.dynamic_slice_in_dim(all_b[:, R_CONVW:R_CONVW + 4, :], me * 64, 64, axis=2)
    gathered = [all_a, all_b, all_c, meta_part, convw_part]
    small_params = [
        ("meta_tokens", (3, 0, N_META, 0, 128), (meta_tokens, m_meta_tokens, v_meta_tokens), (N_META, 128)),
        ("mix_norm_g", (0, R_GMIX, 1, 0, D), (mix_norm_g, m_mix_norm_g, v_mix_norm_g), (1, D)),
        ("conv_w", (4, 0, 4, 0, 64), (conv_w, m_conv_w, v_conv_w), (4, 64)),
        ("conv_b", (1, R_CONVB, 1, 0, D_RG), (conv_b, m_conv_b, v_conv_b), (1, D_RG)),
        ("w_rgate", (2, 0, 512, 0, 64), (w_rgate, m_w_rgate, v_w_rgate), (512, 64)),
        ("b_rgate", (1, R_BR, 1, 0, D_RG), (b_rgate, m_b_rgate, v_b_rgate), (1, D_RG)),
        ("w_igate", (2, 0, 512, 64, 64), (w_igate, m_w_igate, v_w_igate), (512, 64)),
        ("b_igate", (1, R_BI, 1, 0, D_RG), (b_igate, m_b_igate, v_b_igate), (1, D_RG)),
        ("lru_lambda", (1, R_LAM, 1, 0, D_RG), (lru_lambda, m_lru_lambda, v_lru_lambda), (1, D_RG)),
        ("rg_norm_g", (1, R_GRG, 1, 0, D_RG), (rg_norm_g, m_rg_norm_g, v_rg_norm_g), (1, D_RG)),
        ("hg_lower_bound", (1, R_HB0, 2, 0, D_HG), (hg_lower_bound, m_hg_lower_bound, v_hg_lower_bound), (2, D_HG)),
        ("hg_norm_g", (1, R_GHG, 1, 0, HD), (hg_norm_g, m_hg_norm_g, v_hg_norm_g), (1, HD)),
        ("ffn_norm_g", (0, R_GFFN, 1, 0, D), (ffn_norm_g, m_ffn_norm_g, v_ffn_norm_g), (1, D)),
        ("final_norm_g", (0, R_GFIN, 1, 0, D), (final_norm_g, m_final_norm_g, v_final_norm_g), (1, D)),
    ]
    res = _adamw_small(gathered, [s[1] for s in small_params],
                       [tuple(t.reshape(s[3]) for t in s[2]) for s in small_params])
    for i, s in enumerate(small_params):
        outs[s[0]] = [r.reshape(s[2][0].shape) for r in res[4 * i:4 * i + 4]]
    for n, ref in (("w_in", w_in), ("w_gate_up", w_gate_up), ("w_out", w_out), ("w_down", w_down)):
        outs[n] = [r.reshape(ref.shape) for r in outs[n]]

    loss_all = res[4 * len(small_params)][0, 0]
    order = ["meta_tokens", "mix_norm_g", "w_in", "conv_w", "conv_b", "w_rgate", "b_rgate", "w_igate", "b_igate",
             "lru_lambda", "rg_norm_g", "hg_lower_bound", "hg_norm_g", "w_out", "ffn_norm_g", "w_gate_up", "w_down",
             "final_norm_g"]
    return (loss_all, grad_x, *[outs[n][0] for n in order], *[outs[n][1] for n in order],
            *[outs[n][2] for n in order], *[outs[n][3] for n in order])
```

```python
import functools

import jax
import jax.numpy as jnp
from jax import lax
from jax.experimental import pallas as pl
from jax.experimental.pallas import tpu as pltpu

_BF = jnp.bfloat16
_F32 = jnp.float32
_S = jax.ShapeDtypeStruct
_MESH = pl.DeviceIdType.MESH

N_DEV = 8
N_META = 16
D = 1024
D_RG = 512
D_HG = 512
HD = 128
NH = D_HG // HD
D_IN = 3072
D_FF = 2816
FFB = D_FF // 4
WIN_B = D_IN // N_DEV
WIN_P = 2 * WIN_B
EPS = 1e-6
LRU_C = 8.0
TM = 320
HC = 64
VMEM_LIMIT = 62 * 1024 * 1024

ADAM_LR = 0.001
ADAM_B1 = 0.9
ADAM_B2 = 0.999
ADAM_EPS = 1e-08
ADAM_WD = 0.01
ADAM_STEP = 10

_SEND_ORDER = (6, 4, 2, 7, 5, 3, 1, 0)

R_CONVB, R_BR, R_BI, R_LAM, R_GRG, R_HB0, R_HB1, R_GHG, R_CONVW = 0, 1, 2, 3, 4, 5, 6, 7, 8
R_GMIX, R_GFFN, R_GFIN, R_LOSS, R_META = 0, 1, 2, 3, 8


def _cp(sem=None, **kw):
    return pltpu.CompilerParams(dimension_semantics=sem, vmem_limit_bytes=VMEM_LIMIT, **kw)


def _dot(a, b):
    return jnp.dot(a, b, preferred_element_type=_F32)


def _dot_nt(a, b):
    return lax.dot_general(a, b, (((1,), (1,)), ((), ())), preferred_element_type=_F32)


def _dot_tn(a, b):
    return lax.dot_general(a, b, (((0,), (0,)), ((), ())), preferred_element_type=_F32)


def _sigmoid(x):
    return 0.5 * jnp.tanh(0.5 * x) + 0.5


def _dsilu(x, s):
    return s * (1.0 + x * (1.0 - s))


_GELU_C = 0.7978845608028654


def _gelu_parts(x):
    t = jnp.tanh(_GELU_C * (x + 0.044715 * (x * x * x)))
    g = 0.5 * x * (1.0 + t)
    dg = 0.5 * (1.0 + t) + 0.5 * x * (1.0 - t * t) * (_GELU_C * (1.0 + 3.0 * 0.044715 * (x * x)))
    return g, dg


def _softplus(z):
    e = jnp.exp(-jnp.abs(z))
    w = 1.0 + e
    l1p = jnp.where(w == 1.0, e, jnp.log(w) * e / jnp.where(w == 1.0, 1.0, w - 1.0))
    return jnp.maximum(z, 0.0) + l1p


def _rms_fwd(x):
    r = lax.rsqrt(jnp.mean(x * x, axis=-1, keepdims=True) + EPS)
    return x * r, r


def _rms_bwd(dyg, n, r):
    return r * (dyg - n * jnp.mean(dyg * n, axis=-1, keepdims=True))


def _full(shape):
    nd = len(shape)
    return pl.BlockSpec(shape, lambda i: (0,) * nd)


def _const(shape):
    nd = len(shape)
    return pl.BlockSpec(shape, lambda i: (0,) * nd, pipeline_mode=pl.Buffered(1))


def _carry_gather(gather, i, nt, early=0):
    @pl.when(i == 0)
    def _():
        gather.start()

    def tail():
        for j in range(3):
            if early:
                @pl.when(i == min(nt // 3 + j, nt - 1))
                def _(j=j):
                    gather.forward(j, range(early))

            @pl.when(i == max(nt - 4 + j, 0))
            def _(j=j):
                gather.forward(j, range(early, gather.na))

        @pl.when(i == nt - 1)
        def _():
            gather.finish()

    return tail


def _pair_place(ref, block):
    return ref.at[block // 2, :, pl.ds(pl.multiple_of((block % 2) * WIN_B, WIN_B), WIN_B)]


def _rg_gates(xc, wr_ref, wi_ref, vec_ref):
    xcb = xc.astype(_BF)
    r = _sigmoid(_dot(xcb, wr_ref[...]) + vec_ref[R_BR:R_BR + 1, :])
    ig = _sigmoid(_dot(xcb, wi_ref[...]) + vec_ref[R_BI:R_BI + 1, :])
    nsp8 = -LRU_C * _softplus(-vec_ref[R_LAM:R_LAM + 1, :])
    la = nsp8 * r
    a = jnp.exp(la)
    th = jnp.tanh(la)
    s = jnp.sqrt(-2.0 * th / (1.0 - th))
    return r, ig, a, s, nsp8


def _conv(xbuf, vec_ref):
    acc = vec_ref[R_CONVW:R_CONVW + 1, :] * xbuf[pl.ds(5, TM), :]
    for j in range(1, 4):
        acc = acc + vec_ref[R_CONVW + j:R_CONVW + j + 1, :] * xbuf[pl.ds(5 + j, TM), :]
    return vec_ref[R_CONVB:R_CONVB + 1, :] + acc


def _dot3(m01, x):
    hi = x.astype(_BF)
    r1 = x - hi.astype(_F32)
    mid = r1.astype(_BF)
    lo = (r1 - mid.astype(_F32)).astype(_BF)
    return (_dot(m01, lo) + _dot(m01, mid)) + _dot(m01, hi)


def _chunk_dot3(m01, x):
    return jnp.concatenate([_dot3(m01, x[HC * c:HC * (c + 1), :]) for c in range(x.shape[0] // HC)], axis=0)


def _chunk_masks():
    row = lax.broadcasted_iota(jnp.int32, (HC, HC), 0)
    col = lax.broadcasted_iota(jnp.int32, (HC, HC), 1)
    return (row >= col).astype(_BF), (col >= row).astype(_BF), jnp.ones((HC, HC), _BF)


def _per_chunk_rows(x, r):
    return jnp.concatenate([jnp.broadcast_to(x[HC * c + r:HC * c + r + 1, :], (HC, x.shape[1]))
                            for c in range(TM // HC)], axis=0)


def _hg_prep(p_ref, lb, tri):
    hq = p_ref[:, pl.ds(2 * D_RG, D_HG)]
    hf = p_ref[:, pl.ds(2 * D_RG + D_HG, D_HG)]
    sq = _sigmoid(hq)
    q = hq * sq
    sg = _sigmoid(hf)
    f = lb + (1.0 - lb) * sg
    k = 1.0 - f
    b = _chunk_dot3(tri, jnp.log(f))
    bm = _per_chunk_rows(b, HC // 2 - 1)
    bl = _per_chunk_rows(b, HC - 1)
    e_q = jnp.exp(b - bm)
    e_k = jnp.exp(bm - b)
    e_b = jnp.exp(b)
    e_l = jnp.exp(bl - b)
    return dict(hq=hq, sq=sq, q=q, sg=sg, f=f, k=k, e_q=e_q, e_k=e_k, e_b=e_b, e_l=e_l,
                qd=q * e_q, kd=k * e_k, qe=q * e_b, ke=k * e_l, e_end=jnp.exp(bl))


def _mixer_fwd(h0, g_mix, w_in, wr, wi, vec, hb, g_hg, shards):
    t_pad = h0.shape[0]
    nt = t_pad // TM
    nc_t = TM // HC
    nsh = len(shards)

    def body(h_ref, gmix_ref, win_ref, wr_ref, wi_ref, vec_ref, hb_ref, ghg_ref, *rest):
        sh_refs, rest = rest[:nsh], rest[nsh:]
        pout_ref, uout_ref, y_ref, hs_ref, o_ref, sc_ref = rest[:6]
        gath_refs, rest = rest[6:6 + nsh], rest[6 + nsh:]
        xbuf, a_s, b_s, hcar, st, qd_s, kd_s, qe_s, ke_s, v_s, u_s, p_s, p_ref = rest[:13]
        i = pl.program_id(0)
        tail = _carry_gather(_Gather(sh_refs, gath_refs, rest[13:]), i, nt + 1, early=1)

        @pl.when(i == 0)
        def _():
            p_s[...] = jnp.zeros_like(p_s)

        p_ref[...] = p_s[...]

        @pl.when(i <= 1)
        def _():
            xbuf[pl.ds(0, 8), :] = jnp.zeros((8, D_RG), _F32)
            hcar[...] = jnp.zeros_like(hcar)
            st[...] = jnp.zeros_like(st)

        n_h, _ = _rms_fwd(h_ref[...])
        u = (n_h * gmix_ref[...]).astype(_BF)
        uout_ref[...] = u
        pieces = [(j, k) for j in range(4) for k in range(WIN_P // 256)]

        def project(count):
            for _ in range(count):
                j, k = pieces.pop(0)
                blk = _dot(u, win_ref[j, :, pl.ds(256 * k, 256)])
                p_s[:, pl.ds(WIN_P * j + 256 * k, 256)] = blk
                pout_ref[:, pl.ds(WIN_P * j + 256 * k, 256)] = blk

        x = p_ref[:, pl.ds(0, D_RG)]
        xbuf[pl.ds(8, TM), :] = x
        xc = _conv(xbuf, vec_ref)
        xbuf[pl.ds(0, 8), :] = x[TM - 8:, :]
        r, ig, a, s, _ = _rg_gates(xc, wr_ref, wi_ref, vec_ref)
        a_s[...] = a
        b_s[...] = s * (ig * xc)

        def step(t, h):
            h = a_s[pl.ds(t, 1), :] * h + b_s[pl.ds(t, 1), :]
            hs_ref[pl.ds(t, 1), :] = h
            return h

        hcar[pl.ds(0, 1), :] = lax.fori_loop(0, TM, step, hcar[pl.ds(0, 1), :], unroll=8)
        gel, _ = _gelu_parts(p_ref[:, pl.ds(D_RG, D_RG)])
        n, _ = _rms_fwd(gel * hs_ref[...])
        y_ref[:, pl.ds(0, D_RG)] = (n * vec_ref[R_GRG:R_GRG + 1, :]).astype(_BF)

        lb = _sigmoid(hb_ref[0:1, :] - hb_ref[1:2, :])
        tri, _, _ = _chunk_masks()
        q = _hg_prep(p_ref, lb, tri)
        for name, ref in (("qd", qd_s), ("kd", kd_s), ("qe", qe_s), ("ke", ke_s)):
            ref[...] = q[name].astype(_BF)
        v_s[...] = p_ref[:, pl.ds(2 * D_RG + 2 * D_HG, D_HG)].astype(_BF)
        e_end = q["e_end"]
        causal = (lax.broadcasted_iota(jnp.int32, (HC, HC), 0) >= lax.broadcasted_iota(jnp.int32, (HC, HC), 1))
        for c in range(nc_t):
            for h in range(NH):
                rs, cs = pl.ds(HC * c, HC), pl.ds(HD * h, HD)
                amat = jnp.where(causal, _dot_nt(qd_s[rs, cs], kd_s[rs, cs]), 0.0)
                o_ref[rs, cs] = _dot(amat.astype(_BF), v_s[rs, cs])
                u_s[NH * c + h] = _dot_tn(v_s[rs, cs], ke_s[rs, cs])
                if pieces:
                    project(1)
        assert not pieces
        for h in range(NH):
            cs = pl.ds(HD * h, HD)
            s_run = st[h]
            for c in range(nc_t):
                rs = pl.ds(HC * c, HC)
                sc_ref[c, h] = s_run
                o_ref[rs, cs] += _dot_nt(qe_s[rs, cs], s_run.astype(_BF))
                s_run = e_end[HC * c:HC * c + 1, HD * h:HD * (h + 1)] * s_run + u_s[NH * c + h]
            st[h] = s_run
        for h in range(NH):
            cs = pl.ds(HD * h, HD)
            n_o, _ = _rms_fwd(o_ref[:, cs])
            hg = p_ref[:, pl.ds(2 * D_RG + 3 * D_HG + HD * h, HD)]
            y_ref[:, pl.ds(D_RG + HD * h, HD)] = ((n_o * ghg_ref[...]) * (hg * _sigmoid(hg))).astype(_BF)

        tail()

    hbm = pl.BlockSpec(memory_space=pl.ANY)

    def proj(i):
        return jnp.minimum(i, nt - 1)

    def mixed(i):
        return jnp.maximum(i - 1, 0)

    return pl.pallas_call(
        body, name="mixer_fwd", grid=(nt + 1,),
        in_specs=[pl.BlockSpec((TM, D), lambda i: (proj(i), 0)), _full((1, D)), _const((4, D, WIN_P)),
                  _full((D_RG, D_RG)), _full((D_RG, D_RG)),
                  _full((16, D_RG)), _full((2, D_HG)), _full((1, HD))] + [hbm] * nsh,
        out_specs=[pl.BlockSpec((TM, D_IN), lambda i: (proj(i), 0)), pl.BlockSpec((TM, D), lambda i: (proj(i), 0)),
                   pl.BlockSpec((TM, D), lambda i: (mixed(i), 0)), pl.BlockSpec((TM, D_RG), lambda i: (mixed(i), 0)),
                   pl.BlockSpec((TM, D_HG), lambda i: (mixed(i), 0)),
                   pl.BlockSpec((nc_t, NH, HD, HD), lambda i: (mixed(i), 0, 0, 0))] + [hbm] * nsh,
        out_shape=[_S((t_pad, D_IN), _F32), _S((t_pad, D), _BF),
                   _S((t_pad, D), _BF), _S((t_pad, D_RG), _F32), _S((t_pad, D_HG), _F32),
                   _S((t_pad // HC, NH, HD, HD), _F32)] + [_S((N_DEV,) + s.shape, s.dtype) for s in shards],
        scratch_shapes=[pltpu.VMEM((TM + 8, D_RG), _F32), pltpu.VMEM((TM, D_RG), _F32),
                        pltpu.VMEM((TM, D_RG), _F32), pltpu.VMEM((8, D_RG), _F32),
                        pltpu.VMEM((NH, HD, HD), _F32)] + [pltpu.VMEM((TM, D_HG), _BF) for _ in range(5)]
        + [pltpu.VMEM((nc_t * NH, HD, HD), _F32), pltpu.VMEM((TM, D_IN), _F32), pltpu.VMEM((TM, D_IN), _F32)]
        + _sem_shapes(nsh),
        compiler_params=_cp(("arbitrary",)),
    )(h0, g_mix, w_in, wr, wi, vec, hb, g_hg, *shards)


def _ffn_loss(h0, y, w_out, g_ffn, w_gu, w_down, g_fin, tgt, n_valid):
    t_pad = h0.shape[0]

    def body(h_ref, y_ref, wo_ref, gffn_ref, wgu_ref, wd_ref, g_ref, t_ref,
             h1_ref, v_ref, gu_ref, act_ref, dh2_ref, dh2b_ref, loss_ref, gfin_ref):
        i = pl.program_id(0)

        @pl.when(i == 0)
        def _():
            loss_ref[...] = jnp.zeros_like(loss_ref)
            gfin_ref[...] = jnp.zeros_like(gfin_ref)

        h1 = h_ref[...] + _dot(y_ref[...], wo_ref[...])
        h1_ref[...] = h1
        n1, _ = _rms_fwd(h1)
        vb = (n1 * gffn_ref[...]).astype(_BF)
        v_ref[...] = vb
        h2 = h1
        for b in range(4):
            gate = _dot_nt(vb, wgu_ref[b])
            up = _dot_nt(vb, wgu_ref[4 + b])
            gu_ref[b] = gate
            gu_ref[4 + b] = up
            act = ((gate * _sigmoid(gate)) * up).astype(_BF)
            act_ref[b] = act
            h2 = h2 + _dot(act, wd_ref[b])
        n, r = _rms_fwd(h2)
        out = n * g_ref[...]
        row = i * TM + lax.broadcasted_iota(jnp.int32, (TM, 1), 0)
        valid = (row >= N_META) & (row < n_valid)
        err = jnp.where(valid, out - t_ref[...], 0.0)
        loss_ref[...] += (0.5 / D) * jnp.sum(err * err)
        dout = err * (1.0 / D)
        gfin_ref[...] += jnp.sum(dout * n, axis=0, keepdims=True)
        dh2 = _rms_bwd(dout * g_ref[...], n, r)
        dh2_ref[...] = dh2
        dh2b_ref[...] = dh2.astype(_BF)

    tile = pl.BlockSpec((TM, D), lambda i: (i, 0))
    return pl.pallas_call(
        body, name="ffn_loss", grid=(t_pad // TM,),
        in_specs=[tile, tile, _const((D, D)), _full((1, D)),
                  _const((N_DEV, FFB, D)), _const((4, FFB, D)), _full((1, D)), tile],
        out_specs=[tile, tile,
                   pl.BlockSpec((N_DEV, TM, FFB), lambda i: (0, i, 0)), pl.BlockSpec((4, TM, FFB), lambda i: (0, i, 0)),
                   tile, tile, _full((8, 128)), _full((1, D))],
        out_shape=[_S((t_pad, D), _F32), _S((t_pad, D), _BF),
                   _S((N_DEV, t_pad, FFB), _F32), _S((4, t_pad, FFB), _BF), _S((t_pad, D), _F32),
                   _S((t_pad, D), _BF), _S((8, 128), _F32), _S((1, D), _F32)],
        compiler_params=_cp(("arbitrary",)),
    )(h0, y, w_out, g_ffn, w_gu, w_down, g_fin, tgt)


def _ffn_bwd(dh2, dh2b, gu, h1, g_ffn, w_gu, w_down, w_out, scatter):
    t_pad = dh2.shape[0]
    nsc = len(scatter)
    nt = t_pad // TM

    def body(dh2_ref, dh2b_ref, gu_ref, h1_ref, g_ref, wgu_ref, wd_ref, wo_ref, *rest):
        dgu_ref, dh1_ref, dh1b_ref, dy_ref, gffn_ref = rest[nsc:nsc + 5]
        exchange = _Exchange(rest[:nsc], [], rest[nsc + 5:2 * nsc + 5], rest[2 * nsc + 5:])
        i = pl.program_id(0)

        @pl.when(i == 0)
        def _():
            exchange.start()
            gffn_ref[...] = jnp.zeros_like(gffn_ref)

        db = dh2b_ref[...]
        dv = jnp.zeros((TM, D), _F32)
        for b in range(4):
            dact = _dot_nt(db, wd_ref[b])
            gate = gu_ref[b]
            up = gu_ref[4 + b]
            sg = _sigmoid(gate)
            dgate = ((dact * up) * _dsilu(gate, sg)).astype(_BF)
            dup = (dact * (gate * sg)).astype(_BF)
            dgu_ref[b] = dgate
            dgu_ref[4 + b] = dup
            dv = dv + _dot(dgate, wgu_ref[b]) + _dot(dup, wgu_ref[4 + b])
        n, r = _rms_fwd(h1_ref[...])
        gffn_ref[...] += jnp.sum(dv * n, axis=0, keepdims=True)
        dh1 = dh2_ref[...] + _rms_bwd(dv * g_ref[...], n, r)
        dh1_ref[...] = dh1
        dh1b = dh1.astype(_BF)
        dh1b_ref[...] = dh1b
        dy_ref[...] = _dot_nt(dh1b, wo_ref[...])

        @pl.when(i == nt - 1)
        def _():
            exchange.finish()

    tile = pl.BlockSpec((TM, D), lambda i: (i, 0))
    hbm = pl.BlockSpec(memory_space=pl.ANY)
    return pl.pallas_call(
        body, name="ffn_bwd", grid=(nt,),
        in_specs=[tile, tile, pl.BlockSpec((N_DEV, TM, FFB), lambda i: (0, i, 0)), tile, _full((1, D)),
                  _const((N_DEV, FFB, D)), _const((4, FFB, D)), _const((D, D))] + [hbm] * nsc,
        out_specs=[pl.BlockSpec((N_DEV, TM, FFB), lambda i: (0, i, 0)), tile, tile, tile, _full((1, D))] + [hbm] * nsc,
        out_shape=[_S((N_DEV, t_pad, FFB), _BF), _S((t_pad, D), _F32), _S((t_pad, D), _BF),
                   _S((t_pad, D), _F32), _S((1, D), _F32)] + _recv_shapes(scatter, [None] * nsc),
        scratch_shapes=_sem_shapes(nsc),
        compiler_params=_cp(("arbitrary",)),
    )(dh2, dh2b, gu, h1, g_ffn, w_gu, w_down, w_out, *scatter)


def _mixer_bwd(p, hs, o, sc, dy, wr, wi, vec, hb, g_hg, scatter, windows):
    t_pad = p.shape[0]
    nt = t_pad // TM
    nc_t = TM // HC
    nsc = len(scatter)

    def rev(i):
        return nt - 1 - i

    def body(p_ref, pprev_ref, hs_ref, hprev_ref, o_ref, sc_ref, dy_ref, wr_ref, wi_ref, vec_ref, hb_ref, ghg_ref,
             *rest):
        send_refs, rest = rest[:nsc], rest[nsc:]
        dp_ref, gvec_ref, gw_ref = rest[:3]
        recv_refs, rest = rest[3:3 + nsc], rest[3 + nsc:]
        xbuf, hbuf, dbuf, a_s, g_s, ccar, dst = rest[:7]
        qd_s, kd_s, qe_s, ke_s, v_s, do_s, dqd_s, dkd_s, dqe_s, dke_s, dv_s, w_s, dend_s = rest[7:20]
        exchange = _Exchange(send_refs, [], recv_refs, rest[20:], windows)
        i = pl.program_id(0)
        first_tile = i == nt - 1

        @pl.when(i == 0)
        def _():
            exchange.start()
            gvec_ref[...] = jnp.zeros_like(gvec_ref)
            gw_ref[...] = jnp.zeros_like(gw_ref)
            dbuf[pl.ds(TM, 8), :] = jnp.zeros((8, D_RG), _F32)
            ccar[...] = jnp.zeros_like(ccar)
            dst[...] = jnp.zeros_like(dst)

        def acc(row, val):
            gvec_ref[row:row + 1, :] += jnp.sum(val, axis=0, keepdims=True)

        keep = jnp.where(first_tile, 0.0, 1.0)
        x = p_ref[:, pl.ds(0, D_RG)]
        xbuf[pl.ds(0, 8), :] = pprev_ref[...] * keep
        xbuf[pl.ds(8, TM), :] = x
        xc = _conv(xbuf, vec_ref)
        r, ig, a, s, nsp8 = _rg_gates(xc, wr_ref, wi_ref, vec_ref)
        h = hs_ref[...]
        hbuf[pl.ds(0, 8), :] = hprev_ref[...] * keep
        hbuf[pl.ds(8, TM), :] = h
        hm1 = hbuf[pl.ds(7, TM), :]
        gr = p_ref[:, pl.ds(D_RG, D_RG)]
        gel, dgel = _gelu_parts(gr)
        n, rr = _rms_fwd(gel * h)
        dyn = dy_ref[:, pl.ds(0, D_RG)]
        acc(R_GRG, dyn * n)
        dpre = _rms_bwd(dyn * vec_ref[R_GRG:R_GRG + 1, :], n, rr)
        dp_ref[:, pl.ds(D_RG, D_RG)] = ((dpre * h) * dgel).astype(_BF)
        a_s[...] = a
        g_s[...] = dpre * gel

        def step(k, c):
            t = TM - 1 - k
            g = g_s[pl.ds(t, 1), :] + c
            g_s[pl.ds(t, 1), :] = g
            return a_s[pl.ds(t, 1), :] * g

        ccar[pl.ds(0, 1), :] = lax.fori_loop(0, TM, step, ccar[pl.ds(0, 1), :], unroll=8)
        gt = g_s[...]
        da = gt * hm1
        ixc = ig * xc
        ds = gt * ixc
        dig = (gt * s) * xc
        dxc = (gt * s) * ig
        dla = da * a - ds * ((a * a) / s)
        lam = vec_ref[R_LAM:R_LAM + 1, :]
        gvec_ref[R_LAM:R_LAM + 1, :] += jnp.sum(dla * r, axis=0, keepdims=True) * (LRU_C * _sigmoid(-lam))
        dzr = (dla * nsp8) * (r * (1.0 - r))
        dzi = dig * (ig * (1.0 - ig))
        acc(R_BR, dzr)
        acc(R_BI, dzi)
        xcb = xc.astype(_BF)
        dzrb = dzr.astype(_BF)
        dzib = dzi.astype(_BF)
        gw_ref[0] += _dot_tn(xcb, dzrb)
        gw_ref[1] += _dot_tn(xcb, dzib)
        dxc = dxc + _dot_nt(dzrb, wr_ref[...]) + _dot_nt(dzib, wi_ref[...])
        acc(R_CONVB, dxc)
        for j in range(4):
            acc(R_CONVW + j, dxc * xbuf[pl.ds(5 + j, TM), :])
        dbuf[pl.ds(0, TM), :] = dxc
        dx = vec_ref[R_CONVW + 3:R_CONVW + 4, :] * dxc
        for j in range(3):
            dx = dx + vec_ref[R_CONVW + j:R_CONVW + j + 1, :] * dbuf[pl.ds(3 - j, TM), :]
        dbuf[pl.ds(TM, 8), :] = dxc[0:8, :]
        dp_ref[:, pl.ds(0, D_RG)] = dx.astype(_BF)

        lb = _sigmoid(hb_ref[0:1, :] - hb_ref[1:2, :])
        tri, tri_rev, ones = _chunk_masks()
        q = _hg_prep(p_ref, lb, tri)
        qdb, kdb = q["qd"].astype(_BF), q["kd"].astype(_BF)
        qd_s[...] = qdb
        kd_s[...] = kdb
        qe_s[...] = q["qe"].astype(_BF)
        ke_s[...] = q["ke"].astype(_BF)
        v_s[...] = p_ref[:, pl.ds(2 * D_RG + 2 * D_HG, D_HG)].astype(_BF)
        e_end = q["e_end"]
        ghg = ghg_ref[...]
        for h in range(NH):
            cs = pl.ds(HD * h, HD)
            hg = p_ref[:, pl.ds(2 * D_RG + 3 * D_HG + HD * h, HD)]
            sh = _sigmoid(hg)
            n_o, r_o = _rms_fwd(o_ref[:, cs])
            dyh = dy_ref[:, pl.ds(D_RG + HD * h, HD)]
            dp_ref[:, pl.ds(2 * D_RG + 3 * D_HG + HD * h, HD)] = ((dyh * (n_o * ghg)) * _dsilu(hg, sh)).astype(_BF)
            dn = dyh * (hg * sh)
            gvec_ref[R_GHG:R_GHG + 1, pl.ds(0, HD)] += jnp.sum(dn * n_o, axis=0, keepdims=True)
            do_s[:, cs] = _rms_bwd(dn * ghg, n_o, r_o).astype(_BF)
        causal = (lax.broadcasted_iota(jnp.int32, (HC, HC), 0) >= lax.broadcasted_iota(jnp.int32, (HC, HC), 1))
        for c in range(nc_t):
            for h in range(NH):
                rs, cs = pl.ds(HC * c, HC), pl.ds(HD * h, HD)
                qd_c, kd_c, do_c = qd_s[rs, cs], kd_s[rs, cs], do_s[rs, cs]
                amat = jnp.where(causal, _dot_nt(qd_c, kd_c), 0.0).astype(_BF)
                da_m = jnp.where(causal, _dot_nt(do_c, v_s[rs, cs]), 0.0).astype(_BF)
                dqd_s[rs, cs] = _dot(da_m, kd_c)
                dkd_s[rs, cs] = _dot_tn(da_m, qd_c)
                dqe_s[rs, cs] = _dot(do_c, sc_ref[c, h].astype(_BF))
                dv_s[rs, cs] = _dot_tn(amat, do_c)
                w_s[NH * c + h] = _dot_tn(do_c, qe_s[rs, cs])
        for h in range(NH):
            cs = pl.ds(HD * h, HD)
            d_run = dst[h]
            for c in reversed(range(nc_t)):
                rs = pl.ds(HC * c, HC)
                d_b = d_run.astype(_BF)
                dke_s[rs, cs] = _dot(v_s[rs, cs], d_b)
                dp_ref[rs, pl.ds(2 * D_RG + 2 * D_HG + HD * h, HD)] = (
                    dv_s[rs, cs] + _dot_nt(ke_s[rs, cs], d_b)).astype(_BF)
                dend_s[pl.ds(c, 1), cs] = jnp.sum(sc_ref[c, h] * d_run, axis=0, keepdims=True)
                d_run = w_s[NH * c + h] + e_end[HC * c:HC * c + 1, HD * h:HD * (h + 1)] * d_run
            dst[h] = d_run
        dqd, dkd, dqe, dke = dqd_s[...], dkd_s[...], dqe_s[...], dke_s[...]
        dq = dqd * q["e_q"] + dqe * q["e_b"]
        dk = dkd * q["e_k"] + dke * q["e_l"]
        dkeke = dke * q["ke"]
        db = dqd * qdb.astype(_F32) - dkd * kdb.astype(_F32) + dqe * q["qe"] - dkeke
        d_end = jnp.concatenate([jnp.broadcast_to(dend_s[pl.ds(c, 1), :], (HC, D_HG)) for c in range(nc_t)], axis=0)
        dlf = _chunk_dot3(tri_rev, db) + _chunk_dot3(ones, dkeke) + d_end * e_end
        df = dlf / q["f"] - dk
        sg = q["sg"]
        gvec_ref[R_HB0:R_HB0 + 1, :] += jnp.sum(df * (1.0 - sg), axis=0, keepdims=True)
        dp_ref[:, pl.ds(2 * D_RG, D_HG)] = (dq * _dsilu(q["hq"], q["sq"])).astype(_BF)
        dp_ref[:, pl.ds(2 * D_RG + D_HG, D_HG)] = ((df * (1.0 - lb)) * (sg * (1.0 - sg))).astype(_BF)

        @pl.when(i == nt - 1)
        def _():
            glb = gvec_ref[R_HB0:R_HB0 + 1, :] * (lb * (1.0 - lb))
            gvec_ref[R_HB0:R_HB0 + 1, :] = glb
            gvec_ref[R_HB1:R_HB1 + 1, :] = -glb
            exchange.finish()

    hbm = pl.BlockSpec(memory_space=pl.ANY)
    return pl.pallas_call(
        body, name="mixer_bwd", grid=(nt,),
        in_specs=[pl.BlockSpec((TM, D_IN), lambda i: (rev(i), 0)),
                  pl.BlockSpec((8, D_RG), lambda i: (jnp.maximum(rev(i) * (TM // 8) - 1, 0), 0)),
                  pl.BlockSpec((TM, D_RG), lambda i: (rev(i), 0)),
                  pl.BlockSpec((8, D_RG), lambda i: (jnp.maximum(rev(i) * (TM // 8) - 1, 0), 0)),
                  pl.BlockSpec((TM, D_HG), lambda i: (rev(i), 0)),
                  pl.BlockSpec((nc_t, NH, HD, HD), lambda i: (rev(i), 0, 0, 0)),
                  pl.BlockSpec((TM, D), lambda i: (rev(i), 0)),
                  _full((D_RG, D_RG)), _full((D_RG, D_RG)), _full((16, D_RG)), _full((2, D_HG)), _full((1, HD))]
        + [hbm] * nsc,
        out_specs=[pl.BlockSpec((TM, D_IN), lambda i: (rev(i), 0)), _full((16, D_RG)), _full((2, D_RG, D_RG))]
        + [hbm] * nsc,
        out_shape=[_S((t_pad, D_IN), _BF), _S((16, D_RG), _F32), _S((2, D_RG, D_RG), _F32)]
        + _recv_shapes(scatter, windows),
        scratch_shapes=[pltpu.VMEM((TM + 8, D_RG), _F32), pltpu.VMEM((TM + 8, D_RG), _F32),
                        pltpu.VMEM((TM + 8, D_RG), _F32), pltpu.VMEM((TM, D_RG), _F32),
                        pltpu.VMEM((TM, D_RG), _F32), pltpu.VMEM((8, D_RG), _F32),
                        pltpu.VMEM((NH, HD, HD), _F32)]
        + [pltpu.VMEM((TM, D_HG), _BF) for _ in range(6)] + [pltpu.VMEM((TM, D_HG), _F32) for _ in range(5)]
        + [pltpu.VMEM((nc_t * NH, HD, HD), _F32), pltpu.VMEM((8, D_HG), _F32)] + _sem_shapes(nsc),
        compiler_params=_cp(("arbitrary",)),
    )(p, p, hs, hs, o, sc, dy, wr, wi, vec, hb, g_hg, *scatter)


def _inproj_bwd_send(dp, w_in, h0, dh1, g_mix, u, order, gffn, gfin, loss, to_all):
    t_pad = dp.shape[0]
    rb = TM
    n_steps = N_DEV + t_pad // rb
    na = len(to_all)

    def body(order_ref, dpc_ref, dpr_ref, u_ref, w_ref, h_ref, dh1_ref, g_ref, gffn_ref, gfin_ref, loss_ref, *rest):
        all_in = rest[:na]
        dh0_ref, recv_ref = rest[na:na + 2]
        all_out = rest[na + 2:2 * na + 2]
        alla_ref = rest[2 * na + 2]
        buf, pack, blk_send, blk_recv, blk_local = rest[2 * na + 3:2 * na + 8]
        exchange = _Exchange([], all_in, all_out, rest[2 * na + 8:2 * na + 11])
        last = _Exchange([], [pack], [alla_ref], rest[2 * na + 11:])
        s = pl.program_id(0)
        x, y, c = _coords()
        me = 4 * x + 2 * y + c

        def send(step):
            r = _SEND_ORDER[step]
            return pltpu.make_async_remote_copy(
                src_ref=buf.at[step], dst_ref=recv_ref.at[me], send_sem=blk_send.at[step], recv_sem=blk_recv.at[r - 1],
                device_id=(x ^ (r >> 2), y ^ ((r >> 1) & 1), c ^ (r & 1)), device_id_type=_MESH)

        @pl.when(s == 0)
        def _():
            exchange.start()
            pack[...] = jnp.zeros_like(pack)

        @pl.when(s < N_DEV)
        def _():
            buf[s] = _dot_tn(u_ref[...], dpc_ref[...]).astype(_BF)

            for step in range(N_DEV - 1):
                @pl.when(s == step)
                def _(step=step):
                    send(step).start()

        @pl.when(s >= N_DEV)
        def _():
            du = jnp.zeros((rb, D), _F32)
            for j in range(4):
                du = du + _dot_nt(dpr_ref[:, WIN_P * j:WIN_P * (j + 1)], w_ref[j])
            n, r = _rms_fwd(h_ref[...])
            pack[R_GMIX:R_GMIX + 1, :] += jnp.sum(du * n, axis=0, keepdims=True)
            dh0 = dh1_ref[...] + _rms_bwd(du * g_ref[...], n, r)
            dh0_ref[...] = dh0

            @pl.when(s == N_DEV)
            def _():
                pack[R_META:R_META + N_META, :] = dh0[0:N_META, :]

        @pl.when(s == n_steps - 1)
        def _():
            pack[R_GFFN:R_GFFN + 1, :] = gffn_ref[...]
            pack[R_GFIN:R_GFIN + 1, :] = gfin_ref[...]
            pack[R_LOSS:R_LOSS + 1, pl.ds(0, 128)] = loss_ref[0:1, :]
            last.start()
            mine = pltpu.make_async_copy(buf.at[N_DEV - 1], recv_ref.at[me], blk_local.at[0])
            mine.start()
            for step in range(N_DEV - 1):
                send(step).wait_send()
            for r in range(1, N_DEV):
                px, py, pc = x ^ (r >> 2), y ^ ((r >> 1) & 1), c ^ (r & 1)
                pltpu.make_async_remote_copy(
                    src_ref=buf.at[0], dst_ref=recv_ref.at[4 * px + 2 * py + pc], send_sem=blk_send.at[0],
                    recv_sem=blk_recv.at[r - 1], device_id=(px, py, pc), device_id_type=_MESH).wait_recv()
            mine.wait()
            exchange.finish()
            last.finish()

    hbm = pl.BlockSpec(memory_space=pl.ANY)
    rows = pl.BlockSpec((rb, D), lambda s, order: (jnp.maximum(s - N_DEV, 0), 0))
    one = pl.BlockSpec((1, D), lambda s, order: (0, 0))
    res = pl.pallas_call(
        body, name="inproj_bwd_send",
        grid_spec=pltpu.PrefetchScalarGridSpec(
            num_scalar_prefetch=1, grid=(n_steps,),
            in_specs=[pl.BlockSpec((t_pad, WIN_B), lambda s, order: (0, order[jnp.minimum(s, N_DEV - 1)])),
                      pl.BlockSpec((rb, D_IN), lambda s, order: (jnp.maximum(s - N_DEV, 0), 0)),
                      pl.BlockSpec((t_pad, D), lambda s, order: (0, 0), pipeline_mode=pl.Buffered(1)),
                      pl.BlockSpec((4, D, WIN_P), lambda s, order: (0, 0, 0), pipeline_mode=pl.Buffered(1)),
                      rows, rows, one, one, one, pl.BlockSpec((8, 128), lambda s, order: (0, 0))] + [hbm] * na,
            out_specs=[rows] + [hbm] * (na + 2),
            scratch_shapes=[pltpu.VMEM((N_DEV, D, WIN_B), _BF), pltpu.VMEM((24, D), _F32),
                            pltpu.SemaphoreType.DMA((N_DEV - 1,)), pltpu.SemaphoreType.DMA((N_DEV - 1,)),
                            pltpu.SemaphoreType.DMA((1,))] + _sem_shapes(na) + _sem_shapes(1)),
        out_shape=[_S((t_pad, D), _F32), _S((N_DEV, D, WIN_B), _BF)]
        + [_S((N_DEV,) + g.shape, g.dtype) for g in to_all] + [_S((N_DEV, 24, D), _F32)],
        compiler_params=_cp(("arbitrary",)),
    )(order, dp, dp, u, w_in, h0, dh1, g_mix, gffn, gfin, loss, *to_all)
    return res


def _recv_shapes(scatter, windows):
    return [_S(s.shape if w is None else (s.shape[0], w[1]) + s.shape[2:], s.dtype) for s, w in zip(scatter, windows)]


def _wgrad(name, a, b, a_spec, b_spec, n_blocks, out_block, scatter=(), windows=None):
    nsc = len(scatter)
    windows = windows if windows is not None else [None] * nsc

    def body(a_ref, b_ref, *rest):
        o_ref = rest[nsc]
        j = pl.program_id(0)
        if nsc:
            exchange = _Exchange(rest[:nsc], [], rest[nsc + 1:2 * nsc + 1], rest[2 * nsc + 1:], windows)

            @pl.when(j == 0)
            def _():
                exchange.start()

        av = a_ref[0] if len(a_ref.shape) == 3 else a_ref[...]
        bv = b_ref[0] if len(b_ref.shape) == 3 else b_ref[...]
        o_ref[0] = _dot_tn(av, bv).astype(_BF)

        if nsc:
            @pl.when(j == n_blocks - 1)
            def _():
                exchange.finish()

    hbm = pl.BlockSpec(memory_space=pl.ANY)
    res = pl.pallas_call(
        body, name=name, grid=(n_blocks,),
        in_specs=[a_spec, b_spec] + [hbm] * nsc,
        out_specs=[pl.BlockSpec((1,) + out_block, lambda j: (j, 0, 0))] + [hbm] * nsc,
        out_shape=[_S((n_blocks,) + out_block, _BF)] + _recv_shapes(scatter, windows),
        scratch_shapes=_sem_shapes(nsc) if nsc else [],
        compiler_params=_cp(("arbitrary",)),
    )(a, b, *scatter)
    return res if nsc else res[0]


def _coords():
    return lax.axis_index("x"), lax.axis_index("y"), lax.axis_index("c")


def _sem_shapes(na):
    return [pltpu.SemaphoreType.DMA((7 * na,)), pltpu.SemaphoreType.DMA((7 * na,)), pltpu.SemaphoreType.DMA((na,))]


class _Gather:
    def __init__(self, srcs, outs, sems, place=None):
        self.srcs, self.outs = srcs, outs
        self.send_sems, self.recv_sems, self.local_sems = sems
        self.place = place if place is not None else (lambda ref, block: ref.at[block])
        self.na = len(srcs)
        x, y, c = _coords()
        self.pos = (x, y, c)
        self.me = 4 * x + 2 * y + c
        self.sibling = (x, y, 1 - c)
        self.chips = [(1 - x, y), (x, 1 - y), (1 - x, 1 - y)]

    @staticmethod
    def _slot(px, py, pc):
        return 4 * px + 2 * py + pc

    def _copy(self, a, k, block, to, own=False):
        dst = self.place(self.outs[a], block)
        return pltpu.make_async_remote_copy(
            src_ref=self.srcs[a] if own else dst, dst_ref=dst,
            send_sem=self.send_sems.at[7 * a + k], recv_sem=self.recv_sems.at[7 * a + k],
            device_id=to, device_id_type=_MESH)

    def _mine(self, a):
        return pltpu.make_async_copy(self.srcs[a], self.place(self.outs[a], self.me), self.local_sems.at[a])

    def _first(self):
        c = self.pos[2]
        cps = []
        for a in range(self.na):
            cps.append(self._copy(a, 0, self.me, self.sibling, own=True))
            cps += [self._copy(a, 1 + j, self.me, (*chip, c), own=True) for j, chip in enumerate(self.chips)]
        return cps

    def _passed(self):
        c = self.pos[2]
        return [self._copy(a, 4 + j, self._slot(*chip, c), self.sibling)
                for j, chip in enumerate(self.chips) for a in range(self.na)]

    def start(self):
        for a in range(self.na):
            self._mine(a).start()
        for cp in self._first():
            cp.start()

    def forward(self, j, arrays=None):
        c = self.pos[2]
        chip = self.chips[j]
        for a in (range(self.na) if arrays is None else arrays):
            self._copy(a, 1 + j, self._slot(*chip, c), self.pos).wait_recv()
            self._copy(a, 4 + j, self._slot(*chip, c), self.sibling).start()

    def wait_sibling(self):
        x, y, c = self.pos
        for a in range(self.na):
            self._copy(a, 0, self._slot(x, y, 1 - c), self.pos).wait_recv()

    def wait_passed(self, j):
        c = self.pos[2]
        for a in range(self.na):
            self._copy(a, 4 + j, self._slot(*self.chips[j], 1 - c), self.pos).wait_recv()

    def finish_sends(self):
        for cp in self._first() + self._passed():
            cp.wait_send()
        for a in range(self.na):
            self._mine(a).wait()

    def finish(self):
        self.wait_sibling()
        for j in range(3):
            self.wait_passed(j)
        self.finish_sends()


class _Exchange:
    def __init__(self, scatter, gather, outs, sems, windows=None):
        self.windows = windows if windows is not None else [None] * len(scatter)
        self.ins = list(scatter) + list(gather)
        self.ns, self.na = len(scatter), len(scatter) + len(gather)
        self.outs = outs
        self.send_sems, self.recv_sems, self.local_sems = sems
        x, y, c = _coords()
        self.pos = (x, y, c)
        self.me = 4 * x + 2 * y + c

    def _peer(self, r):
        x, y, c = self.pos
        return x ^ (r >> 2), y ^ ((r >> 1) & 1), c ^ (r & 1)

    def _src(self, a, block):
        if a >= self.ns:
            return self.ins[a]
        if self.windows[a] is None:
            return self.ins[a].at[block]
        row0, rows = self.windows[a]
        return self.ins[a].at[block, pl.ds(row0, rows)]

    def _local(self, a):
        return pltpu.make_async_copy(self._src(a, self.me), self.outs[a].at[self.me], self.local_sems.at[a])

    def _send(self, a, r):
        px, py, pc = self._peer(r)
        return pltpu.make_async_remote_copy(
            src_ref=self._src(a, 4 * px + 2 * py + pc), dst_ref=self.outs[a].at[self.me],
            send_sem=self.send_sems.at[7 * a + r - 1], recv_sem=self.recv_sems.at[7 * a + r - 1],
            device_id=(px, py, pc), device_id_type=_MESH)

    def _recv(self, a, r):
        px, py, pc = self._peer(r)
        return pltpu.make_async_remote_copy(
            src_ref=self._src(a, self.me), dst_ref=self.outs[a].at[4 * px + 2 * py + pc],
            send_sem=self.send_sems.at[7 * a + r - 1], recv_sem=self.recv_sems.at[7 * a + r - 1],
            device_id=(px, py, pc), device_id_type=_MESH)

    def start(self):
        for a in range(self.na):
            self._local(a).start()
        for r in range(1, N_DEV):
            for a in range(self.na):
                self._send(a, r).start()

    def finish(self):
        for r in range(1, N_DEV):
            for a in range(self.na):
                self._recv(a, r).wait_recv()
        for r in range(1, N_DEV):
            for a in range(self.na):
                self._send(a, r).wait_send()
        for a in range(self.na):
            self._local(a).wait()


def _prologue(x, tgt, small_l, w_in_l, cast_f32):
    seq = x.shape[0]
    nx = seq // TM
    rest_rows = seq - nx * TM
    nt = nx + 1
    nc = len(cast_f32)
    body_rows = TM - N_META
    assert nx >= 1 and rest_rows % 8 == 0 and rest_rows <= body_rows
    x_rest, t_rest = x[nx * TM:], tgt[nx * TM:]

    def last_tile_body(rest_ref):
        parts = ([rest_ref[...]] if rest_rows else []) + (
            [jnp.zeros((body_rows - rest_rows, D), _F32)] if body_rows > rest_rows else [])
        return parts[0] if len(parts) == 1 else jnp.concatenate(parts, axis=0)

    def body(xm_ref, xp_ref, tm_ref, tp_ref, *rest):
        if rest_rows:
            xr_ref, tr_ref, rest = rest[0], rest[1], rest[2:]
        else:
            xr_ref = tr_ref = None
        s_ref, w_ref, rest = rest[0], rest[1], rest[2:]
        cins = rest[:nc]
        h0_ref, tgt_ref, small_ref, wg_ref = rest[nc:nc + 4]
        couts = rest[nc + 4:2 * nc + 4]
        s_stage, w_stage, meta, msem = rest[2 * nc + 4:2 * nc + 8]
        g_s = _Gather([s_stage], [small_ref], rest[2 * nc + 8:2 * nc + 11])
        g_w = _Gather([w_stage], [wg_ref], rest[2 * nc + 11:], place=_pair_place)
        s = pl.program_id(0)
        i = (s + 1) % nt

        @pl.when(s == 0)
        def _():
            s_stage[...] = s_ref[...]
            w_stage[...] = w_ref[...].astype(_BF)
            g_s.start()
            g_w.start()
            meta[...] = jnp.zeros_like(meta)
            for a in range(nc):
                couts[a][...] = cins[a][...].astype(_BF)

        @pl.when(s == nt - 1)
        def _():
            for j in range(3):
                g_s.forward(j)
            g_s.finish()
            cps = [pltpu.make_async_copy(small_ref.at[k, pl.ds(0, N_META), :], meta.at[:, pl.ds(128 * k, 128)],
                                         msem.at[k]) for k in range(N_DEV)]
            for cp in cps:
                cp.start()
            for cp in cps:
                cp.wait()
            for j in range(3):
                g_w.forward(j)
            g_w.finish()

        has_x = i < nx
        h0_ref[pl.ds(0, N_META), :] = jnp.where(i == 0, meta[...], xp_ref[...])
        h0_ref[pl.ds(N_META, body_rows), :] = jnp.where(has_x, xm_ref[pl.ds(0, body_rows), :], last_tile_body(xr_ref))
        tgt_ref[pl.ds(0, N_META), :] = jnp.where(i == 0, 0.0, tp_ref[...])
        tgt_ref[pl.ds(N_META, body_rows), :] = jnp.where(has_x, tm_ref[pl.ds(0, body_rows), :], last_tile_body(tr_ref))

    def tile_of(s):
        return (s + 1) % nt

    hbm = pl.BlockSpec(memory_space=pl.ANY)
    main = pl.BlockSpec((TM, D), lambda s: (jnp.minimum(tile_of(s), nx - 1), 0))
    prev = pl.BlockSpec((N_META, D), lambda s: (jnp.maximum(tile_of(s) * (TM // N_META) - 1, 0), 0))
    tile = pl.BlockSpec((TM, D), lambda s: (tile_of(s), 0))
    rests = [x_rest, t_rest] if rest_rows else []
    return pl.pallas_call(
        body, name="prologue", grid=(nt,),
        in_specs=[main, prev, main, prev] + [_const(r.shape) for r in rests]
        + [_const(small_l.shape), _const(w_in_l.shape)] + [_const(l.shape) for l in cast_f32],
        out_specs=[tile, tile, hbm, hbm] + [_full(l.shape) for l in cast_f32],
        out_shape=[_S((nt * TM, D), _F32), _S((nt * TM, D), _F32), _S((N_DEV,) + small_l.shape, _F32),
                   _S((4, D, WIN_P), _BF)] + [_S(l.shape, _BF) for l in cast_f32],
        scratch_shapes=[pltpu.VMEM(small_l.shape, _F32), pltpu.VMEM(w_in_l.shape, _BF), pltpu.VMEM((N_META, D), _F32),
                        pltpu.SemaphoreType.DMA((N_DEV,))] + _sem_shapes(1) + _sem_shapes(1),
        compiler_params=_cp(("arbitrary",)),
    )(x, x, tgt, tgt, *rests, small_l, w_in_l, *cast_f32)


def _adamw_math(w, g, m, v):
    m2 = ADAM_B1 * m + (1.0 - ADAM_B1) * g
    v2 = ADAM_B2 * v + (1.0 - ADAM_B2) * (g * g)
    m_hat = m2 / (1.0 - ADAM_B1 ** ADAM_STEP)
    v_hat = v2 / (1.0 - ADAM_B2 ** ADAM_STEP)
    delta = -ADAM_LR * (m_hat / (jnp.sqrt(v_hat) + ADAM_EPS) + ADAM_WD * w)
    return delta, m2, v2


def _adamw_big(name, recv, w, m, v, rows):
    r_all, c_all = w.shape

    def body(r_ref, w_ref, m_ref, v_ref, g_out, d_out, m_out, v_out):
        g = r_ref[0].astype(_F32)
        for k in range(1, N_DEV):
            g = g + r_ref[k].astype(_F32)
        delta, m2, v2 = _adamw_math(w_ref[...], g, m_ref[...], v_ref[...])
        g_out[...] = g
        d_out[...] = delta
        m_out[...] = m2
        v_out[...] = v2

    tile = pl.BlockSpec((rows, c_all), lambda i: (i, 0))
    return pl.pallas_call(
        body, name=name, grid=(r_all // rows,),
        in_specs=[pl.BlockSpec((N_DEV, rows, c_all), lambda i: (0, i, 0)), tile, tile, tile],
        out_specs=[tile] * 4,
        out_shape=[_S(w.shape, _F32)] * 4,
        compiler_params=_cp(("arbitrary",)),
    )(recv, w, m, v)


def _adamw_small(gathered, slices, wmv):
    ng, npar = len(gathered), len(slices)

    def body(*refs):
        g_refs = refs[:ng]
        wmv_refs = refs[ng:ng + 3 * npar]
        outs = refs[ng + 3 * npar:]
        for i, (ai, r0, nr, c0, ncol) in enumerate(slices):
            g = g_refs[ai][0, pl.ds(r0, nr), pl.ds(c0, ncol)].astype(_F32)
            for k in range(1, N_DEV):
                g = g + g_refs[ai][k, pl.ds(r0, nr), pl.ds(c0, ncol)].astype(_F32)
            w_ref, m_ref, v_ref = wmv_refs[3 * i:3 * i + 3]
            delta, m2, v2 = _adamw_math(w_ref[...], g, m_ref[...], v_ref[...])
            outs[4 * i][...] = g
            outs[4 * i + 1][...] = delta
            outs[4 * i + 2][...] = m2
            outs[4 * i + 3][...] = v2
        total = g_refs[0][0, pl.ds(R_LOSS, 1), pl.ds(0, 128)]
        for k in range(1, N_DEV):
            total = total + g_refs[0][k, pl.ds(R_LOSS, 1), pl.ds(0, 128)]
        outs[4 * npar][...] = total

    flat = [t for trip in wmv for t in trip]
    out_shape = []
    for w, _, _ in wmv:
        out_shape += [_S(w.shape, _F32)] * 4
    out_shape.append(_S((1, 128), _F32))
    return pl.pallas_call(
        body, name="adamw_small", out_shape=out_shape,
        compiler_params=pltpu.CompilerParams(vmem_limit_bytes=VMEM_LIMIT),
    )(*gathered, *flat)


def _block_diag(w):
    eye = jnp.eye(8, dtype=w.dtype)
    return (w[:, :, None, :] * eye[:, None, :, None]).reshape(D_RG, D_RG)


def _diag_blocks(g):
    return jnp.concatenate([g[64 * h:64 * (h + 1), 64 * h:64 * (h + 1)] for h in range(8)], axis=0)


def _local_step(h0, tgt_p, n_valid, g_mix, w_in, vec, wr, wi, hb, g_hg, w_out_l, g_ffn, w_gu_l, w_down_l, g_fin):
    t_pad = h0.shape[0]
    me = 4 * lax.axis_index("x") + 2 * lax.axis_index("y") + lax.axis_index("c")
    p, u, y, hs, o, sc, w_out, w_gu, w_down = _mixer_fwd(h0, g_mix, w_in, wr, wi, vec, hb, g_hg,
                                                         [w_out_l, w_gu_l, w_down_l])
    w_out = w_out.reshape(D, D)
    w_down = w_down.reshape(4, FFB, D)
    h1, v, gu, act, dh2, dh2b, loss, gfin = _ffn_loss(h0, y, w_out, g_ffn, w_gu, w_down, g_fin, tgt_p, n_valid)

    g_wdown = _wgrad("wgrad_down", act, dh2b, pl.BlockSpec((1, t_pad, FFB), lambda j: (j, 0, 0)),
                     pl.BlockSpec((t_pad, D), lambda j: (0, 0)), 4, (FFB, D))
    g_wdown = g_wdown.reshape(N_DEV, D_FF // N_DEV, D)
    dgu, dh1, dh1b, dy, gffn, r_wdown = _ffn_bwd(dh2, dh2b, gu, h1, g_ffn, w_gu, w_down, w_out, [g_wdown])
    g_wgu = _wgrad("wgrad_gate_up", dgu, v, pl.BlockSpec((1, t_pad, FFB), lambda j: (j, 0, 0)),
                   pl.BlockSpec((t_pad, D), lambda j: (0, 0)), N_DEV, (FFB, D))
    g_wout = _wgrad("wgrad_out", y, dh1b, pl.BlockSpec((t_pad, D // N_DEV), lambda j: (0, j)),
                    pl.BlockSpec((t_pad, D), lambda j: (0, 0)), N_DEV, (D // N_DEV, D))
    dp, gvec, gw, r_wgu, r_wout = _mixer_bwd(p, hs, o, sc, dy, wr, wi, vec, hb, g_hg, [g_wgu, g_wout], [None, None])
    pack_c = jnp.concatenate([_diag_blocks(gw[0]), _diag_blocks(gw[1])], axis=1).astype(_BF)
    order = (me ^ jnp.array(_SEND_ORDER, jnp.int32)).astype(jnp.int32)
    dh0, r_win, all_b, all_c, all_a = _inproj_bwd_send(dp, w_in, h0, dh1, g_mix, u, order, gffn, gfin, loss,
                                                       [gvec, pack_c])
    return dh0, (r_win, r_wgu, r_wout, r_wdown), (all_a, all_b, all_c)


def kernel(x, meta_tokens, mix_norm_g, w_in, conv_w, conv_b, w_rgate, b_rgate, w_igate, b_igate, lru_lambda, rg_norm_g, hg_lower_bound, hg_norm_g, w_out, ffn_norm_g, w_gate_up, w_down, final_norm_g, loss_target, m_meta_tokens, m_mix_norm_g, m_w_in, m_conv_w, m_conv_b, m_w_rgate, m_b_rgate, m_w_igate, m_b_igate, m_lru_lambda, m_rg_norm_g, m_hg_lower_bound, m_hg_norm_g, m_w_out, m_ffn_norm_g, m_w_gate_up, m_w_down, m_final_norm_g, v_meta_tokens, v_mix_norm_g, v_w_in, v_conv_w, v_conv_b, v_w_rgate, v_b_rgate, v_w_igate, v_b_igate, v_lru_lambda, v_rg_norm_g, v_hg_lower_bound, v_hg_norm_g, v_w_out, v_ffn_norm_g, v_w_gate_up, v_w_down, v_final_norm_g):
    seq = x.shape[1]
    me = 4 * lax.axis_index("x") + 2 * lax.axis_index("y") + lax.axis_index("c")

    n_valid = N_META + seq
    small_l = jnp.concatenate([meta_tokens, jnp.pad(conv_w[0], ((0, 4), (0, 64)))], axis=0)
    h0, tgt_p, small_g, w_in_g, w_gu_l, w_out_l, w_down_l = _prologue(
        x[0], loss_target[0], small_l, w_in[0], [w_gate_up[0].T, w_out[0], w_down[0]])
    conv_w_full = jnp.transpose(small_g[:, N_META:N_META + 4, :64], (1, 0, 2)).reshape(4, D_RG)
    vec = jnp.concatenate([conv_b, b_rgate, b_igate, lru_lambda, rg_norm_g, jnp.zeros((3, D_RG), _F32),
                           conv_w_full, jnp.zeros((4, D_RG), _F32)], axis=0)
    wr = _block_diag(w_rgate[0]).astype(_BF)
    wi = _block_diag(w_igate[0]).astype(_BF)

    dh0, (r_win, r_wgu, r_wout, r_wdown), (all_a, all_b, all_c) = _local_step(
        h0, tgt_p, n_valid, mix_norm_g, w_in_g, vec, wr, wi, hg_lower_bound, hg_norm_g,
        w_out_l, ffn_norm_g, w_gu_l, w_down_l, final_norm_g.reshape(1, D))
    grad_x = dh0[N_META:N_META + seq][None]

    outs = {}
    outs["w_in"] = _adamw_big("adamw_w_in", r_win, w_in[0], m_w_in[0], v_w_in[0], 256)
    outs["w_gate_up"] = [r.T for r in _adamw_big("adamw_w_gate_up", r_wgu, w_gate_up[0].T, m_w_gate_up[0].T,
                                                 v_w_gate_up[0].T, 176)]
    outs["w_out"] = _adamw_big("adamw_w_out", r_wout, w_out[0], m_w_out[0], v_w_out[0], 128)
    outs["w_down"] = _adamw_big("adamw_w_down", r_wdown, w_down[0], m_w_down[0], v_w_down[0], 176)

    meta_part = lax.dynamic_slice_in_dim(all_a[:, R_META:R_META + N_META, :], me * 128, 128, axis=2)
    convw_part = lax.dynamic_slice_in_dim(all_b[:, R_CONVW:R_CONVW + 4, :], me * 64, 64, axis=2)
    gathered = [all_a, all_b, all_c, meta_part, convw_part]
    small_params = [
        ("meta_tokens", (3, 0, N_META, 0, 128), (meta_tokens, m_meta_tokens, v_meta_tokens), (N_META, 128)),
        ("mix_norm_g", (0, R_GMIX, 1, 0, D), (mix_norm_g, m_mix_norm_g, v_mix_norm_g), (1, D)),
        ("conv_w", (4, 0, 4, 0, 64), (conv_w, m_conv_w, v_conv_w), (4, 64)),
        ("conv_b", (1, R_CONVB, 1, 0, D_RG), (conv_b, m_conv_b, v_conv_b), (1, D_RG)),
        ("w_rgate", (2, 0, 512, 0, 64), (w_rgate, m_w_rgate, v_w_rgate), (512, 64)),
        ("b_rgate", (1, R_BR, 1, 0, D_RG), (b_rgate, m_b_rgate, v_b_rgate), (1, D_RG)),
        ("w_igate", (2, 0, 512, 64, 64), (w_igate, m_w_igate, v_w_igate), (512, 64)),
        ("b_igate", (1, R_BI, 1, 0, D_RG), (b_igate, m_b_igate, v_b_igate), (1, D_RG)),
        ("lru_lambda", (1, R_LAM, 1, 0, D_RG), (lru_lambda, m_lru_lambda, v_lru_lambda), (1, D_RG)),
        ("rg_norm_g", (1, R_GRG, 1, 0, D_RG), (rg_norm_g, m_rg_norm_g, v_rg_norm_g), (1, D_RG)),
        ("hg_lower_bound", (1, R_HB0, 2, 0, D_HG), (hg_lower_bound, m_hg_lower_bound, v_hg_lower_bound), (2, D_HG)),
        ("hg_norm_g", (1, R_GHG, 1, 0, HD), (hg_norm_g, m_hg_norm_g, v_hg_norm_g), (1, HD)),
        ("ffn_norm_g", (0, R_GFFN, 1, 0, D), (ffn_norm_g, m_ffn_norm_g, v_ffn_norm_g), (1, D)),
        ("final_norm_g", (0, R_GFIN, 1, 0, D), (final_norm_g, m_final_norm_g, v_final_norm_g), (1, D)),
    ]
    res = _adamw_small(gathered, [s[1] for s in small_params],
                       [tuple(t.reshape(s[3]) for t in s[2]) for s in small_params])
    for i, s in enumerate(small_params):
        outs[s[0]] = [r.reshape(s[2][0].shape) for r in res[4 * i:4 * i + 4]]
    for n, ref in (("w_in", w_in), ("w_gate_up", w_gate_up), ("w_out", w_out), ("w_down", w_down)):
        outs[n] = [r.reshape(ref.shape) for r in outs[n]]

    loss_all = res[4 * len(small_params)][0, 0]
    order = ["meta_tokens", "mix_norm_g", "w_in", "conv_w", "conv_b", "w_rgate", "b_rgate", "w_igate", "b_igate",
             "lru_lambda", "rg_norm_g", "hg_lower_bound", "hg_norm_g", "w_out", "ffn_norm_g", "w_gate_up", "w_down",
             "final_norm_g"]
    return (loss_all, grad_x, *[outs[n][0] for n in order], *[outs[n][1] for n in order],
            *[outs[n][2] for n in order], *[outs[n][3] for n in order])
```

```python
import functools

import jax
import jax.numpy as jnp
from jax import lax
from jax.experimental import pallas as pl
from jax.experimental.pallas import tpu as pltpu

_BF = jnp.bfloat16
_F32 = jnp.float32
_S = jax.ShapeDtypeStruct
_MESH = pl.DeviceIdType.MESH

N_DEV = 8
N_META = 16
D = 1024
D_RG = 512
D_HG = 512
HD = 128
NH = D_HG // HD
D_IN = 3072
D_FF = 2816
FFB = D_FF // 4
WIN_B = D_IN // N_DEV
WIN_P = 2 * WIN_B
EPS = 1e-6
LRU_C = 8.0
TM = 320
HC = 64
VMEM_LIMIT = 62 * 1024 * 1024

ADAM_LR = 0.001
ADAM_B1 = 0.9
ADAM_B2 = 0.999
ADAM_EPS = 1e-08
ADAM_WD = 0.01
ADAM_STEP = 10

_SEND_ORDER = (6, 4, 2, 7, 5, 3, 1, 0)

R_CONVB, R_BR, R_BI, R_LAM, R_GRG, R_HB0, R_HB1, R_GHG, R_CONVW = 0, 1, 2, 3, 4, 5, 6, 7, 8
R_GFFN, R_GFIN, R_LOSS, R_META = 1, 2, 3, 8


def _cp(sem=None, **kw):
    return pltpu.CompilerParams(dimension_semantics=sem, vmem_limit_bytes=VMEM_LIMIT, **kw)


def _dot(a, b):
    return jnp.dot(a, b, preferred_element_type=_F32)


def _dot_nt(a, b):
    return lax.dot_general(a, b, (((1,), (1,)), ((), ())), preferred_element_type=_F32)


def _dot_tn(a, b):
    return lax.dot_general(a, b, (((0,), (0,)), ((), ())), preferred_element_type=_F32)


def _sigmoid(x):
    return 0.5 * jnp.tanh(0.5 * x) + 0.5


def _dsilu(x, s):
    return s * (1.0 + x * (1.0 - s))


_GELU_C = 0.7978845608028654


def _gelu_parts(x):
    t = jnp.tanh(_GELU_C * (x + 0.044715 * (x * x * x)))
    g = 0.5 * x * (1.0 + t)
    dg = 0.5 * (1.0 + t) + 0.5 * x * (1.0 - t * t) * (_GELU_C * (1.0 + 3.0 * 0.044715 * (x * x)))
    return g, dg


def _softplus(z):
    e = jnp.exp(-jnp.abs(z))
    w = 1.0 + e
    l1p = jnp.where(w == 1.0, e, jnp.log(w) * e / jnp.where(w == 1.0, 1.0, w - 1.0))
    return jnp.maximum(z, 0.0) + l1p


def _rms_fwd(x):
    r = lax.rsqrt(jnp.mean(x * x, axis=-1, keepdims=True) + EPS)
    return x * r, r


def _rms_bwd(dyg, n, r):
    return r * (dyg - n * jnp.mean(dyg * n, axis=-1, keepdims=True))


def _full(shape):
    nd = len(shape)
    return pl.BlockSpec(shape, lambda i: (0,) * nd)


def _const(shape):
    nd = len(shape)
    return pl.BlockSpec(shape, lambda i: (0,) * nd, pipeline_mode=pl.Buffered(1))


def _carry_gather(gather, i, nt, early=0):
    @pl.when(i == 0)
    def _():
        gather.start()

    def tail():
        for j in range(3):
            if early:
                @pl.when(i == min(nt // 3 + j, nt - 1))
                def _(j=j):
                    gather.forward(j, range(early))

            @pl.when(i == max(nt - 4 + j, 0))
            def _(j=j):
                gather.forward(j, range(early, gather.na))

        @pl.when(i == nt - 1)
        def _():
            gather.finish()

    return tail


def _pair_place(ref, block):
    return ref.at[block // 2, :, pl.ds(pl.multiple_of((block % 2) * WIN_B, WIN_B), WIN_B)]


def _rg_gates(xc, wr_ref, wi_ref, vec_ref):
    xcb = xc.astype(_BF)
    r = _sigmoid(_dot(xcb, wr_ref[...]) + vec_ref[R_BR:R_BR + 1, :])
    ig = _sigmoid(_dot(xcb, wi_ref[...]) + vec_ref[R_BI:R_BI + 1, :])
    nsp8 = -LRU_C * _softplus(-vec_ref[R_LAM:R_LAM + 1, :])
    la = nsp8 * r
    a = jnp.exp(la)
    th = jnp.tanh(la)
    s = jnp.sqrt(-2.0 * th / (1.0 - th))
    return r, ig, a, s, nsp8


def _conv(xbuf, vec_ref):
    acc = vec_ref[R_CONVW:R_CONVW + 1, :] * xbuf[pl.ds(5, TM), :]
    for j in range(1, 4):
        acc = acc + vec_ref[R_CONVW + j:R_CONVW + j + 1, :] * xbuf[pl.ds(5 + j, TM), :]
    return vec_ref[R_CONVB:R_CONVB + 1, :] + acc


def _dot3(m01, x):
    hi = x.astype(_BF)
    r1 = x - hi.astype(_F32)
    mid = r1.astype(_BF)
    lo = (r1 - mid.astype(_F32)).astype(_BF)
    return (_dot(m01, lo) + _dot(m01, mid)) + _dot(m01, hi)


def _chunk_dot3(m01, x):
    return jnp.concatenate([_dot3(m01, x[HC * c:HC * (c + 1), :]) for c in range(x.shape[0] // HC)], axis=0)


def _chunk_masks():
    row = lax.broadcasted_iota(jnp.int32, (HC, HC), 0)
    col = lax.broadcasted_iota(jnp.int32, (HC, HC), 1)
    return (row >= col).astype(_BF), (col >= row).astype(_BF), jnp.ones((HC, HC), _BF)


def _per_chunk_rows(x, r):
    return jnp.concatenate([jnp.broadcast_to(x[HC * c + r:HC * c + r + 1, :], (HC, x.shape[1]))
                            for c in range(TM // HC)], axis=0)


def _hg_prep(p_ref, lb, tri):
    hq = p_ref[:, pl.ds(2 * D_RG, D_HG)]
    hf = p_ref[:, pl.ds(2 * D_RG + D_HG, D_HG)]
    sq = _sigmoid(hq)
    q = hq * sq
    sg = _sigmoid(hf)
    f = lb + (1.0 - lb) * sg
    k = 1.0 - f
    b = _chunk_dot3(tri, jnp.log(f))
    bm = _per_chunk_rows(b, HC // 2 - 1)
    bl = _per_chunk_rows(b, HC - 1)
    e_q = jnp.exp(b - bm)
    e_k = jnp.exp(bm - b)
    e_b = jnp.exp(b)
    e_l = jnp.exp(bl - b)
    return dict(hq=hq, sq=sq, q=q, sg=sg, f=f, k=k, e_q=e_q, e_k=e_k, e_b=e_b, e_l=e_l,
                qd=q * e_q, kd=k * e_k, qe=q * e_b, ke=k * e_l, e_end=jnp.exp(bl))


def _mixer_fwd(h0, g_mix, w_in, wr, wi, vec, hb, g_hg, shards):
    t_pad = h0.shape[0]
    nt = t_pad // TM
    nc_t = TM // HC
    nsh = len(shards)

    def body(h_ref, gmix_ref, win_ref, wr_ref, wi_ref, vec_ref, hb_ref, ghg_ref, *rest):
        sh_refs, rest = rest[:nsh], rest[nsh:]
        pout_ref, uout_ref, y_ref, hs_ref, o_ref, sc_ref = rest[:6]
        gath_refs, rest = rest[6:6 + nsh], rest[6 + nsh:]
        xbuf, a_s, b_s, hcar, st, qd_s, kd_s, qe_s, ke_s, v_s, u_s, p_s, p_ref = rest[:13]
        i = pl.program_id(0)
        tail = _carry_gather(_Gather(sh_refs, gath_refs, rest[13:]), i, nt + 1, early=1)

        @pl.when(i == 0)
        def _():
            p_s[...] = jnp.zeros_like(p_s)

        p_ref[...] = p_s[...]

        @pl.when(i <= 1)
        def _():
            xbuf[pl.ds(0, 8), :] = jnp.zeros((8, D_RG), _F32)
            hcar[...] = jnp.zeros_like(hcar)
            st[...] = jnp.zeros_like(st)

        n_h, _ = _rms_fwd(h_ref[...])
        u = (n_h * gmix_ref[...]).astype(_BF)
        uout_ref[...] = u
        pieces = [(j, k) for j in range(4) for k in range(WIN_P // 256)]

        def project(count):
            for _ in range(count):
                j, k = pieces.pop(0)
                blk = _dot(u, win_ref[j, :, pl.ds(256 * k, 256)])
                p_s[:, pl.ds(WIN_P * j + 256 * k, 256)] = blk
                pout_ref[:, pl.ds(WIN_P * j + 256 * k, 256)] = blk

        x = p_ref[:, pl.ds(0, D_RG)]
        xbuf[pl.ds(8, TM), :] = x
        xc = _conv(xbuf, vec_ref)
        xbuf[pl.ds(0, 8), :] = x[TM - 8:, :]
        r, ig, a, s, _ = _rg_gates(xc, wr_ref, wi_ref, vec_ref)
        a_s[...] = a
        b_s[...] = s * (ig * xc)

        def step(t, h):
            h = a_s[pl.ds(t, 1), :] * h + b_s[pl.ds(t, 1), :]
            hs_ref[pl.ds(t, 1), :] = h
            return h

        hcar[pl.ds(0, 1), :] = lax.fori_loop(0, TM, step, hcar[pl.ds(0, 1), :], unroll=8)
        gel, _ = _gelu_parts(p_ref[:, pl.ds(D_RG, D_RG)])
        n, _ = _rms_fwd(gel * hs_ref[...])
        y_ref[:, pl.ds(0, D_RG)] = (n * vec_ref[R_GRG:R_GRG + 1, :]).astype(_BF)

        lb = _sigmoid(hb_ref[0:1, :] - hb_ref[1:2, :])
        tri, _, _ = _chunk_masks()
        q = _hg_prep(p_ref, lb, tri)
        for name, ref in (("qd", qd_s), ("kd", kd_s), ("qe", qe_s), ("ke", ke_s)):
            ref[...] = q[name].astype(_BF)
        v_s[...] = p_ref[:, pl.ds(2 * D_RG + 2 * D_HG, D_HG)].astype(_BF)
        e_end = q["e_end"]
        causal = (lax.broadcasted_iota(jnp.int32, (HC, HC), 0) >= lax.broadcasted_iota(jnp.int32, (HC, HC), 1))
        for c in range(nc_t):
            for h in range(NH):
                rs, cs = pl.ds(HC * c, HC), pl.ds(HD * h, HD)
                amat = jnp.where(causal, _dot_nt(qd_s[rs, cs], kd_s[rs, cs]), 0.0)
                o_ref[rs, cs] = _dot(amat.astype(_BF), v_s[rs, cs])
                u_s[NH * c + h] = _dot_tn(v_s[rs, cs], ke_s[rs, cs])
                if pieces:
                    project(1)
        assert not pieces
        for h in range(NH):
            cs = pl.ds(HD * h, HD)
            s_run = st[h]
            for c in range(nc_t):
                rs = pl.ds(HC * c, HC)
                sc_ref[c, h] = s_run
                o_ref[rs, cs] += _dot_nt(qe_s[rs, cs], s_run.astype(_BF))
                s_run = e_end[HC * c:HC * c + 1, HD * h:HD * (h + 1)] * s_run + u_s[NH * c + h]
            st[h] = s_run
        for h in range(NH):
            cs = pl.ds(HD * h, HD)
            n_o, _ = _rms_fwd(o_ref[:, cs])
            hg = p_ref[:, pl.ds(2 * D_RG + 3 * D_HG + HD * h, HD)]
            y_ref[:, pl.ds(D_RG + HD * h, HD)] = ((n_o * ghg_ref[...]) * (hg * _sigmoid(hg))).astype(_BF)

        tail()

    hbm = pl.BlockSpec(memory_space=pl.ANY)

    def proj(i):
        return jnp.minimum(i, nt - 1)

    def mixed(i):
        return jnp.maximum(i - 1, 0)

    return pl.pallas_call(
        body, name="mixer_fwd", grid=(nt + 1,),
        in_specs=[pl.BlockSpec((TM, D), lambda i: (proj(i), 0)), _full((1, D)), _const((4, D, WIN_P)),
                  _full((D_RG, D_RG)), _full((D_RG, D_RG)),
                  _full((16, D_RG)), _full((2, D_HG)), _full((1, HD))] + [hbm] * nsh,
        out_specs=[pl.BlockSpec((TM, D_IN), lambda i: (proj(i), 0)), pl.BlockSpec((TM, D), lambda i: (proj(i), 0)),
                   pl.BlockSpec((TM, D), lambda i: (mixed(i), 0)), pl.BlockSpec((TM, D_RG), lambda i: (mixed(i), 0)),
                   pl.BlockSpec((TM, D_HG), lambda i: (mixed(i), 0)),
                   pl.BlockSpec((nc_t, NH, HD, HD), lambda i: (mixed(i), 0, 0, 0))] + [hbm] * nsh,
        out_shape=[_S((t_pad, D_IN), _F32), _S((t_pad, D), _BF),
                   _S((t_pad, D), _BF), _S((t_pad, D_RG), _F32), _S((t_pad, D_HG), _F32),
                   _S((t_pad // HC, NH, HD, HD), _F32)] + [_S((N_DEV,) + s.shape, s.dtype) for s in shards],
        scratch_shapes=[pltpu.VMEM((TM + 8, D_RG), _F32), pltpu.VMEM((TM, D_RG), _F32),
                        pltpu.VMEM((TM, D_RG), _F32), pltpu.VMEM((8, D_RG), _F32),
                        pltpu.VMEM((NH, HD, HD), _F32)] + [pltpu.VMEM((TM, D_HG), _BF) for _ in range(5)]
        + [pltpu.VMEM((nc_t * NH, HD, HD), _F32), pltpu.VMEM((TM, D_IN), _F32), pltpu.VMEM((TM, D_IN), _F32)]
        + _sem_shapes(nsh),
        compiler_params=_cp(("arbitrary",)),
    )(h0, g_mix, w_in, wr, wi, vec, hb, g_hg, *shards)


def _ffn_loss(h0, y, w_out, g_ffn, w_gu, w_down, g_fin, tgt, n_valid):
    t_pad = h0.shape[0]

    def body(h_ref, y_ref, wo_ref, gffn_ref, wgu_ref, wd_ref, g_ref, t_ref,
             h1_ref, v_ref, gu_ref, act_ref, dh2_ref, dh2b_ref, loss_ref, gfin_ref):
        i = pl.program_id(0)

        @pl.when(i == 0)
        def _():
            loss_ref[...] = jnp.zeros_like(loss_ref)
            gfin_ref[...] = jnp.zeros_like(gfin_ref)

        h1 = h_ref[...] + _dot(y_ref[...], wo_ref[...])
        h1_ref[...] = h1
        n1, _ = _rms_fwd(h1)
        vb = (n1 * gffn_ref[...]).astype(_BF)
        v_ref[...] = vb
        h2 = h1
        for b in range(4):
            gate = _dot_nt(vb, wgu_ref[b])
            up = _dot_nt(vb, wgu_ref[4 + b])
            gu_ref[b] = gate
            gu_ref[4 + b] = up
            act = ((gate * _sigmoid(gate)) * up).astype(_BF)
            act_ref[b] = act
            h2 = h2 + _dot(act, wd_ref[b])
        n, r = _rms_fwd(h2)
        out = n * g_ref[...]
        row = i * TM + lax.broadcasted_iota(jnp.int32, (TM, 1), 0)
        valid = (row >= N_META) & (row < n_valid)
        err = jnp.where(valid, out - t_ref[...], 0.0)
        loss_ref[...] += (0.5 / D) * jnp.sum(err * err)
        dout = err * (1.0 / D)
        gfin_ref[...] += jnp.sum(dout * n, axis=0, keepdims=True)
        dh2 = _rms_bwd(dout * g_ref[...], n, r)
        dh2_ref[...] = dh2
        dh2b_ref[...] = dh2.astype(_BF)

    tile = pl.BlockSpec((TM, D), lambda i: (i, 0))
    return pl.pallas_call(
        body, name="ffn_loss", grid=(t_pad // TM,),
        in_specs=[tile, tile, _const((D, D)), _full((1, D)),
                  _const((N_DEV, FFB, D)), _const((4, FFB, D)), _full((1, D)), tile],
        out_specs=[tile, tile,
                   pl.BlockSpec((N_DEV, TM, FFB), lambda i: (0, i, 0)), pl.BlockSpec((4, TM, FFB), lambda i: (0, i, 0)),
                   tile, tile, _full((8, 128)), _full((1, D))],
        out_shape=[_S((t_pad, D), _F32), _S((t_pad, D), _BF),
                   _S((N_DEV, t_pad, FFB), _F32), _S((4, t_pad, FFB), _BF), _S((t_pad, D), _F32),
                   _S((t_pad, D), _BF), _S((8, 128), _F32), _S((1, D), _F32)],
        compiler_params=_cp(("arbitrary",)),
    )(h0, y, w_out, g_ffn, w_gu, w_down, g_fin, tgt)


def _ffn_bwd(dh2, dh2b, gu, h1, g_ffn, w_gu, w_down, w_out, scatter):
    t_pad = dh2.shape[0]
    nsc = len(scatter)
    nt = t_pad // TM

    def body(dh2_ref, dh2b_ref, gu_ref, h1_ref, g_ref, wgu_ref, wd_ref, wo_ref, *rest):
        dgu_ref, dh1_ref, dh1b_ref, dy_ref, gffn_ref = rest[nsc:nsc + 5]
        exchange = _Exchange(rest[:nsc], [], rest[nsc + 5:2 * nsc + 5], rest[2 * nsc + 5:])
        i = pl.program_id(0)

        @pl.when(i == 0)
        def _():
            exchange.start()
            gffn_ref[...] = jnp.zeros_like(gffn_ref)

        db = dh2b_ref[...]
        dv = jnp.zeros((TM, D), _F32)
        for b in range(4):
            dact = _dot_nt(db, wd_ref[b])
            gate = gu_ref[b]
            up = gu_ref[4 + b]
            sg = _sigmoid(gate)
            dgate = ((dact * up) * _dsilu(gate, sg)).astype(_BF)
            dup = (dact * (gate * sg)).astype(_BF)
            dgu_ref[b] = dgate
            dgu_ref[4 + b] = dup
            dv = dv + _dot(dgate, wgu_ref[b]) + _dot(dup, wgu_ref[4 + b])
        n, r = _rms_fwd(h1_ref[...])
        gffn_ref[...] += jnp.sum(dv * n, axis=0, keepdims=True)
        dh1 = dh2_ref[...] + _rms_bwd(dv * g_ref[...], n, r)
        dh1_ref[...] = dh1
        dh1b = dh1.astype(_BF)
        dh1b_ref[...] = dh1b
        dy_ref[...] = _dot_nt(dh1b, wo_ref[...])

        @pl.when(i == nt - 1)
        def _():
            exchange.finish()

    tile = pl.BlockSpec((TM, D), lambda i: (i, 0))
    hbm = pl.BlockSpec(memory_space=pl.ANY)
    return pl.pallas_call(
        body, name="ffn_bwd", grid=(nt,),
        in_specs=[tile, tile, pl.BlockSpec((N_DEV, TM, FFB), lambda i: (0, i, 0)), tile, _full((1, D)),
                  _const((N_DEV, FFB, D)), _const((4, FFB, D)), _const((D, D))] + [hbm] * nsc,
        out_specs=[pl.BlockSpec((N_DEV, TM, FFB), lambda i: (0, i, 0)), tile, tile, tile, _full((1, D))] + [hbm] * nsc,
        out_shape=[_S((N_DEV, t_pad, FFB), _BF), _S((t_pad, D), _F32), _S((t_pad, D), _BF),
                   _S((t_pad, D), _F32), _S((1, D), _F32)] + _recv_shapes(scatter, [None] * nsc),
        scratch_shapes=_sem_shapes(nsc),
        compiler_params=_cp(("arbitrary",)),
    )(dh2, dh2b, gu, h1, g_ffn, w_gu, w_down, w_out, *scatter)


def _mixer_bwd(p, hs, o, sc, dy, wr, wi, vec, hb, g_hg, scatter, windows):
    t_pad = p.shape[0]
    nt = t_pad // TM
    nc_t = TM // HC
    nsc = len(scatter)

    def rev(i):
        return nt - 1 - i

    def body(p_ref, pprev_ref, hs_ref, hprev_ref, o_ref, sc_ref, dy_ref, wr_ref, wi_ref, vec_ref, hb_ref, ghg_ref,
             *rest):
        send_refs, rest = rest[:nsc], rest[nsc:]
        dp_ref, gvec_ref, gw_ref = rest[:3]
        recv_refs, rest = rest[3:3 + nsc], rest[3 + nsc:]
        xbuf, hbuf, dbuf, a_s, g_s, ccar, dst = rest[:7]
        qd_s, kd_s, qe_s, ke_s, v_s, do_s, dqd_s, dkd_s, dqe_s, dke_s, dv_s, w_s, dend_s = rest[7:20]
        exchange = _Exchange(send_refs, [], recv_refs, rest[20:], windows)
        i = pl.program_id(0)
        first_tile = i == nt - 1

        @pl.when(i == 0)
        def _():
            exchange.start()
            gvec_ref[...] = jnp.zeros_like(gvec_ref)
            gw_ref[...] = jnp.zeros_like(gw_ref)
            dbuf[pl.ds(TM, 8), :] = jnp.zeros((8, D_RG), _F32)
            ccar[...] = jnp.zeros_like(ccar)
            dst[...] = jnp.zeros_like(dst)

        def acc(row, val):
            gvec_ref[row:row + 1, :] += jnp.sum(val, axis=0, keepdims=True)

        keep = jnp.where(first_tile, 0.0, 1.0)
        x = p_ref[:, pl.ds(0, D_RG)]
        xbuf[pl.ds(0, 8), :] = pprev_ref[...] * keep
        xbuf[pl.ds(8, TM), :] = x
        xc = _conv(xbuf, vec_ref)
        r, ig, a, s, nsp8 = _rg_gates(xc, wr_ref, wi_ref, vec_ref)
        h = hs_ref[...]
        hbuf[pl.ds(0, 8), :] = hprev_ref[...] * keep
        hbuf[pl.ds(8, TM), :] = h
        hm1 = hbuf[pl.ds(7, TM), :]
        gr = p_ref[:, pl.ds(D_RG, D_RG)]
        gel, dgel = _gelu_parts(gr)
        n, rr = _rms_fwd(gel * h)
        dyn = dy_ref[:, pl.ds(0, D_RG)]
        acc(R_GRG, dyn * n)
        dpre = _rms_bwd(dyn * vec_ref[R_GRG:R_GRG + 1, :], n, rr)
        dp_ref[:, pl.ds(D_RG, D_RG)] = ((dpre * h) * dgel).astype(_BF)
        a_s[...] = a
        g_s[...] = dpre * gel

        def step(k, c):
            t = TM - 1 - k
            g = g_s[pl.ds(t, 1), :] + c
            g_s[pl.ds(t, 1), :] = g
            return a_s[pl.ds(t, 1), :] * g

        ccar[pl.ds(0, 1), :] = lax.fori_loop(0, TM, step, ccar[pl.ds(0, 1), :], unroll=8)
        gt = g_s[...]
        da = gt * hm1
        ixc = ig * xc
        ds = gt * ixc
        dig = (gt * s) * xc
        dxc = (gt * s) * ig
        dla = da * a - ds * ((a * a) / s)
        lam = vec_ref[R_LAM:R_LAM + 1, :]
        gvec_ref[R_LAM:R_LAM + 1, :] += jnp.sum(dla * r, axis=0, keepdims=True) * (LRU_C * _sigmoid(-lam))
        dzr = (dla * nsp8) * (r * (1.0 - r))
        dzi = dig * (ig * (1.0 - ig))
        acc(R_BR, dzr)
        acc(R_BI, dzi)
        xcb = xc.astype(_BF)
        dzrb = dzr.astype(_BF)
        dzib = dzi.astype(_BF)
        gw_ref[0] += _dot_tn(xcb, dzrb)
        gw_ref[1] += _dot_tn(xcb, dzib)
        dxc = dxc + _dot_nt(dzrb, wr_ref[...]) + _dot_nt(dzib, wi_ref[...])
        acc(R_CONVB, dxc)
        for j in range(4):
            acc(R_CONVW + j, dxc * xbuf[pl.ds(5 + j, TM), :])
        dbuf[pl.ds(0, TM), :] = dxc
        dx = vec_ref[R_CONVW + 3:R_CONVW + 4, :] * dxc
        for j in range(3):
            dx = dx + vec_ref[R_CONVW + j:R_CONVW + j + 1, :] * dbuf[pl.ds(3 - j, TM), :]
        dbuf[pl.ds(TM, 8), :] = dxc[0:8, :]
        dp_ref[:, pl.ds(0, D_RG)] = dx.astype(_BF)

        lb = _sigmoid(hb_ref[0:1, :] - hb_ref[1:2, :])
        tri, tri_rev, ones = _chunk_masks()
        q = _hg_prep(p_ref, lb, tri)
        qdb, kdb = q["qd"].astype(_BF), q["kd"].astype(_BF)
        qd_s[...] = qdb
        kd_s[...] = kdb
        qe_s[...] = q["qe"].astype(_BF)
        ke_s[...] = q["ke"].astype(_BF)
        v_s[...] = p_ref[:, pl.ds(2 * D_RG + 2 * D_HG, D_HG)].astype(_BF)
        e_end = q["e_end"]
        ghg = ghg_ref[...]
        for h in range(NH):
            cs = pl.ds(HD * h, HD)
            hg = p_ref[:, pl.ds(2 * D_RG + 3 * D_HG + HD * h, HD)]
            sh = _sigmoid(hg)
            n_o, r_o = _rms_fwd(o_ref[:, cs])
            dyh = dy_ref[:, pl.ds(D_RG + HD * h, HD)]
            dp_ref[:, pl.ds(2 * D_RG + 3 * D_HG + HD * h, HD)] = ((dyh * (n_o * ghg)) * _dsilu(hg, sh)).astype(_BF)
            dn = dyh * (hg * sh)
            gvec_ref[R_GHG:R_GHG + 1, pl.ds(0, HD)] += jnp.sum(dn * n_o, axis=0, keepdims=True)
            do_s[:, cs] = _rms_bwd(dn * ghg, n_o, r_o).astype(_BF)
        causal = (lax.broadcasted_iota(jnp.int32, (HC, HC), 0) >= lax.broadcasted_iota(jnp.int32, (HC, HC), 1))
        for c in range(nc_t):
            for h in range(NH):
                rs, cs = pl.ds(HC * c, HC), pl.ds(HD * h, HD)
                qd_c, kd_c, do_c = qd_s[rs, cs], kd_s[rs, cs], do_s[rs, cs]
                amat = jnp.where(causal, _dot_nt(qd_c, kd_c), 0.0).astype(_BF)
                da_m = jnp.where(causal, _dot_nt(do_c, v_s[rs, cs]), 0.0).astype(_BF)
                dqd_s[rs, cs] = _dot(da_m, kd_c)
                dkd_s[rs, cs] = _dot_tn(da_m, qd_c)
                dqe_s[rs, cs] = _dot(do_c, sc_ref[c, h].astype(_BF))
                dv_s[rs, cs] = _dot_tn(amat, do_c)
                w_s[NH * c + h] = _dot_tn(do_c, qe_s[rs, cs])
        for h in range(NH):
            cs = pl.ds(HD * h, HD)
            d_run = dst[h]
            for c in reversed(range(nc_t)):
                rs = pl.ds(HC * c, HC)
                d_b = d_run.astype(_BF)
                dke_s[rs, cs] = _dot(v_s[rs, cs], d_b)
                dp_ref[rs, pl.ds(2 * D_RG + 2 * D_HG + HD * h, HD)] = (
                    dv_s[rs, cs] + _dot_nt(ke_s[rs, cs], d_b)).astype(_BF)
                dend_s[pl.ds(c, 1), cs] = jnp.sum(sc_ref[c, h] * d_run, axis=0, keepdims=True)
                d_run = w_s[NH * c + h] + e_end[HC * c:HC * c + 1, HD * h:HD * (h + 1)] * d_run
            dst[h] = d_run
        dqd, dkd, dqe, dke = dqd_s[...], dkd_s[...], dqe_s[...], dke_s[...]
        dq = dqd * q["e_q"] + dqe * q["e_b"]
        dk = dkd * q["e_k"] + dke * q["e_l"]
        dkeke = dke * q["ke"]
        db = dqd * qdb.astype(_F32) - dkd * kdb.astype(_F32) + dqe * q["qe"] - dkeke
        d_end = jnp.concatenate([jnp.broadcast_to(dend_s[pl.ds(c, 1), :], (HC, D_HG)) for c in range(nc_t)], axis=0)
        dlf = _chunk_dot3(tri_rev, db) + _chunk_dot3(ones, dkeke) + d_end * e_end
        df = dlf / q["f"] - dk
        sg = q["sg"]
        gvec_ref[R_HB0:R_HB0 + 1, :] += jnp.sum(df * (1.0 - sg), axis=0, keepdims=True)
        dp_ref[:, pl.ds(2 * D_RG, D_HG)] = (dq * _dsilu(q["hq"], q["sq"])).astype(_BF)
        dp_ref[:, pl.ds(2 * D_RG + D_HG, D_HG)] = ((df * (1.0 - lb)) * (sg * (1.0 - sg))).astype(_BF)

        @pl.when(i == nt - 1)
        def _():
            glb = gvec_ref[R_HB0:R_HB0 + 1, :] * (lb * (1.0 - lb))
            gvec_ref[R_HB0:R_HB0 + 1, :] = glb
            gvec_ref[R_HB1:R_HB1 + 1, :] = -glb
            exchange.finish()

    hbm = pl.BlockSpec(memory_space=pl.ANY)
    return pl.pallas_call(
        body, name="mixer_bwd", grid=(nt,),
        in_specs=[pl.BlockSpec((TM, D_IN), lambda i: (rev(i), 0)),
                  pl.BlockSpec((8, D_RG), lambda i: (jnp.maximum(rev(i) * (TM // 8) - 1, 0), 0)),
                  pl.BlockSpec((TM, D_RG), lambda i: (rev(i), 0)),
                  pl.BlockSpec((8, D_RG), lambda i: (jnp.maximum(rev(i) * (TM // 8) - 1, 0), 0)),
                  pl.BlockSpec((TM, D_HG), lambda i: (rev(i), 0)),
                  pl.BlockSpec((nc_t, NH, HD, HD), lambda i: (rev(i), 0, 0, 0)),
                  pl.BlockSpec((TM, D), lambda i: (rev(i), 0)),
                  _full((D_RG, D_RG)), _full((D_RG, D_RG)), _full((16, D_RG)), _full((2, D_HG)), _full((1, HD))]
        + [hbm] * nsc,
        out_specs=[pl.BlockSpec((TM, D_IN), lambda i: (rev(i), 0)), _full((16, D_RG)), _full((2, D_RG, D_RG))]
        + [hbm] * nsc,
        out_shape=[_S((t_pad, D_IN), _BF), _S((16, D_RG), _F32), _S((2, D_RG, D_RG), _F32)]
        + _recv_shapes(scatter, windows),
        scratch_shapes=[pltpu.VMEM((TM + 8, D_RG), _F32), pltpu.VMEM((TM + 8, D_RG), _F32),
                        pltpu.VMEM((TM + 8, D_RG), _F32), pltpu.VMEM((TM, D_RG), _F32),
                        pltpu.VMEM((TM, D_RG), _F32), pltpu.VMEM((8, D_RG), _F32),
                        pltpu.VMEM((NH, HD, HD), _F32)]
        + [pltpu.VMEM((TM, D_HG), _BF) for _ in range(6)] + [pltpu.VMEM((TM, D_HG), _F32) for _ in range(5)]
        + [pltpu.VMEM((nc_t * NH, HD, HD), _F32), pltpu.VMEM((8, D_HG), _F32)] + _sem_shapes(nsc),
        compiler_params=_cp(("arbitrary",)),
    )(p, p, hs, hs, o, sc, dy, wr, wi, vec, hb, g_hg, *scatter)


def _inproj_bwd_send(dp, w_in, h0, dh1, g_mix, u, order, gffn, gfin, loss, to_all):
    t_pad = dp.shape[0]
    rb = TM
    n_steps = N_DEV + t_pad // rb
    na = len(to_all)

    def body(order_ref, dpc_ref, dpr_ref, u_ref, w_ref, h_ref, dh1_ref, g_ref, gffn_ref, gfin_ref, loss_ref, *rest):
        all_in = rest[:na]
        dh0_ref, recv_ref = rest[na:na + 2]
        all_out = rest[na + 2:2 * na + 2]
        alla_ref, allg_ref = rest[2 * na + 2:2 * na + 4]
        buf, pack, gmix, blk_send, blk_recv, blk_local = rest[2 * na + 4:2 * na + 10]
        exchange = _Exchange([], all_in, all_out, rest[2 * na + 10:2 * na + 13])
        early = _Exchange([], [pack], [alla_ref], rest[2 * na + 13:2 * na + 16])
        last = _Exchange([], [gmix], [allg_ref], rest[2 * na + 16:])
        s = pl.program_id(0)
        x, y, c = _coords()
        me = 4 * x + 2 * y + c

        def send(step):
            r = _SEND_ORDER[step]
            return pltpu.make_async_remote_copy(
                src_ref=buf.at[step], dst_ref=recv_ref.at[me], send_sem=blk_send.at[step], recv_sem=blk_recv.at[r - 1],
                device_id=(x ^ (r >> 2), y ^ ((r >> 1) & 1), c ^ (r & 1)), device_id_type=_MESH)

        @pl.when(s == 0)
        def _():
            exchange.start()
            pack[...] = jnp.zeros_like(pack)
            gmix[...] = jnp.zeros_like(gmix)

        @pl.when(s < N_DEV)
        def _():
            buf[s] = _dot_tn(u_ref[...], dpc_ref[...]).astype(_BF)

            for step in range(N_DEV - 1):
                @pl.when(s == step)
                def _(step=step):
                    send(step).start()

        @pl.when(s >= N_DEV)
        def _():
            du = jnp.zeros((rb, D), _F32)
            for j in range(4):
                du = du + _dot_nt(dpr_ref[:, WIN_P * j:WIN_P * (j + 1)], w_ref[j])
            n, r = _rms_fwd(h_ref[...])
            gmix[0:1, :] += jnp.sum(du * n, axis=0, keepdims=True)
            dh0 = dh1_ref[...] + _rms_bwd(du * g_ref[...], n, r)
            dh0_ref[...] = dh0

            @pl.when(s == N_DEV)
            def _():
                pack[R_META:R_META + N_META, :] = dh0[0:N_META, :]
                pack[R_GFFN:R_GFFN + 1, :] = gffn_ref[...]
                pack[R_GFIN:R_GFIN + 1, :] = gfin_ref[...]
                pack[R_LOSS:R_LOSS + 1, pl.ds(0, 128)] = loss_ref[0:1, :]
                early.start()

        @pl.when(s == n_steps - 1)
        def _():
            last.start()
            mine = pltpu.make_async_copy(buf.at[N_DEV - 1], recv_ref.at[me], blk_local.at[0])
            mine.start()
            for step in range(N_DEV - 1):
                send(step).wait_send()
            for r in range(1, N_DEV):
                px, py, pc = x ^ (r >> 2), y ^ ((r >> 1) & 1), c ^ (r & 1)
                pltpu.make_async_remote_copy(
                    src_ref=buf.at[0], dst_ref=recv_ref.at[4 * px + 2 * py + pc], send_sem=blk_send.at[0],
                    recv_sem=blk_recv.at[r - 1], device_id=(px, py, pc), device_id_type=_MESH).wait_recv()
            mine.wait()
            exchange.finish()
            early.finish()
            last.finish()

    hbm = pl.BlockSpec(memory_space=pl.ANY)
    rows = pl.BlockSpec((rb, D), lambda s, order: (jnp.maximum(s - N_DEV, 0), 0))
    one = pl.BlockSpec((1, D), lambda s, order: (0, 0))
    res = pl.pallas_call(
        body, name="inproj_bwd_send",
        grid_spec=pltpu.PrefetchScalarGridSpec(
            num_scalar_prefetch=1, grid=(n_steps,),
            in_specs=[pl.BlockSpec((t_pad, WIN_B), lambda s, order: (0, order[jnp.minimum(s, N_DEV - 1)])),
                      pl.BlockSpec((rb, D_IN), lambda s, order: (jnp.maximum(s - N_DEV, 0), 0)),
                      pl.BlockSpec((t_pad, D), lambda s, order: (0, 0), pipeline_mode=pl.Buffered(1)),
                      pl.BlockSpec((4, D, WIN_P), lambda s, order: (0, 0, 0), pipeline_mode=pl.Buffered(1)),
                      rows, rows, one, one, one, pl.BlockSpec((8, 128), lambda s, order: (0, 0))] + [hbm] * na,
            out_specs=[rows] + [hbm] * (na + 3),
            scratch_shapes=[pltpu.VMEM((N_DEV, D, WIN_B), _BF), pltpu.VMEM((24, D), _F32), pltpu.VMEM((8, D), _F32),
                            pltpu.SemaphoreType.DMA((N_DEV - 1,)), pltpu.SemaphoreType.DMA((N_DEV - 1,)),
                            pltpu.SemaphoreType.DMA((1,))] + _sem_shapes(na) + _sem_shapes(1) + _sem_shapes(1)),
        out_shape=[_S((t_pad, D), _F32), _S((N_DEV, D, WIN_B), _BF)]
        + [_S((N_DEV,) + g.shape, g.dtype) for g in to_all] + [_S((N_DEV, 24, D), _F32), _S((N_DEV, 8, D), _F32)],
        compiler_params=_cp(("arbitrary",)),
    )(order, dp, dp, u, w_in, h0, dh1, g_mix, gffn, gfin, loss, *to_all)
    return res


def _recv_shapes(scatter, windows):
    return [_S(s.shape if w is None else (s.shape[0], w[1]) + s.shape[2:], s.dtype) for s, w in zip(scatter, windows)]


def _wgrad(name, a, b, a_spec, b_spec, n_blocks, out_block, scatter=(), windows=None):
    nsc = len(scatter)
    windows = windows if windows is not None else [None] * nsc

    def body(a_ref, b_ref, *rest):
        o_ref = rest[nsc]
        j = pl.program_id(0)
        if nsc:
            exchange = _Exchange(rest[:nsc], [], rest[nsc + 1:2 * nsc + 1], rest[2 * nsc + 1:], windows)

            @pl.when(j == 0)
            def _():
                exchange.start()

        av = a_ref[0] if len(a_ref.shape) == 3 else a_ref[...]
        bv = b_ref[0] if len(b_ref.shape) == 3 else b_ref[...]
        o_ref[0] = _dot_tn(av, bv).astype(_BF)

        if nsc:
            @pl.when(j == n_blocks - 1)
            def _():
                exchange.finish()

    hbm = pl.BlockSpec(memory_space=pl.ANY)
    res = pl.pallas_call(
        body, name=name, grid=(n_blocks,),
        in_specs=[a_spec, b_spec] + [hbm] * nsc,
        out_specs=[pl.BlockSpec((1,) + out_block, lambda j: (j, 0, 0))] + [hbm] * nsc,
        out_shape=[_S((n_blocks,) + out_block, _BF)] + _recv_shapes(scatter, windows),
        scratch_shapes=_sem_shapes(nsc) if nsc else [],
        compiler_params=_cp(("arbitrary",)),
    )(a, b, *scatter)
    return res if nsc else res[0]


def _coords():
    return lax.axis_index("x"), lax.axis_index("y"), lax.axis_index("c")


def _sem_shapes(na):
    return [pltpu.SemaphoreType.DMA((7 * na,)), pltpu.SemaphoreType.DMA((7 * na,)), pltpu.SemaphoreType.DMA((na,))]


class _Gather:
    def __init__(self, srcs, outs, sems, place=None):
        self.srcs, self.outs = srcs, outs
        self.send_sems, self.recv_sems, self.local_sems = sems
        self.place = place if place is not None else (lambda ref, block: ref.at[block])
        self.na = len(srcs)
        x, y, c = _coords()
        self.pos = (x, y, c)
        self.me = 4 * x + 2 * y + c
        self.sibling = (x, y, 1 - c)
        self.chips = [(1 - x, y), (x, 1 - y), (1 - x, 1 - y)]

    @staticmethod
    def _slot(px, py, pc):
        return 4 * px + 2 * py + pc

    def _copy(self, a, k, block, to, own=False):
        dst = self.place(self.outs[a], block)
        return pltpu.make_async_remote_copy(
            src_ref=self.srcs[a] if own else dst, dst_ref=dst,
            send_sem=self.send_sems.at[7 * a + k], recv_sem=self.recv_sems.at[7 * a + k],
            device_id=to, device_id_type=_MESH)

    def _mine(self, a):
        return pltpu.make_async_copy(self.srcs[a], self.place(self.outs[a], self.me), self.local_sems.at[a])

    def _first(self):
        c = self.pos[2]
        cps = []
        for a in range(self.na):
            cps.append(self._copy(a, 0, self.me, self.sibling, own=True))
            cps += [self._copy(a, 1 + j, self.me, (*chip, c), own=True) for j, chip in enumerate(self.chips)]
        return cps

    def _passed(self):
        c = self.pos[2]
        return [self._copy(a, 4 + j, self._slot(*chip, c), self.sibling)
                for j, chip in enumerate(self.chips) for a in range(self.na)]

    def start(self):
        for a in range(self.na):
            self._mine(a).start()
        for cp in self._first():
            cp.start()

    def forward(self, j, arrays=None):
        c = self.pos[2]
        chip = self.chips[j]
        for a in (range(self.na) if arrays is None else arrays):
            self._copy(a, 1 + j, self._slot(*chip, c), self.pos).wait_recv()
            self._copy(a, 4 + j, self._slot(*chip, c), self.sibling).start()

    def wait_sibling(self):
        x, y, c = self.pos
        for a in range(self.na):
            self._copy(a, 0, self._slot(x, y, 1 - c), self.pos).wait_recv()

    def wait_passed(self, j):
        c = self.pos[2]
        for a in range(self.na):
            self._copy(a, 4 + j, self._slot(*self.chips[j], 1 - c), self.pos).wait_recv()

    def finish_sends(self):
        for cp in self._first() + self._passed():
            cp.wait_send()
        for a in range(self.na):
            self._mine(a).wait()

    def finish(self):
        self.wait_sibling()
        for j in range(3):
            self.wait_passed(j)
        self.finish_sends()


class _Exchange:
    def __init__(self, scatter, gather, outs, sems, windows=None):
        self.windows = windows if windows is not None else [None] * len(scatter)
        self.ins = list(scatter) + list(gather)
        self.ns, self.na = len(scatter), len(scatter) + len(gather)
        self.outs = outs
        self.send_sems, self.recv_sems, self.local_sems = sems
        x, y, c = _coords()
        self.pos = (x, y, c)
        self.me = 4 * x + 2 * y + c

    def _peer(self, r):
        x, y, c = self.pos
        return x ^ (r >> 2), y ^ ((r >> 1) & 1), c ^ (r & 1)

    def _src(self, a, block):
        if a >= self.ns:
            return self.ins[a]
        if self.windows[a] is None:
            return self.ins[a].at[block]
        row0, rows = self.windows[a]
        return self.ins[a].at[block, pl.ds(row0, rows)]

    def _local(self, a):
        return pltpu.make_async_copy(self._src(a, self.me), self.outs[a].at[self.me], self.local_sems.at[a])

    def _send(self, a, r):
        px, py, pc = self._peer(r)
        return pltpu.make_async_remote_copy(
            src_ref=self._src(a, 4 * px + 2 * py + pc), dst_ref=self.outs[a].at[self.me],
            send_sem=self.send_sems.at[7 * a + r - 1], recv_sem=self.recv_sems.at[7 * a + r - 1],
            device_id=(px, py, pc), device_id_type=_MESH)

    def _recv(self, a, r):
        px, py, pc = self._peer(r)
        return pltpu.make_async_remote_copy(
            src_ref=self._src(a, self.me), dst_ref=self.outs[a].at[4 * px + 2 * py + pc],
            send_sem=self.send_sems.at[7 * a + r - 1], recv_sem=self.recv_sems.at[7 * a + r - 1],
            device_id=(px, py, pc), device_id_type=_MESH)

    def start(self):
        for a in range(self.na):
            self._local(a).start()
        for r in range(1, N_DEV):
            for a in range(self.na):
                self._send(a, r).start()

    def finish(self):
        for r in range(1, N_DEV):
            for a in range(self.na):
                self._recv(a, r).wait_recv()
        for r in range(1, N_DEV):
            for a in range(self.na):
                self._send(a, r).wait_send()
        for a in range(self.na):
            self._local(a).wait()


def _prologue(x, tgt, small_l, w_in_l, cast_f32):
    seq = x.shape[0]
    nx = seq // TM
    rest_rows = seq - nx * TM
    nt = nx + 1
    nc = len(cast_f32)
    body_rows = TM - N_META
    assert nx >= 1 and rest_rows % 8 == 0 and rest_rows <= body_rows
    x_rest, t_rest = x[nx * TM:], tgt[nx * TM:]

    def last_tile_body(rest_ref):
        parts = ([rest_ref[...]] if rest_rows else []) + (
            [jnp.zeros((body_rows - rest_rows, D), _F32)] if body_rows > rest_rows else [])
        return parts[0] if len(parts) == 1 else jnp.concatenate(parts, axis=0)

    def body(xm_ref, xp_ref, tm_ref, tp_ref, *rest):
        if rest_rows:
            xr_ref, tr_ref, rest = rest[0], rest[1], rest[2:]
        else:
            xr_ref = tr_ref = None
        s_ref, w_ref, rest = rest[0], rest[1], rest[2:]
        cins = rest[:nc]
        h0_ref, tgt_ref, small_ref, wg_ref = rest[nc:nc + 4]
        couts = rest[nc + 4:2 * nc + 4]
        s_stage, w_stage, meta, msem = rest[2 * nc + 4:2 * nc + 8]
        g_s = _Gather([s_stage], [small_ref], rest[2 * nc + 8:2 * nc + 11])
        g_w = _Gather([w_stage], [wg_ref], rest[2 * nc + 11:], place=_pair_place)
        s = pl.program_id(0)
        i = (s + 1) % nt

        @pl.when(s == 0)
        def _():
            s_stage[...] = s_ref[...]
            w_stage[...] = w_ref[...].astype(_BF)
            g_s.start()
            g_w.start()
            meta[...] = jnp.zeros_like(meta)
            for a in range(nc):
                couts[a][...] = cins[a][...].astype(_BF)

        @pl.when(s == nt - 1)
        def _():
            for j in range(3):
                g_s.forward(j)
            g_s.finish()
            cps = [pltpu.make_async_copy(small_ref.at[k, pl.ds(0, N_META), :], meta.at[:, pl.ds(128 * k, 128)],
                                         msem.at[k]) for k in range(N_DEV)]
            for cp in cps:
                cp.start()
            for cp in cps:
                cp.wait()
            for j in range(3):
                g_w.forward(j)
            g_w.finish()

        has_x = i < nx
        h0_ref[pl.ds(0, N_META), :] = jnp.where(i == 0, meta[...], xp_ref[...])
        h0_ref[pl.ds(N_META, body_rows), :] = jnp.where(has_x, xm_ref[pl.ds(0, body_rows), :], last_tile_body(xr_ref))
        tgt_ref[pl.ds(0, N_META), :] = jnp.where(i == 0, 0.0, tp_ref[...])
        tgt_ref[pl.ds(N_META, body_rows), :] = jnp.where(has_x, tm_ref[pl.ds(0, body_rows), :], last_tile_body(tr_ref))

    def tile_of(s):
        return (s + 1) % nt

    hbm = pl.BlockSpec(memory_space=pl.ANY)
    main = pl.BlockSpec((TM, D), lambda s: (jnp.minimum(tile_of(s), nx - 1), 0))
    prev = pl.BlockSpec((N_META, D), lambda s: (jnp.maximum(tile_of(s) * (TM // N_META) - 1, 0), 0))
    tile = pl.BlockSpec((TM, D), lambda s: (tile_of(s), 0))
    rests = [x_rest, t_rest] if rest_rows else []
    return pl.pallas_call(
        body, name="prologue", grid=(nt,),
        in_specs=[main, prev, main, prev] + [_const(r.shape) for r in rests]
        + [_const(small_l.shape), _const(w_in_l.shape)] + [_const(l.shape) for l in cast_f32],
        out_specs=[tile, tile, hbm, hbm] + [_full(l.shape) for l in cast_f32],
        out_shape=[_S((nt * TM, D), _F32), _S((nt * TM, D), _F32), _S((N_DEV,) + small_l.shape, _F32),
                   _S((4, D, WIN_P), _BF)] + [_S(l.shape, _BF) for l in cast_f32],
        scratch_shapes=[pltpu.VMEM(small_l.shape, _F32), pltpu.VMEM(w_in_l.shape, _BF), pltpu.VMEM((N_META, D), _F32),
                        pltpu.SemaphoreType.DMA((N_DEV,))] + _sem_shapes(1) + _sem_shapes(1),
        compiler_params=_cp(("arbitrary",)),
    )(x, x, tgt, tgt, *rests, small_l, w_in_l, *cast_f32)


def _adamw_math(w, g, m, v):
    m2 = ADAM_B1 * m + (1.0 - ADAM_B1) * g
    v2 = ADAM_B2 * v + (1.0 - ADAM_B2) * (g * g)
    m_hat = m2 / (1.0 - ADAM_B1 ** ADAM_STEP)
    v_hat = v2 / (1.0 - ADAM_B2 ** ADAM_STEP)
    delta = -ADAM_LR * (m_hat / (jnp.sqrt(v_hat) + ADAM_EPS) + ADAM_WD * w)
    return delta, m2, v2


def _adamw_big(name, recv, w, m, v, rows):
    r_all, c_all = w.shape

    def body(r_ref, w_ref, m_ref, v_ref, g_out, d_out, m_out, v_out):
        g = r_ref[0].astype(_F32)
        for k in range(1, N_DEV):
            g = g + r_ref[k].astype(_F32)
        delta, m2, v2 = _adamw_math(w_ref[...], g, m_ref[...], v_ref[...])
        g_out[...] = g
        d_out[...] = delta
        m_out[...] = m2
        v_out[...] = v2

    tile = pl.BlockSpec((rows, c_all), lambda i: (i, 0))
    return pl.pallas_call(
        body, name=name, grid=(r_all // rows,),
        in_specs=[pl.BlockSpec((N_DEV, rows, c_all), lambda i: (0, i, 0)), tile, tile, tile],
        out_specs=[tile] * 4,
        out_shape=[_S(w.shape, _F32)] * 4,
        compiler_params=_cp(("arbitrary",)),
    )(recv, w, m, v)


def _adamw_small(gathered, slices, wmv):
    ng, npar = len(gathered), len(slices)

    def body(*refs):
        g_refs = refs[:ng]
        wmv_refs = refs[ng:ng + 3 * npar]
        outs = refs[ng + 3 * npar:]
        for i, (ai, r0, nr, c0, ncol) in enumerate(slices):
            g = g_refs[ai][0, pl.ds(r0, nr), pl.ds(c0, ncol)].astype(_F32)
            for k in range(1, N_DEV):
                g = g + g_refs[ai][k, pl.ds(r0, nr), pl.ds(c0, ncol)].astype(_F32)
            w_ref, m_ref, v_ref = wmv_refs[3 * i:3 * i + 3]
            delta, m2, v2 = _adamw_math(w_ref[...], g, m_ref[...], v_ref[...])
            outs[4 * i][...] = g
            outs[4 * i + 1][...] = delta
            outs[4 * i + 2][...] = m2
            outs[4 * i + 3][...] = v2
        total = g_refs[0][0, pl.ds(R_LOSS, 1), pl.ds(0, 128)]
        for k in range(1, N_DEV):
            total = total + g_refs[0][k, pl.ds(R_LOSS, 1), pl.ds(0, 128)]
        outs[4 * npar][...] = total

    flat = [t for trip in wmv for t in trip]
    out_shape = []
    for w, _, _ in wmv:
        out_shape += [_S(w.shape, _F32)] * 4
    out_shape.append(_S((1, 128), _F32))
    return pl.pallas_call(
        body, name="adamw_small", out_shape=out_shape,
        compiler_params=pltpu.CompilerParams(vmem_limit_bytes=VMEM_LIMIT),
    )(*gathered, *flat)


def _block_diag(w):
    eye = jnp.eye(8, dtype=w.dtype)
    return (w[:, :, None, :] * eye[:, None, :, None]).reshape(D_RG, D_RG)


def _diag_blocks(g):
    return jnp.concatenate([g[64 * h:64 * (h + 1), 64 * h:64 * (h + 1)] for h in range(8)], axis=0)


def _local_step(h0, tgt_p, n_valid, g_mix, w_in, vec, wr, wi, hb, g_hg, w_out_l, g_ffn, w_gu_l, w_down_l, g_fin):
    t_pad = h0.shape[0]
    me = 4 * lax.axis_index("x") + 2 * lax.axis_index("y") + lax.axis_index("c")
    p, u, y, hs, o, sc, w_out, w_gu, w_down = _mixer_fwd(h0, g_mix, w_in, wr, wi, vec, hb, g_hg,
                                                         [w_out_l, w_gu_l, w_down_l])
    w_out = w_out.reshape(D, D)
    w_down = w_down.reshape(4, FFB, D)
    h1, v, gu, act, dh2, dh2b, loss, gfin = _ffn_loss(h0, y, w_out, g_ffn, w_gu, w_down, g_fin, tgt_p, n_valid)

    g_wdown = _wgrad("wgrad_down", act, dh2b, pl.BlockSpec((1, t_pad, FFB), lambda j: (j, 0, 0)),
                     pl.BlockSpec((t_pad, D), lambda j: (0, 0)), 4, (FFB, D))
    g_wdown = g_wdown.reshape(N_DEV, D_FF // N_DEV, D)
    dgu, dh1, dh1b, dy, gffn, r_wdown = _ffn_bwd(dh2, dh2b, gu, h1, g_ffn, w_gu, w_down, w_out, [g_wdown])
    g_wgu = _wgrad("wgrad_gate_up", dgu, v, pl.BlockSpec((1, t_pad, FFB), lambda j: (j, 0, 0)),
                   pl.BlockSpec((t_pad, D), lambda j: (0, 0)), N_DEV, (FFB, D))
    g_wout = _wgrad("wgrad_out", y, dh1b, pl.BlockSpec((t_pad, D // N_DEV), lambda j: (0, j)),
                    pl.BlockSpec((t_pad, D), lambda j: (0, 0)), N_DEV, (D // N_DEV, D))
    dp, gvec, gw, r_wgu, r_wout = _mixer_bwd(p, hs, o, sc, dy, wr, wi, vec, hb, g_hg, [g_wgu, g_wout], [None, None])
    pack_c = jnp.concatenate([_diag_blocks(gw[0]), _diag_blocks(gw[1])], axis=1).astype(_BF)
    order = (me ^ jnp.array(_SEND_ORDER, jnp.int32)).astype(jnp.int32)
    dh0, r_win, all_b, all_c, all_a, all_g = _inproj_bwd_send(dp, w_in, h0, dh1, g_mix, u, order, gffn, gfin, loss,
                                                              [gvec, pack_c])
    return dh0, (r_win, r_wgu, r_wout, r_wdown), (all_a, all_b, all_c, all_g)


def kernel(x, meta_tokens, mix_norm_g, w_in, conv_w, conv_b, w_rgate, b_rgate, w_igate, b_igate, lru_lambda, rg_norm_g, hg_lower_bound, hg_norm_g, w_out, ffn_norm_g, w_gate_up, w_down, final_norm_g, loss_target, m_meta_tokens, m_mix_norm_g, m_w_in, m_conv_w, m_conv_b, m_w_rgate, m_b_rgate, m_w_igate, m_b_igate, m_lru_lambda, m_rg_norm_g, m_hg_lower_bound, m_hg_norm_g, m_w_out, m_ffn_norm_g, m_w_gate_up, m_w_down, m_final_norm_g, v_meta_tokens, v_mix_norm_g, v_w_in, v_conv_w, v_conv_b, v_w_rgate, v_b_rgate, v_w_igate, v_b_igate, v_lru_lambda, v_rg_norm_g, v_hg_lower_bound, v_hg_norm_g, v_w_out, v_ffn_norm_g, v_w_gate_up, v_w_down, v_final_norm_g):
    seq = x.shape[1]
    me = 4 * lax.axis_index("x") + 2 * lax.axis_index("y") + lax.axis_index("c")

    n_valid = N_META + seq
    small_l = jnp.concatenate([meta_tokens, jnp.pad(conv_w[0], ((0, 4), (0, 64)))], axis=0)
    h0, tgt_p, small_g, w_in_g, w_gu_l, w_out_l, w_down_l = _prologue(
        x[0], loss_target[0], small_l, w_in[0], [w_gate_up[0].T, w_out[0], w_down[0]])
    conv_w_full = jnp.transpose(small_g[:, N_META:N_META + 4, :64], (1, 0, 2)).reshape(4, D_RG)
    vec = jnp.concatenate([conv_b, b_rgate, b_igate, lru_lambda, rg_norm_g, jnp.zeros((3, D_RG), _F32),
                           conv_w_full, jnp.zeros((4, D_RG), _F32)], axis=0)
    wr = _block_diag(w_rgate[0]).astype(_BF)
    wi = _block_diag(w_igate[0]).astype(_BF)

    dh0, (r_win, r_wgu, r_wout, r_wdown), (all_a, all_b, all_c, all_g) = _local_step(
        h0, tgt_p, n_valid, mix_norm_g, w_in_g, vec, wr, wi, hg_lower_bound, hg_norm_g,
        w_out_l, ffn_norm_g, w_gu_l, w_down_l, final_norm_g.reshape(1, D))
    grad_x = dh0[N_META:N_META + seq][None]

    outs = {}
    outs["w_in"] = _adamw_big("adamw_w_in", r_win, w_in[0], m_w_in[0], v_w_in[0], 256)
    outs["w_gate_up"] = [r.T for r in _adamw_big("adamw_w_gate_up", r_wgu, w_gate_up[0].T, m_w_gate_up[0].T,
                                                 v_w_gate_up[0].T, 176)]
    outs["w_out"] = _adamw_big("adamw_w_out", r_wout, w_out[0], m_w_out[0], v_w_out[0], 128)
    outs["w_down"] = _adamw_big("adamw_w_down", r_wdown, w_down[0], m_w_down[0], v_w_down[0], 176)

    meta_part = lax.dynamic_slice_in_dim(all_a[:, R_META:R_META + N_META, :], me * 128, 128, axis=2)
    convw_part = lax.dynamic_slice_in_dim(all_b[:, R_CONVW:R_CONVW + 4, :], me * 64, 64, axis=2)
    gathered = [all_a, all_b, all_c, meta_part, convw_part, all_g]
    small_params = [
        ("meta_tokens", (3, 0, N_META, 0, 128), (meta_tokens, m_meta_tokens, v_meta_tokens), (N_META, 128)),
        ("mix_norm_g", (5, 0, 1, 0, D),(mix_norm_g, m_mix_norm_g, v_mix_norm_g), (1, D)),
        ("conv_w", (4, 0, 4, 0, 64), (conv_w, m_conv_w, v_conv_w), (4, 64)),
        ("conv_b", (1, R_CONVB, 1, 0, D_RG), (conv_b, m_conv_b, v_conv_b), (1, D_RG)),
        ("w_rgate", (2, 0, 512, 0, 64), (w_rgate, m_w_rgate, v_w_rgate), (512, 64)),
        ("b_rgate", (1, R_BR, 1, 0, D_RG), (b_rgate, m_b_rgate, v_b_rgate), (1, D_RG)),
        ("w_igate", (2, 0, 512, 64, 64), (w_igate, m_w_igate, v_w_igate), (512, 64)),
        ("b_igate", (1, R_BI, 1, 0, D_RG), (b_igate, m_b_igate, v_b_igate), (1, D_RG)),
        ("lru_lambda", (1, R_LAM, 1, 0, D_RG), (lru_lambda, m_lru_lambda, v_lru_lambda), (1, D_RG)),
        ("rg_norm_g", (1, R_GRG, 1, 0, D_RG), (rg_norm_g, m_rg_norm_g, v_rg_norm_g), (1, D_RG)),
        ("hg_lower_bound", (1, R_HB0, 2, 0, D_HG), (hg_lower_bound, m_hg_lower_bound, v_hg_lower_bound), (2, D_HG)),
        ("hg_norm_g", (1, R_GHG, 1, 0, HD), (hg_norm_g, m_hg_norm_g, v_hg_norm_g), (1, HD)),
        ("ffn_norm_g", (0, R_GFFN, 1, 0, D), (ffn_norm_g, m_ffn_norm_g, v_ffn_norm_g), (1, D)),
        ("final_norm_g", (0, R_GFIN, 1, 0, D), (final_norm_g, m_final_norm_g, v_final_norm_g), (1, D)),
    ]
    res = _adamw_small(gathered, [s[1] for s in small_params],
                       [tuple(t.reshape(s[3]) for t in s[2]) for s in small_params])
    for i, s in enumerate(small_params):
        outs[s[0]] = [r.reshape(s[2][0].shape) for r in res[4 * i:4 * i + 4]]
    for n, ref in (("w_in", w_in), ("w_gate_up", w_gate_up), ("w_out", w_out), ("w_down", w_down)):
        outs[n] = [r.reshape(ref.shape) for r in outs[n]]

    loss_all = res[4 * len(small_params)][0, 0]
    order = ["meta_tokens", "mix_norm_g", "w_in", "conv_w", "conv_b", "w_rgate", "b_rgate", "w_igate", "b_igate",
             "lru_lambda", "rg_norm_g", "hg_lower_bound", "hg_norm_g", "w_out", "ffn_norm_g", "w_gate_up", "w_down",
             "final_norm_g"]
    return (loss_all, grad_x, *[outs[n][0] for n in order], *[outs[n][1] for n in order],
            *[outs[n][2] for n in order], *[outs[n][3] for n in order])
```

```python
import functools

import jax
import jax.numpy as jnp
from jax import lax
from jax.experimental import pallas as pl
from jax.experimental.pallas import tpu as pltpu

_BF = jnp.bfloat16
_F32 = jnp.float32
_S = jax.ShapeDtypeStruct
_MESH = pl.DeviceIdType.MESH

N_DEV = 8
N_META = 16
D = 1024
D_RG = 512
D_HG = 512
HD = 128
NH = D_HG // HD
D_IN = 3072
D_FF = 2816
FFB = D_FF // 4
WIN_B = D_IN // N_DEV
WIN_P = 2 * WIN_B
EPS = 1e-6
LRU_C = 8.0
TM = 320
HC = 64
VMEM_LIMIT = 62 * 1024 * 1024

ADAM_LR = 0.001
ADAM_B1 = 0.9
ADAM_B2 = 0.999
ADAM_EPS = 1e-08
ADAM_WD = 0.01
ADAM_STEP = 10

_SEND_ORDER = (6, 4, 2, 7, 5, 3, 1, 0)

R_CONVB, R_BR, R_BI, R_LAM, R_GRG, R_HB0, R_HB1, R_GHG, R_CONVW = 0, 1, 2, 3, 4, 5, 6, 7, 8
R_GMIX, R_GFFN, R_GFIN, R_LOSS = 0, 1, 2, 3


def _cp(sem=None, **kw):
    return pltpu.CompilerParams(dimension_semantics=sem, vmem_limit_bytes=VMEM_LIMIT, **kw)


def _dot(a, b):
    return jnp.dot(a, b, preferred_element_type=_F32)


def _dot_nt(a, b):
    return lax.dot_general(a, b, (((1,), (1,)), ((), ())), preferred_element_type=_F32)


def _dot_tn(a, b):
    return lax.dot_general(a, b, (((0,), (0,)), ((), ())), preferred_element_type=_F32)


def _sigmoid(x):
    return 0.5 * jnp.tanh(0.5 * x) + 0.5


def _dsilu(x, s):
    return s * (1.0 + x * (1.0 - s))


_GELU_C = 0.7978845608028654


def _gelu_parts(x):
    t = jnp.tanh(_GELU_C * (x + 0.044715 * (x * x * x)))
    g = 0.5 * x * (1.0 + t)
    dg = 0.5 * (1.0 + t) + 0.5 * x * (1.0 - t * t) * (_GELU_C * (1.0 + 3.0 * 0.044715 * (x * x)))
    return g, dg


def _softplus(z):
    e = jnp.exp(-jnp.abs(z))
    w = 1.0 + e
    l1p = jnp.where(w == 1.0, e, jnp.log(w) * e / jnp.where(w == 1.0, 1.0, w - 1.0))
    return jnp.maximum(z, 0.0) + l1p


def _rms_fwd(x):
    r = lax.rsqrt(jnp.mean(x * x, axis=-1, keepdims=True) + EPS)
    return x * r, r


def _rms_bwd(dyg, n, r):
    return r * (dyg - n * jnp.mean(dyg * n, axis=-1, keepdims=True))


def _full(shape):
    nd = len(shape)
    return pl.BlockSpec(shape, lambda i: (0,) * nd)


def _const(shape):
    nd = len(shape)
    return pl.BlockSpec(shape, lambda i: (0,) * nd, pipeline_mode=pl.Buffered(1))


def _carry_gather(gather, i, nt, early=0):
    @pl.when(i == 0)
    def _():
        gather.start()

    def tail():
        for j in range(3):
            if early:
                @pl.when(i == min(nt // 3 + j, nt - 1))
                def _(j=j):
                    gather.forward(j, range(early))

            @pl.when(i == max(nt - 4 + j, 0))
            def _(j=j):
                gather.forward(j, range(early, gather.na))

        @pl.when(i == nt - 1)
        def _():
            gather.finish()

    return tail


def _pair_place(ref, block):
    return ref.at[block // 2, :, pl.ds(pl.multiple_of((block % 2) * WIN_B, WIN_B), WIN_B)]


def _rg_gates(xc, wr_ref, wi_ref, vec_ref):
    xcb = xc.astype(_BF)
    r = _sigmoid(_dot(xcb, wr_ref[...]) + vec_ref[R_BR:R_BR + 1, :])
    ig = _sigmoid(_dot(xcb, wi_ref[...]) + vec_ref[R_BI:R_BI + 1, :])
    nsp8 = -LRU_C * _softplus(-vec_ref[R_LAM:R_LAM + 1, :])
    la = nsp8 * r
    a = jnp.exp(la)
    th = jnp.tanh(la)
    s = jnp.sqrt(-2.0 * th / (1.0 - th))
    return r, ig, a, s, nsp8


def _conv(xbuf, vec_ref):
    acc = vec_ref[R_CONVW:R_CONVW + 1, :] * xbuf[pl.ds(5, TM), :]
    for j in range(1, 4):
        acc = acc + vec_ref[R_CONVW + j:R_CONVW + j + 1, :] * xbuf[pl.ds(5 + j, TM), :]
    return vec_ref[R_CONVB:R_CONVB + 1, :] + acc


def _dot3(m01, x):
    hi = x.astype(_BF)
    r1 = x - hi.astype(_F32)
    mid = r1.astype(_BF)
    lo = (r1 - mid.astype(_F32)).astype(_BF)
    return (_dot(m01, lo) + _dot(m01, mid)) + _dot(m01, hi)


def _chunk_dot3(m01, x):
    return jnp.concatenate([_dot3(m01, x[HC * c:HC * (c + 1), :]) for c in range(x.shape[0] // HC)], axis=0)


def _chunk_masks():
    row = lax.broadcasted_iota(jnp.int32, (HC, HC), 0)
    col = lax.broadcasted_iota(jnp.int32, (HC, HC), 1)
    return (row >= col).astype(_BF), (col >= row).astype(_BF), jnp.ones((HC, HC), _BF)


def _per_chunk_rows(x, r):
    return jnp.concatenate([jnp.broadcast_to(x[HC * c + r:HC * c + r + 1, :], (HC, x.shape[1]))
                            for c in range(TM // HC)], axis=0)


def _hg_prep(p_ref, lb, tri):
    hq = p_ref[:, pl.ds(2 * D_RG, D_HG)]
    hf = p_ref[:, pl.ds(2 * D_RG + D_HG, D_HG)]
    sq = _sigmoid(hq)
    q = hq * sq
    sg = _sigmoid(hf)
    f = lb + (1.0 - lb) * sg
    k = 1.0 - f
    b = _chunk_dot3(tri, jnp.log(f))
    bm = _per_chunk_rows(b, HC // 2 - 1)
    bl = _per_chunk_rows(b, HC - 1)
    e_q = jnp.exp(b - bm)
    e_k = jnp.exp(bm - b)
    e_b = jnp.exp(b)
    e_l = jnp.exp(bl - b)
    return dict(hq=hq, sq=sq, q=q, sg=sg, f=f, k=k, e_q=e_q, e_k=e_k, e_b=e_b, e_l=e_l,
                qd=q * e_q, kd=k * e_k, qe=q * e_b, ke=k * e_l, e_end=jnp.exp(bl))


def _mixer_fwd(h0, g_mix, w_in, wr, wi, vec, hb, g_hg, shards):
    t_pad = h0.shape[0]
    nt = t_pad // TM
    nc_t = TM // HC
    nsh = len(shards)

    def body(h_ref, gmix_ref, win_ref, wr_ref, wi_ref, vec_ref, hb_ref, ghg_ref, *rest):
        sh_refs, rest = rest[:nsh], rest[nsh:]
        pout_ref, uout_ref, y_ref, hs_ref, o_ref, sc_ref = rest[:6]
        gath_refs, rest = rest[6:6 + nsh], rest[6 + nsh:]
        xbuf, a_s, b_s, hcar, st, qd_s, kd_s, qe_s, ke_s, v_s, u_s, p_s, p_ref = rest[:13]
        i = pl.program_id(0)
        tail = _carry_gather(_Gather(sh_refs, gath_refs, rest[13:]), i, nt + 1, early=1)

        @pl.when(i == 0)
        def _():
            p_s[...] = jnp.zeros_like(p_s)

        p_ref[...] = p_s[...]

        @pl.when(i <= 1)
        def _():
            xbuf[pl.ds(0, 8), :] = jnp.zeros((8, D_RG), _F32)
            hcar[...] = jnp.zeros_like(hcar)
            st[...] = jnp.zeros_like(st)

        n_h, _ = _rms_fwd(h_ref[...])
        u = (n_h * gmix_ref[...]).astype(_BF)
        uout_ref[...] = u
        pieces = [(j, k) for j in range(4) for k in range(WIN_P // 256)]

        def project(count):
            for _ in range(count):
                j, k = pieces.pop(0)
                blk = _dot(u, win_ref[j, :, pl.ds(256 * k, 256)])
                p_s[:, pl.ds(WIN_P * j + 256 * k, 256)] = blk
                pout_ref[:, pl.ds(WIN_P * j + 256 * k, 256)] = blk

        x = p_ref[:, pl.ds(0, D_RG)]
        xbuf[pl.ds(8, TM), :] = x
        xc = _conv(xbuf, vec_ref)
        xbuf[pl.ds(0, 8), :] = x[TM - 8:, :]
        r, ig, a, s, _ = _rg_gates(xc, wr_ref, wi_ref, vec_ref)
        a_s[...] = a
        b_s[...] = s * (ig * xc)

        def step(t, h):
            h = a_s[pl.ds(t, 1), :] * h + b_s[pl.ds(t, 1), :]
            hs_ref[pl.ds(t, 1), :] = h
            return h

        hcar[pl.ds(0, 1), :] = lax.fori_loop(0, TM, step, hcar[pl.ds(0, 1), :], unroll=8)
        gel, _ = _gelu_parts(p_ref[:, pl.ds(D_RG, D_RG)])
        n, _ = _rms_fwd(gel * hs_ref[...])
        y_ref[:, pl.ds(0, D_RG)] = (n * vec_ref[R_GRG:R_GRG + 1, :]).astype(_BF)

        lb = _sigmoid(hb_ref[0:1, :] - hb_ref[1:2, :])
        tri, _, _ = _chunk_masks()
        q = _hg_prep(p_ref, lb, tri)
        for name, ref in (("qd", qd_s), ("kd", kd_s), ("qe", qe_s), ("ke", ke_s)):
            ref[...] = q[name].astype(_BF)
        v_s[...] = p_ref[:, pl.ds(2 * D_RG + 2 * D_HG, D_HG)].astype(_BF)
        e_end = q["e_end"]
        causal = (lax.broadcasted_iota(jnp.int32, (HC, HC), 0) >= lax.broadcasted_iota(jnp.int32, (HC, HC), 1))
        for c in range(nc_t):
            for h in range(NH):
                rs, cs = pl.ds(HC * c, HC), pl.ds(HD * h, HD)
                amat = jnp.where(causal, _dot_nt(qd_s[rs, cs], kd_s[rs, cs]), 0.0)
                o_ref[rs, cs] = _dot(amat.astype(_BF), v_s[rs, cs])
                u_s[NH * c + h] = _dot_tn(v_s[rs, cs], ke_s[rs, cs])
                if pieces:
                    project(1)
        assert not pieces
        for h in range(NH):
            cs = pl.ds(HD * h, HD)
            s_run = st[h]
            for c in range(nc_t):
                rs = pl.ds(HC * c, HC)
                sc_ref[c, h] = s_run
                o_ref[rs, cs] += _dot_nt(qe_s[rs, cs], s_run.astype(_BF))
                s_run = e_end[HC * c:HC * c + 1, HD * h:HD * (h + 1)] * s_run + u_s[NH * c + h]
            st[h] = s_run
        for h in range(NH):
            cs = pl.ds(HD * h, HD)
            n_o, _ = _rms_fwd(o_ref[:, cs])
            hg = p_ref[:, pl.ds(2 * D_RG + 3 * D_HG + HD * h, HD)]
            y_ref[:, pl.ds(D_RG + HD * h, HD)] = ((n_o * ghg_ref[...]) * (hg * _sigmoid(hg))).astype(_BF)

        tail()

    hbm = pl.BlockSpec(memory_space=pl.ANY)

    def proj(i):
        return jnp.minimum(i, nt - 1)

    def mixed(i):
        return jnp.maximum(i - 1, 0)

    return pl.pallas_call(
        body, name="mixer_fwd", grid=(nt + 1,),
        in_specs=[pl.BlockSpec((TM, D), lambda i: (proj(i), 0)), _full((1, D)), _const((4, D, WIN_P)),
                  _full((D_RG, D_RG)), _full((D_RG, D_RG)),
                  _full((16, D_RG)), _full((2, D_HG)), _full((1, HD))] + [hbm] * nsh,
        out_specs=[pl.BlockSpec((TM, D_IN), lambda i: (proj(i), 0)), pl.BlockSpec((TM, D), lambda i: (proj(i), 0)),
                   pl.BlockSpec((TM, D), lambda i: (mixed(i), 0)), pl.BlockSpec((TM, D_RG), lambda i: (mixed(i), 0)),
                   pl.BlockSpec((TM, D_HG), lambda i: (mixed(i), 0)),
                   pl.BlockSpec((nc_t, NH, HD, HD), lambda i: (mixed(i), 0, 0, 0))] + [hbm] * nsh,
        out_shape=[_S((t_pad, D_IN), _F32), _S((t_pad, D), _BF),
                   _S((t_pad, D), _BF), _S((t_pad, D_RG), _F32), _S((t_pad, D_HG), _F32),
                   _S((t_pad // HC, NH, HD, HD), _F32)] + [_S((N_DEV,) + s.shape, s.dtype) for s in shards],
        scratch_shapes=[pltpu.VMEM((TM + 8, D_RG), _F32), pltpu.VMEM((TM, D_RG), _F32),
                        pltpu.VMEM((TM, D_RG), _F32), pltpu.VMEM((8, D_RG), _F32),
                        pltpu.VMEM((NH, HD, HD), _F32)] + [pltpu.VMEM((TM, D_HG), _BF) for _ in range(5)]
        + [pltpu.VMEM((nc_t * NH, HD, HD), _F32), pltpu.VMEM((TM, D_IN), _F32), pltpu.VMEM((TM, D_IN), _F32)]
        + _sem_shapes(nsh),
        compiler_params=_cp(("arbitrary",)),
    )(h0, g_mix, w_in, wr, wi, vec, hb, g_hg, *shards)


def _ffn_loss(h0, y, w_out, g_ffn, w_gu, w_down, g_fin, tgt, n_valid):
    t_pad = h0.shape[0]

    def body(h_ref, y_ref, wo_ref, gffn_ref, wgu_ref, wd_ref, g_ref, t_ref,
             h1_ref, v_ref, gu_ref, act_ref, dh2_ref, dh2b_ref, loss_ref, gfin_ref):
        i = pl.program_id(0)

        @pl.when(i == 0)
        def _():
            loss_ref[...] = jnp.zeros_like(loss_ref)
            gfin_ref[...] = jnp.zeros_like(gfin_ref)

        h1 = h_ref[...] + _dot(y_ref[...], wo_ref[...])
        h1_ref[...] = h1
        n1, _ = _rms_fwd(h1)
        vb = (n1 * gffn_ref[...]).astype(_BF)
        v_ref[...] = vb
        h2 = h1
        for b in range(4):
            gate = _dot_nt(vb, wgu_ref[b])
            up = _dot_nt(vb, wgu_ref[4 + b])
            gu_ref[b] = gate
            gu_ref[4 + b] = up
            act = ((gate * _sigmoid(gate)) * up).astype(_BF)
            act_ref[b] = act
            h2 = h2 + _dot(act, wd_ref[b])
        n, r = _rms_fwd(h2)
        out = n * g_ref[...]
        row = i * TM + lax.broadcasted_iota(jnp.int32, (TM, 1), 0)
        valid = (row >= N_META) & (row < n_valid)
        err = jnp.where(valid, out - t_ref[...], 0.0)
        loss_ref[...] += (0.5 / D) * jnp.sum(err * err)
        dout = err * (1.0 / D)
        gfin_ref[...] += jnp.sum(dout * n, axis=0, keepdims=True)
        dh2 = _rms_bwd(dout * g_ref[...], n, r)
        dh2_ref[...] = dh2
        dh2b_ref[...] = dh2.astype(_BF)

    tile = pl.BlockSpec((TM, D), lambda i: (i, 0))
    return pl.pallas_call(
        body, name="ffn_loss", grid=(t_pad // TM,),
        in_specs=[tile, tile, _const((D, D)), _full((1, D)),
                  _const((N_DEV, FFB, D)), _const((4, FFB, D)), _full((1, D)), tile],
        out_specs=[tile, tile,
                   pl.BlockSpec((N_DEV, TM, FFB), lambda i: (0, i, 0)), pl.BlockSpec((4, TM, FFB), lambda i: (0, i, 0)),
                   tile, tile, _full((8, 128)), _full((1, D))],
        out_shape=[_S((t_pad, D), _F32), _S((t_pad, D), _BF),
                   _S((N_DEV, t_pad, FFB), _F32), _S((4, t_pad, FFB), _BF), _S((t_pad, D), _F32),
                   _S((t_pad, D), _BF), _S((8, 128), _F32), _S((1, D), _F32)],
        compiler_params=_cp(("arbitrary",)),
    )(h0, y, w_out, g_ffn, w_gu, w_down, g_fin, tgt)


def _ffn_bwd(dh2, dh2b, gu, h1, g_ffn, w_gu, w_down, w_out, scatter):
    t_pad = dh2.shape[0]
    nsc = len(scatter)
    nt = t_pad // TM

    def body(dh2_ref, dh2b_ref, gu_ref, h1_ref, g_ref, wgu_ref, wd_ref, wo_ref, *rest):
        dgu_ref, dh1_ref, dh1b_ref, dy_ref, gffn_ref = rest[nsc:nsc + 5]
        exchange = _Exchange(rest[:nsc], [], rest[nsc + 5:2 * nsc + 5], rest[2 * nsc + 5:])
        i = pl.program_id(0)

        @pl.when(i == 0)
        def _():
            exchange.start()
            gffn_ref[...] = jnp.zeros_like(gffn_ref)

        db = dh2b_ref[...]
        dv = jnp.zeros((TM, D), _F32)
        for b in range(4):
            dact = _dot_nt(db, wd_ref[b])
            gate = gu_ref[b]
            up = gu_ref[4 + b]
            sg = _sigmoid(gate)
            dgate = ((dact * up) * _dsilu(gate, sg)).astype(_BF)
            dup = (dact * (gate * sg)).astype(_BF)
            dgu_ref[b] = dgate
            dgu_ref[4 + b] = dup
            dv = dv + _dot(dgate, wgu_ref[b]) + _dot(dup, wgu_ref[4 + b])
        n, r = _rms_fwd(h1_ref[...])
        gffn_ref[...] += jnp.sum(dv * n, axis=0, keepdims=True)
        dh1 = dh2_ref[...] + _rms_bwd(dv * g_ref[...], n, r)
        dh1_ref[...] = dh1
        dh1b = dh1.astype(_BF)
        dh1b_ref[...] = dh1b
        dy_ref[...] = _dot_nt(dh1b, wo_ref[...])

        @pl.when(i == nt - 1)
        def _():
            exchange.finish()

    tile = pl.BlockSpec((TM, D), lambda i: (i, 0))
    hbm = pl.BlockSpec(memory_space=pl.ANY)
    return pl.pallas_call(
        body, name="ffn_bwd", grid=(nt,),
        in_specs=[tile, tile, pl.BlockSpec((N_DEV, TM, FFB), lambda i: (0, i, 0)), tile, _full((1, D)),
                  _const((N_DEV, FFB, D)), _const((4, FFB, D)), _const((D, D))] + [hbm] * nsc,
        out_specs=[pl.BlockSpec((N_DEV, TM, FFB), lambda i: (0, i, 0)), tile, tile, tile, _full((1, D))] + [hbm] * nsc,
        out_shape=[_S((N_DEV, t_pad, FFB), _BF), _S((t_pad, D), _F32), _S((t_pad, D), _BF),
                   _S((t_pad, D), _F32), _S((1, D), _F32)] + _recv_shapes(scatter, [None] * nsc),
        scratch_shapes=_sem_shapes(nsc),
        compiler_params=_cp(("arbitrary",)),
    )(dh2, dh2b, gu, h1, g_ffn, w_gu, w_down, w_out, *scatter)


def _mixer_bwd(p, hs, o, sc, dy, wr, wi, vec, hb, g_hg, scatter, windows):
    t_pad = p.shape[0]
    nt = t_pad // TM
    nc_t = TM // HC
    nsc = len(scatter)

    def rev(i):
        return nt - 1 - i

    def body(p_ref, pprev_ref, hs_ref, hprev_ref, o_ref, sc_ref, dy_ref, wr_ref, wi_ref, vec_ref, hb_ref, ghg_ref,
             *rest):
        send_refs, rest = rest[:nsc], rest[nsc:]
        dp_ref, gvec_ref, gw_ref = rest[:3]
        recv_refs, rest = rest[3:3 + nsc], rest[3 + nsc:]
        xbuf, hbuf, dbuf, a_s, g_s, ccar, dst = rest[:7]
        qd_s, kd_s, qe_s, ke_s, v_s, do_s, dqd_s, dkd_s, dqe_s, dke_s, dv_s, w_s, dend_s = rest[7:20]
        exchange = _Exchange(send_refs, [], recv_refs, rest[20:], windows)
        i = pl.program_id(0)
        first_tile = i == nt - 1

        @pl.when(i == 0)
        def _():
            exchange.start()
            gvec_ref[...] = jnp.zeros_like(gvec_ref)
            gw_ref[...] = jnp.zeros_like(gw_ref)
            dbuf[pl.ds(TM, 8), :] = jnp.zeros((8, D_RG), _F32)
            ccar[...] = jnp.zeros_like(ccar)
            dst[...] = jnp.zeros_like(dst)

        def acc(row, val):
            gvec_ref[row:row + 1, :] += jnp.sum(val, axis=0, keepdims=True)

        keep = jnp.where(first_tile, 0.0, 1.0)
        x = p_ref[:, pl.ds(0, D_RG)]
        xbuf[pl.ds(0, 8), :] = pprev_ref[...] * keep
        xbuf[pl.ds(8, TM), :] = x
        xc = _conv(xbuf, vec_ref)
        r, ig, a, s, nsp8 = _rg_gates(xc, wr_ref, wi_ref, vec_ref)
        h = hs_ref[...]
        hbuf[pl.ds(0, 8), :] = hprev_ref[...] * keep
        hbuf[pl.ds(8, TM), :] = h
        hm1 = hbuf[pl.ds(7, TM), :]
        gr = p_ref[:, pl.ds(D_RG, D_RG)]
        gel, dgel = _gelu_parts(gr)
        n, rr = _rms_fwd(gel * h)
        dyn = dy_ref[:, pl.ds(0, D_RG)]
        acc(R_GRG, dyn * n)
        dpre = _rms_bwd(dyn * vec_ref[R_GRG:R_GRG + 1, :], n, rr)
        dp_ref[:, pl.ds(D_RG, D_RG)] = ((dpre * h) * dgel).astype(_BF)
        a_s[...] = a
        g_s[...] = dpre * gel

        def step(k, c):
            t = TM - 1 - k
            g = g_s[pl.ds(t, 1), :] + c
            g_s[pl.ds(t, 1), :] = g
            return a_s[pl.ds(t, 1), :] * g

        ccar[pl.ds(0, 1), :] = lax.fori_loop(0, TM, step, ccar[pl.ds(0, 1), :], unroll=8)
        gt = g_s[...]
        da = gt * hm1
        ixc = ig * xc
        ds = gt * ixc
        dig = (gt * s) * xc
        dxc = (gt * s) * ig
        dla = da * a - ds * ((a * a) / s)
        lam = vec_ref[R_LAM:R_LAM + 1, :]
        gvec_ref[R_LAM:R_LAM + 1, :] += jnp.sum(dla * r, axis=0, keepdims=True) * (LRU_C * _sigmoid(-lam))
        dzr = (dla * nsp8) * (r * (1.0 - r))
        dzi = dig * (ig * (1.0 - ig))
        acc(R_BR, dzr)
        acc(R_BI, dzi)
        xcb = xc.astype(_BF)
        dzrb = dzr.astype(_BF)
        dzib = dzi.astype(_BF)
        gw_ref[0] += _dot_tn(xcb, dzrb)
        gw_ref[1] += _dot_tn(xcb, dzib)
        dxc = dxc + _dot_nt(dzrb, wr_ref[...]) + _dot_nt(dzib, wi_ref[...])
        acc(R_CONVB, dxc)
        for j in range(4):
            acc(R_CONVW + j, dxc * xbuf[pl.ds(5 + j, TM), :])
        dbuf[pl.ds(0, TM), :] = dxc
        dx = vec_ref[R_CONVW + 3:R_CONVW + 4, :] * dxc
        for j in range(3):
            dx = dx + vec_ref[R_CONVW + j:R_CONVW + j + 1, :] * dbuf[pl.ds(3 - j, TM), :]
        dbuf[pl.ds(TM, 8), :] = dxc[0:8, :]
        dp_ref[:, pl.ds(0, D_RG)] = dx.astype(_BF)

        lb = _sigmoid(hb_ref[0:1, :] - hb_ref[1:2, :])
        tri, tri_rev, ones = _chunk_masks()
        q = _hg_prep(p_ref, lb, tri)
        qdb, kdb = q["qd"].astype(_BF), q["kd"].astype(_BF)
        qd_s[...] = qdb
        kd_s[...] = kdb
        qe_s[...] = q["qe"].astype(_BF)
        ke_s[...] = q["ke"].astype(_BF)
        v_s[...] = p_ref[:, pl.ds(2 * D_RG + 2 * D_HG, D_HG)].astype(_BF)
        e_end = q["e_end"]
        ghg = ghg_ref[...]
        for h in range(NH):
            cs = pl.ds(HD * h, HD)
            hg = p_ref[:, pl.ds(2 * D_RG + 3 * D_HG + HD * h, HD)]
            sh = _sigmoid(hg)
            n_o, r_o = _rms_fwd(o_ref[:, cs])
            dyh = dy_ref[:, pl.ds(D_RG + HD * h, HD)]
            dp_ref[:, pl.ds(2 * D_RG + 3 * D_HG + HD * h, HD)] = ((dyh * (n_o * ghg)) * _dsilu(hg, sh)).astype(_BF)
            dn = dyh * (hg * sh)
            gvec_ref[R_GHG:R_GHG + 1, pl.ds(0, HD)] += jnp.sum(dn * n_o, axis=0, keepdims=True)
            do_s[:, cs] = _rms_bwd(dn * ghg, n_o, r_o).astype(_BF)
        causal = (lax.broadcasted_iota(jnp.int32, (HC, HC), 0) >= lax.broadcasted_iota(jnp.int32, (HC, HC), 1))
        for c in range(nc_t):
            for h in range(NH):
                rs, cs = pl.ds(HC * c, HC), pl.ds(HD * h, HD)
                qd_c, kd_c, do_c = qd_s[rs, cs], kd_s[rs, cs], do_s[rs, cs]
                amat = jnp.where(causal, _dot_nt(qd_c, kd_c), 0.0).astype(_BF)
                da_m = jnp.where(causal, _dot_nt(do_c, v_s[rs, cs]), 0.0).astype(_BF)
                dqd_s[rs, cs] = _dot(da_m, kd_c)
                dkd_s[rs, cs] = _dot_tn(da_m, qd_c)
                dqe_s[rs, cs] = _dot(do_c, sc_ref[c, h].astype(_BF))
                dv_s[rs, cs] = _dot_tn(amat, do_c)
                w_s[NH * c + h] = _dot_tn(do_c, qe_s[rs, cs])
        for h in range(NH):
            cs = pl.ds(HD * h, HD)
            d_run = dst[h]
            for c in reversed(range(nc_t)):
                rs = pl.ds(HC * c, HC)
                d_b = d_run.astype(_BF)
                dke_s[rs, cs] = _dot(v_s[rs, cs], d_b)
                dp_ref[rs, pl.ds(2 * D_RG + 2 * D_HG + HD * h, HD)] = (
                    dv_s[rs, cs] + _dot_nt(ke_s[rs, cs], d_b)).astype(_BF)
                dend_s[pl.ds(c, 1), cs] = jnp.sum(sc_ref[c, h] * d_run, axis=0, keepdims=True)
                d_run = w_s[NH * c + h] + e_end[HC * c:HC * c + 1, HD * h:HD * (h + 1)] * d_run
            dst[h] = d_run
        dqd, dkd, dqe, dke = dqd_s[...], dkd_s[...], dqe_s[...], dke_s[...]
        dq = dqd * q["e_q"] + dqe * q["e_b"]
        dk = dkd * q["e_k"] + dke * q["e_l"]
        dkeke = dke * q["ke"]
        db = dqd * qdb.astype(_F32) - dkd * kdb.astype(_F32) + dqe * q["qe"] - dkeke
        d_end = jnp.concatenate([jnp.broadcast_to(dend_s[pl.ds(c, 1), :], (HC, D_HG)) for c in range(nc_t)], axis=0)
        dlf = _chunk_dot3(tri_rev, db) + _chunk_dot3(ones, dkeke) + d_end * e_end
        df = dlf / q["f"] - dk
        sg = q["sg"]
        gvec_ref[R_HB0:R_HB0 + 1, :] += jnp.sum(df * (1.0 - sg), axis=0, keepdims=True)
        dp_ref[:, pl.ds(2 * D_RG, D_HG)] = (dq * _dsilu(q["hq"], q["sq"])).astype(_BF)
        dp_ref[:, pl.ds(2 * D_RG + D_HG, D_HG)] = ((df * (1.0 - lb)) * (sg * (1.0 - sg))).astype(_BF)

        @pl.when(i == nt - 1)
        def _():
            glb = gvec_ref[R_HB0:R_HB0 + 1, :] * (lb * (1.0 - lb))
            gvec_ref[R_HB0:R_HB0 + 1, :] = glb
            gvec_ref[R_HB1:R_HB1 + 1, :] = -glb
            exchange.finish()

    hbm = pl.BlockSpec(memory_space=pl.ANY)
    return pl.pallas_call(
        body, name="mixer_bwd", grid=(nt,),
        in_specs=[pl.BlockSpec((TM, D_IN), lambda i: (rev(i), 0)),
                  pl.BlockSpec((8, D_RG), lambda i: (jnp.maximum(rev(i) * (TM // 8) - 1, 0), 0)),
                  pl.BlockSpec((TM, D_RG), lambda i: (rev(i), 0)),
                  pl.BlockSpec((8, D_RG), lambda i: (jnp.maximum(rev(i) * (TM // 8) - 1, 0), 0)),
                  pl.BlockSpec((TM, D_HG), lambda i: (rev(i), 0)),
                  pl.BlockSpec((nc_t, NH, HD, HD), lambda i: (rev(i), 0, 0, 0)),
                  pl.BlockSpec((TM, D), lambda i: (rev(i), 0)),
                  _full((D_RG, D_RG)), _full((D_RG, D_RG)), _full((16, D_RG)), _full((2, D_HG)), _full((1, HD))]
        + [hbm] * nsc,
        out_specs=[pl.BlockSpec((TM, D_IN), lambda i: (rev(i), 0)), _full((16, D_RG)), _full((2, D_RG, D_RG))]
        + [hbm] * nsc,
        out_shape=[_S((t_pad, D_IN), _BF), _S((16, D_RG), _F32), _S((2, D_RG, D_RG), _F32)]
        + _recv_shapes(scatter, windows),
        scratch_shapes=[pltpu.VMEM((TM + 8, D_RG), _F32), pltpu.VMEM((TM + 8, D_RG), _F32),
                        pltpu.VMEM((TM + 8, D_RG), _F32), pltpu.VMEM((TM, D_RG), _F32),
                        pltpu.VMEM((TM, D_RG), _F32), pltpu.VMEM((8, D_RG), _F32),
                        pltpu.VMEM((NH, HD, HD), _F32)]
        + [pltpu.VMEM((TM, D_HG), _BF) for _ in range(6)] + [pltpu.VMEM((TM, D_HG), _F32) for _ in range(5)]
        + [pltpu.VMEM((nc_t * NH, HD, HD), _F32), pltpu.VMEM((8, D_HG), _F32)] + _sem_shapes(nsc),
        compiler_params=_cp(("arbitrary",)),
    )(p, p, hs, hs, o, sc, dy, wr, wi, vec, hb, g_hg, *scatter)


def _inproj_bwd_send(dp, w_in, h0, dh1, g_mix, u, order, gffn, gfin, loss, to_all):
    t_pad = dp.shape[0]
    rb = TM
    n_steps = N_DEV + t_pad // rb
    na = len(to_all)

    def body(order_ref, dpc_ref, dpr_ref, u_ref, w_ref, h_ref, dh1_ref, g_ref, gffn_ref, gfin_ref, loss_ref, *rest):
        all_in = rest[:na]
        dh0_ref, recv_ref = rest[na:na + 2]
        all_out = rest[na + 2:2 * na + 2]
        meta_ref, alla_ref = rest[2 * na + 2:2 * na + 4]
        buf, pack, meta, blk_send, blk_recv, blk_local = rest[2 * na + 4:2 * na + 10]
        exchange = _Exchange([], all_in, all_out, rest[2 * na + 10:2 * na + 13])
        last = _Exchange([meta], [pack], [meta_ref, alla_ref], rest[2 * na + 13:])
        s = pl.program_id(0)
        x, y, c = _coords()
        me = 4 * x + 2 * y + c

        def send(step):
            r = _SEND_ORDER[step]
            return pltpu.make_async_remote_copy(
                src_ref=buf.at[step], dst_ref=recv_ref.at[me], send_sem=blk_send.at[step], recv_sem=blk_recv.at[r - 1],
                device_id=(x ^ (r >> 2), y ^ ((r >> 1) & 1), c ^ (r & 1)), device_id_type=_MESH)

        @pl.when(s == 0)
        def _():
            exchange.start()
            pack[...] = jnp.zeros_like(pack)

        @pl.when(s < N_DEV)
        def _():
            buf[s] = _dot_tn(u_ref[...], dpc_ref[...]).astype(_BF)

            for step in range(N_DEV - 1):
                @pl.when(s == step)
                def _(step=step):
                    send(step).start()

        @pl.when(s >= N_DEV)
        def _():
            du = jnp.zeros((rb, D), _F32)
            for j in range(4):
                du = du + _dot_nt(dpr_ref[:, WIN_P * j:WIN_P * (j + 1)], w_ref[j])
            n, r = _rms_fwd(h_ref[...])
            pack[R_GMIX:R_GMIX + 1, :] += jnp.sum(du * n, axis=0, keepdims=True)
            dh0 = dh1_ref[...] + _rms_bwd(du * g_ref[...], n, r)
            dh0_ref[...] = dh0

            @pl.when(s == N_DEV)
            def _():
                for k in range(N_DEV):
                    meta[k] = dh0[0:N_META, 128 * k:128 * (k + 1)]

        @pl.when(s == n_steps - 1)
        def _():
            pack[R_GFFN:R_GFFN + 1, :] = gffn_ref[...]
            pack[R_GFIN:R_GFIN + 1, :] = gfin_ref[...]
            pack[R_LOSS:R_LOSS + 1, pl.ds(0, 128)] = loss_ref[0:1, :]
            last.start()
            mine = pltpu.make_async_copy(buf.at[N_DEV - 1], recv_ref.at[me], blk_local.at[0])
            mine.start()
            for step in range(N_DEV - 1):
                send(step).wait_send()
            for r in range(1, N_DEV):
                px, py, pc = x ^ (r >> 2), y ^ ((r >> 1) & 1), c ^ (r & 1)
                pltpu.make_async_remote_copy(
                    src_ref=buf.at[0], dst_ref=recv_ref.at[4 * px + 2 * py + pc], send_sem=blk_send.at[0],
                    recv_sem=blk_recv.at[r - 1], device_id=(px, py, pc), device_id_type=_MESH).wait_recv()
            mine.wait()
            exchange.finish()
            last.finish()

    hbm = pl.BlockSpec(memory_space=pl.ANY)
    rows = pl.BlockSpec((rb, D), lambda s, order: (jnp.maximum(s - N_DEV, 0), 0))
    one = pl.BlockSpec((1, D), lambda s, order: (0, 0))
    res = pl.pallas_call(
        body, name="inproj_bwd_send",
        grid_spec=pltpu.PrefetchScalarGridSpec(
            num_scalar_prefetch=1, grid=(n_steps,),
            in_specs=[pl.BlockSpec((t_pad, WIN_B), lambda s, order: (0, order[jnp.minimum(s, N_DEV - 1)])),
                      pl.BlockSpec((rb, D_IN), lambda s, order: (jnp.maximum(s - N_DEV, 0), 0)),
                      pl.BlockSpec((t_pad, D), lambda s, order: (0, 0), pipeline_mode=pl.Buffered(1)),
                      pl.BlockSpec((4, D, WIN_P), lambda s, order: (0, 0, 0), pipeline_mode=pl.Buffered(1)),
                      rows, rows, one, one, one, pl.BlockSpec((8, 128), lambda s, order: (0, 0))] + [hbm] * na,
            out_specs=[rows] + [hbm] * (na + 3),
            scratch_shapes=[pltpu.VMEM((N_DEV, D, WIN_B), _BF), pltpu.VMEM((8, D), _F32),
                            pltpu.VMEM((N_DEV, N_META, 128), _F32),
                            pltpu.SemaphoreType.DMA((N_DEV - 1,)), pltpu.SemaphoreType.DMA((N_DEV - 1,)),
                            pltpu.SemaphoreType.DMA((1,))] + _sem_shapes(na) + _sem_shapes(2)),
        out_shape=[_S((t_pad, D), _F32), _S((N_DEV, D, WIN_B), _BF)]
        + [_S((N_DEV,) + g.shape, g.dtype) for g in to_all]
        + [_S((N_DEV, N_META, 128), _F32), _S((N_DEV, 8, D), _F32)],
        compiler_params=_cp(("arbitrary",)),
    )(order, dp, dp, u, w_in, h0, dh1, g_mix, gffn, gfin, loss, *to_all)
    return res


def _recv_shapes(scatter, windows):
    return [_S(s.shape if w is None else (s.shape[0], w[1]) + s.shape[2:], s.dtype) for s, w in zip(scatter, windows)]


def _wgrad(name, a, b, a_spec, b_spec, n_blocks, out_block, scatter=(), windows=None):
    nsc = len(scatter)
    windows = windows if windows is not None else [None] * nsc

    def body(a_ref, b_ref, *rest):
        o_ref = rest[nsc]
        j = pl.program_id(0)
        if nsc:
            exchange = _Exchange(rest[:nsc], [], rest[nsc + 1:2 * nsc + 1], rest[2 * nsc + 1:], windows)

            @pl.when(j == 0)
            def _():
                exchange.start()

        av = a_ref[0] if len(a_ref.shape) == 3 else a_ref[...]
        bv = b_ref[0] if len(b_ref.shape) == 3 else b_ref[...]
        o_ref[0] = _dot_tn(av, bv).astype(_BF)

        if nsc:
            @pl.when(j == n_blocks - 1)
            def _():
                exchange.finish()

    hbm = pl.BlockSpec(memory_space=pl.ANY)
    res = pl.pallas_call(
        body, name=name, grid=(n_blocks,),
        in_specs=[a_spec, b_spec] + [hbm] * nsc,
        out_specs=[pl.BlockSpec((1,) + out_block, lambda j: (j, 0, 0))] + [hbm] * nsc,
        out_shape=[_S((n_blocks,) + out_block, _BF)] + _recv_shapes(scatter, windows),
        scratch_shapes=_sem_shapes(nsc) if nsc else [],
        compiler_params=_cp(("arbitrary",)),
    )(a, b, *scatter)
    return res if nsc else res[0]


def _coords():
    return lax.axis_index("x"), lax.axis_index("y"), lax.axis_index("c")


def _sem_shapes(na):
    return [pltpu.SemaphoreType.DMA((7 * na,)), pltpu.SemaphoreType.DMA((7 * na,)), pltpu.SemaphoreType.DMA((na,))]


class _Gather:
    def __init__(self, srcs, outs, sems, place=None):
        self.srcs, self.outs = srcs, outs
        self.send_sems, self.recv_sems, self.local_sems = sems
        self.place = place if place is not None else (lambda ref, block: ref.at[block])
        self.na = len(srcs)
        x, y, c = _coords()
        self.pos = (x, y, c)
        self.me = 4 * x + 2 * y + c
        self.sibling = (x, y, 1 - c)
        self.chips = [(1 - x, y), (x, 1 - y), (1 - x, 1 - y)]

    @staticmethod
    def _slot(px, py, pc):
        return 4 * px + 2 * py + pc

    def _copy(self, a, k, block, to, own=False):
        dst = self.place(self.outs[a], block)
        return pltpu.make_async_remote_copy(
            src_ref=self.srcs[a] if own else dst, dst_ref=dst,
            send_sem=self.send_sems.at[7 * a + k], recv_sem=self.recv_sems.at[7 * a + k],
            device_id=to, device_id_type=_MESH)

    def _mine(self, a):
        return pltpu.make_async_copy(self.srcs[a], self.place(self.outs[a], self.me), self.local_sems.at[a])

    def _first(self):
        c = self.pos[2]
        cps = []
        for a in range(self.na):
            cps.append(self._copy(a, 0, self.me, self.sibling, own=True))
            cps += [self._copy(a, 1 + j, self.me, (*chip, c), own=True) for j, chip in enumerate(self.chips)]
        return cps

    def _passed(self):
        c = self.pos[2]
        return [self._copy(a, 4 + j, self._slot(*chip, c), self.sibling)
                for j, chip in enumerate(self.chips) for a in range(self.na)]

    def start(self):
        for a in range(self.na):
            self._mine(a).start()
        for cp in self._first():
            cp.start()

    def forward(self, j, arrays=None):
        c = self.pos[2]
        chip = self.chips[j]
        for a in (range(self.na) if arrays is None else arrays):
            self._copy(a, 1 + j, self._slot(*chip, c), self.pos).wait_recv()
            self._copy(a, 4 + j, self._slot(*chip, c), self.sibling).start()

    def wait_sibling(self):
        x, y, c = self.pos
        for a in range(self.na):
            self._copy(a, 0, self._slot(x, y, 1 - c), self.pos).wait_recv()

    def wait_passed(self, j):
        c = self.pos[2]
        for a in range(self.na):
            self._copy(a, 4 + j, self._slot(*self.chips[j], 1 - c), self.pos).wait_recv()

    def finish_sends(self):
        for cp in self._first() + self._passed():
            cp.wait_send()
        for a in range(self.na):
            self._mine(a).wait()

    def finish(self):
        self.wait_sibling()
        for j in range(3):
            self.wait_passed(j)
        self.finish_sends()


class _Exchange:
    def __init__(self, scatter, gather, outs, sems, windows=None):
        self.windows = windows if windows is not None else [None] * len(scatter)
        self.ins = list(scatter) + list(gather)
        self.ns, self.na = len(scatter), len(scatter) + len(gather)
        self.outs = outs
        self.send_sems, self.recv_sems, self.local_sems = sems
        x, y, c = _coords()
        self.pos = (x, y, c)
        self.me = 4 * x + 2 * y + c

    def _peer(self, r):
        x, y, c = self.pos
        return x ^ (r >> 2), y ^ ((r >> 1) & 1), c ^ (r & 1)

    def _src(self, a, block):
        if a >= self.ns:
            return self.ins[a]
        if self.windows[a] is None:
            return self.ins[a].at[block]
        row0, rows = self.windows[a]
        return self.ins[a].at[block, pl.ds(row0, rows)]

    def _local(self, a):
        return pltpu.make_async_copy(self._src(a, self.me), self.outs[a].at[self.me], self.local_sems.at[a])

    def _send(self, a, r):
        px, py, pc = self._peer(r)
        return pltpu.make_async_remote_copy(
            src_ref=self._src(a, 4 * px + 2 * py + pc), dst_ref=self.outs[a].at[self.me],
            send_sem=self.send_sems.at[7 * a + r - 1], recv_sem=self.recv_sems.at[7 * a + r - 1],
            device_id=(px, py, pc), device_id_type=_MESH)

    def _recv(self, a, r):
        px, py, pc = self._peer(r)
        return pltpu.make_async_remote_copy(
            src_ref=self._src(a, self.me), dst_ref=self.outs[a].at[4 * px + 2 * py + pc],
            send_sem=self.send_sems.at[7 * a + r - 1], recv_sem=self.recv_sems.at[7 * a + r - 1],
            device_id=(px, py, pc), device_id_type=_MESH)

    def start(self):
        for a in range(self.na):
            self._local(a).start()
        for r in range(1, N_DEV):
            for a in range(self.na):
                self._send(a, r).start()

    def finish(self):
        for r in range(1, N_DEV):
            for a in range(self.na):
                self._recv(a, r).wait_recv()
        for r in range(1, N_DEV):
            for a in range(self.na):
                self._send(a, r).wait_send()
        for a in range(self.na):
            self._local(a).wait()


def _prologue(x, tgt, small_l, w_in_l, cast_f32):
    seq = x.shape[0]
    nx = seq // TM
    rest_rows = seq - nx * TM
    nt = nx + 1
    nc = len(cast_f32)
    body_rows = TM - N_META
    assert nx >= 1 and rest_rows % 8 == 0 and rest_rows <= body_rows
    x_rest, t_rest = x[nx * TM:], tgt[nx * TM:]

    def last_tile_body(rest_ref):
        parts = ([rest_ref[...]] if rest_rows else []) + (
            [jnp.zeros((body_rows - rest_rows, D), _F32)] if body_rows > rest_rows else [])
        return parts[0] if len(parts) == 1 else jnp.concatenate(parts, axis=0)

    def body(xm_ref, xp_ref, tm_ref, tp_ref, *rest):
        if rest_rows:
            xr_ref, tr_ref, rest = rest[0], rest[1], rest[2:]
        else:
            xr_ref = tr_ref = None
        s_ref, w_ref, rest = rest[0], rest[1], rest[2:]
        cins = rest[:nc]
        h0_ref, tgt_ref, small_ref, wg_ref = rest[nc:nc + 4]
        couts = rest[nc + 4:2 * nc + 4]
        s_stage, w_stage, meta, msem = rest[2 * nc + 4:2 * nc + 8]
        g_s = _Gather([s_stage], [small_ref], rest[2 * nc + 8:2 * nc + 11])
        g_w = _Gather([w_stage], [wg_ref], rest[2 * nc + 11:], place=_pair_place)
        s = pl.program_id(0)
        i = (s + 1) % nt

        @pl.when(s == 0)
        def _():
            s_stage[...] = s_ref[...]
            w_stage[...] = w_ref[...].astype(_BF)
            g_s.start()
            g_w.start()
            meta[...] = jnp.zeros_like(meta)
            for a in range(nc):
                couts[a][...] = cins[a][...].astype(_BF)

        @pl.when(s == nt - 1)
        def _():
            for j in range(3):
                g_s.forward(j)
            g_s.finish()
            cps = [pltpu.make_async_copy(small_ref.at[k, pl.ds(0, N_META), :], meta.at[:, pl.ds(128 * k, 128)],
                                         msem.at[k]) for k in range(N_DEV)]
            for cp in cps:
                cp.start()
            for cp in cps:
                cp.wait()
            for j in range(3):
                g_w.forward(j)
            g_w.finish()

        has_x = i < nx
        h0_ref[pl.ds(0, N_META), :] = jnp.where(i == 0, meta[...], xp_ref[...])
        h0_ref[pl.ds(N_META, body_rows), :] = jnp.where(has_x, xm_ref[pl.ds(0, body_rows), :], last_tile_body(xr_ref))
        tgt_ref[pl.ds(0, N_META), :] = jnp.where(i == 0, 0.0, tp_ref[...])
        tgt_ref[pl.ds(N_META, body_rows), :] = jnp.where(has_x, tm_ref[pl.ds(0, body_rows), :], last_tile_body(tr_ref))

    def tile_of(s):
        return (s + 1) % nt

    hbm = pl.BlockSpec(memory_space=pl.ANY)
    main = pl.BlockSpec((TM, D), lambda s: (jnp.minimum(tile_of(s), nx - 1), 0))
    prev = pl.BlockSpec((N_META, D), lambda s: (jnp.maximum(tile_of(s) * (TM // N_META) - 1, 0), 0))
    tile = pl.BlockSpec((TM, D), lambda s: (tile_of(s), 0))
    rests = [x_rest, t_rest] if rest_rows else []
    return pl.pallas_call(
        body, name="prologue", grid=(nt,),
        in_specs=[main, prev, main, prev] + [_const(r.shape) for r in rests]
        + [_const(small_l.shape), _const(w_in_l.shape)] + [_const(l.shape) for l in cast_f32],
        out_specs=[tile, tile, hbm, hbm] + [_full(l.shape) for l in cast_f32],
        out_shape=[_S((nt * TM, D), _F32), _S((nt * TM, D), _F32), _S((N_DEV,) + small_l.shape, _F32),
                   _S((4, D, WIN_P), _BF)] + [_S(l.shape, _BF) for l in cast_f32],
        scratch_shapes=[pltpu.VMEM(small_l.shape, _F32), pltpu.VMEM(w_in_l.shape, _BF), pltpu.VMEM((N_META, D), _F32),
                        pltpu.SemaphoreType.DMA((N_DEV,))] + _sem_shapes(1) + _sem_shapes(1),
        compiler_params=_cp(("arbitrary",)),
    )(x, x, tgt, tgt, *rests, small_l, w_in_l, *cast_f32)


def _adamw_math(w, g, m, v):
    m2 = ADAM_B1 * m + (1.0 - ADAM_B1) * g
    v2 = ADAM_B2 * v + (1.0 - ADAM_B2) * (g * g)
    m_hat = m2 / (1.0 - ADAM_B1 ** ADAM_STEP)
    v_hat = v2 / (1.0 - ADAM_B2 ** ADAM_STEP)
    delta = -ADAM_LR * (m_hat / (jnp.sqrt(v_hat) + ADAM_EPS) + ADAM_WD * w)
    return delta, m2, v2


def _adamw_big(name, recv, w, m, v, rows):
    r_all, c_all = w.shape

    def body(r_ref, w_ref, m_ref, v_ref, g_out, d_out, m_out, v_out):
        g = r_ref[0].astype(_F32)
        for k in range(1, N_DEV):
            g = g + r_ref[k].astype(_F32)
        delta, m2, v2 = _adamw_math(w_ref[...], g, m_ref[...], v_ref[...])
        g_out[...] = g
        d_out[...] = delta
        m_out[...] = m2
        v_out[...] = v2

    tile = pl.BlockSpec((rows, c_all), lambda i: (i, 0))
    return pl.pallas_call(
        body, name=name, grid=(r_all // rows,),
        in_specs=[pl.BlockSpec((N_DEV, rows, c_all), lambda i: (0, i, 0)), tile, tile, tile],
        out_specs=[tile] * 4,
        out_shape=[_S(w.shape, _F32)] * 4,
        compiler_params=_cp(("arbitrary",)),
    )(recv, w, m, v)


def _adamw_small(gathered, slices, wmv):
    ng, npar = len(gathered), len(slices)

    def body(*refs):
        g_refs = refs[:ng]
        wmv_refs = refs[ng:ng + 3 * npar]
        outs = refs[ng + 3 * npar:]
        for i, (ai, r0, nr, c0, ncol) in enumerate(slices):
            g = g_refs[ai][0, pl.ds(r0, nr), pl.ds(c0, ncol)].astype(_F32)
            for k in range(1, N_DEV):
                g = g + g_refs[ai][k, pl.ds(r0, nr), pl.ds(c0, ncol)].astype(_F32)
            w_ref, m_ref, v_ref = wmv_refs[3 * i:3 * i + 3]
            delta, m2, v2 = _adamw_math(w_ref[...], g, m_ref[...], v_ref[...])
            outs[4 * i][...] = g
            outs[4 * i + 1][...] = delta
            outs[4 * i + 2][...] = m2
            outs[4 * i + 3][...] = v2
        total = g_refs[0][0, pl.ds(R_LOSS, 1), pl.ds(0, 128)]
        for k in range(1, N_DEV):
            total = total + g_refs[0][k, pl.ds(R_LOSS, 1), pl.ds(0, 128)]
        outs[4 * npar][...] = total

    flat = [t for trip in wmv for t in trip]
    out_shape = []
    for w, _, _ in wmv:
        out_shape += [_S(w.shape, _F32)] * 4
    out_shape.append(_S((1, 128), _F32))
    return pl.pallas_call(
        body, name="adamw_small", out_shape=out_shape,
        compiler_params=pltpu.CompilerParams(vmem_limit_bytes=VMEM_LIMIT),
    )(*gathered, *flat)


def _block_diag(w):
    eye = jnp.eye(8, dtype=w.dtype)
    return (w[:, :, None, :] * eye[:, None, :, None]).reshape(D_RG, D_RG)


def _diag_blocks(g):
    return jnp.concatenate([g[64 * h:64 * (h + 1), 64 * h:64 * (h + 1)] for h in range(8)], axis=0)


def _local_step(h0, tgt_p, n_valid, g_mix, w_in, vec, wr, wi, hb, g_hg, w_out_l, g_ffn, w_gu_l, w_down_l, g_fin):
    t_pad = h0.shape[0]
    me = 4 * lax.axis_index("x") + 2 * lax.axis_index("y") + lax.axis_index("c")
    p, u, y, hs, o, sc, w_out, w_gu, w_down = _mixer_fwd(h0, g_mix, w_in, wr, wi, vec, hb, g_hg,
                                                         [w_out_l, w_gu_l, w_down_l])
    w_out = w_out.reshape(D, D)
    w_down = w_down.reshape(4, FFB, D)
    h1, v, gu, act, dh2, dh2b, loss, gfin = _ffn_loss(h0, y, w_out, g_ffn, w_gu, w_down, g_fin, tgt_p, n_valid)

    g_wdown = _wgrad("wgrad_down", act, dh2b, pl.BlockSpec((1, t_pad, FFB), lambda j: (j, 0, 0)),
                     pl.BlockSpec((t_pad, D), lambda j: (0, 0)), 4, (FFB, D))
    g_wdown = g_wdown.reshape(N_DEV, D_FF // N_DEV, D)
    dgu, dh1, dh1b, dy, gffn, r_wdown = _ffn_bwd(dh2, dh2b, gu, h1, g_ffn, w_gu, w_down, w_out, [g_wdown])
    g_wgu = _wgrad("wgrad_gate_up", dgu, v, pl.BlockSpec((1, t_pad, FFB), lambda j: (j, 0, 0)),
                   pl.BlockSpec((t_pad, D), lambda j: (0, 0)), N_DEV, (FFB, D))
    g_wout = _wgrad("wgrad_out", y, dh1b, pl.BlockSpec((t_pad, D // N_DEV), lambda j: (0, j)),
                    pl.BlockSpec((t_pad, D), lambda j: (0, 0)), N_DEV, (D // N_DEV, D))
    dp, gvec, gw, r_wgu, r_wout = _mixer_bwd(p, hs, o, sc, dy, wr, wi, vec, hb, g_hg, [g_wgu, g_wout], [None, None])
    pack_c = jnp.concatenate([_diag_blocks(gw[0]), _diag_blocks(gw[1])], axis=1).astype(_BF)
    order = (me ^ jnp.array(_SEND_ORDER, jnp.int32)).astype(jnp.int32)
    dh0, r_win, all_b, all_c, r_meta, all_a = _inproj_bwd_send(dp, w_in, h0, dh1, g_mix, u, order, gffn, gfin, loss,
                                                               [gvec, pack_c])
    return dh0, (r_win, r_wgu, r_wout, r_wdown), (all_a, all_b, all_c, r_meta)


def kernel(x, meta_tokens, mix_norm_g, w_in, conv_w, conv_b, w_rgate, b_rgate, w_igate, b_igate, lru_lambda, rg_norm_g, hg_lower_bound, hg_norm_g, w_out, ffn_norm_g, w_gate_up, w_down, final_norm_g, loss_target, m_meta_tokens, m_mix_norm_g, m_w_in, m_conv_w, m_conv_b, m_w_rgate, m_b_rgate, m_w_igate, m_b_igate, m_lru_lambda, m_rg_norm_g, m_hg_lower_bound, m_hg_norm_g, m_w_out, m_ffn_norm_g, m_w_gate_up, m_w_down, m_final_norm_g, v_meta_tokens, v_mix_norm_g, v_w_in, v_conv_w, v_conv_b, v_w_rgate, v_b_rgate, v_w_igate, v_b_igate, v_lru_lambda, v_rg_norm_g, v_hg_lower_bound, v_hg_norm_g, v_w_out, v_ffn_norm_g, v_w_gate_up, v_w_down, v_final_norm_g):
    seq = x.shape[1]
    me = 4 * lax.axis_index("x") + 2 * lax.axis_index("y") + lax.axis_index("c")

    n_valid = N_META + seq
    small_l = jnp.concatenate([meta_tokens, jnp.pad(conv_w[0], ((0, 4), (0, 64)))], axis=0)
    h0, tgt_p, small_g, w_in_g, w_gu_l, w_out_l, w_down_l = _prologue(
        x[0], loss_target[0], small_l, w_in[0], [w_gate_up[0].T, w_out[0], w_down[0]])
    conv_w_full = jnp.transpose(small_g[:, N_META:N_META + 4, :64], (1, 0, 2)).reshape(4, D_RG)
    vec = jnp.concatenate([conv_b, b_rgate, b_igate, lru_lambda, rg_norm_g, jnp.zeros((3, D_RG), _F32),
                           conv_w_full, jnp.zeros((4, D_RG), _F32)], axis=0)
    wr = _block_diag(w_rgate[0]).astype(_BF)
    wi = _block_diag(w_igate[0]).astype(_BF)

    dh0, (r_win, r_wgu, r_wout, r_wdown), (all_a, all_b, all_c, meta_part) = _local_step(
        h0, tgt_p, n_valid, mix_norm_g, w_in_g, vec, wr, wi, hg_lower_bound, hg_norm_g,
        w_out_l, ffn_norm_g, w_gu_l, w_down_l, final_norm_g.reshape(1, D))
    grad_x = dh0[N_META:N_META + seq][None]

    outs = {}
    outs["w_in"] = _adamw_big("adamw_w_in", r_win, w_in[0], m_w_in[0], v_w_in[0], 256)
    outs["w_gate_up"] = [r.T for r in _adamw_big("adamw_w_gate_up", r_wgu, w_gate_up[0].T, m_w_gate_up[0].T,
                                                 v_w_gate_up[0].T, 176)]
    outs["w_out"] = _adamw_big("adamw_w_out", r_wout, w_out[0], m_w_out[0], v_w_out[0], 128)
    outs["w_down"] = _adamw_big("adamw_w_down", r_wdown, w_down[0], m_w_down[0], v_w_down[0], 176)

    convw_part = lax.dynamic_slice_in_dim(all_b[:, R_CONVW:R_CONVW + 4, :], me * 64, 64, axis=2)
    gathered = [all_a, all_b, all_c, meta_part, convw_part]
    small_params = [
        ("meta_tokens", (3, 0, N_META, 0, 128), (meta_tokens, m_meta_tokens, v_meta_tokens), (N_META, 128)),
        ("mix_norm_g", (0, R_GMIX, 1, 0, D), (mix_norm_g, m_mix_norm_g, v_mix_norm_g), (1, D)),
        ("conv_w", (4, 0, 4, 0, 64), (conv_w, m_conv_w, v_conv_w), (4, 64)),
        ("conv_b", (1, R_CONVB, 1, 0, D_RG), (conv_b, m_conv_b, v_conv_b), (1, D_RG)),
        ("w_rgate", (2, 0, 512, 0, 64), (w_rgate, m_w_rgate, v_w_rgate), (512, 64)),
        ("b_rgate", (1, R_BR, 1, 0, D_RG), (b_rgate, m_b_rgate, v_b_rgate), (1, D_RG)),
        ("w_igate", (2, 0, 512, 64, 64), (w_igate, m_w_igate, v_w_igate), (512, 64)),
        ("b_igate", (1, R_BI, 1, 0, D_RG), (b_igate, m_b_igate, v_b_igate), (1, D_RG)),
        ("lru_lambda", (1, R_LAM, 1, 0, D_RG), (lru_lambda, m_lru_lambda, v_lru_lambda), (1, D_RG)),
        ("rg_norm_g", (1, R_GRG, 1, 0, D_RG), (rg_norm_g, m_rg_norm_g, v_rg_norm_g), (1, D_RG)),
        ("hg_lower_bound", (1, R_HB0, 2, 0, D_HG), (hg_lower_bound, m_hg_lower_bound, v_hg_lower_bound), (2, D_HG)),
        ("hg_norm_g", (1, R_GHG, 1, 0, HD), (hg_norm_g, m_hg_norm_g, v_hg_norm_g), (1, HD)),
        ("ffn_norm_g", (0, R_GFFN, 1, 0, D), (ffn_norm_g, m_ffn_norm_g, v_ffn_norm_g), (1, D)),
        ("final_norm_g", (0, R_GFIN, 1, 0, D), (final_norm_g, m_final_norm_g, v_final_norm_g), (1, D)),
    ]
    res = _adamw_small(gathered, [s[1] for s in small_params],
                       [tuple(t.reshape(s[3]) for t in s[2]) for s in small_params])
    for i, s in enumerate(small_params):
        outs[s[0]] = [r.reshape(s[2][0].shape) for r in res[4 * i:4 * i + 4]]
    for n, ref in (("w_in", w_in), ("w_gate_up", w_gate_up), ("w_out", w_out), ("w_down", w_down)):
        outs[n] = [r.reshape(ref.shape) for r in outs[n]]

    loss_all = res[4 * len(small_params)][0, 0]
    order = ["meta_tokens", "mix_norm_g", "w_in", "conv_w", "conv_b", "w_rgate", "b_rgate", "w_igate", "b_igate",
             "lru_lambda", "rg_norm_g", "hg_lower_bound", "hg_norm_g", "w_out", "ffn_norm_g", "w_gate_up", "w_down",
             "final_norm_g"]
    return (loss_all, grad_x, *[outs[n][0] for n in order], *[outs[n][1] for n in order],
            *[outs[n][2] for n in order], *[outs[n][3] for n in order])
```

```python
import functools

import jax
import jax.numpy as jnp
from jax import lax
from jax.experimental import pallas as pl
from jax.experimental.pallas import tpu as pltpu

_BF = jnp.bfloat16
_F32 = jnp.float32
_S = jax.ShapeDtypeStruct
_MESH = pl.DeviceIdType.MESH

N_DEV = 8
N_META = 16
D = 1024
D_RG = 512
D_HG = 512
HD = 128
NH = D_HG // HD
D_IN = 3072
D_FF = 2816
FFB = D_FF // 4
WIN_B = D_IN // N_DEV
WIN_P = 2 * WIN_B
EPS = 1e-6
LRU_C = 8.0
TM = 320
HC = 64
VMEM_LIMIT = 62 * 1024 * 1024

ADAM_LR = 0.001
ADAM_B1 = 0.9
ADAM_B2 = 0.999
ADAM_EPS = 1e-08
ADAM_WD = 0.01
ADAM_STEP = 10

_SEND_ORDER = (6, 4, 2, 7, 5, 3, 1, 0)

R_CONVB, R_BR, R_BI, R_LAM, R_GRG, R_HB0, R_HB1, R_GHG, R_CONVW = 0, 1, 2, 3, 4, 5, 6, 7, 8
R_GMIX, R_GFFN, R_GFIN, R_LOSS = 0, 1, 2, 3


def _cp(sem=None, **kw):
    return pltpu.CompilerParams(dimension_semantics=sem, vmem_limit_bytes=VMEM_LIMIT, **kw)


def _dot(a, b):
    return jnp.dot(a, b, preferred_element_type=_F32)


def _dot_nt(a, b):
    return lax.dot_general(a, b, (((1,), (1,)), ((), ())), preferred_element_type=_F32)


def _dot_tn(a, b):
    return lax.dot_general(a, b, (((0,), (0,)), ((), ())), preferred_element_type=_F32)


def _sigmoid(x):
    return 0.5 * jnp.tanh(0.5 * x) + 0.5


def _dsilu(x, s):
    return s * (1.0 + x * (1.0 - s))


_GELU_C = 0.7978845608028654


def _gelu_parts(x):
    t = jnp.tanh(_GELU_C * (x + 0.044715 * (x * x * x)))
    g = 0.5 * x * (1.0 + t)
    dg = 0.5 * (1.0 + t) + 0.5 * x * (1.0 - t * t) * (_GELU_C * (1.0 + 3.0 * 0.044715 * (x * x)))
    return g, dg


def _softplus(z):
    e = jnp.exp(-jnp.abs(z))
    w = 1.0 + e
    l1p = jnp.where(w == 1.0, e, jnp.log(w) * e / jnp.where(w == 1.0, 1.0, w - 1.0))
    return jnp.maximum(z, 0.0) + l1p


def _rms_fwd(x):
    r = lax.rsqrt(jnp.mean(x * x, axis=-1, keepdims=True) + EPS)
    return x * r, r


def _rms_bwd(dyg, n, r):
    return r * (dyg - n * jnp.mean(dyg * n, axis=-1, keepdims=True))


def _full(shape):
    nd = len(shape)
    return pl.BlockSpec(shape, lambda i: (0,) * nd)


def _const(shape):
    nd = len(shape)
    return pl.BlockSpec(shape, lambda i: (0,) * nd, pipeline_mode=pl.Buffered(1))


def _carry_gather(gather, i, nt, early=0):
    @pl.when(i == 0)
    def _():
        gather.start()

    def tail():
        for j in range(3):
            if early:
                @pl.when(i == min(nt // 3 + j, nt - 1))
                def _(j=j):
                    gather.forward(j, range(early))

            @pl.when(i == max(nt - 4 + j, 0))
            def _(j=j):
                gather.forward(j, range(early, gather.na))

        @pl.when(i == nt - 1)
        def _():
            gather.finish()

    return tail


def _pair_place(ref, block):
    return ref.at[block // 2, :, pl.ds(pl.multiple_of((block % 2) * WIN_B, WIN_B), WIN_B)]


def _rg_gates(xc, wr_ref, wi_ref, vec_ref):
    xcb = xc.astype(_BF)
    r = _sigmoid(_dot(xcb, wr_ref[...]) + vec_ref[R_BR:R_BR + 1, :])
    ig = _sigmoid(_dot(xcb, wi_ref[...]) + vec_ref[R_BI:R_BI + 1, :])
    nsp8 = -LRU_C * _softplus(-vec_ref[R_LAM:R_LAM + 1, :])
    la = nsp8 * r
    a = jnp.exp(la)
    th = jnp.tanh(la)
    s = jnp.sqrt(-2.0 * th / (1.0 - th))
    return r, ig, a, s, nsp8


def _conv(xbuf, vec_ref):
    acc = vec_ref[R_CONVW:R_CONVW + 1, :] * xbuf[pl.ds(5, TM), :]
    for j in range(1, 4):
        acc = acc + vec_ref[R_CONVW + j:R_CONVW + j + 1, :] * xbuf[pl.ds(5 + j, TM), :]
    return vec_ref[R_CONVB:R_CONVB + 1, :] + acc


def _dot3(m01, x):
    hi = x.astype(_BF)
    r1 = x - hi.astype(_F32)
    mid = r1.astype(_BF)
    lo = (r1 - mid.astype(_F32)).astype(_BF)
    return (_dot(m01, lo) + _dot(m01, mid)) + _dot(m01, hi)


def _chunk_dot3(m01, x):
    return jnp.concatenate([_dot3(m01, x[HC * c:HC * (c + 1), :]) for c in range(x.shape[0] // HC)], axis=0)


def _chunk_masks():
    row = lax.broadcasted_iota(jnp.int32, (HC, HC), 0)
    col = lax.broadcasted_iota(jnp.int32, (HC, HC), 1)
    return (row >= col).astype(_BF), (col >= row).astype(_BF), jnp.ones((HC, HC), _BF)


def _per_chunk_rows(x, r):
    return jnp.concatenate([jnp.broadcast_to(x[HC * c + r:HC * c + r + 1, :], (HC, x.shape[1]))
                            for c in range(TM // HC)], axis=0)


def _hg_prep(p_ref, lb, tri):
    hq = p_ref[:, pl.ds(2 * D_RG, D_HG)]
    hf = p_ref[:, pl.ds(2 * D_RG + D_HG, D_HG)]
    sq = _sigmoid(hq)
    q = hq * sq
    sg = _sigmoid(hf)
    f = lb + (1.0 - lb) * sg
    k = 1.0 - f
    b = _chunk_dot3(tri, jnp.log(f))
    bm = _per_chunk_rows(b, HC // 2 - 1)
    bl = _per_chunk_rows(b, HC - 1)
    e_q = jnp.exp(b - bm)
    e_k = jnp.exp(bm - b)
    e_b = jnp.exp(b)
    e_l = jnp.exp(bl - b)
    return dict(hq=hq, sq=sq, q=q, sg=sg, f=f, k=k, e_q=e_q, e_k=e_k, e_b=e_b, e_l=e_l,
                qd=q * e_q, kd=k * e_k, qe=q * e_b, ke=k * e_l, e_end=jnp.exp(bl))


def _mixer_fwd(h0, g_mix, w_in, wr, wi, vec, hb, g_hg, shards):
    t_pad = h0.shape[0]
    nt = t_pad // TM
    nc_t = TM // HC
    nsh = len(shards)

    def body(h_ref, gmix_ref, win_ref, wr_ref, wi_ref, vec_ref, hb_ref, ghg_ref, *rest):
        sh_refs, rest = rest[:nsh], rest[nsh:]
        pout_ref, uout_ref, y_ref, hs_ref, o_ref, sc_ref = rest[:6]
        gath_refs, rest = rest[6:6 + nsh], rest[6 + nsh:]
        xbuf, a_s, b_s, hcar, st, qd_s, kd_s, qe_s, ke_s, v_s, u_s, p_s, p_ref = rest[:13]
        i = pl.program_id(0)
        tail = _carry_gather(_Gather(sh_refs, gath_refs, rest[13:]), i, nt + 1, early=1)

        @pl.when(i == 0)
        def _():
            p_s[...] = jnp.zeros_like(p_s)

        p_ref[...] = p_s[...]

        @pl.when(i <= 1)
        def _():
            xbuf[pl.ds(0, 8), :] = jnp.zeros((8, D_RG), _F32)
            hcar[...] = jnp.zeros_like(hcar)
            st[...] = jnp.zeros_like(st)

        n_h, _ = _rms_fwd(h_ref[...])
        u = (n_h * gmix_ref[...]).astype(_BF)
        uout_ref[...] = u
        pieces = [(j, k) for j in range(4) for k in range(WIN_P // 256)]

        def project(count):
            for _ in range(count):
                j, k = pieces.pop(0)
                blk = _dot(u, win_ref[j, :, pl.ds(256 * k, 256)])
                p_s[:, pl.ds(WIN_P * j + 256 * k, 256)] = blk
                pout_ref[:, pl.ds(WIN_P * j + 256 * k, 256)] = blk

        x = p_ref[:, pl.ds(0, D_RG)]
        xbuf[pl.ds(8, TM), :] = x
        xc = _conv(xbuf, vec_ref)
        xbuf[pl.ds(0, 8), :] = x[TM - 8:, :]
        r, ig, a, s, _ = _rg_gates(xc, wr_ref, wi_ref, vec_ref)
        a_s[...] = a
        b_s[...] = s * (ig * xc)

        def step(t, h):
            h = a_s[pl.ds(t, 1), :] * h + b_s[pl.ds(t, 1), :]
            hs_ref[pl.ds(t, 1), :] = h
            return h

        hcar[pl.ds(0, 1), :] = lax.fori_loop(0, TM, step, hcar[pl.ds(0, 1), :], unroll=8)
        gel, _ = _gelu_parts(p_ref[:, pl.ds(D_RG, D_RG)])
        n, _ = _rms_fwd(gel * hs_ref[...])
        y_ref[:, pl.ds(0, D_RG)] = (n * vec_ref[R_GRG:R_GRG + 1, :]).astype(_BF)

        lb = _sigmoid(hb_ref[0:1, :] - hb_ref[1:2, :])
        tri, _, _ = _chunk_masks()
        q = _hg_prep(p_ref, lb, tri)
        for name, ref in (("qd", qd_s), ("kd", kd_s), ("qe", qe_s), ("ke", ke_s)):
            ref[...] = q[name].astype(_BF)
        v_s[...] = p_ref[:, pl.ds(2 * D_RG + 2 * D_HG, D_HG)].astype(_BF)
        e_end = q["e_end"]
        causal = (lax.broadcasted_iota(jnp.int32, (HC, HC), 0) >= lax.broadcasted_iota(jnp.int32, (HC, HC), 1))
        for c in range(nc_t):
            for h in range(NH):
                rs, cs = pl.ds(HC * c, HC), pl.ds(HD * h, HD)
                amat = jnp.where(causal, _dot_nt(qd_s[rs, cs], kd_s[rs, cs]), 0.0)
                o_ref[rs, cs] = _dot(amat.astype(_BF), v_s[rs, cs])
                u_s[NH * c + h] = _dot_tn(v_s[rs, cs], ke_s[rs, cs])
                if pieces:
                    project(1)
        assert not pieces
        for h in range(NH):
            cs = pl.ds(HD * h, HD)
            s_run = st[h]
            for c in range(nc_t):
                rs = pl.ds(HC * c, HC)
                sc_ref[c, h] = s_run
                o_ref[rs, cs] += _dot_nt(qe_s[rs, cs], s_run.astype(_BF))
                s_run = e_end[HC * c:HC * c + 1, HD * h:HD * (h + 1)] * s_run + u_s[NH * c + h]
            st[h] = s_run
        for h in range(NH):
            cs = pl.ds(HD * h, HD)
            n_o, _ = _rms_fwd(o_ref[:, cs])
            hg = p_ref[:, pl.ds(2 * D_RG + 3 * D_HG + HD * h, HD)]
            y_ref[:, pl.ds(D_RG + HD * h, HD)] = ((n_o * ghg_ref[...]) * (hg * _sigmoid(hg))).astype(_BF)

        tail()

    hbm = pl.BlockSpec(memory_space=pl.ANY)

    def proj(i):
        return jnp.minimum(i, nt - 1)

    def mixed(i):
        return jnp.maximum(i - 1, 0)

    return pl.pallas_call(
        body, name="mixer_fwd", grid=(nt + 1,),
        in_specs=[pl.BlockSpec((TM, D), lambda i: (proj(i), 0)), _full((1, D)), _const((4, D, WIN_P)),
                  _full((D_RG, D_RG)), _full((D_RG, D_RG)),
                  _full((16, D_RG)), _full((2, D_HG)), _full((1, HD))] + [hbm] * nsh,
        out_specs=[pl.BlockSpec((TM, D_IN), lambda i: (proj(i), 0)), pl.BlockSpec((TM, D), lambda i: (proj(i), 0)),
                   pl.BlockSpec((TM, D), lambda i: (mixed(i), 0)), pl.BlockSpec((TM, D_RG), lambda i: (mixed(i), 0)),
                   pl.BlockSpec((TM, D_HG), lambda i: (mixed(i), 0)),
                   pl.BlockSpec((nc_t, NH, HD, HD), lambda i: (mixed(i), 0, 0, 0))] + [hbm] * nsh,
        out_shape=[_S((t_pad, D_IN), _F32), _S((t_pad, D), _BF),
                   _S((t_pad, D), _BF), _S((t_pad, D_RG), _F32), _S((t_pad, D_HG), _F32),
                   _S((t_pad // HC, NH, HD, HD), _F32)] + [_S((N_DEV,) + s.shape, s.dtype) for s in shards],
        scratch_shapes=[pltpu.VMEM((TM + 8, D_RG), _F32), pltpu.VMEM((TM, D_RG), _F32),
                        pltpu.VMEM((TM, D_RG), _F32), pltpu.VMEM((8, D_RG), _F32),
                        pltpu.VMEM((NH, HD, HD), _F32)] + [pltpu.VMEM((TM, D_HG), _BF) for _ in range(5)]
        + [pltpu.VMEM((nc_t * NH, HD, HD), _F32), pltpu.VMEM((TM, D_IN), _F32), pltpu.VMEM((TM, D_IN), _F32)]
        + _sem_shapes(nsh),
        compiler_params=_cp(("arbitrary",)),
    )(h0, g_mix, w_in, wr, wi, vec, hb, g_hg, *shards)


def _ffn_loss(h0, y, w_out, g_ffn, w_gu, w_down, g_fin, tgt, n_valid):
    t_pad = h0.shape[0]

    def body(h_ref, y_ref, wo_ref, gffn_ref, wgu_ref, wd_ref, g_ref, t_ref,
             h1_ref, v_ref, gu_ref, act_ref, dh2_ref, dh2b_ref, loss_ref, gfin_ref):
        i = pl.program_id(0)

        @pl.when(i == 0)
        def _():
            loss_ref[...] = jnp.zeros_like(loss_ref)
            gfin_ref[...] = jnp.zeros_like(gfin_ref)

        h1 = h_ref[...] + _dot(y_ref[...], wo_ref[...])
        h1_ref[...] = h1
        n1, _ = _rms_fwd(h1)
        vb = (n1 * gffn_ref[...]).astype(_BF)
        v_ref[...] = vb
        h2 = h1
        for b in range(4):
            gate = _dot_nt(vb, wgu_ref[b])
            up = _dot_nt(vb, wgu_ref[4 + b])
            gu_ref[b] = gate
            gu_ref[4 + b] = up
            act = ((gate * _sigmoid(gate)) * up).astype(_BF)
            act_ref[b] = act
            h2 = h2 + _dot(act, wd_ref[b])
        n, r = _rms_fwd(h2)
        out = n * g_ref[...]
        row = i * TM + lax.broadcasted_iota(jnp.int32, (TM, 1), 0)
        valid = (row >= N_META) & (row < n_valid)
        err = jnp.where(valid, out - t_ref[...], 0.0)
        loss_ref[...] += (0.5 / D) * jnp.sum(err * err)
        dout = err * (1.0 / D)
        gfin_ref[...] += jnp.sum(dout * n, axis=0, keepdims=True)
        dh2 = _rms_bwd(dout * g_ref[...], n, r)
        dh2_ref[...] = dh2
        dh2b_ref[...] = dh2.astype(_BF)

    tile = pl.BlockSpec((TM, D), lambda i: (i, 0))
    return pl.pallas_call(
        body, name="ffn_loss", grid=(t_pad // TM,),
        in_specs=[tile, tile, _const((D, D)), _full((1, D)),
                  _const((N_DEV, FFB, D)), _const((4, FFB, D)), _full((1, D)), tile],
        out_specs=[tile, tile,
                   pl.BlockSpec((N_DEV, TM, FFB), lambda i: (0, i, 0)), pl.BlockSpec((4, TM, FFB), lambda i: (0, i, 0)),
                   tile, tile, _full((8, 128)), _full((1, D))],
        out_shape=[_S((t_pad, D), _F32), _S((t_pad, D), _BF),
                   _S((N_DEV, t_pad, FFB), _F32), _S((4, t_pad, FFB), _BF), _S((t_pad, D), _F32),
                   _S((t_pad, D), _BF), _S((8, 128), _F32), _S((1, D), _F32)],
        compiler_params=_cp(("arbitrary",)),
    )(h0, y, w_out, g_ffn, w_gu, w_down, g_fin, tgt)


def _ffn_bwd(dh2, dh2b, gu, h1, g_ffn, w_gu, w_down, w_out, scatter):
    t_pad = dh2.shape[0]
    nsc = len(scatter)
    nt = t_pad // TM

    def body(dh2_ref, dh2b_ref, gu_ref, h1_ref, g_ref, wgu_ref, wd_ref, wo_ref, *rest):
        dgu_ref, dh1_ref, dh1b_ref, dy_ref, gffn_ref = rest[nsc:nsc + 5]
        exchange = _Exchange(rest[:nsc], [], rest[nsc + 5:2 * nsc + 5], rest[2 * nsc + 5:])
        i = pl.program_id(0)

        @pl.when(i == 0)
        def _():
            exchange.start()
            gffn_ref[...] = jnp.zeros_like(gffn_ref)

        db = dh2b_ref[...]
        dv = jnp.zeros((TM, D), _F32)
        for b in range(4):
            dact = _dot_nt(db, wd_ref[b])
            gate = gu_ref[b]
            up = gu_ref[4 + b]
            sg = _sigmoid(gate)
            dgate = ((dact * up) * _dsilu(gate, sg)).astype(_BF)
            dup = (dact * (gate * sg)).astype(_BF)
            dgu_ref[b] = dgate
            dgu_ref[4 + b] = dup
            dv = dv + _dot(dgate, wgu_ref[b]) + _dot(dup, wgu_ref[4 + b])
        n, r = _rms_fwd(h1_ref[...])
        gffn_ref[...] += jnp.sum(dv * n, axis=0, keepdims=True)
        dh1 = dh2_ref[...] + _rms_bwd(dv * g_ref[...], n, r)
        dh1_ref[...] = dh1
        dh1b = dh1.astype(_BF)
        dh1b_ref[...] = dh1b
        dy_ref[...] = _dot_nt(dh1b, wo_ref[...])

        @pl.when(i == nt - 1)
        def _():
            exchange.finish()

    tile = pl.BlockSpec((TM, D), lambda i: (i, 0))
    hbm = pl.BlockSpec(memory_space=pl.ANY)
    return pl.pallas_call(
        body, name="ffn_bwd", grid=(nt,),
        in_specs=[tile, tile, pl.BlockSpec((N_DEV, TM, FFB), lambda i: (0, i, 0)), tile, _full((1, D)),
                  _const((N_DEV, FFB, D)), _const((4, FFB, D)), _const((D, D))] + [hbm] * nsc,
        out_specs=[pl.BlockSpec((N_DEV, TM, FFB), lambda i: (0, i, 0)), tile, tile, tile, _full((1, D))] + [hbm] * nsc,
        out_shape=[_S((N_DEV, t_pad, FFB), _BF), _S((t_pad, D), _F32), _S((t_pad, D), _BF),
                   _S((t_pad, D), _F32), _S((1, D), _F32)] + _recv_shapes(scatter, [None] * nsc),
        scratch_shapes=_sem_shapes(nsc),
        compiler_params=_cp(("arbitrary",)),
    )(dh2, dh2b, gu, h1, g_ffn, w_gu, w_down, w_out, *scatter)


def _mixer_bwd(p, hs, o, sc, dy, wr, wi, vec, hb, g_hg, scatter, windows):
    t_pad = p.shape[0]
    nt = t_pad // TM
    nc_t = TM // HC
    nsc = len(scatter)

    def rev(i):
        return nt - 1 - i

    def body(p_ref, pprev_ref, hs_ref, hprev_ref, o_ref, sc_ref, dy_ref, wr_ref, wi_ref, vec_ref, hb_ref, ghg_ref,
             *rest):
        send_refs, rest = rest[:nsc], rest[nsc:]
        dp_ref, gvec_ref, gw_ref = rest[:3]
        recv_refs, rest = rest[3:3 + nsc], rest[3 + nsc:]
        xbuf, hbuf, dbuf, a_s, g_s, ccar, dst = rest[:7]
        qd_s, kd_s, qe_s, ke_s, v_s, do_s, dqd_s, dkd_s, dqe_s, dke_s, dv_s, w_s, dend_s = rest[7:20]
        exchange = _Exchange(send_refs, [], recv_refs, rest[20:], windows)
        i = pl.program_id(0)
        first_tile = i == nt - 1

        @pl.when(i == 0)
        def _():
            exchange.start()
            gvec_ref[...] = jnp.zeros_like(gvec_ref)
            gw_ref[...] = jnp.zeros_like(gw_ref)
            dbuf[pl.ds(TM, 8), :] = jnp.zeros((8, D_RG), _F32)
            ccar[...] = jnp.zeros_like(ccar)
            dst[...] = jnp.zeros_like(dst)

        def acc(row, val):
            gvec_ref[row:row + 1, :] += jnp.sum(val, axis=0, keepdims=True)

        keep = jnp.where(first_tile, 0.0, 1.0)
        x = p_ref[:, pl.ds(0, D_RG)]
        xbuf[pl.ds(0, 8), :] = pprev_ref[...] * keep
        xbuf[pl.ds(8, TM), :] = x
        xc = _conv(xbuf, vec_ref)
        r, ig, a, s, nsp8 = _rg_gates(xc, wr_ref, wi_ref, vec_ref)
        h = hs_ref[...]
        hbuf[pl.ds(0, 8), :] = hprev_ref[...] * keep
        hbuf[pl.ds(8, TM), :] = h
        hm1 = hbuf[pl.ds(7, TM), :]
        gr = p_ref[:, pl.ds(D_RG, D_RG)]
        gel, dgel = _gelu_parts(gr)
        n, rr = _rms_fwd(gel * h)
        dyn = dy_ref[:, pl.ds(0, D_RG)]
        acc(R_GRG, dyn * n)
        dpre = _rms_bwd(dyn * vec_ref[R_GRG:R_GRG + 1, :], n, rr)
        dp_ref[:, pl.ds(D_RG, D_RG)] = ((dpre * h) * dgel).astype(_BF)
        a_s[...] = a
        g_s[...] = dpre * gel

        def step(k, c):
            t = TM - 1 - k
            g = g_s[pl.ds(t, 1), :] + c
            g_s[pl.ds(t, 1), :] = g
            return a_s[pl.ds(t, 1), :] * g

        ccar[pl.ds(0, 1), :] = lax.fori_loop(0, TM, step, ccar[pl.ds(0, 1), :], unroll=8)
        gt = g_s[...]
        da = gt * hm1
        ixc = ig * xc
        ds = gt * ixc
        dig = (gt * s) * xc
        dxc = (gt * s) * ig
        dla = da * a - ds * ((a * a) / s)
        lam = vec_ref[R_LAM:R_LAM + 1, :]
        gvec_ref[R_LAM:R_LAM + 1, :] += jnp.sum(dla * r, axis=0, keepdims=True) * (LRU_C * _sigmoid(-lam))
        dzr = (dla * nsp8) * (r * (1.0 - r))
        dzi = dig * (ig * (1.0 - ig))
        acc(R_BR, dzr)
        acc(R_BI, dzi)
        xcb = xc.astype(_BF)
        dzrb = dzr.astype(_BF)
        dzib = dzi.astype(_BF)
        gw_ref[0] += _dot_tn(xcb, dzrb)
        gw_ref[1] += _dot_tn(xcb, dzib)
        dxc = dxc + _dot_nt(dzrb, wr_ref[...]) + _dot_nt(dzib, wi_ref[...])
        acc(R_CONVB, dxc)
        for j in range(4):
            acc(R_CONVW + j, dxc * xbuf[pl.ds(5 + j, TM), :])
        dbuf[pl.ds(0, TM), :] = dxc
        dx = vec_ref[R_CONVW + 3:R_CONVW + 4, :] * dxc
        for j in range(3):
            dx = dx + vec_ref[R_CONVW + j:R_CONVW + j + 1, :] * dbuf[pl.ds(3 - j, TM), :]
        dbuf[pl.ds(TM, 8), :] = dxc[0:8, :]
        dp_ref[:, pl.ds(0, D_RG)] = dx.astype(_BF)

        lb = _sigmoid(hb_ref[0:1, :] - hb_ref[1:2, :])
        tri, tri_rev, ones = _chunk_masks()
        q = _hg_prep(p_ref, lb, tri)
        qdb, kdb = q["qd"].astype(_BF), q["kd"].astype(_BF)
        qd_s[...] = qdb
        kd_s[...] = kdb
        qe_s[...] = q["qe"].astype(_BF)
        ke_s[...] = q["ke"].astype(_BF)
        v_s[...] = p_ref[:, pl.ds(2 * D_RG + 2 * D_HG, D_HG)].astype(_BF)
        e_end = q["e_end"]
        ghg = ghg_ref[...]
        for h in range(NH):
            cs = pl.ds(HD * h, HD)
            hg = p_ref[:, pl.ds(2 * D_RG + 3 * D_HG + HD * h, HD)]
            sh = _sigmoid(hg)
            n_o, r_o = _rms_fwd(o_ref[:, cs])
            dyh = dy_ref[:, pl.ds(D_RG + HD * h, HD)]
            dp_ref[:, pl.ds(2 * D_RG + 3 * D_HG + HD * h, HD)] = ((dyh * (n_o * ghg)) * _dsilu(hg, sh)).astype(_BF)
            dn = dyh * (hg * sh)
            gvec_ref[R_GHG:R_GHG + 1, pl.ds(0, HD)] += jnp.sum(dn * n_o, axis=0, keepdims=True)
            do_s[:, cs] = _rms_bwd(dn * ghg, n_o, r_o).astype(_BF)
        causal = (lax.broadcasted_iota(jnp.int32, (HC, HC), 0) >= lax.broadcasted_iota(jnp.int32, (HC, HC), 1))
        for c in range(nc_t):
            for h in range(NH):
                rs, cs = pl.ds(HC * c, HC), pl.ds(HD * h, HD)
                qd_c, kd_c, do_c = qd_s[rs, cs], kd_s[rs, cs], do_s[rs, cs]
                amat = jnp.where(causal, _dot_nt(qd_c, kd_c), 0.0).astype(_BF)
                da_m = jnp.where(causal, _dot_nt(do_c, v_s[rs, cs]), 0.0).astype(_BF)
                dqd_s[rs, cs] = _dot(da_m, kd_c)
                dkd_s[rs, cs] = _dot_tn(da_m, qd_c)
                dqe_s[rs, cs] = _dot(do_c, sc_ref[c, h].astype(_BF))
                dv_s[rs, cs] = _dot_tn(amat, do_c)
                w_s[NH * c + h] = _dot_tn(do_c, qe_s[rs, cs])
        for h in range(NH):
            cs = pl.ds(HD * h, HD)
            d_run = dst[h]
            for c in reversed(range(nc_t)):
                rs = pl.ds(HC * c, HC)
                d_b = d_run.astype(_BF)
                dke_s[rs, cs] = _dot(v_s[rs, cs], d_b)
                dp_ref[rs, pl.ds(2 * D_RG + 2 * D_HG + HD * h, HD)] = (
                    dv_s[rs, cs] + _dot_nt(ke_s[rs, cs], d_b)).astype(_BF)
                dend_s[pl.ds(c, 1), cs] = jnp.sum(sc_ref[c, h] * d_run, axis=0, keepdims=True)
                d_run = w_s[NH * c + h] + e_end[HC * c:HC * c + 1, HD * h:HD * (h + 1)] * d_run
            dst[h] = d_run
        dqd, dkd, dqe, dke = dqd_s[...], dkd_s[...], dqe_s[...], dke_s[...]
        dq = dqd * q["e_q"] + dqe * q["e_b"]
        dk = dkd * q["e_k"] + dke * q["e_l"]
        dkeke = dke * q["ke"]
        db = dqd * qdb.astype(_F32) - dkd * kdb.astype(_F32) + dqe * q["qe"] - dkeke
        d_end = jnp.concatenate([jnp.broadcast_to(dend_s[pl.ds(c, 1), :], (HC, D_HG)) for c in range(nc_t)], axis=0)
        dlf = _chunk_dot3(tri_rev, db) + _chunk_dot3(ones, dkeke) + d_end * e_end
        df = dlf / q["f"] - dk
        sg = q["sg"]
        gvec_ref[R_HB0:R_HB0 + 1, :] += jnp.sum(df * (1.0 - sg), axis=0, keepdims=True)
        dp_ref[:, pl.ds(2 * D_RG, D_HG)] = (dq * _dsilu(q["hq"], q["sq"])).astype(_BF)
        dp_ref[:, pl.ds(2 * D_RG + D_HG, D_HG)] = ((df * (1.0 - lb)) * (sg * (1.0 - sg))).astype(_BF)

        @pl.when(i == nt - 1)
        def _():
            glb = gvec_ref[R_HB0:R_HB0 + 1, :] * (lb * (1.0 - lb))
            gvec_ref[R_HB0:R_HB0 + 1, :] = glb
            gvec_ref[R_HB1:R_HB1 + 1, :] = -glb
            exchange.finish()

    hbm = pl.BlockSpec(memory_space=pl.ANY)
    return pl.pallas_call(
        body, name="mixer_bwd", grid=(nt,),
        in_specs=[pl.BlockSpec((TM, D_IN), lambda i: (rev(i), 0)),
                  pl.BlockSpec((8, D_RG), lambda i: (jnp.maximum(rev(i) * (TM // 8) - 1, 0), 0)),
                  pl.BlockSpec((TM, D_RG), lambda i: (rev(i), 0)),
                  pl.BlockSpec((8, D_RG), lambda i: (jnp.maximum(rev(i) * (TM // 8) - 1, 0), 0)),
                  pl.BlockSpec((TM, D_HG), lambda i: (rev(i), 0)),
                  pl.BlockSpec((nc_t, NH, HD, HD), lambda i: (rev(i), 0, 0, 0)),
                  pl.BlockSpec((TM, D), lambda i: (rev(i), 0)),
                  _full((D_RG, D_RG)), _full((D_RG, D_RG)), _full((16, D_RG)), _full((2, D_HG)), _full((1, HD))]
        + [hbm] * nsc,
        out_specs=[pl.BlockSpec((TM, D_IN), lambda i: (rev(i), 0)), _full((16, D_RG)), _full((2, D_RG, D_RG))]
        + [hbm] * nsc,
        out_shape=[_S((t_pad, D_IN), _BF), _S((16, D_RG), _F32), _S((2, D_RG, D_RG), _F32)]
        + _recv_shapes(scatter, windows),
        scratch_shapes=[pltpu.VMEM((TM + 8, D_RG), _F32), pltpu.VMEM((TM + 8, D_RG), _F32),
                        pltpu.VMEM((TM + 8, D_RG), _F32), pltpu.VMEM((TM, D_RG), _F32),
                        pltpu.VMEM((TM, D_RG), _F32), pltpu.VMEM((8, D_RG), _F32),
                        pltpu.VMEM((NH, HD, HD), _F32)]
        + [pltpu.VMEM((TM, D_HG), _BF) for _ in range(6)] + [pltpu.VMEM((TM, D_HG), _F32) for _ in range(5)]
        + [pltpu.VMEM((nc_t * NH, HD, HD), _F32), pltpu.VMEM((8, D_HG), _F32)] + _sem_shapes(nsc),
        compiler_params=_cp(("arbitrary",)),
    )(p, p, hs, hs, o, sc, dy, wr, wi, vec, hb, g_hg, *scatter)


def _inproj_bwd_send(dp, w_in, h0, dh1, g_mix, u, order, gffn, gfin, loss, to_all):
    t_pad = dp.shape[0]
    rb = TM
    n_steps = N_DEV + t_pad // rb
    na = len(to_all)

    def body(order_ref, dpc_ref, dpr_ref, u_ref, w_ref, h_ref, dh1_ref, g_ref, gffn_ref, gfin_ref, loss_ref, *rest):
        all_in = rest[:na]
        dh0_ref, recv_ref = rest[na:na + 2]
        all_out = rest[na + 2:2 * na + 2]
        meta_ref, alla_ref = rest[2 * na + 2:2 * na + 4]
        buf, pack, meta, blk_send, blk_recv, blk_local = rest[2 * na + 4:2 * na + 10]
        exchange = _Exchange([], all_in, all_out, rest[2 * na + 10:2 * na + 13])
        last = _Exchange([meta], [pack], [meta_ref, alla_ref], rest[2 * na + 13:])
        s = pl.program_id(0)
        x, y, c = _coords()
        me = 4 * x + 2 * y + c

        def send(step):
            r = _SEND_ORDER[step]
            return pltpu.make_async_remote_copy(
                src_ref=buf.at[step], dst_ref=recv_ref.at[me], send_sem=blk_send.at[step], recv_sem=blk_recv.at[r - 1],
                device_id=(x ^ (r >> 2), y ^ ((r >> 1) & 1), c ^ (r & 1)), device_id_type=_MESH)

        @pl.when(s == 0)
        def _():
            exchange.start()
            pack[...] = jnp.zeros_like(pack)

        @pl.when(s < N_DEV)
        def _():
            buf[s] = _dot_tn(u_ref[...], dpc_ref[...]).astype(_BF)

            for step in range(N_DEV - 1):
                @pl.when(s == step)
                def _(step=step):
                    send(step).start()

        @pl.when(s >= N_DEV)
        def _():
            du = jnp.zeros((rb, D), _F32)
            for j in range(4):
                du = du + _dot_nt(dpr_ref[:, WIN_P * j:WIN_P * (j + 1)], w_ref[j])
            n, r = _rms_fwd(h_ref[...])
            pack[R_GMIX:R_GMIX + 1, :] += jnp.sum(du * n, axis=0, keepdims=True)
            dh0 = dh1_ref[...] + _rms_bwd(du * g_ref[...], n, r)
            dh0_ref[...] = dh0

            @pl.when(s == N_DEV)
            def _():
                for k in range(N_DEV):
                    meta[k] = dh0[0:N_META, 128 * k:128 * (k + 1)]

        @pl.when(s == n_steps - 1)
        def _():
            pack[R_GFFN:R_GFFN + 1, :] = gffn_ref[...]
            pack[R_GFIN:R_GFIN + 1, :] = gfin_ref[...]
            pack[R_LOSS:R_LOSS + 1, pl.ds(0, 128)] = loss_ref[0:1, :]
            last.start()
            mine = pltpu.make_async_copy(buf.at[N_DEV - 1], recv_ref.at[me], blk_local.at[0])
            mine.start()
            for step in range(N_DEV - 1):
                send(step).wait_send()
            for r in range(1, N_DEV):
                px, py, pc = x ^ (r >> 2), y ^ ((r >> 1) & 1), c ^ (r & 1)
                pltpu.make_async_remote_copy(
                    src_ref=buf.at[0], dst_ref=recv_ref.at[4 * px + 2 * py + pc], send_sem=blk_send.at[0],
                    recv_sem=blk_recv.at[r - 1], device_id=(px, py, pc), device_id_type=_MESH).wait_recv()
            mine.wait()
            exchange.finish()
            last.finish()

    hbm = pl.BlockSpec(memory_space=pl.ANY)
    rows = pl.BlockSpec((rb, D), lambda s, order: (jnp.maximum(s - N_DEV, 0), 0))
    one = pl.BlockSpec((1, D), lambda s, order: (0, 0))
    res = pl.pallas_call(
        body, name="inproj_bwd_send",
        grid_spec=pltpu.PrefetchScalarGridSpec(
            num_scalar_prefetch=1, grid=(n_steps,),
            in_specs=[pl.BlockSpec((t_pad, WIN_B), lambda s, order: (0, order[jnp.minimum(s, N_DEV - 1)])),
                      pl.BlockSpec((rb, D_IN), lambda s, order: (jnp.maximum(s - N_DEV, 0), 0)),
                      pl.BlockSpec((t_pad, D), lambda s, order: (0, 0), pipeline_mode=pl.Buffered(1)),
                      pl.BlockSpec((4, D, WIN_P), lambda s, order: (0, 0, 0), pipeline_mode=pl.Buffered(1)),
                      rows, rows, one, one, one, pl.BlockSpec((8, 128), lambda s, order: (0, 0))] + [hbm] * na,
            out_specs=[rows] + [hbm] * (na + 3),
            scratch_shapes=[pltpu.VMEM((N_DEV, D, WIN_B), _BF), pltpu.VMEM((8, D), _F32),
                            pltpu.VMEM((N_DEV, N_META, 128), _F32),
                            pltpu.SemaphoreType.DMA((N_DEV - 1,)), pltpu.SemaphoreType.DMA((N_DEV - 1,)),
                            pltpu.SemaphoreType.DMA((1,))] + _sem_shapes(na) + _sem_shapes(2)),
        out_shape=[_S((t_pad, D), _F32), _S((N_DEV, D, WIN_B), _BF)]
        + [_S((N_DEV,) + g.shape, g.dtype) for g in to_all]
        + [_S((N_DEV, N_META, 128), _F32), _S((N_DEV, 8, D), _F32)],
        compiler_params=_cp(("arbitrary",)),
    )(order, dp, dp, u, w_in, h0, dh1, g_mix, gffn, gfin, loss, *to_all)
    return res


def _recv_shapes(scatter, windows):
    return [_S(s.shape if w is None else (s.shape[0], w[1]) + s.shape[2:], s.dtype) for s, w in zip(scatter, windows)]


def _wgrad(name, a, b, a_spec, b_spec, n_blocks, out_block, scatter=(), windows=None):
    nsc = len(scatter)
    windows = windows if windows is not None else [None] * nsc

    def body(a_ref, b_ref, *rest):
        o_ref = rest[nsc]
        j = pl.program_id(0)
        if nsc:
            exchange = _Exchange(rest[:nsc], [], rest[nsc + 1:2 * nsc + 1], rest[2 * nsc + 1:], windows)

            @pl.when(j == 0)
            def _():
                exchange.start()

        av = a_ref[0] if len(a_ref.shape) == 3 else a_ref[...]
        bv = b_ref[0] if len(b_ref.shape) == 3 else b_ref[...]
        o_ref[0] = _dot_tn(av, bv).astype(_BF)

        if nsc:
            @pl.when(j == n_blocks - 1)
            def _():
                exchange.finish()

    hbm = pl.BlockSpec(memory_space=pl.ANY)
    res = pl.pallas_call(
        body, name=name, grid=(n_blocks,),
        in_specs=[a_spec, b_spec] + [hbm] * nsc,
        out_specs=[pl.BlockSpec((1,) + out_block, lambda j: (j, 0, 0))] + [hbm] * nsc,
        out_shape=[_S((n_blocks,) + out_block, _BF)] + _recv_shapes(scatter, windows),
        scratch_shapes=_sem_shapes(nsc) if nsc else [],
        compiler_params=_cp(("arbitrary",)),
    )(a, b, *scatter)
    return res if nsc else res[0]


def _coords():
    return lax.axis_index("x"), lax.axis_index("y"), lax.axis_index("c")


def _sem_shapes(na):
    return [pltpu.SemaphoreType.DMA((7 * na,)), pltpu.SemaphoreType.DMA((7 * na,)), pltpu.SemaphoreType.DMA((na,))]


class _Gather:
    def __init__(self, srcs, outs, sems, place=None):
        self.srcs, self.outs = srcs, outs
        self.send_sems, self.recv_sems, self.local_sems = sems
        self.place = place if place is not None else (lambda ref, block: ref.at[block])
        self.na = len(srcs)
        x, y, c = _coords()
        self.pos = (x, y, c)
        self.me = 4 * x + 2 * y + c
        self.sibling = (x, y, 1 - c)
        self.chips = [(1 - x, y), (x, 1 - y), (1 - x, 1 - y)]

    @staticmethod
    def _slot(px, py, pc):
        return 4 * px + 2 * py + pc

    def _copy(self, a, k, block, to, own=False):
        dst = self.place(self.outs[a], block)
        return pltpu.make_async_remote_copy(
            src_ref=self.srcs[a] if own else dst, dst_ref=dst,
            send_sem=self.send_sems.at[7 * a + k], recv_sem=self.recv_sems.at[7 * a + k],
            device_id=to, device_id_type=_MESH)

    def _mine(self, a):
        return pltpu.make_async_copy(self.srcs[a], self.place(self.outs[a], self.me), self.local_sems.at[a])

    def _first(self):
        c = self.pos[2]
        cps = []
        for a in range(self.na):
            cps.append(self._copy(a, 0, self.me, self.sibling, own=True))
            cps += [self._copy(a, 1 + j, self.me, (*chip, c), own=True) for j, chip in enumerate(self.chips)]
        return cps

    def _passed(self):
        c = self.pos[2]
        return [self._copy(a, 4 + j, self._slot(*chip, c), self.sibling)
                for j, chip in enumerate(self.chips) for a in range(self.na)]

    def start(self):
        for a in range(self.na):
            self._mine(a).start()
        for cp in self._first():
            cp.start()

    def forward(self, j, arrays=None):
        c = self.pos[2]
        chip = self.chips[j]
        for a in (range(self.na) if arrays is None else arrays):
            self._copy(a, 1 + j, self._slot(*chip, c), self.pos).wait_recv()
            self._copy(a, 4 + j, self._slot(*chip, c), self.sibling).start()

    def wait_sibling(self):
        x, y, c = self.pos
        for a in range(self.na):
            self._copy(a, 0, self._slot(x, y, 1 - c), self.pos).wait_recv()

    def wait_passed(self, j):
        c = self.pos[2]
        for a in range(self.na):
            self._copy(a, 4 + j, self._slot(*self.chips[j], 1 - c), self.pos).wait_recv()

    def finish_sends(self):
        for cp in self._first() + self._passed():
            cp.wait_send()
        for a in range(self.na):
            self._mine(a).wait()

    def finish(self):
        self.wait_sibling()
        for j in range(3):
            self.wait_passed(j)
        self.finish_sends()


class _Exchange:
    def __init__(self, scatter, gather, outs, sems, windows=None):
        self.windows = windows if windows is not None else [None] * len(scatter)
        self.ins = list(scatter) + list(gather)
        self.ns, self.na = len(scatter), len(scatter) + len(gather)
        self.outs = outs
        self.send_sems, self.recv_sems, self.local_sems = sems
        x, y, c = _coords()
        self.pos = (x, y, c)
        self.me = 4 * x + 2 * y + c

    def _peer(self, r):
        x, y, c = self.pos
        return x ^ (r >> 2), y ^ ((r >> 1) & 1), c ^ (r & 1)

    def _src(self, a, block):
        if a >= self.ns:
            return self.ins[a]
        if self.windows[a] is None:
            return self.ins[a].at[block]
        row0, rows = self.windows[a]
        return self.ins[a].at[block, pl.ds(row0, rows)]

    def _local(self, a):
        return pltpu.make_async_copy(self._src(a, self.me), self.outs[a].at[self.me], self.local_sems.at[a])

    def _send(self, a, r):
        px, py, pc = self._peer(r)
        return pltpu.make_async_remote_copy(
            src_ref=self._src(a, 4 * px + 2 * py + pc), dst_ref=self.outs[a].at[self.me],
            send_sem=self.send_sems.at[7 * a + r - 1], recv_sem=self.recv_sems.at[7 * a + r - 1],
            device_id=(px, py, pc), device_id_type=_MESH)

    def _recv(self, a, r):
        px, py, pc = self._peer(r)
        return pltpu.make_async_remote_copy(
            src_ref=self._src(a, self.me), dst_ref=self.outs[a].at[4 * px + 2 * py + pc],
            send_sem=self.send_sems.at[7 * a + r - 1], recv_sem=self.recv_sems.at[7 * a + r - 1],
            device_id=(px, py, pc), device_id_type=_MESH)

    def start(self):
        for a in range(self.na):
            self._local(a).start()
        for r in range(1, N_DEV):
            for a in range(self.na):
                self._send(a, r).start()

    def finish(self):
        for r in range(1, N_DEV):
            for a in range(self.na):
                self._recv(a, r).wait_recv()
        for r in range(1, N_DEV):
            for a in range(self.na):
                self._send(a, r).wait_send()
        for a in range(self.na):
            self._local(a).wait()


def _prologue(x, tgt, small_l, w_in_l, cast_f32):
    seq = x.shape[0]
    nx = seq // TM
    rest_rows = seq - nx * TM
    nt = nx + 1
    nc = len(cast_f32)
    body_rows = TM - N_META
    assert nx >= 1 and rest_rows % 8 == 0 and rest_rows <= body_rows
    x_rest, t_rest = x[nx * TM:], tgt[nx * TM:]

    def last_tile_body(rest_ref):
        parts = ([rest_ref[...]] if rest_rows else []) + (
            [jnp.zeros((body_rows - rest_rows, D), _F32)] if body_rows > rest_rows else [])
        return parts[0] if len(parts) == 1 else jnp.concatenate(parts, axis=0)

    def body(xm_ref, xp_ref, tm_ref, tp_ref, *rest):
        if rest_rows:
            xr_ref, tr_ref, rest = rest[0], rest[1], rest[2:]
        else:
            xr_ref = tr_ref = None
        s_ref, w_ref, rest = rest[0], rest[1], rest[2:]
        cins = rest[:nc]
        h0_ref, tgt_ref, small_ref, wg_ref = rest[nc:nc + 4]
        couts = rest[nc + 4:2 * nc + 4]
        s_stage, w_stage, meta, msem = rest[2 * nc + 4:2 * nc + 8]
        g_s = _Gather([s_stage], [small_ref], rest[2 * nc + 8:2 * nc + 11])
        g_w = _Gather([w_stage], [wg_ref], rest[2 * nc + 11:], place=_pair_place)
        s = pl.program_id(0)
        i = (s + 1) % nt

        @pl.when(s == 0)
        def _():
            s_stage[...] = s_ref[...]
            w_stage[...] = w_ref[...].astype(_BF)
            g_s.start()
            g_w.start()
            meta[...] = jnp.zeros_like(meta)
            for a in range(nc):
                couts[a][...] = cins[a][...].astype(_BF)

        @pl.when(s == nt - 1)
        def _():
            for j in range(3):
                g_s.forward(j)
            g_s.finish()
            cps = [pltpu.make_async_copy(small_ref.at[k, pl.ds(0, N_META), :], meta.at[:, pl.ds(128 * k, 128)],
                                         msem.at[k]) for k in range(N_DEV)]
            for cp in cps:
                cp.start()
            for cp in cps:
                cp.wait()
            for j in range(3):
                g_w.forward(j)
            g_w.finish()

        has_x = i < nx
        h0_ref[pl.ds(0, N_META), :] = jnp.where(i == 0, meta[...], xp_ref[...])
        h0_ref[pl.ds(N_META, body_rows), :] = jnp.where(has_x, xm_ref[pl.ds(0, body_rows), :], last_tile_body(xr_ref))
        tgt_ref[pl.ds(0, N_META), :] = jnp.where(i == 0, 0.0, tp_ref[...])
        tgt_ref[pl.ds(N_META, body_rows), :] = jnp.where(has_x, tm_ref[pl.ds(0, body_rows), :], last_tile_body(tr_ref))

    def tile_of(s):
        return (s + 1) % nt

    hbm = pl.BlockSpec(memory_space=pl.ANY)
    main = pl.BlockSpec((TM, D), lambda s: (jnp.minimum(tile_of(s), nx - 1), 0))
    prev = pl.BlockSpec((N_META, D), lambda s: (jnp.maximum(tile_of(s) * (TM // N_META) - 1, 0), 0))
    tile = pl.BlockSpec((TM, D), lambda s: (tile_of(s), 0))
    rests = [x_rest, t_rest] if rest_rows else []
    return pl.pallas_call(
        body, name="prologue", grid=(nt,),
        in_specs=[main, prev, main, prev] + [_const(r.shape) for r in rests]
        + [_const(small_l.shape), _const(w_in_l.shape)] + [_const(l.shape) for l in cast_f32],
        out_specs=[tile, tile, hbm, hbm] + [_full(l.shape) for l in cast_f32],
        out_shape=[_S((nt * TM, D), _F32), _S((nt * TM, D), _F32), _S((N_DEV,) + small_l.shape, _F32),
                   _S((4, D, WIN_P), _BF)] + [_S(l.shape, _BF) for l in cast_f32],
        scratch_shapes=[pltpu.VMEM(small_l.shape, _F32), pltpu.VMEM(w_in_l.shape, _BF), pltpu.VMEM((N_META, D), _F32),
                        pltpu.SemaphoreType.DMA((N_DEV,))] + _sem_shapes(1) + _sem_shapes(1),
        compiler_params=_cp(("arbitrary",)),
    )(x, x, tgt, tgt, *rests, small_l, w_in_l, *cast_f32)


def _adamw_math(w, g, m, v):
    m2 = ADAM_B1 * m + (1.0 - ADAM_B1) * g
    v2 = ADAM_B2 * v + (1.0 - ADAM_B2) * (g * g)
    m_hat = m2 / (1.0 - ADAM_B1 ** ADAM_STEP)
    v_hat = v2 / (1.0 - ADAM_B2 ** ADAM_STEP)
    delta = -ADAM_LR * (m_hat / (jnp.sqrt(v_hat) + ADAM_EPS) + ADAM_WD * w)
    return delta, m2, v2


def _adamw_big(name, recv, w, m, v, rows):
    r_all, c_all = w.shape

    def body(r_ref, w_ref, m_ref, v_ref, g_out, d_out, m_out, v_out):
        g = r_ref[0].astype(_F32)
        for k in range(1, N_DEV):
            g = g + r_ref[k].astype(_F32)
        delta, m2, v2 = _adamw_math(w_ref[...], g, m_ref[...], v_ref[...])
        g_out[...] = g
        d_out[...] = delta
        m_out[...] = m2
        v_out[...] = v2

    tile = pl.BlockSpec((rows, c_all), lambda i: (i, 0))
    return pl.pallas_call(
        body, name=name, grid=(r_all // rows,),
        in_specs=[pl.BlockSpec((N_DEV, rows, c_all), lambda i: (0, i, 0)), tile, tile, tile],
        out_specs=[tile] * 4,
        out_shape=[_S(w.shape, _F32)] * 4,
        compiler_params=_cp(("arbitrary",)),
    )(recv, w, m, v)


def _adamw_small(gathered, slices, wmv):
    ng, npar = len(gathered), len(slices)

    def body(*refs):
        g_refs = refs[:ng]
        wmv_refs = refs[ng:ng + 3 * npar]
        outs = refs[ng + 3 * npar:]
        for i, (ai, r0, nr, c0, ncol) in enumerate(slices):
            g = g_refs[ai][0, pl.ds(r0, nr), pl.ds(c0, ncol)].astype(_F32)
            for k in range(1, N_DEV):
                g = g + g_refs[ai][k, pl.ds(r0, nr), pl.ds(c0, ncol)].astype(_F32)
            w_ref, m_ref, v_ref = wmv_refs[3 * i:3 * i + 3]
            delta, m2, v2 = _adamw_math(w_ref[...], g, m_ref[...], v_ref[...])
            outs[4 * i][...] = g
            outs[4 * i + 1][...] = delta
            outs[4 * i + 2][...] = m2
            outs[4 * i + 3][...] = v2
        total = g_refs[0][0, pl.ds(R_LOSS, 1), pl.ds(0, 128)]
        for k in range(1, N_DEV):
            total = total + g_refs[0][k, pl.ds(R_LOSS, 1), pl.ds(0, 128)]
        outs[4 * npar][...] = total

    flat = [t for trip in wmv for t in trip]
    out_shape = []
    for w, _, _ in wmv:
        out_shape += [_S(w.shape, _F32)] * 4
    out_shape.append(_S((1, 128), _F32))
    return pl.pallas_call(
        body, name="adamw_small", out_shape=out_shape,
        compiler_params=pltpu.CompilerParams(vmem_limit_bytes=VMEM_LIMIT),
    )(*gathered, *flat)


def _block_diag(w):
    eye = jnp.eye(8, dtype=w.dtype)
    return (w[:, :, None, :] * eye[:, None, :, None]).reshape(D_RG, D_RG)


def _diag_blocks(g):
    return jnp.concatenate([g[64 * h:64 * (h + 1), 64 * h:64 * (h + 1)] for h in range(8)], axis=0)


def _local_step(h0, tgt_p, n_valid, g_mix, w_in, vec, wr, wi, hb, g_hg, w_out_l, g_ffn, w_gu_l, w_down_l, g_fin):
    t_pad = h0.shape[0]
    me = 4 * lax.axis_index("x") + 2 * lax.axis_index("y") + lax.axis_index("c")
    p, u, y, hs, o, sc, w_out, w_gu, w_down = _mixer_fwd(h0, g_mix, w_in, wr, wi, vec, hb, g_hg,
                                                         [w_out_l, w_gu_l, w_down_l])
    w_out = w_out.reshape(D, D)
    w_down = w_down.reshape(4, FFB, D)
    h1, v, gu, act, dh2, dh2b, loss, gfin = _ffn_loss(h0, y, w_out, g_ffn, w_gu, w_down, g_fin, tgt_p, n_valid)

    g_wdown = _wgrad("wgrad_down", act, dh2b, pl.BlockSpec((1, t_pad, FFB), lambda j: (j, 0, 0)),
                     pl.BlockSpec((t_pad, D), lambda j: (0, 0)), 4, (FFB, D))
    g_wdown = g_wdown.reshape(N_DEV, D_FF // N_DEV, D)
    dgu, dh1, dh1b, dy, gffn, r_wdown = _ffn_bwd(dh2, dh2b, gu, h1, g_ffn, w_gu, w_down, w_out, [g_wdown])
    g_wgu = _wgrad("wgrad_gate_up", dgu, v, pl.BlockSpec((1, t_pad, FFB), lambda j: (j, 0, 0)),
                   pl.BlockSpec((t_pad, D), lambda j: (0, 0)), N_DEV, (FFB, D))
    g_wout = _wgrad("wgrad_out", y, dh1b, pl.BlockSpec((t_pad, D // 4), lambda j: (0, j)),
                    pl.BlockSpec((t_pad, D), lambda j: (0, 0)), 4, (D // 4, D)).reshape(N_DEV, D // N_DEV, D)
    dp, gvec, gw, r_wgu, r_wout = _mixer_bwd(p, hs, o, sc, dy, wr, wi, vec, hb, g_hg, [g_wgu, g_wout], [None, None])
    pack_c = jnp.concatenate([_diag_blocks(gw[0]), _diag_blocks(gw[1])], axis=1).astype(_BF)
    order = (me ^ jnp.array(_SEND_ORDER, jnp.int32)).astype(jnp.int32)
    dh0, r_win, all_b, all_c, r_meta, all_a = _inproj_bwd_send(dp, w_in, h0, dh1, g_mix, u, order, gffn, gfin, loss,
                                                               [gvec, pack_c])
    return dh0, (r_win, r_wgu, r_wout, r_wdown), (all_a, all_b, all_c, r_meta)


def kernel(x, meta_tokens, mix_norm_g, w_in, conv_w, conv_b, w_rgate, b_rgate, w_igate, b_igate, lru_lambda, rg_norm_g, hg_lower_bound, hg_norm_g, w_out, ffn_norm_g, w_gate_up, w_down, final_norm_g, loss_target, m_meta_tokens, m_mix_norm_g, m_w_in, m_conv_w, m_conv_b, m_w_rgate, m_b_rgate, m_w_igate, m_b_igate, m_lru_lambda, m_rg_norm_g, m_hg_lower_bound, m_hg_norm_g, m_w_out, m_ffn_norm_g, m_w_gate_up, m_w_down, m_final_norm_g, v_meta_tokens, v_mix_norm_g, v_w_in, v_conv_w, v_conv_b, v_w_rgate, v_b_rgate, v_w_igate, v_b_igate, v_lru_lambda, v_rg_norm_g, v_hg_lower_bound, v_hg_norm_g, v_w_out, v_ffn_norm_g, v_w_gate_up, v_w_down, v_final_norm_g):
    seq = x.shape[1]
    me = 4 * lax.axis_index("x") + 2 * lax.axis_index("y") + lax.axis_index("c")

    n_valid = N_META + seq
    small_l = jnp.concatenate([meta_tokens, jnp.pad(conv_w[0], ((0, 4), (0, 64)))], axis=0)
    h0, tgt_p, small_g, w_in_g, w_gu_l, w_out_l, w_down_l = _prologue(
        x[0], loss_target[0], small_l, w_in[0], [w_gate_up[0].T, w_out[0], w_down[0]])
    conv_w_full = jnp.transpose(small_g[:, N_META:N_META + 4, :64], (1, 0, 2)).reshape(4, D_RG)
    vec = jnp.concatenate([conv_b, b_rgate, b_igate, lru_lambda, rg_norm_g, jnp.zeros((3, D_RG), _F32),
                           conv_w_full, jnp.zeros((4, D_RG), _F32)], axis=0)
    wr = _block_diag(w_rgate[0]).astype(_BF)
    wi = _block_diag(w_igate[0]).astype(_BF)

    dh0, (r_win, r_wgu, r_wout, r_wdown), (all_a, all_b, all_c, meta_part) = _local_step(
        h0, tgt_p, n_valid, mix_norm_g, w_in_g, vec, wr, wi, hg_lower_bound, hg_norm_g,
        w_out_l, ffn_norm_g, w_gu_l, w_down_l, final_norm_g.reshape(1, D))
    grad_x = dh0[N_META:N_META + seq][None]

    outs = {}
    outs["w_in"] = _adamw_big("adamw_w_in", r_win, w_in[0], m_w_in[0], v_w_in[0], 256)
    outs["w_gate_up"] = [r.T for r in _adamw_big("adamw_w_gate_up", r_wgu, w_gate_up[0].T, m_w_gate_up[0].T,
                                                 v_w_gate_up[0].T, 176)]
    outs["w_out"] = _adamw_big("adamw_w_out", r_wout, w_out[0], m_w_out[0], v_w_out[0], 128)
    outs["w_down"] = _adamw_big("adamw_w_down", r_wdown, w_down[0], m_w_down[0], v_w_down[0], 176)

    convw_part = lax.dynamic_slice_in_dim(all_b[:, R_CONVW:R_CONVW + 4, :], me * 64, 64, axis=2)
    gathered = [all_a, all_b, all_c, meta_part, convw_part]
    small_params = [
        ("meta_tokens", (3, 0, N_META, 0, 128), (meta_tokens, m_meta_tokens, v_meta_tokens), (N_META, 128)),
        ("mix_norm_g", (0, R_GMIX, 1, 0, D), (mix_norm_g, m_mix_norm_g, v_mix_norm_g), (1, D)),
        ("conv_w", (4, 0, 4, 0, 64), (conv_w, m_conv_w, v_conv_w), (4, 64)),
        ("conv_b", (1, R_CONVB, 1, 0, D_RG), (conv_b, m_conv_b, v_conv_b), (1, D_RG)),
        ("w_rgate", (2, 0, 512, 0, 64), (w_rgate, m_w_rgate, v_w_rgate), (512, 64)),
        ("b_rgate", (1, R_BR, 1, 0, D_RG), (b_rgate, m_b_rgate, v_b_rgate), (1, D_RG)),
        ("w_igate", (2, 0, 512, 64, 64), (w_igate, m_w_igate, v_w_igate), (512, 64)),
        ("b_igate", (1, R_BI, 1, 0, D_RG), (b_igate, m_b_igate, v_b_igate), (1, D_RG)),
        ("lru_lambda", (1, R_LAM, 1, 0, D_RG), (lru_lambda, m_lru_lambda, v_lru_lambda), (1, D_RG)),
        ("rg_norm_g", (1, R_GRG, 1, 0, D_RG), (rg_norm_g, m_rg_norm_g, v_rg_norm_g), (1, D_RG)),
        ("hg_lower_bound", (1, R_HB0, 2, 0, D_HG), (hg_lower_bound, m_hg_lower_bound, v_hg_lower_bound), (2, D_HG)),
        ("hg_norm_g", (1, R_GHG, 1, 0, HD), (hg_norm_g, m_hg_norm_g, v_hg_norm_g), (1, HD)),
        ("ffn_norm_g", (0, R_GFFN, 1, 0, D), (ffn_norm_g, m_ffn_norm_g, v_ffn_norm_g), (1, D)),
        ("final_norm_g", (0, R_GFIN, 1, 0, D), (final_norm_g, m_final_norm_g, v_final_norm_g), (1, D)),
    ]
    res = _adamw_small(gathered, [s[1] for s in small_params],
                       [tuple(t.reshape(s[3]) for t in s[2]) for s in small_params])
    for i, s in enumerate(small_params):
        outs[s[0]] = [r.reshape(s[2][0].shape) for r in res[4 * i:4 * i + 4]]
    for n, ref in (("w_in", w_in), ("w_gate_up", w_gate_up), ("w_out", w_out), ("w_down", w_down)):
        outs[n] = [r.reshape(ref.shape) for r in outs[n]]

    loss_all = res[4 * len(small_params)][0, 0]
    order = ["meta_tokens", "mix_norm_g", "w_in", "conv_w", "conv_b", "w_rgate", "b_rgate", "w_igate", "b_igate",
             "lru_lambda", "rg_norm_g", "hg_lower_bound", "hg_norm_g", "w_out", "ffn_norm_g", "w_gate_up", "w_down",
             "final_norm_g"]
    return (loss_all, grad_x, *[outs[n][0] for n in order], *[outs[n][1] for n in order],
            *[outs[n][2] for n in order], *[outs[n][3] for n in order])
```

```python
import functools

import jax
import jax.numpy as jnp
from jax import lax
from jax.experimental import pallas as pl
from jax.experimental.pallas import tpu as pltpu

_BF = jnp.bfloat16
_F32 = jnp.float32
_S = jax.ShapeDtypeStruct
_MESH = pl.DeviceIdType.MESH

N_DEV = 8
N_META = 16
D = 1024
D_RG = 512
D_HG = 512
HD = 128
NH = D_HG // HD
D_IN = 3072
D_FF = 2816
FFB = D_FF // 4
WIN_B = D_IN // N_DEV
WIN_P = 2 * WIN_B
EPS = 1e-6
LRU_C = 8.0
TM = 320
HC = 64
VMEM_LIMIT = 62 * 1024 * 1024

ADAM_LR = 0.001
ADAM_B1 = 0.9
ADAM_B2 = 0.999
ADAM_EPS = 1e-08
ADAM_WD = 0.01
ADAM_STEP = 10

_SEND_ORDER = (6, 4, 2, 7, 5, 3, 1, 0)

R_CONVB, R_BR, R_BI, R_LAM, R_GRG, R_HB0, R_HB1, R_GHG, R_CONVW = 0, 1, 2, 3, 4, 5, 6, 7, 8
R_GMIX, R_GFFN, R_GFIN, R_LOSS = 0, 1, 2, 3


def _cp(sem=None, **kw):
    return pltpu.CompilerParams(dimension_semantics=sem, vmem_limit_bytes=VMEM_LIMIT, **kw)


def _dot(a, b):
    return jnp.dot(a, b, preferred_element_type=_F32)


def _dot_nt(a, b):
    return lax.dot_general(a, b, (((1,), (1,)), ((), ())), preferred_element_type=_F32)


def _dot_tn(a, b):
    return lax.dot_general(a, b, (((0,), (0,)), ((), ())), preferred_element_type=_F32)


def _sigmoid(x):
    return 0.5 * jnp.tanh(0.5 * x) + 0.5


def _dsilu(x, s):
    return s * (1.0 + x * (1.0 - s))


_GELU_C = 0.7978845608028654


def _gelu_parts(x):
    t = jnp.tanh(_GELU_C * (x + 0.044715 * (x * x * x)))
    g = 0.5 * x * (1.0 + t)
    dg = 0.5 * (1.0 + t) + 0.5 * x * (1.0 - t * t) * (_GELU_C * (1.0 + 3.0 * 0.044715 * (x * x)))
    return g, dg


def _softplus(z):
    e = jnp.exp(-jnp.abs(z))
    w = 1.0 + e
    l1p = jnp.where(w == 1.0, e, jnp.log(w) * e / jnp.where(w == 1.0, 1.0, w - 1.0))
    return jnp.maximum(z, 0.0) + l1p


def _rms_fwd(x):
    r = lax.rsqrt(jnp.mean(x * x, axis=-1, keepdims=True) + EPS)
    return x * r, r


def _rms_bwd(dyg, n, r):
    return r * (dyg - n * jnp.mean(dyg * n, axis=-1, keepdims=True))


def _full(shape):
    nd = len(shape)
    return pl.BlockSpec(shape, lambda i: (0,) * nd)


def _const(shape):
    nd = len(shape)
    return pl.BlockSpec(shape, lambda i: (0,) * nd, pipeline_mode=pl.Buffered(1))


def _carry_gather(gather, i, nt, early=0):
    @pl.when(i == 0)
    def _():
        gather.start()

    def tail():
        for j in range(3):
            if early:
                @pl.when(i == min(nt // 3 + j, nt - 1))
                def _(j=j):
                    gather.forward(j, range(early))

            @pl.when(i == max(nt - 4 + j, 0))
            def _(j=j):
                gather.forward(j, range(early, gather.na))

        @pl.when(i == nt - 1)
        def _():
            gather.finish()

    return tail


def _pair_place(ref, block):
    return ref.at[block // 2, :, pl.ds(pl.multiple_of((block % 2) * WIN_B, WIN_B), WIN_B)]


def _rg_gates(xc, wr_ref, wi_ref, vec_ref):
    xcb = xc.astype(_BF)
    r = _sigmoid(_dot(xcb, wr_ref[...]) + vec_ref[R_BR:R_BR + 1, :])
    ig = _sigmoid(_dot(xcb, wi_ref[...]) + vec_ref[R_BI:R_BI + 1, :])
    nsp8 = -LRU_C * _softplus(-vec_ref[R_LAM:R_LAM + 1, :])
    la = nsp8 * r
    a = jnp.exp(la)
    th = jnp.tanh(la)
    s = jnp.sqrt(-2.0 * th / (1.0 - th))
    return r, ig, a, s, nsp8


def _conv(xbuf, vec_ref):
    acc = vec_ref[R_CONVW:R_CONVW + 1, :] * xbuf[pl.ds(5, TM), :]
    for j in range(1, 4):
        acc = acc + vec_ref[R_CONVW + j:R_CONVW + j + 1, :] * xbuf[pl.ds(5 + j, TM), :]
    return vec_ref[R_CONVB:R_CONVB + 1, :] + acc


def _dot3(m01, x):
    hi = x.astype(_BF)
    r1 = x - hi.astype(_F32)
    mid = r1.astype(_BF)
    lo = (r1 - mid.astype(_F32)).astype(_BF)
    return (_dot(m01, lo) + _dot(m01, mid)) + _dot(m01, hi)


def _chunk_dot3(m01, x):
    return jnp.concatenate([_dot3(m01, x[HC * c:HC * (c + 1), :]) for c in range(x.shape[0] // HC)], axis=0)


def _chunk_masks():
    row = lax.broadcasted_iota(jnp.int32, (HC, HC), 0)
    col = lax.broadcasted_iota(jnp.int32, (HC, HC), 1)
    return (row >= col).astype(_BF), (col >= row).astype(_BF), jnp.ones((HC, HC), _BF)


def _per_chunk_rows(x, r):
    return jnp.concatenate([jnp.broadcast_to(x[HC * c + r:HC * c + r + 1, :], (HC, x.shape[1]))
                            for c in range(TM // HC)], axis=0)


def _hg_prep(p_ref, lb, tri):
    hq = p_ref[:, pl.ds(2 * D_RG, D_HG)]
    hf = p_ref[:, pl.ds(2 * D_RG + D_HG, D_HG)]
    sq = _sigmoid(hq)
    q = hq * sq
    sg = _sigmoid(hf)
    f = lb + (1.0 - lb) * sg
    k = 1.0 - f
    b = _chunk_dot3(tri, jnp.log(f))
    bm = _per_chunk_rows(b, HC // 2 - 1)
    bl = _per_chunk_rows(b, HC - 1)
    e_q = jnp.exp(b - bm)
    e_k = jnp.exp(bm - b)
    e_b = jnp.exp(b)
    e_l = jnp.exp(bl - b)
    return dict(hq=hq, sq=sq, q=q, sg=sg, f=f, k=k, e_q=e_q, e_k=e_k, e_b=e_b, e_l=e_l,
                qd=q * e_q, kd=k * e_k, qe=q * e_b, ke=k * e_l, e_end=jnp.exp(bl))


def _mixer_fwd(h0, g_mix, w_in, wr, wi, vec, hb, g_hg, shards):
    t_pad = h0.shape[0]
    nt = t_pad // TM
    nc_t = TM // HC
    nsh = len(shards)

    def body(h_ref, gmix_ref, win_ref, wr_ref, wi_ref, vec_ref, hb_ref, ghg_ref, *rest):
        sh_refs, rest = rest[:nsh], rest[nsh:]
        pout_ref, uout_ref, y_ref, hs_ref, o_ref, sc_ref = rest[:6]
        gath_refs, rest = rest[6:6 + nsh], rest[6 + nsh:]
        xbuf, a_s, b_s, hcar, st, qd_s, kd_s, qe_s, ke_s, v_s, u_s, p_s, p_ref = rest[:13]
        i = pl.program_id(0)
        tail = _carry_gather(_Gather(sh_refs, gath_refs, rest[13:]), i, nt + 1, early=1)

        @pl.when(i == 0)
        def _():
            p_s[...] = jnp.zeros_like(p_s)

        p_ref[...] = p_s[...]

        @pl.when(i <= 1)
        def _():
            xbuf[pl.ds(0, 8), :] = jnp.zeros((8, D_RG), _F32)
            hcar[...] = jnp.zeros_like(hcar)
            st[...] = jnp.zeros_like(st)

        n_h, _ = _rms_fwd(h_ref[...])
        u = (n_h * gmix_ref[...]).astype(_BF)
        uout_ref[...] = u
        pieces = [(j, k) for j in range(4) for k in range(WIN_P // 256)]

        def project(count):
            for _ in range(count):
                j, k = pieces.pop(0)
                blk = _dot(u, win_ref[j, :, pl.ds(256 * k, 256)])
                p_s[:, pl.ds(WIN_P * j + 256 * k, 256)] = blk
                pout_ref[:, pl.ds(WIN_P * j + 256 * k, 256)] = blk

        x = p_ref[:, pl.ds(0, D_RG)]
        xbuf[pl.ds(8, TM), :] = x
        xc = _conv(xbuf, vec_ref)
        xbuf[pl.ds(0, 8), :] = x[TM - 8:, :]
        r, ig, a, s, _ = _rg_gates(xc, wr_ref, wi_ref, vec_ref)
        a_s[...] = a
        b_s[...] = s * (ig * xc)

        def step(t, h):
            h = a_s[pl.ds(t, 1), :] * h + b_s[pl.ds(t, 1), :]
            hs_ref[pl.ds(t, 1), :] = h
            return h

        hcar[pl.ds(0, 1), :] = lax.fori_loop(0, TM, step, hcar[pl.ds(0, 1), :], unroll=8)
        gel, _ = _gelu_parts(p_ref[:, pl.ds(D_RG, D_RG)])
        n, _ = _rms_fwd(gel * hs_ref[...])
        y_ref[:, pl.ds(0, D_RG)] = (n * vec_ref[R_GRG:R_GRG + 1, :]).astype(_BF)

        lb = _sigmoid(hb_ref[0:1, :] - hb_ref[1:2, :])
        tri, _, _ = _chunk_masks()
        q = _hg_prep(p_ref, lb, tri)
        for name, ref in (("qd", qd_s), ("kd", kd_s), ("qe", qe_s), ("ke", ke_s)):
            ref[...] = q[name].astype(_BF)
        v_s[...] = p_ref[:, pl.ds(2 * D_RG + 2 * D_HG, D_HG)].astype(_BF)
        e_end = q["e_end"]
        causal = (lax.broadcasted_iota(jnp.int32, (HC, HC), 0) >= lax.broadcasted_iota(jnp.int32, (HC, HC), 1))
        for c in range(nc_t):
            for h in range(NH):
                rs, cs = pl.ds(HC * c, HC), pl.ds(HD * h, HD)
                amat = jnp.where(causal, _dot_nt(qd_s[rs, cs], kd_s[rs, cs]), 0.0)
                o_ref[rs, cs] = _dot(amat.astype(_BF), v_s[rs, cs])
                u_s[NH * c + h] = _dot_tn(v_s[rs, cs], ke_s[rs, cs])
                if pieces:
                    project(1)
        assert not pieces
        for h in range(NH):
            cs = pl.ds(HD * h, HD)
            s_run = st[h]
            for c in range(nc_t):
                rs = pl.ds(HC * c, HC)
                sc_ref[c, h] = s_run
                o_ref[rs, cs] += _dot_nt(qe_s[rs, cs], s_run.astype(_BF))
                s_run = e_end[HC * c:HC * c + 1, HD * h:HD * (h + 1)] * s_run + u_s[NH * c + h]
            st[h] = s_run
        for h in range(NH):
            cs = pl.ds(HD * h, HD)
            n_o, _ = _rms_fwd(o_ref[:, cs])
            hg = p_ref[:, pl.ds(2 * D_RG + 3 * D_HG + HD * h, HD)]
            y_ref[:, pl.ds(D_RG + HD * h, HD)] = ((n_o * ghg_ref[...]) * (hg * _sigmoid(hg))).astype(_BF)

        tail()

    hbm = pl.BlockSpec(memory_space=pl.ANY)

    def proj(i):
        return jnp.minimum(i, nt - 1)

    def mixed(i):
        return jnp.maximum(i - 1, 0)

    return pl.pallas_call(
        body, name="mixer_fwd", grid=(nt + 1,),
        in_specs=[pl.BlockSpec((TM, D), lambda i: (proj(i), 0)), _full((1, D)), _const((4, D, WIN_P)),
                  _full((D_RG, D_RG)), _full((D_RG, D_RG)),
                  _full((16, D_RG)), _full((2, D_HG)), _full((1, HD))] + [hbm] * nsh,
        out_specs=[pl.BlockSpec((TM, D_IN), lambda i: (proj(i), 0)), pl.BlockSpec((TM, D), lambda i: (proj(i), 0)),
                   pl.BlockSpec((TM, D), lambda i: (mixed(i), 0)), pl.BlockSpec((TM, D_RG), lambda i: (mixed(i), 0)),
                   pl.BlockSpec((TM, D_HG), lambda i: (mixed(i), 0)),
                   pl.BlockSpec((nc_t, NH, HD, HD), lambda i: (mixed(i), 0, 0, 0))] + [hbm] * nsh,
        out_shape=[_S((t_pad, D_IN), _F32), _S((t_pad, D), _BF),
                   _S((t_pad, D), _BF), _S((t_pad, D_RG), _F32), _S((t_pad, D_HG), _F32),
                   _S((t_pad // HC, NH, HD, HD), _F32)] + [_S((N_DEV,) + s.shape, s.dtype) for s in shards],
        scratch_shapes=[pltpu.VMEM((TM + 8, D_RG), _F32), pltpu.VMEM((TM, D_RG), _F32),
                        pltpu.VMEM((TM, D_RG), _F32), pltpu.VMEM((8, D_RG), _F32),
                        pltpu.VMEM((NH, HD, HD), _F32)] + [pltpu.VMEM((TM, D_HG), _BF) for _ in range(5)]
        + [pltpu.VMEM((nc_t * NH, HD, HD), _F32), pltpu.VMEM((TM, D_IN), _F32), pltpu.VMEM((TM, D_IN), _F32)]
        + _sem_shapes(nsh),
        compiler_params=_cp(("arbitrary",)),
    )(h0, g_mix, w_in, wr, wi, vec, hb, g_hg, *shards)


def _ffn_loss(h0, y, w_out, g_ffn, w_gu, w_down, g_fin, tgt, n_valid):
    t_pad = h0.shape[0]

    def body(h_ref, y_ref, wo_ref, gffn_ref, wgu_ref, wd_ref, g_ref, t_ref,
             h1_ref, v_ref, gu_ref, act_ref, dh2_ref, dh2b_ref, loss_ref, gfin_ref):
        i = pl.program_id(0)

        @pl.when(i == 0)
        def _():
            loss_ref[...] = jnp.zeros_like(loss_ref)
            gfin_ref[...] = jnp.zeros_like(gfin_ref)

        h1 = h_ref[...] + _dot(y_ref[...], wo_ref[...])
        h1_ref[...] = h1
        n1, _ = _rms_fwd(h1)
        vb = (n1 * gffn_ref[...]).astype(_BF)
        v_ref[...] = vb
        h2 = h1
        for b in range(4):
            gate = _dot_nt(vb, wgu_ref[b])
            up = _dot_nt(vb, wgu_ref[4 + b])
            gu_ref[b] = gate
            gu_ref[4 + b] = up
            act = ((gate * _sigmoid(gate)) * up).astype(_BF)
            act_ref[b] = act
            h2 = h2 + _dot(act, wd_ref[b])
        n, r = _rms_fwd(h2)
        out = n * g_ref[...]
        row = i * TM + lax.broadcasted_iota(jnp.int32, (TM, 1), 0)
        valid = (row >= N_META) & (row < n_valid)
        err = jnp.where(valid, out - t_ref[...], 0.0)
        loss_ref[...] += (0.5 / D) * jnp.sum(err * err)
        dout = err * (1.0 / D)
        gfin_ref[...] += jnp.sum(dout * n, axis=0, keepdims=True)
        dh2 = _rms_bwd(dout * g_ref[...], n, r)
        dh2_ref[...] = dh2
        dh2b_ref[...] = dh2.astype(_BF)

    tile = pl.BlockSpec((TM, D), lambda i: (i, 0))
    return pl.pallas_call(
        body, name="ffn_loss", grid=(t_pad // TM,),
        in_specs=[tile, tile, _const((D, D)), _full((1, D)),
                  _const((N_DEV, FFB, D)), _const((4, FFB, D)), _full((1, D)), tile],
        out_specs=[tile, tile,
                   pl.BlockSpec((N_DEV, TM, FFB), lambda i: (0, i, 0)), pl.BlockSpec((4, TM, FFB), lambda i: (0, i, 0)),
                   tile, tile, _full((8, 128)), _full((1, D))],
        out_shape=[_S((t_pad, D), _F32), _S((t_pad, D), _BF),
                   _S((N_DEV, t_pad, FFB), _F32), _S((4, t_pad, FFB), _BF), _S((t_pad, D), _F32),
                   _S((t_pad, D), _BF), _S((8, 128), _F32), _S((1, D), _F32)],
        compiler_params=_cp(("arbitrary",)),
    )(h0, y, w_out, g_ffn, w_gu, w_down, g_fin, tgt)


def _ffn_bwd(dh2, dh2b, gu, h1, g_ffn, w_gu, w_down, w_out, scatter):
    t_pad = dh2.shape[0]
    nsc = len(scatter)
    nt = t_pad // TM

    def body(dh2_ref, dh2b_ref, gu_ref, h1_ref, g_ref, wgu_ref, wd_ref, wo_ref, *rest):
        dgu_ref, dh1_ref, dh1b_ref, dy_ref, gffn_ref = rest[nsc:nsc + 5]
        exchange = _Exchange(rest[:nsc], [], rest[nsc + 5:2 * nsc + 5], rest[2 * nsc + 5:])
        i = pl.program_id(0)

        @pl.when(i == 0)
        def _():
            exchange.start()
            gffn_ref[...] = jnp.zeros_like(gffn_ref)

        db = dh2b_ref[...]
        dv = jnp.zeros((TM, D), _F32)
        for b in range(4):
            dact = _dot_nt(db, wd_ref[b])
            gate = gu_ref[b]
            up = gu_ref[4 + b]
            sg = _sigmoid(gate)
            dgate = ((dact * up) * _dsilu(gate, sg)).astype(_BF)
            dup = (dact * (gate * sg)).astype(_BF)
            dgu_ref[b] = dgate
            dgu_ref[4 + b] = dup
            dv = dv + _dot(dgate, wgu_ref[b]) + _dot(dup, wgu_ref[4 + b])
        n, r = _rms_fwd(h1_ref[...])
        gffn_ref[...] += jnp.sum(dv * n, axis=0, keepdims=True)
        dh1 = dh2_ref[...] + _rms_bwd(dv * g_ref[...], n, r)
        dh1_ref[...] = dh1
        dh1b = dh1.astype(_BF)
        dh1b_ref[...] = dh1b
        dy_ref[...] = _dot_nt(dh1b, wo_ref[...])

        @pl.when(i == nt - 1)
        def _():
            exchange.finish()

    tile = pl.BlockSpec((TM, D), lambda i: (i, 0))
    hbm = pl.BlockSpec(memory_space=pl.ANY)
    return pl.pallas_call(
        body, name="ffn_bwd", grid=(nt,),
        in_specs=[tile, tile, pl.BlockSpec((N_DEV, TM, FFB), lambda i: (0, i, 0)), tile, _full((1, D)),
                  _const((N_DEV, FFB, D)), _const((4, FFB, D)), _const((D, D))] + [hbm] * nsc,
        out_specs=[pl.BlockSpec((N_DEV, TM, FFB), lambda i: (0, i, 0)), tile, tile, tile, _full((1, D))] + [hbm] * nsc,
        out_shape=[_S((N_DEV, t_pad, FFB), _BF), _S((t_pad, D), _F32), _S((t_pad, D), _BF),
                   _S((t_pad, D), _F32), _S((1, D), _F32)] + _recv_shapes(scatter, [None] * nsc),
        scratch_shapes=_sem_shapes(nsc),
        compiler_params=_cp(("arbitrary",)),
    )(dh2, dh2b, gu, h1, g_ffn, w_gu, w_down, w_out, *scatter)


def _mixer_bwd(p, hs, o, sc, dy, wr, wi, vec, hb, g_hg, scatter, windows):
    t_pad = p.shape[0]
    nt = t_pad // TM
    nc_t = TM // HC
    nsc = len(scatter)

    def rev(i):
        return nt - 1 - i

    def body(p_ref, pprev_ref, hs_ref, hprev_ref, o_ref, sc_ref, dy_ref, wr_ref, wi_ref, vec_ref, hb_ref, ghg_ref,
             *rest):
        send_refs, rest = rest[:nsc], rest[nsc:]
        dp_ref, gvec_ref, gw_ref = rest[:3]
        recv_refs, rest = rest[3:3 + nsc], rest[3 + nsc:]
        xbuf, hbuf, dbuf, a_s, g_s, ccar, dst = rest[:7]
        qd_s, kd_s, qe_s, ke_s, v_s, do_s, dqd_s, dkd_s, dqe_s, dke_s, dv_s, w_s, dend_s = rest[7:20]
        exchange = _Exchange(send_refs, [], recv_refs, rest[20:], windows)
        i = pl.program_id(0)
        first_tile = i == nt - 1

        @pl.when(i == 0)
        def _():
            exchange.start()
            gvec_ref[...] = jnp.zeros_like(gvec_ref)
            gw_ref[...] = jnp.zeros_like(gw_ref)
            dbuf[pl.ds(TM, 8), :] = jnp.zeros((8, D_RG), _F32)
            ccar[...] = jnp.zeros_like(ccar)
            dst[...] = jnp.zeros_like(dst)

        def acc(row, val):
            gvec_ref[row:row + 1, :] += jnp.sum(val, axis=0, keepdims=True)

        keep = jnp.where(first_tile, 0.0, 1.0)
        x = p_ref[:, pl.ds(0, D_RG)]
        xbuf[pl.ds(0, 8), :] = pprev_ref[...] * keep
        xbuf[pl.ds(8, TM), :] = x
        xc = _conv(xbuf, vec_ref)
        r, ig, a, s, nsp8 = _rg_gates(xc, wr_ref, wi_ref, vec_ref)
        h = hs_ref[...]
        hbuf[pl.ds(0, 8), :] = hprev_ref[...] * keep
        hbuf[pl.ds(8, TM), :] = h
        hm1 = hbuf[pl.ds(7, TM), :]
        gr = p_ref[:, pl.ds(D_RG, D_RG)]
        gel, dgel = _gelu_parts(gr)
        n, rr = _rms_fwd(gel * h)
        dyn = dy_ref[:, pl.ds(0, D_RG)]
        acc(R_GRG, dyn * n)
        dpre = _rms_bwd(dyn * vec_ref[R_GRG:R_GRG + 1, :], n, rr)
        dp_ref[:, pl.ds(D_RG, D_RG)] = ((dpre * h) * dgel).astype(_BF)
        a_s[...] = a
        g_s[...] = dpre * gel

        def step(k, c):
            t = TM - 1 - k
            g = g_s[pl.ds(t, 1), :] + c
            g_s[pl.ds(t, 1), :] = g
            return a_s[pl.ds(t, 1), :] * g

        ccar[pl.ds(0, 1), :] = lax.fori_loop(0, TM, step, ccar[pl.ds(0, 1), :], unroll=8)
        gt = g_s[...]
        da = gt * hm1
        ixc = ig * xc
        ds = gt * ixc
        dig = (gt * s) * xc
        dxc = (gt * s) * ig
        dla = da * a - ds * ((a * a) / s)
        lam = vec_ref[R_LAM:R_LAM + 1, :]
        gvec_ref[R_LAM:R_LAM + 1, :] += jnp.sum(dla * r, axis=0, keepdims=True) * (LRU_C * _sigmoid(-lam))
        dzr = (dla * nsp8) * (r * (1.0 - r))
        dzi = dig * (ig * (1.0 - ig))
        acc(R_BR, dzr)
        acc(R_BI, dzi)
        xcb = xc.astype(_BF)
        dzrb = dzr.astype(_BF)
        dzib = dzi.astype(_BF)
        gw_ref[0] += _dot_tn(xcb, dzrb)
        gw_ref[1] += _dot_tn(xcb, dzib)
        dxc = dxc + _dot_nt(dzrb, wr_ref[...]) + _dot_nt(dzib, wi_ref[...])
        acc(R_CONVB, dxc)
        for j in range(4):
            acc(R_CONVW + j, dxc * xbuf[pl.ds(5 + j, TM), :])
        dbuf[pl.ds(0, TM), :] = dxc
        dx = vec_ref[R_CONVW + 3:R_CONVW + 4, :] * dxc
        for j in range(3):
            dx = dx + vec_ref[R_CONVW + j:R_CONVW + j + 1, :] * dbuf[pl.ds(3 - j, TM), :]
        dbuf[pl.ds(TM, 8), :] = dxc[0:8, :]
        dp_ref[:, pl.ds(0, D_RG)] = dx.astype(_BF)

        lb = _sigmoid(hb_ref[0:1, :] - hb_ref[1:2, :])
        tri, tri_rev, ones = _chunk_masks()
        q = _hg_prep(p_ref, lb, tri)
        qdb, kdb = q["qd"].astype(_BF), q["kd"].astype(_BF)
        qd_s[...] = qdb
        kd_s[...] = kdb
        qe_s[...] = q["qe"].astype(_BF)
        ke_s[...] = q["ke"].astype(_BF)
        v_s[...] = p_ref[:, pl.ds(2 * D_RG + 2 * D_HG, D_HG)].astype(_BF)
        e_end = q["e_end"]
        ghg = ghg_ref[...]
        for h in range(NH):
            cs = pl.ds(HD * h, HD)
            hg = p_ref[:, pl.ds(2 * D_RG + 3 * D_HG + HD * h, HD)]
            sh = _sigmoid(hg)
            n_o, r_o = _rms_fwd(o_ref[:, cs])
            dyh = dy_ref[:, pl.ds(D_RG + HD * h, HD)]
            dp_ref[:, pl.ds(2 * D_RG + 3 * D_HG + HD * h, HD)] = ((dyh * (n_o * ghg)) * _dsilu(hg, sh)).astype(_BF)
            dn = dyh * (hg * sh)
            gvec_ref[R_GHG:R_GHG + 1, pl.ds(0, HD)] += jnp.sum(dn * n_o, axis=0, keepdims=True)
            do_s[:, cs] = _rms_bwd(dn * ghg, n_o, r_o).astype(_BF)
        causal = (lax.broadcasted_iota(jnp.int32, (HC, HC), 0) >= lax.broadcasted_iota(jnp.int32, (HC, HC), 1))
        for c in range(nc_t):
            for h in range(NH):
                rs, cs = pl.ds(HC * c, HC), pl.ds(HD * h, HD)
                qd_c, kd_c, do_c = qd_s[rs, cs], kd_s[rs, cs], do_s[rs, cs]
                amat = jnp.where(causal, _dot_nt(qd_c, kd_c), 0.0).astype(_BF)
                da_m = jnp.where(causal, _dot_nt(do_c, v_s[rs, cs]), 0.0).astype(_BF)
                dqd_s[rs, cs] = _dot(da_m, kd_c)
                dkd_s[rs, cs] = _dot_tn(da_m, qd_c)
                dqe_s[rs, cs] = _dot(do_c, sc_ref[c, h].astype(_BF))
                dv_s[rs, cs] = _dot_tn(amat, do_c)
                w_s[NH * c + h] = _dot_tn(do_c, qe_s[rs, cs])
        for h in range(NH):
            cs = pl.ds(HD * h, HD)
            d_run = dst[h]
            for c in reversed(range(nc_t)):
                rs = pl.ds(HC * c, HC)
                d_b = d_run.astype(_BF)
                dke_s[rs, cs] = _dot(v_s[rs, cs], d_b)
                dp_ref[rs, pl.ds(2 * D_RG + 2 * D_HG + HD * h, HD)] = (
                    dv_s[rs, cs] + _dot_nt(ke_s[rs, cs], d_b)).astype(_BF)
                dend_s[pl.ds(c, 1), cs] = jnp.sum(sc_ref[c, h] * d_run, axis=0, keepdims=True)
                d_run = w_s[NH * c + h] + e_end[HC * c:HC * c + 1, HD * h:HD * (h + 1)] * d_run
            dst[h] = d_run
        dqd, dkd, dqe, dke = dqd_s[...], dkd_s[...], dqe_s[...], dke_s[...]
        dq = dqd * q["e_q"] + dqe * q["e_b"]
        dk = dkd * q["e_k"] + dke * q["e_l"]
        dkeke = dke * q["ke"]
        db = dqd * qdb.astype(_F32) - dkd * kdb.astype(_F32) + dqe * q["qe"] - dkeke
        d_end = jnp.concatenate([jnp.broadcast_to(dend_s[pl.ds(c, 1), :], (HC, D_HG)) for c in range(nc_t)], axis=0)
        dlf = _chunk_dot3(tri_rev, db) + _chunk_dot3(ones, dkeke) + d_end * e_end
        df = dlf / q["f"] - dk
        sg = q["sg"]
        gvec_ref[R_HB0:R_HB0 + 1, :] += jnp.sum(df * (1.0 - sg), axis=0, keepdims=True)
        dp_ref[:, pl.ds(2 * D_RG, D_HG)] = (dq * _dsilu(q["hq"], q["sq"])).astype(_BF)
        dp_ref[:, pl.ds(2 * D_RG + D_HG, D_HG)] = ((df * (1.0 - lb)) * (sg * (1.0 - sg))).astype(_BF)

        @pl.when(i == nt - 1)
        def _():
            glb = gvec_ref[R_HB0:R_HB0 + 1, :] * (lb * (1.0 - lb))
            gvec_ref[R_HB0:R_HB0 + 1, :] = glb
            gvec_ref[R_HB1:R_HB1 + 1, :] = -glb
            exchange.finish()

    hbm = pl.BlockSpec(memory_space=pl.ANY)
    return pl.pallas_call(
        body, name="mixer_bwd", grid=(nt,),
        in_specs=[pl.BlockSpec((TM, D_IN), lambda i: (rev(i), 0)),
                  pl.BlockSpec((8, D_RG), lambda i: (jnp.maximum(rev(i) * (TM // 8) - 1, 0), 0)),
                  pl.BlockSpec((TM, D_RG), lambda i: (rev(i), 0)),
                  pl.BlockSpec((8, D_RG), lambda i: (jnp.maximum(rev(i) * (TM // 8) - 1, 0), 0)),
                  pl.BlockSpec((TM, D_HG), lambda i: (rev(i), 0)),
                  pl.BlockSpec((nc_t, NH, HD, HD), lambda i: (rev(i), 0, 0, 0)),
                  pl.BlockSpec((TM, D), lambda i: (rev(i), 0)),
                  _full((D_RG, D_RG)), _full((D_RG, D_RG)), _full((16, D_RG)), _full((2, D_HG)), _full((1, HD))]
        + [hbm] * nsc,
        out_specs=[pl.BlockSpec((TM, D_IN), lambda i: (rev(i), 0)), _full((16, D_RG)), _full((2, D_RG, D_RG))]
        + [hbm] * nsc,
        out_shape=[_S((t_pad, D_IN), _BF), _S((16, D_RG), _F32), _S((2, D_RG, D_RG), _F32)]
        + _recv_shapes(scatter, windows),
        scratch_shapes=[pltpu.VMEM((TM + 8, D_RG), _F32), pltpu.VMEM((TM + 8, D_RG), _F32),
                        pltpu.VMEM((TM + 8, D_RG), _F32), pltpu.VMEM((TM, D_RG), _F32),
                        pltpu.VMEM((TM, D_RG), _F32), pltpu.VMEM((8, D_RG), _F32),
                        pltpu.VMEM((NH, HD, HD), _F32)]
        + [pltpu.VMEM((TM, D_HG), _BF) for _ in range(6)] + [pltpu.VMEM((TM, D_HG), _F32) for _ in range(5)]
        + [pltpu.VMEM((nc_t * NH, HD, HD), _F32), pltpu.VMEM((8, D_HG), _F32)] + _sem_shapes(nsc),
        compiler_params=_cp(("arbitrary",)),
    )(p, p, hs, hs, o, sc, dy, wr, wi, vec, hb, g_hg, *scatter)


def _inproj_bwd_send(dp, w_in, h0, dh1, g_mix, u, order, gffn, gfin, loss, to_all):
    t_pad = dp.shape[0]
    rb = TM
    n_steps = N_DEV + t_pad // rb
    na = len(to_all)

    def body(order_ref, dpc_ref, dpr_ref, u_ref, w_ref, h_ref, dh1_ref, g_ref, gffn_ref, gfin_ref, loss_ref, *rest):
        all_in = rest[:na]
        dh0_ref, recv_ref = rest[na:na + 2]
        all_out = rest[na + 2:2 * na + 2]
        meta_ref, alla_ref = rest[2 * na + 2:2 * na + 4]
        buf, pack, meta, blk_send, blk_recv, blk_local = rest[2 * na + 4:2 * na + 10]
        exchange = _Exchange([], all_in, all_out, rest[2 * na + 10:2 * na + 13])
        last = _Exchange([meta], [pack], [meta_ref, alla_ref], rest[2 * na + 13:])
        s = pl.program_id(0)
        x, y, c = _coords()
        me = 4 * x + 2 * y + c

        def send(step):
            r = _SEND_ORDER[step]
            return pltpu.make_async_remote_copy(
                src_ref=buf.at[step], dst_ref=recv_ref.at[me], send_sem=blk_send.at[step], recv_sem=blk_recv.at[r - 1],
                device_id=(x ^ (r >> 2), y ^ ((r >> 1) & 1), c ^ (r & 1)), device_id_type=_MESH)

        @pl.when(s == 0)
        def _():
            exchange.start()
            pack[...] = jnp.zeros_like(pack)

        @pl.when(s < N_DEV)
        def _():
            buf[s] = _dot_tn(u_ref[...], dpc_ref[...]).astype(_BF)

            for step in range(N_DEV - 1):
                @pl.when(s == step)
                def _(step=step):
                    send(step).start()

        @pl.when(s >= N_DEV)
        def _():
            du = jnp.zeros((rb, D), _F32)
            for j in range(4):
                du = du + _dot_nt(dpr_ref[:, WIN_P * j:WIN_P * (j + 1)], w_ref[j])
            n, r = _rms_fwd(h_ref[...])
            pack[R_GMIX:R_GMIX + 1, :] += jnp.sum(du * n, axis=0, keepdims=True)
            dh0 = dh1_ref[...] + _rms_bwd(du * g_ref[...], n, r)
            dh0_ref[...] = dh0

            @pl.when(s == N_DEV)
            def _():
                for k in range(N_DEV):
                    meta[k] = dh0[0:N_META, 128 * k:128 * (k + 1)]

        @pl.when(s == n_steps - 1)
        def _():
            pack[R_GFFN:R_GFFN + 1, :] = gffn_ref[...]
            pack[R_GFIN:R_GFIN + 1, :] = gfin_ref[...]
            pack[R_LOSS:R_LOSS + 1, pl.ds(0, 128)] = loss_ref[0:1, :]
            last.start()
            mine = pltpu.make_async_copy(buf.at[N_DEV - 1], recv_ref.at[me], blk_local.at[0])
            mine.start()
            for step in range(N_DEV - 1):
                send(step).wait_send()
            for r in range(1, N_DEV):
                px, py, pc = x ^ (r >> 2), y ^ ((r >> 1) & 1), c ^ (r & 1)
                pltpu.make_async_remote_copy(
                    src_ref=buf.at[0], dst_ref=recv_ref.at[4 * px + 2 * py + pc], send_sem=blk_send.at[0],
                    recv_sem=blk_recv.at[r - 1], device_id=(px, py, pc), device_id_type=_MESH).wait_recv()
            mine.wait()
            exchange.finish()
            last.finish()

    hbm = pl.BlockSpec(memory_space=pl.ANY)
    rows = pl.BlockSpec((rb, D), lambda s, order: (jnp.maximum(s - N_DEV, 0), 0))
    one = pl.BlockSpec((1, D), lambda s, order: (0, 0))
    res = pl.pallas_call(
        body, name="inproj_bwd_send",
        grid_spec=pltpu.PrefetchScalarGridSpec(
            num_scalar_prefetch=1, grid=(n_steps,),
            in_specs=[pl.BlockSpec((t_pad, WIN_B), lambda s, order: (0, order[jnp.minimum(s, N_DEV - 1)])),
                      pl.BlockSpec((rb, D_IN), lambda s, order: (jnp.maximum(s - N_DEV, 0), 0)),
                      pl.BlockSpec((t_pad, D), lambda s, order: (0, 0), pipeline_mode=pl.Buffered(1)),
                      pl.BlockSpec((4, D, WIN_P), lambda s, order: (0, 0, 0), pipeline_mode=pl.Buffered(1)),
                      rows, rows, one, one, one, pl.BlockSpec((8, 128), lambda s, order: (0, 0))] + [hbm] * na,
            out_specs=[rows] + [hbm] * (na + 3),
            scratch_shapes=[pltpu.VMEM((N_DEV, D, WIN_B), _BF), pltpu.VMEM((8, D), _F32),
                            pltpu.VMEM((N_DEV, N_META, 128), _F32),
                            pltpu.SemaphoreType.DMA((N_DEV - 1,)), pltpu.SemaphoreType.DMA((N_DEV - 1,)),
                            pltpu.SemaphoreType.DMA((1,))] + _sem_shapes(na) + _sem_shapes(2)),
        out_shape=[_S((t_pad, D), _F32), _S((N_DEV, D, WIN_B), _BF)]
        + [_S((N_DEV,) + g.shape, g.dtype) for g in to_all]
        + [_S((N_DEV, N_META, 128), _F32), _S((N_DEV, 8, D), _F32)],
        compiler_params=_cp(("arbitrary",)),
    )(order, dp, dp, u, w_in, h0, dh1, g_mix, gffn, gfin, loss, *to_all)
    return res


def _recv_shapes(scatter, windows):
    return [_S(s.shape if w is None else (s.shape[0], w[1]) + s.shape[2:], s.dtype) for s, w in zip(scatter, windows)]


def _wgrad(name, a, b, a_spec, b_spec, n_blocks, out_block, scatter=(), windows=None):
    nsc = len(scatter)
    windows = windows if windows is not None else [None] * nsc

    def body(a_ref, b_ref, *rest):
        o_ref = rest[nsc]
        j = pl.program_id(0)
        if nsc:
            exchange = _Exchange(rest[:nsc], [], rest[nsc + 1:2 * nsc + 1], rest[2 * nsc + 1:], windows)

            @pl.when(j == 0)
            def _():
                exchange.start()

        av = a_ref[0] if len(a_ref.shape) == 3 else a_ref[...]
        bv = b_ref[0] if len(b_ref.shape) == 3 else b_ref[...]
        o_ref[0] = _dot_tn(av, bv).astype(_BF)

        if nsc:
            @pl.when(j == n_blocks - 1)
            def _():
                exchange.finish()

    hbm = pl.BlockSpec(memory_space=pl.ANY)
    res = pl.pallas_call(
        body, name=name, grid=(n_blocks,),
        in_specs=[a_spec, b_spec] + [hbm] * nsc,
        out_specs=[pl.BlockSpec((1,) + out_block, lambda j: (j, 0, 0))] + [hbm] * nsc,
        out_shape=[_S((n_blocks,) + out_block, _BF)] + _recv_shapes(scatter, windows),
        scratch_shapes=_sem_shapes(nsc) if nsc else [],
        compiler_params=_cp(("arbitrary",)),
    )(a, b, *scatter)
    return res if nsc else res[0]


def _coords():
    return lax.axis_index("x"), lax.axis_index("y"), lax.axis_index("c")


def _sem_shapes(na):
    return [pltpu.SemaphoreType.DMA((7 * na,)), pltpu.SemaphoreType.DMA((7 * na,)), pltpu.SemaphoreType.DMA((na,))]


class _Gather:
    def __init__(self, srcs, outs, sems, place=None):
        self.srcs, self.outs = srcs, outs
        self.send_sems, self.recv_sems, self.local_sems = sems
        self.place = place if place is not None else (lambda ref, block: ref.at[block])
        self.na = len(srcs)
        x, y, c = _coords()
        self.pos = (x, y, c)
        self.me = 4 * x + 2 * y + c
        self.sibling = (x, y, 1 - c)
        self.chips = [(1 - x, y), (x, 1 - y), (1 - x, 1 - y)]

    @staticmethod
    def _slot(px, py, pc):
        return 4 * px + 2 * py + pc

    def _copy(self, a, k, block, to, own=False):
        dst = self.place(self.outs[a], block)
        return pltpu.make_async_remote_copy(
            src_ref=self.srcs[a] if own else dst, dst_ref=dst,
            send_sem=self.send_sems.at[7 * a + k], recv_sem=self.recv_sems.at[7 * a + k],
            device_id=to, device_id_type=_MESH)

    def _mine(self, a):
        return pltpu.make_async_copy(self.srcs[a], self.place(self.outs[a], self.me), self.local_sems.at[a])

    def _first(self):
        c = self.pos[2]
        cps = []
        for a in range(self.na):
            cps.append(self._copy(a, 0, self.me, self.sibling, own=True))
            cps += [self._copy(a, 1 + j, self.me, (*chip, c), own=True) for j, chip in enumerate(self.chips)]
        return cps

    def _passed(self):
        c = self.pos[2]
        return [self._copy(a, 4 + j, self._slot(*chip, c), self.sibling)
                for j, chip in enumerate(self.chips) for a in range(self.na)]

    def start(self):
        for a in range(self.na):
            self._mine(a).start()
        for cp in self._first():
            cp.start()

    def forward(self, j, arrays=None):
        c = self.pos[2]
        chip = self.chips[j]
        for a in (range(self.na) if arrays is None else arrays):
            self._copy(a, 1 + j, self._slot(*chip, c), self.pos).wait_recv()
            self._copy(a, 4 + j, self._slot(*chip, c), self.sibling).start()

    def wait_sibling(self):
        x, y, c = self.pos
        for a in range(self.na):
            self._copy(a, 0, self._slot(x, y, 1 - c), self.pos).wait_recv()

    def wait_passed(self, j):
        c = self.pos[2]
        for a in range(self.na):
            self._copy(a, 4 + j, self._slot(*self.chips[j], 1 - c), self.pos).wait_recv()

    def finish_sends(self):
        for cp in self._first() + self._passed():
            cp.wait_send()
        for a in range(self.na):
            self._mine(a).wait()

    def finish(self):
        self.wait_sibling()
        for j in range(3):
            self.wait_passed(j)
        self.finish_sends()


class _Exchange:
    def __init__(self, scatter, gather, outs, sems, windows=None):
        self.windows = windows if windows is not None else [None] * len(scatter)
        self.ins = list(scatter) + list(gather)
        self.ns, self.na = len(scatter), len(scatter) + len(gather)
        self.outs = outs
        self.send_sems, self.recv_sems, self.local_sems = sems
        x, y, c = _coords()
        self.pos = (x, y, c)
        self.me = 4 * x + 2 * y + c

    def _peer(self, r):
        x, y, c = self.pos
        return x ^ (r >> 2), y ^ ((r >> 1) & 1), c ^ (r & 1)

    def _src(self, a, block):
        if a >= self.ns:
            return self.ins[a]
        if self.windows[a] is None:
            return self.ins[a].at[block]
        row0, rows = self.windows[a]
        return self.ins[a].at[block, pl.ds(row0, rows)]

    def _local(self, a):
        return pltpu.make_async_copy(self._src(a, self.me), self.outs[a].at[self.me], self.local_sems.at[a])

    def _send(self, a, r):
        px, py, pc = self._peer(r)
        return pltpu.make_async_remote_copy(
            src_ref=self._src(a, 4 * px + 2 * py + pc), dst_ref=self.outs[a].at[self.me],
            send_sem=self.send_sems.at[7 * a + r - 1], recv_sem=self.recv_sems.at[7 * a + r - 1],
            device_id=(px, py, pc), device_id_type=_MESH)

    def _recv(self, a, r):
        px, py, pc = self._peer(r)
        return pltpu.make_async_remote_copy(
            src_ref=self._src(a, self.me), dst_ref=self.outs[a].at[4 * px + 2 * py + pc],
            send_sem=self.send_sems.at[7 * a + r - 1], recv_sem=self.recv_sems.at[7 * a + r - 1],
            device_id=(px, py, pc), device_id_type=_MESH)

    def start(self):
        for a in range(self.na):
            self._local(a).start()
        for r in range(1, N_DEV):
            for a in range(self.na):
                self._send(a, r).start()

    def finish(self):
        for r in range(1, N_DEV):
            for a in range(self.na):
                self._recv(a, r).wait_recv()
        for r in range(1, N_DEV):
            for a in range(self.na):
                self._send(a, r).wait_send()
        for a in range(self.na):
            self._local(a).wait()


def _prologue(x, tgt, small_l, w_in_l, cast_f32):
    seq = x.shape[0]
    nx = seq // TM
    rest_rows = seq - nx * TM
    nt = nx + 1
    nc = len(cast_f32)
    body_rows = TM - N_META
    assert nx >= 1 and rest_rows % 8 == 0 and rest_rows <= body_rows
    x_rest, t_rest = x[nx * TM:], tgt[nx * TM:]

    def last_tile_body(rest_ref):
        parts = ([rest_ref[...]] if rest_rows else []) + (
            [jnp.zeros((body_rows - rest_rows, D), _F32)] if body_rows > rest_rows else [])
        return parts[0] if len(parts) == 1 else jnp.concatenate(parts, axis=0)

    def body(xm_ref, xp_ref, tm_ref, tp_ref, *rest):
        if rest_rows:
            xr_ref, tr_ref, rest = rest[0], rest[1], rest[2:]
        else:
            xr_ref = tr_ref = None
        s_ref, w_ref, rest = rest[0], rest[1], rest[2:]
        cins = rest[:nc]
        h0_ref, tgt_ref, small_ref, wg_ref = rest[nc:nc + 4]
        couts = rest[nc + 4:2 * nc + 4]
        s_stage, w_stage, meta, msem = rest[2 * nc + 4:2 * nc + 8]
        g_s = _Gather([s_stage], [small_ref], rest[2 * nc + 8:2 * nc + 11])
        g_w = _Gather([w_stage], [wg_ref], rest[2 * nc + 11:], place=_pair_place)
        s = pl.program_id(0)
        i = (s + 1) % nt

        @pl.when(s == 0)
        def _():
            s_stage[...] = s_ref[...]
            w_stage[...] = w_ref[...].astype(_BF)
            g_s.start()
            g_w.start()
            meta[...] = jnp.zeros_like(meta)
            for a in range(nc):
                couts[a][...] = cins[a][...].astype(_BF)

        @pl.when(s == nt - 1)
        def _():
            for j in range(3):
                g_s.forward(j)
            g_s.finish()
            cps = [pltpu.make_async_copy(small_ref.at[k, pl.ds(0, N_META), :], meta.at[:, pl.ds(128 * k, 128)],
                                         msem.at[k]) for k in range(N_DEV)]
            for cp in cps:
                cp.start()
            for cp in cps:
                cp.wait()
            for j in range(3):
                g_w.forward(j)
            g_w.finish()

        has_x = i < nx
        h0_ref[pl.ds(0, N_META), :] = jnp.where(i == 0, meta[...], xp_ref[...])
        h0_ref[pl.ds(N_META, body_rows), :] = jnp.where(has_x, xm_ref[pl.ds(0, body_rows), :], last_tile_body(xr_ref))
        tgt_ref[pl.ds(0, N_META), :] = jnp.where(i == 0, 0.0, tp_ref[...])
        tgt_ref[pl.ds(N_META, body_rows), :] = jnp.where(has_x, tm_ref[pl.ds(0, body_rows), :], last_tile_body(tr_ref))

    def tile_of(s):
        return (s + 1) % nt

    hbm = pl.BlockSpec(memory_space=pl.ANY)
    main = pl.BlockSpec((TM, D), lambda s: (jnp.minimum(tile_of(s), nx - 1), 0))
    prev = pl.BlockSpec((N_META, D), lambda s: (jnp.maximum(tile_of(s) * (TM // N_META) - 1, 0), 0))
    tile = pl.BlockSpec((TM, D), lambda s: (tile_of(s), 0))
    rests = [x_rest, t_rest] if rest_rows else []
    return pl.pallas_call(
        body, name="prologue", grid=(nt,),
        in_specs=[main, prev, main, prev] + [_const(r.shape) for r in rests]
        + [_const(small_l.shape), _const(w_in_l.shape)] + [_const(l.shape) for l in cast_f32],
        out_specs=[tile, tile, hbm, hbm] + [_full(l.shape) for l in cast_f32],
        out_shape=[_S((nt * TM, D), _F32), _S((nt * TM, D), _F32), _S((N_DEV,) + small_l.shape, _F32),
                   _S((4, D, WIN_P), _BF)] + [_S(l.shape, _BF) for l in cast_f32],
        scratch_shapes=[pltpu.VMEM(small_l.shape, _F32), pltpu.VMEM(w_in_l.shape, _BF), pltpu.VMEM((N_META, D), _F32),
                        pltpu.SemaphoreType.DMA((N_DEV,))] + _sem_shapes(1) + _sem_shapes(1),
        compiler_params=_cp(("arbitrary",)),
    )(x, x, tgt, tgt, *rests, small_l, w_in_l, *cast_f32)


def _adamw_math(w, g, m, v):
    m2 = ADAM_B1 * m + (1.0 - ADAM_B1) * g
    v2 = ADAM_B2 * v + (1.0 - ADAM_B2) * (g * g)
    m_hat = m2 / (1.0 - ADAM_B1 ** ADAM_STEP)
    v_hat = v2 / (1.0 - ADAM_B2 ** ADAM_STEP)
    delta = -ADAM_LR * (m_hat / (jnp.sqrt(v_hat) + ADAM_EPS) + ADAM_WD * w)
    return delta, m2, v2


def _adamw_big(name, recv, w, m, v, rows):
    r_all, c_all = w.shape

    def body(r_ref, w_ref, m_ref, v_ref, g_out, d_out, m_out, v_out):
        g = r_ref[0].astype(_F32)
        for k in range(1, N_DEV):
            g = g + r_ref[k].astype(_F32)
        delta, m2, v2 = _adamw_math(w_ref[...], g, m_ref[...], v_ref[...])
        g_out[...] = g
        d_out[...] = delta
        m_out[...] = m2
        v_out[...] = v2

    tile = pl.BlockSpec((rows, c_all), lambda i: (i, 0))
    return pl.pallas_call(
        body, name=name, grid=(r_all // rows,),
        in_specs=[pl.BlockSpec((N_DEV, rows, c_all), lambda i: (0, i, 0)), tile, tile, tile],
        out_specs=[tile] * 4,
        out_shape=[_S(w.shape, _F32)] * 4,
        compiler_params=_cp(("arbitrary",)),
    )(recv, w, m, v)


def _adamw_small(gathered, slices, wmv):
    ng, npar = len(gathered), len(slices)

    def body(*refs):
        g_refs = refs[:ng]
        wmv_refs = refs[ng:ng + 3 * npar]
        outs = refs[ng + 3 * npar:]
        for i, (ai, r0, nr, c0, ncol) in enumerate(slices):
            g = g_refs[ai][0, pl.ds(r0, nr), pl.ds(c0, ncol)].astype(_F32)
            for k in range(1, N_DEV):
                g = g + g_refs[ai][k, pl.ds(r0, nr), pl.ds(c0, ncol)].astype(_F32)
            w_ref, m_ref, v_ref = wmv_refs[3 * i:3 * i + 3]
            delta, m2, v2 = _adamw_math(w_ref[...], g, m_ref[...], v_ref[...])
            outs[4 * i][...] = g
            outs[4 * i + 1][...] = delta
            outs[4 * i + 2][...] = m2
            outs[4 * i + 3][...] = v2
        total = g_refs[0][0, pl.ds(R_LOSS, 1), pl.ds(0, 128)]
        for k in range(1, N_DEV):
            total = total + g_refs[0][k, pl.ds(R_LOSS, 1), pl.ds(0, 128)]
        outs[4 * npar][...] = total

    flat = [t for trip in wmv for t in trip]
    out_shape = []
    for w, _, _ in wmv:
        out_shape += [_S(w.shape, _F32)] * 4
    out_shape.append(_S((1, 128), _F32))
    return pl.pallas_call(
        body, name="adamw_small", out_shape=out_shape,
        compiler_params=pltpu.CompilerParams(vmem_limit_bytes=VMEM_LIMIT),
    )(*gathered, *flat)


def _block_diag(w):
    eye = jnp.eye(8, dtype=w.dtype)
    return (w[:, :, None, :] * eye[:, None, :, None]).reshape(D_RG, D_RG)


def _diag_blocks(g):
    return jnp.concatenate([g[64 * h:64 * (h + 1), 64 * h:64 * (h + 1)] for h in range(8)], axis=0)


def _local_step(h0, tgt_p, n_valid, g_mix, w_in, vec, wr, wi, hb, g_hg, w_out_l, g_ffn, w_gu_l, w_down_l, g_fin):
    t_pad = h0.shape[0]
    me = 4 * lax.axis_index("x") + 2 * lax.axis_index("y") + lax.axis_index("c")
    p, u, y, hs, o, sc, w_out, w_gu, w_down = _mixer_fwd(h0, g_mix, w_in, wr, wi, vec, hb, g_hg,
                                                         [w_out_l, w_gu_l, w_down_l])
    w_out = w_out.reshape(D, D)
    w_down = w_down.reshape(4, FFB, D)
    h1, v, gu, act, dh2, dh2b, loss, gfin = _ffn_loss(h0, y, w_out, g_ffn, w_gu, w_down, g_fin, tgt_p, n_valid)

    g_wdown = _wgrad("wgrad_down", act, dh2b, pl.BlockSpec((1, t_pad, FFB), lambda j: (j, 0, 0)),
                     pl.BlockSpec((t_pad, D), lambda j: (0, 0)), 4, (FFB, D))
    g_wdown = g_wdown.reshape(N_DEV, D_FF // N_DEV, D)
    dgu, dh1, dh1b, dy, gffn, r_wdown = _ffn_bwd(dh2, dh2b, gu, h1, g_ffn, w_gu, w_down, w_out, [g_wdown])
    g_wgu = _wgrad("wgrad_gate_up", dgu, v, pl.BlockSpec((1, t_pad, FFB), lambda j: (j, 0, 0)),
                   pl.BlockSpec((t_pad, D), lambda j: (0, 0)), N_DEV, (FFB, D))
    g_wout = _wgrad("wgrad_out", y, dh1b, pl.BlockSpec((t_pad, D // 2), lambda j: (0, j)),
                    pl.BlockSpec((t_pad, D), lambda j: (0, 0)), 2, (D // 2, D)).reshape(N_DEV, D // N_DEV, D)
    dp, gvec, gw, r_wgu, r_wout = _mixer_bwd(p, hs, o, sc, dy, wr, wi, vec, hb, g_hg, [g_wgu, g_wout], [None, None])
    pack_c = jnp.concatenate([_diag_blocks(gw[0]), _diag_blocks(gw[1])], axis=1).astype(_BF)
    order = (me ^ jnp.array(_SEND_ORDER, jnp.int32)).astype(jnp.int32)
    dh0, r_win, all_b, all_c, r_meta, all_a = _inproj_bwd_send(dp, w_in, h0, dh1, g_mix, u, order, gffn, gfin, loss,
                                                               [gvec, pack_c])
    return dh0, (r_win, r_wgu, r_wout, r_wdown), (all_a, all_b, all_c, r_meta)


def kernel(x, meta_tokens, mix_norm_g, w_in, conv_w, conv_b, w_rgate, b_rgate, w_igate, b_igate, lru_lambda, rg_norm_g, hg_lower_bound, hg_norm_g, w_out, ffn_norm_g, w_gate_up, w_down, final_norm_g, loss_target, m_meta_tokens, m_mix_norm_g, m_w_in, m_conv_w, m_conv_b, m_w_rgate, m_b_rgate, m_w_igate, m_b_igate, m_lru_lambda, m_rg_norm_g, m_hg_lower_bound, m_hg_norm_g, m_w_out, m_ffn_norm_g, m_w_gate_up, m_w_down, m_final_norm_g, v_meta_tokens, v_mix_norm_g, v_w_in, v_conv_w, v_conv_b, v_w_rgate, v_b_rgate, v_w_igate, v_b_igate, v_lru_lambda, v_rg_norm_g, v_hg_lower_bound, v_hg_norm_g, v_w_out, v_ffn_norm_g, v_w_gate_up, v_w_down, v_final_norm_g):
    seq = x.shape[1]
    me = 4 * lax.axis_index("x") + 2 * lax.axis_index("y") + lax.axis_index("c")

    n_valid = N_META + seq
    small_l = jnp.concatenate([meta_tokens, jnp.pad(conv_w[0], ((0, 4), (0, 64)))], axis=0)
    h0, tgt_p, small_g, w_in_g, w_gu_l, w_out_l, w_down_l = _prologue(
        x[0], loss_target[0], small_l, w_in[0], [w_gate_up[0].T, w_out[0], w_down[0]])
    conv_w_full = jnp.transpose(small_g[:, N_META:N_META + 4, :64], (1, 0, 2)).reshape(4, D_RG)
    vec = jnp.concatenate([conv_b, b_rgate, b_igate, lru_lambda, rg_norm_g, jnp.zeros((3, D_RG), _F32),
                           conv_w_full, jnp.zeros((4, D_RG), _F32)], axis=0)
    wr = _block_diag(w_rgate[0]).astype(_BF)
    wi = _block_diag(w_igate[0]).astype(_BF)

    dh0, (r_win, r_wgu, r_wout, r_wdown), (all_a, all_b, all_c, meta_part) = _local_step(
        h0, tgt_p, n_valid, mix_norm_g, w_in_g, vec, wr, wi, hg_lower_bound, hg_norm_g,
        w_out_l, ffn_norm_g, w_gu_l, w_down_l, final_norm_g.reshape(1, D))
    grad_x = dh0[N_META:N_META + seq][None]

    outs = {}
    outs["w_in"] = _adamw_big("adamw_w_in", r_win, w_in[0], m_w_in[0], v_w_in[0], 256)
    outs["w_gate_up"] = [r.T for r in _adamw_big("adamw_w_gate_up", r_wgu, w_gate_up[0].T, m_w_gate_up[0].T,
                                                 v_w_gate_up[0].T, 176)]
    outs["w_out"] = _adamw_big("adamw_w_out", r_wout, w_out[0], m_w_out[0], v_w_out[0], 128)
    outs["w_down"] = _adamw_big("adamw_w_down", r_wdown, w_down[0], m_w_down[0], v_w_down[0], 176)

    convw_part = lax.dynamic_slice_in_dim(all_b[:, R_CONVW:R_CONVW + 4, :], me * 64, 64, axis=2)
    gathered = [all_a, all_b, all_c, meta_part, convw_part]
    small_params = [
        ("meta_tokens", (3, 0, N_META, 0, 128), (meta_tokens, m_meta_tokens, v_meta_tokens), (N_META, 128)),
        ("mix_norm_g", (0, R_GMIX, 1, 0, D), (mix_norm_g, m_mix_norm_g, v_mix_norm_g), (1, D)),
        ("conv_w", (4, 0, 4, 0, 64), (conv_w, m_conv_w, v_conv_w), (4, 64)),
        ("conv_b", (1, R_CONVB, 1, 0, D_RG), (conv_b, m_conv_b, v_conv_b), (1, D_RG)),
        ("w_rgate", (2, 0, 512, 0, 64), (w_rgate, m_w_rgate, v_w_rgate), (512, 64)),
        ("b_rgate", (1, R_BR, 1, 0, D_RG), (b_rgate, m_b_rgate, v_b_rgate), (1, D_RG)),
        ("w_igate", (2, 0, 512, 64, 64), (w_igate, m_w_igate, v_w_igate), (512, 64)),
        ("b_igate", (1, R_BI, 1, 0, D_RG), (b_igate, m_b_igate, v_b_igate), (1, D_RG)),
        ("lru_lambda", (1, R_LAM, 1, 0, D_RG), (lru_lambda, m_lru_lambda, v_lru_lambda), (1, D_RG)),
        ("rg_norm_g", (1, R_GRG, 1, 0, D_RG), (rg_norm_g, m_rg_norm_g, v_rg_norm_g), (1, D_RG)),
        ("hg_lower_bound", (1, R_HB0, 2, 0, D_HG), (hg_lower_bound, m_hg_lower_bound, v_hg_lower_bound), (2, D_HG)),
        ("hg_norm_g", (1, R_GHG, 1, 0, HD), (hg_norm_g, m_hg_norm_g, v_hg_norm_g), (1, HD)),
        ("ffn_norm_g", (0, R_GFFN, 1, 0, D), (ffn_norm_g, m_ffn_norm_g, v_ffn_norm_g), (1, D)),
        ("final_norm_g", (0, R_GFIN, 1, 0, D), (final_norm_g, m_final_norm_g, v_final_norm_g), (1, D)),
    ]
    res = _adamw_small(gathered, [s[1] for s in small_params],
                       [tuple(t.reshape(s[3]) for t in s[2]) for s in small_params])
    for i, s in enumerate(small_params):
        outs[s[0]] = [r.reshape(s[2][0].shape) for r in res[4 * i:4 * i + 4]]
    for n, ref in (("w_in", w_in), ("w_gate_up", w_gate_up), ("w_out", w_out), ("w_down", w_down)):
        outs[n] = [r.reshape(ref.shape) for r in outs[n]]

    loss_all = res[4 * len(small_params)][0, 0]
    order = ["meta_tokens", "mix_norm_g", "w_in", "conv_w", "conv_b", "w_rgate", "b_rgate", "w_igate", "b_igate",
             "lru_lambda", "rg_norm_g", "hg_lower_bound", "hg_norm_g", "w_out", "ffn_norm_g", "w_gate_up", "w_down",
             "final_norm_g"]
    return (loss_all, grad_x, *[outs[n][0] for n in order], *[outs[n][1] for n in order],
            *[outs[n][2] for n in order], *[outs[n][3] for n in order])
```

```python
import functools

import jax
import jax.numpy as jnp
from jax import lax
from jax.experimental import pallas as pl
from jax.experimental.pallas import tpu as pltpu

_BF = jnp.bfloat16
_F32 = jnp.float32
_S = jax.ShapeDtypeStruct
_MESH = pl.DeviceIdType.MESH

N_DEV = 8
N_META = 16
D = 1024
D_RG = 512
D_HG = 512
HD = 128
NH = D_HG // HD
D_IN = 3072
D_FF = 2816
FFB = D_FF // 4
WIN_B = D_IN // N_DEV
WIN_P = 2 * WIN_B
EPS = 1e-6
LRU_C = 8.0
TM = 320
HC = 64
VMEM_LIMIT = 62 * 1024 * 1024

ADAM_LR = 0.001
ADAM_B1 = 0.9
ADAM_B2 = 0.999
ADAM_EPS = 1e-08
ADAM_WD = 0.01
ADAM_STEP = 10

_SEND_ORDER = (6, 4, 2, 7, 5, 3, 1, 0)

R_CONVB, R_BR, R_BI, R_LAM, R_GRG, R_HB0, R_HB1, R_GHG, R_CONVW = 0, 1, 2, 3, 4, 5, 6, 7, 8
R_GMIX, R_GFFN, R_GFIN, R_LOSS = 0, 1, 2, 3


def _cp(sem=None, **kw):
    return pltpu.CompilerParams(dimension_semantics=sem, vmem_limit_bytes=VMEM_LIMIT, **kw)


def _dot(a, b):
    return jnp.dot(a, b, preferred_element_type=_F32)


def _dot_nt(a, b):
    return lax.dot_general(a, b, (((1,), (1,)), ((), ())), preferred_element_type=_F32)


def _dot_tn(a, b):
    return lax.dot_general(a, b, (((0,), (0,)), ((), ())), preferred_element_type=_F32)


def _sigmoid(x):
    return 0.5 * jnp.tanh(0.5 * x) + 0.5


def _dsilu(x, s):
    return s * (1.0 + x * (1.0 - s))


_GELU_C = 0.7978845608028654


def _gelu_parts(x):
    t = jnp.tanh(_GELU_C * (x + 0.044715 * (x * x * x)))
    g = 0.5 * x * (1.0 + t)
    dg = 0.5 * (1.0 + t) + 0.5 * x * (1.0 - t * t) * (_GELU_C * (1.0 + 3.0 * 0.044715 * (x * x)))
    return g, dg


def _softplus(z):
    e = jnp.exp(-jnp.abs(z))
    w = 1.0 + e
    l1p = jnp.where(w == 1.0, e, jnp.log(w) * e / jnp.where(w == 1.0, 1.0, w - 1.0))
    return jnp.maximum(z, 0.0) + l1p


def _rms_fwd(x):
    r = lax.rsqrt(jnp.mean(x * x, axis=-1, keepdims=True) + EPS)
    return x * r, r


def _rms_bwd(dyg, n, r):
    return r * (dyg - n * jnp.mean(dyg * n, axis=-1, keepdims=True))


def _full(shape):
    nd = len(shape)
    return pl.BlockSpec(shape, lambda i: (0,) * nd)


def _const(shape):
    nd = len(shape)
    return pl.BlockSpec(shape, lambda i: (0,) * nd, pipeline_mode=pl.Buffered(1))


def _carry_gather(gather, i, nt, early=0):
    @pl.when(i == 0)
    def _():
        gather.start()

    def tail():
        for j in range(3):
            if early:
                @pl.when(i == min(nt // 3 + j, nt - 1))
                def _(j=j):
                    gather.forward(j, range(early))

            @pl.when(i == max(nt - 4 + j, 0))
            def _(j=j):
                gather.forward(j, range(early, gather.na))

        @pl.when(i == nt - 1)
        def _():
            gather.finish()

    return tail


def _pair_place(ref, block):
    return ref.at[block // 2, :, pl.ds(pl.multiple_of((block % 2) * WIN_B, WIN_B), WIN_B)]


def _rg_gates(xc, wr_ref, wi_ref, vec_ref):
    xcb = xc.astype(_BF)
    r = _sigmoid(_dot(xcb, wr_ref[...]) + vec_ref[R_BR:R_BR + 1, :])
    ig = _sigmoid(_dot(xcb, wi_ref[...]) + vec_ref[R_BI:R_BI + 1, :])
    nsp8 = -LRU_C * _softplus(-vec_ref[R_LAM:R_LAM + 1, :])
    la = nsp8 * r
    a = jnp.exp(la)
    th = jnp.tanh(la)
    s = jnp.sqrt(-2.0 * th / (1.0 - th))
    return r, ig, a, s, nsp8


def _conv(xbuf, vec_ref):
    acc = vec_ref[R_CONVW:R_CONVW + 1, :] * xbuf[pl.ds(5, TM), :]
    for j in range(1, 4):
        acc = acc + vec_ref[R_CONVW + j:R_CONVW + j + 1, :] * xbuf[pl.ds(5 + j, TM), :]
    return vec_ref[R_CONVB:R_CONVB + 1, :] + acc


def _dot3(m01, x):
    hi = x.astype(_BF)
    r1 = x - hi.astype(_F32)
    mid = r1.astype(_BF)
    lo = (r1 - mid.astype(_F32)).astype(_BF)
    return (_dot(m01, lo) + _dot(m01, mid)) + _dot(m01, hi)


def _chunk_dot3(m01, x):
    return jnp.concatenate([_dot3(m01, x[HC * c:HC * (c + 1), :]) for c in range(x.shape[0] // HC)], axis=0)


def _chunk_masks():
    row = lax.broadcasted_iota(jnp.int32, (HC, HC), 0)
    col = lax.broadcasted_iota(jnp.int32, (HC, HC), 1)
    return (row >= col).astype(_BF), (col >= row).astype(_BF), jnp.ones((HC, HC), _BF)


def _per_chunk_rows(x, r):
    return jnp.concatenate([jnp.broadcast_to(x[HC * c + r:HC * c + r + 1, :], (HC, x.shape[1]))
                            for c in range(TM // HC)], axis=0)


def _hg_prep(p_ref, lb, tri):
    hq = p_ref[:, pl.ds(2 * D_RG, D_HG)]
    hf = p_ref[:, pl.ds(2 * D_RG + D_HG, D_HG)]
    sq = _sigmoid(hq)
    q = hq * sq
    sg = _sigmoid(hf)
    f = lb + (1.0 - lb) * sg
    k = 1.0 - f
    b = _chunk_dot3(tri, jnp.log(f))
    bm = _per_chunk_rows(b, HC // 2 - 1)
    bl = _per_chunk_rows(b, HC - 1)
    e_q = jnp.exp(b - bm)
    e_k = jnp.exp(bm - b)
    e_b = jnp.exp(b)
    e_l = jnp.exp(bl - b)
    return dict(hq=hq, sq=sq, q=q, sg=sg, f=f, k=k, e_q=e_q, e_k=e_k, e_b=e_b, e_l=e_l,
                qd=q * e_q, kd=k * e_k, qe=q * e_b, ke=k * e_l, e_end=jnp.exp(bl))


def _mixer_fwd(h0, g_mix, w_in, wr, wi, vec, hb, g_hg, shards):
    t_pad = h0.shape[0]
    nt = t_pad // TM
    nc_t = TM // HC
    nsh = len(shards)

    def body(h_ref, gmix_ref, win_ref, wr_ref, wi_ref, vec_ref, hb_ref, ghg_ref, *rest):
        sh_refs, rest = rest[:nsh], rest[nsh:]
        pout_ref, uout_ref, y_ref, hs_ref, o_ref, sc_ref = rest[:6]
        gath_refs, rest = rest[6:6 + nsh], rest[6 + nsh:]
        xbuf, a_s, b_s, hcar, st, qd_s, kd_s, qe_s, ke_s, v_s, u_s, p_s, p_ref = rest[:13]
        i = pl.program_id(0)
        tail = _carry_gather(_Gather(sh_refs, gath_refs, rest[13:]), i, nt + 1, early=1)

        @pl.when(i == 0)
        def _():
            p_s[...] = jnp.zeros_like(p_s)

        p_ref[...] = p_s[...]

        @pl.when(i <= 1)
        def _():
            xbuf[pl.ds(0, 8), :] = jnp.zeros((8, D_RG), _F32)
            hcar[...] = jnp.zeros_like(hcar)
            st[...] = jnp.zeros_like(st)

        n_h, _ = _rms_fwd(h_ref[...])
        u = (n_h * gmix_ref[...]).astype(_BF)
        uout_ref[...] = u
        pieces = [(j, k) for j in range(4) for k in range(WIN_P // 256)]

        def project(count):
            for _ in range(count):
                j, k = pieces.pop(0)
                blk = _dot(u, win_ref[j, :, pl.ds(256 * k, 256)])
                p_s[:, pl.ds(WIN_P * j + 256 * k, 256)] = blk
                pout_ref[:, pl.ds(WIN_P * j + 256 * k, 256)] = blk

        x = p_ref[:, pl.ds(0, D_RG)]
        xbuf[pl.ds(8, TM), :] = x
        xc = _conv(xbuf, vec_ref)
        xbuf[pl.ds(0, 8), :] = x[TM - 8:, :]
        r, ig, a, s, _ = _rg_gates(xc, wr_ref, wi_ref, vec_ref)
        a_s[...] = a
        b_s[...] = s * (ig * xc)

        def step(t, h):
            h = a_s[pl.ds(t, 1), :] * h + b_s[pl.ds(t, 1), :]
            hs_ref[pl.ds(t, 1), :] = h
            return h

        hcar[pl.ds(0, 1), :] = lax.fori_loop(0, TM, step, hcar[pl.ds(0, 1), :], unroll=8)
        gel, _ = _gelu_parts(p_ref[:, pl.ds(D_RG, D_RG)])
        n, _ = _rms_fwd(gel * hs_ref[...])
        y_ref[:, pl.ds(0, D_RG)] = (n * vec_ref[R_GRG:R_GRG + 1, :]).astype(_BF)

        lb = _sigmoid(hb_ref[0:1, :] - hb_ref[1:2, :])
        tri, _, _ = _chunk_masks()
        q = _hg_prep(p_ref, lb, tri)
        for name, ref in (("qd", qd_s), ("kd", kd_s), ("qe", qe_s), ("ke", ke_s)):
            ref[...] = q[name].astype(_BF)
        v_s[...] = p_ref[:, pl.ds(2 * D_RG + 2 * D_HG, D_HG)].astype(_BF)
        e_end = q["e_end"]
        causal = (lax.broadcasted_iota(jnp.int32, (HC, HC), 0) >= lax.broadcasted_iota(jnp.int32, (HC, HC), 1))
        for c in range(nc_t):
            for h in range(NH):
                rs, cs = pl.ds(HC * c, HC), pl.ds(HD * h, HD)
                amat = jnp.where(causal, _dot_nt(qd_s[rs, cs], kd_s[rs, cs]), 0.0)
                o_ref[rs, cs] = _dot(amat.astype(_BF), v_s[rs, cs])
                u_s[NH * c + h] = _dot_tn(v_s[rs, cs], ke_s[rs, cs])
                if pieces:
                    project(1)
        assert not pieces
        for h in range(NH):
            cs = pl.ds(HD * h, HD)
            s_run = st[h]
            for c in range(nc_t):
                rs = pl.ds(HC * c, HC)
                sc_ref[c, h] = s_run
                o_ref[rs, cs] += _dot_nt(qe_s[rs, cs], s_run.astype(_BF))
                s_run = e_end[HC * c:HC * c + 1, HD * h:HD * (h + 1)] * s_run + u_s[NH * c + h]
            st[h] = s_run
        for h in range(NH):
            cs = pl.ds(HD * h, HD)
            n_o, _ = _rms_fwd(o_ref[:, cs])
            hg = p_ref[:, pl.ds(2 * D_RG + 3 * D_HG + HD * h, HD)]
            y_ref[:, pl.ds(D_RG + HD * h, HD)] = ((n_o * ghg_ref[...]) * (hg * _sigmoid(hg))).astype(_BF)

        tail()

    hbm = pl.BlockSpec(memory_space=pl.ANY)

    def proj(i):
        return jnp.minimum(i, nt - 1)

    def mixed(i):
        return jnp.maximum(i - 1, 0)

    return pl.pallas_call(
        body, name="mixer_fwd", grid=(nt + 1,),
        in_specs=[pl.BlockSpec((TM, D), lambda i: (proj(i), 0)), _full((1, D)), _const((4, D, WIN_P)),
                  _full((D_RG, D_RG)), _full((D_RG, D_RG)),
                  _full((16, D_RG)), _full((2, D_HG)), _full((1, HD))] + [hbm] * nsh,
        out_specs=[pl.BlockSpec((TM, D_IN), lambda i: (proj(i), 0)), pl.BlockSpec((TM, D), lambda i: (proj(i), 0)),
                   pl.BlockSpec((TM, D), lambda i: (mixed(i), 0)), pl.BlockSpec((TM, D_RG), lambda i: (mixed(i), 0)),
                   pl.BlockSpec((TM, D_HG), lambda i: (mixed(i), 0)),
                   pl.BlockSpec((nc_t, NH, HD, HD), lambda i: (mixed(i), 0, 0, 0))] + [hbm] * nsh,
        out_shape=[_S((t_pad, D_IN), _F32), _S((t_pad, D), _BF),
                   _S((t_pad, D), _BF), _S((t_pad, D_RG), _F32), _S((t_pad, D_HG), _F32),
                   _S((t_pad // HC, NH, HD, HD), _F32)] + [_S((N_DEV,) + s.shape, s.dtype) for s in shards],
        scratch_shapes=[pltpu.VMEM((TM + 8, D_RG), _F32), pltpu.VMEM((TM, D_RG), _F32),
                        pltpu.VMEM((TM, D_RG), _F32), pltpu.VMEM((8, D_RG), _F32),
                        pltpu.VMEM((NH, HD, HD), _F32)] + [pltpu.VMEM((TM, D_HG), _BF) for _ in range(5)]
        + [pltpu.VMEM((nc_t * NH, HD, HD), _F32), pltpu.VMEM((TM, D_IN), _F32), pltpu.VMEM((TM, D_IN), _F32)]
        + _sem_shapes(nsh),
        compiler_params=_cp(("arbitrary",)),
    )(h0, g_mix, w_in, wr, wi, vec, hb, g_hg, *shards)


def _ffn_loss(h0, y, w_out, g_ffn, w_gu, w_down, g_fin, tgt, n_valid):
    t_pad = h0.shape[0]

    def body(h_ref, y_ref, wo_ref, gffn_ref, wgu_ref, wd_ref, g_ref, t_ref,
             h1_ref, v_ref, gu_ref, act_ref, dh2_ref, dh2b_ref, loss_ref, gfin_ref):
        i = pl.program_id(0)

        @pl.when(i == 0)
        def _():
            loss_ref[...] = jnp.zeros_like(loss_ref)
            gfin_ref[...] = jnp.zeros_like(gfin_ref)

        h1 = h_ref[...] + _dot(y_ref[...], wo_ref[...])
        h1_ref[...] = h1
        n1, _ = _rms_fwd(h1)
        vb = (n1 * gffn_ref[...]).astype(_BF)
        v_ref[...] = vb
        h2 = h1
        for b in range(4):
            gate = _dot_nt(vb, wgu_ref[b])
            up = _dot_nt(vb, wgu_ref[4 + b])
            gu_ref[b] = gate
            gu_ref[4 + b] = up
            act = ((gate * _sigmoid(gate)) * up).astype(_BF)
            act_ref[b] = act
            h2 = h2 + _dot(act, wd_ref[b])
        n, r = _rms_fwd(h2)
        out = n * g_ref[...]
        row = i * TM + lax.broadcasted_iota(jnp.int32, (TM, 1), 0)
        valid = (row >= N_META) & (row < n_valid)
        err = jnp.where(valid, out - t_ref[...], 0.0)
        loss_ref[...] += (0.5 / D) * jnp.sum(err * err)
        dout = err * (1.0 / D)
        gfin_ref[...] += jnp.sum(dout * n, axis=0, keepdims=True)
        dh2 = _rms_bwd(dout * g_ref[...], n, r)
        dh2_ref[...] = dh2
        dh2b_ref[...] = dh2.astype(_BF)

    tile = pl.BlockSpec((TM, D), lambda i: (i, 0))
    return pl.pallas_call(
        body, name="ffn_loss", grid=(t_pad // TM,),
        in_specs=[tile, tile, _const((D, D)), _full((1, D)),
                  _const((N_DEV, FFB, D)), _const((4, FFB, D)), _full((1, D)), tile],
        out_specs=[tile, tile,
                   pl.BlockSpec((N_DEV, TM, FFB), lambda i: (0, i, 0)), pl.BlockSpec((4, TM, FFB), lambda i: (0, i, 0)),
                   tile, tile, _full((8, 128)), _full((1, D))],
        out_shape=[_S((t_pad, D), _F32), _S((t_pad, D), _BF),
                   _S((N_DEV, t_pad, FFB), _F32), _S((4, t_pad, FFB), _BF), _S((t_pad, D), _F32),
                   _S((t_pad, D), _BF), _S((8, 128), _F32), _S((1, D), _F32)],
        compiler_params=_cp(("arbitrary",)),
    )(h0, y, w_out, g_ffn, w_gu, w_down, g_fin, tgt)


def _ffn_bwd(dh2, dh2b, gu, h1, g_ffn, w_gu, w_down, w_out, scatter):
    t_pad = dh2.shape[0]
    nsc = len(scatter)
    nt = t_pad // TM

    def body(dh2_ref, dh2b_ref, gu_ref, h1_ref, g_ref, wgu_ref, wd_ref, wo_ref, *rest):
        dgu_ref, dh1_ref, dh1b_ref, dy_ref, gffn_ref = rest[nsc:nsc + 5]
        exchange = _Exchange(rest[:nsc], [], rest[nsc + 5:2 * nsc + 5], rest[2 * nsc + 5:])
        i = pl.program_id(0)

        @pl.when(i == 0)
        def _():
            exchange.start()
            gffn_ref[...] = jnp.zeros_like(gffn_ref)

        db = dh2b_ref[...]
        dv = jnp.zeros((TM, D), _F32)
        for b in range(4):
            dact = _dot_nt(db, wd_ref[b])
            gate = gu_ref[b]
            up = gu_ref[4 + b]
            sg = _sigmoid(gate)
            dgate = ((dact * up) * _dsilu(gate, sg)).astype(_BF)
            dup = (dact * (gate * sg)).astype(_BF)
            dgu_ref[b] = dgate
            dgu_ref[4 + b] = dup
            dv = dv + _dot(dgate, wgu_ref[b]) + _dot(dup, wgu_ref[4 + b])
        n, r = _rms_fwd(h1_ref[...])
        gffn_ref[...] += jnp.sum(dv * n, axis=0, keepdims=True)
        dh1 = dh2_ref[...] + _rms_bwd(dv * g_ref[...], n, r)
        dh1_ref[...] = dh1
        dh1b = dh1.astype(_BF)
        dh1b_ref[...] = dh1b
        dy_ref[...] = _dot_nt(dh1b, wo_ref[...])

        @pl.when(i == nt - 1)
        def _():
            exchange.finish()

    tile = pl.BlockSpec((TM, D), lambda i: (i, 0))
    hbm = pl.BlockSpec(memory_space=pl.ANY)
    return pl.pallas_call(
        body, name="ffn_bwd", grid=(nt,),
        in_specs=[tile, tile, pl.BlockSpec((N_DEV, TM, FFB), lambda i: (0, i, 0)), tile, _full((1, D)),
                  _const((N_DEV, FFB, D)), _const((4, FFB, D)), _const((D, D))] + [hbm] * nsc,
        out_specs=[pl.BlockSpec((N_DEV, TM, FFB), lambda i: (0, i, 0)), tile, tile, tile, _full((1, D))] + [hbm] * nsc,
        out_shape=[_S((N_DEV, t_pad, FFB), _BF), _S((t_pad, D), _F32), _S((t_pad, D), _BF),
                   _S((t_pad, D), _F32), _S((1, D), _F32)] + _recv_shapes(scatter, [None] * nsc),
        scratch_shapes=_sem_shapes(nsc),
        compiler_params=_cp(("arbitrary",)),
    )(dh2, dh2b, gu, h1, g_ffn, w_gu, w_down, w_out, *scatter)


def _mixer_bwd(p, hs, o, sc, dy, wr, wi, vec, hb, g_hg, scatter, windows):
    t_pad = p.shape[0]
    nt = t_pad // TM
    nc_t = TM // HC
    nsc = len(scatter)

    def rev(i):
        return nt - 1 - i

    def body(p_ref, pprev_ref, hs_ref, hprev_ref, o_ref, sc_ref, dy_ref, wr_ref, wi_ref, vec_ref, hb_ref, ghg_ref,
             *rest):
        send_refs, rest = rest[:nsc], rest[nsc:]
        dp_ref, gvec_ref, gw_ref = rest[:3]
        recv_refs, rest = rest[3:3 + nsc], rest[3 + nsc:]
        xbuf, hbuf, dbuf, a_s, g_s, ccar, dst = rest[:7]
        qd_s, kd_s, qe_s, ke_s, v_s, do_s, dqd_s, dkd_s, dqe_s, dke_s, dv_s, w_s, dend_s = rest[7:20]
        exchange = _Exchange(send_refs, [], recv_refs, rest[20:], windows)
        i = pl.program_id(0)
        first_tile = i == nt - 1

        @pl.when(i == 0)
        def _():
            exchange.start()
            gvec_ref[...] = jnp.zeros_like(gvec_ref)
            gw_ref[...] = jnp.zeros_like(gw_ref)
            dbuf[pl.ds(TM, 8), :] = jnp.zeros((8, D_RG), _F32)
            ccar[...] = jnp.zeros_like(ccar)
            dst[...] = jnp.zeros_like(dst)

        def acc(row, val):
            gvec_ref[row:row + 1, :] += jnp.sum(val, axis=0, keepdims=True)

        keep = jnp.where(first_tile, 0.0, 1.0)
        x = p_ref[:, pl.ds(0, D_RG)]
        xbuf[pl.ds(0, 8), :] = pprev_ref[...] * keep
        xbuf[pl.ds(8, TM), :] = x
        xc = _conv(xbuf, vec_ref)
        r, ig, a, s, nsp8 = _rg_gates(xc, wr_ref, wi_ref, vec_ref)
        h = hs_ref[...]
        hbuf[pl.ds(0, 8), :] = hprev_ref[...] * keep
        hbuf[pl.ds(8, TM), :] = h
        hm1 = hbuf[pl.ds(7, TM), :]
        gr = p_ref[:, pl.ds(D_RG, D_RG)]
        gel, dgel = _gelu_parts(gr)
        n, rr = _rms_fwd(gel * h)
        dyn = dy_ref[:, pl.ds(0, D_RG)]
        acc(R_GRG, dyn * n)
        dpre = _rms_bwd(dyn * vec_ref[R_GRG:R_GRG + 1, :], n, rr)
        dp_ref[:, pl.ds(D_RG, D_RG)] = ((dpre * h) * dgel).astype(_BF)
        a_s[...] = a
        g_s[...] = dpre * gel

        def step(k, c):
            t = TM - 1 - k
            g = g_s[pl.ds(t, 1), :] + c
            g_s[pl.ds(t, 1), :] = g
            return a_s[pl.ds(t, 1), :] * g

        ccar[pl.ds(0, 1), :] = lax.fori_loop(0, TM, step, ccar[pl.ds(0, 1), :], unroll=8)
        gt = g_s[...]
        da = gt * hm1
        ixc = ig * xc
        ds = gt * ixc
        dig = (gt * s) * xc
        dxc = (gt * s) * ig
        dla = da * a - ds * ((a * a) / s)
        lam = vec_ref[R_LAM:R_LAM + 1, :]
        gvec_ref[R_LAM:R_LAM + 1, :] += jnp.sum(dla * r, axis=0, keepdims=True) * (LRU_C * _sigmoid(-lam))
        dzr = (dla * nsp8) * (r * (1.0 - r))
        dzi = dig * (ig * (1.0 - ig))
        acc(R_BR, dzr)
        acc(R_BI, dzi)
        xcb = xc.astype(_BF)
        dzrb = dzr.astype(_BF)
        dzib = dzi.astype(_BF)
        gw_ref[0] += _dot_tn(xcb, dzrb)
        gw_ref[1] += _dot_tn(xcb, dzib)
        dxc = dxc + _dot_nt(dzrb, wr_ref[...]) + _dot_nt(dzib, wi_ref[...])
        acc(R_CONVB, dxc)
        for j in range(4):
            acc(R_CONVW + j, dxc * xbuf[pl.ds(5 + j, TM), :])
        dbuf[pl.ds(0, TM), :] = dxc
        dx = vec_ref[R_CONVW + 3:R_CONVW + 4, :] * dxc
        for j in range(3):
            dx = dx + vec_ref[R_CONVW + j:R_CONVW + j + 1, :] * dbuf[pl.ds(3 - j, TM), :]
        dbuf[pl.ds(TM, 8), :] = dxc[0:8, :]
        dp_ref[:, pl.ds(0, D_RG)] = dx.astype(_BF)

        lb = _sigmoid(hb_ref[0:1, :] - hb_ref[1:2, :])
        tri, tri_rev, ones = _chunk_masks()
        q = _hg_prep(p_ref, lb, tri)
        qdb, kdb = q["qd"].astype(_BF), q["kd"].astype(_BF)
        qd_s[...] = qdb
        kd_s[...] = kdb
        qe_s[...] = q["qe"].astype(_BF)
        ke_s[...] = q["ke"].astype(_BF)
        v_s[...] = p_ref[:, pl.ds(2 * D_RG + 2 * D_HG, D_HG)].astype(_BF)
        e_end = q["e_end"]
        ghg = ghg_ref[...]
        for h in range(NH):
            cs = pl.ds(HD * h, HD)
            hg = p_ref[:, pl.ds(2 * D_RG + 3 * D_HG + HD * h, HD)]
            sh = _sigmoid(hg)
            n_o, r_o = _rms_fwd(o_ref[:, cs])
            dyh = dy_ref[:, pl.ds(D_RG + HD * h, HD)]
            dp_ref[:, pl.ds(2 * D_RG + 3 * D_HG + HD * h, HD)] = ((dyh * (n_o * ghg)) * _dsilu(hg, sh)).astype(_BF)
            dn = dyh * (hg * sh)
            gvec_ref[R_GHG:R_GHG + 1, pl.ds(0, HD)] += jnp.sum(dn * n_o, axis=0, keepdims=True)
            do_s[:, cs] = _rms_bwd(dn * ghg, n_o, r_o).astype(_BF)
        causal = (lax.broadcasted_iota(jnp.int32, (HC, HC), 0) >= lax.broadcasted_iota(jnp.int32, (HC, HC), 1))
        for c in range(nc_t):
            for h in range(NH):
                rs, cs = pl.ds(HC * c, HC), pl.ds(HD * h, HD)
                qd_c, kd_c, do_c = qd_s[rs, cs], kd_s[rs, cs], do_s[rs, cs]
                amat = jnp.where(causal, _dot_nt(qd_c, kd_c), 0.0).astype(_BF)
                da_m = jnp.where(causal, _dot_nt(do_c, v_s[rs, cs]), 0.0).astype(_BF)
                dqd_s[rs, cs] = _dot(da_m, kd_c)
                dkd_s[rs, cs] = _dot_tn(da_m, qd_c)
                dqe_s[rs, cs] = _dot(do_c, sc_ref[c, h].astype(_BF))
                dv_s[rs, cs] = _dot_tn(amat, do_c)
                w_s[NH * c + h] = _dot_tn(do_c, qe_s[rs, cs])
        for h in range(NH):
            cs = pl.ds(HD * h, HD)
            d_run = dst[h]
            for c in reversed(range(nc_t)):
                rs = pl.ds(HC * c, HC)
                d_b = d_run.astype(_BF)
                dke_s[rs, cs] = _dot(v_s[rs, cs], d_b)
                dp_ref[rs, pl.ds(2 * D_RG + 2 * D_HG + HD * h, HD)] = (
                    dv_s[rs, cs] + _dot_nt(ke_s[rs, cs], d_b)).astype(_BF)
                dend_s[pl.ds(c, 1), cs] = jnp.sum(sc_ref[c, h] * d_run, axis=0, keepdims=True)
                d_run = w_s[NH * c + h] + e_end[HC * c:HC * c + 1, HD * h:HD * (h + 1)] * d_run
            dst[h] = d_run
        dqd, dkd, dqe, dke = dqd_s[...], dkd_s[...], dqe_s[...], dke_s[...]
        dq = dqd * q["e_q"] + dqe * q["e_b"]
        dk = dkd * q["e_k"] + dke * q["e_l"]
        dkeke = dke * q["ke"]
        db = dqd * qdb.astype(_F32) - dkd * kdb.astype(_F32) + dqe * q["qe"] - dkeke
        d_end = jnp.concatenate([jnp.broadcast_to(dend_s[pl.ds(c, 1), :], (HC, D_HG)) for c in range(nc_t)], axis=0)
        dlf = _chunk_dot3(tri_rev, db) + _chunk_dot3(ones, dkeke) + d_end * e_end
        df = dlf / q["f"] - dk
        sg = q["sg"]
        gvec_ref[R_HB0:R_HB0 + 1, :] += jnp.sum(df * (1.0 - sg), axis=0, keepdims=True)
        dp_ref[:, pl.ds(2 * D_RG, D_HG)] = (dq * _dsilu(q["hq"], q["sq"])).astype(_BF)
        dp_ref[:, pl.ds(2 * D_RG + D_HG, D_HG)] = ((df * (1.0 - lb)) * (sg * (1.0 - sg))).astype(_BF)

        @pl.when(i == nt - 1)
        def _():
            glb = gvec_ref[R_HB0:R_HB0 + 1, :] * (lb * (1.0 - lb))
            gvec_ref[R_HB0:R_HB0 + 1, :] = glb
            gvec_ref[R_HB1:R_HB1 + 1, :] = -glb
            exchange.finish()

    hbm = pl.BlockSpec(memory_space=pl.ANY)
    return pl.pallas_call(
        body, name="mixer_bwd", grid=(nt,),
        in_specs=[pl.BlockSpec((TM, D_IN), lambda i: (rev(i), 0)),
                  pl.BlockSpec((8, D_RG), lambda i: (jnp.maximum(rev(i) * (TM // 8) - 1, 0), 0)),
                  pl.BlockSpec((TM, D_RG), lambda i: (rev(i), 0)),
                  pl.BlockSpec((8, D_RG), lambda i: (jnp.maximum(rev(i) * (TM // 8) - 1, 0), 0)),
                  pl.BlockSpec((TM, D_HG), lambda i: (rev(i), 0)),
                  pl.BlockSpec((nc_t, NH, HD, HD), lambda i: (rev(i), 0, 0, 0)),
                  pl.BlockSpec((TM, D), lambda i: (rev(i), 0)),
                  _full((D_RG, D_RG)), _full((D_RG, D_RG)), _full((16, D_RG)), _full((2, D_HG)), _full((1, HD))]
        + [hbm] * nsc,
        out_specs=[pl.BlockSpec((TM, D_IN), lambda i: (rev(i), 0)), _full((16, D_RG)), _full((2, D_RG, D_RG))]
        + [hbm] * nsc,
        out_shape=[_S((t_pad, D_IN), _BF), _S((16, D_RG), _F32), _S((2, D_RG, D_RG), _F32)]
        + _recv_shapes(scatter, windows),
        scratch_shapes=[pltpu.VMEM((TM + 8, D_RG), _F32), pltpu.VMEM((TM + 8, D_RG), _F32),
                        pltpu.VMEM((TM + 8, D_RG), _F32), pltpu.VMEM((TM, D_RG), _F32),
                        pltpu.VMEM((TM, D_RG), _F32), pltpu.VMEM((8, D_RG), _F32),
                        pltpu.VMEM((NH, HD, HD), _F32)]
        + [pltpu.VMEM((TM, D_HG), _BF) for _ in range(6)] + [pltpu.VMEM((TM, D_HG), _F32) for _ in range(5)]
        + [pltpu.VMEM((nc_t * NH, HD, HD), _F32), pltpu.VMEM((8, D_HG), _F32)] + _sem_shapes(nsc),
        compiler_params=_cp(("arbitrary",)),
    )(p, p, hs, hs, o, sc, dy, wr, wi, vec, hb, g_hg, *scatter)


def _inproj_bwd_send(dp, w_in, h0, dh1, g_mix, u, order, gffn, gfin, loss, to_all):
    t_pad = dp.shape[0]
    rb = TM
    n_steps = N_DEV + t_pad // rb
    na = len(to_all)

    def body(order_ref, dpc_ref, dpr_ref, u_ref, w_ref, h_ref, dh1_ref, g_ref, gffn_ref, gfin_ref, loss_ref, *rest):
        all_in = rest[:na]
        dh0_ref, recv_ref = rest[na:na + 2]
        all_out = rest[na + 2:2 * na + 2]
        meta_ref, alla_ref = rest[2 * na + 2:2 * na + 4]
        buf, pack, meta, blk_send, blk_recv, blk_local = rest[2 * na + 4:2 * na + 10]
        exchange = _Exchange([], all_in, all_out, rest[2 * na + 10:2 * na + 13])
        last = _Exchange([meta], [pack], [meta_ref, alla_ref], rest[2 * na + 13:])
        s = pl.program_id(0)
        x, y, c = _coords()
        me = 4 * x + 2 * y + c

        def send(step):
            r = _SEND_ORDER[step]
            return pltpu.make_async_remote_copy(
                src_ref=buf.at[step], dst_ref=recv_ref.at[me], send_sem=blk_send.at[step], recv_sem=blk_recv.at[r - 1],
                device_id=(x ^ (r >> 2), y ^ ((r >> 1) & 1), c ^ (r & 1)), device_id_type=_MESH)

        @pl.when(s == 0)
        def _():
            exchange.start()
            pack[...] = jnp.zeros_like(pack)

        @pl.when(s < N_DEV)
        def _():
            buf[s] = _dot_tn(u_ref[...], dpc_ref[...]).astype(_BF)

            for step in range(N_DEV - 1):
                @pl.when(s == step)
                def _(step=step):
                    send(step).start()

        @pl.when(s >= N_DEV)
        def _():
            du = jnp.zeros((rb, D), _F32)
            for j in range(4):
                du = du + _dot_nt(dpr_ref[:, WIN_P * j:WIN_P * (j + 1)], w_ref[j])
            n, r = _rms_fwd(h_ref[...])
            pack[R_GMIX:R_GMIX + 1, :] += jnp.sum(du * n, axis=0, keepdims=True)
            dh0 = dh1_ref[...] + _rms_bwd(du * g_ref[...], n, r)
            dh0_ref[...] = dh0

            @pl.when(s == N_DEV)
            def _():
                for k in range(N_DEV):
                    meta[k] = dh0[0:N_META, 128 * k:128 * (k + 1)]

        @pl.when(s == n_steps - 1)
        def _():
            pack[R_GFFN:R_GFFN + 1, :] = gffn_ref[...]
            pack[R_GFIN:R_GFIN + 1, :] = gfin_ref[...]
            pack[R_LOSS:R_LOSS + 1, pl.ds(0, 128)] = loss_ref[0:1, :]
            last.start()
            mine = pltpu.make_async_copy(buf.at[N_DEV - 1], recv_ref.at[me], blk_local.at[0])
            mine.start()
            for step in range(N_DEV - 1):
                send(step).wait_send()
            for r in range(1, N_DEV):
                px, py, pc = x ^ (r >> 2), y ^ ((r >> 1) & 1), c ^ (r & 1)
                pltpu.make_async_remote_copy(
                    src_ref=buf.at[0], dst_ref=recv_ref.at[4 * px + 2 * py + pc], send_sem=blk_send.at[0],
                    recv_sem=blk_recv.at[r - 1], device_id=(px, py, pc), device_id_type=_MESH).wait_recv()
            mine.wait()
            exchange.finish()
            last.finish()

    hbm = pl.BlockSpec(memory_space=pl.ANY)
    rows = pl.BlockSpec((rb, D), lambda s, order: (jnp.maximum(s - N_DEV, 0), 0))
    one = pl.BlockSpec((1, D), lambda s, order: (0, 0))
    res = pl.pallas_call(
        body, name="inproj_bwd_send",
        grid_spec=pltpu.PrefetchScalarGridSpec(
            num_scalar_prefetch=1, grid=(n_steps,),
            in_specs=[pl.BlockSpec((t_pad, WIN_B), lambda s, order: (0, order[jnp.minimum(s, N_DEV - 1)])),
                      pl.BlockSpec((rb, D_IN), lambda s, order: (jnp.maximum(s - N_DEV, 0), 0)),
                      pl.BlockSpec((t_pad, D), lambda s, order: (0, 0), pipeline_mode=pl.Buffered(1)),
                      pl.BlockSpec((4, D, WIN_P), lambda s, order: (0, 0, 0), pipeline_mode=pl.Buffered(1)),
                      rows, rows, one, one, one, pl.BlockSpec((8, 128), lambda s, order: (0, 0))] + [hbm] * na,
            out_specs=[rows] + [hbm] * (na + 3),
            scratch_shapes=[pltpu.VMEM((N_DEV, D, WIN_B), _BF), pltpu.VMEM((8, D), _F32),
                            pltpu.VMEM((N_DEV, N_META, 128), _F32),
                            pltpu.SemaphoreType.DMA((N_DEV - 1,)), pltpu.SemaphoreType.DMA((N_DEV - 1,)),
                            pltpu.SemaphoreType.DMA((1,))] + _sem_shapes(na) + _sem_shapes(2)),
        out_shape=[_S((t_pad, D), _F32), _S((N_DEV, D, WIN_B), _BF)]
        + [_S((N_DEV,) + g.shape, g.dtype) for g in to_all]
        + [_S((N_DEV, N_META, 128), _F32), _S((N_DEV, 8, D), _F32)],
        compiler_params=_cp(("arbitrary",)),
    )(order, dp, dp, u, w_in, h0, dh1, g_mix, gffn, gfin, loss, *to_all)
    return res


def _recv_shapes(scatter, windows):
    return [_S(s.shape if w is None else (s.shape[0], w[1]) + s.shape[2:], s.dtype) for s, w in zip(scatter, windows)]


def _wgrad(name, a, b, a_spec, b_spec, n_blocks, out_block, scatter=(), windows=None):
    nsc = len(scatter)
    windows = windows if windows is not None else [None] * nsc

    def body(a_ref, b_ref, *rest):
        o_ref = rest[nsc]
        j = pl.program_id(0)
        if nsc:
            exchange = _Exchange(rest[:nsc], [], rest[nsc + 1:2 * nsc + 1], rest[2 * nsc + 1:], windows)

            @pl.when(j == 0)
            def _():
                exchange.start()

        av = a_ref[0] if len(a_ref.shape) == 3 else a_ref[...]
        bv = b_ref[0] if len(b_ref.shape) == 3 else b_ref[...]
        o_ref[0] = _dot_tn(av, bv).astype(_BF)

        if nsc:
            @pl.when(j == n_blocks - 1)
            def _():
                exchange.finish()

    hbm = pl.BlockSpec(memory_space=pl.ANY)
    res = pl.pallas_call(
        body, name=name, grid=(n_blocks,),
        in_specs=[a_spec, b_spec] + [hbm] * nsc,
        out_specs=[pl.BlockSpec((1,) + out_block, lambda j: (j, 0, 0))] + [hbm] * nsc,
        out_shape=[_S((n_blocks,) + out_block, _BF)] + _recv_shapes(scatter, windows),
        scratch_shapes=_sem_shapes(nsc) if nsc else [],
        compiler_params=_cp(("arbitrary",)),
    )(a, b, *scatter)
    return res if nsc else res[0]


def _coords():
    return lax.axis_index("x"), lax.axis_index("y"), lax.axis_index("c")


def _sem_shapes(na):
    return [pltpu.SemaphoreType.DMA((7 * na,)), pltpu.SemaphoreType.DMA((7 * na,)), pltpu.SemaphoreType.DMA((na,))]


class _Gather:
    def __init__(self, srcs, outs, sems, place=None):
        self.srcs, self.outs = srcs, outs
        self.send_sems, self.recv_sems, self.local_sems = sems
        self.place = place if place is not None else (lambda ref, block: ref.at[block])
        self.na = len(srcs)
        x, y, c = _coords()
        self.pos = (x, y, c)
        self.me = 4 * x + 2 * y + c
        self.sibling = (x, y, 1 - c)
        self.chips = [(1 - x, y), (x, 1 - y), (1 - x, 1 - y)]

    @staticmethod
    def _slot(px, py, pc):
        return 4 * px + 2 * py + pc

    def _copy(self, a, k, block, to, own=False):
        dst = self.place(self.outs[a], block)
        return pltpu.make_async_remote_copy(
            src_ref=self.srcs[a] if own else dst, dst_ref=dst,
            send_sem=self.send_sems.at[7 * a + k], recv_sem=self.recv_sems.at[7 * a + k],
            device_id=to, device_id_type=_MESH)

    def _mine(self, a):
        return pltpu.make_async_copy(self.srcs[a], self.place(self.outs[a], self.me), self.local_sems.at[a])

    def _first(self):
        c = self.pos[2]
        cps = []
        for a in range(self.na):
            cps.append(self._copy(a, 0, self.me, self.sibling, own=True))
            cps += [self._copy(a, 1 + j, self.me, (*chip, c), own=True) for j, chip in enumerate(self.chips)]
        return cps

    def _passed(self):
        c = self.pos[2]
        return [self._copy(a, 4 + j, self._slot(*chip, c), self.sibling)
                for j, chip in enumerate(self.chips) for a in range(self.na)]

    def start(self):
        for a in range(self.na):
            self._mine(a).start()
        for cp in self._first():
            cp.start()

    def forward(self, j, arrays=None):
        c = self.pos[2]
        chip = self.chips[j]
        for a in (range(self.na) if arrays is None else arrays):
            self._copy(a, 1 + j, self._slot(*chip, c), self.pos).wait_recv()
            self._copy(a, 4 + j, self._slot(*chip, c), self.sibling).start()

    def wait_sibling(self):
        x, y, c = self.pos
        for a in range(self.na):
            self._copy(a, 0, self._slot(x, y, 1 - c), self.pos).wait_recv()

    def wait_passed(self, j):
        c = self.pos[2]
        for a in range(self.na):
            self._copy(a, 4 + j, self._slot(*self.chips[j], 1 - c), self.pos).wait_recv()

    def finish_sends(self):
        for cp in self._first() + self._passed():
            cp.wait_send()
        for a in range(self.na):
            self._mine(a).wait()

    def finish(self):
        self.wait_sibling()
        for j in range(3):
            self.wait_passed(j)
        self.finish_sends()


class _Exchange:
    def __init__(self, scatter, gather, outs, sems, windows=None):
        self.windows = windows if windows is not None else [None] * len(scatter)
        self.ins = list(scatter) + list(gather)
        self.ns, self.na = len(scatter), len(scatter) + len(gather)
        self.outs = outs
        self.send_sems, self.recv_sems, self.local_sems = sems
        x, y, c = _coords()
        self.pos = (x, y, c)
        self.me = 4 * x + 2 * y + c

    def _peer(self, r):
        x, y, c = self.pos
        return x ^ (r >> 2), y ^ ((r >> 1) & 1), c ^ (r & 1)

    def _src(self, a, block):
        if a >= self.ns:
            return self.ins[a]
        if self.windows[a] is None:
            return self.ins[a].at[block]
        row0, rows = self.windows[a]
        return self.ins[a].at[block, pl.ds(row0, rows)]

    def _local(self, a):
        return pltpu.make_async_copy(self._src(a, self.me), self.outs[a].at[self.me], self.local_sems.at[a])

    def _send(self, a, r):
        px, py, pc = self._peer(r)
        return pltpu.make_async_remote_copy(
            src_ref=self._src(a, 4 * px + 2 * py + pc), dst_ref=self.outs[a].at[self.me],
            send_sem=self.send_sems.at[7 * a + r - 1], recv_sem=self.recv_sems.at[7 * a + r - 1],
            device_id=(px, py, pc), device_id_type=_MESH)

    def _recv(self, a, r):
        px, py, pc = self._peer(r)
        return pltpu.make_async_remote_copy(
            src_ref=self._src(a, self.me), dst_ref=self.outs[a].at[4 * px + 2 * py + pc],
            send_sem=self.send_sems.at[7 * a + r - 1], recv_sem=self.recv_sems.at[7 * a + r - 1],
            device_id=(px, py, pc), device_id_type=_MESH)

    def start(self):
        for a in range(self.na):
            self._local(a).start()
        for r in range(1, N_DEV):
            for a in range(self.na):
                self._send(a, r).start()

    def finish(self):
        for r in range(1, N_DEV):
            for a in range(self.na):
                self._recv(a, r).wait_recv()
        for r in range(1, N_DEV):
            for a in range(self.na):
                self._send(a, r).wait_send()
        for a in range(self.na):
            self._local(a).wait()


def _prologue(x, tgt, small_l, w_in_l, cast_f32):
    seq = x.shape[0]
    nx = seq // TM
    rest_rows = seq - nx * TM
    nt = nx + 1
    nc = len(cast_f32)
    body_rows = TM - N_META
    assert nx >= 1 and rest_rows % 8 == 0 and rest_rows <= body_rows
    x_rest, t_rest = x[nx * TM:], tgt[nx * TM:]

    def last_tile_body(rest_ref):
        parts = ([rest_ref[...]] if rest_rows else []) + (
            [jnp.zeros((body_rows - rest_rows, D), _F32)] if body_rows > rest_rows else [])
        return parts[0] if len(parts) == 1 else jnp.concatenate(parts, axis=0)

    def body(xm_ref, xp_ref, tm_ref, tp_ref, *rest):
        if rest_rows:
            xr_ref, tr_ref, rest = rest[0], rest[1], rest[2:]
        else:
            xr_ref = tr_ref = None
        s_ref, w_ref, rest = rest[0], rest[1], rest[2:]
        cins = rest[:nc]
        h0_ref, tgt_ref, small_ref, wg_ref = rest[nc:nc + 4]
        couts = rest[nc + 4:2 * nc + 4]
        s_stage, w_stage, meta, msem = rest[2 * nc + 4:2 * nc + 8]
        g_s = _Gather([s_stage], [small_ref], rest[2 * nc + 8:2 * nc + 11])
        g_w = _Gather([w_stage], [wg_ref], rest[2 * nc + 11:], place=_pair_place)
        s = pl.program_id(0)
        i = (s + 1) % nt

        @pl.when(s == 0)
        def _():
            s_stage[...] = s_ref[...]
            w_stage[...] = w_ref[...].astype(_BF)
            g_s.start()
            g_w.start()
            meta[...] = jnp.zeros_like(meta)
            for a in range(nc):
                couts[a][...] = cins[a][...].astype(_BF)

        @pl.when(s == nt - 1)
        def _():
            for j in range(3):
                g_s.forward(j)
            g_s.finish()
            cps = [pltpu.make_async_copy(small_ref.at[k, pl.ds(0, N_META), :], meta.at[:, pl.ds(128 * k, 128)],
                                         msem.at[k]) for k in range(N_DEV)]
            for cp in cps:
                cp.start()
            for cp in cps:
                cp.wait()
            for j in range(3):
                g_w.forward(j)
            g_w.finish()

        has_x = i < nx
        h0_ref[pl.ds(0, N_META), :] = jnp.where(i == 0, meta[...], xp_ref[...])
        h0_ref[pl.ds(N_META, body_rows), :] = jnp.where(has_x, xm_ref[pl.ds(0, body_rows), :], last_tile_body(xr_ref))
        tgt_ref[pl.ds(0, N_META), :] = jnp.where(i == 0, 0.0, tp_ref[...])
        tgt_ref[pl.ds(N_META, body_rows), :] = jnp.where(has_x, tm_ref[pl.ds(0, body_rows), :], last_tile_body(tr_ref))

    def tile_of(s):
        return (s + 1) % nt

    hbm = pl.BlockSpec(memory_space=pl.ANY)
    main = pl.BlockSpec((TM, D), lambda s: (jnp.minimum(tile_of(s), nx - 1), 0))
    prev = pl.BlockSpec((N_META, D), lambda s: (jnp.maximum(tile_of(s) * (TM // N_META) - 1, 0), 0))
    tile = pl.BlockSpec((TM, D), lambda s: (tile_of(s), 0))
    rests = [x_rest, t_rest] if rest_rows else []
    return pl.pallas_call(
        body, name="prologue", grid=(nt,),
        in_specs=[main, prev, main, prev] + [_const(r.shape) for r in rests]
        + [_const(small_l.shape), _const(w_in_l.shape)] + [_const(l.shape) for l in cast_f32],
        out_specs=[tile, tile, hbm, hbm] + [_full(l.shape) for l in cast_f32],
        out_shape=[_S((nt * TM, D), _F32), _S((nt * TM, D), _F32), _S((N_DEV,) + small_l.shape, _F32),
                   _S((4, D, WIN_P), _BF)] + [_S(l.shape, _BF) for l in cast_f32],
        scratch_shapes=[pltpu.VMEM(small_l.shape, _F32), pltpu.VMEM(w_in_l.shape, _BF), pltpu.VMEM((N_META, D), _F32),
                        pltpu.SemaphoreType.DMA((N_DEV,))] + _sem_shapes(1) + _sem_shapes(1),
        compiler_params=_cp(("arbitrary",)),
    )(x, x, tgt, tgt, *rests, small_l, w_in_l, *cast_f32)


def _adamw_math(w, g, m, v):
    m2 = ADAM_B1 * m + (1.0 - ADAM_B1) * g
    v2 = ADAM_B2 * v + (1.0 - ADAM_B2) * (g * g)
    m_hat = m2 / (1.0 - ADAM_B1 ** ADAM_STEP)
    v_hat = v2 / (1.0 - ADAM_B2 ** ADAM_STEP)
    delta = -ADAM_LR * (m_hat / (jnp.sqrt(v_hat) + ADAM_EPS) + ADAM_WD * w)
    return delta, m2, v2


def _adamw_big(name, recv, w, m, v, rows):
    r_all, c_all = w.shape

    def body(r_ref, w_ref, m_ref, v_ref, g_out, d_out, m_out, v_out):
        g = r_ref[0].astype(_F32)
        for k in range(1, N_DEV):
            g = g + r_ref[k].astype(_F32)
        delta, m2, v2 = _adamw_math(w_ref[...], g, m_ref[...], v_ref[...])
        g_out[...] = g
        d_out[...] = delta
        m_out[...] = m2
        v_out[...] = v2

    tile = pl.BlockSpec((rows, c_all), lambda i: (i, 0))
    return pl.pallas_call(
        body, name=name, grid=(r_all // rows,),
        in_specs=[pl.BlockSpec((N_DEV, rows, c_all), lambda i: (0, i, 0)), tile, tile, tile],
        out_specs=[tile] * 4,
        out_shape=[_S(w.shape, _F32)] * 4,
        compiler_params=_cp(("arbitrary",)),
    )(recv, w, m, v)


def _adamw_small(gathered, slices, wmv):
    ng, npar = len(gathered), len(slices)

    def body(*refs):
        g_refs = refs[:ng]
        wmv_refs = refs[ng:ng + 3 * npar]
        outs = refs[ng + 3 * npar:]
        for i, (ai, r0, nr, c0, ncol) in enumerate(slices):
            g = g_refs[ai][0, pl.ds(r0, nr), pl.ds(c0, ncol)].astype(_F32)
            for k in range(1, N_DEV):
                g = g + g_refs[ai][k, pl.ds(r0, nr), pl.ds(c0, ncol)].astype(_F32)
            w_ref, m_ref, v_ref = wmv_refs[3 * i:3 * i + 3]
            delta, m2, v2 = _adamw_math(w_ref[...], g, m_ref[...], v_ref[...])
            outs[4 * i][...] = g
            outs[4 * i + 1][...] = delta
            outs[4 * i + 2][...] = m2
            outs[4 * i + 3][...] = v2
        total = g_refs[0][0, pl.ds(R_LOSS, 1), pl.ds(0, 128)]
        for k in range(1, N_DEV):
            total = total + g_refs[0][k, pl.ds(R_LOSS, 1), pl.ds(0, 128)]
        outs[4 * npar][...] = total

    flat = [t for trip in wmv for t in trip]
    out_shape = []
    for w, _, _ in wmv:
        out_shape += [_S(w.shape, _F32)] * 4
    out_shape.append(_S((1, 128), _F32))
    return pl.pallas_call(
        body, name="adamw_small", out_shape=out_shape,
        compiler_params=pltpu.CompilerParams(vmem_limit_bytes=VMEM_LIMIT),
    )(*gathered, *flat)


def _block_diag(w):
    eye = jnp.eye(8, dtype=w.dtype)
    return (w[:, :, None, :] * eye[:, None, :, None]).reshape(D_RG, D_RG)


def _diag_blocks(g):
    return jnp.concatenate([g[64 * h:64 * (h + 1), 64 * h:64 * (h + 1)] for h in range(8)], axis=0)


def _local_step(h0, tgt_p, n_valid, g_mix, w_in, vec, wr, wi, hb, g_hg, w_out_l, g_ffn, w_gu_l, w_down_l, g_fin):
    t_pad = h0.shape[0]
    me = 4 * lax.axis_index("x") + 2 * lax.axis_index("y") + lax.axis_index("c")
    p, u, y, hs, o, sc, w_out, w_gu, w_down = _mixer_fwd(h0, g_mix, w_in, wr, wi, vec, hb, g_hg,
                                                         [w_out_l, w_gu_l, w_down_l])
    w_out = w_out.reshape(D, D)
    w_down = w_down.reshape(4, FFB, D)
    h1, v, gu, act, dh2, dh2b, loss, gfin = _ffn_loss(h0, y, w_out, g_ffn, w_gu, w_down, g_fin, tgt_p, n_valid)

    g_wdown = _wgrad("wgrad_down", act, dh2b, pl.BlockSpec((1, t_pad, FFB), lambda j: (j, 0, 0)),
                     pl.BlockSpec((t_pad, D), lambda j: (0, 0)), 4, (FFB, D))
    g_wdown = g_wdown.reshape(N_DEV, D_FF // N_DEV, D)
    dgu, dh1, dh1b, dy, gffn, r_wdown = _ffn_bwd(dh2, dh2b, gu, h1, g_ffn, w_gu, w_down, w_out, [g_wdown])
    g_wgu = _wgrad("wgrad_gate_up", dgu, v, pl.BlockSpec((1, t_pad, FFB), lambda j: (j, 0, 0)),
                   pl.BlockSpec((t_pad, D), lambda j: (0, 0)), N_DEV, (FFB, D))
    g_wout = _wgrad("wgrad_out", y, dh1b, pl.BlockSpec((t_pad, D // 2), lambda j: (0, j)),
                    pl.BlockSpec((t_pad, D), lambda j: (0, 0)), 2, (D // 2, D)).reshape(N_DEV, D // N_DEV, D)
    dp, gvec, gw, r_wgu, r_wout = _mixer_bwd(p, hs, o, sc, dy, wr, wi, vec, hb, g_hg, [g_wgu, g_wout], [None, None])
    pack_c = jnp.concatenate([_diag_blocks(gw[0]), _diag_blocks(gw[1])], axis=1).astype(_BF)
    order = (me ^ jnp.array(_SEND_ORDER, jnp.int32)).astype(jnp.int32)
    dh0, r_win, all_b, all_c, r_meta, all_a = _inproj_bwd_send(dp, w_in, h0, dh1, g_mix, u, order, gffn, gfin, loss,
                                                               [gvec, pack_c])
    return dh0, (r_win, r_wgu, r_wout, r_wdown), (all_a, all_b, all_c, r_meta)


def kernel(x, meta_tokens, mix_norm_g, w_in, conv_w, conv_b, w_rgate, b_rgate, w_igate, b_igate, lru_lambda, rg_norm_g, hg_lower_bound, hg_norm_g, w_out, ffn_norm_g, w_gate_up, w_down, final_norm_g, loss_target, m_meta_tokens, m_mix_norm_g, m_w_in, m_conv_w, m_conv_b, m_w_rgate, m_b_rgate, m_w_igate, m_b_igate, m_lru_lambda, m_rg_norm_g, m_hg_lower_bound, m_hg_norm_g, m_w_out, m_ffn_norm_g, m_w_gate_up, m_w_down, m_final_norm_g, v_meta_tokens, v_mix_norm_g, v_w_in, v_conv_w, v_conv_b, v_w_rgate, v_b_rgate, v_w_igate, v_b_igate, v_lru_lambda, v_rg_norm_g, v_hg_lower_bound, v_hg_norm_g, v_w_out, v_ffn_norm_g, v_w_gate_up, v_w_down, v_final_norm_g):
    seq = x.shape[1]
    me = 4 * lax.axis_index("x") + 2 * lax.axis_index("y") + lax.axis_index("c")

    n_valid = N_META + seq
    small_l = jnp.concatenate([meta_tokens, jnp.pad(conv_w[0], ((0, 4), (0, 64)))], axis=0)
    h0, tgt_p, small_g, w_in_g, w_gu_l, w_out_l, w_down_l = _prologue(
        x[0], loss_target[0], small_l, w_in[0], [w_gate_up[0].T, w_out[0], w_down[0]])
    conv_w_full = jnp.transpose(small_g[:, N_META:N_META + 4, :64], (1, 0, 2)).reshape(4, D_RG)
    vec = jnp.concatenate([conv_b, b_rgate, b_igate, lru_lambda, rg_norm_g, jnp.zeros((3, D_RG), _F32),
                           conv_w_full, jnp.zeros((4, D_RG), _F32)], axis=0)
    wr = _block_diag(w_rgate[0]).astype(_BF)
    wi = _block_diag(w_igate[0]).astype(_BF)

    dh0, (r_win, r_wgu, r_wout, r_wdown), (all_a, all_b, all_c, meta_part) = _local_step(
        h0, tgt_p, n_valid, mix_norm_g, w_in_g, vec, wr, wi, hg_lower_bound, hg_norm_g,
        w_out_l, ffn_norm_g, w_gu_l, w_down_l, final_norm_g.reshape(1, D))
    grad_x = dh0[N_META:N_META + seq][None]

    outs = {}
    outs["w_in"] = _adamw_big("adamw_w_in", r_win, w_in[0], m_w_in[0], v_w_in[0], 256)
    outs["w_gate_up"] = [r.T for r in _adamw_big("adamw_w_gate_up", r_wgu, w_gate_up[0].T, m_w_gate_up[0].T,
                                                 v_w_gate_up[0].T, 176)]
    outs["w_down"] = _adamw_big("adamw_w_down", r_wdown, w_down[0], m_w_down[0], v_w_down[0], 176)

    convw_part = lax.dynamic_slice_in_dim(all_b[:, R_CONVW:R_CONVW + 4, :], me * 64, 64, axis=2)
    gathered = [all_a, all_b, all_c, meta_part, convw_part, r_wout]
    small_params = [
        ("meta_tokens", (3, 0, N_META, 0, 128), (meta_tokens, m_meta_tokens, v_meta_tokens), (N_META, 128)),
        ("mix_norm_g", (0, R_GMIX, 1, 0, D), (mix_norm_g, m_mix_norm_g, v_mix_norm_g), (1, D)),
        ("conv_w", (4, 0, 4, 0, 64), (conv_w, m_conv_w, v_conv_w), (4, 64)),
        ("conv_b", (1, R_CONVB, 1, 0, D_RG), (conv_b, m_conv_b, v_conv_b), (1, D_RG)),
        ("w_rgate", (2, 0, 512, 0, 64), (w_rgate, m_w_rgate, v_w_rgate), (512, 64)),
        ("b_rgate", (1, R_BR, 1, 0, D_RG), (b_rgate, m_b_rgate, v_b_rgate), (1, D_RG)),
        ("w_igate", (2, 0, 512, 64, 64), (w_igate, m_w_igate, v_w_igate), (512, 64)),
        ("b_igate", (1, R_BI, 1, 0, D_RG), (b_igate, m_b_igate, v_b_igate), (1, D_RG)),
        ("lru_lambda", (1, R_LAM, 1, 0, D_RG), (lru_lambda, m_lru_lambda, v_lru_lambda), (1, D_RG)),
        ("rg_norm_g", (1, R_GRG, 1, 0, D_RG), (rg_norm_g, m_rg_norm_g, v_rg_norm_g), (1, D_RG)),
        ("hg_lower_bound", (1, R_HB0, 2, 0, D_HG), (hg_lower_bound, m_hg_lower_bound, v_hg_lower_bound), (2, D_HG)),
        ("hg_norm_g", (1, R_GHG, 1, 0, HD), (hg_norm_g, m_hg_norm_g, v_hg_norm_g), (1, HD)),
        ("ffn_norm_g", (0, R_GFFN, 1, 0, D), (ffn_norm_g, m_ffn_norm_g, v_ffn_norm_g), (1, D)),
        ("final_norm_g", (0, R_GFIN, 1, 0, D), (final_norm_g, m_final_norm_g, v_final_norm_g), (1, D)),
        ("w_out", (5, 0, D // N_DEV, 0, D), (w_out, m_w_out, v_w_out), (D // N_DEV, D)),
    ]
    res = _adamw_small(gathered, [s[1] for s in small_params],
                       [tuple(t.reshape(s[3]) for t in s[2]) for s in small_params])
    for i, s in enumerate(small_params):
        outs[s[0]] = [r.reshape(s[2][0].shape) for r in res[4 * i:4 * i + 4]]
    for n, ref in (("w_in", w_in), ("w_gate_up", w_gate_up), ("w_out", w_out), ("w_down", w_down)):
        outs[n] = [r.reshape(ref.shape) for r in outs[n]]

    loss_all = res[4 * len(small_params)][0, 0]
    order = ["meta_tokens", "mix_norm_g", "w_in", "conv_w", "conv_b", "w_rgate", "b_rgate", "w_igate", "b_igate",
             "lru_lambda", "rg_norm_g", "hg_lower_bound", "hg_norm_g", "w_out", "ffn_norm_g", "w_gate_up", "w_down",
             "final_norm_g"]
    return (loss_all, grad_x, *[outs[n][0] for n in order], *[outs[n][1] for n in order],
            *[outs[n][2] for n in order], *[outs[n][3] for n in order])
```

```python
import functools

import jax
import jax.numpy as jnp
from jax import lax
from jax.experimental import pallas as pl
from jax.experimental.pallas import tpu as pltpu

_BF = jnp.bfloat16
_F32 = jnp.float32
_S = jax.ShapeDtypeStruct
_MESH = pl.DeviceIdType.MESH

N_DEV = 8
N_META = 16
D = 1024
D_RG = 512
D_HG = 512
HD = 128
NH = D_HG // HD
D_IN = 3072
D_FF = 2816
FFB = D_FF // 4
WIN_B = D_IN // N_DEV
WIN_P = 2 * WIN_B
EPS = 1e-6
LRU_C = 8.0
TM = 320
HC = 64
VMEM_LIMIT = 62 * 1024 * 1024

ADAM_LR = 0.001
ADAM_B1 = 0.9
ADAM_B2 = 0.999
ADAM_EPS = 1e-08
ADAM_WD = 0.01
ADAM_STEP = 10

_SEND_ORDER = (6, 4, 2, 7, 5, 3, 1, 0)

R_CONVB, R_BR, R_BI, R_LAM, R_GRG, R_HB0, R_HB1, R_GHG, R_CONVW = 0, 1, 2, 3, 4, 5, 6, 7, 8
R_GMIX, R_GFFN, R_GFIN, R_LOSS = 0, 1, 2, 3


def _cp(sem=None, **kw):
    return pltpu.CompilerParams(dimension_semantics=sem, vmem_limit_bytes=VMEM_LIMIT, **kw)


def _dot(a, b):
    return jnp.dot(a, b, preferred_element_type=_F32)


def _dot_nt(a, b):
    return lax.dot_general(a, b, (((1,), (1,)), ((), ())), preferred_element_type=_F32)


def _dot_tn(a, b):
    return lax.dot_general(a, b, (((0,), (0,)), ((), ())), preferred_element_type=_F32)


def _sigmoid(x):
    return 0.5 * jnp.tanh(0.5 * x) + 0.5


def _dsilu(x, s):
    return s * (1.0 + x * (1.0 - s))


_GELU_C = 0.7978845608028654


def _gelu_parts(x):
    t = jnp.tanh(_GELU_C * (x + 0.044715 * (x * x * x)))
    g = 0.5 * x * (1.0 + t)
    dg = 0.5 * (1.0 + t) + 0.5 * x * (1.0 - t * t) * (_GELU_C * (1.0 + 3.0 * 0.044715 * (x * x)))
    return g, dg


def _softplus(z):
    e = jnp.exp(-jnp.abs(z))
    w = 1.0 + e
    l1p = jnp.where(w == 1.0, e, jnp.log(w) * e / jnp.where(w == 1.0, 1.0, w - 1.0))
    return jnp.maximum(z, 0.0) + l1p


def _rms_fwd(x):
    r = lax.rsqrt(jnp.mean(x * x, axis=-1, keepdims=True) + EPS)
    return x * r, r


def _rms_bwd(dyg, n, r):
    return r * (dyg - n * jnp.mean(dyg * n, axis=-1, keepdims=True))


def _full(shape):
    nd = len(shape)
    return pl.BlockSpec(shape, lambda i: (0,) * nd)


def _const(shape):
    nd = len(shape)
    return pl.BlockSpec(shape, lambda i: (0,) * nd, pipeline_mode=pl.Buffered(1))


def _carry_gather(gather, i, nt, early=0):
    @pl.when(i == 0)
    def _():
        gather.start()

    def tail():
        for j in range(3):
            if early:
                @pl.when(i == min(nt // 3 + j, nt - 1))
                def _(j=j):
                    gather.forward(j, range(early))

            @pl.when(i == max(nt - 4 + j, 0))
            def _(j=j):
                gather.forward(j, range(early, gather.na))

        @pl.when(i == nt - 1)
        def _():
            gather.finish()

    return tail


def _pair_place(ref, block):
    return ref.at[block // 2, :, pl.ds(pl.multiple_of((block % 2) * WIN_B, WIN_B), WIN_B)]


def _rg_gates(xc, wr_ref, wi_ref, vec_ref):
    xcb = xc.astype(_BF)
    r = _sigmoid(_dot(xcb, wr_ref[...]) + vec_ref[R_BR:R_BR + 1, :])
    ig = _sigmoid(_dot(xcb, wi_ref[...]) + vec_ref[R_BI:R_BI + 1, :])
    nsp8 = -LRU_C * _softplus(-vec_ref[R_LAM:R_LAM + 1, :])
    la = nsp8 * r
    a = jnp.exp(la)
    th = jnp.tanh(la)
    s = jnp.sqrt(-2.0 * th / (1.0 - th))
    return r, ig, a, s, nsp8


def _conv(xbuf, vec_ref):
    acc = vec_ref[R_CONVW:R_CONVW + 1, :] * xbuf[pl.ds(5, TM), :]
    for j in range(1, 4):
        acc = acc + vec_ref[R_CONVW + j:R_CONVW + j + 1, :] * xbuf[pl.ds(5 + j, TM), :]
    return vec_ref[R_CONVB:R_CONVB + 1, :] + acc


def _dot3(m01, x):
    hi = x.astype(_BF)
    r1 = x - hi.astype(_F32)
    mid = r1.astype(_BF)
    lo = (r1 - mid.astype(_F32)).astype(_BF)
    return (_dot(m01, lo) + _dot(m01, mid)) + _dot(m01, hi)


def _chunk_dot3(m01, x):
    return jnp.concatenate([_dot3(m01, x[HC * c:HC * (c + 1), :]) for c in range(x.shape[0] // HC)], axis=0)


def _chunk_masks():
    row = lax.broadcasted_iota(jnp.int32, (HC, HC), 0)
    col = lax.broadcasted_iota(jnp.int32, (HC, HC), 1)
    return (row >= col).astype(_BF), (col >= row).astype(_BF), jnp.ones((HC, HC), _BF)


def _per_chunk_rows(x, r):
    return jnp.concatenate([jnp.broadcast_to(x[HC * c + r:HC * c + r + 1, :], (HC, x.shape[1]))
                            for c in range(TM // HC)], axis=0)


def _hg_prep(p_ref, lb, tri):
    hq = p_ref[:, pl.ds(2 * D_RG, D_HG)]
    hf = p_ref[:, pl.ds(2 * D_RG + D_HG, D_HG)]
    sq = _sigmoid(hq)
    q = hq * sq
    sg = _sigmoid(hf)
    f = lb + (1.0 - lb) * sg
    k = 1.0 - f
    b = _chunk_dot3(tri, jnp.log(f))
    bm = _per_chunk_rows(b, HC // 2 - 1)
    bl = _per_chunk_rows(b, HC - 1)
    e_q = jnp.exp(b - bm)
    e_k = jnp.exp(bm - b)
    e_b = jnp.exp(b)
    e_l = jnp.exp(bl - b)
    return dict(hq=hq, sq=sq, q=q, sg=sg, f=f, k=k, e_q=e_q, e_k=e_k, e_b=e_b, e_l=e_l,
                qd=q * e_q, kd=k * e_k, qe=q * e_b, ke=k * e_l, e_end=jnp.exp(bl))


def _mixer_fwd(h0, g_mix, w_in, wr, wi, vec, hb, g_hg, shards):
    t_pad = h0.shape[0]
    nt = t_pad // TM
    nc_t = TM // HC
    nsh = len(shards)

    def body(h_ref, gmix_ref, win_ref, wr_ref, wi_ref, vec_ref, hb_ref, ghg_ref, *rest):
        sh_refs, rest = rest[:nsh], rest[nsh:]
        pout_ref, uout_ref, y_ref, hs_ref, o_ref, sc_ref = rest[:6]
        gath_refs, rest = rest[6:6 + nsh], rest[6 + nsh:]
        xbuf, a_s, b_s, hcar, st, qd_s, kd_s, qe_s, ke_s, v_s, u_s, p_s, p_ref = rest[:13]
        i = pl.program_id(0)
        tail = _carry_gather(_Gather(sh_refs, gath_refs, rest[13:]), i, nt + 1, early=1)

        @pl.when(i == 0)
        def _():
            p_s[...] = jnp.zeros_like(p_s)

        p_ref[...] = p_s[...]

        @pl.when(i <= 1)
        def _():
            xbuf[pl.ds(0, 8), :] = jnp.zeros((8, D_RG), _F32)
            hcar[...] = jnp.zeros_like(hcar)
            st[...] = jnp.zeros_like(st)

        n_h, _ = _rms_fwd(h_ref[...])
        u = (n_h * gmix_ref[...]).astype(_BF)
        uout_ref[...] = u
        pieces = [(j, k) for j in range(4) for k in range(WIN_P // 256)]

        def project(count):
            for _ in range(count):
                j, k = pieces.pop(0)
                blk = _dot(u, win_ref[j, :, pl.ds(256 * k, 256)])
                p_s[:, pl.ds(WIN_P * j + 256 * k, 256)] = blk
                pout_ref[:, pl.ds(WIN_P * j + 256 * k, 256)] = blk

        x = p_ref[:, pl.ds(0, D_RG)]
        xbuf[pl.ds(8, TM), :] = x
        xc = _conv(xbuf, vec_ref)
        xbuf[pl.ds(0, 8), :] = x[TM - 8:, :]
        r, ig, a, s, _ = _rg_gates(xc, wr_ref, wi_ref, vec_ref)
        a_s[...] = a
        b_s[...] = s * (ig * xc)

        def step(t, h):
            h = a_s[pl.ds(t, 1), :] * h + b_s[pl.ds(t, 1), :]
            hs_ref[pl.ds(t, 1), :] = h
            return h

        hcar[pl.ds(0, 1), :] = lax.fori_loop(0, TM, step, hcar[pl.ds(0, 1), :], unroll=8)
        gel, _ = _gelu_parts(p_ref[:, pl.ds(D_RG, D_RG)])
        n, _ = _rms_fwd(gel * hs_ref[...])
        y_ref[:, pl.ds(0, D_RG)] = (n * vec_ref[R_GRG:R_GRG + 1, :]).astype(_BF)

        lb = _sigmoid(hb_ref[0:1, :] - hb_ref[1:2, :])
        tri, _, _ = _chunk_masks()
        q = _hg_prep(p_ref, lb, tri)
        for name, ref in (("qd", qd_s), ("kd", kd_s), ("qe", qe_s), ("ke", ke_s)):
            ref[...] = q[name].astype(_BF)
        v_s[...] = p_ref[:, pl.ds(2 * D_RG + 2 * D_HG, D_HG)].astype(_BF)
        e_end = q["e_end"]
        causal = (lax.broadcasted_iota(jnp.int32, (HC, HC), 0) >= lax.broadcasted_iota(jnp.int32, (HC, HC), 1))
        for c in range(nc_t):
            for h in range(NH):
                rs, cs = pl.ds(HC * c, HC), pl.ds(HD * h, HD)
                amat = jnp.where(causal, _dot_nt(qd_s[rs, cs], kd_s[rs, cs]), 0.0)
                o_ref[rs, cs] = _dot(amat.astype(_BF), v_s[rs, cs])
                u_s[NH * c + h] = _dot_tn(v_s[rs, cs], ke_s[rs, cs])
                if pieces:
                    project(1)
        assert not pieces
        for h in range(NH):
            cs = pl.ds(HD * h, HD)
            s_run = st[h]
            for c in range(nc_t):
                rs = pl.ds(HC * c, HC)
                sc_ref[c, h] = s_run
                o_ref[rs, cs] += _dot_nt(qe_s[rs, cs], s_run.astype(_BF))
                s_run = e_end[HC * c:HC * c + 1, HD * h:HD * (h + 1)] * s_run + u_s[NH * c + h]
            st[h] = s_run
        for h in range(NH):
            cs = pl.ds(HD * h, HD)
            n_o, _ = _rms_fwd(o_ref[:, cs])
            hg = p_ref[:, pl.ds(2 * D_RG + 3 * D_HG + HD * h, HD)]
            y_ref[:, pl.ds(D_RG + HD * h, HD)] = ((n_o * ghg_ref[...]) * (hg * _sigmoid(hg))).astype(_BF)

        tail()

    hbm = pl.BlockSpec(memory_space=pl.ANY)

    def proj(i):
        return jnp.minimum(i, nt - 1)

    def mixed(i):
        return jnp.maximum(i - 1, 0)

    return pl.pallas_call(
        body, name="mixer_fwd", grid=(nt + 1,),
        in_specs=[pl.BlockSpec((TM, D), lambda i: (proj(i), 0)), _full((1, D)), _const((4, D, WIN_P)),
                  _full((D_RG, D_RG)), _full((D_RG, D_RG)),
                  _full((16, D_RG)), _full((2, D_HG)), _full((1, HD))] + [hbm] * nsh,
        out_specs=[pl.BlockSpec((TM, D_IN), lambda i: (proj(i), 0)), pl.BlockSpec((TM, D), lambda i: (proj(i), 0)),
                   pl.BlockSpec((TM, D), lambda i: (mixed(i), 0)), pl.BlockSpec((TM, D_RG), lambda i: (mixed(i), 0)),
                   pl.BlockSpec((TM, D_HG), lambda i: (mixed(i), 0)),
                   pl.BlockSpec((nc_t, NH, HD, HD), lambda i: (mixed(i), 0, 0, 0))] + [hbm] * nsh,
        out_shape=[_S((t_pad, D_IN), _F32), _S((t_pad, D), _BF),
                   _S((t_pad, D), _BF), _S((t_pad, D_RG), _F32), _S((t_pad, D_HG), _F32),
                   _S((t_pad // HC, NH, HD, HD), _F32)] + [_S((N_DEV,) + s.shape, s.dtype) for s in shards],
        scratch_shapes=[pltpu.VMEM((TM + 8, D_RG), _F32), pltpu.VMEM((TM, D_RG), _F32),
                        pltpu.VMEM((TM, D_RG), _F32), pltpu.VMEM((8, D_RG), _F32),
                        pltpu.VMEM((NH, HD, HD), _F32)] + [pltpu.VMEM((TM, D_HG), _BF) for _ in range(5)]
        + [pltpu.VMEM((nc_t * NH, HD, HD), _F32), pltpu.VMEM((TM, D_IN), _F32), pltpu.VMEM((TM, D_IN), _F32)]
        + _sem_shapes(nsh),
        compiler_params=_cp(("arbitrary",)),
    )(h0, g_mix, w_in, wr, wi, vec, hb, g_hg, *shards)


def _ffn_loss(h0, y, w_out, g_ffn, w_gu, w_down, g_fin, tgt, n_valid):
    t_pad = h0.shape[0]

    def body(h_ref, y_ref, wo_ref, gffn_ref, wgu_ref, wd_ref, g_ref, t_ref,
             h1_ref, v_ref, gu_ref, act_ref, dh2_ref, dh2b_ref, loss_ref, gfin_ref):
        i = pl.program_id(0)

        @pl.when(i == 0)
        def _():
            loss_ref[...] = jnp.zeros_like(loss_ref)
            gfin_ref[...] = jnp.zeros_like(gfin_ref)

        h1 = h_ref[...] + _dot(y_ref[...], wo_ref[...])
        h1_ref[...] = h1
        n1, _ = _rms_fwd(h1)
        vb = (n1 * gffn_ref[...]).astype(_BF)
        v_ref[...] = vb
        h2 = h1
        for b in range(4):
            gate = _dot_nt(vb, wgu_ref[b])
            up = _dot_nt(vb, wgu_ref[4 + b])
            gu_ref[b] = gate
            gu_ref[4 + b] = up
            act = ((gate * _sigmoid(gate)) * up).astype(_BF)
            act_ref[b] = act
            h2 = h2 + _dot(act, wd_ref[b])
        n, r = _rms_fwd(h2)
        out = n * g_ref[...]
        row = i * TM + lax.broadcasted_iota(jnp.int32, (TM, 1), 0)
        valid = (row >= N_META) & (row < n_valid)
        err = jnp.where(valid, out - t_ref[...], 0.0)
        loss_ref[...] += (0.5 / D) * jnp.sum(err * err)
        dout = err * (1.0 / D)
        gfin_ref[...] += jnp.sum(dout * n, axis=0, keepdims=True)
        dh2 = _rms_bwd(dout * g_ref[...], n, r)
        dh2_ref[...] = dh2
        dh2b_ref[...] = dh2.astype(_BF)

    tile = pl.BlockSpec((TM, D), lambda i: (i, 0))
    return pl.pallas_call(
        body, name="ffn_loss", grid=(t_pad // TM,),
        in_specs=[tile, tile, _const((D, D)), _full((1, D)),
                  _const((N_DEV, FFB, D)), _const((4, FFB, D)), _full((1, D)), tile],
        out_specs=[tile, tile,
                   pl.BlockSpec((N_DEV, TM, FFB), lambda i: (0, i, 0)), pl.BlockSpec((4, TM, FFB), lambda i: (0, i, 0)),
                   tile, tile, _full((8, 128)), _full((1, D))],
        out_shape=[_S((t_pad, D), _F32), _S((t_pad, D), _BF),
                   _S((N_DEV, t_pad, FFB), _F32), _S((4, t_pad, FFB), _BF), _S((t_pad, D), _F32),
                   _S((t_pad, D), _BF), _S((8, 128), _F32), _S((1, D), _F32)],
        compiler_params=_cp(("arbitrary",)),
    )(h0, y, w_out, g_ffn, w_gu, w_down, g_fin, tgt)


def _ffn_bwd(dh2, dh2b, gu, h1, g_ffn, w_gu, w_down, w_out, scatter):
    t_pad = dh2.shape[0]
    nsc = len(scatter)
    nt = t_pad // TM

    def body(dh2_ref, dh2b_ref, gu_ref, h1_ref, g_ref, wgu_ref, wd_ref, wo_ref, *rest):
        dgu_ref, dh1_ref, dh1b_ref, dy_ref, gffn_ref = rest[nsc:nsc + 5]
        exchange = _Exchange(rest[:nsc], [], rest[nsc + 5:2 * nsc + 5], rest[2 * nsc + 5:])
        i = pl.program_id(0)

        @pl.when(i == 0)
        def _():
            exchange.start()
            gffn_ref[...] = jnp.zeros_like(gffn_ref)

        db = dh2b_ref[...]
        dv = jnp.zeros((TM, D), _F32)
        for b in range(4):
            dact = _dot_nt(db, wd_ref[b])
            gate = gu_ref[b]
            up = gu_ref[4 + b]
            sg = _sigmoid(gate)
            dgate = ((dact * up) * _dsilu(gate, sg)).astype(_BF)
            dup = (dact * (gate * sg)).astype(_BF)
            dgu_ref[b] = dgate
            dgu_ref[4 + b] = dup
            dv = dv + _dot(dgate, wgu_ref[b]) + _dot(dup, wgu_ref[4 + b])
        n, r = _rms_fwd(h1_ref[...])
        gffn_ref[...] += jnp.sum(dv * n, axis=0, keepdims=True)
        dh1 = dh2_ref[...] + _rms_bwd(dv * g_ref[...], n, r)
        dh1_ref[...] = dh1
        dh1b = dh1.astype(_BF)
        dh1b_ref[...] = dh1b
        dy_ref[...] = _dot_nt(dh1b, wo_ref[...])

        @pl.when(i == nt - 1)
        def _():
            exchange.finish()

    tile = pl.BlockSpec((TM, D), lambda i: (i, 0))
    hbm = pl.BlockSpec(memory_space=pl.ANY)
    return pl.pallas_call(
        body, name="ffn_bwd", grid=(nt,),
        in_specs=[tile, tile, pl.BlockSpec((N_DEV, TM, FFB), lambda i: (0, i, 0)), tile, _full((1, D)),
                  _const((N_DEV, FFB, D)), _const((4, FFB, D)), _const((D, D))] + [hbm] * nsc,
        out_specs=[pl.BlockSpec((N_DEV, TM, FFB), lambda i: (0, i, 0)), tile, tile, tile, _full((1, D))] + [hbm] * nsc,
        out_shape=[_S((N_DEV, t_pad, FFB), _BF), _S((t_pad, D), _F32), _S((t_pad, D), _BF),
                   _S((t_pad, D), _F32), _S((1, D), _F32)] + _recv_shapes(scatter, [None] * nsc),
        scratch_shapes=_sem_shapes(nsc),
        compiler_params=_cp(("arbitrary",)),
    )(dh2, dh2b, gu, h1, g_ffn, w_gu, w_down, w_out, *scatter)


def _mixer_bwd(p, hs, o, sc, dy, wr, wi, vec, hb, g_hg, scatter, windows):
    t_pad = p.shape[0]
    nt = t_pad // TM
    nc_t = TM // HC
    nsc = len(scatter)

    def rev(i):
        return nt - 1 - i

    def body(p_ref, pprev_ref, hs_ref, hprev_ref, o_ref, sc_ref, dy_ref, wr_ref, wi_ref, vec_ref, hb_ref, ghg_ref,
             *rest):
        send_refs, rest = rest[:nsc], rest[nsc:]
        dp_ref, gvec_ref, gw_ref = rest[:3]
        recv_refs, rest = rest[3:3 + nsc], rest[3 + nsc:]
        xbuf, hbuf, dbuf, a_s, g_s, ccar, dst = rest[:7]
        qd_s, kd_s, qe_s, ke_s, v_s, do_s, dqd_s, dkd_s, dqe_s, dke_s, dv_s, w_s, dend_s = rest[7:20]
        exchange = _Exchange(send_refs, [], recv_refs, rest[20:], windows)
        i = pl.program_id(0)
        first_tile = i == nt - 1

        @pl.when(i == 0)
        def _():
            exchange.start()
            gvec_ref[...] = jnp.zeros_like(gvec_ref)
            gw_ref[...] = jnp.zeros_like(gw_ref)
            dbuf[pl.ds(TM, 8), :] = jnp.zeros((8, D_RG), _F32)
            ccar[...] = jnp.zeros_like(ccar)
            dst[...] = jnp.zeros_like(dst)

        def acc(row, val):
            gvec_ref[row:row + 1, :] += jnp.sum(val, axis=0, keepdims=True)

        keep = jnp.where(first_tile, 0.0, 1.0)
        x = p_ref[:, pl.ds(0, D_RG)]
        xbuf[pl.ds(0, 8), :] = pprev_ref[...] * keep
        xbuf[pl.ds(8, TM), :] = x
        xc = _conv(xbuf, vec_ref)
        r, ig, a, s, nsp8 = _rg_gates(xc, wr_ref, wi_ref, vec_ref)
        h = hs_ref[...]
        hbuf[pl.ds(0, 8), :] = hprev_ref[...] * keep
        hbuf[pl.ds(8, TM), :] = h
        hm1 = hbuf[pl.ds(7, TM), :]
        gr = p_ref[:, pl.ds(D_RG, D_RG)]
        gel, dgel = _gelu_parts(gr)
        n, rr = _rms_fwd(gel * h)
        dyn = dy_ref[:, pl.ds(0, D_RG)]
        acc(R_GRG, dyn * n)
        dpre = _rms_bwd(dyn * vec_ref[R_GRG:R_GRG + 1, :], n, rr)
        dp_ref[:, pl.ds(D_RG, D_RG)] = ((dpre * h) * dgel).astype(_BF)
        a_s[...] = a
        g_s[...] = dpre * gel

        def step(k, c):
            t = TM - 1 - k
            g = g_s[pl.ds(t, 1), :] + c
            g_s[pl.ds(t, 1), :] = g
            return a_s[pl.ds(t, 1), :] * g

        ccar[pl.ds(0, 1), :] = lax.fori_loop(0, TM, step, ccar[pl.ds(0, 1), :], unroll=8)
        gt = g_s[...]
        da = gt * hm1
        ixc = ig * xc
        ds = gt * ixc
        dig = (gt * s) * xc
        dxc = (gt * s) * ig
        dla = da * a - ds * ((a * a) / s)
        lam = vec_ref[R_LAM:R_LAM + 1, :]
        gvec_ref[R_LAM:R_LAM + 1, :] += jnp.sum(dla * r, axis=0, keepdims=True) * (LRU_C * _sigmoid(-lam))
        dzr = (dla * nsp8) * (r * (1.0 - r))
        dzi = dig * (ig * (1.0 - ig))
        acc(R_BR, dzr)
        acc(R_BI, dzi)
        xcb = xc.astype(_BF)
        dzrb = dzr.astype(_BF)
        dzib = dzi.astype(_BF)
        gw_ref[0] += _dot_tn(xcb, dzrb)
        gw_ref[1] += _dot_tn(xcb, dzib)
        dxc = dxc + _dot_nt(dzrb, wr_ref[...]) + _dot_nt(dzib, wi_ref[...])
        acc(R_CONVB, dxc)
        for j in range(4):
            acc(R_CONVW + j, dxc * xbuf[pl.ds(5 + j, TM), :])
        dbuf[pl.ds(0, TM), :] = dxc
        dx = vec_ref[R_CONVW + 3:R_CONVW + 4, :] * dxc
        for j in range(3):
            dx = dx + vec_ref[R_CONVW + j:R_CONVW + j + 1, :] * dbuf[pl.ds(3 - j, TM), :]
        dbuf[pl.ds(TM, 8), :] = dxc[0:8, :]
        dp_ref[:, pl.ds(0, D_RG)] = dx.astype(_BF)

        lb = _sigmoid(hb_ref[0:1, :] - hb_ref[1:2, :])
        tri, tri_rev, ones = _chunk_masks()
        q = _hg_prep(p_ref, lb, tri)
        qdb, kdb = q["qd"].astype(_BF), q["kd"].astype(_BF)
        qd_s[...] = qdb
        kd_s[...] = kdb
        qe_s[...] = q["qe"].astype(_BF)
        ke_s[...] = q["ke"].astype(_BF)
        v_s[...] = p_ref[:, pl.ds(2 * D_RG + 2 * D_HG, D_HG)].astype(_BF)
        e_end = q["e_end"]
        ghg = ghg_ref[...]
        for h in range(NH):
            cs = pl.ds(HD * h, HD)
            hg = p_ref[:, pl.ds(2 * D_RG + 3 * D_HG + HD * h, HD)]
            sh = _sigmoid(hg)
            n_o, r_o = _rms_fwd(o_ref[:, cs])
            dyh = dy_ref[:, pl.ds(D_RG + HD * h, HD)]
            dp_ref[:, pl.ds(2 * D_RG + 3 * D_HG + HD * h, HD)] = ((dyh * (n_o * ghg)) * _dsilu(hg, sh)).astype(_BF)
            dn = dyh * (hg * sh)
            gvec_ref[R_GHG:R_GHG + 1, pl.ds(0, HD)] += jnp.sum(dn * n_o, axis=0, keepdims=True)
            do_s[:, cs] = _rms_bwd(dn * ghg, n_o, r_o).astype(_BF)
        causal = (lax.broadcasted_iota(jnp.int32, (HC, HC), 0) >= lax.broadcasted_iota(jnp.int32, (HC, HC), 1))
        for c in range(nc_t):
            for h in range(NH):
                rs, cs = pl.ds(HC * c, HC), pl.ds(HD * h, HD)
                qd_c, kd_c, do_c = qd_s[rs, cs], kd_s[rs, cs], do_s[rs, cs]
                amat = jnp.where(causal, _dot_nt(qd_c, kd_c), 0.0).astype(_BF)
                da_m = jnp.where(causal, _dot_nt(do_c, v_s[rs, cs]), 0.0).astype(_BF)
                dqd_s[rs, cs] = _dot(da_m, kd_c)
                dkd_s[rs, cs] = _dot_tn(da_m, qd_c)
                dqe_s[rs, cs] = _dot(do_c, sc_ref[c, h].astype(_BF))
                dv_s[rs, cs] = _dot_tn(amat, do_c)
                w_s[NH * c + h] = _dot_tn(do_c, qe_s[rs, cs])
        for h in range(NH):
            cs = pl.ds(HD * h, HD)
            d_run = dst[h]
            for c in reversed(range(nc_t)):
                rs = pl.ds(HC * c, HC)
                d_b = d_run.astype(_BF)
                dke_s[rs, cs] = _dot(v_s[rs, cs], d_b)
                dp_ref[rs, pl.ds(2 * D_RG + 2 * D_HG + HD * h, HD)] = (
                    dv_s[rs, cs] + _dot_nt(ke_s[rs, cs], d_b)).astype(_BF)
                dend_s[pl.ds(c, 1), cs] = jnp.sum(sc_ref[c, h] * d_run, axis=0, keepdims=True)
                d_run = w_s[NH * c + h] + e_end[HC * c:HC * c + 1, HD * h:HD * (h + 1)] * d_run
            dst[h] = d_run
        dqd, dkd, dqe, dke = dqd_s[...], dkd_s[...], dqe_s[...], dke_s[...]
        dq = dqd * q["e_q"] + dqe * q["e_b"]
        dk = dkd * q["e_k"] + dke * q["e_l"]
        dkeke = dke * q["ke"]
        db = dqd * qdb.astype(_F32) - dkd * kdb.astype(_F32) + dqe * q["qe"] - dkeke
        d_end = jnp.concatenate([jnp.broadcast_to(dend_s[pl.ds(c, 1), :], (HC, D_HG)) for c in range(nc_t)], axis=0)
        dlf = _chunk_dot3(tri_rev, db) + _chunk_dot3(ones, dkeke) + d_end * e_end
        df = dlf / q["f"] - dk
        sg = q["sg"]
        gvec_ref[R_HB0:R_HB0 + 1, :] += jnp.sum(df * (1.0 - sg), axis=0, keepdims=True)
        dp_ref[:, pl.ds(2 * D_RG, D_HG)] = (dq * _dsilu(q["hq"], q["sq"])).astype(_BF)
        dp_ref[:, pl.ds(2 * D_RG + D_HG, D_HG)] = ((df * (1.0 - lb)) * (sg * (1.0 - sg))).astype(_BF)

        @pl.when(i == nt - 1)
        def _():
            glb = gvec_ref[R_HB0:R_HB0 + 1, :] * (lb * (1.0 - lb))
            gvec_ref[R_HB0:R_HB0 + 1, :] = glb
            gvec_ref[R_HB1:R_HB1 + 1, :] = -glb
            exchange.finish()

    hbm = pl.BlockSpec(memory_space=pl.ANY)
    return pl.pallas_call(
        body, name="mixer_bwd", grid=(nt,),
        in_specs=[pl.BlockSpec((TM, D_IN), lambda i: (rev(i), 0)),
                  pl.BlockSpec((8, D_RG), lambda i: (jnp.maximum(rev(i) * (TM // 8) - 1, 0), 0)),
                  pl.BlockSpec((TM, D_RG), lambda i: (rev(i), 0)),
                  pl.BlockSpec((8, D_RG), lambda i: (jnp.maximum(rev(i) * (TM // 8) - 1, 0), 0)),
                  pl.BlockSpec((TM, D_HG), lambda i: (rev(i), 0)),
                  pl.BlockSpec((nc_t, NH, HD, HD), lambda i: (rev(i), 0, 0, 0)),
                  pl.BlockSpec((TM, D), lambda i: (rev(i), 0)),
                  _full((D_RG, D_RG)), _full((D_RG, D_RG)), _full((16, D_RG)), _full((2, D_HG)), _full((1, HD))]
        + [hbm] * nsc,
        out_specs=[pl.BlockSpec((TM, D_IN), lambda i: (rev(i), 0)), _full((16, D_RG)), _full((2, D_RG, D_RG))]
        + [hbm] * nsc,
        out_shape=[_S((t_pad, D_IN), _BF), _S((16, D_RG), _F32), _S((2, D_RG, D_RG), _F32)]
        + _recv_shapes(scatter, windows),
        scratch_shapes=[pltpu.VMEM((TM + 8, D_RG), _F32), pltpu.VMEM((TM + 8, D_RG), _F32),
                        pltpu.VMEM((TM + 8, D_RG), _F32), pltpu.VMEM((TM, D_RG), _F32),
                        pltpu.VMEM((TM, D_RG), _F32), pltpu.VMEM((8, D_RG), _F32),
                        pltpu.VMEM((NH, HD, HD), _F32)]
        + [pltpu.VMEM((TM, D_HG), _BF) for _ in range(6)] + [pltpu.VMEM((TM, D_HG), _F32) for _ in range(5)]
        + [pltpu.VMEM((nc_t * NH, HD, HD), _F32), pltpu.VMEM((8, D_HG), _F32)] + _sem_shapes(nsc),
        compiler_params=_cp(("arbitrary",)),
    )(p, p, hs, hs, o, sc, dy, wr, wi, vec, hb, g_hg, *scatter)


def _inproj_bwd_send(dp, w_in, h0, dh1, g_mix, u, order, gffn, gfin, loss, to_all):
    t_pad = dp.shape[0]
    rb = TM
    n_steps = N_DEV + t_pad // rb
    na = len(to_all)

    def body(order_ref, dpc_ref, dpr_ref, u_ref, w_ref, h_ref, dh1_ref, g_ref, gffn_ref, gfin_ref, loss_ref, *rest):
        all_in = rest[:na]
        dh0_ref, recv_ref = rest[na:na + 2]
        all_out = rest[na + 2:2 * na + 2]
        meta_ref, alla_ref = rest[2 * na + 2:2 * na + 4]
        buf, pack, meta, blk_send, blk_recv, blk_local = rest[2 * na + 4:2 * na + 10]
        exchange = _Exchange([], all_in, all_out, rest[2 * na + 10:2 * na + 13])
        last = _Exchange([meta], [pack], [meta_ref, alla_ref], rest[2 * na + 13:])
        s = pl.program_id(0)
        x, y, c = _coords()
        me = 4 * x + 2 * y + c

        def send(step):
            r = _SEND_ORDER[step]
            return pltpu.make_async_remote_copy(
                src_ref=buf.at[step], dst_ref=recv_ref.at[me], send_sem=blk_send.at[step], recv_sem=blk_recv.at[r - 1],
                device_id=(x ^ (r >> 2), y ^ ((r >> 1) & 1), c ^ (r & 1)), device_id_type=_MESH)

        @pl.when(s == 0)
        def _():
            exchange.start()
            pack[...] = jnp.zeros_like(pack)

        @pl.when(s < N_DEV)
        def _():
            buf[s] = _dot_tn(u_ref[...], dpc_ref[...]).astype(_BF)

            for step in range(N_DEV - 1):
                @pl.when(s == step)
                def _(step=step):
                    send(step).start()

        @pl.when(s >= N_DEV)
        def _():
            du = jnp.zeros((rb, D), _F32)
            for j in range(4):
                du = du + _dot_nt(dpr_ref[:, WIN_P * j:WIN_P * (j + 1)], w_ref[j])
            n, r = _rms_fwd(h_ref[...])
            pack[R_GMIX:R_GMIX + 1, :] += jnp.sum(du * n, axis=0, keepdims=True)
            dh0 = dh1_ref[...] + _rms_bwd(du * g_ref[...], n, r)
            dh0_ref[...] = dh0

            @pl.when(s == N_DEV)
            def _():
                for k in range(N_DEV):
                    meta[k] = dh0[0:N_META, 128 * k:128 * (k + 1)]

        @pl.when(s == n_steps - 1)
        def _():
            pack[R_GFFN:R_GFFN + 1, :] = gffn_ref[...]
            pack[R_GFIN:R_GFIN + 1, :] = gfin_ref[...]
            pack[R_LOSS:R_LOSS + 1, pl.ds(0, 128)] = loss_ref[0:1, :]
            last.start()
            mine = pltpu.make_async_copy(buf.at[N_DEV - 1], recv_ref.at[me], blk_local.at[0])
            mine.start()
            for step in range(N_DEV - 1):
                send(step).wait_send()
            for r in range(1, N_DEV):
                px, py, pc = x ^ (r >> 2), y ^ ((r >> 1) & 1), c ^ (r & 1)
                pltpu.make_async_remote_copy(
                    src_ref=buf.at[0], dst_ref=recv_ref.at[4 * px + 2 * py + pc], send_sem=blk_send.at[0],
                    recv_sem=blk_recv.at[r - 1], device_id=(px, py, pc), device_id_type=_MESH).wait_recv()
            mine.wait()
            exchange.finish()
            last.finish()

    hbm = pl.BlockSpec(memory_space=pl.ANY)
    rows = pl.BlockSpec((rb, D), lambda s, order: (jnp.maximum(s - N_DEV, 0), 0))
    one = pl.BlockSpec((1, D), lambda s, order: (0, 0))
    res = pl.pallas_call(
        body, name="inproj_bwd_send",
        grid_spec=pltpu.PrefetchScalarGridSpec(
            num_scalar_prefetch=1, grid=(n_steps,),
            in_specs=[pl.BlockSpec((t_pad, WIN_B), lambda s, order: (0, order[jnp.minimum(s, N_DEV - 1)])),
                      pl.BlockSpec((rb, D_IN), lambda s, order: (jnp.maximum(s - N_DEV, 0), 0)),
                      pl.BlockSpec((t_pad, D), lambda s, order: (0, 0), pipeline_mode=pl.Buffered(1)),
                      pl.BlockSpec((4, D, WIN_P), lambda s, order: (0, 0, 0), pipeline_mode=pl.Buffered(1)),
                      rows, rows, one, one, one, pl.BlockSpec((8, 128), lambda s, order: (0, 0))] + [hbm] * na,
            out_specs=[rows] + [hbm] * (na + 3),
            scratch_shapes=[pltpu.VMEM((N_DEV, D, WIN_B), _BF), pltpu.VMEM((8, D), _F32),
                            pltpu.VMEM((N_DEV, N_META, 128), _F32),
                            pltpu.SemaphoreType.DMA((N_DEV - 1,)), pltpu.SemaphoreType.DMA((N_DEV - 1,)),
                            pltpu.SemaphoreType.DMA((1,))] + _sem_shapes(na) + _sem_shapes(2)),
        out_shape=[_S((t_pad, D), _F32), _S((N_DEV, D, WIN_B), _BF)]
        + [_S((N_DEV,) + g.shape, g.dtype) for g in to_all]
        + [_S((N_DEV, N_META, 128), _F32), _S((N_DEV, 8, D), _F32)],
        compiler_params=_cp(("arbitrary",)),
    )(order, dp, dp, u, w_in, h0, dh1, g_mix, gffn, gfin, loss, *to_all)
    return res


def _recv_shapes(scatter, windows):
    return [_S(s.shape if w is None else (s.shape[0], w[1]) + s.shape[2:], s.dtype) for s, w in zip(scatter, windows)]


def _wgrad(name, a, b, a_spec, b_spec, n_blocks, out_block, scatter=(), windows=None):
    nsc = len(scatter)
    windows = windows if windows is not None else [None] * nsc

    def body(a_ref, b_ref, *rest):
        o_ref = rest[nsc]
        j = pl.program_id(0)
        if nsc:
            exchange = _Exchange(rest[:nsc], [], rest[nsc + 1:2 * nsc + 1], rest[2 * nsc + 1:], windows)

            @pl.when(j == 0)
            def _():
                exchange.start()

        av = a_ref[0] if len(a_ref.shape) == 3 else a_ref[...]
        bv = b_ref[0] if len(b_ref.shape) == 3 else b_ref[...]
        o_ref[0] = _dot_tn(av, bv).astype(_BF)

        if nsc:
            @pl.when(j == n_blocks - 1)
            def _():
                exchange.finish()

    hbm = pl.BlockSpec(memory_space=pl.ANY)
    res = pl.pallas_call(
        body, name=name, grid=(n_blocks,),
        in_specs=[a_spec, b_spec] + [hbm] * nsc,
        out_specs=[pl.BlockSpec((1,) + out_block, lambda j: (j, 0, 0))] + [hbm] * nsc,
        out_shape=[_S((n_blocks,) + out_block, _BF)] + _recv_shapes(scatter, windows),
        scratch_shapes=_sem_shapes(nsc) if nsc else [],
        compiler_params=_cp(("arbitrary",)),
    )(a, b, *scatter)
    return res if nsc else res[0]


def _coords():
    return lax.axis_index("x"), lax.axis_index("y"), lax.axis_index("c")


def _sem_shapes(na):
    return [pltpu.SemaphoreType.DMA((7 * na,)), pltpu.SemaphoreType.DMA((7 * na,)), pltpu.SemaphoreType.DMA((na,))]


class _Gather:
    def __init__(self, srcs, outs, sems, place=None):
        self.srcs, self.outs = srcs, outs
        self.send_sems, self.recv_sems, self.local_sems = sems
        self.place = place if place is not None else (lambda ref, block: ref.at[block])
        self.na = len(srcs)
        x, y, c = _coords()
        self.pos = (x, y, c)
        self.me = 4 * x + 2 * y + c
        self.sibling = (x, y, 1 - c)
        self.chips = [(1 - x, y), (x, 1 - y), (1 - x, 1 - y)]

    @staticmethod
    def _slot(px, py, pc):
        return 4 * px + 2 * py + pc

    def _copy(self, a, k, block, to, own=False):
        dst = self.place(self.outs[a], block)
        return pltpu.make_async_remote_copy(
            src_ref=self.srcs[a] if own else dst, dst_ref=dst,
            send_sem=self.send_sems.at[7 * a + k], recv_sem=self.recv_sems.at[7 * a + k],
            device_id=to, device_id_type=_MESH)

    def _mine(self, a):
        return pltpu.make_async_copy(self.srcs[a], self.place(self.outs[a], self.me), self.local_sems.at[a])

    def _first(self):
        c = self.pos[2]
        cps = []
        for a in range(self.na):
            cps.append(self._copy(a, 0, self.me, self.sibling, own=True))
            cps += [self._copy(a, 1 + j, self.me, (*chip, c), own=True) for j, chip in enumerate(self.chips)]
        return cps

    def _passed(self):
        c = self.pos[2]
        return [self._copy(a, 4 + j, self._slot(*chip, c), self.sibling)
                for j, chip in enumerate(self.chips) for a in range(self.na)]

    def start(self):
        for a in range(self.na):
            self._mine(a).start()
        for cp in self._first():
            cp.start()

    def forward(self, j, arrays=None):
        c = self.pos[2]
        chip = self.chips[j]
        for a in (range(self.na) if arrays is None else arrays):
            self._copy(a, 1 + j, self._slot(*chip, c), self.pos).wait_recv()
            self._copy(a, 4 + j, self._slot(*chip, c), self.sibling).start()

    def wait_sibling(self):
        x, y, c = self.pos
        for a in range(self.na):
            self._copy(a, 0, self._slot(x, y, 1 - c), self.pos).wait_recv()

    def wait_passed(self, j):
        c = self.pos[2]
        for a in range(self.na):
            self._copy(a, 4 + j, self._slot(*self.chips[j], 1 - c), self.pos).wait_recv()

    def finish_sends(self):
        for cp in self._first() + self._passed():
            cp.wait_send()
        for a in range(self.na):
            self._mine(a).wait()

    def finish(self):
        self.wait_sibling()
        for j in range(3):
            self.wait_passed(j)
        self.finish_sends()


class _Exchange:
    def __init__(self, scatter, gather, outs, sems, windows=None):
        self.windows = windows if windows is not None else [None] * len(scatter)
        self.ins = list(scatter) + list(gather)
        self.ns, self.na = len(scatter), len(scatter) + len(gather)
        self.outs = outs
        self.send_sems, self.recv_sems, self.local_sems = sems
        x, y, c = _coords()
        self.pos = (x, y, c)
        self.me = 4 * x + 2 * y + c

    def _peer(self, r):
        x, y, c = self.pos
        return x ^ (r >> 2), y ^ ((r >> 1) & 1), c ^ (r & 1)

    def _src(self, a, block):
        if a >= self.ns:
            return self.ins[a]
        if self.windows[a] is None:
            return self.ins[a].at[block]
        row0, rows = self.windows[a]
        return self.ins[a].at[block, pl.ds(row0, rows)]

    def _local(self, a):
        return pltpu.make_async_copy(self._src(a, self.me), self.outs[a].at[self.me], self.local_sems.at[a])

    def _send(self, a, r):
        px, py, pc = self._peer(r)
        return pltpu.make_async_remote_copy(
            src_ref=self._src(a, 4 * px + 2 * py + pc), dst_ref=self.outs[a].at[self.me],
            send_sem=self.send_sems.at[7 * a + r - 1], recv_sem=self.recv_sems.at[7 * a + r - 1],
            device_id=(px, py, pc), device_id_type=_MESH)

    def _recv(self, a, r):
        px, py, pc = self._peer(r)
        return pltpu.make_async_remote_copy(
            src_ref=self._src(a, self.me), dst_ref=self.outs[a].at[4 * px + 2 * py + pc],
            send_sem=self.send_sems.at[7 * a + r - 1], recv_sem=self.recv_sems.at[7 * a + r - 1],
            device_id=(px, py, pc), device_id_type=_MESH)

    def start(self):
        for a in range(self.na):
            self._local(a).start()
        for r in range(1, N_DEV):
            for a in range(self.na):
                self._send(a, r).start()

    def finish(self):
        for r in range(1, N_DEV):
            for a in range(self.na):
                self._recv(a, r).wait_recv()
        for r in range(1, N_DEV):
            for a in range(self.na):
                self._send(a, r).wait_send()
        for a in range(self.na):
            self._local(a).wait()


def _prologue(x, tgt, small_l, w_in_l, cast_f32):
    seq = x.shape[0]
    nx = seq // TM
    rest_rows = seq - nx * TM
    nt = nx + 1
    nc = len(cast_f32)
    body_rows = TM - N_META
    assert nx >= 1 and rest_rows % 8 == 0 and rest_rows <= body_rows
    x_rest, t_rest = x[nx * TM:], tgt[nx * TM:]

    def last_tile_body(rest_ref):
        parts = ([rest_ref[...]] if rest_rows else []) + (
            [jnp.zeros((body_rows - rest_rows, D), _F32)] if body_rows > rest_rows else [])
        return parts[0] if len(parts) == 1 else jnp.concatenate(parts, axis=0)

    def body(xm_ref, xp_ref, tm_ref, tp_ref, *rest):
        if rest_rows:
            xr_ref, tr_ref, rest = rest[0], rest[1], rest[2:]
        else:
            xr_ref = tr_ref = None
        s_ref, w_ref, rest = rest[0], rest[1], rest[2:]
        cins = rest[:nc]
        h0_ref, tgt_ref, small_ref, wg_ref = rest[nc:nc + 4]
        couts = rest[nc + 4:2 * nc + 4]
        s_stage, w_stage, meta, msem = rest[2 * nc + 4:2 * nc + 8]
        g_s = _Gather([s_stage], [small_ref], rest[2 * nc + 8:2 * nc + 11])
        g_w = _Gather([w_stage], [wg_ref], rest[2 * nc + 11:], place=_pair_place)
        s = pl.program_id(0)
        i = (s + 1) % nt

        @pl.when(s == 0)
        def _():
            s_stage[...] = s_ref[...]
            w_stage[...] = w_ref[...].astype(_BF)
            g_s.start()
            g_w.start()
            meta[...] = jnp.zeros_like(meta)
            for a in range(nc):
                couts[a][...] = cins[a][...].astype(_BF)

        @pl.when(s == nt - 1)
        def _():
            for j in range(3):
                g_s.forward(j)
            g_s.finish()
            cps = [pltpu.make_async_copy(small_ref.at[k, pl.ds(0, N_META), :], meta.at[:, pl.ds(128 * k, 128)],
                                         msem.at[k]) for k in range(N_DEV)]
            for cp in cps:
                cp.start()
            for cp in cps:
                cp.wait()
            for j in range(3):
                g_w.forward(j)
            g_w.finish()

        has_x = i < nx
        h0_ref[pl.ds(0, N_META), :] = jnp.where(i == 0, meta[...], xp_ref[...])
        h0_ref[pl.ds(N_META, body_rows), :] = jnp.where(has_x, xm_ref[pl.ds(0, body_rows), :], last_tile_body(xr_ref))
        tgt_ref[pl.ds(0, N_META), :] = jnp.where(i == 0, 0.0, tp_ref[...])
        tgt_ref[pl.ds(N_META, body_rows), :] = jnp.where(has_x, tm_ref[pl.ds(0, body_rows), :], last_tile_body(tr_ref))

    def tile_of(s):
        return (s + 1) % nt

    hbm = pl.BlockSpec(memory_space=pl.ANY)
    main = pl.BlockSpec((TM, D), lambda s: (jnp.minimum(tile_of(s), nx - 1), 0))
    prev = pl.BlockSpec((N_META, D), lambda s: (jnp.maximum(tile_of(s) * (TM // N_META) - 1, 0), 0))
    tile = pl.BlockSpec((TM, D), lambda s: (tile_of(s), 0))
    rests = [x_rest, t_rest] if rest_rows else []
    return pl.pallas_call(
        body, name="prologue", grid=(nt,),
        in_specs=[main, prev, main, prev] + [_const(r.shape) for r in rests]
        + [_const(small_l.shape), _const(w_in_l.shape)] + [_const(l.shape) for l in cast_f32],
        out_specs=[tile, tile, hbm, hbm] + [_full(l.shape) for l in cast_f32],
        out_shape=[_S((nt * TM, D), _F32), _S((nt * TM, D), _F32), _S((N_DEV,) + small_l.shape, _F32),
                   _S((4, D, WIN_P), _BF)] + [_S(l.shape, _BF) for l in cast_f32],
        scratch_shapes=[pltpu.VMEM(small_l.shape, _F32), pltpu.VMEM(w_in_l.shape, _BF), pltpu.VMEM((N_META, D), _F32),
                        pltpu.SemaphoreType.DMA((N_DEV,))] + _sem_shapes(1) + _sem_shapes(1),
        compiler_params=_cp(("arbitrary",)),
    )(x, x, tgt, tgt, *rests, small_l, w_in_l, *cast_f32)


def _adamw_math(w, g, m, v):
    m2 = ADAM_B1 * m + (1.0 - ADAM_B1) * g
    v2 = ADAM_B2 * v + (1.0 - ADAM_B2) * (g * g)
    m_hat = m2 / (1.0 - ADAM_B1 ** ADAM_STEP)
    v_hat = v2 / (1.0 - ADAM_B2 ** ADAM_STEP)
    delta = -ADAM_LR * (m_hat / (jnp.sqrt(v_hat) + ADAM_EPS) + ADAM_WD * w)
    return delta, m2, v2


def _adamw_big(name, recv, w, m, v, rows):
    r_all, c_all = w.shape

    def body(r_ref, w_ref, m_ref, v_ref, g_out, d_out, m_out, v_out):
        g = r_ref[0].astype(_F32)
        for k in range(1, N_DEV):
            g = g + r_ref[k].astype(_F32)
        delta, m2, v2 = _adamw_math(w_ref[...], g, m_ref[...], v_ref[...])
        g_out[...] = g
        d_out[...] = delta
        m_out[...] = m2
        v_out[...] = v2

    tile = pl.BlockSpec((rows, c_all), lambda i: (i, 0))
    return pl.pallas_call(
        body, name=name, grid=(r_all // rows,),
        in_specs=[pl.BlockSpec((N_DEV, rows, c_all), lambda i: (0, i, 0)), tile, tile, tile],
        out_specs=[tile] * 4,
        out_shape=[_S(w.shape, _F32)] * 4,
        compiler_params=_cp(("arbitrary",)),
    )(recv, w, m, v)


def _adamw_small(gathered, slices, wmv):
    ng, npar = len(gathered), len(slices)

    def body(*refs):
        g_refs = refs[:ng]
        wmv_refs = refs[ng:ng + 3 * npar]
        outs = refs[ng + 3 * npar:]
        for i, (ai, r0, nr, c0, ncol) in enumerate(slices):
            g = g_refs[ai][0, pl.ds(r0, nr), pl.ds(c0, ncol)].astype(_F32)
            for k in range(1, N_DEV):
                g = g + g_refs[ai][k, pl.ds(r0, nr), pl.ds(c0, ncol)].astype(_F32)
            w_ref, m_ref, v_ref = wmv_refs[3 * i:3 * i + 3]
            delta, m2, v2 = _adamw_math(w_ref[...], g, m_ref[...], v_ref[...])
            outs[4 * i][...] = g
            outs[4 * i + 1][...] = delta
            outs[4 * i + 2][...] = m2
            outs[4 * i + 3][...] = v2
        total = g_refs[0][0, pl.ds(R_LOSS, 1), pl.ds(0, 128)]
        for k in range(1, N_DEV):
            total = total + g_refs[0][k, pl.ds(R_LOSS, 1), pl.ds(0, 128)]
        outs[4 * npar][...] = total

    flat = [t for trip in wmv for t in trip]
    out_shape = []
    for w, _, _ in wmv:
        out_shape += [_S(w.shape, _F32)] * 4
    out_shape.append(_S((1, 128), _F32))
    return pl.pallas_call(
        body, name="adamw_small", out_shape=out_shape,
        compiler_params=pltpu.CompilerParams(vmem_limit_bytes=VMEM_LIMIT),
    )(*gathered, *flat)


def _block_diag(w):
    eye = jnp.eye(8, dtype=w.dtype)
    return (w[:, :, None, :] * eye[:, None, :, None]).reshape(D_RG, D_RG)


def _diag_blocks(g):
    return jnp.concatenate([g[64 * h:64 * (h + 1), 64 * h:64 * (h + 1)] for h in range(8)], axis=0)


def _local_step(h0, tgt_p, n_valid, g_mix, w_in, vec, wr, wi, hb, g_hg, w_out_l, g_ffn, w_gu_l, w_down_l, g_fin):
    t_pad = h0.shape[0]
    me = 4 * lax.axis_index("x") + 2 * lax.axis_index("y") + lax.axis_index("c")
    p, u, y, hs, o, sc, w_out, w_gu, w_down = _mixer_fwd(h0, g_mix, w_in, wr, wi, vec, hb, g_hg,
                                                         [w_out_l, w_gu_l, w_down_l])
    w_out = w_out.reshape(D, D)
    w_down = w_down.reshape(4, FFB, D)
    h1, v, gu, act, dh2, dh2b, loss, gfin = _ffn_loss(h0, y, w_out, g_ffn, w_gu, w_down, g_fin, tgt_p, n_valid)

    g_wdown = _wgrad("wgrad_down", act, dh2b, pl.BlockSpec((1, t_pad, FFB), lambda j: (j, 0, 0)),
                     pl.BlockSpec((t_pad, D), lambda j: (0, 0)), 4, (FFB, D))
    g_wdown = g_wdown.reshape(N_DEV, D_FF // N_DEV, D)
    dgu, dh1, dh1b, dy, gffn, r_wdown = _ffn_bwd(dh2, dh2b, gu, h1, g_ffn, w_gu, w_down, w_out, [g_wdown])
    g_wgu = _wgrad("wgrad_gate_up", dgu, v, pl.BlockSpec((1, t_pad, FFB), lambda j: (j, 0, 0)),
                   pl.BlockSpec((t_pad, D), lambda j: (0, 0)), N_DEV, (FFB, D))
    g_wout = _wgrad("wgrad_out", y, dh1b, pl.BlockSpec((t_pad, D // 2), lambda j: (0, j)),
                    pl.BlockSpec((t_pad, D), lambda j: (0, 0)), 2, (D // 2, D)).reshape(N_DEV, D // N_DEV, D)
    dp, gvec, gw, r_wgu, r_wout = _mixer_bwd(p, hs, o, sc, dy, wr, wi, vec, hb, g_hg, [g_wgu, g_wout], [None, None])
    pack_c = jnp.concatenate([_diag_blocks(gw[0]), _diag_blocks(gw[1])], axis=1).astype(_BF)
    order = (me ^ jnp.array(_SEND_ORDER, jnp.int32)).astype(jnp.int32)
    dh0, r_win, all_b, all_c, r_meta, all_a = _inproj_bwd_send(dp, w_in, h0, dh1, g_mix, u, order, gffn, gfin, loss,
                                                               [gvec, pack_c])
    return dh0, (r_win, r_wgu, r_wout, r_wdown), (all_a, all_b, all_c, r_meta)


def kernel(x, meta_tokens, mix_norm_g, w_in, conv_w, conv_b, w_rgate, b_rgate, w_igate, b_igate, lru_lambda, rg_norm_g, hg_lower_bound, hg_norm_g, w_out, ffn_norm_g, w_gate_up, w_down, final_norm_g, loss_target, m_meta_tokens, m_mix_norm_g, m_w_in, m_conv_w, m_conv_b, m_w_rgate, m_b_rgate, m_w_igate, m_b_igate, m_lru_lambda, m_rg_norm_g, m_hg_lower_bound, m_hg_norm_g, m_w_out, m_ffn_norm_g, m_w_gate_up, m_w_down, m_final_norm_g, v_meta_tokens, v_mix_norm_g, v_w_in, v_conv_w, v_conv_b, v_w_rgate, v_b_rgate, v_w_igate, v_b_igate, v_lru_lambda, v_rg_norm_g, v_hg_lower_bound, v_hg_norm_g, v_w_out, v_ffn_norm_g, v_w_gate_up, v_w_down, v_final_norm_g):
    seq = x.shape[1]
    me = 4 * lax.axis_index("x") + 2 * lax.axis_index("y") + lax.axis_index("c")

    n_valid = N_META + seq
    small_l = jnp.concatenate([meta_tokens, jnp.pad(conv_w[0], ((0, 4), (0, 64)))], axis=0)
    h0, tgt_p, small_g, w_in_g, w_gu_l, w_out_l, w_down_l = _prologue(
        x[0], loss_target[0], small_l, w_in[0], [w_gate_up[0].T, w_out[0], w_down[0]])
    conv_w_full = jnp.transpose(small_g[:, N_META:N_META + 4, :64], (1, 0, 2)).reshape(4, D_RG)
    vec = jnp.concatenate([conv_b, b_rgate, b_igate, lru_lambda, rg_norm_g, jnp.zeros((3, D_RG), _F32),
                           conv_w_full, jnp.zeros((4, D_RG), _F32)], axis=0)
    wr = _block_diag(w_rgate[0]).astype(_BF)
    wi = _block_diag(w_igate[0]).astype(_BF)

    dh0, (r_win, r_wgu, r_wout, r_wdown), (all_a, all_b, all_c, meta_part) = _local_step(
        h0, tgt_p, n_valid, mix_norm_g, w_in_g, vec, wr, wi, hg_lower_bound, hg_norm_g,
        w_out_l, ffn_norm_g, w_gu_l, w_down_l, final_norm_g.reshape(1, D))
    grad_x = dh0[N_META:N_META + seq][None]

    outs = {}
    outs["w_in"] = _adamw_big("adamw_w_in", r_win, w_in[0], m_w_in[0], v_w_in[0], 256)
    outs["w_gate_up"] = [r.T for r in _adamw_big("adamw_w_gate_up", r_wgu, w_gate_up[0].T, m_w_gate_up[0].T,
                                                 v_w_gate_up[0].T, 176)]

    convw_part = lax.dynamic_slice_in_dim(all_b[:, R_CONVW:R_CONVW + 4, :], me * 64, 64, axis=2)
    gathered = [all_a, all_b, all_c, meta_part, convw_part, r_wout, r_wdown]
    small_params = [
        ("meta_tokens", (3, 0, N_META, 0, 128), (meta_tokens, m_meta_tokens, v_meta_tokens), (N_META, 128)),
        ("mix_norm_g", (0, R_GMIX, 1, 0, D), (mix_norm_g, m_mix_norm_g, v_mix_norm_g), (1, D)),
        ("conv_w", (4, 0, 4, 0, 64), (conv_w, m_conv_w, v_conv_w), (4, 64)),
        ("conv_b", (1, R_CONVB, 1, 0, D_RG), (conv_b, m_conv_b, v_conv_b), (1, D_RG)),
        ("w_rgate", (2, 0, 512, 0, 64), (w_rgate, m_w_rgate, v_w_rgate), (512, 64)),
        ("b_rgate", (1, R_BR, 1, 0, D_RG), (b_rgate, m_b_rgate, v_b_rgate), (1, D_RG)),
        ("w_igate", (2, 0, 512, 64, 64), (w_igate, m_w_igate, v_w_igate), (512, 64)),
        ("b_igate", (1, R_BI, 1, 0, D_RG), (b_igate, m_b_igate, v_b_igate), (1, D_RG)),
        ("lru_lambda", (1, R_LAM, 1, 0, D_RG), (lru_lambda, m_lru_lambda, v_lru_lambda), (1, D_RG)),
        ("rg_norm_g", (1, R_GRG, 1, 0, D_RG), (rg_norm_g, m_rg_norm_g, v_rg_norm_g), (1, D_RG)),
        ("hg_lower_bound", (1, R_HB0, 2, 0, D_HG), (hg_lower_bound, m_hg_lower_bound, v_hg_lower_bound), (2, D_HG)),
        ("hg_norm_g", (1, R_GHG, 1, 0, HD), (hg_norm_g, m_hg_norm_g, v_hg_norm_g), (1, HD)),
        ("ffn_norm_g", (0, R_GFFN, 1, 0, D), (ffn_norm_g, m_ffn_norm_g, v_ffn_norm_g), (1, D)),
        ("final_norm_g", (0, R_GFIN, 1, 0, D), (final_norm_g, m_final_norm_g, v_final_norm_g), (1, D)),
        ("w_out", (5, 0, D // N_DEV, 0, D), (w_out, m_w_out, v_w_out), (D // N_DEV, D)),
        ("w_down", (6, 0, D_FF // N_DEV, 0, D), (w_down, m_w_down, v_w_down), (D_FF // N_DEV, D)),
    ]
    res = _adamw_small(gathered, [s[1] for s in small_params],
                       [tuple(t.reshape(s[3]) for t in s[2]) for s in small_params])
    for i, s in enumerate(small_params):
        outs[s[0]] = [r.reshape(s[2][0].shape) for r in res[4 * i:4 * i + 4]]
    for n, ref in (("w_in", w_in), ("w_gate_up", w_gate_up), ("w_out", w_out), ("w_down", w_down)):
        outs[n] = [r.reshape(ref.shape) for r in outs[n]]

    loss_all = res[4 * len(small_params)][0, 0]
    order = ["meta_tokens", "mix_norm_g", "w_in", "conv_w", "conv_b", "w_rgate", "b_rgate", "w_igate", "b_igate",
             "lru_lambda", "rg_norm_g", "hg_lower_bound", "hg_norm_g", "w_out", "ffn_norm_g", "w_gate_up", "w_down",
             "final_norm_g"]
    return (loss_all, grad_x, *[outs[n][0] for n in order], *[outs[n][1] for n in order],
            *[outs[n][2] for n in order], *[outs[n][3] for n in order])
```
